```python
import math
import jax, jax.numpy as jnp
from jax import lax
import numpy as np

D_MODEL = 2048
BATCH = 8
SEQ = 4096
DEPTH = 1

MIX_WIDTH = D_MODEL
ATTN_WIDTH = MIX_WIDTH // 2
LRU_WIDTH = MIX_WIDTH - ATTN_WIDTH
HEAD_DIM = 128
N_ATTN_HEADS = ATTN_WIDTH // HEAD_DIM
N_LRU_BLOCKS = 8
LRU_BLOCK = LRU_WIDTH // N_LRU_BLOCKS
CONV_WIDTH = 4
LRU_C = 8.0
Q_BLOCK = 128
D_FF = -(-8 * D_MODEL // (3 * 256)) * 256
IN_WIDTH = 3 * ATTN_WIDTH + 2 * LRU_WIDTH
RMS_EPS = 1e-6

kernel_name = "hymba_stickbreak_rglru_swiglu_sandwich"


def rmsnorm(x, gain):
    xf = x.astype(jnp.float32)
    var = jnp.mean(xf * xf, axis=-1, keepdims=True)
    return (xf * lax.rsqrt(var + RMS_EPS)).astype(x.dtype) * gain


def stick_breaking_attention(q, k, v):
    b, h, s, dh = q.shape
    n_blk = s // Q_BLOCK
    scale = 1.0 / math.sqrt(dh)
    qf = q.astype(jnp.float32)
    kf = k.astype(jnp.float32)
    vf = v.astype(jnp.float32)
    q_blocks = qf.reshape(b, h, n_blk, Q_BLOCK, dh).transpose(2, 0, 1, 3, 4)
    key_pos = jnp.arange(s)

    def one_block(args):
        qb, blk = args
        z = jnp.einsum('bhqd,bhkd->bhqk', qb, kf) * scale
        q_pos = blk * Q_BLOCK + jnp.arange(Q_BLOCK)
        causal = key_pos[None, :] < q_pos[:, None]
        log_beta = jax.nn.log_sigmoid(z)
        log_1m = jnp.where(causal, jax.nn.log_sigmoid(-z), 0.0)
        suffix = lax.cumsum(log_1m, axis=3, reverse=True) - log_1m
        weights = jnp.where(causal, jnp.exp(log_beta + suffix), 0.0)
        return jnp.einsum('bhqk,bhkd->bhqd', weights, vf)

    out = lax.map(one_block, (q_blocks, jnp.arange(n_blk)))
    out = out.transpose(1, 0, 3, 2, 4).reshape(b, s, h * dh)
    return out.astype(q.dtype)


def causal_depthwise_conv(x, w, bias):
    s = x.shape[1]
    xp = jnp.pad(x, ((0, 0), (CONV_WIDTH - 1, 0), (0, 0)))
    out = sum(xp[:, k:k + s, :] * w[k] for k in range(CONV_WIDTH))
    return out + bias


def block_diag_linear(x, w, bias):
    b, s, c = x.shape
    xr = x.reshape(b, s, N_LRU_BLOCKS, LRU_BLOCK)
    return jnp.einsum('bsnc,ncd->bsnd', xr, w).reshape(b, s, c) + bias


def rg_lru(x, w_rgate, b_rgate, w_igate, b_igate, lru_lambda):
    xf = x.astype(jnp.float32)
    r = jax.nn.sigmoid(block_diag_linear(x, w_rgate, b_rgate).astype(jnp.float32))
    i = jax.nn.sigmoid(block_diag_linear(x, w_igate, b_igate).astype(jnp.float32))
    log_a = LRU_C * r * jax.nn.log_sigmoid(lru_lambda.astype(jnp.float32))
    a = jnp.exp(log_a)
    mult = jnp.sqrt(-jnp.expm1(2.0 * log_a))
    bterm = mult * (i * xf)

    def combine(left, right):
        a_l, b_l = left
        a_r, b_r = right
        return a_l * a_r, a_r * b_l + b_r

    _, h = lax.associative_scan(combine, (a, bterm), axis=1)
    return h.astype(x.dtype)


def hybrid_mixer(hn, w_in, conv_w, conv_b, w_rgate, b_rgate, w_igate, b_igate, lru_lambda,
                 attn_out_norm, lru_out_norm, w_out):
    b, s, _ = hn.shape
    proj = hn @ w_in
    q, k, v, x_lru, g_lru = jnp.split(
        proj, [ATTN_WIDTH, 2 * ATTN_WIDTH, 3 * ATTN_WIDTH, 3 * ATTN_WIDTH + LRU_WIDTH], axis=-1)

    def heads(t):
        return t.reshape(b, s, N_ATTN_HEADS, HEAD_DIM).transpose(0, 2, 1, 3)

    y_attn = stick_breaking_attention(heads(q), heads(k), heads(v))

    xc = causal_depthwise_conv(x_lru, conv_w, conv_b)
    h = rg_lru(xc, w_rgate, b_rgate, w_igate, b_igate, lru_lambda)
    y_lru = h * jax.nn.gelu(g_lru)

    y = jnp.concatenate([rmsnorm(y_attn, attn_out_norm), rmsnorm(y_lru, lru_out_norm)], axis=-1)
    return y @ w_out


def swiglu_ffn(hn, w_gate, w_up, w_down):
    return (jax.nn.silu(hn @ w_gate) * (hn @ w_up)) @ w_down


def _fwd_setup_inputs(seed: int = 0) -> dict:
    key = jax.random.key(seed)
    ks = jax.random.split(key, 20)
    f32 = jnp.float32

    def normal(k, shape, fan_in):
        return jax.random.normal(k, shape, f32) * (fan_in ** -0.5)

    def gain(k, shape):
        return 1.0 + 0.02 * jax.random.normal(k, shape, f32)

    def small(k, shape):
        return 0.01 * jax.random.normal(k, shape, f32)

    x = jax.random.normal(ks[0], (BATCH, SEQ, D_MODEL), f32)
    u = jax.random.uniform(ks[10], (DEPTH, LRU_WIDTH), f32, 0.9, 0.999)
    a0 = u ** (1.0 / LRU_C)
    lru_lambda = jnp.log(a0) - jnp.log1p(-a0)
    return {
        "x": x,
        "pre_mix_norm": gain(ks[1], (DEPTH, D_MODEL)),
        "post_mix_norm": gain(ks[2], (DEPTH, D_MODEL)),
        "pre_ffn_norm": gain(ks[3], (DEPTH, D_MODEL)),
        "post_ffn_norm": gain(ks[4], (DEPTH, D_MODEL)),
        "w_in": normal(ks[5], (DEPTH, D_MODEL, IN_WIDTH), D_MODEL),
        "conv_w": normal(ks[6], (DEPTH, CONV_WIDTH, LRU_WIDTH), CONV_WIDTH),
        "conv_b": small(ks[7], (DEPTH, LRU_WIDTH)),
        "w_rgate": normal(ks[8], (DEPTH, N_LRU_BLOCKS, LRU_BLOCK, LRU_BLOCK), LRU_BLOCK),
        "b_rgate": small(ks[9], (DEPTH, LRU_WIDTH)),
        "w_igate": normal(ks[11], (DEPTH, N_LRU_BLOCKS, LRU_BLOCK, LRU_BLOCK), LRU_BLOCK),
        "b_igate": small(ks[12], (DEPTH, LRU_WIDTH)),
        "lru_lambda": lru_lambda,
        "attn_out_norm": gain(ks[13], (DEPTH, ATTN_WIDTH)),
        "lru_out_norm": gain(ks[14], (DEPTH, LRU_WIDTH)),
        "w_out": normal(ks[15], (DEPTH, MIX_WIDTH, D_MODEL), MIX_WIDTH),
        "w_ffn_gate": normal(ks[16], (DEPTH, D_MODEL, D_FF), D_MODEL),
        "w_ffn_up": normal(ks[17], (DEPTH, D_MODEL, D_FF), D_MODEL),
        "w_ffn_down": normal(ks[18], (DEPTH, D_FF, D_MODEL), D_FF),
    }


def _fwd_reference(x, pre_mix_norm, post_mix_norm, pre_ffn_norm, post_ffn_norm, w_in, conv_w, conv_b,
              w_rgate, b_rgate, w_igate, b_igate, lru_lambda, attn_out_norm, lru_out_norm, w_out,
              w_ffn_gate, w_ffn_up, w_ffn_down):
    for l in range(DEPTH):
        hn = rmsnorm(x, pre_mix_norm[l])
        m = hybrid_mixer(hn, w_in[l], conv_w[l], conv_b[l], w_rgate[l], b_rgate[l], w_igate[l],
                         b_igate[l], lru_lambda[l], attn_out_norm[l], lru_out_norm[l], w_out[l])
        x = x + rmsnorm(m, post_mix_norm[l])
        hn = rmsnorm(x, pre_ffn_norm[l])
        f = swiglu_ffn(hn, w_ffn_gate[l], w_ffn_up[l], w_ffn_down[l])
        x = x + rmsnorm(f, post_ffn_norm[l])
    return x


import jax as _jax
import jax.numpy as _jnp

TWIN_FORMAT = 'train_step'
FWD_PARAMS = ['x', 'pre_mix_norm', 'post_mix_norm', 'pre_ffn_norm', 'post_ffn_norm', 'w_in', 'conv_w', 'conv_b', 'w_rgate', 'b_rgate', 'w_igate', 'b_igate', 'lru_lambda', 'attn_out_norm', 'lru_out_norm', 'w_out', 'w_ffn_gate', 'w_ffn_up', 'w_ffn_down']
TWIN_WEIGHTS = ['pre_mix_norm', 'post_mix_norm', 'pre_ffn_norm', 'post_ffn_norm', 'w_in', 'conv_w', 'conv_b', 'w_rgate', 'b_rgate', 'w_igate', 'b_igate', 'lru_lambda', 'attn_out_norm', 'lru_out_norm', 'w_out', 'w_ffn_gate', 'w_ffn_up', 'w_ffn_down']
TWIN_DIFF_INPUT = 'x'
TWIN_INPUTS = ['x', 'pre_mix_norm', 'post_mix_norm', 'pre_ffn_norm', 'post_ffn_norm', 'w_in', 'conv_w', 'conv_b', 'w_rgate', 'b_rgate', 'w_igate', 'b_igate', 'lru_lambda', 'attn_out_norm', 'lru_out_norm', 'w_out', 'w_ffn_gate', 'w_ffn_up', 'w_ffn_down', 'loss_target', 'm_pre_mix_norm', 'm_post_mix_norm', 'm_pre_ffn_norm', 'm_post_ffn_norm', 'm_w_in', 'm_conv_w', 'm_conv_b', 'm_w_rgate', 'm_b_rgate', 'm_w_igate', 'm_b_igate', 'm_lru_lambda', 'm_attn_out_norm', 'm_lru_out_norm', 'm_w_out', 'm_w_ffn_gate', 'm_w_ffn_up', 'm_w_ffn_down', 'v_pre_mix_norm', 'v_post_mix_norm', 'v_pre_ffn_norm', 'v_post_ffn_norm', 'v_w_in', 'v_conv_w', 'v_conv_b', 'v_w_rgate', 'v_b_rgate', 'v_w_igate', 'v_b_igate', 'v_lru_lambda', 'v_attn_out_norm', 'v_lru_out_norm', 'v_w_out', 'v_w_ffn_gate', 'v_w_ffn_up', 'v_w_ffn_down']
TWIN_OUTPUTS = ['loss', 'grad_x', 'grad_pre_mix_norm', 'grad_post_mix_norm', 'grad_pre_ffn_norm', 'grad_post_ffn_norm', 'grad_w_in', 'grad_conv_w', 'grad_conv_b', 'grad_w_rgate', 'grad_b_rgate', 'grad_w_igate', 'grad_b_igate', 'grad_lru_lambda', 'grad_attn_out_norm', 'grad_lru_out_norm', 'grad_w_out', 'grad_w_ffn_gate', 'grad_w_ffn_up', 'grad_w_ffn_down', 'delta_pre_mix_norm', 'delta_post_mix_norm', 'delta_pre_ffn_norm', 'delta_post_ffn_norm', 'delta_w_in', 'delta_conv_w', 'delta_conv_b', 'delta_w_rgate', 'delta_b_rgate', 'delta_w_igate', 'delta_b_igate', 'delta_lru_lambda', 'delta_attn_out_norm', 'delta_lru_out_norm', 'delta_w_out', 'delta_w_ffn_gate', 'delta_w_ffn_up', 'delta_w_ffn_down', 'new_m_pre_mix_norm', 'new_m_post_mix_norm', 'new_m_pre_ffn_norm', 'new_m_post_ffn_norm', 'new_m_w_in', 'new_m_conv_w', 'new_m_conv_b', 'new_m_w_rgate', 'new_m_b_rgate', 'new_m_w_igate', 'new_m_b_igate', 'new_m_lru_lambda', 'new_m_attn_out_norm', 'new_m_lru_out_norm', 'new_m_w_out', 'new_m_w_ffn_gate', 'new_m_w_ffn_up', 'new_m_w_ffn_down', 'new_v_pre_mix_norm', 'new_v_post_mix_norm', 'new_v_pre_ffn_norm', 'new_v_post_ffn_norm', 'new_v_w_in', 'new_v_conv_w', 'new_v_conv_b', 'new_v_w_rgate', 'new_v_b_rgate', 'new_v_w_igate', 'new_v_b_igate', 'new_v_lru_lambda', 'new_v_attn_out_norm', 'new_v_lru_out_norm', 'new_v_w_out', 'new_v_w_ffn_gate', 'new_v_w_ffn_up', 'new_v_w_ffn_down']
TWIN_LEAF_KINDS = {'loss': 'loss', 'grad_x': 'grad_x', 'grad_pre_mix_norm': 'grad_w', 'grad_post_mix_norm': 'grad_w', 'grad_pre_ffn_norm': 'grad_w', 'grad_post_ffn_norm': 'grad_w', 'grad_w_in': 'grad_w', 'grad_conv_w': 'grad_w', 'grad_conv_b': 'grad_w', 'grad_w_rgate': 'grad_w', 'grad_b_rgate': 'grad_w', 'grad_w_igate': 'grad_w', 'grad_b_igate': 'grad_w', 'grad_lru_lambda': 'grad_w', 'grad_attn_out_norm': 'grad_w', 'grad_lru_out_norm': 'grad_w', 'grad_w_out': 'grad_w', 'grad_w_ffn_gate': 'grad_w', 'grad_w_ffn_up': 'grad_w', 'grad_w_ffn_down': 'grad_w', 'delta_pre_mix_norm': 'delta_w', 'delta_post_mix_norm': 'delta_w', 'delta_pre_ffn_norm': 'delta_w', 'delta_post_ffn_norm': 'delta_w', 'delta_w_in': 'delta_w', 'delta_conv_w': 'delta_w', 'delta_conv_b': 'delta_w', 'delta_w_rgate': 'delta_w', 'delta_b_rgate': 'delta_w', 'delta_w_igate': 'delta_w', 'delta_b_igate': 'delta_w', 'delta_lru_lambda': 'delta_w', 'delta_attn_out_norm': 'delta_w', 'delta_lru_out_norm': 'delta_w', 'delta_w_out': 'delta_w', 'delta_w_ffn_gate': 'delta_w', 'delta_w_ffn_up': 'delta_w', 'delta_w_ffn_down': 'delta_w', 'new_m_pre_mix_norm': 'new_m', 'new_m_post_mix_norm': 'new_m', 'new_m_pre_ffn_norm': 'new_m', 'new_m_post_ffn_norm': 'new_m', 'new_m_w_in': 'new_m', 'new_m_conv_w': 'new_m', 'new_m_conv_b': 'new_m', 'new_m_w_rgate': 'new_m', 'new_m_b_rgate': 'new_m', 'new_m_w_igate': 'new_m', 'new_m_b_igate': 'new_m', 'new_m_lru_lambda': 'new_m', 'new_m_attn_out_norm': 'new_m', 'new_m_lru_out_norm': 'new_m', 'new_m_w_out': 'new_m', 'new_m_w_ffn_gate': 'new_m', 'new_m_w_ffn_up': 'new_m', 'new_m_w_ffn_down': 'new_m', 'new_v_pre_mix_norm': 'new_v', 'new_v_post_mix_norm': 'new_v', 'new_v_pre_ffn_norm': 'new_v', 'new_v_post_ffn_norm': 'new_v', 'new_v_w_in': 'new_v', 'new_v_conv_w': 'new_v', 'new_v_conv_b': 'new_v', 'new_v_w_rgate': 'new_v', 'new_v_b_rgate': 'new_v', 'new_v_w_igate': 'new_v', 'new_v_b_igate': 'new_v', 'new_v_lru_lambda': 'new_v', 'new_v_attn_out_norm': 'new_v', 'new_v_lru_out_norm': 'new_v', 'new_v_w_out': 'new_v', 'new_v_w_ffn_gate': 'new_v', 'new_v_w_ffn_up': 'new_v', 'new_v_w_ffn_down': 'new_v'}


def _forward(args):
    return _fwd_reference(*[args[k] for k in FWD_PARAMS])


def _output_shape():
    def fwd():
        inp = _fwd_setup_inputs(0)
        return _fwd_reference(*[inp[k] for k in FWD_PARAMS])
    out = _jax.eval_shape(fwd)
    return out.shape, out.dtype

N_MICROBATCH = 1
ADAM_LR = 0.001
ADAM_B1 = 0.9
ADAM_B2 = 0.999
ADAM_EPS = 1e-08
ADAM_WD = 0.01
ADAM_STEP = 10
PER_EXAMPLE_BATCH_AXIS = {'x': 0, 'loss_target': 0}
SHARED_INPUTS = []
_WEIGHT_DTYPES = {'pre_mix_norm': _jnp.float32, 'post_mix_norm': _jnp.float32, 'pre_ffn_norm': _jnp.float32, 'post_ffn_norm': _jnp.float32, 'w_in': _jnp.float32, 'conv_w': _jnp.float32, 'conv_b': _jnp.float32, 'w_rgate': _jnp.float32, 'b_rgate': _jnp.float32, 'w_igate': _jnp.float32, 'b_igate': _jnp.float32, 'lru_lambda': _jnp.float32, 'attn_out_norm': _jnp.float32, 'lru_out_norm': _jnp.float32, 'w_out': _jnp.float32, 'w_ffn_gate': _jnp.float32, 'w_ffn_up': _jnp.float32, 'w_ffn_down': _jnp.float32}
MOMENT_SCALE = {'pre_mix_norm': 2.964077e-01, 'post_mix_norm': 1.600554e+01, 'pre_ffn_norm': 2.778641e-01, 'post_ffn_norm': 1.599227e+01, 'w_in': 1.891378e-01, 'conv_w': 3.060556e-01, 'conv_b': 4.945977e+00, 'w_rgate': 1.175065e-01, 'b_rgate': 6.902976e-02, 'w_igate': 2.162769e-01, 'b_igate': 1.167316e-01, 'lru_lambda': 1.166298e-01, 'attn_out_norm': 2.665464e-01, 'lru_out_norm': 3.951636e-01, 'w_out': 2.810531e-01, 'w_ffn_gate': 1.030127e-01, 'w_ffn_up': 1.408691e-01, 'w_ffn_down': 2.344074e-01}


def _to_microbatches(a, axis):
    t = _jnp.moveaxis(a, axis, 0)
    t = t.reshape((N_MICROBATCH, t.shape[0] // N_MICROBATCH) + t.shape[1:])
    return _jnp.moveaxis(t, 1, axis + 1)


def setup_inputs(seed: int = 0) -> dict:
    inp = _fwd_setup_inputs(seed)
    key = _jax.random.fold_in(_jax.random.key(seed), 7919)
    shape, _ = _output_shape()
    out = dict(inp)
    out["loss_target"] = _jax.random.normal(_jax.random.fold_in(key, 0), shape, _jnp.float32)
    for i, name in enumerate(TWIN_WEIGHTS):
        w = inp[name].astype(_jnp.float32)
        if MOMENT_SCALE is None:
            s = _jnp.sqrt(_jnp.mean(_jnp.square(w)) + 1e-30)
        else:
            s = MOMENT_SCALE[name]
        km, kv = _jax.random.split(_jax.random.fold_in(key, i + 1))
        out[name] = w
        out["m_" + name] = s * _jax.random.normal(km, w.shape, _jnp.float32)
        out["v_" + name] = (s * s) * _jax.random.uniform(kv, w.shape, _jnp.float32, 0.5, 1.5)
    if N_MICROBATCH > 1:
        for name, axis in PER_EXAMPLE_BATCH_AXIS.items():
            out[name] = _to_microbatches(out[name], axis)
    return {'x': out['x'], 'pre_mix_norm': out['pre_mix_norm'], 'post_mix_norm': out['post_mix_norm'], 'pre_ffn_norm': out['pre_ffn_norm'], 'post_ffn_norm': out['post_ffn_norm'], 'w_in': out['w_in'], 'conv_w': out['conv_w'], 'conv_b': out['conv_b'], 'w_rgate': out['w_rgate'], 'b_rgate': out['b_rgate'], 'w_igate': out['w_igate'], 'b_igate': out['b_igate'], 'lru_lambda': out['lru_lambda'], 'attn_out_norm': out['attn_out_norm'], 'lru_out_norm': out['lru_out_norm'], 'w_out': out['w_out'], 'w_ffn_gate': out['w_ffn_gate'], 'w_ffn_up': out['w_ffn_up'], 'w_ffn_down': out['w_ffn_down'], 'loss_target': out['loss_target'], 'm_pre_mix_norm': out['m_pre_mix_norm'], 'm_post_mix_norm': out['m_post_mix_norm'], 'm_pre_ffn_norm': out['m_pre_ffn_norm'], 'm_post_ffn_norm': out['m_post_ffn_norm'], 'm_w_in': out['m_w_in'], 'm_conv_w': out['m_conv_w'], 'm_conv_b': out['m_conv_b'], 'm_w_rgate': out['m_w_rgate'], 'm_b_rgate': out['m_b_rgate'], 'm_w_igate': out['m_w_igate'], 'm_b_igate': out['m_b_igate'], 'm_lru_lambda': out['m_lru_lambda'], 'm_attn_out_norm': out['m_attn_out_norm'], 'm_lru_out_norm': out['m_lru_out_norm'], 'm_w_out': out['m_w_out'], 'm_w_ffn_gate': out['m_w_ffn_gate'], 'm_w_ffn_up': out['m_w_ffn_up'], 'm_w_ffn_down': out['m_w_ffn_down'], 'v_pre_mix_norm': out['v_pre_mix_norm'], 'v_post_mix_norm': out['v_post_mix_norm'], 'v_pre_ffn_norm': out['v_pre_ffn_norm'], 'v_post_ffn_norm': out['v_post_ffn_norm'], 'v_w_in': out['v_w_in'], 'v_conv_w': out['v_conv_w'], 'v_conv_b': out['v_conv_b'], 'v_w_rgate': out['v_w_rgate'], 'v_b_rgate': out['v_b_rgate'], 'v_w_igate': out['v_w_igate'], 'v_b_igate': out['v_b_igate'], 'v_lru_lambda': out['v_lru_lambda'], 'v_attn_out_norm': out['v_attn_out_norm'], 'v_lru_out_norm': out['v_lru_out_norm'], 'v_w_out': out['v_w_out'], 'v_w_ffn_gate': out['v_w_ffn_gate'], 'v_w_ffn_up': out['v_w_ffn_up'], 'v_w_ffn_down': out['v_w_ffn_down']}


def _loss(weights, diff, rest, loss_target):
    with _jax.named_scope("forward"):
        args = {**rest, TWIN_DIFF_INPUT: diff, **{k: w.astype(_WEIGHT_DTYPES[k]) for k, w in weights.items()}}
        y = _forward(args)
    with _jax.named_scope("loss_head"):
        err = _jnp.square(y.astype(_jnp.float32) - loss_target)
        return 0.5 * _jnp.sum(_jnp.mean(err, axis=-1)) if err.ndim else 0.5 * err


def _adamw(w, g, m, v):
    m = ADAM_B1 * m + (1.0 - ADAM_B1) * g
    v = ADAM_B2 * v + (1.0 - ADAM_B2) * _jnp.square(g)
    m_hat = m / (1.0 - ADAM_B1 ** ADAM_STEP)
    v_hat = v / (1.0 - ADAM_B2 ** ADAM_STEP)
    delta = -ADAM_LR * (m_hat / (_jnp.sqrt(v_hat) + ADAM_EPS) + ADAM_WD * w)
    return delta, m, v


def reference(x, pre_mix_norm, post_mix_norm, pre_ffn_norm, post_ffn_norm, w_in, conv_w, conv_b, w_rgate, b_rgate, w_igate, b_igate, lru_lambda, attn_out_norm, lru_out_norm, w_out, w_ffn_gate, w_ffn_up, w_ffn_down, loss_target, m_pre_mix_norm, m_post_mix_norm, m_pre_ffn_norm, m_post_ffn_norm, m_w_in, m_conv_w, m_conv_b, m_w_rgate, m_b_rgate, m_w_igate, m_b_igate, m_lru_lambda, m_attn_out_norm, m_lru_out_norm, m_w_out, m_w_ffn_gate, m_w_ffn_up, m_w_ffn_down, v_pre_mix_norm, v_post_mix_norm, v_pre_ffn_norm, v_post_ffn_norm, v_w_in, v_conv_w, v_conv_b, v_w_rgate, v_b_rgate, v_w_igate, v_b_igate, v_lru_lambda, v_attn_out_norm, v_lru_out_norm, v_w_out, v_w_ffn_gate, v_w_ffn_up, v_w_ffn_down):
    given = dict(x=x, pre_mix_norm=pre_mix_norm, post_mix_norm=post_mix_norm, pre_ffn_norm=pre_ffn_norm, post_ffn_norm=post_ffn_norm, w_in=w_in, conv_w=conv_w, conv_b=conv_b, w_rgate=w_rgate, b_rgate=b_rgate, w_igate=w_igate, b_igate=b_igate, lru_lambda=lru_lambda, attn_out_norm=attn_out_norm, lru_out_norm=lru_out_norm, w_out=w_out, w_ffn_gate=w_ffn_gate, w_ffn_up=w_ffn_up, w_ffn_down=w_ffn_down, loss_target=loss_target, m_pre_mix_norm=m_pre_mix_norm, m_post_mix_norm=m_post_mix_norm, m_pre_ffn_norm=m_pre_ffn_norm, m_post_ffn_norm=m_post_ffn_norm, m_w_in=m_w_in, m_conv_w=m_conv_w, m_conv_b=m_conv_b, m_w_rgate=m_w_rgate, m_b_rgate=m_b_rgate, m_w_igate=m_w_igate, m_b_igate=m_b_igate, m_lru_lambda=m_lru_lambda, m_attn_out_norm=m_attn_out_norm, m_lru_out_norm=m_lru_out_norm, m_w_out=m_w_out, m_w_ffn_gate=m_w_ffn_gate, m_w_ffn_up=m_w_ffn_up, m_w_ffn_down=m_w_ffn_down, v_pre_mix_norm=v_pre_mix_norm, v_post_mix_norm=v_post_mix_norm, v_pre_ffn_norm=v_pre_ffn_norm, v_post_ffn_norm=v_post_ffn_norm, v_w_in=v_w_in, v_conv_w=v_conv_w, v_conv_b=v_conv_b, v_w_rgate=v_w_rgate, v_b_rgate=v_b_rgate, v_w_igate=v_w_igate, v_b_igate=v_b_igate, v_lru_lambda=v_lru_lambda, v_attn_out_norm=v_attn_out_norm, v_lru_out_norm=v_lru_out_norm, v_w_out=v_w_out, v_w_ffn_gate=v_w_ffn_gate, v_w_ffn_up=v_w_ffn_up, v_w_ffn_down=v_w_ffn_down)
    weights = {n: given[n] for n in TWIN_WEIGHTS}
    shared = {n: given[n] for n in SHARED_INPUTS}
    per_example = {n: given[n] for n in ['x']}
    grad_fn = _jax.value_and_grad(_loss, argnums=(0, 1))

    def one_microbatch(ex, loss_target):
        ex = dict(ex)
        diff = ex.pop(TWIN_DIFF_INPUT)
        return grad_fn(weights, diff, {**shared, **ex}, loss_target)

    if N_MICROBATCH == 1:
        loss, (grad_w, grad_x) = one_microbatch(per_example, given["loss_target"])
    else:
        def body(carry, xs):
            loss_sum, grad_sum = carry
            l_k, (gw_k, gx_k) = one_microbatch(xs[0], xs[1])
            with _jax.named_scope("update"):
                return (loss_sum + l_k, _jax.tree.map(_jnp.add, grad_sum, gw_k)), gx_k

        init = (_jnp.zeros((), _jnp.float32), _jax.tree.map(_jnp.zeros_like, weights))
        (loss, grad_w), grad_x = _jax.lax.scan(body, init, (per_example, given["loss_target"]))
    with _jax.named_scope("update"):
        delta_w, new_m, new_v = {}, {}, {}
        for n in TWIN_WEIGHTS:
            delta_w[n], new_m[n], new_v[n] = _adamw(weights[n], grad_w[n], given["m_" + n], given["v_" + n])
    return (loss, grad_x, *[grad_w[n] for n in TWIN_WEIGHTS], *[delta_w[n] for n in TWIN_WEIGHTS],
            *[new_m[n] for n in TWIN_WEIGHTS], *[new_v[n] for n in TWIN_WEIGHTS])
```

```python
import functools
import math

import jax
import jax.numpy as jnp
from jax import lax
from jax.experimental import pallas as pl
from jax.experimental.pallas import tpu as pltpu

F32 = jnp.float32
BF16 = jnp.bfloat16
_MXU = BF16
S = jax.ShapeDtypeStruct

RMS_EPS = 1e-6
HEAD_DIM = 128
CONV_WIDTH = 4
LRU_C = 8.0
ADAM_LR, ADAM_B1, ADAM_B2, ADAM_EPS, ADAM_WD, ADAM_STEP = 0.001, 0.9, 0.999, 1e-08, 0.01, 10
EXP_CUT = -105.0
VMEM_LIMIT = 56 * 1024 * 1024
ROW_TILE = 256
SEQ_TILE = 256
MESH = pl.DeviceIdType.MESH


def _cp(*sem):
    return pltpu.CompilerParams(dimension_semantics=sem, vmem_limit_bytes=VMEM_LIMIT)


def _dot(a, b):
    return jnp.dot(a, b, preferred_element_type=F32)


def _dot_nt(a, b):
    return lax.dot_general(a, b, (((1,), (1,)), ((), ())), preferred_element_type=F32)


def _dot_tn(a, b):
    return lax.dot_general(a, b, (((0,), (0,)), ((), ())), preferred_element_type=F32)


def _rstd(v):
    return lax.rsqrt(jnp.mean(v * v, axis=-1, keepdims=True) + RMS_EPS)


def _rms_bwd(dn, vh, r, gain):
    dvh = dn * gain
    dv = r * (dvh - vh * jnp.mean(dvh * vh, axis=-1, keepdims=True))
    return dv, jnp.sum(dn * vh, axis=0, keepdims=True)


def _log_sigmoid(z):
    return jnp.minimum(z, 0.0) - jnp.log(1.0 + jnp.exp(-jnp.abs(z)))


def _expm1(v):
    small = v * (1.0 + v * (0.5 + v * (1.0 / 6.0 + v * (1.0 / 24.0 + v * (1.0 / 120.0)))))
    return jnp.where(jnp.abs(v) < 0.04, small, jnp.exp(v) - 1.0)


_GELU_C = math.sqrt(2.0 / math.pi)


def _gelu(v):
    return 0.5 * v * (1.0 + jnp.tanh(_GELU_C * (v + 0.044715 * v * v * v)))


def _gelu_grad(v):
    th = jnp.tanh(_GELU_C * (v + 0.044715 * v * v * v))
    return 0.5 * (1.0 + th) + 0.5 * v * (1.0 - th * th) * _GELU_C * (1.0 + 3.0 * 0.044715 * v * v)


def _row_spec(tm, d):
    return pl.BlockSpec((tm, d), lambda i: (i, 0))


def _vec_spec(d):
    return pl.BlockSpec((1, d), lambda i: (0, 0))


def _mm_nn(a, b3, *, bm, bn, bk, out_dtype, name):
    m, k = a.shape
    c, _, n = b3.shape
    ni, nj, nk = m // bm, n // bn, k // bk

    def body(a_ref, b_ref, o_ref, *acc):
        if nk == 1:
            o_ref[...] = _dot(a_ref[...], b_ref[...]).astype(o_ref.dtype)
            return
        kk = pl.program_id(3)

        @pl.when(kk == 0)
        def _():
            acc[0][...] = jnp.zeros_like(acc[0])

        acc[0][...] += _dot(a_ref[...], b_ref[...])

        @pl.when(kk == nk - 1)
        def _():
            o_ref[...] = acc[0][...].astype(o_ref.dtype)

    return pl.pallas_call(
        body, grid=(c, nj, ni, nk),
        in_specs=[pl.BlockSpec((bm, bk), lambda cc, j, i, kk: (i, kk)),
                  pl.BlockSpec((None, bk, bn), lambda cc, j, i, kk: (cc, kk, j))],
        out_specs=pl.BlockSpec((bm, bn), lambda cc, j, i, kk: (i, cc * nj + j)),
        out_shape=S((m, c * n), out_dtype),
        scratch_shapes=[] if nk == 1 else [pltpu.VMEM((bm, bn), F32)],
        compiler_params=_cp("parallel", "parallel", "parallel", "arbitrary"), name=name)(a, b3)


def _mm_nt(a, b3, *, bm, bo, out_dtype, name):
    m = a.shape[0]
    c, ko, n = b3.shape
    ni, nj = m // bm, ko // bo

    def body(a_ref, b_ref, o_ref, *acc):
        if c == 1:
            o_ref[...] = _dot_nt(a_ref[...], b_ref[...]).astype(o_ref.dtype)
            return
        cc = pl.program_id(2)

        @pl.when(cc == 0)
        def _():
            acc[0][...] = jnp.zeros_like(acc[0])

        acc[0][...] += _dot_nt(a_ref[...], b_ref[...])

        @pl.when(cc == c - 1)
        def _():
            o_ref[...] = acc[0][...].astype(o_ref.dtype)

    return pl.pallas_call(
        body, grid=(nj, ni, c),
        in_specs=[pl.BlockSpec((bm, n), lambda j, i, cc: (i, cc)),
                  pl.BlockSpec((None, bo, n), lambda j, i, cc: (cc, j, 0))],
        out_specs=pl.BlockSpec((bm, bo), lambda j, i, cc: (i, j)),
        out_shape=S((m, ko), out_dtype),
        scratch_shapes=[] if c == 1 else [pltpu.VMEM((bm, bo), F32)],
        compiler_params=_cp("parallel", "parallel", "arbitrary"), name=name)(a, b3)


def _mm_tn(a, b, c, *, bm, bk, name):
    m, k = a.shape
    n = b.shape[1] // c
    nm, nk = m // bm, k // bk

    def body(a_ref, b_ref, o_ref, acc):
        mm = pl.program_id(2)

        @pl.when(mm == 0)
        def _():
            acc[...] = jnp.zeros_like(acc)

        acc[...] += _dot_tn(a_ref[...], b_ref[...])

        @pl.when(mm == nm - 1)
        def _():
            o_ref[...] = acc[...]

    return pl.pallas_call(
        body, grid=(c, nk, nm),
        in_specs=[pl.BlockSpec((bm, bk), lambda cc, j, mm: (mm, j)),
                  pl.BlockSpec((bm, n), lambda cc, j, mm: (mm, cc))],
        out_specs=pl.BlockSpec((None, bk, n), lambda cc, j, mm: (cc, j, 0)),
        out_shape=S((c, k, n), F32),
        scratch_shapes=[pltpu.VMEM((bk, n), F32)],
        compiler_params=_cp("parallel", "parallel", "arbitrary"), name=name)(a, b)


def _swiglu_fwd(hn, wg3, wu3, *, bm, name):
    m, k = hn.shape
    c, _, n = wg3.shape

    def body(a_ref, g_ref, u_ref, gate_ref, up_ref, act_ref):
        a = a_ref[...]
        gate = _dot(a, g_ref[...])
        up = _dot(a, u_ref[...])
        gate_ref[...] = gate
        up_ref[...] = up
        act_ref[...] = (gate * jax.nn.sigmoid(gate) * up).astype(act_ref.dtype)

    wspec = pl.BlockSpec((None, k, n), lambda cc, i: (cc, 0, 0))
    ospec = pl.BlockSpec((bm, n), lambda cc, i: (i, cc))
    return pl.pallas_call(
        body, grid=(c, m // bm),
        in_specs=[pl.BlockSpec((bm, k), lambda cc, i: (i, 0)), wspec, wspec],
        out_specs=[ospec, ospec, ospec],
        out_shape=[S((m, c * n), F32), S((m, c * n), F32), S((m, c * n), _MXU)],
        compiler_params=_cp("parallel", "parallel"), name=name)(hn, wg3, wu3)


def _swiglu_bwd(df, wd, gate, up, *, bm, bo, name):
    m, k = df.shape
    ko = wd.shape[0]

    def body(a_ref, b_ref, g_ref, u_ref, dg_ref, du_ref):
        dact = _dot_nt(a_ref[...], b_ref[...])
        gate = g_ref[...]
        sg = jax.nn.sigmoid(gate)
        dg_ref[...] = (dact * u_ref[...] * (sg * (1.0 + gate * (1.0 - sg)))).astype(dg_ref.dtype)
        du_ref[...] = (dact * (gate * sg)).astype(du_ref.dtype)

    ospec = pl.BlockSpec((bm, bo), lambda j, i: (i, j))
    return pl.pallas_call(
        body, grid=(ko // bo, m // bm),
        in_specs=[pl.BlockSpec((bm, k), lambda j, i: (i, 0)), pl.BlockSpec((bo, k), lambda j, i: (j, 0)), ospec, ospec],
        out_specs=[ospec, ospec],
        out_shape=[S((m, ko), _MXU), S((m, ko), _MXU)],
        compiler_params=_cp("parallel", "parallel"), name=name)(df, wd, gate, up)


def _rms_fwd(x, gain, name):
    t, d = x.shape
    tm = min(t, ROW_TILE)

    def body(x_ref, g_ref, o_ref):
        xv = x_ref[...]
        o_ref[...] = ((xv * _rstd(xv)) * g_ref[...]).astype(o_ref.dtype)

    return pl.pallas_call(body, grid=(t // tm,), in_specs=[_row_spec(tm, d), _vec_spec(d)], out_specs=_row_spec(tm, d),
                          out_shape=S((t, d), _MXU), compiler_params=_cp("parallel"), name=name)(x, gain)


def _outnorm_fwd(o, yl, ga, gl, name):
    t, w = o.shape
    tm = min(t, ROW_TILE)

    def body(o_ref, l_ref, ga_ref, gl_ref, y_ref):
        ov, lv = o_ref[...], l_ref[...]
        y_ref[:, :w] = ((ov * _rstd(ov)) * ga_ref[...]).astype(y_ref.dtype)
        y_ref[:, w:] = ((lv * _rstd(lv)) * gl_ref[...]).astype(y_ref.dtype)

    return pl.pallas_call(body, grid=(t // tm,), in_specs=[_row_spec(tm, w), _row_spec(tm, w), _vec_spec(w), _vec_spec(w)],
                          out_specs=_row_spec(tm, 2 * w), out_shape=S((t, 2 * w), _MXU),
                          compiler_params=_cp("parallel"), name=name)(o, yl, ga, gl)


def _mid_fwd(x, mix, g_post, g_pre, name):
    t, d = x.shape
    tm = min(t, ROW_TILE)

    def body(x_ref, m_ref, gp_ref, gn_ref, x2_ref, hn_ref):
        mv = m_ref[...]
        x2 = x_ref[...] + (mv * _rstd(mv)) * gp_ref[...]
        x2_ref[...] = x2
        hn_ref[...] = ((x2 * _rstd(x2)) * gn_ref[...]).astype(hn_ref.dtype)

    return pl.pallas_call(body, grid=(t // tm,), in_specs=[_row_spec(tm, d), _row_spec(tm, d), _vec_spec(d), _vec_spec(d)],
                          out_specs=[_row_spec(tm, d), _row_spec(tm, d)], out_shape=[S((t, d), F32), S((t, d), _MXU)],
                          compiler_params=_cp("parallel"), name=name)(x, mix, g_post, g_pre)


def _final(f, x2, target, g_post, name):
    t, d = f.shape
    tm = min(t, ROW_TILE)

    def body(f_ref, x2_ref, t_ref, g_ref, loss_ref, dout_ref, df_ref, dg_ref):
        @pl.when(pl.program_id(0) == 0)
        def _():
            loss_ref[...] = jnp.zeros_like(loss_ref)
            dg_ref[...] = jnp.zeros_like(dg_ref)

        fv = f_ref[...]
        r = _rstd(fv)
        fh = fv * r
        err = (x2_ref[...] + fh * g_ref[...]) - t_ref[...]
        loss_ref[...] += jnp.sum(err * err, axis=0, keepdims=True)
        dout = err * (1.0 / d)
        dout_ref[...] = dout
        dfv, dg = _rms_bwd(dout, fh, r, g_ref[...])
        df_ref[...] = dfv.astype(df_ref.dtype)
        dg_ref[...] += dg

    return pl.pallas_call(
        body, grid=(t // tm,),
        in_specs=[_row_spec(tm, d), _row_spec(tm, d), _row_spec(tm, d), _vec_spec(d)],
        out_specs=[_vec_spec(d), _row_spec(tm, d), _row_spec(tm, d), _vec_spec(d)],
        out_shape=[S((1, d), F32), S((t, d), F32), S((t, d), _MXU), S((1, d), F32)],
        compiler_params=_cp("arbitrary"), name=name)(f, x2, target, g_post)


def _mid_bwd(dhn_a, dhn_b, dout, x2, mix, g_pre, g_post, name):
    t, d = x2.shape
    tm = min(t, ROW_TILE)

    def body(da_ref, db_ref, do_ref, x2_ref, m_ref, gn_ref, gp_ref, dx2_ref, dm_ref, dgn_ref, dgp_ref):
        @pl.when(pl.program_id(0) == 0)
        def _():
            dgn_ref[...] = jnp.zeros_like(dgn_ref)
            dgp_ref[...] = jnp.zeros_like(dgp_ref)

        x2 = x2_ref[...]
        r = _rstd(x2)
        dxa, dgn = _rms_bwd(da_ref[...] + db_ref[...], x2 * r, r, gn_ref[...])
        dx2 = do_ref[...] + dxa
        dx2_ref[...] = dx2
        dgn_ref[...] += dgn
        mv = m_ref[...]
        rm = _rstd(mv)
        dmv, dgp = _rms_bwd(dx2, mv * rm, rm, gp_ref[...])
        dm_ref[...] = dmv.astype(dm_ref.dtype)
        dgp_ref[...] += dgp

    rs, vs = _row_spec(tm, d), _vec_spec(d)
    return pl.pallas_call(
        body, grid=(t // tm,), in_specs=[rs, rs, rs, rs, rs, vs, vs], out_specs=[rs, rs, vs, vs],
        out_shape=[S((t, d), F32), S((t, d), _MXU), S((1, d), F32), S((1, d), F32)],
        compiler_params=_cp("arbitrary"), name=name)(dhn_a, dhn_b, dout, x2, mix, g_pre, g_post)


def _first_bwd(dhn, dx2, x, gain, name):
    t, d = x.shape
    tm = min(t, ROW_TILE)

    def body(dh_ref, dx2_ref, x_ref, g_ref, dx_ref, dg_ref):
        @pl.when(pl.program_id(0) == 0)
        def _():
            dg_ref[...] = jnp.zeros_like(dg_ref)

        xv = x_ref[...]
        r = _rstd(xv)
        dxa, dg = _rms_bwd(dh_ref[...], xv * r, r, g_ref[...])
        dx_ref[...] = dx2_ref[...] + dxa
        dg_ref[...] += dg

    rs, vs = _row_spec(tm, d), _vec_spec(d)
    return pl.pallas_call(body, grid=(t // tm,), in_specs=[rs, rs, rs, vs], out_specs=[rs, vs],
                          out_shape=[S((t, d), F32), S((1, d), F32)], compiler_params=_cp("arbitrary"), name=name)(dhn, dx2, x, gain)


def _outnorm_bwd(dy, o, yl, ga, gl, name):
    t, w = o.shape
    tm = min(t, ROW_TILE)

    def body(dy_ref, o_ref, l_ref, ga_ref, gl_ref, do_ref, dl_ref, dga_ref, dgl_ref):
        @pl.when(pl.program_id(0) == 0)
        def _():
            dga_ref[...] = jnp.zeros_like(dga_ref)
            dgl_ref[...] = jnp.zeros_like(dgl_ref)

        ov, lv = o_ref[...], l_ref[...]
        ra, rl = _rstd(ov), _rstd(lv)
        dov, dga = _rms_bwd(dy_ref[:, :w], ov * ra, ra, ga_ref[...])
        dlv, dgl = _rms_bwd(dy_ref[:, w:], lv * rl, rl, gl_ref[...])
        do_ref[...] = dov
        dl_ref[...] = dlv
        dga_ref[...] += dga
        dgl_ref[...] += dgl

    rs, vs = _row_spec(tm, w), _vec_spec(w)
    return pl.pallas_call(body, grid=(t // tm,), in_specs=[_row_spec(tm, 2 * w), rs, rs, vs, vs], out_specs=[rs, rs, vs, vs],
                          out_shape=[S((t, w), F32), S((t, w), F32), S((1, w), F32), S((1, w), F32)],
                          compiler_params=_cp("arbitrary"), name=name)(dy, o, yl, ga, gl)


def _split_dot(v, tri):
    hi = v.astype(_MXU)
    lo = (v - hi.astype(F32)).astype(_MXU)
    return _dot(hi, tri) + _dot(lo, tri)


def _attn_tile(qb, kb, row, col, shift, scale):
    z = _dot_nt(qb, kb) * scale
    mask = (col + shift) < row
    lb = _log_sigmoid(z)
    lm = jnp.where(mask, lb - z, 0.0)
    return mask, lb, lm


def _attn_fwd(proj, n_heads, name):
    t = proj.shape[0]
    bq = HEAD_DIM
    nq = t // bq
    scale = 1.0 / math.sqrt(HEAD_DIM)

    def body(q_ref, k_ref, v_ref, o_ref, kb_ref, vb_ref):
        kb_ref[...] = k_ref[...].astype(_MXU)
        vb_ref[...] = v_ref[...].astype(_MXU)
        row = lax.broadcasted_iota(jnp.int32, (bq, bq), 0)
        col = lax.broadcasted_iota(jnp.int32, (bq, bq), 1)
        tri = (row > col).astype(_MXU)

        def per_q(qi, _):
            q0 = pl.multiple_of(qi * bq, bq)
            qb = q_ref[pl.ds(q0, bq), :].astype(_MXU)

            def cond(st):
                return jnp.logical_and(st[0] >= 0, st[1])

            def step(st):
                kj, _, carry, acc = st
                k0 = pl.multiple_of(kj * bq, bq)
                mask, lb, lm = _attn_tile(qb, kb_ref[pl.ds(k0, bq), :], row, col, (kj - qi) * bq, scale)
                w = jnp.where(mask, jnp.exp(lb + _split_dot(lm, tri) + carry), 0.0)
                acc = acc + _dot(w.astype(_MXU), vb_ref[pl.ds(k0, bq), :])
                carry = carry + jnp.sum(lm, axis=1, keepdims=True)
                return kj - 1, jnp.max(carry) > EXP_CUT, carry, acc

            st = lax.while_loop(cond, step, (qi, jnp.bool_(True), jnp.zeros((bq, 1), F32), jnp.zeros((bq, HEAD_DIM), F32)))
            o_ref[pl.ds(q0, bq), :] = st[3]
            return 0

        lax.fori_loop(0, nq, per_q, 0)

    hs = lambda off: pl.BlockSpec((t, HEAD_DIM), lambda h: (0, off + h))
    return pl.pallas_call(
        body, grid=(n_heads,), in_specs=[hs(0), hs(n_heads), hs(2 * n_heads)], out_specs=hs(0),
        out_shape=S((t, n_heads * HEAD_DIM), F32),
        scratch_shapes=[pltpu.VMEM((t, HEAD_DIM), _MXU), pltpu.VMEM((t, HEAD_DIM), _MXU)],
        compiler_params=_cp("parallel"), name=name)(proj, proj, proj)


def _attn_bwd(proj, do, n_heads, name):
    t = proj.shape[0]
    bq = HEAD_DIM
    nq = t // bq
    scale = 1.0 / math.sqrt(HEAD_DIM)

    def body(q_ref, k_ref, v_ref, do_ref, dq_ref, dk_ref, dv_ref, kb_ref, vb_ref, dka_ref, dva_ref, g_ref, b_ref):
        kb_ref[...] = k_ref[...].astype(_MXU)
        vb_ref[...] = v_ref[...].astype(_MXU)
        dka_ref[...] = jnp.zeros_like(dka_ref)
        dva_ref[...] = jnp.zeros_like(dva_ref)
        row = lax.broadcasted_iota(jnp.int32, (bq, bq), 0)
        col = lax.broadcasted_iota(jnp.int32, (bq, bq), 1)
        tri = (row > col).astype(_MXU)
        tri_lt = (row < col).astype(_MXU)

        def per_q(qi, _):
            q0 = pl.multiple_of(qi * bq, bq)
            qb = q_ref[pl.ds(q0, bq), :].astype(_MXU)
            dob = do_ref[pl.ds(q0, bq), :].astype(_MXU)

            def cond(st):
                return jnp.logical_and(st[0] >= 0, st[1])

            def step(st):
                kj, _, carry = st
                k0 = pl.multiple_of(kj * bq, bq)
                mask, lb, lm = _attn_tile(qb, kb_ref[pl.ds(k0, bq), :], row, col, (kj - qi) * bq, scale)
                w = jnp.where(mask, jnp.exp(lb + _split_dot(lm, tri) + carry), 0.0)
                g_ref[pl.ds(k0, bq), :] = w * _dot_nt(dob, vb_ref[pl.ds(k0, bq), :])
                b_ref[pl.ds(k0, bq), :] = jnp.where(mask, jnp.exp(lb), 0.0)
                dva_ref[pl.ds(k0, bq), :] += _dot_tn(w.astype(_MXU), dob)
                carry = carry + jnp.sum(lm, axis=1, keepdims=True)
                return kj - 1, jnp.max(carry) > EXP_CUT, carry

            st = lax.while_loop(cond, step, (qi, jnp.bool_(True), jnp.zeros((bq, 1), F32)))

            def back(kj, st2):
                before, dq = st2
                k0 = pl.multiple_of(kj * bq, bq)
                kb = kb_ref[pl.ds(k0, bq), :]
                g = g_ref[pl.ds(k0, bq), :]
                beta = b_ref[pl.ds(k0, bq), :]
                dz = ((g * (1.0 - beta) - (before + _split_dot(g, tri_lt)) * beta) * scale).astype(_MXU)
                dka_ref[pl.ds(k0, bq), :] += _dot_tn(dz, qb)
                return before + jnp.sum(g, axis=1, keepdims=True), dq + _dot(dz, kb)

            st2 = lax.fori_loop(st[0] + 1, qi + 1, back, (jnp.zeros((bq, 1), F32), jnp.zeros((bq, HEAD_DIM), F32)))
            dq_ref[pl.ds(q0, bq), :] = st2[1].astype(dq_ref.dtype)
            return 0

        lax.fori_loop(0, nq, per_q, 0)
        dk_ref[...] = dka_ref[...].astype(dk_ref.dtype)
        dv_ref[...] = dva_ref[...].astype(dv_ref.dtype)

    hs = lambda off: pl.BlockSpec((t, HEAD_DIM), lambda h: (0, off + h))
    w = n_heads * HEAD_DIM
    return pl.pallas_call(
        body, grid=(n_heads,), in_specs=[hs(0), hs(n_heads), hs(2 * n_heads), hs(0)], out_specs=[hs(0), hs(0), hs(0)],
        out_shape=[S((t, w), _MXU)] * 3,
        scratch_shapes=[pltpu.VMEM((t, HEAD_DIM), _MXU), pltpu.VMEM((t, HEAD_DIM), _MXU)] + [pltpu.VMEM((t, HEAD_DIM), F32)] * 4,
        compiler_params=_cp("parallel"), name=name)(proj, proj, proj, do)


def _shift_down(cur, prev8, k):
    if k == 0:
        return cur
    row8 = lax.broadcasted_iota(jnp.int32, prev8.shape, 0)
    rc = pltpu.roll(cur, k, 0)
    top = jnp.where(row8 < k, pltpu.roll(prev8, k, 0), rc[0:8, :])
    return jnp.concatenate([top, rc[8:, :]], axis=0)


def _shift_up(cur, next8, k):
    if k == 0:
        return cur
    n = cur.shape[0]
    row8 = lax.broadcasted_iota(jnp.int32, next8.shape, 0)
    rc = pltpu.roll(cur, n - k, 0)
    bottom = jnp.where(row8 >= 8 - k, pltpu.roll(next8, 8 - k, 0), rc[n - 8:, :])
    return jnp.concatenate([rc[:n - 8, :], bottom], axis=0)


def _lru_gates(xl, prev8, cw, cb, wr, br, wi, bi, ls):
    xs = [_shift_down(xl, prev8, CONV_WIDTH - 1 - k) for k in range(CONV_WIDTH)]
    xc = xs[0] * cw[0:1, :]
    for k in range(1, CONV_WIDTH):
        xc = xc + xs[k] * cw[k:k + 1, :]
    xc = xc + cb
    xcb = xc.astype(_MXU)
    r = jax.nn.sigmoid(_dot(xcb, wr) + br)
    i = jax.nn.sigmoid(_dot(xcb, wi) + bi)
    la = (LRU_C * r) * ls
    a = jnp.exp(la)
    mult = jnp.sqrt(-_expm1(2.0 * la))
    return xs, xc, r, i, a, mult


def _group_scan(a, b, reverse):
    n = a.shape[0]
    row = lax.broadcasted_iota(jnp.int32, a.shape, 0) % 8
    for d in (1, 2, 4):
        if reverse:
            m = row < 8 - d
            a_s, b_s = pltpu.roll(a, n - d, 0), pltpu.roll(b, n - d, 0)
        else:
            m = row >= d
            a_s, b_s = pltpu.roll(a, d, 0), pltpu.roll(b, d, 0)
        b = jnp.where(m, a * b_s + b, b)
        a = jnp.where(m, a * a_s, a)
    return a, b


def _lru_fwd(proj, col0, n_blocks, cw, cb, wr, br, wi, bi, lam, name):
    t = proj.shape[0]
    tt = min(t, SEQ_TILE)
    nt = t // tt

    def body(xl_ref, gl_ref, cw_ref, cb_ref, wr_ref, br_ref, wi_ref, bi_ref, lam_ref, h_ref, y_ref):
        cwv, cbv, brv, biv = cw_ref[...], cb_ref[...], br_ref[...], bi_ref[...]
        wrv, wiv = wr_ref[...].astype(_MXU), wi_ref[...].astype(_MXU)
        ls = _log_sigmoid(lam_ref[...])

        def tile(ti, hin):
            t0 = pl.multiple_of(ti * tt, tt)
            p0 = pl.multiple_of(jnp.maximum(t0 - 8, 0), 8)
            prev8 = xl_ref[pl.ds(p0, 8), :] * (ti > 0).astype(F32)
            xl = xl_ref[pl.ds(t0, tt), :]
            _, xc, _, ig, a, mult = _lru_gates(xl, prev8, cwv, cbv, wrv, brv, wiv, biv, ls)
            ga, gb = _group_scan(a, mult * (ig * xc), False)
            for g in range(tt // 8):
                hg = ga[8 * g:8 * g + 8, :] * hin + gb[8 * g:8 * g + 8, :]
                h_ref[pl.ds(t0 + 8 * g, 8), :] = hg
                hin = hg[7:8, :]
            y_ref[pl.ds(t0, tt), :] = h_ref[pl.ds(t0, tt), :] * _gelu(gl_ref[pl.ds(t0, tt), :])
            return hin

        lax.fori_loop(0, nt, tile, jnp.zeros((1, HEAD_DIM), F32))

    cs = lambda off: pl.BlockSpec((t, HEAD_DIM), lambda n: (0, off + n))
    vs = pl.BlockSpec((1, HEAD_DIM), lambda n: (0, n))
    ws = pl.BlockSpec((None, HEAD_DIM, HEAD_DIM), lambda n: (n, 0, 0))
    w = n_blocks * HEAD_DIM
    return pl.pallas_call(
        body, grid=(n_blocks,),
        in_specs=[cs(col0), cs(col0 + n_blocks), pl.BlockSpec((CONV_WIDTH, HEAD_DIM), lambda n: (0, n)), vs, ws, vs, ws, vs, vs],
        out_specs=[cs(0), cs(0)], out_shape=[S((t, w), F32), S((t, w), F32)],
        compiler_params=_cp("parallel"), name=name)(proj, proj, cw, cb, wr, br, wi, bi, lam)


def _lru_bwd(proj, col0, n_blocks, h, dyl, cw, cb, wr, br, wi, bi, lam, name):
    t = proj.shape[0]
    tt = min(t, SEQ_TILE)
    nt = t // tt

    def body(xl_ref, gl_ref, h_ref, dy_ref, cw_ref, cb_ref, wr_ref, br_ref, wi_ref, bi_ref, lam_ref,
             dxl_ref, dgl_ref, dcw_ref, dcb_ref, dwr_ref, dbr_ref, dwi_ref, dbi_ref, dlam_ref, g_ref):
        cwv, cbv, brv, biv = cw_ref[...], cb_ref[...], br_ref[...], bi_ref[...]
        wrv, wiv = wr_ref[...].astype(_MXU), wi_ref[...].astype(_MXU)
        lamv = lam_ref[...]
        ls = _log_sigmoid(lamv)
        for ref in (dcw_ref, dcb_ref, dwr_ref, dbr_ref, dwi_ref, dbi_ref, dlam_ref):
            ref[...] = jnp.zeros_like(ref)

        def tile(s, carry):
            e_in, dxc_next8 = carry
            ti = nt - 1 - s
            t0 = pl.multiple_of(ti * tt, tt)
            p0 = pl.multiple_of(jnp.maximum(t0 - 8, 0), 8)
            first = (ti > 0).astype(F32)
            xl = xl_ref[pl.ds(t0, tt), :]
            xs, xc, r, ig, a, mult = _lru_gates(xl, xl_ref[pl.ds(p0, 8), :] * first, cwv, cbv, wrv, brv, wiv, biv, ls)
            hv = h_ref[pl.ds(t0, tt), :]
            h_before = _shift_down(hv, h_ref[pl.ds(p0, 8), :] * first, 1)
            glv = gl_ref[pl.ds(t0, tt), :]
            dyv = dy_ref[pl.ds(t0, tt), :]
            dgl_ref[pl.ds(t0, tt), :] = (dyv * hv * _gelu_grad(glv)).astype(dgl_ref.dtype)
            dh = dyv * _gelu(glv)
            row = lax.broadcasted_iota(jnp.int32, a.shape, 0)
            coef = jnp.where(row == tt - 1, 1.0, pltpu.roll(a, tt - 1, 0))
            ga, gb = _group_scan(coef, dh, True)
            gin = e_in
            for g in reversed(range(tt // 8)):
                gg = ga[8 * g:8 * g + 8, :] * gin + gb[8 * g:8 * g + 8, :]
                g_ref[8 * g:8 * g + 8, :] = gg
                gin = gg[0:1, :]
            gv = g_ref[...]
            e_out = a[0:1, :] * gv[0:1, :]
            ix = ig * xc
            dla = (gv * h_before) * a - (gv * ix) * (a * a / mult)
            dlam_ref[...] += jnp.sum(dla * (LRU_C * r), axis=0, keepdims=True)
            dpr = (dla * (LRU_C * ls)) * (r * (1.0 - r))
            dpi = (gv * mult * xc) * (ig * (1.0 - ig))
            dbr_ref[...] += jnp.sum(dpr, axis=0, keepdims=True)
            dbi_ref[...] += jnp.sum(dpi, axis=0, keepdims=True)
            xcb, dprb, dpib = xc.astype(_MXU), dpr.astype(_MXU), dpi.astype(_MXU)
            dwr_ref[...] += _dot_tn(xcb, dprb)
            dwi_ref[...] += _dot_tn(xcb, dpib)
            dxc = gv * mult * ig + _dot_nt(dprb, wrv) + _dot_nt(dpib, wiv)
            dcb_ref[...] += jnp.sum(dxc, axis=0, keepdims=True)
            dxl = None
            for k in range(CONV_WIDTH):
                dcw_ref[k:k + 1, :] += jnp.sum(dxc * xs[k], axis=0, keepdims=True)
                term = _shift_up(dxc, dxc_next8, CONV_WIDTH - 1 - k) * cwv[k:k + 1, :]
                dxl = term if dxl is None else dxl + term
            dxl_ref[pl.ds(t0, tt), :] = dxl.astype(dxl_ref.dtype)
            return e_out, dxc[0:8, :]

        lax.fori_loop(0, nt, tile, (jnp.zeros((1, HEAD_DIM), F32), jnp.zeros((8, HEAD_DIM), F32)))
        dlam_ref[...] = dlam_ref[...] * (1.0 - jax.nn.sigmoid(lamv))

    cs = lambda off: pl.BlockSpec((t, HEAD_DIM), lambda n: (0, off + n))
    vs = pl.BlockSpec((1, HEAD_DIM), lambda n: (0, n))
    ws = pl.BlockSpec((None, HEAD_DIM, HEAD_DIM), lambda n: (n, 0, 0))
    cws = pl.BlockSpec((CONV_WIDTH, HEAD_DIM), lambda n: (0, n))
    w = n_blocks * HEAD_DIM
    vec = S((1, w), F32)
    mat = S((n_blocks, HEAD_DIM, HEAD_DIM), F32)
    return pl.pallas_call(
        body, grid=(n_blocks,),
        in_specs=[cs(col0), cs(col0 + n_blocks), cs(0), cs(0), cws, vs, ws, vs, ws, vs, vs],
        out_specs=[cs(0), cs(0), cws, vs, ws, vs, ws, vs, vs],
        out_shape=[S((t, w), _MXU), S((t, w), _MXU), S((CONV_WIDTH, w), F32), vec, mat, vec, mat, vec, vec],
        scratch_shapes=[pltpu.VMEM((tt, HEAD_DIM), F32)],
        compiler_params=_cp("parallel"), name=name)(proj, proj, h, dyl, cw, cb, wr, br, wi, bi, lam)


def _local_step(x, target, norms, win3, wout, wg3, wu3, wd, cw, cb, wr, br, wi, bi, lam, ga, gl):
    g_pre_mix, g_post_mix, g_pre_ffn, g_post_ffn = norms
    t, d = x.shape
    c = win3.shape[0]
    mix = wout.shape[0]
    aw = mix // 2
    n_heads = aw // HEAD_DIM
    n_blocks = (mix - aw) // HEAD_DIM
    ff = wd.shape[0]
    bm = min(t, 512)

    hn1 = _rms_fwd(x, g_pre_mix, "rms1")
    proj = _mm_nn(hn1, win3, bm=bm, bn=win3.shape[2], bk=d, out_dtype=F32, name="in_proj")
    o = _attn_fwd(proj, n_heads, "attn_fwd")
    h, yl = _lru_fwd(proj, 3 * n_heads, n_blocks, cw, cb, wr, br, wi, bi, lam, "lru_fwd")
    y = _outnorm_fwd(o, yl, ga, gl, "outnorm_fwd")
    mixo = _mm_nn(y, wout[None], bm=bm, bn=d, bk=mix, out_dtype=F32, name="out_proj")
    x2, hn2 = _mid_fwd(x, mixo, g_post_mix, g_pre_ffn, "mid_fwd")
    gate, up, act = _swiglu_fwd(hn2, wg3, wu3, bm=min(t, 256), name="ffn_gate_up")
    f = _mm_nn(act, wd[None], bm=bm, bn=d, bk=ff // 4, out_dtype=F32, name="ffn_down")
    loss_cols, dout, df, dg_post_ffn = _final(f, x2, target, g_post_ffn, "final")

    dgate, dup = _swiglu_bwd(df, wd, gate, up, bm=bm, bo=ff // 4, name="ffn_down_bwd")
    d_wd = _mm_tn(act, df, 1, bm=bm, bk=512, name="ffn_down_dw")
    d_wg3 = _mm_tn(hn2, dgate, c, bm=bm, bk=d // 2, name="ffn_gate_dw")
    d_wu3 = _mm_tn(hn2, dup, c, bm=bm, bk=d // 2, name="ffn_up_dw")
    dhn2_g = _mm_nt(dgate, wg3, bm=bm, bo=d, out_dtype=F32, name="ffn_gate_dx")
    dhn2_u = _mm_nt(dup, wu3, bm=bm, bo=d, out_dtype=F32, name="ffn_up_dx")
    dx2, dmix, dg_pre_ffn, dg_post_mix = _mid_bwd(dhn2_g, dhn2_u, dout, x2, mixo, g_pre_ffn, g_post_mix, "mid_bwd")
    dy = _mm_nt(dmix, wout[None], bm=bm, bo=mix, out_dtype=F32, name="out_proj_dx")
    d_wout = _mm_tn(y, dmix, 1, bm=bm, bk=mix // 2, name="out_proj_dw")
    do, dyl, dga, dgl_norm = _outnorm_bwd(dy, o, yl, ga, gl, "outnorm_bwd")
    dxl, dglu, dcw, dcb, dwr, dbr, dwi, dbi, dlam = _lru_bwd(proj, 3 * n_heads, n_blocks, h, dyl, cw, cb, wr, br, wi, bi, lam, "lru_bwd")
    dq, dk, dv = _attn_bwd(proj, do, n_heads, "attn_bwd")
    dproj = jnp.concatenate([dq, dk, dv, dxl, dglu], axis=1)
    dhn1 = _mm_nt(dproj, win3, bm=bm, bo=d, out_dtype=F32, name="in_proj_dx")
    d_win3 = _mm_tn(hn1, dproj, c, bm=bm, bk=d // 2, name="in_proj_dw")
    grad_x, dg_pre_mix = _first_bwd(dhn1, dx2, x, g_pre_mix, "first_bwd")

    big = (d_win3, d_wout.reshape(c, mix // c, d), d_wg3, d_wu3, d_wd.reshape(c, ff // c, d))
    small = dict(pre_mix_norm=dg_pre_mix, post_mix_norm=dg_post_mix, pre_ffn_norm=dg_pre_ffn, post_ffn_norm=dg_post_ffn,
                 conv_w=dcw, conv_b=dcb, w_rgate=dwr, b_rgate=dbr, w_igate=dwi, b_igate=dbi, lru_lambda=dlam,
                 attn_out_norm=dga, lru_out_norm=dgl_norm)
    return loss_cols, grad_x, big, small


_ANY = pl.BlockSpec(memory_space=pl.ANY)


def _place():
    x, y, c = lax.axis_index("x"), lax.axis_index("y"), lax.axis_index("c")
    return x, y, c, [(1 - x, y), (x, 1 - y), (1 - x, 1 - y)]


def _remote(src, dst, send_sem, recv_sem, to):
    return pltpu.make_async_remote_copy(src_ref=src, dst_ref=dst, send_sem=send_sem, recv_sem=recv_sem,
                                        device_id=to, device_id_type=MESH)


def _all_gather(shards, split, name):
    n = len(shards)

    def body(*refs):
        ins, outs = refs[:n], refs[n:2 * n]
        ici_send, ici_recv, d2d_send, d2d_recv, loc = refs[2 * n:]
        x, y, c, chips = _place()
        me = 2 * x + y

        def part(ref, a, cc):
            if not split[a]:
                return ref
            half = shards[a].shape[0] // 2
            return ref.at[pl.ds(cc * half, half)]

        started, local = [], []
        for a in range(n):
            lc = pltpu.make_async_copy(ins[a], outs[a].at[me], loc.at[a])
            lc.start()
            local.append(lc)
            for j, (px, py) in enumerate(chips):
                cp = _remote(part(ins[a], a, c), part(outs[a].at[me], a, c), ici_send.at[3 * a + j], ici_recv.at[3 * a + j], (px, py, c))
                cp.start()
                started.append(cp)
        for a in range(n):
            for j, (px, py) in enumerate(chips):
                land = part(outs[a].at[2 * px + py], a, c)
                _remote(land, land, ici_send.at[3 * a + j], ici_recv.at[3 * a + j], (px, py, c)).wait_recv()
                if split[a]:
                    fw = _remote(land, land, d2d_send.at[3 * a + j], d2d_recv.at[3 * a + j], (x, y, 1 - c))
                    fw.start()
                    started.append(fw)
        for a in range(n):
            if split[a]:
                for j, (px, py) in enumerate(chips):
                    land = part(outs[a].at[2 * px + py], a, 1 - c)
                    _remote(land, land, d2d_send.at[3 * a + j], d2d_recv.at[3 * a + j], (x, y, 1 - c)).wait_recv()
        for cp in started:
            cp.wait_send()
        for lc in local:
            lc.wait()

    sem = pltpu.SemaphoreType.DMA((3 * n,))
    return pl.pallas_call(
        body, in_specs=[_ANY] * n, out_specs=[_ANY] * n,
        out_shape=[S((4,) + s.shape, s.dtype) for s in shards],
        scratch_shapes=[sem, sem, sem, sem, pltpu.SemaphoreType.DMA((n,))], name=name)(*shards)


def _half_swap(grads, name):
    n = len(grads)

    def body(*refs):
        ins, outs = refs[:n], refs[n:2 * n]
        send, recv = refs[2 * n:]
        x, y, c, _ = _place()
        started = []
        for a in range(n):
            half = grads[a].shape[1] // 2
            cp = _remote(ins[a].at[:, pl.ds((1 - c) * half, half)], outs[a], send.at[a], recv.at[a], (x, y, 1 - c))
            cp.start()
            started.append(cp)
        for cp in started:
            cp.wait()

    sem = pltpu.SemaphoreType.DMA((n,))
    return pl.pallas_call(
        body, in_specs=[_ANY] * n, out_specs=[_ANY] * n,
        out_shape=[S((4, g.shape[1] // 2, g.shape[2]), g.dtype) for g in grads],
        scratch_shapes=[sem, sem], name=name)(*grads)


def _chip_scatter(parts, name):
    n = len(parts)

    def body(*refs):
        ins, outs = refs[:n], refs[n:2 * n]
        send, recv, loc = refs[2 * n:]
        x, y, c, chips = _place()
        me = 2 * x + y
        started, local = [], []
        for a in range(n):
            lc = pltpu.make_async_copy(ins[a].at[me], outs[a].at[me], loc.at[a])
            lc.start()
            local.append(lc)
            for j, (px, py) in enumerate(chips):
                cp = _remote(ins[a].at[2 * px + py], outs[a].at[me], send.at[3 * a + j], recv.at[3 * a + j], (px, py, c))
                cp.start()
                started.append(cp)
        for a in range(n):
            for j, (px, py) in enumerate(chips):
                land = outs[a].at[2 * px + py]
                _remote(land, land, send.at[3 * a + j], recv.at[3 * a + j], (px, py, c)).wait_recv()
        for cp in started:
            cp.wait_send()
        for lc in local:
            lc.wait()

    sem = pltpu.SemaphoreType.DMA((3 * n,))
    return pl.pallas_call(
        body, in_specs=[_ANY] * n, out_specs=[_ANY] * n, out_shape=[S(p.shape, p.dtype) for p in parts],
        scratch_shapes=[sem, sem, pltpu.SemaphoreType.DMA((n,))], name=name)(*parts)


def _half_share(halves, name):
    n = len(halves)

    def body(*refs):
        ins, outs = refs[:n], refs[n:2 * n]
        send, recv, loc = refs[2 * n:]
        x, y, c, _ = _place()
        started, local = [], []
        for a in range(n):
            r = halves[a].shape[0]
            lc = pltpu.make_async_copy(ins[a], outs[a].at[pl.ds(c * r, r)], loc.at[a])
            lc.start()
            local.append(lc)
            cp = _remote(ins[a], outs[a].at[pl.ds(c * r, r)], send.at[a], recv.at[a], (x, y, 1 - c))
            cp.start()
            started.append(cp)
        for a in range(n):
            r = halves[a].shape[0]
            land = outs[a].at[pl.ds((1 - c) * r, r)]
            _remote(land, land, send.at[a], recv.at[a], (x, y, 1 - c)).wait_recv()
        for cp in started:
            cp.wait_send()
        for lc in local:
            lc.wait()

    sem = pltpu.SemaphoreType.DMA((n,))
    return pl.pallas_call(
        body, in_specs=[_ANY] * n, out_specs=[_ANY] * n,
        out_shape=[S((2 * h.shape[0], h.shape[1]), h.dtype) for h in halves],
        scratch_shapes=[sem, sem, sem], name=name)(*halves)


def _all_reduce_small(v, name):
    r = v.shape[0]

    def body(v_ref, o_ref, buf, send, recv):
        x, y, c, _ = _place()
        me = 4 * x + 2 * y + c
        buf[me] = v_ref[...]
        started = []
        for d in range(1, 8):
            to = (me + d) % 8
            cp = _remote(v_ref, buf.at[me], send.at[d - 1], recv.at[d - 1], (to // 4, (to // 2) % 2, to % 2))
            cp.start()
            started.append(cp)
        for d in range(1, 8):
            land = buf.at[(me + 8 - d) % 8]
            _remote(land, land, send.at[d - 1], recv.at[d - 1], (x, y, c)).wait_recv()
        for cp in started:
            cp.wait_send()
        acc = buf[0]
        for k in range(1, 8):
            acc = acc + buf[k]
        o_ref[...] = acc

    vm = pl.BlockSpec(memory_space=pltpu.VMEM)
    sem = pltpu.SemaphoreType.DMA((7,))
    return pl.pallas_call(
        body, in_specs=[vm], out_specs=vm, out_shape=S(v.shape, F32),
        scratch_shapes=[pltpu.VMEM((8, r, 128), F32), sem, sem],
        compiler_params=pltpu.CompilerParams(vmem_limit_bytes=VMEM_LIMIT), name=name)(v)


def _row_block(rows, cap):
    return max(b for b in range(8, cap + 1, 8) if rows % b == 0)


def _add_own_half(g, recv, core, name):
    _, rows, n = g.shape
    half = rows // 2
    rb = _row_block(half, 512)
    nb = half // rb

    def body(c_ref, g_ref, r_ref, o_ref):
        o_ref[...] = g_ref[...] + r_ref[...]

    return pl.pallas_call(
        body,
        grid_spec=pltpu.PrefetchScalarGridSpec(
            num_scalar_prefetch=1, grid=(4, nb),
            in_specs=[pl.BlockSpec((None, rb, n), lambda k, i, c_ref: (k, c_ref[0] * nb + i, 0)),
                      pl.BlockSpec((None, rb, n), lambda k, i, c_ref: (k, i, 0))],
            out_specs=pl.BlockSpec((None, rb, n), lambda k, i, c_ref: (k, i, 0))),
        out_shape=S((4, half, n), F32), compiler_params=_cp("parallel", "parallel"), name=name)(core, g, recv)


def _sum_chips(r4, name):
    _, rows, n = r4.shape
    rb = _row_block(rows, 64)

    def body(r_ref, o_ref):
        o_ref[...] = ((r_ref[0] + r_ref[1]) + r_ref[2]) + r_ref[3]

    return pl.pallas_call(body, grid=(rows // rb,), in_specs=[pl.BlockSpec((4, rb, n), lambda i: (0, i, 0))],
                          out_specs=pl.BlockSpec((rb, n), lambda i: (i, 0)), out_shape=S((rows, n), F32),
                          compiler_params=_cp("parallel"), name=name)(r4)


def _adamw(w, g, m, v, name):
    rows, n = w.shape
    rb = _row_block(rows, 128)
    c1 = 1.0 - ADAM_B1 ** ADAM_STEP
    c2 = 1.0 - ADAM_B2 ** ADAM_STEP

    def body(w_ref, g_ref, m_ref, v_ref, d_ref, nm_ref, nv_ref):
        gv = g_ref[...]
        nm = ADAM_B1 * m_ref[...] + (1.0 - ADAM_B1) * gv
        nv = ADAM_B2 * v_ref[...] + (1.0 - ADAM_B2) * (gv * gv)
        nm_ref[...] = nm
        nv_ref[...] = nv
        d_ref[...] = -ADAM_LR * ((nm / c1) / (jnp.sqrt(nv / c2) + ADAM_EPS) + ADAM_WD * w_ref[...])

    bs = pl.BlockSpec((rb, n), lambda i: (i, 0))
    return pl.pallas_call(body, grid=(rows // rb,), in_specs=[bs] * 4, out_specs=[bs] * 3, out_shape=[S((rows, n), F32)] * 3,
                          compiler_params=_cp("parallel"), name=name)(w, g, m, v)


_BIG = ("w_in", "w_out", "w_ffn_gate", "w_ffn_up", "w_ffn_down")
_SMALL = ("pre_mix_norm", "post_mix_norm", "pre_ffn_norm", "post_ffn_norm", "conv_w", "conv_b", "w_rgate", "b_rgate",
          "w_igate", "b_igate", "lru_lambda", "attn_out_norm", "lru_out_norm")
_WEIGHTS = ("pre_mix_norm", "post_mix_norm", "pre_ffn_norm", "post_ffn_norm", "w_in", "conv_w", "conv_b", "w_rgate", "b_rgate",
            "w_igate", "b_igate", "lru_lambda", "attn_out_norm", "lru_out_norm", "w_out", "w_ffn_gate", "w_ffn_up", "w_ffn_down")


def _pack(arrays):
    flat = []
    for a in arrays:
        f = a.reshape(-1)
        flat.append(jnp.pad(f, (0, (-f.shape[0]) % 1024)))
    return jnp.concatenate(flat).reshape(-1, 128)


def _unpack(packed, shapes):
    out, pos = [], 0
    flat = packed.reshape(-1)
    for s in shapes:
        size = math.prod(s)
        out.append(flat[pos:pos + size].reshape(s))
        pos += size + (-size) % 1024
    return out


def kernel(x, pre_mix_norm, post_mix_norm, pre_ffn_norm, post_ffn_norm, w_in, conv_w, conv_b, w_rgate, b_rgate, w_igate, b_igate, lru_lambda, attn_out_norm, lru_out_norm, w_out, w_ffn_gate, w_ffn_up, w_ffn_down, loss_target, m_pre_mix_norm, m_post_mix_norm, m_pre_ffn_norm, m_post_ffn_norm, m_w_in, m_conv_w, m_conv_b, m_w_rgate, m_b_rgate, m_w_igate, m_b_igate, m_lru_lambda, m_attn_out_norm, m_lru_out_norm, m_w_out, m_w_ffn_gate, m_w_ffn_up, m_w_ffn_down, v_pre_mix_norm, v_post_mix_norm, v_pre_ffn_norm, v_post_ffn_norm, v_w_in, v_conv_w, v_conv_b, v_w_rgate, v_b_rgate, v_w_igate, v_b_igate, v_lru_lambda, v_attn_out_norm, v_lru_out_norm, v_w_out, v_w_ffn_gate, v_w_ffn_up, v_w_ffn_down):
    given = dict(locals())
    w = {n: given[n][0] for n in _WEIGHTS}
    m = {n: given["m_" + n][0] for n in _WEIGHTS}
    v = {n: given["v_" + n][0] for n in _WEIGHTS}
    xs, target = x[0], loss_target[0]
    d = xs.shape[1]
    chip = 2 * lax.axis_index("x") + lax.axis_index("y")
    core = lax.axis_index("c").astype(jnp.int32).reshape(1)

    gathered = _all_gather([w[n].astype(_MXU) for n in _BIG] + [w["conv_w"]], [True] * 5 + [False], "gather_weights")
    win3, wout4, wg3, wu3, wd4, cw4 = gathered
    wout = wout4.reshape(-1, d)
    wd = wd4.reshape(-1, d)
    conv_full = jnp.transpose(cw4, (1, 0, 2)).reshape(CONV_WIDTH, -1)
    row = lambda a: a.reshape(1, -1)
    norms = tuple(row(w[n]) for n in ("pre_mix_norm", "post_mix_norm", "pre_ffn_norm", "post_ffn_norm"))

    loss_cols, grad_x, big, small = _local_step(
        xs, target, norms, win3, wout, wg3, wu3, wd, conv_full, row(w["conv_b"]), w["w_rgate"], row(w["b_rgate"]),
        w["w_igate"], row(w["b_igate"]), row(w["lru_lambda"]), row(w["attn_out_norm"]), row(w["lru_out_norm"]))

    loss = lax.psum(0.5 * jnp.sum(loss_cols) / d, ("x", "y", "c"))

    from_sibling = _half_swap(list(big), "grads_half_swap")
    chip_part = [_add_own_half(g, r, core, "grads_add_%d" % i) for i, (g, r) in enumerate(zip(big, from_sibling))]
    from_chips = _chip_scatter(chip_part, "grads_chip_scatter")
    halves = [_sum_chips(r4, "grads_sum_%d" % i) for i, r4 in enumerate(from_chips)]
    reduced = dict(zip(_BIG, _half_share(halves, "grads_half_share")))

    small_sum = _unpack(_all_reduce_small(_pack([small[n] for n in _SMALL]), "grads_small_all_reduce"), [small[n].shape for n in _SMALL])
    for n, g in zip(_SMALL, small_sum):
        reduced[n] = g.reshape(w[n].shape) if n != "conv_w" else lax.dynamic_slice_in_dim(g, chip * w[n].shape[1], w[n].shape[1], axis=1)

    delta, new_m, new_v = {}, {}, {}
    for n in _BIG:
        delta[n], new_m[n], new_v[n] = _adamw(w[n], reduced[n], m[n], v[n], "adamw_" + n)
    shapes = [w[n].shape for n in _SMALL]
    packed = _adamw(*[_pack([src[n] for n in _SMALL]) for src in (w, reduced, m, v)], "adamw_small")
    for out, p in zip((delta, new_m, new_v), packed):
        out.update(zip(_SMALL, _unpack(p, shapes)))

    lead = lambda a: a[None]
    return (loss, lead(grad_x), *[lead(reduced[n]) for n in _WEIGHTS], *[lead(delta[n]) for n in _WEIGHTS],
            *[lead(new_m[n]) for n in _WEIGHTS], *[lead(new_v[n]) for n in _WEIGHTS])
```

```python
import functools
import math

import jax
import jax.numpy as jnp
from jax import lax
from jax.experimental import pallas as pl
from jax.experimental.pallas import tpu as pltpu

F32 = jnp.float32
BF16 = jnp.bfloat16
_MXU = BF16
S = jax.ShapeDtypeStruct

RMS_EPS = 1e-6
HEAD_DIM = 128
CONV_WIDTH = 4
LRU_C = 8.0
ADAM_LR, ADAM_B1, ADAM_B2, ADAM_EPS, ADAM_WD, ADAM_STEP = 0.001, 0.9, 0.999, 1e-08, 0.01, 10
EXP_CUT = -105.0
VMEM_LIMIT = 56 * 1024 * 1024
ROW_TILE = 256
SEQ_TILE = 256
ATTN_BLOCK = 256
MESH = pl.DeviceIdType.MESH


def _cp(*sem):
    return pltpu.CompilerParams(dimension_semantics=sem, vmem_limit_bytes=VMEM_LIMIT)


def _dot(a, b):
    return jnp.dot(a, b, preferred_element_type=F32)


def _dot_nt(a, b):
    return lax.dot_general(a, b, (((1,), (1,)), ((), ())), preferred_element_type=F32)


def _dot_tn(a, b):
    return lax.dot_general(a, b, (((0,), (0,)), ((), ())), preferred_element_type=F32)


def _rstd(v):
    return lax.rsqrt(jnp.mean(v * v, axis=-1, keepdims=True) + RMS_EPS)


def _rms_bwd(dn, vh, r, gain):
    dvh = dn * gain
    dv = r * (dvh - vh * jnp.mean(dvh * vh, axis=-1, keepdims=True))
    return dv, jnp.sum(dn * vh, axis=0, keepdims=True)


def _log_sigmoid(z):
    return jnp.minimum(z, 0.0) - jnp.log(1.0 + jnp.exp(-jnp.abs(z)))


def _expm1(v):
    small = v * (1.0 + v * (0.5 + v * (1.0 / 6.0 + v * (1.0 / 24.0 + v * (1.0 / 120.0)))))
    return jnp.where(jnp.abs(v) < 0.04, small, jnp.exp(v) - 1.0)


_GELU_C = math.sqrt(2.0 / math.pi)


def _gelu(v):
    return 0.5 * v * (1.0 + jnp.tanh(_GELU_C * (v + 0.044715 * v * v * v)))


def _gelu_grad(v):
    th = jnp.tanh(_GELU_C * (v + 0.044715 * v * v * v))
    return 0.5 * (1.0 + th) + 0.5 * v * (1.0 - th * th) * _GELU_C * (1.0 + 3.0 * 0.044715 * v * v)


def _row_spec(tm, d):
    return pl.BlockSpec((tm, d), lambda i: (i, 0))


def _vec_spec(d):
    return pl.BlockSpec((1, d), lambda i: (0, 0))


def _mm_nn(a, b3, *, bm, bn, bk, out_dtype, name):
    m, k = a.shape
    c, _, n = b3.shape
    ni, nj, nk = m // bm, n // bn, k // bk

    def body(a_ref, b_ref, o_ref, *acc):
        if nk == 1:
            o_ref[...] = _dot(a_ref[...], b_ref[...]).astype(o_ref.dtype)
            return
        kk = pl.program_id(3)

        @pl.when(kk == 0)
        def _():
            acc[0][...] = jnp.zeros_like(acc[0])

        acc[0][...] += _dot(a_ref[...], b_ref[...])

        @pl.when(kk == nk - 1)
        def _():
            o_ref[...] = acc[0][...].astype(o_ref.dtype)

    return pl.pallas_call(
        body, grid=(c, nj, ni, nk),
        in_specs=[pl.BlockSpec((bm, bk), lambda cc, j, i, kk: (i, kk)),
                  pl.BlockSpec((None, bk, bn), lambda cc, j, i, kk: (cc, kk, j))],
        out_specs=pl.BlockSpec((bm, bn), lambda cc, j, i, kk: (i, cc * nj + j)),
        out_shape=S((m, c * n), out_dtype),
        scratch_shapes=[] if nk == 1 else [pltpu.VMEM((bm, bn), F32)],
        compiler_params=_cp("parallel", "parallel", "parallel", "arbitrary"), name=name)(a, b3)


def _mm_nt(a, b3, *, bm, bo, out_dtype, name):
    m = a.shape[0]
    c, ko, n = b3.shape
    ni, nj = m // bm, ko // bo

    def body(a_ref, b_ref, o_ref, *acc):
        if c == 1:
            o_ref[...] = _dot_nt(a_ref[...], b_ref[...]).astype(o_ref.dtype)
            return
        cc = pl.program_id(2)

        @pl.when(cc == 0)
        def _():
            acc[0][...] = jnp.zeros_like(acc[0])

        acc[0][...] += _dot_nt(a_ref[...], b_ref[...])

        @pl.when(cc == c - 1)
        def _():
            o_ref[...] = acc[0][...].astype(o_ref.dtype)

    return pl.pallas_call(
        body, grid=(nj, ni, c),
        in_specs=[pl.BlockSpec((bm, n), lambda j, i, cc: (i, cc)),
                  pl.BlockSpec((None, bo, n), lambda j, i, cc: (cc, j, 0))],
        out_specs=pl.BlockSpec((bm, bo), lambda j, i, cc: (i, j)),
        out_shape=S((m, ko), out_dtype),
        scratch_shapes=[] if c == 1 else [pltpu.VMEM((bm, bo), F32)],
        compiler_params=_cp("parallel", "parallel", "arbitrary"), name=name)(a, b3)


def _mm_tn(a, b, c, *, bm, bk, name):
    m, k = a.shape
    n = b.shape[1] // c
    nm, nk = m // bm, k // bk

    def body(a_ref, b_ref, o_ref, acc):
        mm = pl.program_id(2)

        @pl.when(mm == 0)
        def _():
            acc[...] = jnp.zeros_like(acc)

        acc[...] += _dot_tn(a_ref[...], b_ref[...])

        @pl.when(mm == nm - 1)
        def _():
            o_ref[...] = acc[...]

    return pl.pallas_call(
        body, grid=(c, nk, nm),
        in_specs=[pl.BlockSpec((bm, bk), lambda cc, j, mm: (mm, j)),
                  pl.BlockSpec((bm, n), lambda cc, j, mm: (mm, cc))],
        out_specs=pl.BlockSpec((None, bk, n), lambda cc, j, mm: (cc, j, 0)),
        out_shape=S((c, k, n), F32),
        scratch_shapes=[pltpu.VMEM((bk, n), F32)],
        compiler_params=_cp("parallel", "parallel", "arbitrary"), name=name)(a, b)


def _swiglu_fwd(hn, wg3, wu3, *, bm, name):
    m, k = hn.shape
    c, _, n = wg3.shape

    def body(a_ref, g_ref, u_ref, gate_ref, up_ref, act_ref):
        a = a_ref[...]
        gate = _dot(a, g_ref[...])
        up = _dot(a, u_ref[...])
        gate_ref[...] = gate
        up_ref[...] = up
        act_ref[...] = (gate * jax.nn.sigmoid(gate) * up).astype(act_ref.dtype)

    wspec = pl.BlockSpec((None, k, n), lambda cc, i: (cc, 0, 0))
    ospec = pl.BlockSpec((bm, n), lambda cc, i: (i, cc))
    return pl.pallas_call(
        body, grid=(c, m // bm),
        in_specs=[pl.BlockSpec((bm, k), lambda cc, i: (i, 0)), wspec, wspec],
        out_specs=[ospec, ospec, ospec],
        out_shape=[S((m, c * n), F32), S((m, c * n), F32), S((m, c * n), _MXU)],
        compiler_params=_cp("parallel", "parallel"), name=name)(hn, wg3, wu3)


def _swiglu_bwd(df, wd, gate, up, *, bm, bo, name):
    m, k = df.shape
    ko = wd.shape[0]

    def body(a_ref, b_ref, g_ref, u_ref, dg_ref, du_ref):
        dact = _dot_nt(a_ref[...], b_ref[...])
        gate = g_ref[...]
        sg = jax.nn.sigmoid(gate)
        dg_ref[...] = (dact * u_ref[...] * (sg * (1.0 + gate * (1.0 - sg)))).astype(dg_ref.dtype)
        du_ref[...] = (dact * (gate * sg)).astype(du_ref.dtype)

    ospec = pl.BlockSpec((bm, bo), lambda j, i: (i, j))
    return pl.pallas_call(
        body, grid=(ko // bo, m // bm),
        in_specs=[pl.BlockSpec((bm, k), lambda j, i: (i, 0)), pl.BlockSpec((bo, k), lambda j, i: (j, 0)), ospec, ospec],
        out_specs=[ospec, ospec],
        out_shape=[S((m, ko), _MXU), S((m, ko), _MXU)],
        compiler_params=_cp("parallel", "parallel"), name=name)(df, wd, gate, up)


def _rms_fwd(x, gain, name):
    t, d = x.shape
    tm = min(t, ROW_TILE)

    def body(x_ref, g_ref, o_ref):
        xv = x_ref[...]
        o_ref[...] = ((xv * _rstd(xv)) * g_ref[...]).astype(o_ref.dtype)

    return pl.pallas_call(body, grid=(t // tm,), in_specs=[_row_spec(tm, d), _vec_spec(d)], out_specs=_row_spec(tm, d),
                          out_shape=S((t, d), _MXU), compiler_params=_cp("parallel"), name=name)(x, gain)


def _outnorm_fwd(o, yl, ga, gl, name):
    t, w = o.shape
    tm = min(t, ROW_TILE)

    def body(o_ref, l_ref, ga_ref, gl_ref, y_ref):
        ov, lv = o_ref[...], l_ref[...]
        y_ref[:, :w] = ((ov * _rstd(ov)) * ga_ref[...]).astype(y_ref.dtype)
        y_ref[:, w:] = ((lv * _rstd(lv)) * gl_ref[...]).astype(y_ref.dtype)

    return pl.pallas_call(body, grid=(t // tm,), in_specs=[_row_spec(tm, w), _row_spec(tm, w), _vec_spec(w), _vec_spec(w)],
                          out_specs=_row_spec(tm, 2 * w), out_shape=S((t, 2 * w), _MXU),
                          compiler_params=_cp("parallel"), name=name)(o, yl, ga, gl)


def _mid_fwd(x, mix, g_post, g_pre, name):
    t, d = x.shape
    tm = min(t, ROW_TILE)

    def body(x_ref, m_ref, gp_ref, gn_ref, x2_ref, hn_ref):
        mv = m_ref[...]
        x2 = x_ref[...] + (mv * _rstd(mv)) * gp_ref[...]
        x2_ref[...] = x2
        hn_ref[...] = ((x2 * _rstd(x2)) * gn_ref[...]).astype(hn_ref.dtype)

    return pl.pallas_call(body, grid=(t // tm,), in_specs=[_row_spec(tm, d), _row_spec(tm, d), _vec_spec(d), _vec_spec(d)],
                          out_specs=[_row_spec(tm, d), _row_spec(tm, d)], out_shape=[S((t, d), F32), S((t, d), _MXU)],
                          compiler_params=_cp("parallel"), name=name)(x, mix, g_post, g_pre)


def _final(f, x2, target, g_post, name):
    t, d = f.shape
    tm = min(t, ROW_TILE)

    def body(f_ref, x2_ref, t_ref, g_ref, loss_ref, dout_ref, df_ref, dg_ref):
        @pl.when(pl.program_id(0) == 0)
        def _():
            loss_ref[...] = jnp.zeros_like(loss_ref)
            dg_ref[...] = jnp.zeros_like(dg_ref)

        fv = f_ref[...]
        r = _rstd(fv)
        fh = fv * r
        err = (x2_ref[...] + fh * g_ref[...]) - t_ref[...]
        loss_ref[...] += jnp.sum(err * err, axis=0, keepdims=True)
        dout = err * (1.0 / d)
        dout_ref[...] = dout
        dfv, dg = _rms_bwd(dout, fh, r, g_ref[...])
        df_ref[...] = dfv.astype(df_ref.dtype)
        dg_ref[...] += dg

    return pl.pallas_call(
        body, grid=(t // tm,),
        in_specs=[_row_spec(tm, d), _row_spec(tm, d), _row_spec(tm, d), _vec_spec(d)],
        out_specs=[_vec_spec(d), _row_spec(tm, d), _row_spec(tm, d), _vec_spec(d)],
        out_shape=[S((1, d), F32), S((t, d), F32), S((t, d), _MXU), S((1, d), F32)],
        compiler_params=_cp("arbitrary"), name=name)(f, x2, target, g_post)


def _mid_bwd(dhn_a, dhn_b, dout, x2, mix, g_pre, g_post, name):
    t, d = x2.shape
    tm = min(t, ROW_TILE)

    def body(da_ref, db_ref, do_ref, x2_ref, m_ref, gn_ref, gp_ref, dx2_ref, dm_ref, dgn_ref, dgp_ref):
        @pl.when(pl.program_id(0) == 0)
        def _():
            dgn_ref[...] = jnp.zeros_like(dgn_ref)
            dgp_ref[...] = jnp.zeros_like(dgp_ref)

        x2 = x2_ref[...]
        r = _rstd(x2)
        dxa, dgn = _rms_bwd(da_ref[...] + db_ref[...], x2 * r, r, gn_ref[...])
        dx2 = do_ref[...] + dxa
        dx2_ref[...] = dx2
        dgn_ref[...] += dgn
        mv = m_ref[...]
        rm = _rstd(mv)
        dmv, dgp = _rms_bwd(dx2, mv * rm, rm, gp_ref[...])
        dm_ref[...] = dmv.astype(dm_ref.dtype)
        dgp_ref[...] += dgp

    rs, vs = _row_spec(tm, d), _vec_spec(d)
    return pl.pallas_call(
        body, grid=(t // tm,), in_specs=[rs, rs, rs, rs, rs, vs, vs], out_specs=[rs, rs, vs, vs],
        out_shape=[S((t, d), F32), S((t, d), _MXU), S((1, d), F32), S((1, d), F32)],
        compiler_params=_cp("arbitrary"), name=name)(dhn_a, dhn_b, dout, x2, mix, g_pre, g_post)


def _first_bwd(dhn, dx2, x, gain, name):
    t, d = x.shape
    tm = min(t, ROW_TILE)

    def body(dh_ref, dx2_ref, x_ref, g_ref, dx_ref, dg_ref):
        @pl.when(pl.program_id(0) == 0)
        def _():
            dg_ref[...] = jnp.zeros_like(dg_ref)

        xv = x_ref[...]
        r = _rstd(xv)
        dxa, dg = _rms_bwd(dh_ref[...], xv * r, r, g_ref[...])
        dx_ref[...] = dx2_ref[...] + dxa
        dg_ref[...] += dg

    rs, vs = _row_spec(tm, d), _vec_spec(d)
    return pl.pallas_call(body, grid=(t // tm,), in_specs=[rs, rs, rs, vs], out_specs=[rs, vs],
                          out_shape=[S((t, d), F32), S((1, d), F32)], compiler_params=_cp("arbitrary"), name=name)(dhn, dx2, x, gain)


def _outnorm_bwd(dy, o, yl, ga, gl, name):
    t, w = o.shape
    tm = min(t, ROW_TILE)

    def body(dy_ref, o_ref, l_ref, ga_ref, gl_ref, do_ref, dl_ref, dga_ref, dgl_ref):
        @pl.when(pl.program_id(0) == 0)
        def _():
            dga_ref[...] = jnp.zeros_like(dga_ref)
            dgl_ref[...] = jnp.zeros_like(dgl_ref)

        ov, lv = o_ref[...], l_ref[...]
        ra, rl = _rstd(ov), _rstd(lv)
        dov, dga = _rms_bwd(dy_ref[:, :w], ov * ra, ra, ga_ref[...])
        dlv, dgl = _rms_bwd(dy_ref[:, w:], lv * rl, rl, gl_ref[...])
        do_ref[...] = dov
        dl_ref[...] = dlv
        dga_ref[...] += dga
        dgl_ref[...] += dgl

    rs, vs = _row_spec(tm, w), _vec_spec(w)
    return pl.pallas_call(body, grid=(t // tm,), in_specs=[_row_spec(tm, 2 * w), rs, rs, vs, vs], out_specs=[rs, rs, vs, vs],
                          out_shape=[S((t, w), F32), S((t, w), F32), S((1, w), F32), S((1, w), F32)],
                          compiler_params=_cp("arbitrary"), name=name)(dy, o, yl, ga, gl)


def _split_dot(v, tri):
    hi = v.astype(_MXU)
    lo = (v - hi.astype(F32)).astype(_MXU)
    return _dot(hi, tri) + _dot(lo, tri)


def _attn_tile(qb, kb, row, col, shift, scale):
    z = _dot_nt(qb, kb) * scale
    mask = (col + shift) < row
    lb = _log_sigmoid(z)
    lm = jnp.where(mask, lb - z, 0.0)
    return mask, lb, lm


def _attn_fwd(proj, n_heads, name):
    t = proj.shape[0]
    bq = min(t, ATTN_BLOCK)
    nq = t // bq
    scale = 1.0 / math.sqrt(HEAD_DIM)

    def body(q_ref, k_ref, v_ref, o_ref, kb_ref, vb_ref):
        kb_ref[...] = k_ref[...].astype(_MXU)
        vb_ref[...] = v_ref[...].astype(_MXU)
        row = lax.broadcasted_iota(jnp.int32, (bq, bq), 0)
        col = lax.broadcasted_iota(jnp.int32, (bq, bq), 1)
        tri = (row > col).astype(_MXU)

        def per_q(qi, _):
            q0 = pl.multiple_of(qi * bq, bq)
            qb = q_ref[pl.ds(q0, bq), :].astype(_MXU)

            def cond(st):
                return jnp.logical_and(st[0] >= 0, st[1])

            def step(st):
                kj, _, carry, acc = st
                k0 = pl.multiple_of(kj * bq, bq)
                mask, lb, lm = _attn_tile(qb, kb_ref[pl.ds(k0, bq), :], row, col, (kj - qi) * bq, scale)
                w = jnp.where(mask, jnp.exp(lb + _split_dot(lm, tri) + carry), 0.0)
                acc = acc + _dot(w.astype(_MXU), vb_ref[pl.ds(k0, bq), :])
                carry = carry + jnp.sum(lm, axis=1, keepdims=True)
                return kj - 1, jnp.max(carry) > EXP_CUT, carry, acc

            st = lax.while_loop(cond, step, (qi, jnp.bool_(True), jnp.zeros((bq, 1), F32), jnp.zeros((bq, HEAD_DIM), F32)))
            o_ref[pl.ds(q0, bq), :] = st[3]
            return 0

        lax.fori_loop(0, nq, per_q, 0)

    hs = lambda off: pl.BlockSpec((t, HEAD_DIM), lambda h: (0, off + h))
    return pl.pallas_call(
        body, grid=(n_heads,), in_specs=[hs(0), hs(n_heads), hs(2 * n_heads)], out_specs=hs(0),
        out_shape=S((t, n_heads * HEAD_DIM), F32),
        scratch_shapes=[pltpu.VMEM((t, HEAD_DIM), _MXU), pltpu.VMEM((t, HEAD_DIM), _MXU)],
        compiler_params=_cp("parallel"), name=name)(proj, proj, proj)


def _attn_bwd(proj, do, n_heads, name):
    t = proj.shape[0]
    bq = min(t, ATTN_BLOCK)
    nq = t // bq
    scale = 1.0 / math.sqrt(HEAD_DIM)

    def body(q_ref, k_ref, v_ref, do_ref, dq_ref, dk_ref, dv_ref, kb_ref, vb_ref, dka_ref, dva_ref, g_ref, b_ref):
        kb_ref[...] = k_ref[...].astype(_MXU)
        vb_ref[...] = v_ref[...].astype(_MXU)
        dka_ref[...] = jnp.zeros_like(dka_ref)
        dva_ref[...] = jnp.zeros_like(dva_ref)
        row = lax.broadcasted_iota(jnp.int32, (bq, bq), 0)
        col = lax.broadcasted_iota(jnp.int32, (bq, bq), 1)
        tri = (row > col).astype(_MXU)
        tri_lt = (row < col).astype(_MXU)

        def per_q(qi, _):
            q0 = pl.multiple_of(qi * bq, bq)
            qb = q_ref[pl.ds(q0, bq), :].astype(_MXU)
            dob = do_ref[pl.ds(q0, bq), :].astype(_MXU)

            def cond(st):
                return jnp.logical_and(st[0] >= 0, st[1])

            def step(st):
                kj, _, carry = st
                k0 = pl.multiple_of(kj * bq, bq)
                mask, lb, lm = _attn_tile(qb, kb_ref[pl.ds(k0, bq), :], row, col, (kj - qi) * bq, scale)
                w = jnp.where(mask, jnp.exp(lb + _split_dot(lm, tri) + carry), 0.0)
                g_ref[pl.ds(k0, bq), :] = w * _dot_nt(dob, vb_ref[pl.ds(k0, bq), :])
                b_ref[pl.ds(k0, bq), :] = jnp.where(mask, jnp.exp(lb), 0.0)
                dva_ref[pl.ds(k0, bq), :] += _dot_tn(w.astype(_MXU), dob)
                carry = carry + jnp.sum(lm, axis=1, keepdims=True)
                return kj - 1, jnp.max(carry) > EXP_CUT, carry

            st = lax.while_loop(cond, step, (qi, jnp.bool_(True), jnp.zeros((bq, 1), F32)))

            def back(kj, st2):
                before, dq = st2
                k0 = pl.multiple_of(kj * bq, bq)
                kb = kb_ref[pl.ds(k0, bq), :]
                g = g_ref[pl.ds(k0, bq), :]
                beta = b_ref[pl.ds(k0, bq), :]
                dz = ((g * (1.0 - beta) - (before + _split_dot(g, tri_lt)) * beta) * scale).astype(_MXU)
                dka_ref[pl.ds(k0, bq), :] += _dot_tn(dz, qb)
                return before + jnp.sum(g, axis=1, keepdims=True), dq + _dot(dz, kb)

            st2 = lax.fori_loop(st[0] + 1, qi + 1, back, (jnp.zeros((bq, 1), F32), jnp.zeros((bq, HEAD_DIM), F32)))
            dq_ref[pl.ds(q0, bq), :] = st2[1].astype(dq_ref.dtype)
            return 0

        lax.fori_loop(0, nq, per_q, 0)
        dk_ref[...] = dka_ref[...].astype(dk_ref.dtype)
        dv_ref[...] = dva_ref[...].astype(dv_ref.dtype)

    hs = lambda off: pl.BlockSpec((t, HEAD_DIM), lambda h: (0, off + h))
    w = n_heads * HEAD_DIM
    return pl.pallas_call(
        body, grid=(n_heads,), in_specs=[hs(0), hs(n_heads), hs(2 * n_heads), hs(0)], out_specs=[hs(0), hs(0), hs(0)],
        out_shape=[S((t, w), _MXU)] * 3,
        scratch_shapes=[pltpu.VMEM((t, HEAD_DIM), _MXU), pltpu.VMEM((t, HEAD_DIM), _MXU), pltpu.VMEM((t, HEAD_DIM), F32),
                        pltpu.VMEM((t, HEAD_DIM), F32), pltpu.VMEM((t, bq), F32), pltpu.VMEM((t, bq), F32)],
        compiler_params=_cp("parallel"), name=name)(proj, proj, proj, do)


def _shift_down(cur, prev8, k):
    if k == 0:
        return cur
    row8 = lax.broadcasted_iota(jnp.int32, prev8.shape, 0)
    rc = pltpu.roll(cur, k, 0)
    top = jnp.where(row8 < k, pltpu.roll(prev8, k, 0), rc[0:8, :])
    return jnp.concatenate([top, rc[8:, :]], axis=0)


def _shift_up(cur, next8, k):
    if k == 0:
        return cur
    n = cur.shape[0]
    row8 = lax.broadcasted_iota(jnp.int32, next8.shape, 0)
    rc = pltpu.roll(cur, n - k, 0)
    bottom = jnp.where(row8 >= 8 - k, pltpu.roll(next8, 8 - k, 0), rc[n - 8:, :])
    return jnp.concatenate([rc[:n - 8, :], bottom], axis=0)


def _lru_gates(xl, prev8, cw, cb, wr, br, wi, bi, ls):
    xs = [_shift_down(xl, prev8, CONV_WIDTH - 1 - k) for k in range(CONV_WIDTH)]
    xc = xs[0] * cw[0:1, :]
    for k in range(1, CONV_WIDTH):
        xc = xc + xs[k] * cw[k:k + 1, :]
    xc = xc + cb
    xcb = xc.astype(_MXU)
    r = jax.nn.sigmoid(_dot(xcb, wr) + br)
    i = jax.nn.sigmoid(_dot(xcb, wi) + bi)
    la = (LRU_C * r) * ls
    a = jnp.exp(la)
    mult = jnp.sqrt(-_expm1(2.0 * la))
    return xs, xc, r, i, a, mult


def _group_scan(a, b, reverse):
    n = a.shape[0]
    row = lax.broadcasted_iota(jnp.int32, a.shape, 0) % 8
    for d in (1, 2, 4):
        if reverse:
            m = row < 8 - d
            a_s, b_s = pltpu.roll(a, n - d, 0), pltpu.roll(b, n - d, 0)
        else:
            m = row >= d
            a_s, b_s = pltpu.roll(a, d, 0), pltpu.roll(b, d, 0)
        b = jnp.where(m, a * b_s + b, b)
        a = jnp.where(m, a * a_s, a)
    return a, b


def _lru_fwd(proj, col0, n_blocks, cw, cb, wr, br, wi, bi, lam, name):
    t = proj.shape[0]
    tt = min(t, SEQ_TILE)
    nt = t // tt

    def body(xl_ref, gl_ref, cw_ref, cb_ref, wr_ref, br_ref, wi_ref, bi_ref, lam_ref, h_ref, y_ref):
        cwv, cbv, brv, biv = cw_ref[...], cb_ref[...], br_ref[...], bi_ref[...]
        wrv, wiv = wr_ref[...].astype(_MXU), wi_ref[...].astype(_MXU)
        ls = _log_sigmoid(lam_ref[...])

        def tile(ti, hin):
            t0 = pl.multiple_of(ti * tt, tt)
            p0 = pl.multiple_of(jnp.maximum(t0 - 8, 0), 8)
            prev8 = xl_ref[pl.ds(p0, 8), :] * (ti > 0).astype(F32)
            xl = xl_ref[pl.ds(t0, tt), :]
            _, xc, _, ig, a, mult = _lru_gates(xl, prev8, cwv, cbv, wrv, brv, wiv, biv, ls)
            ga, gb = _group_scan(a, mult * (ig * xc), False)
            for g in range(tt // 8):
                hg = ga[8 * g:8 * g + 8, :] * hin + gb[8 * g:8 * g + 8, :]
                h_ref[pl.ds(t0 + 8 * g, 8), :] = hg
                hin = hg[7:8, :]
            y_ref[pl.ds(t0, tt), :] = h_ref[pl.ds(t0, tt), :] * _gelu(gl_ref[pl.ds(t0, tt), :])
            return hin

        lax.fori_loop(0, nt, tile, jnp.zeros((1, HEAD_DIM), F32))

    cs = lambda off: pl.BlockSpec((t, HEAD_DIM), lambda n: (0, off + n))
    vs = pl.BlockSpec((1, HEAD_DIM), lambda n: (0, n))
    ws = pl.BlockSpec((None, HEAD_DIM, HEAD_DIM), lambda n: (n, 0, 0))
    w = n_blocks * HEAD_DIM
    return pl.pallas_call(
        body, grid=(n_blocks,),
        in_specs=[cs(col0), cs(col0 + n_blocks), pl.BlockSpec((CONV_WIDTH, HEAD_DIM), lambda n: (0, n)), vs, ws, vs, ws, vs, vs],
        out_specs=[cs(0), cs(0)], out_shape=[S((t, w), F32), S((t, w), F32)],
        compiler_params=_cp("parallel"), name=name)(proj, proj, cw, cb, wr, br, wi, bi, lam)


def _lru_bwd(proj, col0, n_blocks, h, dyl, cw, cb, wr, br, wi, bi, lam, name):
    t = proj.shape[0]
    tt = min(t, SEQ_TILE)
    nt = t // tt

    def body(xl_ref, gl_ref, h_ref, dy_ref, cw_ref, cb_ref, wr_ref, br_ref, wi_ref, bi_ref, lam_ref,
             dxl_ref, dgl_ref, dcw_ref, dcb_ref, dwr_ref, dbr_ref, dwi_ref, dbi_ref, dlam_ref, g_ref):
        cwv, cbv, brv, biv = cw_ref[...], cb_ref[...], br_ref[...], bi_ref[...]
        wrv, wiv = wr_ref[...].astype(_MXU), wi_ref[...].astype(_MXU)
        lamv = lam_ref[...]
        ls = _log_sigmoid(lamv)
        for ref in (dcw_ref, dcb_ref, dwr_ref, dbr_ref, dwi_ref, dbi_ref, dlam_ref):
            ref[...] = jnp.zeros_like(ref)

        def tile(s, carry):
            e_in, dxc_next8 = carry
            ti = nt - 1 - s
            t0 = pl.multiple_of(ti * tt, tt)
            p0 = pl.multiple_of(jnp.maximum(t0 - 8, 0), 8)
            first = (ti > 0).astype(F32)
            xl = xl_ref[pl.ds(t0, tt), :]
            xs, xc, r, ig, a, mult = _lru_gates(xl, xl_ref[pl.ds(p0, 8), :] * first, cwv, cbv, wrv, brv, wiv, biv, ls)
            hv = h_ref[pl.ds(t0, tt), :]
            h_before = _shift_down(hv, h_ref[pl.ds(p0, 8), :] * first, 1)
            glv = gl_ref[pl.ds(t0, tt), :]
            dyv = dy_ref[pl.ds(t0, tt), :]
            dgl_ref[pl.ds(t0, tt), :] = (dyv * hv * _gelu_grad(glv)).astype(dgl_ref.dtype)
            dh = dyv * _gelu(glv)
            row = lax.broadcasted_iota(jnp.int32, a.shape, 0)
            coef = jnp.where(row == tt - 1, 1.0, pltpu.roll(a, tt - 1, 0))
            ga, gb = _group_scan(coef, dh, True)
            gin = e_in
            for g in reversed(range(tt // 8)):
                gg = ga[8 * g:8 * g + 8, :] * gin + gb[8 * g:8 * g + 8, :]
                g_ref[8 * g:8 * g + 8, :] = gg
                gin = gg[0:1, :]
            gv = g_ref[...]
            e_out = a[0:1, :] * gv[0:1, :]
            ix = ig * xc
            dla = (gv * h_before) * a - (gv * ix) * (a * a / mult)
            dlam_ref[...] += jnp.sum(dla * (LRU_C * r), axis=0, keepdims=True)
            dpr = (dla * (LRU_C * ls)) * (r * (1.0 - r))
            dpi = (gv * mult * xc) * (ig * (1.0 - ig))
            dbr_ref[...] += jnp.sum(dpr, axis=0, keepdims=True)
            dbi_ref[...] += jnp.sum(dpi, axis=0, keepdims=True)
            xcb, dprb, dpib = xc.astype(_MXU), dpr.astype(_MXU), dpi.astype(_MXU)
            dwr_ref[...] += _dot_tn(xcb, dprb)
            dwi_ref[...] += _dot_tn(xcb, dpib)
            dxc = gv * mult * ig + _dot_nt(dprb, wrv) + _dot_nt(dpib, wiv)
            dcb_ref[...] += jnp.sum(dxc, axis=0, keepdims=True)
            dxl = None
            for k in range(CONV_WIDTH):
                dcw_ref[k:k + 1, :] += jnp.sum(dxc * xs[k], axis=0, keepdims=True)
                term = _shift_up(dxc, dxc_next8, CONV_WIDTH - 1 - k) * cwv[k:k + 1, :]
                dxl = term if dxl is None else dxl + term
            dxl_ref[pl.ds(t0, tt), :] = dxl.astype(dxl_ref.dtype)
            return e_out, dxc[0:8, :]

        lax.fori_loop(0, nt, tile, (jnp.zeros((1, HEAD_DIM), F32), jnp.zeros((8, HEAD_DIM), F32)))
        dlam_ref[...] = dlam_ref[...] * (1.0 - jax.nn.sigmoid(lamv))

    cs = lambda off: pl.BlockSpec((t, HEAD_DIM), lambda n: (0, off + n))
    vs = pl.BlockSpec((1, HEAD_DIM), lambda n: (0, n))
    ws = pl.BlockSpec((None, HEAD_DIM, HEAD_DIM), lambda n: (n, 0, 0))
    cws = pl.BlockSpec((CONV_WIDTH, HEAD_DIM), lambda n: (0, n))
    w = n_blocks * HEAD_DIM
    vec = S((1, w), F32)
    mat = S((n_blocks, HEAD_DIM, HEAD_DIM), F32)
    return pl.pallas_call(
        body, grid=(n_blocks,),
        in_specs=[cs(col0), cs(col0 + n_blocks), cs(0), cs(0), cws, vs, ws, vs, ws, vs, vs],
        out_specs=[cs(0), cs(0), cws, vs, ws, vs, ws, vs, vs],
        out_shape=[S((t, w), _MXU), S((t, w), _MXU), S((CONV_WIDTH, w), F32), vec, mat, vec, mat, vec, vec],
        scratch_shapes=[pltpu.VMEM((tt, HEAD_DIM), F32)],
        compiler_params=_cp("parallel"), name=name)(proj, proj, h, dyl, cw, cb, wr, br, wi, bi, lam)


def _local_step(x, target, norms, win3, wout, wg3, wu3, wd, cw, cb, wr, br, wi, bi, lam, ga, gl):
    g_pre_mix, g_post_mix, g_pre_ffn, g_post_ffn = norms
    t, d = x.shape
    c = win3.shape[0]
    mix = wout.shape[0]
    aw = mix // 2
    n_heads = aw // HEAD_DIM
    n_blocks = (mix - aw) // HEAD_DIM
    ff = wd.shape[0]
    bm = min(t, 512)

    hn1 = _rms_fwd(x, g_pre_mix, "rms1")
    proj = _mm_nn(hn1, win3, bm=bm, bn=win3.shape[2], bk=d, out_dtype=F32, name="in_proj")
    o = _attn_fwd(proj, n_heads, "attn_fwd")
    h, yl = _lru_fwd(proj, 3 * n_heads, n_blocks, cw, cb, wr, br, wi, bi, lam, "lru_fwd")
    y = _outnorm_fwd(o, yl, ga, gl, "outnorm_fwd")
    mixo = _mm_nn(y, wout[None], bm=bm, bn=d, bk=mix, out_dtype=F32, name="out_proj")
    x2, hn2 = _mid_fwd(x, mixo, g_post_mix, g_pre_ffn, "mid_fwd")
    gate, up, act = _swiglu_fwd(hn2, wg3, wu3, bm=min(t, 256), name="ffn_gate_up")
    f = _mm_nn(act, wd[None], bm=bm, bn=d, bk=ff // 4, out_dtype=F32, name="ffn_down")
    loss_cols, dout, df, dg_post_ffn = _final(f, x2, target, g_post_ffn, "final")

    dgate, dup = _swiglu_bwd(df, wd, gate, up, bm=bm, bo=ff // 4, name="ffn_down_bwd")
    d_wd = _mm_tn(act, df, 1, bm=bm, bk=512, name="ffn_down_dw")
    d_wg3 = _mm_tn(hn2, dgate, c, bm=bm, bk=d // 2, name="ffn_gate_dw")
    d_wu3 = _mm_tn(hn2, dup, c, bm=bm, bk=d // 2, name="ffn_up_dw")
    dhn2_g = _mm_nt(dgate, wg3, bm=bm, bo=d, out_dtype=F32, name="ffn_gate_dx")
    dhn2_u = _mm_nt(dup, wu3, bm=bm, bo=d, out_dtype=F32, name="ffn_up_dx")
    dx2, dmix, dg_pre_ffn, dg_post_mix = _mid_bwd(dhn2_g, dhn2_u, dout, x2, mixo, g_pre_ffn, g_post_mix, "mid_bwd")
    dy = _mm_nt(dmix, wout[None], bm=bm, bo=mix, out_dtype=F32, name="out_proj_dx")
    d_wout = _mm_tn(y, dmix, 1, bm=bm, bk=mix // 2, name="out_proj_dw")
    do, dyl, dga, dgl_norm = _outnorm_bwd(dy, o, yl, ga, gl, "outnorm_bwd")
    dxl, dglu, dcw, dcb, dwr, dbr, dwi, dbi, dlam = _lru_bwd(proj, 3 * n_heads, n_blocks, h, dyl, cw, cb, wr, br, wi, bi, lam, "lru_bwd")
    dq, dk, dv = _attn_bwd(proj, do, n_heads, "attn_bwd")
    dproj = jnp.concatenate([dq, dk, dv, dxl, dglu], axis=1)
    dhn1 = _mm_nt(dproj, win3, bm=bm, bo=d, out_dtype=F32, name="in_proj_dx")
    d_win3 = _mm_tn(hn1, dproj, c, bm=bm, bk=d // 2, name="in_proj_dw")
    grad_x, dg_pre_mix = _first_bwd(dhn1, dx2, x, g_pre_mix, "first_bwd")

    big = (d_win3, d_wout.reshape(c, mix // c, d), d_wg3, d_wu3, d_wd.reshape(c, ff // c, d))
    small = dict(pre_mix_norm=dg_pre_mix, post_mix_norm=dg_post_mix, pre_ffn_norm=dg_pre_ffn, post_ffn_norm=dg_post_ffn,
                 conv_w=dcw, conv_b=dcb, w_rgate=dwr, b_rgate=dbr, w_igate=dwi, b_igate=dbi, lru_lambda=dlam,
                 attn_out_norm=dga, lru_out_norm=dgl_norm)
    return loss_cols, grad_x, big, small


_ANY = pl.BlockSpec(memory_space=pl.ANY)


def _place():
    x, y, c = lax.axis_index("x"), lax.axis_index("y"), lax.axis_index("c")
    return x, y, c, [(1 - x, y), (x, 1 - y), (1 - x, 1 - y)]


def _remote(src, dst, send_sem, recv_sem, to):
    return pltpu.make_async_remote_copy(src_ref=src, dst_ref=dst, send_sem=send_sem, recv_sem=recv_sem,
                                        device_id=to, device_id_type=MESH)


def _into_slot(wsh, slot, dtype, name):
    rows, n = wsh.shape
    rb = _row_block(rows, 256) if rows % 8 == 0 else rows

    def body(s_ref, w_ref, o_ref):
        o_ref[...] = w_ref[...].astype(o_ref.dtype)

    return pl.pallas_call(
        body,
        grid_spec=pltpu.PrefetchScalarGridSpec(
            num_scalar_prefetch=1, grid=(rows // rb,),
            in_specs=[pl.BlockSpec((rb, n), lambda i, s_ref: (i, 0))],
            out_specs=pl.BlockSpec((None, rb, n), lambda i, s_ref: (s_ref[0], i, 0))),
        out_shape=S((4, rows, n), dtype), compiler_params=_cp("parallel"), name=name)(slot, wsh)


def _all_gather(bufs, split, name):
    n = len(bufs)

    def body(*refs):
        outs = refs[n:2 * n]
        ici_send, ici_recv, d2d_send, d2d_recv = refs[2 * n:]
        x, y, c, chips = _place()
        me = 2 * x + y

        def part(ref, a, cc):
            if not split[a]:
                return ref
            half = bufs[a].shape[1] // 2
            return ref.at[pl.ds(cc * half, half)]

        started = []
        for a in range(n):
            for j, (px, py) in enumerate(chips):
                mine = part(outs[a].at[me], a, c)
                cp = _remote(mine, mine, ici_send.at[3 * a + j], ici_recv.at[3 * a + j], (px, py, c))
                cp.start()
                started.append(cp)
        for a in range(n):
            for j, (px, py) in enumerate(chips):
                land = part(outs[a].at[2 * px + py], a, c)
                _remote(land, land, ici_send.at[3 * a + j], ici_recv.at[3 * a + j], (px, py, c)).wait_recv()
                if split[a]:
                    fw = _remote(land, land, d2d_send.at[3 * a + j], d2d_recv.at[3 * a + j], (x, y, 1 - c))
                    fw.start()
                    started.append(fw)
        for a in range(n):
            if split[a]:
                for j, (px, py) in enumerate(chips):
                    land = part(outs[a].at[2 * px + py], a, 1 - c)
                    _remote(land, land, d2d_send.at[3 * a + j], d2d_recv.at[3 * a + j], (x, y, 1 - c)).wait_recv()
        for cp in started:
            cp.wait_send()

    sem = pltpu.SemaphoreType.DMA((3 * n,))
    return pl.pallas_call(
        body, in_specs=[_ANY] * n, out_specs=[_ANY] * n, out_shape=[S(b.shape, b.dtype) for b in bufs],
        input_output_aliases={i: i for i in range(n)}, scratch_shapes=[sem, sem, sem, sem], name=name)(*bufs)


def _half_swap(grads, name):
    n = len(grads)

    def body(*refs):
        ins, outs = refs[:n], refs[n:2 * n]
        send, recv = refs[2 * n:]
        x, y, c, _ = _place()
        started = []
        for a in range(n):
            half = grads[a].shape[1] // 2
            cp = _remote(ins[a].at[:, pl.ds((1 - c) * half, half)], outs[a], send.at[a], recv.at[a], (x, y, 1 - c))
            cp.start()
            started.append(cp)
        for cp in started:
            cp.wait()

    sem = pltpu.SemaphoreType.DMA((n,))
    return pl.pallas_call(
        body, in_specs=[_ANY] * n, out_specs=[_ANY] * n,
        out_shape=[S((4, g.shape[1] // 2, g.shape[2]), g.dtype) for g in grads],
        scratch_shapes=[sem, sem], name=name)(*grads)


def _chip_scatter(parts, name):
    n = len(parts)

    def body(*refs):
        ins, outs = refs[:n], refs[n:2 * n]
        send, recv = refs[2 * n:]
        x, y, c, chips = _place()
        me = 2 * x + y
        started = []
        for a in range(n):
            for j, (px, py) in enumerate(chips):
                cp = _remote(ins[a].at[2 * px + py], outs[a].at[me], send.at[3 * a + j], recv.at[3 * a + j], (px, py, c))
                cp.start()
                started.append(cp)
        for a in range(n):
            for j, (px, py) in enumerate(chips):
                land = outs[a].at[2 * px + py]
                _remote(land, land, send.at[3 * a + j], recv.at[3 * a + j], (px, py, c)).wait_recv()
        for cp in started:
            cp.wait_send()

    sem = pltpu.SemaphoreType.DMA((3 * n,))
    return pl.pallas_call(
        body, in_specs=[_ANY] * n, out_specs=[_ANY] * n, out_shape=[S(p.shape, p.dtype) for p in parts],
        scratch_shapes=[sem, sem], name=name)(*parts)


def _half_share(fulls, name):
    n = len(fulls)

    def body(*refs):
        outs = refs[n:2 * n]
        send, recv = refs[2 * n:]
        x, y, c, _ = _place()
        started = []
        for a in range(n):
            r = fulls[a].shape[0] // 2
            mine = outs[a].at[pl.ds(c * r, r)]
            cp = _remote(mine, mine, send.at[a], recv.at[a], (x, y, 1 - c))
            cp.start()
            started.append(cp)
        for a in range(n):
            r = fulls[a].shape[0] // 2
            land = outs[a].at[pl.ds((1 - c) * r, r)]
            _remote(land, land, send.at[a], recv.at[a], (x, y, 1 - c)).wait_recv()
        for cp in started:
            cp.wait_send()

    sem = pltpu.SemaphoreType.DMA((n,))
    return pl.pallas_call(
        body, in_specs=[_ANY] * n, out_specs=[_ANY] * n, out_shape=[S(f.shape, f.dtype) for f in fulls],
        input_output_aliases={i: i for i in range(n)}, scratch_shapes=[sem, sem], name=name)(*fulls)


def _all_reduce_small(v, name):
    r = v.shape[0]

    def body(v_ref, o_ref, buf, send, recv):
        x, y, c, _ = _place()
        me = 4 * x + 2 * y + c
        buf[me] = v_ref[...]
        started = []
        for d in range(1, 8):
            to = (me + d) % 8
            cp = _remote(v_ref, buf.at[me], send.at[d - 1], recv.at[d - 1], (to // 4, (to // 2) % 2, to % 2))
            cp.start()
            started.append(cp)
        for d in range(1, 8):
            land = buf.at[(me + 8 - d) % 8]
            _remote(land, land, send.at[d - 1], recv.at[d - 1], (x, y, c)).wait_recv()
        for cp in started:
            cp.wait_send()
        acc = buf[0]
        for k in range(1, 8):
            acc = acc + buf[k]
        o_ref[...] = acc

    vm = pl.BlockSpec(memory_space=pltpu.VMEM)
    sem = pltpu.SemaphoreType.DMA((7,))
    return pl.pallas_call(
        body, in_specs=[vm], out_specs=vm, out_shape=S(v.shape, F32),
        scratch_shapes=[pltpu.VMEM((8, r, 128), F32), sem, sem],
        compiler_params=pltpu.CompilerParams(vmem_limit_bytes=VMEM_LIMIT), name=name)(v)


def _row_block(rows, cap):
    return max(b for b in range(8, cap + 1, 8) if rows % b == 0)


def _add_own_half(g, recv, core, name):
    _, rows, n = g.shape
    half = rows // 2
    rb = _row_block(half, 512)
    nb = half // rb

    def body(c_ref, g_ref, r_ref, o_ref):
        o_ref[...] = (g_ref[...] + r_ref[...]).astype(o_ref.dtype)

    return pl.pallas_call(
        body,
        grid_spec=pltpu.PrefetchScalarGridSpec(
            num_scalar_prefetch=1, grid=(4, nb),
            in_specs=[pl.BlockSpec((None, rb, n), lambda k, i, c_ref: (k, c_ref[0] * nb + i, 0)),
                      pl.BlockSpec((None, rb, n), lambda k, i, c_ref: (k, i, 0))],
            out_specs=pl.BlockSpec((None, rb, n), lambda k, i, c_ref: (k, i, 0))),
        out_shape=S((4, half, n), BF16), compiler_params=_cp("parallel", "parallel"), name=name)(core, g, recv)


def _sum_chips(part, recv, place, name):
    _, rows, n = part.shape
    rb = _row_block(rows, 64)
    nb = rows // rb

    def body(p_ref, own_ref, r0, r1, r2, r3, o_ref):
        own = own_ref[...].astype(F32)
        terms = [jnp.where(p_ref[0] == k, own, r[...].astype(F32)) for k, r in enumerate((r0, r1, r2, r3))]
        o_ref[...] = ((terms[0] + terms[1]) + terms[2]) + terms[3]

    def slot(k):
        return pl.BlockSpec((None, rb, n), lambda i, p_ref: (jnp.where(p_ref[0] == k, (k + 1) % 4, k), i, 0))

    return pl.pallas_call(
        body,
        grid_spec=pltpu.PrefetchScalarGridSpec(
            num_scalar_prefetch=1, grid=(nb,),
            in_specs=[pl.BlockSpec((None, rb, n), lambda i, p_ref: (p_ref[0], i, 0))] + [slot(k) for k in range(4)],
            out_specs=pl.BlockSpec((rb, n), lambda i, p_ref: (p_ref[1] * nb + i, 0))),
        out_shape=S((2 * rows, n), F32), compiler_params=_cp("parallel"), name=name)(place, part, recv, recv, recv, recv)


def _adamw(w, g, m, v, name):
    rows, n = w.shape
    rb = rows if rows * n * 4 <= (1 << 21) else _row_block(rows, 128)
    c1 = 1.0 - ADAM_B1 ** ADAM_STEP
    c2 = 1.0 - ADAM_B2 ** ADAM_STEP

    def body(w_ref, g_ref, m_ref, v_ref, d_ref, nm_ref, nv_ref):
        gv = g_ref[...]
        nm = ADAM_B1 * m_ref[...] + (1.0 - ADAM_B1) * gv
        nv = ADAM_B2 * v_ref[...] + (1.0 - ADAM_B2) * (gv * gv)
        nm_ref[...] = nm
        nv_ref[...] = nv
        d_ref[...] = -ADAM_LR * ((nm / c1) / (jnp.sqrt(nv / c2) + ADAM_EPS) + ADAM_WD * w_ref[...])

    bs = pl.BlockSpec((rb, n), lambda i: (i, 0))
    return pl.pallas_call(body, grid=(rows // rb,), in_specs=[bs] * 4, out_specs=[bs] * 3, out_shape=[S((rows, n), F32)] * 3,
                          compiler_params=_cp("parallel"), name=name)(w, g, m, v)


_BIG = ("w_in", "w_out", "w_ffn_gate", "w_ffn_up", "w_ffn_down")
_SMALL = ("pre_mix_norm", "post_mix_norm", "pre_ffn_norm", "post_ffn_norm", "conv_w", "conv_b", "w_rgate", "b_rgate",
          "w_igate", "b_igate", "lru_lambda", "attn_out_norm", "lru_out_norm")
_WEIGHTS = ("pre_mix_norm", "post_mix_norm", "pre_ffn_norm", "post_ffn_norm", "w_in", "conv_w", "conv_b", "w_rgate", "b_rgate",
            "w_igate", "b_igate", "lru_lambda", "attn_out_norm", "lru_out_norm", "w_out", "w_ffn_gate", "w_ffn_up", "w_ffn_down")


def _pack(arrays):
    flat = []
    for a in arrays:
        f = a.reshape(-1)
        flat.append(jnp.pad(f, (0, (-f.shape[0]) % 1024)))
    return jnp.concatenate(flat).reshape(-1, 128)


def _unpack(packed, shapes):
    out, pos = [], 0
    flat = packed.reshape(-1)
    for s in shapes:
        size = math.prod(s)
        out.append(flat[pos:pos + size].reshape(s))
        pos += size + (-size) % 1024
    return out


def kernel(x, pre_mix_norm, post_mix_norm, pre_ffn_norm, post_ffn_norm, w_in, conv_w, conv_b, w_rgate, b_rgate, w_igate, b_igate, lru_lambda, attn_out_norm, lru_out_norm, w_out, w_ffn_gate, w_ffn_up, w_ffn_down, loss_target, m_pre_mix_norm, m_post_mix_norm, m_pre_ffn_norm, m_post_ffn_norm, m_w_in, m_conv_w, m_conv_b, m_w_rgate, m_b_rgate, m_w_igate, m_b_igate, m_lru_lambda, m_attn_out_norm, m_lru_out_norm, m_w_out, m_w_ffn_gate, m_w_ffn_up, m_w_ffn_down, v_pre_mix_norm, v_post_mix_norm, v_pre_ffn_norm, v_post_ffn_norm, v_w_in, v_conv_w, v_conv_b, v_w_rgate, v_b_rgate, v_w_igate, v_b_igate, v_lru_lambda, v_attn_out_norm, v_lru_out_norm, v_w_out, v_w_ffn_gate, v_w_ffn_up, v_w_ffn_down):
    given = dict(locals())
    w = {n: given[n][0] for n in _WEIGHTS}
    m = {n: given["m_" + n][0] for n in _WEIGHTS}
    v = {n: given["v_" + n][0] for n in _WEIGHTS}
    xs, target = x[0], loss_target[0]
    d = xs.shape[1]
    chip = (2 * lax.axis_index("x") + lax.axis_index("y")).astype(jnp.int32)
    core = lax.axis_index("c").astype(jnp.int32).reshape(1)
    place = jnp.stack([chip, core[0]])

    slots = [_into_slot(w[n], chip.reshape(1), _MXU, "slot_" + n) for n in _BIG] + [_into_slot(w["conv_w"], chip.reshape(1), F32, "slot_conv_w")]
    win3, wout4, wg3, wu3, wd4, cw4 = _all_gather(slots, [True] * 5 + [False], "gather_weights")
    wout = wout4.reshape(-1, d)
    wd = wd4.reshape(-1, d)
    conv_full = jnp.transpose(cw4, (1, 0, 2)).reshape(CONV_WIDTH, -1)
    row = lambda a: a.reshape(1, -1)
    norms = tuple(row(w[n]) for n in ("pre_mix_norm", "post_mix_norm", "pre_ffn_norm", "post_ffn_norm"))

    loss_cols, grad_x, big, small = _local_step(
        xs, target, norms, win3, wout, wg3, wu3, wd, conv_full, row(w["conv_b"]), w["w_rgate"], row(w["b_rgate"]),
        w["w_igate"], row(w["b_igate"]), row(w["lru_lambda"]), row(w["attn_out_norm"]), row(w["lru_out_norm"]))

    loss = lax.psum(0.5 * jnp.sum(loss_cols) / d, ("x", "y", "c"))

    from_sibling = _half_swap(list(big), "grads_half_swap")
    chip_part = [_add_own_half(g, r, core, "grads_add_%d" % i) for i, (g, r) in enumerate(zip(big, from_sibling))]
    from_chips = _chip_scatter(chip_part, "grads_chip_scatter")
    halves = [_sum_chips(p, r4, place, "grads_sum_%d" % i) for i, (p, r4) in enumerate(zip(chip_part, from_chips))]
    reduced = dict(zip(_BIG, _half_share(halves, "grads_half_share")))

    small_sum = _unpack(_all_reduce_small(_pack([small[n] for n in _SMALL]), "grads_small_all_reduce"), [small[n].shape for n in _SMALL])
    for n, g in zip(_SMALL, small_sum):
        reduced[n] = g.reshape(w[n].shape) if n != "conv_w" else lax.dynamic_slice_in_dim(g, chip * w[n].shape[1], w[n].shape[1], axis=1)

    delta, new_m, new_v = {}, {}, {}
    for n in _BIG:
        delta[n], new_m[n], new_v[n] = _adamw(w[n], reduced[n], m[n], v[n], "adamw_" + n)
    shapes = [w[n].shape for n in _SMALL]
    packed = _adamw(*[_pack([src[n] for n in _SMALL]) for src in (w, reduced, m, v)], "adamw_small")
    for out, p in zip((delta, new_m, new_v), packed):
        out.update(zip(_SMALL, _unpack(p, shapes)))

    lead = lambda a: a[None]
    return (loss, lead(grad_x), *[lead(reduced[n]) for n in _WEIGHTS], *[lead(delta[n]) for n in _WEIGHTS],
            *[lead(new_m[n]) for n in _WEIGHTS], *[lead(new_v[n]) for n in _WEIGHTS])
```

```python
import functools
import math

import jax
import jax.numpy as jnp
from jax import lax
from jax.experimental import pallas as pl
from jax.experimental.pallas import tpu as pltpu

F32 = jnp.float32
BF16 = jnp.bfloat16
_MXU = BF16
S = jax.ShapeDtypeStruct

RMS_EPS = 1e-6
HEAD_DIM = 128
CONV_WIDTH = 4
LRU_C = 8.0
ADAM_LR, ADAM_B1, ADAM_B2, ADAM_EPS, ADAM_WD, ADAM_STEP = 0.001, 0.9, 0.999, 1e-08, 0.01, 10
EXP_CUT = -105.0
VMEM_LIMIT = 56 * 1024 * 1024
ROW_TILE = 256
SEQ_TILE = 256
ATTN_BLOCK = 256
MESH = pl.DeviceIdType.MESH


def _cp(*sem):
    return pltpu.CompilerParams(dimension_semantics=sem, vmem_limit_bytes=VMEM_LIMIT)


def _dot(a, b):
    return jnp.dot(a, b, preferred_element_type=F32)


def _dot_nt(a, b):
    return lax.dot_general(a, b, (((1,), (1,)), ((), ())), preferred_element_type=F32)


def _dot_tn(a, b):
    return lax.dot_general(a, b, (((0,), (0,)), ((), ())), preferred_element_type=F32)


def _rstd(v):
    return lax.rsqrt(jnp.mean(v * v, axis=-1, keepdims=True) + RMS_EPS)


def _rms_bwd(dn, vh, r, gain):
    dvh = dn * gain
    dv = r * (dvh - vh * jnp.mean(dvh * vh, axis=-1, keepdims=True))
    return dv, jnp.sum(dn * vh, axis=0, keepdims=True)


def _log_sigmoid(z):
    return jnp.minimum(z, 0.0) - jnp.log(1.0 + jnp.exp(-jnp.abs(z)))


def _expm1(v):
    small = v * (1.0 + v * (0.5 + v * (1.0 / 6.0 + v * (1.0 / 24.0 + v * (1.0 / 120.0)))))
    return jnp.where(jnp.abs(v) < 0.04, small, jnp.exp(v) - 1.0)


_GELU_C = math.sqrt(2.0 / math.pi)


def _gelu(v):
    return 0.5 * v * (1.0 + jnp.tanh(_GELU_C * (v + 0.044715 * v * v * v)))


def _gelu_grad(v):
    th = jnp.tanh(_GELU_C * (v + 0.044715 * v * v * v))
    return 0.5 * (1.0 + th) + 0.5 * v * (1.0 - th * th) * _GELU_C * (1.0 + 3.0 * 0.044715 * v * v)


def _row_spec(tm, d):
    return pl.BlockSpec((tm, d), lambda i: (i, 0))


def _vec_spec(d):
    return pl.BlockSpec((1, d), lambda i: (0, 0))


_ANY = pl.BlockSpec(memory_space=pl.ANY)


def _place():
    x, y, c = lax.axis_index("x"), lax.axis_index("y"), lax.axis_index("c")
    return x, y, c, [(1 - x, y), (x, 1 - y), (1 - x, 1 - y)]


def _remote(src, dst, send_sem, recv_sem, to):
    return pltpu.make_async_remote_copy(src_ref=src, dst_ref=dst, send_sem=send_sem, recv_sem=recv_sem,
                                        device_id=to, device_id_type=MESH)


class _Carrier:
    def __init__(self):
        self.inputs, self.out_shapes, self.aliases, self.ops, self.n_sems, self.results = [], [], {}, [], 0, None

    def inplace(self, arr):
        self.aliases[len(self.inputs)] = len(self.out_shapes)
        self.inputs.append(arr)
        self.out_shapes.append(S(arr.shape, arr.dtype))
        return len(self.out_shapes) - 1

    def read(self, arr):
        self.inputs.append(arr)
        return len(self.inputs) - 1

    def fresh(self, shape, dtype):
        self.out_shapes.append(S(shape, dtype))
        return len(self.out_shapes) - 1

    def _add(self, n_sems, copies):
        base = self.n_sems
        self.n_sems += n_sems

        def start(ins, outs, send, recv):
            for k, (src, dst, _, to) in enumerate(copies(ins, outs)):
                _remote(src, dst, send.at[base + k], recv.at[base + k], to).start()

        def finish(ins, outs, send, recv):
            for k, (src, _, land, to) in enumerate(copies(ins, outs)):
                _remote(src, land, send.at[base + k], recv.at[base + k], to).wait()

        self.ops.append((start, finish))

    def gather_ici(self, o, rows=None, split=True):
        half = self.out_shapes[o].shape[1] // 2
        lo, size = rows or (0, half)

        def copies(ins, outs):
            x, y, c, chips = _place()
            part = (lambda ref: ref.at[pl.ds(c * half + lo, size)]) if split else (lambda ref: ref)
            mine = part(outs[o].at[2 * x + y])
            return [(mine, mine, part(outs[o].at[2 * px + py]), (px, py, c)) for px, py in chips]

        self._add(3, copies)

    def gather_d2d(self, o, rows=None):
        half = self.out_shapes[o].shape[1] // 2
        lo, size = rows or (0, half)

        def copies(ins, outs):
            x, y, c, chips = _place()
            at = lambda k, cc: outs[o].at[k].at[pl.ds(cc * half + lo, size)]
            return [(at(2 * px + py, c), at(2 * px + py, c), at(2 * px + py, 1 - c), (x, y, 1 - c)) for px, py in chips]

        self._add(3, copies)

    def swap(self, i, o):
        half = self.inputs[i].shape[1] // 2

        def copies(ins, outs):
            x, y, c, _ = _place()
            return [(ins[i].at[:, pl.ds((1 - c) * half, half)], outs[o], outs[o], (x, y, 1 - c))]

        self._add(1, copies)

    def scatter(self, i, o):
        def copies(ins, outs):
            x, y, c, chips = _place()
            return [(ins[i].at[2 * px + py], outs[o].at[2 * x + y], outs[o].at[2 * px + py], (px, py, c)) for px, py in chips]

        self._add(3, copies)

    def share(self, o):
        r = self.out_shapes[o].shape[0] // 2

        def copies(ins, outs):
            x, y, c, _ = _place()
            mine = outs[o].at[pl.ds(c * r, r)]
            return [(mine, mine, outs[o].at[pl.ds((1 - c) * r, r)], (x, y, 1 - c))]

        self._add(1, copies)

    def spread(self, i, o):
        def copies(ins, outs):
            x, y, c, _ = _place()
            me = 4 * x + 2 * y + c
            out = []
            for d in range(1, 8):
                to, frm = (me + d) % 8, (me + 8 - d) % 8
                out.append((ins[i], outs[o].at[me], outs[o].at[frm], (to // 4, (to // 2) % 2, to % 2)))
            return out

        self._add(7, copies)

    def _pallas(self, body, n_in, n_out, scratch, **kw):
        k_in, k_out = len(self.inputs), len(self.out_shapes)
        grid = kw.get("grid", ())

        def wrapped(*refs):
            ins, cins = refs[:n_in], refs[n_in:n_in + k_in]
            outs = refs[n_in + k_in:n_in + k_in + n_out]
            couts = refs[n_in + k_in + n_out:n_in + k_in + n_out + k_out]
            own = refs[n_in + k_in + n_out + k_out:]
            send, recv = own[len(scratch):]
            ids = [pl.program_id(a) for a in range(len(grid))]
            first = functools.reduce(jnp.logical_and, [a == 0 for a in ids], True)
            last = functools.reduce(jnp.logical_and, [a == g - 1 for a, g in zip(ids, grid)], True)

            def go(stage):
                for op in self.ops:
                    op[stage](cins, couts, send, recv)

            if grid:
                pl.when(first)(lambda: go(0))
                body(*ins, *outs, *own[:len(scratch)])
                pl.when(last)(lambda: go(1))
            else:
                go(0)
                go(1)

        sem = pltpu.SemaphoreType.DMA((self.n_sems,))
        return pl.pallas_call(
            wrapped, in_specs=list(kw.get("in_specs", [])) + [_ANY] * k_in, out_specs=list(kw.get("out_specs", [])) + [_ANY] * k_out,
            out_shape=list(kw.get("out_shape", [])) + self.out_shapes, scratch_shapes=list(scratch) + [sem, sem],
            input_output_aliases={n_in + i: n_out + o for i, o in self.aliases.items()}, name=kw["name"],
            **({"grid": grid, "compiler_params": _cp(*["arbitrary"] * len(grid))} if grid else {}))

    def run(self, body, kw, *args):
        single = not isinstance(kw["out_shape"], (list, tuple))
        out_shape = [kw["out_shape"]] if single else list(kw["out_shape"])
        out_specs = [kw["out_specs"]] if single else list(kw["out_specs"])
        res = self._pallas(body, len(args), len(out_shape), kw.get("scratch_shapes", []), grid=kw["grid"], in_specs=kw["in_specs"],
                           out_specs=out_specs, out_shape=out_shape, name=kw["name"])(*args, *self.inputs)
        self.results = list(res[len(out_shape):])
        return res[0] if single else list(res[:len(out_shape)])

    def run_alone(self, name):
        self.results = list(self._pallas(None, 0, 0, [], name=name)(*self.inputs))


def _call(comm, body, **kw):
    if comm is None:
        return pl.pallas_call(body, **kw)
    return functools.partial(comm.run, body, kw)


def _mm_nn(a, b3, *, bm, bn, bk, out_dtype, name, comm=None):
    m, k = a.shape
    c, _, n = b3.shape
    ni, nj, nk = m // bm, n // bn, k // bk

    def body(a_ref, b_ref, o_ref, *acc):
        if nk == 1:
            o_ref[...] = _dot(a_ref[...], b_ref[...]).astype(o_ref.dtype)
            return
        kk = pl.program_id(3)

        @pl.when(kk == 0)
        def _():
            acc[0][...] = jnp.zeros_like(acc[0])

        acc[0][...] += _dot(a_ref[...], b_ref[...])

        @pl.when(kk == nk - 1)
        def _():
            o_ref[...] = acc[0][...].astype(o_ref.dtype)

    return _call(
        comm, body, grid=(c, nj, ni, nk),
        in_specs=[pl.BlockSpec((bm, bk), lambda cc, j, i, kk: (i, kk)),
                  pl.BlockSpec((None, bk, bn), lambda cc, j, i, kk: (cc, kk, j))],
        out_specs=pl.BlockSpec((bm, bn), lambda cc, j, i, kk: (i, cc * nj + j)),
        out_shape=S((m, c * n), out_dtype),
        scratch_shapes=[] if nk == 1 else [pltpu.VMEM((bm, bn), F32)],
        compiler_params=_cp("parallel", "parallel", "parallel", "arbitrary"), name=name)(a, b3)


def _mm_nt(a, b3, *, bm, bo, out_dtype, name, comm=None):
    m = a.shape[0]
    c, ko, n = b3.shape
    ni, nj = m // bm, ko // bo

    def body(a_ref, b_ref, o_ref, *acc):
        if c == 1:
            o_ref[...] = _dot_nt(a_ref[...], b_ref[...]).astype(o_ref.dtype)
            return
        cc = pl.program_id(2)

        @pl.when(cc == 0)
        def _():
            acc[0][...] = jnp.zeros_like(acc[0])

        acc[0][...] += _dot_nt(a_ref[...], b_ref[...])

        @pl.when(cc == c - 1)
        def _():
            o_ref[...] = acc[0][...].astype(o_ref.dtype)

    return _call(
        comm, body, grid=(nj, ni, c),
        in_specs=[pl.BlockSpec((bm, n), lambda j, i, cc: (i, cc)),
                  pl.BlockSpec((None, bo, n), lambda j, i, cc: (cc, j, 0))],
        out_specs=pl.BlockSpec((bm, bo), lambda j, i, cc: (i, j)),
        out_shape=S((m, ko), out_dtype),
        scratch_shapes=[] if c == 1 else [pltpu.VMEM((bm, bo), F32)],
        compiler_params=_cp("parallel", "parallel", "arbitrary"), name=name)(a, b3)


def _mm_tn(a, b, c, *, bm, bk, name, comm=None):
    m, k = a.shape
    n = b.shape[1] // c
    nm, nk = m // bm, k // bk

    def body(a_ref, b_ref, o_ref, acc):
        mm = pl.program_id(2)

        @pl.when(mm == 0)
        def _():
            acc[...] = jnp.zeros_like(acc)

        acc[...] += _dot_tn(a_ref[...], b_ref[...])

        @pl.when(mm == nm - 1)
        def _():
            o_ref[...] = acc[...]

    return _call(
        comm, body, grid=(c, nk, nm),
        in_specs=[pl.BlockSpec((bm, bk), lambda cc, j, mm: (mm, j)),
                  pl.BlockSpec((bm, n), lambda cc, j, mm: (mm, cc))],
        out_specs=pl.BlockSpec((None, bk, n), lambda cc, j, mm: (cc, j, 0)),
        out_shape=S((c, k, n), F32),
        scratch_shapes=[pltpu.VMEM((bk, n), F32)],
        compiler_params=_cp("parallel", "parallel", "arbitrary"), name=name)(a, b)


def _swiglu_fwd(hn, wg3, wu3, *, bm, name, comm=None):
    m, k = hn.shape
    c, _, n = wg3.shape

    def body(a_ref, g_ref, u_ref, gate_ref, up_ref, act_ref):
        a = a_ref[...]
        gate = _dot(a, g_ref[...])
        up = _dot(a, u_ref[...])
        gate_ref[...] = gate
        up_ref[...] = up
        act_ref[...] = (gate * jax.nn.sigmoid(gate) * up).astype(act_ref.dtype)

    wspec = pl.BlockSpec((None, k, n), lambda cc, i: (cc, 0, 0))
    ospec = pl.BlockSpec((bm, n), lambda cc, i: (i, cc))
    return _call(
        comm, body, grid=(c, m // bm),
        in_specs=[pl.BlockSpec((bm, k), lambda cc, i: (i, 0)), wspec, wspec],
        out_specs=[ospec, ospec, ospec],
        out_shape=[S((m, c * n), F32), S((m, c * n), F32), S((m, c * n), _MXU)],
        compiler_params=_cp("parallel", "parallel"), name=name)(hn, wg3, wu3)


def _swiglu_bwd(df, wd, gate, up, *, bm, bo, name):
    m, k = df.shape
    ko = wd.shape[0]

    def body(a_ref, b_ref, g_ref, u_ref, dg_ref, du_ref):
        dact = _dot_nt(a_ref[...], b_ref[...])
        gate = g_ref[...]
        sg = jax.nn.sigmoid(gate)
        dg_ref[...] = (dact * u_ref[...] * (sg * (1.0 + gate * (1.0 - sg)))).astype(dg_ref.dtype)
        du_ref[...] = (dact * (gate * sg)).astype(du_ref.dtype)

    ospec = pl.BlockSpec((bm, bo), lambda j, i: (i, j))
    return pl.pallas_call(
        body, grid=(ko // bo, m // bm),
        in_specs=[pl.BlockSpec((bm, k), lambda j, i: (i, 0)), pl.BlockSpec((bo, k), lambda j, i: (j, 0)), ospec, ospec],
        out_specs=[ospec, ospec],
        out_shape=[S((m, ko), _MXU), S((m, ko), _MXU)],
        compiler_params=_cp("parallel", "parallel"), name=name)(df, wd, gate, up)


def _rms_fwd(x, gain, name, comm=None):
    t, d = x.shape
    tm = min(t, ROW_TILE)

    def body(x_ref, g_ref, o_ref):
        xv = x_ref[...]
        o_ref[...] = ((xv * _rstd(xv)) * g_ref[...]).astype(o_ref.dtype)

    return _call(comm, body, grid=(t // tm,), in_specs=[_row_spec(tm, d), _vec_spec(d)], out_specs=_row_spec(tm, d),
                          out_shape=S((t, d), _MXU), compiler_params=_cp("parallel"), name=name)(x, gain)


def _outnorm_fwd(o, yl, ga, gl, name):
    t, w = o.shape
    tm = min(t, ROW_TILE)

    def body(o_ref, l_ref, ga_ref, gl_ref, y_ref):
        ov, lv = o_ref[...], l_ref[...]
        y_ref[:, :w] = ((ov * _rstd(ov)) * ga_ref[...]).astype(y_ref.dtype)
        y_ref[:, w:] = ((lv * _rstd(lv)) * gl_ref[...]).astype(y_ref.dtype)

    return pl.pallas_call(body, grid=(t // tm,), in_specs=[_row_spec(tm, w), _row_spec(tm, w), _vec_spec(w), _vec_spec(w)],
                          out_specs=_row_spec(tm, 2 * w), out_shape=S((t, 2 * w), _MXU),
                          compiler_params=_cp("parallel"), name=name)(o, yl, ga, gl)


def _mid_fwd(x, mix, g_post, g_pre, name, comm=None):
    t, d = x.shape
    tm = min(t, ROW_TILE)

    def body(x_ref, m_ref, gp_ref, gn_ref, x2_ref, hn_ref):
        mv = m_ref[...]
        x2 = x_ref[...] + (mv * _rstd(mv)) * gp_ref[...]
        x2_ref[...] = x2
        hn_ref[...] = ((x2 * _rstd(x2)) * gn_ref[...]).astype(hn_ref.dtype)

    return _call(comm, body, grid=(t // tm,), in_specs=[_row_spec(tm, d), _row_spec(tm, d), _vec_spec(d), _vec_spec(d)],
                          out_specs=[_row_spec(tm, d), _row_spec(tm, d)], out_shape=[S((t, d), F32), S((t, d), _MXU)],
                          compiler_params=_cp("parallel"), name=name)(x, mix, g_post, g_pre)


def _final(f, x2, target, g_post, name):
    t, d = f.shape
    tm = min(t, ROW_TILE)

    def body(f_ref, x2_ref, t_ref, g_ref, loss_ref, dout_ref, df_ref, dg_ref):
        @pl.when(pl.program_id(0) == 0)
        def _():
            loss_ref[...] = jnp.zeros_like(loss_ref)
            dg_ref[...] = jnp.zeros_like(dg_ref)

        fv = f_ref[...]
        r = _rstd(fv)
        fh = fv * r
        err = (x2_ref[...] + fh * g_ref[...]) - t_ref[...]
        loss_ref[...] += jnp.sum(err * err, axis=0, keepdims=True)
        dout = err * (1.0 / d)
        dout_ref[...] = dout
        dfv, dg = _rms_bwd(dout, fh, r, g_ref[...])
        df_ref[...] = dfv.astype(df_ref.dtype)
        dg_ref[...] += dg

    return pl.pallas_call(
        body, grid=(t // tm,),
        in_specs=[_row_spec(tm, d), _row_spec(tm, d), _row_spec(tm, d), _vec_spec(d)],
        out_specs=[_vec_spec(d), _row_spec(tm, d), _row_spec(tm, d), _vec_spec(d)],
        out_shape=[S((1, d), F32), S((t, d), F32), S((t, d), _MXU), S((1, d), F32)],
        compiler_params=_cp("arbitrary"), name=name)(f, x2, target, g_post)


def _mid_bwd(dhn_a, dhn_b, dout, x2, mix, g_pre, g_post, name, comm=None):
    t, d = x2.shape
    tm = min(t, ROW_TILE)

    def body(da_ref, db_ref, do_ref, x2_ref, m_ref, gn_ref, gp_ref, dx2_ref, dm_ref, dgn_ref, dgp_ref):
        @pl.when(pl.program_id(0) == 0)
        def _():
            dgn_ref[...] = jnp.zeros_like(dgn_ref)
            dgp_ref[...] = jnp.zeros_like(dgp_ref)

        x2 = x2_ref[...]
        r = _rstd(x2)
        dxa, dgn = _rms_bwd(da_ref[...] + db_ref[...], x2 * r, r, gn_ref[...])
        dx2 = do_ref[...] + dxa
        dx2_ref[...] = dx2
        dgn_ref[...] += dgn
        mv = m_ref[...]
        rm = _rstd(mv)
        dmv, dgp = _rms_bwd(dx2, mv * rm, rm, gp_ref[...])
        dm_ref[...] = dmv.astype(dm_ref.dtype)
        dgp_ref[...] += dgp

    rs, vs = _row_spec(tm, d), _vec_spec(d)
    return _call(
        comm, body, grid=(t // tm,), in_specs=[rs, rs, rs, rs, rs, vs, vs], out_specs=[rs, rs, vs, vs],
        out_shape=[S((t, d), F32), S((t, d), _MXU), S((1, d), F32), S((1, d), F32)],
        compiler_params=_cp("arbitrary"), name=name)(dhn_a, dhn_b, dout, x2, mix, g_pre, g_post)


def _first_bwd(dhn, dx2, x, gain, name, comm=None):
    t, d = x.shape
    tm = min(t, ROW_TILE)

    def body(dh_ref, dx2_ref, x_ref, g_ref, dx_ref, dg_ref):
        @pl.when(pl.program_id(0) == 0)
        def _():
            dg_ref[...] = jnp.zeros_like(dg_ref)

        xv = x_ref[...]
        r = _rstd(xv)
        dxa, dg = _rms_bwd(dh_ref[...], xv * r, r, g_ref[...])
        dx_ref[...] = dx2_ref[...] + dxa
        dg_ref[...] += dg

    rs, vs = _row_spec(tm, d), _vec_spec(d)
    return _call(comm, body, grid=(t // tm,), in_specs=[rs, rs, rs, vs], out_specs=[rs, vs],
                          out_shape=[S((t, d), F32), S((1, d), F32)], compiler_params=_cp("arbitrary"), name=name)(dhn, dx2, x, gain)


def _outnorm_bwd(dy, o, yl, ga, gl, name, comm=None):
    t, w = o.shape
    tm = min(t, ROW_TILE)

    def body(dy_ref, o_ref, l_ref, ga_ref, gl_ref, do_ref, dl_ref, dga_ref, dgl_ref):
        @pl.when(pl.program_id(0) == 0)
        def _():
            dga_ref[...] = jnp.zeros_like(dga_ref)
            dgl_ref[...] = jnp.zeros_like(dgl_ref)

        ov, lv = o_ref[...], l_ref[...]
        ra, rl = _rstd(ov), _rstd(lv)
        dov, dga = _rms_bwd(dy_ref[:, :w], ov * ra, ra, ga_ref[...])
        dlv, dgl = _rms_bwd(dy_ref[:, w:], lv * rl, rl, gl_ref[...])
        do_ref[...] = dov
        dl_ref[...] = dlv
        dga_ref[...] += dga
        dgl_ref[...] += dgl

    rs, vs = _row_spec(tm, w), _vec_spec(w)
    return _call(comm, body, grid=(t // tm,), in_specs=[_row_spec(tm, 2 * w), rs, rs, vs, vs], out_specs=[rs, rs, vs, vs],
                          out_shape=[S((t, w), F32), S((t, w), F32), S((1, w), F32), S((1, w), F32)],
                          compiler_params=_cp("arbitrary"), name=name)(dy, o, yl, ga, gl)


def _split_dot(v, tri):
    hi = v.astype(_MXU)
    lo = (v - hi.astype(F32)).astype(_MXU)
    return _dot(hi, tri) + _dot(lo, tri)


def _attn_tile(qb, kb, row, col, shift, scale):
    z = _dot_nt(qb, kb) * scale
    mask = (col + shift) < row
    lb = _log_sigmoid(z)
    lm = jnp.where(mask, lb - z, 0.0)
    return mask, lb, lm


def _attn_fwd(proj, n_heads, name, comm=None):
    t = proj.shape[0]
    bq = min(t, ATTN_BLOCK)
    nq = t // bq
    scale = 1.0 / math.sqrt(HEAD_DIM)

    def body(q_ref, k_ref, v_ref, o_ref, kb_ref, vb_ref):
        kb_ref[...] = k_ref[...].astype(_MXU)
        vb_ref[...] = v_ref[...].astype(_MXU)
        row = lax.broadcasted_iota(jnp.int32, (bq, bq), 0)
        col = lax.broadcasted_iota(jnp.int32, (bq, bq), 1)
        tri = (row > col).astype(_MXU)

        def per_q(qi, _):
            q0 = pl.multiple_of(qi * bq, bq)
            qb = q_ref[pl.ds(q0, bq), :].astype(_MXU)

            def cond(st):
                return jnp.logical_and(st[0] >= 0, st[1])

            def step(st):
                kj, _, carry, acc = st
                k0 = pl.multiple_of(kj * bq, bq)
                mask, lb, lm = _attn_tile(qb, kb_ref[pl.ds(k0, bq), :], row, col, (kj - qi) * bq, scale)
                w = jnp.where(mask, jnp.exp(lb + _split_dot(lm, tri) + carry), 0.0)
                acc = acc + _dot(w.astype(_MXU), vb_ref[pl.ds(k0, bq), :])
                carry = carry + jnp.sum(lm, axis=1, keepdims=True)
                return kj - 1, jnp.max(carry) > EXP_CUT, carry, acc

            st = lax.while_loop(cond, step, (qi, jnp.bool_(True), jnp.zeros((bq, 1), F32), jnp.zeros((bq, HEAD_DIM), F32)))
            o_ref[pl.ds(q0, bq), :] = st[3]
            return 0

        lax.fori_loop(0, nq, per_q, 0)

    hs = lambda off: pl.BlockSpec((t, HEAD_DIM), lambda h: (0, off + h))
    return _call(
        comm, body, grid=(n_heads,), in_specs=[hs(0), hs(n_heads), hs(2 * n_heads)], out_specs=hs(0),
        out_shape=S((t, n_heads * HEAD_DIM), F32),
        scratch_shapes=[pltpu.VMEM((t, HEAD_DIM), _MXU), pltpu.VMEM((t, HEAD_DIM), _MXU)],
        compiler_params=_cp("parallel"), name=name)(proj, proj, proj)


def _attn_bwd(proj, do, n_heads, name, comm=None):
    t = proj.shape[0]
    bq = min(t, ATTN_BLOCK)
    nq = t // bq
    scale = 1.0 / math.sqrt(HEAD_DIM)

    def body(q_ref, k_ref, v_ref, do_ref, dq_ref, dk_ref, dv_ref, kb_ref, vb_ref, dka_ref, dva_ref, g_ref, b_ref):
        kb_ref[...] = k_ref[...].astype(_MXU)
        vb_ref[...] = v_ref[...].astype(_MXU)
        dka_ref[...] = jnp.zeros_like(dka_ref)
        dva_ref[...] = jnp.zeros_like(dva_ref)
        row = lax.broadcasted_iota(jnp.int32, (bq, bq), 0)
        col = lax.broadcasted_iota(jnp.int32, (bq, bq), 1)
        tri = (row > col).astype(_MXU)
        tri_lt = (row < col).astype(_MXU)

        def per_q(qi, _):
            q0 = pl.multiple_of(qi * bq, bq)
            qb = q_ref[pl.ds(q0, bq), :].astype(_MXU)
            dob = do_ref[pl.ds(q0, bq), :].astype(_MXU)

            def cond(st):
                return jnp.logical_and(st[0] >= 0, st[1])

            def step(st):
                kj, _, carry = st
                k0 = pl.multiple_of(kj * bq, bq)
                mask, lb, lm = _attn_tile(qb, kb_ref[pl.ds(k0, bq), :], row, col, (kj - qi) * bq, scale)
                w = jnp.where(mask, jnp.exp(lb + _split_dot(lm, tri) + carry), 0.0)
                g_ref[pl.ds(k0, bq), :] = w * _dot_nt(dob, vb_ref[pl.ds(k0, bq), :])
                b_ref[pl.ds(k0, bq), :] = jnp.where(mask, jnp.exp(lb), 0.0)
                dva_ref[pl.ds(k0, bq), :] += _dot_tn(w.astype(_MXU), dob)
                carry = carry + jnp.sum(lm, axis=1, keepdims=True)
                return kj - 1, jnp.max(carry) > EXP_CUT, carry

            st = lax.while_loop(cond, step, (qi, jnp.bool_(True), jnp.zeros((bq, 1), F32)))

            def back(kj, st2):
                before, dq = st2
                k0 = pl.multiple_of(kj * bq, bq)
                kb = kb_ref[pl.ds(k0, bq), :]
                g = g_ref[pl.ds(k0, bq), :]
                beta = b_ref[pl.ds(k0, bq), :]
                dz = ((g * (1.0 - beta) - (before + _split_dot(g, tri_lt)) * beta) * scale).astype(_MXU)
                dka_ref[pl.ds(k0, bq), :] += _dot_tn(dz, qb)
                return before + jnp.sum(g, axis=1, keepdims=True), dq + _dot(dz, kb)

            st2 = lax.fori_loop(st[0] + 1, qi + 1, back, (jnp.zeros((bq, 1), F32), jnp.zeros((bq, HEAD_DIM), F32)))
            dq_ref[pl.ds(q0, bq), :] = st2[1].astype(dq_ref.dtype)
            return 0

        lax.fori_loop(0, nq, per_q, 0)
        dk_ref[...] = dka_ref[...].astype(dk_ref.dtype)
        dv_ref[...] = dva_ref[...].astype(dv_ref.dtype)

    hs = lambda off: pl.BlockSpec((t, HEAD_DIM), lambda h: (0, off + h))
    w = n_heads * HEAD_DIM
    return _call(
        comm, body, grid=(n_heads,), in_specs=[hs(0), hs(n_heads), hs(2 * n_heads), hs(0)], out_specs=[hs(0), hs(0), hs(0)],
        out_shape=[S((t, w), _MXU)] * 3,
        scratch_shapes=[pltpu.VMEM((t, HEAD_DIM), _MXU), pltpu.VMEM((t, HEAD_DIM), _MXU), pltpu.VMEM((t, HEAD_DIM), F32),
                        pltpu.VMEM((t, HEAD_DIM), F32), pltpu.VMEM((t, bq), F32), pltpu.VMEM((t, bq), F32)],
        compiler_params=_cp("parallel"), name=name)(proj, proj, proj, do)


def _shift_down(cur, prev8, k):
    if k == 0:
        return cur
    row8 = lax.broadcasted_iota(jnp.int32, prev8.shape, 0)
    rc = pltpu.roll(cur, k, 0)
    top = jnp.where(row8 < k, pltpu.roll(prev8, k, 0), rc[0:8, :])
    return jnp.concatenate([top, rc[8:, :]], axis=0)


def _shift_up(cur, next8, k):
    if k == 0:
        return cur
    n = cur.shape[0]
    row8 = lax.broadcasted_iota(jnp.int32, next8.shape, 0)
    rc = pltpu.roll(cur, n - k, 0)
    bottom = jnp.where(row8 >= 8 - k, pltpu.roll(next8, 8 - k, 0), rc[n - 8:, :])
    return jnp.concatenate([rc[:n - 8, :], bottom], axis=0)


def _lru_gates(xl, prev8, cw, cb, wr, br, wi, bi, ls):
    xs = [_shift_down(xl, prev8, CONV_WIDTH - 1 - k) for k in range(CONV_WIDTH)]
    xc = xs[0] * cw[0:1, :]
    for k in range(1, CONV_WIDTH):
        xc = xc + xs[k] * cw[k:k + 1, :]
    xc = xc + cb
    xcb = xc.astype(_MXU)
    r = jax.nn.sigmoid(_dot(xcb, wr) + br)
    i = jax.nn.sigmoid(_dot(xcb, wi) + bi)
    la = (LRU_C * r) * ls
    a = jnp.exp(la)
    mult = jnp.sqrt(-_expm1(2.0 * la))
    return xs, xc, r, i, a, mult


def _group_scan(a, b, reverse):
    n = a.shape[0]
    row = lax.broadcasted_iota(jnp.int32, a.shape, 0) % 8
    for d in (1, 2, 4):
        if reverse:
            m = row < 8 - d
            a_s, b_s = pltpu.roll(a, n - d, 0), pltpu.roll(b, n - d, 0)
        else:
            m = row >= d
            a_s, b_s = pltpu.roll(a, d, 0), pltpu.roll(b, d, 0)
        b = jnp.where(m, a * b_s + b, b)
        a = jnp.where(m, a * a_s, a)
    return a, b


def _lru_fwd(proj, col0, n_blocks, cw, cb, wr, br, wi, bi, lam, name, comm=None):
    t = proj.shape[0]
    tt = min(t, SEQ_TILE)
    nt = t // tt

    def body(xl_ref, gl_ref, cw_ref, cb_ref, wr_ref, br_ref, wi_ref, bi_ref, lam_ref, h_ref, y_ref):
        cwv, cbv, brv, biv = cw_ref[...], cb_ref[...], br_ref[...], bi_ref[...]
        wrv, wiv = wr_ref[...].astype(_MXU), wi_ref[...].astype(_MXU)
        ls = _log_sigmoid(lam_ref[...])

        def tile(ti, hin):
            t0 = pl.multiple_of(ti * tt, tt)
            p0 = pl.multiple_of(jnp.maximum(t0 - 8, 0), 8)
            prev8 = xl_ref[pl.ds(p0, 8), :] * (ti > 0).astype(F32)
            xl = xl_ref[pl.ds(t0, tt), :]
            _, xc, _, ig, a, mult = _lru_gates(xl, prev8, cwv, cbv, wrv, brv, wiv, biv, ls)
            ga, gb = _group_scan(a, mult * (ig * xc), False)
            for g in range(tt // 8):
                hg = ga[8 * g:8 * g + 8, :] * hin + gb[8 * g:8 * g + 8, :]
                h_ref[pl.ds(t0 + 8 * g, 8), :] = hg
                hin = hg[7:8, :]
            y_ref[pl.ds(t0, tt), :] = h_ref[pl.ds(t0, tt), :] * _gelu(gl_ref[pl.ds(t0, tt), :])
            return hin

        lax.fori_loop(0, nt, tile, jnp.zeros((1, HEAD_DIM), F32))

    cs = lambda off: pl.BlockSpec((t, HEAD_DIM), lambda n: (0, off + n))
    vs = pl.BlockSpec((1, HEAD_DIM), lambda n: (0, n))
    ws = pl.BlockSpec((None, HEAD_DIM, HEAD_DIM), lambda n: (n, 0, 0))
    w = n_blocks * HEAD_DIM
    return _call(
        comm, body, grid=(n_blocks,),
        in_specs=[cs(col0), cs(col0 + n_blocks), pl.BlockSpec((CONV_WIDTH, HEAD_DIM), lambda n: (0, n)), vs, ws, vs, ws, vs, vs],
        out_specs=[cs(0), cs(0)], out_shape=[S((t, w), F32), S((t, w), F32)],
        compiler_params=_cp("parallel"), name=name)(proj, proj, cw, cb, wr, br, wi, bi, lam)


def _lru_bwd(proj, col0, n_blocks, h, dyl, cw, cb, wr, br, wi, bi, lam, name, comm=None):
    t = proj.shape[0]
    tt = min(t, SEQ_TILE)
    nt = t // tt

    def body(xl_ref, gl_ref, h_ref, dy_ref, cw_ref, cb_ref, wr_ref, br_ref, wi_ref, bi_ref, lam_ref,
             dxl_ref, dgl_ref, dcw_ref, dcb_ref, dwr_ref, dbr_ref, dwi_ref, dbi_ref, dlam_ref, g_ref):
        cwv, cbv, brv, biv = cw_ref[...], cb_ref[...], br_ref[...], bi_ref[...]
        wrv, wiv = wr_ref[...].astype(_MXU), wi_ref[...].astype(_MXU)
        lamv = lam_ref[...]
        ls = _log_sigmoid(lamv)
        for ref in (dcw_ref, dcb_ref, dwr_ref, dbr_ref, dwi_ref, dbi_ref, dlam_ref):
            ref[...] = jnp.zeros_like(ref)

        def tile(s, carry):
            e_in, dxc_next8 = carry
            ti = nt - 1 - s
            t0 = pl.multiple_of(ti * tt, tt)
            p0 = pl.multiple_of(jnp.maximum(t0 - 8, 0), 8)
            first = (ti > 0).astype(F32)
            xl = xl_ref[pl.ds(t0, tt), :]
            xs, xc, r, ig, a, mult = _lru_gates(xl, xl_ref[pl.ds(p0, 8), :] * first, cwv, cbv, wrv, brv, wiv, biv, ls)
            hv = h_ref[pl.ds(t0, tt), :]
            h_before = _shift_down(hv, h_ref[pl.ds(p0, 8), :] * first, 1)
            glv = gl_ref[pl.ds(t0, tt), :]
            dyv = dy_ref[pl.ds(t0, tt), :]
            dgl_ref[pl.ds(t0, tt), :] = (dyv * hv * _gelu_grad(glv)).astype(dgl_ref.dtype)
            dh = dyv * _gelu(glv)
            row = lax.broadcasted_iota(jnp.int32, a.shape, 0)
            coef = jnp.where(row == tt - 1, 1.0, pltpu.roll(a, tt - 1, 0))
            ga, gb = _group_scan(coef, dh, True)
            gin = e_in
            for g in reversed(range(tt // 8)):
                gg = ga[8 * g:8 * g + 8, :] * gin + gb[8 * g:8 * g + 8, :]
                g_ref[8 * g:8 * g + 8, :] = gg
                gin = gg[0:1, :]
            gv = g_ref[...]
            e_out = a[0:1, :] * gv[0:1, :]
            ix = ig * xc
            dla = (gv * h_before) * a - (gv * ix) * (a * a / mult)
            dlam_ref[...] += jnp.sum(dla * (LRU_C * r), axis=0, keepdims=True)
            dpr = (dla * (LRU_C * ls)) * (r * (1.0 - r))
            dpi = (gv * mult * xc) * (ig * (1.0 - ig))
            dbr_ref[...] += jnp.sum(dpr, axis=0, keepdims=True)
            dbi_ref[...] += jnp.sum(dpi, axis=0, keepdims=True)
            xcb, dprb, dpib = xc.astype(_MXU), dpr.astype(_MXU), dpi.astype(_MXU)
            dwr_ref[...] += _dot_tn(xcb, dprb)
            dwi_ref[...] += _dot_tn(xcb, dpib)
            dxc = gv * mult * ig + _dot_nt(dprb, wrv) + _dot_nt(dpib, wiv)
            dcb_ref[...] += jnp.sum(dxc, axis=0, keepdims=True)
            dxl = None
            for k in range(CONV_WIDTH):
                dcw_ref[k:k + 1, :] += jnp.sum(dxc * xs[k], axis=0, keepdims=True)
                term = _shift_up(dxc, dxc_next8, CONV_WIDTH - 1 - k) * cwv[k:k + 1, :]
                dxl = term if dxl is None else dxl + term
            dxl_ref[pl.ds(t0, tt), :] = dxl.astype(dxl_ref.dtype)
            return e_out, dxc[0:8, :]

        lax.fori_loop(0, nt, tile, (jnp.zeros((1, HEAD_DIM), F32), jnp.zeros((8, HEAD_DIM), F32)))
        dlam_ref[...] = dlam_ref[...] * (1.0 - jax.nn.sigmoid(lamv))

    cs = lambda off: pl.BlockSpec((t, HEAD_DIM), lambda n: (0, off + n))
    vs = pl.BlockSpec((1, HEAD_DIM), lambda n: (0, n))
    ws = pl.BlockSpec((None, HEAD_DIM, HEAD_DIM), lambda n: (n, 0, 0))
    cws = pl.BlockSpec((CONV_WIDTH, HEAD_DIM), lambda n: (0, n))
    w = n_blocks * HEAD_DIM
    vec = S((1, w), F32)
    mat = S((n_blocks, HEAD_DIM, HEAD_DIM), F32)
    return _call(
        comm, body, grid=(n_blocks,),
        in_specs=[cs(col0), cs(col0 + n_blocks), cs(0), cs(0), cws, vs, ws, vs, ws, vs, vs],
        out_specs=[cs(0), cs(0), cws, vs, ws, vs, ws, vs, vs],
        out_shape=[S((t, w), _MXU), S((t, w), _MXU), S((CONV_WIDTH, w), F32), vec, mat, vec, mat, vec, vec],
        scratch_shapes=[pltpu.VMEM((tt, HEAD_DIM), F32)],
        compiler_params=_cp("parallel"), name=name)(proj, proj, h, dyl, cw, cb, wr, br, wi, bi, lam)


class _NoExchange:
    def __init__(self, weights):
        self.weights, self.grads, self.packs = weights, {}, {}

    def weight(self, name):
        return self.weights[name]

    def carrier(self, call):
        return None

    def harvest(self, car):
        pass

    def alone(self, call):
        pass


def _local_step(x, target, norms, ex, cw, cb, wr, br, wi, bi, lam, ga, gl):
    g_pre_mix, g_post_mix, g_pre_ffn, g_post_ffn = norms
    t, d = x.shape
    bm = min(t, 512)

    def run(fn, name, *args, **kw):
        car = ex.carrier(name)
        out = fn(*args, name=name, comm=car, **kw)
        ex.harvest(car)
        return out

    hn1 = run(_rms_fwd, "rms1", x, g_pre_mix)
    win3 = ex.weight("w_in")
    c = win3.shape[0]
    proj = run(_mm_nn, "in_proj", hn1, win3, bm=bm, bn=win3.shape[2], bk=d, out_dtype=F32)
    o = run(_attn_fwd, "attn_fwd", proj, (proj.shape[1] - d) // 3 // HEAD_DIM)
    mix = 2 * o.shape[1]
    n_heads = n_blocks = o.shape[1] // HEAD_DIM
    h, yl = run(_lru_fwd, "lru_fwd", proj, 3 * n_heads, n_blocks, cw, cb, wr, br, wi, bi, lam)
    y = _outnorm_fwd(o, yl, ga, gl, "outnorm_fwd")
    wout = ex.weight("w_out")
    mixo = run(_mm_nn, "out_proj", y, wout[None], bm=bm, bn=d, bk=mix, out_dtype=F32)
    x2, hn2 = run(_mid_fwd, "mid_fwd", x, mixo, g_post_mix, g_pre_ffn)
    wg3, wu3 = ex.weight("w_ffn_gate"), ex.weight("w_ffn_up")
    gate, up, act = run(_swiglu_fwd, "ffn_gate_up", hn2, wg3, wu3, bm=min(t, 256))
    ex.alone("gather_w_down")
    wd = ex.weight("w_ffn_down")
    ff = wd.shape[0]
    f = _mm_nn(act, wd[None], bm=bm, bn=d, bk=ff // 4, out_dtype=F32, name="ffn_down")
    loss_cols, dout, df, dg_post_ffn = _final(f, x2, target, g_post_ffn, "final")

    dgate, dup = _swiglu_bwd(df, wd, gate, up, bm=bm, bo=ff // 4, name="ffn_down_bwd")
    ex.grads["w_ffn_down"] = _mm_tn(act, df, 1, bm=bm, bk=512, name="ffn_down_dw").reshape(c, ff // c, d)
    ex.grads["w_ffn_gate"] = run(_mm_tn, "ffn_gate_dw", hn2, dgate, c, bm=bm, bk=d // 2)
    ex.grads["w_ffn_up"] = run(_mm_tn, "ffn_up_dw", hn2, dup, c, bm=bm, bk=d // 2)
    dhn2_g = run(_mm_nt, "ffn_gate_dx", dgate, wg3, bm=bm, bo=d, out_dtype=F32)
    dhn2_u = run(_mm_nt, "ffn_up_dx", dup, wu3, bm=bm, bo=d, out_dtype=F32)
    dx2, dmix, dg_pre_ffn, dg_post_mix = run(_mid_bwd, "mid_bwd", dhn2_g, dhn2_u, dout, x2, mixo, g_pre_ffn, g_post_mix)
    dy = _mm_nt(dmix, wout[None], bm=bm, bo=mix, out_dtype=F32, name="out_proj_dx")
    ex.grads["w_out"] = _mm_tn(y, dmix, 1, bm=bm, bk=mix // 2, name="out_proj_dw").reshape(c, mix // c, d)
    do, dyl, dga, dgl_norm = run(_outnorm_bwd, "outnorm_bwd", dy, o, yl, ga, gl)
    dxl, dglu, dcw, dcb, dwr, dbr, dwi, dbi, dlam = run(_lru_bwd, "lru_bwd", proj, 3 * n_heads, n_blocks, h, dyl, cw, cb, wr, br, wi, bi, lam)
    small = dict(post_mix_norm=dg_post_mix, pre_ffn_norm=dg_pre_ffn, post_ffn_norm=dg_post_ffn, conv_w=dcw, conv_b=dcb,
                 w_rgate=dwr, b_rgate=dbr, w_igate=dwi, b_igate=dbi, lru_lambda=dlam, attn_out_norm=dga, lru_out_norm=dgl_norm)
    ex.packs["early"] = _pack([small[n] for n in _SMALL_EARLY])
    dq, dk, dv = run(_attn_bwd, "attn_bwd", proj, do, n_heads)
    dproj = jnp.concatenate([dq, dk, dv, dxl, dglu], axis=1)
    dhn1 = _mm_nt(dproj, win3, bm=bm, bo=d, out_dtype=F32, name="in_proj_dx")
    ex.grads["w_in"] = _mm_tn(hn1, dproj, c, bm=bm, bk=d // 2, name="in_proj_dw")
    grad_x, small["pre_mix_norm"] = run(_first_bwd, "first_bwd", dhn1, dx2, x, g_pre_mix)
    ex.packs["late"] = _pack([small["pre_mix_norm"]])
    ex.alone("grads_w_in_scatter")
    ex.alone("grads_w_in_share")
    return loss_cols, grad_x, small


def _into_slot(wsh, slot, dtype, name):
    rows, n = wsh.shape
    rb = _row_block(rows, 256) if rows % 8 == 0 else rows

    def body(s_ref, w_ref, o_ref):
        o_ref[...] = w_ref[...].astype(o_ref.dtype)

    return pl.pallas_call(
        body,
        grid_spec=pltpu.PrefetchScalarGridSpec(
            num_scalar_prefetch=1, grid=(rows // rb,),
            in_specs=[pl.BlockSpec((rb, n), lambda i, s_ref: (i, 0))],
            out_specs=pl.BlockSpec((None, rb, n), lambda i, s_ref: (s_ref[0], i, 0))),
        out_shape=S((4, rows, n), dtype), compiler_params=_cp("parallel"), name=name)(slot, wsh)


class _Exchange:
    SCHEDULE = {
        "gather_w_in": [("ici", "w_in"), ("ici", "conv_w")],
        "rms1": [("d2d", "w_in")],
        "in_proj": [("ici", "w_out")],
        "attn_fwd": [("d2d", "w_out"), ("ici", "w_ffn_gate")],
        "lru_fwd": [("d2d", "w_ffn_gate"), ("ici", "w_ffn_up", 0)],
        "out_proj": [("d2d", "w_ffn_up", 0), ("ici", "w_ffn_up", 1)],
        "mid_fwd": [("d2d", "w_ffn_up", 1)],
        "ffn_gate_up": [("ici", "w_ffn_down")],
        "gather_w_down": [("d2d", "w_ffn_down")],
        "ffn_gate_dw": [("swap", "w_ffn_down")],
        "ffn_up_dw": [("scatter", "w_ffn_down"), ("swap", "w_ffn_gate")],
        "ffn_gate_dx": [("share", "w_ffn_down"), ("scatter", "w_ffn_gate"), ("swap", "w_ffn_up")],
        "ffn_up_dx": [("share", "w_ffn_gate"), ("scatter", "w_ffn_up")],
        "mid_bwd": [("share", "w_ffn_up")],
        "outnorm_bwd": [("swap", "w_out")],
        "lru_bwd": [("scatter", "w_out")],
        "attn_bwd": [("share", "w_out"), ("spread", "early")],
        "first_bwd": [("swap", "w_in")],
        "grads_w_in_scatter": [("scatter", "w_in"), ("spread", "late")],
        "grads_w_in_share": [("share", "w_in")],
    }

    def __init__(self, slots, place):
        self.buf, self.place = dict(slots), place
        self.grads, self.packs, self.swapped, self.part, self.scattered, self.full, self.spreaded = {}, {}, {}, {}, {}, {}, {}

    def weight(self, name):
        b = self.buf[name]
        return b.reshape(-1, b.shape[2]) if name in ("w_out", "w_ffn_down") else b

    def carrier(self, call):
        car = _Carrier()
        car.todo, slot = [], {}
        for kind, name, *piece in self.SCHEDULE[call]:
            if kind in ("ici", "d2d"):
                if name not in slot:
                    slot[name] = car.inplace(self.buf[name])
                    car.todo.append((self.buf, name, slot[name]))
                quarter = self.buf[name].shape[1] // 4
                rows = (piece[0] * quarter, quarter) if piece else None
                if kind == "ici":
                    car.gather_ici(slot[name], rows, split=name != "conv_w")
                else:
                    car.gather_d2d(slot[name], rows)
            elif kind == "swap":
                g = self.grads[name]
                o = car.fresh((4, g.shape[1] // 2, g.shape[2]), F32)
                car.swap(car.read(g), o)
                car.todo.append((self.swapped, name, o))
            elif kind == "scatter":
                self.part[name] = _add_own_half(self.grads[name], self.swapped[name], self.place[1:], "grads_add_" + name)
                o = car.fresh(self.part[name].shape, self.part[name].dtype)
                car.scatter(car.read(self.part[name]), o)
                car.todo.append((self.scattered, name, o))
            elif kind == "share":
                o = car.inplace(_sum_chips(self.part[name], self.scattered[name], self.place, "grads_sum_" + name))
                car.share(o)
                car.todo.append((self.full, name, o))
            else:
                o = car.fresh((8,) + self.packs[name].shape, F32)
                car.spread(car.read(self.packs[name]), o)
                car.todo.append((self.spreaded, name, o))
        return car

    def harvest(self, car):
        for state, name, o in car.todo:
            state[name] = car.results[o]

    def alone(self, call):
        car = self.carrier(call)
        car.run_alone(call)
        self.harvest(car)

    def small_sum(self, key):
        return _sum_devices(self.packs[key], self.spreaded[key], 2 * self.place[0:1] + self.place[1:], "grads_small_sum_" + key)


def _row_block(rows, cap):
    return max(b for b in range(8, cap + 1, 8) if rows % b == 0)


def _add_own_half(g, recv, core, name):
    _, rows, n = g.shape
    half = rows // 2
    rb = _row_block(half, 512)
    nb = half // rb

    def body(c_ref, g_ref, r_ref, o_ref):
        o_ref[...] = (g_ref[...] + r_ref[...]).astype(o_ref.dtype)

    return pl.pallas_call(
        body,
        grid_spec=pltpu.PrefetchScalarGridSpec(
            num_scalar_prefetch=1, grid=(4, nb),
            in_specs=[pl.BlockSpec((None, rb, n), lambda k, i, c_ref: (k, c_ref[0] * nb + i, 0)),
                      pl.BlockSpec((None, rb, n), lambda k, i, c_ref: (k, i, 0))],
            out_specs=pl.BlockSpec((None, rb, n), lambda k, i, c_ref: (k, i, 0))),
        out_shape=S((4, half, n), BF16), compiler_params=_cp("parallel", "parallel"), name=name)(core, g, recv)


def _sum_chips(part, recv, place, name):
    _, rows, n = part.shape
    rb = _row_block(rows, 64)
    nb = rows // rb

    def body(p_ref, own_ref, r0, r1, r2, r3, o_ref):
        own = own_ref[...].astype(F32)
        terms = [jnp.where(p_ref[0] == k, own, r[...].astype(F32)) for k, r in enumerate((r0, r1, r2, r3))]
        o_ref[...] = ((terms[0] + terms[1]) + terms[2]) + terms[3]

    def slot(k):
        return pl.BlockSpec((None, rb, n), lambda i, p_ref: (jnp.where(p_ref[0] == k, (k + 1) % 4, k), i, 0))

    return pl.pallas_call(
        body,
        grid_spec=pltpu.PrefetchScalarGridSpec(
            num_scalar_prefetch=1, grid=(nb,),
            in_specs=[pl.BlockSpec((None, rb, n), lambda i, p_ref: (p_ref[0], i, 0))] + [slot(k) for k in range(4)],
            out_specs=pl.BlockSpec((rb, n), lambda i, p_ref: (p_ref[1] * nb + i, 0))),
        out_shape=S((2 * rows, n), F32), compiler_params=_cp("parallel"), name=name)(place, part, recv, recv, recv, recv)


def _sum_devices(own, spread, me, name):
    rows = own.shape[0]

    def body(me_ref, own_ref, *refs):
        acc = None
        for k, r in enumerate(refs[:8]):
            term = jnp.where(me_ref[0] == k, own_ref[...], r[...])
            acc = term if acc is None else acc + term
        refs[8][...] = acc

    def slot(k):
        return pl.BlockSpec((None, rows, 128), lambda i, me_ref: (jnp.where(me_ref[0] == k, (k + 1) % 8, k), 0, 0))

    whole = pl.BlockSpec((rows, 128), lambda i, me_ref: (0, 0))
    return pl.pallas_call(
        body,
        grid_spec=pltpu.PrefetchScalarGridSpec(num_scalar_prefetch=1, grid=(1,), in_specs=[whole] + [slot(k) for k in range(8)],
                                               out_specs=whole),
        out_shape=S((rows, 128), F32), compiler_params=_cp("arbitrary"), name=name)(me, own, *[spread] * 8)


def _adamw(w, g, m, v, name):
    rows, n = w.shape
    rb = rows if rows * n * 4 <= (1 << 21) else _row_block(rows, 128)
    c1 = 1.0 - ADAM_B1 ** ADAM_STEP
    c2 = 1.0 - ADAM_B2 ** ADAM_STEP

    def body(w_ref, g_ref, m_ref, v_ref, d_ref, nm_ref, nv_ref):
        gv = g_ref[...]
        nm = ADAM_B1 * m_ref[...] + (1.0 - ADAM_B1) * gv
        nv = ADAM_B2 * v_ref[...] + (1.0 - ADAM_B2) * (gv * gv)
        nm_ref[...] = nm
        nv_ref[...] = nv
        d_ref[...] = -ADAM_LR * ((nm / c1) / (jnp.sqrt(nv / c2) + ADAM_EPS) + ADAM_WD * w_ref[...])

    bs = pl.BlockSpec((rb, n), lambda i: (i, 0))
    return pl.pallas_call(body, grid=(rows // rb,), in_specs=[bs] * 4, out_specs=[bs] * 3, out_shape=[S((rows, n), F32)] * 3,
                          compiler_params=_cp("parallel"), name=name)(w, g, m, v)


_BIG = ("w_in", "w_out", "w_ffn_gate", "w_ffn_up", "w_ffn_down")
_SMALL = ("pre_mix_norm", "post_mix_norm", "pre_ffn_norm", "post_ffn_norm", "conv_w", "conv_b", "w_rgate", "b_rgate",
          "w_igate", "b_igate", "lru_lambda", "attn_out_norm", "lru_out_norm")
_SMALL_EARLY = _SMALL[1:]
_WEIGHTS = ("pre_mix_norm", "post_mix_norm", "pre_ffn_norm", "post_ffn_norm", "w_in", "conv_w", "conv_b", "w_rgate", "b_rgate",
            "w_igate", "b_igate", "lru_lambda", "attn_out_norm", "lru_out_norm", "w_out", "w_ffn_gate", "w_ffn_up", "w_ffn_down")


def _pack(arrays):
    flat = []
    for a in arrays:
        f = a.reshape(-1)
        flat.append(jnp.pad(f, (0, (-f.shape[0]) % 1024)))
    return jnp.concatenate(flat).reshape(-1, 128)


def _unpack(packed, shapes):
    out, pos = [], 0
    flat = packed.reshape(-1)
    for s in shapes:
        size = math.prod(s)
        out.append(flat[pos:pos + size].reshape(s))
        pos += size + (-size) % 1024
    return out


def kernel(x, pre_mix_norm, post_mix_norm, pre_ffn_norm, post_ffn_norm, w_in, conv_w, conv_b, w_rgate, b_rgate, w_igate, b_igate, lru_lambda, attn_out_norm, lru_out_norm, w_out, w_ffn_gate, w_ffn_up, w_ffn_down, loss_target, m_pre_mix_norm, m_post_mix_norm, m_pre_ffn_norm, m_post_ffn_norm, m_w_in, m_conv_w, m_conv_b, m_w_rgate, m_b_rgate, m_w_igate, m_b_igate, m_lru_lambda, m_attn_out_norm, m_lru_out_norm, m_w_out, m_w_ffn_gate, m_w_ffn_up, m_w_ffn_down, v_pre_mix_norm, v_post_mix_norm, v_pre_ffn_norm, v_post_ffn_norm, v_w_in, v_conv_w, v_conv_b, v_w_rgate, v_b_rgate, v_w_igate, v_b_igate, v_lru_lambda, v_attn_out_norm, v_lru_out_norm, v_w_out, v_w_ffn_gate, v_w_ffn_up, v_w_ffn_down):
    given = dict(locals())
    w = {n: given[n][0] for n in _WEIGHTS}
    m = {n: given["m_" + n][0] for n in _WEIGHTS}
    v = {n: given["v_" + n][0] for n in _WEIGHTS}
    xs, target = x[0], loss_target[0]
    d = xs.shape[1]
    chip = (2 * lax.axis_index("x") + lax.axis_index("y")).astype(jnp.int32)
    place = jnp.stack([chip, lax.axis_index("c").astype(jnp.int32)])

    slots = {n: _into_slot(w[n], place[0:1], _MXU, "slot_" + n) for n in _BIG}
    slots["conv_w"] = _into_slot(w["conv_w"], place[0:1], F32, "slot_conv_w")
    ex = _Exchange(slots, place)
    ex.alone("gather_w_in")
    conv_full = jnp.transpose(ex.buf["conv_w"], (1, 0, 2)).reshape(CONV_WIDTH, -1)
    row = lambda a: a.reshape(1, -1)
    norms = tuple(row(w[n]) for n in ("pre_mix_norm", "post_mix_norm", "pre_ffn_norm", "post_ffn_norm"))

    loss_cols, grad_x, small = _local_step(
        xs, target, norms, ex, conv_full, row(w["conv_b"]), w["w_rgate"], row(w["b_rgate"]),
        w["w_igate"], row(w["b_igate"]), row(w["lru_lambda"]), row(w["attn_out_norm"]), row(w["lru_out_norm"]))

    loss = lax.psum(0.5 * jnp.sum(loss_cols) / d, ("x", "y", "c"))

    reduced = {n: ex.full[n] for n in _BIG}
    early = _unpack(ex.small_sum("early"), [small[n].shape for n in _SMALL_EARLY])
    late = _unpack(ex.small_sum("late"), [small["pre_mix_norm"].shape])
    for n, g in zip(_SMALL_EARLY + ("pre_mix_norm",), early + late):
        reduced[n] = g.reshape(w[n].shape) if n != "conv_w" else lax.dynamic_slice_in_dim(g, chip * w[n].shape[1], w[n].shape[1], axis=1)

    delta, new_m, new_v = {}, {}, {}
    for n in _BIG:
        delta[n], new_m[n], new_v[n] = _adamw(w[n], reduced[n], m[n], v[n], "adamw_" + n)
    shapes = [w[n].shape for n in _SMALL]
    packed = _adamw(*[_pack([src[n] for n in _SMALL]) for src in (w, reduced, m, v)], "adamw_small")
    for out, p in zip((delta, new_m, new_v), packed):
        out.update(zip(_SMALL, _unpack(p, shapes)))

    lead = lambda a: a[None]
    return (loss, lead(grad_x), *[lead(reduced[n]) for n in _WEIGHTS], *[lead(delta[n]) for n in _WEIGHTS],
            *[lead(new_m[n]) for n in _WEIGHTS], *[lead(new_v[n]) for n in _WEIGHTS])
```

```python
import functools
import math

import jax
import jax.numpy as jnp
from jax import lax
from jax.experimental import pallas as pl
from jax.experimental.pallas import tpu as pltpu

F32 = jnp.float32
BF16 = jnp.bfloat16
_MXU = BF16
S = jax.ShapeDtypeStruct

RMS_EPS = 1e-6
HEAD_DIM = 128
CONV_WIDTH = 4
LRU_C = 8.0
ADAM_LR, ADAM_B1, ADAM_B2, ADAM_EPS, ADAM_WD, ADAM_STEP = 0.001, 0.9, 0.999, 1e-08, 0.01, 10
EXP_CUT = -105.0
VMEM_LIMIT = 56 * 1024 * 1024
ROW_TILE = 256
SEQ_TILE = 256
ATTN_BLOCK = 256
MESH = pl.DeviceIdType.MESH


def _cp(*sem):
    return pltpu.CompilerParams(dimension_semantics=sem, vmem_limit_bytes=VMEM_LIMIT)


def _dot(a, b):
    return jnp.dot(a, b, preferred_element_type=F32)


def _dot_nt(a, b):
    return lax.dot_general(a, b, (((1,), (1,)), ((), ())), preferred_element_type=F32)


def _dot_tn(a, b):
    return lax.dot_general(a, b, (((0,), (0,)), ((), ())), preferred_element_type=F32)


def _rstd(v):
    return lax.rsqrt(jnp.mean(v * v, axis=-1, keepdims=True) + RMS_EPS)


def _rms_bwd(dn, vh, r, gain):
    dvh = dn * gain
    dv = r * (dvh - vh * jnp.mean(dvh * vh, axis=-1, keepdims=True))
    return dv, jnp.sum(dn * vh, axis=0, keepdims=True)


def _log_sigmoid(z):
    return jnp.minimum(z, 0.0) - jnp.log(1.0 + jnp.exp(-jnp.abs(z)))


def _expm1(v):
    small = v * (1.0 + v * (0.5 + v * (1.0 / 6.0 + v * (1.0 / 24.0 + v * (1.0 / 120.0)))))
    return jnp.where(jnp.abs(v) < 0.04, small, jnp.exp(v) - 1.0)


_GELU_C = math.sqrt(2.0 / math.pi)


def _gelu(v):
    return 0.5 * v * (1.0 + jnp.tanh(_GELU_C * (v + 0.044715 * v * v * v)))


def _gelu_grad(v):
    th = jnp.tanh(_GELU_C * (v + 0.044715 * v * v * v))
    return 0.5 * (1.0 + th) + 0.5 * v * (1.0 - th * th) * _GELU_C * (1.0 + 3.0 * 0.044715 * v * v)


def _row_spec(tm, d):
    return pl.BlockSpec((tm, d), lambda i: (i, 0))


def _vec_spec(d):
    return pl.BlockSpec((1, d), lambda i: (0, 0))


_ANY = pl.BlockSpec(memory_space=pl.ANY)


def _place():
    x, y, c = lax.axis_index("x"), lax.axis_index("y"), lax.axis_index("c")
    return x, y, c, [(1 - x, y), (x, 1 - y), (1 - x, 1 - y)]


def _remote(src, dst, send_sem, recv_sem, to):
    return pltpu.make_async_remote_copy(src_ref=src, dst_ref=dst, send_sem=send_sem, recv_sem=recv_sem,
                                        device_id=to, device_id_type=MESH)


class _Carrier:
    def __init__(self):
        self.inputs, self.out_shapes, self.aliases, self.ops, self.n_sems, self.results = [], [], {}, [], 0, None

    def inplace(self, arr):
        self.aliases[len(self.inputs)] = len(self.out_shapes)
        self.inputs.append(arr)
        self.out_shapes.append(S(arr.shape, arr.dtype))
        return len(self.out_shapes) - 1

    def read(self, arr):
        self.inputs.append(arr)
        return len(self.inputs) - 1

    def fresh(self, shape, dtype):
        self.out_shapes.append(S(shape, dtype))
        return len(self.out_shapes) - 1

    def _add(self, n_sems, copies):
        base = self.n_sems
        self.n_sems += n_sems

        def start(ins, outs, send, recv):
            for k, (src, dst, _, to) in enumerate(copies(ins, outs)):
                _remote(src, dst, send.at[base + k], recv.at[base + k], to).start()

        def finish(ins, outs, send, recv):
            for k, (src, _, land, to) in enumerate(copies(ins, outs)):
                _remote(src, land, send.at[base + k], recv.at[base + k], to).wait()

        self.ops.append((start, finish))

    def gather_ici(self, o, rows=None, split=True):
        half = self.out_shapes[o].shape[1] // 2
        lo, size = rows or (0, half)

        def copies(ins, outs):
            x, y, c, chips = _place()
            part = (lambda ref: ref.at[pl.ds(c * half + lo, size)]) if split else (lambda ref: ref)
            mine = part(outs[o].at[2 * x + y])
            return [(mine, mine, part(outs[o].at[2 * px + py]), (px, py, c)) for px, py in chips]

        self._add(3, copies)

    def gather_d2d(self, o, rows=None):
        half = self.out_shapes[o].shape[1] // 2
        lo, size = rows or (0, half)

        def copies(ins, outs):
            x, y, c, chips = _place()
            at = lambda k, cc: outs[o].at[k].at[pl.ds(cc * half + lo, size)]
            return [(at(2 * px + py, c), at(2 * px + py, c), at(2 * px + py, 1 - c), (x, y, 1 - c)) for px, py in chips]

        self._add(3, copies)

    def swap(self, i, o):
        half = self.inputs[i].shape[1] // 2

        def copies(ins, outs):
            x, y, c, _ = _place()
            return [(ins[i].at[:, pl.ds((1 - c) * half, half)], outs[o], outs[o], (x, y, 1 - c))]

        self._add(1, copies)

    def scatter(self, i, o, rows=None):
        lo, size = rows or (0, self.inputs[i].shape[1])

        def copies(ins, outs):
            x, y, c, chips = _place()
            cut = lambda ref: ref.at[pl.ds(lo, size)]
            return [(cut(ins[i].at[2 * px + py]), cut(outs[o].at[2 * x + y]), cut(outs[o].at[2 * px + py]), (px, py, c)) for px, py in chips]

        self._add(3, copies)

    def share(self, o):
        r = self.out_shapes[o].shape[0] // 2

        def copies(ins, outs):
            x, y, c, _ = _place()
            mine = outs[o].at[pl.ds(c * r, r)]
            return [(mine, mine, outs[o].at[pl.ds((1 - c) * r, r)], (x, y, 1 - c))]

        self._add(1, copies)

    def spread(self, i, o):
        def copies(ins, outs):
            x, y, c, _ = _place()
            me = 4 * x + 2 * y + c
            out = []
            for d in range(1, 8):
                to, frm = (me + d) % 8, (me + 8 - d) % 8
                out.append((ins[i], outs[o].at[me], outs[o].at[frm], (to // 4, (to // 2) % 2, to % 2)))
            return out

        self._add(7, copies)

    def _pallas(self, body, n_in, n_out, scratch, **kw):
        k_in, k_out = len(self.inputs), len(self.out_shapes)
        grid = kw.get("grid", ())

        def wrapped(*refs):
            ins, cins = refs[:n_in], refs[n_in:n_in + k_in]
            outs = refs[n_in + k_in:n_in + k_in + n_out]
            couts = refs[n_in + k_in + n_out:n_in + k_in + n_out + k_out]
            own = refs[n_in + k_in + n_out + k_out:]
            send, recv = own[len(scratch):]
            ids = [pl.program_id(a) for a in range(len(grid))]
            first = functools.reduce(jnp.logical_and, [a == 0 for a in ids], True)
            last = functools.reduce(jnp.logical_and, [a == g - 1 for a, g in zip(ids, grid)], True)

            def go(stage):
                for op in self.ops:
                    op[stage](cins, couts, send, recv)

            if grid:
                pl.when(first)(lambda: go(0))
                body(*ins, *outs, *own[:len(scratch)])
                pl.when(last)(lambda: go(1))
            else:
                go(0)
                go(1)

        sem = pltpu.SemaphoreType.DMA((self.n_sems,))
        return pl.pallas_call(
            wrapped, in_specs=list(kw.get("in_specs", [])) + [_ANY] * k_in, out_specs=list(kw.get("out_specs", [])) + [_ANY] * k_out,
            out_shape=list(kw.get("out_shape", [])) + self.out_shapes, scratch_shapes=list(scratch) + [sem, sem],
            input_output_aliases={n_in + i: n_out + o for i, o in self.aliases.items()}, name=kw["name"],
            **({"grid": grid, "compiler_params": _cp(*["arbitrary"] * len(grid))} if grid else {}))

    def run(self, body, kw, *args):
        single = not isinstance(kw["out_shape"], (list, tuple))
        out_shape = [kw["out_shape"]] if single else list(kw["out_shape"])
        out_specs = [kw["out_specs"]] if single else list(kw["out_specs"])
        res = self._pallas(body, len(args), len(out_shape), kw.get("scratch_shapes", []), grid=kw["grid"], in_specs=kw["in_specs"],
                           out_specs=out_specs, out_shape=out_shape, name=kw["name"])(*args, *self.inputs)
        self.results = list(res[len(out_shape):])
        return res[0] if single else list(res[:len(out_shape)])

    def run_alone(self, name):
        self.results = list(self._pallas(None, 0, 0, [], name=name)(*self.inputs))


def _call(comm, body, **kw):
    if comm is None:
        return pl.pallas_call(body, **kw)
    return functools.partial(comm.run, body, kw)


def _mm_nn(a, b3, *, bm, bn, bk, out_dtype, name, comm=None):
    m, k = a.shape
    c, _, n = b3.shape
    ni, nj, nk = m // bm, n // bn, k // bk

    def body(a_ref, b_ref, o_ref, *acc):
        if nk == 1:
            o_ref[...] = _dot(a_ref[...], b_ref[...]).astype(o_ref.dtype)
            return
        kk = pl.program_id(3)

        @pl.when(kk == 0)
        def _():
            acc[0][...] = jnp.zeros_like(acc[0])

        acc[0][...] += _dot(a_ref[...], b_ref[...])

        @pl.when(kk == nk - 1)
        def _():
            o_ref[...] = acc[0][...].astype(o_ref.dtype)

    return _call(
        comm, body, grid=(c, nj, ni, nk),
        in_specs=[pl.BlockSpec((bm, bk), lambda cc, j, i, kk: (i, kk)),
                  pl.BlockSpec((None, bk, bn), lambda cc, j, i, kk: (cc, kk, j))],
        out_specs=pl.BlockSpec((bm, bn), lambda cc, j, i, kk: (i, cc * nj + j)),
        out_shape=S((m, c * n), out_dtype),
        scratch_shapes=[] if nk == 1 else [pltpu.VMEM((bm, bn), F32)],
        compiler_params=_cp("parallel", "parallel", "parallel", "arbitrary"), name=name)(a, b3)


def _mm_nt(a, b3, *, bm, bo, out_dtype, name, comm=None):
    m = a.shape[0]
    c, ko, n = b3.shape
    ni, nj = m // bm, ko // bo

    def body(a_ref, b_ref, o_ref):
        acc = _dot_nt(a_ref[:, 0:n], b_ref[0])
        for cc in range(1, c):
            acc = acc + _dot_nt(a_ref[:, cc * n:(cc + 1) * n], b_ref[cc])
        o_ref[...] = acc.astype(o_ref.dtype)

    return _call(
        comm, body, grid=(nj, ni),
        in_specs=[pl.BlockSpec((bm, c * n), lambda j, i: (i, 0)),
                  pl.BlockSpec((c, bo, n), lambda j, i: (0, j, 0))],
        out_specs=pl.BlockSpec((bm, bo), lambda j, i: (i, j)),
        out_shape=S((m, ko), out_dtype),
        compiler_params=_cp("parallel", "parallel"), name=name)(a, b3)


def _mm_tn(a, b, c, *, bm, bk, name, comm=None):
    m, k = a.shape
    n = b.shape[1] // c
    nm, nk = m // bm, k // bk

    def body(a_ref, b_ref, o_ref, acc):
        mm = pl.program_id(2)

        @pl.when(mm == 0)
        def _():
            acc[...] = jnp.zeros_like(acc)

        acc[...] += _dot_tn(a_ref[...], b_ref[...])

        @pl.when(mm == nm - 1)
        def _():
            o_ref[...] = acc[...]

    return _call(
        comm, body, grid=(c, nk, nm),
        in_specs=[pl.BlockSpec((bm, bk), lambda cc, j, mm: (mm, j)),
                  pl.BlockSpec((bm, n), lambda cc, j, mm: (mm, cc))],
        out_specs=pl.BlockSpec((None, bk, n), lambda cc, j, mm: (cc, j, 0)),
        out_shape=S((c, k, n), F32),
        scratch_shapes=[pltpu.VMEM((bk, n), F32)],
        compiler_params=_cp("parallel", "parallel", "arbitrary"), name=name)(a, b)


def _swiglu_fwd(hn, wg3, wu3, *, bm, name, comm=None):
    m, k = hn.shape
    c, _, n = wg3.shape

    def body(a_ref, g_ref, u_ref, gate_ref, up_ref, act_ref):
        a = a_ref[...]
        gate = _dot(a, g_ref[...])
        up = _dot(a, u_ref[...])
        gate_ref[...] = gate
        up_ref[...] = up
        act_ref[...] = (gate * jax.nn.sigmoid(gate) * up).astype(act_ref.dtype)

    wspec = pl.BlockSpec((None, k, n), lambda cc, i: (cc, 0, 0))
    ospec = pl.BlockSpec((bm, n), lambda cc, i: (i, cc))
    return _call(
        comm, body, grid=(c, m // bm),
        in_specs=[pl.BlockSpec((bm, k), lambda cc, i: (i, 0)), wspec, wspec],
        out_specs=[ospec, ospec, ospec],
        out_shape=[S((m, c * n), F32), S((m, c * n), F32), S((m, c * n), _MXU)],
        compiler_params=_cp("parallel", "parallel"), name=name)(hn, wg3, wu3)


def _swiglu_bwd(df, wd, gate, up, *, bm, bo, name):
    m, k = df.shape
    ko = wd.shape[0]

    def body(a_ref, b_ref, g_ref, u_ref, dg_ref, du_ref):
        dact = _dot_nt(a_ref[...], b_ref[...])
        gate = g_ref[...]
        sg = jax.nn.sigmoid(gate)
        dg_ref[...] = (dact * u_ref[...] * (sg * (1.0 + gate * (1.0 - sg)))).astype(dg_ref.dtype)
        du_ref[...] = (dact * (gate * sg)).astype(du_ref.dtype)

    ospec = pl.BlockSpec((bm, bo), lambda j, i: (i, j))
    return pl.pallas_call(
        body, grid=(ko // bo, m // bm),
        in_specs=[pl.BlockSpec((bm, k), lambda j, i: (i, 0)), pl.BlockSpec((bo, k), lambda j, i: (j, 0)), ospec, ospec],
        out_specs=[ospec, ospec],
        out_shape=[S((m, ko), _MXU), S((m, ko), _MXU)],
        compiler_params=_cp("parallel", "parallel"), name=name)(df, wd, gate, up)


def _rms_fwd(x, gain, name, comm=None):
    t, d = x.shape
    tm = min(t, ROW_TILE)

    def body(x_ref, g_ref, o_ref):
        xv = x_ref[...]
        o_ref[...] = ((xv * _rstd(xv)) * g_ref[...]).astype(o_ref.dtype)

    return _call(comm, body, grid=(t // tm,), in_specs=[_row_spec(tm, d), _vec_spec(d)], out_specs=_row_spec(tm, d),
                          out_shape=S((t, d), _MXU), compiler_params=_cp("parallel"), name=name)(x, gain)


def _outnorm_fwd(o, yl, ga, gl, name):
    t, w = o.shape
    tm = min(t, ROW_TILE)

    def body(o_ref, l_ref, ga_ref, gl_ref, y_ref):
        ov, lv = o_ref[...], l_ref[...]
        y_ref[:, :w] = ((ov * _rstd(ov)) * ga_ref[...]).astype(y_ref.dtype)
        y_ref[:, w:] = ((lv * _rstd(lv)) * gl_ref[...]).astype(y_ref.dtype)

    return pl.pallas_call(body, grid=(t // tm,), in_specs=[_row_spec(tm, w), _row_spec(tm, w), _vec_spec(w), _vec_spec(w)],
                          out_specs=_row_spec(tm, 2 * w), out_shape=S((t, 2 * w), _MXU),
                          compiler_params=_cp("parallel"), name=name)(o, yl, ga, gl)


def _mid_fwd(x, mix, g_post, g_pre, name, comm=None):
    t, d = x.shape
    tm = min(t, ROW_TILE)

    def body(x_ref, m_ref, gp_ref, gn_ref, x2_ref, hn_ref):
        mv = m_ref[...]
        x2 = x_ref[...] + (mv * _rstd(mv)) * gp_ref[...]
        x2_ref[...] = x2
        hn_ref[...] = ((x2 * _rstd(x2)) * gn_ref[...]).astype(hn_ref.dtype)

    return _call(comm, body, grid=(t // tm,), in_specs=[_row_spec(tm, d), _row_spec(tm, d), _vec_spec(d), _vec_spec(d)],
                          out_specs=[_row_spec(tm, d), _row_spec(tm, d)], out_shape=[S((t, d), F32), S((t, d), _MXU)],
                          compiler_params=_cp("parallel"), name=name)(x, mix, g_post, g_pre)


def _final(f, x2, target, g_post, name):
    t, d = f.shape
    tm = min(t, ROW_TILE)

    def body(f_ref, x2_ref, t_ref, g_ref, loss_ref, dout_ref, df_ref, dg_ref):
        @pl.when(pl.program_id(0) == 0)
        def _():
            loss_ref[...] = jnp.zeros_like(loss_ref)
            dg_ref[...] = jnp.zeros_like(dg_ref)

        fv = f_ref[...]
        r = _rstd(fv)
        fh = fv * r
        err = (x2_ref[...] + fh * g_ref[...]) - t_ref[...]
        loss_ref[...] += jnp.sum(err * err, axis=0, keepdims=True)
        dout = err * (1.0 / d)
        dout_ref[...] = dout
        dfv, dg = _rms_bwd(dout, fh, r, g_ref[...])
        df_ref[...] = dfv.astype(df_ref.dtype)
        dg_ref[...] += dg

    return pl.pallas_call(
        body, grid=(t // tm,),
        in_specs=[_row_spec(tm, d), _row_spec(tm, d), _row_spec(tm, d), _vec_spec(d)],
        out_specs=[_vec_spec(d), _row_spec(tm, d), _row_spec(tm, d), _vec_spec(d)],
        out_shape=[S((1, d), F32), S((t, d), F32), S((t, d), _MXU), S((1, d), F32)],
        compiler_params=_cp("arbitrary"), name=name)(f, x2, target, g_post)


def _mid_bwd(dhn_a, dhn_b, dout, x2, mix, g_pre, g_post, name, comm=None):
    t, d = x2.shape
    tm = min(t, ROW_TILE)

    def body(da_ref, db_ref, do_ref, x2_ref, m_ref, gn_ref, gp_ref, dx2_ref, dm_ref, dgn_ref, dgp_ref):
        @pl.when(pl.program_id(0) == 0)
        def _():
            dgn_ref[...] = jnp.zeros_like(dgn_ref)
            dgp_ref[...] = jnp.zeros_like(dgp_ref)

        x2 = x2_ref[...]
        r = _rstd(x2)
        dxa, dgn = _rms_bwd(da_ref[...] + db_ref[...], x2 * r, r, gn_ref[...])
        dx2 = do_ref[...] + dxa
        dx2_ref[...] = dx2
        dgn_ref[...] += dgn
        mv = m_ref[...]
        rm = _rstd(mv)
        dmv, dgp = _rms_bwd(dx2, mv * rm, rm, gp_ref[...])
        dm_ref[...] = dmv.astype(dm_ref.dtype)
        dgp_ref[...] += dgp

    rs, vs = _row_spec(tm, d), _vec_spec(d)
    return _call(
        comm, body, grid=(t // tm,), in_specs=[rs, rs, rs, rs, rs, vs, vs], out_specs=[rs, rs, vs, vs],
        out_shape=[S((t, d), F32), S((t, d), _MXU), S((1, d), F32), S((1, d), F32)],
        compiler_params=_cp("arbitrary"), name=name)(dhn_a, dhn_b, dout, x2, mix, g_pre, g_post)


def _first_bwd(dhn, dx2, x, gain, name, comm=None):
    t, d = x.shape
    tm = min(t, ROW_TILE)

    def body(dh_ref, dx2_ref, x_ref, g_ref, dx_ref, dg_ref):
        @pl.when(pl.program_id(0) == 0)
        def _():
            dg_ref[...] = jnp.zeros_like(dg_ref)

        xv = x_ref[...]
        r = _rstd(xv)
        dxa, dg = _rms_bwd(dh_ref[...], xv * r, r, g_ref[...])
        dx_ref[...] = dx2_ref[...] + dxa
        dg_ref[...] += dg

    rs, vs = _row_spec(tm, d), _vec_spec(d)
    return _call(comm, body, grid=(t // tm,), in_specs=[rs, rs, rs, vs], out_specs=[rs, vs],
                          out_shape=[S((t, d), F32), S((1, d), F32)], compiler_params=_cp("arbitrary"), name=name)(dhn, dx2, x, gain)


def _outnorm_bwd(dy, o, yl, ga, gl, name, comm=None):
    t, w = o.shape
    tm = min(t, ROW_TILE)

    def body(dy_ref, o_ref, l_ref, ga_ref, gl_ref, do_ref, dl_ref, dga_ref, dgl_ref):
        @pl.when(pl.program_id(0) == 0)
        def _():
            dga_ref[...] = jnp.zeros_like(dga_ref)
            dgl_ref[...] = jnp.zeros_like(dgl_ref)

        ov, lv = o_ref[...], l_ref[...]
        ra, rl = _rstd(ov), _rstd(lv)
        dov, dga = _rms_bwd(dy_ref[:, :w], ov * ra, ra, ga_ref[...])
        dlv, dgl = _rms_bwd(dy_ref[:, w:], lv * rl, rl, gl_ref[...])
        do_ref[...] = dov
        dl_ref[...] = dlv
        dga_ref[...] += dga
        dgl_ref[...] += dgl

    rs, vs = _row_spec(tm, w), _vec_spec(w)
    return _call(comm, body, grid=(t // tm,), in_specs=[_row_spec(tm, 2 * w), rs, rs, vs, vs], out_specs=[rs, rs, vs, vs],
                          out_shape=[S((t, w), F32), S((t, w), F32), S((1, w), F32), S((1, w), F32)],
                          compiler_params=_cp("arbitrary"), name=name)(dy, o, yl, ga, gl)


def _split_dot(v, tri):
    hi = v.astype(_MXU)
    lo = (v - hi.astype(F32)).astype(_MXU)
    return _dot(hi, tri) + _dot(lo, tri)


def _attn_tile(qb, kb, row, col, shift, scale):
    z = _dot_nt(qb, kb) * scale
    mask = (col + shift) < row
    lb = _log_sigmoid(z)
    lm = jnp.where(mask, lb - z, 0.0)
    return mask, lb, lm


def _attn_fwd(proj, n_heads, name, comm=None):
    t = proj.shape[0]
    bq = min(t, ATTN_BLOCK)
    nq = t // bq
    scale = 1.0 / math.sqrt(HEAD_DIM)

    def body(q_ref, k_ref, v_ref, o_ref, kb_ref, vb_ref):
        kb_ref[...] = k_ref[...].astype(_MXU)
        vb_ref[...] = v_ref[...].astype(_MXU)
        row = lax.broadcasted_iota(jnp.int32, (bq, bq), 0)
        col = lax.broadcasted_iota(jnp.int32, (bq, bq), 1)
        tri = (row > col).astype(_MXU)

        def per_q(qi, _):
            q0 = pl.multiple_of(qi * bq, bq)
            qb = q_ref[pl.ds(q0, bq), :].astype(_MXU)

            def cond(st):
                return jnp.logical_and(st[0] >= 0, st[1])

            def step(st):
                kj, _, carry, acc = st
                k0 = pl.multiple_of(kj * bq, bq)
                mask, lb, lm = _attn_tile(qb, kb_ref[pl.ds(k0, bq), :], row, col, (kj - qi) * bq, scale)
                w = jnp.where(mask, jnp.exp(lb + _split_dot(lm, tri) + carry), 0.0)
                acc = acc + _dot(w.astype(_MXU), vb_ref[pl.ds(k0, bq), :])
                carry = carry + jnp.sum(lm, axis=1, keepdims=True)
                return kj - 1, jnp.max(carry) > EXP_CUT, carry, acc

            st = lax.while_loop(cond, step, (qi, jnp.bool_(True), jnp.zeros((bq, 1), F32), jnp.zeros((bq, HEAD_DIM), F32)))
            o_ref[pl.ds(q0, bq), :] = st[3]
            return 0

        lax.fori_loop(0, nq, per_q, 0)

    hs = lambda off: pl.BlockSpec((t, HEAD_DIM), lambda h: (0, off + h))
    return _call(
        comm, body, grid=(n_heads,), in_specs=[hs(0), hs(n_heads), hs(2 * n_heads)], out_specs=hs(0),
        out_shape=S((t, n_heads * HEAD_DIM), F32),
        scratch_shapes=[pltpu.VMEM((t, HEAD_DIM), _MXU), pltpu.VMEM((t, HEAD_DIM), _MXU)],
        compiler_params=_cp("parallel"), name=name)(proj, proj, proj)


def _attn_bwd(proj, do, n_heads, name, comm=None):
    t = proj.shape[0]
    bq = min(t, ATTN_BLOCK)
    nq = t // bq
    scale = 1.0 / math.sqrt(HEAD_DIM)

    def body(q_ref, k_ref, v_ref, do_ref, dq_ref, dk_ref, dv_ref, kb_ref, vb_ref, dka_ref, dva_ref, g_ref, b_ref):
        kb_ref[...] = k_ref[...].astype(_MXU)
        vb_ref[...] = v_ref[...].astype(_MXU)
        dka_ref[...] = jnp.zeros_like(dka_ref)
        dva_ref[...] = jnp.zeros_like(dva_ref)
        row = lax.broadcasted_iota(jnp.int32, (bq, bq), 0)
        col = lax.broadcasted_iota(jnp.int32, (bq, bq), 1)
        tri = (row > col).astype(_MXU)
        tri_lt = (row < col).astype(_MXU)

        def per_q(qi, _):
            q0 = pl.multiple_of(qi * bq, bq)
            qb = q_ref[pl.ds(q0, bq), :].astype(_MXU)
            dob = do_ref[pl.ds(q0, bq), :].astype(_MXU)

            def cond(st):
                return jnp.logical_and(st[0] >= 0, st[1])

            def step(st):
                kj, _, carry = st
                k0 = pl.multiple_of(kj * bq, bq)
                mask, lb, lm = _attn_tile(qb, kb_ref[pl.ds(k0, bq), :], row, col, (kj - qi) * bq, scale)
                w = jnp.where(mask, jnp.exp(lb + _split_dot(lm, tri) + carry), 0.0)
                g_ref[pl.ds(k0, bq), :] = w * _dot_nt(dob, vb_ref[pl.ds(k0, bq), :])
                b_ref[pl.ds(k0, bq), :] = jnp.where(mask, jnp.exp(lb), 0.0)
                dva_ref[pl.ds(k0, bq), :] += _dot_tn(w.astype(_MXU), dob)
                carry = carry + jnp.sum(lm, axis=1, keepdims=True)
                return kj - 1, jnp.max(carry) > EXP_CUT, carry

            st = lax.while_loop(cond, step, (qi, jnp.bool_(True), jnp.zeros((bq, 1), F32)))

            def back(kj, st2):
                before, dq = st2
                k0 = pl.multiple_of(kj * bq, bq)
                kb = kb_ref[pl.ds(k0, bq), :]
                g = g_ref[pl.ds(k0, bq), :]
                beta = b_ref[pl.ds(k0, bq), :]
                dz = ((g * (1.0 - beta) - (before + _split_dot(g, tri_lt)) * beta) * scale).astype(_MXU)
                dka_ref[pl.ds(k0, bq), :] += _dot_tn(dz, qb)
                return before + jnp.sum(g, axis=1, keepdims=True), dq + _dot(dz, kb)

            st2 = lax.fori_loop(st[0] + 1, qi + 1, back, (jnp.zeros((bq, 1), F32), jnp.zeros((bq, HEAD_DIM), F32)))
            dq_ref[pl.ds(q0, bq), :] = st2[1].astype(dq_ref.dtype)
            return 0

        lax.fori_loop(0, nq, per_q, 0)
        dk_ref[...] = dka_ref[...].astype(dk_ref.dtype)
        dv_ref[...] = dva_ref[...].astype(dv_ref.dtype)

    hs = lambda off: pl.BlockSpec((t, HEAD_DIM), lambda h: (0, off + h))
    w = n_heads * HEAD_DIM
    return _call(
        comm, body, grid=(n_heads,), in_specs=[hs(0), hs(n_heads), hs(2 * n_heads), hs(0)], out_specs=[hs(0), hs(0), hs(0)],
        out_shape=[S((t, w), _MXU)] * 3,
        scratch_shapes=[pltpu.VMEM((t, HEAD_DIM), _MXU), pltpu.VMEM((t, HEAD_DIM), _MXU), pltpu.VMEM((t, HEAD_DIM), F32),
                        pltpu.VMEM((t, HEAD_DIM), F32), pltpu.VMEM((t, bq), F32), pltpu.VMEM((t, bq), F32)],
        compiler_params=_cp("parallel"), name=name)(proj, proj, proj, do)


def _shift_down(cur, prev8, k):
    if k == 0:
        return cur
    row8 = lax.broadcasted_iota(jnp.int32, prev8.shape, 0)
    rc = pltpu.roll(cur, k, 0)
    top = jnp.where(row8 < k, pltpu.roll(prev8, k, 0), rc[0:8, :])
    return jnp.concatenate([top, rc[8:, :]], axis=0)


def _shift_up(cur, next8, k):
    if k == 0:
        return cur
    n = cur.shape[0]
    row8 = lax.broadcasted_iota(jnp.int32, next8.shape, 0)
    rc = pltpu.roll(cur, n - k, 0)
    bottom = jnp.where(row8 >= 8 - k, pltpu.roll(next8, 8 - k, 0), rc[n - 8:, :])
    return jnp.concatenate([rc[:n - 8, :], bottom], axis=0)


def _lru_gates(xl, prev8, cw, cb, wr, br, wi, bi, ls):
    xs = [_shift_down(xl, prev8, CONV_WIDTH - 1 - k) for k in range(CONV_WIDTH)]
    xc = xs[0] * cw[0:1, :]
    for k in range(1, CONV_WIDTH):
        xc = xc + xs[k] * cw[k:k + 1, :]
    xc = xc + cb
    xcb = xc.astype(_MXU)
    r = jax.nn.sigmoid(_dot(xcb, wr) + br)
    i = jax.nn.sigmoid(_dot(xcb, wi) + bi)
    la = (LRU_C * r) * ls
    a = jnp.exp(la)
    mult = jnp.sqrt(-_expm1(2.0 * la))
    return xs, xc, r, i, a, mult


def _group_scan(a, b, reverse):
    n = a.shape[0]
    row = lax.broadcasted_iota(jnp.int32, a.shape, 0) % 8
    for d in (1, 2, 4):
        if reverse:
            m = row < 8 - d
            a_s, b_s = pltpu.roll(a, n - d, 0), pltpu.roll(b, n - d, 0)
        else:
            m = row >= d
            a_s, b_s = pltpu.roll(a, d, 0), pltpu.roll(b, d, 0)
        b = jnp.where(m, a * b_s + b, b)
        a = jnp.where(m, a * a_s, a)
    return a, b


def _lru_fwd(proj, col0, n_blocks, cw, cb, wr, br, wi, bi, lam, name, comm=None):
    t = proj.shape[0]
    tt = min(t, SEQ_TILE)
    nt = t // tt

    def body(xl_ref, gl_ref, cw_ref, cb_ref, wr_ref, br_ref, wi_ref, bi_ref, lam_ref, h_ref, y_ref):
        cwv, cbv, brv, biv = cw_ref[...], cb_ref[...], br_ref[...], bi_ref[...]
        wrv, wiv = wr_ref[...].astype(_MXU), wi_ref[...].astype(_MXU)
        ls = _log_sigmoid(lam_ref[...])

        def tile(ti, hin):
            t0 = pl.multiple_of(ti * tt, tt)
            p0 = pl.multiple_of(jnp.maximum(t0 - 8, 0), 8)
            prev8 = xl_ref[pl.ds(p0, 8), :] * (ti > 0).astype(F32)
            xl = xl_ref[pl.ds(t0, tt), :]
            _, xc, _, ig, a, mult = _lru_gates(xl, prev8, cwv, cbv, wrv, brv, wiv, biv, ls)
            ga, gb = _group_scan(a, mult * (ig * xc), False)
            for g in range(tt // 8):
                hg = ga[8 * g:8 * g + 8, :] * hin + gb[8 * g:8 * g + 8, :]
                h_ref[pl.ds(t0 + 8 * g, 8), :] = hg
                hin = hg[7:8, :]
            y_ref[pl.ds(t0, tt), :] = h_ref[pl.ds(t0, tt), :] * _gelu(gl_ref[pl.ds(t0, tt), :])
            return hin

        lax.fori_loop(0, nt, tile, jnp.zeros((1, HEAD_DIM), F32))

    cs = lambda off: pl.BlockSpec((t, HEAD_DIM), lambda n: (0, off + n))
    vs = pl.BlockSpec((1, HEAD_DIM), lambda n: (0, n))
    ws = pl.BlockSpec((None, HEAD_DIM, HEAD_DIM), lambda n: (n, 0, 0))
    w = n_blocks * HEAD_DIM
    return _call(
        comm, body, grid=(n_blocks,),
        in_specs=[cs(col0), cs(col0 + n_blocks), pl.BlockSpec((CONV_WIDTH, HEAD_DIM), lambda n: (0, n)), vs, ws, vs, ws, vs, vs],
        out_specs=[cs(0), cs(0)], out_shape=[S((t, w), F32), S((t, w), F32)],
        compiler_params=_cp("parallel"), name=name)(proj, proj, cw, cb, wr, br, wi, bi, lam)


def _lru_bwd(proj, col0, n_blocks, h, dyl, cw, cb, wr, br, wi, bi, lam, name, comm=None):
    t = proj.shape[0]
    tt = min(t, SEQ_TILE)
    nt = t // tt

    def body(xl_ref, gl_ref, h_ref, dy_ref, cw_ref, cb_ref, wr_ref, br_ref, wi_ref, bi_ref, lam_ref,
             dxl_ref, dgl_ref, dcw_ref, dcb_ref, dwr_ref, dbr_ref, dwi_ref, dbi_ref, dlam_ref, g_ref):
        cwv, cbv, brv, biv = cw_ref[...], cb_ref[...], br_ref[...], bi_ref[...]
        wrv, wiv = wr_ref[...].astype(_MXU), wi_ref[...].astype(_MXU)
        lamv = lam_ref[...]
        ls = _log_sigmoid(lamv)
        for ref in (dcw_ref, dcb_ref, dwr_ref, dbr_ref, dwi_ref, dbi_ref, dlam_ref):
            ref[...] = jnp.zeros_like(ref)

        def tile(s, carry):
            e_in, dxc_next8 = carry
            ti = nt - 1 - s
            t0 = pl.multiple_of(ti * tt, tt)
            p0 = pl.multiple_of(jnp.maximum(t0 - 8, 0), 8)
            first = (ti > 0).astype(F32)
            xl = xl_ref[pl.ds(t0, tt), :]
            xs, xc, r, ig, a, mult = _lru_gates(xl, xl_ref[pl.ds(p0, 8), :] * first, cwv, cbv, wrv, brv, wiv, biv, ls)
            hv = h_ref[pl.ds(t0, tt), :]
            h_before = _shift_down(hv, h_ref[pl.ds(p0, 8), :] * first, 1)
            glv = gl_ref[pl.ds(t0, tt), :]
            dyv = dy_ref[pl.ds(t0, tt), :]
            dgl_ref[pl.ds(t0, tt), :] = (dyv * hv * _gelu_grad(glv)).astype(dgl_ref.dtype)
            dh = dyv * _gelu(glv)
            row = lax.broadcasted_iota(jnp.int32, a.shape, 0)
            coef = jnp.where(row == tt - 1, 1.0, pltpu.roll(a, tt - 1, 0))
            ga, gb = _group_scan(coef, dh, True)
            gin = e_in
            for g in reversed(range(tt // 8)):
                gg = ga[8 * g:8 * g + 8, :] * gin + gb[8 * g:8 * g + 8, :]
                g_ref[8 * g:8 * g + 8, :] = gg
                gin = gg[0:1, :]
            gv = g_ref[...]
            e_out = a[0:1, :] * gv[0:1, :]
            ix = ig * xc
            dla = (gv * h_before) * a - (gv * ix) * (a * a / mult)
            dlam_ref[...] += jnp.sum(dla * (LRU_C * r), axis=0, keepdims=True)
            dpr = (dla * (LRU_C * ls)) * (r * (1.0 - r))
            dpi = (gv * mult * xc) * (ig * (1.0 - ig))
            dbr_ref[...] += jnp.sum(dpr, axis=0, keepdims=True)
            dbi_ref[...] += jnp.sum(dpi, axis=0, keepdims=True)
            xcb, dprb, dpib = xc.astype(_MXU), dpr.astype(_MXU), dpi.astype(_MXU)
            dwr_ref[...] += _dot_tn(xcb, dprb)
            dwi_ref[...] += _dot_tn(xcb, dpib)
            dxc = gv * mult * ig + _dot_nt(dprb, wrv) + _dot_nt(dpib, wiv)
            dcb_ref[...] += jnp.sum(dxc, axis=0, keepdims=True)
            dxl = None
            for k in range(CONV_WIDTH):
                dcw_ref[k:k + 1, :] += jnp.sum(dxc * xs[k], axis=0, keepdims=True)
                term = _shift_up(dxc, dxc_next8, CONV_WIDTH - 1 - k) * cwv[k:k + 1, :]
                dxl = term if dxl is None else dxl + term
            dxl_ref[pl.ds(t0, tt), :] = dxl.astype(dxl_ref.dtype)
            return e_out, dxc[0:8, :]

        lax.fori_loop(0, nt, tile, (jnp.zeros((1, HEAD_DIM), F32), jnp.zeros((8, HEAD_DIM), F32)))
        dlam_ref[...] = dlam_ref[...] * (1.0 - jax.nn.sigmoid(lamv))

    cs = lambda off: pl.BlockSpec((t, HEAD_DIM), lambda n: (0, off + n))
    vs = pl.BlockSpec((1, HEAD_DIM), lambda n: (0, n))
    ws = pl.BlockSpec((None, HEAD_DIM, HEAD_DIM), lambda n: (n, 0, 0))
    cws = pl.BlockSpec((CONV_WIDTH, HEAD_DIM), lambda n: (0, n))
    w = n_blocks * HEAD_DIM
    vec = S((1, w), F32)
    mat = S((n_blocks, HEAD_DIM, HEAD_DIM), F32)
    return _call(
        comm, body, grid=(n_blocks,),
        in_specs=[cs(col0), cs(col0 + n_blocks), cs(0), cs(0), cws, vs, ws, vs, ws, vs, vs],
        out_specs=[cs(0), cs(0), cws, vs, ws, vs, ws, vs, vs],
        out_shape=[S((t, w), _MXU), S((t, w), _MXU), S((CONV_WIDTH, w), F32), vec, mat, vec, mat, vec, vec],
        scratch_shapes=[pltpu.VMEM((tt, HEAD_DIM), F32)],
        compiler_params=_cp("parallel"), name=name)(proj, proj, h, dyl, cw, cb, wr, br, wi, bi, lam)


class _NoExchange:
    def __init__(self, weights):
        self.weights, self.grads, self.packs = weights, {}, {}

    def weight(self, name):
        return self.weights[name]

    def carrier(self, call):
        return None

    def harvest(self, car):
        pass

    def alone(self, call):
        pass


def _local_step(x, target, norms, ex, cw, cb, wr, br, wi, bi, lam, ga, gl):
    g_pre_mix, g_post_mix, g_pre_ffn, g_post_ffn = norms
    t, d = x.shape
    bm = min(t, 512)
    bt = min(t, 2048)

    def run(fn, name, *args, **kw):
        car = ex.carrier(name)
        out = fn(*args, name=name, comm=car, **kw)
        ex.harvest(car)
        return out

    hn1 = run(_rms_fwd, "rms1", x, g_pre_mix)
    win3 = ex.weight("w_in")
    c = win3.shape[0]
    proj = run(_mm_nn, "in_proj", hn1, win3, bm=bm, bn=win3.shape[2], bk=d, out_dtype=F32)
    o = run(_attn_fwd, "attn_fwd", proj, (proj.shape[1] - d) // 3 // HEAD_DIM)
    mix = 2 * o.shape[1]
    n_heads = n_blocks = o.shape[1] // HEAD_DIM
    h, yl = run(_lru_fwd, "lru_fwd", proj, 3 * n_heads, n_blocks, cw, cb, wr, br, wi, bi, lam)
    y = _outnorm_fwd(o, yl, ga, gl, "outnorm_fwd")
    wout = ex.weight("w_out")
    mixo = run(_mm_nn, "out_proj", y, wout[None], bm=bm, bn=d, bk=mix, out_dtype=F32)
    x2, hn2 = run(_mid_fwd, "mid_fwd", x, mixo, g_post_mix, g_pre_ffn)
    wg3, wu3 = ex.weight("w_ffn_gate"), ex.weight("w_ffn_up")
    gate, up, act = run(_swiglu_fwd, "ffn_gate_up", hn2, wg3, wu3, bm=min(t, 256))
    ex.alone("gather_w_down")
    wd = ex.weight("w_ffn_down")
    ff = wd.shape[0]
    f = _mm_nn(act, wd[None], bm=bm, bn=d // 2, bk=ff, out_dtype=F32, name="ffn_down")
    loss_cols, dout, df, dg_post_ffn = _final(f, x2, target, g_post_ffn, "final")

    dgate, dup = _swiglu_bwd(df, wd, gate, up, bm=bm, bo=ff // 4, name="ffn_down_bwd")
    ex.grads["w_ffn_down"] = _mm_tn(act, df, 1, bm=bt, bk=512, name="ffn_down_dw").reshape(c, ff // c, d)
    ex.grads["w_ffn_gate"] = run(_mm_tn, "ffn_gate_dw", hn2, dgate, c, bm=bt, bk=d // 2)
    ex.grads["w_ffn_up"] = run(_mm_tn, "ffn_up_dw", hn2, dup, c, bm=bt, bk=d // 2)
    dhn2_g = run(_mm_nt, "ffn_gate_dx", dgate, wg3, bm=bm, bo=d // 2, out_dtype=F32)
    dhn2_u = run(_mm_nt, "ffn_up_dx", dup, wu3, bm=bm, bo=d // 2, out_dtype=F32)
    dx2, dmix, dg_pre_ffn, dg_post_mix = run(_mid_bwd, "mid_bwd", dhn2_g, dhn2_u, dout, x2, mixo, g_pre_ffn, g_post_mix)
    dy = _mm_nt(dmix, wout[None], bm=bm, bo=mix, out_dtype=F32, name="out_proj_dx")
    ex.grads["w_out"] = _mm_tn(y, dmix, 1, bm=bt, bk=mix // 4, name="out_proj_dw").reshape(c, mix // c, d)
    do, dyl, dga, dgl_norm = run(_outnorm_bwd, "outnorm_bwd", dy, o, yl, ga, gl)
    dxl, dglu, dcw, dcb, dwr, dbr, dwi, dbi, dlam = run(_lru_bwd, "lru_bwd", proj, 3 * n_heads, n_blocks, h, dyl, cw, cb, wr, br, wi, bi, lam)
    small = dict(post_mix_norm=dg_post_mix, pre_ffn_norm=dg_pre_ffn, post_ffn_norm=dg_post_ffn, conv_w=dcw, conv_b=dcb,
                 w_rgate=dwr, b_rgate=dbr, w_igate=dwi, b_igate=dbi, lru_lambda=dlam, attn_out_norm=dga, lru_out_norm=dgl_norm)
    ex.packs["early"] = _pack([small[n] for n in _SMALL_EARLY])
    dq, dk, dv = run(_attn_bwd, "attn_bwd", proj, do, n_heads)
    dproj = jnp.concatenate([dq, dk, dv, dxl, dglu], axis=1)
    ex.grads["w_in"] = _mm_tn(hn1, dproj, c, bm=bt, bk=d // 2, name="in_proj_dw")
    dhn1 = run(_mm_nt, "in_proj_dx", dproj, win3, bm=bm, bo=d // 2, out_dtype=F32)
    grad_x, small["pre_mix_norm"] = run(_first_bwd, "first_bwd", dhn1, dx2, x, g_pre_mix)
    ex.packs["late"] = _pack([small["pre_mix_norm"]])
    return loss_cols, grad_x, small


def _into_slot(wsh, slot, dtype, name):
    rows, n = wsh.shape
    rb = _row_block(rows, 256) if rows % 8 == 0 else rows

    def body(s_ref, w_ref, o_ref):
        o_ref[...] = w_ref[...].astype(o_ref.dtype)

    return pl.pallas_call(
        body,
        grid_spec=pltpu.PrefetchScalarGridSpec(
            num_scalar_prefetch=1, grid=(rows // rb,),
            in_specs=[pl.BlockSpec((rb, n), lambda i, s_ref: (i, 0))],
            out_specs=pl.BlockSpec((None, rb, n), lambda i, s_ref: (s_ref[0], i, 0))),
        out_shape=S((4, rows, n), dtype), compiler_params=_cp("parallel"), name=name)(slot, wsh)


class _Exchange:
    SCHEDULE = {
        "gather_w_in": [("ici", "w_in"), ("ici", "conv_w")],
        "rms1": [("d2d", "w_in")],
        "in_proj": [("ici", "w_out")],
        "attn_fwd": [("d2d", "w_out"), ("ici", "w_ffn_gate"), ("ici", "w_ffn_up", 0)],
        "lru_fwd": [("d2d", "w_ffn_gate"), ("d2d", "w_ffn_up", 0), ("ici", "w_ffn_up", 1), ("ici", "w_ffn_up", 2)],
        "out_proj": [("d2d", "w_ffn_up", 1), ("d2d", "w_ffn_up", 2), ("ici", "w_ffn_up", 3)],
        "mid_fwd": [("d2d", "w_ffn_up", 3)],
        "ffn_gate_up": [("ici", "w_ffn_down")],
        "gather_w_down": [("d2d", "w_ffn_down")],
        "ffn_gate_dw": [("swap", "w_ffn_down")],
        "ffn_up_dw": [("scatter", "w_ffn_down"), ("swap", "w_ffn_gate")],
        "ffn_gate_dx": [("share", "w_ffn_down"), ("scatter", "w_ffn_gate"), ("swap", "w_ffn_up")],
        "ffn_up_dx": [("share", "w_ffn_gate"), ("scatter", "w_ffn_up")],
        "mid_bwd": [("share", "w_ffn_up")],
        "outnorm_bwd": [("swap", "w_out")],
        "lru_bwd": [("scatter", "w_out")],
        "attn_bwd": [("share", "w_out"), ("spread", "early")],
        "in_proj_dx": [("swap", "w_in")],
        "first_bwd": [("scatter", "w_in", 0)],
        "adamw_w_ffn_down": [("scatter", "w_in", 1)],
        "adamw_w_ffn_gate": [("scatter", "w_in", 2)],
        "adamw_w_ffn_up": [("scatter", "w_in", 3)],
        "grads_w_in_share": [("share", "w_in"), ("spread", "late")],
    }
    PIECES = 4

    def __init__(self, slots, place):
        self.buf, self.place = dict(slots), place
        self.grads, self.packs, self.swapped, self.part, self.scattered, self.full, self.spreaded = {}, {}, {}, {}, {}, {}, {}

    def weight(self, name):
        b = self.buf[name]
        return b.reshape(-1, b.shape[2]) if name in ("w_out", "w_ffn_down") else b

    def carrier(self, call):
        car = _Carrier()
        car.todo, slot = [], {}
        for kind, name, *piece in self.SCHEDULE[call]:
            if kind in ("ici", "d2d"):
                if name not in slot:
                    slot[name] = car.inplace(self.buf[name])
                    car.todo.append((self.buf, name, slot[name]))
                size = self.buf[name].shape[1] // 2 // self.PIECES
                rows = (piece[0] * size, size) if piece else None
                if kind == "ici":
                    car.gather_ici(slot[name], rows, split=name != "conv_w")
                else:
                    car.gather_d2d(slot[name], rows)
            elif kind == "swap":
                g = self.grads[name]
                o = car.fresh((4, g.shape[1] // 2, g.shape[2]), F32)
                car.swap(car.read(g), o)
                car.todo.append((self.swapped, name, o))
            elif kind == "scatter":
                if name not in self.part:
                    self.part[name] = _add_own_half(self.grads[name], self.swapped[name], self.place[1:], "grads_add_" + name)
                p = self.part[name]
                o = car.inplace(self.scattered[name]) if name in self.scattered else car.fresh(p.shape, p.dtype)
                size = p.shape[1] // self.PIECES
                car.scatter(car.read(p), o, (piece[0] * size, size) if piece else None)
                car.todo.append((self.scattered, name, o))
            elif kind == "share":
                o = car.inplace(_sum_chips(self.part[name], self.scattered[name], self.place, "grads_sum_" + name))
                car.share(o)
                car.todo.append((self.full, name, o))
            else:
                o = car.fresh((8,) + self.packs[name].shape, F32)
                car.spread(car.read(self.packs[name]), o)
                car.todo.append((self.spreaded, name, o))
        return car

    def harvest(self, car):
        for state, name, o in car.todo:
            state[name] = car.results[o]

    def alone(self, call):
        car = self.carrier(call)
        car.run_alone(call)
        self.harvest(car)

    def small_sum(self, key):
        return _sum_devices(self.packs[key], self.spreaded[key], 2 * self.place[0:1] + self.place[1:], "grads_small_sum_" + key)


def _row_block(rows, cap):
    return max(b for b in range(8, cap + 1, 8) if rows % b == 0)


def _add_own_half(g, recv, core, name):
    _, rows, n = g.shape
    half = rows // 2
    rb = _row_block(half, 512)
    nb = half // rb

    def body(c_ref, g_ref, r_ref, o_ref):
        o_ref[...] = (g_ref[...] + r_ref[...]).astype(o_ref.dtype)

    return pl.pallas_call(
        body,
        grid_spec=pltpu.PrefetchScalarGridSpec(
            num_scalar_prefetch=1, grid=(4, nb),
            in_specs=[pl.BlockSpec((None, rb, n), lambda k, i, c_ref: (k, c_ref[0] * nb + i, 0)),
                      pl.BlockSpec((None, rb, n), lambda k, i, c_ref: (k, i, 0))],
            out_specs=pl.BlockSpec((None, rb, n), lambda k, i, c_ref: (k, i, 0))),
        out_shape=S((4, half, n), BF16), compiler_params=_cp("parallel", "parallel"), name=name)(core, g, recv)


def _sum_chips(part, recv, place, name):
    _, rows, n = part.shape
    rb = _row_block(rows, 64)
    nb = rows // rb

    def body(p_ref, own_ref, r0, r1, r2, r3, o_ref):
        own = own_ref[...].astype(F32)
        terms = [jnp.where(p_ref[0] == k, own, r[...].astype(F32)) for k, r in enumerate((r0, r1, r2, r3))]
        o_ref[...] = ((terms[0] + terms[1]) + terms[2]) + terms[3]

    def slot(k):
        return pl.BlockSpec((None, rb, n), lambda i, p_ref: (jnp.where(p_ref[0] == k, (k + 1) % 4, k), i, 0))

    return pl.pallas_call(
        body,
        grid_spec=pltpu.PrefetchScalarGridSpec(
            num_scalar_prefetch=1, grid=(nb,),
            in_specs=[pl.BlockSpec((None, rb, n), lambda i, p_ref: (p_ref[0], i, 0))] + [slot(k) for k in range(4)],
            out_specs=pl.BlockSpec((rb, n), lambda i, p_ref: (p_ref[1] * nb + i, 0))),
        out_shape=S((2 * rows, n), F32), compiler_params=_cp("parallel"), name=name)(place, part, recv, recv, recv, recv)


def _sum_devices(own, spread, me, name):
    rows = own.shape[0]

    def body(me_ref, own_ref, *refs):
        acc = None
        for k, r in enumerate(refs[:8]):
            term = jnp.where(me_ref[0] == k, own_ref[...], r[...])
            acc = term if acc is None else acc + term
        refs[8][...] = acc

    def slot(k):
        return pl.BlockSpec((None, rows, 128), lambda i, me_ref: (jnp.where(me_ref[0] == k, (k + 1) % 8, k), 0, 0))

    whole = pl.BlockSpec((rows, 128), lambda i, me_ref: (0, 0))
    return pl.pallas_call(
        body,
        grid_spec=pltpu.PrefetchScalarGridSpec(num_scalar_prefetch=1, grid=(1,), in_specs=[whole] + [slot(k) for k in range(8)],
                                               out_specs=whole),
        out_shape=S((rows, 128), F32), compiler_params=_cp("arbitrary"), name=name)(me, own, *[spread] * 8)


def _adamw(w, g, m, v, name, comm=None):
    rows, n = w.shape
    rb = rows if rows * n * 4 <= (1 << 21) else _row_block(rows, 128)
    c1 = 1.0 - ADAM_B1 ** ADAM_STEP
    c2 = 1.0 - ADAM_B2 ** ADAM_STEP

    def body(w_ref, g_ref, m_ref, v_ref, d_ref, nm_ref, nv_ref):
        gv = g_ref[...]
        nm = ADAM_B1 * m_ref[...] + (1.0 - ADAM_B1) * gv
        nv = ADAM_B2 * v_ref[...] + (1.0 - ADAM_B2) * (gv * gv)
        nm_ref[...] = nm
        nv_ref[...] = nv
        d_ref[...] = -ADAM_LR * ((nm / c1) / (jnp.sqrt(nv / c2) + ADAM_EPS) + ADAM_WD * w_ref[...])

    bs = pl.BlockSpec((rb, n), lambda i: (i, 0))
    return _call(comm, body, grid=(rows // rb,), in_specs=[bs] * 4, out_specs=[bs] * 3, out_shape=[S((rows, n), F32)] * 3,
                 compiler_params=_cp("parallel"), name=name)(w, g, m, v)


_BIG = ("w_in", "w_out", "w_ffn_gate", "w_ffn_up", "w_ffn_down")
_SMALL = ("pre_mix_norm", "post_mix_norm", "pre_ffn_norm", "post_ffn_norm", "conv_w", "conv_b", "w_rgate", "b_rgate",
          "w_igate", "b_igate", "lru_lambda", "attn_out_norm", "lru_out_norm")
_SMALL_EARLY = _SMALL[1:]
_WEIGHTS = ("pre_mix_norm", "post_mix_norm", "pre_ffn_norm", "post_ffn_norm", "w_in", "conv_w", "conv_b", "w_rgate", "b_rgate",
            "w_igate", "b_igate", "lru_lambda", "attn_out_norm", "lru_out_norm", "w_out", "w_ffn_gate", "w_ffn_up", "w_ffn_down")


def _pack(arrays):
    flat = []
    for a in arrays:
        f = a.reshape(-1)
        flat.append(jnp.pad(f, (0, (-f.shape[0]) % 1024)))
    return jnp.concatenate(flat).reshape(-1, 128)


def _unpack(packed, shapes):
    out, pos = [], 0
    flat = packed.reshape(-1)
    for s in shapes:
        size = math.prod(s)
        out.append(flat[pos:pos + size].reshape(s))
        pos += size + (-size) % 1024
    return out


def kernel(x, pre_mix_norm, post_mix_norm, pre_ffn_norm, post_ffn_norm, w_in, conv_w, conv_b, w_rgate, b_rgate, w_igate, b_igate, lru_lambda, attn_out_norm, lru_out_norm, w_out, w_ffn_gate, w_ffn_up, w_ffn_down, loss_target, m_pre_mix_norm, m_post_mix_norm, m_pre_ffn_norm, m_post_ffn_norm, m_w_in, m_conv_w, m_conv_b, m_w_rgate, m_b_rgate, m_w_igate, m_b_igate, m_lru_lambda, m_attn_out_norm, m_lru_out_norm, m_w_out, m_w_ffn_gate, m_w_ffn_up, m_w_ffn_down, v_pre_mix_norm, v_post_mix_norm, v_pre_ffn_norm, v_post_ffn_norm, v_w_in, v_conv_w, v_conv_b, v_w_rgate, v_b_rgate, v_w_igate, v_b_igate, v_lru_lambda, v_attn_out_norm, v_lru_out_norm, v_w_out, v_w_ffn_gate, v_w_ffn_up, v_w_ffn_down):
    given = dict(locals())
    w = {n: given[n][0] for n in _WEIGHTS}
    m = {n: given["m_" + n][0] for n in _WEIGHTS}
    v = {n: given["v_" + n][0] for n in _WEIGHTS}
    xs, target = x[0], loss_target[0]
    d = xs.shape[1]
    chip = (2 * lax.axis_index("x") + lax.axis_index("y")).astype(jnp.int32)
    place = jnp.stack([chip, lax.axis_index("c").astype(jnp.int32)])

    slots = {n: _into_slot(w[n], place[0:1], _MXU, "slot_" + n) for n in _BIG}
    slots["conv_w"] = _into_slot(w["conv_w"], place[0:1], F32, "slot_conv_w")
    ex = _Exchange(slots, place)
    ex.alone("gather_w_in")
    conv_full = jnp.transpose(ex.buf["conv_w"], (1, 0, 2)).reshape(CONV_WIDTH, -1)
    row = lambda a: a.reshape(1, -1)
    norms = tuple(row(w[n]) for n in ("pre_mix_norm", "post_mix_norm", "pre_ffn_norm", "post_ffn_norm"))

    loss_cols, grad_x, small = _local_step(
        xs, target, norms, ex, conv_full, row(w["conv_b"]), w["w_rgate"], row(w["b_rgate"]),
        w["w_igate"], row(w["b_igate"]), row(w["lru_lambda"]), row(w["attn_out_norm"]), row(w["lru_out_norm"]))

    loss = lax.psum(0.5 * jnp.sum(loss_cols) / d, ("x", "y", "c"))

    delta, new_m, new_v = {}, {}, {}
    for n in ("w_ffn_down", "w_ffn_gate", "w_ffn_up"):
        car = ex.carrier("adamw_" + n)
        delta[n], new_m[n], new_v[n] = _adamw(w[n], ex.full[n], m[n], v[n], "adamw_" + n, comm=car)
        ex.harvest(car)
    ex.alone("grads_w_in_share")
    reduced = {n: ex.full[n] for n in _BIG}
    early = _unpack(ex.small_sum("early"), [small[n].shape for n in _SMALL_EARLY])
    late = _unpack(ex.small_sum("late"), [small["pre_mix_norm"].shape])
    for n, g in zip(_SMALL_EARLY + ("pre_mix_norm",), early + late):
        reduced[n] = g.reshape(w[n].shape) if n != "conv_w" else lax.dynamic_slice_in_dim(g, chip * w[n].shape[1], w[n].shape[1], axis=1)

    for n in ("w_in", "w_out"):
        delta[n], new_m[n], new_v[n] = _adamw(w[n], reduced[n], m[n], v[n], "adamw_" + n)
    shapes = [w[n].shape for n in _SMALL]
    packed = _adamw(*[_pack([src[n] for n in _SMALL]) for src in (w, reduced, m, v)], "adamw_small")
    for out, p in zip((delta, new_m, new_v), packed):
        out.update(zip(_SMALL, _unpack(p, shapes)))

    lead = lambda a: a[None]
    return (loss, lead(grad_x), *[lead(reduced[n]) for n in _WEIGHTS], *[lead(delta[n]) for n in _WEIGHTS],
            *[lead(new_m[n]) for n in _WEIGHTS], *[lead(new_v[n]) for n in _WEIGHTS])
```

```python
import functools
import math

import jax
import jax.numpy as jnp
from jax import lax
from jax.experimental import pallas as pl
from jax.experimental.pallas import tpu as pltpu

F32 = jnp.float32
BF16 = jnp.bfloat16
_MXU = BF16
S = jax.ShapeDtypeStruct

RMS_EPS = 1e-6
HEAD_DIM = 128
CONV_WIDTH = 4
LRU_C = 8.0
ADAM_LR, ADAM_B1, ADAM_B2, ADAM_EPS, ADAM_WD, ADAM_STEP = 0.001, 0.9, 0.999, 1e-08, 0.01, 10
EXP_CUT = -105.0
VMEM_LIMIT = 60 * 1024 * 1024
ROW_TILE = 256
SEQ_TILE = 256
ATTN_BLOCK = 256
ATTN_HEADS = 2
MESH = pl.DeviceIdType.MESH


def _cp(*sem):
    return pltpu.CompilerParams(dimension_semantics=sem, vmem_limit_bytes=VMEM_LIMIT)


def _dot(a, b):
    return jnp.dot(a, b, preferred_element_type=F32)


def _dot_nt(a, b):
    return lax.dot_general(a, b, (((1,), (1,)), ((), ())), preferred_element_type=F32)


def _dot_tn(a, b):
    return lax.dot_general(a, b, (((0,), (0,)), ((), ())), preferred_element_type=F32)


def _rstd(v):
    return lax.rsqrt(jnp.mean(v * v, axis=-1, keepdims=True) + RMS_EPS)


def _rms_bwd(dn, vh, r, gain):
    dvh = dn * gain
    dv = r * (dvh - vh * jnp.mean(dvh * vh, axis=-1, keepdims=True))
    return dv, jnp.sum(dn * vh, axis=0, keepdims=True)


def _log_sigmoid(z):
    return jnp.minimum(z, 0.0) - jnp.log(1.0 + jnp.exp(-jnp.abs(z)))


def _expm1(v):
    small = v * (1.0 + v * (0.5 + v * (1.0 / 6.0 + v * (1.0 / 24.0 + v * (1.0 / 120.0)))))
    return jnp.where(jnp.abs(v) < 0.04, small, jnp.exp(v) - 1.0)


_GELU_C = math.sqrt(2.0 / math.pi)


def _gelu(v):
    return 0.5 * v * (1.0 + jnp.tanh(_GELU_C * (v + 0.044715 * v * v * v)))


def _gelu_grad(v):
    th = jnp.tanh(_GELU_C * (v + 0.044715 * v * v * v))
    return 0.5 * (1.0 + th) + 0.5 * v * (1.0 - th * th) * _GELU_C * (1.0 + 3.0 * 0.044715 * v * v)


def _row_spec(tm, d):
    return pl.BlockSpec((tm, d), lambda i: (i, 0))


def _vec_spec(d):
    return pl.BlockSpec((1, d), lambda i: (0, 0))


_ANY = pl.BlockSpec(memory_space=pl.ANY)


def _place():
    x, y, c = lax.axis_index("x"), lax.axis_index("y"), lax.axis_index("c")
    return x, y, c, [(1 - x, y), (x, 1 - y), (1 - x, 1 - y)]


def _remote(src, dst, send_sem, recv_sem, to):
    return pltpu.make_async_remote_copy(src_ref=src, dst_ref=dst, send_sem=send_sem, recv_sem=recv_sem,
                                        device_id=to, device_id_type=MESH)


class _Carrier:
    def __init__(self):
        self.inputs, self.out_shapes, self.aliases, self.ops, self.n_sems, self.results = [], [], {}, [], 0, None

    def inplace(self, arr):
        self.aliases[len(self.inputs)] = len(self.out_shapes)
        self.inputs.append(arr)
        self.out_shapes.append(S(arr.shape, arr.dtype))
        return len(self.out_shapes) - 1

    def read(self, arr):
        self.inputs.append(arr)
        return len(self.inputs) - 1

    def fresh(self, shape, dtype):
        self.out_shapes.append(S(shape, dtype))
        return len(self.out_shapes) - 1

    def _add(self, n_sems, copies):
        base = self.n_sems
        self.n_sems += n_sems

        def start(ins, outs, send, recv):
            for k, (src, dst, _, to) in enumerate(copies(ins, outs)):
                _remote(src, dst, send.at[base + k], recv.at[base + k], to).start()

        def finish(ins, outs, send, recv):
            for k, (src, _, land, to) in enumerate(copies(ins, outs)):
                _remote(src, land, send.at[base + k], recv.at[base + k], to).wait()

        self.ops.append((start, finish))

    def gather_ici(self, o, rows=None, split=True):
        half = self.out_shapes[o].shape[1] // 2
        lo, size = rows or (0, half)

        def copies(ins, outs):
            x, y, c, chips = _place()
            part = (lambda ref: ref.at[pl.ds(c * half + lo, size)]) if split else (lambda ref: ref)
            mine = part(outs[o].at[2 * x + y])
            return [(mine, mine, part(outs[o].at[2 * px + py]), (px, py, c)) for px, py in chips]

        self._add(3, copies)

    def gather_d2d(self, o, rows=None):
        half = self.out_shapes[o].shape[1] // 2
        lo, size = rows or (0, half)

        def copies(ins, outs):
            x, y, c, chips = _place()
            at = lambda k, cc: outs[o].at[k].at[pl.ds(cc * half + lo, size)]
            return [(at(2 * px + py, c), at(2 * px + py, c), at(2 * px + py, 1 - c), (x, y, 1 - c)) for px, py in chips]

        self._add(3, copies)

    def swap(self, i, o):
        half = self.inputs[i].shape[1] // 2

        def copies(ins, outs):
            x, y, c, _ = _place()
            return [(ins[i].at[:, pl.ds((1 - c) * half, half)], outs[o], outs[o], (x, y, 1 - c))]

        self._add(1, copies)

    def scatter(self, i, o, rows=None):
        lo, size = rows or (0, self.inputs[i].shape[1])

        def copies(ins, outs):
            x, y, c, chips = _place()
            cut = lambda ref: ref.at[pl.ds(lo, size)]
            return [(cut(ins[i].at[2 * px + py]), cut(outs[o].at[2 * x + y]), cut(outs[o].at[2 * px + py]), (px, py, c)) for px, py in chips]

        self._add(3, copies)

    def share(self, o):
        r = self.out_shapes[o].shape[0] // 2

        def copies(ins, outs):
            x, y, c, _ = _place()
            mine = outs[o].at[pl.ds(c * r, r)]
            return [(mine, mine, outs[o].at[pl.ds((1 - c) * r, r)], (x, y, 1 - c))]

        self._add(1, copies)

    def spread(self, i, o):
        def copies(ins, outs):
            x, y, c, _ = _place()
            me = 4 * x + 2 * y + c
            out = []
            for d in range(1, 8):
                to, frm = (me + d) % 8, (me + 8 - d) % 8
                out.append((ins[i], outs[o].at[me], outs[o].at[frm], (to // 4, (to // 2) % 2, to % 2)))
            return out

        self._add(7, copies)

    def _pallas(self, body, n_in, n_out, scratch, **kw):
        k_in, k_out = len(self.inputs), len(self.out_shapes)
        grid = kw.get("grid", ())

        def wrapped(*refs):
            ins, cins = refs[:n_in], refs[n_in:n_in + k_in]
            outs = refs[n_in + k_in:n_in + k_in + n_out]
            couts = refs[n_in + k_in + n_out:n_in + k_in + n_out + k_out]
            own = refs[n_in + k_in + n_out + k_out:]
            send, recv = own[len(scratch):]
            ids = [pl.program_id(a) for a in range(len(grid))]
            first = functools.reduce(jnp.logical_and, [a == 0 for a in ids], True)
            last = functools.reduce(jnp.logical_and, [a == g - 1 for a, g in zip(ids, grid)], True)

            def go(stage):
                for op in self.ops:
                    op[stage](cins, couts, send, recv)

            if grid:
                pl.when(first)(lambda: go(0))
                body(*ins, *outs, *own[:len(scratch)])
                pl.when(last)(lambda: go(1))
            else:
                go(0)
                go(1)

        sem = pltpu.SemaphoreType.DMA((self.n_sems,))
        return pl.pallas_call(
            wrapped, in_specs=list(kw.get("in_specs", [])) + [_ANY] * k_in, out_specs=list(kw.get("out_specs", [])) + [_ANY] * k_out,
            out_shape=list(kw.get("out_shape", [])) + self.out_shapes, scratch_shapes=list(scratch) + [sem, sem],
            input_output_aliases={n_in + i: n_out + o for i, o in self.aliases.items()}, name=kw["name"],
            **({"grid": grid, "compiler_params": _cp(*["arbitrary"] * len(grid))} if grid else {}))

    def run(self, body, kw, *args):
        single = not isinstance(kw["out_shape"], (list, tuple))
        out_shape = [kw["out_shape"]] if single else list(kw["out_shape"])
        out_specs = [kw["out_specs"]] if single else list(kw["out_specs"])
        res = self._pallas(body, len(args), len(out_shape), kw.get("scratch_shapes", []), grid=kw["grid"], in_specs=kw["in_specs"],
                           out_specs=out_specs, out_shape=out_shape, name=kw["name"])(*args, *self.inputs)
        self.results = list(res[len(out_shape):])
        return res[0] if single else list(res[:len(out_shape)])

    def run_alone(self, name):
        self.results = list(self._pallas(None, 0, 0, [], name=name)(*self.inputs))


def _call(comm, body, **kw):
    if comm is None:
        return pl.pallas_call(body, **kw)
    return functools.partial(comm.run, body, kw)


def _mm_nn(a, b3, *, bm, bn, name, also=None, comm=None):
    m, k = a.shape
    c, _, n = b3.shape
    ni, nj = m // bm, n // bn

    def body(a_ref, b_ref, *o_refs):
        res = _dot(a_ref[...], b_ref[...])
        for o_ref in o_refs:
            o_ref[...] = res.astype(o_ref.dtype)

    ospec = pl.BlockSpec((bm, bn), lambda cc, j, i: (i, cc * nj + j))
    dtypes = [F32] + ([] if also is None else [also])
    out = _call(
        comm, body, grid=(c, nj, ni),
        in_specs=[pl.BlockSpec((bm, k), lambda cc, j, i: (i, 0)), pl.BlockSpec((None, k, bn), lambda cc, j, i: (cc, 0, j))],
        out_specs=[ospec] * len(dtypes), out_shape=[S((m, c * n), dt) for dt in dtypes],
        compiler_params=_cp("parallel", "parallel", "parallel"), name=name)(a, b3)
    return out[0] if also is None else out


def _mm_nt(a, b3, *, bm, bo, out_dtype, name, comm=None):
    m = a.shape[0]
    c, ko, n = b3.shape
    ni, nj = m // bm, ko // bo

    def body(a_ref, b_ref, o_ref):
        acc = _dot_nt(a_ref[:, 0:n], b_ref[0])
        for cc in range(1, c):
            acc = acc + _dot_nt(a_ref[:, cc * n:(cc + 1) * n], b_ref[cc])
        o_ref[...] = acc.astype(o_ref.dtype)

    return _call(
        comm, body, grid=(nj, ni),
        in_specs=[pl.BlockSpec((bm, c * n), lambda j, i: (i, 0)),
                  pl.BlockSpec((c, bo, n), lambda j, i: (0, j, 0))],
        out_specs=pl.BlockSpec((bm, bo), lambda j, i: (i, j)),
        out_shape=S((m, ko), out_dtype),
        compiler_params=_cp("parallel", "parallel"), name=name)(a, b3)


def _mm_tn(a, b, c, *, bm, bk, name, comm=None):
    m, k = a.shape
    n = b.shape[1] // c
    nm, nk = m // bm, k // bk

    def body(a_ref, b_ref, o_ref, acc):
        mm = pl.program_id(2)

        @pl.when(mm == 0)
        def _():
            acc[...] = jnp.zeros_like(acc)

        acc[...] += _dot_tn(a_ref[...], b_ref[...])

        @pl.when(mm == nm - 1)
        def _():
            o_ref[...] = acc[...]

    return _call(
        comm, body, grid=(c, nk, nm),
        in_specs=[pl.BlockSpec((bm, bk), lambda cc, j, mm: (mm, j)),
                  pl.BlockSpec((bm, n), lambda cc, j, mm: (mm, cc))],
        out_specs=pl.BlockSpec((None, bk, n), lambda cc, j, mm: (cc, j, 0)),
        out_shape=S((c, k, n), F32),
        scratch_shapes=[pltpu.VMEM((bk, n), F32)],
        compiler_params=_cp("parallel", "parallel", "arbitrary"), name=name)(a, b)


def _lane_pieces(n, parts):
    base, extra = divmod(n // 128, parts)
    sizes = [128 * (base + (p < extra)) for p in range(parts)]
    return [slice(sum(sizes[:p]), sum(sizes[:p + 1])) for p in range(parts) if sizes[p]]


def _swiglu_fwd(hn, wg3, wu3, *, bm, name, comm=None):
    m, k = hn.shape
    c, _, n = wg3.shape

    def body(a_ref, g_ref, u_ref, gate_ref, up_ref, act_ref):
        a = a_ref[...]
        for cols in _lane_pieces(n, 4):
            gate = _dot(a, g_ref[:, cols])
            up = _dot(a, u_ref[:, cols])
            gate_ref[:, cols] = gate
            up_ref[:, cols] = up
            act_ref[:, cols] = (gate * jax.nn.sigmoid(gate) * up).astype(act_ref.dtype)

    wspec = pl.BlockSpec((None, k, n), lambda cc, i: (cc, 0, 0))
    ospec = pl.BlockSpec((bm, n), lambda cc, i: (i, cc))
    return _call(
        comm, body, grid=(c, m // bm),
        in_specs=[pl.BlockSpec((bm, k), lambda cc, i: (i, 0)), wspec, wspec],
        out_specs=[ospec, ospec, ospec],
        out_shape=[S((m, c * n), F32), S((m, c * n), F32), S((m, c * n), _MXU)],
        compiler_params=_cp("parallel", "parallel"), name=name)(hn, wg3, wu3)


def _swiglu_bwd(df, wd, gate, up, *, bm, bo, name):
    m, k = df.shape
    ko = wd.shape[0]

    def body(a_ref, b_ref, g_ref, u_ref, dg_ref, du_ref):
        a = a_ref[...]
        for cols in _lane_pieces(bo, 4):
            dact = _dot_nt(a, b_ref[cols, :])
            gate = g_ref[:, cols]
            sg = jax.nn.sigmoid(gate)
            dg_ref[:, cols] = (dact * u_ref[:, cols] * (sg * (1.0 + gate * (1.0 - sg)))).astype(dg_ref.dtype)
            du_ref[:, cols] = (dact * (gate * sg)).astype(du_ref.dtype)

    ospec = pl.BlockSpec((bm, bo), lambda j, i: (i, j))
    return pl.pallas_call(
        body, grid=(ko // bo, m // bm),
        in_specs=[pl.BlockSpec((bm, k), lambda j, i: (i, 0)), pl.BlockSpec((bo, k), lambda j, i: (j, 0)), ospec, ospec],
        out_specs=[ospec, ospec],
        out_shape=[S((m, ko), _MXU), S((m, ko), _MXU)],
        compiler_params=_cp("parallel", "parallel"), name=name)(df, wd, gate, up)


def _rms_fwd(x, gain, name, comm=None):
    t, d = x.shape
    tm = min(t, ROW_TILE)

    def body(x_ref, g_ref, o_ref):
        xv = x_ref[...]
        o_ref[...] = ((xv * _rstd(xv)) * g_ref[...]).astype(o_ref.dtype)

    return _call(comm, body, grid=(t // tm,), in_specs=[_row_spec(tm, d), _vec_spec(d)], out_specs=_row_spec(tm, d),
                          out_shape=S((t, d), _MXU), compiler_params=_cp("parallel"), name=name)(x, gain)


def _outnorm_fwd(o, yl, ga, gl, name):
    t, w = o.shape
    tm = min(t, ROW_TILE)

    def body(o_ref, l_ref, ga_ref, gl_ref, y_ref):
        ov, lv = o_ref[...], l_ref[...]
        y_ref[:, :w] = ((ov * _rstd(ov)) * ga_ref[...]).astype(y_ref.dtype)
        y_ref[:, w:] = ((lv * _rstd(lv)) * gl_ref[...]).astype(y_ref.dtype)

    return pl.pallas_call(body, grid=(t // tm,), in_specs=[_row_spec(tm, w), _row_spec(tm, w), _vec_spec(w), _vec_spec(w)],
                          out_specs=_row_spec(tm, 2 * w), out_shape=S((t, 2 * w), _MXU),
                          compiler_params=_cp("parallel"), name=name)(o, yl, ga, gl)


def _mid_fwd(x, mix, g_post, g_pre, name, comm=None):
    t, d = x.shape
    tm = min(t, ROW_TILE)

    def body(x_ref, m_ref, gp_ref, gn_ref, x2_ref, hn_ref):
        mv = m_ref[...]
        x2 = x_ref[...] + (mv * _rstd(mv)) * gp_ref[...]
        x2_ref[...] = x2
        hn_ref[...] = ((x2 * _rstd(x2)) * gn_ref[...]).astype(hn_ref.dtype)

    return _call(comm, body, grid=(t // tm,), in_specs=[_row_spec(tm, d), _row_spec(tm, d), _vec_spec(d), _vec_spec(d)],
                          out_specs=[_row_spec(tm, d), _row_spec(tm, d)], out_shape=[S((t, d), F32), S((t, d), _MXU)],
                          compiler_params=_cp("parallel"), name=name)(x, mix, g_post, g_pre)


def _final(f, x2, target, g_post, name):
    t, d = f.shape
    tm = min(t, ROW_TILE)

    def body(f_ref, x2_ref, t_ref, g_ref, loss_ref, dout_ref, df_ref, dg_ref):
        @pl.when(pl.program_id(0) == 0)
        def _():
            loss_ref[...] = jnp.zeros_like(loss_ref)
            dg_ref[...] = jnp.zeros_like(dg_ref)

        fv = f_ref[...]
        r = _rstd(fv)
        fh = fv * r
        err = (x2_ref[...] + fh * g_ref[...]) - t_ref[...]
        loss_ref[...] += jnp.sum(err * err, axis=0, keepdims=True)
        dout = err * (1.0 / d)
        dout_ref[...] = dout
        dfv, dg = _rms_bwd(dout, fh, r, g_ref[...])
        df_ref[...] = dfv.astype(df_ref.dtype)
        dg_ref[...] += dg

    return pl.pallas_call(
        body, grid=(t // tm,),
        in_specs=[_row_spec(tm, d), _row_spec(tm, d), _row_spec(tm, d), _vec_spec(d)],
        out_specs=[_vec_spec(d), _row_spec(tm, d), _row_spec(tm, d), _vec_spec(d)],
        out_shape=[S((1, d), F32), S((t, d), F32), S((t, d), _MXU), S((1, d), F32)],
        compiler_params=_cp("arbitrary"), name=name)(f, x2, target, g_post)


def _mid_bwd(dhn_a, dhn_b, dout, x2, mix, g_pre, g_post, name, comm=None):
    t, d = x2.shape
    tm = min(t, ROW_TILE)

    def body(da_ref, db_ref, do_ref, x2_ref, m_ref, gn_ref, gp_ref, dx2_ref, dm_ref, dgn_ref, dgp_ref):
        @pl.when(pl.program_id(0) == 0)
        def _():
            dgn_ref[...] = jnp.zeros_like(dgn_ref)
            dgp_ref[...] = jnp.zeros_like(dgp_ref)

        x2 = x2_ref[...]
        r = _rstd(x2)
        dxa, dgn = _rms_bwd(da_ref[...] + db_ref[...], x2 * r, r, gn_ref[...])
        dx2 = do_ref[...] + dxa
        dx2_ref[...] = dx2
        dgn_ref[...] += dgn
        mv = m_ref[...]
        rm = _rstd(mv)
        dmv, dgp = _rms_bwd(dx2, mv * rm, rm, gp_ref[...])
        dm_ref[...] = dmv.astype(dm_ref.dtype)
        dgp_ref[...] += dgp

    rs, vs = _row_spec(tm, d), _vec_spec(d)
    return _call(
        comm, body, grid=(t // tm,), in_specs=[rs, rs, rs, rs, rs, vs, vs], out_specs=[rs, rs, vs, vs],
        out_shape=[S((t, d), F32), S((t, d), _MXU), S((1, d), F32), S((1, d), F32)],
        compiler_params=_cp("arbitrary"), name=name)(dhn_a, dhn_b, dout, x2, mix, g_pre, g_post)


def _first_bwd(dhn, dx2, x, gain, name, comm=None):
    t, d = x.shape
    tm = min(t, ROW_TILE)

    def body(dh_ref, dx2_ref, x_ref, g_ref, dx_ref, dg_ref):
        @pl.when(pl.program_id(0) == 0)
        def _():
            dg_ref[...] = jnp.zeros_like(dg_ref)

        xv = x_ref[...]
        r = _rstd(xv)
        dxa, dg = _rms_bwd(dh_ref[...], xv * r, r, g_ref[...])
        dx_ref[...] = dx2_ref[...] + dxa
        dg_ref[...] += dg

    rs, vs = _row_spec(tm, d), _vec_spec(d)
    return _call(comm, body, grid=(t // tm,), in_specs=[rs, rs, rs, vs], out_specs=[rs, vs],
                          out_shape=[S((t, d), F32), S((1, d), F32)], compiler_params=_cp("arbitrary"), name=name)(dhn, dx2, x, gain)


def _outnorm_bwd(dy, o, yl, ga, gl, name, comm=None):
    t, w = o.shape
    tm = min(t, ROW_TILE)

    def body(dy_ref, o_ref, l_ref, ga_ref, gl_ref, do_ref, dl_ref, dga_ref, dgl_ref):
        @pl.when(pl.program_id(0) == 0)
        def _():
            dga_ref[...] = jnp.zeros_like(dga_ref)
            dgl_ref[...] = jnp.zeros_like(dgl_ref)

        ov, lv = o_ref[...], l_ref[...]
        ra, rl = _rstd(ov), _rstd(lv)
        dov, dga = _rms_bwd(dy_ref[:, :w], ov * ra, ra, ga_ref[...])
        dlv, dgl = _rms_bwd(dy_ref[:, w:], lv * rl, rl, gl_ref[...])
        do_ref[...] = dov.astype(do_ref.dtype)
        dl_ref[...] = dlv
        dga_ref[...] += dga
        dgl_ref[...] += dgl

    rs, vs = _row_spec(tm, w), _vec_spec(w)
    return _call(comm, body, grid=(t // tm,), in_specs=[_row_spec(tm, 2 * w), rs, rs, vs, vs], out_specs=[rs, rs, vs, vs],
                          out_shape=[S((t, w), _MXU), S((t, w), F32), S((1, w), F32), S((1, w), F32)],
                          compiler_params=_cp("arbitrary"), name=name)(dy, o, yl, ga, gl)


def _split_dot(v, tri):
    hi = v.astype(_MXU)
    lo = (v - hi.astype(F32)).astype(_MXU)
    return _dot(hi, tri) + _dot(lo, tri)


def _attn_tile(qb, kb, row, col, shift, scale):
    z = _dot_nt(qb, kb) * scale
    mask = (col + shift) < row
    lb = _log_sigmoid(z)
    lm = jnp.where(mask, lb - z, 0.0)
    return mask, lb, lm


def _attn_fwd(proj, n_heads, name, comm=None):
    t = proj.shape[0]
    bq = min(t, ATTN_BLOCK)
    nq = t // bq
    scale = 1.0 / math.sqrt(HEAD_DIM)

    heads = [slice(a * HEAD_DIM, (a + 1) * HEAD_DIM) for a in range(ATTN_HEADS)]

    def body(q_ref, k_ref, v_ref, o_ref):
        row = lax.broadcasted_iota(jnp.int32, (bq, bq), 0)
        col = lax.broadcasted_iota(jnp.int32, (bq, bq), 1)
        tri = (row > col).astype(_MXU)

        def per_q(qi, _):
            q0 = pl.multiple_of(qi * bq, bq)
            qbs = [q_ref[pl.ds(q0, bq), hd] for hd in heads]

            def cond(st):
                return jnp.logical_and(st[0] >= 0, st[1])

            def step(st):
                kj, _, carries, accs = st
                k0 = pl.multiple_of(kj * bq, bq)
                alive, new_carries, new_accs = None, [], []
                for hd, qb, carry, acc in zip(heads, qbs, carries, accs):
                    mask, lb, lm = _attn_tile(qb, k_ref[pl.ds(k0, bq), hd], row, col, (kj - qi) * bq, scale)
                    w = jnp.where(mask, jnp.exp(lb + _split_dot(lm, tri) + carry), 0.0)
                    new_accs.append(acc + _dot(w.astype(_MXU), v_ref[pl.ds(k0, bq), hd]))
                    carry = carry + jnp.sum(lm, axis=1, keepdims=True)
                    new_carries.append(carry)
                    live = jnp.max(carry) > EXP_CUT
                    alive = live if alive is None else jnp.logical_or(alive, live)
                return kj - 1, alive, tuple(new_carries), tuple(new_accs)

            st = lax.while_loop(cond, step, (qi, jnp.bool_(True), (jnp.zeros((bq, 1), F32),) * ATTN_HEADS,
                                             (jnp.zeros((bq, HEAD_DIM), F32),) * ATTN_HEADS))
            for hd, acc in zip(heads, st[3]):
                o_ref[pl.ds(q0, bq), hd] = acc
            return 0

        lax.fori_loop(0, nq, per_q, 0)

    groups = n_heads // ATTN_HEADS
    hs = lambda off: pl.BlockSpec((t, ATTN_HEADS * HEAD_DIM), lambda h: (0, off + h))
    return _call(
        comm, body, grid=(groups,), in_specs=[hs(0), hs(groups), hs(2 * groups)], out_specs=hs(0),
        out_shape=S((t, n_heads * HEAD_DIM), F32), compiler_params=_cp("parallel"), name=name)(proj, proj, proj)


def _attn_bwd(proj, do, n_heads, name, comm=None):
    t = proj.shape[0]
    bq = min(t, ATTN_BLOCK)
    nq = t // bq
    scale = 1.0 / math.sqrt(HEAD_DIM)

    heads = [slice(a * HEAD_DIM, (a + 1) * HEAD_DIM) for a in range(ATTN_HEADS)]

    def body(q_ref, k_ref, v_ref, do_ref, dq_ref, dk_ref, dv_ref, dka_ref, dva_ref, g_ref, b_ref):
        dka_ref[...] = jnp.zeros_like(dka_ref)
        dva_ref[...] = jnp.zeros_like(dva_ref)
        row = lax.broadcasted_iota(jnp.int32, (bq, bq), 0)
        col = lax.broadcasted_iota(jnp.int32, (bq, bq), 1)
        tri = (row > col).astype(_MXU)
        tri_lt = (row < col).astype(_MXU)

        def per_q(qi, _):
            q0 = pl.multiple_of(qi * bq, bq)
            qbs = [q_ref[pl.ds(q0, bq), hd] for hd in heads]
            dobs = [do_ref[pl.ds(q0, bq), hd] for hd in heads]

            def cond(st):
                return jnp.logical_and(st[0] >= 0, st[1])

            def step(st):
                kj, _, carries = st
                k0 = pl.multiple_of(kj * bq, bq)
                alive, new_carries = None, []
                for a, (hd, qb, dob, carry) in enumerate(zip(heads, qbs, dobs, carries)):
                    mask, lb, lm = _attn_tile(qb, k_ref[pl.ds(k0, bq), hd], row, col, (kj - qi) * bq, scale)
                    w = jnp.where(mask, jnp.exp(lb + _split_dot(lm, tri) + carry), 0.0)
                    g_ref[a, pl.ds(k0, bq), :] = w * _dot_nt(dob, v_ref[pl.ds(k0, bq), hd])
                    b_ref[a, pl.ds(k0, bq), :] = jnp.where(mask, jnp.exp(lb), 0.0)
                    dva_ref[pl.ds(k0, bq), hd] += _dot_tn(w.astype(_MXU), dob)
                    carry = carry + jnp.sum(lm, axis=1, keepdims=True)
                    new_carries.append(carry)
                    live = jnp.max(carry) > EXP_CUT
                    alive = live if alive is None else jnp.logical_or(alive, live)
                return kj - 1, alive, tuple(new_carries)

            st = lax.while_loop(cond, step, (qi, jnp.bool_(True), (jnp.zeros((bq, 1), F32),) * ATTN_HEADS))

            def back(kj, st2):
                k0 = pl.multiple_of(kj * bq, bq)
                out = []
                for a, (hd, qb, (before, dq)) in enumerate(zip(heads, qbs, st2)):
                    g = g_ref[a, pl.ds(k0, bq), :]
                    beta = b_ref[a, pl.ds(k0, bq), :]
                    dz = ((g * (1.0 - beta) - (before + _split_dot(g, tri_lt)) * beta) * scale).astype(_MXU)
                    dka_ref[pl.ds(k0, bq), hd] += _dot_tn(dz, qb)
                    out.append((before + jnp.sum(g, axis=1, keepdims=True), dq + _dot(dz, k_ref[pl.ds(k0, bq), hd])))
                return tuple(out)

            st2 = lax.fori_loop(st[0] + 1, qi + 1, back, ((jnp.zeros((bq, 1), F32), jnp.zeros((bq, HEAD_DIM), F32)),) * ATTN_HEADS)
            for hd, (_, dq) in zip(heads, st2):
                dq_ref[pl.ds(q0, bq), hd] = dq.astype(dq_ref.dtype)
            return 0

        lax.fori_loop(0, nq, per_q, 0)
        dk_ref[...] = dka_ref[...].astype(dk_ref.dtype)
        dv_ref[...] = dva_ref[...].astype(dv_ref.dtype)

    groups = n_heads // ATTN_HEADS
    wide = ATTN_HEADS * HEAD_DIM
    hs = lambda off: pl.BlockSpec((t, wide), lambda h: (0, off + h))
    return _call(
        comm, body, grid=(groups,), in_specs=[hs(0), hs(groups), hs(2 * groups), hs(0)], out_specs=[hs(0), hs(0), hs(0)],
        out_shape=[S((t, n_heads * HEAD_DIM), _MXU)] * 3,
        scratch_shapes=[pltpu.VMEM((t, wide), F32), pltpu.VMEM((t, wide), F32),
                        pltpu.VMEM((ATTN_HEADS, t, bq), F32), pltpu.VMEM((ATTN_HEADS, t, bq), F32)],
        compiler_params=_cp("parallel"), name=name)(proj, proj, proj, do)


def _shift_down(cur, prev8, k):
    if k == 0:
        return cur
    row8 = lax.broadcasted_iota(jnp.int32, prev8.shape, 0)
    rc = pltpu.roll(cur, k, 0)
    top = jnp.where(row8 < k, pltpu.roll(prev8, k, 0), rc[0:8, :])
    return jnp.concatenate([top, rc[8:, :]], axis=0)


def _shift_up(cur, next8, k):
    if k == 0:
        return cur
    n = cur.shape[0]
    row8 = lax.broadcasted_iota(jnp.int32, next8.shape, 0)
    rc = pltpu.roll(cur, n - k, 0)
    bottom = jnp.where(row8 >= 8 - k, pltpu.roll(next8, 8 - k, 0), rc[n - 8:, :])
    return jnp.concatenate([rc[:n - 8, :], bottom], axis=0)


def _lru_gates(xl, prev8, cw, cb, wr, br, wi, bi, ls):
    xs = [_shift_down(xl, prev8, CONV_WIDTH - 1 - k) for k in range(CONV_WIDTH)]
    xc = xs[0] * cw[0:1, :]
    for k in range(1, CONV_WIDTH):
        xc = xc + xs[k] * cw[k:k + 1, :]
    xc = xc + cb
    xcb = xc.astype(_MXU)
    r = jax.nn.sigmoid(_dot(xcb, wr) + br)
    i = jax.nn.sigmoid(_dot(xcb, wi) + bi)
    la = (LRU_C * r) * ls
    a = jnp.exp(la)
    mult = jnp.sqrt(-_expm1(2.0 * la))
    return xs, xc, r, i, a, mult


def _group_scan(a, b, reverse):
    n = a.shape[0]
    row = lax.broadcasted_iota(jnp.int32, a.shape, 0) % 8
    for d in (1, 2, 4):
        if reverse:
            m = row < 8 - d
            a_s, b_s = pltpu.roll(a, n - d, 0), pltpu.roll(b, n - d, 0)
        else:
            m = row >= d
            a_s, b_s = pltpu.roll(a, d, 0), pltpu.roll(b, d, 0)
        b = jnp.where(m, a * b_s + b, b)
        a = jnp.where(m, a * a_s, a)
    return a, b


def _lru_fwd(proj, col0, n_blocks, cw, cb, wr, br, wi, bi, lam, name, comm=None):
    t = proj.shape[0]
    tt = min(t, SEQ_TILE)
    nt = t // tt

    def body(xl_ref, gl_ref, cw_ref, cb_ref, wr_ref, br_ref, wi_ref, bi_ref, lam_ref, h_ref, y_ref):
        cwv, cbv, brv, biv = cw_ref[...], cb_ref[...], br_ref[...], bi_ref[...]
        wrv, wiv = wr_ref[...].astype(_MXU), wi_ref[...].astype(_MXU)
        ls = _log_sigmoid(lam_ref[...])

        def tile(ti, hin):
            t0 = pl.multiple_of(ti * tt, tt)
            p0 = pl.multiple_of(jnp.maximum(t0 - 8, 0), 8)
            prev8 = xl_ref[pl.ds(p0, 8), :] * (ti > 0).astype(F32)
            xl = xl_ref[pl.ds(t0, tt), :]
            _, xc, _, ig, a, mult = _lru_gates(xl, prev8, cwv, cbv, wrv, brv, wiv, biv, ls)
            ga, gb = _group_scan(a, mult * (ig * xc), False)
            for g in range(tt // 8):
                hg = ga[8 * g:8 * g + 8, :] * hin + gb[8 * g:8 * g + 8, :]
                h_ref[pl.ds(t0 + 8 * g, 8), :] = hg
                hin = hg[7:8, :]
            y_ref[pl.ds(t0, tt), :] = h_ref[pl.ds(t0, tt), :] * _gelu(gl_ref[pl.ds(t0, tt), :])
            return hin

        lax.fori_loop(0, nt, tile, jnp.zeros((1, HEAD_DIM), F32))

    cs = lambda off: pl.BlockSpec((t, HEAD_DIM), lambda n: (0, off + n))
    vs = pl.BlockSpec((1, HEAD_DIM), lambda n: (0, n))
    ws = pl.BlockSpec((None, HEAD_DIM, HEAD_DIM), lambda n: (n, 0, 0))
    w = n_blocks * HEAD_DIM
    return _call(
        comm, body, grid=(n_blocks,),
        in_specs=[cs(col0), cs(col0 + n_blocks), pl.BlockSpec((CONV_WIDTH, HEAD_DIM), lambda n: (0, n)), vs, ws, vs, ws, vs, vs],
        out_specs=[cs(0), cs(0)], out_shape=[S((t, w), F32), S((t, w), F32)],
        compiler_params=_cp("parallel"), name=name)(proj, proj, cw, cb, wr, br, wi, bi, lam)


def _lru_bwd(proj, col0, n_blocks, h, dyl, cw, cb, wr, br, wi, bi, lam, name, comm=None):
    t = proj.shape[0]
    tt = min(t, SEQ_TILE)
    nt = t // tt

    def body(xl_ref, gl_ref, h_ref, dy_ref, cw_ref, cb_ref, wr_ref, br_ref, wi_ref, bi_ref, lam_ref,
             dxl_ref, dgl_ref, dcw_ref, dcb_ref, dwr_ref, dbr_ref, dwi_ref, dbi_ref, dlam_ref, g_ref):
        cwv, cbv, brv, biv = cw_ref[...], cb_ref[...], br_ref[...], bi_ref[...]
        wrv, wiv = wr_ref[...].astype(_MXU), wi_ref[...].astype(_MXU)
        lamv = lam_ref[...]
        ls = _log_sigmoid(lamv)
        for ref in (dcw_ref, dcb_ref, dwr_ref, dbr_ref, dwi_ref, dbi_ref, dlam_ref):
            ref[...] = jnp.zeros_like(ref)

        def tile(s, carry):
            e_in, dxc_next8 = carry
            ti = nt - 1 - s
            t0 = pl.multiple_of(ti * tt, tt)
            p0 = pl.multiple_of(jnp.maximum(t0 - 8, 0), 8)
            first = (ti > 0).astype(F32)
            xl = xl_ref[pl.ds(t0, tt), :]
            xs, xc, r, ig, a, mult = _lru_gates(xl, xl_ref[pl.ds(p0, 8), :] * first, cwv, cbv, wrv, brv, wiv, biv, ls)
            hv = h_ref[pl.ds(t0, tt), :]
            h_before = _shift_down(hv, h_ref[pl.ds(p0, 8), :] * first, 1)
            glv = gl_ref[pl.ds(t0, tt), :]
            dyv = dy_ref[pl.ds(t0, tt), :]
            dgl_ref[pl.ds(t0, tt), :] = (dyv * hv * _gelu_grad(glv)).astype(dgl_ref.dtype)
            dh = dyv * _gelu(glv)
            row = lax.broadcasted_iota(jnp.int32, a.shape, 0)
            coef = jnp.where(row == tt - 1, 1.0, pltpu.roll(a, tt - 1, 0))
            ga, gb = _group_scan(coef, dh, True)
            gin = e_in
            for g in reversed(range(tt // 8)):
                gg = ga[8 * g:8 * g + 8, :] * gin + gb[8 * g:8 * g + 8, :]
                g_ref[8 * g:8 * g + 8, :] = gg
                gin = gg[0:1, :]
            gv = g_ref[...]
            e_out = a[0:1, :] * gv[0:1, :]
            ix = ig * xc
            dla = (gv * h_before) * a - (gv * ix) * (a * a / mult)
            dlam_ref[...] += jnp.sum(dla * (LRU_C * r), axis=0, keepdims=True)
            dpr = (dla * (LRU_C * ls)) * (r * (1.0 - r))
            dpi = (gv * mult * xc) * (ig * (1.0 - ig))
            dbr_ref[...] += jnp.sum(dpr, axis=0, keepdims=True)
            dbi_ref[...] += jnp.sum(dpi, axis=0, keepdims=True)
            xcb, dprb, dpib = xc.astype(_MXU), dpr.astype(_MXU), dpi.astype(_MXU)
            dwr_ref[...] += _dot_tn(xcb, dprb)
            dwi_ref[...] += _dot_tn(xcb, dpib)
            dxc = gv * mult * ig + _dot_nt(dprb, wrv) + _dot_nt(dpib, wiv)
            dcb_ref[...] += jnp.sum(dxc, axis=0, keepdims=True)
            dxl = None
            for k in range(CONV_WIDTH):
                dcw_ref[k:k + 1, :] += jnp.sum(dxc * xs[k], axis=0, keepdims=True)
                term = _shift_up(dxc, dxc_next8, CONV_WIDTH - 1 - k) * cwv[k:k + 1, :]
                dxl = term if dxl is None else dxl + term
            dxl_ref[pl.ds(t0, tt), :] = dxl.astype(dxl_ref.dtype)
            return e_out, dxc[0:8, :]

        lax.fori_loop(0, nt, tile, (jnp.zeros((1, HEAD_DIM), F32), jnp.zeros((8, HEAD_DIM), F32)))
        dlam_ref[...] = dlam_ref[...] * (1.0 - jax.nn.sigmoid(lamv))

    cs = lambda off: pl.BlockSpec((t, HEAD_DIM), lambda n: (0, off + n))
    vs = pl.BlockSpec((1, HEAD_DIM), lambda n: (0, n))
    ws = pl.BlockSpec((None, HEAD_DIM, HEAD_DIM), lambda n: (n, 0, 0))
    cws = pl.BlockSpec((CONV_WIDTH, HEAD_DIM), lambda n: (0, n))
    w = n_blocks * HEAD_DIM
    vec = S((1, w), F32)
    mat = S((n_blocks, HEAD_DIM, HEAD_DIM), F32)
    return _call(
        comm, body, grid=(n_blocks,),
        in_specs=[cs(col0), cs(col0 + n_blocks), cs(0), cs(0), cws, vs, ws, vs, ws, vs, vs],
        out_specs=[cs(0), cs(0), cws, vs, ws, vs, ws, vs, vs],
        out_shape=[S((t, w), _MXU), S((t, w), _MXU), S((CONV_WIDTH, w), F32), vec, mat, vec, mat, vec, vec],
        scratch_shapes=[pltpu.VMEM((tt, HEAD_DIM), F32)],
        compiler_params=_cp("parallel"), name=name)(proj, proj, h, dyl, cw, cb, wr, br, wi, bi, lam)


class _NoExchange:
    def __init__(self, weights):
        self.weights, self.grads, self.packs = weights, {}, {}

    def weight(self, name):
        return self.weights[name]

    def carrier(self, call):
        return None

    def harvest(self, car):
        pass

    def alone(self, call):
        pass


def _local_step(x, target, norms, ex, cw, cb, wr, br, wi, bi, lam, ga, gl):
    g_pre_mix, g_post_mix, g_pre_ffn, g_post_ffn = norms
    t, d = x.shape
    bm = min(t, 512)
    bt = min(t, 2048)

    def run(fn, name, *args, **kw):
        car = ex.carrier(name)
        out = fn(*args, name=name, comm=car, **kw)
        ex.harvest(car)
        return out

    hn1 = run(_rms_fwd, "rms1", x, g_pre_mix)
    win3 = ex.weight("w_in")
    c = win3.shape[0]
    proj, proj_mx = run(_mm_nn, "in_proj", hn1, win3, bm=bm, bn=win3.shape[2], also=_MXU)
    o = run(_attn_fwd, "attn_fwd", proj_mx, (proj.shape[1] - d) // 3 // HEAD_DIM)
    mix = 2 * o.shape[1]
    n_heads = n_blocks = o.shape[1] // HEAD_DIM
    h, yl = run(_lru_fwd, "lru_fwd", proj, 3 * n_heads, n_blocks, cw, cb, wr, br, wi, bi, lam)
    y = _outnorm_fwd(o, yl, ga, gl, "outnorm_fwd")
    wout = ex.weight("w_out")
    mixo = run(_mm_nn, "out_proj", y, wout[None], bm=bm, bn=d)
    x2, hn2 = run(_mid_fwd, "mid_fwd", x, mixo, g_post_mix, g_pre_ffn)
    wg3, wu3 = ex.weight("w_ffn_gate"), ex.weight("w_ffn_up")
    gate, up, act = run(_swiglu_fwd, "ffn_gate_up", hn2, wg3, wu3, bm=min(t, 256))
    ex.alone("gather_w_down")
    wd = ex.weight("w_ffn_down")
    ff = wd.shape[0]
    f = _mm_nn(act, wd[None], bm=bm, bn=d // 2, name="ffn_down")
    loss_cols, dout, df, dg_post_ffn = _final(f, x2, target, g_post_ffn, "final")

    dgate, dup = _swiglu_bwd(df, wd, gate, up, bm=bm, bo=ff // 4, name="ffn_down_bwd")
    ex.grads["w_ffn_down"] = _mm_tn(act, df, 1, bm=bt, bk=512, name="ffn_down_dw").reshape(c, ff // c, d)
    ex.grads["w_ffn_gate"] = run(_mm_tn, "ffn_gate_dw", hn2, dgate, c, bm=bt, bk=d // 2)
    ex.grads["w_ffn_up"] = run(_mm_tn, "ffn_up_dw", hn2, dup, c, bm=bt, bk=d // 2)
    dhn2_g = run(_mm_nt, "ffn_gate_dx", dgate, wg3, bm=bm, bo=d // 2, out_dtype=F32)
    dhn2_u = run(_mm_nt, "ffn_up_dx", dup, wu3, bm=bm, bo=d // 2, out_dtype=F32)
    dx2, dmix, dg_pre_ffn, dg_post_mix = run(_mid_bwd, "mid_bwd", dhn2_g, dhn2_u, dout, x2, mixo, g_pre_ffn, g_post_mix)
    dy = _mm_nt(dmix, wout[None], bm=bm, bo=mix, out_dtype=F32, name="out_proj_dx")
    ex.grads["w_out"] = _mm_tn(y, dmix, 1, bm=bt, bk=mix // 4, name="out_proj_dw").reshape(c, mix // c, d)
    do, dyl, dga, dgl_norm = run(_outnorm_bwd, "outnorm_bwd", dy, o, yl, ga, gl)
    dxl, dglu, dcw, dcb, dwr, dbr, dwi, dbi, dlam = run(_lru_bwd, "lru_bwd", proj, 3 * n_heads, n_blocks, h, dyl, cw, cb, wr, br, wi, bi, lam)
    small = dict(post_mix_norm=dg_post_mix, pre_ffn_norm=dg_pre_ffn, post_ffn_norm=dg_post_ffn, conv_w=dcw, conv_b=dcb,
                 w_rgate=dwr, b_rgate=dbr, w_igate=dwi, b_igate=dbi, lru_lambda=dlam, attn_out_norm=dga, lru_out_norm=dgl_norm)
    ex.packs["early"] = _pack([small[n] for n in _SMALL_EARLY])
    dq, dk, dv = run(_attn_bwd, "attn_bwd", proj_mx, do, n_heads)
    dproj = jnp.concatenate([dq, dk, dv, dxl, dglu], axis=1)
    ex.grads["w_in"] = _mm_tn(hn1, dproj, c, bm=bt, bk=d // 2, name="in_proj_dw")
    dhn1 = run(_mm_nt, "in_proj_dx", dproj, win3, bm=bm, bo=d // 2, out_dtype=F32)
    grad_x, small["pre_mix_norm"] = run(_first_bwd, "first_bwd", dhn1, dx2, x, g_pre_mix)
    ex.packs["late"] = _pack([small["pre_mix_norm"]])
    return loss_cols, grad_x, small


def _into_slot(wsh, slot, dtype, name):
    rows, n = wsh.shape
    rb = _row_block(rows, 256) if rows % 8 == 0 else rows

    def body(s_ref, w_ref, o_ref):
        o_ref[...] = w_ref[...].astype(o_ref.dtype)

    return pl.pallas_call(
        body,
        grid_spec=pltpu.PrefetchScalarGridSpec(
            num_scalar_prefetch=1, grid=(rows // rb,),
            in_specs=[pl.BlockSpec((rb, n), lambda i, s_ref: (i, 0))],
            out_specs=pl.BlockSpec((None, rb, n), lambda i, s_ref: (s_ref[0], i, 0))),
        out_shape=S((4, rows, n), dtype), compiler_params=_cp("parallel"), name=name)(slot, wsh)


class _Exchange:
    SCHEDULE = {
        "gather_w_in": [("ici", "w_in"), ("ici", "conv_w")],
        "rms1": [("d2d", "w_in")],
        "in_proj": [("ici", "w_out")],
        "attn_fwd": [("d2d", "w_out"), ("ici", "w_ffn_gate"), ("ici", "w_ffn_up", 0)],
        "lru_fwd": [("d2d", "w_ffn_gate"), ("d2d", "w_ffn_up", 0), ("ici", "w_ffn_up", 1), ("ici", "w_ffn_up", 2)],
        "out_proj": [("d2d", "w_ffn_up", 1), ("d2d", "w_ffn_up", 2), ("ici", "w_ffn_up", 3)],
        "mid_fwd": [("d2d", "w_ffn_up", 3)],
        "ffn_gate_up": [("ici", "w_ffn_down")],
        "gather_w_down": [("d2d", "w_ffn_down")],
        "ffn_gate_dw": [("swap", "w_ffn_down")],
        "ffn_up_dw": [("scatter", "w_ffn_down"), ("swap", "w_ffn_gate")],
        "ffn_gate_dx": [("share", "w_ffn_down"), ("scatter", "w_ffn_gate"), ("swap", "w_ffn_up")],
        "ffn_up_dx": [("share", "w_ffn_gate"), ("scatter", "w_ffn_up")],
        "mid_bwd": [("share", "w_ffn_up")],
        "outnorm_bwd": [("swap", "w_out")],
        "lru_bwd": [("scatter", "w_out")],
        "attn_bwd": [("share", "w_out"), ("spread", "early")],
        "in_proj_dx": [("swap", "w_in")],
        "first_bwd": [("scatter", "w_in", 0)],
        "adamw_w_ffn_down": [("scatter", "w_in", 1)],
        "adamw_w_ffn_gate": [("scatter", "w_in", 2)],
        "adamw_w_ffn_up": [("scatter", "w_in", 3)],
        "grads_w_in_share": [("share", "w_in"), ("spread", "late")],
    }
    PIECES = 4

    def __init__(self, slots, place):
        self.buf, self.place = dict(slots), place
        self.grads, self.packs, self.swapped, self.part, self.scattered, self.full, self.spreaded = {}, {}, {}, {}, {}, {}, {}

    def weight(self, name):
        b = self.buf[name]
        return b.reshape(-1, b.shape[2]) if name in ("w_out", "w_ffn_down") else b

    def carrier(self, call):
        car = _Carrier()
        car.todo, slot = [], {}
        for kind, name, *piece in self.SCHEDULE[call]:
            if kind in ("ici", "d2d"):
                if name not in slot:
                    slot[name] = car.inplace(self.buf[name])
                    car.todo.append((self.buf, name, slot[name]))
                size = self.buf[name].shape[1] // 2 // self.PIECES
                rows = (piece[0] * size, size) if piece else None
                if kind == "ici":
                    car.gather_ici(slot[name], rows, split=name != "conv_w")
                else:
                    car.gather_d2d(slot[name], rows)
            elif kind == "swap":
                g = self.grads[name]
                o = car.fresh((4, g.shape[1] // 2, g.shape[2]), F32)
                car.swap(car.read(g), o)
                car.todo.append((self.swapped, name, o))
            elif kind == "scatter":
                if name not in self.part:
                    self.part[name] = _add_own_half(self.grads[name], self.swapped[name], self.place[1:], "grads_add_" + name)
                p = self.part[name]
                o = car.inplace(self.scattered[name]) if name in self.scattered else car.fresh(p.shape, p.dtype)
                size = p.shape[1] // self.PIECES
                car.scatter(car.read(p), o, (piece[0] * size, size) if piece else None)
                car.todo.append((self.scattered, name, o))
            elif kind == "share":
                o = car.inplace(_sum_chips(self.part[name], self.scattered[name], self.place, "grads_sum_" + name))
                car.share(o)
                car.todo.append((self.full, name, o))
            else:
                o = car.fresh((8,) + self.packs[name].shape, F32)
                car.spread(car.read(self.packs[name]), o)
                car.todo.append((self.spreaded, name, o))
        return car

    def harvest(self, car):
        for state, name, o in car.todo:
            state[name] = car.results[o]

    def alone(self, call):
        car = self.carrier(call)
        car.run_alone(call)
        self.harvest(car)

    def small_sum(self, key):
        return _sum_devices(self.packs[key], self.spreaded[key], 2 * self.place[0:1] + self.place[1:], "grads_small_sum_" + key)


def _row_block(rows, cap):
    return max(b for b in range(8, cap + 1, 8) if rows % b == 0)


def _add_own_half(g, recv, core, name):
    _, rows, n = g.shape
    half = rows // 2
    rb = _row_block(half, 512)
    nb = half // rb

    def body(c_ref, g_ref, r_ref, o_ref):
        o_ref[...] = (g_ref[...] + r_ref[...]).astype(o_ref.dtype)

    return pl.pallas_call(
        body,
        grid_spec=pltpu.PrefetchScalarGridSpec(
            num_scalar_prefetch=1, grid=(4, nb),
            in_specs=[pl.BlockSpec((None, rb, n), lambda k, i, c_ref: (k, c_ref[0] * nb + i, 0)),
                      pl.BlockSpec((None, rb, n), lambda k, i, c_ref: (k, i, 0))],
            out_specs=pl.BlockSpec((None, rb, n), lambda k, i, c_ref: (k, i, 0))),
        out_shape=S((4, half, n), BF16), compiler_params=_cp("parallel", "parallel"), name=name)(core, g, recv)


def _sum_chips(part, recv, place, name):
    _, rows, n = part.shape
    rb = _row_block(rows, 64)
    nb = rows // rb

    def body(p_ref, own_ref, r0, r1, r2, r3, o_ref):
        own = own_ref[...].astype(F32)
        terms = [jnp.where(p_ref[0] == k, own, r[...].astype(F32)) for k, r in enumerate((r0, r1, r2, r3))]
        o_ref[...] = ((terms[0] + terms[1]) + terms[2]) + terms[3]

    def slot(k):
        return pl.BlockSpec((None, rb, n), lambda i, p_ref: (jnp.where(p_ref[0] == k, (k + 1) % 4, k), i, 0))

    return pl.pallas_call(
        body,
        grid_spec=pltpu.PrefetchScalarGridSpec(
            num_scalar_prefetch=1, grid=(nb,),
            in_specs=[pl.BlockSpec((None, rb, n), lambda i, p_ref: (p_ref[0], i, 0))] + [slot(k) for k in range(4)],
            out_specs=pl.BlockSpec((rb, n), lambda i, p_ref: (p_ref[1] * nb + i, 0))),
        out_shape=S((2 * rows, n), F32), compiler_params=_cp("parallel"), name=name)(place, part, recv, recv, recv, recv)


def _handover(arrays, name):
    n = len(arrays)

    def body(*refs):
        pass

    return pl.pallas_call(body, in_specs=[_ANY] * n, out_specs=[_ANY] * n, out_shape=[S(a.shape, a.dtype) for a in arrays],
                          input_output_aliases={i: i for i in range(n)}, name=name)(*arrays)


def _sum_devices(own, spread, me, name):
    rows = own.shape[0]

    def body(me_ref, own_ref, *refs):
        acc = None
        for k, r in enumerate(refs[:8]):
            term = jnp.where(me_ref[0] == k, own_ref[...], r[...])
            acc = term if acc is None else acc + term
        refs[8][...] = acc

    def slot(k):
        return pl.BlockSpec((None, rows, 128), lambda i, me_ref: (jnp.where(me_ref[0] == k, (k + 1) % 8, k), 0, 0))

    whole = pl.BlockSpec((rows, 128), lambda i, me_ref: (0, 0))
    return pl.pallas_call(
        body,
        grid_spec=pltpu.PrefetchScalarGridSpec(num_scalar_prefetch=1, grid=(1,), in_specs=[whole] + [slot(k) for k in range(8)],
                                               out_specs=whole),
        out_shape=S((rows, 128), F32), compiler_params=_cp("arbitrary"), name=name)(me, own, *[spread] * 8)


def _adamw(w, g, m, v, name, comm=None):
    rows, n = w.shape
    rb = rows if rows * n * 4 <= (1 << 21) else _row_block(rows, 128)
    c1 = 1.0 - ADAM_B1 ** ADAM_STEP
    c2 = 1.0 - ADAM_B2 ** ADAM_STEP

    def body(w_ref, g_ref, m_ref, v_ref, d_ref, nm_ref, nv_ref):
        gv = g_ref[...]
        nm = ADAM_B1 * m_ref[...] + (1.0 - ADAM_B1) * gv
        nv = ADAM_B2 * v_ref[...] + (1.0 - ADAM_B2) * (gv * gv)
        nm_ref[...] = nm
        nv_ref[...] = nv
        d_ref[...] = -ADAM_LR * ((nm / c1) / (jnp.sqrt(nv / c2) + ADAM_EPS) + ADAM_WD * w_ref[...])

    bs = pl.BlockSpec((rb, n), lambda i: (i, 0))
    return _call(comm, body, grid=(rows // rb,), in_specs=[bs] * 4, out_specs=[bs] * 3, out_shape=[S((rows, n), F32)] * 3,
                 compiler_params=_cp("parallel"), name=name)(w, g, m, v)


_BIG = ("w_in", "w_out", "w_ffn_gate", "w_ffn_up", "w_ffn_down")
_SMALL = ("pre_mix_norm", "post_mix_norm", "pre_ffn_norm", "post_ffn_norm", "conv_w", "conv_b", "w_rgate", "b_rgate",
          "w_igate", "b_igate", "lru_lambda", "attn_out_norm", "lru_out_norm")
_SMALL_EARLY = _SMALL[1:]
_WEIGHTS = ("pre_mix_norm", "post_mix_norm", "pre_ffn_norm", "post_ffn_norm", "w_in", "conv_w", "conv_b", "w_rgate", "b_rgate",
            "w_igate", "b_igate", "lru_lambda", "attn_out_norm", "lru_out_norm", "w_out", "w_ffn_gate", "w_ffn_up", "w_ffn_down")


def _pack(arrays):
    flat = []
    for a in arrays:
        f = a.reshape(-1)
        flat.append(jnp.pad(f, (0, (-f.shape[0]) % 1024)))
    return jnp.concatenate(flat).reshape(-1, 128)


def _unpack(packed, shapes):
    out, pos = [], 0
    flat = packed.reshape(-1)
    for s in shapes:
        size = math.prod(s)
        out.append(flat[pos:pos + size].reshape(s))
        pos += size + (-size) % 1024
    return out


def kernel(x, pre_mix_norm, post_mix_norm, pre_ffn_norm, post_ffn_norm, w_in, conv_w, conv_b, w_rgate, b_rgate, w_igate, b_igate, lru_lambda, attn_out_norm, lru_out_norm, w_out, w_ffn_gate, w_ffn_up, w_ffn_down, loss_target, m_pre_mix_norm, m_post_mix_norm, m_pre_ffn_norm, m_post_ffn_norm, m_w_in, m_conv_w, m_conv_b, m_w_rgate, m_b_rgate, m_w_igate, m_b_igate, m_lru_lambda, m_attn_out_norm, m_lru_out_norm, m_w_out, m_w_ffn_gate, m_w_ffn_up, m_w_ffn_down, v_pre_mix_norm, v_post_mix_norm, v_pre_ffn_norm, v_post_ffn_norm, v_w_in, v_conv_w, v_conv_b, v_w_rgate, v_b_rgate, v_w_igate, v_b_igate, v_lru_lambda, v_attn_out_norm, v_lru_out_norm, v_w_out, v_w_ffn_gate, v_w_ffn_up, v_w_ffn_down):
    given = dict(locals())
    w = {n: given[n][0] for n in _WEIGHTS}
    m = {n: given["m_" + n][0] for n in _WEIGHTS}
    v = {n: given["v_" + n][0] for n in _WEIGHTS}
    xs, target = x[0], loss_target[0]
    d = xs.shape[1]
    chip = (2 * lax.axis_index("x") + lax.axis_index("y")).astype(jnp.int32)
    place = jnp.stack([chip, lax.axis_index("c").astype(jnp.int32)])

    slots = {n: _into_slot(w[n], place[0:1], _MXU, "slot_" + n) for n in _BIG}
    slots["conv_w"] = _into_slot(w["conv_w"], place[0:1], F32, "slot_conv_w")
    ex = _Exchange(slots, place)
    ex.alone("gather_w_in")
    conv_full = jnp.transpose(ex.buf["conv_w"], (1, 0, 2)).reshape(CONV_WIDTH, -1)
    row = lambda a: a.reshape(1, -1)
    norms = tuple(row(w[n]) for n in ("pre_mix_norm", "post_mix_norm", "pre_ffn_norm", "post_ffn_norm"))

    loss_cols, grad_x, small = _local_step(
        xs, target, norms, ex, conv_full, row(w["conv_b"]), w["w_rgate"], row(w["b_rgate"]),
        w["w_igate"], row(w["b_igate"]), row(w["lru_lambda"]), row(w["attn_out_norm"]), row(w["lru_out_norm"]))

    loss = lax.psum(0.5 * jnp.sum(loss_cols) / d, ("x", "y", "c"))

    delta, new_m, new_v = {}, {}, {}
    for n in ("w_ffn_down", "w_ffn_gate", "w_ffn_up"):
        car = ex.carrier("adamw_" + n)
        delta[n], new_m[n], new_v[n] = _adamw(w[n], ex.full[n], m[n], v[n], "adamw_" + n, comm=car)
        ex.harvest(car)
    ex.alone("grads_w_in_share")
    reduced = {n: ex.full[n] for n in _BIG}
    early = _unpack(ex.small_sum("early"), [small[n].shape for n in _SMALL_EARLY])
    late = _unpack(ex.small_sum("late"), [small["pre_mix_norm"].shape])
    for n, g in zip(_SMALL_EARLY + ("pre_mix_norm",), early + late):
        reduced[n] = g.reshape(w[n].shape) if n != "conv_w" else lax.dynamic_slice_in_dim(g, chip * w[n].shape[1], w[n].shape[1], axis=1)

    for n in ("w_in", "w_out"):
        delta[n], new_m[n], new_v[n] = _adamw(w[n], reduced[n], m[n], v[n], "adamw_" + n)
    shapes = [w[n].shape for n in _SMALL]
    packed = _adamw(*[_pack([src[n] for n in _SMALL]) for src in (w, reduced, m, v)], "adamw_small")
    for out, p in zip((delta, new_m, new_v), packed):
        out.update(zip(_SMALL, _unpack(p, shapes)))

    carried = ("w_ffn_down", "w_ffn_gate", "w_ffn_up")
    handed = _handover([grad_x] + [reduced[n] for n in _BIG] + [src[n] for src in (delta, new_m, new_v) for n in carried], "handover")
    grad_x = handed[0]
    reduced.update(zip(_BIG, handed[1:6]))
    for k, src in enumerate((delta, new_m, new_v)):
        src.update(zip(carried, handed[6 + 3 * k:9 + 3 * k]))

    lead = lambda a: a[None]
    return (loss, lead(grad_x), *[lead(reduced[n]) for n in _WEIGHTS], *[lead(delta[n]) for n in _WEIGHTS],
            *[lead(new_m[n]) for n in _WEIGHTS], *[lead(new_v[n]) for n in _WEIGHTS])
```

```python
import functools
import math

import jax
import jax.numpy as jnp
from jax import lax
from jax.experimental import pallas as pl
from jax.experimental.pallas import tpu as pltpu

F32 = jnp.float32
BF16 = jnp.bfloat16
_MXU = BF16
S = jax.ShapeDtypeStruct

RMS_EPS = 1e-6
HEAD_DIM = 128
CONV_WIDTH = 4
LRU_C = 8.0
ADAM_LR, ADAM_B1, ADAM_B2, ADAM_EPS, ADAM_WD, ADAM_STEP = 0.001, 0.9, 0.999, 1e-08, 0.01, 10
EXP_CUT = -105.0
VMEM_LIMIT = 60 * 1024 * 1024
ROW_TILE = 256
SEQ_TILE = 256
ATTN_BLOCK = 256
ATTN_HEADS = 2
MESH = pl.DeviceIdType.MESH


def _cp(*sem):
    return pltpu.CompilerParams(dimension_semantics=sem, vmem_limit_bytes=VMEM_LIMIT)


def _dot(a, b):
    return jnp.dot(a, b, preferred_element_type=F32)


def _dot_nt(a, b):
    return lax.dot_general(a, b, (((1,), (1,)), ((), ())), preferred_element_type=F32)


def _dot_tn(a, b):
    return lax.dot_general(a, b, (((0,), (0,)), ((), ())), preferred_element_type=F32)


def _rstd(v):
    return lax.rsqrt(jnp.mean(v * v, axis=-1, keepdims=True) + RMS_EPS)


def _rms_bwd(dn, vh, r, gain):
    dvh = dn * gain
    dv = r * (dvh - vh * jnp.mean(dvh * vh, axis=-1, keepdims=True))
    return dv, jnp.sum(dn * vh, axis=0, keepdims=True)


def _log_sigmoid(z):
    return jnp.minimum(z, 0.0) - jnp.log(1.0 + jnp.exp(-jnp.abs(z)))


def _expm1(v):
    small = v * (1.0 + v * (0.5 + v * (1.0 / 6.0 + v * (1.0 / 24.0 + v * (1.0 / 120.0)))))
    return jnp.where(jnp.abs(v) < 0.04, small, jnp.exp(v) - 1.0)


_GELU_C = math.sqrt(2.0 / math.pi)


def _gelu(v):
    return 0.5 * v * (1.0 + jnp.tanh(_GELU_C * (v + 0.044715 * v * v * v)))


def _gelu_grad(v):
    th = jnp.tanh(_GELU_C * (v + 0.044715 * v * v * v))
    return 0.5 * (1.0 + th) + 0.5 * v * (1.0 - th * th) * _GELU_C * (1.0 + 3.0 * 0.044715 * v * v)


def _row_spec(tm, d):
    return pl.BlockSpec((tm, d), lambda i: (i, 0))


def _vec_spec(d):
    return pl.BlockSpec((1, d), lambda i: (0, 0))


_ANY = pl.BlockSpec(memory_space=pl.ANY)


def _place():
    x, y, c = lax.axis_index("x"), lax.axis_index("y"), lax.axis_index("c")
    return x, y, c, [(1 - x, y), (x, 1 - y), (1 - x, 1 - y)]


def _remote(src, dst, send_sem, recv_sem, to):
    return pltpu.make_async_remote_copy(src_ref=src, dst_ref=dst, send_sem=send_sem, recv_sem=recv_sem,
                                        device_id=to, device_id_type=MESH)


class _Carrier:
    def __init__(self):
        self.inputs, self.out_shapes, self.aliases, self.ops, self.n_sems, self.results = [], [], {}, [], 0, None

    def inplace(self, arr):
        self.aliases[len(self.inputs)] = len(self.out_shapes)
        self.inputs.append(arr)
        self.out_shapes.append(S(arr.shape, arr.dtype))
        return len(self.out_shapes) - 1

    def read(self, arr):
        self.inputs.append(arr)
        return len(self.inputs) - 1

    def fresh(self, shape, dtype):
        self.out_shapes.append(S(shape, dtype))
        return len(self.out_shapes) - 1

    def _add(self, n_sems, copies):
        base = self.n_sems
        self.n_sems += n_sems

        def start(ins, outs, send, recv):
            for k, (src, dst, _, to) in enumerate(copies(ins, outs)):
                _remote(src, dst, send.at[base + k], recv.at[base + k], to).start()

        def finish(ins, outs, send, recv):
            for k, (src, _, land, to) in enumerate(copies(ins, outs)):
                _remote(src, land, send.at[base + k], recv.at[base + k], to).wait()

        self.ops.append((start, finish))

    def gather_ici(self, o, rows=None, split=True):
        half = self.out_shapes[o].shape[1] // 2
        lo, size = rows or (0, half)

        def copies(ins, outs):
            x, y, c, chips = _place()
            part = (lambda ref: ref.at[pl.ds(c * half + lo, size)]) if split else (lambda ref: ref)
            mine = part(outs[o].at[2 * x + y])
            return [(mine, mine, part(outs[o].at[2 * px + py]), (px, py, c)) for px, py in chips]

        self._add(3, copies)

    def gather_d2d(self, o, rows=None):
        half = self.out_shapes[o].shape[1] // 2
        lo, size = rows or (0, half)

        def copies(ins, outs):
            x, y, c, chips = _place()
            at = lambda k, cc: outs[o].at[k].at[pl.ds(cc * half + lo, size)]
            return [(at(2 * px + py, c), at(2 * px + py, c), at(2 * px + py, 1 - c), (x, y, 1 - c)) for px, py in chips]

        self._add(3, copies)

    def swap(self, i, o):
        half = self.inputs[i].shape[1] // 2

        def copies(ins, outs):
            x, y, c, _ = _place()
            return [(ins[i].at[:, pl.ds((1 - c) * half, half)], outs[o], outs[o], (x, y, 1 - c))]

        self._add(1, copies)

    def scatter(self, i, o, rows=None):
        lo, size = rows or (0, self.inputs[i].shape[1])

        def copies(ins, outs):
            x, y, c, chips = _place()
            cut = lambda ref: ref.at[pl.ds(lo, size)]
            return [(cut(ins[i].at[2 * px + py]), cut(outs[o].at[2 * x + y]), cut(outs[o].at[2 * px + py]), (px, py, c)) for px, py in chips]

        self._add(3, copies)

    def share(self, o):
        r = self.out_shapes[o].shape[0] // 2

        def copies(ins, outs):
            x, y, c, _ = _place()
            mine = outs[o].at[pl.ds(c * r, r)]
            return [(mine, mine, outs[o].at[pl.ds((1 - c) * r, r)], (x, y, 1 - c))]

        self._add(1, copies)

    def spread(self, i, o):
        def copies(ins, outs):
            x, y, c, _ = _place()
            me = 4 * x + 2 * y + c
            out = []
            for d in range(1, 8):
                to, frm = (me + d) % 8, (me + 8 - d) % 8
                out.append((ins[i], outs[o].at[me], outs[o].at[frm], (to // 4, (to // 2) % 2, to % 2)))
            return out

        self._add(7, copies)

    def _pallas(self, body, n_in, n_out, scratch, **kw):
        k_in, k_out = len(self.inputs), len(self.out_shapes)
        grid = kw.get("grid", ())

        def wrapped(*refs):
            ins, cins = refs[:n_in], refs[n_in:n_in + k_in]
            outs = refs[n_in + k_in:n_in + k_in + n_out]
            couts = refs[n_in + k_in + n_out:n_in + k_in + n_out + k_out]
            own = refs[n_in + k_in + n_out + k_out:]
            send, recv = own[len(scratch):]
            ids = [pl.program_id(a) for a in range(len(grid))]
            first = functools.reduce(jnp.logical_and, [a == 0 for a in ids], True)
            last = functools.reduce(jnp.logical_and, [a == g - 1 for a, g in zip(ids, grid)], True)

            def go(stage):
                for op in self.ops:
                    op[stage](cins, couts, send, recv)

            if grid:
                pl.when(first)(lambda: go(0))
                body(*ins, *outs, *own[:len(scratch)])
                pl.when(last)(lambda: go(1))
            else:
                go(0)
                go(1)

        sem = pltpu.SemaphoreType.DMA((self.n_sems,))
        return pl.pallas_call(
            wrapped, in_specs=list(kw.get("in_specs", [])) + [_ANY] * k_in, out_specs=list(kw.get("out_specs", [])) + [_ANY] * k_out,
            out_shape=list(kw.get("out_shape", [])) + self.out_shapes, scratch_shapes=list(scratch) + [sem, sem],
            input_output_aliases={n_in + i: n_out + o for i, o in self.aliases.items()}, name=kw["name"],
            **({"grid": grid, "compiler_params": _cp(*["arbitrary"] * len(grid))} if grid else {}))

    def run(self, body, kw, *args):
        single = not isinstance(kw["out_shape"], (list, tuple))
        out_shape = [kw["out_shape"]] if single else list(kw["out_shape"])
        out_specs = [kw["out_specs"]] if single else list(kw["out_specs"])
        res = self._pallas(body, len(args), len(out_shape), kw.get("scratch_shapes", []), grid=kw["grid"], in_specs=kw["in_specs"],
                           out_specs=out_specs, out_shape=out_shape, name=kw["name"])(*args, *self.inputs)
        self.results = list(res[len(out_shape):])
        return res[0] if single else list(res[:len(out_shape)])

    def run_alone(self, name):
        self.results = list(self._pallas(None, 0, 0, [], name=name)(*self.inputs))


def _call(comm, body, **kw):
    if comm is None:
        return pl.pallas_call(body, **kw)
    return functools.partial(comm.run, body, kw)


def _mm_nn(a, b3, *, bm, bn, name, also=None, comm=None):
    m, k = a.shape
    c, _, n = b3.shape
    ni, nj = m // bm, n // bn

    def body(a_ref, b_ref, *o_refs):
        res = _dot(a_ref[...], b_ref[...])
        for o_ref in o_refs:
            o_ref[...] = res.astype(o_ref.dtype)

    ospec = pl.BlockSpec((bm, bn), lambda cc, j, i: (i, cc * nj + j))
    dtypes = [F32] + ([] if also is None else [also])
    out = _call(
        comm, body, grid=(c, nj, ni),
        in_specs=[pl.BlockSpec((bm, k), lambda cc, j, i: (i, 0)), pl.BlockSpec((None, k, bn), lambda cc, j, i: (cc, 0, j))],
        out_specs=[ospec] * len(dtypes), out_shape=[S((m, c * n), dt) for dt in dtypes],
        compiler_params=_cp("parallel", "parallel", "parallel"), name=name)(a, b3)
    return out[0] if also is None else out


def _mm_nt(a, b3, *, bm, bo, out_dtype, name, comm=None):
    m = a.shape[0]
    c, ko, n = b3.shape
    ni, nj = m // bm, ko // bo

    def body(a_ref, b_ref, o_ref):
        acc = _dot_nt(a_ref[:, 0:n], b_ref[0])
        for cc in range(1, c):
            acc = acc + _dot_nt(a_ref[:, cc * n:(cc + 1) * n], b_ref[cc])
        o_ref[...] = acc.astype(o_ref.dtype)

    return _call(
        comm, body, grid=(nj, ni),
        in_specs=[pl.BlockSpec((bm, c * n), lambda j, i: (i, 0)),
                  pl.BlockSpec((c, bo, n), lambda j, i: (0, j, 0))],
        out_specs=pl.BlockSpec((bm, bo), lambda j, i: (i, j)),
        out_shape=S((m, ko), out_dtype),
        compiler_params=_cp("parallel", "parallel"), name=name)(a, b3)


def _mm_tn(a, b, c, *, bm, bk, name, comm=None):
    m, k = a.shape
    n = b.shape[1] // c
    nm, nk = m // bm, k // bk

    def body(a_ref, b_ref, o_ref, acc):
        mm = pl.program_id(2)

        @pl.when(mm == 0)
        def _():
            acc[...] = jnp.zeros_like(acc)

        acc[...] += _dot_tn(a_ref[...], b_ref[...])

        @pl.when(mm == nm - 1)
        def _():
            o_ref[...] = acc[...]

    return _call(
        comm, body, grid=(c, nk, nm),
        in_specs=[pl.BlockSpec((bm, bk), lambda cc, j, mm: (mm, j)),
                  pl.BlockSpec((bm, n), lambda cc, j, mm: (mm, cc))],
        out_specs=pl.BlockSpec((None, bk, n), lambda cc, j, mm: (cc, j, 0)),
        out_shape=S((c, k, n), F32),
        scratch_shapes=[pltpu.VMEM((bk, n), F32)],
        compiler_params=_cp("parallel", "parallel", "arbitrary"), name=name)(a, b)


def _lane_pieces(n, parts):
    base, extra = divmod(n // 128, parts)
    sizes = [128 * (base + (p < extra)) for p in range(parts)]
    return [slice(sum(sizes[:p]), sum(sizes[:p + 1])) for p in range(parts) if sizes[p]]


def _swiglu_fwd(hn, wg3, wu3, *, bm, name, comm=None):
    m, k = hn.shape
    c, _, n = wg3.shape

    def body(a_ref, g_ref, u_ref, gate_ref, up_ref, act_ref):
        a = a_ref[...]
        gate = _dot(a, g_ref[...])
        up = _dot(a, u_ref[...])
        gate_ref[...] = gate
        up_ref[...] = up
        act_ref[...] = (gate * jax.nn.sigmoid(gate) * up).astype(act_ref.dtype)

    wspec = pl.BlockSpec((None, k, n), lambda cc, i: (cc, 0, 0))
    ospec = pl.BlockSpec((bm, n), lambda cc, i: (i, cc))
    return _call(
        comm, body, grid=(c, m // bm),
        in_specs=[pl.BlockSpec((bm, k), lambda cc, i: (i, 0)), wspec, wspec],
        out_specs=[ospec, ospec, ospec],
        out_shape=[S((m, c * n), F32), S((m, c * n), F32), S((m, c * n), _MXU)],
        compiler_params=_cp("parallel", "parallel"), name=name)(hn, wg3, wu3)


def _swiglu_bwd(df, wd, gate, up, *, bm, bo, name):
    m, k = df.shape
    ko = wd.shape[0]

    def body(a_ref, b_ref, g_ref, u_ref, dg_ref, du_ref):
        a = a_ref[...]
        for cols in _lane_pieces(bo, 4):
            dact = _dot_nt(a, b_ref[cols, :])
            gate = g_ref[:, cols]
            sg = jax.nn.sigmoid(gate)
            dg_ref[:, cols] = (dact * u_ref[:, cols] * (sg * (1.0 + gate * (1.0 - sg)))).astype(dg_ref.dtype)
            du_ref[:, cols] = (dact * (gate * sg)).astype(du_ref.dtype)

    ospec = pl.BlockSpec((bm, bo), lambda j, i: (i, j))
    return pl.pallas_call(
        body, grid=(ko // bo, m // bm),
        in_specs=[pl.BlockSpec((bm, k), lambda j, i: (i, 0)), pl.BlockSpec((bo, k), lambda j, i: (j, 0)), ospec, ospec],
        out_specs=[ospec, ospec],
        out_shape=[S((m, ko), _MXU), S((m, ko), _MXU)],
        compiler_params=_cp("parallel", "parallel"), name=name)(df, wd, gate, up)


def _rms_fwd(x, gain, name, comm=None):
    t, d = x.shape
    tm = min(t, ROW_TILE)

    def body(x_ref, g_ref, o_ref):
        xv = x_ref[...]
        o_ref[...] = ((xv * _rstd(xv)) * g_ref[...]).astype(o_ref.dtype)

    return _call(comm, body, grid=(t // tm,), in_specs=[_row_spec(tm, d), _vec_spec(d)], out_specs=_row_spec(tm, d),
                          out_shape=S((t, d), _MXU), compiler_params=_cp("parallel"), name=name)(x, gain)


def _outnorm_fwd(o, yl, ga, gl, name):
    t, w = o.shape
    tm = min(t, ROW_TILE)

    def body(o_ref, l_ref, ga_ref, gl_ref, y_ref):
        ov, lv = o_ref[...], l_ref[...]
        y_ref[:, :w] = ((ov * _rstd(ov)) * ga_ref[...]).astype(y_ref.dtype)
        y_ref[:, w:] = ((lv * _rstd(lv)) * gl_ref[...]).astype(y_ref.dtype)

    return pl.pallas_call(body, grid=(t // tm,), in_specs=[_row_spec(tm, w), _row_spec(tm, w), _vec_spec(w), _vec_spec(w)],
                          out_specs=_row_spec(tm, 2 * w), out_shape=S((t, 2 * w), _MXU),
                          compiler_params=_cp("parallel"), name=name)(o, yl, ga, gl)


def _mid_fwd(x, mix, g_post, g_pre, name, comm=None):
    t, d = x.shape
    tm = min(t, ROW_TILE)

    def body(x_ref, m_ref, gp_ref, gn_ref, x2_ref, hn_ref):
        mv = m_ref[...]
        x2 = x_ref[...] + (mv * _rstd(mv)) * gp_ref[...]
        x2_ref[...] = x2
        hn_ref[...] = ((x2 * _rstd(x2)) * gn_ref[...]).astype(hn_ref.dtype)

    return _call(comm, body, grid=(t // tm,), in_specs=[_row_spec(tm, d), _row_spec(tm, d), _vec_spec(d), _vec_spec(d)],
                          out_specs=[_row_spec(tm, d), _row_spec(tm, d)], out_shape=[S((t, d), F32), S((t, d), _MXU)],
                          compiler_params=_cp("parallel"), name=name)(x, mix, g_post, g_pre)


def _final(f, x2, target, g_post, name):
    t, d = f.shape
    tm = min(t, ROW_TILE)

    def body(f_ref, x2_ref, t_ref, g_ref, loss_ref, dout_ref, df_ref, dg_ref):
        @pl.when(pl.program_id(0) == 0)
        def _():
            loss_ref[...] = jnp.zeros_like(loss_ref)
            dg_ref[...] = jnp.zeros_like(dg_ref)

        fv = f_ref[...]
        r = _rstd(fv)
        fh = fv * r
        err = (x2_ref[...] + fh * g_ref[...]) - t_ref[...]
        loss_ref[...] += jnp.sum(err * err, axis=0, keepdims=True)
        dout = err * (1.0 / d)
        dout_ref[...] = dout
        dfv, dg = _rms_bwd(dout, fh, r, g_ref[...])
        df_ref[...] = dfv.astype(df_ref.dtype)
        dg_ref[...] += dg

    return pl.pallas_call(
        body, grid=(t // tm,),
        in_specs=[_row_spec(tm, d), _row_spec(tm, d), _row_spec(tm, d), _vec_spec(d)],
        out_specs=[_vec_spec(d), _row_spec(tm, d), _row_spec(tm, d), _vec_spec(d)],
        out_shape=[S((1, d), F32), S((t, d), F32), S((t, d), _MXU), S((1, d), F32)],
        compiler_params=_cp("arbitrary"), name=name)(f, x2, target, g_post)


def _mid_bwd(dhn_a, dhn_b, dout, x2, mix, g_pre, g_post, name, comm=None):
    t, d = x2.shape
    tm = min(t, ROW_TILE)

    def body(da_ref, db_ref, do_ref, x2_ref, m_ref, gn_ref, gp_ref, dx2_ref, dm_ref, dgn_ref, dgp_ref):
        @pl.when(pl.program_id(0) == 0)
        def _():
            dgn_ref[...] = jnp.zeros_like(dgn_ref)
            dgp_ref[...] = jnp.zeros_like(dgp_ref)

        x2 = x2_ref[...]
        r = _rstd(x2)
        dxa, dgn = _rms_bwd(da_ref[...] + db_ref[...], x2 * r, r, gn_ref[...])
        dx2 = do_ref[...] + dxa
        dx2_ref[...] = dx2
        dgn_ref[...] += dgn
        mv = m_ref[...]
        rm = _rstd(mv)
        dmv, dgp = _rms_bwd(dx2, mv * rm, rm, gp_ref[...])
        dm_ref[...] = dmv.astype(dm_ref.dtype)
        dgp_ref[...] += dgp

    rs, vs = _row_spec(tm, d), _vec_spec(d)
    return _call(
        comm, body, grid=(t // tm,), in_specs=[rs, rs, rs, rs, rs, vs, vs], out_specs=[rs, rs, vs, vs],
        out_shape=[S((t, d), F32), S((t, d), _MXU), S((1, d), F32), S((1, d), F32)],
        compiler_params=_cp("arbitrary"), name=name)(dhn_a, dhn_b, dout, x2, mix, g_pre, g_post)


def _first_bwd(dhn, dx2, x, gain, name, comm=None):
    t, d = x.shape
    tm = min(t, ROW_TILE)

    def body(dh_ref, dx2_ref, x_ref, g_ref, dx_ref, dg_ref):
        @pl.when(pl.program_id(0) == 0)
        def _():
            dg_ref[...] = jnp.zeros_like(dg_ref)

        xv = x_ref[...]
        r = _rstd(xv)
        dxa, dg = _rms_bwd(dh_ref[...], xv * r, r, g_ref[...])
        dx_ref[...] = dx2_ref[...] + dxa
        dg_ref[...] += dg

    rs, vs = _row_spec(tm, d), _vec_spec(d)
    return _call(comm, body, grid=(t // tm,), in_specs=[rs, rs, rs, vs], out_specs=[rs, vs],
                          out_shape=[S((t, d), F32), S((1, d), F32)], compiler_params=_cp("arbitrary"), name=name)(dhn, dx2, x, gain)


def _outnorm_bwd(dy, o, yl, ga, gl, name, comm=None):
    t, w = o.shape
    tm = min(t, ROW_TILE)

    def body(dy_ref, o_ref, l_ref, ga_ref, gl_ref, do_ref, dl_ref, dga_ref, dgl_ref):
        @pl.when(pl.program_id(0) == 0)
        def _():
            dga_ref[...] = jnp.zeros_like(dga_ref)
            dgl_ref[...] = jnp.zeros_like(dgl_ref)

        ov, lv = o_ref[...], l_ref[...]
        ra, rl = _rstd(ov), _rstd(lv)
        dov, dga = _rms_bwd(dy_ref[:, :w], ov * ra, ra, ga_ref[...])
        dlv, dgl = _rms_bwd(dy_ref[:, w:], lv * rl, rl, gl_ref[...])
        do_ref[...] = dov.astype(do_ref.dtype)
        dl_ref[...] = dlv
        dga_ref[...] += dga
        dgl_ref[...] += dgl

    rs, vs = _row_spec(tm, w), _vec_spec(w)
    return _call(comm, body, grid=(t // tm,), in_specs=[_row_spec(tm, 2 * w), rs, rs, vs, vs], out_specs=[rs, rs, vs, vs],
                          out_shape=[S((t, w), _MXU), S((t, w), F32), S((1, w), F32), S((1, w), F32)],
                          compiler_params=_cp("arbitrary"), name=name)(dy, o, yl, ga, gl)


def _split_dot(v, tri):
    hi = v.astype(_MXU)
    lo = (v - hi.astype(F32)).astype(_MXU)
    return _dot(hi, tri) + _dot(lo, tri)


def _attn_tile(qb, kb, row, col, shift, scale):
    z = _dot_nt(qb, kb) * scale
    mask = (col + shift) < row
    lb = _log_sigmoid(z)
    lm = jnp.where(mask, lb - z, 0.0)
    return mask, lb, lm


def _attn_fwd(proj, n_heads, name, comm=None):
    t = proj.shape[0]
    bq = min(t, ATTN_BLOCK)
    nq = t // bq
    scale = 1.0 / math.sqrt(HEAD_DIM)

    heads = [slice(a * HEAD_DIM, (a + 1) * HEAD_DIM) for a in range(ATTN_HEADS)]

    def body(q_ref, k_ref, v_ref, o_ref):
        row = lax.broadcasted_iota(jnp.int32, (bq, bq), 0)
        col = lax.broadcasted_iota(jnp.int32, (bq, bq), 1)
        tri = (row > col).astype(_MXU)

        def per_q(qi, _):
            q0 = pl.multiple_of(qi * bq, bq)
            qbs = [q_ref[pl.ds(q0, bq), hd] for hd in heads]

            def cond(st):
                return jnp.logical_and(st[0] >= 0, st[1])

            def step(st):
                kj, _, carries, accs = st
                k0 = pl.multiple_of(kj * bq, bq)
                alive, new_carries, new_accs = None, [], []
                for hd, qb, carry, acc in zip(heads, qbs, carries, accs):
                    mask, lb, lm = _attn_tile(qb, k_ref[pl.ds(k0, bq), hd], row, col, (kj - qi) * bq, scale)
                    w = jnp.where(mask, jnp.exp(lb + _split_dot(lm, tri) + carry), 0.0)
                    new_accs.append(acc + _dot(w.astype(_MXU), v_ref[pl.ds(k0, bq), hd]))
                    carry = carry + jnp.sum(lm, axis=1, keepdims=True)
                    new_carries.append(carry)
                    live = jnp.max(carry) > EXP_CUT
                    alive = live if alive is None else jnp.logical_or(alive, live)
                return kj - 1, alive, tuple(new_carries), tuple(new_accs)

            st = lax.while_loop(cond, step, (qi, jnp.bool_(True), (jnp.zeros((bq, 1), F32),) * ATTN_HEADS,
                                             (jnp.zeros((bq, HEAD_DIM), F32),) * ATTN_HEADS))
            for hd, acc in zip(heads, st[3]):
                o_ref[pl.ds(q0, bq), hd] = acc
            return 0

        lax.fori_loop(0, nq, per_q, 0)

    groups = n_heads // ATTN_HEADS
    hs = lambda off: pl.BlockSpec((t, ATTN_HEADS * HEAD_DIM), lambda h: (0, off + h))
    return _call(
        comm, body, grid=(groups,), in_specs=[hs(0), hs(groups), hs(2 * groups)], out_specs=hs(0),
        out_shape=S((t, n_heads * HEAD_DIM), F32), compiler_params=_cp("parallel"), name=name)(proj, proj, proj)


def _attn_bwd(proj, do, n_heads, name, comm=None):
    t = proj.shape[0]
    bq = min(t, ATTN_BLOCK)
    nq = t // bq
    scale = 1.0 / math.sqrt(HEAD_DIM)

    heads = [slice(a * HEAD_DIM, (a + 1) * HEAD_DIM) for a in range(ATTN_HEADS)]

    def body(q_ref, k_ref, v_ref, do_ref, dq_ref, dk_ref, dv_ref, dka_ref, dva_ref, g_ref, b_ref):
        dka_ref[...] = jnp.zeros_like(dka_ref)
        dva_ref[...] = jnp.zeros_like(dva_ref)
        row = lax.broadcasted_iota(jnp.int32, (bq, bq), 0)
        col = lax.broadcasted_iota(jnp.int32, (bq, bq), 1)
        tri = (row > col).astype(_MXU)
        tri_lt = (row < col).astype(_MXU)

        def per_q(qi, _):
            q0 = pl.multiple_of(qi * bq, bq)
            qbs = [q_ref[pl.ds(q0, bq), hd] for hd in heads]
            dobs = [do_ref[pl.ds(q0, bq), hd] for hd in heads]

            def cond(st):
                return jnp.logical_and(st[0] >= 0, st[1])

            def step(st):
                kj, _, carries = st
                k0 = pl.multiple_of(kj * bq, bq)
                alive, new_carries = None, []
                for a, (hd, qb, dob, carry) in enumerate(zip(heads, qbs, dobs, carries)):
                    mask, lb, lm = _attn_tile(qb, k_ref[pl.ds(k0, bq), hd], row, col, (kj - qi) * bq, scale)
                    w = jnp.where(mask, jnp.exp(lb + _split_dot(lm, tri) + carry), 0.0)
                    g_ref[a, pl.ds(k0, bq), :] = w * _dot_nt(dob, v_ref[pl.ds(k0, bq), hd])
                    b_ref[a, pl.ds(k0, bq), :] = jnp.where(mask, jnp.exp(lb), 0.0)
                    dva_ref[pl.ds(k0, bq), hd] += _dot_tn(w.astype(_MXU), dob)
                    carry = carry + jnp.sum(lm, axis=1, keepdims=True)
                    new_carries.append(carry)
                    live = jnp.max(carry) > EXP_CUT
                    alive = live if alive is None else jnp.logical_or(alive, live)
                return kj - 1, alive, tuple(new_carries)

            st = lax.while_loop(cond, step, (qi, jnp.bool_(True), (jnp.zeros((bq, 1), F32),) * ATTN_HEADS))

            def back(kj, st2):
                k0 = pl.multiple_of(kj * bq, bq)
                out = []
                for a, (hd, qb, (before, dq)) in enumerate(zip(heads, qbs, st2)):
                    g = g_ref[a, pl.ds(k0, bq), :]
                    beta = b_ref[a, pl.ds(k0, bq), :]
                    dz = ((g * (1.0 - beta) - (before + _split_dot(g, tri_lt)) * beta) * scale).astype(_MXU)
                    dka_ref[pl.ds(k0, bq), hd] += _dot_tn(dz, qb)
                    out.append((before + jnp.sum(g, axis=1, keepdims=True), dq + _dot(dz, k_ref[pl.ds(k0, bq), hd])))
                return tuple(out)

            st2 = lax.fori_loop(st[0] + 1, qi + 1, back, ((jnp.zeros((bq, 1), F32), jnp.zeros((bq, HEAD_DIM), F32)),) * ATTN_HEADS)
            for hd, (_, dq) in zip(heads, st2):
                dq_ref[pl.ds(q0, bq), hd] = dq.astype(dq_ref.dtype)
            return 0

        lax.fori_loop(0, nq, per_q, 0)
        dk_ref[...] = dka_ref[...].astype(dk_ref.dtype)
        dv_ref[...] = dva_ref[...].astype(dv_ref.dtype)

    groups = n_heads // ATTN_HEADS
    wide = ATTN_HEADS * HEAD_DIM
    hs = lambda off: pl.BlockSpec((t, wide), lambda h: (0, off + h))
    return _call(
        comm, body, grid=(groups,), in_specs=[hs(0), hs(groups), hs(2 * groups), hs(0)], out_specs=[hs(0), hs(0), hs(0)],
        out_shape=[S((t, n_heads * HEAD_DIM), _MXU)] * 3,
        scratch_shapes=[pltpu.VMEM((t, wide), F32), pltpu.VMEM((t, wide), F32),
                        pltpu.VMEM((ATTN_HEADS, t, bq), F32), pltpu.VMEM((ATTN_HEADS, t, bq), F32)],
        compiler_params=_cp("parallel"), name=name)(proj, proj, proj, do)


def _shift_down(cur, prev8, k):
    if k == 0:
        return cur
    row8 = lax.broadcasted_iota(jnp.int32, prev8.shape, 0)
    rc = pltpu.roll(cur, k, 0)
    top = jnp.where(row8 < k, pltpu.roll(prev8, k, 0), rc[0:8, :])
    return jnp.concatenate([top, rc[8:, :]], axis=0)


def _shift_up(cur, next8, k):
    if k == 0:
        return cur
    n = cur.shape[0]
    row8 = lax.broadcasted_iota(jnp.int32, next8.shape, 0)
    rc = pltpu.roll(cur, n - k, 0)
    bottom = jnp.where(row8 >= 8 - k, pltpu.roll(next8, 8 - k, 0), rc[n - 8:, :])
    return jnp.concatenate([rc[:n - 8, :], bottom], axis=0)


def _lru_gates(xl, prev8, cw, cb, wr, br, wi, bi, ls):
    xs = [_shift_down(xl, prev8, CONV_WIDTH - 1 - k) for k in range(CONV_WIDTH)]
    xc = xs[0] * cw[0:1, :]
    for k in range(1, CONV_WIDTH):
        xc = xc + xs[k] * cw[k:k + 1, :]
    xc = xc + cb
    xcb = xc.astype(_MXU)
    r = jax.nn.sigmoid(_dot(xcb, wr) + br)
    i = jax.nn.sigmoid(_dot(xcb, wi) + bi)
    la = (LRU_C * r) * ls
    a = jnp.exp(la)
    mult = jnp.sqrt(-_expm1(2.0 * la))
    return xs, xc, r, i, a, mult


def _group_scan(a, b, reverse):
    n = a.shape[0]
    row = lax.broadcasted_iota(jnp.int32, a.shape, 0) % 8
    for d in (1, 2, 4):
        if reverse:
            m = row < 8 - d
            a_s, b_s = pltpu.roll(a, n - d, 0), pltpu.roll(b, n - d, 0)
        else:
            m = row >= d
            a_s, b_s = pltpu.roll(a, d, 0), pltpu.roll(b, d, 0)
        b = jnp.where(m, a * b_s + b, b)
        a = jnp.where(m, a * a_s, a)
    return a, b


def _lru_fwd(proj, col0, n_blocks, cw, cb, wr, br, wi, bi, lam, name, comm=None):
    t = proj.shape[0]
    tt = min(t, SEQ_TILE)
    nt = t // tt

    def body(xl_ref, gl_ref, cw_ref, cb_ref, wr_ref, br_ref, wi_ref, bi_ref, lam_ref, h_ref, y_ref):
        cwv, cbv, brv, biv = cw_ref[...], cb_ref[...], br_ref[...], bi_ref[...]
        wrv, wiv = wr_ref[...].astype(_MXU), wi_ref[...].astype(_MXU)
        ls = _log_sigmoid(lam_ref[...])

        def tile(ti, hin):
            t0 = pl.multiple_of(ti * tt, tt)
            p0 = pl.multiple_of(jnp.maximum(t0 - 8, 0), 8)
            prev8 = xl_ref[pl.ds(p0, 8), :] * (ti > 0).astype(F32)
            xl = xl_ref[pl.ds(t0, tt), :]
            _, xc, _, ig, a, mult = _lru_gates(xl, prev8, cwv, cbv, wrv, brv, wiv, biv, ls)
            ga, gb = _group_scan(a, mult * (ig * xc), False)
            for g in range(tt // 8):
                hg = ga[8 * g:8 * g + 8, :] * hin + gb[8 * g:8 * g + 8, :]
                h_ref[pl.ds(t0 + 8 * g, 8), :] = hg
                hin = hg[7:8, :]
            y_ref[pl.ds(t0, tt), :] = h_ref[pl.ds(t0, tt), :] * _gelu(gl_ref[pl.ds(t0, tt), :])
            return hin

        lax.fori_loop(0, nt, tile, jnp.zeros((1, HEAD_DIM), F32))

    cs = lambda off: pl.BlockSpec((t, HEAD_DIM), lambda n: (0, off + n))
    vs = pl.BlockSpec((1, HEAD_DIM), lambda n: (0, n))
    ws = pl.BlockSpec((None, HEAD_DIM, HEAD_DIM), lambda n: (n, 0, 0))
    w = n_blocks * HEAD_DIM
    return _call(
        comm, body, grid=(n_blocks,),
        in_specs=[cs(col0), cs(col0 + n_blocks), pl.BlockSpec((CONV_WIDTH, HEAD_DIM), lambda n: (0, n)), vs, ws, vs, ws, vs, vs],
        out_specs=[cs(0), cs(0)], out_shape=[S((t, w), F32), S((t, w), F32)],
        compiler_params=_cp("parallel"), name=name)(proj, proj, cw, cb, wr, br, wi, bi, lam)


def _lru_bwd(proj, col0, n_blocks, h, dyl, cw, cb, wr, br, wi, bi, lam, name, comm=None):
    t = proj.shape[0]
    tt = min(t, SEQ_TILE)
    nt = t // tt

    def body(xl_ref, gl_ref, h_ref, dy_ref, cw_ref, cb_ref, wr_ref, br_ref, wi_ref, bi_ref, lam_ref,
             dxl_ref, dgl_ref, dcw_ref, dcb_ref, dwr_ref, dbr_ref, dwi_ref, dbi_ref, dlam_ref, g_ref):
        cwv, cbv, brv, biv = cw_ref[...], cb_ref[...], br_ref[...], bi_ref[...]
        wrv, wiv = wr_ref[...].astype(_MXU), wi_ref[...].astype(_MXU)
        lamv = lam_ref[...]
        ls = _log_sigmoid(lamv)
        for ref in (dcw_ref, dcb_ref, dwr_ref, dbr_ref, dwi_ref, dbi_ref, dlam_ref):
            ref[...] = jnp.zeros_like(ref)

        def tile(s, carry):
            e_in, dxc_next8 = carry
            ti = nt - 1 - s
            t0 = pl.multiple_of(ti * tt, tt)
            p0 = pl.multiple_of(jnp.maximum(t0 - 8, 0), 8)
            first = (ti > 0).astype(F32)
            xl = xl_ref[pl.ds(t0, tt), :]
            xs, xc, r, ig, a, mult = _lru_gates(xl, xl_ref[pl.ds(p0, 8), :] * first, cwv, cbv, wrv, brv, wiv, biv, ls)
            hv = h_ref[pl.ds(t0, tt), :]
            h_before = _shift_down(hv, h_ref[pl.ds(p0, 8), :] * first, 1)
            glv = gl_ref[pl.ds(t0, tt), :]
            dyv = dy_ref[pl.ds(t0, tt), :]
            dgl_ref[pl.ds(t0, tt), :] = (dyv * hv * _gelu_grad(glv)).astype(dgl_ref.dtype)
            dh = dyv * _gelu(glv)
            row = lax.broadcasted_iota(jnp.int32, a.shape, 0)
            coef = jnp.where(row == tt - 1, 1.0, pltpu.roll(a, tt - 1, 0))
            ga, gb = _group_scan(coef, dh, True)
            gin = e_in
            for g in reversed(range(tt // 8)):
                gg = ga[8 * g:8 * g + 8, :] * gin + gb[8 * g:8 * g + 8, :]
                g_ref[8 * g:8 * g + 8, :] = gg
                gin = gg[0:1, :]
            gv = g_ref[...]
            e_out = a[0:1, :] * gv[0:1, :]
            ix = ig * xc
            dla = (gv * h_before) * a - (gv * ix) * (a * a / mult)
            dlam_ref[...] += jnp.sum(dla * (LRU_C * r), axis=0, keepdims=True)
            dpr = (dla * (LRU_C * ls)) * (r * (1.0 - r))
            dpi = (gv * mult * xc) * (ig * (1.0 - ig))
            dbr_ref[...] += jnp.sum(dpr, axis=0, keepdims=True)
            dbi_ref[...] += jnp.sum(dpi, axis=0, keepdims=True)
            xcb, dprb, dpib = xc.astype(_MXU), dpr.astype(_MXU), dpi.astype(_MXU)
            dwr_ref[...] += _dot_tn(xcb, dprb)
            dwi_ref[...] += _dot_tn(xcb, dpib)
            dxc = gv * mult * ig + _dot_nt(dprb, wrv) + _dot_nt(dpib, wiv)
            dcb_ref[...] += jnp.sum(dxc, axis=0, keepdims=True)
            dxl = None
            for k in range(CONV_WIDTH):
                dcw_ref[k:k + 1, :] += jnp.sum(dxc * xs[k], axis=0, keepdims=True)
                term = _shift_up(dxc, dxc_next8, CONV_WIDTH - 1 - k) * cwv[k:k + 1, :]
                dxl = term if dxl is None else dxl + term
            dxl_ref[pl.ds(t0, tt), :] = dxl.astype(dxl_ref.dtype)
            return e_out, dxc[0:8, :]

        lax.fori_loop(0, nt, tile, (jnp.zeros((1, HEAD_DIM), F32), jnp.zeros((8, HEAD_DIM), F32)))
        dlam_ref[...] = dlam_ref[...] * (1.0 - jax.nn.sigmoid(lamv))

    cs = lambda off: pl.BlockSpec((t, HEAD_DIM), lambda n: (0, off + n))
    vs = pl.BlockSpec((1, HEAD_DIM), lambda n: (0, n))
    ws = pl.BlockSpec((None, HEAD_DIM, HEAD_DIM), lambda n: (n, 0, 0))
    cws = pl.BlockSpec((CONV_WIDTH, HEAD_DIM), lambda n: (0, n))
    w = n_blocks * HEAD_DIM
    vec = S((1, w), F32)
    mat = S((n_blocks, HEAD_DIM, HEAD_DIM), F32)
    return _call(
        comm, body, grid=(n_blocks,),
        in_specs=[cs(col0), cs(col0 + n_blocks), cs(0), cs(0), cws, vs, ws, vs, ws, vs, vs],
        out_specs=[cs(0), cs(0), cws, vs, ws, vs, ws, vs, vs],
        out_shape=[S((t, w), _MXU), S((t, w), _MXU), S((CONV_WIDTH, w), F32), vec, mat, vec, mat, vec, vec],
        scratch_shapes=[pltpu.VMEM((tt, HEAD_DIM), F32)],
        compiler_params=_cp("parallel"), name=name)(proj, proj, h, dyl, cw, cb, wr, br, wi, bi, lam)


class _NoExchange:
    def __init__(self, weights):
        self.weights, self.grads, self.packs = weights, {}, {}

    def weight(self, name):
        return self.weights[name]

    def carrier(self, call):
        return None

    def harvest(self, car):
        pass

    def alone(self, call):
        pass


def _local_step(x, target, norms, ex, cw, cb, wr, br, wi, bi, lam, ga, gl):
    g_pre_mix, g_post_mix, g_pre_ffn, g_post_ffn = norms
    t, d = x.shape
    bm = min(t, 512)
    bt = min(t, 2048)

    def run(fn, name, *args, **kw):
        car = ex.carrier(name)
        out = fn(*args, name=name, comm=car, **kw)
        ex.harvest(car)
        return out

    hn1 = run(_rms_fwd, "rms1", x, g_pre_mix)
    win3 = ex.weight("w_in")
    c = win3.shape[0]
    proj, proj_mx = run(_mm_nn, "in_proj", hn1, win3, bm=bm, bn=win3.shape[2], also=_MXU)
    o = run(_attn_fwd, "attn_fwd", proj_mx, (proj.shape[1] - d) // 3 // HEAD_DIM)
    mix = 2 * o.shape[1]
    n_heads = n_blocks = o.shape[1] // HEAD_DIM
    h, yl = run(_lru_fwd, "lru_fwd", proj, 3 * n_heads, n_blocks, cw, cb, wr, br, wi, bi, lam)
    y = _outnorm_fwd(o, yl, ga, gl, "outnorm_fwd")
    wout = ex.weight("w_out")
    mixo = run(_mm_nn, "out_proj", y, wout[None], bm=bm, bn=d)
    x2, hn2 = run(_mid_fwd, "mid_fwd", x, mixo, g_post_mix, g_pre_ffn)
    wg3, wu3 = ex.weight("w_ffn_gate"), ex.weight("w_ffn_up")
    gate, up, act = run(_swiglu_fwd, "ffn_gate_up", hn2, wg3, wu3, bm=bm)
    ex.alone("gather_w_down")
    wd = ex.weight("w_ffn_down")
    ff = wd.shape[0]
    f = _mm_nn(act, wd[None], bm=bm, bn=d // 2, name="ffn_down")
    loss_cols, dout, df, dg_post_ffn = _final(f, x2, target, g_post_ffn, "final")

    dgate, dup = _swiglu_bwd(df, wd, gate, up, bm=bm, bo=ff // 4, name="ffn_down_bwd")
    ex.grads["w_ffn_down"] = _mm_tn(act, df, 1, bm=bt, bk=512, name="ffn_down_dw").reshape(c, ff // c, d)
    ex.grads["w_ffn_gate"] = run(_mm_tn, "ffn_gate_dw", hn2, dgate, c, bm=bt, bk=d // 2)
    ex.grads["w_ffn_up"] = run(_mm_tn, "ffn_up_dw", hn2, dup, c, bm=bt, bk=d // 2)
    dhn2_g = run(_mm_nt, "ffn_gate_dx", dgate, wg3, bm=bm, bo=d // 2, out_dtype=F32)
    dhn2_u = run(_mm_nt, "ffn_up_dx", dup, wu3, bm=bm, bo=d // 2, out_dtype=F32)
    dx2, dmix, dg_pre_ffn, dg_post_mix = run(_mid_bwd, "mid_bwd", dhn2_g, dhn2_u, dout, x2, mixo, g_pre_ffn, g_post_mix)
    dy = _mm_nt(dmix, wout[None], bm=bm, bo=mix, out_dtype=F32, name="out_proj_dx")
    ex.grads["w_out"] = _mm_tn(y, dmix, 1, bm=bt, bk=mix // 4, name="out_proj_dw").reshape(c, mix // c, d)
    do, dyl, dga, dgl_norm = run(_outnorm_bwd, "outnorm_bwd", dy, o, yl, ga, gl)
    dxl, dglu, dcw, dcb, dwr, dbr, dwi, dbi, dlam = run(_lru_bwd, "lru_bwd", proj, 3 * n_heads, n_blocks, h, dyl, cw, cb, wr, br, wi, bi, lam)
    small = dict(post_mix_norm=dg_post_mix, pre_ffn_norm=dg_pre_ffn, post_ffn_norm=dg_post_ffn, conv_w=dcw, conv_b=dcb,
                 w_rgate=dwr, b_rgate=dbr, w_igate=dwi, b_igate=dbi, lru_lambda=dlam, attn_out_norm=dga, lru_out_norm=dgl_norm)
    ex.packs["early"] = _pack([small[n] for n in _SMALL_EARLY])
    dq, dk, dv = run(_attn_bwd, "attn_bwd", proj_mx, do, n_heads)
    dproj = jnp.concatenate([dq, dk, dv, dxl, dglu], axis=1)
    ex.grads["w_in"] = _mm_tn(hn1, dproj, c, bm=bt, bk=d // 2, name="in_proj_dw")
    ex.alone("grads_w_in_swap")
    dhn1 = run(_mm_nt, "in_proj_dx", dproj, win3, bm=bm, bo=d // 2, out_dtype=F32)
    grad_x, small["pre_mix_norm"] = run(_first_bwd, "first_bwd", dhn1, dx2, x, g_pre_mix)
    ex.packs["late"] = _pack([small["pre_mix_norm"]])
    return loss_cols, grad_x, small


def _into_slot(wsh, slot, dtype, name):
    rows, n = wsh.shape
    rb = _row_block(rows, 256) if rows % 8 == 0 else rows

    def body(s_ref, w_ref, o_ref):
        o_ref[...] = w_ref[...].astype(o_ref.dtype)

    return pl.pallas_call(
        body,
        grid_spec=pltpu.PrefetchScalarGridSpec(
            num_scalar_prefetch=1, grid=(rows // rb,),
            in_specs=[pl.BlockSpec((rb, n), lambda i, s_ref: (i, 0))],
            out_specs=pl.BlockSpec((None, rb, n), lambda i, s_ref: (s_ref[0], i, 0))),
        out_shape=S((4, rows, n), dtype), compiler_params=_cp("parallel"), name=name)(slot, wsh)


class _Exchange:
    SCHEDULE = {
        "gather_w_in": [("ici", "w_in"), ("ici", "conv_w")],
        "rms1": [("d2d", "w_in")],
        "in_proj": [("ici", "w_out")],
        "attn_fwd": [("d2d", "w_out"), ("ici", "w_ffn_gate"), ("ici", "w_ffn_up", 0)],
        "lru_fwd": [("d2d", "w_ffn_gate"), ("d2d", "w_ffn_up", 0), ("ici", "w_ffn_up", 1), ("ici", "w_ffn_up", 2)],
        "out_proj": [("d2d", "w_ffn_up", 1), ("d2d", "w_ffn_up", 2), ("ici", "w_ffn_up", 3)],
        "mid_fwd": [("d2d", "w_ffn_up", 3)],
        "ffn_gate_up": [("ici", "w_ffn_down")],
        "gather_w_down": [("d2d", "w_ffn_down")],
        "ffn_gate_dw": [("swap", "w_ffn_down")],
        "ffn_up_dw": [("scatter", "w_ffn_down"), ("swap", "w_ffn_gate")],
        "ffn_gate_dx": [("share", "w_ffn_down"), ("scatter", "w_ffn_gate"), ("swap", "w_ffn_up")],
        "ffn_up_dx": [("share", "w_ffn_gate"), ("scatter", "w_ffn_up")],
        "mid_bwd": [("share", "w_ffn_up")],
        "outnorm_bwd": [("swap", "w_out")],
        "lru_bwd": [("scatter", "w_out")],
        "attn_bwd": [("share", "w_out"), ("spread", "early")],
        "grads_w_in_swap": [("swap", "w_in")],
        "in_proj_dx": [("scatter", "w_in")],
        "grads_w_in_share": [("share", "w_in"), ("spread", "late")],
    }
    PIECES = 4

    def __init__(self, slots, place):
        self.buf, self.place = dict(slots), place
        self.grads, self.packs, self.swapped, self.part, self.scattered, self.full, self.spreaded = {}, {}, {}, {}, {}, {}, {}

    def weight(self, name):
        b = self.buf[name]
        return b.reshape(-1, b.shape[2]) if name in ("w_out", "w_ffn_down") else b

    def carrier(self, call):
        if call not in self.SCHEDULE:
            return None
        car = _Carrier()
        car.todo, slot = [], {}
        for kind, name, *piece in self.SCHEDULE[call]:
            if kind in ("ici", "d2d"):
                if name not in slot:
                    slot[name] = car.inplace(self.buf[name])
                    car.todo.append((self.buf, name, slot[name]))
                size = self.buf[name].shape[1] // 2 // self.PIECES
                rows = (piece[0] * size, size) if piece else None
                if kind == "ici":
                    car.gather_ici(slot[name], rows, split=name != "conv_w")
                else:
                    car.gather_d2d(slot[name], rows)
            elif kind == "swap":
                g = self.grads[name]
                o = car.fresh((4, g.shape[1] // 2, g.shape[2]), F32)
                car.swap(car.read(g), o)
                car.todo.append((self.swapped, name, o))
            elif kind == "scatter":
                if name not in self.part:
                    self.part[name] = _add_own_half(self.grads[name], self.swapped[name], self.place[1:], "grads_add_" + name)
                p = self.part[name]
                o = car.inplace(self.scattered[name]) if name in self.scattered else car.fresh(p.shape, p.dtype)
                size = p.shape[1] // self.PIECES
                car.scatter(car.read(p), o, (piece[0] * size, size) if piece else None)
                car.todo.append((self.scattered, name, o))
            elif kind == "share":
                o = car.inplace(_sum_chips(self.part[name], self.scattered[name], self.place, "grads_sum_" + name))
                car.share(o)
                car.todo.append((self.full, name, o))
            else:
                o = car.fresh((8,) + self.packs[name].shape, F32)
                car.spread(car.read(self.packs[name]), o)
                car.todo.append((self.spreaded, name, o))
        return car

    def harvest(self, car):
        for state, name, o in (car.todo if car is not None else []):
            state[name] = car.results[o]

    def alone(self, call):
        car = self.carrier(call)
        car.run_alone(call)
        self.harvest(car)

    def small_sum(self, key):
        return _sum_devices(self.packs[key], self.spreaded[key], 2 * self.place[0:1] + self.place[1:], "grads_small_sum_" + key)


def _row_block(rows, cap):
    return max(b for b in range(8, cap + 1, 8) if rows % b == 0)


def _add_own_half(g, recv, core, name):
    _, rows, n = g.shape
    half = rows // 2
    rb = _row_block(half, 512)
    nb = half // rb

    def body(c_ref, g_ref, r_ref, o_ref):
        o_ref[...] = (g_ref[...] + r_ref[...]).astype(o_ref.dtype)

    return pl.pallas_call(
        body,
        grid_spec=pltpu.PrefetchScalarGridSpec(
            num_scalar_prefetch=1, grid=(4, nb),
            in_specs=[pl.BlockSpec((None, rb, n), lambda k, i, c_ref: (k, c_ref[0] * nb + i, 0)),
                      pl.BlockSpec((None, rb, n), lambda k, i, c_ref: (k, i, 0))],
            out_specs=pl.BlockSpec((None, rb, n), lambda k, i, c_ref: (k, i, 0))),
        out_shape=S((4, half, n), BF16), compiler_params=_cp("parallel", "parallel"), name=name)(core, g, recv)


def _sum_chips(part, recv, place, name):
    _, rows, n = part.shape
    rb = _row_block(rows, 64)
    nb = rows // rb

    def body(p_ref, own_ref, r0, r1, r2, r3, o_ref):
        own = own_ref[...].astype(F32)
        terms = [jnp.where(p_ref[0] == k, own, r[...].astype(F32)) for k, r in enumerate((r0, r1, r2, r3))]
        o_ref[...] = ((terms[0] + terms[1]) + terms[2]) + terms[3]

    def slot(k):
        return pl.BlockSpec((None, rb, n), lambda i, p_ref: (jnp.where(p_ref[0] == k, (k + 1) % 4, k), i, 0))

    return pl.pallas_call(
        body,
        grid_spec=pltpu.PrefetchScalarGridSpec(
            num_scalar_prefetch=1, grid=(nb,),
            in_specs=[pl.BlockSpec((None, rb, n), lambda i, p_ref: (p_ref[0], i, 0))] + [slot(k) for k in range(4)],
            out_specs=pl.BlockSpec((rb, n), lambda i, p_ref: (p_ref[1] * nb + i, 0))),
        out_shape=S((2 * rows, n), F32), compiler_params=_cp("parallel"), name=name)(place, part, recv, recv, recv, recv)


def _sum_devices(own, spread, me, name):
    rows = own.shape[0]

    def body(me_ref, own_ref, *refs):
        acc = None
        for k, r in enumerate(refs[:8]):
            term = jnp.where(me_ref[0] == k, own_ref[...], r[...])
            acc = term if acc is None else acc + term
        refs[8][...] = acc

    def slot(k):
        return pl.BlockSpec((None, rows, 128), lambda i, me_ref: (jnp.where(me_ref[0] == k, (k + 1) % 8, k), 0, 0))

    whole = pl.BlockSpec((rows, 128), lambda i, me_ref: (0, 0))
    return pl.pallas_call(
        body,
        grid_spec=pltpu.PrefetchScalarGridSpec(num_scalar_prefetch=1, grid=(1,), in_specs=[whole] + [slot(k) for k in range(8)],
                                               out_specs=whole),
        out_shape=S((rows, 128), F32), compiler_params=_cp("arbitrary"), name=name)(me, own, *[spread] * 8)


def _adamw(w, g, m, v, name):
    rows, n = w.shape
    rb = rows if rows * n * 4 <= (1 << 21) else _row_block(rows, 128)
    c1 = 1.0 - ADAM_B1 ** ADAM_STEP
    c2 = 1.0 - ADAM_B2 ** ADAM_STEP

    def body(w_ref, g_ref, m_ref, v_ref, d_ref, nm_ref, nv_ref):
        gv = g_ref[...]
        nm = ADAM_B1 * m_ref[...] + (1.0 - ADAM_B1) * gv
        nv = ADAM_B2 * v_ref[...] + (1.0 - ADAM_B2) * (gv * gv)
        nm_ref[...] = nm
        nv_ref[...] = nv
        d_ref[...] = -ADAM_LR * ((nm / c1) / (jnp.sqrt(nv / c2) + ADAM_EPS) + ADAM_WD * w_ref[...])

    bs = pl.BlockSpec((rb, n), lambda i: (i, 0))
    return pl.pallas_call(body, grid=(rows // rb,), in_specs=[bs] * 4, out_specs=[bs] * 3, out_shape=[S((rows, n), F32)] * 3,
                          compiler_params=_cp("parallel"), name=name)(w, g, m, v)


_BIG = ("w_in", "w_out", "w_ffn_gate", "w_ffn_up", "w_ffn_down")
_SMALL = ("pre_mix_norm", "post_mix_norm", "pre_ffn_norm", "post_ffn_norm", "conv_w", "conv_b", "w_rgate", "b_rgate",
          "w_igate", "b_igate", "lru_lambda", "attn_out_norm", "lru_out_norm")
_SMALL_EARLY = _SMALL[1:]
_WEIGHTS = ("pre_mix_norm", "post_mix_norm", "pre_ffn_norm", "post_ffn_norm", "w_in", "conv_w", "conv_b", "w_rgate", "b_rgate",
            "w_igate", "b_igate", "lru_lambda", "attn_out_norm", "lru_out_norm", "w_out", "w_ffn_gate", "w_ffn_up", "w_ffn_down")


def _pack(arrays):
    flat = []
    for a in arrays:
        f = a.reshape(-1)
        flat.append(jnp.pad(f, (0, (-f.shape[0]) % 1024)))
    return jnp.concatenate(flat).reshape(-1, 128)


def _unpack(packed, shapes):
    out, pos = [], 0
    flat = packed.reshape(-1)
    for s in shapes:
        size = math.prod(s)
        out.append(flat[pos:pos + size].reshape(s))
        pos += size + (-size) % 1024
    return out


def kernel(x, pre_mix_norm, post_mix_norm, pre_ffn_norm, post_ffn_norm, w_in, conv_w, conv_b, w_rgate, b_rgate, w_igate, b_igate, lru_lambda, attn_out_norm, lru_out_norm, w_out, w_ffn_gate, w_ffn_up, w_ffn_down, loss_target, m_pre_mix_norm, m_post_mix_norm, m_pre_ffn_norm, m_post_ffn_norm, m_w_in, m_conv_w, m_conv_b, m_w_rgate, m_b_rgate, m_w_igate, m_b_igate, m_lru_lambda, m_attn_out_norm, m_lru_out_norm, m_w_out, m_w_ffn_gate, m_w_ffn_up, m_w_ffn_down, v_pre_mix_norm, v_post_mix_norm, v_pre_ffn_norm, v_post_ffn_norm, v_w_in, v_conv_w, v_conv_b, v_w_rgate, v_b_rgate, v_w_igate, v_b_igate, v_lru_lambda, v_attn_out_norm, v_lru_out_norm, v_w_out, v_w_ffn_gate, v_w_ffn_up, v_w_ffn_down):
    given = dict(locals())
    w = {n: given[n][0] for n in _WEIGHTS}
    m = {n: given["m_" + n][0] for n in _WEIGHTS}
    v = {n: given["v_" + n][0] for n in _WEIGHTS}
    xs, target = x[0], loss_target[0]
    d = xs.shape[1]
    chip = (2 * lax.axis_index("x") + lax.axis_index("y")).astype(jnp.int32)
    place = jnp.stack([chip, lax.axis_index("c").astype(jnp.int32)])

    slots = {n: _into_slot(w[n], place[0:1], _MXU, "slot_" + n) for n in _BIG}
    slots["conv_w"] = _into_slot(w["conv_w"], place[0:1], F32, "slot_conv_w")
    ex = _Exchange(slots, place)
    ex.alone("gather_w_in")
    conv_full = jnp.transpose(ex.buf["conv_w"], (1, 0, 2)).reshape(CONV_WIDTH, -1)
    row = lambda a: a.reshape(1, -1)
    norms = tuple(row(w[n]) for n in ("pre_mix_norm", "post_mix_norm", "pre_ffn_norm", "post_ffn_norm"))

    loss_cols, grad_x, small = _local_step(
        xs, target, norms, ex, conv_full, row(w["conv_b"]), w["w_rgate"], row(w["b_rgate"]),
        w["w_igate"], row(w["b_igate"]), row(w["lru_lambda"]), row(w["attn_out_norm"]), row(w["lru_out_norm"]))

    loss = lax.psum(0.5 * jnp.sum(loss_cols) / d, ("x", "y", "c"))

    ex.alone("grads_w_in_share")
    reduced = {n: ex.full[n] for n in _BIG}
    early = _unpack(ex.small_sum("early"), [small[n].shape for n in _SMALL_EARLY])
    late = _unpack(ex.small_sum("late"), [small["pre_mix_norm"].shape])
    for n, g in zip(_SMALL_EARLY + ("pre_mix_norm",), early + late):
        reduced[n] = g.reshape(w[n].shape) if n != "conv_w" else lax.dynamic_slice_in_dim(g, chip * w[n].shape[1], w[n].shape[1], axis=1)

    delta, new_m, new_v = {}, {}, {}
    for n in _BIG:
        delta[n], new_m[n], new_v[n] = _adamw(w[n], reduced[n], m[n], v[n], "adamw_" + n)
    shapes = [w[n].shape for n in _SMALL]
    packed = _adamw(*[_pack([src[n] for n in _SMALL]) for src in (w, reduced, m, v)], "adamw_small")
    for out, p in zip((delta, new_m, new_v), packed):
        out.update(zip(_SMALL, _unpack(p, shapes)))

    lead = lambda a: a[None]
    return (loss, lead(grad_x), *[lead(reduced[n]) for n in _WEIGHTS], *[lead(delta[n]) for n in _WEIGHTS],
            *[lead(new_m[n]) for n in _WEIGHTS], *[lead(new_v[n]) for n in _WEIGHTS])
```

```python
import functools
import math

import jax
import jax.numpy as jnp
from jax import lax
from jax.experimental import pallas as pl
from jax.experimental.pallas import tpu as pltpu

F32 = jnp.float32
BF16 = jnp.bfloat16
_MXU = BF16
S = jax.ShapeDtypeStruct

RMS_EPS = 1e-6
HEAD_DIM = 128
CONV_WIDTH = 4
LRU_C = 8.0
ADAM_LR, ADAM_B1, ADAM_B2, ADAM_EPS, ADAM_WD, ADAM_STEP = 0.001, 0.9, 0.999, 1e-08, 0.01, 10
EXP_CUT = -105.0
VMEM_LIMIT = 60 * 1024 * 1024
ROW_TILE = 256
SEQ_TILE = 256
ATTN_BLOCK = 256
ATTN_HEADS = 2
MESH = pl.DeviceIdType.MESH


def _cp(*sem):
    return pltpu.CompilerParams(dimension_semantics=sem, vmem_limit_bytes=VMEM_LIMIT)


def _dot(a, b):
    return jnp.dot(a, b, preferred_element_type=F32)


def _dot_nt(a, b):
    return lax.dot_general(a, b, (((1,), (1,)), ((), ())), preferred_element_type=F32)


def _dot_tn(a, b):
    return lax.dot_general(a, b, (((0,), (0,)), ((), ())), preferred_element_type=F32)


def _rstd(v):
    return lax.rsqrt(jnp.mean(v * v, axis=-1, keepdims=True) + RMS_EPS)


def _rms_bwd(dn, vh, r, gain):
    dvh = dn * gain
    dv = r * (dvh - vh * jnp.mean(dvh * vh, axis=-1, keepdims=True))
    return dv, jnp.sum(dn * vh, axis=0, keepdims=True)


def _log_sigmoid(z):
    return jnp.minimum(z, 0.0) - jnp.log(1.0 + jnp.exp(-jnp.abs(z)))


def _expm1(v):
    small = v * (1.0 + v * (0.5 + v * (1.0 / 6.0 + v * (1.0 / 24.0 + v * (1.0 / 120.0)))))
    return jnp.where(jnp.abs(v) < 0.04, small, jnp.exp(v) - 1.0)


_GELU_C = math.sqrt(2.0 / math.pi)


def _gelu(v):
    return 0.5 * v * (1.0 + jnp.tanh(_GELU_C * (v + 0.044715 * v * v * v)))


def _gelu_grad(v):
    th = jnp.tanh(_GELU_C * (v + 0.044715 * v * v * v))
    return 0.5 * (1.0 + th) + 0.5 * v * (1.0 - th * th) * _GELU_C * (1.0 + 3.0 * 0.044715 * v * v)


def _row_spec(tm, d):
    return pl.BlockSpec((tm, d), lambda i: (i, 0))


def _vec_spec(d):
    return pl.BlockSpec((1, d), lambda i: (0, 0))


_ANY = pl.BlockSpec(memory_space=pl.ANY)


def _place():
    x, y, c = lax.axis_index("x"), lax.axis_index("y"), lax.axis_index("c")
    return x, y, c, [(1 - x, y), (x, 1 - y), (1 - x, 1 - y)]


def _remote(src, dst, send_sem, recv_sem, to):
    return pltpu.make_async_remote_copy(src_ref=src, dst_ref=dst, send_sem=send_sem, recv_sem=recv_sem,
                                        device_id=to, device_id_type=MESH)


class _Carrier:
    def __init__(self):
        self.inputs, self.out_shapes, self.aliases, self.ops, self.n_sems, self.results = [], [], {}, [], 0, None

    def inplace(self, arr):
        self.aliases[len(self.inputs)] = len(self.out_shapes)
        self.inputs.append(arr)
        self.out_shapes.append(S(arr.shape, arr.dtype))
        return len(self.out_shapes) - 1

    def read(self, arr):
        self.inputs.append(arr)
        return len(self.inputs) - 1

    def fresh(self, shape, dtype):
        self.out_shapes.append(S(shape, dtype))
        return len(self.out_shapes) - 1

    def _add(self, n_sems, copies):
        base = self.n_sems
        self.n_sems += n_sems

        def start(ins, outs, send, recv):
            for k, (src, dst, _, to) in enumerate(copies(ins, outs)):
                _remote(src, dst, send.at[base + k], recv.at[base + k], to).start()

        def finish(ins, outs, send, recv):
            for k, (src, _, land, to) in enumerate(copies(ins, outs)):
                _remote(src, land, send.at[base + k], recv.at[base + k], to).wait()

        self.ops.append((start, finish))

    def gather_ici(self, o, rows=None, split=True):
        half = self.out_shapes[o].shape[1] // 2
        lo, size = rows or (0, half)

        def copies(ins, outs):
            x, y, c, chips = _place()
            part = (lambda ref: ref.at[pl.ds(c * half + lo, size)]) if split else (lambda ref: ref)
            mine = part(outs[o].at[2 * x + y])
            return [(mine, mine, part(outs[o].at[2 * px + py]), (px, py, c)) for px, py in chips]

        self._add(3, copies)

    def gather_d2d(self, o, rows=None):
        half = self.out_shapes[o].shape[1] // 2
        lo, size = rows or (0, half)

        def copies(ins, outs):
            x, y, c, chips = _place()
            at = lambda k, cc: outs[o].at[k].at[pl.ds(cc * half + lo, size)]
            return [(at(2 * px + py, c), at(2 * px + py, c), at(2 * px + py, 1 - c), (x, y, 1 - c)) for px, py in chips]

        self._add(3, copies)

    def swap(self, i, o):
        half = self.inputs[i].shape[1] // 2

        def copies(ins, outs):
            x, y, c, _ = _place()
            return [(ins[i].at[:, pl.ds((1 - c) * half, half)], outs[o], outs[o], (x, y, 1 - c))]

        self._add(1, copies)

    def scatter(self, i, o, rows=None):
        lo, size = rows or (0, self.inputs[i].shape[1])

        def copies(ins, outs):
            x, y, c, chips = _place()
            cut = lambda ref: ref.at[pl.ds(lo, size)]
            return [(cut(ins[i].at[2 * px + py]), cut(outs[o].at[2 * x + y]), cut(outs[o].at[2 * px + py]), (px, py, c)) for px, py in chips]

        self._add(3, copies)

    def share(self, o):
        r = self.out_shapes[o].shape[0] // 2

        def copies(ins, outs):
            x, y, c, _ = _place()
            mine = outs[o].at[pl.ds(c * r, r)]
            return [(mine, mine, outs[o].at[pl.ds((1 - c) * r, r)], (x, y, 1 - c))]

        self._add(1, copies)

    def spread(self, i, o):
        def copies(ins, outs):
            x, y, c, _ = _place()
            me = 4 * x + 2 * y + c
            out = []
            for d in range(1, 8):
                to, frm = (me + d) % 8, (me + 8 - d) % 8
                out.append((ins[i], outs[o].at[me], outs[o].at[frm], (to // 4, (to // 2) % 2, to % 2)))
            return out

        self._add(7, copies)

    def _pallas(self, body, n_in, n_out, scratch, **kw):
        k_in, k_out = len(self.inputs), len(self.out_shapes)
        grid = kw.get("grid", ())

        def wrapped(*refs):
            ins, cins = refs[:n_in], refs[n_in:n_in + k_in]
            outs = refs[n_in + k_in:n_in + k_in + n_out]
            couts = refs[n_in + k_in + n_out:n_in + k_in + n_out + k_out]
            own = refs[n_in + k_in + n_out + k_out:]
            send, recv = own[len(scratch):]
            ids = [pl.program_id(a) for a in range(len(grid))]
            first = functools.reduce(jnp.logical_and, [a == 0 for a in ids], True)
            last = functools.reduce(jnp.logical_and, [a == g - 1 for a, g in zip(ids, grid)], True)

            def go(stage):
                for op in self.ops:
                    op[stage](cins, couts, send, recv)

            if grid:
                pl.when(first)(lambda: go(0))
                body(*ins, *outs, *own[:len(scratch)])
                pl.when(last)(lambda: go(1))
            else:
                go(0)
                go(1)

        sem = pltpu.SemaphoreType.DMA((self.n_sems,))
        return pl.pallas_call(
            wrapped, in_specs=list(kw.get("in_specs", [])) + [_ANY] * k_in, out_specs=list(kw.get("out_specs", [])) + [_ANY] * k_out,
            out_shape=list(kw.get("out_shape", [])) + self.out_shapes, scratch_shapes=list(scratch) + [sem, sem],
            input_output_aliases={n_in + i: n_out + o for i, o in self.aliases.items()}, name=kw["name"],
            **({"grid": grid, "compiler_params": _cp(*["arbitrary"] * len(grid))} if grid else {}))

    def run(self, body, kw, *args):
        single = not isinstance(kw["out_shape"], (list, tuple))
        out_shape = [kw["out_shape"]] if single else list(kw["out_shape"])
        out_specs = [kw["out_specs"]] if single else list(kw["out_specs"])
        res = self._pallas(body, len(args), len(out_shape), kw.get("scratch_shapes", []), grid=kw["grid"], in_specs=kw["in_specs"],
                           out_specs=out_specs, out_shape=out_shape, name=kw["name"])(*args, *self.inputs)
        self.results = list(res[len(out_shape):])
        return res[0] if single else list(res[:len(out_shape)])

    def run_alone(self, name):
        self.results = list(self._pallas(None, 0, 0, [], name=name)(*self.inputs))


def _call(comm, body, **kw):
    if comm is None:
        return pl.pallas_call(body, **kw)
    return functools.partial(comm.run, body, kw)


def _mm_nn(a, b3, *, bm, bn, name, also=None, comm=None):
    m, k = a.shape
    c, _, n = b3.shape
    ni, nj = m // bm, n // bn

    def body(a_ref, b_ref, *o_refs):
        res = _dot(a_ref[...], b_ref[...])
        for o_ref in o_refs:
            o_ref[...] = res.astype(o_ref.dtype)

    ospec = pl.BlockSpec((bm, bn), lambda cc, j, i: (i, cc * nj + j))
    dtypes = [F32] + ([] if also is None else [also])
    out = _call(
        comm, body, grid=(c, nj, ni),
        in_specs=[pl.BlockSpec((bm, k), lambda cc, j, i: (i, 0)), pl.BlockSpec((None, k, bn), lambda cc, j, i: (cc, 0, j))],
        out_specs=[ospec] * len(dtypes), out_shape=[S((m, c * n), dt) for dt in dtypes],
        compiler_params=_cp("parallel", "parallel", "parallel"), name=name)(a, b3)
    return out[0] if also is None else out


def _mm_nt(a, b3, *, bm, bo, out_dtype, name, comm=None):
    m = a.shape[0]
    c, ko, n = b3.shape
    ni, nj = m // bm, ko // bo

    def body(a_ref, b_ref, o_ref):
        acc = _dot_nt(a_ref[:, 0:n], b_ref[0])
        for cc in range(1, c):
            acc = acc + _dot_nt(a_ref[:, cc * n:(cc + 1) * n], b_ref[cc])
        o_ref[...] = acc.astype(o_ref.dtype)

    return _call(
        comm, body, grid=(nj, ni),
        in_specs=[pl.BlockSpec((bm, c * n), lambda j, i: (i, 0)),
                  pl.BlockSpec((c, bo, n), lambda j, i: (0, j, 0))],
        out_specs=pl.BlockSpec((bm, bo), lambda j, i: (i, j)),
        out_shape=S((m, ko), out_dtype),
        compiler_params=_cp("parallel", "parallel"), name=name)(a, b3)


def _mm_tn(a, b, c, *, bm, bk, name, comm=None):
    m, k = a.shape
    n = b.shape[1] // c
    nm, nk = m // bm, k // bk

    def body(a_ref, b_ref, o_ref, acc):
        mm = pl.program_id(2)

        @pl.when(mm == 0)
        def _():
            acc[...] = jnp.zeros_like(acc)

        acc[...] += _dot_tn(a_ref[...], b_ref[...])

        @pl.when(mm == nm - 1)
        def _():
            o_ref[...] = acc[...]

    return _call(
        comm, body, grid=(c, nk, nm),
        in_specs=[pl.BlockSpec((bm, bk), lambda cc, j, mm: (mm, j)),
                  pl.BlockSpec((bm, n), lambda cc, j, mm: (mm, cc))],
        out_specs=pl.BlockSpec((None, bk, n), lambda cc, j, mm: (cc, j, 0)),
        out_shape=S((c, k, n), F32),
        scratch_shapes=[pltpu.VMEM((bk, n), F32)],
        compiler_params=_cp("parallel", "parallel", "arbitrary"), name=name)(a, b)


def _lane_pieces(n, parts):
    base, extra = divmod(n // 128, parts)
    sizes = [128 * (base + (p < extra)) for p in range(parts)]
    return [slice(sum(sizes[:p]), sum(sizes[:p + 1])) for p in range(parts) if sizes[p]]


def _swiglu_fwd(hn, wg3, wu3, *, bm, name, comm=None):
    m, k = hn.shape
    c, _, n = wg3.shape

    def body(a_ref, g_ref, u_ref, gate_ref, up_ref, act_ref):
        a = a_ref[...]
        gate = _dot(a, g_ref[...])
        up = _dot(a, u_ref[...])
        gate_ref[...] = gate
        up_ref[...] = up
        act_ref[...] = (gate * jax.nn.sigmoid(gate) * up).astype(act_ref.dtype)

    wspec = pl.BlockSpec((None, k, n), lambda cc, i: (cc, 0, 0))
    ospec = pl.BlockSpec((bm, n), lambda cc, i: (i, cc))
    return _call(
        comm, body, grid=(c, m // bm),
        in_specs=[pl.BlockSpec((bm, k), lambda cc, i: (i, 0)), wspec, wspec],
        out_specs=[ospec, ospec, ospec],
        out_shape=[S((m, c * n), F32), S((m, c * n), F32), S((m, c * n), _MXU)],
        compiler_params=_cp("parallel", "parallel"), name=name)(hn, wg3, wu3)


def _swiglu_bwd(df, wd, gate, up, *, bm, bo, name):
    m, k = df.shape
    ko = wd.shape[0]

    def body(a_ref, b_ref, g_ref, u_ref, dg_ref, du_ref):
        a = a_ref[...]
        for cols in _lane_pieces(bo, 4):
            dact = _dot_nt(a, b_ref[cols, :])
            gate = g_ref[:, cols]
            sg = jax.nn.sigmoid(gate)
            dg_ref[:, cols] = (dact * u_ref[:, cols] * (sg * (1.0 + gate * (1.0 - sg)))).astype(dg_ref.dtype)
            du_ref[:, cols] = (dact * (gate * sg)).astype(du_ref.dtype)

    ospec = pl.BlockSpec((bm, bo), lambda j, i: (i, j))
    return pl.pallas_call(
        body, grid=(ko // bo, m // bm),
        in_specs=[pl.BlockSpec((bm, k), lambda j, i: (i, 0)), pl.BlockSpec((bo, k), lambda j, i: (j, 0)), ospec, ospec],
        out_specs=[ospec, ospec],
        out_shape=[S((m, ko), _MXU), S((m, ko), _MXU)],
        compiler_params=_cp("parallel", "parallel"), name=name)(df, wd, gate, up)


def _rms_fwd(x, gain, name, comm=None):
    t, d = x.shape
    tm = min(t, ROW_TILE)

    def body(x_ref, g_ref, o_ref):
        xv = x_ref[...]
        o_ref[...] = ((xv * _rstd(xv)) * g_ref[...]).astype(o_ref.dtype)

    return _call(comm, body, grid=(t // tm,), in_specs=[_row_spec(tm, d), _vec_spec(d)], out_specs=_row_spec(tm, d),
                          out_shape=S((t, d), _MXU), compiler_params=_cp("parallel"), name=name)(x, gain)


def _outnorm_fwd(o, yl, ga, gl, name):
    t, w = o.shape
    tm = min(t, ROW_TILE)

    def body(o_ref, l_ref, ga_ref, gl_ref, y_ref):
        ov, lv = o_ref[...], l_ref[...]
        y_ref[:, :w] = ((ov * _rstd(ov)) * ga_ref[...]).astype(y_ref.dtype)
        y_ref[:, w:] = ((lv * _rstd(lv)) * gl_ref[...]).astype(y_ref.dtype)

    return pl.pallas_call(body, grid=(t // tm,), in_specs=[_row_spec(tm, w), _row_spec(tm, w), _vec_spec(w), _vec_spec(w)],
                          out_specs=_row_spec(tm, 2 * w), out_shape=S((t, 2 * w), _MXU),
                          compiler_params=_cp("parallel"), name=name)(o, yl, ga, gl)


def _mid_fwd(x, mix, g_post, g_pre, name, comm=None):
    t, d = x.shape
    tm = min(t, ROW_TILE)

    def body(x_ref, m_ref, gp_ref, gn_ref, x2_ref, hn_ref):
        mv = m_ref[...]
        x2 = x_ref[...] + (mv * _rstd(mv)) * gp_ref[...]
        x2_ref[...] = x2
        hn_ref[...] = ((x2 * _rstd(x2)) * gn_ref[...]).astype(hn_ref.dtype)

    return _call(comm, body, grid=(t // tm,), in_specs=[_row_spec(tm, d), _row_spec(tm, d), _vec_spec(d), _vec_spec(d)],
                          out_specs=[_row_spec(tm, d), _row_spec(tm, d)], out_shape=[S((t, d), F32), S((t, d), _MXU)],
                          compiler_params=_cp("parallel"), name=name)(x, mix, g_post, g_pre)


def _final(f, x2, target, g_post, name):
    t, d = f.shape
    tm = min(t, ROW_TILE)

    def body(f_ref, x2_ref, t_ref, g_ref, loss_ref, dout_ref, df_ref, dg_ref):
        @pl.when(pl.program_id(0) == 0)
        def _():
            loss_ref[...] = jnp.zeros_like(loss_ref)
            dg_ref[...] = jnp.zeros_like(dg_ref)

        fv = f_ref[...]
        r = _rstd(fv)
        fh = fv * r
        err = (x2_ref[...] + fh * g_ref[...]) - t_ref[...]
        loss_ref[...] += jnp.sum(err * err, axis=0, keepdims=True)
        dout = err * (1.0 / d)
        dout_ref[...] = dout
        dfv, dg = _rms_bwd(dout, fh, r, g_ref[...])
        df_ref[...] = dfv.astype(df_ref.dtype)
        dg_ref[...] += dg

    return pl.pallas_call(
        body, grid=(t // tm,),
        in_specs=[_row_spec(tm, d), _row_spec(tm, d), _row_spec(tm, d), _vec_spec(d)],
        out_specs=[_vec_spec(d), _row_spec(tm, d), _row_spec(tm, d), _vec_spec(d)],
        out_shape=[S((1, d), F32), S((t, d), F32), S((t, d), _MXU), S((1, d), F32)],
        compiler_params=_cp("arbitrary"), name=name)(f, x2, target, g_post)


def _mid_bwd(dhn_a, dhn_b, dout, x2, mix, g_pre, g_post, name, comm=None):
    t, d = x2.shape
    tm = min(t, ROW_TILE)

    def body(da_ref, db_ref, do_ref, x2_ref, m_ref, gn_ref, gp_ref, dx2_ref, dm_ref, dgn_ref, dgp_ref):
        @pl.when(pl.program_id(0) == 0)
        def _():
            dgn_ref[...] = jnp.zeros_like(dgn_ref)
            dgp_ref[...] = jnp.zeros_like(dgp_ref)

        x2 = x2_ref[...]
        r = _rstd(x2)
        dxa, dgn = _rms_bwd(da_ref[...] + db_ref[...], x2 * r, r, gn_ref[...])
        dx2 = do_ref[...] + dxa
        dx2_ref[...] = dx2
        dgn_ref[...] += dgn
        mv = m_ref[...]
        rm = _rstd(mv)
        dmv, dgp = _rms_bwd(dx2, mv * rm, rm, gp_ref[...])
        dm_ref[...] = dmv.astype(dm_ref.dtype)
        dgp_ref[...] += dgp

    rs, vs = _row_spec(tm, d), _vec_spec(d)
    return _call(
        comm, body, grid=(t // tm,), in_specs=[rs, rs, rs, rs, rs, vs, vs], out_specs=[rs, rs, vs, vs],
        out_shape=[S((t, d), F32), S((t, d), _MXU), S((1, d), F32), S((1, d), F32)],
        compiler_params=_cp("arbitrary"), name=name)(dhn_a, dhn_b, dout, x2, mix, g_pre, g_post)


def _first_bwd(dhn, dx2, x, gain, name, comm=None):
    t, d = x.shape
    tm = min(t, ROW_TILE)

    def body(dh_ref, dx2_ref, x_ref, g_ref, dx_ref, dg_ref):
        @pl.when(pl.program_id(0) == 0)
        def _():
            dg_ref[...] = jnp.zeros_like(dg_ref)

        xv = x_ref[...]
        r = _rstd(xv)
        dxa, dg = _rms_bwd(dh_ref[...], xv * r, r, g_ref[...])
        dx_ref[...] = dx2_ref[...] + dxa
        dg_ref[...] += dg

    rs, vs = _row_spec(tm, d), _vec_spec(d)
    return _call(comm, body, grid=(t // tm,), in_specs=[rs, rs, rs, vs], out_specs=[rs, vs],
                          out_shape=[S((t, d), F32), S((1, d), F32)], compiler_params=_cp("arbitrary"), name=name)(dhn, dx2, x, gain)


def _outnorm_bwd(dy, o, yl, ga, gl, name, comm=None):
    t, w = o.shape
    tm = min(t, ROW_TILE)

    def body(dy_ref, o_ref, l_ref, ga_ref, gl_ref, do_ref, dl_ref, dga_ref, dgl_ref):
        @pl.when(pl.program_id(0) == 0)
        def _():
            dga_ref[...] = jnp.zeros_like(dga_ref)
            dgl_ref[...] = jnp.zeros_like(dgl_ref)

        ov, lv = o_ref[...], l_ref[...]
        ra, rl = _rstd(ov), _rstd(lv)
        dov, dga = _rms_bwd(dy_ref[:, :w], ov * ra, ra, ga_ref[...])
        dlv, dgl = _rms_bwd(dy_ref[:, w:], lv * rl, rl, gl_ref[...])
        do_ref[...] = dov.astype(do_ref.dtype)
        dl_ref[...] = dlv
        dga_ref[...] += dga
        dgl_ref[...] += dgl

    rs, vs = _row_spec(tm, w), _vec_spec(w)
    return _call(comm, body, grid=(t // tm,), in_specs=[_row_spec(tm, 2 * w), rs, rs, vs, vs], out_specs=[rs, rs, vs, vs],
                          out_shape=[S((t, w), _MXU), S((t, w), F32), S((1, w), F32), S((1, w), F32)],
                          compiler_params=_cp("arbitrary"), name=name)(dy, o, yl, ga, gl)


def _split_dot(v, tri):
    hi = v.astype(_MXU)
    lo = (v - hi.astype(F32)).astype(_MXU)
    return _dot(hi, tri) + _dot(lo, tri)


def _attn_tile(qb, kb, row, col, shift, scale):
    z = _dot_nt(qb, kb) * scale
    mask = (col + shift) < row
    lb = _log_sigmoid(z)
    lm = jnp.where(mask, lb - z, 0.0)
    return mask, lb, lm


def _attn_fwd(proj, n_heads, name, comm=None):
    t = proj.shape[0]
    bq = min(t, ATTN_BLOCK)
    nq = t // bq
    scale = 1.0 / math.sqrt(HEAD_DIM)

    heads = [slice(a * HEAD_DIM, (a + 1) * HEAD_DIM) for a in range(ATTN_HEADS)]

    def body(q_ref, k_ref, v_ref, o_ref):
        row = lax.broadcasted_iota(jnp.int32, (bq, bq), 0)
        col = lax.broadcasted_iota(jnp.int32, (bq, bq), 1)
        tri = (row > col).astype(_MXU)

        def per_q(qi, _):
            q0 = pl.multiple_of(qi * bq, bq)
            qbs = [q_ref[pl.ds(q0, bq), hd] for hd in heads]

            def cond(st):
                return jnp.logical_and(st[0] >= 0, st[1])

            def step(st):
                kj, _, carries, accs = st
                k0 = pl.multiple_of(kj * bq, bq)
                alive, new_carries, new_accs = None, [], []
                for hd, qb, carry, acc in zip(heads, qbs, carries, accs):
                    mask, lb, lm = _attn_tile(qb, k_ref[pl.ds(k0, bq), hd], row, col, (kj - qi) * bq, scale)
                    w = jnp.where(mask, jnp.exp(lb + _split_dot(lm, tri) + carry), 0.0)
                    new_accs.append(acc + _dot(w.astype(_MXU), v_ref[pl.ds(k0, bq), hd]))
                    carry = carry + jnp.sum(lm, axis=1, keepdims=True)
                    new_carries.append(carry)
                    live = jnp.max(carry) > EXP_CUT
                    alive = live if alive is None else jnp.logical_or(alive, live)
                return kj - 1, alive, tuple(new_carries), tuple(new_accs)

            st = lax.while_loop(cond, step, (qi, jnp.bool_(True), (jnp.zeros((bq, 1), F32),) * ATTN_HEADS,
                                             (jnp.zeros((bq, HEAD_DIM), F32),) * ATTN_HEADS))
            for hd, acc in zip(heads, st[3]):
                o_ref[pl.ds(q0, bq), hd] = acc
            return 0

        lax.fori_loop(0, nq, per_q, 0)

    groups = n_heads // ATTN_HEADS
    hs = lambda off: pl.BlockSpec((t, ATTN_HEADS * HEAD_DIM), lambda h: (0, off + h))
    return _call(
        comm, body, grid=(groups,), in_specs=[hs(0), hs(groups), hs(2 * groups)], out_specs=hs(0),
        out_shape=S((t, n_heads * HEAD_DIM), F32), compiler_params=_cp("parallel"), name=name)(proj, proj, proj)


def _attn_bwd(proj, do, n_heads, name, comm=None):
    t = proj.shape[0]
    bq = min(t, ATTN_BLOCK)
    nq = t // bq
    scale = 1.0 / math.sqrt(HEAD_DIM)

    heads = [slice(a * HEAD_DIM, (a + 1) * HEAD_DIM) for a in range(ATTN_HEADS)]

    def body(q_ref, k_ref, v_ref, do_ref, dq_ref, dk_ref, dv_ref, dka_ref, dva_ref, g_ref, b_ref):
        dka_ref[...] = jnp.zeros_like(dka_ref)
        dva_ref[...] = jnp.zeros_like(dva_ref)
        row = lax.broadcasted_iota(jnp.int32, (bq, bq), 0)
        col = lax.broadcasted_iota(jnp.int32, (bq, bq), 1)
        tri = (row > col).astype(_MXU)
        tri_lt = (row < col).astype(_MXU)

        def per_q(qi, _):
            q0 = pl.multiple_of(qi * bq, bq)
            qbs = [q_ref[pl.ds(q0, bq), hd] for hd in heads]
            dobs = [do_ref[pl.ds(q0, bq), hd] for hd in heads]

            def cond(st):
                return jnp.logical_and(st[0] >= 0, st[1])

            def step(st):
                kj, _, carries = st
                k0 = pl.multiple_of(kj * bq, bq)
                alive, new_carries = None, []
                for a, (hd, qb, dob, carry) in enumerate(zip(heads, qbs, dobs, carries)):
                    mask, lb, lm = _attn_tile(qb, k_ref[pl.ds(k0, bq), hd], row, col, (kj - qi) * bq, scale)
                    w = jnp.where(mask, jnp.exp(lb + _split_dot(lm, tri) + carry), 0.0)
                    g_ref[a, pl.ds(k0, bq), :] = w * _dot_nt(dob, v_ref[pl.ds(k0, bq), hd])
                    b_ref[a, pl.ds(k0, bq), :] = jnp.where(mask, jnp.exp(lb), 0.0)
                    dva_ref[pl.ds(k0, bq), hd] += _dot_tn(w.astype(_MXU), dob)
                    carry = carry + jnp.sum(lm, axis=1, keepdims=True)
                    new_carries.append(carry)
                    live = jnp.max(carry) > EXP_CUT
                    alive = live if alive is None else jnp.logical_or(alive, live)
                return kj - 1, alive, tuple(new_carries)

            st = lax.while_loop(cond, step, (qi, jnp.bool_(True), (jnp.zeros((bq, 1), F32),) * ATTN_HEADS))

            def back(kj, st2):
                k0 = pl.multiple_of(kj * bq, bq)
                out = []
                for a, (hd, qb, (before, dq)) in enumerate(zip(heads, qbs, st2)):
                    g = g_ref[a, pl.ds(k0, bq), :]
                    beta = b_ref[a, pl.ds(k0, bq), :]
                    dz = ((g * (1.0 - beta) - (before + _split_dot(g, tri_lt)) * beta) * scale).astype(_MXU)
                    dka_ref[pl.ds(k0, bq), hd] += _dot_tn(dz, qb)
                    out.append((before + jnp.sum(g, axis=1, keepdims=True), dq + _dot(dz, k_ref[pl.ds(k0, bq), hd])))
                return tuple(out)

            st2 = lax.fori_loop(st[0] + 1, qi + 1, back, ((jnp.zeros((bq, 1), F32), jnp.zeros((bq, HEAD_DIM), F32)),) * ATTN_HEADS)
            for hd, (_, dq) in zip(heads, st2):
                dq_ref[pl.ds(q0, bq), hd] = dq.astype(dq_ref.dtype)
            return 0

        lax.fori_loop(0, nq, per_q, 0)
        dk_ref[...] = dka_ref[...].astype(dk_ref.dtype)
        dv_ref[...] = dva_ref[...].astype(dv_ref.dtype)

    groups = n_heads // ATTN_HEADS
    wide = ATTN_HEADS * HEAD_DIM
    hs = lambda off: pl.BlockSpec((t, wide), lambda h: (0, off + h))
    return _call(
        comm, body, grid=(groups,), in_specs=[hs(0), hs(groups), hs(2 * groups), hs(0)], out_specs=[hs(0), hs(0), hs(0)],
        out_shape=[S((t, n_heads * HEAD_DIM), _MXU)] * 3,
        scratch_shapes=[pltpu.VMEM((t, wide), F32), pltpu.VMEM((t, wide), F32),
                        pltpu.VMEM((ATTN_HEADS, t, bq), F32), pltpu.VMEM((ATTN_HEADS, t, bq), F32)],
        compiler_params=_cp("parallel"), name=name)(proj, proj, proj, do)


def _shift_down(cur, prev8, k):
    if k == 0:
        return cur
    row8 = lax.broadcasted_iota(jnp.int32, prev8.shape, 0)
    rc = pltpu.roll(cur, k, 0)
    top = jnp.where(row8 < k, pltpu.roll(prev8, k, 0), rc[0:8, :])
    return jnp.concatenate([top, rc[8:, :]], axis=0)


def _shift_up(cur, next8, k):
    if k == 0:
        return cur
    n = cur.shape[0]
    row8 = lax.broadcasted_iota(jnp.int32, next8.shape, 0)
    rc = pltpu.roll(cur, n - k, 0)
    bottom = jnp.where(row8 >= 8 - k, pltpu.roll(next8, 8 - k, 0), rc[n - 8:, :])
    return jnp.concatenate([rc[:n - 8, :], bottom], axis=0)


def _lru_gates(xl, prev8, cw, cb, wr, br, wi, bi, ls):
    xs = [_shift_down(xl, prev8, CONV_WIDTH - 1 - k) for k in range(CONV_WIDTH)]
    xc = xs[0] * cw[0:1, :]
    for k in range(1, CONV_WIDTH):
        xc = xc + xs[k] * cw[k:k + 1, :]
    xc = xc + cb
    xcb = xc.astype(_MXU)
    r = jax.nn.sigmoid(_dot(xcb, wr) + br)
    i = jax.nn.sigmoid(_dot(xcb, wi) + bi)
    la = (LRU_C * r) * ls
    a = jnp.exp(la)
    mult = jnp.sqrt(-_expm1(2.0 * la))
    return xs, xc, r, i, a, mult


def _group_scan(a, b, reverse):
    n = a.shape[0]
    row = lax.broadcasted_iota(jnp.int32, a.shape, 0) % 8
    for d in (1, 2, 4):
        if reverse:
            m = row < 8 - d
            a_s, b_s = pltpu.roll(a, n - d, 0), pltpu.roll(b, n - d, 0)
        else:
            m = row >= d
            a_s, b_s = pltpu.roll(a, d, 0), pltpu.roll(b, d, 0)
        b = jnp.where(m, a * b_s + b, b)
        a = jnp.where(m, a * a_s, a)
    return a, b


def _lru_fwd(proj, col0, n_blocks, cw, cb, wr, br, wi, bi, lam, name, comm=None):
    t = proj.shape[0]
    tt = min(t, SEQ_TILE)
    nt = t // tt

    def body(xl_ref, gl_ref, cw_ref, cb_ref, wr_ref, br_ref, wi_ref, bi_ref, lam_ref, h_ref, y_ref):
        cwv, cbv, brv, biv = cw_ref[...], cb_ref[...], br_ref[...], bi_ref[...]
        wrv, wiv = wr_ref[...].astype(_MXU), wi_ref[...].astype(_MXU)
        ls = _log_sigmoid(lam_ref[...])

        def tile(ti, hin):
            t0 = pl.multiple_of(ti * tt, tt)
            p0 = pl.multiple_of(jnp.maximum(t0 - 8, 0), 8)
            prev8 = xl_ref[pl.ds(p0, 8), :] * (ti > 0).astype(F32)
            xl = xl_ref[pl.ds(t0, tt), :]
            _, xc, _, ig, a, mult = _lru_gates(xl, prev8, cwv, cbv, wrv, brv, wiv, biv, ls)
            ga, gb = _group_scan(a, mult * (ig * xc), False)
            for g in range(tt // 8):
                hg = ga[8 * g:8 * g + 8, :] * hin + gb[8 * g:8 * g + 8, :]
                h_ref[pl.ds(t0 + 8 * g, 8), :] = hg
                hin = hg[7:8, :]
            y_ref[pl.ds(t0, tt), :] = h_ref[pl.ds(t0, tt), :] * _gelu(gl_ref[pl.ds(t0, tt), :])
            return hin

        lax.fori_loop(0, nt, tile, jnp.zeros((1, HEAD_DIM), F32))

    cs = lambda off: pl.BlockSpec((t, HEAD_DIM), lambda n: (0, off + n))
    vs = pl.BlockSpec((1, HEAD_DIM), lambda n: (0, n))
    ws = pl.BlockSpec((None, HEAD_DIM, HEAD_DIM), lambda n: (n, 0, 0))
    w = n_blocks * HEAD_DIM
    return _call(
        comm, body, grid=(n_blocks,),
        in_specs=[cs(col0), cs(col0 + n_blocks), pl.BlockSpec((CONV_WIDTH, HEAD_DIM), lambda n: (0, n)), vs, ws, vs, ws, vs, vs],
        out_specs=[cs(0), cs(0)], out_shape=[S((t, w), F32), S((t, w), F32)],
        compiler_params=_cp("parallel"), name=name)(proj, proj, cw, cb, wr, br, wi, bi, lam)


def _lru_bwd(proj, col0, n_blocks, h, dyl, cw, cb, wr, br, wi, bi, lam, name, comm=None):
    t = proj.shape[0]
    tt = min(t, SEQ_TILE)
    nt = t // tt

    def body(xl_ref, gl_ref, h_ref, dy_ref, cw_ref, cb_ref, wr_ref, br_ref, wi_ref, bi_ref, lam_ref,
             dxl_ref, dgl_ref, dcw_ref, dcb_ref, dwr_ref, dbr_ref, dwi_ref, dbi_ref, dlam_ref, g_ref):
        cwv, cbv, brv, biv = cw_ref[...], cb_ref[...], br_ref[...], bi_ref[...]
        wrv, wiv = wr_ref[...].astype(_MXU), wi_ref[...].astype(_MXU)
        lamv = lam_ref[...]
        ls = _log_sigmoid(lamv)
        for ref in (dcw_ref, dcb_ref, dwr_ref, dbr_ref, dwi_ref, dbi_ref, dlam_ref):
            ref[...] = jnp.zeros_like(ref)

        def tile(s, carry):
            e_in, dxc_next8 = carry
            ti = nt - 1 - s
            t0 = pl.multiple_of(ti * tt, tt)
            p0 = pl.multiple_of(jnp.maximum(t0 - 8, 0), 8)
            first = (ti > 0).astype(F32)
            xl = xl_ref[pl.ds(t0, tt), :]
            xs, xc, r, ig, a, mult = _lru_gates(xl, xl_ref[pl.ds(p0, 8), :] * first, cwv, cbv, wrv, brv, wiv, biv, ls)
            hv = h_ref[pl.ds(t0, tt), :]
            h_before = _shift_down(hv, h_ref[pl.ds(p0, 8), :] * first, 1)
            glv = gl_ref[pl.ds(t0, tt), :]
            dyv = dy_ref[pl.ds(t0, tt), :]
            dgl_ref[pl.ds(t0, tt), :] = (dyv * hv * _gelu_grad(glv)).astype(dgl_ref.dtype)
            dh = dyv * _gelu(glv)
            row = lax.broadcasted_iota(jnp.int32, a.shape, 0)
            coef = jnp.where(row == tt - 1, 1.0, pltpu.roll(a, tt - 1, 0))
            ga, gb = _group_scan(coef, dh, True)
            gin = e_in
            for g in reversed(range(tt // 8)):
                gg = ga[8 * g:8 * g + 8, :] * gin + gb[8 * g:8 * g + 8, :]
                g_ref[8 * g:8 * g + 8, :] = gg
                gin = gg[0:1, :]
            gv = g_ref[...]
            e_out = a[0:1, :] * gv[0:1, :]
            ix = ig * xc
            dla = (gv * h_before) * a - (gv * ix) * (a * a / mult)
            dlam_ref[...] += jnp.sum(dla * (LRU_C * r), axis=0, keepdims=True)
            dpr = (dla * (LRU_C * ls)) * (r * (1.0 - r))
            dpi = (gv * mult * xc) * (ig * (1.0 - ig))
            dbr_ref[...] += jnp.sum(dpr, axis=0, keepdims=True)
            dbi_ref[...] += jnp.sum(dpi, axis=0, keepdims=True)
            xcb, dprb, dpib = xc.astype(_MXU), dpr.astype(_MXU), dpi.astype(_MXU)
            dwr_ref[...] += _dot_tn(xcb, dprb)
            dwi_ref[...] += _dot_tn(xcb, dpib)
            dxc = gv * mult * ig + _dot_nt(dprb, wrv) + _dot_nt(dpib, wiv)
            dcb_ref[...] += jnp.sum(dxc, axis=0, keepdims=True)
            dxl = None
            for k in range(CONV_WIDTH):
                dcw_ref[k:k + 1, :] += jnp.sum(dxc * xs[k], axis=0, keepdims=True)
                term = _shift_up(dxc, dxc_next8, CONV_WIDTH - 1 - k) * cwv[k:k + 1, :]
                dxl = term if dxl is None else dxl + term
            dxl_ref[pl.ds(t0, tt), :] = dxl.astype(dxl_ref.dtype)
            return e_out, dxc[0:8, :]

        lax.fori_loop(0, nt, tile, (jnp.zeros((1, HEAD_DIM), F32), jnp.zeros((8, HEAD_DIM), F32)))
        dlam_ref[...] = dlam_ref[...] * (1.0 - jax.nn.sigmoid(lamv))

    cs = lambda off: pl.BlockSpec((t, HEAD_DIM), lambda n: (0, off + n))
    vs = pl.BlockSpec((1, HEAD_DIM), lambda n: (0, n))
    ws = pl.BlockSpec((None, HEAD_DIM, HEAD_DIM), lambda n: (n, 0, 0))
    cws = pl.BlockSpec((CONV_WIDTH, HEAD_DIM), lambda n: (0, n))
    w = n_blocks * HEAD_DIM
    vec = S((1, w), F32)
    mat = S((n_blocks, HEAD_DIM, HEAD_DIM), F32)
    return _call(
        comm, body, grid=(n_blocks,),
        in_specs=[cs(col0), cs(col0 + n_blocks), cs(0), cs(0), cws, vs, ws, vs, ws, vs, vs],
        out_specs=[cs(0), cs(0), cws, vs, ws, vs, ws, vs, vs],
        out_shape=[S((t, w), _MXU), S((t, w), _MXU), S((CONV_WIDTH, w), F32), vec, mat, vec, mat, vec, vec],
        scratch_shapes=[pltpu.VMEM((tt, HEAD_DIM), F32)],
        compiler_params=_cp("parallel"), name=name)(proj, proj, h, dyl, cw, cb, wr, br, wi, bi, lam)


class _NoExchange:
    def __init__(self, weights):
        self.weights, self.grads, self.packs = weights, {}, {}

    def weight(self, name):
        return self.weights[name]

    def carrier(self, call):
        return None

    def harvest(self, car):
        pass

    def alone(self, call):
        pass


def _local_step(x, target, norms, ex, cw, cb, wr, br, wi, bi, lam, ga, gl):
    g_pre_mix, g_post_mix, g_pre_ffn, g_post_ffn = norms
    t, d = x.shape
    bm = min(t, 512)
    bt = min(t, 2048)

    def run(fn, name, *args, **kw):
        car = ex.carrier(name)
        out = fn(*args, name=name, comm=car, **kw)
        ex.harvest(car)
        return out

    hn1 = run(_rms_fwd, "rms1", x, g_pre_mix)
    win3 = ex.weight("w_in")
    c = win3.shape[0]
    proj, proj_mx = run(_mm_nn, "in_proj", hn1, win3, bm=bm, bn=win3.shape[2], also=_MXU)
    o = run(_attn_fwd, "attn_fwd", proj_mx, (proj.shape[1] - d) // 3 // HEAD_DIM)
    mix = 2 * o.shape[1]
    n_heads = n_blocks = o.shape[1] // HEAD_DIM
    h, yl = run(_lru_fwd, "lru_fwd", proj, 3 * n_heads, n_blocks, cw, cb, wr, br, wi, bi, lam)
    y = _outnorm_fwd(o, yl, ga, gl, "outnorm_fwd")
    wout = ex.weight("w_out")
    mixo = run(_mm_nn, "out_proj", y, wout[None], bm=bm, bn=d)
    x2, hn2 = run(_mid_fwd, "mid_fwd", x, mixo, g_post_mix, g_pre_ffn)
    wg3, wu3 = ex.weight("w_ffn_gate"), ex.weight("w_ffn_up")
    gate, up, act = run(_swiglu_fwd, "ffn_gate_up", hn2, wg3, wu3, bm=bm)
    ex.alone("gather_w_down")
    wd = ex.weight("w_ffn_down")
    ff = wd.shape[0]
    f = _mm_nn(act, wd[None], bm=bm, bn=d // 2, name="ffn_down")
    loss_cols, dout, df, dg_post_ffn = _final(f, x2, target, g_post_ffn, "final")

    dgate, dup = _swiglu_bwd(df, wd, gate, up, bm=bm, bo=ff // 4, name="ffn_down_bwd")
    ex.grads["w_ffn_down"] = _mm_tn(act, df, 1, bm=bt, bk=512, name="ffn_down_dw").reshape(c, ff // c, d)
    ex.grads["w_ffn_gate"] = run(_mm_tn, "ffn_gate_dw", hn2, dgate, c, bm=bt, bk=d // 2)
    ex.grads["w_ffn_up"] = run(_mm_tn, "ffn_up_dw", hn2, dup, c, bm=bt, bk=d // 2)
    dhn2_g = run(_mm_nt, "ffn_gate_dx", dgate, wg3, bm=bm, bo=d // 2, out_dtype=F32)
    dhn2_u = run(_mm_nt, "ffn_up_dx", dup, wu3, bm=bm, bo=d // 2, out_dtype=F32)
    dx2, dmix, dg_pre_ffn, dg_post_mix = run(_mid_bwd, "mid_bwd", dhn2_g, dhn2_u, dout, x2, mixo, g_pre_ffn, g_post_mix)
    dy = run(_mm_nt, "out_proj_dx", dmix, wout[None], bm=bm, bo=mix, out_dtype=F32)
    ex.grads["w_out"] = _mm_tn(y, dmix, 1, bm=bt, bk=mix // 4, name="out_proj_dw").reshape(c, mix // c, d)
    do, dyl, dga, dgl_norm = run(_outnorm_bwd, "outnorm_bwd", dy, o, yl, ga, gl)
    dxl, dglu, dcw, dcb, dwr, dbr, dwi, dbi, dlam = run(_lru_bwd, "lru_bwd", proj, 3 * n_heads, n_blocks, h, dyl, cw, cb, wr, br, wi, bi, lam)
    small = dict(post_mix_norm=dg_post_mix, pre_ffn_norm=dg_pre_ffn, post_ffn_norm=dg_post_ffn, conv_w=dcw, conv_b=dcb,
                 w_rgate=dwr, b_rgate=dbr, w_igate=dwi, b_igate=dbi, lru_lambda=dlam, attn_out_norm=dga, lru_out_norm=dgl_norm)
    ex.packs["early"] = _pack([small[n] for n in _SMALL_EARLY])
    dq, dk, dv = run(_attn_bwd, "attn_bwd", proj_mx, do, n_heads)
    dproj = jnp.concatenate([dq, dk, dv, dxl, dglu], axis=1)
    ex.grads["w_in"] = _mm_tn(hn1, dproj, c, bm=bt, bk=d // 2, name="in_proj_dw")
    ex.alone("grads_w_in_swap")
    dhn1 = run(_mm_nt, "in_proj_dx", dproj, win3, bm=bm, bo=d // 2, out_dtype=F32)
    grad_x, small["pre_mix_norm"] = run(_first_bwd, "first_bwd", dhn1, dx2, x, g_pre_mix)
    ex.packs["late"] = _pack([small["pre_mix_norm"]])
    return loss_cols, grad_x, small


def _into_slot(wsh, slot, dtype, name):
    rows, n = wsh.shape
    rb = _row_block(rows, 256) if rows % 8 == 0 else rows

    def body(s_ref, w_ref, o_ref):
        o_ref[...] = w_ref[...].astype(o_ref.dtype)

    return pl.pallas_call(
        body,
        grid_spec=pltpu.PrefetchScalarGridSpec(
            num_scalar_prefetch=1, grid=(rows // rb,),
            in_specs=[pl.BlockSpec((rb, n), lambda i, s_ref: (i, 0))],
            out_specs=pl.BlockSpec((None, rb, n), lambda i, s_ref: (s_ref[0], i, 0))),
        out_shape=S((4, rows, n), dtype), compiler_params=_cp("parallel"), name=name)(slot, wsh)


class _Exchange:
    SCHEDULE = {
        "gather_w_in": [("ici", "w_in"), ("ici", "conv_w")],
        "rms1": [("d2d", "w_in")],
        "in_proj": [("ici", "w_out"), ("ici", "w_ffn_up", 0)],
        "attn_fwd": [("d2d", "w_out"), ("d2d", "w_ffn_up", 0), ("ici", "w_ffn_gate")],
        "lru_fwd": [("d2d", "w_ffn_gate"), ("ici", "w_ffn_up", 1), ("ici", "w_ffn_up", 2)],
        "out_proj": [("d2d", "w_ffn_up", 1), ("d2d", "w_ffn_up", 2), ("ici", "w_ffn_up", 3)],
        "mid_fwd": [("d2d", "w_ffn_up", 3)],
        "ffn_gate_up": [("ici", "w_ffn_down")],
        "gather_w_down": [("d2d", "w_ffn_down")],
        "ffn_gate_dw": [("swap", "w_ffn_down")],
        "ffn_up_dw": [("scatter", "w_ffn_down", 0), ("scatter", "w_ffn_down", 1), ("scatter", "w_ffn_down", 2), ("swap", "w_ffn_gate")],
        "ffn_gate_dx": [("scatter", "w_ffn_down", 3), ("scatter", "w_ffn_gate", 0), ("scatter", "w_ffn_gate", 1), ("swap", "w_ffn_up")],
        "ffn_up_dx": [("share", "w_ffn_down"), ("scatter", "w_ffn_gate", 2), ("scatter", "w_ffn_gate", 3), ("scatter", "w_ffn_up", 0)],
        "mid_bwd": [("share", "w_ffn_gate"), ("scatter", "w_ffn_up", 1), ("scatter", "w_ffn_up", 2)],
        "out_proj_dx": [("scatter", "w_ffn_up", 3)],
        "outnorm_bwd": [("share", "w_ffn_up"), ("swap", "w_out")],
        "lru_bwd": [("scatter", "w_out")],
        "attn_bwd": [("share", "w_out"), ("spread", "early")],
        "grads_w_in_swap": [("swap", "w_in")],
        "in_proj_dx": [("scatter", "w_in")],
        "grads_w_in_share": [("share", "w_in"), ("spread", "late")],
    }
    PIECES = 4

    def __init__(self, slots, place):
        self.buf, self.place = dict(slots), place
        self.grads, self.packs, self.swapped, self.part, self.scattered, self.full, self.spreaded = {}, {}, {}, {}, {}, {}, {}

    def weight(self, name):
        b = self.buf[name]
        return b.reshape(-1, b.shape[2]) if name in ("w_out", "w_ffn_down") else b

    def carrier(self, call):
        if call not in self.SCHEDULE:
            return None
        car = _Carrier()
        car.todo, slot = [], {}
        for kind, name, *piece in self.SCHEDULE[call]:
            if kind in ("ici", "d2d"):
                if name not in slot:
                    slot[name] = car.inplace(self.buf[name])
                    car.todo.append((self.buf, name, slot[name]))
                size = self.buf[name].shape[1] // 2 // self.PIECES
                rows = (piece[0] * size, size) if piece else None
                if kind == "ici":
                    car.gather_ici(slot[name], rows, split=name != "conv_w")
                else:
                    car.gather_d2d(slot[name], rows)
            elif kind == "swap":
                g = self.grads[name]
                o = car.fresh((4, g.shape[1] // 2, g.shape[2]), F32)
                car.swap(car.read(g), o)
                car.todo.append((self.swapped, name, o))
            elif kind == "scatter":
                if name not in self.part:
                    self.part[name] = _add_own_half(self.grads[name], self.swapped[name], self.place[1:], "grads_add_" + name)
                p = self.part[name]
                key = ("scatter", name)
                if key not in slot:
                    slot[key] = (car.read(p), car.inplace(self.scattered[name]) if name in self.scattered else car.fresh(p.shape, p.dtype))
                    car.todo.append((self.scattered, name, slot[key][1]))
                size = p.shape[1] // self.PIECES
                car.scatter(*slot[key], (piece[0] * size, size) if piece else None)
            elif kind == "share":
                o = car.inplace(_sum_chips(self.part[name], self.scattered[name], self.place, "grads_sum_" + name))
                car.share(o)
                car.todo.append((self.full, name, o))
            else:
                o = car.fresh((8,) + self.packs[name].shape, F32)
                car.spread(car.read(self.packs[name]), o)
                car.todo.append((self.spreaded, name, o))
        return car

    def harvest(self, car):
        for state, name, o in (car.todo if car is not None else []):
            state[name] = car.results[o]

    def alone(self, call):
        car = self.carrier(call)
        car.run_alone(call)
        self.harvest(car)

    def small_sum(self, key):
        return _sum_devices(self.packs[key], self.spreaded[key], 2 * self.place[0:1] + self.place[1:], "grads_small_sum_" + key)


def _row_block(rows, cap):
    return max(b for b in range(8, cap + 1, 8) if rows % b == 0)


def _add_own_half(g, recv, core, name):
    _, rows, n = g.shape
    half = rows // 2
    rb = _row_block(half, 512)
    nb = half // rb

    def body(c_ref, g_ref, r_ref, o_ref):
        o_ref[...] = (g_ref[...] + r_ref[...]).astype(o_ref.dtype)

    return pl.pallas_call(
        body,
        grid_spec=pltpu.PrefetchScalarGridSpec(
            num_scalar_prefetch=1, grid=(4, nb),
            in_specs=[pl.BlockSpec((None, rb, n), lambda k, i, c_ref: (k, c_ref[0] * nb + i, 0)),
                      pl.BlockSpec((None, rb, n), lambda k, i, c_ref: (k, i, 0))],
            out_specs=pl.BlockSpec((None, rb, n), lambda k, i, c_ref: (k, i, 0))),
        out_shape=S((4, half, n), BF16), compiler_params=_cp("parallel", "parallel"), name=name)(core, g, recv)


def _sum_chips(part, recv, place, name):
    _, rows, n = part.shape
    rb = _row_block(rows, 64)
    nb = rows // rb

    def body(p_ref, own_ref, r0, r1, r2, r3, o_ref):
        own = own_ref[...].astype(F32)
        terms = [jnp.where(p_ref[0] == k, own, r[...].astype(F32)) for k, r in enumerate((r0, r1, r2, r3))]
        o_ref[...] = ((terms[0] + terms[1]) + terms[2]) + terms[3]

    def slot(k):
        return pl.BlockSpec((None, rb, n), lambda i, p_ref: (jnp.where(p_ref[0] == k, (k + 1) % 4, k), i, 0))

    return pl.pallas_call(
        body,
        grid_spec=pltpu.PrefetchScalarGridSpec(
            num_scalar_prefetch=1, grid=(nb,),
            in_specs=[pl.BlockSpec((None, rb, n), lambda i, p_ref: (p_ref[0], i, 0))] + [slot(k) for k in range(4)],
            out_specs=pl.BlockSpec((rb, n), lambda i, p_ref: (p_ref[1] * nb + i, 0))),
        out_shape=S((2 * rows, n), F32), compiler_params=_cp("parallel"), name=name)(place, part, recv, recv, recv, recv)


def _sum_devices(own, spread, me, name):
    rows = own.shape[0]

    def body(me_ref, own_ref, *refs):
        acc = None
        for k, r in enumerate(refs[:8]):
            term = jnp.where(me_ref[0] == k, own_ref[...], r[...])
            acc = term if acc is None else acc + term
        refs[8][...] = acc

    def slot(k):
        return pl.BlockSpec((None, rows, 128), lambda i, me_ref: (jnp.where(me_ref[0] == k, (k + 1) % 8, k), 0, 0))

    whole = pl.BlockSpec((rows, 128), lambda i, me_ref: (0, 0))
    return pl.pallas_call(
        body,
        grid_spec=pltpu.PrefetchScalarGridSpec(num_scalar_prefetch=1, grid=(1,), in_specs=[whole] + [slot(k) for k in range(8)],
                                               out_specs=whole),
        out_shape=S((rows, 128), F32), compiler_params=_cp("arbitrary"), name=name)(me, own, *[spread] * 8)


def _adamw(w, g, m, v, name):
    rows, n = w.shape
    rb = rows if rows * n * 4 <= (1 << 21) else _row_block(rows, 128)
    c1 = 1.0 - ADAM_B1 ** ADAM_STEP
    c2 = 1.0 - ADAM_B2 ** ADAM_STEP

    def body(w_ref, g_ref, m_ref, v_ref, d_ref, nm_ref, nv_ref):
        gv = g_ref[...]
        nm = ADAM_B1 * m_ref[...] + (1.0 - ADAM_B1) * gv
        nv = ADAM_B2 * v_ref[...] + (1.0 - ADAM_B2) * (gv * gv)
        nm_ref[...] = nm
        nv_ref[...] = nv
        d_ref[...] = -ADAM_LR * ((nm / c1) / (jnp.sqrt(nv / c2) + ADAM_EPS) + ADAM_WD * w_ref[...])

    bs = pl.BlockSpec((rb, n), lambda i: (i, 0))
    return pl.pallas_call(body, grid=(rows // rb,), in_specs=[bs] * 4, out_specs=[bs] * 3, out_shape=[S((rows, n), F32)] * 3,
                          compiler_params=_cp("parallel"), name=name)(w, g, m, v)


_BIG = ("w_in", "w_out", "w_ffn_gate", "w_ffn_up", "w_ffn_down")
_SMALL = ("pre_mix_norm", "post_mix_norm", "pre_ffn_norm", "post_ffn_norm", "conv_w", "conv_b", "w_rgate", "b_rgate",
          "w_igate", "b_igate", "lru_lambda", "attn_out_norm", "lru_out_norm")
_SMALL_EARLY = _SMALL[1:]
_WEIGHTS = ("pre_mix_norm", "post_mix_norm", "pre_ffn_norm", "post_ffn_norm", "w_in", "conv_w", "conv_b", "w_rgate", "b_rgate",
            "w_igate", "b_igate", "lru_lambda", "attn_out_norm", "lru_out_norm", "w_out", "w_ffn_gate", "w_ffn_up", "w_ffn_down")


def _pack(arrays):
    flat = []
    for a in arrays:
        f = a.reshape(-1)
        flat.append(jnp.pad(f, (0, (-f.shape[0]) % 1024)))
    return jnp.concatenate(flat).reshape(-1, 128)


def _unpack(packed, shapes):
    out, pos = [], 0
    flat = packed.reshape(-1)
    for s in shapes:
        size = math.prod(s)
        out.append(flat[pos:pos + size].reshape(s))
        pos += size + (-size) % 1024
    return out


def kernel(x, pre_mix_norm, post_mix_norm, pre_ffn_norm, post_ffn_norm, w_in, conv_w, conv_b, w_rgate, b_rgate, w_igate, b_igate, lru_lambda, attn_out_norm, lru_out_norm, w_out, w_ffn_gate, w_ffn_up, w_ffn_down, loss_target, m_pre_mix_norm, m_post_mix_norm, m_pre_ffn_norm, m_post_ffn_norm, m_w_in, m_conv_w, m_conv_b, m_w_rgate, m_b_rgate, m_w_igate, m_b_igate, m_lru_lambda, m_attn_out_norm, m_lru_out_norm, m_w_out, m_w_ffn_gate, m_w_ffn_up, m_w_ffn_down, v_pre_mix_norm, v_post_mix_norm, v_pre_ffn_norm, v_post_ffn_norm, v_w_in, v_conv_w, v_conv_b, v_w_rgate, v_b_rgate, v_w_igate, v_b_igate, v_lru_lambda, v_attn_out_norm, v_lru_out_norm, v_w_out, v_w_ffn_gate, v_w_ffn_up, v_w_ffn_down):
    given = dict(locals())
    w = {n: given[n][0] for n in _WEIGHTS}
    m = {n: given["m_" + n][0] for n in _WEIGHTS}
    v = {n: given["v_" + n][0] for n in _WEIGHTS}
    xs, target = x[0], loss_target[0]
    d = xs.shape[1]
    chip = (2 * lax.axis_index("x") + lax.axis_index("y")).astype(jnp.int32)
    place = jnp.stack([chip, lax.axis_index("c").astype(jnp.int32)])

    slots = {n: _into_slot(w[n], place[0:1], _MXU, "slot_" + n) for n in _BIG}
    slots["conv_w"] = _into_slot(w["conv_w"], place[0:1], F32, "slot_conv_w")
    ex = _Exchange(slots, place)
    ex.alone("gather_w_in")
    conv_full = jnp.transpose(ex.buf["conv_w"], (1, 0, 2)).reshape(CONV_WIDTH, -1)
    row = lambda a: a.reshape(1, -1)
    norms = tuple(row(w[n]) for n in ("pre_mix_norm", "post_mix_norm", "pre_ffn_norm", "post_ffn_norm"))

    loss_cols, grad_x, small = _local_step(
        xs, target, norms, ex, conv_full, row(w["conv_b"]), w["w_rgate"], row(w["b_rgate"]),
        w["w_igate"], row(w["b_igate"]), row(w["lru_lambda"]), row(w["attn_out_norm"]), row(w["lru_out_norm"]))

    loss = lax.psum(0.5 * jnp.sum(loss_cols) / d, ("x", "y", "c"))

    ex.alone("grads_w_in_share")
    reduced = {n: ex.full[n] for n in _BIG}
    early = _unpack(ex.small_sum("early"), [small[n].shape for n in _SMALL_EARLY])
    late = _unpack(ex.small_sum("late"), [small["pre_mix_norm"].shape])
    for n, g in zip(_SMALL_EARLY + ("pre_mix_norm",), early + late):
        reduced[n] = g.reshape(w[n].shape) if n != "conv_w" else lax.dynamic_slice_in_dim(g, chip * w[n].shape[1], w[n].shape[1], axis=1)

    delta, new_m, new_v = {}, {}, {}
    for n in _BIG:
        delta[n], new_m[n], new_v[n] = _adamw(w[n], reduced[n], m[n], v[n], "adamw_" + n)
    shapes = [w[n].shape for n in _SMALL]
    packed = _adamw(*[_pack([src[n] for n in _SMALL]) for src in (w, reduced, m, v)], "adamw_small")
    for out, p in zip((delta, new_m, new_v), packed):
        out.update(zip(_SMALL, _unpack(p, shapes)))

    lead = lambda a: a[None]
    return (loss, lead(grad_x), *[lead(reduced[n]) for n in _WEIGHTS], *[lead(delta[n]) for n in _WEIGHTS],
            *[lead(new_m[n]) for n in _WEIGHTS], *[lead(new_v[n]) for n in _WEIGHTS])
```

```python
import functools
import math

import jax
import jax.numpy as jnp
from jax import lax
from jax.experimental import pallas as pl
from jax.experimental.pallas import tpu as pltpu

F32 = jnp.float32
BF16 = jnp.bfloat16
_MXU = BF16
S = jax.ShapeDtypeStruct

RMS_EPS = 1e-6
HEAD_DIM = 128
CONV_WIDTH = 4
LRU_C = 8.0
ADAM_LR, ADAM_B1, ADAM_B2, ADAM_EPS, ADAM_WD, ADAM_STEP = 0.001, 0.9, 0.999, 1e-08, 0.01, 10
EXP_CUT = -105.0
VMEM_LIMIT = 60 * 1024 * 1024
ROW_TILE = 256
SEQ_TILE = 256
ATTN_BLOCK = 256
ATTN_HEADS = 2
MESH = pl.DeviceIdType.MESH


def _cp(*sem):
    return pltpu.CompilerParams(dimension_semantics=sem, vmem_limit_bytes=VMEM_LIMIT)


def _dot(a, b):
    return jnp.dot(a, b, preferred_element_type=F32)


def _dot_nt(a, b):
    return lax.dot_general(a, b, (((1,), (1,)), ((), ())), preferred_element_type=F32)


def _dot_tn(a, b):
    return lax.dot_general(a, b, (((0,), (0,)), ((), ())), preferred_element_type=F32)


def _rstd(v):
    return lax.rsqrt(jnp.mean(v * v, axis=-1, keepdims=True) + RMS_EPS)


def _rms_bwd(dn, vh, r, gain):
    dvh = dn * gain
    dv = r * (dvh - vh * jnp.mean(dvh * vh, axis=-1, keepdims=True))
    return dv, jnp.sum(dn * vh, axis=0, keepdims=True)


def _log_sigmoid(z):
    return jnp.minimum(z, 0.0) - jnp.log(1.0 + jnp.exp(-jnp.abs(z)))


def _expm1(v):
    small = v * (1.0 + v * (0.5 + v * (1.0 / 6.0 + v * (1.0 / 24.0 + v * (1.0 / 120.0)))))
    return jnp.where(jnp.abs(v) < 0.04, small, jnp.exp(v) - 1.0)


_GELU_C = math.sqrt(2.0 / math.pi)


def _gelu(v):
    return 0.5 * v * (1.0 + jnp.tanh(_GELU_C * (v + 0.044715 * v * v * v)))


def _gelu_grad(v):
    th = jnp.tanh(_GELU_C * (v + 0.044715 * v * v * v))
    return 0.5 * (1.0 + th) + 0.5 * v * (1.0 - th * th) * _GELU_C * (1.0 + 3.0 * 0.044715 * v * v)


def _row_spec(tm, d):
    return pl.BlockSpec((tm, d), lambda i: (i, 0))


def _vec_spec(d):
    return pl.BlockSpec((1, d), lambda i: (0, 0))


_ANY = pl.BlockSpec(memory_space=pl.ANY)


def _place():
    x, y, c = lax.axis_index("x"), lax.axis_index("y"), lax.axis_index("c")
    return x, y, c, [(1 - x, y), (x, 1 - y), (1 - x, 1 - y)]


def _remote(src, dst, send_sem, recv_sem, to):
    return pltpu.make_async_remote_copy(src_ref=src, dst_ref=dst, send_sem=send_sem, recv_sem=recv_sem,
                                        device_id=to, device_id_type=MESH)


class _Carrier:
    def __init__(self):
        self.inputs, self.out_shapes, self.aliases, self.ops, self.n_sems, self.results = [], [], {}, [], 0, None

    def inplace(self, arr):
        self.aliases[len(self.inputs)] = len(self.out_shapes)
        self.inputs.append(arr)
        self.out_shapes.append(S(arr.shape, arr.dtype))
        return len(self.out_shapes) - 1

    def read(self, arr):
        self.inputs.append(arr)
        return len(self.inputs) - 1

    def fresh(self, shape, dtype):
        self.out_shapes.append(S(shape, dtype))
        return len(self.out_shapes) - 1

    def _add(self, n_sems, copies):
        base = self.n_sems
        self.n_sems += n_sems

        def start(ins, outs, send, recv):
            for k, (src, dst, _, to) in enumerate(copies(ins, outs)):
                _remote(src, dst, send.at[base + k], recv.at[base + k], to).start()

        def finish(ins, outs, send, recv):
            for k, (src, _, land, to) in enumerate(copies(ins, outs)):
                _remote(src, land, send.at[base + k], recv.at[base + k], to).wait()

        self.ops.append((start, finish))

    def gather_ici(self, o, rows=None, split=True):
        half = self.out_shapes[o].shape[1] // 2
        lo, size = rows or (0, half)

        def copies(ins, outs):
            x, y, c, chips = _place()
            part = (lambda ref: ref.at[pl.ds(c * half + lo, size)]) if split else (lambda ref: ref)
            mine = part(outs[o].at[2 * x + y])
            return [(mine, mine, part(outs[o].at[2 * px + py]), (px, py, c)) for px, py in chips]

        self._add(3, copies)

    def gather_d2d(self, o, rows=None):
        half = self.out_shapes[o].shape[1] // 2
        lo, size = rows or (0, half)

        def copies(ins, outs):
            x, y, c, chips = _place()
            at = lambda k, cc: outs[o].at[k].at[pl.ds(cc * half + lo, size)]
            return [(at(2 * px + py, c), at(2 * px + py, c), at(2 * px + py, 1 - c), (x, y, 1 - c)) for px, py in chips]

        self._add(3, copies)

    def swap(self, i, o):
        half = self.inputs[i].shape[1] // 2

        def copies(ins, outs):
            x, y, c, _ = _place()
            return [(ins[i].at[:, pl.ds((1 - c) * half, half)], outs[o], outs[o], (x, y, 1 - c))]

        self._add(1, copies)

    def scatter(self, i, o, rows=None):
        lo, size = rows or (0, self.inputs[i].shape[1])

        def copies(ins, outs):
            x, y, c, chips = _place()
            cut = lambda ref: ref.at[pl.ds(lo, size)]
            return [(cut(ins[i].at[2 * px + py]), cut(outs[o].at[2 * x + y]), cut(outs[o].at[2 * px + py]), (px, py, c)) for px, py in chips]

        self._add(3, copies)

    def share(self, o):
        r = self.out_shapes[o].shape[0] // 2

        def copies(ins, outs):
            x, y, c, _ = _place()
            mine = outs[o].at[pl.ds(c * r, r)]
            return [(mine, mine, outs[o].at[pl.ds((1 - c) * r, r)], (x, y, 1 - c))]

        self._add(1, copies)

    def spread(self, i, o):
        def copies(ins, outs):
            x, y, c, _ = _place()
            me = 4 * x + 2 * y + c
            out = []
            for d in range(1, 8):
                to, frm = (me + d) % 8, (me + 8 - d) % 8
                out.append((ins[i], outs[o].at[me], outs[o].at[frm], (to // 4, (to // 2) % 2, to % 2)))
            return out

        self._add(7, copies)

    def _pallas(self, body, n_in, n_out, scratch, **kw):
        k_in, k_out = len(self.inputs), len(self.out_shapes)
        grid = kw.get("grid", ())

        def wrapped(*refs):
            ins, cins = refs[:n_in], refs[n_in:n_in + k_in]
            outs = refs[n_in + k_in:n_in + k_in + n_out]
            couts = refs[n_in + k_in + n_out:n_in + k_in + n_out + k_out]
            own = refs[n_in + k_in + n_out + k_out:]
            send, recv = own[len(scratch):]
            ids = [pl.program_id(a) for a in range(len(grid))]
            first = functools.reduce(jnp.logical_and, [a == 0 for a in ids], True)
            last = functools.reduce(jnp.logical_and, [a == g - 1 for a, g in zip(ids, grid)], True)

            def go(stage):
                for op in self.ops:
                    op[stage](cins, couts, send, recv)

            if grid:
                pl.when(first)(lambda: go(0))
                body(*ins, *outs, *own[:len(scratch)])
                pl.when(last)(lambda: go(1))
            else:
                go(0)
                go(1)

        sem = pltpu.SemaphoreType.DMA((self.n_sems,))
        return pl.pallas_call(
            wrapped, in_specs=list(kw.get("in_specs", [])) + [_ANY] * k_in, out_specs=list(kw.get("out_specs", [])) + [_ANY] * k_out,
            out_shape=list(kw.get("out_shape", [])) + self.out_shapes, scratch_shapes=list(scratch) + [sem, sem],
            input_output_aliases={n_in + i: n_out + o for i, o in self.aliases.items()}, name=kw["name"],
            **({"grid": grid, "compiler_params": _cp(*["arbitrary"] * len(grid))} if grid else {}))

    def run(self, body, kw, *args):
        single = not isinstance(kw["out_shape"], (list, tuple))
        out_shape = [kw["out_shape"]] if single else list(kw["out_shape"])
        out_specs = [kw["out_specs"]] if single else list(kw["out_specs"])
        res = self._pallas(body, len(args), len(out_shape), kw.get("scratch_shapes", []), grid=kw["grid"], in_specs=kw["in_specs"],
                           out_specs=out_specs, out_shape=out_shape, name=kw["name"])(*args, *self.inputs)
        self.results = list(res[len(out_shape):])
        return res[0] if single else list(res[:len(out_shape)])

    def run_alone(self, name):
        self.results = list(self._pallas(None, 0, 0, [], name=name)(*self.inputs))


def _call(comm, body, **kw):
    if comm is None:
        return pl.pallas_call(body, **kw)
    return functools.partial(comm.run, body, kw)


def _in_proj_streamed(hn, car, o_w, place, *, bm, name):
    m, k = hn.shape
    n = car.out_shapes[o_w].shape[2]
    ni, half = m // bm, k // 2
    k_in, k_out = len(car.inputs), len(car.out_shapes)
    order = lambda p: ((p & 1) << 1) | (p >> 1)

    def body(place_ref, a_ref, *refs):
        cins, (o_ref, ob_ref), couts = refs[:k_in], refs[k_in:k_in + 2], refs[k_in + 2:k_in + 2 + k_out]
        wbuf, local, ici_send, ici_recv, d2d_send, d2d_recv, send, recv = refs[k_in + 2 + k_out:]
        p, i = pl.program_id(0), pl.program_id(1)
        x, y, c, chips = _place()
        me = 2 * x + y
        rows = lambda chunk, cc: couts[o_w].at[chunk].at[pl.ds(cc * half, half)]

        @pl.when(jnp.logical_and(p == 0, i == 0))
        def _():
            for j, (px, py) in enumerate(chips):
                _remote(rows(me, c), rows(me, c), ici_send.at[j], ici_recv.at[j], (px, py, c)).start()
            for op in car.ops:
                op[0](cins, couts, send, recv)

        for j, (px, py) in enumerate(chips):
            @pl.when(jnp.logical_and(p == j + 1, i == 0))
            def _(j=j, px=px, py=py):
                landed, other = rows(2 * px + py, c), rows(2 * px + py, 1 - c)
                _remote(landed, landed, ici_send.at[j], ici_recv.at[j], (px, py, c)).wait_recv()
                _remote(landed, landed, d2d_send.at[j], d2d_recv.at[j], (x, y, 1 - c)).start()
                _remote(other, other, d2d_send.at[j], d2d_recv.at[j], (x, y, 1 - c)).wait_recv()

        @pl.when(i == 0)
        def _():
            cp = pltpu.make_async_copy(couts[o_w].at[me ^ order(p)], wbuf, local.at[0])
            cp.start()
            cp.wait()

        res = _dot(a_ref[...], wbuf[...])
        o_ref[...] = res
        ob_ref[...] = res.astype(ob_ref.dtype)

        @pl.when(jnp.logical_and(p == 3, i == ni - 1))
        def _():
            for j, (px, py) in enumerate(chips):
                _remote(rows(me, c), rows(me, c), ici_send.at[j], ici_recv.at[j], (px, py, c)).wait_send()
                _remote(rows(me, c), rows(me, c), d2d_send.at[j], d2d_recv.at[j], (x, y, 1 - c)).wait_send()
            for op in car.ops:
                op[1](cins, couts, send, recv)

    ospec = pl.BlockSpec((bm, n), lambda p, i, place_ref: (i, place_ref[0] ^ order(p)))
    three, sems = pltpu.SemaphoreType.DMA((3,)), pltpu.SemaphoreType.DMA((max(car.n_sems, 1),))
    res = pl.pallas_call(
        body,
        grid_spec=pltpu.PrefetchScalarGridSpec(
            num_scalar_prefetch=1, grid=(4, ni),
            in_specs=[pl.BlockSpec((bm, k), lambda p, i, place_ref: (i, 0))] + [_ANY] * k_in,
            out_specs=[ospec, ospec] + [_ANY] * k_out,
            scratch_shapes=[pltpu.VMEM((k, n), _MXU), pltpu.SemaphoreType.DMA((1,)), three, three, three, three, sems, sems]),
        out_shape=[S((m, 4 * n), F32), S((m, 4 * n), _MXU)] + car.out_shapes,
        input_output_aliases={2 + a: 2 + o for a, o in car.aliases.items()},
        compiler_params=_cp("arbitrary", "arbitrary"), name=name)(place, hn, *car.inputs)
    car.results = list(res[2:])
    return res[0], res[1]


def _mm_nn(a, b3, *, bm, bn, name, also=None, comm=None):
    m, k = a.shape
    c, _, n = b3.shape
    ni, nj = m // bm, n // bn

    def body(a_ref, b_ref, *o_refs):
        res = _dot(a_ref[...], b_ref[...])
        for o_ref in o_refs:
            o_ref[...] = res.astype(o_ref.dtype)

    ospec = pl.BlockSpec((bm, bn), lambda cc, j, i: (i, cc * nj + j))
    dtypes = [F32] + ([] if also is None else [also])
    out = _call(
        comm, body, grid=(c, nj, ni),
        in_specs=[pl.BlockSpec((bm, k), lambda cc, j, i: (i, 0)), pl.BlockSpec((None, k, bn), lambda cc, j, i: (cc, 0, j))],
        out_specs=[ospec] * len(dtypes), out_shape=[S((m, c * n), dt) for dt in dtypes],
        compiler_params=_cp("parallel", "parallel", "parallel"), name=name)(a, b3)
    return out[0] if also is None else out


def _mm_nt(a, b3, *, bm, bo, out_dtype, name, comm=None):
    m = a.shape[0]
    c, ko, n = b3.shape
    ni, nj = m // bm, ko // bo

    def body(a_ref, b_ref, o_ref):
        acc = _dot_nt(a_ref[:, 0:n], b_ref[0])
        for cc in range(1, c):
            acc = acc + _dot_nt(a_ref[:, cc * n:(cc + 1) * n], b_ref[cc])
        o_ref[...] = acc.astype(o_ref.dtype)

    return _call(
        comm, body, grid=(nj, ni),
        in_specs=[pl.BlockSpec((bm, c * n), lambda j, i: (i, 0)),
                  pl.BlockSpec((c, bo, n), lambda j, i: (0, j, 0))],
        out_specs=pl.BlockSpec((bm, bo), lambda j, i: (i, j)),
        out_shape=S((m, ko), out_dtype),
        compiler_params=_cp("parallel", "parallel"), name=name)(a, b3)


def _mm_tn(a, b, c, *, bm, bk, name, comm=None):
    m, k = a.shape
    n = b.shape[1] // c
    nm, nk = m // bm, k // bk

    def body(a_ref, b_ref, o_ref, acc):
        mm = pl.program_id(2)

        @pl.when(mm == 0)
        def _():
            acc[...] = jnp.zeros_like(acc)

        acc[...] += _dot_tn(a_ref[...], b_ref[...])

        @pl.when(mm == nm - 1)
        def _():
            o_ref[...] = acc[...]

    return _call(
        comm, body, grid=(c, nk, nm),
        in_specs=[pl.BlockSpec((bm, bk), lambda cc, j, mm: (mm, j)),
                  pl.BlockSpec((bm, n), lambda cc, j, mm: (mm, cc))],
        out_specs=pl.BlockSpec((None, bk, n), lambda cc, j, mm: (cc, j, 0)),
        out_shape=S((c, k, n), F32),
        scratch_shapes=[pltpu.VMEM((bk, n), F32)],
        compiler_params=_cp("parallel", "parallel", "arbitrary"), name=name)(a, b)


def _swiglu_fwd(hn, wg3, wu3, *, bm, name, comm=None):
    m, k = hn.shape
    c, _, n = wg3.shape

    def body(a_ref, g_ref, u_ref, dgate_ref, dup_ref, act_ref):
        a = a_ref[...]
        gate = _dot(a, g_ref[...])
        up = _dot(a, u_ref[...])
        sg = jax.nn.sigmoid(gate)
        silu = gate * sg
        dgate_ref[...] = (up * (sg * (1.0 + gate * (1.0 - sg)))).astype(dgate_ref.dtype)
        dup_ref[...] = silu.astype(dup_ref.dtype)
        act_ref[...] = (silu * up).astype(act_ref.dtype)

    wspec = pl.BlockSpec((None, k, n), lambda cc, i: (cc, 0, 0))
    ospec = pl.BlockSpec((bm, n), lambda cc, i: (i, cc))
    return _call(
        comm, body, grid=(c, m // bm),
        in_specs=[pl.BlockSpec((bm, k), lambda cc, i: (i, 0)), wspec, wspec],
        out_specs=[ospec, ospec, ospec],
        out_shape=[S((m, c * n), _MXU), S((m, c * n), _MXU), S((m, c * n), _MXU)],
        compiler_params=_cp("parallel", "parallel"), name=name)(hn, wg3, wu3)


def _swiglu_bwd(df, wd, act_dgate, act_dup, *, bm, bo, name):
    m, k = df.shape
    ko = wd.shape[0]

    def body(a_ref, b_ref, g_ref, u_ref, dg_ref, du_ref):
        dact = _dot_nt(a_ref[...], b_ref[...])
        dg_ref[...] = (dact * g_ref[...].astype(F32)).astype(dg_ref.dtype)
        du_ref[...] = (dact * u_ref[...].astype(F32)).astype(du_ref.dtype)

    ospec = pl.BlockSpec((bm, bo), lambda j, i: (i, j))
    return pl.pallas_call(
        body, grid=(ko // bo, m // bm),
        in_specs=[pl.BlockSpec((bm, k), lambda j, i: (i, 0)), pl.BlockSpec((bo, k), lambda j, i: (j, 0)), ospec, ospec],
        out_specs=[ospec, ospec],
        out_shape=[S((m, ko), _MXU), S((m, ko), _MXU)],
        compiler_params=_cp("parallel", "parallel"), name=name)(df, wd, act_dgate, act_dup)


def _rms_fwd(x, gain, name, comm=None):
    t, d = x.shape
    tm = min(t, ROW_TILE)

    def body(x_ref, g_ref, o_ref):
        xv = x_ref[...]
        o_ref[...] = ((xv * _rstd(xv)) * g_ref[...]).astype(o_ref.dtype)

    return _call(comm, body, grid=(t // tm,), in_specs=[_row_spec(tm, d), _vec_spec(d)], out_specs=_row_spec(tm, d),
                          out_shape=S((t, d), _MXU), compiler_params=_cp("parallel"), name=name)(x, gain)


def _outnorm_fwd(o, yl, ga, gl, name):
    t, w = o.shape
    tm = min(t, ROW_TILE)

    def body(o_ref, l_ref, ga_ref, gl_ref, y_ref):
        ov, lv = o_ref[...], l_ref[...]
        y_ref[:, :w] = ((ov * _rstd(ov)) * ga_ref[...]).astype(y_ref.dtype)
        y_ref[:, w:] = ((lv * _rstd(lv)) * gl_ref[...]).astype(y_ref.dtype)

    return pl.pallas_call(body, grid=(t // tm,), in_specs=[_row_spec(tm, w), _row_spec(tm, w), _vec_spec(w), _vec_spec(w)],
                          out_specs=_row_spec(tm, 2 * w), out_shape=S((t, 2 * w), _MXU),
                          compiler_params=_cp("parallel"), name=name)(o, yl, ga, gl)


def _mid_fwd(x, mix, g_post, g_pre, name, comm=None):
    t, d = x.shape
    tm = min(t, ROW_TILE)

    def body(x_ref, m_ref, gp_ref, gn_ref, x2_ref, hn_ref):
        mv = m_ref[...]
        x2 = x_ref[...] + (mv * _rstd(mv)) * gp_ref[...]
        x2_ref[...] = x2
        hn_ref[...] = ((x2 * _rstd(x2)) * gn_ref[...]).astype(hn_ref.dtype)

    return _call(comm, body, grid=(t // tm,), in_specs=[_row_spec(tm, d), _row_spec(tm, d), _vec_spec(d), _vec_spec(d)],
                          out_specs=[_row_spec(tm, d), _row_spec(tm, d)], out_shape=[S((t, d), F32), S((t, d), _MXU)],
                          compiler_params=_cp("parallel"), name=name)(x, mix, g_post, g_pre)


def _final(f, x2, target, g_post, name):
    t, d = f.shape
    tm = min(t, ROW_TILE)

    def body(f_ref, x2_ref, t_ref, g_ref, loss_ref, dout_ref, df_ref, dg_ref):
        @pl.when(pl.program_id(0) == 0)
        def _():
            loss_ref[...] = jnp.zeros_like(loss_ref)
            dg_ref[...] = jnp.zeros_like(dg_ref)

        fv = f_ref[...]
        r = _rstd(fv)
        fh = fv * r
        err = (x2_ref[...] + fh * g_ref[...]) - t_ref[...]
        loss_ref[...] += jnp.sum(err * err, axis=0, keepdims=True)
        dout = err * (1.0 / d)
        dout_ref[...] = dout
        dfv, dg = _rms_bwd(dout, fh, r, g_ref[...])
        df_ref[...] = dfv.astype(df_ref.dtype)
        dg_ref[...] += dg

    return pl.pallas_call(
        body, grid=(t // tm,),
        in_specs=[_row_spec(tm, d), _row_spec(tm, d), _row_spec(tm, d), _vec_spec(d)],
        out_specs=[_vec_spec(d), _row_spec(tm, d), _row_spec(tm, d), _vec_spec(d)],
        out_shape=[S((1, d), F32), S((t, d), F32), S((t, d), _MXU), S((1, d), F32)],
        compiler_params=_cp("arbitrary"), name=name)(f, x2, target, g_post)


def _mid_bwd(dhn_a, dhn_b, dout, x2, mix, g_pre, g_post, name, comm=None):
    t, d = x2.shape
    tm = min(t, ROW_TILE)

    def body(da_ref, db_ref, do_ref, x2_ref, m_ref, gn_ref, gp_ref, dx2_ref, dm_ref, dgn_ref, dgp_ref):
        @pl.when(pl.program_id(0) == 0)
        def _():
            dgn_ref[...] = jnp.zeros_like(dgn_ref)
            dgp_ref[...] = jnp.zeros_like(dgp_ref)

        x2 = x2_ref[...]
        r = _rstd(x2)
        dxa, dgn = _rms_bwd(da_ref[...] + db_ref[...], x2 * r, r, gn_ref[...])
        dx2 = do_ref[...] + dxa
        dx2_ref[...] = dx2
        dgn_ref[...] += dgn
        mv = m_ref[...]
        rm = _rstd(mv)
        dmv, dgp = _rms_bwd(dx2, mv * rm, rm, gp_ref[...])
        dm_ref[...] = dmv.astype(dm_ref.dtype)
        dgp_ref[...] += dgp

    rs, vs = _row_spec(tm, d), _vec_spec(d)
    return _call(
        comm, body, grid=(t // tm,), in_specs=[rs, rs, rs, rs, rs, vs, vs], out_specs=[rs, rs, vs, vs],
        out_shape=[S((t, d), F32), S((t, d), _MXU), S((1, d), F32), S((1, d), F32)],
        compiler_params=_cp("arbitrary"), name=name)(dhn_a, dhn_b, dout, x2, mix, g_pre, g_post)


def _first_bwd(dhn, dx2, x, gain, name, comm=None):
    t, d = x.shape
    tm = min(t, ROW_TILE)

    def body(dh_ref, dx2_ref, x_ref, g_ref, dx_ref, dg_ref):
        @pl.when(pl.program_id(0) == 0)
        def _():
            dg_ref[...] = jnp.zeros_like(dg_ref)

        xv = x_ref[...]
        r = _rstd(xv)
        dxa, dg = _rms_bwd(dh_ref[...], xv * r, r, g_ref[...])
        dx_ref[...] = dx2_ref[...] + dxa
        dg_ref[...] += dg

    rs, vs = _row_spec(tm, d), _vec_spec(d)
    return _call(comm, body, grid=(t // tm,), in_specs=[rs, rs, rs, vs], out_specs=[rs, vs],
                          out_shape=[S((t, d), F32), S((1, d), F32)], compiler_params=_cp("arbitrary"), name=name)(dhn, dx2, x, gain)


def _outnorm_bwd(dy, o, yl, ga, gl, name, comm=None):
    t, w = o.shape
    tm = min(t, ROW_TILE)

    def body(dy_ref, o_ref, l_ref, ga_ref, gl_ref, do_ref, dl_ref, dga_ref, dgl_ref):
        @pl.when(pl.program_id(0) == 0)
        def _():
            dga_ref[...] = jnp.zeros_like(dga_ref)
            dgl_ref[...] = jnp.zeros_like(dgl_ref)

        ov, lv = o_ref[...], l_ref[...]
        ra, rl = _rstd(ov), _rstd(lv)
        dov, dga = _rms_bwd(dy_ref[:, :w], ov * ra, ra, ga_ref[...])
        dlv, dgl = _rms_bwd(dy_ref[:, w:], lv * rl, rl, gl_ref[...])
        do_ref[...] = dov.astype(do_ref.dtype)
        dl_ref[...] = dlv
        dga_ref[...] += dga
        dgl_ref[...] += dgl

    rs, vs = _row_spec(tm, w), _vec_spec(w)
    return _call(comm, body, grid=(t // tm,), in_specs=[_row_spec(tm, 2 * w), rs, rs, vs, vs], out_specs=[rs, rs, vs, vs],
                          out_shape=[S((t, w), _MXU), S((t, w), F32), S((1, w), F32), S((1, w), F32)],
                          compiler_params=_cp("arbitrary"), name=name)(dy, o, yl, ga, gl)


def _tri_sum(v, tri):
    return _dot(v.astype(_MXU), tri)


def _attn_tile(qb, kb, row, col, shift, scale):
    z = _dot_nt(qb, kb) * scale
    mask = (col + shift) < row
    lb = _log_sigmoid(z)
    lm = jnp.where(mask, lb - z, 0.0)
    return mask, lb, lm


def _attn_fwd(proj, n_heads, name, comm=None):
    t = proj.shape[0]
    bq = min(t, ATTN_BLOCK)
    nq = t // bq
    scale = 1.0 / math.sqrt(HEAD_DIM)

    heads = [slice(a * HEAD_DIM, (a + 1) * HEAD_DIM) for a in range(ATTN_HEADS)]

    def body(q_ref, k_ref, v_ref, o_ref):
        row = lax.broadcasted_iota(jnp.int32, (bq, bq), 0)
        col = lax.broadcasted_iota(jnp.int32, (bq, bq), 1)
        tri = (row > col).astype(_MXU)

        def per_q(qi, _):
            q0 = pl.multiple_of(qi * bq, bq)
            qbs = [q_ref[pl.ds(q0, bq), hd] for hd in heads]

            def cond(st):
                return jnp.logical_and(st[0] >= 0, st[1])

            def step(st):
                kj, _, carries, accs = st
                k0 = pl.multiple_of(kj * bq, bq)
                alive, new_carries, new_accs = None, [], []
                for hd, qb, carry, acc in zip(heads, qbs, carries, accs):
                    mask, lb, lm = _attn_tile(qb, k_ref[pl.ds(k0, bq), hd], row, col, (kj - qi) * bq, scale)
                    w = jnp.where(mask, jnp.exp(lb + _tri_sum(lm, tri) + carry), 0.0)
                    new_accs.append(acc + _dot(w.astype(_MXU), v_ref[pl.ds(k0, bq), hd]))
                    carry = carry + jnp.sum(lm, axis=1, keepdims=True)
                    new_carries.append(carry)
                    live = jnp.max(carry) > EXP_CUT
                    alive = live if alive is None else jnp.logical_or(alive, live)
                return kj - 1, alive, tuple(new_carries), tuple(new_accs)

            st = lax.while_loop(cond, step, (qi, jnp.bool_(True), (jnp.zeros((bq, 1), F32),) * ATTN_HEADS,
                                             (jnp.zeros((bq, HEAD_DIM), F32),) * ATTN_HEADS))
            for hd, acc in zip(heads, st[3]):
                o_ref[pl.ds(q0, bq), hd] = acc
            return 0

        lax.fori_loop(0, nq, per_q, 0)

    groups = n_heads // ATTN_HEADS
    hs = lambda off: pl.BlockSpec((t, ATTN_HEADS * HEAD_DIM), lambda h: (0, off + h))
    return _call(
        comm, body, grid=(groups,), in_specs=[hs(0), hs(groups), hs(2 * groups)], out_specs=hs(0),
        out_shape=S((t, n_heads * HEAD_DIM), F32), compiler_params=_cp("parallel"), name=name)(proj, proj, proj)


def _attn_bwd(proj, do, n_heads, name, comm=None):
    t = proj.shape[0]
    bq = min(t, ATTN_BLOCK)
    nq = t // bq
    scale = 1.0 / math.sqrt(HEAD_DIM)

    heads = [slice(a * HEAD_DIM, (a + 1) * HEAD_DIM) for a in range(ATTN_HEADS)]

    def body(q_ref, k_ref, v_ref, do_ref, dq_ref, dk_ref, dv_ref, dka_ref, dva_ref, g_ref, b_ref):
        dka_ref[...] = jnp.zeros_like(dka_ref)
        dva_ref[...] = jnp.zeros_like(dva_ref)
        row = lax.broadcasted_iota(jnp.int32, (bq, bq), 0)
        col = lax.broadcasted_iota(jnp.int32, (bq, bq), 1)
        tri = (row > col).astype(_MXU)
        tri_lt = (row < col).astype(_MXU)

        def per_q(qi, _):
            q0 = pl.multiple_of(qi * bq, bq)
            qbs = [q_ref[pl.ds(q0, bq), hd] for hd in heads]
            dobs = [do_ref[pl.ds(q0, bq), hd] for hd in heads]

            def cond(st):
                return jnp.logical_and(st[0] >= 0, st[1])

            def step(st):
                kj, _, carries = st
                k0 = pl.multiple_of(kj * bq, bq)
                alive, new_carries = None, []
                for a, (hd, qb, dob, carry) in enumerate(zip(heads, qbs, dobs, carries)):
                    mask, lb, lm = _attn_tile(qb, k_ref[pl.ds(k0, bq), hd], row, col, (kj - qi) * bq, scale)
                    w = jnp.where(mask, jnp.exp(lb + _tri_sum(lm, tri) + carry), 0.0)
                    g_ref[a, pl.ds(k0, bq), :] = w * _dot_nt(dob, v_ref[pl.ds(k0, bq), hd])
                    b_ref[a, pl.ds(k0, bq), :] = jnp.where(mask, jnp.exp(lb), 0.0)
                    dva_ref[pl.ds(k0, bq), hd] += _dot_tn(w.astype(_MXU), dob)
                    carry = carry + jnp.sum(lm, axis=1, keepdims=True)
                    new_carries.append(carry)
                    live = jnp.max(carry) > EXP_CUT
                    alive = live if alive is None else jnp.logical_or(alive, live)
                return kj - 1, alive, tuple(new_carries)

            st = lax.while_loop(cond, step, (qi, jnp.bool_(True), (jnp.zeros((bq, 1), F32),) * ATTN_HEADS))

            def back(kj, st2):
                k0 = pl.multiple_of(kj * bq, bq)
                out = []
                for a, (hd, qb, (before, dq)) in enumerate(zip(heads, qbs, st2)):
                    g = g_ref[a, pl.ds(k0, bq), :]
                    beta = b_ref[a, pl.ds(k0, bq), :]
                    dz = ((g * (1.0 - beta) - (before + _tri_sum(g, tri_lt)) * beta) * scale).astype(_MXU)
                    dka_ref[pl.ds(k0, bq), hd] += _dot_tn(dz, qb)
                    out.append((before + jnp.sum(g, axis=1, keepdims=True), dq + _dot(dz, k_ref[pl.ds(k0, bq), hd])))
                return tuple(out)

            st2 = lax.fori_loop(st[0] + 1, qi + 1, back, ((jnp.zeros((bq, 1), F32), jnp.zeros((bq, HEAD_DIM), F32)),) * ATTN_HEADS)
            for hd, (_, dq) in zip(heads, st2):
                dq_ref[pl.ds(q0, bq), hd] = dq.astype(dq_ref.dtype)
            return 0

        lax.fori_loop(0, nq, per_q, 0)
        dk_ref[...] = dka_ref[...].astype(dk_ref.dtype)
        dv_ref[...] = dva_ref[...].astype(dv_ref.dtype)

    groups = n_heads // ATTN_HEADS
    wide = ATTN_HEADS * HEAD_DIM
    hs = lambda off: pl.BlockSpec((t, wide), lambda h: (0, off + h))
    return _call(
        comm, body, grid=(groups,), in_specs=[hs(0), hs(groups), hs(2 * groups), hs(0)], out_specs=[hs(0), hs(0), hs(0)],
        out_shape=[S((t, n_heads * HEAD_DIM), _MXU)] * 3,
        scratch_shapes=[pltpu.VMEM((t, wide), F32), pltpu.VMEM((t, wide), F32),
                        pltpu.VMEM((ATTN_HEADS, t, bq), F32), pltpu.VMEM((ATTN_HEADS, t, bq), F32)],
        compiler_params=_cp("parallel"), name=name)(proj, proj, proj, do)


def _shift_down(cur, prev8, k):
    if k == 0:
        return cur
    row8 = lax.broadcasted_iota(jnp.int32, prev8.shape, 0)
    rc = pltpu.roll(cur, k, 0)
    top = jnp.where(row8 < k, pltpu.roll(prev8, k, 0), rc[0:8, :])
    return jnp.concatenate([top, rc[8:, :]], axis=0)


def _shift_up(cur, next8, k):
    if k == 0:
        return cur
    n = cur.shape[0]
    row8 = lax.broadcasted_iota(jnp.int32, next8.shape, 0)
    rc = pltpu.roll(cur, n - k, 0)
    bottom = jnp.where(row8 >= 8 - k, pltpu.roll(next8, 8 - k, 0), rc[n - 8:, :])
    return jnp.concatenate([rc[:n - 8, :], bottom], axis=0)


def _lru_gates(xl, prev8, cw, cb, wr, br, wi, bi, ls):
    xs = [_shift_down(xl, prev8, CONV_WIDTH - 1 - k) for k in range(CONV_WIDTH)]
    xc = xs[0] * cw[0:1, :]
    for k in range(1, CONV_WIDTH):
        xc = xc + xs[k] * cw[k:k + 1, :]
    xc = xc + cb
    xcb = xc.astype(_MXU)
    r = jax.nn.sigmoid(_dot(xcb, wr) + br)
    i = jax.nn.sigmoid(_dot(xcb, wi) + bi)
    la = (LRU_C * r) * ls
    a = jnp.exp(la)
    mult = jnp.sqrt(-_expm1(2.0 * la))
    return xs, xc, r, i, a, mult


def _group_scan(a, b, reverse):
    n = a.shape[0]
    row = lax.broadcasted_iota(jnp.int32, a.shape, 0) % 8
    for d in (1, 2, 4):
        if reverse:
            m = row < 8 - d
            a_s, b_s = pltpu.roll(a, n - d, 0), pltpu.roll(b, n - d, 0)
        else:
            m = row >= d
            a_s, b_s = pltpu.roll(a, d, 0), pltpu.roll(b, d, 0)
        b = jnp.where(m, a * b_s + b, b)
        a = jnp.where(m, a * a_s, a)
    return a, b


def _lru_fwd(proj, col0, n_blocks, cw, cb, wr, br, wi, bi, lam, name, comm=None):
    t = proj.shape[0]
    tt = min(t, SEQ_TILE)
    nt = t // tt

    def body(xl_ref, gl_ref, cw_ref, cb_ref, wr_ref, br_ref, wi_ref, bi_ref, lam_ref, h_ref, y_ref):
        cwv, cbv, brv, biv = cw_ref[...], cb_ref[...], br_ref[...], bi_ref[...]
        wrv, wiv = wr_ref[...].astype(_MXU), wi_ref[...].astype(_MXU)
        ls = _log_sigmoid(lam_ref[...])

        def tile(ti, hin):
            t0 = pl.multiple_of(ti * tt, tt)
            p0 = pl.multiple_of(jnp.maximum(t0 - 8, 0), 8)
            prev8 = xl_ref[pl.ds(p0, 8), :] * (ti > 0).astype(F32)
            xl = xl_ref[pl.ds(t0, tt), :]
            _, xc, _, ig, a, mult = _lru_gates(xl, prev8, cwv, cbv, wrv, brv, wiv, biv, ls)
            ga, gb = _group_scan(a, mult * (ig * xc), False)
            for g in range(tt // 8):
                hg = ga[8 * g:8 * g + 8, :] * hin + gb[8 * g:8 * g + 8, :]
                h_ref[pl.ds(t0 + 8 * g, 8), :] = hg
                hin = hg[7:8, :]
            y_ref[pl.ds(t0, tt), :] = h_ref[pl.ds(t0, tt), :] * _gelu(gl_ref[pl.ds(t0, tt), :])
            return hin

        lax.fori_loop(0, nt, tile, jnp.zeros((1, HEAD_DIM), F32))

    cs = lambda off: pl.BlockSpec((t, HEAD_DIM), lambda n: (0, off + n))
    vs = pl.BlockSpec((1, HEAD_DIM), lambda n: (0, n))
    ws = pl.BlockSpec((None, HEAD_DIM, HEAD_DIM), lambda n: (n, 0, 0))
    w = n_blocks * HEAD_DIM
    return _call(
        comm, body, grid=(n_blocks,),
        in_specs=[cs(col0), cs(col0 + n_blocks), pl.BlockSpec((CONV_WIDTH, HEAD_DIM), lambda n: (0, n)), vs, ws, vs, ws, vs, vs],
        out_specs=[cs(0), cs(0)], out_shape=[S((t, w), F32), S((t, w), F32)],
        compiler_params=_cp("parallel"), name=name)(proj, proj, cw, cb, wr, br, wi, bi, lam)


def _lru_bwd(proj, col0, n_blocks, h, dyl, cw, cb, wr, br, wi, bi, lam, name, comm=None):
    t = proj.shape[0]
    tt = min(t, SEQ_TILE)
    nt = t // tt

    def body(xl_ref, gl_ref, h_ref, dy_ref, cw_ref, cb_ref, wr_ref, br_ref, wi_ref, bi_ref, lam_ref,
             dxl_ref, dgl_ref, dcw_ref, dcb_ref, dwr_ref, dbr_ref, dwi_ref, dbi_ref, dlam_ref, g_ref):
        cwv, cbv, brv, biv = cw_ref[...], cb_ref[...], br_ref[...], bi_ref[...]
        wrv, wiv = wr_ref[...].astype(_MXU), wi_ref[...].astype(_MXU)
        lamv = lam_ref[...]
        ls = _log_sigmoid(lamv)
        for ref in (dcw_ref, dcb_ref, dwr_ref, dbr_ref, dwi_ref, dbi_ref, dlam_ref):
            ref[...] = jnp.zeros_like(ref)

        def tile(s, carry):
            e_in, dxc_next8 = carry
            ti = nt - 1 - s
            t0 = pl.multiple_of(ti * tt, tt)
            p0 = pl.multiple_of(jnp.maximum(t0 - 8, 0), 8)
            first = (ti > 0).astype(F32)
            xl = xl_ref[pl.ds(t0, tt), :]
            xs, xc, r, ig, a, mult = _lru_gates(xl, xl_ref[pl.ds(p0, 8), :] * first, cwv, cbv, wrv, brv, wiv, biv, ls)
            hv = h_ref[pl.ds(t0, tt), :]
            h_before = _shift_down(hv, h_ref[pl.ds(p0, 8), :] * first, 1)
            glv = gl_ref[pl.ds(t0, tt), :]
            dyv = dy_ref[pl.ds(t0, tt), :]
            dgl_ref[pl.ds(t0, tt), :] = (dyv * hv * _gelu_grad(glv)).astype(dgl_ref.dtype)
            dh = dyv * _gelu(glv)
            row = lax.broadcasted_iota(jnp.int32, a.shape, 0)
            coef = jnp.where(row == tt - 1, 1.0, pltpu.roll(a, tt - 1, 0))
            ga, gb = _group_scan(coef, dh, True)
            gin = e_in
            for g in reversed(range(tt // 8)):
                gg = ga[8 * g:8 * g + 8, :] * gin + gb[8 * g:8 * g + 8, :]
                g_ref[8 * g:8 * g + 8, :] = gg
                gin = gg[0:1, :]
            gv = g_ref[...]
            e_out = a[0:1, :] * gv[0:1, :]
            ix = ig * xc
            dla = (gv * h_before) * a - (gv * ix) * (a * a / mult)
            dlam_ref[...] += jnp.sum(dla * (LRU_C * r), axis=0, keepdims=True)
            dpr = (dla * (LRU_C * ls)) * (r * (1.0 - r))
            dpi = (gv * mult * xc) * (ig * (1.0 - ig))
            dbr_ref[...] += jnp.sum(dpr, axis=0, keepdims=True)
            dbi_ref[...] += jnp.sum(dpi, axis=0, keepdims=True)
            xcb, dprb, dpib = xc.astype(_MXU), dpr.astype(_MXU), dpi.astype(_MXU)
            dwr_ref[...] += _dot_tn(xcb, dprb)
            dwi_ref[...] += _dot_tn(xcb, dpib)
            dxc = gv * mult * ig + _dot_nt(dprb, wrv) + _dot_nt(dpib, wiv)
            dcb_ref[...] += jnp.sum(dxc, axis=0, keepdims=True)
            dxl = None
            for k in range(CONV_WIDTH):
                dcw_ref[k:k + 1, :] += jnp.sum(dxc * xs[k], axis=0, keepdims=True)
                term = _shift_up(dxc, dxc_next8, CONV_WIDTH - 1 - k) * cwv[k:k + 1, :]
                dxl = term if dxl is None else dxl + term
            dxl_ref[pl.ds(t0, tt), :] = dxl.astype(dxl_ref.dtype)
            return e_out, dxc[0:8, :]

        lax.fori_loop(0, nt, tile, (jnp.zeros((1, HEAD_DIM), F32), jnp.zeros((8, HEAD_DIM), F32)))
        dlam_ref[...] = dlam_ref[...] * (1.0 - jax.nn.sigmoid(lamv))

    cs = lambda off: pl.BlockSpec((t, HEAD_DIM), lambda n: (0, off + n))
    vs = pl.BlockSpec((1, HEAD_DIM), lambda n: (0, n))
    ws = pl.BlockSpec((None, HEAD_DIM, HEAD_DIM), lambda n: (n, 0, 0))
    cws = pl.BlockSpec((CONV_WIDTH, HEAD_DIM), lambda n: (0, n))
    w = n_blocks * HEAD_DIM
    vec = S((1, w), F32)
    mat = S((n_blocks, HEAD_DIM, HEAD_DIM), F32)
    return _call(
        comm, body, grid=(n_blocks,),
        in_specs=[cs(col0), cs(col0 + n_blocks), cs(0), cs(0), cws, vs, ws, vs, ws, vs, vs],
        out_specs=[cs(0), cs(0), cws, vs, ws, vs, ws, vs, vs],
        out_shape=[S((t, w), _MXU), S((t, w), _MXU), S((CONV_WIDTH, w), F32), vec, mat, vec, mat, vec, vec],
        scratch_shapes=[pltpu.VMEM((tt, HEAD_DIM), F32)],
        compiler_params=_cp("parallel"), name=name)(proj, proj, h, dyl, cw, cb, wr, br, wi, bi, lam)


class _NoExchange:
    def __init__(self, weights):
        self.weights, self.grads, self.packs = weights, {}, {}

    def weight(self, name):
        return self.weights[name]

    def in_proj(self, hn, bm):
        return _mm_nn(hn, self.weights["w_in"], bm=bm, bn=self.weights["w_in"].shape[2], name="in_proj", also=_MXU)

    def conv_w(self):
        return self.weights["conv_w"]

    def carrier(self, call):
        return None

    def harvest(self, car):
        pass

    def alone(self, call):
        pass


def _local_step(x, target, norms, ex, cb, wr, br, wi, bi, lam, ga, gl):
    g_pre_mix, g_post_mix, g_pre_ffn, g_post_ffn = norms
    t, d = x.shape
    bm = min(t, 512)
    bt = min(t, 2048)

    def run(fn, name, *args, **kw):
        car = ex.carrier(name)
        out = fn(*args, name=name, comm=car, **kw)
        ex.harvest(car)
        return out

    hn1 = _rms_fwd(x, g_pre_mix, "rms1")
    proj, proj_mx = ex.in_proj(hn1, bm)
    win3, cw = ex.weight("w_in"), ex.conv_w()
    c = win3.shape[0]
    o = run(_attn_fwd, "attn_fwd", proj_mx, (proj.shape[1] - d) // 3 // HEAD_DIM)
    mix = 2 * o.shape[1]
    n_heads = n_blocks = o.shape[1] // HEAD_DIM
    h, yl = run(_lru_fwd, "lru_fwd", proj, 3 * n_heads, n_blocks, cw, cb, wr, br, wi, bi, lam)
    y = _outnorm_fwd(o, yl, ga, gl, "outnorm_fwd")
    wout = ex.weight("w_out")
    mixo = run(_mm_nn, "out_proj", y, wout[None], bm=bm, bn=d)
    x2, hn2 = run(_mid_fwd, "mid_fwd", x, mixo, g_post_mix, g_pre_ffn)
    wg3, wu3 = ex.weight("w_ffn_gate"), ex.weight("w_ffn_up")
    act_dgate, act_dup, act = run(_swiglu_fwd, "ffn_gate_up", hn2, wg3, wu3, bm=bm)
    ex.alone("gather_w_down")
    wd = ex.weight("w_ffn_down")
    ff = wd.shape[0]
    f = _mm_nn(act, wd[None], bm=bm, bn=d // 2, name="ffn_down")
    loss_cols, dout, df, dg_post_ffn = _final(f, x2, target, g_post_ffn, "final")

    dgate, dup = _swiglu_bwd(df, wd, act_dgate, act_dup, bm=bm, bo=ff // 4, name="ffn_down_bwd")
    ex.grads["w_ffn_down"] = _mm_tn(act, df, 1, bm=bt, bk=512, name="ffn_down_dw").reshape(c, ff // c, d)
    ex.grads["w_ffn_gate"] = run(_mm_tn, "ffn_gate_dw", hn2, dgate, c, bm=bt, bk=d // 2)
    ex.grads["w_ffn_up"] = run(_mm_tn, "ffn_up_dw", hn2, dup, c, bm=bt, bk=d // 2)
    dhn2_g = run(_mm_nt, "ffn_gate_dx", dgate, wg3, bm=bm, bo=d // 2, out_dtype=F32)
    dhn2_u = run(_mm_nt, "ffn_up_dx", dup, wu3, bm=bm, bo=d // 2, out_dtype=F32)
    dx2, dmix, dg_pre_ffn, dg_post_mix = run(_mid_bwd, "mid_bwd", dhn2_g, dhn2_u, dout, x2, mixo, g_pre_ffn, g_post_mix)
    dy = run(_mm_nt, "out_proj_dx", dmix, wout[None], bm=bm, bo=mix, out_dtype=F32)
    ex.grads["w_out"] = _mm_tn(y, dmix, 1, bm=bt, bk=mix // 4, name="out_proj_dw").reshape(c, mix // c, d)
    do, dyl, dga, dgl_norm = run(_outnorm_bwd, "outnorm_bwd", dy, o, yl, ga, gl)
    dxl, dglu, dcw, dcb, dwr, dbr, dwi, dbi, dlam = run(_lru_bwd, "lru_bwd", proj, 3 * n_heads, n_blocks, h, dyl, cw, cb, wr, br, wi, bi, lam)
    small = dict(post_mix_norm=dg_post_mix, pre_ffn_norm=dg_pre_ffn, post_ffn_norm=dg_post_ffn, conv_w=dcw, conv_b=dcb,
                 w_rgate=dwr, b_rgate=dbr, w_igate=dwi, b_igate=dbi, lru_lambda=dlam, attn_out_norm=dga, lru_out_norm=dgl_norm)
    ex.packs["early"] = _pack([small[n] for n in _SMALL_EARLY])
    dq, dk, dv = run(_attn_bwd, "attn_bwd", proj_mx, do, n_heads)
    dproj = jnp.concatenate([dq, dk, dv, dxl, dglu], axis=1)
    ex.grads["w_in"] = _mm_tn(hn1, dproj, c, bm=bt, bk=d // 2, name="in_proj_dw")
    ex.alone("grads_w_in_swap")
    dhn1 = run(_mm_nt, "in_proj_dx", dproj, win3, bm=bm, bo=d // 2, out_dtype=F32)
    grad_x, small["pre_mix_norm"] = run(_first_bwd, "first_bwd", dhn1, dx2, x, g_pre_mix)
    ex.packs["late"] = _pack([small["pre_mix_norm"]])
    return loss_cols, grad_x, small


def _into_slot(wsh, slot, dtype, name):
    rows, n = wsh.shape
    rb = _row_block(rows, 256) if rows % 8 == 0 else rows

    def body(s_ref, w_ref, o_ref):
        o_ref[...] = w_ref[...].astype(o_ref.dtype)

    return pl.pallas_call(
        body,
        grid_spec=pltpu.PrefetchScalarGridSpec(
            num_scalar_prefetch=1, grid=(rows // rb,),
            in_specs=[pl.BlockSpec((rb, n), lambda i, s_ref: (i, 0))],
            out_specs=pl.BlockSpec((None, rb, n), lambda i, s_ref: (s_ref[0], i, 0))),
        out_shape=S((4, rows, n), dtype), compiler_params=_cp("parallel"), name=name)(slot, wsh)


class _Exchange:
    SCHEDULE = {
        "in_proj": [("stream", "w_in"), ("ici", "conv_w"), ("ici", "w_out"), ("ici", "w_ffn_up", 0)],
        "attn_fwd": [("d2d", "w_out"), ("d2d", "w_ffn_up", 0), ("ici", "w_ffn_gate")],
        "lru_fwd": [("d2d", "w_ffn_gate"), ("ici", "w_ffn_up", 1), ("ici", "w_ffn_up", 2)],
        "out_proj": [("d2d", "w_ffn_up", 1), ("d2d", "w_ffn_up", 2), ("ici", "w_ffn_up", 3)],
        "mid_fwd": [("d2d", "w_ffn_up", 3)],
        "ffn_gate_up": [("ici", "w_ffn_down")],
        "gather_w_down": [("d2d", "w_ffn_down")],
        "ffn_gate_dw": [("swap", "w_ffn_down")],
        "ffn_up_dw": [("scatter", "w_ffn_down", 0), ("scatter", "w_ffn_down", 1), ("scatter", "w_ffn_down", 2), ("swap", "w_ffn_gate")],
        "ffn_gate_dx": [("scatter", "w_ffn_down", 3), ("scatter", "w_ffn_gate", 0), ("scatter", "w_ffn_gate", 1), ("swap", "w_ffn_up")],
        "ffn_up_dx": [("share", "w_ffn_down"), ("scatter", "w_ffn_gate", 2), ("scatter", "w_ffn_gate", 3), ("scatter", "w_ffn_up", 0)],
        "mid_bwd": [("share", "w_ffn_gate"), ("scatter", "w_ffn_up", 1), ("scatter", "w_ffn_up", 2)],
        "out_proj_dx": [("scatter", "w_ffn_up", 3)],
        "outnorm_bwd": [("share", "w_ffn_up"), ("swap", "w_out")],
        "lru_bwd": [("scatter", "w_out")],
        "attn_bwd": [("share", "w_out"), ("spread", "early")],
        "grads_w_in_swap": [("swap", "w_in")],
        "in_proj_dx": [("scatter", "w_in")],
        "grads_w_in_share": [("share", "w_in"), ("spread", "late")],
    }
    PIECES = 4

    def __init__(self, slots, place):
        self.buf, self.place = dict(slots), place
        self.grads, self.packs, self.swapped, self.part, self.scattered, self.full, self.spreaded = {}, {}, {}, {}, {}, {}, {}

    def weight(self, name):
        b = self.buf[name]
        return b.reshape(-1, b.shape[2]) if name in ("w_out", "w_ffn_down") else b

    def in_proj(self, hn, bm):
        car = self.carrier("in_proj")
        out = _in_proj_streamed(hn, car, car.streamed, self.place, bm=bm, name="in_proj")
        self.harvest(car)
        return out

    def conv_w(self):
        return jnp.transpose(self.buf["conv_w"], (1, 0, 2)).reshape(CONV_WIDTH, -1)

    def carrier(self, call):
        if call not in self.SCHEDULE:
            return None
        car = _Carrier()
        car.todo, slot = [], {}
        for kind, name, *piece in self.SCHEDULE[call]:
            if kind in ("ici", "d2d", "stream"):
                if name not in slot:
                    slot[name] = car.inplace(self.buf[name])
                    car.todo.append((self.buf, name, slot[name]))
            if kind == "stream":
                car.streamed = slot[name]
            elif kind in ("ici", "d2d"):
                size = self.buf[name].shape[1] // 2 // self.PIECES
                rows = (piece[0] * size, size) if piece else None
                if kind == "ici":
                    car.gather_ici(slot[name], rows, split=name != "conv_w")
                else:
                    car.gather_d2d(slot[name], rows)
            elif kind == "swap":
                g = self.grads[name]
                o = car.fresh((4, g.shape[1] // 2, g.shape[2]), F32)
                car.swap(car.read(g), o)
                car.todo.append((self.swapped, name, o))
            elif kind == "scatter":
                if name not in self.part:
                    self.part[name] = _add_own_half(self.grads[name], self.swapped[name], self.place[1:], "grads_add_" + name)
                p = self.part[name]
                key = ("scatter", name)
                if key not in slot:
                    slot[key] = (car.read(p), car.inplace(self.scattered[name]) if name in self.scattered else car.fresh(p.shape, p.dtype))
                    car.todo.append((self.scattered, name, slot[key][1]))
                size = p.shape[1] // self.PIECES
                car.scatter(*slot[key], (piece[0] * size, size) if piece else None)
            elif kind == "share":
                o = car.inplace(_sum_chips(self.part[name], self.scattered[name], self.place, "grads_sum_" + name))
                car.share(o)
                car.todo.append((self.full, name, o))
            else:
                o = car.fresh((8,) + self.packs[name].shape, F32)
                car.spread(car.read(self.packs[name]), o)
                car.todo.append((self.spreaded, name, o))
        return car

    def harvest(self, car):
        for state, name, o in (car.todo if car is not None else []):
            state[name] = car.results[o]

    def alone(self, call):
        car = self.carrier(call)
        car.run_alone(call)
        self.harvest(car)

    def small_sum(self, key):
        return _sum_devices(self.packs[key], self.spreaded[key], 2 * self.place[0:1] + self.place[1:], "grads_small_sum_" + key)


def _row_block(rows, cap):
    return max(b for b in range(8, cap + 1, 8) if rows % b == 0)


def _add_own_half(g, recv, core, name):
    _, rows, n = g.shape
    half = rows // 2
    rb = _row_block(half, 512)
    nb = half // rb

    def body(c_ref, g_ref, r_ref, o_ref):
        o_ref[...] = (g_ref[...] + r_ref[...]).astype(o_ref.dtype)

    return pl.pallas_call(
        body,
        grid_spec=pltpu.PrefetchScalarGridSpec(
            num_scalar_prefetch=1, grid=(4, nb),
            in_specs=[pl.BlockSpec((None, rb, n), lambda k, i, c_ref: (k, c_ref[0] * nb + i, 0)),
                      pl.BlockSpec((None, rb, n), lambda k, i, c_ref: (k, i, 0))],
            out_specs=pl.BlockSpec((None, rb, n), lambda k, i, c_ref: (k, i, 0))),
        out_shape=S((4, half, n), BF16), compiler_params=_cp("parallel", "parallel"), name=name)(core, g, recv)


def _sum_chips(part, recv, place, name):
    _, rows, n = part.shape
    rb = _row_block(rows, 64)
    nb = rows // rb

    def body(p_ref, own_ref, r0, r1, r2, r3, o_ref):
        own = own_ref[...].astype(F32)
        terms = [jnp.where(p_ref[0] == k, own, r[...].astype(F32)) for k, r in enumerate((r0, r1, r2, r3))]
        o_ref[...] = ((terms[0] + terms[1]) + terms[2]) + terms[3]

    def slot(k):
        return pl.BlockSpec((None, rb, n), lambda i, p_ref: (jnp.where(p_ref[0] == k, (k + 1) % 4, k), i, 0))

    return pl.pallas_call(
        body,
        grid_spec=pltpu.PrefetchScalarGridSpec(
            num_scalar_prefetch=1, grid=(nb,),
            in_specs=[pl.BlockSpec((None, rb, n), lambda i, p_ref: (p_ref[0], i, 0))] + [slot(k) for k in range(4)],
            out_specs=pl.BlockSpec((rb, n), lambda i, p_ref: (p_ref[1] * nb + i, 0))),
        out_shape=S((2 * rows, n), F32), compiler_params=_cp("parallel"), name=name)(place, part, recv, recv, recv, recv)


def _sum_devices(own, spread, me, name):
    rows = own.shape[0]

    def body(me_ref, own_ref, *refs):
        acc = None
        for k, r in enumerate(refs[:8]):
            term = jnp.where(me_ref[0] == k, own_ref[...], r[...])
            acc = term if acc is None else acc + term
        refs[8][...] = acc

    def slot(k):
        return pl.BlockSpec((None, rows, 128), lambda i, me_ref: (jnp.where(me_ref[0] == k, (k + 1) % 8, k), 0, 0))

    whole = pl.BlockSpec((rows, 128), lambda i, me_ref: (0, 0))
    return pl.pallas_call(
        body,
        grid_spec=pltpu.PrefetchScalarGridSpec(num_scalar_prefetch=1, grid=(1,), in_specs=[whole] + [slot(k) for k in range(8)],
                                               out_specs=whole),
        out_shape=S((rows, 128), F32), compiler_params=_cp("arbitrary"), name=name)(me, own, *[spread] * 8)


def _adamw(w, g, m, v, name):
    rows, n = w.shape
    rb = rows if rows * n * 4 <= (1 << 21) else _row_block(rows, 128)
    c1 = 1.0 - ADAM_B1 ** ADAM_STEP
    c2 = 1.0 - ADAM_B2 ** ADAM_STEP

    def body(w_ref, g_ref, m_ref, v_ref, d_ref, nm_ref, nv_ref):
        gv = g_ref[...]
        nm = ADAM_B1 * m_ref[...] + (1.0 - ADAM_B1) * gv
        nv = ADAM_B2 * v_ref[...] + (1.0 - ADAM_B2) * (gv * gv)
        nm_ref[...] = nm
        nv_ref[...] = nv
        d_ref[...] = -ADAM_LR * ((nm / c1) / (jnp.sqrt(nv / c2) + ADAM_EPS) + ADAM_WD * w_ref[...])

    bs = pl.BlockSpec((rb, n), lambda i: (i, 0))
    return pl.pallas_call(body, grid=(rows // rb,), in_specs=[bs] * 4, out_specs=[bs] * 3, out_shape=[S((rows, n), F32)] * 3,
                          compiler_params=_cp("parallel"), name=name)(w, g, m, v)


_BIG = ("w_in", "w_out", "w_ffn_gate", "w_ffn_up", "w_ffn_down")
_SMALL = ("pre_mix_norm", "post_mix_norm", "pre_ffn_norm", "post_ffn_norm", "conv_w", "conv_b", "w_rgate", "b_rgate",
          "w_igate", "b_igate", "lru_lambda", "attn_out_norm", "lru_out_norm")
_SMALL_EARLY = _SMALL[1:]
_WEIGHTS = ("pre_mix_norm", "post_mix_norm", "pre_ffn_norm", "post_ffn_norm", "w_in", "conv_w", "conv_b", "w_rgate", "b_rgate",
            "w_igate", "b_igate", "lru_lambda", "attn_out_norm", "lru_out_norm", "w_out", "w_ffn_gate", "w_ffn_up", "w_ffn_down")


def _pack(arrays):
    flat = []
    for a in arrays:
        f = a.reshape(-1)
        flat.append(jnp.pad(f, (0, (-f.shape[0]) % 1024)))
    return jnp.concatenate(flat).reshape(-1, 128)


def _unpack(packed, shapes):
    out, pos = [], 0
    flat = packed.reshape(-1)
    for s in shapes:
        size = math.prod(s)
        out.append(flat[pos:pos + size].reshape(s))
        pos += size + (-size) % 1024
    return out


def kernel(x, pre_mix_norm, post_mix_norm, pre_ffn_norm, post_ffn_norm, w_in, conv_w, conv_b, w_rgate, b_rgate, w_igate, b_igate, lru_lambda, attn_out_norm, lru_out_norm, w_out, w_ffn_gate, w_ffn_up, w_ffn_down, loss_target, m_pre_mix_norm, m_post_mix_norm, m_pre_ffn_norm, m_post_ffn_norm, m_w_in, m_conv_w, m_conv_b, m_w_rgate, m_b_rgate, m_w_igate, m_b_igate, m_lru_lambda, m_attn_out_norm, m_lru_out_norm, m_w_out, m_w_ffn_gate, m_w_ffn_up, m_w_ffn_down, v_pre_mix_norm, v_post_mix_norm, v_pre_ffn_norm, v_post_ffn_norm, v_w_in, v_conv_w, v_conv_b, v_w_rgate, v_b_rgate, v_w_igate, v_b_igate, v_lru_lambda, v_attn_out_norm, v_lru_out_norm, v_w_out, v_w_ffn_gate, v_w_ffn_up, v_w_ffn_down):
    given = dict(locals())
    w = {n: given[n][0] for n in _WEIGHTS}
    m = {n: given["m_" + n][0] for n in _WEIGHTS}
    v = {n: given["v_" + n][0] for n in _WEIGHTS}
    xs, target = x[0], loss_target[0]
    d = xs.shape[1]
    chip = (2 * lax.axis_index("x") + lax.axis_index("y")).astype(jnp.int32)
    place = jnp.stack([chip, lax.axis_index("c").astype(jnp.int32)])

    slots = {n: _into_slot(w[n], place[0:1], _MXU, "slot_" + n) for n in _BIG}
    slots["conv_w"] = _into_slot(w["conv_w"], place[0:1], F32, "slot_conv_w")
    ex = _Exchange(slots, place)
    row = lambda a: a.reshape(1, -1)
    norms = tuple(row(w[n]) for n in ("pre_mix_norm", "post_mix_norm", "pre_ffn_norm", "post_ffn_norm"))

    loss_cols, grad_x, small = _local_step(
        xs, target, norms, ex, row(w["conv_b"]), w["w_rgate"], row(w["b_rgate"]),
        w["w_igate"], row(w["b_igate"]), row(w["lru_lambda"]), row(w["attn_out_norm"]), row(w["lru_out_norm"]))

    loss = lax.psum(0.5 * jnp.sum(loss_cols) / d, ("x", "y", "c"))

    ex.alone("grads_w_in_share")
    reduced = {n: ex.full[n] for n in _BIG}
    early = _unpack(ex.small_sum("early"), [small[n].shape for n in _SMALL_EARLY])
    late = _unpack(ex.small_sum("late"), [small["pre_mix_norm"].shape])
    for n, g in zip(_SMALL_EARLY + ("pre_mix_norm",), early + late):
        reduced[n] = g.reshape(w[n].shape) if n != "conv_w" else lax.dynamic_slice_in_dim(g, chip * w[n].shape[1], w[n].shape[1], axis=1)

    delta, new_m, new_v = {}, {}, {}
    for n in _BIG:
        delta[n], new_m[n], new_v[n] = _adamw(w[n], reduced[n], m[n], v[n], "adamw_" + n)
    shapes = [w[n].shape for n in _SMALL]
    packed = _adamw(*[_pack([src[n] for n in _SMALL]) for src in (w, reduced, m, v)], "adamw_small")
    for out, p in zip((delta, new_m, new_v), packed):
        out.update(zip(_SMALL, _unpack(p, shapes)))

    lead = lambda a: a[None]
    return (loss, lead(grad_x), *[lead(reduced[n]) for n in _WEIGHTS], *[lead(delta[n]) for n in _WEIGHTS],
            *[lead(new_m[n]) for n in _WEIGHTS], *[lead(new_v[n]) for n in _WEIGHTS])
```

```python
import functools
import math

import jax
import jax.numpy as jnp
from jax import lax
from jax.experimental import pallas as pl
from jax.experimental.pallas import tpu as pltpu

F32 = jnp.float32
BF16 = jnp.bfloat16
_MXU = BF16
S = jax.ShapeDtypeStruct

RMS_EPS = 1e-6
HEAD_DIM = 128
CONV_WIDTH = 4
LRU_C = 8.0
ADAM_LR, ADAM_B1, ADAM_B2, ADAM_EPS, ADAM_WD, ADAM_STEP = 0.001, 0.9, 0.999, 1e-08, 0.01, 10
EXP_CUT = -105.0
VMEM_LIMIT = 60 * 1024 * 1024
ROW_TILE = 256
SEQ_TILE = 256
ATTN_BLOCK = 256
ATTN_HEADS = 2
MESH = pl.DeviceIdType.MESH


def _cp(*sem):
    return pltpu.CompilerParams(dimension_semantics=sem, vmem_limit_bytes=VMEM_LIMIT)


def _dot(a, b):
    return jnp.dot(a, b, preferred_element_type=F32)


def _dot_nt(a, b):
    return lax.dot_general(a, b, (((1,), (1,)), ((), ())), preferred_element_type=F32)


def _dot_tn(a, b):
    return lax.dot_general(a, b, (((0,), (0,)), ((), ())), preferred_element_type=F32)


def _rstd(v):
    return lax.rsqrt(jnp.mean(v * v, axis=-1, keepdims=True) + RMS_EPS)


def _rms_bwd(dn, vh, r, gain):
    dvh = dn * gain
    dv = r * (dvh - vh * jnp.mean(dvh * vh, axis=-1, keepdims=True))
    return dv, jnp.sum(dn * vh, axis=0, keepdims=True)


def _log_sigmoid(z):
    return jnp.minimum(z, 0.0) - jnp.log(1.0 + jnp.exp(-jnp.abs(z)))


def _expm1(v):
    small = v * (1.0 + v * (0.5 + v * (1.0 / 6.0 + v * (1.0 / 24.0 + v * (1.0 / 120.0)))))
    return jnp.where(jnp.abs(v) < 0.04, small, jnp.exp(v) - 1.0)


_GELU_C = math.sqrt(2.0 / math.pi)


def _gelu(v):
    return 0.5 * v * (1.0 + jnp.tanh(_GELU_C * (v + 0.044715 * v * v * v)))


def _gelu_grad(v):
    th = jnp.tanh(_GELU_C * (v + 0.044715 * v * v * v))
    return 0.5 * (1.0 + th) + 0.5 * v * (1.0 - th * th) * _GELU_C * (1.0 + 3.0 * 0.044715 * v * v)


def _row_spec(tm, d):
    return pl.BlockSpec((tm, d), lambda i: (i, 0))


def _vec_spec(d):
    return pl.BlockSpec((1, d), lambda i: (0, 0))


_ANY = pl.BlockSpec(memory_space=pl.ANY)


def _place():
    x, y, c = lax.axis_index("x"), lax.axis_index("y"), lax.axis_index("c")
    return x, y, c, [(1 - x, y), (x, 1 - y), (1 - x, 1 - y)]


def _remote(src, dst, send_sem, recv_sem, to):
    return pltpu.make_async_remote_copy(src_ref=src, dst_ref=dst, send_sem=send_sem, recv_sem=recv_sem,
                                        device_id=to, device_id_type=MESH)


class _Carrier:
    def __init__(self):
        self.inputs, self.out_shapes, self.aliases, self.ops, self.n_sems, self.results = [], [], {}, [], 0, None

    def inplace(self, arr):
        self.aliases[len(self.inputs)] = len(self.out_shapes)
        self.inputs.append(arr)
        self.out_shapes.append(S(arr.shape, arr.dtype))
        return len(self.out_shapes) - 1

    def read(self, arr):
        self.inputs.append(arr)
        return len(self.inputs) - 1

    def fresh(self, shape, dtype):
        self.out_shapes.append(S(shape, dtype))
        return len(self.out_shapes) - 1

    def _add(self, n_sems, copies):
        base = self.n_sems
        self.n_sems += n_sems

        def start(ins, outs, send, recv):
            for k, (src, dst, _, to) in enumerate(copies(ins, outs)):
                _remote(src, dst, send.at[base + k], recv.at[base + k], to).start()

        def finish(ins, outs, send, recv):
            for k, (src, _, land, to) in enumerate(copies(ins, outs)):
                _remote(src, land, send.at[base + k], recv.at[base + k], to).wait()

        self.ops.append((start, finish))

    def gather_ici(self, o, rows=None, split=True):
        half = self.out_shapes[o].shape[1] // 2
        lo, size = rows or (0, half)

        def copies(ins, outs):
            x, y, c, chips = _place()
            part = (lambda ref: ref.at[pl.ds(c * half + lo, size)]) if split else (lambda ref: ref)
            mine = part(outs[o].at[2 * x + y])
            return [(mine, mine, part(outs[o].at[2 * px + py]), (px, py, c)) for px, py in chips]

        self._add(3, copies)

    def gather_d2d(self, o, rows=None):
        half = self.out_shapes[o].shape[1] // 2
        lo, size = rows or (0, half)

        def copies(ins, outs):
            x, y, c, chips = _place()
            at = lambda k, cc: outs[o].at[k].at[pl.ds(cc * half + lo, size)]
            return [(at(2 * px + py, c), at(2 * px + py, c), at(2 * px + py, 1 - c), (x, y, 1 - c)) for px, py in chips]

        self._add(3, copies)

    def swap(self, i, o):
        half = self.inputs[i].shape[1] // 2

        def copies(ins, outs):
            x, y, c, _ = _place()
            return [(ins[i].at[:, pl.ds((1 - c) * half, half)], outs[o], outs[o], (x, y, 1 - c))]

        self._add(1, copies)

    def scatter(self, i, o, rows=None):
        lo, size = rows or (0, self.inputs[i].shape[1])

        def copies(ins, outs):
            x, y, c, chips = _place()
            cut = lambda ref: ref.at[pl.ds(lo, size)]
            return [(cut(ins[i].at[2 * px + py]), cut(outs[o].at[2 * x + y]), cut(outs[o].at[2 * px + py]), (px, py, c)) for px, py in chips]

        self._add(3, copies)

    def share(self, o):
        r = self.out_shapes[o].shape[0] // 2

        def copies(ins, outs):
            x, y, c, _ = _place()
            mine = outs[o].at[pl.ds(c * r, r)]
            return [(mine, mine, outs[o].at[pl.ds((1 - c) * r, r)], (x, y, 1 - c))]

        self._add(1, copies)

    def spread(self, i, o):
        def copies(ins, outs):
            x, y, c, _ = _place()
            me = 4 * x + 2 * y + c
            out = []
            for d in range(1, 8):
                to, frm = (me + d) % 8, (me + 8 - d) % 8
                out.append((ins[i], outs[o].at[me], outs[o].at[frm], (to // 4, (to // 2) % 2, to % 2)))
            return out

        self._add(7, copies)

    def _pallas(self, body, n_in, n_out, scratch, **kw):
        k_in, k_out = len(self.inputs), len(self.out_shapes)
        grid = kw.get("grid", ())

        def wrapped(*refs):
            ins, cins = refs[:n_in], refs[n_in:n_in + k_in]
            outs = refs[n_in + k_in:n_in + k_in + n_out]
            couts = refs[n_in + k_in + n_out:n_in + k_in + n_out + k_out]
            own = refs[n_in + k_in + n_out + k_out:]
            send, recv = own[len(scratch):]
            ids = [pl.program_id(a) for a in range(len(grid))]
            first = functools.reduce(jnp.logical_and, [a == 0 for a in ids], True)
            last = functools.reduce(jnp.logical_and, [a == g - 1 for a, g in zip(ids, grid)], True)

            def go(stage):
                for op in self.ops:
                    op[stage](cins, couts, send, recv)

            if grid:
                pl.when(first)(lambda: go(0))
                body(*ins, *outs, *own[:len(scratch)])
                pl.when(last)(lambda: go(1))
            else:
                go(0)
                go(1)

        sem = pltpu.SemaphoreType.DMA((self.n_sems,))
        return pl.pallas_call(
            wrapped, in_specs=list(kw.get("in_specs", [])) + [_ANY] * k_in, out_specs=list(kw.get("out_specs", [])) + [_ANY] * k_out,
            out_shape=list(kw.get("out_shape", [])) + self.out_shapes, scratch_shapes=list(scratch) + [sem, sem],
            input_output_aliases={n_in + i: n_out + o for i, o in self.aliases.items()}, name=kw["name"],
            **({"grid": grid, "compiler_params": _cp(*["arbitrary"] * len(grid))} if grid else {}))

    def run(self, body, kw, *args):
        single = not isinstance(kw["out_shape"], (list, tuple))
        out_shape = [kw["out_shape"]] if single else list(kw["out_shape"])
        out_specs = [kw["out_specs"]] if single else list(kw["out_specs"])
        res = self._pallas(body, len(args), len(out_shape), kw.get("scratch_shapes", []), grid=kw["grid"], in_specs=kw["in_specs"],
                           out_specs=out_specs, out_shape=out_shape, name=kw["name"])(*args, *self.inputs)
        self.results = list(res[len(out_shape):])
        return res[0] if single else list(res[:len(out_shape)])

    def run_alone(self, name):
        self.results = list(self._pallas(None, 0, 0, [], name=name)(*self.inputs))


def _call(comm, body, **kw):
    if comm is None:
        return pl.pallas_call(body, **kw)
    return functools.partial(comm.run, body, kw)


def _in_proj_streamed(x, gain, car, o_w, place, *, bm, name):
    m, k = x.shape
    n = car.out_shapes[o_w].shape[2]
    ni, half = m // bm, k // 2
    k_in, k_out = len(car.inputs), len(car.out_shapes)
    order = lambda p: ((p & 1) << 1) | (p >> 1)

    def body(place_ref, x_ref, g_ref, *refs):
        cins, (hn_ref, o_ref, ob_ref), couts = refs[:k_in], refs[k_in:k_in + 3], refs[k_in + 3:k_in + 3 + k_out]
        wbuf, local, ici_send, ici_recv, d2d_send, d2d_recv, send, recv = refs[k_in + 3 + k_out:]
        p, i = pl.program_id(0), pl.program_id(1)
        x, y, c, chips = _place()
        me = 2 * x + y
        rows = lambda chunk, cc: couts[o_w].at[chunk].at[pl.ds(cc * half, half)]

        @pl.when(jnp.logical_and(p == 0, i == 0))
        def _():
            for j, (px, py) in enumerate(chips):
                _remote(rows(me, c), rows(me, c), ici_send.at[j], ici_recv.at[j], (px, py, c)).start()
            for op in car.ops:
                op[0](cins, couts, send, recv)

        for j, (px, py) in enumerate(chips):
            @pl.when(jnp.logical_and(p == j + 1, i == 0))
            def _(j=j, px=px, py=py):
                landed, other = rows(2 * px + py, c), rows(2 * px + py, 1 - c)
                _remote(landed, landed, ici_send.at[j], ici_recv.at[j], (px, py, c)).wait_recv()
                _remote(landed, landed, d2d_send.at[j], d2d_recv.at[j], (x, y, 1 - c)).start()
                _remote(other, other, d2d_send.at[j], d2d_recv.at[j], (x, y, 1 - c)).wait_recv()

        @pl.when(i == 0)
        def _():
            cp = pltpu.make_async_copy(couts[o_w].at[me ^ order(p)], wbuf, local.at[0])
            cp.start()
            cp.wait()

        xv = x_ref[...]
        hn = ((xv * _rstd(xv)) * g_ref[...]).astype(_MXU)
        hn_ref[...] = hn
        res = _dot(hn, wbuf[...])
        o_ref[...] = res
        ob_ref[...] = res.astype(ob_ref.dtype)

        @pl.when(jnp.logical_and(p == 3, i == ni - 1))
        def _():
            for j, (px, py) in enumerate(chips):
                _remote(rows(me, c), rows(me, c), ici_send.at[j], ici_recv.at[j], (px, py, c)).wait_send()
                _remote(rows(me, c), rows(me, c), d2d_send.at[j], d2d_recv.at[j], (x, y, 1 - c)).wait_send()
            for op in car.ops:
                op[1](cins, couts, send, recv)

    ospec = pl.BlockSpec((bm, n), lambda p, i, place_ref: (i, place_ref[0] ^ order(p)))
    rows = pl.BlockSpec((bm, k), lambda p, i, place_ref: (i, 0))
    three, sems = pltpu.SemaphoreType.DMA((3,)), pltpu.SemaphoreType.DMA((max(car.n_sems, 1),))
    res = pl.pallas_call(
        body,
        grid_spec=pltpu.PrefetchScalarGridSpec(
            num_scalar_prefetch=1, grid=(4, ni),
            in_specs=[rows, pl.BlockSpec((1, k), lambda p, i, place_ref: (0, 0))] + [_ANY] * k_in,
            out_specs=[pl.BlockSpec((bm, k), lambda p, i, place_ref: (p * ni + i, 0)), ospec, ospec] + [_ANY] * k_out,
            scratch_shapes=[pltpu.VMEM((k, n), _MXU), pltpu.SemaphoreType.DMA((1,)), three, three, three, three, sems, sems]),
        out_shape=[S((4 * m, k), _MXU), S((m, 4 * n), F32), S((m, 4 * n), _MXU)] + car.out_shapes,
        input_output_aliases={3 + a: 3 + o for a, o in car.aliases.items()},
        compiler_params=_cp("arbitrary", "arbitrary"), name=name)(place, x, gain, *car.inputs)
    car.results = list(res[3:])
    return res[0], res[1], res[2]


def _mm_nn(a, b3, *, bm, bn, name, also=None, comm=None):
    m, k = a.shape
    c, _, n = b3.shape
    ni, nj = m // bm, n // bn

    def body(a_ref, b_ref, *o_refs):
        res = _dot(a_ref[...], b_ref[...])
        for o_ref in o_refs:
            o_ref[...] = res.astype(o_ref.dtype)

    ospec = pl.BlockSpec((bm, bn), lambda cc, j, i: (i, cc * nj + j))
    dtypes = [F32] + ([] if also is None else [also])
    out = _call(
        comm, body, grid=(c, nj, ni),
        in_specs=[pl.BlockSpec((bm, k), lambda cc, j, i: (i, 0)), pl.BlockSpec((None, k, bn), lambda cc, j, i: (cc, 0, j))],
        out_specs=[ospec] * len(dtypes), out_shape=[S((m, c * n), dt) for dt in dtypes],
        compiler_params=_cp("parallel", "parallel", "parallel"), name=name)(a, b3)
    return out[0] if also is None else out


def _mm_nt(a, b3, *, bm, bo, out_dtype, name, comm=None):
    m = a.shape[0]
    c, ko, n = b3.shape
    ni, nj = m // bm, ko // bo

    def body(a_ref, b_ref, o_ref):
        acc = _dot_nt(a_ref[:, 0:n], b_ref[0])
        for cc in range(1, c):
            acc = acc + _dot_nt(a_ref[:, cc * n:(cc + 1) * n], b_ref[cc])
        o_ref[...] = acc.astype(o_ref.dtype)

    return _call(
        comm, body, grid=(nj, ni),
        in_specs=[pl.BlockSpec((bm, c * n), lambda j, i: (i, 0)),
                  pl.BlockSpec((c, bo, n), lambda j, i: (0, j, 0))],
        out_specs=pl.BlockSpec((bm, bo), lambda j, i: (i, j)),
        out_shape=S((m, ko), out_dtype),
        compiler_params=_cp("parallel", "parallel"), name=name)(a, b3)


def _mm_tn(a, b, c, *, bm, bk, name, comm=None):
    m, k = b.shape[0], a.shape[1]
    n = b.shape[1] // c
    nm, nk = m // bm, k // bk

    def body(a_ref, b_ref, o_ref, acc):
        mm = pl.program_id(2)

        @pl.when(mm == 0)
        def _():
            acc[...] = jnp.zeros_like(acc)

        acc[...] += _dot_tn(a_ref[...], b_ref[...])

        @pl.when(mm == nm - 1)
        def _():
            o_ref[...] = acc[...]

    return _call(
        comm, body, grid=(c, nk, nm),
        in_specs=[pl.BlockSpec((bm, bk), lambda cc, j, mm: (mm, j)),
                  pl.BlockSpec((bm, n), lambda cc, j, mm: (mm, cc))],
        out_specs=pl.BlockSpec((None, bk, n), lambda cc, j, mm: (cc, j, 0)),
        out_shape=S((c, k, n), F32),
        scratch_shapes=[pltpu.VMEM((bk, n), F32)],
        compiler_params=_cp("parallel", "parallel", "arbitrary"), name=name)(a, b)


def _swiglu_fwd(hn, wg3, wu3, *, bm, name, comm=None):
    m, k = hn.shape
    c, _, n = wg3.shape

    def body(a_ref, g_ref, u_ref, dgate_ref, dup_ref, act_ref):
        a = a_ref[...]
        gate = _dot(a, g_ref[...])
        up = _dot(a, u_ref[...])
        sg = jax.nn.sigmoid(gate)
        silu = gate * sg
        dgate_ref[...] = (up * (sg * (1.0 + gate * (1.0 - sg)))).astype(dgate_ref.dtype)
        dup_ref[...] = silu.astype(dup_ref.dtype)
        act_ref[...] = (silu * up).astype(act_ref.dtype)

    wspec = pl.BlockSpec((None, k, n), lambda cc, i: (cc, 0, 0))
    ospec = pl.BlockSpec((bm, n), lambda cc, i: (i, cc))
    return _call(
        comm, body, grid=(c, m // bm),
        in_specs=[pl.BlockSpec((bm, k), lambda cc, i: (i, 0)), wspec, wspec],
        out_specs=[ospec, ospec, ospec],
        out_shape=[S((m, c * n), _MXU), S((m, c * n), _MXU), S((m, c * n), _MXU)],
        compiler_params=_cp("parallel", "parallel"), name=name)(hn, wg3, wu3)


def _swiglu_bwd(df, wd, act_dgate, act_dup, *, bm, bo, name):
    m, k = df.shape
    ko = wd.shape[0]

    def body(a_ref, b_ref, g_ref, u_ref, dg_ref, du_ref):
        dact = _dot_nt(a_ref[...], b_ref[...])
        dg_ref[...] = (dact * g_ref[...].astype(F32)).astype(dg_ref.dtype)
        du_ref[...] = (dact * u_ref[...].astype(F32)).astype(du_ref.dtype)

    ospec = pl.BlockSpec((bm, bo), lambda j, i: (i, j))
    return pl.pallas_call(
        body, grid=(ko // bo, m // bm),
        in_specs=[pl.BlockSpec((bm, k), lambda j, i: (i, 0)), pl.BlockSpec((bo, k), lambda j, i: (j, 0)), ospec, ospec],
        out_specs=[ospec, ospec],
        out_shape=[S((m, ko), _MXU), S((m, ko), _MXU)],
        compiler_params=_cp("parallel", "parallel"), name=name)(df, wd, act_dgate, act_dup)


def _rms_fwd(x, gain, name, comm=None):
    t, d = x.shape
    tm = min(t, ROW_TILE)

    def body(x_ref, g_ref, o_ref):
        xv = x_ref[...]
        o_ref[...] = ((xv * _rstd(xv)) * g_ref[...]).astype(o_ref.dtype)

    return _call(comm, body, grid=(t // tm,), in_specs=[_row_spec(tm, d), _vec_spec(d)], out_specs=_row_spec(tm, d),
                          out_shape=S((t, d), _MXU), compiler_params=_cp("parallel"), name=name)(x, gain)


def _outnorm_fwd(o, yl, ga, gl, name, comm=None):
    t, w = o.shape
    tm = min(t, ROW_TILE)

    def body(o_ref, l_ref, ga_ref, gl_ref, y_ref):
        ov, lv = o_ref[...], l_ref[...]
        y_ref[:, :w] = ((ov * _rstd(ov)) * ga_ref[...]).astype(y_ref.dtype)
        y_ref[:, w:] = ((lv * _rstd(lv)) * gl_ref[...]).astype(y_ref.dtype)

    return _call(comm, body, grid=(t // tm,), in_specs=[_row_spec(tm, w), _row_spec(tm, w), _vec_spec(w), _vec_spec(w)],
                 out_specs=_row_spec(tm, 2 * w), out_shape=S((t, 2 * w), _MXU),
                 compiler_params=_cp("parallel"), name=name)(o, yl, ga, gl)


def _mid_fwd(x, mix, g_post, g_pre, name, comm=None):
    t, d = x.shape
    tm = min(t, ROW_TILE)

    def body(x_ref, m_ref, gp_ref, gn_ref, x2_ref, hn_ref):
        mv = m_ref[...]
        x2 = x_ref[...] + (mv * _rstd(mv)) * gp_ref[...]
        x2_ref[...] = x2
        hn_ref[...] = ((x2 * _rstd(x2)) * gn_ref[...]).astype(hn_ref.dtype)

    return _call(comm, body, grid=(t // tm,), in_specs=[_row_spec(tm, d), _row_spec(tm, d), _vec_spec(d), _vec_spec(d)],
                          out_specs=[_row_spec(tm, d), _row_spec(tm, d)], out_shape=[S((t, d), F32), S((t, d), _MXU)],
                          compiler_params=_cp("parallel"), name=name)(x, mix, g_post, g_pre)


def _final(f, x2, target, g_post, name):
    t, d = f.shape
    tm = min(t, ROW_TILE)

    def body(f_ref, x2_ref, t_ref, g_ref, loss_ref, dout_ref, df_ref, dg_ref):
        @pl.when(pl.program_id(0) == 0)
        def _():
            loss_ref[...] = jnp.zeros_like(loss_ref)
            dg_ref[...] = jnp.zeros_like(dg_ref)

        fv = f_ref[...]
        r = _rstd(fv)
        fh = fv * r
        err = (x2_ref[...] + fh * g_ref[...]) - t_ref[...]
        loss_ref[...] += jnp.sum(err * err, axis=0, keepdims=True)
        dout = err * (1.0 / d)
        dout_ref[...] = dout
        dfv, dg = _rms_bwd(dout, fh, r, g_ref[...])
        df_ref[...] = dfv.astype(df_ref.dtype)
        dg_ref[...] += dg

    return pl.pallas_call(
        body, grid=(t // tm,),
        in_specs=[_row_spec(tm, d), _row_spec(tm, d), _row_spec(tm, d), _vec_spec(d)],
        out_specs=[_vec_spec(d), _row_spec(tm, d), _row_spec(tm, d), _vec_spec(d)],
        out_shape=[S((1, d), F32), S((t, d), F32), S((t, d), _MXU), S((1, d), F32)],
        compiler_params=_cp("arbitrary"), name=name)(f, x2, target, g_post)


def _mid_bwd(dhn_a, dhn_b, dout, x2, mix, g_pre, g_post, name, comm=None):
    t, d = x2.shape
    tm = min(t, ROW_TILE)

    def body(da_ref, db_ref, do_ref, x2_ref, m_ref, gn_ref, gp_ref, dx2_ref, dm_ref, dgn_ref, dgp_ref):
        @pl.when(pl.program_id(0) == 0)
        def _():
            dgn_ref[...] = jnp.zeros_like(dgn_ref)
            dgp_ref[...] = jnp.zeros_like(dgp_ref)

        x2 = x2_ref[...]
        r = _rstd(x2)
        dxa, dgn = _rms_bwd(da_ref[...] + db_ref[...], x2 * r, r, gn_ref[...])
        dx2 = do_ref[...] + dxa
        dx2_ref[...] = dx2
        dgn_ref[...] += dgn
        mv = m_ref[...]
        rm = _rstd(mv)
        dmv, dgp = _rms_bwd(dx2, mv * rm, rm, gp_ref[...])
        dm_ref[...] = dmv.astype(dm_ref.dtype)
        dgp_ref[...] += dgp

    rs, vs = _row_spec(tm, d), _vec_spec(d)
    return _call(
        comm, body, grid=(t // tm,), in_specs=[rs, rs, rs, rs, rs, vs, vs], out_specs=[rs, rs, vs, vs],
        out_shape=[S((t, d), F32), S((t, d), _MXU), S((1, d), F32), S((1, d), F32)],
        compiler_params=_cp("arbitrary"), name=name)(dhn_a, dhn_b, dout, x2, mix, g_pre, g_post)


def _first_bwd(dhn, dx2, x, gain, name, comm=None):
    t, d = x.shape
    tm = min(t, ROW_TILE)

    def body(dh_ref, dx2_ref, x_ref, g_ref, dx_ref, dg_ref):
        @pl.when(pl.program_id(0) == 0)
        def _():
            dg_ref[...] = jnp.zeros_like(dg_ref)

        xv = x_ref[...]
        r = _rstd(xv)
        dxa, dg = _rms_bwd(dh_ref[...], xv * r, r, g_ref[...])
        dx_ref[...] = dx2_ref[...] + dxa
        dg_ref[...] += dg

    rs, vs = _row_spec(tm, d), _vec_spec(d)
    return _call(comm, body, grid=(t // tm,), in_specs=[rs, rs, rs, vs], out_specs=[rs, vs],
                          out_shape=[S((t, d), F32), S((1, d), F32)], compiler_params=_cp("arbitrary"), name=name)(dhn, dx2, x, gain)


def _outnorm_bwd(dy, o, yl, ga, gl, name, comm=None):
    t, w = o.shape
    tm = min(t, ROW_TILE)

    def body(dy_ref, o_ref, l_ref, ga_ref, gl_ref, do_ref, dl_ref, dga_ref, dgl_ref):
        @pl.when(pl.program_id(0) == 0)
        def _():
            dga_ref[...] = jnp.zeros_like(dga_ref)
            dgl_ref[...] = jnp.zeros_like(dgl_ref)

        ov, lv = o_ref[...], l_ref[...]
        ra, rl = _rstd(ov), _rstd(lv)
        dov, dga = _rms_bwd(dy_ref[:, :w], ov * ra, ra, ga_ref[...])
        dlv, dgl = _rms_bwd(dy_ref[:, w:], lv * rl, rl, gl_ref[...])
        do_ref[...] = dov.astype(do_ref.dtype)
        dl_ref[...] = dlv
        dga_ref[...] += dga
        dgl_ref[...] += dgl

    rs, vs = _row_spec(tm, w), _vec_spec(w)
    return _call(comm, body, grid=(t // tm,), in_specs=[_row_spec(tm, 2 * w), rs, rs, vs, vs], out_specs=[rs, rs, vs, vs],
                          out_shape=[S((t, w), _MXU), S((t, w), F32), S((1, w), F32), S((1, w), F32)],
                          compiler_params=_cp("arbitrary"), name=name)(dy, o, yl, ga, gl)


def _tri_sum(v, tri):
    return _dot(v.astype(_MXU), tri)


def _attn_tile(qb, kb, row, col, shift, scale):
    z = _dot_nt(qb, kb) * scale
    mask = (col + shift) < row
    lb = _log_sigmoid(z)
    lm = jnp.where(mask, lb - z, 0.0)
    return mask, lb, lm


def _attn_fwd(proj, n_heads, name, comm=None):
    t = proj.shape[0]
    bq = min(t, ATTN_BLOCK)
    nq = t // bq
    scale = 1.0 / math.sqrt(HEAD_DIM)

    heads = [slice(a * HEAD_DIM, (a + 1) * HEAD_DIM) for a in range(ATTN_HEADS)]

    def body(q_ref, k_ref, v_ref, o_ref):
        row = lax.broadcasted_iota(jnp.int32, (bq, bq), 0)
        col = lax.broadcasted_iota(jnp.int32, (bq, bq), 1)
        tri = (row > col).astype(_MXU)

        def per_q(qi, _):
            q0 = pl.multiple_of(qi * bq, bq)
            qbs = [q_ref[pl.ds(q0, bq), hd] for hd in heads]

            def cond(st):
                return jnp.logical_and(st[0] >= 0, st[1])

            def step(st):
                kj, _, carries, accs = st
                k0 = pl.multiple_of(kj * bq, bq)
                alive, new_carries, new_accs = None, [], []
                for hd, qb, carry, acc in zip(heads, qbs, carries, accs):
                    mask, lb, lm = _attn_tile(qb, k_ref[pl.ds(k0, bq), hd], row, col, (kj - qi) * bq, scale)
                    w = jnp.where(mask, jnp.exp(lb + _tri_sum(lm, tri) + carry), 0.0)
                    new_accs.append(acc + _dot(w.astype(_MXU), v_ref[pl.ds(k0, bq), hd]))
                    carry = carry + jnp.sum(lm, axis=1, keepdims=True)
                    new_carries.append(carry)
                    live = jnp.max(carry) > EXP_CUT
                    alive = live if alive is None else jnp.logical_or(alive, live)
                return kj - 1, alive, tuple(new_carries), tuple(new_accs)

            st = lax.while_loop(cond, step, (qi, jnp.bool_(True), (jnp.zeros((bq, 1), F32),) * ATTN_HEADS,
                                             (jnp.zeros((bq, HEAD_DIM), F32),) * ATTN_HEADS))
            for hd, acc in zip(heads, st[3]):
                o_ref[pl.ds(q0, bq), hd] = acc
            return 0

        lax.fori_loop(0, nq, per_q, 0)

    groups = n_heads // ATTN_HEADS
    hs = lambda off: pl.BlockSpec((t, ATTN_HEADS * HEAD_DIM), lambda h: (0, off + h))
    return _call(
        comm, body, grid=(groups,), in_specs=[hs(0), hs(groups), hs(2 * groups)], out_specs=hs(0),
        out_shape=S((t, n_heads * HEAD_DIM), F32), compiler_params=_cp("parallel"), name=name)(proj, proj, proj)


def _attn_bwd(proj, do, n_heads, name, comm=None):
    t = proj.shape[0]
    bq = min(t, ATTN_BLOCK)
    nq = t // bq
    scale = 1.0 / math.sqrt(HEAD_DIM)

    heads = [slice(a * HEAD_DIM, (a + 1) * HEAD_DIM) for a in range(ATTN_HEADS)]

    def body(q_ref, k_ref, v_ref, do_ref, dq_ref, dk_ref, dv_ref, dka_ref, dva_ref, g_ref, b_ref):
        dka_ref[...] = jnp.zeros_like(dka_ref)
        dva_ref[...] = jnp.zeros_like(dva_ref)
        row = lax.broadcasted_iota(jnp.int32, (bq, bq), 0)
        col = lax.broadcasted_iota(jnp.int32, (bq, bq), 1)
        tri = (row > col).astype(_MXU)
        tri_lt = (row < col).astype(_MXU)

        def per_q(qi, _):
            q0 = pl.multiple_of(qi * bq, bq)
            qbs = [q_ref[pl.ds(q0, bq), hd] for hd in heads]
            dobs = [do_ref[pl.ds(q0, bq), hd] for hd in heads]

            def cond(st):
                return jnp.logical_and(st[0] >= 0, st[1])

            def step(st):
                kj, _, carries = st
                k0 = pl.multiple_of(kj * bq, bq)
                alive, new_carries = None, []
                for a, (hd, qb, dob, carry) in enumerate(zip(heads, qbs, dobs, carries)):
                    mask, lb, lm = _attn_tile(qb, k_ref[pl.ds(k0, bq), hd], row, col, (kj - qi) * bq, scale)
                    w = jnp.where(mask, jnp.exp(lb + _tri_sum(lm, tri) + carry), 0.0)
                    g_ref[a, pl.ds(k0, bq), :] = w * _dot_nt(dob, v_ref[pl.ds(k0, bq), hd])
                    b_ref[a, pl.ds(k0, bq), :] = jnp.where(mask, jnp.exp(lb), 0.0)
                    dva_ref[pl.ds(k0, bq), hd] += _dot_tn(w.astype(_MXU), dob)
                    carry = carry + jnp.sum(lm, axis=1, keepdims=True)
                    new_carries.append(carry)
                    live = jnp.max(carry) > EXP_CUT
                    alive = live if alive is None else jnp.logical_or(alive, live)
                return kj - 1, alive, tuple(new_carries)

            st = lax.while_loop(cond, step, (qi, jnp.bool_(True), (jnp.zeros((bq, 1), F32),) * ATTN_HEADS))

            def back(kj, st2):
                k0 = pl.multiple_of(kj * bq, bq)
                out = []
                for a, (hd, qb, (before, dq)) in enumerate(zip(heads, qbs, st2)):
                    g = g_ref[a, pl.ds(k0, bq), :]
                    beta = b_ref[a, pl.ds(k0, bq), :]
                    dz = ((g * (1.0 - beta) - (before + _tri_sum(g, tri_lt)) * beta) * scale).astype(_MXU)
                    dka_ref[pl.ds(k0, bq), hd] += _dot_tn(dz, qb)
                    out.append((before + jnp.sum(g, axis=1, keepdims=True), dq + _dot(dz, k_ref[pl.ds(k0, bq), hd])))
                return tuple(out)

            st2 = lax.fori_loop(st[0] + 1, qi + 1, back, ((jnp.zeros((bq, 1), F32), jnp.zeros((bq, HEAD_DIM), F32)),) * ATTN_HEADS)
            for hd, (_, dq) in zip(heads, st2):
                dq_ref[pl.ds(q0, bq), hd] = dq.astype(dq_ref.dtype)
            return 0

        lax.fori_loop(0, nq, per_q, 0)
        dk_ref[...] = dka_ref[...].astype(dk_ref.dtype)
        dv_ref[...] = dva_ref[...].astype(dv_ref.dtype)

    groups = n_heads // ATTN_HEADS
    wide = ATTN_HEADS * HEAD_DIM
    hs = lambda off: pl.BlockSpec((t, wide), lambda h: (0, off + h))
    return _call(
        comm, body, grid=(groups,), in_specs=[hs(0), hs(groups), hs(2 * groups), hs(0)], out_specs=[hs(0), hs(0), hs(0)],
        out_shape=[S((t, n_heads * HEAD_DIM), _MXU)] * 3,
        scratch_shapes=[pltpu.VMEM((t, wide), F32), pltpu.VMEM((t, wide), F32),
                        pltpu.VMEM((ATTN_HEADS, t, bq), F32), pltpu.VMEM((ATTN_HEADS, t, bq), F32)],
        compiler_params=_cp("parallel"), name=name)(proj, proj, proj, do)


def _shift_down(cur, prev8, k):
    if k == 0:
        return cur
    row8 = lax.broadcasted_iota(jnp.int32, prev8.shape, 0)
    rc = pltpu.roll(cur, k, 0)
    top = jnp.where(row8 < k, pltpu.roll(prev8, k, 0), rc[0:8, :])
    return jnp.concatenate([top, rc[8:, :]], axis=0)


def _shift_up(cur, next8, k):
    if k == 0:
        return cur
    n = cur.shape[0]
    row8 = lax.broadcasted_iota(jnp.int32, next8.shape, 0)
    rc = pltpu.roll(cur, n - k, 0)
    bottom = jnp.where(row8 >= 8 - k, pltpu.roll(next8, 8 - k, 0), rc[n - 8:, :])
    return jnp.concatenate([rc[:n - 8, :], bottom], axis=0)


def _lru_gates(xl, prev8, cw, cb, wr, br, wi, bi, ls):
    xs = [_shift_down(xl, prev8, CONV_WIDTH - 1 - k) for k in range(CONV_WIDTH)]
    xc = xs[0] * cw[0:1, :]
    for k in range(1, CONV_WIDTH):
        xc = xc + xs[k] * cw[k:k + 1, :]
    xc = xc + cb
    xcb = xc.astype(_MXU)
    r = jax.nn.sigmoid(_dot(xcb, wr) + br)
    i = jax.nn.sigmoid(_dot(xcb, wi) + bi)
    la = (LRU_C * r) * ls
    a = jnp.exp(la)
    mult = jnp.sqrt(-_expm1(2.0 * la))
    return xs, xc, r, i, a, mult


def _group_scan(a, b, reverse):
    n = a.shape[0]
    row = lax.broadcasted_iota(jnp.int32, a.shape, 0) % 8
    for d in (1, 2, 4):
        if reverse:
            m = row < 8 - d
            a_s, b_s = pltpu.roll(a, n - d, 0), pltpu.roll(b, n - d, 0)
        else:
            m = row >= d
            a_s, b_s = pltpu.roll(a, d, 0), pltpu.roll(b, d, 0)
        b = jnp.where(m, a * b_s + b, b)
        a = jnp.where(m, a * a_s, a)
    return a, b


def _lru_fwd(proj, col0, n_blocks, cw, cb, wr, br, wi, bi, lam, name, comm=None):
    t = proj.shape[0]
    tt = min(t, SEQ_TILE)
    nt = t // tt

    def body(xl_ref, gl_ref, cw_ref, cb_ref, wr_ref, br_ref, wi_ref, bi_ref, lam_ref, h_ref, y_ref):
        cwv, cbv, brv, biv = cw_ref[...], cb_ref[...], br_ref[...], bi_ref[...]
        wrv, wiv = wr_ref[...].astype(_MXU), wi_ref[...].astype(_MXU)
        ls = _log_sigmoid(lam_ref[...])

        def tile(ti, hin):
            t0 = pl.multiple_of(ti * tt, tt)
            p0 = pl.multiple_of(jnp.maximum(t0 - 8, 0), 8)
            prev8 = xl_ref[pl.ds(p0, 8), :] * (ti > 0).astype(F32)
            xl = xl_ref[pl.ds(t0, tt), :]
            _, xc, _, ig, a, mult = _lru_gates(xl, prev8, cwv, cbv, wrv, brv, wiv, biv, ls)
            ga, gb = _group_scan(a, mult * (ig * xc), False)
            for g in range(tt // 8):
                hg = ga[8 * g:8 * g + 8, :] * hin + gb[8 * g:8 * g + 8, :]
                h_ref[pl.ds(t0 + 8 * g, 8), :] = hg
                hin = hg[7:8, :]
            y_ref[pl.ds(t0, tt), :] = h_ref[pl.ds(t0, tt), :] * _gelu(gl_ref[pl.ds(t0, tt), :])
            return hin

        lax.fori_loop(0, nt, tile, jnp.zeros((1, HEAD_DIM), F32))

    cs = lambda off: pl.BlockSpec((t, HEAD_DIM), lambda n: (0, off + n))
    vs = pl.BlockSpec((1, HEAD_DIM), lambda n: (0, n))
    ws = pl.BlockSpec((None, HEAD_DIM, HEAD_DIM), lambda n: (n, 0, 0))
    w = n_blocks * HEAD_DIM
    return _call(
        comm, body, grid=(n_blocks,),
        in_specs=[cs(col0), cs(col0 + n_blocks), pl.BlockSpec((CONV_WIDTH, HEAD_DIM), lambda n: (0, n)), vs, ws, vs, ws, vs, vs],
        out_specs=[cs(0), cs(0)], out_shape=[S((t, w), F32), S((t, w), F32)],
        compiler_params=_cp("parallel"), name=name)(proj, proj, cw, cb, wr, br, wi, bi, lam)


def _lru_bwd(proj, col0, n_blocks, h, dyl, cw, cb, wr, br, wi, bi, lam, name, comm=None):
    t = proj.shape[0]
    tt = min(t, SEQ_TILE)
    nt = t // tt

    def body(xl_ref, gl_ref, h_ref, dy_ref, cw_ref, cb_ref, wr_ref, br_ref, wi_ref, bi_ref, lam_ref,
             dxl_ref, dgl_ref, dcw_ref, dcb_ref, dwr_ref, dbr_ref, dwi_ref, dbi_ref, dlam_ref, g_ref):
        cwv, cbv, brv, biv = cw_ref[...], cb_ref[...], br_ref[...], bi_ref[...]
        wrv, wiv = wr_ref[...].astype(_MXU), wi_ref[...].astype(_MXU)
        lamv = lam_ref[...]
        ls = _log_sigmoid(lamv)
        for ref in (dcw_ref, dcb_ref, dwr_ref, dbr_ref, dwi_ref, dbi_ref, dlam_ref):
            ref[...] = jnp.zeros_like(ref)

        def tile(s, carry):
            e_in, dxc_next8 = carry
            ti = nt - 1 - s
            t0 = pl.multiple_of(ti * tt, tt)
            p0 = pl.multiple_of(jnp.maximum(t0 - 8, 0), 8)
            first = (ti > 0).astype(F32)
            xl = xl_ref[pl.ds(t0, tt), :]
            xs, xc, r, ig, a, mult = _lru_gates(xl, xl_ref[pl.ds(p0, 8), :] * first, cwv, cbv, wrv, brv, wiv, biv, ls)
            hv = h_ref[pl.ds(t0, tt), :]
            h_before = _shift_down(hv, h_ref[pl.ds(p0, 8), :] * first, 1)
            glv = gl_ref[pl.ds(t0, tt), :]
            dyv = dy_ref[pl.ds(t0, tt), :]
            dgl_ref[pl.ds(t0, tt), :] = (dyv * hv * _gelu_grad(glv)).astype(dgl_ref.dtype)
            dh = dyv * _gelu(glv)
            row = lax.broadcasted_iota(jnp.int32, a.shape, 0)
            coef = jnp.where(row == tt - 1, 1.0, pltpu.roll(a, tt - 1, 0))
            ga, gb = _group_scan(coef, dh, True)
            gin = e_in
            for g in reversed(range(tt // 8)):
                gg = ga[8 * g:8 * g + 8, :] * gin + gb[8 * g:8 * g + 8, :]
                g_ref[8 * g:8 * g + 8, :] = gg
                gin = gg[0:1, :]
            gv = g_ref[...]
            e_out = a[0:1, :] * gv[0:1, :]
            ix = ig * xc
            dla = (gv * h_before) * a - (gv * ix) * (a * a / mult)
            dlam_ref[...] += jnp.sum(dla * (LRU_C * r), axis=0, keepdims=True)
            dpr = (dla * (LRU_C * ls)) * (r * (1.0 - r))
            dpi = (gv * mult * xc) * (ig * (1.0 - ig))
            dbr_ref[...] += jnp.sum(dpr, axis=0, keepdims=True)
            dbi_ref[...] += jnp.sum(dpi, axis=0, keepdims=True)
            xcb, dprb, dpib = xc.astype(_MXU), dpr.astype(_MXU), dpi.astype(_MXU)
            dwr_ref[...] += _dot_tn(xcb, dprb)
            dwi_ref[...] += _dot_tn(xcb, dpib)
            dxc = gv * mult * ig + _dot_nt(dprb, wrv) + _dot_nt(dpib, wiv)
            dcb_ref[...] += jnp.sum(dxc, axis=0, keepdims=True)
            dxl = None
            for k in range(CONV_WIDTH):
                dcw_ref[k:k + 1, :] += jnp.sum(dxc * xs[k], axis=0, keepdims=True)
                term = _shift_up(dxc, dxc_next8, CONV_WIDTH - 1 - k) * cwv[k:k + 1, :]
                dxl = term if dxl is None else dxl + term
            dxl_ref[pl.ds(t0, tt), :] = dxl.astype(dxl_ref.dtype)
            return e_out, dxc[0:8, :]

        lax.fori_loop(0, nt, tile, (jnp.zeros((1, HEAD_DIM), F32), jnp.zeros((8, HEAD_DIM), F32)))
        dlam_ref[...] = dlam_ref[...] * (1.0 - jax.nn.sigmoid(lamv))

    cs = lambda off: pl.BlockSpec((t, HEAD_DIM), lambda n: (0, off + n))
    vs = pl.BlockSpec((1, HEAD_DIM), lambda n: (0, n))
    ws = pl.BlockSpec((None, HEAD_DIM, HEAD_DIM), lambda n: (n, 0, 0))
    cws = pl.BlockSpec((CONV_WIDTH, HEAD_DIM), lambda n: (0, n))
    w = n_blocks * HEAD_DIM
    vec = S((1, w), F32)
    mat = S((n_blocks, HEAD_DIM, HEAD_DIM), F32)
    return _call(
        comm, body, grid=(n_blocks,),
        in_specs=[cs(col0), cs(col0 + n_blocks), cs(0), cs(0), cws, vs, ws, vs, ws, vs, vs],
        out_specs=[cs(0), cs(0), cws, vs, ws, vs, ws, vs, vs],
        out_shape=[S((t, w), _MXU), S((t, w), _MXU), S((CONV_WIDTH, w), F32), vec, mat, vec, mat, vec, vec],
        scratch_shapes=[pltpu.VMEM((tt, HEAD_DIM), F32)],
        compiler_params=_cp("parallel"), name=name)(proj, proj, h, dyl, cw, cb, wr, br, wi, bi, lam)


class _NoExchange:
    def __init__(self, weights):
        self.weights, self.grads, self.packs = weights, {}, {}

    def weight(self, name):
        return self.weights[name]

    def in_proj(self, x, gain, bm):
        hn = _rms_fwd(x, gain, "rms1")
        return [hn, *_mm_nn(hn, self.weights["w_in"], bm=bm, bn=self.weights["w_in"].shape[2], name="in_proj", also=_MXU)]

    def conv_w(self):
        return self.weights["conv_w"]

    def carrier(self, call):
        return None

    def harvest(self, car):
        pass

    def alone(self, call):
        pass


def _local_step(x, target, norms, ex, cb, wr, br, wi, bi, lam, ga, gl):
    g_pre_mix, g_post_mix, g_pre_ffn, g_post_ffn = norms
    t, d = x.shape
    bm = min(t, 512)
    bt = min(t, 2048)

    def run(fn, name, *args, **kw):
        car = ex.carrier(name)
        out = fn(*args, name=name, comm=car, **kw)
        ex.harvest(car)
        return out

    hn1, proj, proj_mx = ex.in_proj(x, g_pre_mix, bm)
    win3, cw = ex.weight("w_in"), ex.conv_w()
    c = win3.shape[0]
    o = run(_attn_fwd, "attn_fwd", proj_mx, (proj.shape[1] - d) // 3 // HEAD_DIM)
    mix = 2 * o.shape[1]
    n_heads = n_blocks = o.shape[1] // HEAD_DIM
    h, yl = run(_lru_fwd, "lru_fwd", proj, 3 * n_heads, n_blocks, cw, cb, wr, br, wi, bi, lam)
    y = run(_outnorm_fwd, "outnorm_fwd", o, yl, ga, gl)
    wout = ex.weight("w_out")
    mixo = run(_mm_nn, "out_proj", y, wout[None], bm=bm, bn=d)
    x2, hn2 = run(_mid_fwd, "mid_fwd", x, mixo, g_post_mix, g_pre_ffn)
    wg3, wu3 = ex.weight("w_ffn_gate"), ex.weight("w_ffn_up")
    act_dgate, act_dup, act = run(_swiglu_fwd, "ffn_gate_up", hn2, wg3, wu3, bm=bm)
    ex.alone("gather_w_down")
    wd = ex.weight("w_ffn_down")
    ff = wd.shape[0]
    f = _mm_nn(act, wd[None], bm=bm, bn=d // 2, name="ffn_down")
    loss_cols, dout, df, dg_post_ffn = _final(f, x2, target, g_post_ffn, "final")

    dgate, dup = _swiglu_bwd(df, wd, act_dgate, act_dup, bm=bm, bo=ff // 4, name="ffn_down_bwd")
    ex.grads["w_ffn_down"] = _mm_tn(act, df, 1, bm=bt, bk=512, name="ffn_down_dw").reshape(c, ff // c, d)
    ex.grads["w_ffn_gate"] = run(_mm_tn, "ffn_gate_dw", hn2, dgate, c, bm=bt, bk=d // 2)
    ex.grads["w_ffn_up"] = run(_mm_tn, "ffn_up_dw", hn2, dup, c, bm=bt, bk=d // 2)
    dhn2_g = run(_mm_nt, "ffn_gate_dx", dgate, wg3, bm=bm, bo=d // 2, out_dtype=F32)
    dhn2_u = run(_mm_nt, "ffn_up_dx", dup, wu3, bm=bm, bo=d // 2, out_dtype=F32)
    dx2, dmix, dg_pre_ffn, dg_post_mix = run(_mid_bwd, "mid_bwd", dhn2_g, dhn2_u, dout, x2, mixo, g_pre_ffn, g_post_mix)
    dy = run(_mm_nt, "out_proj_dx", dmix, wout[None], bm=bm, bo=mix, out_dtype=F32)
    ex.grads["w_out"] = _mm_tn(y, dmix, 1, bm=bt, bk=mix // 4, name="out_proj_dw").reshape(c, mix // c, d)
    do, dyl, dga, dgl_norm = run(_outnorm_bwd, "outnorm_bwd", dy, o, yl, ga, gl)
    dxl, dglu, dcw, dcb, dwr, dbr, dwi, dbi, dlam = run(_lru_bwd, "lru_bwd", proj, 3 * n_heads, n_blocks, h, dyl, cw, cb, wr, br, wi, bi, lam)
    small = dict(post_mix_norm=dg_post_mix, pre_ffn_norm=dg_pre_ffn, post_ffn_norm=dg_post_ffn, conv_w=dcw, conv_b=dcb,
                 w_rgate=dwr, b_rgate=dbr, w_igate=dwi, b_igate=dbi, lru_lambda=dlam, attn_out_norm=dga, lru_out_norm=dgl_norm)
    ex.packs["early"] = _pack([small[n] for n in _SMALL_EARLY])
    dq, dk, dv = run(_attn_bwd, "attn_bwd", proj_mx, do, n_heads)
    dproj = jnp.concatenate([dq, dk, dv, dxl, dglu], axis=1)
    ex.grads["w_in"] = _mm_tn(hn1, dproj, c, bm=bt, bk=d // 2, name="in_proj_dw")
    ex.alone("grads_w_in_swap")
    dhn1 = run(_mm_nt, "in_proj_dx", dproj, win3, bm=bm, bo=d // 2, out_dtype=F32)
    grad_x, small["pre_mix_norm"] = run(_first_bwd, "first_bwd", dhn1, dx2, x, g_pre_mix)
    ex.packs["late"] = _pack([small["pre_mix_norm"]])
    return loss_cols, grad_x, small


def _into_slot(wsh, slot, dtype, name):
    rows, n = wsh.shape
    rb = _row_block(rows, 256) if rows % 8 == 0 else rows

    def body(s_ref, w_ref, o_ref):
        o_ref[...] = w_ref[...].astype(o_ref.dtype)

    return pl.pallas_call(
        body,
        grid_spec=pltpu.PrefetchScalarGridSpec(
            num_scalar_prefetch=1, grid=(rows // rb,),
            in_specs=[pl.BlockSpec((rb, n), lambda i, s_ref: (i, 0))],
            out_specs=pl.BlockSpec((None, rb, n), lambda i, s_ref: (s_ref[0], i, 0))),
        out_shape=S((4, rows, n), dtype), compiler_params=_cp("parallel"), name=name)(slot, wsh)


class _Exchange:
    SCHEDULE = {
        "in_proj": [("stream", "w_in"), ("ici", "conv_w"), ("ici", "w_ffn_up", 0)],
        "attn_fwd": [("d2d", "w_ffn_up", 0), ("ici", "w_ffn_gate")],
        "lru_fwd": [("d2d", "w_ffn_gate"), ("ici", "w_out"), ("ici", "w_ffn_up", 1)],
        "outnorm_fwd": [("d2d", "w_out"), ("d2d", "w_ffn_up", 1)],
        "out_proj": [("ici", "w_ffn_up", 2), ("ici", "w_ffn_up", 3)],
        "mid_fwd": [("d2d", "w_ffn_up", 2), ("d2d", "w_ffn_up", 3)],
        "ffn_gate_up": [("ici", "w_ffn_down")],
        "gather_w_down": [("d2d", "w_ffn_down")],
        "ffn_gate_dw": [("swap", "w_ffn_down")],
        "ffn_up_dw": [("scatter", "w_ffn_down", 0), ("scatter", "w_ffn_down", 1), ("scatter", "w_ffn_down", 2), ("swap", "w_ffn_gate")],
        "ffn_gate_dx": [("scatter", "w_ffn_down", 3), ("scatter", "w_ffn_gate", 0), ("scatter", "w_ffn_gate", 1), ("swap", "w_ffn_up")],
        "ffn_up_dx": [("share", "w_ffn_down"), ("scatter", "w_ffn_gate", 2), ("scatter", "w_ffn_gate", 3), ("scatter", "w_ffn_up", 0)],
        "mid_bwd": [("share", "w_ffn_gate"), ("scatter", "w_ffn_up", 1), ("scatter", "w_ffn_up", 2)],
        "out_proj_dx": [("scatter", "w_ffn_up", 3)],
        "outnorm_bwd": [("share", "w_ffn_up"), ("swap", "w_out")],
        "lru_bwd": [("scatter", "w_out")],
        "attn_bwd": [("share", "w_out"), ("spread", "early")],
        "grads_w_in_swap": [("swap", "w_in")],
        "in_proj_dx": [("scatter", "w_in")],
        "grads_w_in_share": [("share", "w_in"), ("spread", "late")],
    }
    PIECES = 4

    def __init__(self, slots, place):
        self.buf, self.place = dict(slots), place
        self.grads, self.packs, self.swapped, self.part, self.scattered, self.full, self.spreaded = {}, {}, {}, {}, {}, {}, {}

    def weight(self, name):
        b = self.buf[name]
        return b.reshape(-1, b.shape[2]) if name in ("w_out", "w_ffn_down") else b

    def in_proj(self, x, gain, bm):
        car = self.carrier("in_proj")
        out = _in_proj_streamed(x, gain, car, car.streamed, self.place, bm=bm, name="in_proj")
        self.harvest(car)
        return out

    def conv_w(self):
        return jnp.transpose(self.buf["conv_w"], (1, 0, 2)).reshape(CONV_WIDTH, -1)

    def carrier(self, call):
        if call not in self.SCHEDULE:
            return None
        car = _Carrier()
        car.todo, slot = [], {}
        for kind, name, *piece in self.SCHEDULE[call]:
            if kind in ("ici", "d2d", "stream"):
                if name not in slot:
                    slot[name] = car.inplace(self.buf[name])
                    car.todo.append((self.buf, name, slot[name]))
            if kind == "stream":
                car.streamed = slot[name]
            elif kind in ("ici", "d2d"):
                size = self.buf[name].shape[1] // 2 // self.PIECES
                rows = (piece[0] * size, size) if piece else None
                if kind == "ici":
                    car.gather_ici(slot[name], rows, split=name != "conv_w")
                else:
                    car.gather_d2d(slot[name], rows)
            elif kind == "swap":
                g = self.grads[name]
                o = car.fresh((4, g.shape[1] // 2, g.shape[2]), F32)
                car.swap(car.read(g), o)
                car.todo.append((self.swapped, name, o))
            elif kind == "scatter":
                if name not in self.part:
                    self.part[name] = _add_own_half(self.grads[name], self.swapped[name], self.place[1:], "grads_add_" + name)
                p = self.part[name]
                key = ("scatter", name)
                if key not in slot:
                    slot[key] = (car.read(p), car.inplace(self.scattered[name]) if name in self.scattered else car.fresh(p.shape, p.dtype))
                    car.todo.append((self.scattered, name, slot[key][1]))
                size = p.shape[1] // self.PIECES
                car.scatter(*slot[key], (piece[0] * size, size) if piece else None)
            elif kind == "share":
                o = car.inplace(_sum_chips(self.part[name], self.scattered[name], self.place, "grads_sum_" + name))
                car.share(o)
                car.todo.append((self.full, name, o))
            else:
                o = car.fresh((8,) + self.packs[name].shape, F32)
                car.spread(car.read(self.packs[name]), o)
                car.todo.append((self.spreaded, name, o))
        return car

    def harvest(self, car):
        for state, name, o in (car.todo if car is not None else []):
            state[name] = car.results[o]

    def alone(self, call):
        car = self.carrier(call)
        car.run_alone(call)
        self.harvest(car)

    def small_sum(self, key):
        return _sum_devices(self.packs[key], self.spreaded[key], 2 * self.place[0:1] + self.place[1:], "grads_small_sum_" + key)


def _row_block(rows, cap):
    return max(b for b in range(8, cap + 1, 8) if rows % b == 0)


def _add_own_half(g, recv, core, name):
    _, rows, n = g.shape
    half = rows // 2
    rb = _row_block(half, 512)
    nb = half // rb

    def body(c_ref, g_ref, r_ref, o_ref):
        o_ref[...] = (g_ref[...] + r_ref[...]).astype(o_ref.dtype)

    return pl.pallas_call(
        body,
        grid_spec=pltpu.PrefetchScalarGridSpec(
            num_scalar_prefetch=1, grid=(4, nb),
            in_specs=[pl.BlockSpec((None, rb, n), lambda k, i, c_ref: (k, c_ref[0] * nb + i, 0)),
                      pl.BlockSpec((None, rb, n), lambda k, i, c_ref: (k, i, 0))],
            out_specs=pl.BlockSpec((None, rb, n), lambda k, i, c_ref: (k, i, 0))),
        out_shape=S((4, half, n), BF16), compiler_params=_cp("parallel", "parallel"), name=name)(core, g, recv)


def _sum_chips(part, recv, place, name):
    _, rows, n = part.shape
    rb = _row_block(rows, 64)
    nb = rows // rb

    def body(p_ref, own_ref, r0, r1, r2, r3, o_ref):
        own = own_ref[...].astype(F32)
        terms = [jnp.where(p_ref[0] == k, own, r[...].astype(F32)) for k, r in enumerate((r0, r1, r2, r3))]
        o_ref[...] = ((terms[0] + terms[1]) + terms[2]) + terms[3]

    def slot(k):
        return pl.BlockSpec((None, rb, n), lambda i, p_ref: (jnp.where(p_ref[0] == k, (k + 1) % 4, k), i, 0))

    return pl.pallas_call(
        body,
        grid_spec=pltpu.PrefetchScalarGridSpec(
            num_scalar_prefetch=1, grid=(nb,),
            in_specs=[pl.BlockSpec((None, rb, n), lambda i, p_ref: (p_ref[0], i, 0))] + [slot(k) for k in range(4)],
            out_specs=pl.BlockSpec((rb, n), lambda i, p_ref: (p_ref[1] * nb + i, 0))),
        out_shape=S((2 * rows, n), F32), compiler_params=_cp("parallel"), name=name)(place, part, recv, recv, recv, recv)


def _sum_devices(own, spread, me, name):
    rows = own.shape[0]

    def body(me_ref, own_ref, *refs):
        acc = None
        for k, r in enumerate(refs[:8]):
            term = jnp.where(me_ref[0] == k, own_ref[...], r[...])
            acc = term if acc is None else acc + term
        refs[8][...] = acc

    def slot(k):
        return pl.BlockSpec((None, rows, 128), lambda i, me_ref: (jnp.where(me_ref[0] == k, (k + 1) % 8, k), 0, 0))

    whole = pl.BlockSpec((rows, 128), lambda i, me_ref: (0, 0))
    return pl.pallas_call(
        body,
        grid_spec=pltpu.PrefetchScalarGridSpec(num_scalar_prefetch=1, grid=(1,), in_specs=[whole] + [slot(k) for k in range(8)],
                                               out_specs=whole),
        out_shape=S((rows, 128), F32), compiler_params=_cp("arbitrary"), name=name)(me, own, *[spread] * 8)


def _adamw(w, g, m, v, name, regive=False):
    rows, n = w.shape
    rb = rows if rows * n * 4 <= (1 << 21) else _row_block(rows, 128)
    c1 = 1.0 - ADAM_B1 ** ADAM_STEP
    c2 = 1.0 - ADAM_B2 ** ADAM_STEP

    def body(w_ref, g_ref, m_ref, v_ref, d_ref, nm_ref, nv_ref, *again):
        gv = g_ref[...]
        for ref in again:
            ref[...] = gv
        nm = ADAM_B1 * m_ref[...] + (1.0 - ADAM_B1) * gv
        nv = ADAM_B2 * v_ref[...] + (1.0 - ADAM_B2) * (gv * gv)
        nm_ref[...] = nm
        nv_ref[...] = nv
        d_ref[...] = -ADAM_LR * ((nm / c1) / (jnp.sqrt(nv / c2) + ADAM_EPS) + ADAM_WD * w_ref[...])

    bs = pl.BlockSpec((rb, n), lambda i: (i, 0))
    n_out = 4 if regive else 3
    return pl.pallas_call(body, grid=(rows // rb,), in_specs=[bs] * 4, out_specs=[bs] * n_out, out_shape=[S((rows, n), F32)] * n_out,
                          compiler_params=_cp("parallel"), name=name)(w, g, m, v)


_BIG = ("w_in", "w_out", "w_ffn_gate", "w_ffn_up", "w_ffn_down")
_SMALL = ("pre_mix_norm", "post_mix_norm", "pre_ffn_norm", "post_ffn_norm", "conv_w", "conv_b", "w_rgate", "b_rgate",
          "w_igate", "b_igate", "lru_lambda", "attn_out_norm", "lru_out_norm")
_SMALL_EARLY = _SMALL[1:]
_WEIGHTS = ("pre_mix_norm", "post_mix_norm", "pre_ffn_norm", "post_ffn_norm", "w_in", "conv_w", "conv_b", "w_rgate", "b_rgate",
            "w_igate", "b_igate", "lru_lambda", "attn_out_norm", "lru_out_norm", "w_out", "w_ffn_gate", "w_ffn_up", "w_ffn_down")


def _pack(arrays):
    flat = []
    for a in arrays:
        f = a.reshape(-1)
        flat.append(jnp.pad(f, (0, (-f.shape[0]) % 1024)))
    return jnp.concatenate(flat).reshape(-1, 128)


def _unpack(packed, shapes):
    out, pos = [], 0
    flat = packed.reshape(-1)
    for s in shapes:
        size = math.prod(s)
        out.append(flat[pos:pos + size].reshape(s))
        pos += size + (-size) % 1024
    return out


def kernel(x, pre_mix_norm, post_mix_norm, pre_ffn_norm, post_ffn_norm, w_in, conv_w, conv_b, w_rgate, b_rgate, w_igate, b_igate, lru_lambda, attn_out_norm, lru_out_norm, w_out, w_ffn_gate, w_ffn_up, w_ffn_down, loss_target, m_pre_mix_norm, m_post_mix_norm, m_pre_ffn_norm, m_post_ffn_norm, m_w_in, m_conv_w, m_conv_b, m_w_rgate, m_b_rgate, m_w_igate, m_b_igate, m_lru_lambda, m_attn_out_norm, m_lru_out_norm, m_w_out, m_w_ffn_gate, m_w_ffn_up, m_w_ffn_down, v_pre_mix_norm, v_post_mix_norm, v_pre_ffn_norm, v_post_ffn_norm, v_w_in, v_conv_w, v_conv_b, v_w_rgate, v_b_rgate, v_w_igate, v_b_igate, v_lru_lambda, v_attn_out_norm, v_lru_out_norm, v_w_out, v_w_ffn_gate, v_w_ffn_up, v_w_ffn_down):
    given = dict(locals())
    w = {n: given[n][0] for n in _WEIGHTS}
    m = {n: given["m_" + n][0] for n in _WEIGHTS}
    v = {n: given["v_" + n][0] for n in _WEIGHTS}
    xs, target = x[0], loss_target[0]
    d = xs.shape[1]
    chip = (2 * lax.axis_index("x") + lax.axis_index("y")).astype(jnp.int32)
    place = jnp.stack([chip, lax.axis_index("c").astype(jnp.int32)])

    slots = {n: _into_slot(w[n], place[0:1], _MXU, "slot_" + n) for n in _BIG}
    slots["conv_w"] = _into_slot(w["conv_w"], place[0:1], F32, "slot_conv_w")
    ex = _Exchange(slots, place)
    row = lambda a: a.reshape(1, -1)
    norms = tuple(row(w[n]) for n in ("pre_mix_norm", "post_mix_norm", "pre_ffn_norm", "post_ffn_norm"))

    loss_cols, grad_x, small = _local_step(
        xs, target, norms, ex, row(w["conv_b"]), w["w_rgate"], row(w["b_rgate"]),
        w["w_igate"], row(w["b_igate"]), row(w["lru_lambda"]), row(w["attn_out_norm"]), row(w["lru_out_norm"]))

    loss = lax.psum(0.5 * jnp.sum(loss_cols) / d, ("x", "y", "c"))

    ex.alone("grads_w_in_share")
    reduced = {n: ex.full[n] for n in _BIG}
    early = _unpack(ex.small_sum("early"), [small[n].shape for n in _SMALL_EARLY])
    late = _unpack(ex.small_sum("late"), [small["pre_mix_norm"].shape])
    for n, g in zip(_SMALL_EARLY + ("pre_mix_norm",), early + late):
        reduced[n] = g.reshape(w[n].shape) if n != "conv_w" else lax.dynamic_slice_in_dim(g, chip * w[n].shape[1], w[n].shape[1], axis=1)

    delta, new_m, new_v = {}, {}, {}
    for n in _BIG:
        delta[n], new_m[n], new_v[n], reduced[n] = _adamw(w[n], reduced[n], m[n], v[n], "adamw_" + n, regive=True)
    shapes = [w[n].shape for n in _SMALL]
    packed = _adamw(*[_pack([src[n] for n in _SMALL]) for src in (w, reduced, m, v)], "adamw_small")
    for out, p in zip((delta, new_m, new_v), packed):
        out.update(zip(_SMALL, _unpack(p, shapes)))

    lead = lambda a: a[None]
    return (loss, lead(grad_x), *[lead(reduced[n]) for n in _WEIGHTS], *[lead(delta[n]) for n in _WEIGHTS],
            *[lead(new_m[n]) for n in _WEIGHTS], *[lead(new_v[n]) for n in _WEIGHTS])
```

```python
import functools
import math

import jax
import jax.numpy as jnp
from jax import lax
from jax.experimental import pallas as pl
from jax.experimental.pallas import tpu as pltpu

F32 = jnp.float32
BF16 = jnp.bfloat16
_MXU = BF16
S = jax.ShapeDtypeStruct

RMS_EPS = 1e-6
HEAD_DIM = 128
CONV_WIDTH = 4
LRU_C = 8.0
ADAM_LR, ADAM_B1, ADAM_B2, ADAM_EPS, ADAM_WD, ADAM_STEP = 0.001, 0.9, 0.999, 1e-08, 0.01, 10
EXP_CUT = -105.0
VMEM_LIMIT = 60 * 1024 * 1024
ROW_TILE = 256
SEQ_TILE = 256
ATTN_BLOCK = 256
ATTN_HEADS = 2
MESH = pl.DeviceIdType.MESH


def _cp(*sem):
    return pltpu.CompilerParams(dimension_semantics=sem, vmem_limit_bytes=VMEM_LIMIT)


def _dot(a, b):
    return jnp.dot(a, b, preferred_element_type=F32)


def _dot_nt(a, b):
    return lax.dot_general(a, b, (((1,), (1,)), ((), ())), preferred_element_type=F32)


def _dot_tn(a, b):
    return lax.dot_general(a, b, (((0,), (0,)), ((), ())), preferred_element_type=F32)


def _rstd(v):
    return lax.rsqrt(jnp.mean(v * v, axis=-1, keepdims=True) + RMS_EPS)


def _rms_bwd(dn, vh, r, gain):
    dvh = dn * gain
    dv = r * (dvh - vh * jnp.mean(dvh * vh, axis=-1, keepdims=True))
    return dv, jnp.sum(dn * vh, axis=0, keepdims=True)


def _log_sigmoid(z):
    return jnp.minimum(z, 0.0) - jnp.log(1.0 + jnp.exp(-jnp.abs(z)))


def _expm1(v):
    small = v * (1.0 + v * (0.5 + v * (1.0 / 6.0 + v * (1.0 / 24.0 + v * (1.0 / 120.0)))))
    return jnp.where(jnp.abs(v) < 0.04, small, jnp.exp(v) - 1.0)


_GELU_C = math.sqrt(2.0 / math.pi)


def _gelu(v):
    return 0.5 * v * (1.0 + jnp.tanh(_GELU_C * (v + 0.044715 * v * v * v)))


def _gelu_grad(v):
    th = jnp.tanh(_GELU_C * (v + 0.044715 * v * v * v))
    return 0.5 * (1.0 + th) + 0.5 * v * (1.0 - th * th) * _GELU_C * (1.0 + 3.0 * 0.044715 * v * v)


def _row_spec(tm, d):
    return pl.BlockSpec((tm, d), lambda i: (i, 0))


def _vec_spec(d):
    return pl.BlockSpec((1, d), lambda i: (0, 0))


_ANY = pl.BlockSpec(memory_space=pl.ANY)


def _place():
    x, y, c = lax.axis_index("x"), lax.axis_index("y"), lax.axis_index("c")
    return x, y, c, [(1 - x, y), (x, 1 - y), (1 - x, 1 - y)]


def _remote(src, dst, send_sem, recv_sem, to):
    return pltpu.make_async_remote_copy(src_ref=src, dst_ref=dst, send_sem=send_sem, recv_sem=recv_sem,
                                        device_id=to, device_id_type=MESH)


class _Carrier:
    def __init__(self):
        self.inputs, self.out_shapes, self.aliases, self.ops, self.n_sems, self.results = [], [], {}, [], 0, None

    def inplace(self, arr):
        self.aliases[len(self.inputs)] = len(self.out_shapes)
        self.inputs.append(arr)
        self.out_shapes.append(S(arr.shape, arr.dtype))
        return len(self.out_shapes) - 1

    def read(self, arr):
        self.inputs.append(arr)
        return len(self.inputs) - 1

    def fresh(self, shape, dtype):
        self.out_shapes.append(S(shape, dtype))
        return len(self.out_shapes) - 1

    def _add(self, n_sems, copies):
        base = self.n_sems
        self.n_sems += n_sems

        def start(ins, outs, send, recv):
            for k, (src, dst, _, to) in enumerate(copies(ins, outs)):
                _remote(src, dst, send.at[base + k], recv.at[base + k], to).start()

        def finish(ins, outs, send, recv):
            for k, (src, _, land, to) in enumerate(copies(ins, outs)):
                _remote(src, land, send.at[base + k], recv.at[base + k], to).wait()

        self.ops.append((start, finish))

    def gather_ici(self, o, rows=None, split=True):
        half = self.out_shapes[o].shape[1] // 2
        lo, size = rows or (0, half)

        def copies(ins, outs):
            x, y, c, chips = _place()
            part = (lambda ref: ref.at[pl.ds(c * half + lo, size)]) if split else (lambda ref: ref)
            mine = part(outs[o].at[2 * x + y])
            return [(mine, mine, part(outs[o].at[2 * px + py]), (px, py, c)) for px, py in chips]

        self._add(3, copies)

    def gather_d2d(self, o, rows=None):
        half = self.out_shapes[o].shape[1] // 2
        lo, size = rows or (0, half)

        def copies(ins, outs):
            x, y, c, chips = _place()
            at = lambda k, cc: outs[o].at[k].at[pl.ds(cc * half + lo, size)]
            return [(at(2 * px + py, c), at(2 * px + py, c), at(2 * px + py, 1 - c), (x, y, 1 - c)) for px, py in chips]

        self._add(3, copies)

    def swap(self, i, o):
        half = self.inputs[i].shape[1] // 2

        def copies(ins, outs):
            x, y, c, _ = _place()
            return [(ins[i].at[:, pl.ds((1 - c) * half, half)], outs[o], outs[o], (x, y, 1 - c))]

        self._add(1, copies)

    def scatter(self, i, o, rows=None):
        lo, size = rows or (0, self.inputs[i].shape[1])

        def copies(ins, outs):
            x, y, c, chips = _place()
            cut = lambda ref: ref.at[pl.ds(lo, size)]
            return [(cut(ins[i].at[2 * px + py]), cut(outs[o].at[2 * x + y]), cut(outs[o].at[2 * px + py]), (px, py, c)) for px, py in chips]

        self._add(3, copies)

    def share(self, o):
        r = self.out_shapes[o].shape[0] // 2

        def copies(ins, outs):
            x, y, c, _ = _place()
            mine = outs[o].at[pl.ds(c * r, r)]
            return [(mine, mine, outs[o].at[pl.ds((1 - c) * r, r)], (x, y, 1 - c))]

        self._add(1, copies)

    def spread(self, i, o):
        def copies(ins, outs):
            x, y, c, _ = _place()
            me = 4 * x + 2 * y + c
            out = []
            for d in range(1, 8):
                to, frm = (me + d) % 8, (me + 8 - d) % 8
                out.append((ins[i], outs[o].at[me], outs[o].at[frm], (to // 4, (to // 2) % 2, to % 2)))
            return out

        self._add(7, copies)

    def _pallas(self, body, n_in, n_out, scratch, **kw):
        k_in, k_out = len(self.inputs), len(self.out_shapes)
        grid = kw.get("grid", ())

        def wrapped(*refs):
            ins, cins = refs[:n_in], refs[n_in:n_in + k_in]
            outs = refs[n_in + k_in:n_in + k_in + n_out]
            couts = refs[n_in + k_in + n_out:n_in + k_in + n_out + k_out]
            own = refs[n_in + k_in + n_out + k_out:]
            send, recv = own[len(scratch):]
            ids = [pl.program_id(a) for a in range(len(grid))]
            first = functools.reduce(jnp.logical_and, [a == 0 for a in ids], True)
            last = functools.reduce(jnp.logical_and, [a == g - 1 for a, g in zip(ids, grid)], True)

            def go(stage):
                for op in self.ops:
                    op[stage](cins, couts, send, recv)

            if grid:
                pl.when(first)(lambda: go(0))
                body(*ins, *outs, *own[:len(scratch)])
                pl.when(last)(lambda: go(1))
            else:
                go(0)
                go(1)

        sem = pltpu.SemaphoreType.DMA((self.n_sems,))
        return pl.pallas_call(
            wrapped, in_specs=list(kw.get("in_specs", [])) + [_ANY] * k_in, out_specs=list(kw.get("out_specs", [])) + [_ANY] * k_out,
            out_shape=list(kw.get("out_shape", [])) + self.out_shapes, scratch_shapes=list(scratch) + [sem, sem],
            input_output_aliases={n_in + i: n_out + o for i, o in self.aliases.items()}, name=kw["name"],
            **({"grid": grid, "compiler_params": _cp(*["arbitrary"] * len(grid))} if grid else {}))

    def run(self, body, kw, *args):
        single = not isinstance(kw["out_shape"], (list, tuple))
        out_shape = [kw["out_shape"]] if single else list(kw["out_shape"])
        out_specs = [kw["out_specs"]] if single else list(kw["out_specs"])
        res = self._pallas(body, len(args), len(out_shape), kw.get("scratch_shapes", []), grid=kw["grid"], in_specs=kw["in_specs"],
                           out_specs=out_specs, out_shape=out_shape, name=kw["name"])(*args, *self.inputs)
        self.results = list(res[len(out_shape):])
        return res[0] if single else list(res[:len(out_shape)])

    def run_alone(self, name):
        self.results = list(self._pallas(None, 0, 0, [], name=name)(*self.inputs))


def _call(comm, body, **kw):
    if comm is None:
        return pl.pallas_call(body, **kw)
    return functools.partial(comm.run, body, kw)


def _in_proj_streamed(x, gain, car, o_w, place, *, bm, name):
    m, k = x.shape
    n = car.out_shapes[o_w].shape[2]
    ni, half = m // bm, k // 2
    k_in, k_out = len(car.inputs), len(car.out_shapes)
    order = lambda p: ((p & 1) << 1) | (p >> 1)

    def body(place_ref, x_ref, g_ref, *refs):
        cins, (hn_ref, o_ref, ob_ref), couts = refs[:k_in], refs[k_in:k_in + 3], refs[k_in + 3:k_in + 3 + k_out]
        wbuf, local, ici_send, ici_recv, d2d_send, d2d_recv, send, recv = refs[k_in + 3 + k_out:]
        p, i = pl.program_id(0), pl.program_id(1)
        x, y, c, chips = _place()
        me = 2 * x + y
        rows = lambda chunk, cc: couts[o_w].at[chunk].at[pl.ds(cc * half, half)]

        @pl.when(jnp.logical_and(p == 0, i == 0))
        def _():
            for j, (px, py) in enumerate(chips):
                _remote(rows(me, c), rows(me, c), ici_send.at[j], ici_recv.at[j], (px, py, c)).start()
            for op in car.ops:
                op[0](cins, couts, send, recv)

        for j, (px, py) in enumerate(chips):
            @pl.when(jnp.logical_and(p == j + 1, i == 0))
            def _(j=j, px=px, py=py):
                landed, other = rows(2 * px + py, c), rows(2 * px + py, 1 - c)
                _remote(landed, landed, ici_send.at[j], ici_recv.at[j], (px, py, c)).wait_recv()
                _remote(landed, landed, d2d_send.at[j], d2d_recv.at[j], (x, y, 1 - c)).start()
                _remote(other, other, d2d_send.at[j], d2d_recv.at[j], (x, y, 1 - c)).wait_recv()

        @pl.when(i == 0)
        def _():
            cp = pltpu.make_async_copy(couts[o_w].at[me ^ order(p)], wbuf, local.at[0])
            cp.start()
            cp.wait()

        xv = x_ref[...]
        hn = ((xv * _rstd(xv)) * g_ref[...]).astype(_MXU)
        hn_ref[...] = hn
        res = _dot(hn, wbuf[...])
        o_ref[...] = res
        ob_ref[...] = res.astype(ob_ref.dtype)

        @pl.when(jnp.logical_and(p == 3, i == ni - 1))
        def _():
            for j, (px, py) in enumerate(chips):
                _remote(rows(me, c), rows(me, c), ici_send.at[j], ici_recv.at[j], (px, py, c)).wait_send()
                _remote(rows(me, c), rows(me, c), d2d_send.at[j], d2d_recv.at[j], (x, y, 1 - c)).wait_send()
            for op in car.ops:
                op[1](cins, couts, send, recv)

    ospec = pl.BlockSpec((bm, n), lambda p, i, place_ref: (i, place_ref[0] ^ order(p)))
    rows = pl.BlockSpec((bm, k), lambda p, i, place_ref: (i, 0))
    three, sems = pltpu.SemaphoreType.DMA((3,)), pltpu.SemaphoreType.DMA((max(car.n_sems, 1),))
    res = pl.pallas_call(
        body,
        grid_spec=pltpu.PrefetchScalarGridSpec(
            num_scalar_prefetch=1, grid=(4, ni),
            in_specs=[rows, pl.BlockSpec((1, k), lambda p, i, place_ref: (0, 0))] + [_ANY] * k_in,
            out_specs=[pl.BlockSpec((bm, k), lambda p, i, place_ref: (p * ni + i, 0)), ospec, ospec] + [_ANY] * k_out,
            scratch_shapes=[pltpu.VMEM((k, n), _MXU), pltpu.SemaphoreType.DMA((1,)), three, three, three, three, sems, sems]),
        out_shape=[S((4 * m, k), _MXU), S((m, 4 * n), F32), S((m, 4 * n), _MXU)] + car.out_shapes,
        input_output_aliases={3 + a: 3 + o for a, o in car.aliases.items()},
        compiler_params=_cp("arbitrary", "arbitrary"), name=name)(place, x, gain, *car.inputs)
    car.results = list(res[3:])
    return res[0], res[1], res[2]


def _mm_nn(a, b3, *, bm, bn, name, also=None, comm=None):
    m, k = a.shape
    c, _, n = b3.shape
    ni, nj = m // bm, n // bn

    def body(a_ref, b_ref, *o_refs):
        res = _dot(a_ref[...], b_ref[...])
        for o_ref in o_refs:
            o_ref[...] = res.astype(o_ref.dtype)

    ospec = pl.BlockSpec((bm, bn), lambda cc, j, i: (i, cc * nj + j))
    dtypes = [F32] + ([] if also is None else [also])
    out = _call(
        comm, body, grid=(c, nj, ni),
        in_specs=[pl.BlockSpec((bm, k), lambda cc, j, i: (i, 0)), pl.BlockSpec((None, k, bn), lambda cc, j, i: (cc, 0, j))],
        out_specs=[ospec] * len(dtypes), out_shape=[S((m, c * n), dt) for dt in dtypes],
        compiler_params=_cp("parallel", "parallel", "parallel"), name=name)(a, b3)
    return out[0] if also is None else out


def _mm_nt(a, b3, *, bm, bo, out_dtype, name, comm=None):
    m = a.shape[0]
    c, ko, n = b3.shape
    ni, nj = m // bm, ko // bo

    def body(a_ref, b_ref, o_ref):
        acc = _dot_nt(a_ref[:, 0:n], b_ref[0])
        for cc in range(1, c):
            acc = acc + _dot_nt(a_ref[:, cc * n:(cc + 1) * n], b_ref[cc])
        o_ref[...] = acc.astype(o_ref.dtype)

    return _call(
        comm, body, grid=(nj, ni),
        in_specs=[pl.BlockSpec((bm, c * n), lambda j, i: (i, 0)),
                  pl.BlockSpec((c, bo, n), lambda j, i: (0, j, 0))],
        out_specs=pl.BlockSpec((bm, bo), lambda j, i: (i, j)),
        out_shape=S((m, ko), out_dtype),
        compiler_params=_cp("parallel", "parallel"), name=name)(a, b3)


def _mm_tn(a, b, c, *, bm, bk, name, comm=None):
    m, k = b.shape[0], a.shape[1]
    n = b.shape[1] // c
    nm, nk = m // bm, k // bk

    def body(a_ref, b_ref, o_ref, acc):
        mm = pl.program_id(2)

        @pl.when(mm == 0)
        def _():
            acc[...] = jnp.zeros_like(acc)

        acc[...] += _dot_tn(a_ref[...], b_ref[...])

        @pl.when(mm == nm - 1)
        def _():
            o_ref[...] = acc[...]

    return _call(
        comm, body, grid=(c, nk, nm),
        in_specs=[pl.BlockSpec((bm, bk), lambda cc, j, mm: (mm, j)),
                  pl.BlockSpec((bm, n), lambda cc, j, mm: (mm, cc))],
        out_specs=pl.BlockSpec((None, bk, n), lambda cc, j, mm: (cc, j, 0)),
        out_shape=S((c, k, n), F32),
        scratch_shapes=[pltpu.VMEM((bk, n), F32)],
        compiler_params=_cp("parallel", "parallel", "arbitrary"), name=name)(a, b)


def _swiglu_fwd(hn, wg3, wu3, *, bm, name, comm=None):
    m, k = hn.shape
    c, _, n = wg3.shape

    def body(a_ref, g_ref, u_ref, dgate_ref, dup_ref, act_ref):
        a = a_ref[...]
        gate = _dot(a, g_ref[...])
        up = _dot(a, u_ref[...])
        sg = jax.nn.sigmoid(gate)
        silu = gate * sg
        dgate_ref[...] = (up * (sg * (1.0 + gate * (1.0 - sg)))).astype(dgate_ref.dtype)
        dup_ref[...] = silu.astype(dup_ref.dtype)
        act_ref[...] = (silu * up).astype(act_ref.dtype)

    wspec = pl.BlockSpec((None, k, n), lambda cc, i: (cc, 0, 0))
    ospec = pl.BlockSpec((bm, n), lambda cc, i: (i, cc))
    return _call(
        comm, body, grid=(c, m // bm),
        in_specs=[pl.BlockSpec((bm, k), lambda cc, i: (i, 0)), wspec, wspec],
        out_specs=[ospec, ospec, ospec],
        out_shape=[S((m, c * n), _MXU), S((m, c * n), _MXU), S((m, c * n), _MXU)],
        compiler_params=_cp("parallel", "parallel"), name=name)(hn, wg3, wu3)


def _swiglu_bwd(df, wd, act_dgate, act_dup, *, bm, bo, name):
    m, k = df.shape
    ko = wd.shape[0]

    def body(a_ref, b_ref, g_ref, u_ref, dg_ref, du_ref):
        dact = _dot_nt(a_ref[...], b_ref[...])
        dg_ref[...] = (dact * g_ref[...].astype(F32)).astype(dg_ref.dtype)
        du_ref[...] = (dact * u_ref[...].astype(F32)).astype(du_ref.dtype)

    ospec = pl.BlockSpec((bm, bo), lambda j, i: (i, j))
    return pl.pallas_call(
        body, grid=(ko // bo, m // bm),
        in_specs=[pl.BlockSpec((bm, k), lambda j, i: (i, 0)), pl.BlockSpec((bo, k), lambda j, i: (j, 0)), ospec, ospec],
        out_specs=[ospec, ospec],
        out_shape=[S((m, ko), _MXU), S((m, ko), _MXU)],
        compiler_params=_cp("parallel", "parallel"), name=name)(df, wd, act_dgate, act_dup)


def _rms_fwd(x, gain, name, comm=None):
    t, d = x.shape
    tm = min(t, ROW_TILE)

    def body(x_ref, g_ref, o_ref):
        xv = x_ref[...]
        o_ref[...] = ((xv * _rstd(xv)) * g_ref[...]).astype(o_ref.dtype)

    return _call(comm, body, grid=(t // tm,), in_specs=[_row_spec(tm, d), _vec_spec(d)], out_specs=_row_spec(tm, d),
                          out_shape=S((t, d), _MXU), compiler_params=_cp("parallel"), name=name)(x, gain)


def _outnorm_fwd(o, yl, ga, gl, name, comm=None):
    t, w = o.shape
    tm = min(t, ROW_TILE)

    def body(o_ref, l_ref, ga_ref, gl_ref, y_ref):
        ov, lv = o_ref[...], l_ref[...]
        y_ref[:, :w] = ((ov * _rstd(ov)) * ga_ref[...]).astype(y_ref.dtype)
        y_ref[:, w:] = ((lv * _rstd(lv)) * gl_ref[...]).astype(y_ref.dtype)

    return _call(comm, body, grid=(t // tm,), in_specs=[_row_spec(tm, w), _row_spec(tm, w), _vec_spec(w), _vec_spec(w)],
                 out_specs=_row_spec(tm, 2 * w), out_shape=S((t, 2 * w), _MXU),
                 compiler_params=_cp("parallel"), name=name)(o, yl, ga, gl)


def _mid_fwd(x, mix, g_post, g_pre, name, comm=None):
    t, d = x.shape
    tm = min(t, ROW_TILE)

    def body(x_ref, m_ref, gp_ref, gn_ref, x2_ref, hn_ref):
        mv = m_ref[...]
        x2 = x_ref[...] + (mv * _rstd(mv)) * gp_ref[...]
        x2_ref[...] = x2
        hn_ref[...] = ((x2 * _rstd(x2)) * gn_ref[...]).astype(hn_ref.dtype)

    return _call(comm, body, grid=(t // tm,), in_specs=[_row_spec(tm, d), _row_spec(tm, d), _vec_spec(d), _vec_spec(d)],
                          out_specs=[_row_spec(tm, d), _row_spec(tm, d)], out_shape=[S((t, d), F32), S((t, d), _MXU)],
                          compiler_params=_cp("parallel"), name=name)(x, mix, g_post, g_pre)


def _final(f, x2, target, g_post, name):
    t, d = f.shape
    tm = min(t, ROW_TILE)

    def body(f_ref, x2_ref, t_ref, g_ref, loss_ref, dout_ref, df_ref, dg_ref):
        @pl.when(pl.program_id(0) == 0)
        def _():
            loss_ref[...] = jnp.zeros_like(loss_ref)
            dg_ref[...] = jnp.zeros_like(dg_ref)

        fv = f_ref[...]
        r = _rstd(fv)
        fh = fv * r
        err = (x2_ref[...] + fh * g_ref[...]) - t_ref[...]
        loss_ref[...] += jnp.sum(err * err, axis=0, keepdims=True)
        dout = err * (1.0 / d)
        dout_ref[...] = dout
        dfv, dg = _rms_bwd(dout, fh, r, g_ref[...])
        df_ref[...] = dfv.astype(df_ref.dtype)
        dg_ref[...] += dg

    return pl.pallas_call(
        body, grid=(t // tm,),
        in_specs=[_row_spec(tm, d), _row_spec(tm, d), _row_spec(tm, d), _vec_spec(d)],
        out_specs=[_vec_spec(d), _row_spec(tm, d), _row_spec(tm, d), _vec_spec(d)],
        out_shape=[S((1, d), F32), S((t, d), F32), S((t, d), _MXU), S((1, d), F32)],
        compiler_params=_cp("arbitrary"), name=name)(f, x2, target, g_post)


def _mid_bwd(dhn_a, dhn_b, dout, x2, mix, g_pre, g_post, name, comm=None):
    t, d = x2.shape
    tm = min(t, ROW_TILE)

    def body(da_ref, db_ref, do_ref, x2_ref, m_ref, gn_ref, gp_ref, dx2_ref, dm_ref, dgn_ref, dgp_ref):
        @pl.when(pl.program_id(0) == 0)
        def _():
            dgn_ref[...] = jnp.zeros_like(dgn_ref)
            dgp_ref[...] = jnp.zeros_like(dgp_ref)

        x2 = x2_ref[...]
        r = _rstd(x2)
        dxa, dgn = _rms_bwd(da_ref[...] + db_ref[...], x2 * r, r, gn_ref[...])
        dx2 = do_ref[...] + dxa
        dx2_ref[...] = dx2
        dgn_ref[...] += dgn
        mv = m_ref[...]
        rm = _rstd(mv)
        dmv, dgp = _rms_bwd(dx2, mv * rm, rm, gp_ref[...])
        dm_ref[...] = dmv.astype(dm_ref.dtype)
        dgp_ref[...] += dgp

    rs, vs = _row_spec(tm, d), _vec_spec(d)
    return _call(
        comm, body, grid=(t // tm,), in_specs=[rs, rs, rs, rs, rs, vs, vs], out_specs=[rs, rs, vs, vs],
        out_shape=[S((t, d), F32), S((t, d), _MXU), S((1, d), F32), S((1, d), F32)],
        compiler_params=_cp("arbitrary"), name=name)(dhn_a, dhn_b, dout, x2, mix, g_pre, g_post)


def _first_bwd(dhn, dx2, x, gain, name, comm=None):
    t, d = x.shape
    tm = min(t, ROW_TILE)

    def body(dh_ref, dx2_ref, x_ref, g_ref, dx_ref, dg_ref):
        @pl.when(pl.program_id(0) == 0)
        def _():
            dg_ref[...] = jnp.zeros_like(dg_ref)

        xv = x_ref[...]
        r = _rstd(xv)
        dxa, dg = _rms_bwd(dh_ref[...], xv * r, r, g_ref[...])
        dx_ref[...] = dx2_ref[...] + dxa
        dg_ref[...] += dg

    rs, vs = _row_spec(tm, d), _vec_spec(d)
    return _call(comm, body, grid=(t // tm,), in_specs=[rs, rs, rs, vs], out_specs=[rs, vs],
                          out_shape=[S((t, d), F32), S((1, d), F32)], compiler_params=_cp("arbitrary"), name=name)(dhn, dx2, x, gain)


def _outnorm_bwd(dy, o, yl, ga, gl, name, comm=None):
    t, w = o.shape
    tm = min(t, ROW_TILE)

    def body(dy_ref, o_ref, l_ref, ga_ref, gl_ref, do_ref, dl_ref, dga_ref, dgl_ref):
        @pl.when(pl.program_id(0) == 0)
        def _():
            dga_ref[...] = jnp.zeros_like(dga_ref)
            dgl_ref[...] = jnp.zeros_like(dgl_ref)

        ov, lv = o_ref[...], l_ref[...]
        ra, rl = _rstd(ov), _rstd(lv)
        dov, dga = _rms_bwd(dy_ref[:, :w], ov * ra, ra, ga_ref[...])
        dlv, dgl = _rms_bwd(dy_ref[:, w:], lv * rl, rl, gl_ref[...])
        do_ref[...] = dov.astype(do_ref.dtype)
        dl_ref[...] = dlv
        dga_ref[...] += dga
        dgl_ref[...] += dgl

    rs, vs = _row_spec(tm, w), _vec_spec(w)
    return _call(comm, body, grid=(t // tm,), in_specs=[_row_spec(tm, 2 * w), rs, rs, vs, vs], out_specs=[rs, rs, vs, vs],
                          out_shape=[S((t, w), _MXU), S((t, w), F32), S((1, w), F32), S((1, w), F32)],
                          compiler_params=_cp("arbitrary"), name=name)(dy, o, yl, ga, gl)


def _tri_sum(v, tri):
    return _dot(v.astype(_MXU), tri)


def _attn_tile(qb, kb, row, col, shift, scale):
    z = _dot_nt(qb, kb) * scale
    mask = (col + shift) < row
    lb = _log_sigmoid(z)
    lm = jnp.where(mask, lb - z, 0.0)
    return mask, lb, lm


def _attn_fwd(proj, n_heads, name, comm=None):
    t = proj.shape[0]
    bq = min(t, ATTN_BLOCK)
    nq = t // bq
    scale = 1.0 / math.sqrt(HEAD_DIM)

    heads = [slice(a * HEAD_DIM, (a + 1) * HEAD_DIM) for a in range(ATTN_HEADS)]

    def body(q_ref, k_ref, v_ref, o_ref):
        row = lax.broadcasted_iota(jnp.int32, (bq, bq), 0)
        col = lax.broadcasted_iota(jnp.int32, (bq, bq), 1)
        tri = (row > col).astype(_MXU)

        def per_q(qi, _):
            q0 = pl.multiple_of(qi * bq, bq)
            qbs = [q_ref[pl.ds(q0, bq), hd] for hd in heads]

            def cond(st):
                return jnp.logical_and(st[0] >= 0, st[1])

            def step(st):
                kj, _, carries, accs = st
                k0 = pl.multiple_of(kj * bq, bq)
                alive, new_carries, new_accs = None, [], []
                for hd, qb, carry, acc in zip(heads, qbs, carries, accs):
                    mask, lb, lm = _attn_tile(qb, k_ref[pl.ds(k0, bq), hd], row, col, (kj - qi) * bq, scale)
                    w = jnp.where(mask, jnp.exp(lb + _tri_sum(lm, tri) + carry), 0.0)
                    new_accs.append(acc + _dot(w.astype(_MXU), v_ref[pl.ds(k0, bq), hd]))
                    carry = carry + jnp.sum(lm, axis=1, keepdims=True)
                    new_carries.append(carry)
                    live = jnp.max(carry) > EXP_CUT
                    alive = live if alive is None else jnp.logical_or(alive, live)
                return kj - 1, alive, tuple(new_carries), tuple(new_accs)

            st = lax.while_loop(cond, step, (qi, jnp.bool_(True), (jnp.zeros((bq, 1), F32),) * ATTN_HEADS,
                                             (jnp.zeros((bq, HEAD_DIM), F32),) * ATTN_HEADS))
            for hd, acc in zip(heads, st[3]):
                o_ref[pl.ds(q0, bq), hd] = acc
            return 0

        lax.fori_loop(0, nq, per_q, 0)

    groups = n_heads // ATTN_HEADS
    hs = lambda off: pl.BlockSpec((t, ATTN_HEADS * HEAD_DIM), lambda h: (0, off + h))
    return _call(
        comm, body, grid=(groups,), in_specs=[hs(0), hs(groups), hs(2 * groups)], out_specs=hs(0),
        out_shape=S((t, n_heads * HEAD_DIM), F32), compiler_params=_cp("parallel"), name=name)(proj, proj, proj)


def _attn_bwd(proj, do, n_heads, name, comm=None):
    t = proj.shape[0]
    bq = min(t, ATTN_BLOCK)
    nq = t // bq
    scale = 1.0 / math.sqrt(HEAD_DIM)

    heads = [slice(a * HEAD_DIM, (a + 1) * HEAD_DIM) for a in range(ATTN_HEADS)]

    def body(q_ref, k_ref, v_ref, do_ref, dq_ref, dk_ref, dv_ref, dka_ref, dva_ref, g_ref, b_ref):
        dka_ref[...] = jnp.zeros_like(dka_ref)
        dva_ref[...] = jnp.zeros_like(dva_ref)
        row = lax.broadcasted_iota(jnp.int32, (bq, bq), 0)
        col = lax.broadcasted_iota(jnp.int32, (bq, bq), 1)
        tri = (row > col).astype(_MXU)
        tri_lt = (row < col).astype(_MXU)

        def per_q(qi, _):
            q0 = pl.multiple_of(qi * bq, bq)
            qbs = [q_ref[pl.ds(q0, bq), hd] for hd in heads]
            dobs = [do_ref[pl.ds(q0, bq), hd] for hd in heads]

            def cond(st):
                return jnp.logical_and(st[0] >= 0, st[1])

            def step(st):
                kj, _, carries = st
                k0 = pl.multiple_of(kj * bq, bq)
                alive, new_carries = None, []
                for a, (hd, qb, dob, carry) in enumerate(zip(heads, qbs, dobs, carries)):
                    mask, lb, lm = _attn_tile(qb, k_ref[pl.ds(k0, bq), hd], row, col, (kj - qi) * bq, scale)
                    w = jnp.where(mask, jnp.exp(lb + _tri_sum(lm, tri) + carry), 0.0)
                    g_ref[a, pl.ds(k0, bq), :] = w * _dot_nt(dob, v_ref[pl.ds(k0, bq), hd])
                    b_ref[a, pl.ds(k0, bq), :] = jnp.where(mask, jnp.exp(lb), 0.0)
                    dva_ref[pl.ds(k0, bq), hd] += _dot_tn(w.astype(_MXU), dob)
                    carry = carry + jnp.sum(lm, axis=1, keepdims=True)
                    new_carries.append(carry)
                    live = jnp.max(carry) > EXP_CUT
                    alive = live if alive is None else jnp.logical_or(alive, live)
                return kj - 1, alive, tuple(new_carries)

            st = lax.while_loop(cond, step, (qi, jnp.bool_(True), (jnp.zeros((bq, 1), F32),) * ATTN_HEADS))

            def back(kj, st2):
                k0 = pl.multiple_of(kj * bq, bq)
                out = []
                for a, (hd, qb, (before, dq)) in enumerate(zip(heads, qbs, st2)):
                    g = g_ref[a, pl.ds(k0, bq), :]
                    beta = b_ref[a, pl.ds(k0, bq), :]
                    dz = ((g * (1.0 - beta) - (before + _tri_sum(g, tri_lt)) * beta) * scale).astype(_MXU)
                    dka_ref[pl.ds(k0, bq), hd] += _dot_tn(dz, qb)
                    out.append((before + jnp.sum(g, axis=1, keepdims=True), dq + _dot(dz, k_ref[pl.ds(k0, bq), hd])))
                return tuple(out)

            st2 = lax.fori_loop(st[0] + 1, qi + 1, back, ((jnp.zeros((bq, 1), F32), jnp.zeros((bq, HEAD_DIM), F32)),) * ATTN_HEADS)
            for hd, (_, dq) in zip(heads, st2):
                dq_ref[pl.ds(q0, bq), hd] = dq.astype(dq_ref.dtype)
            return 0

        lax.fori_loop(0, nq, per_q, 0)
        dk_ref[...] = dka_ref[...].astype(dk_ref.dtype)
        dv_ref[...] = dva_ref[...].astype(dv_ref.dtype)

    groups = n_heads // ATTN_HEADS
    wide = ATTN_HEADS * HEAD_DIM
    hs = lambda off: pl.BlockSpec((t, wide), lambda h: (0, off + h))
    return _call(
        comm, body, grid=(groups,), in_specs=[hs(0), hs(groups), hs(2 * groups), hs(0)], out_specs=[hs(0), hs(0), hs(0)],
        out_shape=[S((t, n_heads * HEAD_DIM), _MXU)] * 3,
        scratch_shapes=[pltpu.VMEM((t, wide), F32), pltpu.VMEM((t, wide), F32),
                        pltpu.VMEM((ATTN_HEADS, t, bq), F32), pltpu.VMEM((ATTN_HEADS, t, bq), F32)],
        compiler_params=_cp("parallel"), name=name)(proj, proj, proj, do)


def _shift_down(cur, prev8, k):
    if k == 0:
        return cur
    row8 = lax.broadcasted_iota(jnp.int32, prev8.shape, 0)
    rc = pltpu.roll(cur, k, 0)
    top = jnp.where(row8 < k, pltpu.roll(prev8, k, 0), rc[0:8, :])
    return jnp.concatenate([top, rc[8:, :]], axis=0)


def _shift_up(cur, next8, k):
    if k == 0:
        return cur
    n = cur.shape[0]
    row8 = lax.broadcasted_iota(jnp.int32, next8.shape, 0)
    rc = pltpu.roll(cur, n - k, 0)
    bottom = jnp.where(row8 >= 8 - k, pltpu.roll(next8, 8 - k, 0), rc[n - 8:, :])
    return jnp.concatenate([rc[:n - 8, :], bottom], axis=0)


def _lru_conv(xl, prev8, cw, cb):
    xs = [_shift_down(xl, prev8, CONV_WIDTH - 1 - k) for k in range(CONV_WIDTH)]
    xc = xs[0] * cw[0:1, :]
    for k in range(1, CONV_WIDTH):
        xc = xc + xs[k] * cw[k:k + 1, :]
    return xs, xc + cb


def _lru_gates(xl, prev8, cw, cb, wr, br, wi, bi, ls):
    xs, xc = _lru_conv(xl, prev8, cw, cb)
    xcb = xc.astype(_MXU)
    r = jax.nn.sigmoid(_dot(xcb, wr) + br)
    i = jax.nn.sigmoid(_dot(xcb, wi) + bi)
    la = (LRU_C * r) * ls
    a = jnp.exp(la)
    mult = jnp.sqrt(-_expm1(2.0 * la))
    return xs, xc, r, i, a, mult


def _group_scan(a, b, reverse):
    n = a.shape[0]
    row = lax.broadcasted_iota(jnp.int32, a.shape, 0) % 8
    for d in (1, 2, 4):
        if reverse:
            m = row < 8 - d
            a_s, b_s = pltpu.roll(a, n - d, 0), pltpu.roll(b, n - d, 0)
        else:
            m = row >= d
            a_s, b_s = pltpu.roll(a, d, 0), pltpu.roll(b, d, 0)
        b = jnp.where(m, a * b_s + b, b)
        a = jnp.where(m, a * a_s, a)
    return a, b


def _lru_fwd(proj, col0, n_blocks, cw, cb, wr, br, wi, bi, lam, name, comm=None):
    t = proj.shape[0]
    tt = min(t, SEQ_TILE)
    nt = t // tt

    def body(xl_ref, gl_ref, cw_ref, cb_ref, wr_ref, br_ref, wi_ref, bi_ref, lam_ref, h_ref, y_ref, *kept):
        cwv, cbv, brv, biv = cw_ref[...], cb_ref[...], br_ref[...], bi_ref[...]
        wrv, wiv = wr_ref[...].astype(_MXU), wi_ref[...].astype(_MXU)
        ls = _log_sigmoid(lam_ref[...])

        def tile(ti, hin):
            t0 = pl.multiple_of(ti * tt, tt)
            p0 = pl.multiple_of(jnp.maximum(t0 - 8, 0), 8)
            prev8 = xl_ref[pl.ds(p0, 8), :] * (ti > 0).astype(F32)
            xl = xl_ref[pl.ds(t0, tt), :]
            _, xc, r, ig, a, mult = _lru_gates(xl, prev8, cwv, cbv, wrv, brv, wiv, biv, ls)
            for ref, val in zip(kept, (r, ig, a, mult)):
                ref[pl.ds(t0, tt), :] = val
            ga, gb = _group_scan(a, mult * (ig * xc), False)
            for g in range(tt // 8):
                hg = ga[8 * g:8 * g + 8, :] * hin + gb[8 * g:8 * g + 8, :]
                h_ref[pl.ds(t0 + 8 * g, 8), :] = hg
                hin = hg[7:8, :]
            y_ref[pl.ds(t0, tt), :] = h_ref[pl.ds(t0, tt), :] * _gelu(gl_ref[pl.ds(t0, tt), :])
            return hin

        lax.fori_loop(0, nt, tile, jnp.zeros((1, HEAD_DIM), F32))

    cs = lambda off: pl.BlockSpec((t, HEAD_DIM), lambda n: (0, off + n))
    vs = pl.BlockSpec((1, HEAD_DIM), lambda n: (0, n))
    ws = pl.BlockSpec((None, HEAD_DIM, HEAD_DIM), lambda n: (n, 0, 0))
    w = n_blocks * HEAD_DIM
    return _call(
        comm, body, grid=(n_blocks,),
        in_specs=[cs(col0), cs(col0 + n_blocks), pl.BlockSpec((CONV_WIDTH, HEAD_DIM), lambda n: (0, n)), vs, ws, vs, ws, vs, vs],
        out_specs=[cs(0)] * 6, out_shape=[S((t, w), F32)] * 6,
        compiler_params=_cp("parallel"), name=name)(proj, proj, cw, cb, wr, br, wi, bi, lam)


def _lru_bwd(proj, col0, n_blocks, h, kept, dyl, cw, cb, wr, wi, lam, name, comm=None):
    t = proj.shape[0]
    tt = min(t, SEQ_TILE)
    nt = t // tt

    def body(xl_ref, gl_ref, h_ref, r_ref, i_ref, a_ref, m_ref, dy_ref, cw_ref, cb_ref, wr_ref, wi_ref, lam_ref,
             dxl_ref, dgl_ref, dcw_ref, dcb_ref, dwr_ref, dbr_ref, dwi_ref, dbi_ref, dlam_ref, g_ref):
        cwv, cbv = cw_ref[...], cb_ref[...]
        wrv, wiv = wr_ref[...].astype(_MXU), wi_ref[...].astype(_MXU)
        lamv = lam_ref[...]
        ls = _log_sigmoid(lamv)
        for ref in (dcw_ref, dcb_ref, dwr_ref, dbr_ref, dwi_ref, dbi_ref, dlam_ref):
            ref[...] = jnp.zeros_like(ref)

        def tile(s, carry):
            e_in, dxc_next8 = carry
            ti = nt - 1 - s
            t0 = pl.multiple_of(ti * tt, tt)
            p0 = pl.multiple_of(jnp.maximum(t0 - 8, 0), 8)
            first = (ti > 0).astype(F32)
            xl = xl_ref[pl.ds(t0, tt), :]
            xs, xc = _lru_conv(xl, xl_ref[pl.ds(p0, 8), :] * first, cwv, cbv)
            r, ig, a, mult = (ref[pl.ds(t0, tt), :] for ref in (r_ref, i_ref, a_ref, m_ref))
            hv = h_ref[pl.ds(t0, tt), :]
            h_before = _shift_down(hv, h_ref[pl.ds(p0, 8), :] * first, 1)
            glv = gl_ref[pl.ds(t0, tt), :]
            dyv = dy_ref[pl.ds(t0, tt), :]
            dgl_ref[pl.ds(t0, tt), :] = (dyv * hv * _gelu_grad(glv)).astype(dgl_ref.dtype)
            dh = dyv * _gelu(glv)
            row = lax.broadcasted_iota(jnp.int32, a.shape, 0)
            coef = jnp.where(row == tt - 1, 1.0, pltpu.roll(a, tt - 1, 0))
            ga, gb = _group_scan(coef, dh, True)
            gin = e_in
            for g in reversed(range(tt // 8)):
                gg = ga[8 * g:8 * g + 8, :] * gin + gb[8 * g:8 * g + 8, :]
                g_ref[8 * g:8 * g + 8, :] = gg
                gin = gg[0:1, :]
            gv = g_ref[...]
            e_out = a[0:1, :] * gv[0:1, :]
            ix = ig * xc
            dla = (gv * h_before) * a - (gv * ix) * (a * a / mult)
            dlam_ref[...] += jnp.sum(dla * (LRU_C * r), axis=0, keepdims=True)
            dpr = (dla * (LRU_C * ls)) * (r * (1.0 - r))
            dpi = (gv * mult * xc) * (ig * (1.0 - ig))
            dbr_ref[...] += jnp.sum(dpr, axis=0, keepdims=True)
            dbi_ref[...] += jnp.sum(dpi, axis=0, keepdims=True)
            xcb, dprb, dpib = xc.astype(_MXU), dpr.astype(_MXU), dpi.astype(_MXU)
            dwr_ref[...] += _dot_tn(xcb, dprb)
            dwi_ref[...] += _dot_tn(xcb, dpib)
            dxc = gv * mult * ig + _dot_nt(dprb, wrv) + _dot_nt(dpib, wiv)
            dcb_ref[...] += jnp.sum(dxc, axis=0, keepdims=True)
            dxl = None
            for k in range(CONV_WIDTH):
                dcw_ref[k:k + 1, :] += jnp.sum(dxc * xs[k], axis=0, keepdims=True)
                term = _shift_up(dxc, dxc_next8, CONV_WIDTH - 1 - k) * cwv[k:k + 1, :]
                dxl = term if dxl is None else dxl + term
            dxl_ref[pl.ds(t0, tt), :] = dxl.astype(dxl_ref.dtype)
            return e_out, dxc[0:8, :]

        lax.fori_loop(0, nt, tile, (jnp.zeros((1, HEAD_DIM), F32), jnp.zeros((8, HEAD_DIM), F32)))
        dlam_ref[...] = dlam_ref[...] * (1.0 - jax.nn.sigmoid(lamv))

    cs = lambda off: pl.BlockSpec((t, HEAD_DIM), lambda n: (0, off + n))
    vs = pl.BlockSpec((1, HEAD_DIM), lambda n: (0, n))
    ws = pl.BlockSpec((None, HEAD_DIM, HEAD_DIM), lambda n: (n, 0, 0))
    cws = pl.BlockSpec((CONV_WIDTH, HEAD_DIM), lambda n: (0, n))
    w = n_blocks * HEAD_DIM
    vec = S((1, w), F32)
    mat = S((n_blocks, HEAD_DIM, HEAD_DIM), F32)
    return _call(
        comm, body, grid=(n_blocks,),
        in_specs=[cs(col0), cs(col0 + n_blocks)] + [cs(0)] * 6 + [cws, vs, ws, ws, vs],
        out_specs=[cs(0), cs(0), cws, vs, ws, vs, ws, vs, vs],
        out_shape=[S((t, w), _MXU), S((t, w), _MXU), S((CONV_WIDTH, w), F32), vec, mat, vec, mat, vec, vec],
        scratch_shapes=[pltpu.VMEM((tt, HEAD_DIM), F32)],
        compiler_params=_cp("parallel"), name=name)(proj, proj, h, *kept, dyl, cw, cb, wr, wi, lam)


class _NoExchange:
    def __init__(self, weights):
        self.weights, self.grads, self.packs = weights, {}, {}

    def weight(self, name):
        return self.weights[name]

    def in_proj(self, x, gain, bm):
        hn = _rms_fwd(x, gain, "rms1")
        return [hn, *_mm_nn(hn, self.weights["w_in"], bm=bm, bn=self.weights["w_in"].shape[2], name="in_proj", also=_MXU)]

    def conv_w(self):
        return self.weights["conv_w"]

    def carrier(self, call):
        return None

    def harvest(self, car):
        pass

    def alone(self, call):
        pass


def _local_step(x, target, norms, ex, cb, wr, br, wi, bi, lam, ga, gl):
    g_pre_mix, g_post_mix, g_pre_ffn, g_post_ffn = norms
    t, d = x.shape
    bm = min(t, 512)
    bt = min(t, 2048)

    def run(fn, name, *args, **kw):
        car = ex.carrier(name)
        out = fn(*args, name=name, comm=car, **kw)
        ex.harvest(car)
        return out

    hn1, proj, proj_mx = ex.in_proj(x, g_pre_mix, bm)
    win3, cw = ex.weight("w_in"), ex.conv_w()
    c = win3.shape[0]
    o = run(_attn_fwd, "attn_fwd", proj_mx, (proj.shape[1] - d) // 3 // HEAD_DIM)
    mix = 2 * o.shape[1]
    n_heads = n_blocks = o.shape[1] // HEAD_DIM
    h, yl, *kept = run(_lru_fwd, "lru_fwd", proj, 3 * n_heads, n_blocks, cw, cb, wr, br, wi, bi, lam)
    y = run(_outnorm_fwd, "outnorm_fwd", o, yl, ga, gl)
    wout = ex.weight("w_out")
    mixo = run(_mm_nn, "out_proj", y, wout[None], bm=bm, bn=d)
    x2, hn2 = run(_mid_fwd, "mid_fwd", x, mixo, g_post_mix, g_pre_ffn)
    wg3, wu3 = ex.weight("w_ffn_gate"), ex.weight("w_ffn_up")
    act_dgate, act_dup, act = run(_swiglu_fwd, "ffn_gate_up", hn2, wg3, wu3, bm=bm)
    ex.alone("gather_w_down")
    wd = ex.weight("w_ffn_down")
    ff = wd.shape[0]
    f = _mm_nn(act, wd[None], bm=bm, bn=d // 2, name="ffn_down")
    loss_cols, dout, df, dg_post_ffn = _final(f, x2, target, g_post_ffn, "final")

    dgate, dup = _swiglu_bwd(df, wd, act_dgate, act_dup, bm=bm, bo=ff // 4, name="ffn_down_bwd")
    ex.grads["w_ffn_down"] = _mm_tn(act, df, 1, bm=bt, bk=512, name="ffn_down_dw").reshape(c, ff // c, d)
    ex.grads["w_ffn_gate"] = run(_mm_tn, "ffn_gate_dw", hn2, dgate, c, bm=bt, bk=d // 2)
    ex.grads["w_ffn_up"] = run(_mm_tn, "ffn_up_dw", hn2, dup, c, bm=bt, bk=d // 2)
    dhn2_g = run(_mm_nt, "ffn_gate_dx", dgate, wg3, bm=bm, bo=d // 2, out_dtype=F32)
    dhn2_u = run(_mm_nt, "ffn_up_dx", dup, wu3, bm=bm, bo=d // 2, out_dtype=F32)
    dx2, dmix, dg_pre_ffn, dg_post_mix = run(_mid_bwd, "mid_bwd", dhn2_g, dhn2_u, dout, x2, mixo, g_pre_ffn, g_post_mix)
    dy = run(_mm_nt, "out_proj_dx", dmix, wout[None], bm=bm, bo=mix, out_dtype=F32)
    ex.grads["w_out"] = _mm_tn(y, dmix, 1, bm=bt, bk=mix // 4, name="out_proj_dw").reshape(c, mix // c, d)
    do, dyl, dga, dgl_norm = run(_outnorm_bwd, "outnorm_bwd", dy, o, yl, ga, gl)
    dxl, dglu, dcw, dcb, dwr, dbr, dwi, dbi, dlam = run(_lru_bwd, "lru_bwd", proj, 3 * n_heads, n_blocks, h, kept, dyl, cw, cb, wr, wi, lam)
    small = dict(post_mix_norm=dg_post_mix, pre_ffn_norm=dg_pre_ffn, post_ffn_norm=dg_post_ffn, conv_w=dcw, conv_b=dcb,
                 w_rgate=dwr, b_rgate=dbr, w_igate=dwi, b_igate=dbi, lru_lambda=dlam, attn_out_norm=dga, lru_out_norm=dgl_norm)
    ex.packs["early"] = _pack([small[n] for n in _SMALL_EARLY])
    dq, dk, dv = run(_attn_bwd, "attn_bwd", proj_mx, do, n_heads)
    dproj = jnp.concatenate([dq, dk, dv, dxl, dglu], axis=1)
    ex.grads["w_in"] = _mm_tn(hn1, dproj, c, bm=bt, bk=d // 2, name="in_proj_dw")
    ex.alone("grads_w_in_swap")
    dhn1 = run(_mm_nt, "in_proj_dx", dproj, win3, bm=bm, bo=d // 2, out_dtype=F32)
    grad_x, small["pre_mix_norm"] = run(_first_bwd, "first_bwd", dhn1, dx2, x, g_pre_mix)
    ex.packs["late"] = _pack([small["pre_mix_norm"], (0.5 / d) * jnp.sum(loss_cols, keepdims=True)])
    return loss_cols, grad_x, small


def _into_slot(wsh, slot, dtype, name):
    rows, n = wsh.shape
    rb = _row_block(rows, 256) if rows % 8 == 0 else rows

    def body(s_ref, w_ref, o_ref):
        o_ref[...] = w_ref[...].astype(o_ref.dtype)

    return pl.pallas_call(
        body,
        grid_spec=pltpu.PrefetchScalarGridSpec(
            num_scalar_prefetch=1, grid=(rows // rb,),
            in_specs=[pl.BlockSpec((rb, n), lambda i, s_ref: (i, 0))],
            out_specs=pl.BlockSpec((None, rb, n), lambda i, s_ref: (s_ref[0], i, 0))),
        out_shape=S((4, rows, n), dtype), compiler_params=_cp("parallel"), name=name)(slot, wsh)


class _Exchange:
    SCHEDULE = {
        "in_proj": [("stream", "w_in"), ("ici", "conv_w"), ("ici", "w_ffn_up", 0)],
        "attn_fwd": [("d2d", "w_ffn_up", 0), ("ici", "w_ffn_gate")],
        "lru_fwd": [("d2d", "w_ffn_gate"), ("ici", "w_out"), ("ici", "w_ffn_up", 1)],
        "outnorm_fwd": [("d2d", "w_out"), ("d2d", "w_ffn_up", 1)],
        "out_proj": [("ici", "w_ffn_up", 2), ("ici", "w_ffn_up", 3)],
        "mid_fwd": [("d2d", "w_ffn_up", 2), ("d2d", "w_ffn_up", 3)],
        "ffn_gate_up": [("ici", "w_ffn_down")],
        "gather_w_down": [("d2d", "w_ffn_down")],
        "ffn_gate_dw": [("swap", "w_ffn_down")],
        "ffn_up_dw": [("scatter", "w_ffn_down", 0), ("scatter", "w_ffn_down", 1), ("scatter", "w_ffn_down", 2), ("swap", "w_ffn_gate")],
        "ffn_gate_dx": [("scatter", "w_ffn_down", 3), ("scatter", "w_ffn_gate", 0), ("scatter", "w_ffn_gate", 1), ("swap", "w_ffn_up")],
        "ffn_up_dx": [("share", "w_ffn_down"), ("scatter", "w_ffn_gate", 2), ("scatter", "w_ffn_gate", 3), ("scatter", "w_ffn_up", 0)],
        "mid_bwd": [("share", "w_ffn_gate"), ("scatter", "w_ffn_up", 1), ("scatter", "w_ffn_up", 2)],
        "out_proj_dx": [("scatter", "w_ffn_up", 3)],
        "outnorm_bwd": [("share", "w_ffn_up"), ("swap", "w_out")],
        "lru_bwd": [("scatter", "w_out")],
        "attn_bwd": [("share", "w_out"), ("spread", "early")],
        "grads_w_in_swap": [("swap", "w_in")],
        "in_proj_dx": [("scatter", "w_in")],
        "grads_w_in_share": [("share", "w_in"), ("spread", "late")],
    }
    PIECES = 4

    def __init__(self, slots, place):
        self.buf, self.place = dict(slots), place
        self.grads, self.packs, self.swapped, self.part, self.scattered, self.full, self.spreaded = {}, {}, {}, {}, {}, {}, {}

    def weight(self, name):
        b = self.buf[name]
        return b.reshape(-1, b.shape[2]) if name in ("w_out", "w_ffn_down") else b

    def in_proj(self, x, gain, bm):
        car = self.carrier("in_proj")
        out = _in_proj_streamed(x, gain, car, car.streamed, self.place, bm=bm, name="in_proj")
        self.harvest(car)
        return out

    def conv_w(self):
        return jnp.transpose(self.buf["conv_w"], (1, 0, 2)).reshape(CONV_WIDTH, -1)

    def carrier(self, call):
        if call not in self.SCHEDULE:
            return None
        car = _Carrier()
        car.todo, slot = [], {}
        for kind, name, *piece in self.SCHEDULE[call]:
            if kind in ("ici", "d2d", "stream"):
                if name not in slot:
                    slot[name] = car.inplace(self.buf[name])
                    car.todo.append((self.buf, name, slot[name]))
            if kind == "stream":
                car.streamed = slot[name]
            elif kind in ("ici", "d2d"):
                size = self.buf[name].shape[1] // 2 // self.PIECES
                rows = (piece[0] * size, size) if piece else None
                if kind == "ici":
                    car.gather_ici(slot[name], rows, split=name != "conv_w")
                else:
                    car.gather_d2d(slot[name], rows)
            elif kind == "swap":
                g = self.grads[name]
                o = car.fresh((4, g.shape[1] // 2, g.shape[2]), F32)
                car.swap(car.read(g), o)
                car.todo.append((self.swapped, name, o))
            elif kind == "scatter":
                if name not in self.part:
                    self.part[name] = _add_own_half(self.grads[name], self.swapped[name], self.place[1:], "grads_add_" + name)
                p = self.part[name]
                key = ("scatter", name)
                if key not in slot:
                    slot[key] = (car.read(p), car.inplace(self.scattered[name]) if name in self.scattered else car.fresh(p.shape, p.dtype))
                    car.todo.append((self.scattered, name, slot[key][1]))
                size = p.shape[1] // self.PIECES
                car.scatter(*slot[key], (piece[0] * size, size) if piece else None)
            elif kind == "share":
                o = car.inplace(_sum_chips(self.part[name], self.scattered[name], self.place, "grads_sum_" + name))
                car.share(o)
                car.todo.append((self.full, name, o))
            else:
                o = car.fresh((8,) + self.packs[name].shape, F32)
                car.spread(car.read(self.packs[name]), o)
                car.todo.append((self.spreaded, name, o))
        return car

    def harvest(self, car):
        for state, name, o in (car.todo if car is not None else []):
            state[name] = car.results[o]

    def alone(self, call):
        car = self.carrier(call)
        car.run_alone(call)
        self.harvest(car)

    def small_sum(self, key):
        return _sum_devices(self.packs[key], self.spreaded[key], 2 * self.place[0:1] + self.place[1:], "grads_small_sum_" + key)


def _row_block(rows, cap):
    return max(b for b in range(8, cap + 1, 8) if rows % b == 0)


def _add_own_half(g, recv, core, name):
    _, rows, n = g.shape
    half = rows // 2
    rb = _row_block(half, 512)
    nb = half // rb

    def body(c_ref, g_ref, r_ref, o_ref):
        o_ref[...] = (g_ref[...] + r_ref[...]).astype(o_ref.dtype)

    return pl.pallas_call(
        body,
        grid_spec=pltpu.PrefetchScalarGridSpec(
            num_scalar_prefetch=1, grid=(4, nb),
            in_specs=[pl.BlockSpec((None, rb, n), lambda k, i, c_ref: (k, c_ref[0] * nb + i, 0)),
                      pl.BlockSpec((None, rb, n), lambda k, i, c_ref: (k, i, 0))],
            out_specs=pl.BlockSpec((None, rb, n), lambda k, i, c_ref: (k, i, 0))),
        out_shape=S((4, half, n), BF16), compiler_params=_cp("parallel", "parallel"), name=name)(core, g, recv)


def _sum_chips(part, recv, place, name):
    _, rows, n = part.shape
    rb = _row_block(rows, 64)
    nb = rows // rb

    def body(p_ref, own_ref, r0, r1, r2, r3, o_ref):
        own = own_ref[...].astype(F32)
        terms = [jnp.where(p_ref[0] == k, own, r[...].astype(F32)) for k, r in enumerate((r0, r1, r2, r3))]
        o_ref[...] = ((terms[0] + terms[1]) + terms[2]) + terms[3]

    def slot(k):
        return pl.BlockSpec((None, rb, n), lambda i, p_ref: (jnp.where(p_ref[0] == k, (k + 1) % 4, k), i, 0))

    return pl.pallas_call(
        body,
        grid_spec=pltpu.PrefetchScalarGridSpec(
            num_scalar_prefetch=1, grid=(nb,),
            in_specs=[pl.BlockSpec((None, rb, n), lambda i, p_ref: (p_ref[0], i, 0))] + [slot(k) for k in range(4)],
            out_specs=pl.BlockSpec((rb, n), lambda i, p_ref: (p_ref[1] * nb + i, 0))),
        out_shape=S((2 * rows, n), F32), compiler_params=_cp("parallel"), name=name)(place, part, recv, recv, recv, recv)


def _sum_devices(own, spread, me, name):
    rows = own.shape[0]

    def body(me_ref, own_ref, *refs):
        acc = None
        for k, r in enumerate(refs[:8]):
            term = jnp.where(me_ref[0] == k, own_ref[...], r[...])
            acc = term if acc is None else acc + term
        refs[8][...] = acc

    def slot(k):
        return pl.BlockSpec((None, rows, 128), lambda i, me_ref: (jnp.where(me_ref[0] == k, (k + 1) % 8, k), 0, 0))

    whole = pl.BlockSpec((rows, 128), lambda i, me_ref: (0, 0))
    return pl.pallas_call(
        body,
        grid_spec=pltpu.PrefetchScalarGridSpec(num_scalar_prefetch=1, grid=(1,), in_specs=[whole] + [slot(k) for k in range(8)],
                                               out_specs=whole),
        out_shape=S((rows, 128), F32), compiler_params=_cp("arbitrary"), name=name)(me, own, *[spread] * 8)


def _adamw(w, g, m, v, name, regive=False):
    rows, n = w.shape
    rb = rows if rows * n * 4 <= (1 << 21) else _row_block(rows, 256)
    c1 = 1.0 - ADAM_B1 ** ADAM_STEP
    c2 = 1.0 - ADAM_B2 ** ADAM_STEP

    def body(w_ref, g_ref, m_ref, v_ref, d_ref, nm_ref, nv_ref, *again):
        gv = g_ref[...]
        for ref in again:
            ref[...] = gv
        nm = ADAM_B1 * m_ref[...] + (1.0 - ADAM_B1) * gv
        nv = ADAM_B2 * v_ref[...] + (1.0 - ADAM_B2) * (gv * gv)
        nm_ref[...] = nm
        nv_ref[...] = nv
        d_ref[...] = -ADAM_LR * ((nm / c1) / (jnp.sqrt(nv / c2) + ADAM_EPS) + ADAM_WD * w_ref[...])

    bs = pl.BlockSpec((rb, n), lambda i: (i, 0))
    n_out = 4 if regive else 3
    return pl.pallas_call(body, grid=(rows // rb,), in_specs=[bs] * 4, out_specs=[bs] * n_out, out_shape=[S((rows, n), F32)] * n_out,
                          compiler_params=_cp("parallel"), name=name)(w, g, m, v)


_BIG = ("w_in", "w_out", "w_ffn_gate", "w_ffn_up", "w_ffn_down")
_SMALL = ("pre_mix_norm", "post_mix_norm", "pre_ffn_norm", "post_ffn_norm", "conv_w", "conv_b", "w_rgate", "b_rgate",
          "w_igate", "b_igate", "lru_lambda", "attn_out_norm", "lru_out_norm")
_SMALL_EARLY = _SMALL[1:]
_WEIGHTS = ("pre_mix_norm", "post_mix_norm", "pre_ffn_norm", "post_ffn_norm", "w_in", "conv_w", "conv_b", "w_rgate", "b_rgate",
            "w_igate", "b_igate", "lru_lambda", "attn_out_norm", "lru_out_norm", "w_out", "w_ffn_gate", "w_ffn_up", "w_ffn_down")


def _pack(arrays):
    flat = []
    for a in arrays:
        f = a.reshape(-1)
        flat.append(jnp.pad(f, (0, (-f.shape[0]) % 1024)))
    return jnp.concatenate(flat).reshape(-1, 128)


def _unpack(packed, shapes):
    out, pos = [], 0
    flat = packed.reshape(-1)
    for s in shapes:
        size = math.prod(s)
        out.append(flat[pos:pos + size].reshape(s))
        pos += size + (-size) % 1024
    return out


def kernel(x, pre_mix_norm, post_mix_norm, pre_ffn_norm, post_ffn_norm, w_in, conv_w, conv_b, w_rgate, b_rgate, w_igate, b_igate, lru_lambda, attn_out_norm, lru_out_norm, w_out, w_ffn_gate, w_ffn_up, w_ffn_down, loss_target, m_pre_mix_norm, m_post_mix_norm, m_pre_ffn_norm, m_post_ffn_norm, m_w_in, m_conv_w, m_conv_b, m_w_rgate, m_b_rgate, m_w_igate, m_b_igate, m_lru_lambda, m_attn_out_norm, m_lru_out_norm, m_w_out, m_w_ffn_gate, m_w_ffn_up, m_w_ffn_down, v_pre_mix_norm, v_post_mix_norm, v_pre_ffn_norm, v_post_ffn_norm, v_w_in, v_conv_w, v_conv_b, v_w_rgate, v_b_rgate, v_w_igate, v_b_igate, v_lru_lambda, v_attn_out_norm, v_lru_out_norm, v_w_out, v_w_ffn_gate, v_w_ffn_up, v_w_ffn_down):
    given = dict(locals())
    w = {n: given[n][0] for n in _WEIGHTS}
    m = {n: given["m_" + n][0] for n in _WEIGHTS}
    v = {n: given["v_" + n][0] for n in _WEIGHTS}
    xs, target = x[0], loss_target[0]
    d = xs.shape[1]
    chip = (2 * lax.axis_index("x") + lax.axis_index("y")).astype(jnp.int32)
    place = jnp.stack([chip, lax.axis_index("c").astype(jnp.int32)])

    slots = {n: _into_slot(w[n], place[0:1], _MXU, "slot_" + n) for n in _BIG}
    slots["conv_w"] = _into_slot(w["conv_w"], place[0:1], F32, "slot_conv_w")
    ex = _Exchange(slots, place)
    row = lambda a: a.reshape(1, -1)
    norms = tuple(row(w[n]) for n in ("pre_mix_norm", "post_mix_norm", "pre_ffn_norm", "post_ffn_norm"))

    loss_cols, grad_x, small = _local_step(
        xs, target, norms, ex, row(w["conv_b"]), w["w_rgate"], row(w["b_rgate"]),
        w["w_igate"], row(w["b_igate"]), row(w["lru_lambda"]), row(w["attn_out_norm"]), row(w["lru_out_norm"]))


    ex.alone("grads_w_in_share")
    reduced = {n: ex.full[n] for n in _BIG}
    early = _unpack(ex.small_sum("early"), [small[n].shape for n in _SMALL_EARLY])
    late = _unpack(ex.small_sum("late"), [small["pre_mix_norm"].shape, (1, 1)])
    loss = late[1][0, 0]
    for n, g in zip(_SMALL_EARLY + ("pre_mix_norm",), early + late[:1]):
        reduced[n] = g.reshape(w[n].shape) if n != "conv_w" else lax.dynamic_slice_in_dim(g, chip * w[n].shape[1], w[n].shape[1], axis=1)

    delta, new_m, new_v = {}, {}, {}
    for n in _BIG:
        delta[n], new_m[n], new_v[n], reduced[n] = _adamw(w[n], reduced[n], m[n], v[n], "adamw_" + n, regive=True)
    shapes = [w[n].shape for n in _SMALL]
    packed = _adamw(*[_pack([src[n] for n in _SMALL]) for src in (w, reduced, m, v)], "adamw_small")
    for out, p in zip((delta, new_m, new_v), packed):
        out.update(zip(_SMALL, _unpack(p, shapes)))

    lead = lambda a: a[None]
    return (loss, lead(grad_x), *[lead(reduced[n]) for n in _WEIGHTS], *[lead(delta[n]) for n in _WEIGHTS],
            *[lead(new_m[n]) for n in _WEIGHTS], *[lead(new_v[n]) for n in _WEIGHTS])
```

```python
import functools
import math

import jax
import jax.numpy as jnp
from jax import lax
from jax.experimental import pallas as pl
from jax.experimental.pallas import tpu as pltpu

F32 = jnp.float32
BF16 = jnp.bfloat16
_MXU = BF16
S = jax.ShapeDtypeStruct

RMS_EPS = 1e-6
HEAD_DIM = 128
CONV_WIDTH = 4
LRU_C = 8.0
ADAM_LR, ADAM_B1, ADAM_B2, ADAM_EPS, ADAM_WD, ADAM_STEP = 0.001, 0.9, 0.999, 1e-08, 0.01, 10
EXP_CUT = -105.0
VMEM_LIMIT = 60 * 1024 * 1024
ROW_TILE = 256
SEQ_TILE = 256
ATTN_BLOCK = 256
ATTN_HEADS = 2
MESH = pl.DeviceIdType.MESH


def _cp(*sem):
    return pltpu.CompilerParams(dimension_semantics=sem, vmem_limit_bytes=VMEM_LIMIT)


def _dot(a, b):
    return jnp.dot(a, b, preferred_element_type=F32)


def _dot_nt(a, b):
    return lax.dot_general(a, b, (((1,), (1,)), ((), ())), preferred_element_type=F32)


def _dot_tn(a, b):
    return lax.dot_general(a, b, (((0,), (0,)), ((), ())), preferred_element_type=F32)


def _rstd(v):
    return lax.rsqrt(jnp.mean(v * v, axis=-1, keepdims=True) + RMS_EPS)


def _rms_bwd(dn, vh, r, gain):
    dvh = dn * gain
    dv = r * (dvh - vh * jnp.mean(dvh * vh, axis=-1, keepdims=True))
    return dv, jnp.sum(dn * vh, axis=0, keepdims=True)


def _log_sigmoid(z):
    return jnp.minimum(z, 0.0) - jnp.log(1.0 + jnp.exp(-jnp.abs(z)))


def _expm1(v):
    small = v * (1.0 + v * (0.5 + v * (1.0 / 6.0 + v * (1.0 / 24.0 + v * (1.0 / 120.0)))))
    return jnp.where(jnp.abs(v) < 0.04, small, jnp.exp(v) - 1.0)


_GELU_C = math.sqrt(2.0 / math.pi)


def _gelu(v):
    return 0.5 * v * (1.0 + jnp.tanh(_GELU_C * (v + 0.044715 * v * v * v)))


def _gelu_grad(v):
    th = jnp.tanh(_GELU_C * (v + 0.044715 * v * v * v))
    return 0.5 * (1.0 + th) + 0.5 * v * (1.0 - th * th) * _GELU_C * (1.0 + 3.0 * 0.044715 * v * v)


def _row_spec(tm, d):
    return pl.BlockSpec((tm, d), lambda i: (i, 0))


def _vec_spec(d):
    return pl.BlockSpec((1, d), lambda i: (0, 0))


_ANY = pl.BlockSpec(memory_space=pl.ANY)


def _place():
    x, y, c = lax.axis_index("x"), lax.axis_index("y"), lax.axis_index("c")
    return x, y, c, [(1 - x, y), (x, 1 - y), (1 - x, 1 - y)]


def _remote(src, dst, send_sem, recv_sem, to):
    return pltpu.make_async_remote_copy(src_ref=src, dst_ref=dst, send_sem=send_sem, recv_sem=recv_sem,
                                        device_id=to, device_id_type=MESH)


class _Carrier:
    def __init__(self):
        self.inputs, self.out_shapes, self.aliases, self.ops, self.n_sems, self.results = [], [], {}, [], 0, None

    def inplace(self, arr):
        self.aliases[len(self.inputs)] = len(self.out_shapes)
        self.inputs.append(arr)
        self.out_shapes.append(S(arr.shape, arr.dtype))
        return len(self.out_shapes) - 1

    def read(self, arr):
        self.inputs.append(arr)
        return len(self.inputs) - 1

    def fresh(self, shape, dtype):
        self.out_shapes.append(S(shape, dtype))
        return len(self.out_shapes) - 1

    def _add(self, n_sems, copies):
        base = self.n_sems
        self.n_sems += n_sems

        def start(ins, outs, send, recv):
            for k, (src, dst, _, to) in enumerate(copies(ins, outs)):
                _remote(src, dst, send.at[base + k], recv.at[base + k], to).start()

        def finish(ins, outs, send, recv):
            for k, (src, _, land, to) in enumerate(copies(ins, outs)):
                _remote(src, land, send.at[base + k], recv.at[base + k], to).wait()

        self.ops.append((start, finish))

    def gather_ici(self, o, rows=None, split=True):
        half = self.out_shapes[o].shape[1] // 2
        lo, size = rows or (0, half)

        def copies(ins, outs):
            x, y, c, chips = _place()
            part = (lambda ref: ref.at[pl.ds(c * half + lo, size)]) if split else (lambda ref: ref)
            mine = part(outs[o].at[2 * x + y])
            return [(mine, mine, part(outs[o].at[2 * px + py]), (px, py, c)) for px, py in chips]

        self._add(3, copies)

    def gather_d2d(self, o, rows=None):
        half = self.out_shapes[o].shape[1] // 2
        lo, size = rows or (0, half)

        def copies(ins, outs):
            x, y, c, chips = _place()
            at = lambda k, cc: outs[o].at[k].at[pl.ds(cc * half + lo, size)]
            return [(at(2 * px + py, c), at(2 * px + py, c), at(2 * px + py, 1 - c), (x, y, 1 - c)) for px, py in chips]

        self._add(3, copies)

    def swap(self, i, o):
        half = self.inputs[i].shape[1] // 2

        def copies(ins, outs):
            x, y, c, _ = _place()
            return [(ins[i].at[:, pl.ds((1 - c) * half, half)], outs[o], outs[o], (x, y, 1 - c))]

        self._add(1, copies)

    def scatter(self, i, o, rows=None):
        lo, size = rows or (0, self.inputs[i].shape[1])

        def copies(ins, outs):
            x, y, c, chips = _place()
            cut = lambda ref: ref.at[pl.ds(lo, size)]
            return [(cut(ins[i].at[2 * px + py]), cut(outs[o].at[2 * x + y]), cut(outs[o].at[2 * px + py]), (px, py, c)) for px, py in chips]

        self._add(3, copies)

    def share(self, o):
        r = self.out_shapes[o].shape[0] // 2

        def copies(ins, outs):
            x, y, c, _ = _place()
            mine = outs[o].at[pl.ds(c * r, r)]
            return [(mine, mine, outs[o].at[pl.ds((1 - c) * r, r)], (x, y, 1 - c))]

        self._add(1, copies)

    def spread(self, i, o):
        def copies(ins, outs):
            x, y, c, _ = _place()
            me = 4 * x + 2 * y + c
            out = []
            for d in range(1, 8):
                to, frm = (me + d) % 8, (me + 8 - d) % 8
                out.append((ins[i], outs[o].at[me], outs[o].at[frm], (to // 4, (to // 2) % 2, to % 2)))
            return out

        self._add(7, copies)

    def _pallas(self, body, n_in, n_out, scratch, **kw):
        k_in, k_out = len(self.inputs), len(self.out_shapes)
        grid = kw.get("grid", ())

        def wrapped(*refs):
            ins, cins = refs[:n_in], refs[n_in:n_in + k_in]
            outs = refs[n_in + k_in:n_in + k_in + n_out]
            couts = refs[n_in + k_in + n_out:n_in + k_in + n_out + k_out]
            own = refs[n_in + k_in + n_out + k_out:]
            send, recv = own[len(scratch):]
            ids = [pl.program_id(a) for a in range(len(grid))]
            first = functools.reduce(jnp.logical_and, [a == 0 for a in ids], True)
            last = functools.reduce(jnp.logical_and, [a == g - 1 for a, g in zip(ids, grid)], True)

            def go(stage):
                for op in self.ops:
                    op[stage](cins, couts, send, recv)

            if grid:
                pl.when(first)(lambda: go(0))
                body(*ins, *outs, *own[:len(scratch)])
                pl.when(last)(lambda: go(1))
            else:
                go(0)
                go(1)

        sem = pltpu.SemaphoreType.DMA((self.n_sems,))
        return pl.pallas_call(
            wrapped, in_specs=list(kw.get("in_specs", [])) + [_ANY] * k_in, out_specs=list(kw.get("out_specs", [])) + [_ANY] * k_out,
            out_shape=list(kw.get("out_shape", [])) + self.out_shapes, scratch_shapes=list(scratch) + [sem, sem],
            input_output_aliases={**kw.get("aliases", {}), **{n_in + i: n_out + o for i, o in self.aliases.items()}}, name=kw["name"],
            **({"grid": grid, "compiler_params": _cp(*["arbitrary"] * len(grid))} if grid else {}))

    def run(self, body, kw, *args):
        single = not isinstance(kw["out_shape"], (list, tuple))
        out_shape = [kw["out_shape"]] if single else list(kw["out_shape"])
        out_specs = [kw["out_specs"]] if single else list(kw["out_specs"])
        res = self._pallas(body, len(args), len(out_shape), kw.get("scratch_shapes", []), grid=kw["grid"], in_specs=kw["in_specs"],
                           out_specs=out_specs, out_shape=out_shape, name=kw["name"],
                           aliases=kw.get("input_output_aliases", {}))(*args, *self.inputs)
        self.results = list(res[len(out_shape):])
        return res[0] if single else list(res[:len(out_shape)])

    def run_alone(self, name):
        self.results = list(self._pallas(None, 0, 0, [], name=name)(*self.inputs))


def _call(comm, body, **kw):
    if comm is None:
        return pl.pallas_call(body, **kw)
    return functools.partial(comm.run, body, kw)


def _in_proj_streamed(x, gain, car, o_w, place, *, bm, name):
    m, k = x.shape
    n = car.out_shapes[o_w].shape[2]
    ni, half = m // bm, k // 2
    k_in, k_out = len(car.inputs), len(car.out_shapes)
    order = lambda p: ((p & 1) << 1) | (p >> 1)

    def body(place_ref, x_ref, g_ref, *refs):
        cins, (hn_ref, o_ref, ob_ref), couts = refs[:k_in], refs[k_in:k_in + 3], refs[k_in + 3:k_in + 3 + k_out]
        wbuf, local, ici_send, ici_recv, d2d_send, d2d_recv, send, recv = refs[k_in + 3 + k_out:]
        p, i = pl.program_id(0), pl.program_id(1)
        x, y, c, chips = _place()
        me = 2 * x + y
        rows = lambda chunk, cc: couts[o_w].at[chunk].at[pl.ds(cc * half, half)]

        @pl.when(jnp.logical_and(p == 0, i == 0))
        def _():
            for j, (px, py) in enumerate(chips):
                _remote(rows(me, c), rows(me, c), ici_send.at[j], ici_recv.at[j], (px, py, c)).start()
            for op in car.ops:
                op[0](cins, couts, send, recv)

        for j, (px, py) in enumerate(chips):
            @pl.when(jnp.logical_and(p == j + 1, i == 0))
            def _(j=j, px=px, py=py):
                landed, other = rows(2 * px + py, c), rows(2 * px + py, 1 - c)
                _remote(landed, landed, ici_send.at[j], ici_recv.at[j], (px, py, c)).wait_recv()
                _remote(landed, landed, d2d_send.at[j], d2d_recv.at[j], (x, y, 1 - c)).start()
                _remote(other, other, d2d_send.at[j], d2d_recv.at[j], (x, y, 1 - c)).wait_recv()

        @pl.when(i == 0)
        def _():
            cp = pltpu.make_async_copy(couts[o_w].at[me ^ order(p)], wbuf, local.at[0])
            cp.start()
            cp.wait()

        xv = x_ref[...]
        hn = ((xv * _rstd(xv)) * g_ref[...]).astype(_MXU)
        hn_ref[...] = hn
        res = _dot(hn, wbuf[...])
        o_ref[...] = res
        ob_ref[...] = res.astype(ob_ref.dtype)

        @pl.when(jnp.logical_and(p == 3, i == ni - 1))
        def _():
            for j, (px, py) in enumerate(chips):
                _remote(rows(me, c), rows(me, c), ici_send.at[j], ici_recv.at[j], (px, py, c)).wait_send()
                _remote(rows(me, c), rows(me, c), d2d_send.at[j], d2d_recv.at[j], (x, y, 1 - c)).wait_send()
            for op in car.ops:
                op[1](cins, couts, send, recv)

    ospec = pl.BlockSpec((bm, n), lambda p, i, place_ref: (i, place_ref[0] ^ order(p)))
    rows = pl.BlockSpec((bm, k), lambda p, i, place_ref: (i, 0))
    three, sems = pltpu.SemaphoreType.DMA((3,)), pltpu.SemaphoreType.DMA((max(car.n_sems, 1),))
    res = pl.pallas_call(
        body,
        grid_spec=pltpu.PrefetchScalarGridSpec(
            num_scalar_prefetch=1, grid=(4, ni),
            in_specs=[rows, pl.BlockSpec((1, k), lambda p, i, place_ref: (0, 0))] + [_ANY] * k_in,
            out_specs=[pl.BlockSpec((bm, k), lambda p, i, place_ref: (p * ni + i, 0)), ospec, ospec] + [_ANY] * k_out,
            scratch_shapes=[pltpu.VMEM((k, n), _MXU), pltpu.SemaphoreType.DMA((1,)), three, three, three, three, sems, sems]),
        out_shape=[S((4 * m, k), _MXU), S((m, 4 * n), F32), S((m, 4 * n), _MXU)] + car.out_shapes,
        input_output_aliases={3 + a: 3 + o for a, o in car.aliases.items()},
        compiler_params=_cp("arbitrary", "arbitrary"), name=name)(place, x, gain, *car.inputs)
    car.results = list(res[3:])
    return res[0], res[1], res[2]


def _mm_nn(a, b3, *, bm, bn, name, also=None, comm=None):
    m, k = a.shape
    c, _, n = b3.shape
    ni, nj = m // bm, n // bn

    def body(a_ref, b_ref, *o_refs):
        res = _dot(a_ref[...], b_ref[...])
        for o_ref in o_refs:
            o_ref[...] = res.astype(o_ref.dtype)

    ospec = pl.BlockSpec((bm, bn), lambda cc, j, i: (i, cc * nj + j))
    dtypes = [F32] + ([] if also is None else [also])
    out = _call(
        comm, body, grid=(c, nj, ni),
        in_specs=[pl.BlockSpec((bm, k), lambda cc, j, i: (i, 0)), pl.BlockSpec((None, k, bn), lambda cc, j, i: (cc, 0, j))],
        out_specs=[ospec] * len(dtypes), out_shape=[S((m, c * n), dt) for dt in dtypes],
        compiler_params=_cp("parallel", "parallel", "parallel"), name=name)(a, b3)
    return out[0] if also is None else out


def _mm_nt(a, b3, *, bm, bo, out_dtype, name, comm=None):
    m = a.shape[0]
    c, ko, n = b3.shape
    ni, nj = m // bm, ko // bo

    def body(a_ref, b_ref, o_ref):
        acc = _dot_nt(a_ref[:, 0:n], b_ref[0])
        for cc in range(1, c):
            acc = acc + _dot_nt(a_ref[:, cc * n:(cc + 1) * n], b_ref[cc])
        o_ref[...] = acc.astype(o_ref.dtype)

    return _call(
        comm, body, grid=(nj, ni),
        in_specs=[pl.BlockSpec((bm, c * n), lambda j, i: (i, 0)),
                  pl.BlockSpec((c, bo, n), lambda j, i: (0, j, 0))],
        out_specs=pl.BlockSpec((bm, bo), lambda j, i: (i, j)),
        out_shape=S((m, ko), out_dtype),
        compiler_params=_cp("parallel", "parallel"), name=name)(a, b3)


def _mm_tn(a, b, c, *, bm, bk, name, comm=None):
    m, k = b.shape[0], a.shape[1]
    n = b.shape[1] // c
    nm, nk = m // bm, k // bk

    def body(a_ref, b_ref, o_ref, acc):
        mm = pl.program_id(2)

        @pl.when(mm == 0)
        def _():
            acc[...] = jnp.zeros_like(acc)

        acc[...] += _dot_tn(a_ref[...], b_ref[...])

        @pl.when(mm == nm - 1)
        def _():
            o_ref[...] = acc[...]

    return _call(
        comm, body, grid=(c, nk, nm),
        in_specs=[pl.BlockSpec((bm, bk), lambda cc, j, mm: (mm, j)),
                  pl.BlockSpec((bm, n), lambda cc, j, mm: (mm, cc))],
        out_specs=pl.BlockSpec((None, bk, n), lambda cc, j, mm: (cc, j, 0)),
        out_shape=S((c, k, n), F32),
        scratch_shapes=[pltpu.VMEM((bk, n), F32)],
        compiler_params=_cp("parallel", "parallel", "arbitrary"), name=name)(a, b)


def _swiglu_fwd(hn, wg3, wu3, *, bm, name, comm=None):
    m, k = hn.shape
    c, _, n = wg3.shape

    def body(a_ref, g_ref, u_ref, dgate_ref, dup_ref, act_ref):
        a = a_ref[...]
        gate = _dot(a, g_ref[...])
        up = _dot(a, u_ref[...])
        sg = jax.nn.sigmoid(gate)
        silu = gate * sg
        dgate_ref[...] = (up * (sg * (1.0 + gate * (1.0 - sg)))).astype(dgate_ref.dtype)
        dup_ref[...] = silu.astype(dup_ref.dtype)
        act_ref[...] = (silu * up).astype(act_ref.dtype)

    wspec = pl.BlockSpec((None, k, n), lambda cc, i: (cc, 0, 0))
    ospec = pl.BlockSpec((bm, n), lambda cc, i: (i, cc))
    return _call(
        comm, body, grid=(c, m // bm),
        in_specs=[pl.BlockSpec((bm, k), lambda cc, i: (i, 0)), wspec, wspec],
        out_specs=[ospec, ospec, ospec],
        out_shape=[S((m, c * n), _MXU), S((m, c * n), _MXU), S((m, c * n), _MXU)],
        compiler_params=_cp("parallel", "parallel"), name=name)(hn, wg3, wu3)


def _swiglu_bwd(df, wd, act_dgate, act_dup, *, bm, bo, name):
    m, k = df.shape
    ko = wd.shape[0]

    def body(a_ref, b_ref, g_ref, u_ref, dg_ref, du_ref):
        dact = _dot_nt(a_ref[...], b_ref[...])
        dg_ref[...] = (dact * g_ref[...].astype(F32)).astype(dg_ref.dtype)
        du_ref[...] = (dact * u_ref[...].astype(F32)).astype(du_ref.dtype)

    ospec = pl.BlockSpec((bm, bo), lambda j, i: (i, j))
    return pl.pallas_call(
        body, grid=(ko // bo, m // bm),
        in_specs=[pl.BlockSpec((bm, k), lambda j, i: (i, 0)), pl.BlockSpec((bo, k), lambda j, i: (j, 0)), ospec, ospec],
        out_specs=[ospec, ospec],
        out_shape=[S((m, ko), _MXU), S((m, ko), _MXU)],
        compiler_params=_cp("parallel", "parallel"), name=name)(df, wd, act_dgate, act_dup)


def _rms_fwd(x, gain, name, comm=None):
    t, d = x.shape
    tm = min(t, ROW_TILE)

    def body(x_ref, g_ref, o_ref):
        xv = x_ref[...]
        o_ref[...] = ((xv * _rstd(xv)) * g_ref[...]).astype(o_ref.dtype)

    return _call(comm, body, grid=(t // tm,), in_specs=[_row_spec(tm, d), _vec_spec(d)], out_specs=_row_spec(tm, d),
                          out_shape=S((t, d), _MXU), compiler_params=_cp("parallel"), name=name)(x, gain)


def _outnorm_fwd(o, yl, ga, gl, name, comm=None):
    t, w = o.shape
    tm = min(t, ROW_TILE)

    def body(o_ref, l_ref, ga_ref, gl_ref, y_ref):
        ov, lv = o_ref[...], l_ref[...]
        y_ref[:, :w] = ((ov * _rstd(ov)) * ga_ref[...]).astype(y_ref.dtype)
        y_ref[:, w:] = ((lv * _rstd(lv)) * gl_ref[...]).astype(y_ref.dtype)

    return _call(comm, body, grid=(t // tm,), in_specs=[_row_spec(tm, w), _row_spec(tm, w), _vec_spec(w), _vec_spec(w)],
                 out_specs=_row_spec(tm, 2 * w), out_shape=S((t, 2 * w), _MXU),
                 compiler_params=_cp("parallel"), name=name)(o, yl, ga, gl)


def _mid_fwd(x, mix, g_post, g_pre, name, comm=None):
    t, d = x.shape
    tm = min(t, ROW_TILE)

    def body(x_ref, m_ref, gp_ref, gn_ref, x2_ref, hn_ref):
        mv = m_ref[...]
        x2 = x_ref[...] + (mv * _rstd(mv)) * gp_ref[...]
        x2_ref[...] = x2
        hn_ref[...] = ((x2 * _rstd(x2)) * gn_ref[...]).astype(hn_ref.dtype)

    return _call(comm, body, grid=(t // tm,), in_specs=[_row_spec(tm, d), _row_spec(tm, d), _vec_spec(d), _vec_spec(d)],
                          out_specs=[_row_spec(tm, d), _row_spec(tm, d)], out_shape=[S((t, d), F32), S((t, d), _MXU)],
                          compiler_params=_cp("parallel"), name=name)(x, mix, g_post, g_pre)


def _final(f, x2, target, g_post, name):
    t, d = f.shape
    tm = min(t, ROW_TILE)

    def body(f_ref, x2_ref, t_ref, g_ref, loss_ref, dout_ref, df_ref, dg_ref):
        @pl.when(pl.program_id(0) == 0)
        def _():
            loss_ref[...] = jnp.zeros_like(loss_ref)
            dg_ref[...] = jnp.zeros_like(dg_ref)

        fv = f_ref[...]
        r = _rstd(fv)
        fh = fv * r
        err = (x2_ref[...] + fh * g_ref[...]) - t_ref[...]
        loss_ref[...] += jnp.sum(err * err, axis=0, keepdims=True)
        dout = err * (1.0 / d)
        dout_ref[...] = dout
        dfv, dg = _rms_bwd(dout, fh, r, g_ref[...])
        df_ref[...] = dfv.astype(df_ref.dtype)
        dg_ref[...] += dg

    return pl.pallas_call(
        body, grid=(t // tm,),
        in_specs=[_row_spec(tm, d), _row_spec(tm, d), _row_spec(tm, d), _vec_spec(d)],
        out_specs=[_vec_spec(d), _row_spec(tm, d), _row_spec(tm, d), _vec_spec(d)],
        out_shape=[S((1, d), F32), S((t, d), F32), S((t, d), _MXU), S((1, d), F32)],
        compiler_params=_cp("arbitrary"), name=name)(f, x2, target, g_post)


def _mid_bwd(dhn_a, dhn_b, dout, x2, mix, g_pre, g_post, name, comm=None):
    t, d = x2.shape
    tm = min(t, ROW_TILE)

    def body(da_ref, db_ref, do_ref, x2_ref, m_ref, gn_ref, gp_ref, dx2_ref, dm_ref, dgn_ref, dgp_ref):
        @pl.when(pl.program_id(0) == 0)
        def _():
            dgn_ref[...] = jnp.zeros_like(dgn_ref)
            dgp_ref[...] = jnp.zeros_like(dgp_ref)

        x2 = x2_ref[...]
        r = _rstd(x2)
        dxa, dgn = _rms_bwd(da_ref[...] + db_ref[...], x2 * r, r, gn_ref[...])
        dx2 = do_ref[...] + dxa
        dx2_ref[...] = dx2
        dgn_ref[...] += dgn
        mv = m_ref[...]
        rm = _rstd(mv)
        dmv, dgp = _rms_bwd(dx2, mv * rm, rm, gp_ref[...])
        dm_ref[...] = dmv.astype(dm_ref.dtype)
        dgp_ref[...] += dgp

    rs, vs = _row_spec(tm, d), _vec_spec(d)
    return _call(
        comm, body, grid=(t // tm,), in_specs=[rs, rs, rs, rs, rs, vs, vs], out_specs=[rs, rs, vs, vs],
        out_shape=[S((t, d), F32), S((t, d), _MXU), S((1, d), F32), S((1, d), F32)],
        compiler_params=_cp("arbitrary"), name=name)(dhn_a, dhn_b, dout, x2, mix, g_pre, g_post)


def _first_bwd(dhn, dx2, x, gain, name, comm=None):
    t, d = x.shape
    tm = min(t, ROW_TILE)

    def body(dh_ref, dx2_ref, x_ref, g_ref, dx_ref, dg_ref):
        @pl.when(pl.program_id(0) == 0)
        def _():
            dg_ref[...] = jnp.zeros_like(dg_ref)

        xv = x_ref[...]
        r = _rstd(xv)
        dxa, dg = _rms_bwd(dh_ref[...], xv * r, r, g_ref[...])
        dx_ref[...] = dx2_ref[...] + dxa
        dg_ref[...] += dg

    rs, vs = _row_spec(tm, d), _vec_spec(d)
    return _call(comm, body, grid=(t // tm,), in_specs=[rs, rs, rs, vs], out_specs=[rs, vs],
                          out_shape=[S((t, d), F32), S((1, d), F32)], compiler_params=_cp("arbitrary"), name=name)(dhn, dx2, x, gain)


def _outnorm_bwd(dy, o, yl, ga, gl, name, comm=None):
    t, w = o.shape
    tm = min(t, ROW_TILE)

    def body(dy_ref, o_ref, l_ref, ga_ref, gl_ref, do_ref, dl_ref, dga_ref, dgl_ref):
        @pl.when(pl.program_id(0) == 0)
        def _():
            dga_ref[...] = jnp.zeros_like(dga_ref)
            dgl_ref[...] = jnp.zeros_like(dgl_ref)

        ov, lv = o_ref[...], l_ref[...]
        ra, rl = _rstd(ov), _rstd(lv)
        dov, dga = _rms_bwd(dy_ref[:, :w], ov * ra, ra, ga_ref[...])
        dlv, dgl = _rms_bwd(dy_ref[:, w:], lv * rl, rl, gl_ref[...])
        do_ref[...] = dov.astype(do_ref.dtype)
        dl_ref[...] = dlv
        dga_ref[...] += dga
        dgl_ref[...] += dgl

    rs, vs = _row_spec(tm, w), _vec_spec(w)
    return _call(comm, body, grid=(t // tm,), in_specs=[_row_spec(tm, 2 * w), rs, rs, vs, vs], out_specs=[rs, rs, vs, vs],
                          out_shape=[S((t, w), _MXU), S((t, w), F32), S((1, w), F32), S((1, w), F32)],
                          compiler_params=_cp("arbitrary"), name=name)(dy, o, yl, ga, gl)


def _tri_sum(v, tri):
    return _dot(v.astype(_MXU), tri)


def _attn_tile(qb, kb, row, col, shift, scale):
    z = _dot_nt(qb, kb) * scale
    mask = (col + shift) < row
    lb = _log_sigmoid(z)
    lm = jnp.where(mask, lb - z, 0.0)
    return mask, lb, lm


def _attn_fwd(proj, n_heads, name, comm=None):
    t = proj.shape[0]
    bq = min(t, ATTN_BLOCK)
    nq = t // bq
    scale = 1.0 / math.sqrt(HEAD_DIM)

    heads = [slice(a * HEAD_DIM, (a + 1) * HEAD_DIM) for a in range(ATTN_HEADS)]

    def body(q_ref, k_ref, v_ref, o_ref):
        row = lax.broadcasted_iota(jnp.int32, (bq, bq), 0)
        col = lax.broadcasted_iota(jnp.int32, (bq, bq), 1)
        tri = (row > col).astype(_MXU)

        def per_q(qi, _):
            q0 = pl.multiple_of(qi * bq, bq)
            qbs = [q_ref[pl.ds(q0, bq), hd] for hd in heads]

            def cond(st):
                return jnp.logical_and(st[0] >= 0, st[1])

            def step(st):
                kj, _, carries, accs = st
                k0 = pl.multiple_of(kj * bq, bq)
                alive, new_carries, new_accs = None, [], []
                for hd, qb, carry, acc in zip(heads, qbs, carries, accs):
                    mask, lb, lm = _attn_tile(qb, k_ref[pl.ds(k0, bq), hd], row, col, (kj - qi) * bq, scale)
                    w = jnp.where(mask, jnp.exp(lb + _tri_sum(lm, tri) + carry), 0.0)
                    new_accs.append(acc + _dot(w.astype(_MXU), v_ref[pl.ds(k0, bq), hd]))
                    carry = carry + jnp.sum(lm, axis=1, keepdims=True)
                    new_carries.append(carry)
                    live = jnp.max(carry) > EXP_CUT
                    alive = live if alive is None else jnp.logical_or(alive, live)
                return kj - 1, alive, tuple(new_carries), tuple(new_accs)

            st = lax.while_loop(cond, step, (qi, jnp.bool_(True), (jnp.zeros((bq, 1), F32),) * ATTN_HEADS,
                                             (jnp.zeros((bq, HEAD_DIM), F32),) * ATTN_HEADS))
            for hd, acc in zip(heads, st[3]):
                o_ref[pl.ds(q0, bq), hd] = acc
            return 0

        lax.fori_loop(0, nq, per_q, 0)

    groups = n_heads // ATTN_HEADS
    hs = lambda off: pl.BlockSpec((t, ATTN_HEADS * HEAD_DIM), lambda h: (0, off + h))
    return _call(
        comm, body, grid=(groups,), in_specs=[hs(0), hs(groups), hs(2 * groups)], out_specs=hs(0),
        out_shape=S((t, n_heads * HEAD_DIM), F32), compiler_params=_cp("parallel"), name=name)(proj, proj, proj)


def _emit(blocks, out_ref, starts, sems):
    copies = [pltpu.make_async_copy(b, out_ref.at[:, pl.ds(c0, b.shape[1])], sems.at[k]) for k, (b, c0) in enumerate(zip(blocks, starts))]
    for cp in copies:
        cp.start()
    for cp in copies:
        cp.wait()


def _attn_bwd(proj, do, dproj, n_heads, name, comm=None):
    t = proj.shape[0]
    bq = min(t, ATTN_BLOCK)
    nq = t // bq
    scale = 1.0 / math.sqrt(HEAD_DIM)
    groups = n_heads // ATTN_HEADS
    wide = ATTN_HEADS * HEAD_DIM

    heads = [slice(a * HEAD_DIM, (a + 1) * HEAD_DIM) for a in range(ATTN_HEADS)]

    def body(q_ref, k_ref, v_ref, do_ref, _, dproj_ref, dka_ref, dva_ref, g_ref, b_ref, dq_ref, dk_ref, dv_ref, out_sems):
        group = pl.program_id(0)
        dka_ref[...] = jnp.zeros_like(dka_ref)
        dva_ref[...] = jnp.zeros_like(dva_ref)
        row = lax.broadcasted_iota(jnp.int32, (bq, bq), 0)
        col = lax.broadcasted_iota(jnp.int32, (bq, bq), 1)
        tri = (row > col).astype(_MXU)
        tri_lt = (row < col).astype(_MXU)

        def per_q(qi, _):
            q0 = pl.multiple_of(qi * bq, bq)
            qbs = [q_ref[pl.ds(q0, bq), hd] for hd in heads]
            dobs = [do_ref[pl.ds(q0, bq), hd] for hd in heads]

            def cond(st):
                return jnp.logical_and(st[0] >= 0, st[1])

            def step(st):
                kj, _, carries = st
                k0 = pl.multiple_of(kj * bq, bq)
                alive, new_carries = None, []
                for a, (hd, qb, dob, carry) in enumerate(zip(heads, qbs, dobs, carries)):
                    mask, lb, lm = _attn_tile(qb, k_ref[pl.ds(k0, bq), hd], row, col, (kj - qi) * bq, scale)
                    w = jnp.where(mask, jnp.exp(lb + _tri_sum(lm, tri) + carry), 0.0)
                    g_ref[a, pl.ds(k0, bq), :] = w * _dot_nt(dob, v_ref[pl.ds(k0, bq), hd])
                    b_ref[a, pl.ds(k0, bq), :] = jnp.where(mask, jnp.exp(lb), 0.0)
                    dva_ref[pl.ds(k0, bq), hd] += _dot_tn(w.astype(_MXU), dob)
                    carry = carry + jnp.sum(lm, axis=1, keepdims=True)
                    new_carries.append(carry)
                    live = jnp.max(carry) > EXP_CUT
                    alive = live if alive is None else jnp.logical_or(alive, live)
                return kj - 1, alive, tuple(new_carries)

            st = lax.while_loop(cond, step, (qi, jnp.bool_(True), (jnp.zeros((bq, 1), F32),) * ATTN_HEADS))

            def back(kj, st2):
                k0 = pl.multiple_of(kj * bq, bq)
                out = []
                for a, (hd, qb, (before, dq)) in enumerate(zip(heads, qbs, st2)):
                    g = g_ref[a, pl.ds(k0, bq), :]
                    beta = b_ref[a, pl.ds(k0, bq), :]
                    dz = ((g * (1.0 - beta) - (before + _tri_sum(g, tri_lt)) * beta) * scale).astype(_MXU)
                    dka_ref[pl.ds(k0, bq), hd] += _dot_tn(dz, qb)
                    out.append((before + jnp.sum(g, axis=1, keepdims=True), dq + _dot(dz, k_ref[pl.ds(k0, bq), hd])))
                return tuple(out)

            st2 = lax.fori_loop(st[0] + 1, qi + 1, back, ((jnp.zeros((bq, 1), F32), jnp.zeros((bq, HEAD_DIM), F32)),) * ATTN_HEADS)
            for hd, (_, dq) in zip(heads, st2):
                dq_ref[pl.ds(q0, bq), hd] = dq.astype(dq_ref.dtype)
            return 0

        lax.fori_loop(0, nq, per_q, 0)
        dk_ref[...] = dka_ref[...].astype(dk_ref.dtype)
        dv_ref[...] = dva_ref[...].astype(dv_ref.dtype)
        _emit([dq_ref, dk_ref, dv_ref], dproj_ref, [(a * groups + group) * wide for a in range(3)], out_sems)

    hs = lambda off: pl.BlockSpec((t, wide), lambda h: (0, off + h))
    return _call(
        comm, body, grid=(groups,), in_specs=[hs(0), hs(groups), hs(2 * groups), hs(0), _ANY], out_specs=_ANY,
        out_shape=S(dproj.shape, dproj.dtype), input_output_aliases={4: 0},
        scratch_shapes=[pltpu.VMEM((t, wide), F32), pltpu.VMEM((t, wide), F32),
                        pltpu.VMEM((ATTN_HEADS, t, bq), F32), pltpu.VMEM((ATTN_HEADS, t, bq), F32)]
        + [pltpu.VMEM((t, wide), dproj.dtype)] * 3 + [pltpu.SemaphoreType.DMA((3,))],
        compiler_params=_cp("parallel"), name=name)(proj, proj, proj, do, dproj)


def _shift_down(cur, prev8, k):
    if k == 0:
        return cur
    row8 = lax.broadcasted_iota(jnp.int32, prev8.shape, 0)
    rc = pltpu.roll(cur, k, 0)
    top = jnp.where(row8 < k, pltpu.roll(prev8, k, 0), rc[0:8, :])
    return jnp.concatenate([top, rc[8:, :]], axis=0)


def _shift_up(cur, next8, k):
    if k == 0:
        return cur
    n = cur.shape[0]
    row8 = lax.broadcasted_iota(jnp.int32, next8.shape, 0)
    rc = pltpu.roll(cur, n - k, 0)
    bottom = jnp.where(row8 >= 8 - k, pltpu.roll(next8, 8 - k, 0), rc[n - 8:, :])
    return jnp.concatenate([rc[:n - 8, :], bottom], axis=0)


def _lru_conv(xl, prev8, cw, cb):
    xs = [_shift_down(xl, prev8, CONV_WIDTH - 1 - k) for k in range(CONV_WIDTH)]
    xc = xs[0] * cw[0:1, :]
    for k in range(1, CONV_WIDTH):
        xc = xc + xs[k] * cw[k:k + 1, :]
    return xs, xc + cb


def _lru_gates(xl, prev8, cw, cb, wr, br, wi, bi, ls):
    xs, xc = _lru_conv(xl, prev8, cw, cb)
    xcb = xc.astype(_MXU)
    r = jax.nn.sigmoid(_dot(xcb, wr) + br)
    i = jax.nn.sigmoid(_dot(xcb, wi) + bi)
    la = (LRU_C * r) * ls
    a = jnp.exp(la)
    mult = jnp.sqrt(-_expm1(2.0 * la))
    return xs, xc, r, i, a, mult


def _group_scan(a, b, reverse):
    n = a.shape[0]
    row = lax.broadcasted_iota(jnp.int32, a.shape, 0) % 8
    for d in (1, 2, 4):
        if reverse:
            m = row < 8 - d
            a_s, b_s = pltpu.roll(a, n - d, 0), pltpu.roll(b, n - d, 0)
        else:
            m = row >= d
            a_s, b_s = pltpu.roll(a, d, 0), pltpu.roll(b, d, 0)
        b = jnp.where(m, a * b_s + b, b)
        a = jnp.where(m, a * a_s, a)
    return a, b


def _lru_fwd(proj, col0, n_blocks, cw, cb, wr, br, wi, bi, lam, name, comm=None):
    t = proj.shape[0]
    tt = min(t, SEQ_TILE)
    nt = t // tt

    def body(xl_ref, gl_ref, cw_ref, cb_ref, wr_ref, br_ref, wi_ref, bi_ref, lam_ref, h_ref, y_ref, *kept):
        cwv, cbv, brv, biv = cw_ref[...], cb_ref[...], br_ref[...], bi_ref[...]
        wrv, wiv = wr_ref[...].astype(_MXU), wi_ref[...].astype(_MXU)
        ls = _log_sigmoid(lam_ref[...])

        def tile(ti, hin):
            t0 = pl.multiple_of(ti * tt, tt)
            p0 = pl.multiple_of(jnp.maximum(t0 - 8, 0), 8)
            prev8 = xl_ref[pl.ds(p0, 8), :] * (ti > 0).astype(F32)
            xl = xl_ref[pl.ds(t0, tt), :]
            _, xc, r, ig, a, mult = _lru_gates(xl, prev8, cwv, cbv, wrv, brv, wiv, biv, ls)
            for ref, val in zip(kept, (r, ig, a, mult)):
                ref[pl.ds(t0, tt), :] = val
            ga, gb = _group_scan(a, mult * (ig * xc), False)
            for g in range(tt // 8):
                hg = ga[8 * g:8 * g + 8, :] * hin + gb[8 * g:8 * g + 8, :]
                h_ref[pl.ds(t0 + 8 * g, 8), :] = hg
                hin = hg[7:8, :]
            y_ref[pl.ds(t0, tt), :] = h_ref[pl.ds(t0, tt), :] * _gelu(gl_ref[pl.ds(t0, tt), :])
            return hin

        lax.fori_loop(0, nt, tile, jnp.zeros((1, HEAD_DIM), F32))

    cs = lambda off: pl.BlockSpec((t, HEAD_DIM), lambda n: (0, off + n))
    vs = pl.BlockSpec((1, HEAD_DIM), lambda n: (0, n))
    ws = pl.BlockSpec((None, HEAD_DIM, HEAD_DIM), lambda n: (n, 0, 0))
    w = n_blocks * HEAD_DIM
    return _call(
        comm, body, grid=(n_blocks,),
        in_specs=[cs(col0), cs(col0 + n_blocks), pl.BlockSpec((CONV_WIDTH, HEAD_DIM), lambda n: (0, n)), vs, ws, vs, ws, vs, vs],
        out_specs=[cs(0)] * 6, out_shape=[S((t, w), F32)] * 6,
        compiler_params=_cp("parallel"), name=name)(proj, proj, cw, cb, wr, br, wi, bi, lam)


def _lru_bwd(proj, col0, n_blocks, h, kept, dyl, cw, cb, wr, wi, lam, name, comm=None):
    t = proj.shape[0]
    tt = min(t, SEQ_TILE)
    nt = t // tt

    def body(xl_ref, gl_ref, h_ref, r_ref, i_ref, a_ref, m_ref, dy_ref, cw_ref, cb_ref, wr_ref, wi_ref, lam_ref,
             dproj_ref, dcw_ref, dcb_ref, dwr_ref, dbr_ref, dwi_ref, dbi_ref, dlam_ref, g_ref, dxl_ref, dgl_ref, out_sems):
        block = pl.program_id(0)
        cwv, cbv = cw_ref[...], cb_ref[...]
        wrv, wiv = wr_ref[...].astype(_MXU), wi_ref[...].astype(_MXU)
        lamv = lam_ref[...]
        ls = _log_sigmoid(lamv)
        for ref in (dcw_ref, dcb_ref, dwr_ref, dbr_ref, dwi_ref, dbi_ref, dlam_ref):
            ref[...] = jnp.zeros_like(ref)

        def tile(s, carry):
            e_in, dxc_next8 = carry
            ti = nt - 1 - s
            t0 = pl.multiple_of(ti * tt, tt)
            p0 = pl.multiple_of(jnp.maximum(t0 - 8, 0), 8)
            first = (ti > 0).astype(F32)
            xl = xl_ref[pl.ds(t0, tt), :]
            xs, xc = _lru_conv(xl, xl_ref[pl.ds(p0, 8), :] * first, cwv, cbv)
            r, ig, a, mult = (ref[pl.ds(t0, tt), :] for ref in (r_ref, i_ref, a_ref, m_ref))
            hv = h_ref[pl.ds(t0, tt), :]
            h_before = _shift_down(hv, h_ref[pl.ds(p0, 8), :] * first, 1)
            glv = gl_ref[pl.ds(t0, tt), :]
            dyv = dy_ref[pl.ds(t0, tt), :]
            dgl_ref[pl.ds(t0, tt), :] = (dyv * hv * _gelu_grad(glv)).astype(dgl_ref.dtype)
            dh = dyv * _gelu(glv)
            row = lax.broadcasted_iota(jnp.int32, a.shape, 0)
            coef = jnp.where(row == tt - 1, 1.0, pltpu.roll(a, tt - 1, 0))
            ga, gb = _group_scan(coef, dh, True)
            gin = e_in
            for g in reversed(range(tt // 8)):
                gg = ga[8 * g:8 * g + 8, :] * gin + gb[8 * g:8 * g + 8, :]
                g_ref[8 * g:8 * g + 8, :] = gg
                gin = gg[0:1, :]
            gv = g_ref[...]
            e_out = a[0:1, :] * gv[0:1, :]
            ix = ig * xc
            dla = (gv * h_before) * a - (gv * ix) * (a * a / mult)
            dlam_ref[...] += jnp.sum(dla * (LRU_C * r), axis=0, keepdims=True)
            dpr = (dla * (LRU_C * ls)) * (r * (1.0 - r))
            dpi = (gv * mult * xc) * (ig * (1.0 - ig))
            dbr_ref[...] += jnp.sum(dpr, axis=0, keepdims=True)
            dbi_ref[...] += jnp.sum(dpi, axis=0, keepdims=True)
            xcb, dprb, dpib = xc.astype(_MXU), dpr.astype(_MXU), dpi.astype(_MXU)
            dwr_ref[...] += _dot_tn(xcb, dprb)
            dwi_ref[...] += _dot_tn(xcb, dpib)
            dxc = gv * mult * ig + _dot_nt(dprb, wrv) + _dot_nt(dpib, wiv)
            dcb_ref[...] += jnp.sum(dxc, axis=0, keepdims=True)
            dxl = None
            for k in range(CONV_WIDTH):
                dcw_ref[k:k + 1, :] += jnp.sum(dxc * xs[k], axis=0, keepdims=True)
                term = _shift_up(dxc, dxc_next8, CONV_WIDTH - 1 - k) * cwv[k:k + 1, :]
                dxl = term if dxl is None else dxl + term
            dxl_ref[pl.ds(t0, tt), :] = dxl.astype(dxl_ref.dtype)
            return e_out, dxc[0:8, :]

        lax.fori_loop(0, nt, tile, (jnp.zeros((1, HEAD_DIM), F32), jnp.zeros((8, HEAD_DIM), F32)))
        dlam_ref[...] = dlam_ref[...] * (1.0 - jax.nn.sigmoid(lamv))
        _emit([dxl_ref, dgl_ref], dproj_ref, [(col0 + block) * HEAD_DIM, (col0 + n_blocks + block) * HEAD_DIM], out_sems)

    cs = lambda off: pl.BlockSpec((t, HEAD_DIM), lambda n: (0, off + n))
    vs = pl.BlockSpec((1, HEAD_DIM), lambda n: (0, n))
    ws = pl.BlockSpec((None, HEAD_DIM, HEAD_DIM), lambda n: (n, 0, 0))
    cws = pl.BlockSpec((CONV_WIDTH, HEAD_DIM), lambda n: (0, n))
    w = n_blocks * HEAD_DIM
    vec = S((1, w), F32)
    mat = S((n_blocks, HEAD_DIM, HEAD_DIM), F32)
    return _call(
        comm, body, grid=(n_blocks,),
        in_specs=[cs(col0), cs(col0 + n_blocks)] + [cs(0)] * 6 + [cws, vs, ws, ws, vs],
        out_specs=[_ANY, cws, vs, ws, vs, ws, vs, vs],
        out_shape=[S(proj.shape, _MXU), S((CONV_WIDTH, w), F32), vec, mat, vec, mat, vec, vec],
        scratch_shapes=[pltpu.VMEM((tt, HEAD_DIM), F32), pltpu.VMEM((t, HEAD_DIM), _MXU), pltpu.VMEM((t, HEAD_DIM), _MXU),
                        pltpu.SemaphoreType.DMA((2,))],
        compiler_params=_cp("parallel"), name=name)(proj, proj, h, *kept, dyl, cw, cb, wr, wi, lam)


class _NoExchange:
    def __init__(self, weights):
        self.weights, self.grads, self.packs = weights, {}, {}

    def weight(self, name):
        return self.weights[name]

    def in_proj(self, x, gain, bm):
        hn = _rms_fwd(x, gain, "rms1")
        return [hn, *_mm_nn(hn, self.weights["w_in"], bm=bm, bn=self.weights["w_in"].shape[2], name="in_proj", also=_MXU)]

    def conv_w(self):
        return self.weights["conv_w"]

    def carrier(self, call):
        return None

    def harvest(self, car):
        pass

    def alone(self, call):
        pass


def _local_step(x, target, norms, ex, cb, wr, br, wi, bi, lam, ga, gl):
    g_pre_mix, g_post_mix, g_pre_ffn, g_post_ffn = norms
    t, d = x.shape
    bm = min(t, 512)
    bt = min(t, 2048)

    def run(fn, name, *args, **kw):
        car = ex.carrier(name)
        out = fn(*args, name=name, comm=car, **kw)
        ex.harvest(car)
        return out

    hn1, proj, proj_mx = ex.in_proj(x, g_pre_mix, bm)
    win3, cw = ex.weight("w_in"), ex.conv_w()
    c = win3.shape[0]
    o = run(_attn_fwd, "attn_fwd", proj_mx, (proj.shape[1] - d) // 3 // HEAD_DIM)
    mix = 2 * o.shape[1]
    n_heads = n_blocks = o.shape[1] // HEAD_DIM
    h, yl, *kept = run(_lru_fwd, "lru_fwd", proj, 3 * n_heads, n_blocks, cw, cb, wr, br, wi, bi, lam)
    y = run(_outnorm_fwd, "outnorm_fwd", o, yl, ga, gl)
    wout = ex.weight("w_out")
    mixo = run(_mm_nn, "out_proj", y, wout[None], bm=bm, bn=d)
    x2, hn2 = run(_mid_fwd, "mid_fwd", x, mixo, g_post_mix, g_pre_ffn)
    wg3, wu3 = ex.weight("w_ffn_gate"), ex.weight("w_ffn_up")
    act_dgate, act_dup, act = run(_swiglu_fwd, "ffn_gate_up", hn2, wg3, wu3, bm=bm)
    ex.alone("gather_w_down")
    wd = ex.weight("w_ffn_down")
    ff = wd.shape[0]
    f = _mm_nn(act, wd[None], bm=bm, bn=d // 2, name="ffn_down")
    loss_cols, dout, df, dg_post_ffn = _final(f, x2, target, g_post_ffn, "final")

    dgate, dup = _swiglu_bwd(df, wd, act_dgate, act_dup, bm=bm, bo=ff // 4, name="ffn_down_bwd")
    ex.grads["w_ffn_down"] = _mm_tn(act, df, 1, bm=bt, bk=512, name="ffn_down_dw").reshape(c, ff // c, d)
    ex.grads["w_ffn_gate"] = run(_mm_tn, "ffn_gate_dw", hn2, dgate, c, bm=bt, bk=d // 2)
    ex.grads["w_ffn_up"] = run(_mm_tn, "ffn_up_dw", hn2, dup, c, bm=bt, bk=d // 2)
    dhn2_g = run(_mm_nt, "ffn_gate_dx", dgate, wg3, bm=bm, bo=d // 2, out_dtype=F32)
    dhn2_u = run(_mm_nt, "ffn_up_dx", dup, wu3, bm=bm, bo=d // 2, out_dtype=F32)
    dx2, dmix, dg_pre_ffn, dg_post_mix = run(_mid_bwd, "mid_bwd", dhn2_g, dhn2_u, dout, x2, mixo, g_pre_ffn, g_post_mix)
    dy = run(_mm_nt, "out_proj_dx", dmix, wout[None], bm=bm, bo=mix, out_dtype=F32)
    ex.grads["w_out"] = _mm_tn(y, dmix, 1, bm=bt, bk=mix // 4, name="out_proj_dw").reshape(c, mix // c, d)
    do, dyl, dga, dgl_norm = run(_outnorm_bwd, "outnorm_bwd", dy, o, yl, ga, gl)
    dproj, dcw, dcb, dwr, dbr, dwi, dbi, dlam = run(_lru_bwd, "lru_bwd", proj, 3 * n_heads, n_blocks, h, kept, dyl, cw, cb, wr, wi, lam)
    small = dict(post_mix_norm=dg_post_mix, pre_ffn_norm=dg_pre_ffn, post_ffn_norm=dg_post_ffn, conv_w=dcw, conv_b=dcb,
                 w_rgate=dwr, b_rgate=dbr, w_igate=dwi, b_igate=dbi, lru_lambda=dlam, attn_out_norm=dga, lru_out_norm=dgl_norm)
    ex.packs["early"] = _pack([small[n] for n in _SMALL_EARLY])
    dproj = run(_attn_bwd, "attn_bwd", proj_mx, do, dproj, n_heads)
    ex.grads["w_in"] = _mm_tn(hn1, dproj, c, bm=bt, bk=d // 2, name="in_proj_dw")
    ex.alone("grads_w_in_swap")
    dhn1 = run(_mm_nt, "in_proj_dx", dproj, win3, bm=bm, bo=d // 2, out_dtype=F32)
    grad_x, small["pre_mix_norm"] = run(_first_bwd, "first_bwd", dhn1, dx2, x, g_pre_mix)
    ex.packs["late"] = _pack([small["pre_mix_norm"], (0.5 / d) * jnp.sum(loss_cols, keepdims=True)])
    return loss_cols, grad_x, small


def _into_slot(wsh, slot, dtype, name):
    rows, n = wsh.shape
    rb = _row_block(rows, 256) if rows % 8 == 0 else rows

    def body(s_ref, w_ref, o_ref):
        o_ref[...] = w_ref[...].astype(o_ref.dtype)

    return pl.pallas_call(
        body,
        grid_spec=pltpu.PrefetchScalarGridSpec(
            num_scalar_prefetch=1, grid=(rows // rb,),
            in_specs=[pl.BlockSpec((rb, n), lambda i, s_ref: (i, 0))],
            out_specs=pl.BlockSpec((None, rb, n), lambda i, s_ref: (s_ref[0], i, 0))),
        out_shape=S((4, rows, n), dtype), compiler_params=_cp("parallel"), name=name)(slot, wsh)


class _Exchange:
    SCHEDULE = {
        "in_proj": [("stream", "w_in"), ("ici", "conv_w"), ("ici", "w_ffn_up", 0)],
        "attn_fwd": [("d2d", "w_ffn_up", 0), ("ici", "w_ffn_gate")],
        "lru_fwd": [("d2d", "w_ffn_gate"), ("ici", "w_out"), ("ici", "w_ffn_up", 1)],
        "outnorm_fwd": [("d2d", "w_out"), ("d2d", "w_ffn_up", 1)],
        "out_proj": [("ici", "w_ffn_up", 2), ("ici", "w_ffn_up", 3)],
        "mid_fwd": [("d2d", "w_ffn_up", 2), ("d2d", "w_ffn_up", 3)],
        "ffn_gate_up": [("ici", "w_ffn_down")],
        "gather_w_down": [("d2d", "w_ffn_down")],
        "ffn_gate_dw": [("swap", "w_ffn_down")],
        "ffn_up_dw": [("scatter", "w_ffn_down", 0), ("scatter", "w_ffn_down", 1), ("scatter", "w_ffn_down", 2), ("swap", "w_ffn_gate")],
        "ffn_gate_dx": [("scatter", "w_ffn_down", 3), ("scatter", "w_ffn_gate", 0), ("scatter", "w_ffn_gate", 1), ("swap", "w_ffn_up")],
        "ffn_up_dx": [("share", "w_ffn_down"), ("scatter", "w_ffn_gate", 2), ("scatter", "w_ffn_gate", 3), ("scatter", "w_ffn_up", 0)],
        "mid_bwd": [("share", "w_ffn_gate"), ("scatter", "w_ffn_up", 1), ("scatter", "w_ffn_up", 2)],
        "out_proj_dx": [("scatter", "w_ffn_up", 3)],
        "outnorm_bwd": [("share", "w_ffn_up"), ("swap", "w_out")],
        "lru_bwd": [("scatter", "w_out")],
        "attn_bwd": [("share", "w_out"), ("spread", "early")],
        "grads_w_in_swap": [("swap", "w_in")],
        "in_proj_dx": [("scatter", "w_in")],
        "grads_w_in_share": [("share", "w_in"), ("spread", "late")],
    }
    PIECES = 4

    def __init__(self, slots, place):
        self.buf, self.place = dict(slots), place
        self.grads, self.packs, self.swapped, self.part, self.scattered, self.full, self.spreaded = {}, {}, {}, {}, {}, {}, {}

    def weight(self, name):
        b = self.buf[name]
        return b.reshape(-1, b.shape[2]) if name in ("w_out", "w_ffn_down") else b

    def in_proj(self, x, gain, bm):
        car = self.carrier("in_proj")
        out = _in_proj_streamed(x, gain, car, car.streamed, self.place, bm=bm, name="in_proj")
        self.harvest(car)
        return out

    def conv_w(self):
        return jnp.transpose(self.buf["conv_w"], (1, 0, 2)).reshape(CONV_WIDTH, -1)

    def carrier(self, call):
        if call not in self.SCHEDULE:
            return None
        car = _Carrier()
        car.todo, slot = [], {}
        for kind, name, *piece in self.SCHEDULE[call]:
            if kind in ("ici", "d2d", "stream"):
                if name not in slot:
                    slot[name] = car.inplace(self.buf[name])
                    car.todo.append((self.buf, name, slot[name]))
            if kind == "stream":
                car.streamed = slot[name]
            elif kind in ("ici", "d2d"):
                size = self.buf[name].shape[1] // 2 // self.PIECES
                rows = (piece[0] * size, size) if piece else None
                if kind == "ici":
                    car.gather_ici(slot[name], rows, split=name != "conv_w")
                else:
                    car.gather_d2d(slot[name], rows)
            elif kind == "swap":
                g = self.grads[name]
                o = car.fresh((4, g.shape[1] // 2, g.shape[2]), F32)
                car.swap(car.read(g), o)
                car.todo.append((self.swapped, name, o))
            elif kind == "scatter":
                if name not in self.part:
                    self.part[name] = _add_own_half(self.grads[name], self.swapped[name], self.place[1:], "grads_add_" + name)
                p = self.part[name]
                key = ("scatter", name)
                if key not in slot:
                    slot[key] = (car.read(p), car.inplace(self.scattered[name]) if name in self.scattered else car.fresh(p.shape, p.dtype))
                    car.todo.append((self.scattered, name, slot[key][1]))
                size = p.shape[1] // self.PIECES
                car.scatter(*slot[key], (piece[0] * size, size) if piece else None)
            elif kind == "share":
                o = car.inplace(_sum_chips(self.part[name], self.scattered[name], self.place, "grads_sum_" + name))
                car.share(o)
                car.todo.append((self.full, name, o))
            else:
                o = car.fresh((8,) + self.packs[name].shape, F32)
                car.spread(car.read(self.packs[name]), o)
                car.todo.append((self.spreaded, name, o))
        return car

    def harvest(self, car):
        for state, name, o in (car.todo if car is not None else []):
            state[name] = car.results[o]

    def alone(self, call):
        car = self.carrier(call)
        car.run_alone(call)
        self.harvest(car)

    def small_sum(self, key):
        return _sum_devices(self.packs[key], self.spreaded[key], 2 * self.place[0:1] + self.place[1:], "grads_small_sum_" + key)


def _row_block(rows, cap):
    return max(b for b in range(8, cap + 1, 8) if rows % b == 0)


def _add_own_half(g, recv, core, name):
    _, rows, n = g.shape
    half = rows // 2
    rb = _row_block(half, 512)
    nb = half // rb

    def body(c_ref, g_ref, r_ref, o_ref):
        o_ref[...] = (g_ref[...] + r_ref[...]).astype(o_ref.dtype)

    return pl.pallas_call(
        body,
        grid_spec=pltpu.PrefetchScalarGridSpec(
            num_scalar_prefetch=1, grid=(4, nb),
            in_specs=[pl.BlockSpec((None, rb, n), lambda k, i, c_ref: (k, c_ref[0] * nb + i, 0)),
                      pl.BlockSpec((None, rb, n), lambda k, i, c_ref: (k, i, 0))],
            out_specs=pl.BlockSpec((None, rb, n), lambda k, i, c_ref: (k, i, 0))),
        out_shape=S((4, half, n), BF16), compiler_params=_cp("parallel", "parallel"), name=name)(core, g, recv)


def _sum_chips(part, recv, place, name):
    _, rows, n = part.shape
    rb = _row_block(rows, 64)
    nb = rows // rb

    def body(p_ref, own_ref, r0, r1, r2, r3, o_ref):
        own = own_ref[...].astype(F32)
        terms = [jnp.where(p_ref[0] == k, own, r[...].astype(F32)) for k, r in enumerate((r0, r1, r2, r3))]
        o_ref[...] = ((terms[0] + terms[1]) + terms[2]) + terms[3]

    def slot(k):
        return pl.BlockSpec((None, rb, n), lambda i, p_ref: (jnp.where(p_ref[0] == k, (k + 1) % 4, k), i, 0))

    return pl.pallas_call(
        body,
        grid_spec=pltpu.PrefetchScalarGridSpec(
            num_scalar_prefetch=1, grid=(nb,),
            in_specs=[pl.BlockSpec((None, rb, n), lambda i, p_ref: (p_ref[0], i, 0))] + [slot(k) for k in range(4)],
            out_specs=pl.BlockSpec((rb, n), lambda i, p_ref: (p_ref[1] * nb + i, 0))),
        out_shape=S((2 * rows, n), F32), compiler_params=_cp("parallel"), name=name)(place, part, recv, recv, recv, recv)


def _sum_devices(own, spread, me, name):
    rows = own.shape[0]

    def body(me_ref, own_ref, *refs):
        acc = None
        for k, r in enumerate(refs[:8]):
            term = jnp.where(me_ref[0] == k, own_ref[...], r[...])
            acc = term if acc is None else acc + term
        refs[8][...] = acc

    def slot(k):
        return pl.BlockSpec((None, rows, 128), lambda i, me_ref: (jnp.where(me_ref[0] == k, (k + 1) % 8, k), 0, 0))

    whole = pl.BlockSpec((rows, 128), lambda i, me_ref: (0, 0))
    return pl.pallas_call(
        body,
        grid_spec=pltpu.PrefetchScalarGridSpec(num_scalar_prefetch=1, grid=(1,), in_specs=[whole] + [slot(k) for k in range(8)],
                                               out_specs=whole),
        out_shape=S((rows, 128), F32), compiler_params=_cp("arbitrary"), name=name)(me, own, *[spread] * 8)


def _adamw(w, g, m, v, name, regive=False):
    rows, n = w.shape
    rb = rows if rows * n * 4 <= (1 << 21) else _row_block(rows, 256)
    c1 = 1.0 - ADAM_B1 ** ADAM_STEP
    c2 = 1.0 - ADAM_B2 ** ADAM_STEP

    def body(w_ref, g_ref, m_ref, v_ref, d_ref, nm_ref, nv_ref, *again):
        gv = g_ref[...]
        for ref in again:
            ref[...] = gv
        nm = ADAM_B1 * m_ref[...] + (1.0 - ADAM_B1) * gv
        nv = ADAM_B2 * v_ref[...] + (1.0 - ADAM_B2) * (gv * gv)
        nm_ref[...] = nm
        nv_ref[...] = nv
        d_ref[...] = -ADAM_LR * ((nm / c1) / (jnp.sqrt(nv / c2) + ADAM_EPS) + ADAM_WD * w_ref[...])

    bs = pl.BlockSpec((rb, n), lambda i: (i, 0))
    n_out = 4 if regive else 3
    return pl.pallas_call(body, grid=(rows // rb,), in_specs=[bs] * 4, out_specs=[bs] * n_out, out_shape=[S((rows, n), F32)] * n_out,
                          compiler_params=_cp("parallel"), name=name)(w, g, m, v)


_BIG = ("w_in", "w_out", "w_ffn_gate", "w_ffn_up", "w_ffn_down")
_SMALL = ("pre_mix_norm", "post_mix_norm", "pre_ffn_norm", "post_ffn_norm", "conv_w", "conv_b", "w_rgate", "b_rgate",
          "w_igate", "b_igate", "lru_lambda", "attn_out_norm", "lru_out_norm")
_SMALL_EARLY = _SMALL[1:]
_WEIGHTS = ("pre_mix_norm", "post_mix_norm", "pre_ffn_norm", "post_ffn_norm", "w_in", "conv_w", "conv_b", "w_rgate", "b_rgate",
            "w_igate", "b_igate", "lru_lambda", "attn_out_norm", "lru_out_norm", "w_out", "w_ffn_gate", "w_ffn_up", "w_ffn_down")


def _pack(arrays):
    flat = []
    for a in arrays:
        f = a.reshape(-1)
        flat.append(jnp.pad(f, (0, (-f.shape[0]) % 1024)))
    return jnp.concatenate(flat).reshape(-1, 128)


def _unpack(packed, shapes):
    out, pos = [], 0
    flat = packed.reshape(-1)
    for s in shapes:
        size = math.prod(s)
        out.append(flat[pos:pos + size].reshape(s))
        pos += size + (-size) % 1024
    return out


def kernel(x, pre_mix_norm, post_mix_norm, pre_ffn_norm, post_ffn_norm, w_in, conv_w, conv_b, w_rgate, b_rgate, w_igate, b_igate, lru_lambda, attn_out_norm, lru_out_norm, w_out, w_ffn_gate, w_ffn_up, w_ffn_down, loss_target, m_pre_mix_norm, m_post_mix_norm, m_pre_ffn_norm, m_post_ffn_norm, m_w_in, m_conv_w, m_conv_b, m_w_rgate, m_b_rgate, m_w_igate, m_b_igate, m_lru_lambda, m_attn_out_norm, m_lru_out_norm, m_w_out, m_w_ffn_gate, m_w_ffn_up, m_w_ffn_down, v_pre_mix_norm, v_post_mix_norm, v_pre_ffn_norm, v_post_ffn_norm, v_w_in, v_conv_w, v_conv_b, v_w_rgate, v_b_rgate, v_w_igate, v_b_igate, v_lru_lambda, v_attn_out_norm, v_lru_out_norm, v_w_out, v_w_ffn_gate, v_w_ffn_up, v_w_ffn_down):
    given = dict(locals())
    w = {n: given[n][0] for n in _WEIGHTS}
    m = {n: given["m_" + n][0] for n in _WEIGHTS}
    v = {n: given["v_" + n][0] for n in _WEIGHTS}
    xs, target = x[0], loss_target[0]
    d = xs.shape[1]
    chip = (2 * lax.axis_index("x") + lax.axis_index("y")).astype(jnp.int32)
    place = jnp.stack([chip, lax.axis_index("c").astype(jnp.int32)])

    slots = {n: _into_slot(w[n], place[0:1], _MXU, "slot_" + n) for n in _BIG}
    slots["conv_w"] = _into_slot(w["conv_w"], place[0:1], F32, "slot_conv_w")
    ex = _Exchange(slots, place)
    row = lambda a: a.reshape(1, -1)
    norms = tuple(row(w[n]) for n in ("pre_mix_norm", "post_mix_norm", "pre_ffn_norm", "post_ffn_norm"))

    loss_cols, grad_x, small = _local_step(
        xs, target, norms, ex, row(w["conv_b"]), w["w_rgate"], row(w["b_rgate"]),
        w["w_igate"], row(w["b_igate"]), row(w["lru_lambda"]), row(w["attn_out_norm"]), row(w["lru_out_norm"]))


    ex.alone("grads_w_in_share")
    reduced = {n: ex.full[n] for n in _BIG}
    early = _unpack(ex.small_sum("early"), [small[n].shape for n in _SMALL_EARLY])
    late = _unpack(ex.small_sum("late"), [small["pre_mix_norm"].shape, (1, 1)])
    loss = late[1][0, 0]
    for n, g in zip(_SMALL_EARLY + ("pre_mix_norm",), early + late[:1]):
        reduced[n] = g.reshape(w[n].shape) if n != "conv_w" else lax.dynamic_slice_in_dim(g, chip * w[n].shape[1], w[n].shape[1], axis=1)

    delta, new_m, new_v = {}, {}, {}
    for n in _BIG:
        delta[n], new_m[n], new_v[n], reduced[n] = _adamw(w[n], reduced[n], m[n], v[n], "adamw_" + n, regive=True)
    shapes = [w[n].shape for n in _SMALL]
    packed = _adamw(*[_pack([src[n] for n in _SMALL]) for src in (w, reduced, m, v)], "adamw_small")
    for out, p in zip((delta, new_m, new_v), packed):
        out.update(zip(_SMALL, _unpack(p, shapes)))

    lead = lambda a: a[None]
    return (loss, lead(grad_x), *[lead(reduced[n]) for n in _WEIGHTS], *[lead(delta[n]) for n in _WEIGHTS],
            *[lead(new_m[n]) for n in _WEIGHTS], *[lead(new_v[n]) for n in _WEIGHTS])
```

```python
import functools
import math

import jax
import jax.numpy as jnp
from jax import lax
from jax.experimental import pallas as pl
from jax.experimental.pallas import tpu as pltpu

F32 = jnp.float32
BF16 = jnp.bfloat16
_MXU = BF16
S = jax.ShapeDtypeStruct

RMS_EPS = 1e-6
HEAD_DIM = 128
CONV_WIDTH = 4
LRU_C = 8.0
ADAM_LR, ADAM_B1, ADAM_B2, ADAM_EPS, ADAM_WD, ADAM_STEP = 0.001, 0.9, 0.999, 1e-08, 0.01, 10
EXP_CUT = -105.0
VMEM_LIMIT = 60 * 1024 * 1024
ROW_TILE = 256
SEQ_TILE = 256
ATTN_BLOCK = 256
ATTN_HEADS = 2
MESH = pl.DeviceIdType.MESH


def _cp(*sem):
    return pltpu.CompilerParams(dimension_semantics=sem, vmem_limit_bytes=VMEM_LIMIT)


def _dot(a, b):
    return jnp.dot(a, b, preferred_element_type=F32)


def _dot_nt(a, b):
    return lax.dot_general(a, b, (((1,), (1,)), ((), ())), preferred_element_type=F32)


def _dot_tn(a, b):
    return lax.dot_general(a, b, (((0,), (0,)), ((), ())), preferred_element_type=F32)


def _rstd(v):
    return lax.rsqrt(jnp.mean(v * v, axis=-1, keepdims=True) + RMS_EPS)


def _rms_bwd(dn, vh, r, gain):
    dvh = dn * gain
    dv = r * (dvh - vh * jnp.mean(dvh * vh, axis=-1, keepdims=True))
    return dv, jnp.sum(dn * vh, axis=0, keepdims=True)


def _log_sigmoid(z):
    return jnp.minimum(z, 0.0) - jnp.log(1.0 + jnp.exp(-jnp.abs(z)))


def _expm1(v):
    small = v * (1.0 + v * (0.5 + v * (1.0 / 6.0 + v * (1.0 / 24.0 + v * (1.0 / 120.0)))))
    return jnp.where(jnp.abs(v) < 0.04, small, jnp.exp(v) - 1.0)


_GELU_C = math.sqrt(2.0 / math.pi)


def _gelu(v):
    return 0.5 * v * (1.0 + jnp.tanh(_GELU_C * (v + 0.044715 * v * v * v)))


def _gelu_grad(v):
    th = jnp.tanh(_GELU_C * (v + 0.044715 * v * v * v))
    return 0.5 * (1.0 + th) + 0.5 * v * (1.0 - th * th) * _GELU_C * (1.0 + 3.0 * 0.044715 * v * v)


def _row_spec(tm, d):
    return pl.BlockSpec((tm, d), lambda i: (i, 0))


def _vec_spec(d):
    return pl.BlockSpec((1, d), lambda i: (0, 0))


_ANY = pl.BlockSpec(memory_space=pl.ANY)


def _place():
    x, y, c = lax.axis_index("x"), lax.axis_index("y"), lax.axis_index("c")
    return x, y, c, [(1 - x, y), (x, 1 - y), (1 - x, 1 - y)]


def _remote(src, dst, send_sem, recv_sem, to):
    return pltpu.make_async_remote_copy(src_ref=src, dst_ref=dst, send_sem=send_sem, recv_sem=recv_sem,
                                        device_id=to, device_id_type=MESH)


class _Carrier:
    def __init__(self):
        self.inputs, self.out_shapes, self.aliases, self.ops, self.n_sems, self.results = [], [], {}, [], 0, None

    def inplace(self, arr):
        self.aliases[len(self.inputs)] = len(self.out_shapes)
        self.inputs.append(arr)
        self.out_shapes.append(S(arr.shape, arr.dtype))
        return len(self.out_shapes) - 1

    def read(self, arr):
        self.inputs.append(arr)
        return len(self.inputs) - 1

    def fresh(self, shape, dtype):
        self.out_shapes.append(S(shape, dtype))
        return len(self.out_shapes) - 1

    def _add(self, n_sems, copies):
        base = self.n_sems
        self.n_sems += n_sems

        def start(ins, outs, send, recv):
            for k, (src, dst, _, to) in enumerate(copies(ins, outs)):
                _remote(src, dst, send.at[base + k], recv.at[base + k], to).start()

        def finish(ins, outs, send, recv):
            for k, (src, _, land, to) in enumerate(copies(ins, outs)):
                _remote(src, land, send.at[base + k], recv.at[base + k], to).wait()

        self.ops.append((start, finish))

    def gather_ici(self, o, rows=None, split=True, rel=(0, 1, 2)):
        half = self.out_shapes[o].shape[1] // 2
        lo, size = rows or (0, half)

        def copies(ins, outs):
            x, y, c, chips = _place()
            part = (lambda ref: ref.at[pl.ds(c * half + lo, size)]) if split else (lambda ref: ref)
            mine = part(outs[o].at[2 * x + y])
            return [(mine, mine, part(outs[o].at[2 * px + py]), (px, py, c)) for px, py in (chips[j] for j in rel)]

        self._add(len(rel), copies)

    def gather_d2d(self, o, rows=None, rel=(0, 1, 2)):
        half = self.out_shapes[o].shape[1] // 2
        lo, size = rows or (0, half)

        def copies(ins, outs):
            x, y, c, chips = _place()
            at = lambda k, cc: outs[o].at[k].at[pl.ds(cc * half + lo, size)]
            return [(at(2 * px + py, c), at(2 * px + py, c), at(2 * px + py, 1 - c), (x, y, 1 - c)) for px, py in (chips[j] for j in rel)]

        self._add(len(rel), copies)

    def swap(self, i, o):
        half = self.inputs[i].shape[1] // 2

        def copies(ins, outs):
            x, y, c, _ = _place()
            return [(ins[i].at[:, pl.ds((1 - c) * half, half)], outs[o], outs[o], (x, y, 1 - c))]

        self._add(1, copies)

    def scatter(self, i, o, rows=None):
        lo, size = rows or (0, self.inputs[i].shape[1])

        def copies(ins, outs):
            x, y, c, chips = _place()
            cut = lambda ref: ref.at[pl.ds(lo, size)]
            return [(cut(ins[i].at[2 * px + py]), cut(outs[o].at[2 * x + y]), cut(outs[o].at[2 * px + py]), (px, py, c)) for px, py in chips]

        self._add(3, copies)

    def share(self, o):
        r = self.out_shapes[o].shape[0] // 2

        def copies(ins, outs):
            x, y, c, _ = _place()
            mine = outs[o].at[pl.ds(c * r, r)]
            return [(mine, mine, outs[o].at[pl.ds((1 - c) * r, r)], (x, y, 1 - c))]

        self._add(1, copies)

    def spread(self, i, o):
        def copies(ins, outs):
            x, y, c, _ = _place()
            me = 4 * x + 2 * y + c
            out = []
            for d in range(1, 8):
                to, frm = (me + d) % 8, (me + 8 - d) % 8
                out.append((ins[i], outs[o].at[me], outs[o].at[frm], (to // 4, (to // 2) % 2, to % 2)))
            return out

        self._add(7, copies)

    def _pallas(self, body, n_in, n_out, scratch, **kw):
        k_in, k_out = len(self.inputs), len(self.out_shapes)
        grid = kw.get("grid", ())
        prefetch = kw.get("prefetch", [])
        n_pre = len(prefetch)

        def wrapped(*refs):
            pre, refs = refs[:n_pre], refs[n_pre:]
            ins, cins = refs[:n_in], refs[n_in:n_in + k_in]
            outs = refs[n_in + k_in:n_in + k_in + n_out]
            couts = refs[n_in + k_in + n_out:n_in + k_in + n_out + k_out]
            own = refs[n_in + k_in + n_out + k_out:]
            send, recv = own[len(scratch):]
            ids = [pl.program_id(a) for a in range(len(grid))]
            first = functools.reduce(jnp.logical_and, [a == 0 for a in ids], True)
            last = functools.reduce(jnp.logical_and, [a == g - 1 for a, g in zip(ids, grid)], True)

            def go(stage):
                for op in self.ops:
                    op[stage](cins, couts, send, recv)

            if grid:
                pl.when(first)(lambda: go(0))
                body(*pre, *ins, *outs, *own[:len(scratch)])
                pl.when(last)(lambda: go(1))
            else:
                go(0)
                go(1)

        sem = pltpu.SemaphoreType.DMA((self.n_sems,))
        aliases = {n_pre + i: o for i, o in kw.get("aliases", {}).items()}
        aliases.update({n_pre + n_in + i: n_out + o for i, o in self.aliases.items()})
        return _pallas_call(
            wrapped, prefetch, in_specs=list(kw.get("in_specs", [])) + [_ANY] * k_in, out_specs=list(kw.get("out_specs", [])) + [_ANY] * k_out,
            out_shape=list(kw.get("out_shape", [])) + self.out_shapes, scratch_shapes=list(scratch) + [sem, sem],
            input_output_aliases=aliases, name=kw["name"],
            **({"grid": grid, "compiler_params": _cp(*["arbitrary"] * len(grid))} if grid else {}))

    def run(self, body, kw, *args):
        single = not isinstance(kw["out_shape"], (list, tuple))
        out_shape = [kw["out_shape"]] if single else list(kw["out_shape"])
        out_specs = [kw["out_specs"]] if single else list(kw["out_specs"])
        res = self._pallas(body, len(args), len(out_shape), kw.get("scratch_shapes", []), grid=kw["grid"], in_specs=kw["in_specs"],
                           out_specs=out_specs, out_shape=out_shape, name=kw["name"], prefetch=kw.get("prefetch", []),
                           aliases=kw.get("input_output_aliases", {}))(*args, *self.inputs)
        self.results = list(res[len(out_shape):])
        return res[0] if single else list(res[:len(out_shape)])

    def run_alone(self, name):
        self.results = list(self._pallas(None, 0, 0, [], name=name)(*self.inputs))


def _pallas_call(body, prefetch, **kw):
    if not prefetch:
        return pl.pallas_call(body, **kw)
    spec = pltpu.PrefetchScalarGridSpec(num_scalar_prefetch=len(prefetch), grid=kw.pop("grid"), in_specs=kw.pop("in_specs"),
                                        out_specs=kw.pop("out_specs"), scratch_shapes=kw.pop("scratch_shapes", []))
    fn = pl.pallas_call(body, grid_spec=spec, **kw)
    return lambda *args: fn(*prefetch, *args)


def _call(comm, body, prefetch=(), **kw):
    if comm is None:
        kw["input_output_aliases"] = {len(prefetch) + i: o for i, o in kw.get("input_output_aliases", {}).items()}
        return _pallas_call(body, list(prefetch), **kw)
    return functools.partial(comm.run, body, dict(kw, prefetch=list(prefetch)))


def _in_proj_streamed(x, gain, car, o_w, place, *, bm, name):
    m, k = x.shape
    n = car.out_shapes[o_w].shape[2]
    ni, half = m // bm, k // 2
    k_in, k_out = len(car.inputs), len(car.out_shapes)
    order = lambda p: ((p & 1) << 1) | (p >> 1)

    def body(place_ref, x_ref, g_ref, *refs):
        cins, (hn_ref, o_ref, ob_ref), couts = refs[:k_in], refs[k_in:k_in + 3], refs[k_in + 3:k_in + 3 + k_out]
        wbuf, local, ici_send, ici_recv, d2d_send, d2d_recv, send, recv = refs[k_in + 3 + k_out:]
        p, i = pl.program_id(0), pl.program_id(1)
        x, y, c, chips = _place()
        me = 2 * x + y
        rows = lambda chunk, cc: couts[o_w].at[chunk].at[pl.ds(cc * half, half)]

        @pl.when(jnp.logical_and(p == 0, i == 0))
        def _():
            for j, (px, py) in enumerate(chips):
                _remote(rows(me, c), rows(me, c), ici_send.at[j], ici_recv.at[j], (px, py, c)).start()
            for op in car.ops:
                op[0](cins, couts, send, recv)

        for j, (px, py) in enumerate(chips):
            @pl.when(jnp.logical_and(p == j + 1, i == 0))
            def _(j=j, px=px, py=py):
                landed, other = rows(2 * px + py, c), rows(2 * px + py, 1 - c)
                _remote(landed, landed, ici_send.at[j], ici_recv.at[j], (px, py, c)).wait_recv()
                _remote(landed, landed, d2d_send.at[j], d2d_recv.at[j], (x, y, 1 - c)).start()
                _remote(other, other, d2d_send.at[j], d2d_recv.at[j], (x, y, 1 - c)).wait_recv()

        @pl.when(i == 0)
        def _():
            cp = pltpu.make_async_copy(couts[o_w].at[me ^ order(p)], wbuf, local.at[0])
            cp.start()
            cp.wait()

        xv = x_ref[...]
        hn = ((xv * _rstd(xv)) * g_ref[...]).astype(_MXU)
        hn_ref[...] = hn
        res = _dot(hn, wbuf[...])
        o_ref[...] = res
        ob_ref[...] = res.astype(ob_ref.dtype)

        @pl.when(jnp.logical_and(p == 3, i == ni - 1))
        def _():
            for j, (px, py) in enumerate(chips):
                _remote(rows(me, c), rows(me, c), ici_send.at[j], ici_recv.at[j], (px, py, c)).wait_send()
                _remote(rows(me, c), rows(me, c), d2d_send.at[j], d2d_recv.at[j], (x, y, 1 - c)).wait_send()
            for op in car.ops:
                op[1](cins, couts, send, recv)

    ospec = pl.BlockSpec((bm, n), lambda p, i, place_ref: (i, place_ref[0] ^ order(p)))
    rows = pl.BlockSpec((bm, k), lambda p, i, place_ref: (i, 0))
    three, sems = pltpu.SemaphoreType.DMA((3,)), pltpu.SemaphoreType.DMA((max(car.n_sems, 1),))
    res = pl.pallas_call(
        body,
        grid_spec=pltpu.PrefetchScalarGridSpec(
            num_scalar_prefetch=1, grid=(4, ni),
            in_specs=[rows, pl.BlockSpec((1, k), lambda p, i, place_ref: (0, 0))] + [_ANY] * k_in,
            out_specs=[pl.BlockSpec((bm, k), lambda p, i, place_ref: (p * ni + i, 0)), ospec, ospec] + [_ANY] * k_out,
            scratch_shapes=[pltpu.VMEM((k, n), _MXU), pltpu.SemaphoreType.DMA((1,)), three, three, three, three, sems, sems]),
        out_shape=[S((4 * m, k), _MXU), S((m, 4 * n), F32), S((m, 4 * n), _MXU)] + car.out_shapes,
        input_output_aliases={3 + a: 3 + o for a, o in car.aliases.items()},
        compiler_params=_cp("arbitrary", "arbitrary"), name=name)(place, x, gain, *car.inputs)
    car.results = list(res[3:])
    return res[0], res[1], res[2]


def _mm_nn(a, b3, *, bm, bn, name, also=None, comm=None):
    m, k = a.shape
    c, _, n = b3.shape
    ni, nj = m // bm, n // bn

    def body(a_ref, b_ref, *o_refs):
        res = _dot(a_ref[...], b_ref[...])
        for o_ref in o_refs:
            o_ref[...] = res.astype(o_ref.dtype)

    ospec = pl.BlockSpec((bm, bn), lambda cc, j, i: (i, cc * nj + j))
    dtypes = [F32] + ([] if also is None else [also])
    out = _call(
        comm, body, grid=(c, nj, ni),
        in_specs=[pl.BlockSpec((bm, k), lambda cc, j, i: (i, 0)), pl.BlockSpec((None, k, bn), lambda cc, j, i: (cc, 0, j))],
        out_specs=[ospec] * len(dtypes), out_shape=[S((m, c * n), dt) for dt in dtypes],
        compiler_params=_cp("parallel", "parallel", "parallel"), name=name)(a, b3)
    return out[0] if also is None else out


def _mm_nt(a, b3, *, bm, bo, out_dtype, name, comm=None):
    m = a.shape[0]
    c, ko, n = b3.shape
    ni, nj = m // bm, ko // bo

    def body(a_ref, b_ref, o_ref):
        acc = _dot_nt(a_ref[:, 0:n], b_ref[0])
        for cc in range(1, c):
            acc = acc + _dot_nt(a_ref[:, cc * n:(cc + 1) * n], b_ref[cc])
        o_ref[...] = acc.astype(o_ref.dtype)

    return _call(
        comm, body, grid=(nj, ni),
        in_specs=[pl.BlockSpec((bm, c * n), lambda j, i: (i, 0)),
                  pl.BlockSpec((c, bo, n), lambda j, i: (0, j, 0))],
        out_specs=pl.BlockSpec((bm, bo), lambda j, i: (i, j)),
        out_shape=S((m, ko), out_dtype),
        compiler_params=_cp("parallel", "parallel"), name=name)(a, b3)


def _mm_tn(a, b, c, *, bm, bk, name, comm=None):
    m, k = b.shape[0], a.shape[1]
    n = b.shape[1] // c
    nm, nk = m // bm, k // bk

    def body(a_ref, b_ref, o_ref, acc):
        mm = pl.program_id(2)

        @pl.when(mm == 0)
        def _():
            acc[...] = jnp.zeros_like(acc)

        acc[...] += _dot_tn(a_ref[...], b_ref[...])

        @pl.when(mm == nm - 1)
        def _():
            o_ref[...] = acc[...]

    return _call(
        comm, body, grid=(c, nk, nm),
        in_specs=[pl.BlockSpec((bm, bk), lambda cc, j, mm: (mm, j)),
                  pl.BlockSpec((bm, n), lambda cc, j, mm: (mm, cc))],
        out_specs=pl.BlockSpec((None, bk, n), lambda cc, j, mm: (cc, j, 0)),
        out_shape=S((c, k, n), F32),
        scratch_shapes=[pltpu.VMEM((bk, n), F32)],
        compiler_params=_cp("parallel", "parallel", "arbitrary"), name=name)(a, b)


def _chunk_order(p):
    return ((p & 1) << 1) | (p >> 1)


def _swiglu_fwd(hn, wg3, wu3, place, span, into, *, bm, name, comm=None):
    m, k = hn.shape
    c, _, n = wg3.shape
    chunk = lambda p, place_ref: place_ref[0] ^ _chunk_order(p + span[0])

    def body(place_ref, a_ref, g_ref, u_ref, *refs):
        dgate_ref, dup_ref, act_ref = refs[-3:]
        a = a_ref[...]
        gate = _dot(a, g_ref[...])
        up = _dot(a, u_ref[...])
        sg = jax.nn.sigmoid(gate)
        silu = gate * sg
        dgate_ref[...] = (up * (sg * (1.0 + gate * (1.0 - sg)))).astype(dgate_ref.dtype)
        dup_ref[...] = silu.astype(dup_ref.dtype)
        act_ref[...] = (silu * up).astype(act_ref.dtype)

    wspec = pl.BlockSpec((None, k, n), lambda p, i, place_ref: (chunk(p, place_ref), 0, 0))
    ospec = pl.BlockSpec((bm, n), lambda p, i, place_ref: (i, chunk(p, place_ref)))
    given = list(into) if into is not None else []
    return _call(
        comm, body, prefetch=[place], grid=(span[1] - span[0], m // bm),
        in_specs=[pl.BlockSpec((bm, k), lambda p, i, place_ref: (i, 0)), wspec, wspec] + [_ANY] * len(given),
        out_specs=[ospec, ospec, ospec],
        out_shape=[S((m, c * n), _MXU), S((m, c * n), _MXU), S((m, c * n), _MXU)],
        input_output_aliases={3 + a: a for a in range(len(given))},
        compiler_params=_cp("arbitrary", "arbitrary"), name=name)(hn, wg3, wu3, *given)


def _swiglu_bwd(df, wd, act_dgate, act_dup, *, bm, bo, name):
    m, k = df.shape
    ko = wd.shape[0]

    def body(a_ref, b_ref, g_ref, u_ref, dg_ref, du_ref):
        dact = _dot_nt(a_ref[...], b_ref[...])
        dg_ref[...] = (dact * g_ref[...].astype(F32)).astype(dg_ref.dtype)
        du_ref[...] = (dact * u_ref[...].astype(F32)).astype(du_ref.dtype)

    ospec = pl.BlockSpec((bm, bo), lambda j, i: (i, j))
    return pl.pallas_call(
        body, grid=(ko // bo, m // bm),
        in_specs=[pl.BlockSpec((bm, k), lambda j, i: (i, 0)), pl.BlockSpec((bo, k), lambda j, i: (j, 0)), ospec, ospec],
        out_specs=[ospec, ospec],
        out_shape=[S((m, ko), _MXU), S((m, ko), _MXU)],
        compiler_params=_cp("parallel", "parallel"), name=name)(df, wd, act_dgate, act_dup)


def _rms_fwd(x, gain, name, comm=None):
    t, d = x.shape
    tm = min(t, ROW_TILE)

    def body(x_ref, g_ref, o_ref):
        xv = x_ref[...]
        o_ref[...] = ((xv * _rstd(xv)) * g_ref[...]).astype(o_ref.dtype)

    return _call(comm, body, grid=(t // tm,), in_specs=[_row_spec(tm, d), _vec_spec(d)], out_specs=_row_spec(tm, d),
                          out_shape=S((t, d), _MXU), compiler_params=_cp("parallel"), name=name)(x, gain)


def _outnorm_fwd(o, yl, ga, gl, name, comm=None):
    t, w = o.shape
    tm = min(t, ROW_TILE)

    def body(o_ref, l_ref, ga_ref, gl_ref, y_ref):
        ov, lv = o_ref[...], l_ref[...]
        y_ref[:, :w] = ((ov * _rstd(ov)) * ga_ref[...]).astype(y_ref.dtype)
        y_ref[:, w:] = ((lv * _rstd(lv)) * gl_ref[...]).astype(y_ref.dtype)

    return _call(comm, body, grid=(t // tm,), in_specs=[_row_spec(tm, w), _row_spec(tm, w), _vec_spec(w), _vec_spec(w)],
                 out_specs=_row_spec(tm, 2 * w), out_shape=S((t, 2 * w), _MXU),
                 compiler_params=_cp("parallel"), name=name)(o, yl, ga, gl)


def _mid_fwd(x, mix, g_post, g_pre, name, comm=None):
    t, d = x.shape
    tm = min(t, ROW_TILE)

    def body(x_ref, m_ref, gp_ref, gn_ref, x2_ref, hn_ref):
        mv = m_ref[...]
        x2 = x_ref[...] + (mv * _rstd(mv)) * gp_ref[...]
        x2_ref[...] = x2
        hn_ref[...] = ((x2 * _rstd(x2)) * gn_ref[...]).astype(hn_ref.dtype)

    return _call(comm, body, grid=(t // tm,), in_specs=[_row_spec(tm, d), _row_spec(tm, d), _vec_spec(d), _vec_spec(d)],
                          out_specs=[_row_spec(tm, d), _row_spec(tm, d)], out_shape=[S((t, d), F32), S((t, d), _MXU)],
                          compiler_params=_cp("parallel"), name=name)(x, mix, g_post, g_pre)


def _final(f, x2, target, g_post, name):
    t, d = f.shape
    tm = min(t, ROW_TILE)

    def body(f_ref, x2_ref, t_ref, g_ref, loss_ref, dout_ref, df_ref, dg_ref):
        @pl.when(pl.program_id(0) == 0)
        def _():
            loss_ref[...] = jnp.zeros_like(loss_ref)
            dg_ref[...] = jnp.zeros_like(dg_ref)

        fv = f_ref[...]
        r = _rstd(fv)
        fh = fv * r
        err = (x2_ref[...] + fh * g_ref[...]) - t_ref[...]
        loss_ref[...] += jnp.sum(err * err, axis=0, keepdims=True)
        dout = err * (1.0 / d)
        dout_ref[...] = dout
        dfv, dg = _rms_bwd(dout, fh, r, g_ref[...])
        df_ref[...] = dfv.astype(df_ref.dtype)
        dg_ref[...] += dg

    return pl.pallas_call(
        body, grid=(t // tm,),
        in_specs=[_row_spec(tm, d), _row_spec(tm, d), _row_spec(tm, d), _vec_spec(d)],
        out_specs=[_vec_spec(d), _row_spec(tm, d), _row_spec(tm, d), _vec_spec(d)],
        out_shape=[S((1, d), F32), S((t, d), F32), S((t, d), _MXU), S((1, d), F32)],
        compiler_params=_cp("arbitrary"), name=name)(f, x2, target, g_post)


def _mid_bwd(dhn_a, dhn_b, dout, x2, mix, g_pre, g_post, name, comm=None):
    t, d = x2.shape
    tm = min(t, ROW_TILE)

    def body(da_ref, db_ref, do_ref, x2_ref, m_ref, gn_ref, gp_ref, dx2_ref, dm_ref, dgn_ref, dgp_ref):
        @pl.when(pl.program_id(0) == 0)
        def _():
            dgn_ref[...] = jnp.zeros_like(dgn_ref)
            dgp_ref[...] = jnp.zeros_like(dgp_ref)

        x2 = x2_ref[...]
        r = _rstd(x2)
        dxa, dgn = _rms_bwd(da_ref[...] + db_ref[...], x2 * r, r, gn_ref[...])
        dx2 = do_ref[...] + dxa
        dx2_ref[...] = dx2
        dgn_ref[...] += dgn
        mv = m_ref[...]
        rm = _rstd(mv)
        dmv, dgp = _rms_bwd(dx2, mv * rm, rm, gp_ref[...])
        dm_ref[...] = dmv.astype(dm_ref.dtype)
        dgp_ref[...] += dgp

    rs, vs = _row_spec(tm, d), _vec_spec(d)
    return _call(
        comm, body, grid=(t // tm,), in_specs=[rs, rs, rs, rs, rs, vs, vs], out_specs=[rs, rs, vs, vs],
        out_shape=[S((t, d), F32), S((t, d), _MXU), S((1, d), F32), S((1, d), F32)],
        compiler_params=_cp("arbitrary"), name=name)(dhn_a, dhn_b, dout, x2, mix, g_pre, g_post)


def _first_bwd(dhn, dx2, x, gain, name, comm=None):
    t, d = x.shape
    tm = min(t, ROW_TILE)

    def body(dh_ref, dx2_ref, x_ref, g_ref, dx_ref, dg_ref):
        @pl.when(pl.program_id(0) == 0)
        def _():
            dg_ref[...] = jnp.zeros_like(dg_ref)

        xv = x_ref[...]
        r = _rstd(xv)
        dxa, dg = _rms_bwd(dh_ref[...], xv * r, r, g_ref[...])
        dx_ref[...] = dx2_ref[...] + dxa
        dg_ref[...] += dg

    rs, vs = _row_spec(tm, d), _vec_spec(d)
    return _call(comm, body, grid=(t // tm,), in_specs=[rs, rs, rs, vs], out_specs=[rs, vs],
                          out_shape=[S((t, d), F32), S((1, d), F32)], compiler_params=_cp("arbitrary"), name=name)(dhn, dx2, x, gain)


def _outnorm_bwd(dy, o, yl, ga, gl, name, comm=None):
    t, w = o.shape
    tm = min(t, ROW_TILE)

    def body(dy_ref, o_ref, l_ref, ga_ref, gl_ref, do_ref, dl_ref, dga_ref, dgl_ref):
        @pl.when(pl.program_id(0) == 0)
        def _():
            dga_ref[...] = jnp.zeros_like(dga_ref)
            dgl_ref[...] = jnp.zeros_like(dgl_ref)

        ov, lv = o_ref[...], l_ref[...]
        ra, rl = _rstd(ov), _rstd(lv)
        dov, dga = _rms_bwd(dy_ref[:, :w], ov * ra, ra, ga_ref[...])
        dlv, dgl = _rms_bwd(dy_ref[:, w:], lv * rl, rl, gl_ref[...])
        do_ref[...] = dov.astype(do_ref.dtype)
        dl_ref[...] = dlv
        dga_ref[...] += dga
        dgl_ref[...] += dgl

    rs, vs = _row_spec(tm, w), _vec_spec(w)
    return _call(comm, body, grid=(t // tm,), in_specs=[_row_spec(tm, 2 * w), rs, rs, vs, vs], out_specs=[rs, rs, vs, vs],
                          out_shape=[S((t, w), _MXU), S((t, w), F32), S((1, w), F32), S((1, w), F32)],
                          compiler_params=_cp("arbitrary"), name=name)(dy, o, yl, ga, gl)


def _tri_sum(v, tri):
    return _dot(v.astype(_MXU), tri)


def _attn_tile(qb, kb, row, col, shift, scale):
    z = _dot_nt(qb, kb) * scale
    mask = (col + shift) < row
    lb = _log_sigmoid(z)
    lm = jnp.where(mask, lb - z, 0.0)
    return mask, lb, lm


def _attn_fwd(proj, n_heads, name, comm=None):
    t = proj.shape[0]
    bq = min(t, ATTN_BLOCK)
    nq = t // bq
    scale = 1.0 / math.sqrt(HEAD_DIM)

    heads = [slice(a * HEAD_DIM, (a + 1) * HEAD_DIM) for a in range(ATTN_HEADS)]

    def body(q_ref, k_ref, v_ref, o_ref):
        row = lax.broadcasted_iota(jnp.int32, (bq, bq), 0)
        col = lax.broadcasted_iota(jnp.int32, (bq, bq), 1)
        tri = (row > col).astype(_MXU)

        def per_q(qi, _):
            q0 = pl.multiple_of(qi * bq, bq)
            qbs = [q_ref[pl.ds(q0, bq), hd] for hd in heads]

            def cond(st):
                return jnp.logical_and(st[0] >= 0, st[1])

            def step(st):
                kj, _, carries, accs = st
                k0 = pl.multiple_of(kj * bq, bq)
                alive, new_carries, new_accs = None, [], []
                for hd, qb, carry, acc in zip(heads, qbs, carries, accs):
                    mask, lb, lm = _attn_tile(qb, k_ref[pl.ds(k0, bq), hd], row, col, (kj - qi) * bq, scale)
                    w = jnp.where(mask, jnp.exp(lb + _tri_sum(lm, tri) + carry), 0.0)
                    new_accs.append(acc + _dot(w.astype(_MXU), v_ref[pl.ds(k0, bq), hd]))
                    carry = carry + jnp.sum(lm, axis=1, keepdims=True)
                    new_carries.append(carry)
                    live = jnp.max(carry) > EXP_CUT
                    alive = live if alive is None else jnp.logical_or(alive, live)
                return kj - 1, alive, tuple(new_carries), tuple(new_accs)

            st = lax.while_loop(cond, step, (qi, jnp.bool_(True), (jnp.zeros((bq, 1), F32),) * ATTN_HEADS,
                                             (jnp.zeros((bq, HEAD_DIM), F32),) * ATTN_HEADS))
            for hd, acc in zip(heads, st[3]):
                o_ref[pl.ds(q0, bq), hd] = acc
            return 0

        lax.fori_loop(0, nq, per_q, 0)

    groups = n_heads // ATTN_HEADS
    hs = lambda off: pl.BlockSpec((t, ATTN_HEADS * HEAD_DIM), lambda h: (0, off + h))
    return _call(
        comm, body, grid=(groups,), in_specs=[hs(0), hs(groups), hs(2 * groups)], out_specs=hs(0),
        out_shape=S((t, n_heads * HEAD_DIM), F32), compiler_params=_cp("parallel"), name=name)(proj, proj, proj)


def _emit(blocks, out_ref, starts, sems):
    copies = [pltpu.make_async_copy(b, out_ref.at[:, pl.ds(c0, b.shape[1])], sems.at[k]) for k, (b, c0) in enumerate(zip(blocks, starts))]
    for cp in copies:
        cp.start()
    for cp in copies:
        cp.wait()


def _attn_bwd(proj, do, dproj, n_heads, name, comm=None):
    t = proj.shape[0]
    bq = min(t, ATTN_BLOCK)
    nq = t // bq
    scale = 1.0 / math.sqrt(HEAD_DIM)
    groups = n_heads // ATTN_HEADS
    wide = ATTN_HEADS * HEAD_DIM

    heads = [slice(a * HEAD_DIM, (a + 1) * HEAD_DIM) for a in range(ATTN_HEADS)]

    def body(q_ref, k_ref, v_ref, do_ref, _, dproj_ref, dka_ref, dva_ref, g_ref, b_ref, dq_ref, dk_ref, dv_ref, out_sems):
        group = pl.program_id(0)
        dka_ref[...] = jnp.zeros_like(dka_ref)
        dva_ref[...] = jnp.zeros_like(dva_ref)
        row = lax.broadcasted_iota(jnp.int32, (bq, bq), 0)
        col = lax.broadcasted_iota(jnp.int32, (bq, bq), 1)
        tri = (row > col).astype(_MXU)
        tri_lt = (row < col).astype(_MXU)

        def per_q(qi, _):
            q0 = pl.multiple_of(qi * bq, bq)
            qbs = [q_ref[pl.ds(q0, bq), hd] for hd in heads]
            dobs = [do_ref[pl.ds(q0, bq), hd] for hd in heads]

            def cond(st):
                return jnp.logical_and(st[0] >= 0, st[1])

            def step(st):
                kj, _, carries = st
                k0 = pl.multiple_of(kj * bq, bq)
                alive, new_carries = None, []
                for a, (hd, qb, dob, carry) in enumerate(zip(heads, qbs, dobs, carries)):
                    mask, lb, lm = _attn_tile(qb, k_ref[pl.ds(k0, bq), hd], row, col, (kj - qi) * bq, scale)
                    w = jnp.where(mask, jnp.exp(lb + _tri_sum(lm, tri) + carry), 0.0)
                    g_ref[a, pl.ds(k0, bq), :] = w * _dot_nt(dob, v_ref[pl.ds(k0, bq), hd])
                    b_ref[a, pl.ds(k0, bq), :] = jnp.where(mask, jnp.exp(lb), 0.0)
                    dva_ref[pl.ds(k0, bq), hd] += _dot_tn(w.astype(_MXU), dob)
                    carry = carry + jnp.sum(lm, axis=1, keepdims=True)
                    new_carries.append(carry)
                    live = jnp.max(carry) > EXP_CUT
                    alive = live if alive is None else jnp.logical_or(alive, live)
                return kj - 1, alive, tuple(new_carries)

            st = lax.while_loop(cond, step, (qi, jnp.bool_(True), (jnp.zeros((bq, 1), F32),) * ATTN_HEADS))

            def back(kj, st2):
                k0 = pl.multiple_of(kj * bq, bq)
                out = []
                for a, (hd, qb, (before, dq)) in enumerate(zip(heads, qbs, st2)):
                    g = g_ref[a, pl.ds(k0, bq), :]
                    beta = b_ref[a, pl.ds(k0, bq), :]
                    dz = ((g * (1.0 - beta) - (before + _tri_sum(g, tri_lt)) * beta) * scale).astype(_MXU)
                    dka_ref[pl.ds(k0, bq), hd] += _dot_tn(dz, qb)
                    out.append((before + jnp.sum(g, axis=1, keepdims=True), dq + _dot(dz, k_ref[pl.ds(k0, bq), hd])))
                return tuple(out)

            st2 = lax.fori_loop(st[0] + 1, qi + 1, back, ((jnp.zeros((bq, 1), F32), jnp.zeros((bq, HEAD_DIM), F32)),) * ATTN_HEADS)
            for hd, (_, dq) in zip(heads, st2):
                dq_ref[pl.ds(q0, bq), hd] = dq.astype(dq_ref.dtype)
            return 0

        lax.fori_loop(0, nq, per_q, 0)
        dk_ref[...] = dka_ref[...].astype(dk_ref.dtype)
        dv_ref[...] = dva_ref[...].astype(dv_ref.dtype)
        _emit([dq_ref, dk_ref, dv_ref], dproj_ref, [(a * groups + group) * wide for a in range(3)], out_sems)

    hs = lambda off: pl.BlockSpec((t, wide), lambda h: (0, off + h))
    return _call(
        comm, body, grid=(groups,), in_specs=[hs(0), hs(groups), hs(2 * groups), hs(0), _ANY], out_specs=_ANY,
        out_shape=S(dproj.shape, dproj.dtype), input_output_aliases={4: 0},
        scratch_shapes=[pltpu.VMEM((t, wide), F32), pltpu.VMEM((t, wide), F32),
                        pltpu.VMEM((ATTN_HEADS, t, bq), F32), pltpu.VMEM((ATTN_HEADS, t, bq), F32)]
        + [pltpu.VMEM((t, wide), dproj.dtype)] * 3 + [pltpu.SemaphoreType.DMA((3,))],
        compiler_params=_cp("parallel"), name=name)(proj, proj, proj, do, dproj)


def _shift_down(cur, prev8, k):
    if k == 0:
        return cur
    row8 = lax.broadcasted_iota(jnp.int32, prev8.shape, 0)
    rc = pltpu.roll(cur, k, 0)
    top = jnp.where(row8 < k, pltpu.roll(prev8, k, 0), rc[0:8, :])
    return jnp.concatenate([top, rc[8:, :]], axis=0)


def _shift_up(cur, next8, k):
    if k == 0:
        return cur
    n = cur.shape[0]
    row8 = lax.broadcasted_iota(jnp.int32, next8.shape, 0)
    rc = pltpu.roll(cur, n - k, 0)
    bottom = jnp.where(row8 >= 8 - k, pltpu.roll(next8, 8 - k, 0), rc[n - 8:, :])
    return jnp.concatenate([rc[:n - 8, :], bottom], axis=0)


def _lru_conv(xl, prev8, cw, cb):
    xs = [_shift_down(xl, prev8, CONV_WIDTH - 1 - k) for k in range(CONV_WIDTH)]
    xc = xs[0] * cw[0:1, :]
    for k in range(1, CONV_WIDTH):
        xc = xc + xs[k] * cw[k:k + 1, :]
    return xs, xc + cb


def _lru_gates(xl, prev8, cw, cb, wr, br, wi, bi, ls):
    xs, xc = _lru_conv(xl, prev8, cw, cb)
    xcb = xc.astype(_MXU)
    r = jax.nn.sigmoid(_dot(xcb, wr) + br)
    i = jax.nn.sigmoid(_dot(xcb, wi) + bi)
    la = (LRU_C * r) * ls
    a = jnp.exp(la)
    mult = jnp.sqrt(-_expm1(2.0 * la))
    return xs, xc, r, i, a, mult


def _group_scan(a, b, reverse):
    n = a.shape[0]
    row = lax.broadcasted_iota(jnp.int32, a.shape, 0) % 8
    for d in (1, 2, 4):
        if reverse:
            m = row < 8 - d
            a_s, b_s = pltpu.roll(a, n - d, 0), pltpu.roll(b, n - d, 0)
        else:
            m = row >= d
            a_s, b_s = pltpu.roll(a, d, 0), pltpu.roll(b, d, 0)
        b = jnp.where(m, a * b_s + b, b)
        a = jnp.where(m, a * a_s, a)
    return a, b


def _lru_fwd(proj, col0, n_blocks, cw, cb, wr, br, wi, bi, lam, name, comm=None):
    t = proj.shape[0]
    tt = min(t, SEQ_TILE)
    nt = t // tt

    def body(xl_ref, gl_ref, cw_ref, cb_ref, wr_ref, br_ref, wi_ref, bi_ref, lam_ref, h_ref, y_ref, *kept):
        cwv, cbv, brv, biv = cw_ref[...], cb_ref[...], br_ref[...], bi_ref[...]
        wrv, wiv = wr_ref[...].astype(_MXU), wi_ref[...].astype(_MXU)
        ls = _log_sigmoid(lam_ref[...])

        def tile(ti, hin):
            t0 = pl.multiple_of(ti * tt, tt)
            p0 = pl.multiple_of(jnp.maximum(t0 - 8, 0), 8)
            prev8 = xl_ref[pl.ds(p0, 8), :] * (ti > 0).astype(F32)
            xl = xl_ref[pl.ds(t0, tt), :]
            _, xc, r, ig, a, mult = _lru_gates(xl, prev8, cwv, cbv, wrv, brv, wiv, biv, ls)
            for ref, val in zip(kept, (r, ig, a, mult)):
                ref[pl.ds(t0, tt), :] = val
            ga, gb = _group_scan(a, mult * (ig * xc), False)
            for g in range(tt // 8):
                hg = ga[8 * g:8 * g + 8, :] * hin + gb[8 * g:8 * g + 8, :]
                h_ref[pl.ds(t0 + 8 * g, 8), :] = hg
                hin = hg[7:8, :]
            y_ref[pl.ds(t0, tt), :] = h_ref[pl.ds(t0, tt), :] * _gelu(gl_ref[pl.ds(t0, tt), :])
            return hin

        lax.fori_loop(0, nt, tile, jnp.zeros((1, HEAD_DIM), F32))

    cs = lambda off: pl.BlockSpec((t, HEAD_DIM), lambda n: (0, off + n))
    vs = pl.BlockSpec((1, HEAD_DIM), lambda n: (0, n))
    ws = pl.BlockSpec((None, HEAD_DIM, HEAD_DIM), lambda n: (n, 0, 0))
    w = n_blocks * HEAD_DIM
    return _call(
        comm, body, grid=(n_blocks,),
        in_specs=[cs(col0), cs(col0 + n_blocks), pl.BlockSpec((CONV_WIDTH, HEAD_DIM), lambda n: (0, n)), vs, ws, vs, ws, vs, vs],
        out_specs=[cs(0)] * 6, out_shape=[S((t, w), F32)] * 6,
        compiler_params=_cp("parallel"), name=name)(proj, proj, cw, cb, wr, br, wi, bi, lam)


def _lru_bwd(proj, col0, n_blocks, h, kept, dyl, cw, cb, wr, wi, lam, name, comm=None):
    t = proj.shape[0]
    tt = min(t, SEQ_TILE)
    nt = t // tt

    def body(xl_ref, gl_ref, h_ref, r_ref, i_ref, a_ref, m_ref, dy_ref, cw_ref, cb_ref, wr_ref, wi_ref, lam_ref,
             dproj_ref, dcw_ref, dcb_ref, dwr_ref, dbr_ref, dwi_ref, dbi_ref, dlam_ref, g_ref, dxl_ref, dgl_ref, out_sems):
        block = pl.program_id(0)
        cwv, cbv = cw_ref[...], cb_ref[...]
        wrv, wiv = wr_ref[...].astype(_MXU), wi_ref[...].astype(_MXU)
        lamv = lam_ref[...]
        ls = _log_sigmoid(lamv)
        for ref in (dcw_ref, dcb_ref, dwr_ref, dbr_ref, dwi_ref, dbi_ref, dlam_ref):
            ref[...] = jnp.zeros_like(ref)

        def tile(s, carry):
            e_in, dxc_next8 = carry
            ti = nt - 1 - s
            t0 = pl.multiple_of(ti * tt, tt)
            p0 = pl.multiple_of(jnp.maximum(t0 - 8, 0), 8)
            first = (ti > 0).astype(F32)
            xl = xl_ref[pl.ds(t0, tt), :]
            xs, xc = _lru_conv(xl, xl_ref[pl.ds(p0, 8), :] * first, cwv, cbv)
            r, ig, a, mult = (ref[pl.ds(t0, tt), :] for ref in (r_ref, i_ref, a_ref, m_ref))
            hv = h_ref[pl.ds(t0, tt), :]
            h_before = _shift_down(hv, h_ref[pl.ds(p0, 8), :] * first, 1)
            glv = gl_ref[pl.ds(t0, tt), :]
            dyv = dy_ref[pl.ds(t0, tt), :]
            dgl_ref[pl.ds(t0, tt), :] = (dyv * hv * _gelu_grad(glv)).astype(dgl_ref.dtype)
            dh = dyv * _gelu(glv)
            row = lax.broadcasted_iota(jnp.int32, a.shape, 0)
            coef = jnp.where(row == tt - 1, 1.0, pltpu.roll(a, tt - 1, 0))
            ga, gb = _group_scan(coef, dh, True)
            gin = e_in
            for g in reversed(range(tt // 8)):
                gg = ga[8 * g:8 * g + 8, :] * gin + gb[8 * g:8 * g + 8, :]
                g_ref[8 * g:8 * g + 8, :] = gg
                gin = gg[0:1, :]
            gv = g_ref[...]
            e_out = a[0:1, :] * gv[0:1, :]
            ix = ig * xc
            dla = (gv * h_before) * a - (gv * ix) * (a * a / mult)
            dlam_ref[...] += jnp.sum(dla * (LRU_C * r), axis=0, keepdims=True)
            dpr = (dla * (LRU_C * ls)) * (r * (1.0 - r))
            dpi = (gv * mult * xc) * (ig * (1.0 - ig))
            dbr_ref[...] += jnp.sum(dpr, axis=0, keepdims=True)
            dbi_ref[...] += jnp.sum(dpi, axis=0, keepdims=True)
            xcb, dprb, dpib = xc.astype(_MXU), dpr.astype(_MXU), dpi.astype(_MXU)
            dwr_ref[...] += _dot_tn(xcb, dprb)
            dwi_ref[...] += _dot_tn(xcb, dpib)
            dxc = gv * mult * ig + _dot_nt(dprb, wrv) + _dot_nt(dpib, wiv)
            dcb_ref[...] += jnp.sum(dxc, axis=0, keepdims=True)
            dxl = None
            for k in range(CONV_WIDTH):
                dcw_ref[k:k + 1, :] += jnp.sum(dxc * xs[k], axis=0, keepdims=True)
                term = _shift_up(dxc, dxc_next8, CONV_WIDTH - 1 - k) * cwv[k:k + 1, :]
                dxl = term if dxl is None else dxl + term
            dxl_ref[pl.ds(t0, tt), :] = dxl.astype(dxl_ref.dtype)
            return e_out, dxc[0:8, :]

        lax.fori_loop(0, nt, tile, (jnp.zeros((1, HEAD_DIM), F32), jnp.zeros((8, HEAD_DIM), F32)))
        dlam_ref[...] = dlam_ref[...] * (1.0 - jax.nn.sigmoid(lamv))
        _emit([dxl_ref, dgl_ref], dproj_ref, [(col0 + block) * HEAD_DIM, (col0 + n_blocks + block) * HEAD_DIM], out_sems)

    cs = lambda off: pl.BlockSpec((t, HEAD_DIM), lambda n: (0, off + n))
    vs = pl.BlockSpec((1, HEAD_DIM), lambda n: (0, n))
    ws = pl.BlockSpec((None, HEAD_DIM, HEAD_DIM), lambda n: (n, 0, 0))
    cws = pl.BlockSpec((CONV_WIDTH, HEAD_DIM), lambda n: (0, n))
    w = n_blocks * HEAD_DIM
    vec = S((1, w), F32)
    mat = S((n_blocks, HEAD_DIM, HEAD_DIM), F32)
    return _call(
        comm, body, grid=(n_blocks,),
        in_specs=[cs(col0), cs(col0 + n_blocks)] + [cs(0)] * 6 + [cws, vs, ws, ws, vs],
        out_specs=[_ANY, cws, vs, ws, vs, ws, vs, vs],
        out_shape=[S(proj.shape, _MXU), S((CONV_WIDTH, w), F32), vec, mat, vec, mat, vec, vec],
        scratch_shapes=[pltpu.VMEM((tt, HEAD_DIM), F32), pltpu.VMEM((t, HEAD_DIM), _MXU), pltpu.VMEM((t, HEAD_DIM), _MXU),
                        pltpu.SemaphoreType.DMA((2,))],
        compiler_params=_cp("parallel"), name=name)(proj, proj, h, *kept, dyl, cw, cb, wr, wi, lam)


class _NoExchange:
    def __init__(self, weights):
        self.weights, self.grads, self.packs, self.place = weights, {}, {}, jnp.zeros((2,), jnp.int32)

    def weight(self, name):
        return self.weights[name]

    def in_proj(self, x, gain, bm):
        hn = _rms_fwd(x, gain, "rms1")
        return [hn, *_mm_nn(hn, self.weights["w_in"], bm=bm, bn=self.weights["w_in"].shape[2], name="in_proj", also=_MXU)]

    def conv_w(self):
        return self.weights["conv_w"]

    def carrier(self, call):
        return None

    def harvest(self, car):
        pass

    def alone(self, call):
        pass


def _local_step(x, target, norms, ex, cb, wr, br, wi, bi, lam, ga, gl):
    g_pre_mix, g_post_mix, g_pre_ffn, g_post_ffn = norms
    t, d = x.shape
    bm = min(t, 512)
    bt = min(t, 2048)

    def run(fn, name, *args, **kw):
        car = ex.carrier(name)
        out = fn(*args, name=name, comm=car, **kw)
        ex.harvest(car)
        return out

    hn1, proj, proj_mx = ex.in_proj(x, g_pre_mix, bm)
    win3, cw = ex.weight("w_in"), ex.conv_w()
    c = win3.shape[0]
    o = run(_attn_fwd, "attn_fwd", proj_mx, (proj.shape[1] - d) // 3 // HEAD_DIM)
    mix = 2 * o.shape[1]
    n_heads = n_blocks = o.shape[1] // HEAD_DIM
    h, yl, *kept = run(_lru_fwd, "lru_fwd", proj, 3 * n_heads, n_blocks, cw, cb, wr, br, wi, bi, lam)
    y = run(_outnorm_fwd, "outnorm_fwd", o, yl, ga, gl)
    wout = ex.weight("w_out")
    mixo = run(_mm_nn, "out_proj", y, wout[None], bm=bm, bn=d)
    x2, hn2 = run(_mid_fwd, "mid_fwd", x, mixo, g_post_mix, g_pre_ffn)
    wg3, wu3 = ex.weight("w_ffn_gate"), ex.weight("w_ffn_up")
    ffn_mid = run(_swiglu_fwd, "ffn_gate_up", hn2, wg3, wu3, ex.place, (0, min(c, 3)), None, bm=bm)
    if c == 4:
        ex.alone("gather_up_diag")
        wg3, wu3 = ex.weight("w_ffn_gate"), ex.weight("w_ffn_up")
        ffn_mid = run(_swiglu_fwd, "ffn_gate_up_diag", hn2, wg3, wu3, ex.place, (3, 4), ffn_mid, bm=bm)
    act_dgate, act_dup, act = ffn_mid
    wd = ex.weight("w_ffn_down")
    ff = wd.shape[0]
    f = _mm_nn(act, wd[None], bm=bm, bn=d // 2, name="ffn_down")
    loss_cols, dout, df, dg_post_ffn = _final(f, x2, target, g_post_ffn, "final")

    dgate, dup = _swiglu_bwd(df, wd, act_dgate, act_dup, bm=bm, bo=ff // 4, name="ffn_down_bwd")
    ex.grads["w_ffn_down"] = _mm_tn(act, df, 1, bm=bt, bk=512, name="ffn_down_dw").reshape(c, ff // c, d)
    ex.grads["w_ffn_gate"] = run(_mm_tn, "ffn_gate_dw", hn2, dgate, c, bm=bt, bk=d // 2)
    ex.grads["w_ffn_up"] = run(_mm_tn, "ffn_up_dw", hn2, dup, c, bm=bt, bk=d // 2)
    dhn2_g = run(_mm_nt, "ffn_gate_dx", dgate, wg3, bm=bm, bo=d // 2, out_dtype=F32)
    dhn2_u = run(_mm_nt, "ffn_up_dx", dup, wu3, bm=bm, bo=d // 2, out_dtype=F32)
    dx2, dmix, dg_pre_ffn, dg_post_mix = run(_mid_bwd, "mid_bwd", dhn2_g, dhn2_u, dout, x2, mixo, g_pre_ffn, g_post_mix)
    dy = run(_mm_nt, "out_proj_dx", dmix, wout[None], bm=bm, bo=mix, out_dtype=F32)
    ex.grads["w_out"] = _mm_tn(y, dmix, 1, bm=bt, bk=mix // 4, name="out_proj_dw").reshape(c, mix // c, d)
    do, dyl, dga, dgl_norm = run(_outnorm_bwd, "outnorm_bwd", dy, o, yl, ga, gl)
    dproj, dcw, dcb, dwr, dbr, dwi, dbi, dlam = run(_lru_bwd, "lru_bwd", proj, 3 * n_heads, n_blocks, h, kept, dyl, cw, cb, wr, wi, lam)
    small = dict(post_mix_norm=dg_post_mix, pre_ffn_norm=dg_pre_ffn, post_ffn_norm=dg_post_ffn, conv_w=dcw, conv_b=dcb,
                 w_rgate=dwr, b_rgate=dbr, w_igate=dwi, b_igate=dbi, lru_lambda=dlam, attn_out_norm=dga, lru_out_norm=dgl_norm)
    ex.packs["early"] = _pack([small[n] for n in _SMALL_EARLY])
    dproj = run(_attn_bwd, "attn_bwd", proj_mx, do, dproj, n_heads)
    ex.grads["w_in"] = _mm_tn(hn1, dproj, c, bm=bt, bk=d // 2, name="in_proj_dw")
    ex.alone("grads_w_in_swap")
    dhn1 = run(_mm_nt, "in_proj_dx", dproj, win3, bm=bm, bo=d // 2, out_dtype=F32)
    grad_x, small["pre_mix_norm"] = run(_first_bwd, "first_bwd", dhn1, dx2, x, g_pre_mix)
    ex.packs["late"] = _pack([small["pre_mix_norm"], (0.5 / d) * jnp.sum(loss_cols, keepdims=True)])
    return loss_cols, grad_x, small


def _into_slot(wsh, slot, dtype, name):
    rows, n = wsh.shape
    rb = _row_block(rows, 256) if rows % 8 == 0 else rows

    def body(s_ref, w_ref, o_ref):
        o_ref[...] = w_ref[...].astype(o_ref.dtype)

    return pl.pallas_call(
        body,
        grid_spec=pltpu.PrefetchScalarGridSpec(
            num_scalar_prefetch=1, grid=(rows // rb,),
            in_specs=[pl.BlockSpec((rb, n), lambda i, s_ref: (i, 0))],
            out_specs=pl.BlockSpec((None, rb, n), lambda i, s_ref: (s_ref[0], i, 0))),
        out_shape=S((4, rows, n), dtype), compiler_params=_cp("parallel"), name=name)(slot, wsh)


class _Exchange:
    SCHEDULE = {
        "in_proj": [("stream", "w_in"), ("ici", "conv_w"), ("ici", "w_ffn_gate", "x")],
        "attn_fwd": [("d2d", "w_ffn_gate", "x"), ("ici", "w_ffn_gate", "y"), ("ici", "w_ffn_up", "x")],
        "lru_fwd": [("d2d", "w_ffn_gate", "y"), ("d2d", "w_ffn_up", "x"), ("ici", "w_out"), ("ici", "w_ffn_up", "y")],
        "outnorm_fwd": [("d2d", "w_out"), ("d2d", "w_ffn_up", "y")],
        "out_proj": [("ici", "w_ffn_gate", "d")],
        "mid_fwd": [("d2d", "w_ffn_gate", "d"), ("ici", "w_ffn_up", "d")],
        "ffn_gate_up": [("ici", "w_ffn_down")],
        "gather_up_diag": [("d2d", "w_ffn_up", "d")],
        "ffn_gate_up_diag": [("d2d", "w_ffn_down")],
        "ffn_gate_dw": [("swap", "w_ffn_down")],
        "ffn_up_dw": [("scatter", "w_ffn_down", 0), ("scatter", "w_ffn_down", 1), ("scatter", "w_ffn_down", 2), ("swap", "w_ffn_gate")],
        "ffn_gate_dx": [("scatter", "w_ffn_down", 3), ("scatter", "w_ffn_gate", 0), ("scatter", "w_ffn_gate", 1), ("swap", "w_ffn_up")],
        "ffn_up_dx": [("share", "w_ffn_down"), ("scatter", "w_ffn_gate", 2), ("scatter", "w_ffn_gate", 3), ("scatter", "w_ffn_up", 0)],
        "mid_bwd": [("share", "w_ffn_gate"), ("scatter", "w_ffn_up", 1), ("scatter", "w_ffn_up", 2)],
        "out_proj_dx": [("scatter", "w_ffn_up", 3)],
        "outnorm_bwd": [("share", "w_ffn_up"), ("swap", "w_out")],
        "lru_bwd": [("scatter", "w_out")],
        "attn_bwd": [("share", "w_out"), ("spread", "early")],
        "grads_w_in_swap": [("swap", "w_in")],
        "in_proj_dx": [("scatter", "w_in")],
        "grads_w_in_share": [("share", "w_in"), ("spread", "late")],
    }
    PIECES = 4

    def __init__(self, slots, place):
        self.buf, self.place = dict(slots), place
        self.grads, self.packs, self.swapped, self.part, self.scattered, self.full, self.spreaded = {}, {}, {}, {}, {}, {}, {}

    def weight(self, name):
        b = self.buf[name]
        return b.reshape(-1, b.shape[2]) if name in ("w_out", "w_ffn_down") else b

    def in_proj(self, x, gain, bm):
        car = self.carrier("in_proj")
        out = _in_proj_streamed(x, gain, car, car.streamed, self.place, bm=bm, name="in_proj")
        self.harvest(car)
        return out

    def conv_w(self):
        return jnp.transpose(self.buf["conv_w"], (1, 0, 2)).reshape(CONV_WIDTH, -1)

    def carrier(self, call):
        if call not in self.SCHEDULE:
            return None
        car = _Carrier()
        car.todo, slot = [], {}
        for kind, name, *piece in self.SCHEDULE[call]:
            if kind in ("ici", "d2d", "stream"):
                if name not in slot:
                    slot[name] = car.inplace(self.buf[name])
                    car.todo.append((self.buf, name, slot[name]))
            if kind == "stream":
                car.streamed = slot[name]
            elif kind in ("ici", "d2d"):
                rel = tuple("xyd".index(ch) for ch in piece[0]) if piece else (0, 1, 2)
                if kind == "ici":
                    car.gather_ici(slot[name], split=name != "conv_w", rel=rel)
                else:
                    car.gather_d2d(slot[name], rel=rel)
            elif kind == "swap":
                g = self.grads[name]
                o = car.fresh((4, g.shape[1] // 2, g.shape[2]), F32)
                car.swap(car.read(g), o)
                car.todo.append((self.swapped, name, o))
            elif kind == "scatter":
                if name not in self.part:
                    self.part[name] = _add_own_half(self.grads[name], self.swapped[name], self.place[1:], "grads_add_" + name)
                p = self.part[name]
                key = ("scatter", name)
                if key not in slot:
                    slot[key] = (car.read(p), car.inplace(self.scattered[name]) if name in self.scattered else car.fresh(p.shape, p.dtype))
                    car.todo.append((self.scattered, name, slot[key][1]))
                size = p.shape[1] // self.PIECES
                car.scatter(*slot[key], (piece[0] * size, size) if piece else None)
            elif kind == "share":
                o = car.inplace(_sum_chips(self.part[name], self.scattered[name], self.place, "grads_sum_" + name))
                car.share(o)
                car.todo.append((self.full, name, o))
            else:
                o = car.fresh((8,) + self.packs[name].shape, F32)
                car.spread(car.read(self.packs[name]), o)
                car.todo.append((self.spreaded, name, o))
        return car

    def harvest(self, car):
        for state, name, o in (car.todo if car is not None else []):
            state[name] = car.results[o]

    def alone(self, call):
        car = self.carrier(call)
        car.run_alone(call)
        self.harvest(car)

    def small_sum(self, key):
        return _sum_devices(self.packs[key], self.spreaded[key], 2 * self.place[0:1] + self.place[1:], "grads_small_sum_" + key)


def _row_block(rows, cap):
    return max(b for b in range(8, cap + 1, 8) if rows % b == 0)


def _add_own_half(g, recv, core, name):
    _, rows, n = g.shape
    half = rows // 2
    rb = _row_block(half, 512)
    nb = half // rb

    def body(c_ref, g_ref, r_ref, o_ref):
        o_ref[...] = (g_ref[...] + r_ref[...]).astype(o_ref.dtype)

    return pl.pallas_call(
        body,
        grid_spec=pltpu.PrefetchScalarGridSpec(
            num_scalar_prefetch=1, grid=(4, nb),
            in_specs=[pl.BlockSpec((None, rb, n), lambda k, i, c_ref: (k, c_ref[0] * nb + i, 0)),
                      pl.BlockSpec((None, rb, n), lambda k, i, c_ref: (k, i, 0))],
            out_specs=pl.BlockSpec((None, rb, n), lambda k, i, c_ref: (k, i, 0))),
        out_shape=S((4, half, n), BF16), compiler_params=_cp("parallel", "parallel"), name=name)(core, g, recv)


def _sum_chips(part, recv, place, name):
    _, rows, n = part.shape
    rb = _row_block(rows, 64)
    nb = rows // rb

    def body(p_ref, own_ref, r0, r1, r2, r3, o_ref):
        own = own_ref[...].astype(F32)
        terms = [jnp.where(p_ref[0] == k, own, r[...].astype(F32)) for k, r in enumerate((r0, r1, r2, r3))]
        o_ref[...] = ((terms[0] + terms[1]) + terms[2]) + terms[3]

    def slot(k):
        return pl.BlockSpec((None, rb, n), lambda i, p_ref: (jnp.where(p_ref[0] == k, (k + 1) % 4, k), i, 0))

    return pl.pallas_call(
        body,
        grid_spec=pltpu.PrefetchScalarGridSpec(
            num_scalar_prefetch=1, grid=(nb,),
            in_specs=[pl.BlockSpec((None, rb, n), lambda i, p_ref: (p_ref[0], i, 0))] + [slot(k) for k in range(4)],
            out_specs=pl.BlockSpec((rb, n), lambda i, p_ref: (p_ref[1] * nb + i, 0))),
        out_shape=S((2 * rows, n), F32), compiler_params=_cp("parallel"), name=name)(place, part, recv, recv, recv, recv)


def _sum_devices(own, spread, me, name):
    rows = own.shape[0]

    def body(me_ref, own_ref, *refs):
        acc = None
        for k, r in enumerate(refs[:8]):
            term = jnp.where(me_ref[0] == k, own_ref[...], r[...])
            acc = term if acc is None else acc + term
        refs[8][...] = acc

    def slot(k):
        return pl.BlockSpec((None, rows, 128), lambda i, me_ref: (jnp.where(me_ref[0] == k, (k + 1) % 8, k), 0, 0))

    whole = pl.BlockSpec((rows, 128), lambda i, me_ref: (0, 0))
    return pl.pallas_call(
        body,
        grid_spec=pltpu.PrefetchScalarGridSpec(num_scalar_prefetch=1, grid=(1,), in_specs=[whole] + [slot(k) for k in range(8)],
                                               out_specs=whole),
        out_shape=S((rows, 128), F32), compiler_params=_cp("arbitrary"), name=name)(me, own, *[spread] * 8)


def _adamw(w, g, m, v, name, regive=False):
    rows, n = w.shape
    rb = rows if rows * n * 4 <= (1 << 21) else _row_block(rows, 256)
    c1 = 1.0 - ADAM_B1 ** ADAM_STEP
    c2 = 1.0 - ADAM_B2 ** ADAM_STEP

    def body(w_ref, g_ref, m_ref, v_ref, d_ref, nm_ref, nv_ref, *again):
        gv = g_ref[...]
        for ref in again:
            ref[...] = gv
        nm = ADAM_B1 * m_ref[...] + (1.0 - ADAM_B1) * gv
        nv = ADAM_B2 * v_ref[...] + (1.0 - ADAM_B2) * (gv * gv)
        nm_ref[...] = nm
        nv_ref[...] = nv
        d_ref[...] = -ADAM_LR * ((nm / c1) / (jnp.sqrt(nv / c2) + ADAM_EPS) + ADAM_WD * w_ref[...])

    bs = pl.BlockSpec((rb, n), lambda i: (i, 0))
    n_out = 4 if regive else 3
    return pl.pallas_call(body, grid=(rows // rb,), in_specs=[bs] * 4, out_specs=[bs] * n_out, out_shape=[S((rows, n), F32)] * n_out,
                          compiler_params=_cp("parallel"), name=name)(w, g, m, v)


_BIG = ("w_in", "w_out", "w_ffn_gate", "w_ffn_up", "w_ffn_down")
_SMALL = ("pre_mix_norm", "post_mix_norm", "pre_ffn_norm", "post_ffn_norm", "conv_w", "conv_b", "w_rgate", "b_rgate",
          "w_igate", "b_igate", "lru_lambda", "attn_out_norm", "lru_out_norm")
_SMALL_EARLY = _SMALL[1:]
_WEIGHTS = ("pre_mix_norm", "post_mix_norm", "pre_ffn_norm", "post_ffn_norm", "w_in", "conv_w", "conv_b", "w_rgate", "b_rgate",
            "w_igate", "b_igate", "lru_lambda", "attn_out_norm", "lru_out_norm", "w_out", "w_ffn_gate", "w_ffn_up", "w_ffn_down")


def _pack(arrays):
    flat = []
    for a in arrays:
        f = a.reshape(-1)
        flat.append(jnp.pad(f, (0, (-f.shape[0]) % 1024)))
    return jnp.concatenate(flat).reshape(-1, 128)


def _unpack(packed, shapes):
    out, pos = [], 0
    flat = packed.reshape(-1)
    for s in shapes:
        size = math.prod(s)
        out.append(flat[pos:pos + size].reshape(s))
        pos += size + (-size) % 1024
    return out


def kernel(x, pre_mix_norm, post_mix_norm, pre_ffn_norm, post_ffn_norm, w_in, conv_w, conv_b, w_rgate, b_rgate, w_igate, b_igate, lru_lambda, attn_out_norm, lru_out_norm, w_out, w_ffn_gate, w_ffn_up, w_ffn_down, loss_target, m_pre_mix_norm, m_post_mix_norm, m_pre_ffn_norm, m_post_ffn_norm, m_w_in, m_conv_w, m_conv_b, m_w_rgate, m_b_rgate, m_w_igate, m_b_igate, m_lru_lambda, m_attn_out_norm, m_lru_out_norm, m_w_out, m_w_ffn_gate, m_w_ffn_up, m_w_ffn_down, v_pre_mix_norm, v_post_mix_norm, v_pre_ffn_norm, v_post_ffn_norm, v_w_in, v_conv_w, v_conv_b, v_w_rgate, v_b_rgate, v_w_igate, v_b_igate, v_lru_lambda, v_attn_out_norm, v_lru_out_norm, v_w_out, v_w_ffn_gate, v_w_ffn_up, v_w_ffn_down):
    given = dict(locals())
    w = {n: given[n][0] for n in _WEIGHTS}
    m = {n: given["m_" + n][0] for n in _WEIGHTS}
    v = {n: given["v_" + n][0] for n in _WEIGHTS}
    xs, target = x[0], loss_target[0]
    d = xs.shape[1]
    chip = (2 * lax.axis_index("x") + lax.axis_index("y")).astype(jnp.int32)
    place = jnp.stack([chip, lax.axis_index("c").astype(jnp.int32)])

    slots = {n: _into_slot(w[n], place[0:1], _MXU, "slot_" + n) for n in _BIG}
    slots["conv_w"] = _into_slot(w["conv_w"], place[0:1], F32, "slot_conv_w")
    ex = _Exchange(slots, place)
    row = lambda a: a.reshape(1, -1)
    norms = tuple(row(w[n]) for n in ("pre_mix_norm", "post_mix_norm", "pre_ffn_norm", "post_ffn_norm"))

    loss_cols, grad_x, small = _local_step(
        xs, target, norms, ex, row(w["conv_b"]), w["w_rgate"], row(w["b_rgate"]),
        w["w_igate"], row(w["b_igate"]), row(w["lru_lambda"]), row(w["attn_out_norm"]), row(w["lru_out_norm"]))


    ex.alone("grads_w_in_share")
    reduced = {n: ex.full[n] for n in _BIG}
    early = _unpack(ex.small_sum("early"), [small[n].shape for n in _SMALL_EARLY])
    late = _unpack(ex.small_sum("late"), [small["pre_mix_norm"].shape, (1, 1)])
    loss = late[1][0, 0]
    for n, g in zip(_SMALL_EARLY + ("pre_mix_norm",), early + late[:1]):
        reduced[n] = g.reshape(w[n].shape) if n != "conv_w" else lax.dynamic_slice_in_dim(g, chip * w[n].shape[1], w[n].shape[1], axis=1)

    delta, new_m, new_v = {}, {}, {}
    for n in _BIG:
        delta[n], new_m[n], new_v[n], reduced[n] = _adamw(w[n], reduced[n], m[n], v[n], "adamw_" + n, regive=True)
    shapes = [w[n].shape for n in _SMALL]
    packed = _adamw(*[_pack([src[n] for n in _SMALL]) for src in (w, reduced, m, v)], "adamw_small")
    for out, p in zip((delta, new_m, new_v), packed):
        out.update(zip(_SMALL, _unpack(p, shapes)))

    lead = lambda a: a[None]
    return (loss, lead(grad_x), *[lead(reduced[n]) for n in _WEIGHTS], *[lead(delta[n]) for n in _WEIGHTS],
            *[lead(new_m[n]) for n in _WEIGHTS], *[lead(new_v[n]) for n in _WEIGHTS])
```

```python
import functools
import math

import jax
import jax.numpy as jnp
from jax import lax
from jax.experimental import pallas as pl
from jax.experimental.pallas import tpu as pltpu

F32 = jnp.float32
BF16 = jnp.bfloat16
_MXU = BF16
S = jax.ShapeDtypeStruct

RMS_EPS = 1e-6
HEAD_DIM = 128
CONV_WIDTH = 4
LRU_C = 8.0
ADAM_LR, ADAM_B1, ADAM_B2, ADAM_EPS, ADAM_WD, ADAM_STEP = 0.001, 0.9, 0.999, 1e-08, 0.01, 10
EXP_CUT = -105.0
VMEM_LIMIT = 60 * 1024 * 1024
ROW_TILE = 256
SEQ_TILE = 256
ATTN_BLOCK = 256
ATTN_HEADS = 2
MESH = pl.DeviceIdType.MESH


def _cp(*sem):
    return pltpu.CompilerParams(dimension_semantics=sem, vmem_limit_bytes=VMEM_LIMIT)


def _dot(a, b):
    return jnp.dot(a, b, preferred_element_type=F32)


def _dot_nt(a, b):
    return lax.dot_general(a, b, (((1,), (1,)), ((), ())), preferred_element_type=F32)


def _dot_tn(a, b):
    return lax.dot_general(a, b, (((0,), (0,)), ((), ())), preferred_element_type=F32)


def _rstd(v):
    return lax.rsqrt(jnp.mean(v * v, axis=-1, keepdims=True) + RMS_EPS)


def _rms_bwd(dn, vh, r, gain):
    dvh = dn * gain
    dv = r * (dvh - vh * jnp.mean(dvh * vh, axis=-1, keepdims=True))
    return dv, jnp.sum(dn * vh, axis=0, keepdims=True)


def _log_sigmoid(z):
    return jnp.minimum(z, 0.0) - jnp.log(1.0 + jnp.exp(-jnp.abs(z)))


def _expm1(v):
    small = v * (1.0 + v * (0.5 + v * (1.0 / 6.0 + v * (1.0 / 24.0 + v * (1.0 / 120.0)))))
    return jnp.where(jnp.abs(v) < 0.04, small, jnp.exp(v) - 1.0)


_GELU_C = math.sqrt(2.0 / math.pi)


def _gelu(v):
    return 0.5 * v * (1.0 + jnp.tanh(_GELU_C * (v + 0.044715 * v * v * v)))


def _gelu_grad(v):
    th = jnp.tanh(_GELU_C * (v + 0.044715 * v * v * v))
    return 0.5 * (1.0 + th) + 0.5 * v * (1.0 - th * th) * _GELU_C * (1.0 + 3.0 * 0.044715 * v * v)


def _row_spec(tm, d):
    return pl.BlockSpec((tm, d), lambda i: (i, 0))


def _vec_spec(d):
    return pl.BlockSpec((1, d), lambda i: (0, 0))


_ANY = pl.BlockSpec(memory_space=pl.ANY)


def _place():
    x, y, c = lax.axis_index("x"), lax.axis_index("y"), lax.axis_index("c")
    return x, y, c, [(1 - x, y), (x, 1 - y), (1 - x, 1 - y)]


def _remote(src, dst, send_sem, recv_sem, to):
    return pltpu.make_async_remote_copy(src_ref=src, dst_ref=dst, send_sem=send_sem, recv_sem=recv_sem,
                                        device_id=to, device_id_type=MESH)


class _Carrier:
    def __init__(self):
        self.inputs, self.out_shapes, self.aliases, self.ops, self.n_sems, self.results = [], [], {}, [], 0, None

    def inplace(self, arr):
        self.aliases[len(self.inputs)] = len(self.out_shapes)
        self.inputs.append(arr)
        self.out_shapes.append(S(arr.shape, arr.dtype))
        return len(self.out_shapes) - 1

    def read(self, arr):
        self.inputs.append(arr)
        return len(self.inputs) - 1

    def fresh(self, shape, dtype):
        self.out_shapes.append(S(shape, dtype))
        return len(self.out_shapes) - 1

    def _add(self, n_sems, copies):
        base = self.n_sems
        self.n_sems += n_sems

        def start(ins, outs, send, recv):
            for k, (src, dst, _, to) in enumerate(copies(ins, outs)):
                _remote(src, dst, send.at[base + k], recv.at[base + k], to).start()

        def finish(ins, outs, send, recv):
            for k, (src, _, land, to) in enumerate(copies(ins, outs)):
                _remote(src, land, send.at[base + k], recv.at[base + k], to).wait()

        self.ops.append((start, finish))

    def gather_ici(self, o, rows=None, split=True, rel=(0, 1, 2)):
        half = self.out_shapes[o].shape[1] // 2
        lo, size = rows or (0, half)

        def copies(ins, outs):
            x, y, c, chips = _place()
            part = (lambda ref: ref.at[pl.ds(c * half + lo, size)]) if split else (lambda ref: ref)
            mine = part(outs[o].at[2 * x + y])
            return [(mine, mine, part(outs[o].at[2 * px + py]), (px, py, c)) for px, py in (chips[j] for j in rel)]

        self._add(len(rel), copies)

    def gather_d2d(self, o, rows=None, rel=(0, 1, 2)):
        half = self.out_shapes[o].shape[1] // 2
        lo, size = rows or (0, half)

        def copies(ins, outs):
            x, y, c, chips = _place()
            at = lambda k, cc: outs[o].at[k].at[pl.ds(cc * half + lo, size)]
            return [(at(2 * px + py, c), at(2 * px + py, c), at(2 * px + py, 1 - c), (x, y, 1 - c)) for px, py in (chips[j] for j in rel)]

        self._add(len(rel), copies)

    def swap(self, i, o):
        half = self.inputs[i].shape[1] // 2

        def copies(ins, outs):
            x, y, c, _ = _place()
            return [(ins[i].at[:, pl.ds((1 - c) * half, half)], outs[o], outs[o], (x, y, 1 - c))]

        self._add(1, copies)

    def scatter(self, i, o, rows=None):
        lo, size = rows or (0, self.inputs[i].shape[1])

        def copies(ins, outs):
            x, y, c, chips = _place()
            cut = lambda ref: ref.at[pl.ds(lo, size)]
            return [(cut(ins[i].at[2 * px + py]), cut(outs[o].at[2 * x + y]), cut(outs[o].at[2 * px + py]), (px, py, c)) for px, py in chips]

        self._add(3, copies)

    def share(self, o):
        r = self.out_shapes[o].shape[0] // 2

        def copies(ins, outs):
            x, y, c, _ = _place()
            mine = outs[o].at[pl.ds(c * r, r)]
            return [(mine, mine, outs[o].at[pl.ds((1 - c) * r, r)], (x, y, 1 - c))]

        self._add(1, copies)

    def spread(self, i, o):
        def copies(ins, outs):
            x, y, c, _ = _place()
            me = 4 * x + 2 * y + c
            out = []
            for d in range(1, 8):
                to, frm = (me + d) % 8, (me + 8 - d) % 8
                out.append((ins[i], outs[o].at[me], outs[o].at[frm], (to // 4, (to // 2) % 2, to % 2)))
            return out

        self._add(7, copies)

    def _pallas(self, body, n_in, n_out, scratch, **kw):
        k_in, k_out = len(self.inputs), len(self.out_shapes)
        grid = kw.get("grid", ())
        prefetch = kw.get("prefetch", [])
        n_pre = len(prefetch)

        def wrapped(*refs):
            pre, refs = refs[:n_pre], refs[n_pre:]
            ins, cins = refs[:n_in], refs[n_in:n_in + k_in]
            outs = refs[n_in + k_in:n_in + k_in + n_out]
            couts = refs[n_in + k_in + n_out:n_in + k_in + n_out + k_out]
            own = refs[n_in + k_in + n_out + k_out:]
            send, recv = own[len(scratch):]
            ids = [pl.program_id(a) for a in range(len(grid))]
            first = functools.reduce(jnp.logical_and, [a == 0 for a in ids], True)
            last = functools.reduce(jnp.logical_and, [a == g - 1 for a, g in zip(ids, grid)], True)

            def go(stage):
                for op in self.ops:
                    op[stage](cins, couts, send, recv)

            if grid:
                pl.when(first)(lambda: go(0))
                body(*pre, *ins, *outs, *own[:len(scratch)])
                pl.when(last)(lambda: go(1))
            else:
                go(0)
                go(1)

        sem = pltpu.SemaphoreType.DMA((self.n_sems,))
        aliases = {n_pre + i: o for i, o in kw.get("aliases", {}).items()}
        aliases.update({n_pre + n_in + i: n_out + o for i, o in self.aliases.items()})
        return _pallas_call(
            wrapped, prefetch, in_specs=list(kw.get("in_specs", [])) + [_ANY] * k_in, out_specs=list(kw.get("out_specs", [])) + [_ANY] * k_out,
            out_shape=list(kw.get("out_shape", [])) + self.out_shapes, scratch_shapes=list(scratch) + [sem, sem],
            input_output_aliases=aliases, name=kw["name"],
            **({"grid": grid, "compiler_params": _cp(*["arbitrary"] * len(grid))} if grid else {}))

    def run(self, body, kw, *args):
        single = not isinstance(kw["out_shape"], (list, tuple))
        out_shape = [kw["out_shape"]] if single else list(kw["out_shape"])
        out_specs = [kw["out_specs"]] if single else list(kw["out_specs"])
        res = self._pallas(body, len(args), len(out_shape), kw.get("scratch_shapes", []), grid=kw["grid"], in_specs=kw["in_specs"],
                           out_specs=out_specs, out_shape=out_shape, name=kw["name"], prefetch=kw.get("prefetch", []),
                           aliases=kw.get("input_output_aliases", {}))(*args, *self.inputs)
        self.results = list(res[len(out_shape):])
        return res[0] if single else list(res[:len(out_shape)])

    def run_alone(self, name):
        self.results = list(self._pallas(None, 0, 0, [], name=name)(*self.inputs))


def _pallas_call(body, prefetch, **kw):
    if not prefetch:
        return pl.pallas_call(body, **kw)
    spec = pltpu.PrefetchScalarGridSpec(num_scalar_prefetch=len(prefetch), grid=kw.pop("grid"), in_specs=kw.pop("in_specs"),
                                        out_specs=kw.pop("out_specs"), scratch_shapes=kw.pop("scratch_shapes", []))
    fn = pl.pallas_call(body, grid_spec=spec, **kw)
    return lambda *args: fn(*prefetch, *args)


def _call(comm, body, prefetch=(), **kw):
    if comm is None:
        kw["input_output_aliases"] = {len(prefetch) + i: o for i, o in kw.get("input_output_aliases", {}).items()}
        return _pallas_call(body, list(prefetch), **kw)
    return functools.partial(comm.run, body, dict(kw, prefetch=list(prefetch)))


def _in_proj_streamed(x, gain, car, o_w, place, *, bm, name):
    m, k = x.shape
    n = car.out_shapes[o_w].shape[2]
    ni, half = m // bm, k // 2
    k_in, k_out = len(car.inputs), len(car.out_shapes)
    order = lambda p: ((p & 1) << 1) | (p >> 1)

    def body(place_ref, x_ref, g_ref, *refs):
        cins, (hn_ref, o_ref, ob_ref), couts = refs[:k_in], refs[k_in:k_in + 3], refs[k_in + 3:k_in + 3 + k_out]
        wbuf, local, ici_send, ici_recv, d2d_send, d2d_recv, send, recv = refs[k_in + 3 + k_out:]
        p, i = pl.program_id(0), pl.program_id(1)
        x, y, c, chips = _place()
        me = 2 * x + y
        rows = lambda chunk, cc: couts[o_w].at[chunk].at[pl.ds(cc * half, half)]

        @pl.when(jnp.logical_and(p == 0, i == 0))
        def _():
            for j, (px, py) in enumerate(chips):
                _remote(rows(me, c), rows(me, c), ici_send.at[j], ici_recv.at[j], (px, py, c)).start()
            for op in car.ops:
                op[0](cins, couts, send, recv)

        for j, (px, py) in enumerate(chips):
            @pl.when(jnp.logical_and(p == j + 1, i == 0))
            def _(j=j, px=px, py=py):
                landed, other = rows(2 * px + py, c), rows(2 * px + py, 1 - c)
                _remote(landed, landed, ici_send.at[j], ici_recv.at[j], (px, py, c)).wait_recv()
                _remote(landed, landed, d2d_send.at[j], d2d_recv.at[j], (x, y, 1 - c)).start()
                _remote(other, other, d2d_send.at[j], d2d_recv.at[j], (x, y, 1 - c)).wait_recv()

        @pl.when(i == 0)
        def _():
            cp = pltpu.make_async_copy(couts[o_w].at[me ^ order(p)], wbuf, local.at[0])
            cp.start()
            cp.wait()

        xv = x_ref[...]
        hn = ((xv * _rstd(xv)) * g_ref[...]).astype(_MXU)
        hn_ref[...] = hn
        res = _dot(hn, wbuf[...])
        o_ref[...] = res
        ob_ref[...] = res.astype(ob_ref.dtype)

        @pl.when(jnp.logical_and(p == 3, i == ni - 1))
        def _():
            for j, (px, py) in enumerate(chips):
                _remote(rows(me, c), rows(me, c), ici_send.at[j], ici_recv.at[j], (px, py, c)).wait_send()
                _remote(rows(me, c), rows(me, c), d2d_send.at[j], d2d_recv.at[j], (x, y, 1 - c)).wait_send()
            for op in car.ops:
                op[1](cins, couts, send, recv)

    ospec = pl.BlockSpec((bm, n), lambda p, i, place_ref: (i, place_ref[0] ^ order(p)))
    rows = pl.BlockSpec((bm, k), lambda p, i, place_ref: (i, 0))
    three, sems = pltpu.SemaphoreType.DMA((3,)), pltpu.SemaphoreType.DMA((max(car.n_sems, 1),))
    res = pl.pallas_call(
        body,
        grid_spec=pltpu.PrefetchScalarGridSpec(
            num_scalar_prefetch=1, grid=(4, ni),
            in_specs=[rows, pl.BlockSpec((1, k), lambda p, i, place_ref: (0, 0))] + [_ANY] * k_in,
            out_specs=[pl.BlockSpec((bm, k), lambda p, i, place_ref: (p * ni + i, 0)), ospec, ospec] + [_ANY] * k_out,
            scratch_shapes=[pltpu.VMEM((k, n), _MXU), pltpu.SemaphoreType.DMA((1,)), three, three, three, three, sems, sems]),
        out_shape=[S((4 * m, k), _MXU), S((m, 4 * n), F32), S((m, 4 * n), _MXU)] + car.out_shapes,
        input_output_aliases={3 + a: 3 + o for a, o in car.aliases.items()},
        compiler_params=_cp("arbitrary", "arbitrary"), name=name)(place, x, gain, *car.inputs)
    car.results = list(res[3:])
    return res[0], res[1], res[2]


def _mm_nn(a, b3, *, bm, bn, name, also=None, comm=None):
    m, k = a.shape
    c, _, n = b3.shape
    ni, nj = m // bm, n // bn

    def body(a_ref, b_ref, *o_refs):
        res = _dot(a_ref[...], b_ref[...])
        for o_ref in o_refs:
            o_ref[...] = res.astype(o_ref.dtype)

    ospec = pl.BlockSpec((bm, bn), lambda cc, j, i: (i, cc * nj + j))
    dtypes = [F32] + ([] if also is None else [also])
    out = _call(
        comm, body, grid=(c, nj, ni),
        in_specs=[pl.BlockSpec((bm, k), lambda cc, j, i: (i, 0)), pl.BlockSpec((None, k, bn), lambda cc, j, i: (cc, 0, j))],
        out_specs=[ospec] * len(dtypes), out_shape=[S((m, c * n), dt) for dt in dtypes],
        compiler_params=_cp("parallel", "parallel", "parallel"), name=name)(a, b3)
    return out[0] if also is None else out


def _mm_nt(a, b3, *, bm, bo, out_dtype, name, comm=None):
    m = a.shape[0]
    c, ko, n = b3.shape
    ni, nj = m // bm, ko // bo

    def body(a_ref, b_ref, o_ref):
        acc = _dot_nt(a_ref[:, 0:n], b_ref[0])
        for cc in range(1, c):
            acc = acc + _dot_nt(a_ref[:, cc * n:(cc + 1) * n], b_ref[cc])
        o_ref[...] = acc.astype(o_ref.dtype)

    return _call(
        comm, body, grid=(nj, ni),
        in_specs=[pl.BlockSpec((bm, c * n), lambda j, i: (i, 0)),
                  pl.BlockSpec((c, bo, n), lambda j, i: (0, j, 0))],
        out_specs=pl.BlockSpec((bm, bo), lambda j, i: (i, j)),
        out_shape=S((m, ko), out_dtype),
        compiler_params=_cp("parallel", "parallel"), name=name)(a, b3)


def _mm_tn(a, b, c, *, bm, bk, name, comm=None):
    m, k = b.shape[0], a.shape[1]
    n = b.shape[1] // c
    nm, nk = m // bm, k // bk

    def body(a_ref, b_ref, o_ref, acc):
        mm = pl.program_id(2)

        @pl.when(mm == 0)
        def _():
            acc[...] = jnp.zeros_like(acc)

        acc[...] += _dot_tn(a_ref[...], b_ref[...])

        @pl.when(mm == nm - 1)
        def _():
            o_ref[...] = acc[...]

    return _call(
        comm, body, grid=(c, nk, nm),
        in_specs=[pl.BlockSpec((bm, bk), lambda cc, j, mm: (mm, j)),
                  pl.BlockSpec((bm, n), lambda cc, j, mm: (mm, cc))],
        out_specs=pl.BlockSpec((None, bk, n), lambda cc, j, mm: (cc, j, 0)),
        out_shape=S((c, k, n), F32),
        scratch_shapes=[pltpu.VMEM((bk, n), F32)],
        compiler_params=_cp("parallel", "parallel", "arbitrary"), name=name)(a, b)


def _chunk_order(p):
    return ((p & 1) << 1) | (p >> 1)


def _swiglu_fwd(hn, wg3, wu3, place, span, into, *, bm, name, comm=None):
    m, k = hn.shape
    c, _, n = wg3.shape
    chunk = lambda p, place_ref: place_ref[0] ^ _chunk_order(p + span[0])

    def body(place_ref, a_ref, g_ref, u_ref, *refs):
        dgate_ref, dup_ref, act_ref = refs[-3:]
        a = a_ref[...]
        gate = _dot(a, g_ref[...])
        up = _dot(a, u_ref[...])
        sg = jax.nn.sigmoid(gate)
        silu = gate * sg
        dgate_ref[...] = (up * (sg * (1.0 + gate * (1.0 - sg)))).astype(dgate_ref.dtype)
        dup_ref[...] = silu.astype(dup_ref.dtype)
        act_ref[...] = (silu * up).astype(act_ref.dtype)

    wspec = pl.BlockSpec((None, k, n), lambda p, i, place_ref: (chunk(p, place_ref), 0, 0))
    ospec = pl.BlockSpec((bm, n), lambda p, i, place_ref: (i, chunk(p, place_ref)))
    given = list(into) if into is not None else []
    return _call(
        comm, body, prefetch=[place], grid=(span[1] - span[0], m // bm),
        in_specs=[pl.BlockSpec((bm, k), lambda p, i, place_ref: (i, 0)), wspec, wspec] + [_ANY] * len(given),
        out_specs=[ospec, ospec, ospec],
        out_shape=[S((m, c * n), _MXU), S((m, c * n), _MXU), S((m, c * n), _MXU)],
        input_output_aliases={3 + a: a for a in range(len(given))},
        compiler_params=_cp("arbitrary", "arbitrary"), name=name)(hn, wg3, wu3, *given)


def _swiglu_bwd(df, wd, act_dgate, act_dup, *, bm, bo, name):
    m, k = df.shape
    ko = wd.shape[0]

    def body(a_ref, b_ref, g_ref, u_ref, dg_ref, du_ref):
        dact = _dot_nt(a_ref[...], b_ref[...])
        dg_ref[...] = (dact * g_ref[...].astype(F32)).astype(dg_ref.dtype)
        du_ref[...] = (dact * u_ref[...].astype(F32)).astype(du_ref.dtype)

    ospec = pl.BlockSpec((bm, bo), lambda j, i: (i, j))
    return pl.pallas_call(
        body, grid=(ko // bo, m // bm),
        in_specs=[pl.BlockSpec((bm, k), lambda j, i: (i, 0)), pl.BlockSpec((bo, k), lambda j, i: (j, 0)), ospec, ospec],
        out_specs=[ospec, ospec],
        out_shape=[S((m, ko), _MXU), S((m, ko), _MXU)],
        compiler_params=_cp("parallel", "parallel"), name=name)(df, wd, act_dgate, act_dup)


def _rms_fwd(x, gain, name, comm=None):
    t, d = x.shape
    tm = min(t, ROW_TILE)

    def body(x_ref, g_ref, o_ref):
        xv = x_ref[...]
        o_ref[...] = ((xv * _rstd(xv)) * g_ref[...]).astype(o_ref.dtype)

    return _call(comm, body, grid=(t // tm,), in_specs=[_row_spec(tm, d), _vec_spec(d)], out_specs=_row_spec(tm, d),
                          out_shape=S((t, d), _MXU), compiler_params=_cp("parallel"), name=name)(x, gain)


def _outnorm_fwd(o, yl, ga, gl, name, comm=None):
    t, w = o.shape
    tm = min(t, ROW_TILE)

    def body(o_ref, l_ref, ga_ref, gl_ref, y_ref):
        ov, lv = o_ref[...], l_ref[...]
        y_ref[:, :w] = ((ov * _rstd(ov)) * ga_ref[...]).astype(y_ref.dtype)
        y_ref[:, w:] = ((lv * _rstd(lv)) * gl_ref[...]).astype(y_ref.dtype)

    return _call(comm, body, grid=(t // tm,), in_specs=[_row_spec(tm, w), _row_spec(tm, w), _vec_spec(w), _vec_spec(w)],
                 out_specs=_row_spec(tm, 2 * w), out_shape=S((t, 2 * w), _MXU),
                 compiler_params=_cp("parallel"), name=name)(o, yl, ga, gl)


def _mid_fwd(x, mix, g_post, g_pre, name, comm=None):
    t, d = x.shape
    tm = min(t, ROW_TILE)

    def body(x_ref, m_ref, gp_ref, gn_ref, x2_ref, hn_ref):
        mv = m_ref[...]
        x2 = x_ref[...] + (mv * _rstd(mv)) * gp_ref[...]
        x2_ref[...] = x2
        hn_ref[...] = ((x2 * _rstd(x2)) * gn_ref[...]).astype(hn_ref.dtype)

    return _call(comm, body, grid=(t // tm,), in_specs=[_row_spec(tm, d), _row_spec(tm, d), _vec_spec(d), _vec_spec(d)],
                          out_specs=[_row_spec(tm, d), _row_spec(tm, d)], out_shape=[S((t, d), F32), S((t, d), _MXU)],
                          compiler_params=_cp("parallel"), name=name)(x, mix, g_post, g_pre)


def _final(f, x2, target, g_post, name):
    t, d = f.shape
    tm = min(t, ROW_TILE)

    def body(f_ref, x2_ref, t_ref, g_ref, loss_ref, dout_ref, df_ref, dg_ref):
        @pl.when(pl.program_id(0) == 0)
        def _():
            loss_ref[...] = jnp.zeros_like(loss_ref)
            dg_ref[...] = jnp.zeros_like(dg_ref)

        fv = f_ref[...]
        r = _rstd(fv)
        fh = fv * r
        err = (x2_ref[...] + fh * g_ref[...]) - t_ref[...]
        loss_ref[...] += jnp.sum(err * err, axis=0, keepdims=True)
        dout = err * (1.0 / d)
        dout_ref[...] = dout
        dfv, dg = _rms_bwd(dout, fh, r, g_ref[...])
        df_ref[...] = dfv.astype(df_ref.dtype)
        dg_ref[...] += dg

    return pl.pallas_call(
        body, grid=(t // tm,),
        in_specs=[_row_spec(tm, d), _row_spec(tm, d), _row_spec(tm, d), _vec_spec(d)],
        out_specs=[_vec_spec(d), _row_spec(tm, d), _row_spec(tm, d), _vec_spec(d)],
        out_shape=[S((1, d), F32), S((t, d), F32), S((t, d), _MXU), S((1, d), F32)],
        compiler_params=_cp("arbitrary"), name=name)(f, x2, target, g_post)


def _mid_bwd(dhn_a, dhn_b, dout, x2, mix, g_pre, g_post, name, comm=None):
    t, d = x2.shape
    tm = min(t, ROW_TILE)

    def body(da_ref, db_ref, do_ref, x2_ref, m_ref, gn_ref, gp_ref, dx2_ref, dm_ref, dgn_ref, dgp_ref):
        @pl.when(pl.program_id(0) == 0)
        def _():
            dgn_ref[...] = jnp.zeros_like(dgn_ref)
            dgp_ref[...] = jnp.zeros_like(dgp_ref)

        x2 = x2_ref[...]
        r = _rstd(x2)
        dxa, dgn = _rms_bwd(da_ref[...] + db_ref[...], x2 * r, r, gn_ref[...])
        dx2 = do_ref[...] + dxa
        dx2_ref[...] = dx2
        dgn_ref[...] += dgn
        mv = m_ref[...]
        rm = _rstd(mv)
        dmv, dgp = _rms_bwd(dx2, mv * rm, rm, gp_ref[...])
        dm_ref[...] = dmv.astype(dm_ref.dtype)
        dgp_ref[...] += dgp

    rs, vs = _row_spec(tm, d), _vec_spec(d)
    return _call(
        comm, body, grid=(t // tm,), in_specs=[rs, rs, rs, rs, rs, vs, vs], out_specs=[rs, rs, vs, vs],
        out_shape=[S((t, d), F32), S((t, d), _MXU), S((1, d), F32), S((1, d), F32)],
        compiler_params=_cp("arbitrary"), name=name)(dhn_a, dhn_b, dout, x2, mix, g_pre, g_post)


def _first_bwd(dhn, dx2, x, gain, name, comm=None):
    t, d = x.shape
    tm = min(t, ROW_TILE)

    def body(dh_ref, dx2_ref, x_ref, g_ref, dx_ref, dg_ref):
        @pl.when(pl.program_id(0) == 0)
        def _():
            dg_ref[...] = jnp.zeros_like(dg_ref)

        xv = x_ref[...]
        r = _rstd(xv)
        dxa, dg = _rms_bwd(dh_ref[...], xv * r, r, g_ref[...])
        dx_ref[...] = dx2_ref[...] + dxa
        dg_ref[...] += dg

    rs, vs = _row_spec(tm, d), _vec_spec(d)
    return _call(comm, body, grid=(t // tm,), in_specs=[rs, rs, rs, vs], out_specs=[rs, vs],
                          out_shape=[S((t, d), F32), S((1, d), F32)], compiler_params=_cp("arbitrary"), name=name)(dhn, dx2, x, gain)


def _outnorm_bwd(dy, o, yl, ga, gl, name, comm=None):
    t, w = o.shape
    tm = min(t, ROW_TILE)

    def body(dy_ref, o_ref, l_ref, ga_ref, gl_ref, do_ref, dl_ref, dga_ref, dgl_ref):
        @pl.when(pl.program_id(0) == 0)
        def _():
            dga_ref[...] = jnp.zeros_like(dga_ref)
            dgl_ref[...] = jnp.zeros_like(dgl_ref)

        ov, lv = o_ref[...], l_ref[...]
        ra, rl = _rstd(ov), _rstd(lv)
        dov, dga = _rms_bwd(dy_ref[:, :w], ov * ra, ra, ga_ref[...])
        dlv, dgl = _rms_bwd(dy_ref[:, w:], lv * rl, rl, gl_ref[...])
        do_ref[...] = dov.astype(do_ref.dtype)
        dl_ref[...] = dlv
        dga_ref[...] += dga
        dgl_ref[...] += dgl

    rs, vs = _row_spec(tm, w), _vec_spec(w)
    return _call(comm, body, grid=(t // tm,), in_specs=[_row_spec(tm, 2 * w), rs, rs, vs, vs], out_specs=[rs, rs, vs, vs],
                          out_shape=[S((t, w), _MXU), S((t, w), F32), S((1, w), F32), S((1, w), F32)],
                          compiler_params=_cp("arbitrary"), name=name)(dy, o, yl, ga, gl)


def _tri_sum(v, tri):
    return _dot(v.astype(_MXU), tri)


def _attn_tile(qb, kb, row, col, shift, scale):
    z = _dot_nt(qb, kb) * scale
    mask = (col + shift) < row
    lb = _log_sigmoid(z)
    lm = jnp.where(mask, lb - z, 0.0)
    return mask, lb, lm


def _attn_fwd(proj, n_heads, name, comm=None):
    t = proj.shape[0]
    bq = min(t, ATTN_BLOCK)
    nq = t // bq
    scale = 1.0 / math.sqrt(HEAD_DIM)

    heads = [slice(a * HEAD_DIM, (a + 1) * HEAD_DIM) for a in range(ATTN_HEADS)]

    def body(q_ref, k_ref, v_ref, o_ref):
        row = lax.broadcasted_iota(jnp.int32, (bq, bq), 0)
        col = lax.broadcasted_iota(jnp.int32, (bq, bq), 1)
        tri = (row > col).astype(_MXU)

        def per_q(qi, _):
            q0 = pl.multiple_of(qi * bq, bq)
            qbs = [q_ref[pl.ds(q0, bq), hd] for hd in heads]

            def cond(st):
                return jnp.logical_and(st[0] >= 0, st[1])

            def step(st):
                kj, _, carries, accs = st
                k0 = pl.multiple_of(kj * bq, bq)
                alive, new_carries, new_accs = None, [], []
                for hd, qb, carry, acc in zip(heads, qbs, carries, accs):
                    mask, lb, lm = _attn_tile(qb, k_ref[pl.ds(k0, bq), hd], row, col, (kj - qi) * bq, scale)
                    w = jnp.where(mask, jnp.exp(lb + _tri_sum(lm, tri) + carry), 0.0)
                    new_accs.append(acc + _dot(w.astype(_MXU), v_ref[pl.ds(k0, bq), hd]))
                    carry = carry + jnp.sum(lm, axis=1, keepdims=True)
                    new_carries.append(carry)
                    live = jnp.max(carry) > EXP_CUT
                    alive = live if alive is None else jnp.logical_or(alive, live)
                return kj - 1, alive, tuple(new_carries), tuple(new_accs)

            st = lax.while_loop(cond, step, (qi, jnp.bool_(True), (jnp.zeros((bq, 1), F32),) * ATTN_HEADS,
                                             (jnp.zeros((bq, HEAD_DIM), F32),) * ATTN_HEADS))
            for hd, acc in zip(heads, st[3]):
                o_ref[pl.ds(q0, bq), hd] = acc
            return 0

        lax.fori_loop(0, nq, per_q, 0)

    groups = n_heads // ATTN_HEADS
    hs = lambda off: pl.BlockSpec((t, ATTN_HEADS * HEAD_DIM), lambda h: (0, off + h))
    return _call(
        comm, body, grid=(groups,), in_specs=[hs(0), hs(groups), hs(2 * groups)], out_specs=hs(0),
        out_shape=S((t, n_heads * HEAD_DIM), F32), compiler_params=_cp("parallel"), name=name)(proj, proj, proj)


def _emit(blocks, out_ref, starts, sems):
    copies = [pltpu.make_async_copy(b, out_ref.at[:, pl.ds(c0, b.shape[1])], sems.at[k]) for k, (b, c0) in enumerate(zip(blocks, starts))]
    for cp in copies:
        cp.start()
    for cp in copies:
        cp.wait()


def _attn_bwd(proj, do, dproj, n_heads, name, comm=None):
    t = proj.shape[0]
    bq = min(t, ATTN_BLOCK)
    nq = t // bq
    scale = 1.0 / math.sqrt(HEAD_DIM)
    groups = n_heads // ATTN_HEADS
    wide = ATTN_HEADS * HEAD_DIM

    heads = [slice(a * HEAD_DIM, (a + 1) * HEAD_DIM) for a in range(ATTN_HEADS)]

    def body(q_ref, k_ref, v_ref, do_ref, _, dproj_ref, dka_ref, dva_ref, g_ref, b_ref, dq_ref, dk_ref, dv_ref, out_sems):
        group = pl.program_id(0)
        dka_ref[...] = jnp.zeros_like(dka_ref)
        dva_ref[...] = jnp.zeros_like(dva_ref)
        row = lax.broadcasted_iota(jnp.int32, (bq, bq), 0)
        col = lax.broadcasted_iota(jnp.int32, (bq, bq), 1)
        tri = (row > col).astype(_MXU)
        tri_lt = (row < col).astype(_MXU)

        def per_q(qi, _):
            q0 = pl.multiple_of(qi * bq, bq)
            qbs = [q_ref[pl.ds(q0, bq), hd] for hd in heads]
            dobs = [do_ref[pl.ds(q0, bq), hd] for hd in heads]

            def cond(st):
                return jnp.logical_and(st[0] >= 0, st[1])

            def step(st):
                kj, _, carries = st
                k0 = pl.multiple_of(kj * bq, bq)
                alive, new_carries = None, []
                for a, (hd, qb, dob, carry) in enumerate(zip(heads, qbs, dobs, carries)):
                    mask, lb, lm = _attn_tile(qb, k_ref[pl.ds(k0, bq), hd], row, col, (kj - qi) * bq, scale)
                    w = jnp.where(mask, jnp.exp(lb + _tri_sum(lm, tri) + carry), 0.0)
                    g_ref[a, pl.ds(k0, bq), :] = w * _dot_nt(dob, v_ref[pl.ds(k0, bq), hd])
                    b_ref[a, pl.ds(k0, bq), :] = jnp.where(mask, jnp.exp(lb), 0.0)
                    dva_ref[pl.ds(k0, bq), hd] += _dot_tn(w.astype(_MXU), dob)
                    carry = carry + jnp.sum(lm, axis=1, keepdims=True)
                    new_carries.append(carry)
                    live = jnp.max(carry) > EXP_CUT
                    alive = live if alive is None else jnp.logical_or(alive, live)
                return kj - 1, alive, tuple(new_carries)

            st = lax.while_loop(cond, step, (qi, jnp.bool_(True), (jnp.zeros((bq, 1), F32),) * ATTN_HEADS))

            def back(kj, st2):
                k0 = pl.multiple_of(kj * bq, bq)
                out = []
                for a, (hd, qb, (before, dq)) in enumerate(zip(heads, qbs, st2)):
                    g = g_ref[a, pl.ds(k0, bq), :]
                    beta = b_ref[a, pl.ds(k0, bq), :]
                    dz = ((g * (1.0 - beta) - (before + _tri_sum(g, tri_lt)) * beta) * scale).astype(_MXU)
                    dka_ref[pl.ds(k0, bq), hd] += _dot_tn(dz, qb)
                    out.append((before + jnp.sum(g, axis=1, keepdims=True), dq + _dot(dz, k_ref[pl.ds(k0, bq), hd])))
                return tuple(out)

            st2 = lax.fori_loop(st[0] + 1, qi + 1, back, ((jnp.zeros((bq, 1), F32), jnp.zeros((bq, HEAD_DIM), F32)),) * ATTN_HEADS)
            for hd, (_, dq) in zip(heads, st2):
                dq_ref[pl.ds(q0, bq), hd] = dq.astype(dq_ref.dtype)
            return 0

        lax.fori_loop(0, nq, per_q, 0)
        dk_ref[...] = dka_ref[...].astype(dk_ref.dtype)
        dv_ref[...] = dva_ref[...].astype(dv_ref.dtype)
        _emit([dq_ref, dk_ref, dv_ref], dproj_ref, [(a * groups + group) * wide for a in range(3)], out_sems)

    hs = lambda off: pl.BlockSpec((t, wide), lambda h: (0, off + h))
    return _call(
        comm, body, grid=(groups,), in_specs=[hs(0), hs(groups), hs(2 * groups), hs(0), _ANY], out_specs=_ANY,
        out_shape=S(dproj.shape, dproj.dtype), input_output_aliases={4: 0},
        scratch_shapes=[pltpu.VMEM((t, wide), F32), pltpu.VMEM((t, wide), F32),
                        pltpu.VMEM((ATTN_HEADS, t, bq), F32), pltpu.VMEM((ATTN_HEADS, t, bq), F32)]
        + [pltpu.VMEM((t, wide), dproj.dtype)] * 3 + [pltpu.SemaphoreType.DMA((3,))],
        compiler_params=_cp("parallel"), name=name)(proj, proj, proj, do, dproj)


def _shift_down(cur, prev8, k):
    if k == 0:
        return cur
    row8 = lax.broadcasted_iota(jnp.int32, prev8.shape, 0)
    rc = pltpu.roll(cur, k, 0)
    top = jnp.where(row8 < k, pltpu.roll(prev8, k, 0), rc[0:8, :])
    return jnp.concatenate([top, rc[8:, :]], axis=0)


def _shift_up(cur, next8, k):
    if k == 0:
        return cur
    n = cur.shape[0]
    row8 = lax.broadcasted_iota(jnp.int32, next8.shape, 0)
    rc = pltpu.roll(cur, n - k, 0)
    bottom = jnp.where(row8 >= 8 - k, pltpu.roll(next8, 8 - k, 0), rc[n - 8:, :])
    return jnp.concatenate([rc[:n - 8, :], bottom], axis=0)


def _lru_conv(xl, prev8, cw, cb):
    xs = [_shift_down(xl, prev8, CONV_WIDTH - 1 - k) for k in range(CONV_WIDTH)]
    xc = xs[0] * cw[0:1, :]
    for k in range(1, CONV_WIDTH):
        xc = xc + xs[k] * cw[k:k + 1, :]
    return xs, xc + cb


def _lru_gates(xl, prev8, cw, cb, wr, br, wi, bi, ls):
    xs, xc = _lru_conv(xl, prev8, cw, cb)
    xcb = xc.astype(_MXU)
    r = jax.nn.sigmoid(_dot(xcb, wr) + br)
    i = jax.nn.sigmoid(_dot(xcb, wi) + bi)
    la = (LRU_C * r) * ls
    a = jnp.exp(la)
    mult = jnp.sqrt(-_expm1(2.0 * la))
    return xs, xc, r, i, a, mult


def _group_scan(a, b, reverse):
    n = a.shape[0]
    row = lax.broadcasted_iota(jnp.int32, a.shape, 0) % 8
    for d in (1, 2, 4):
        if reverse:
            m = row < 8 - d
            a_s, b_s = pltpu.roll(a, n - d, 0), pltpu.roll(b, n - d, 0)
        else:
            m = row >= d
            a_s, b_s = pltpu.roll(a, d, 0), pltpu.roll(b, d, 0)
        b = jnp.where(m, a * b_s + b, b)
        a = jnp.where(m, a * a_s, a)
    return a, b


def _lru_fwd(proj, col0, n_blocks, cw, cb, wr, br, wi, bi, lam, name, comm=None):
    t = proj.shape[0]
    tt = min(t, SEQ_TILE)
    nt = t // tt

    def body(xl_ref, gl_ref, cw_ref, cb_ref, wr_ref, br_ref, wi_ref, bi_ref, lam_ref, h_ref, y_ref, *kept):
        cwv, cbv, brv, biv = cw_ref[...], cb_ref[...], br_ref[...], bi_ref[...]
        wrv, wiv = wr_ref[...].astype(_MXU), wi_ref[...].astype(_MXU)
        ls = _log_sigmoid(lam_ref[...])

        def tile(ti, hin):
            t0 = pl.multiple_of(ti * tt, tt)
            p0 = pl.multiple_of(jnp.maximum(t0 - 8, 0), 8)
            prev8 = xl_ref[pl.ds(p0, 8), :] * (ti > 0).astype(F32)
            xl = xl_ref[pl.ds(t0, tt), :]
            _, xc, r, ig, a, mult = _lru_gates(xl, prev8, cwv, cbv, wrv, brv, wiv, biv, ls)
            for ref, val in zip(kept, (r, ig, a, mult)):
                ref[pl.ds(t0, tt), :] = val
            ga, gb = _group_scan(a, mult * (ig * xc), False)
            for g in range(tt // 8):
                hg = ga[8 * g:8 * g + 8, :] * hin + gb[8 * g:8 * g + 8, :]
                h_ref[pl.ds(t0 + 8 * g, 8), :] = hg
                hin = hg[7:8, :]
            y_ref[pl.ds(t0, tt), :] = h_ref[pl.ds(t0, tt), :] * _gelu(gl_ref[pl.ds(t0, tt), :])
            return hin

        lax.fori_loop(0, nt, tile, jnp.zeros((1, HEAD_DIM), F32))

    cs = lambda off: pl.BlockSpec((t, HEAD_DIM), lambda n: (0, off + n))
    vs = pl.BlockSpec((1, HEAD_DIM), lambda n: (0, n))
    ws = pl.BlockSpec((None, HEAD_DIM, HEAD_DIM), lambda n: (n, 0, 0))
    w = n_blocks * HEAD_DIM
    return _call(
        comm, body, grid=(n_blocks,),
        in_specs=[cs(col0), cs(col0 + n_blocks), pl.BlockSpec((CONV_WIDTH, HEAD_DIM), lambda n: (0, n)), vs, ws, vs, ws, vs, vs],
        out_specs=[cs(0)] * 6, out_shape=[S((t, w), F32)] * 6,
        compiler_params=_cp("parallel"), name=name)(proj, proj, cw, cb, wr, br, wi, bi, lam)


def _lru_bwd(proj, col0, n_blocks, h, kept, dyl, cw, cb, wr, wi, lam, name, comm=None):
    t = proj.shape[0]
    tt = min(t, SEQ_TILE)
    nt = t // tt

    def body(xl_ref, gl_ref, h_ref, r_ref, i_ref, a_ref, m_ref, dy_ref, cw_ref, cb_ref, wr_ref, wi_ref, lam_ref,
             dproj_ref, dcw_ref, dcb_ref, dwr_ref, dbr_ref, dwi_ref, dbi_ref, dlam_ref, g_ref, dxl_ref, dgl_ref, out_sems):
        block = pl.program_id(0)
        cwv, cbv = cw_ref[...], cb_ref[...]
        wrv, wiv = wr_ref[...].astype(_MXU), wi_ref[...].astype(_MXU)
        lamv = lam_ref[...]
        ls = _log_sigmoid(lamv)
        for ref in (dcw_ref, dcb_ref, dwr_ref, dbr_ref, dwi_ref, dbi_ref, dlam_ref):
            ref[...] = jnp.zeros_like(ref)

        def tile(s, carry):
            e_in, dxc_next8 = carry
            ti = nt - 1 - s
            t0 = pl.multiple_of(ti * tt, tt)
            p0 = pl.multiple_of(jnp.maximum(t0 - 8, 0), 8)
            first = (ti > 0).astype(F32)
            xl = xl_ref[pl.ds(t0, tt), :]
            xs, xc = _lru_conv(xl, xl_ref[pl.ds(p0, 8), :] * first, cwv, cbv)
            r, ig, a, mult = (ref[pl.ds(t0, tt), :] for ref in (r_ref, i_ref, a_ref, m_ref))
            hv = h_ref[pl.ds(t0, tt), :]
            h_before = _shift_down(hv, h_ref[pl.ds(p0, 8), :] * first, 1)
            glv = gl_ref[pl.ds(t0, tt), :]
            dyv = dy_ref[pl.ds(t0, tt), :]
            dgl_ref[pl.ds(t0, tt), :] = (dyv * hv * _gelu_grad(glv)).astype(dgl_ref.dtype)
            dh = dyv * _gelu(glv)
            row = lax.broadcasted_iota(jnp.int32, a.shape, 0)
            coef = jnp.where(row == tt - 1, 1.0, pltpu.roll(a, tt - 1, 0))
            ga, gb = _group_scan(coef, dh, True)
            gin = e_in
            for g in reversed(range(tt // 8)):
                gg = ga[8 * g:8 * g + 8, :] * gin + gb[8 * g:8 * g + 8, :]
                g_ref[8 * g:8 * g + 8, :] = gg
                gin = gg[0:1, :]
            gv = g_ref[...]
            e_out = a[0:1, :] * gv[0:1, :]
            ix = ig * xc
            dla = (gv * h_before) * a - (gv * ix) * (a * a / mult)
            dlam_ref[...] += jnp.sum(dla * (LRU_C * r), axis=0, keepdims=True)
            dpr = (dla * (LRU_C * ls)) * (r * (1.0 - r))
            dpi = (gv * mult * xc) * (ig * (1.0 - ig))
            dbr_ref[...] += jnp.sum(dpr, axis=0, keepdims=True)
            dbi_ref[...] += jnp.sum(dpi, axis=0, keepdims=True)
            xcb, dprb, dpib = xc.astype(_MXU), dpr.astype(_MXU), dpi.astype(_MXU)
            dwr_ref[...] += _dot_tn(xcb, dprb)
            dwi_ref[...] += _dot_tn(xcb, dpib)
            dxc = gv * mult * ig + _dot_nt(dprb, wrv) + _dot_nt(dpib, wiv)
            dcb_ref[...] += jnp.sum(dxc, axis=0, keepdims=True)
            dxl = None
            for k in range(CONV_WIDTH):
                dcw_ref[k:k + 1, :] += jnp.sum(dxc * xs[k], axis=0, keepdims=True)
                term = _shift_up(dxc, dxc_next8, CONV_WIDTH - 1 - k) * cwv[k:k + 1, :]
                dxl = term if dxl is None else dxl + term
            dxl_ref[pl.ds(t0, tt), :] = dxl.astype(dxl_ref.dtype)
            return e_out, dxc[0:8, :]

        lax.fori_loop(0, nt, tile, (jnp.zeros((1, HEAD_DIM), F32), jnp.zeros((8, HEAD_DIM), F32)))
        dlam_ref[...] = dlam_ref[...] * (1.0 - jax.nn.sigmoid(lamv))
        _emit([dxl_ref, dgl_ref], dproj_ref, [(col0 + block) * HEAD_DIM, (col0 + n_blocks + block) * HEAD_DIM], out_sems)

    cs = lambda off: pl.BlockSpec((t, HEAD_DIM), lambda n: (0, off + n))
    vs = pl.BlockSpec((1, HEAD_DIM), lambda n: (0, n))
    ws = pl.BlockSpec((None, HEAD_DIM, HEAD_DIM), lambda n: (n, 0, 0))
    cws = pl.BlockSpec((CONV_WIDTH, HEAD_DIM), lambda n: (0, n))
    w = n_blocks * HEAD_DIM
    vec = S((1, w), F32)
    mat = S((n_blocks, HEAD_DIM, HEAD_DIM), F32)
    return _call(
        comm, body, grid=(n_blocks,),
        in_specs=[cs(col0), cs(col0 + n_blocks)] + [cs(0)] * 6 + [cws, vs, ws, ws, vs],
        out_specs=[_ANY, cws, vs, ws, vs, ws, vs, vs],
        out_shape=[S(proj.shape, _MXU), S((CONV_WIDTH, w), F32), vec, mat, vec, mat, vec, vec],
        scratch_shapes=[pltpu.VMEM((tt, HEAD_DIM), F32), pltpu.VMEM((t, HEAD_DIM), _MXU), pltpu.VMEM((t, HEAD_DIM), _MXU),
                        pltpu.SemaphoreType.DMA((2,))],
        compiler_params=_cp("parallel"), name=name)(proj, proj, h, *kept, dyl, cw, cb, wr, wi, lam)


class _NoExchange:
    def __init__(self, weights):
        self.weights, self.grads, self.packs, self.place = weights, {}, {}, jnp.zeros((2,), jnp.int32)

    def weight(self, name):
        return self.weights[name]

    def in_proj(self, x, gain, bm):
        hn = _rms_fwd(x, gain, "rms1")
        return [hn, *_mm_nn(hn, self.weights["w_in"], bm=bm, bn=self.weights["w_in"].shape[2], name="in_proj", also=_MXU)]

    def conv_w(self):
        return self.weights["conv_w"]

    def carrier(self, call):
        return None

    def harvest(self, car):
        pass

    def alone(self, call):
        pass


def _local_step(x, target, norms, ex, cb, wr, br, wi, bi, lam, ga, gl):
    g_pre_mix, g_post_mix, g_pre_ffn, g_post_ffn = norms
    t, d = x.shape
    bm = min(t, 512)
    bt = min(t, 2048)

    def run(fn, name, *args, **kw):
        car = ex.carrier(name)
        out = fn(*args, name=name, comm=car, **kw)
        ex.harvest(car)
        return out

    hn1, proj, proj_mx = ex.in_proj(x, g_pre_mix, bm)
    win3, cw = ex.weight("w_in"), ex.conv_w()
    c = win3.shape[0]
    o = run(_attn_fwd, "attn_fwd", proj_mx, (proj.shape[1] - d) // 3 // HEAD_DIM)
    mix = 2 * o.shape[1]
    n_heads = n_blocks = o.shape[1] // HEAD_DIM
    h, yl, *kept = run(_lru_fwd, "lru_fwd", proj, 3 * n_heads, n_blocks, cw, cb, wr, br, wi, bi, lam)
    y = run(_outnorm_fwd, "outnorm_fwd", o, yl, ga, gl)
    wout = ex.weight("w_out")
    mixo = run(_mm_nn, "out_proj", y, wout[None], bm=bm, bn=d)
    x2, hn2 = run(_mid_fwd, "mid_fwd", x, mixo, g_post_mix, g_pre_ffn)
    wg3, wu3 = ex.weight("w_ffn_gate"), ex.weight("w_ffn_up")
    ffn_mid = run(_swiglu_fwd, "ffn_gate_up", hn2, wg3, wu3, ex.place, (0, min(c, 3)), None, bm=bm)
    if c == 4:
        ex.alone("gather_up_diag")
        wg3, wu3 = ex.weight("w_ffn_gate"), ex.weight("w_ffn_up")
        ffn_mid = run(_swiglu_fwd, "ffn_gate_up_diag", hn2, wg3, wu3, ex.place, (3, 4), ffn_mid, bm=bm)
    act_dgate, act_dup, act = ffn_mid
    wd = ex.weight("w_ffn_down")
    ff = wd.shape[0]
    f = _mm_nn(act, wd[None], bm=bm, bn=d // 2, name="ffn_down")
    loss_cols, dout, df, dg_post_ffn = _final(f, x2, target, g_post_ffn, "final")

    dgate, dup = _swiglu_bwd(df, wd, act_dgate, act_dup, bm=bm, bo=ff // 4, name="ffn_down_bwd")
    ex.grads["w_ffn_down"] = _mm_tn(act, df, 1, bm=bt, bk=512, name="ffn_down_dw").reshape(c, ff // c, d)
    ex.grads["w_ffn_gate"] = run(_mm_tn, "ffn_gate_dw", hn2, dgate, c, bm=bt, bk=d // 2)
    ex.grads["w_ffn_up"] = run(_mm_tn, "ffn_up_dw", hn2, dup, c, bm=bt, bk=d // 2)
    dhn2_g = run(_mm_nt, "ffn_gate_dx", dgate, wg3, bm=bm, bo=d // 2, out_dtype=F32)
    dhn2_u = run(_mm_nt, "ffn_up_dx", dup, wu3, bm=bm, bo=d // 2, out_dtype=F32)
    dx2, dmix, dg_pre_ffn, dg_post_mix = run(_mid_bwd, "mid_bwd", dhn2_g, dhn2_u, dout, x2, mixo, g_pre_ffn, g_post_mix)
    dy = run(_mm_nt, "out_proj_dx", dmix, wout[None], bm=bm, bo=mix, out_dtype=F32)
    ex.grads["w_out"] = _mm_tn(y, dmix, 1, bm=bt, bk=mix // 4, name="out_proj_dw").reshape(c, mix // c, d)
    do, dyl, dga, dgl_norm = run(_outnorm_bwd, "outnorm_bwd", dy, o, yl, ga, gl)
    dproj, dcw, dcb, dwr, dbr, dwi, dbi, dlam = run(_lru_bwd, "lru_bwd", proj, 3 * n_heads, n_blocks, h, kept, dyl, cw, cb, wr, wi, lam)
    small = dict(post_mix_norm=dg_post_mix, pre_ffn_norm=dg_pre_ffn, post_ffn_norm=dg_post_ffn, conv_w=dcw, conv_b=dcb,
                 w_rgate=dwr, b_rgate=dbr, w_igate=dwi, b_igate=dbi, lru_lambda=dlam, attn_out_norm=dga, lru_out_norm=dgl_norm)
    ex.packs["early"] = _pack([small[n] for n in _SMALL_EARLY])
    dproj = run(_attn_bwd, "attn_bwd", proj_mx, do, dproj, n_heads)
    ex.grads["w_in"] = _mm_tn(hn1, dproj, c, bm=bt, bk=d // 2, name="in_proj_dw")
    ex.alone("grads_w_in_swap")
    dhn1 = run(_mm_nt, "in_proj_dx", dproj, win3, bm=bm, bo=d // 2, out_dtype=F32)
    grad_x, small["pre_mix_norm"] = run(_first_bwd, "first_bwd", dhn1, dx2, x, g_pre_mix)
    ex.packs["late"] = _pack([small["pre_mix_norm"], (0.5 / d) * jnp.sum(loss_cols, keepdims=True)])
    return loss_cols, grad_x, small


def _into_slot(wsh, slot, dtype, name):
    rows, n = wsh.shape
    rb = _row_block(rows, 256) if rows % 8 == 0 else rows

    def body(s_ref, w_ref, o_ref):
        o_ref[...] = w_ref[...].astype(o_ref.dtype)

    return pl.pallas_call(
        body,
        grid_spec=pltpu.PrefetchScalarGridSpec(
            num_scalar_prefetch=1, grid=(rows // rb,),
            in_specs=[pl.BlockSpec((rb, n), lambda i, s_ref: (i, 0))],
            out_specs=pl.BlockSpec((None, rb, n), lambda i, s_ref: (s_ref[0], i, 0))),
        out_shape=S((4, rows, n), dtype), compiler_params=_cp("parallel"), name=name)(slot, wsh)


class _Exchange:
    SCHEDULE = {
        "in_proj": [("stream", "w_in"), ("ici", "conv_w"), ("ici", "w_ffn_gate", "xy")],
        "attn_fwd": [("d2d", "w_ffn_gate", "xy"), ("ici", "w_ffn_up", "xy")],
        "lru_fwd": [("d2d", "w_ffn_up", "xy"), ("ici", "w_out")],
        "outnorm_fwd": [("d2d", "w_out")],
        "out_proj": [("ici", "w_ffn_gate", "d")],
        "mid_fwd": [("d2d", "w_ffn_gate", "d"), ("ici", "w_ffn_up", "d")],
        "ffn_gate_up": [("ici", "w_ffn_down")],
        "gather_up_diag": [("d2d", "w_ffn_up", "d")],
        "ffn_gate_up_diag": [("d2d", "w_ffn_down")],
        "ffn_gate_dw": [("swap", "w_ffn_down")],
        "ffn_up_dw": [("scatter", "w_ffn_down", 0), ("scatter", "w_ffn_down", 1), ("scatter", "w_ffn_down", 2), ("swap", "w_ffn_gate")],
        "ffn_gate_dx": [("scatter", "w_ffn_down", 3), ("scatter", "w_ffn_gate", 0), ("scatter", "w_ffn_gate", 1), ("swap", "w_ffn_up")],
        "ffn_up_dx": [("share", "w_ffn_down"), ("scatter", "w_ffn_gate", 2), ("scatter", "w_ffn_gate", 3), ("scatter", "w_ffn_up", 0)],
        "mid_bwd": [("share", "w_ffn_gate"), ("scatter", "w_ffn_up", 1), ("scatter", "w_ffn_up", 2)],
        "out_proj_dx": [("scatter", "w_ffn_up", 3)],
        "outnorm_bwd": [("share", "w_ffn_up"), ("swap", "w_out")],
        "lru_bwd": [("scatter", "w_out")],
        "attn_bwd": [("share", "w_out"), ("spread", "early")],
        "grads_w_in_swap": [("swap", "w_in")],
        "in_proj_dx": [("scatter", "w_in")],
        "grads_w_in_share": [("share", "w_in"), ("spread", "late")],
    }
    PIECES = 4

    def __init__(self, slots, place):
        self.buf, self.place = dict(slots), place
        self.grads, self.packs, self.swapped, self.part, self.scattered, self.full, self.spreaded = {}, {}, {}, {}, {}, {}, {}

    def weight(self, name):
        b = self.buf[name]
        return b.reshape(-1, b.shape[2]) if name in ("w_out", "w_ffn_down") else b

    def in_proj(self, x, gain, bm):
        car = self.carrier("in_proj")
        out = _in_proj_streamed(x, gain, car, car.streamed, self.place, bm=bm, name="in_proj")
        self.harvest(car)
        return out

    def conv_w(self):
        return jnp.transpose(self.buf["conv_w"], (1, 0, 2)).reshape(CONV_WIDTH, -1)

    def carrier(self, call):
        if call not in self.SCHEDULE:
            return None
        car = _Carrier()
        car.todo, slot = [], {}
        for kind, name, *piece in self.SCHEDULE[call]:
            if kind in ("ici", "d2d", "stream"):
                if name not in slot:
                    slot[name] = car.inplace(self.buf[name])
                    car.todo.append((self.buf, name, slot[name]))
            if kind == "stream":
                car.streamed = slot[name]
            elif kind in ("ici", "d2d"):
                rel = tuple("xyd".index(ch) for ch in piece[0]) if piece else (0, 1, 2)
                if kind == "ici":
                    car.gather_ici(slot[name], split=name != "conv_w", rel=rel)
                else:
                    car.gather_d2d(slot[name], rel=rel)
            elif kind == "swap":
                g = self.grads[name]
                o = car.fresh((4, g.shape[1] // 2, g.shape[2]), F32)
                car.swap(car.read(g), o)
                car.todo.append((self.swapped, name, o))
            elif kind == "scatter":
                if name not in self.part:
                    self.part[name] = _add_own_half(self.grads[name], self.swapped[name], self.place[1:], "grads_add_" + name)
                p = self.part[name]
                key = ("scatter", name)
                if key not in slot:
                    slot[key] = (car.read(p), car.inplace(self.scattered[name]) if name in self.scattered else car.fresh(p.shape, p.dtype))
                    car.todo.append((self.scattered, name, slot[key][1]))
                size = p.shape[1] // self.PIECES
                car.scatter(*slot[key], (piece[0] * size, size) if piece else None)
            elif kind == "share":
                o = car.inplace(_sum_chips(self.part[name], self.scattered[name], self.place, "grads_sum_" + name))
                car.share(o)
                car.todo.append((self.full, name, o))
            else:
                o = car.fresh((8,) + self.packs[name].shape, F32)
                car.spread(car.read(self.packs[name]), o)
                car.todo.append((self.spreaded, name, o))
        return car

    def harvest(self, car):
        for state, name, o in (car.todo if car is not None else []):
            state[name] = car.results[o]

    def alone(self, call):
        car = self.carrier(call)
        car.run_alone(call)
        self.harvest(car)

    def small_sum(self, key):
        return _sum_devices(self.packs[key], self.spreaded[key], 2 * self.place[0:1] + self.place[1:], "grads_small_sum_" + key)


def _row_block(rows, cap):
    return max(b for b in range(8, cap + 1, 8) if rows % b == 0)


def _add_own_half(g, recv, core, name):
    _, rows, n = g.shape
    half = rows // 2
    rb = _row_block(half, 512)
    nb = half // rb

    def body(c_ref, g_ref, r_ref, o_ref):
        o_ref[...] = (g_ref[...] + r_ref[...]).astype(o_ref.dtype)

    return pl.pallas_call(
        body,
        grid_spec=pltpu.PrefetchScalarGridSpec(
            num_scalar_prefetch=1, grid=(4, nb),
            in_specs=[pl.BlockSpec((None, rb, n), lambda k, i, c_ref: (k, c_ref[0] * nb + i, 0)),
                      pl.BlockSpec((None, rb, n), lambda k, i, c_ref: (k, i, 0))],
            out_specs=pl.BlockSpec((None, rb, n), lambda k, i, c_ref: (k, i, 0))),
        out_shape=S((4, half, n), BF16), compiler_params=_cp("parallel", "parallel"), name=name)(core, g, recv)


def _sum_chips(part, recv, place, name):
    _, rows, n = part.shape
    rb = _row_block(rows, 64)
    nb = rows // rb

    def body(p_ref, own_ref, r0, r1, r2, r3, o_ref):
        own = own_ref[...].astype(F32)
        terms = [jnp.where(p_ref[0] == k, own, r[...].astype(F32)) for k, r in enumerate((r0, r1, r2, r3))]
        o_ref[...] = ((terms[0] + terms[1]) + terms[2]) + terms[3]

    def slot(k):
        return pl.BlockSpec((None, rb, n), lambda i, p_ref: (jnp.where(p_ref[0] == k, (k + 1) % 4, k), i, 0))

    return pl.pallas_call(
        body,
        grid_spec=pltpu.PrefetchScalarGridSpec(
            num_scalar_prefetch=1, grid=(nb,),
            in_specs=[pl.BlockSpec((None, rb, n), lambda i, p_ref: (p_ref[0], i, 0))] + [slot(k) for k in range(4)],
            out_specs=pl.BlockSpec((rb, n), lambda i, p_ref: (p_ref[1] * nb + i, 0))),
        out_shape=S((2 * rows, n), F32), compiler_params=_cp("parallel"), name=name)(place, part, recv, recv, recv, recv)


def _sum_devices(own, spread, me, name):
    rows = own.shape[0]

    def body(me_ref, own_ref, *refs):
        acc = None
        for k, r in enumerate(refs[:8]):
            term = jnp.where(me_ref[0] == k, own_ref[...], r[...])
            acc = term if acc is None else acc + term
        refs[8][...] = acc

    def slot(k):
        return pl.BlockSpec((None, rows, 128), lambda i, me_ref: (jnp.where(me_ref[0] == k, (k + 1) % 8, k), 0, 0))

    whole = pl.BlockSpec((rows, 128), lambda i, me_ref: (0, 0))
    return pl.pallas_call(
        body,
        grid_spec=pltpu.PrefetchScalarGridSpec(num_scalar_prefetch=1, grid=(1,), in_specs=[whole] + [slot(k) for k in range(8)],
                                               out_specs=whole),
        out_shape=S((rows, 128), F32), compiler_params=_cp("arbitrary"), name=name)(me, own, *[spread] * 8)


def _adamw(w, g, m, v, name, regive=False):
    rows, n = w.shape
    rb = rows if rows * n * 4 <= (1 << 21) else _row_block(rows, 256)
    c1 = 1.0 - ADAM_B1 ** ADAM_STEP
    c2 = 1.0 - ADAM_B2 ** ADAM_STEP

    def body(w_ref, g_ref, m_ref, v_ref, d_ref, nm_ref, nv_ref, *again):
        gv = g_ref[...]
        for ref in again:
            ref[...] = gv
        nm = ADAM_B1 * m_ref[...] + (1.0 - ADAM_B1) * gv
        nv = ADAM_B2 * v_ref[...] + (1.0 - ADAM_B2) * (gv * gv)
        nm_ref[...] = nm
        nv_ref[...] = nv
        d_ref[...] = -ADAM_LR * ((nm / c1) / (jnp.sqrt(nv / c2) + ADAM_EPS) + ADAM_WD * w_ref[...])

    bs = pl.BlockSpec((rb, n), lambda i: (i, 0))
    n_out = 4 if regive else 3
    return pl.pallas_call(body, grid=(rows // rb,), in_specs=[bs] * 4, out_specs=[bs] * n_out, out_shape=[S((rows, n), F32)] * n_out,
                          compiler_params=_cp("parallel"), name=name)(w, g, m, v)


_BIG = ("w_in", "w_out", "w_ffn_gate", "w_ffn_up", "w_ffn_down")
_SMALL = ("pre_mix_norm", "post_mix_norm", "pre_ffn_norm", "post_ffn_norm", "conv_w", "conv_b", "w_rgate", "b_rgate",
          "w_igate", "b_igate", "lru_lambda", "attn_out_norm", "lru_out_norm")
_SMALL_EARLY = _SMALL[1:]
_WEIGHTS = ("pre_mix_norm", "post_mix_norm", "pre_ffn_norm", "post_ffn_norm", "w_in", "conv_w", "conv_b", "w_rgate", "b_rgate",
            "w_igate", "b_igate", "lru_lambda", "attn_out_norm", "lru_out_norm", "w_out", "w_ffn_gate", "w_ffn_up", "w_ffn_down")


def _pack(arrays):
    flat = []
    for a in arrays:
        f = a.reshape(-1)
        flat.append(jnp.pad(f, (0, (-f.shape[0]) % 1024)))
    return jnp.concatenate(flat).reshape(-1, 128)


def _unpack(packed, shapes):
    out, pos = [], 0
    flat = packed.reshape(-1)
    for s in shapes:
        size = math.prod(s)
        out.append(flat[pos:pos + size].reshape(s))
        pos += size + (-size) % 1024
    return out


def kernel(x, pre_mix_norm, post_mix_norm, pre_ffn_norm, post_ffn_norm, w_in, conv_w, conv_b, w_rgate, b_rgate, w_igate, b_igate, lru_lambda, attn_out_norm, lru_out_norm, w_out, w_ffn_gate, w_ffn_up, w_ffn_down, loss_target, m_pre_mix_norm, m_post_mix_norm, m_pre_ffn_norm, m_post_ffn_norm, m_w_in, m_conv_w, m_conv_b, m_w_rgate, m_b_rgate, m_w_igate, m_b_igate, m_lru_lambda, m_attn_out_norm, m_lru_out_norm, m_w_out, m_w_ffn_gate, m_w_ffn_up, m_w_ffn_down, v_pre_mix_norm, v_post_mix_norm, v_pre_ffn_norm, v_post_ffn_norm, v_w_in, v_conv_w, v_conv_b, v_w_rgate, v_b_rgate, v_w_igate, v_b_igate, v_lru_lambda, v_attn_out_norm, v_lru_out_norm, v_w_out, v_w_ffn_gate, v_w_ffn_up, v_w_ffn_down):
    given = dict(locals())
    w = {n: given[n][0] for n in _WEIGHTS}
    m = {n: given["m_" + n][0] for n in _WEIGHTS}
    v = {n: given["v_" + n][0] for n in _WEIGHTS}
    xs, target = x[0], loss_target[0]
    d = xs.shape[1]
    chip = (2 * lax.axis_index("x") + lax.axis_index("y")).astype(jnp.int32)
    place = jnp.stack([chip, lax.axis_index("c").astype(jnp.int32)])

    slots = {n: _into_slot(w[n], place[0:1], _MXU, "slot_" + n) for n in _BIG}
    slots["conv_w"] = _into_slot(w["conv_w"], place[0:1], F32, "slot_conv_w")
    ex = _Exchange(slots, place)
    row = lambda a: a.reshape(1, -1)
    norms = tuple(row(w[n]) for n in ("pre_mix_norm", "post_mix_norm", "pre_ffn_norm", "post_ffn_norm"))

    loss_cols, grad_x, small = _local_step(
        xs, target, norms, ex, row(w["conv_b"]), w["w_rgate"], row(w["b_rgate"]),
        w["w_igate"], row(w["b_igate"]), row(w["lru_lambda"]), row(w["attn_out_norm"]), row(w["lru_out_norm"]))


    ex.alone("grads_w_in_share")
    reduced = {n: ex.full[n] for n in _BIG}
    early = _unpack(ex.small_sum("early"), [small[n].shape for n in _SMALL_EARLY])
    late = _unpack(ex.small_sum("late"), [small["pre_mix_norm"].shape, (1, 1)])
    loss = late[1][0, 0]
    for n, g in zip(_SMALL_EARLY + ("pre_mix_norm",), early + late[:1]):
        reduced[n] = g.reshape(w[n].shape) if n != "conv_w" else lax.dynamic_slice_in_dim(g, chip * w[n].shape[1], w[n].shape[1], axis=1)

    delta, new_m, new_v = {}, {}, {}
    for n in _BIG:
        delta[n], new_m[n], new_v[n], reduced[n] = _adamw(w[n], reduced[n], m[n], v[n], "adamw_" + n, regive=True)
    shapes = [w[n].shape for n in _SMALL]
    packed = _adamw(*[_pack([src[n] for n in _SMALL]) for src in (w, reduced, m, v)], "adamw_small")
    for out, p in zip((delta, new_m, new_v), packed):
        out.update(zip(_SMALL, _unpack(p, shapes)))

    lead = lambda a: a[None]
    return (loss, lead(grad_x), *[lead(reduced[n]) for n in _WEIGHTS], *[lead(delta[n]) for n in _WEIGHTS],
            *[lead(new_m[n]) for n in _WEIGHTS], *[lead(new_v[n]) for n in _WEIGHTS])
```

```python
import functools
import math

import jax
import jax.numpy as jnp
from jax import lax
from jax.experimental import pallas as pl
from jax.experimental.pallas import tpu as pltpu

F32 = jnp.float32
BF16 = jnp.bfloat16
_MXU = BF16
S = jax.ShapeDtypeStruct

RMS_EPS = 1e-6
HEAD_DIM = 128
CONV_WIDTH = 4
LRU_C = 8.0
ADAM_LR, ADAM_B1, ADAM_B2, ADAM_EPS, ADAM_WD, ADAM_STEP = 0.001, 0.9, 0.999, 1e-08, 0.01, 10
EXP_CUT = -105.0
VMEM_LIMIT = 60 * 1024 * 1024
ROW_TILE = 256
SEQ_TILE = 256
ATTN_BLOCK = 256
ATTN_HEADS = 2
MESH = pl.DeviceIdType.MESH


def _cp(*sem):
    return pltpu.CompilerParams(dimension_semantics=sem, vmem_limit_bytes=VMEM_LIMIT)


def _dot(a, b):
    return jnp.dot(a, b, preferred_element_type=F32)


def _dot_nt(a, b):
    return lax.dot_general(a, b, (((1,), (1,)), ((), ())), preferred_element_type=F32)


def _dot_tn(a, b):
    return lax.dot_general(a, b, (((0,), (0,)), ((), ())), preferred_element_type=F32)


def _rstd(v):
    return lax.rsqrt(jnp.mean(v * v, axis=-1, keepdims=True) + RMS_EPS)


def _rms_bwd(dn, vh, r, gain):
    dvh = dn * gain
    dv = r * (dvh - vh * jnp.mean(dvh * vh, axis=-1, keepdims=True))
    return dv, jnp.sum(dn * vh, axis=0, keepdims=True)


def _log_sigmoid(z):
    return jnp.minimum(z, 0.0) - jnp.log(1.0 + jnp.exp(-jnp.abs(z)))


def _expm1(v):
    small = v * (1.0 + v * (0.5 + v * (1.0 / 6.0 + v * (1.0 / 24.0 + v * (1.0 / 120.0)))))
    return jnp.where(jnp.abs(v) < 0.04, small, jnp.exp(v) - 1.0)


_GELU_C = math.sqrt(2.0 / math.pi)


def _gelu(v):
    return 0.5 * v * (1.0 + jnp.tanh(_GELU_C * (v + 0.044715 * v * v * v)))


def _gelu_grad(v):
    th = jnp.tanh(_GELU_C * (v + 0.044715 * v * v * v))
    return 0.5 * (1.0 + th) + 0.5 * v * (1.0 - th * th) * _GELU_C * (1.0 + 3.0 * 0.044715 * v * v)


def _row_spec(tm, d):
    return pl.BlockSpec((tm, d), lambda i: (i, 0))


def _vec_spec(d):
    return pl.BlockSpec((1, d), lambda i: (0, 0))


_ANY = pl.BlockSpec(memory_space=pl.ANY)


def _place():
    x, y, c = lax.axis_index("x"), lax.axis_index("y"), lax.axis_index("c")
    return x, y, c, [(1 - x, y), (x, 1 - y), (1 - x, 1 - y)]


def _remote(src, dst, send_sem, recv_sem, to):
    return pltpu.make_async_remote_copy(src_ref=src, dst_ref=dst, send_sem=send_sem, recv_sem=recv_sem,
                                        device_id=to, device_id_type=MESH)


class _Carrier:
    def __init__(self):
        self.inputs, self.out_shapes, self.aliases, self.ops, self.n_sems, self.results = [], [], {}, [], 0, None

    def inplace(self, arr):
        self.aliases[len(self.inputs)] = len(self.out_shapes)
        self.inputs.append(arr)
        self.out_shapes.append(S(arr.shape, arr.dtype))
        return len(self.out_shapes) - 1

    def read(self, arr):
        self.inputs.append(arr)
        return len(self.inputs) - 1

    def fresh(self, shape, dtype):
        self.out_shapes.append(S(shape, dtype))
        return len(self.out_shapes) - 1

    def _add(self, n_sems, copies):
        base = self.n_sems
        self.n_sems += n_sems

        def start(ins, outs, send, recv):
            for k, (src, dst, _, to) in enumerate(copies(ins, outs)):
                _remote(src, dst, send.at[base + k], recv.at[base + k], to).start()

        def finish(ins, outs, send, recv):
            for k, (src, _, land, to) in enumerate(copies(ins, outs)):
                _remote(src, land, send.at[base + k], recv.at[base + k], to).wait()

        self.ops.append((start, finish))

    def gather_ici(self, o, rows=None, split=True):
        half = self.out_shapes[o].shape[1] // 2
        lo, size = rows or (0, half)

        def copies(ins, outs):
            x, y, c, chips = _place()
            part = (lambda ref: ref.at[pl.ds(c * half + lo, size)]) if split else (lambda ref: ref)
            mine = part(outs[o].at[2 * x + y])
            return [(mine, mine, part(outs[o].at[2 * px + py]), (px, py, c)) for px, py in chips]

        self._add(3, copies)

    def gather_d2d(self, o, rows=None):
        half = self.out_shapes[o].shape[1] // 2
        lo, size = rows or (0, half)

        def copies(ins, outs):
            x, y, c, chips = _place()
            at = lambda k, cc: outs[o].at[k].at[pl.ds(cc * half + lo, size)]
            return [(at(2 * px + py, c), at(2 * px + py, c), at(2 * px + py, 1 - c), (x, y, 1 - c)) for px, py in chips]

        self._add(3, copies)

    def swap(self, i, o):
        half = self.inputs[i].shape[1] // 2

        def copies(ins, outs):
            x, y, c, _ = _place()
            return [(ins[i].at[:, pl.ds((1 - c) * half, half)], outs[o], outs[o], (x, y, 1 - c))]

        self._add(1, copies)

    def scatter(self, i, o, rows=None):
        lo, size = rows or (0, self.inputs[i].shape[1])

        def copies(ins, outs):
            x, y, c, chips = _place()
            cut = lambda ref: ref.at[pl.ds(lo, size)]
            return [(cut(ins[i].at[2 * px + py]), cut(outs[o].at[2 * x + y]), cut(outs[o].at[2 * px + py]), (px, py, c)) for px, py in chips]

        self._add(3, copies)

    def share(self, o):
        r = self.out_shapes[o].shape[0] // 2

        def copies(ins, outs):
            x, y, c, _ = _place()
            mine = outs[o].at[pl.ds(c * r, r)]
            return [(mine, mine, outs[o].at[pl.ds((1 - c) * r, r)], (x, y, 1 - c))]

        self._add(1, copies)

    def spread(self, i, o):
        def copies(ins, outs):
            x, y, c, _ = _place()
            me = 4 * x + 2 * y + c
            out = []
            for d in range(1, 8):
                to, frm = (me + d) % 8, (me + 8 - d) % 8
                out.append((ins[i], outs[o].at[me], outs[o].at[frm], (to // 4, (to // 2) % 2, to % 2)))
            return out

        self._add(7, copies)

    def _pallas(self, body, n_in, n_out, scratch, **kw):
        k_in, k_out = len(self.inputs), len(self.out_shapes)
        grid = kw.get("grid", ())

        def wrapped(*refs):
            ins, cins = refs[:n_in], refs[n_in:n_in + k_in]
            outs = refs[n_in + k_in:n_in + k_in + n_out]
            couts = refs[n_in + k_in + n_out:n_in + k_in + n_out + k_out]
            own = refs[n_in + k_in + n_out + k_out:]
            send, recv = own[len(scratch):]
            ids = [pl.program_id(a) for a in range(len(grid))]
            first = functools.reduce(jnp.logical_and, [a == 0 for a in ids], True)
            last = functools.reduce(jnp.logical_and, [a == g - 1 for a, g in zip(ids, grid)], True)

            def go(stage):
                for op in self.ops:
                    op[stage](cins, couts, send, recv)

            if grid:
                pl.when(first)(lambda: go(0))
                body(*ins, *outs, *own[:len(scratch)])
                pl.when(last)(lambda: go(1))
            else:
                go(0)
                go(1)

        sem = pltpu.SemaphoreType.DMA((self.n_sems,))
        return pl.pallas_call(
            wrapped, in_specs=list(kw.get("in_specs", [])) + [_ANY] * k_in, out_specs=list(kw.get("out_specs", [])) + [_ANY] * k_out,
            out_shape=list(kw.get("out_shape", [])) + self.out_shapes, scratch_shapes=list(scratch) + [sem, sem],
            input_output_aliases={**kw.get("aliases", {}), **{n_in + i: n_out + o for i, o in self.aliases.items()}}, name=kw["name"],
            **({"grid": grid, "compiler_params": _cp(*["arbitrary"] * len(grid))} if grid else {}))

    def run(self, body, kw, *args):
        single = not isinstance(kw["out_shape"], (list, tuple))
        out_shape = [kw["out_shape"]] if single else list(kw["out_shape"])
        out_specs = [kw["out_specs"]] if single else list(kw["out_specs"])
        res = self._pallas(body, len(args), len(out_shape), kw.get("scratch_shapes", []), grid=kw["grid"], in_specs=kw["in_specs"],
                           out_specs=out_specs, out_shape=out_shape, name=kw["name"],
                           aliases=kw.get("input_output_aliases", {}))(*args, *self.inputs)
        self.results = list(res[len(out_shape):])
        return res[0] if single else list(res[:len(out_shape)])

    def run_alone(self, name):
        self.results = list(self._pallas(None, 0, 0, [], name=name)(*self.inputs))


def _call(comm, body, **kw):
    if comm is None:
        return pl.pallas_call(body, **kw)
    return functools.partial(comm.run, body, kw)


def _in_proj_streamed(x, gain, car, o_w, place, *, bm, name):
    m, k = x.shape
    n = car.out_shapes[o_w].shape[2]
    ni, half = m // bm, k // 2
    k_in, k_out = len(car.inputs), len(car.out_shapes)
    order = lambda p: ((p & 1) << 1) | (p >> 1)

    def body(place_ref, x_ref, g_ref, *refs):
        cins, (hn_ref, o_ref, ob_ref), couts = refs[:k_in], refs[k_in:k_in + 3], refs[k_in + 3:k_in + 3 + k_out]
        wbuf, local, ici_send, ici_recv, d2d_send, d2d_recv, send, recv = refs[k_in + 3 + k_out:]
        p, i = pl.program_id(0), pl.program_id(1)
        x, y, c, chips = _place()
        me = 2 * x + y
        rows = lambda chunk, cc: couts[o_w].at[chunk].at[pl.ds(cc * half, half)]

        @pl.when(jnp.logical_and(p == 0, i == 0))
        def _():
            for j, (px, py) in enumerate(chips):
                _remote(rows(me, c), rows(me, c), ici_send.at[j], ici_recv.at[j], (px, py, c)).start()
            for op in car.ops:
                op[0](cins, couts, send, recv)

        for j, (px, py) in enumerate(chips):
            @pl.when(jnp.logical_and(p == j + 1, i == 0))
            def _(j=j, px=px, py=py):
                landed, other = rows(2 * px + py, c), rows(2 * px + py, 1 - c)
                _remote(landed, landed, ici_send.at[j], ici_recv.at[j], (px, py, c)).wait_recv()
                _remote(landed, landed, d2d_send.at[j], d2d_recv.at[j], (x, y, 1 - c)).start()
                _remote(other, other, d2d_send.at[j], d2d_recv.at[j], (x, y, 1 - c)).wait_recv()

        @pl.when(i == 0)
        def _():
            cp = pltpu.make_async_copy(couts[o_w].at[me ^ order(p)], wbuf, local.at[0])
            cp.start()
            cp.wait()

        xv = x_ref[...]
        hn = ((xv * _rstd(xv)) * g_ref[...]).astype(_MXU)
        hn_ref[...] = hn
        res = _dot(hn, wbuf[...])
        o_ref[...] = res
        ob_ref[...] = res.astype(ob_ref.dtype)

        @pl.when(jnp.logical_and(p == 3, i == ni - 1))
        def _():
            for j, (px, py) in enumerate(chips):
                _remote(rows(me, c), rows(me, c), ici_send.at[j], ici_recv.at[j], (px, py, c)).wait_send()
                _remote(rows(me, c), rows(me, c), d2d_send.at[j], d2d_recv.at[j], (x, y, 1 - c)).wait_send()
            for op in car.ops:
                op[1](cins, couts, send, recv)

    ospec = pl.BlockSpec((bm, n), lambda p, i, place_ref: (i, place_ref[0] ^ order(p)))
    rows = pl.BlockSpec((bm, k), lambda p, i, place_ref: (i, 0))
    three, sems = pltpu.SemaphoreType.DMA((3,)), pltpu.SemaphoreType.DMA((max(car.n_sems, 1),))
    res = pl.pallas_call(
        body,
        grid_spec=pltpu.PrefetchScalarGridSpec(
            num_scalar_prefetch=1, grid=(4, ni),
            in_specs=[rows, pl.BlockSpec((1, k), lambda p, i, place_ref: (0, 0))] + [_ANY] * k_in,
            out_specs=[pl.BlockSpec((bm, k), lambda p, i, place_ref: (p * ni + i, 0)), ospec, ospec] + [_ANY] * k_out,
            scratch_shapes=[pltpu.VMEM((k, n), _MXU), pltpu.SemaphoreType.DMA((1,)), three, three, three, three, sems, sems]),
        out_shape=[S((4 * m, k), _MXU), S((m, 4 * n), F32), S((m, 4 * n), _MXU)] + car.out_shapes,
        input_output_aliases={3 + a: 3 + o for a, o in car.aliases.items()},
        compiler_params=_cp("arbitrary", "arbitrary"), name=name)(place, x, gain, *car.inputs)
    car.results = list(res[3:])
    return res[0], res[1], res[2]


def _mm_nn(a, b3, *, bm, bn, name, also=None, comm=None):
    m, k = a.shape
    c, _, n = b3.shape
    ni, nj = m // bm, n // bn

    def body(a_ref, b_ref, *o_refs):
        res = _dot(a_ref[...], b_ref[...])
        for o_ref in o_refs:
            o_ref[...] = res.astype(o_ref.dtype)

    ospec = pl.BlockSpec((bm, bn), lambda cc, j, i: (i, cc * nj + j))
    dtypes = [F32] + ([] if also is None else [also])
    out = _call(
        comm, body, grid=(c, nj, ni),
        in_specs=[pl.BlockSpec((bm, k), lambda cc, j, i: (i, 0)), pl.BlockSpec((None, k, bn), lambda cc, j, i: (cc, 0, j))],
        out_specs=[ospec] * len(dtypes), out_shape=[S((m, c * n), dt) for dt in dtypes],
        compiler_params=_cp("parallel", "parallel", "parallel"), name=name)(a, b3)
    return out[0] if also is None else out


def _mm_nt(a, b3, *, bm, bo, out_dtype, name, comm=None):
    m = a.shape[0]
    c, ko, n = b3.shape
    ni, nj = m // bm, ko // bo

    def body(a_ref, b_ref, o_ref):
        acc = _dot_nt(a_ref[:, 0:n], b_ref[0])
        for cc in range(1, c):
            acc = acc + _dot_nt(a_ref[:, cc * n:(cc + 1) * n], b_ref[cc])
        o_ref[...] = acc.astype(o_ref.dtype)

    return _call(
        comm, body, grid=(nj, ni),
        in_specs=[pl.BlockSpec((bm, c * n), lambda j, i: (i, 0)),
                  pl.BlockSpec((c, bo, n), lambda j, i: (0, j, 0))],
        out_specs=pl.BlockSpec((bm, bo), lambda j, i: (i, j)),
        out_shape=S((m, ko), out_dtype),
        compiler_params=_cp("parallel", "parallel"), name=name)(a, b3)


def _mm_tn(a, b, c, *, bm, bk, out_dtype, name, comm=None):
    m, k = b.shape[0], a.shape[1]
    n = b.shape[1] // c
    nm, nk = m // bm, k // bk

    def body(a_ref, b_ref, o_ref, acc):
        mm = pl.program_id(2)

        @pl.when(mm == 0)
        def _():
            acc[...] = jnp.zeros_like(acc)

        acc[...] += _dot_tn(a_ref[...], b_ref[...])

        @pl.when(mm == nm - 1)
        def _():
            o_ref[...] = acc[...].astype(o_ref.dtype)

    return _call(
        comm, body, grid=(c, nk, nm),
        in_specs=[pl.BlockSpec((bm, bk), lambda cc, j, mm: (mm, j)),
                  pl.BlockSpec((bm, n), lambda cc, j, mm: (mm, cc))],
        out_specs=pl.BlockSpec((None, bk, n), lambda cc, j, mm: (cc, j, 0)),
        out_shape=S((c, k, n), out_dtype),
        scratch_shapes=[pltpu.VMEM((bk, n), F32)],
        compiler_params=_cp("parallel", "parallel", "arbitrary"), name=name)(a, b)


def _swiglu_fwd(hn, wg3, wu3, *, bm, name, comm=None):
    m, k = hn.shape
    c, _, n = wg3.shape

    def body(a_ref, g_ref, u_ref, dgate_ref, dup_ref, act_ref):
        a = a_ref[...]
        gate = _dot(a, g_ref[...])
        up = _dot(a, u_ref[...])
        sg = jax.nn.sigmoid(gate)
        silu = gate * sg
        dgate_ref[...] = (up * (sg * (1.0 + gate * (1.0 - sg)))).astype(dgate_ref.dtype)
        dup_ref[...] = silu.astype(dup_ref.dtype)
        act_ref[...] = (silu * up).astype(act_ref.dtype)

    wspec = pl.BlockSpec((None, k, n), lambda cc, i: (cc, 0, 0))
    ospec = pl.BlockSpec((bm, n), lambda cc, i: (i, cc))
    return _call(
        comm, body, grid=(c, m // bm),
        in_specs=[pl.BlockSpec((bm, k), lambda cc, i: (i, 0)), wspec, wspec],
        out_specs=[ospec, ospec, ospec],
        out_shape=[S((m, c * n), _MXU), S((m, c * n), _MXU), S((m, c * n), _MXU)],
        compiler_params=_cp("parallel", "parallel"), name=name)(hn, wg3, wu3)


def _swiglu_bwd(df, wd, act_dgate, act_dup, *, bm, bo, name):
    m, k = df.shape
    ko = wd.shape[0]

    def body(a_ref, b_ref, g_ref, u_ref, dg_ref, du_ref):
        dact = _dot_nt(a_ref[...], b_ref[...])
        dg_ref[...] = (dact * g_ref[...].astype(F32)).astype(dg_ref.dtype)
        du_ref[...] = (dact * u_ref[...].astype(F32)).astype(du_ref.dtype)

    ospec = pl.BlockSpec((bm, bo), lambda j, i: (i, j))
    return pl.pallas_call(
        body, grid=(ko // bo, m // bm),
        in_specs=[pl.BlockSpec((bm, k), lambda j, i: (i, 0)), pl.BlockSpec((bo, k), lambda j, i: (j, 0)), ospec, ospec],
        out_specs=[ospec, ospec],
        out_shape=[S((m, ko), _MXU), S((m, ko), _MXU)],
        compiler_params=_cp("parallel", "parallel"), name=name)(df, wd, act_dgate, act_dup)


def _rms_fwd(x, gain, name, comm=None):
    t, d = x.shape
    tm = min(t, ROW_TILE)

    def body(x_ref, g_ref, o_ref):
        xv = x_ref[...]
        o_ref[...] = ((xv * _rstd(xv)) * g_ref[...]).astype(o_ref.dtype)

    return _call(comm, body, grid=(t // tm,), in_specs=[_row_spec(tm, d), _vec_spec(d)], out_specs=_row_spec(tm, d),
                          out_shape=S((t, d), _MXU), compiler_params=_cp("parallel"), name=name)(x, gain)


def _outnorm_fwd(o, yl, ga, gl, name, comm=None):
    t, w = o.shape
    tm = min(t, ROW_TILE)

    def body(o_ref, l_ref, ga_ref, gl_ref, y_ref):
        ov, lv = o_ref[...], l_ref[...]
        y_ref[:, :w] = ((ov * _rstd(ov)) * ga_ref[...]).astype(y_ref.dtype)
        y_ref[:, w:] = ((lv * _rstd(lv)) * gl_ref[...]).astype(y_ref.dtype)

    return _call(comm, body, grid=(t // tm,), in_specs=[_row_spec(tm, w), _row_spec(tm, w), _vec_spec(w), _vec_spec(w)],
                 out_specs=_row_spec(tm, 2 * w), out_shape=S((t, 2 * w), _MXU),
                 compiler_params=_cp("parallel"), name=name)(o, yl, ga, gl)


def _mid_fwd(x, mix, g_post, g_pre, name, comm=None):
    t, d = x.shape
    tm = min(t, ROW_TILE)

    def body(x_ref, m_ref, gp_ref, gn_ref, x2_ref, hn_ref):
        mv = m_ref[...]
        x2 = x_ref[...] + (mv * _rstd(mv)) * gp_ref[...]
        x2_ref[...] = x2
        hn_ref[...] = ((x2 * _rstd(x2)) * gn_ref[...]).astype(hn_ref.dtype)

    return _call(comm, body, grid=(t // tm,), in_specs=[_row_spec(tm, d), _row_spec(tm, d), _vec_spec(d), _vec_spec(d)],
                          out_specs=[_row_spec(tm, d), _row_spec(tm, d)], out_shape=[S((t, d), F32), S((t, d), _MXU)],
                          compiler_params=_cp("parallel"), name=name)(x, mix, g_post, g_pre)


def _final(f, x2, target, g_post, name):
    t, d = f.shape
    tm = min(t, ROW_TILE)

    def body(f_ref, x2_ref, t_ref, g_ref, loss_ref, dout_ref, df_ref, dg_ref):
        @pl.when(pl.program_id(0) == 0)
        def _():
            loss_ref[...] = jnp.zeros_like(loss_ref)
            dg_ref[...] = jnp.zeros_like(dg_ref)

        fv = f_ref[...]
        r = _rstd(fv)
        fh = fv * r
        err = (x2_ref[...] + fh * g_ref[...]) - t_ref[...]
        loss_ref[...] += jnp.sum(err * err, axis=0, keepdims=True)
        dout = err * (1.0 / d)
        dout_ref[...] = dout
        dfv, dg = _rms_bwd(dout, fh, r, g_ref[...])
        df_ref[...] = dfv.astype(df_ref.dtype)
        dg_ref[...] += dg

    return pl.pallas_call(
        body, grid=(t // tm,),
        in_specs=[_row_spec(tm, d), _row_spec(tm, d), _row_spec(tm, d), _vec_spec(d)],
        out_specs=[_vec_spec(d), _row_spec(tm, d), _row_spec(tm, d), _vec_spec(d)],
        out_shape=[S((1, d), F32), S((t, d), F32), S((t, d), _MXU), S((1, d), F32)],
        compiler_params=_cp("arbitrary"), name=name)(f, x2, target, g_post)


def _mid_bwd(dhn_a, dhn_b, dout, x2, mix, g_pre, g_post, name, comm=None):
    t, d = x2.shape
    tm = min(t, ROW_TILE)

    def body(da_ref, db_ref, do_ref, x2_ref, m_ref, gn_ref, gp_ref, dx2_ref, dm_ref, dgn_ref, dgp_ref):
        @pl.when(pl.program_id(0) == 0)
        def _():
            dgn_ref[...] = jnp.zeros_like(dgn_ref)
            dgp_ref[...] = jnp.zeros_like(dgp_ref)

        x2 = x2_ref[...]
        r = _rstd(x2)
        dxa, dgn = _rms_bwd(da_ref[...] + db_ref[...], x2 * r, r, gn_ref[...])
        dx2 = do_ref[...] + dxa
        dx2_ref[...] = dx2
        dgn_ref[...] += dgn
        mv = m_ref[...]
        rm = _rstd(mv)
        dmv, dgp = _rms_bwd(dx2, mv * rm, rm, gp_ref[...])
        dm_ref[...] = dmv.astype(dm_ref.dtype)
        dgp_ref[...] += dgp

    rs, vs = _row_spec(tm, d), _vec_spec(d)
    return _call(
        comm, body, grid=(t // tm,), in_specs=[rs, rs, rs, rs, rs, vs, vs], out_specs=[rs, rs, vs, vs],
        out_shape=[S((t, d), F32), S((t, d), _MXU), S((1, d), F32), S((1, d), F32)],
        compiler_params=_cp("arbitrary"), name=name)(dhn_a, dhn_b, dout, x2, mix, g_pre, g_post)


def _first_bwd(dhn, dx2, x, gain, name, comm=None):
    t, d = x.shape
    tm = min(t, ROW_TILE)

    def body(dh_ref, dx2_ref, x_ref, g_ref, dx_ref, dg_ref):
        @pl.when(pl.program_id(0) == 0)
        def _():
            dg_ref[...] = jnp.zeros_like(dg_ref)

        xv = x_ref[...]
        r = _rstd(xv)
        dxa, dg = _rms_bwd(dh_ref[...], xv * r, r, g_ref[...])
        dx_ref[...] = dx2_ref[...] + dxa
        dg_ref[...] += dg

    rs, vs = _row_spec(tm, d), _vec_spec(d)
    return _call(comm, body, grid=(t // tm,), in_specs=[rs, rs, rs, vs], out_specs=[rs, vs],
                          out_shape=[S((t, d), F32), S((1, d), F32)], compiler_params=_cp("arbitrary"), name=name)(dhn, dx2, x, gain)


def _outnorm_bwd(dy, o, yl, ga, gl, name, comm=None):
    t, w = o.shape
    tm = min(t, ROW_TILE)

    def body(dy_ref, o_ref, l_ref, ga_ref, gl_ref, do_ref, dl_ref, dga_ref, dgl_ref):
        @pl.when(pl.program_id(0) == 0)
        def _():
            dga_ref[...] = jnp.zeros_like(dga_ref)
            dgl_ref[...] = jnp.zeros_like(dgl_ref)

        ov, lv = o_ref[...], l_ref[...]
        ra, rl = _rstd(ov), _rstd(lv)
        dov, dga = _rms_bwd(dy_ref[:, :w], ov * ra, ra, ga_ref[...])
        dlv, dgl = _rms_bwd(dy_ref[:, w:], lv * rl, rl, gl_ref[...])
        do_ref[...] = dov.astype(do_ref.dtype)
        dl_ref[...] = dlv
        dga_ref[...] += dga
        dgl_ref[...] += dgl

    rs, vs = _row_spec(tm, w), _vec_spec(w)
    return _call(comm, body, grid=(t // tm,), in_specs=[_row_spec(tm, 2 * w), rs, rs, vs, vs], out_specs=[rs, rs, vs, vs],
                          out_shape=[S((t, w), _MXU), S((t, w), F32), S((1, w), F32), S((1, w), F32)],
                          compiler_params=_cp("arbitrary"), name=name)(dy, o, yl, ga, gl)


def _tri_sum(v, tri):
    return _dot(v.astype(_MXU), tri)


def _attn_tile(qb, kb, row, col, shift, scale):
    z = _dot_nt(qb, kb) * scale
    mask = (col + shift) < row
    lb = _log_sigmoid(z)
    lm = jnp.where(mask, lb - z, 0.0)
    return mask, lb, lm


def _attn_fwd(proj, n_heads, name, comm=None):
    t = proj.shape[0]
    bq = min(t, ATTN_BLOCK)
    nq = t // bq
    scale = 1.0 / math.sqrt(HEAD_DIM)

    heads = [slice(a * HEAD_DIM, (a + 1) * HEAD_DIM) for a in range(ATTN_HEADS)]

    def body(q_ref, k_ref, v_ref, o_ref):
        row = lax.broadcasted_iota(jnp.int32, (bq, bq), 0)
        col = lax.broadcasted_iota(jnp.int32, (bq, bq), 1)
        tri = (row > col).astype(_MXU)

        def per_q(qi, _):
            q0 = pl.multiple_of(qi * bq, bq)
            qbs = [q_ref[pl.ds(q0, bq), hd] for hd in heads]

            def cond(st):
                return jnp.logical_and(st[0] >= 0, st[1])

            def step(st):
                kj, _, carries, accs = st
                k0 = pl.multiple_of(kj * bq, bq)
                alive, new_carries, new_accs = None, [], []
                for hd, qb, carry, acc in zip(heads, qbs, carries, accs):
                    mask, lb, lm = _attn_tile(qb, k_ref[pl.ds(k0, bq), hd], row, col, (kj - qi) * bq, scale)
                    w = jnp.where(mask, jnp.exp(lb + _tri_sum(lm, tri) + carry), 0.0)
                    new_accs.append(acc + _dot(w.astype(_MXU), v_ref[pl.ds(k0, bq), hd]))
                    carry = carry + jnp.sum(lm, axis=1, keepdims=True)
                    new_carries.append(carry)
                    live = jnp.max(carry) > EXP_CUT
                    alive = live if alive is None else jnp.logical_or(alive, live)
                return kj - 1, alive, tuple(new_carries), tuple(new_accs)

            st = lax.while_loop(cond, step, (qi, jnp.bool_(True), (jnp.zeros((bq, 1), F32),) * ATTN_HEADS,
                                             (jnp.zeros((bq, HEAD_DIM), F32),) * ATTN_HEADS))
            for hd, acc in zip(heads, st[3]):
                o_ref[pl.ds(q0, bq), hd] = acc
            return 0

        lax.fori_loop(0, nq, per_q, 0)

    groups = n_heads // ATTN_HEADS
    hs = lambda off: pl.BlockSpec((t, ATTN_HEADS * HEAD_DIM), lambda h: (0, off + h))
    return _call(
        comm, body, grid=(groups,), in_specs=[hs(0), hs(groups), hs(2 * groups)], out_specs=hs(0),
        out_shape=S((t, n_heads * HEAD_DIM), F32), compiler_params=_cp("parallel"), name=name)(proj, proj, proj)


def _emit(blocks, out_ref, starts, sems):
    copies = [pltpu.make_async_copy(b, out_ref.at[:, pl.ds(c0, b.shape[1])], sems.at[k]) for k, (b, c0) in enumerate(zip(blocks, starts))]
    for cp in copies:
        cp.start()
    for cp in copies:
        cp.wait()


def _attn_bwd(proj, do, dproj, n_heads, name, comm=None):
    t = proj.shape[0]
    bq = min(t, ATTN_BLOCK)
    nq = t // bq
    scale = 1.0 / math.sqrt(HEAD_DIM)
    groups = n_heads // ATTN_HEADS
    wide = ATTN_HEADS * HEAD_DIM

    heads = [slice(a * HEAD_DIM, (a + 1) * HEAD_DIM) for a in range(ATTN_HEADS)]

    def body(q_ref, k_ref, v_ref, do_ref, _, dproj_ref, dka_ref, dva_ref, g_ref, b_ref, dq_ref, dk_ref, dv_ref, out_sems):
        group = pl.program_id(0)
        dka_ref[...] = jnp.zeros_like(dka_ref)
        dva_ref[...] = jnp.zeros_like(dva_ref)
        row = lax.broadcasted_iota(jnp.int32, (bq, bq), 0)
        col = lax.broadcasted_iota(jnp.int32, (bq, bq), 1)
        tri = (row > col).astype(_MXU)
        tri_lt = (row < col).astype(_MXU)

        def per_q(qi, _):
            q0 = pl.multiple_of(qi * bq, bq)
            qbs = [q_ref[pl.ds(q0, bq), hd] for hd in heads]
            dobs = [do_ref[pl.ds(q0, bq), hd] for hd in heads]

            def cond(st):
                return jnp.logical_and(st[0] >= 0, st[1])

            def step(st):
                kj, _, carries = st
                k0 = pl.multiple_of(kj * bq, bq)
                alive, new_carries = None, []
                for a, (hd, qb, dob, carry) in enumerate(zip(heads, qbs, dobs, carries)):
                    mask, lb, lm = _attn_tile(qb, k_ref[pl.ds(k0, bq), hd], row, col, (kj - qi) * bq, scale)
                    w = jnp.where(mask, jnp.exp(lb + _tri_sum(lm, tri) + carry), 0.0)
                    g_ref[a, pl.ds(k0, bq), :] = w * _dot_nt(dob, v_ref[pl.ds(k0, bq), hd])
                    b_ref[a, pl.ds(k0, bq), :] = jnp.where(mask, jnp.exp(lb), 0.0)
                    dva_ref[pl.ds(k0, bq), hd] += _dot_tn(w.astype(_MXU), dob)
                    carry = carry + jnp.sum(lm, axis=1, keepdims=True)
                    new_carries.append(carry)
                    live = jnp.max(carry) > EXP_CUT
                    alive = live if alive is None else jnp.logical_or(alive, live)
                return kj - 1, alive, tuple(new_carries)

            st = lax.while_loop(cond, step, (qi, jnp.bool_(True), (jnp.zeros((bq, 1), F32),) * ATTN_HEADS))

            def back(kj, st2):
                k0 = pl.multiple_of(kj * bq, bq)
                out = []
                for a, (hd, qb, (before, dq)) in enumerate(zip(heads, qbs, st2)):
                    g = g_ref[a, pl.ds(k0, bq), :]
                    beta = b_ref[a, pl.ds(k0, bq), :]
                    dz = ((g * (1.0 - beta) - (before + _tri_sum(g, tri_lt)) * beta) * scale).astype(_MXU)
                    dka_ref[pl.ds(k0, bq), hd] += _dot_tn(dz, qb)
                    out.append((before + jnp.sum(g, axis=1, keepdims=True), dq + _dot(dz, k_ref[pl.ds(k0, bq), hd])))
                return tuple(out)

            st2 = lax.fori_loop(st[0] + 1, qi + 1, back, ((jnp.zeros((bq, 1), F32), jnp.zeros((bq, HEAD_DIM), F32)),) * ATTN_HEADS)
            for hd, (_, dq) in zip(heads, st2):
                dq_ref[pl.ds(q0, bq), hd] = dq.astype(dq_ref.dtype)
            return 0

        lax.fori_loop(0, nq, per_q, 0)
        dk_ref[...] = dka_ref[...].astype(dk_ref.dtype)
        dv_ref[...] = dva_ref[...].astype(dv_ref.dtype)
        _emit([dq_ref, dk_ref, dv_ref], dproj_ref, [(a * groups + group) * wide for a in range(3)], out_sems)

    hs = lambda off: pl.BlockSpec((t, wide), lambda h: (0, off + h))
    return _call(
        comm, body, grid=(groups,), in_specs=[hs(0), hs(groups), hs(2 * groups), hs(0), _ANY], out_specs=_ANY,
        out_shape=S(dproj.shape, dproj.dtype), input_output_aliases={4: 0},
        scratch_shapes=[pltpu.VMEM((t, wide), F32), pltpu.VMEM((t, wide), F32),
                        pltpu.VMEM((ATTN_HEADS, t, bq), F32), pltpu.VMEM((ATTN_HEADS, t, bq), F32)]
        + [pltpu.VMEM((t, wide), dproj.dtype)] * 3 + [pltpu.SemaphoreType.DMA((3,))],
        compiler_params=_cp("parallel"), name=name)(proj, proj, proj, do, dproj)


def _shift_down(cur, prev8, k):
    if k == 0:
        return cur
    row8 = lax.broadcasted_iota(jnp.int32, prev8.shape, 0)
    rc = pltpu.roll(cur, k, 0)
    top = jnp.where(row8 < k, pltpu.roll(prev8, k, 0), rc[0:8, :])
    return jnp.concatenate([top, rc[8:, :]], axis=0)


def _shift_up(cur, next8, k):
    if k == 0:
        return cur
    n = cur.shape[0]
    row8 = lax.broadcasted_iota(jnp.int32, next8.shape, 0)
    rc = pltpu.roll(cur, n - k, 0)
    bottom = jnp.where(row8 >= 8 - k, pltpu.roll(next8, 8 - k, 0), rc[n - 8:, :])
    return jnp.concatenate([rc[:n - 8, :], bottom], axis=0)


def _lru_conv(xl, prev8, cw, cb):
    xs = [_shift_down(xl, prev8, CONV_WIDTH - 1 - k) for k in range(CONV_WIDTH)]
    xc = xs[0] * cw[0:1, :]
    for k in range(1, CONV_WIDTH):
        xc = xc + xs[k] * cw[k:k + 1, :]
    return xs, xc + cb


def _lru_gates(xl, prev8, cw, cb, wr, br, wi, bi, ls):
    xs, xc = _lru_conv(xl, prev8, cw, cb)
    xcb = xc.astype(_MXU)
    r = jax.nn.sigmoid(_dot(xcb, wr) + br)
    i = jax.nn.sigmoid(_dot(xcb, wi) + bi)
    la = (LRU_C * r) * ls
    a = jnp.exp(la)
    mult = jnp.sqrt(-_expm1(2.0 * la))
    return xs, xc, r, i, a, mult


def _group_scan(a, b, reverse):
    n = a.shape[0]
    row = lax.broadcasted_iota(jnp.int32, a.shape, 0) % 8
    for d in (1, 2, 4):
        if reverse:
            m = row < 8 - d
            a_s, b_s = pltpu.roll(a, n - d, 0), pltpu.roll(b, n - d, 0)
        else:
            m = row >= d
            a_s, b_s = pltpu.roll(a, d, 0), pltpu.roll(b, d, 0)
        b = jnp.where(m, a * b_s + b, b)
        a = jnp.where(m, a * a_s, a)
    return a, b


def _lru_fwd(proj, col0, n_blocks, cw, cb, wr, br, wi, bi, lam, name, comm=None):
    t = proj.shape[0]
    tt = min(t, SEQ_TILE)
    nt = t // tt

    def body(xl_ref, gl_ref, cw_ref, cb_ref, wr_ref, br_ref, wi_ref, bi_ref, lam_ref, h_ref, y_ref, *kept):
        cwv, cbv, brv, biv = cw_ref[...], cb_ref[...], br_ref[...], bi_ref[...]
        wrv, wiv = wr_ref[...].astype(_MXU), wi_ref[...].astype(_MXU)
        ls = _log_sigmoid(lam_ref[...])

        def tile(ti, hin):
            t0 = pl.multiple_of(ti * tt, tt)
            p0 = pl.multiple_of(jnp.maximum(t0 - 8, 0), 8)
            prev8 = xl_ref[pl.ds(p0, 8), :] * (ti > 0).astype(F32)
            xl = xl_ref[pl.ds(t0, tt), :]
            _, xc, r, ig, a, mult = _lru_gates(xl, prev8, cwv, cbv, wrv, brv, wiv, biv, ls)
            for ref, val in zip(kept, (r, ig, a, mult)):
                ref[pl.ds(t0, tt), :] = val
            ga, gb = _group_scan(a, mult * (ig * xc), False)
            for g in range(tt // 8):
                hg = ga[8 * g:8 * g + 8, :] * hin + gb[8 * g:8 * g + 8, :]
                h_ref[pl.ds(t0 + 8 * g, 8), :] = hg
                hin = hg[7:8, :]
            y_ref[pl.ds(t0, tt), :] = h_ref[pl.ds(t0, tt), :] * _gelu(gl_ref[pl.ds(t0, tt), :])
            return hin

        lax.fori_loop(0, nt, tile, jnp.zeros((1, HEAD_DIM), F32))

    cs = lambda off: pl.BlockSpec((t, HEAD_DIM), lambda n: (0, off + n))
    vs = pl.BlockSpec((1, HEAD_DIM), lambda n: (0, n))
    ws = pl.BlockSpec((None, HEAD_DIM, HEAD_DIM), lambda n: (n, 0, 0))
    w = n_blocks * HEAD_DIM
    return _call(
        comm, body, grid=(n_blocks,),
        in_specs=[cs(col0), cs(col0 + n_blocks), pl.BlockSpec((CONV_WIDTH, HEAD_DIM), lambda n: (0, n)), vs, ws, vs, ws, vs, vs],
        out_specs=[cs(0)] * 6, out_shape=[S((t, w), F32)] * 6,
        compiler_params=_cp("parallel"), name=name)(proj, proj, cw, cb, wr, br, wi, bi, lam)


def _lru_bwd(proj, col0, n_blocks, h, kept, dyl, cw, cb, wr, wi, lam, name, comm=None):
    t = proj.shape[0]
    tt = min(t, SEQ_TILE)
    nt = t // tt

    def body(xl_ref, gl_ref, h_ref, r_ref, i_ref, a_ref, m_ref, dy_ref, cw_ref, cb_ref, wr_ref, wi_ref, lam_ref,
             dproj_ref, dcw_ref, dcb_ref, dwr_ref, dbr_ref, dwi_ref, dbi_ref, dlam_ref, g_ref, dxl_ref, dgl_ref, out_sems):
        block = pl.program_id(0)
        cwv, cbv = cw_ref[...], cb_ref[...]
        wrv, wiv = wr_ref[...].astype(_MXU), wi_ref[...].astype(_MXU)
        lamv = lam_ref[...]
        ls = _log_sigmoid(lamv)
        for ref in (dcw_ref, dcb_ref, dwr_ref, dbr_ref, dwi_ref, dbi_ref, dlam_ref):
            ref[...] = jnp.zeros_like(ref)

        def tile(s, carry):
            e_in, dxc_next8 = carry
            ti = nt - 1 - s
            t0 = pl.multiple_of(ti * tt, tt)
            p0 = pl.multiple_of(jnp.maximum(t0 - 8, 0), 8)
            first = (ti > 0).astype(F32)
            xl = xl_ref[pl.ds(t0, tt), :]
            xs, xc = _lru_conv(xl, xl_ref[pl.ds(p0, 8), :] * first, cwv, cbv)
            r, ig, a, mult = (ref[pl.ds(t0, tt), :] for ref in (r_ref, i_ref, a_ref, m_ref))
            hv = h_ref[pl.ds(t0, tt), :]
            h_before = _shift_down(hv, h_ref[pl.ds(p0, 8), :] * first, 1)
            glv = gl_ref[pl.ds(t0, tt), :]
            dyv = dy_ref[pl.ds(t0, tt), :]
            dgl_ref[pl.ds(t0, tt), :] = (dyv * hv * _gelu_grad(glv)).astype(dgl_ref.dtype)
            dh = dyv * _gelu(glv)
            row = lax.broadcasted_iota(jnp.int32, a.shape, 0)
            coef = jnp.where(row == tt - 1, 1.0, pltpu.roll(a, tt - 1, 0))
            ga, gb = _group_scan(coef, dh, True)
            gin = e_in
            for g in reversed(range(tt // 8)):
                gg = ga[8 * g:8 * g + 8, :] * gin + gb[8 * g:8 * g + 8, :]
                g_ref[8 * g:8 * g + 8, :] = gg
                gin = gg[0:1, :]
            gv = g_ref[...]
            e_out = a[0:1, :] * gv[0:1, :]
            ix = ig * xc
            dla = (gv * h_before) * a - (gv * ix) * (a * a / mult)
            dlam_ref[...] += jnp.sum(dla * (LRU_C * r), axis=0, keepdims=True)
            dpr = (dla * (LRU_C * ls)) * (r * (1.0 - r))
            dpi = (gv * mult * xc) * (ig * (1.0 - ig))
            dbr_ref[...] += jnp.sum(dpr, axis=0, keepdims=True)
            dbi_ref[...] += jnp.sum(dpi, axis=0, keepdims=True)
            xcb, dprb, dpib = xc.astype(_MXU), dpr.astype(_MXU), dpi.astype(_MXU)
            dwr_ref[...] += _dot_tn(xcb, dprb)
            dwi_ref[...] += _dot_tn(xcb, dpib)
            dxc = gv * mult * ig + _dot_nt(dprb, wrv) + _dot_nt(dpib, wiv)
            dcb_ref[...] += jnp.sum(dxc, axis=0, keepdims=True)
            dxl = None
            for k in range(CONV_WIDTH):
                dcw_ref[k:k + 1, :] += jnp.sum(dxc * xs[k], axis=0, keepdims=True)
                term = _shift_up(dxc, dxc_next8, CONV_WIDTH - 1 - k) * cwv[k:k + 1, :]
                dxl = term if dxl is None else dxl + term
            dxl_ref[pl.ds(t0, tt), :] = dxl.astype(dxl_ref.dtype)
            return e_out, dxc[0:8, :]

        lax.fori_loop(0, nt, tile, (jnp.zeros((1, HEAD_DIM), F32), jnp.zeros((8, HEAD_DIM), F32)))
        dlam_ref[...] = dlam_ref[...] * (1.0 - jax.nn.sigmoid(lamv))
        _emit([dxl_ref, dgl_ref], dproj_ref, [(col0 + block) * HEAD_DIM, (col0 + n_blocks + block) * HEAD_DIM], out_sems)

    cs = lambda off: pl.BlockSpec((t, HEAD_DIM), lambda n: (0, off + n))
    vs = pl.BlockSpec((1, HEAD_DIM), lambda n: (0, n))
    ws = pl.BlockSpec((None, HEAD_DIM, HEAD_DIM), lambda n: (n, 0, 0))
    cws = pl.BlockSpec((CONV_WIDTH, HEAD_DIM), lambda n: (0, n))
    w = n_blocks * HEAD_DIM
    vec = S((1, w), F32)
    mat = S((n_blocks, HEAD_DIM, HEAD_DIM), F32)
    return _call(
        comm, body, grid=(n_blocks,),
        in_specs=[cs(col0), cs(col0 + n_blocks)] + [cs(0)] * 6 + [cws, vs, ws, ws, vs],
        out_specs=[_ANY, cws, vs, ws, vs, ws, vs, vs],
        out_shape=[S(proj.shape, _MXU), S((CONV_WIDTH, w), F32), vec, mat, vec, mat, vec, vec],
        scratch_shapes=[pltpu.VMEM((tt, HEAD_DIM), F32), pltpu.VMEM((t, HEAD_DIM), _MXU), pltpu.VMEM((t, HEAD_DIM), _MXU),
                        pltpu.SemaphoreType.DMA((2,))],
        compiler_params=_cp("parallel"), name=name)(proj, proj, h, *kept, dyl, cw, cb, wr, wi, lam)


class _NoExchange:
    grad_dtype = F32

    def __init__(self, weights):
        self.weights, self.grads, self.packs = weights, {}, {}

    def weight(self, name):
        return self.weights[name]

    def in_proj(self, x, gain, bm):
        hn = _rms_fwd(x, gain, "rms1")
        return [hn, *_mm_nn(hn, self.weights["w_in"], bm=bm, bn=self.weights["w_in"].shape[2], name="in_proj", also=_MXU)]

    def conv_w(self):
        return self.weights["conv_w"]

    def carrier(self, call):
        return None

    def harvest(self, car):
        pass

    def alone(self, call):
        pass


def _local_step(x, target, norms, ex, cb, wr, br, wi, bi, lam, ga, gl):
    g_pre_mix, g_post_mix, g_pre_ffn, g_post_ffn = norms
    t, d = x.shape
    bm = min(t, 512)
    bt = min(t, 2048)

    def run(fn, name, *args, **kw):
        car = ex.carrier(name)
        out = fn(*args, name=name, comm=car, **kw)
        ex.harvest(car)
        return out

    hn1, proj, proj_mx = ex.in_proj(x, g_pre_mix, bm)
    win3, cw = ex.weight("w_in"), ex.conv_w()
    c = win3.shape[0]
    o = run(_attn_fwd, "attn_fwd", proj_mx, (proj.shape[1] - d) // 3 // HEAD_DIM)
    mix = 2 * o.shape[1]
    n_heads = n_blocks = o.shape[1] // HEAD_DIM
    h, yl, *kept = run(_lru_fwd, "lru_fwd", proj, 3 * n_heads, n_blocks, cw, cb, wr, br, wi, bi, lam)
    y = run(_outnorm_fwd, "outnorm_fwd", o, yl, ga, gl)
    wout = ex.weight("w_out")
    mixo = run(_mm_nn, "out_proj", y, wout[None], bm=bm, bn=d)
    x2, hn2 = run(_mid_fwd, "mid_fwd", x, mixo, g_post_mix, g_pre_ffn)
    wg3, wu3 = ex.weight("w_ffn_gate"), ex.weight("w_ffn_up")
    act_dgate, act_dup, act = run(_swiglu_fwd, "ffn_gate_up", hn2, wg3, wu3, bm=bm)
    ex.alone("gather_w_down")
    wd = ex.weight("w_ffn_down")
    ff = wd.shape[0]
    f = _mm_nn(act, wd[None], bm=bm, bn=d // 2, name="ffn_down")
    loss_cols, dout, df, dg_post_ffn = _final(f, x2, target, g_post_ffn, "final")

    dgate, dup = _swiglu_bwd(df, wd, act_dgate, act_dup, bm=min(t, 1024), bo=ff // 4, name="ffn_down_bwd")
    ex.grads["w_ffn_down"] = _mm_tn(act, df, 1, bm=bt, bk=512, out_dtype=ex.grad_dtype, name="ffn_down_dw").reshape(c, ff // c, d)
    ex.grads["w_ffn_gate"] = run(_mm_tn, "ffn_gate_dw", hn2, dgate, c, bm=bt, bk=d // 2, out_dtype=ex.grad_dtype)
    ex.grads["w_ffn_up"] = run(_mm_tn, "ffn_up_dw", hn2, dup, c, bm=bt, bk=d // 2, out_dtype=ex.grad_dtype)
    dhn2_g = run(_mm_nt, "ffn_gate_dx", dgate, wg3, bm=bm, bo=d // 2, out_dtype=F32)
    dhn2_u = run(_mm_nt, "ffn_up_dx", dup, wu3, bm=bm, bo=d // 2, out_dtype=F32)
    dx2, dmix, dg_pre_ffn, dg_post_mix = run(_mid_bwd, "mid_bwd", dhn2_g, dhn2_u, dout, x2, mixo, g_pre_ffn, g_post_mix)
    dy = run(_mm_nt, "out_proj_dx", dmix, wout[None], bm=bm, bo=mix, out_dtype=F32)
    ex.grads["w_out"] = _mm_tn(y, dmix, 1, bm=bt, bk=mix // 4, out_dtype=ex.grad_dtype, name="out_proj_dw").reshape(c, mix // c, d)
    do, dyl, dga, dgl_norm = run(_outnorm_bwd, "outnorm_bwd", dy, o, yl, ga, gl)
    dproj, dcw, dcb, dwr, dbr, dwi, dbi, dlam = run(_lru_bwd, "lru_bwd", proj, 3 * n_heads, n_blocks, h, kept, dyl, cw, cb, wr, wi, lam)
    small = dict(post_mix_norm=dg_post_mix, pre_ffn_norm=dg_pre_ffn, post_ffn_norm=dg_post_ffn, conv_w=dcw, conv_b=dcb,
                 w_rgate=dwr, b_rgate=dbr, w_igate=dwi, b_igate=dbi, lru_lambda=dlam, attn_out_norm=dga, lru_out_norm=dgl_norm)
    ex.packs["early"] = _pack([small[n] for n in _SMALL_EARLY])
    dproj = run(_attn_bwd, "attn_bwd", proj_mx, do, dproj, n_heads)
    ex.grads["w_in"] = _mm_tn(hn1, dproj, c, bm=bt, bk=d // 2, out_dtype=ex.grad_dtype, name="in_proj_dw")
    ex.alone("grads_w_in_swap")
    dhn1 = run(_mm_nt, "in_proj_dx", dproj, win3, bm=bm, bo=d // 2, out_dtype=F32)
    grad_x, small["pre_mix_norm"] = run(_first_bwd, "first_bwd", dhn1, dx2, x, g_pre_mix)
    ex.packs["late"] = _pack([small["pre_mix_norm"], (0.5 / d) * jnp.sum(loss_cols, keepdims=True)])
    return loss_cols, grad_x, small


def _into_slot(wsh, slot, dtype, name):
    rows, n = wsh.shape
    rb = _row_block(rows, 256) if rows % 8 == 0 else rows

    def body(s_ref, w_ref, o_ref):
        o_ref[...] = w_ref[...].astype(o_ref.dtype)

    return pl.pallas_call(
        body,
        grid_spec=pltpu.PrefetchScalarGridSpec(
            num_scalar_prefetch=1, grid=(rows // rb,),
            in_specs=[pl.BlockSpec((rb, n), lambda i, s_ref: (i, 0))],
            out_specs=pl.BlockSpec((None, rb, n), lambda i, s_ref: (s_ref[0], i, 0))),
        out_shape=S((4, rows, n), dtype), compiler_params=_cp("parallel"), name=name)(slot, wsh)


class _Exchange:
    SCHEDULE = {
        "in_proj": [("stream", "w_in"), ("ici", "conv_w"), ("ici", "w_ffn_up", 0)],
        "attn_fwd": [("d2d", "w_ffn_up", 0), ("ici", "w_ffn_gate")],
        "lru_fwd": [("d2d", "w_ffn_gate"), ("ici", "w_out"), ("ici", "w_ffn_up", 1)],
        "outnorm_fwd": [("d2d", "w_out"), ("d2d", "w_ffn_up", 1)],
        "out_proj": [("ici", "w_ffn_up", 2), ("ici", "w_ffn_up", 3)],
        "mid_fwd": [("d2d", "w_ffn_up", 2), ("d2d", "w_ffn_up", 3)],
        "ffn_gate_up": [("ici", "w_ffn_down")],
        "gather_w_down": [("d2d", "w_ffn_down")],
        "ffn_gate_dw": [("swap", "w_ffn_down")],
        "ffn_up_dw": [("scatter", "w_ffn_down", 0), ("scatter", "w_ffn_down", 1), ("scatter", "w_ffn_down", 2), ("swap", "w_ffn_gate")],
        "ffn_gate_dx": [("scatter", "w_ffn_down", 3), ("scatter", "w_ffn_gate", 0), ("scatter", "w_ffn_gate", 1), ("swap", "w_ffn_up")],
        "ffn_up_dx": [("share", "w_ffn_down"), ("scatter", "w_ffn_gate", 2), ("scatter", "w_ffn_gate", 3), ("scatter", "w_ffn_up", 0)],
        "mid_bwd": [("share", "w_ffn_gate"), ("scatter", "w_ffn_up", 1), ("scatter", "w_ffn_up", 2)],
        "out_proj_dx": [("scatter", "w_ffn_up", 3)],
        "outnorm_bwd": [("share", "w_ffn_up"), ("swap", "w_out")],
        "lru_bwd": [("scatter", "w_out")],
        "attn_bwd": [("share", "w_out"), ("spread", "early")],
        "grads_w_in_swap": [("swap", "w_in")],
        "in_proj_dx": [("scatter", "w_in")],
        "grads_w_in_share": [("share", "w_in"), ("spread", "late")],
    }
    PIECES = 4
    grad_dtype = BF16

    def __init__(self, slots, place):
        self.buf, self.place = dict(slots), place
        self.grads, self.packs, self.swapped, self.part, self.scattered, self.full, self.spreaded = {}, {}, {}, {}, {}, {}, {}

    def weight(self, name):
        b = self.buf[name]
        return b.reshape(-1, b.shape[2]) if name in ("w_out", "w_ffn_down") else b

    def in_proj(self, x, gain, bm):
        car = self.carrier("in_proj")
        out = _in_proj_streamed(x, gain, car, car.streamed, self.place, bm=bm, name="in_proj")
        self.harvest(car)
        return out

    def conv_w(self):
        return jnp.transpose(self.buf["conv_w"], (1, 0, 2)).reshape(CONV_WIDTH, -1)

    def carrier(self, call):
        if call not in self.SCHEDULE:
            return None
        car = _Carrier()
        car.todo, slot = [], {}
        for kind, name, *piece in self.SCHEDULE[call]:
            if kind in ("ici", "d2d", "stream"):
                if name not in slot:
                    slot[name] = car.inplace(self.buf[name])
                    car.todo.append((self.buf, name, slot[name]))
            if kind == "stream":
                car.streamed = slot[name]
            elif kind in ("ici", "d2d"):
                size = self.buf[name].shape[1] // 2 // self.PIECES
                rows = (piece[0] * size, size) if piece else None
                if kind == "ici":
                    car.gather_ici(slot[name], rows, split=name != "conv_w")
                else:
                    car.gather_d2d(slot[name], rows)
            elif kind == "swap":
                g = self.grads[name]
                o = car.fresh((4, g.shape[1] // 2, g.shape[2]), g.dtype)
                car.swap(car.read(g), o)
                car.todo.append((self.swapped, name, o))
            elif kind == "scatter":
                if name not in self.part:
                    self.part[name] = _add_own_half(self.grads[name], self.swapped[name], self.place[1:], "grads_add_" + name)
                p = self.part[name]
                key = ("scatter", name)
                if key not in slot:
                    slot[key] = (car.read(p), car.inplace(self.scattered[name]) if name in self.scattered else car.fresh(p.shape, p.dtype))
                    car.todo.append((self.scattered, name, slot[key][1]))
                size = p.shape[1] // self.PIECES
                car.scatter(*slot[key], (piece[0] * size, size) if piece else None)
            elif kind == "share":
                o = car.inplace(_sum_chips(self.part[name], self.scattered[name], self.place, "grads_sum_" + name))
                car.share(o)
                car.todo.append((self.full, name, o))
            else:
                o = car.fresh((8,) + self.packs[name].shape, F32)
                car.spread(car.read(self.packs[name]), o)
                car.todo.append((self.spreaded, name, o))
        return car

    def harvest(self, car):
        for state, name, o in (car.todo if car is not None else []):
            state[name] = car.results[o]

    def alone(self, call):
        car = self.carrier(call)
        car.run_alone(call)
        self.harvest(car)

    def small_sum(self, key):
        return _sum_devices(self.packs[key], self.spreaded[key], 2 * self.place[0:1] + self.place[1:], "grads_small_sum_" + key)


def _row_block(rows, cap):
    return max(b for b in range(8, cap + 1, 8) if rows % b == 0)


def _add_own_half(g, recv, core, name):
    _, rows, n = g.shape
    half = rows // 2
    rb = _row_block(half, 512)
    nb = half // rb

    def body(c_ref, g_ref, r_ref, o_ref):
        o_ref[...] = (g_ref[...].astype(F32) + r_ref[...].astype(F32)).astype(o_ref.dtype)

    return pl.pallas_call(
        body,
        grid_spec=pltpu.PrefetchScalarGridSpec(
            num_scalar_prefetch=1, grid=(4, nb),
            in_specs=[pl.BlockSpec((None, rb, n), lambda k, i, c_ref: (k, c_ref[0] * nb + i, 0)),
                      pl.BlockSpec((None, rb, n), lambda k, i, c_ref: (k, i, 0))],
            out_specs=pl.BlockSpec((None, rb, n), lambda k, i, c_ref: (k, i, 0))),
        out_shape=S((4, half, n), BF16), compiler_params=_cp("parallel", "parallel"), name=name)(core, g, recv)


def _sum_chips(part, recv, place, name):
    _, rows, n = part.shape
    rb = _row_block(rows, 64)
    nb = rows // rb

    def body(p_ref, own_ref, r0, r1, r2, r3, o_ref):
        own = own_ref[...].astype(F32)
        terms = [jnp.where(p_ref[0] == k, own, r[...].astype(F32)) for k, r in enumerate((r0, r1, r2, r3))]
        o_ref[...] = ((terms[0] + terms[1]) + terms[2]) + terms[3]

    def slot(k):
        return pl.BlockSpec((None, rb, n), lambda i, p_ref: (jnp.where(p_ref[0] == k, (k + 1) % 4, k), i, 0))

    return pl.pallas_call(
        body,
        grid_spec=pltpu.PrefetchScalarGridSpec(
            num_scalar_prefetch=1, grid=(nb,),
            in_specs=[pl.BlockSpec((None, rb, n), lambda i, p_ref: (p_ref[0], i, 0))] + [slot(k) for k in range(4)],
            out_specs=pl.BlockSpec((rb, n), lambda i, p_ref: (p_ref[1] * nb + i, 0))),
        out_shape=S((2 * rows, n), F32), compiler_params=_cp("parallel"), name=name)(place, part, recv, recv, recv, recv)


def _sum_devices(own, spread, me, name):
    rows = own.shape[0]

    def body(me_ref, own_ref, *refs):
        acc = None
        for k, r in enumerate(refs[:8]):
            term = jnp.where(me_ref[0] == k, own_ref[...], r[...])
            acc = term if acc is None else acc + term
        refs[8][...] = acc

    def slot(k):
        return pl.BlockSpec((None, rows, 128), lambda i, me_ref: (jnp.where(me_ref[0] == k, (k + 1) % 8, k), 0, 0))

    whole = pl.BlockSpec((rows, 128), lambda i, me_ref: (0, 0))
    return pl.pallas_call(
        body,
        grid_spec=pltpu.PrefetchScalarGridSpec(num_scalar_prefetch=1, grid=(1,), in_specs=[whole] + [slot(k) for k in range(8)],
                                               out_specs=whole),
        out_shape=S((rows, 128), F32), compiler_params=_cp("arbitrary"), name=name)(me, own, *[spread] * 8)


def _adamw(w, g, m, v, name, regive=False):
    rows, n = w.shape
    rb = rows if rows * n * 4 <= (1 << 21) else _row_block(rows, 256)
    c1 = 1.0 - ADAM_B1 ** ADAM_STEP
    c2 = 1.0 - ADAM_B2 ** ADAM_STEP

    def body(w_ref, g_ref, m_ref, v_ref, d_ref, nm_ref, nv_ref, *again):
        gv = g_ref[...]
        for ref in again:
            ref[...] = gv
        nm = ADAM_B1 * m_ref[...] + (1.0 - ADAM_B1) * gv
        nv = ADAM_B2 * v_ref[...] + (1.0 - ADAM_B2) * (gv * gv)
        nm_ref[...] = nm
        nv_ref[...] = nv
        d_ref[...] = -ADAM_LR * ((nm / c1) / (jnp.sqrt(nv / c2) + ADAM_EPS) + ADAM_WD * w_ref[...])

    bs = pl.BlockSpec((rb, n), lambda i: (i, 0))
    n_out = 4 if regive else 3
    return pl.pallas_call(body, grid=(rows // rb,), in_specs=[bs] * 4, out_specs=[bs] * n_out, out_shape=[S((rows, n), F32)] * n_out,
                          compiler_params=_cp("parallel"), name=name)(w, g, m, v)


_BIG = ("w_in", "w_out", "w_ffn_gate", "w_ffn_up", "w_ffn_down")
_SMALL = ("pre_mix_norm", "post_mix_norm", "pre_ffn_norm", "post_ffn_norm", "conv_w", "conv_b", "w_rgate", "b_rgate",
          "w_igate", "b_igate", "lru_lambda", "attn_out_norm", "lru_out_norm")
_SMALL_EARLY = _SMALL[1:]
_WEIGHTS = ("pre_mix_norm", "post_mix_norm", "pre_ffn_norm", "post_ffn_norm", "w_in", "conv_w", "conv_b", "w_rgate", "b_rgate",
            "w_igate", "b_igate", "lru_lambda", "attn_out_norm", "lru_out_norm", "w_out", "w_ffn_gate", "w_ffn_up", "w_ffn_down")


def _pack(arrays):
    flat = []
    for a in arrays:
        f = a.reshape(-1)
        flat.append(jnp.pad(f, (0, (-f.shape[0]) % 1024)))
    return jnp.concatenate(flat).reshape(-1, 128)


def _unpack(packed, shapes):
    out, pos = [], 0
    flat = packed.reshape(-1)
    for s in shapes:
        size = math.prod(s)
        out.append(flat[pos:pos + size].reshape(s))
        pos += size + (-size) % 1024
    return out


def kernel(x, pre_mix_norm, post_mix_norm, pre_ffn_norm, post_ffn_norm, w_in, conv_w, conv_b, w_rgate, b_rgate, w_igate, b_igate, lru_lambda, attn_out_norm, lru_out_norm, w_out, w_ffn_gate, w_ffn_up, w_ffn_down, loss_target, m_pre_mix_norm, m_post_mix_norm, m_pre_ffn_norm, m_post_ffn_norm, m_w_in, m_conv_w, m_conv_b, m_w_rgate, m_b_rgate, m_w_igate, m_b_igate, m_lru_lambda, m_attn_out_norm, m_lru_out_norm, m_w_out, m_w_ffn_gate, m_w_ffn_up, m_w_ffn_down, v_pre_mix_norm, v_post_mix_norm, v_pre_ffn_norm, v_post_ffn_norm, v_w_in, v_conv_w, v_conv_b, v_w_rgate, v_b_rgate, v_w_igate, v_b_igate, v_lru_lambda, v_attn_out_norm, v_lru_out_norm, v_w_out, v_w_ffn_gate, v_w_ffn_up, v_w_ffn_down):
    given = dict(locals())
    w = {n: given[n][0] for n in _WEIGHTS}
    m = {n: given["m_" + n][0] for n in _WEIGHTS}
    v = {n: given["v_" + n][0] for n in _WEIGHTS}
    xs, target = x[0], loss_target[0]
    d = xs.shape[1]
    chip = (2 * lax.axis_index("x") + lax.axis_index("y")).astype(jnp.int32)
    place = jnp.stack([chip, lax.axis_index("c").astype(jnp.int32)])

    slots = {n: _into_slot(w[n], place[0:1], _MXU, "slot_" + n) for n in _BIG}
    slots["conv_w"] = _into_slot(w["conv_w"], place[0:1], F32, "slot_conv_w")
    ex = _Exchange(slots, place)
    row = lambda a: a.reshape(1, -1)
    norms = tuple(row(w[n]) for n in ("pre_mix_norm", "post_mix_norm", "pre_ffn_norm", "post_ffn_norm"))

    loss_cols, grad_x, small = _local_step(
        xs, target, norms, ex, row(w["conv_b"]), w["w_rgate"], row(w["b_rgate"]),
        w["w_igate"], row(w["b_igate"]), row(w["lru_lambda"]), row(w["attn_out_norm"]), row(w["lru_out_norm"]))


    ex.alone("grads_w_in_share")
    reduced = {n: ex.full[n] for n in _BIG}
    early = _unpack(ex.small_sum("early"), [small[n].shape for n in _SMALL_EARLY])
    late = _unpack(ex.small_sum("late"), [small["pre_mix_norm"].shape, (1, 1)])
    loss = late[1][0, 0]
    for n, g in zip(_SMALL_EARLY + ("pre_mix_norm",), early + late[:1]):
        reduced[n] = g.reshape(w[n].shape) if n != "conv_w" else lax.dynamic_slice_in_dim(g, chip * w[n].shape[1], w[n].shape[1], axis=1)

    delta, new_m, new_v = {}, {}, {}
    for n in _BIG:
        delta[n], new_m[n], new_v[n], reduced[n] = _adamw(w[n], reduced[n], m[n], v[n], "adamw_" + n, regive=True)
    shapes = [w[n].shape for n in _SMALL]
    packed = _adamw(*[_pack([src[n] for n in _SMALL]) for src in (w, reduced, m, v)], "adamw_small")
    for out, p in zip((delta, new_m, new_v), packed):
        out.update(zip(_SMALL, _unpack(p, shapes)))

    lead = lambda a: a[None]
    return (loss, lead(grad_x), *[lead(reduced[n]) for n in _WEIGHTS], *[lead(delta[n]) for n in _WEIGHTS],
            *[lead(new_m[n]) for n in _WEIGHTS], *[lead(new_v[n]) for n in _WEIGHTS])
```

```python
import functools
import math

import jax
import jax.numpy as jnp
from jax import lax
from jax.experimental import pallas as pl
from jax.experimental.pallas import tpu as pltpu

F32 = jnp.float32
BF16 = jnp.bfloat16
_MXU = BF16
S = jax.ShapeDtypeStruct

RMS_EPS = 1e-6
HEAD_DIM = 128
CONV_WIDTH = 4
LRU_C = 8.0
ADAM_LR, ADAM_B1, ADAM_B2, ADAM_EPS, ADAM_WD, ADAM_STEP = 0.001, 0.9, 0.999, 1e-08, 0.01, 10
EXP_CUT = -105.0
VMEM_LIMIT = 60 * 1024 * 1024
ROW_TILE = 256
SEQ_TILE = 256
ATTN_BLOCK = 256
ATTN_HEADS = 2
MM_ROWS = 512
DW_TOKENS = 2048
DW_ROWS = 512
MESH = pl.DeviceIdType.MESH


def _cp(*sem):
    return pltpu.CompilerParams(dimension_semantics=sem, vmem_limit_bytes=VMEM_LIMIT)


def _dot(a, b):
    return jnp.dot(a, b, preferred_element_type=F32)


def _dot_nt(a, b):
    return lax.dot_general(a, b, (((1,), (1,)), ((), ())), preferred_element_type=F32)


def _dot_tn(a, b):
    return lax.dot_general(a, b, (((0,), (0,)), ((), ())), preferred_element_type=F32)


def _rstd(v):
    return lax.rsqrt(jnp.mean(v * v, axis=-1, keepdims=True) + RMS_EPS)


def _rms_bwd(dn, vh, r, gain):
    dvh = dn * gain
    dv = r * (dvh - vh * jnp.mean(dvh * vh, axis=-1, keepdims=True))
    return dv, jnp.sum(dn * vh, axis=0, keepdims=True)


def _log_sigmoid(z):
    return jnp.minimum(z, 0.0) - jnp.log(1.0 + jnp.exp(-jnp.abs(z)))


def _expm1(v):
    small = v * (1.0 + v * (0.5 + v * (1.0 / 6.0 + v * (1.0 / 24.0 + v * (1.0 / 120.0)))))
    return jnp.where(jnp.abs(v) < 0.04, small, jnp.exp(v) - 1.0)


_GELU_C = math.sqrt(2.0 / math.pi)


def _gelu(v):
    return 0.5 * v * (1.0 + jnp.tanh(_GELU_C * (v + 0.044715 * v * v * v)))


def _gelu_grad(v):
    th = jnp.tanh(_GELU_C * (v + 0.044715 * v * v * v))
    return 0.5 * (1.0 + th) + 0.5 * v * (1.0 - th * th) * _GELU_C * (1.0 + 3.0 * 0.044715 * v * v)


def _row_spec(tm, d):
    return pl.BlockSpec((tm, d), lambda i: (i, 0))


def _vec_spec(d):
    return pl.BlockSpec((1, d), lambda i: (0, 0))


_ANY = pl.BlockSpec(memory_space=pl.ANY)


def _place():
    x, y, c = lax.axis_index("x"), lax.axis_index("y"), lax.axis_index("c")
    return x, y, c, [(1 - x, y), (x, 1 - y), (1 - x, 1 - y)]


def _remote(src, dst, send_sem, recv_sem, to):
    return pltpu.make_async_remote_copy(src_ref=src, dst_ref=dst, send_sem=send_sem, recv_sem=recv_sem,
                                        device_id=to, device_id_type=MESH)


class _Carrier:
    def __init__(self):
        self.inputs, self.out_shapes, self.aliases, self.ops, self.n_sems, self.results = [], [], {}, [], 0, None

    def inplace(self, arr):
        self.aliases[len(self.inputs)] = len(self.out_shapes)
        self.inputs.append(arr)
        self.out_shapes.append(S(arr.shape, arr.dtype))
        return len(self.out_shapes) - 1

    def read(self, arr):
        self.inputs.append(arr)
        return len(self.inputs) - 1

    def fresh(self, shape, dtype):
        self.out_shapes.append(S(shape, dtype))
        return len(self.out_shapes) - 1

    def _add(self, n_sems, copies):
        base = self.n_sems
        self.n_sems += n_sems

        def start(ins, outs, send, recv):
            for k, (src, dst, _, to) in enumerate(copies(ins, outs)):
                _remote(src, dst, send.at[base + k], recv.at[base + k], to).start()

        def finish(ins, outs, send, recv):
            for k, (src, _, land, to) in enumerate(copies(ins, outs)):
                _remote(src, land, send.at[base + k], recv.at[base + k], to).wait()

        self.ops.append((start, finish))

    def gather_ici(self, o, rows=None, split=True):
        half = self.out_shapes[o].shape[1] // 2
        lo, size = rows or (0, half)

        def copies(ins, outs):
            x, y, c, chips = _place()
            part = (lambda ref: ref.at[pl.ds(c * half + lo, size)]) if split else (lambda ref: ref)
            mine = part(outs[o].at[2 * x + y])
            return [(mine, mine, part(outs[o].at[2 * px + py]), (px, py, c)) for px, py in chips]

        self._add(3, copies)

    def gather_d2d(self, o, rows=None):
        half = self.out_shapes[o].shape[1] // 2
        lo, size = rows or (0, half)

        def copies(ins, outs):
            x, y, c, chips = _place()
            at = lambda k, cc: outs[o].at[k].at[pl.ds(cc * half + lo, size)]
            return [(at(2 * px + py, c), at(2 * px + py, c), at(2 * px + py, 1 - c), (x, y, 1 - c)) for px, py in chips]

        self._add(3, copies)

    def swap(self, i, o):
        half = self.inputs[i].shape[1] // 2

        def copies(ins, outs):
            x, y, c, _ = _place()
            return [(ins[i].at[:, pl.ds((1 - c) * half, half)], outs[o], outs[o], (x, y, 1 - c))]

        self._add(1, copies)

    def scatter(self, i, o, rows=None):
        lo, size = rows or (0, self.inputs[i].shape[1])

        def copies(ins, outs):
            x, y, c, chips = _place()
            cut = lambda ref: ref.at[pl.ds(lo, size)]
            return [(cut(ins[i].at[2 * px + py]), cut(outs[o].at[2 * x + y]), cut(outs[o].at[2 * px + py]), (px, py, c)) for px, py in chips]

        self._add(3, copies)

    def share(self, o):
        r = self.out_shapes[o].shape[0] // 2

        def copies(ins, outs):
            x, y, c, _ = _place()
            mine = outs[o].at[pl.ds(c * r, r)]
            return [(mine, mine, outs[o].at[pl.ds((1 - c) * r, r)], (x, y, 1 - c))]

        self._add(1, copies)

    def spread(self, i, o):
        def copies(ins, outs):
            x, y, c, _ = _place()
            me = 4 * x + 2 * y + c
            out = []
            for d in range(1, 8):
                to, frm = (me + d) % 8, (me + 8 - d) % 8
                out.append((ins[i], outs[o].at[me], outs[o].at[frm], (to // 4, (to // 2) % 2, to % 2)))
            return out

        self._add(7, copies)

    def _pallas(self, body, n_in, n_out, scratch, **kw):
        k_in, k_out = len(self.inputs), len(self.out_shapes)
        grid = kw.get("grid", ())

        def wrapped(*refs):
            ins, cins = refs[:n_in], refs[n_in:n_in + k_in]
            outs = refs[n_in + k_in:n_in + k_in + n_out]
            couts = refs[n_in + k_in + n_out:n_in + k_in + n_out + k_out]
            own = refs[n_in + k_in + n_out + k_out:]
            send, recv = own[len(scratch):]
            ids = [pl.program_id(a) for a in range(len(grid))]
            first = functools.reduce(jnp.logical_and, [a == 0 for a in ids], True)
            last = functools.reduce(jnp.logical_and, [a == g - 1 for a, g in zip(ids, grid)], True)

            def go(stage):
                for op in self.ops:
                    op[stage](cins, couts, send, recv)

            if grid:
                pl.when(first)(lambda: go(0))
                body(*ins, *outs, *own[:len(scratch)])
                pl.when(last)(lambda: go(1))
            else:
                go(0)
                go(1)

        sem = pltpu.SemaphoreType.DMA((self.n_sems,))
        return pl.pallas_call(
            wrapped, in_specs=list(kw.get("in_specs", [])) + [_ANY] * k_in, out_specs=list(kw.get("out_specs", [])) + [_ANY] * k_out,
            out_shape=list(kw.get("out_shape", [])) + self.out_shapes, scratch_shapes=list(scratch) + [sem, sem],
            input_output_aliases={**kw.get("aliases", {}), **{n_in + i: n_out + o for i, o in self.aliases.items()}}, name=kw["name"],
            **({"grid": grid, "compiler_params": _cp(*["arbitrary"] * len(grid))} if grid else {}))

    def run(self, body, kw, *args):
        single = not isinstance(kw["out_shape"], (list, tuple))
        out_shape = [kw["out_shape"]] if single else list(kw["out_shape"])
        out_specs = [kw["out_specs"]] if single else list(kw["out_specs"])
        res = self._pallas(body, len(args), len(out_shape), kw.get("scratch_shapes", []), grid=kw["grid"], in_specs=kw["in_specs"],
                           out_specs=out_specs, out_shape=out_shape, name=kw["name"],
                           aliases=kw.get("input_output_aliases", {}))(*args, *self.inputs)
        self.results = list(res[len(out_shape):])
        return res[0] if single else list(res[:len(out_shape)])

    def run_alone(self, name):
        self.results = list(self._pallas(None, 0, 0, [], name=name)(*self.inputs))


def _call(comm, body, **kw):
    if comm is None:
        return pl.pallas_call(body, **kw)
    return functools.partial(comm.run, body, kw)


def _in_proj_streamed(x, gain, car, o_w, place, *, bm, name):
    m, k = x.shape
    n = car.out_shapes[o_w].shape[2]
    ni, half = m // bm, k // 2
    k_in, k_out = len(car.inputs), len(car.out_shapes)
    order = lambda p: ((p & 1) << 1) | (p >> 1)

    def body(place_ref, x_ref, g_ref, *refs):
        cins, (hn_ref, o_ref, ob_ref), couts = refs[:k_in], refs[k_in:k_in + 3], refs[k_in + 3:k_in + 3 + k_out]
        wbuf, local, ici_send, ici_recv, d2d_send, d2d_recv, send, recv = refs[k_in + 3 + k_out:]
        p, i = pl.program_id(0), pl.program_id(1)
        x, y, c, chips = _place()
        me = 2 * x + y
        rows = lambda chunk, cc: couts[o_w].at[chunk].at[pl.ds(cc * half, half)]

        @pl.when(jnp.logical_and(p == 0, i == 0))
        def _():
            for j, (px, py) in enumerate(chips):
                _remote(rows(me, c), rows(me, c), ici_send.at[j], ici_recv.at[j], (px, py, c)).start()
            for op in car.ops:
                op[0](cins, couts, send, recv)

        for j, (px, py) in enumerate(chips):
            @pl.when(jnp.logical_and(p == j + 1, i == 0))
            def _(j=j, px=px, py=py):
                landed, other = rows(2 * px + py, c), rows(2 * px + py, 1 - c)
                _remote(landed, landed, ici_send.at[j], ici_recv.at[j], (px, py, c)).wait_recv()
                _remote(landed, landed, d2d_send.at[j], d2d_recv.at[j], (x, y, 1 - c)).start()
                _remote(other, other, d2d_send.at[j], d2d_recv.at[j], (x, y, 1 - c)).wait_recv()

        @pl.when(i == 0)
        def _():
            cp = pltpu.make_async_copy(couts[o_w].at[me ^ order(p)], wbuf, local.at[0])
            cp.start()
            cp.wait()

        xv = x_ref[...]
        hn = ((xv * _rstd(xv)) * g_ref[...]).astype(_MXU)
        hn_ref[...] = hn
        res = _dot(hn, wbuf[...])
        o_ref[...] = res
        ob_ref[...] = res.astype(ob_ref.dtype)

        @pl.when(jnp.logical_and(p == 3, i == ni - 1))
        def _():
            for j, (px, py) in enumerate(chips):
                _remote(rows(me, c), rows(me, c), ici_send.at[j], ici_recv.at[j], (px, py, c)).wait_send()
                _remote(rows(me, c), rows(me, c), d2d_send.at[j], d2d_recv.at[j], (x, y, 1 - c)).wait_send()
            for op in car.ops:
                op[1](cins, couts, send, recv)

    ospec = pl.BlockSpec((bm, n), lambda p, i, place_ref: (i, place_ref[0] ^ order(p)))
    rows = pl.BlockSpec((bm, k), lambda p, i, place_ref: (i, 0))
    three, sems = pltpu.SemaphoreType.DMA((3,)), pltpu.SemaphoreType.DMA((max(car.n_sems, 1),))
    res = pl.pallas_call(
        body,
        grid_spec=pltpu.PrefetchScalarGridSpec(
            num_scalar_prefetch=1, grid=(4, ni),
            in_specs=[rows, pl.BlockSpec((1, k), lambda p, i, place_ref: (0, 0))] + [_ANY] * k_in,
            out_specs=[pl.BlockSpec((bm, k), lambda p, i, place_ref: (p * ni + i, 0)), ospec, ospec] + [_ANY] * k_out,
            scratch_shapes=[pltpu.VMEM((k, n), _MXU), pltpu.SemaphoreType.DMA((1,)), three, three, three, three, sems, sems]),
        out_shape=[S((4 * m, k), _MXU), S((m, 4 * n), F32), S((m, 4 * n), _MXU)] + car.out_shapes,
        input_output_aliases={3 + a: 3 + o for a, o in car.aliases.items()},
        compiler_params=_cp("arbitrary", "arbitrary"), name=name)(place, x, gain, *car.inputs)
    car.results = list(res[3:])
    return res[0], res[1], res[2]


def _mm_nn(a, b3, *, bm, bn, name, also=None, comm=None):
    m, k = a.shape
    c, _, n = b3.shape
    ni, nj = m // bm, n // bn

    def body(a_ref, b_ref, *o_refs):
        res = _dot(a_ref[...], b_ref[...])
        for o_ref in o_refs:
            o_ref[...] = res.astype(o_ref.dtype)

    ospec = pl.BlockSpec((bm, bn), lambda cc, j, i: (i, cc * nj + j))
    dtypes = [F32] + ([] if also is None else [also])
    out = _call(
        comm, body, grid=(c, nj, ni),
        in_specs=[pl.BlockSpec((bm, k), lambda cc, j, i: (i, 0)), pl.BlockSpec((None, k, bn), lambda cc, j, i: (cc, 0, j))],
        out_specs=[ospec] * len(dtypes), out_shape=[S((m, c * n), dt) for dt in dtypes],
        compiler_params=_cp("parallel", "parallel", "parallel"), name=name)(a, b3)
    return out[0] if also is None else out


def _mm_nt(a, b3, *, bm, bo, out_dtype, name, comm=None):
    m = a.shape[0]
    c, ko, n = b3.shape
    ni, nj = m // bm, ko // bo

    def body(a_ref, b_ref, o_ref):
        acc = _dot_nt(a_ref[:, 0:n], b_ref[0])
        for cc in range(1, c):
            acc = acc + _dot_nt(a_ref[:, cc * n:(cc + 1) * n], b_ref[cc])
        o_ref[...] = acc.astype(o_ref.dtype)

    return _call(
        comm, body, grid=(nj, ni),
        in_specs=[pl.BlockSpec((bm, c * n), lambda j, i: (i, 0)),
                  pl.BlockSpec((c, bo, n), lambda j, i: (0, j, 0))],
        out_specs=pl.BlockSpec((bm, bo), lambda j, i: (i, j)),
        out_shape=S((m, ko), out_dtype),
        compiler_params=_cp("parallel", "parallel"), name=name)(a, b3)


def _mm_tn(a, b, c, *, bm, bk, out_dtype, name, comm=None):
    m, k = b.shape[0], a.shape[1]
    n = b.shape[1] // c
    nm, nk = m // bm, k // bk

    def body(a_ref, b_ref, o_ref, acc):
        mm = pl.program_id(2)

        @pl.when(mm == 0)
        def _():
            acc[...] = jnp.zeros_like(acc)

        acc[...] += _dot_tn(a_ref[...], b_ref[...])

        @pl.when(mm == nm - 1)
        def _():
            o_ref[...] = acc[...].astype(o_ref.dtype)

    return _call(
        comm, body, grid=(c, nk, nm),
        in_specs=[pl.BlockSpec((bm, bk), lambda cc, j, mm: (mm, j)),
                  pl.BlockSpec((bm, n), lambda cc, j, mm: (mm, cc))],
        out_specs=pl.BlockSpec((None, bk, n), lambda cc, j, mm: (cc, j, 0)),
        out_shape=S((c, k, n), out_dtype),
        scratch_shapes=[pltpu.VMEM((bk, n), F32)],
        compiler_params=_cp("parallel", "parallel", "arbitrary"), name=name)(a, b)


def _swiglu_fwd(hn, wg3, wu3, *, bm, name, comm=None):
    m, k = hn.shape
    c, _, n = wg3.shape

    def body(a_ref, g_ref, u_ref, dgate_ref, dup_ref, act_ref):
        a = a_ref[...]
        gate = _dot(a, g_ref[...])
        up = _dot(a, u_ref[...])
        sg = jax.nn.sigmoid(gate)
        silu = gate * sg
        dgate_ref[...] = (up * (sg * (1.0 + gate * (1.0 - sg)))).astype(dgate_ref.dtype)
        dup_ref[...] = silu.astype(dup_ref.dtype)
        act_ref[...] = (silu * up).astype(act_ref.dtype)

    wspec = pl.BlockSpec((None, k, n), lambda cc, i: (cc, 0, 0))
    ospec = pl.BlockSpec((bm, n), lambda cc, i: (i, cc))
    return _call(
        comm, body, grid=(c, m // bm),
        in_specs=[pl.BlockSpec((bm, k), lambda cc, i: (i, 0)), wspec, wspec],
        out_specs=[ospec, ospec, ospec],
        out_shape=[S((m, c * n), _MXU), S((m, c * n), _MXU), S((m, c * n), _MXU)],
        compiler_params=_cp("parallel", "parallel"), name=name)(hn, wg3, wu3)


def _swiglu_bwd(df, wd, act_dgate, act_dup, *, bm, bo, name):
    m, k = df.shape
    ko = wd.shape[0]

    def body(a_ref, b_ref, g_ref, u_ref, dg_ref, du_ref):
        dact = _dot_nt(a_ref[...], b_ref[...])
        dg_ref[...] = (dact * g_ref[...].astype(F32)).astype(dg_ref.dtype)
        du_ref[...] = (dact * u_ref[...].astype(F32)).astype(du_ref.dtype)

    ospec = pl.BlockSpec((bm, bo), lambda j, i: (i, j))
    return pl.pallas_call(
        body, grid=(ko // bo, m // bm),
        in_specs=[pl.BlockSpec((bm, k), lambda j, i: (i, 0)), pl.BlockSpec((bo, k), lambda j, i: (j, 0)), ospec, ospec],
        out_specs=[ospec, ospec],
        out_shape=[S((m, ko), _MXU), S((m, ko), _MXU)],
        compiler_params=_cp("parallel", "parallel"), name=name)(df, wd, act_dgate, act_dup)


def _rms_fwd(x, gain, name):
    t, d = x.shape
    tm = min(t, ROW_TILE)

    def body(x_ref, g_ref, o_ref):
        xv = x_ref[...]
        o_ref[...] = ((xv * _rstd(xv)) * g_ref[...]).astype(o_ref.dtype)

    return pl.pallas_call(body, grid=(t // tm,), in_specs=[_row_spec(tm, d), _vec_spec(d)], out_specs=_row_spec(tm, d),
                          out_shape=S((t, d), _MXU), compiler_params=_cp("parallel"), name=name)(x, gain)


def _outnorm_fwd(o, yl, ga, gl, name, comm=None):
    t, w = o.shape
    tm = min(t, ROW_TILE)

    def body(o_ref, l_ref, ga_ref, gl_ref, y_ref):
        ov, lv = o_ref[...], l_ref[...]
        y_ref[:, :w] = ((ov * _rstd(ov)) * ga_ref[...]).astype(y_ref.dtype)
        y_ref[:, w:] = ((lv * _rstd(lv)) * gl_ref[...]).astype(y_ref.dtype)

    return _call(comm, body, grid=(t // tm,), in_specs=[_row_spec(tm, w), _row_spec(tm, w), _vec_spec(w), _vec_spec(w)],
                 out_specs=_row_spec(tm, 2 * w), out_shape=S((t, 2 * w), _MXU),
                 compiler_params=_cp("parallel"), name=name)(o, yl, ga, gl)


def _mid_fwd(x, mix, g_post, g_pre, name, comm=None):
    t, d = x.shape
    tm = min(t, ROW_TILE)

    def body(x_ref, m_ref, gp_ref, gn_ref, x2_ref, hn_ref):
        mv = m_ref[...]
        x2 = x_ref[...] + (mv * _rstd(mv)) * gp_ref[...]
        x2_ref[...] = x2
        hn_ref[...] = ((x2 * _rstd(x2)) * gn_ref[...]).astype(hn_ref.dtype)

    return _call(comm, body, grid=(t // tm,), in_specs=[_row_spec(tm, d), _row_spec(tm, d), _vec_spec(d), _vec_spec(d)],
                          out_specs=[_row_spec(tm, d), _row_spec(tm, d)], out_shape=[S((t, d), F32), S((t, d), _MXU)],
                          compiler_params=_cp("parallel"), name=name)(x, mix, g_post, g_pre)


def _final(f, x2, target, g_post, name):
    t, d = f.shape
    tm = min(t, ROW_TILE)

    def body(f_ref, x2_ref, t_ref, g_ref, loss_ref, dout_ref, df_ref, dg_ref):
        @pl.when(pl.program_id(0) == 0)
        def _():
            loss_ref[...] = jnp.zeros_like(loss_ref)
            dg_ref[...] = jnp.zeros_like(dg_ref)

        fv = f_ref[...]
        r = _rstd(fv)
        fh = fv * r
        err = (x2_ref[...] + fh * g_ref[...]) - t_ref[...]
        loss_ref[...] += jnp.sum(err * err, axis=0, keepdims=True)
        dout = err * (1.0 / d)
        dout_ref[...] = dout
        dfv, dg = _rms_bwd(dout, fh, r, g_ref[...])
        df_ref[...] = dfv.astype(df_ref.dtype)
        dg_ref[...] += dg

    return pl.pallas_call(
        body, grid=(t // tm,),
        in_specs=[_row_spec(tm, d), _row_spec(tm, d), _row_spec(tm, d), _vec_spec(d)],
        out_specs=[_vec_spec(d), _row_spec(tm, d), _row_spec(tm, d), _vec_spec(d)],
        out_shape=[S((1, d), F32), S((t, d), F32), S((t, d), _MXU), S((1, d), F32)],
        compiler_params=_cp("arbitrary"), name=name)(f, x2, target, g_post)


def _mid_bwd(dhn_a, dhn_b, dout, x2, mix, g_pre, g_post, name, comm=None):
    t, d = x2.shape
    tm = min(t, ROW_TILE)

    def body(da_ref, db_ref, do_ref, x2_ref, m_ref, gn_ref, gp_ref, dx2_ref, dm_ref, dgn_ref, dgp_ref):
        @pl.when(pl.program_id(0) == 0)
        def _():
            dgn_ref[...] = jnp.zeros_like(dgn_ref)
            dgp_ref[...] = jnp.zeros_like(dgp_ref)

        x2 = x2_ref[...]
        r = _rstd(x2)
        dxa, dgn = _rms_bwd(da_ref[...] + db_ref[...], x2 * r, r, gn_ref[...])
        dx2 = do_ref[...] + dxa
        dx2_ref[...] = dx2
        dgn_ref[...] += dgn
        mv = m_ref[...]
        rm = _rstd(mv)
        dmv, dgp = _rms_bwd(dx2, mv * rm, rm, gp_ref[...])
        dm_ref[...] = dmv.astype(dm_ref.dtype)
        dgp_ref[...] += dgp

    rs, vs = _row_spec(tm, d), _vec_spec(d)
    return _call(
        comm, body, grid=(t // tm,), in_specs=[rs, rs, rs, rs, rs, vs, vs], out_specs=[rs, rs, vs, vs],
        out_shape=[S((t, d), F32), S((t, d), _MXU), S((1, d), F32), S((1, d), F32)],
        compiler_params=_cp("arbitrary"), name=name)(dhn_a, dhn_b, dout, x2, mix, g_pre, g_post)


def _first_bwd(dhn, dx2, x, gain, name, comm=None):
    t, d = x.shape
    tm = min(t, ROW_TILE)

    def body(dh_ref, dx2_ref, x_ref, g_ref, dx_ref, dg_ref):
        @pl.when(pl.program_id(0) == 0)
        def _():
            dg_ref[...] = jnp.zeros_like(dg_ref)

        xv = x_ref[...]
        r = _rstd(xv)
        dxa, dg = _rms_bwd(dh_ref[...], xv * r, r, g_ref[...])
        dx_ref[...] = dx2_ref[...] + dxa
        dg_ref[...] += dg

    rs, vs = _row_spec(tm, d), _vec_spec(d)
    return _call(comm, body, grid=(t // tm,), in_specs=[rs, rs, rs, vs], out_specs=[rs, vs],
                          out_shape=[S((t, d), F32), S((1, d), F32)], compiler_params=_cp("arbitrary"), name=name)(dhn, dx2, x, gain)


def _outnorm_bwd(dy, o, yl, ga, gl, name, comm=None):
    t, w = o.shape
    tm = min(t, ROW_TILE)

    def body(dy_ref, o_ref, l_ref, ga_ref, gl_ref, do_ref, dl_ref, dga_ref, dgl_ref):
        @pl.when(pl.program_id(0) == 0)
        def _():
            dga_ref[...] = jnp.zeros_like(dga_ref)
            dgl_ref[...] = jnp.zeros_like(dgl_ref)

        ov, lv = o_ref[...], l_ref[...]
        ra, rl = _rstd(ov), _rstd(lv)
        dov, dga = _rms_bwd(dy_ref[:, :w], ov * ra, ra, ga_ref[...])
        dlv, dgl = _rms_bwd(dy_ref[:, w:], lv * rl, rl, gl_ref[...])
        do_ref[...] = dov.astype(do_ref.dtype)
        dl_ref[...] = dlv
        dga_ref[...] += dga
        dgl_ref[...] += dgl

    rs, vs = _row_spec(tm, w), _vec_spec(w)
    return _call(comm, body, grid=(t // tm,), in_specs=[_row_spec(tm, 2 * w), rs, rs, vs, vs], out_specs=[rs, rs, vs, vs],
                          out_shape=[S((t, w), _MXU), S((t, w), F32), S((1, w), F32), S((1, w), F32)],
                          compiler_params=_cp("arbitrary"), name=name)(dy, o, yl, ga, gl)


def _tri_sum(v, tri):
    return _dot(v.astype(_MXU), tri)


def _attn_tile(qb, kb, row, col, shift, scale):
    z = _dot_nt(qb, kb) * scale
    mask = (col + shift) < row
    lb = _log_sigmoid(z)
    lm = jnp.where(mask, lb - z, 0.0)
    return mask, lb, lm


def _attn_fwd(proj, n_heads, name, comm=None):
    t = proj.shape[0]
    bq = min(t, ATTN_BLOCK)
    nq = t // bq
    scale = 1.0 / math.sqrt(HEAD_DIM)

    heads = [slice(a * HEAD_DIM, (a + 1) * HEAD_DIM) for a in range(ATTN_HEADS)]

    def body(q_ref, k_ref, v_ref, o_ref):
        row = lax.broadcasted_iota(jnp.int32, (bq, bq), 0)
        col = lax.broadcasted_iota(jnp.int32, (bq, bq), 1)
        tri = (row > col).astype(_MXU)

        def per_q(qi, _):
            q0 = pl.multiple_of(qi * bq, bq)
            qbs = [q_ref[pl.ds(q0, bq), hd] for hd in heads]

            def cond(st):
                return jnp.logical_and(st[0] >= 0, st[1])

            def step(st):
                kj, _, carries, accs = st
                k0 = pl.multiple_of(kj * bq, bq)
                alive, new_carries, new_accs = None, [], []
                for hd, qb, carry, acc in zip(heads, qbs, carries, accs):
                    mask, lb, lm = _attn_tile(qb, k_ref[pl.ds(k0, bq), hd], row, col, (kj - qi) * bq, scale)
                    w = jnp.where(mask, jnp.exp(lb + _tri_sum(lm, tri) + carry), 0.0)
                    new_accs.append(acc + _dot(w.astype(_MXU), v_ref[pl.ds(k0, bq), hd]))
                    carry = carry + jnp.sum(lm, axis=1, keepdims=True)
                    new_carries.append(carry)
                    live = jnp.max(carry) > EXP_CUT
                    alive = live if alive is None else jnp.logical_or(alive, live)
                return kj - 1, alive, tuple(new_carries), tuple(new_accs)

            st = lax.while_loop(cond, step, (qi, jnp.bool_(True), (jnp.zeros((bq, 1), F32),) * ATTN_HEADS,
                                             (jnp.zeros((bq, HEAD_DIM), F32),) * ATTN_HEADS))
            for hd, acc in zip(heads, st[3]):
                o_ref[pl.ds(q0, bq), hd] = acc
            return 0

        lax.fori_loop(0, nq, per_q, 0)

    groups = n_heads // ATTN_HEADS
    hs = lambda off: pl.BlockSpec((t, ATTN_HEADS * HEAD_DIM), lambda h: (0, off + h))
    return _call(
        comm, body, grid=(groups,), in_specs=[hs(0), hs(groups), hs(2 * groups)], out_specs=hs(0),
        out_shape=S((t, n_heads * HEAD_DIM), F32), compiler_params=_cp("parallel"), name=name)(proj, proj, proj)


def _emit(blocks, out_ref, starts, sems):
    copies = [pltpu.make_async_copy(b, out_ref.at[:, pl.ds(c0, b.shape[1])], sems.at[k]) for k, (b, c0) in enumerate(zip(blocks, starts))]
    for cp in copies:
        cp.start()
    for cp in copies:
        cp.wait()


def _attn_bwd(proj, do, dproj, n_heads, name, comm=None):
    t = proj.shape[0]
    bq = min(t, ATTN_BLOCK)
    nq = t // bq
    scale = 1.0 / math.sqrt(HEAD_DIM)
    groups = n_heads // ATTN_HEADS
    wide = ATTN_HEADS * HEAD_DIM

    heads = [slice(a * HEAD_DIM, (a + 1) * HEAD_DIM) for a in range(ATTN_HEADS)]

    def body(q_ref, k_ref, v_ref, do_ref, _, dproj_ref, dka_ref, dva_ref, g_ref, b_ref, dq_ref, dk_ref, dv_ref, out_sems):
        group = pl.program_id(0)
        dka_ref[...] = jnp.zeros_like(dka_ref)
        dva_ref[...] = jnp.zeros_like(dva_ref)
        row = lax.broadcasted_iota(jnp.int32, (bq, bq), 0)
        col = lax.broadcasted_iota(jnp.int32, (bq, bq), 1)
        tri = (row > col).astype(_MXU)
        tri_lt = (row < col).astype(_MXU)

        def per_q(qi, _):
            q0 = pl.multiple_of(qi * bq, bq)
            qbs = [q_ref[pl.ds(q0, bq), hd] for hd in heads]
            dobs = [do_ref[pl.ds(q0, bq), hd] for hd in heads]

            def cond(st):
                return jnp.logical_and(st[0] >= 0, st[1])

            def step(st):
                kj, _, carries = st
                k0 = pl.multiple_of(kj * bq, bq)
                alive, new_carries = None, []
                for a, (hd, qb, dob, carry) in enumerate(zip(heads, qbs, dobs, carries)):
                    mask, lb, lm = _attn_tile(qb, k_ref[pl.ds(k0, bq), hd], row, col, (kj - qi) * bq, scale)
                    w = jnp.where(mask, jnp.exp(lb + _tri_sum(lm, tri) + carry), 0.0)
                    g_ref[a, pl.ds(k0, bq), :] = w * _dot_nt(dob, v_ref[pl.ds(k0, bq), hd])
                    b_ref[a, pl.ds(k0, bq), :] = jnp.where(mask, jnp.exp(lb), 0.0)
                    dva_ref[pl.ds(k0, bq), hd] += _dot_tn(w.astype(_MXU), dob)
                    carry = carry + jnp.sum(lm, axis=1, keepdims=True)
                    new_carries.append(carry)
                    live = jnp.max(carry) > EXP_CUT
                    alive = live if alive is None else jnp.logical_or(alive, live)
                return kj - 1, alive, tuple(new_carries)

            st = lax.while_loop(cond, step, (qi, jnp.bool_(True), (jnp.zeros((bq, 1), F32),) * ATTN_HEADS))

            def back(kj, st2):
                k0 = pl.multiple_of(kj * bq, bq)
                out = []
                for a, (hd, qb, (before, dq)) in enumerate(zip(heads, qbs, st2)):
                    g = g_ref[a, pl.ds(k0, bq), :]
                    beta = b_ref[a, pl.ds(k0, bq), :]
                    dz = ((g * (1.0 - beta) - (before + _tri_sum(g, tri_lt)) * beta) * scale).astype(_MXU)
                    dka_ref[pl.ds(k0, bq), hd] += _dot_tn(dz, qb)
                    out.append((before + jnp.sum(g, axis=1, keepdims=True), dq + _dot(dz, k_ref[pl.ds(k0, bq), hd])))
                return tuple(out)

            st2 = lax.fori_loop(st[0] + 1, qi + 1, back, ((jnp.zeros((bq, 1), F32), jnp.zeros((bq, HEAD_DIM), F32)),) * ATTN_HEADS)
            for hd, (_, dq) in zip(heads, st2):
                dq_ref[pl.ds(q0, bq), hd] = dq.astype(dq_ref.dtype)
            return 0

        lax.fori_loop(0, nq, per_q, 0)
        dk_ref[...] = dka_ref[...].astype(dk_ref.dtype)
        dv_ref[...] = dva_ref[...].astype(dv_ref.dtype)
        _emit([dq_ref, dk_ref, dv_ref], dproj_ref, [(a * groups + group) * wide for a in range(3)], out_sems)

    hs = lambda off: pl.BlockSpec((t, wide), lambda h: (0, off + h))
    return _call(
        comm, body, grid=(groups,), in_specs=[hs(0), hs(groups), hs(2 * groups), hs(0), _ANY], out_specs=_ANY,
        out_shape=S(dproj.shape, dproj.dtype), input_output_aliases={4: 0},
        scratch_shapes=[pltpu.VMEM((t, wide), F32), pltpu.VMEM((t, wide), F32),
                        pltpu.VMEM((ATTN_HEADS, t, bq), F32), pltpu.VMEM((ATTN_HEADS, t, bq), F32)]
        + [pltpu.VMEM((t, wide), dproj.dtype)] * 3 + [pltpu.SemaphoreType.DMA((3,))],
        compiler_params=_cp("parallel"), name=name)(proj, proj, proj, do, dproj)


def _shift_down(cur, prev8, k):
    if k == 0:
        return cur
    row8 = lax.broadcasted_iota(jnp.int32, prev8.shape, 0)
    rc = pltpu.roll(cur, k, 0)
    top = jnp.where(row8 < k, pltpu.roll(prev8, k, 0), rc[0:8, :])
    return jnp.concatenate([top, rc[8:, :]], axis=0)


def _shift_up(cur, next8, k):
    if k == 0:
        return cur
    n = cur.shape[0]
    row8 = lax.broadcasted_iota(jnp.int32, next8.shape, 0)
    rc = pltpu.roll(cur, n - k, 0)
    bottom = jnp.where(row8 >= 8 - k, pltpu.roll(next8, 8 - k, 0), rc[n - 8:, :])
    return jnp.concatenate([rc[:n - 8, :], bottom], axis=0)


def _lru_conv(xl, prev8, cw, cb):
    xs = [_shift_down(xl, prev8, CONV_WIDTH - 1 - k) for k in range(CONV_WIDTH)]
    xc = xs[0] * cw[0:1, :]
    for k in range(1, CONV_WIDTH):
        xc = xc + xs[k] * cw[k:k + 1, :]
    return xs, xc + cb


def _lru_gates(xl, prev8, cw, cb, wr, br, wi, bi, ls):
    xs, xc = _lru_conv(xl, prev8, cw, cb)
    xcb = xc.astype(_MXU)
    r = jax.nn.sigmoid(_dot(xcb, wr) + br)
    i = jax.nn.sigmoid(_dot(xcb, wi) + bi)
    la = (LRU_C * r) * ls
    a = jnp.exp(la)
    mult = jnp.sqrt(-_expm1(2.0 * la))
    return xs, xc, r, i, a, mult


def _group_scan(a, b, reverse):
    n = a.shape[0]
    row = lax.broadcasted_iota(jnp.int32, a.shape, 0) % 8
    for d in (1, 2, 4):
        if reverse:
            m = row < 8 - d
            a_s, b_s = pltpu.roll(a, n - d, 0), pltpu.roll(b, n - d, 0)
        else:
            m = row >= d
            a_s, b_s = pltpu.roll(a, d, 0), pltpu.roll(b, d, 0)
        b = jnp.where(m, a * b_s + b, b)
        a = jnp.where(m, a * a_s, a)
    return a, b


def _lru_fwd(proj, col0, n_blocks, cw, cb, wr, br, wi, bi, lam, name, comm=None):
    t = proj.shape[0]
    tt = min(t, SEQ_TILE)
    nt = t // tt

    def body(xl_ref, gl_ref, cw_ref, cb_ref, wr_ref, br_ref, wi_ref, bi_ref, lam_ref, h_ref, y_ref, *kept):
        cwv, cbv, brv, biv = cw_ref[...], cb_ref[...], br_ref[...], bi_ref[...]
        wrv, wiv = wr_ref[...].astype(_MXU), wi_ref[...].astype(_MXU)
        ls = _log_sigmoid(lam_ref[...])

        def tile(ti, hin):
            t0 = pl.multiple_of(ti * tt, tt)
            p0 = pl.multiple_of(jnp.maximum(t0 - 8, 0), 8)
            prev8 = xl_ref[pl.ds(p0, 8), :] * (ti > 0).astype(F32)
            xl = xl_ref[pl.ds(t0, tt), :]
            _, xc, r, ig, a, mult = _lru_gates(xl, prev8, cwv, cbv, wrv, brv, wiv, biv, ls)
            for ref, val in zip(kept, (r, ig, a, mult)):
                ref[pl.ds(t0, tt), :] = val
            ga, gb = _group_scan(a, mult * (ig * xc), False)
            for g in range(tt // 8):
                hg = ga[8 * g:8 * g + 8, :] * hin + gb[8 * g:8 * g + 8, :]
                h_ref[pl.ds(t0 + 8 * g, 8), :] = hg
                hin = hg[7:8, :]
            y_ref[pl.ds(t0, tt), :] = h_ref[pl.ds(t0, tt), :] * _gelu(gl_ref[pl.ds(t0, tt), :])
            return hin

        lax.fori_loop(0, nt, tile, jnp.zeros((1, HEAD_DIM), F32))

    cs = lambda off: pl.BlockSpec((t, HEAD_DIM), lambda n: (0, off + n))
    vs = pl.BlockSpec((1, HEAD_DIM), lambda n: (0, n))
    ws = pl.BlockSpec((None, HEAD_DIM, HEAD_DIM), lambda n: (n, 0, 0))
    w = n_blocks * HEAD_DIM
    return _call(
        comm, body, grid=(n_blocks,),
        in_specs=[cs(col0), cs(col0 + n_blocks), pl.BlockSpec((CONV_WIDTH, HEAD_DIM), lambda n: (0, n)), vs, ws, vs, ws, vs, vs],
        out_specs=[cs(0)] * 6, out_shape=[S((t, w), F32)] * 6,
        compiler_params=_cp("parallel"), name=name)(proj, proj, cw, cb, wr, br, wi, bi, lam)


def _lru_bwd(proj, col0, n_blocks, h, kept, dyl, cw, cb, wr, wi, lam, name, comm=None):
    t = proj.shape[0]
    tt = min(t, SEQ_TILE)
    nt = t // tt

    def body(xl_ref, gl_ref, h_ref, r_ref, i_ref, a_ref, m_ref, dy_ref, cw_ref, cb_ref, wr_ref, wi_ref, lam_ref,
             dproj_ref, dcw_ref, dcb_ref, dwr_ref, dbr_ref, dwi_ref, dbi_ref, dlam_ref, g_ref, dxl_ref, dgl_ref, out_sems):
        block = pl.program_id(0)
        cwv, cbv = cw_ref[...], cb_ref[...]
        wrv, wiv = wr_ref[...].astype(_MXU), wi_ref[...].astype(_MXU)
        lamv = lam_ref[...]
        ls = _log_sigmoid(lamv)
        for ref in (dcw_ref, dcb_ref, dwr_ref, dbr_ref, dwi_ref, dbi_ref, dlam_ref):
            ref[...] = jnp.zeros_like(ref)

        def tile(s, carry):
            e_in, dxc_next8 = carry
            ti = nt - 1 - s
            t0 = pl.multiple_of(ti * tt, tt)
            p0 = pl.multiple_of(jnp.maximum(t0 - 8, 0), 8)
            first = (ti > 0).astype(F32)
            xl = xl_ref[pl.ds(t0, tt), :]
            xs, xc = _lru_conv(xl, xl_ref[pl.ds(p0, 8), :] * first, cwv, cbv)
            r, ig, a, mult = (ref[pl.ds(t0, tt), :] for ref in (r_ref, i_ref, a_ref, m_ref))
            hv = h_ref[pl.ds(t0, tt), :]
            h_before = _shift_down(hv, h_ref[pl.ds(p0, 8), :] * first, 1)
            glv = gl_ref[pl.ds(t0, tt), :]
            dyv = dy_ref[pl.ds(t0, tt), :]
            dgl_ref[pl.ds(t0, tt), :] = (dyv * hv * _gelu_grad(glv)).astype(dgl_ref.dtype)
            dh = dyv * _gelu(glv)
            row = lax.broadcasted_iota(jnp.int32, a.shape, 0)
            coef = jnp.where(row == tt - 1, 1.0, pltpu.roll(a, tt - 1, 0))
            ga, gb = _group_scan(coef, dh, True)
            gin = e_in
            for g in reversed(range(tt // 8)):
                gg = ga[8 * g:8 * g + 8, :] * gin + gb[8 * g:8 * g + 8, :]
                g_ref[8 * g:8 * g + 8, :] = gg
                gin = gg[0:1, :]
            gv = g_ref[...]
            e_out = a[0:1, :] * gv[0:1, :]
            ix = ig * xc
            dla = (gv * h_before) * a - (gv * ix) * (a * a / mult)
            dlam_ref[...] += jnp.sum(dla * (LRU_C * r), axis=0, keepdims=True)
            dpr = (dla * (LRU_C * ls)) * (r * (1.0 - r))
            dpi = (gv * mult * xc) * (ig * (1.0 - ig))
            dbr_ref[...] += jnp.sum(dpr, axis=0, keepdims=True)
            dbi_ref[...] += jnp.sum(dpi, axis=0, keepdims=True)
            xcb, dprb, dpib = xc.astype(_MXU), dpr.astype(_MXU), dpi.astype(_MXU)
            dwr_ref[...] += _dot_tn(xcb, dprb)
            dwi_ref[...] += _dot_tn(xcb, dpib)
            dxc = gv * mult * ig + _dot_nt(dprb, wrv) + _dot_nt(dpib, wiv)
            dcb_ref[...] += jnp.sum(dxc, axis=0, keepdims=True)
            dxl = None
            for k in range(CONV_WIDTH):
                dcw_ref[k:k + 1, :] += jnp.sum(dxc * xs[k], axis=0, keepdims=True)
                term = _shift_up(dxc, dxc_next8, CONV_WIDTH - 1 - k) * cwv[k:k + 1, :]
                dxl = term if dxl is None else dxl + term
            dxl_ref[pl.ds(t0, tt), :] = dxl.astype(dxl_ref.dtype)
            return e_out, dxc[0:8, :]

        lax.fori_loop(0, nt, tile, (jnp.zeros((1, HEAD_DIM), F32), jnp.zeros((8, HEAD_DIM), F32)))
        dlam_ref[...] = dlam_ref[...] * (1.0 - jax.nn.sigmoid(lamv))
        _emit([dxl_ref, dgl_ref], dproj_ref, [(col0 + block) * HEAD_DIM, (col0 + n_blocks + block) * HEAD_DIM], out_sems)

    cs = lambda off: pl.BlockSpec((t, HEAD_DIM), lambda n: (0, off + n))
    vs = pl.BlockSpec((1, HEAD_DIM), lambda n: (0, n))
    ws = pl.BlockSpec((None, HEAD_DIM, HEAD_DIM), lambda n: (n, 0, 0))
    cws = pl.BlockSpec((CONV_WIDTH, HEAD_DIM), lambda n: (0, n))
    w = n_blocks * HEAD_DIM
    vec = S((1, w), F32)
    mat = S((n_blocks, HEAD_DIM, HEAD_DIM), F32)
    return _call(
        comm, body, grid=(n_blocks,),
        in_specs=[cs(col0), cs(col0 + n_blocks)] + [cs(0)] * 6 + [cws, vs, ws, ws, vs],
        out_specs=[_ANY, cws, vs, ws, vs, ws, vs, vs],
        out_shape=[S(proj.shape, _MXU), S((CONV_WIDTH, w), F32), vec, mat, vec, mat, vec, vec],
        scratch_shapes=[pltpu.VMEM((tt, HEAD_DIM), F32), pltpu.VMEM((t, HEAD_DIM), _MXU), pltpu.VMEM((t, HEAD_DIM), _MXU),
                        pltpu.SemaphoreType.DMA((2,))],
        compiler_params=_cp("parallel"), name=name)(proj, proj, h, *kept, dyl, cw, cb, wr, wi, lam)


class _NoExchange:
    grad_dtype = F32

    def __init__(self, weights):
        self.weights, self.grads, self.packs = weights, {}, {}

    def weight(self, name):
        return self.weights[name]

    def in_proj(self, x, gain, bm):
        hn = _rms_fwd(x, gain, "rms1")
        return [hn, *_mm_nn(hn, self.weights["w_in"], bm=bm, bn=self.weights["w_in"].shape[2], name="in_proj", also=_MXU)]

    def conv_w(self):
        return self.weights["conv_w"]

    def carrier(self, call):
        return None

    def harvest(self, car):
        pass

    def alone(self, call):
        pass


def _local_step(x, target, norms, ex, cb, wr, br, wi, bi, lam, ga, gl):
    g_pre_mix, g_post_mix, g_pre_ffn, g_post_ffn = norms
    t, d = x.shape
    bm = min(t, MM_ROWS)
    bt = min(t, DW_TOKENS)

    def run(fn, name, *args, **kw):
        car = ex.carrier(name)
        out = fn(*args, name=name, comm=car, **kw)
        ex.harvest(car)
        return out

    hn1, proj, proj_mx = ex.in_proj(x, g_pre_mix, bm)
    win3, cw = ex.weight("w_in"), ex.conv_w()
    c = win3.shape[0]
    o = run(_attn_fwd, "attn_fwd", proj_mx, (proj.shape[1] - d) // 3 // HEAD_DIM)
    mix = 2 * o.shape[1]
    n_heads = n_blocks = o.shape[1] // HEAD_DIM
    h, yl, *kept = run(_lru_fwd, "lru_fwd", proj, 3 * n_heads, n_blocks, cw, cb, wr, br, wi, bi, lam)
    y = run(_outnorm_fwd, "outnorm_fwd", o, yl, ga, gl)
    wout = ex.weight("w_out")
    mixo = run(_mm_nn, "out_proj", y, wout[None], bm=bm, bn=d)
    x2, hn2 = run(_mid_fwd, "mid_fwd", x, mixo, g_post_mix, g_pre_ffn)
    ex.alone("gather_w_up_last")
    wg3, wu3 = ex.weight("w_ffn_gate"), ex.weight("w_ffn_up")
    act_dgate, act_dup, act = run(_swiglu_fwd, "ffn_gate_up", hn2, wg3, wu3, bm=bm)
    ex.alone("gather_w_down")
    wd = ex.weight("w_ffn_down")
    ff = wd.shape[0]
    f = _mm_nn(act, wd[None], bm=bm, bn=d // 2, name="ffn_down")
    loss_cols, dout, df, dg_post_ffn = _final(f, x2, target, g_post_ffn, "final")

    dgate, dup = _swiglu_bwd(df, wd, act_dgate, act_dup, bm=min(t, 2 * MM_ROWS), bo=ff // 4, name="ffn_down_bwd")
    ex.grads["w_ffn_down"] = _mm_tn(act, df, 1, bm=bt, bk=DW_ROWS, out_dtype=ex.grad_dtype, name="ffn_down_dw").reshape(c, ff // c, d)
    ex.grads["w_ffn_gate"] = run(_mm_tn, "ffn_gate_dw", hn2, dgate, c, bm=bt, bk=d // 2, out_dtype=ex.grad_dtype)
    ex.grads["w_ffn_up"] = run(_mm_tn, "ffn_up_dw", hn2, dup, c, bm=bt, bk=d // 2, out_dtype=ex.grad_dtype)
    dhn2_g = run(_mm_nt, "ffn_gate_dx", dgate, wg3, bm=bm, bo=d // 2, out_dtype=F32)
    dhn2_u = run(_mm_nt, "ffn_up_dx", dup, wu3, bm=bm, bo=d // 2, out_dtype=F32)
    dx2, dmix, dg_pre_ffn, dg_post_mix = run(_mid_bwd, "mid_bwd", dhn2_g, dhn2_u, dout, x2, mixo, g_pre_ffn, g_post_mix)
    dy = run(_mm_nt, "out_proj_dx", dmix, wout[None], bm=bm, bo=mix, out_dtype=F32)
    ex.grads["w_out"] = _mm_tn(y, dmix, 1, bm=bt, bk=mix // 4, out_dtype=ex.grad_dtype, name="out_proj_dw").reshape(c, mix // c, d)
    do, dyl, dga, dgl_norm = run(_outnorm_bwd, "outnorm_bwd", dy, o, yl, ga, gl)
    dproj, dcw, dcb, dwr, dbr, dwi, dbi, dlam = run(_lru_bwd, "lru_bwd", proj, 3 * n_heads, n_blocks, h, kept, dyl, cw, cb, wr, wi, lam)
    small = dict(post_mix_norm=dg_post_mix, pre_ffn_norm=dg_pre_ffn, post_ffn_norm=dg_post_ffn, conv_w=dcw, conv_b=dcb,
                 w_rgate=dwr, b_rgate=dbr, w_igate=dwi, b_igate=dbi, lru_lambda=dlam, attn_out_norm=dga, lru_out_norm=dgl_norm)
    ex.packs["early"] = _pack([small[n] for n in _SMALL_EARLY])
    dproj = run(_attn_bwd, "attn_bwd", proj_mx, do, dproj, n_heads)
    ex.grads["w_in"] = _mm_tn(hn1, dproj, c, bm=bt, bk=d // 2, out_dtype=ex.grad_dtype, name="in_proj_dw")
    ex.alone("grads_w_in_swap")
    dhn1 = run(_mm_nt, "in_proj_dx", dproj, win3, bm=bm, bo=d // 2, out_dtype=F32)
    grad_x, small["pre_mix_norm"] = run(_first_bwd, "first_bwd", dhn1, dx2, x, g_pre_mix)
    ex.packs["late"] = _pack([small["pre_mix_norm"], (0.5 / d) * jnp.sum(loss_cols, keepdims=True)])
    return loss_cols, grad_x, small


def _into_slot(wsh, slot, dtype, name):
    rows, n = wsh.shape
    rb = _row_block(rows, 256) if rows % 8 == 0 else rows

    def body(s_ref, w_ref, o_ref):
        o_ref[...] = w_ref[...].astype(o_ref.dtype)

    return pl.pallas_call(
        body,
        grid_spec=pltpu.PrefetchScalarGridSpec(
            num_scalar_prefetch=1, grid=(rows // rb,),
            in_specs=[pl.BlockSpec((rb, n), lambda i, s_ref: (i, 0))],
            out_specs=pl.BlockSpec((None, rb, n), lambda i, s_ref: (s_ref[0], i, 0))),
        out_shape=S((4, rows, n), dtype), compiler_params=_cp("parallel"), name=name)(slot, wsh)


class _Exchange:
    SCHEDULE = {
        "in_proj": [("stream", "w_in"), ("ici", "conv_w"), ("ici", "w_ffn_up", 0)],
        "attn_fwd": [("d2d", "w_ffn_up", 0), ("ici", "w_ffn_gate")],
        "lru_fwd": [("d2d", "w_ffn_gate"), ("ici", "w_out"), ("ici", "w_ffn_up", 1)],
        "outnorm_fwd": [("d2d", "w_out"), ("d2d", "w_ffn_up", 1)],
        "out_proj": [("ici", "w_ffn_up", 2)],
        "mid_fwd": [("d2d", "w_ffn_up", 2), ("ici", "w_ffn_up", 3)],
        "gather_w_up_last": [("d2d", "w_ffn_up", 3)],
        "ffn_gate_up": [("ici", "w_ffn_down")],
        "gather_w_down": [("d2d", "w_ffn_down")],
        "ffn_gate_dw": [("swap", "w_ffn_down")],
        "ffn_up_dw": [("scatter", "w_ffn_down", 0), ("scatter", "w_ffn_down", 1), ("scatter", "w_ffn_down", 2), ("swap", "w_ffn_gate")],
        "ffn_gate_dx": [("scatter", "w_ffn_down", 3), ("scatter", "w_ffn_gate", 0), ("scatter", "w_ffn_gate", 1), ("swap", "w_ffn_up")],
        "ffn_up_dx": [("share", "w_ffn_down"), ("scatter", "w_ffn_gate", 2), ("scatter", "w_ffn_gate", 3), ("scatter", "w_ffn_up", 0)],
        "mid_bwd": [("share", "w_ffn_gate"), ("scatter", "w_ffn_up", 1), ("scatter", "w_ffn_up", 2)],
        "out_proj_dx": [("scatter", "w_ffn_up", 3)],
        "outnorm_bwd": [("share", "w_ffn_up"), ("swap", "w_out")],
        "lru_bwd": [("scatter", "w_out")],
        "attn_bwd": [("share", "w_out"), ("spread", "early")],
        "grads_w_in_swap": [("swap", "w_in")],
        "in_proj_dx": [("scatter", "w_in")],
        "grads_w_in_share": [("share", "w_in"), ("spread", "late")],
    }
    PIECES = 4
    grad_dtype = BF16

    def __init__(self, slots, place):
        self.buf, self.place = dict(slots), place
        self.grads, self.packs, self.swapped, self.part, self.scattered, self.full, self.spreaded = {}, {}, {}, {}, {}, {}, {}

    def weight(self, name):
        b = self.buf[name]
        return b.reshape(-1, b.shape[2]) if name in ("w_out", "w_ffn_down") else b

    def in_proj(self, x, gain, bm):
        car = self.carrier("in_proj")
        out = _in_proj_streamed(x, gain, car, car.streamed, self.place, bm=bm, name="in_proj")
        self.harvest(car)
        return out

    def conv_w(self):
        return jnp.transpose(self.buf["conv_w"], (1, 0, 2)).reshape(CONV_WIDTH, -1)

    def carrier(self, call):
        if call not in self.SCHEDULE:
            return None
        car = _Carrier()
        car.todo, slot = [], {}
        for kind, name, *piece in self.SCHEDULE[call]:
            if kind in ("ici", "d2d", "stream"):
                if name not in slot:
                    slot[name] = car.inplace(self.buf[name])
                    car.todo.append((self.buf, name, slot[name]))
            if kind == "stream":
                car.streamed = slot[name]
            elif kind in ("ici", "d2d"):
                size = self.buf[name].shape[1] // 2 // self.PIECES
                rows = (piece[0] * size, size) if piece else None
                if kind == "ici":
                    car.gather_ici(slot[name], rows, split=name != "conv_w")
                else:
                    car.gather_d2d(slot[name], rows)
            elif kind == "swap":
                g = self.grads[name]
                o = car.fresh((4, g.shape[1] // 2, g.shape[2]), g.dtype)
                car.swap(car.read(g), o)
                car.todo.append((self.swapped, name, o))
            elif kind == "scatter":
                if name not in self.part:
                    self.part[name] = _add_own_half(self.grads[name], self.swapped[name], self.place[1:], "grads_add_" + name)
                p = self.part[name]
                key = ("scatter", name)
                if key not in slot:
                    slot[key] = (car.read(p), car.inplace(self.scattered[name]) if name in self.scattered else car.fresh(p.shape, p.dtype))
                    car.todo.append((self.scattered, name, slot[key][1]))
                size = p.shape[1] // self.PIECES
                car.scatter(*slot[key], (piece[0] * size, size) if piece else None)
            elif kind == "share":
                o = car.inplace(_sum_chips(self.part[name], self.scattered[name], self.place, "grads_sum_" + name))
                car.share(o)
                car.todo.append((self.full, name, o))
            else:
                o = car.fresh((8,) + self.packs[name].shape, F32)
                car.spread(car.read(self.packs[name]), o)
                car.todo.append((self.spreaded, name, o))
        return car

    def harvest(self, car):
        for state, name, o in (car.todo if car is not None else []):
            state[name] = car.results[o]

    def alone(self, call):
        car = self.carrier(call)
        car.run_alone(call)
        self.harvest(car)

    def small_sum(self, key):
        return _sum_devices(self.packs[key], self.spreaded[key], 2 * self.place[0:1] + self.place[1:], "grads_small_sum_" + key)


def _row_block(rows, cap):
    return max(b for b in range(8, cap + 1, 8) if rows % b == 0)


def _add_own_half(g, recv, core, name):
    _, rows, n = g.shape
    half = rows // 2
    rb = _row_block(half, 512)
    nb = half // rb

    def body(c_ref, g_ref, r_ref, o_ref):
        o_ref[...] = (g_ref[...].astype(F32) + r_ref[...].astype(F32)).astype(o_ref.dtype)

    return pl.pallas_call(
        body,
        grid_spec=pltpu.PrefetchScalarGridSpec(
            num_scalar_prefetch=1, grid=(4, nb),
            in_specs=[pl.BlockSpec((None, rb, n), lambda k, i, c_ref: (k, c_ref[0] * nb + i, 0)),
                      pl.BlockSpec((None, rb, n), lambda k, i, c_ref: (k, i, 0))],
            out_specs=pl.BlockSpec((None, rb, n), lambda k, i, c_ref: (k, i, 0))),
        out_shape=S((4, half, n), BF16), compiler_params=_cp("parallel", "parallel"), name=name)(core, g, recv)


def _sum_chips(part, recv, place, name):
    _, rows, n = part.shape
    rb = _row_block(rows, 64)
    nb = rows // rb

    def body(p_ref, own_ref, r0, r1, r2, r3, o_ref):
        own = own_ref[...].astype(F32)
        terms = [jnp.where(p_ref[0] == k, own, r[...].astype(F32)) for k, r in enumerate((r0, r1, r2, r3))]
        o_ref[...] = ((terms[0] + terms[1]) + terms[2]) + terms[3]

    def slot(k):
        return pl.BlockSpec((None, rb, n), lambda i, p_ref: (jnp.where(p_ref[0] == k, (k + 1) % 4, k), i, 0))

    return pl.pallas_call(
        body,
        grid_spec=pltpu.PrefetchScalarGridSpec(
            num_scalar_prefetch=1, grid=(nb,),
            in_specs=[pl.BlockSpec((None, rb, n), lambda i, p_ref: (p_ref[0], i, 0))] + [slot(k) for k in range(4)],
            out_specs=pl.BlockSpec((rb, n), lambda i, p_ref: (p_ref[1] * nb + i, 0))),
        out_shape=S((2 * rows, n), F32), compiler_params=_cp("parallel"), name=name)(place, part, recv, recv, recv, recv)


def _sum_devices(own, spread, me, name):
    rows = own.shape[0]

    def body(me_ref, own_ref, *refs):
        acc = None
        for k, r in enumerate(refs[:8]):
            term = jnp.where(me_ref[0] == k, own_ref[...], r[...])
            acc = term if acc is None else acc + term
        refs[8][...] = acc

    def slot(k):
        return pl.BlockSpec((None, rows, 128), lambda i, me_ref: (jnp.where(me_ref[0] == k, (k + 1) % 8, k), 0, 0))

    whole = pl.BlockSpec((rows, 128), lambda i, me_ref: (0, 0))
    return pl.pallas_call(
        body,
        grid_spec=pltpu.PrefetchScalarGridSpec(num_scalar_prefetch=1, grid=(1,), in_specs=[whole] + [slot(k) for k in range(8)],
                                               out_specs=whole),
        out_shape=S((rows, 128), F32), compiler_params=_cp("arbitrary"), name=name)(me, own, *[spread] * 8)


def _adamw(w, g, m, v, name, regive=False):
    rows, n = w.shape
    rb = rows if rows * n * 4 <= (1 << 21) else _row_block(rows, 256)
    c1 = 1.0 - ADAM_B1 ** ADAM_STEP
    c2 = 1.0 - ADAM_B2 ** ADAM_STEP

    def body(w_ref, g_ref, m_ref, v_ref, d_ref, nm_ref, nv_ref, *again):
        gv = g_ref[...]
        for ref in again:
            ref[...] = gv
        nm = ADAM_B1 * m_ref[...] + (1.0 - ADAM_B1) * gv
        nv = ADAM_B2 * v_ref[...] + (1.0 - ADAM_B2) * (gv * gv)
        nm_ref[...] = nm
        nv_ref[...] = nv
        d_ref[...] = -ADAM_LR * ((nm / c1) / (jnp.sqrt(nv / c2) + ADAM_EPS) + ADAM_WD * w_ref[...])

    bs = pl.BlockSpec((rb, n), lambda i: (i, 0))
    n_out = 4 if regive else 3
    return pl.pallas_call(body, grid=(rows // rb,), in_specs=[bs] * 4, out_specs=[bs] * n_out, out_shape=[S((rows, n), F32)] * n_out,
                          compiler_params=_cp("parallel"), name=name)(w, g, m, v)


_BIG = ("w_in", "w_out", "w_ffn_gate", "w_ffn_up", "w_ffn_down")
_SMALL = ("pre_mix_norm", "post_mix_norm", "pre_ffn_norm", "post_ffn_norm", "conv_w", "conv_b", "w_rgate", "b_rgate",
          "w_igate", "b_igate", "lru_lambda", "attn_out_norm", "lru_out_norm")
_SMALL_EARLY = _SMALL[1:]
_WEIGHTS = ("pre_mix_norm", "post_mix_norm", "pre_ffn_norm", "post_ffn_norm", "w_in", "conv_w", "conv_b", "w_rgate", "b_rgate",
            "w_igate", "b_igate", "lru_lambda", "attn_out_norm", "lru_out_norm", "w_out", "w_ffn_gate", "w_ffn_up", "w_ffn_down")


def _pack(arrays):
    flat = []
    for a in arrays:
        f = a.reshape(-1)
        flat.append(jnp.pad(f, (0, (-f.shape[0]) % 1024)))
    return jnp.concatenate(flat).reshape(-1, 128)


def _unpack(packed, shapes):
    out, pos = [], 0
    flat = packed.reshape(-1)
    for s in shapes:
        size = math.prod(s)
        out.append(flat[pos:pos + size].reshape(s))
        pos += size + (-size) % 1024
    return out


def kernel(x, pre_mix_norm, post_mix_norm, pre_ffn_norm, post_ffn_norm, w_in, conv_w, conv_b, w_rgate, b_rgate, w_igate, b_igate, lru_lambda, attn_out_norm, lru_out_norm, w_out, w_ffn_gate, w_ffn_up, w_ffn_down, loss_target, m_pre_mix_norm, m_post_mix_norm, m_pre_ffn_norm, m_post_ffn_norm, m_w_in, m_conv_w, m_conv_b, m_w_rgate, m_b_rgate, m_w_igate, m_b_igate, m_lru_lambda, m_attn_out_norm, m_lru_out_norm, m_w_out, m_w_ffn_gate, m_w_ffn_up, m_w_ffn_down, v_pre_mix_norm, v_post_mix_norm, v_pre_ffn_norm, v_post_ffn_norm, v_w_in, v_conv_w, v_conv_b, v_w_rgate, v_b_rgate, v_w_igate, v_b_igate, v_lru_lambda, v_attn_out_norm, v_lru_out_norm, v_w_out, v_w_ffn_gate, v_w_ffn_up, v_w_ffn_down):
    given = dict(locals())
    w = {n: given[n][0] for n in _WEIGHTS}
    m = {n: given["m_" + n][0] for n in _WEIGHTS}
    v = {n: given["v_" + n][0] for n in _WEIGHTS}
    xs, target = x[0], loss_target[0]
    d = xs.shape[1]
    chip = (2 * lax.axis_index("x") + lax.axis_index("y")).astype(jnp.int32)
    place = jnp.stack([chip, lax.axis_index("c").astype(jnp.int32)])

    slots = {n: _into_slot(w[n], place[0:1], _MXU, "slot_" + n) for n in _BIG}
    slots["conv_w"] = _into_slot(w["conv_w"], place[0:1], F32, "slot_conv_w")
    ex = _Exchange(slots, place)
    row = lambda a: a.reshape(1, -1)
    norms = tuple(row(w[n]) for n in ("pre_mix_norm", "post_mix_norm", "pre_ffn_norm", "post_ffn_norm"))

    loss_cols, grad_x, small = _local_step(
        xs, target, norms, ex, row(w["conv_b"]), w["w_rgate"], row(w["b_rgate"]),
        w["w_igate"], row(w["b_igate"]), row(w["lru_lambda"]), row(w["attn_out_norm"]), row(w["lru_out_norm"]))


    ex.alone("grads_w_in_share")
    reduced = {n: ex.full[n] for n in _BIG}
    early = _unpack(ex.small_sum("early"), [small[n].shape for n in _SMALL_EARLY])
    late = _unpack(ex.small_sum("late"), [small["pre_mix_norm"].shape, (1, 1)])
    loss = late[1][0, 0]
    for n, g in zip(_SMALL_EARLY + ("pre_mix_norm",), early + late[:1]):
        reduced[n] = g.reshape(w[n].shape) if n != "conv_w" else lax.dynamic_slice_in_dim(g, chip * w[n].shape[1], w[n].shape[1], axis=1)

    delta, new_m, new_v = {}, {}, {}
    for n in _BIG:
        delta[n], new_m[n], new_v[n], reduced[n] = _adamw(w[n], reduced[n], m[n], v[n], "adamw_" + n, regive=True)
    shapes = [w[n].shape for n in _SMALL]
    packed = _adamw(*[_pack([src[n] for n in _SMALL]) for src in (w, reduced, m, v)], "adamw_small")
    for out, p in zip((delta, new_m, new_v), packed):
        out.update(zip(_SMALL, _unpack(p, shapes)))

    lead = lambda a: a[None]
    return (loss, lead(grad_x), *[lead(reduced[n]) for n in _WEIGHTS], *[lead(delta[n]) for n in _WEIGHTS],
            *[lead(new_m[n]) for n in _WEIGHTS], *[lead(new_v[n]) for n in _WEIGHTS])
```

```python
import functools
import math

import jax
import jax.numpy as jnp
from jax import lax
from jax.experimental import pallas as pl
from jax.experimental.pallas import tpu as pltpu

F32 = jnp.float32
BF16 = jnp.bfloat16
_MXU = BF16
S = jax.ShapeDtypeStruct

RMS_EPS = 1e-6
HEAD_DIM = 128
CONV_WIDTH = 4
LRU_C = 8.0
ADAM_LR, ADAM_B1, ADAM_B2, ADAM_EPS, ADAM_WD, ADAM_STEP = 0.001, 0.9, 0.999, 1e-08, 0.01, 10
EXP_CUT = -105.0
VMEM_LIMIT = 60 * 1024 * 1024
ROW_TILE = 256
SEQ_TILE = 256
ATTN_BLOCK = 256
ATTN_HEADS = 2
MM_ROWS = 512
DW_TOKENS = 2048
DW_ROWS = 512
MESH = pl.DeviceIdType.MESH


def _cp(*sem):
    return pltpu.CompilerParams(dimension_semantics=sem, vmem_limit_bytes=VMEM_LIMIT)


def _dot(a, b):
    return jnp.dot(a, b, preferred_element_type=F32)


def _dot_nt(a, b):
    return lax.dot_general(a, b, (((1,), (1,)), ((), ())), preferred_element_type=F32)


def _dot_tn(a, b):
    return lax.dot_general(a, b, (((0,), (0,)), ((), ())), preferred_element_type=F32)


def _rstd(v):
    return lax.rsqrt(jnp.mean(v * v, axis=-1, keepdims=True) + RMS_EPS)


def _rms_bwd(dn, vh, r, gain):
    dvh = dn * gain
    dv = r * (dvh - vh * jnp.mean(dvh * vh, axis=-1, keepdims=True))
    return dv, jnp.sum(dn * vh, axis=0, keepdims=True)


def _log_sigmoid(z):
    return jnp.minimum(z, 0.0) - jnp.log(1.0 + jnp.exp(-jnp.abs(z)))


def _expm1(v):
    small = v * (1.0 + v * (0.5 + v * (1.0 / 6.0 + v * (1.0 / 24.0 + v * (1.0 / 120.0)))))
    return jnp.where(jnp.abs(v) < 0.04, small, jnp.exp(v) - 1.0)


_GELU_C = math.sqrt(2.0 / math.pi)


def _gelu(v):
    return 0.5 * v * (1.0 + jnp.tanh(_GELU_C * (v + 0.044715 * v * v * v)))


def _gelu_grad(v):
    th = jnp.tanh(_GELU_C * (v + 0.044715 * v * v * v))
    return 0.5 * (1.0 + th) + 0.5 * v * (1.0 - th * th) * _GELU_C * (1.0 + 3.0 * 0.044715 * v * v)


def _row_spec(tm, d):
    return pl.BlockSpec((tm, d), lambda i: (i, 0))


def _vec_spec(d):
    return pl.BlockSpec((1, d), lambda i: (0, 0))


_ANY = pl.BlockSpec(memory_space=pl.ANY)


def _place():
    x, y, c = lax.axis_index("x"), lax.axis_index("y"), lax.axis_index("c")
    return x, y, c, [(1 - x, y), (x, 1 - y), (1 - x, 1 - y)]


def _remote(src, dst, send_sem, recv_sem, to):
    return pltpu.make_async_remote_copy(src_ref=src, dst_ref=dst, send_sem=send_sem, recv_sem=recv_sem,
                                        device_id=to, device_id_type=MESH)


class _Carrier:
    def __init__(self):
        self.inputs, self.out_shapes, self.aliases, self.ops, self.n_sems, self.results = [], [], {}, [], 0, None

    def inplace(self, arr):
        self.aliases[len(self.inputs)] = len(self.out_shapes)
        self.inputs.append(arr)
        self.out_shapes.append(S(arr.shape, arr.dtype))
        return len(self.out_shapes) - 1

    def read(self, arr):
        self.inputs.append(arr)
        return len(self.inputs) - 1

    def fresh(self, shape, dtype):
        self.out_shapes.append(S(shape, dtype))
        return len(self.out_shapes) - 1

    def _add(self, n_sems, copies):
        base = self.n_sems
        self.n_sems += n_sems

        def start(ins, outs, send, recv):
            for k, (src, dst, _, to) in enumerate(copies(ins, outs)):
                _remote(src, dst, send.at[base + k], recv.at[base + k], to).start()

        def finish(ins, outs, send, recv):
            for k, (src, _, land, to) in enumerate(copies(ins, outs)):
                _remote(src, land, send.at[base + k], recv.at[base + k], to).wait()

        self.ops.append((start, finish))

    def gather_ici(self, o, rows=None, split=True):
        half = self.out_shapes[o].shape[1] // 2
        lo, size = rows or (0, half)

        def copies(ins, outs):
            x, y, c, chips = _place()
            part = (lambda ref: ref.at[pl.ds(c * half + lo, size)]) if split else (lambda ref: ref)
            mine = part(outs[o].at[2 * x + y])
            return [(mine, mine, part(outs[o].at[2 * px + py]), (px, py, c)) for px, py in chips]

        self._add(3, copies)

    def gather_d2d(self, o, rows=None):
        half = self.out_shapes[o].shape[1] // 2
        lo, size = rows or (0, half)

        def copies(ins, outs):
            x, y, c, chips = _place()
            at = lambda k, cc: outs[o].at[k].at[pl.ds(cc * half + lo, size)]
            return [(at(2 * px + py, c), at(2 * px + py, c), at(2 * px + py, 1 - c), (x, y, 1 - c)) for px, py in chips]

        self._add(3, copies)

    def swap(self, i, o):
        half = self.inputs[i].shape[1] // 2

        def copies(ins, outs):
            x, y, c, _ = _place()
            return [(ins[i].at[:, pl.ds((1 - c) * half, half)], outs[o], outs[o], (x, y, 1 - c))]

        self._add(1, copies)

    def scatter(self, i, o, rows=None):
        lo, size = rows or (0, self.inputs[i].shape[1])

        def copies(ins, outs):
            x, y, c, chips = _place()
            cut = lambda ref: ref.at[pl.ds(lo, size)]
            return [(cut(ins[i].at[2 * px + py]), cut(outs[o].at[2 * x + y]), cut(outs[o].at[2 * px + py]), (px, py, c)) for px, py in chips]

        self._add(3, copies)

    def share(self, o):
        r = self.out_shapes[o].shape[0] // 2

        def copies(ins, outs):
            x, y, c, _ = _place()
            mine = outs[o].at[pl.ds(c * r, r)]
            return [(mine, mine, outs[o].at[pl.ds((1 - c) * r, r)], (x, y, 1 - c))]

        self._add(1, copies)

    def spread(self, i, o):
        def copies(ins, outs):
            x, y, c, _ = _place()
            me = 4 * x + 2 * y + c
            out = []
            for d in range(1, 8):
                to, frm = (me + d) % 8, (me + 8 - d) % 8
                out.append((ins[i], outs[o].at[me], outs[o].at[frm], (to // 4, (to // 2) % 2, to % 2)))
            return out

        self._add(7, copies)

    def _pallas(self, body, n_in, n_out, scratch, **kw):
        k_in, k_out = len(self.inputs), len(self.out_shapes)
        grid = kw.get("grid", ())

        def wrapped(*refs):
            ins, cins = refs[:n_in], refs[n_in:n_in + k_in]
            outs = refs[n_in + k_in:n_in + k_in + n_out]
            couts = refs[n_in + k_in + n_out:n_in + k_in + n_out + k_out]
            own = refs[n_in + k_in + n_out + k_out:]
            send, recv = own[len(scratch):]
            ids = [pl.program_id(a) for a in range(len(grid))]
            first = functools.reduce(jnp.logical_and, [a == 0 for a in ids], True)
            last = functools.reduce(jnp.logical_and, [a == g - 1 for a, g in zip(ids, grid)], True)

            def go(stage):
                for op in self.ops:
                    op[stage](cins, couts, send, recv)

            if grid:
                pl.when(first)(lambda: go(0))
                body(*ins, *outs, *own[:len(scratch)])
                pl.when(last)(lambda: go(1))
            else:
                go(0)
                go(1)

        sem = pltpu.SemaphoreType.DMA((self.n_sems,))
        return pl.pallas_call(
            wrapped, in_specs=list(kw.get("in_specs", [])) + [_ANY] * k_in, out_specs=list(kw.get("out_specs", [])) + [_ANY] * k_out,
            out_shape=list(kw.get("out_shape", [])) + self.out_shapes, scratch_shapes=list(scratch) + [sem, sem],
            input_output_aliases={**kw.get("aliases", {}), **{n_in + i: n_out + o for i, o in self.aliases.items()}}, name=kw["name"],
            **({"grid": grid, "compiler_params": _cp(*["arbitrary"] * len(grid))} if grid else {}))

    def run(self, body, kw, *args):
        single = not isinstance(kw["out_shape"], (list, tuple))
        out_shape = [kw["out_shape"]] if single else list(kw["out_shape"])
        out_specs = [kw["out_specs"]] if single else list(kw["out_specs"])
        res = self._pallas(body, len(args), len(out_shape), kw.get("scratch_shapes", []), grid=kw["grid"], in_specs=kw["in_specs"],
                           out_specs=out_specs, out_shape=out_shape, name=kw["name"],
                           aliases=kw.get("input_output_aliases", {}))(*args, *self.inputs)
        self.results = list(res[len(out_shape):])
        return res[0] if single else list(res[:len(out_shape)])

    def run_alone(self, name):
        self.results = list(self._pallas(None, 0, 0, [], name=name)(*self.inputs))


def _call(comm, body, **kw):
    if comm is None:
        return pl.pallas_call(body, **kw)
    return functools.partial(comm.run, body, kw)


def _in_proj_streamed(x, gain, car, o_w, place, *, bm, name):
    m, k = x.shape
    n = car.out_shapes[o_w].shape[2]
    ni, half = m // bm, k // 2
    k_in, k_out = len(car.inputs), len(car.out_shapes)
    order = lambda p: ((p & 1) << 1) | (p >> 1)

    def body(place_ref, x_ref, g_ref, *refs):
        cins, (hn_ref, o_ref, ob_ref), couts = refs[:k_in], refs[k_in:k_in + 3], refs[k_in + 3:k_in + 3 + k_out]
        wbuf, local, ici_send, ici_recv, d2d_send, d2d_recv, send, recv = refs[k_in + 3 + k_out:]
        p, i = pl.program_id(0), pl.program_id(1)
        x, y, c, chips = _place()
        me = 2 * x + y
        rows = lambda chunk, cc: couts[o_w].at[chunk].at[pl.ds(cc * half, half)]

        @pl.when(jnp.logical_and(p == 0, i == 0))
        def _():
            for j, (px, py) in enumerate(chips):
                _remote(rows(me, c), rows(me, c), ici_send.at[j], ici_recv.at[j], (px, py, c)).start()
            for op in car.ops:
                op[0](cins, couts, send, recv)

        for j, (px, py) in enumerate(chips):
            @pl.when(jnp.logical_and(p == j + 1, i == 0))
            def _(j=j, px=px, py=py):
                landed, other = rows(2 * px + py, c), rows(2 * px + py, 1 - c)
                _remote(landed, landed, ici_send.at[j], ici_recv.at[j], (px, py, c)).wait_recv()
                _remote(landed, landed, d2d_send.at[j], d2d_recv.at[j], (x, y, 1 - c)).start()
                _remote(other, other, d2d_send.at[j], d2d_recv.at[j], (x, y, 1 - c)).wait_recv()

        @pl.when(i == 0)
        def _():
            cp = pltpu.make_async_copy(couts[o_w].at[me ^ order(p)], wbuf, local.at[0])
            cp.start()
            cp.wait()

        xv = x_ref[...]
        hn = ((xv * _rstd(xv)) * g_ref[...]).astype(_MXU)
        hn_ref[...] = hn
        res = _dot(hn, wbuf[...])
        o_ref[...] = res
        ob_ref[...] = res.astype(ob_ref.dtype)

        @pl.when(jnp.logical_and(p == 3, i == ni - 1))
        def _():
            for j, (px, py) in enumerate(chips):
                _remote(rows(me, c), rows(me, c), ici_send.at[j], ici_recv.at[j], (px, py, c)).wait_send()
                _remote(rows(me, c), rows(me, c), d2d_send.at[j], d2d_recv.at[j], (x, y, 1 - c)).wait_send()
            for op in car.ops:
                op[1](cins, couts, send, recv)

    ospec = pl.BlockSpec((bm, n), lambda p, i, place_ref: (i, place_ref[0] ^ order(p)))
    rows = pl.BlockSpec((bm, k), lambda p, i, place_ref: (i, 0))
    three, sems = pltpu.SemaphoreType.DMA((3,)), pltpu.SemaphoreType.DMA((max(car.n_sems, 1),))
    res = pl.pallas_call(
        body,
        grid_spec=pltpu.PrefetchScalarGridSpec(
            num_scalar_prefetch=1, grid=(4, ni),
            in_specs=[rows, pl.BlockSpec((1, k), lambda p, i, place_ref: (0, 0))] + [_ANY] * k_in,
            out_specs=[pl.BlockSpec((bm, k), lambda p, i, place_ref: (p * ni + i, 0)), ospec, ospec] + [_ANY] * k_out,
            scratch_shapes=[pltpu.VMEM((k, n), _MXU), pltpu.SemaphoreType.DMA((1,)), three, three, three, three, sems, sems]),
        out_shape=[S((4 * m, k), _MXU), S((m, 4 * n), F32), S((m, 4 * n), _MXU)] + car.out_shapes,
        input_output_aliases={3 + a: 3 + o for a, o in car.aliases.items()},
        compiler_params=_cp("arbitrary", "arbitrary"), name=name)(place, x, gain, *car.inputs)
    car.results = list(res[3:])
    return res[0], res[1], res[2]


def _mm_nn(a, b3, *, bm, bn, name, also=None, comm=None):
    m, k = a.shape
    c, _, n = b3.shape
    ni, nj = m // bm, n // bn

    def body(a_ref, b_ref, *o_refs):
        res = _dot(a_ref[...], b_ref[...])
        for o_ref in o_refs:
            o_ref[...] = res.astype(o_ref.dtype)

    ospec = pl.BlockSpec((bm, bn), lambda cc, j, i: (i, cc * nj + j))
    dtypes = [F32] + ([] if also is None else [also])
    out = _call(
        comm, body, grid=(c, nj, ni),
        in_specs=[pl.BlockSpec((bm, k), lambda cc, j, i: (i, 0)), pl.BlockSpec((None, k, bn), lambda cc, j, i: (cc, 0, j))],
        out_specs=[ospec] * len(dtypes), out_shape=[S((m, c * n), dt) for dt in dtypes],
        compiler_params=_cp("parallel", "parallel", "parallel"), name=name)(a, b3)
    return out[0] if also is None else out


def _mm_nt(a, b3, *, bm, bo, out_dtype, name, comm=None):
    m = a.shape[0]
    c, ko, n = b3.shape
    ni, nj = m // bm, ko // bo

    def body(a_ref, b_ref, o_ref):
        acc = _dot_nt(a_ref[:, 0:n], b_ref[0])
        for cc in range(1, c):
            acc = acc + _dot_nt(a_ref[:, cc * n:(cc + 1) * n], b_ref[cc])
        o_ref[...] = acc.astype(o_ref.dtype)

    return _call(
        comm, body, grid=(nj, ni),
        in_specs=[pl.BlockSpec((bm, c * n), lambda j, i: (i, 0)),
                  pl.BlockSpec((c, bo, n), lambda j, i: (0, j, 0))],
        out_specs=pl.BlockSpec((bm, bo), lambda j, i: (i, j)),
        out_shape=S((m, ko), out_dtype),
        compiler_params=_cp("parallel", "parallel"), name=name)(a, b3)


def _mm_tn(a, b, c, *, bm, bk, out_dtype, name, comm=None):
    m, k = b.shape[0], a.shape[1]
    n = b.shape[1] // c
    nm, nk = m // bm, k // bk

    def body(a_ref, b_ref, o_ref, acc):
        mm = pl.program_id(2)

        @pl.when(mm == 0)
        def _():
            acc[...] = jnp.zeros_like(acc)

        acc[...] += _dot_tn(a_ref[...], b_ref[...])

        @pl.when(mm == nm - 1)
        def _():
            o_ref[...] = acc[...].astype(o_ref.dtype)

    return _call(
        comm, body, grid=(c, nk, nm),
        in_specs=[pl.BlockSpec((bm, bk), lambda cc, j, mm: (mm, j)),
                  pl.BlockSpec((bm, n), lambda cc, j, mm: (mm, cc))],
        out_specs=pl.BlockSpec((None, bk, n), lambda cc, j, mm: (cc, j, 0)),
        out_shape=S((c, k, n), out_dtype),
        scratch_shapes=[pltpu.VMEM((bk, n), F32)],
        compiler_params=_cp("parallel", "parallel", "arbitrary"), name=name)(a, b)


def _swiglu_fwd(hn, wg3, wu3, *, bm, name, comm=None):
    m, k = hn.shape
    c, _, n = wg3.shape

    def body(a_ref, g_ref, u_ref, dgate_ref, dup_ref, act_ref):
        a = a_ref[...]
        gate = _dot(a, g_ref[...])
        up = _dot(a, u_ref[...])
        sg = jax.nn.sigmoid(gate)
        silu = gate * sg
        dgate_ref[...] = (up * (sg * (1.0 + gate * (1.0 - sg)))).astype(dgate_ref.dtype)
        dup_ref[...] = silu.astype(dup_ref.dtype)
        act_ref[...] = (silu * up).astype(act_ref.dtype)

    wspec = pl.BlockSpec((None, k, n), lambda cc, i: (cc, 0, 0))
    ospec = pl.BlockSpec((bm, n), lambda cc, i: (i, cc))
    return _call(
        comm, body, grid=(c, m // bm),
        in_specs=[pl.BlockSpec((bm, k), lambda cc, i: (i, 0)), wspec, wspec],
        out_specs=[ospec, ospec, ospec],
        out_shape=[S((m, c * n), _MXU), S((m, c * n), _MXU), S((m, c * n), _MXU)],
        compiler_params=_cp("parallel", "parallel"), name=name)(hn, wg3, wu3)


def _swiglu_bwd(df, wd, act_dgate, act_dup, *, bm, bo, name):
    m, k = df.shape
    ko = wd.shape[0]

    def body(a_ref, b_ref, g_ref, u_ref, dg_ref, du_ref):
        dact = _dot_nt(a_ref[...], b_ref[...])
        dg_ref[...] = (dact * g_ref[...].astype(F32)).astype(dg_ref.dtype)
        du_ref[...] = (dact * u_ref[...].astype(F32)).astype(du_ref.dtype)

    ospec = pl.BlockSpec((bm, bo), lambda j, i: (i, j))
    return pl.pallas_call(
        body, grid=(ko // bo, m // bm),
        in_specs=[pl.BlockSpec((bm, k), lambda j, i: (i, 0)), pl.BlockSpec((bo, k), lambda j, i: (j, 0)), ospec, ospec],
        out_specs=[ospec, ospec],
        out_shape=[S((m, ko), _MXU), S((m, ko), _MXU)],
        compiler_params=_cp("parallel", "parallel"), name=name)(df, wd, act_dgate, act_dup)


def _rms_fwd(x, gain, name):
    t, d = x.shape
    tm = min(t, ROW_TILE)

    def body(x_ref, g_ref, o_ref):
        xv = x_ref[...]
        o_ref[...] = ((xv * _rstd(xv)) * g_ref[...]).astype(o_ref.dtype)

    return pl.pallas_call(body, grid=(t // tm,), in_specs=[_row_spec(tm, d), _vec_spec(d)], out_specs=_row_spec(tm, d),
                          out_shape=S((t, d), _MXU), compiler_params=_cp("parallel"), name=name)(x, gain)


def _outnorm_fwd(o, yl, ga, gl, name, comm=None):
    t, w = o.shape
    tm = min(t, ROW_TILE)

    def body(o_ref, l_ref, ga_ref, gl_ref, y_ref):
        ov, lv = o_ref[...], l_ref[...]
        y_ref[:, :w] = ((ov * _rstd(ov)) * ga_ref[...]).astype(y_ref.dtype)
        y_ref[:, w:] = ((lv * _rstd(lv)) * gl_ref[...]).astype(y_ref.dtype)

    return _call(comm, body, grid=(t // tm,), in_specs=[_row_spec(tm, w), _row_spec(tm, w), _vec_spec(w), _vec_spec(w)],
                 out_specs=_row_spec(tm, 2 * w), out_shape=S((t, 2 * w), _MXU),
                 compiler_params=_cp("parallel"), name=name)(o, yl, ga, gl)


def _mid_fwd(x, mix, g_post, g_pre, name, comm=None):
    t, d = x.shape
    tm = min(t, ROW_TILE)

    def body(x_ref, m_ref, gp_ref, gn_ref, x2_ref, hn_ref):
        mv = m_ref[...]
        x2 = x_ref[...] + (mv * _rstd(mv)) * gp_ref[...]
        x2_ref[...] = x2
        hn_ref[...] = ((x2 * _rstd(x2)) * gn_ref[...]).astype(hn_ref.dtype)

    return _call(comm, body, grid=(t // tm,), in_specs=[_row_spec(tm, d), _row_spec(tm, d), _vec_spec(d), _vec_spec(d)],
                          out_specs=[_row_spec(tm, d), _row_spec(tm, d)], out_shape=[S((t, d), F32), S((t, d), _MXU)],
                          compiler_params=_cp("parallel"), name=name)(x, mix, g_post, g_pre)


def _final(f, x2, target, g_post, name):
    t, d = f.shape
    tm = min(t, ROW_TILE)

    def body(f_ref, x2_ref, t_ref, g_ref, loss_ref, dout_ref, df_ref, dg_ref):
        @pl.when(pl.program_id(0) == 0)
        def _():
            loss_ref[...] = jnp.zeros_like(loss_ref)
            dg_ref[...] = jnp.zeros_like(dg_ref)

        fv = f_ref[...]
        r = _rstd(fv)
        fh = fv * r
        err = (x2_ref[...] + fh * g_ref[...]) - t_ref[...]
        loss_ref[...] += jnp.sum(err * err, axis=0, keepdims=True)
        dout = err * (1.0 / d)
        dout_ref[...] = dout
        dfv, dg = _rms_bwd(dout, fh, r, g_ref[...])
        df_ref[...] = dfv.astype(df_ref.dtype)
        dg_ref[...] += dg

    return pl.pallas_call(
        body, grid=(t // tm,),
        in_specs=[_row_spec(tm, d), _row_spec(tm, d), _row_spec(tm, d), _vec_spec(d)],
        out_specs=[_vec_spec(d), _row_spec(tm, d), _row_spec(tm, d), _vec_spec(d)],
        out_shape=[S((1, d), F32), S((t, d), F32), S((t, d), _MXU), S((1, d), F32)],
        compiler_params=_cp("arbitrary"), name=name)(f, x2, target, g_post)


def _mid_bwd(dhn_a, dhn_b, dout, x2, mix, g_pre, g_post, name, comm=None):
    t, d = x2.shape
    tm = min(t, ROW_TILE)

    def body(da_ref, db_ref, do_ref, x2_ref, m_ref, gn_ref, gp_ref, dx2_ref, dm_ref, dgn_ref, dgp_ref):
        @pl.when(pl.program_id(0) == 0)
        def _():
            dgn_ref[...] = jnp.zeros_like(dgn_ref)
            dgp_ref[...] = jnp.zeros_like(dgp_ref)

        x2 = x2_ref[...]
        r = _rstd(x2)
        dxa, dgn = _rms_bwd(da_ref[...] + db_ref[...], x2 * r, r, gn_ref[...])
        dx2 = do_ref[...] + dxa
        dx2_ref[...] = dx2
        dgn_ref[...] += dgn
        mv = m_ref[...]
        rm = _rstd(mv)
        dmv, dgp = _rms_bwd(dx2, mv * rm, rm, gp_ref[...])
        dm_ref[...] = dmv.astype(dm_ref.dtype)
        dgp_ref[...] += dgp

    rs, vs = _row_spec(tm, d), _vec_spec(d)
    return _call(
        comm, body, grid=(t // tm,), in_specs=[rs, rs, rs, rs, rs, vs, vs], out_specs=[rs, rs, vs, vs],
        out_shape=[S((t, d), F32), S((t, d), _MXU), S((1, d), F32), S((1, d), F32)],
        compiler_params=_cp("arbitrary"), name=name)(dhn_a, dhn_b, dout, x2, mix, g_pre, g_post)


def _first_bwd(dhn, dx2, x, gain, name, comm=None):
    t, d = x.shape
    tm = min(t, ROW_TILE)

    def body(dh_ref, dx2_ref, x_ref, g_ref, dx_ref, dg_ref):
        @pl.when(pl.program_id(0) == 0)
        def _():
            dg_ref[...] = jnp.zeros_like(dg_ref)

        xv = x_ref[...]
        r = _rstd(xv)
        dxa, dg = _rms_bwd(dh_ref[...], xv * r, r, g_ref[...])
        dx_ref[...] = dx2_ref[...] + dxa
        dg_ref[...] += dg

    rs, vs = _row_spec(tm, d), _vec_spec(d)
    return _call(comm, body, grid=(t // tm,), in_specs=[rs, rs, rs, vs], out_specs=[rs, vs],
                          out_shape=[S((t, d), F32), S((1, d), F32)], compiler_params=_cp("arbitrary"), name=name)(dhn, dx2, x, gain)


def _outnorm_bwd(dy, o, yl, ga, gl, name, comm=None):
    t, w = o.shape
    tm = min(t, ROW_TILE)

    def body(dy_ref, o_ref, l_ref, ga_ref, gl_ref, do_ref, dl_ref, dga_ref, dgl_ref):
        @pl.when(pl.program_id(0) == 0)
        def _():
            dga_ref[...] = jnp.zeros_like(dga_ref)
            dgl_ref[...] = jnp.zeros_like(dgl_ref)

        ov, lv = o_ref[...], l_ref[...]
        ra, rl = _rstd(ov), _rstd(lv)
        dov, dga = _rms_bwd(dy_ref[:, :w], ov * ra, ra, ga_ref[...])
        dlv, dgl = _rms_bwd(dy_ref[:, w:], lv * rl, rl, gl_ref[...])
        do_ref[...] = dov.astype(do_ref.dtype)
        dl_ref[...] = dlv
        dga_ref[...] += dga
        dgl_ref[...] += dgl

    rs, vs = _row_spec(tm, w), _vec_spec(w)
    return _call(comm, body, grid=(t // tm,), in_specs=[_row_spec(tm, 2 * w), rs, rs, vs, vs], out_specs=[rs, rs, vs, vs],
                          out_shape=[S((t, w), _MXU), S((t, w), F32), S((1, w), F32), S((1, w), F32)],
                          compiler_params=_cp("arbitrary"), name=name)(dy, o, yl, ga, gl)


def _tri_sum(v, tri):
    return _dot(v.astype(_MXU), tri)


def _attn_tile(qb, kb, row, col, shift, scale):
    z = _dot_nt(qb, kb) * scale
    mask = (col + shift) < row
    lb = _log_sigmoid(z)
    lm = jnp.where(mask, lb - z, 0.0)
    return mask, lb, lm


def _attn_fwd(proj, n_heads, name, comm=None):
    t = proj.shape[0]
    bq = min(t, ATTN_BLOCK)
    nq = t // bq
    scale = 1.0 / math.sqrt(HEAD_DIM)

    heads = [slice(a * HEAD_DIM, (a + 1) * HEAD_DIM) for a in range(ATTN_HEADS)]

    def body(q_ref, k_ref, v_ref, o_ref):
        row = lax.broadcasted_iota(jnp.int32, (bq, bq), 0)
        col = lax.broadcasted_iota(jnp.int32, (bq, bq), 1)
        tri = (row > col).astype(_MXU)

        def per_q(qi, _):
            q0 = pl.multiple_of(qi * bq, bq)
            qbs = [q_ref[pl.ds(q0, bq), hd] for hd in heads]

            def cond(st):
                return jnp.logical_and(st[0] >= 0, st[1])

            def step(st):
                kj, _, carries, accs = st
                k0 = pl.multiple_of(kj * bq, bq)
                alive, new_carries, new_accs = None, [], []
                for hd, qb, carry, acc in zip(heads, qbs, carries, accs):
                    mask, lb, lm = _attn_tile(qb, k_ref[pl.ds(k0, bq), hd], row, col, (kj - qi) * bq, scale)
                    w = jnp.where(mask, jnp.exp(lb + _tri_sum(lm, tri) + carry), 0.0)
                    new_accs.append(acc + _dot(w.astype(_MXU), v_ref[pl.ds(k0, bq), hd]))
                    carry = carry + jnp.sum(lm, axis=1, keepdims=True)
                    new_carries.append(carry)
                    live = jnp.max(carry) > EXP_CUT
                    alive = live if alive is None else jnp.logical_or(alive, live)
                return kj - 1, alive, tuple(new_carries), tuple(new_accs)

            st = lax.while_loop(cond, step, (qi, jnp.bool_(True), (jnp.zeros((bq, 1), F32),) * ATTN_HEADS,
                                             (jnp.zeros((bq, HEAD_DIM), F32),) * ATTN_HEADS))
            for hd, acc in zip(heads, st[3]):
                o_ref[pl.ds(q0, bq), hd] = acc
            return 0

        lax.fori_loop(0, nq, per_q, 0)

    groups = n_heads // ATTN_HEADS
    hs = lambda off: pl.BlockSpec((t, ATTN_HEADS * HEAD_DIM), lambda h: (0, off + h))
    return _call(
        comm, body, grid=(groups,), in_specs=[hs(0), hs(groups), hs(2 * groups)], out_specs=hs(0),
        out_shape=S((t, n_heads * HEAD_DIM), F32), compiler_params=_cp("parallel"), name=name)(proj, proj, proj)


def _emit(blocks, out_ref, starts, sems):
    copies = [pltpu.make_async_copy(b, out_ref.at[:, pl.ds(c0, b.shape[1])], sems.at[k]) for k, (b, c0) in enumerate(zip(blocks, starts))]
    for cp in copies:
        cp.start()
    for cp in copies:
        cp.wait()


def _attn_bwd(proj, do, dproj, n_heads, name, comm=None):
    t = proj.shape[0]
    bq = min(t, ATTN_BLOCK)
    nq = t // bq
    scale = 1.0 / math.sqrt(HEAD_DIM)
    groups = n_heads // ATTN_HEADS
    wide = ATTN_HEADS * HEAD_DIM

    heads = [slice(a * HEAD_DIM, (a + 1) * HEAD_DIM) for a in range(ATTN_HEADS)]

    def body(q_ref, k_ref, v_ref, do_ref, _, dproj_ref, dka_ref, dva_ref, g_ref, b_ref, dq_ref, dk_ref, dv_ref, out_sems):
        group = pl.program_id(0)
        dka_ref[...] = jnp.zeros_like(dka_ref)
        dva_ref[...] = jnp.zeros_like(dva_ref)
        row = lax.broadcasted_iota(jnp.int32, (bq, bq), 0)
        col = lax.broadcasted_iota(jnp.int32, (bq, bq), 1)
        tri = (row > col).astype(_MXU)
        tri_lt = (row < col).astype(_MXU)

        def per_q(qi, _):
            q0 = pl.multiple_of(qi * bq, bq)
            qbs = [q_ref[pl.ds(q0, bq), hd] for hd in heads]
            dobs = [do_ref[pl.ds(q0, bq), hd] for hd in heads]

            def cond(st):
                return jnp.logical_and(st[0] >= 0, st[1])

            def step(st):
                kj, _, carries = st
                k0 = pl.multiple_of(kj * bq, bq)
                alive, new_carries = None, []
                for a, (hd, qb, dob, carry) in enumerate(zip(heads, qbs, dobs, carries)):
                    mask, lb, lm = _attn_tile(qb, k_ref[pl.ds(k0, bq), hd], row, col, (kj - qi) * bq, scale)
                    w = jnp.where(mask, jnp.exp(lb + _tri_sum(lm, tri) + carry), 0.0)
                    g_ref[a, pl.ds(k0, bq), :] = w * _dot_nt(dob, v_ref[pl.ds(k0, bq), hd])
                    b_ref[a, pl.ds(k0, bq), :] = jnp.where(mask, jnp.exp(lb), 0.0)
                    dva_ref[pl.ds(k0, bq), hd] += _dot_tn(w.astype(_MXU), dob)
                    carry = carry + jnp.sum(lm, axis=1, keepdims=True)
                    new_carries.append(carry)
                    live = jnp.max(carry) > EXP_CUT
                    alive = live if alive is None else jnp.logical_or(alive, live)
                return kj - 1, alive, tuple(new_carries)

            st = lax.while_loop(cond, step, (qi, jnp.bool_(True), (jnp.zeros((bq, 1), F32),) * ATTN_HEADS))

            def back(kj, st2):
                k0 = pl.multiple_of(kj * bq, bq)
                out = []
                for a, (hd, qb, (before, dq)) in enumerate(zip(heads, qbs, st2)):
                    g = g_ref[a, pl.ds(k0, bq), :]
                    beta = b_ref[a, pl.ds(k0, bq), :]
                    dz = ((g * (1.0 - beta) - (before + _tri_sum(g, tri_lt)) * beta) * scale).astype(_MXU)
                    dka_ref[pl.ds(k0, bq), hd] += _dot_tn(dz, qb)
                    out.append((before + jnp.sum(g, axis=1, keepdims=True), dq + _dot(dz, k_ref[pl.ds(k0, bq), hd])))
                return tuple(out)

            st2 = lax.fori_loop(st[0] + 1, qi + 1, back, ((jnp.zeros((bq, 1), F32), jnp.zeros((bq, HEAD_DIM), F32)),) * ATTN_HEADS)
            for hd, (_, dq) in zip(heads, st2):
                dq_ref[pl.ds(q0, bq), hd] = dq.astype(dq_ref.dtype)
            return 0

        lax.fori_loop(0, nq, per_q, 0)
        dk_ref[...] = dka_ref[...].astype(dk_ref.dtype)
        dv_ref[...] = dva_ref[...].astype(dv_ref.dtype)
        _emit([dq_ref, dk_ref, dv_ref], dproj_ref, [(a * groups + group) * wide for a in range(3)], out_sems)

    hs = lambda off: pl.BlockSpec((t, wide), lambda h: (0, off + h))
    return _call(
        comm, body, grid=(groups,), in_specs=[hs(0), hs(groups), hs(2 * groups), hs(0), _ANY], out_specs=_ANY,
        out_shape=S(dproj.shape, dproj.dtype), input_output_aliases={4: 0},
        scratch_shapes=[pltpu.VMEM((t, wide), F32), pltpu.VMEM((t, wide), F32),
                        pltpu.VMEM((ATTN_HEADS, t, bq), F32), pltpu.VMEM((ATTN_HEADS, t, bq), F32)]
        + [pltpu.VMEM((t, wide), dproj.dtype)] * 3 + [pltpu.SemaphoreType.DMA((3,))],
        compiler_params=_cp("parallel"), name=name)(proj, proj, proj, do, dproj)


def _shift_down(cur, prev8, k):
    if k == 0:
        return cur
    row8 = lax.broadcasted_iota(jnp.int32, prev8.shape, 0)
    rc = pltpu.roll(cur, k, 0)
    top = jnp.where(row8 < k, pltpu.roll(prev8, k, 0), rc[0:8, :])
    return jnp.concatenate([top, rc[8:, :]], axis=0)


def _shift_up(cur, next8, k):
    if k == 0:
        return cur
    n = cur.shape[0]
    row8 = lax.broadcasted_iota(jnp.int32, next8.shape, 0)
    rc = pltpu.roll(cur, n - k, 0)
    bottom = jnp.where(row8 >= 8 - k, pltpu.roll(next8, 8 - k, 0), rc[n - 8:, :])
    return jnp.concatenate([rc[:n - 8, :], bottom], axis=0)


def _lru_conv(xl, prev8, cw, cb):
    xs = [_shift_down(xl, prev8, CONV_WIDTH - 1 - k) for k in range(CONV_WIDTH)]
    xc = xs[0] * cw[0:1, :]
    for k in range(1, CONV_WIDTH):
        xc = xc + xs[k] * cw[k:k + 1, :]
    return xs, xc + cb


def _lru_gates(xl, prev8, cw, cb, wr, br, wi, bi, ls):
    xs, xc = _lru_conv(xl, prev8, cw, cb)
    xcb = xc.astype(_MXU)
    r = jax.nn.sigmoid(_dot(xcb, wr) + br)
    i = jax.nn.sigmoid(_dot(xcb, wi) + bi)
    la = (LRU_C * r) * ls
    a = jnp.exp(la)
    mult = jnp.sqrt(-_expm1(2.0 * la))
    return xs, xc, r, i, a, mult


def _group_scan(a, b, reverse):
    n = a.shape[0]
    row = lax.broadcasted_iota(jnp.int32, a.shape, 0) % 8
    for d in (1, 2, 4):
        if reverse:
            m = row < 8 - d
            a_s, b_s = pltpu.roll(a, n - d, 0), pltpu.roll(b, n - d, 0)
        else:
            m = row >= d
            a_s, b_s = pltpu.roll(a, d, 0), pltpu.roll(b, d, 0)
        b = jnp.where(m, a * b_s + b, b)
        a = jnp.where(m, a * a_s, a)
    return a, b


def _lru_fwd(proj, col0, n_blocks, cw, cb, wr, br, wi, bi, lam, name, comm=None):
    t = proj.shape[0]
    tt = min(t, SEQ_TILE)
    nt = t // tt

    def body(xl_ref, gl_ref, cw_ref, cb_ref, wr_ref, br_ref, wi_ref, bi_ref, lam_ref, h_ref, y_ref, *kept):
        cwv, cbv, brv, biv = cw_ref[...], cb_ref[...], br_ref[...], bi_ref[...]
        wrv, wiv = wr_ref[...].astype(_MXU), wi_ref[...].astype(_MXU)
        ls = _log_sigmoid(lam_ref[...])

        def tile(ti, hin):
            t0 = pl.multiple_of(ti * tt, tt)
            p0 = pl.multiple_of(jnp.maximum(t0 - 8, 0), 8)
            prev8 = xl_ref[pl.ds(p0, 8), :] * (ti > 0).astype(F32)
            xl = xl_ref[pl.ds(t0, tt), :]
            _, xc, r, ig, a, mult = _lru_gates(xl, prev8, cwv, cbv, wrv, brv, wiv, biv, ls)
            for ref, val in zip(kept, (r, ig, a, mult)):
                ref[pl.ds(t0, tt), :] = val
            ga, gb = _group_scan(a, mult * (ig * xc), False)
            for g in range(tt // 8):
                hg = ga[8 * g:8 * g + 8, :] * hin + gb[8 * g:8 * g + 8, :]
                h_ref[pl.ds(t0 + 8 * g, 8), :] = hg
                hin = hg[7:8, :]
            y_ref[pl.ds(t0, tt), :] = h_ref[pl.ds(t0, tt), :] * _gelu(gl_ref[pl.ds(t0, tt), :])
            return hin

        lax.fori_loop(0, nt, tile, jnp.zeros((1, HEAD_DIM), F32))

    cs = lambda off: pl.BlockSpec((t, HEAD_DIM), lambda n: (0, off + n))
    vs = pl.BlockSpec((1, HEAD_DIM), lambda n: (0, n))
    ws = pl.BlockSpec((None, HEAD_DIM, HEAD_DIM), lambda n: (n, 0, 0))
    w = n_blocks * HEAD_DIM
    return _call(
        comm, body, grid=(n_blocks,),
        in_specs=[cs(col0), cs(col0 + n_blocks), pl.BlockSpec((CONV_WIDTH, HEAD_DIM), lambda n: (0, n)), vs, ws, vs, ws, vs, vs],
        out_specs=[cs(0)] * 6, out_shape=[S((t, w), F32)] * 6,
        compiler_params=_cp("parallel"), name=name)(proj, proj, cw, cb, wr, br, wi, bi, lam)


def _lru_bwd(proj, col0, n_blocks, h, kept, dyl, cw, cb, wr, wi, lam, name, comm=None):
    t = proj.shape[0]
    tt = min(t, SEQ_TILE)
    nt = t // tt

    def body(xl_ref, gl_ref, h_ref, r_ref, i_ref, a_ref, m_ref, dy_ref, cw_ref, cb_ref, wr_ref, wi_ref, lam_ref,
             dproj_ref, dcw_ref, dcb_ref, dwr_ref, dbr_ref, dwi_ref, dbi_ref, dlam_ref, g_ref, dxl_ref, dgl_ref, out_sems):
        block = pl.program_id(0)
        cwv, cbv = cw_ref[...], cb_ref[...]
        wrv, wiv = wr_ref[...].astype(_MXU), wi_ref[...].astype(_MXU)
        lamv = lam_ref[...]
        ls = _log_sigmoid(lamv)
        for ref in (dcw_ref, dcb_ref, dwr_ref, dbr_ref, dwi_ref, dbi_ref, dlam_ref):
            ref[...] = jnp.zeros_like(ref)

        def tile(s, carry):
            e_in, dxc_next8 = carry
            ti = nt - 1 - s
            t0 = pl.multiple_of(ti * tt, tt)
            p0 = pl.multiple_of(jnp.maximum(t0 - 8, 0), 8)
            first = (ti > 0).astype(F32)
            xl = xl_ref[pl.ds(t0, tt), :]
            xs, xc = _lru_conv(xl, xl_ref[pl.ds(p0, 8), :] * first, cwv, cbv)
            r, ig, a, mult = (ref[pl.ds(t0, tt), :] for ref in (r_ref, i_ref, a_ref, m_ref))
            hv = h_ref[pl.ds(t0, tt), :]
            h_before = _shift_down(hv, h_ref[pl.ds(p0, 8), :] * first, 1)
            glv = gl_ref[pl.ds(t0, tt), :]
            dyv = dy_ref[pl.ds(t0, tt), :]
            dgl_ref[pl.ds(t0, tt), :] = (dyv * hv * _gelu_grad(glv)).astype(dgl_ref.dtype)
            dh = dyv * _gelu(glv)
            row = lax.broadcasted_iota(jnp.int32, a.shape, 0)
            coef = jnp.where(row == tt - 1, 1.0, pltpu.roll(a, tt - 1, 0))
            ga, gb = _group_scan(coef, dh, True)
            gin = e_in
            for g in reversed(range(tt // 8)):
                gg = ga[8 * g:8 * g + 8, :] * gin + gb[8 * g:8 * g + 8, :]
                g_ref[8 * g:8 * g + 8, :] = gg
                gin = gg[0:1, :]
            gv = g_ref[...]
            e_out = a[0:1, :] * gv[0:1, :]
            ix = ig * xc
            dla = (gv * h_before) * a - (gv * ix) * (a * a / mult)
            dlam_ref[...] += jnp.sum(dla * (LRU_C * r), axis=0, keepdims=True)
            dpr = (dla * (LRU_C * ls)) * (r * (1.0 - r))
            dpi = (gv * mult * xc) * (ig * (1.0 - ig))
            dbr_ref[...] += jnp.sum(dpr, axis=0, keepdims=True)
            dbi_ref[...] += jnp.sum(dpi, axis=0, keepdims=True)
            xcb, dprb, dpib = xc.astype(_MXU), dpr.astype(_MXU), dpi.astype(_MXU)
            dwr_ref[...] += _dot_tn(xcb, dprb)
            dwi_ref[...] += _dot_tn(xcb, dpib)
            dxc = gv * mult * ig + _dot_nt(dprb, wrv) + _dot_nt(dpib, wiv)
            dcb_ref[...] += jnp.sum(dxc, axis=0, keepdims=True)
            dxl = None
            for k in range(CONV_WIDTH):
                dcw_ref[k:k + 1, :] += jnp.sum(dxc * xs[k], axis=0, keepdims=True)
                term = _shift_up(dxc, dxc_next8, CONV_WIDTH - 1 - k) * cwv[k:k + 1, :]
                dxl = term if dxl is None else dxl + term
            dxl_ref[pl.ds(t0, tt), :] = dxl.astype(dxl_ref.dtype)
            return e_out, dxc[0:8, :]

        lax.fori_loop(0, nt, tile, (jnp.zeros((1, HEAD_DIM), F32), jnp.zeros((8, HEAD_DIM), F32)))
        dlam_ref[...] = dlam_ref[...] * (1.0 - jax.nn.sigmoid(lamv))
        _emit([dxl_ref, dgl_ref], dproj_ref, [(col0 + block) * HEAD_DIM, (col0 + n_blocks + block) * HEAD_DIM], out_sems)

    cs = lambda off: pl.BlockSpec((t, HEAD_DIM), lambda n: (0, off + n))
    vs = pl.BlockSpec((1, HEAD_DIM), lambda n: (0, n))
    ws = pl.BlockSpec((None, HEAD_DIM, HEAD_DIM), lambda n: (n, 0, 0))
    cws = pl.BlockSpec((CONV_WIDTH, HEAD_DIM), lambda n: (0, n))
    w = n_blocks * HEAD_DIM
    vec = S((1, w), F32)
    mat = S((n_blocks, HEAD_DIM, HEAD_DIM), F32)
    return _call(
        comm, body, grid=(n_blocks,),
        in_specs=[cs(col0), cs(col0 + n_blocks)] + [cs(0)] * 6 + [cws, vs, ws, ws, vs],
        out_specs=[_ANY, cws, vs, ws, vs, ws, vs, vs],
        out_shape=[S(proj.shape, _MXU), S((CONV_WIDTH, w), F32), vec, mat, vec, mat, vec, vec],
        scratch_shapes=[pltpu.VMEM((tt, HEAD_DIM), F32), pltpu.VMEM((t, HEAD_DIM), _MXU), pltpu.VMEM((t, HEAD_DIM), _MXU),
                        pltpu.SemaphoreType.DMA((2,))],
        compiler_params=_cp("parallel"), name=name)(proj, proj, h, *kept, dyl, cw, cb, wr, wi, lam)


class _NoExchange:
    grad_dtype = F32

    def __init__(self, weights):
        self.weights, self.grads, self.packs = weights, {}, {}

    def weight(self, name):
        return self.weights[name]

    def in_proj(self, x, gain, bm):
        hn = _rms_fwd(x, gain, "rms1")
        return [hn, *_mm_nn(hn, self.weights["w_in"], bm=bm, bn=self.weights["w_in"].shape[2], name="in_proj", also=_MXU)]

    def conv_w(self):
        return self.weights["conv_w"]

    def carrier(self, call):
        return None

    def harvest(self, car):
        pass

    def alone(self, call):
        pass


def _local_step(x, target, norms, ex, cb, wr, br, wi, bi, lam, ga, gl):
    g_pre_mix, g_post_mix, g_pre_ffn, g_post_ffn = norms
    t, d = x.shape
    bm = min(t, MM_ROWS)
    bt = min(t, DW_TOKENS)

    def run(fn, name, *args, **kw):
        car = ex.carrier(name)
        out = fn(*args, name=name, comm=car, **kw)
        ex.harvest(car)
        return out

    hn1, proj, proj_mx = ex.in_proj(x, g_pre_mix, bm)
    win3, cw = ex.weight("w_in"), ex.conv_w()
    c = win3.shape[0]
    o = run(_attn_fwd, "attn_fwd", proj_mx, (proj.shape[1] - d) // 3 // HEAD_DIM)
    mix = 2 * o.shape[1]
    n_heads = n_blocks = o.shape[1] // HEAD_DIM
    h, yl, *kept = run(_lru_fwd, "lru_fwd", proj, 3 * n_heads, n_blocks, cw, cb, wr, br, wi, bi, lam)
    y = run(_outnorm_fwd, "outnorm_fwd", o, yl, ga, gl)
    wout = ex.weight("w_out")
    mixo = run(_mm_nn, "out_proj", y, wout[None], bm=bm, bn=d)
    x2, hn2 = run(_mid_fwd, "mid_fwd", x, mixo, g_post_mix, g_pre_ffn)
    ex.alone("gather_w_up_last")
    wg3, wu3 = ex.weight("w_ffn_gate"), ex.weight("w_ffn_up")
    act_dgate, act_dup, act = run(_swiglu_fwd, "ffn_gate_up", hn2, wg3, wu3, bm=bm)
    ex.alone("gather_w_down")
    wd = ex.weight("w_ffn_down")
    ff = wd.shape[0]
    f = _mm_nn(act, wd[None], bm=bm, bn=d // 2, name="ffn_down")
    loss_cols, dout, df, dg_post_ffn = _final(f, x2, target, g_post_ffn, "final")

    dgate, dup = _swiglu_bwd(df, wd, act_dgate, act_dup, bm=min(t, 2 * MM_ROWS), bo=ff // 4, name="ffn_down_bwd")
    ex.grads["w_ffn_down"] = _mm_tn(act, df, 1, bm=bt, bk=DW_ROWS, out_dtype=ex.grad_dtype, name="ffn_down_dw").reshape(c, ff // c, d)
    ex.grads["w_ffn_gate"] = run(_mm_tn, "ffn_gate_dw", hn2, dgate, c, bm=bt, bk=d // 2, out_dtype=ex.grad_dtype)
    ex.grads["w_ffn_up"] = run(_mm_tn, "ffn_up_dw", hn2, dup, c, bm=bt, bk=d // 2, out_dtype=ex.grad_dtype)
    dhn2_g = run(_mm_nt, "ffn_gate_dx", dgate, wg3, bm=bm, bo=d // 2, out_dtype=F32)
    dhn2_u = run(_mm_nt, "ffn_up_dx", dup, wu3, bm=bm, bo=d // 2, out_dtype=F32)
    dx2, dmix, dg_pre_ffn, dg_post_mix = run(_mid_bwd, "mid_bwd", dhn2_g, dhn2_u, dout, x2, mixo, g_pre_ffn, g_post_mix)
    dy = run(_mm_nt, "out_proj_dx", dmix, wout[None], bm=bm, bo=mix, out_dtype=F32)
    ex.grads["w_out"] = _mm_tn(y, dmix, 1, bm=bt, bk=mix // 4, out_dtype=ex.grad_dtype, name="out_proj_dw").reshape(c, mix // c, d)
    do, dyl, dga, dgl_norm = run(_outnorm_bwd, "outnorm_bwd", dy, o, yl, ga, gl)
    dproj, dcw, dcb, dwr, dbr, dwi, dbi, dlam = run(_lru_bwd, "lru_bwd", proj, 3 * n_heads, n_blocks, h, kept, dyl, cw, cb, wr, wi, lam)
    small = dict(post_mix_norm=dg_post_mix, pre_ffn_norm=dg_pre_ffn, post_ffn_norm=dg_post_ffn, conv_w=dcw, conv_b=dcb,
                 w_rgate=dwr, b_rgate=dbr, w_igate=dwi, b_igate=dbi, lru_lambda=dlam, attn_out_norm=dga, lru_out_norm=dgl_norm)
    ex.packs["early"] = _pack([small[n] for n in _SMALL_EARLY])
    dproj = run(_attn_bwd, "attn_bwd", proj_mx, do, dproj, n_heads)
    ex.grads["w_in"] = _mm_tn(hn1, dproj, c, bm=bt, bk=d // 2, out_dtype=ex.grad_dtype, name="in_proj_dw")
    ex.alone("grads_w_in_swap")
    dhn1 = run(_mm_nt, "in_proj_dx", dproj, win3, bm=bm, bo=d // 2, out_dtype=F32)
    grad_x, small["pre_mix_norm"] = run(_first_bwd, "first_bwd", dhn1, dx2, x, g_pre_mix)
    ex.packs["late"] = _pack([small["pre_mix_norm"], (0.5 / d) * jnp.sum(loss_cols, keepdims=True)])
    return loss_cols, grad_x, small


def _into_slot(wsh, slot, dtype, name):
    rows, n = wsh.shape
    rb = _row_block(rows, 256) if rows % 8 == 0 else rows

    def body(s_ref, w_ref, o_ref):
        o_ref[...] = w_ref[...].astype(o_ref.dtype)

    return pl.pallas_call(
        body,
        grid_spec=pltpu.PrefetchScalarGridSpec(
            num_scalar_prefetch=1, grid=(rows // rb,),
            in_specs=[pl.BlockSpec((rb, n), lambda i, s_ref: (i, 0))],
            out_specs=pl.BlockSpec((None, rb, n), lambda i, s_ref: (s_ref[0], i, 0))),
        out_shape=S((4, rows, n), dtype), compiler_params=_cp("parallel"), name=name)(slot, wsh)


class _Exchange:
    SCHEDULE = {
        "in_proj": [("stream", "w_in"), ("ici", "conv_w")],
        "attn_fwd": [("ici", "w_ffn_gate"), ("ici", "w_ffn_up", 0)],
        "lru_fwd": [("d2d", "w_ffn_gate"), ("d2d", "w_ffn_up", 0), ("ici", "w_out"), ("ici", "w_ffn_up", 1), ("ici", "w_ffn_up", 2)],
        "outnorm_fwd": [("d2d", "w_out"), ("d2d", "w_ffn_up", 1), ("d2d", "w_ffn_up", 2)],
        "out_proj": [("ici", "w_ffn_up", p) for p in (3, 4, 5)],
        "mid_fwd": [("d2d", "w_ffn_up", p) for p in (3, 4, 5)] + [("ici", "w_ffn_up", 6), ("ici", "w_ffn_up", 7)],
        "gather_w_up_last": [("d2d", "w_ffn_up", 6), ("d2d", "w_ffn_up", 7)],
        "ffn_gate_up": [("ici", "w_ffn_down")],
        "gather_w_down": [("d2d", "w_ffn_down")],
        "ffn_gate_dw": [("swap", "w_ffn_down")],
        "ffn_up_dw": [("scatter", "w_ffn_down", 0), ("scatter", "w_ffn_down", 1), ("scatter", "w_ffn_down", 2), ("swap", "w_ffn_gate")],
        "ffn_gate_dx": [("scatter", "w_ffn_down", 3), ("scatter", "w_ffn_gate", 0), ("scatter", "w_ffn_gate", 1), ("swap", "w_ffn_up")],
        "ffn_up_dx": [("share", "w_ffn_down"), ("scatter", "w_ffn_gate", 2), ("scatter", "w_ffn_gate", 3), ("scatter", "w_ffn_up", 0)],
        "mid_bwd": [("share", "w_ffn_gate"), ("scatter", "w_ffn_up", 1), ("scatter", "w_ffn_up", 2)],
        "out_proj_dx": [("scatter", "w_ffn_up", 3)],
        "outnorm_bwd": [("share", "w_ffn_up"), ("swap", "w_out")],
        "lru_bwd": [("scatter", "w_out")],
        "attn_bwd": [("share", "w_out"), ("spread", "early")],
        "grads_w_in_swap": [("swap", "w_in")],
        "in_proj_dx": [("scatter", "w_in")],
        "grads_w_in_share": [("share", "w_in"), ("spread", "late")],
    }
    PIECES = 4
    GATHER_PIECES = 8
    grad_dtype = BF16

    def __init__(self, slots, place):
        self.buf, self.place = dict(slots), place
        self.grads, self.packs, self.swapped, self.part, self.scattered, self.full, self.spreaded = {}, {}, {}, {}, {}, {}, {}

    def weight(self, name):
        b = self.buf[name]
        return b.reshape(-1, b.shape[2]) if name in ("w_out", "w_ffn_down") else b

    def in_proj(self, x, gain, bm):
        car = self.carrier("in_proj")
        out = _in_proj_streamed(x, gain, car, car.streamed, self.place, bm=bm, name="in_proj")
        self.harvest(car)
        return out

    def conv_w(self):
        return jnp.transpose(self.buf["conv_w"], (1, 0, 2)).reshape(CONV_WIDTH, -1)

    def carrier(self, call):
        if call not in self.SCHEDULE:
            return None
        car = _Carrier()
        car.todo, slot = [], {}
        for kind, name, *piece in self.SCHEDULE[call]:
            if kind in ("ici", "d2d", "stream"):
                if name not in slot:
                    slot[name] = car.inplace(self.buf[name])
                    car.todo.append((self.buf, name, slot[name]))
            if kind == "stream":
                car.streamed = slot[name]
            elif kind in ("ici", "d2d"):
                size = self.buf[name].shape[1] // 2 // self.GATHER_PIECES
                rows = (piece[0] * size, size) if piece else None
                if kind == "ici":
                    car.gather_ici(slot[name], rows, split=name != "conv_w")
                else:
                    car.gather_d2d(slot[name], rows)
            elif kind == "swap":
                g = self.grads[name]
                o = car.fresh((4, g.shape[1] // 2, g.shape[2]), g.dtype)
                car.swap(car.read(g), o)
                car.todo.append((self.swapped, name, o))
            elif kind == "scatter":
                if name not in self.part:
                    self.part[name] = _add_own_half(self.grads[name], self.swapped[name], self.place[1:], "grads_add_" + name)
                p = self.part[name]
                key = ("scatter", name)
                if key not in slot:
                    slot[key] = (car.read(p), car.inplace(self.scattered[name]) if name in self.scattered else car.fresh(p.shape, p.dtype))
                    car.todo.append((self.scattered, name, slot[key][1]))
                size = p.shape[1] // self.PIECES
                car.scatter(*slot[key], (piece[0] * size, size) if piece else None)
            elif kind == "share":
                o = car.inplace(_sum_chips(self.part[name], self.scattered[name], self.place, "grads_sum_" + name))
                car.share(o)
                car.todo.append((self.full, name, o))
            else:
                o = car.fresh((8,) + self.packs[name].shape, F32)
                car.spread(car.read(self.packs[name]), o)
                car.todo.append((self.spreaded, name, o))
        return car

    def harvest(self, car):
        for state, name, o in (car.todo if car is not None else []):
            state[name] = car.results[o]

    def alone(self, call):
        car = self.carrier(call)
        car.run_alone(call)
        self.harvest(car)

    def small_sum(self, key):
        return _sum_devices(self.packs[key], self.spreaded[key], 2 * self.place[0:1] + self.place[1:], "grads_small_sum_" + key)


def _row_block(rows, cap):
    return max(b for b in range(8, cap + 1, 8) if rows % b == 0)


def _add_own_half(g, recv, core, name):
    _, rows, n = g.shape
    half = rows // 2
    rb = _row_block(half, 512)
    nb = half // rb

    def body(c_ref, g_ref, r_ref, o_ref):
        o_ref[...] = (g_ref[...].astype(F32) + r_ref[...].astype(F32)).astype(o_ref.dtype)

    return pl.pallas_call(
        body,
        grid_spec=pltpu.PrefetchScalarGridSpec(
            num_scalar_prefetch=1, grid=(4, nb),
            in_specs=[pl.BlockSpec((None, rb, n), lambda k, i, c_ref: (k, c_ref[0] * nb + i, 0)),
                      pl.BlockSpec((None, rb, n), lambda k, i, c_ref: (k, i, 0))],
            out_specs=pl.BlockSpec((None, rb, n), lambda k, i, c_ref: (k, i, 0))),
        out_shape=S((4, half, n), BF16), compiler_params=_cp("parallel", "parallel"), name=name)(core, g, recv)


def _sum_chips(part, recv, place, name):
    _, rows, n = part.shape
    rb = _row_block(rows, 64)
    nb = rows // rb

    def body(p_ref, own_ref, r0, r1, r2, r3, o_ref):
        own = own_ref[...].astype(F32)
        terms = [jnp.where(p_ref[0] == k, own, r[...].astype(F32)) for k, r in enumerate((r0, r1, r2, r3))]
        o_ref[...] = ((terms[0] + terms[1]) + terms[2]) + terms[3]

    def slot(k):
        return pl.BlockSpec((None, rb, n), lambda i, p_ref: (jnp.where(p_ref[0] == k, (k + 1) % 4, k), i, 0))

    return pl.pallas_call(
        body,
        grid_spec=pltpu.PrefetchScalarGridSpec(
            num_scalar_prefetch=1, grid=(nb,),
            in_specs=[pl.BlockSpec((None, rb, n), lambda i, p_ref: (p_ref[0], i, 0))] + [slot(k) for k in range(4)],
            out_specs=pl.BlockSpec((rb, n), lambda i, p_ref: (p_ref[1] * nb + i, 0))),
        out_shape=S((2 * rows, n), F32), compiler_params=_cp("parallel"), name=name)(place, part, recv, recv, recv, recv)


def _sum_devices(own, spread, me, name):
    rows = own.shape[0]

    def body(me_ref, own_ref, *refs):
        acc = None
        for k, r in enumerate(refs[:8]):
            term = jnp.where(me_ref[0] == k, own_ref[...], r[...])
            acc = term if acc is None else acc + term
        refs[8][...] = acc

    def slot(k):
        return pl.BlockSpec((None, rows, 128), lambda i, me_ref: (jnp.where(me_ref[0] == k, (k + 1) % 8, k), 0, 0))

    whole = pl.BlockSpec((rows, 128), lambda i, me_ref: (0, 0))
    return pl.pallas_call(
        body,
        grid_spec=pltpu.PrefetchScalarGridSpec(num_scalar_prefetch=1, grid=(1,), in_specs=[whole] + [slot(k) for k in range(8)],
                                               out_specs=whole),
        out_shape=S((rows, 128), F32), compiler_params=_cp("arbitrary"), name=name)(me, own, *[spread] * 8)


def _adamw(w, g, m, v, name, regive=False):
    rows, n = w.shape
    rb = rows if rows * n * 4 <= (1 << 21) else _row_block(rows, 256)
    c1 = 1.0 - ADAM_B1 ** ADAM_STEP
    c2 = 1.0 - ADAM_B2 ** ADAM_STEP

    def body(w_ref, g_ref, m_ref, v_ref, d_ref, nm_ref, nv_ref, *again):
        gv = g_ref[...]
        for ref in again:
            ref[...] = gv
        nm = ADAM_B1 * m_ref[...] + (1.0 - ADAM_B1) * gv
        nv = ADAM_B2 * v_ref[...] + (1.0 - ADAM_B2) * (gv * gv)
        nm_ref[...] = nm
        nv_ref[...] = nv
        d_ref[...] = -ADAM_LR * ((nm / c1) / (jnp.sqrt(nv / c2) + ADAM_EPS) + ADAM_WD * w_ref[...])

    bs = pl.BlockSpec((rb, n), lambda i: (i, 0))
    n_out = 4 if regive else 3
    return pl.pallas_call(body, grid=(rows // rb,), in_specs=[bs] * 4, out_specs=[bs] * n_out, out_shape=[S((rows, n), F32)] * n_out,
                          compiler_params=_cp("parallel"), name=name)(w, g, m, v)


_BIG = ("w_in", "w_out", "w_ffn_gate", "w_ffn_up", "w_ffn_down")
_SMALL = ("pre_mix_norm", "post_mix_norm", "pre_ffn_norm", "post_ffn_norm", "conv_w", "conv_b", "w_rgate", "b_rgate",
          "w_igate", "b_igate", "lru_lambda", "attn_out_norm", "lru_out_norm")
_SMALL_EARLY = _SMALL[1:]
_WEIGHTS = ("pre_mix_norm", "post_mix_norm", "pre_ffn_norm", "post_ffn_norm", "w_in", "conv_w", "conv_b", "w_rgate", "b_rgate",
            "w_igate", "b_igate", "lru_lambda", "attn_out_norm", "lru_out_norm", "w_out", "w_ffn_gate", "w_ffn_up", "w_ffn_down")


def _pack(arrays):
    flat = []
    for a in arrays:
        f = a.reshape(-1)
        flat.append(jnp.pad(f, (0, (-f.shape[0]) % 1024)))
    return jnp.concatenate(flat).reshape(-1, 128)


def _unpack(packed, shapes):
    out, pos = [], 0
    flat = packed.reshape(-1)
    for s in shapes:
        size = math.prod(s)
        out.append(flat[pos:pos + size].reshape(s))
        pos += size + (-size) % 1024
    return out


def kernel(x, pre_mix_norm, post_mix_norm, pre_ffn_norm, post_ffn_norm, w_in, conv_w, conv_b, w_rgate, b_rgate, w_igate, b_igate, lru_lambda, attn_out_norm, lru_out_norm, w_out, w_ffn_gate, w_ffn_up, w_ffn_down, loss_target, m_pre_mix_norm, m_post_mix_norm, m_pre_ffn_norm, m_post_ffn_norm, m_w_in, m_conv_w, m_conv_b, m_w_rgate, m_b_rgate, m_w_igate, m_b_igate, m_lru_lambda, m_attn_out_norm, m_lru_out_norm, m_w_out, m_w_ffn_gate, m_w_ffn_up, m_w_ffn_down, v_pre_mix_norm, v_post_mix_norm, v_pre_ffn_norm, v_post_ffn_norm, v_w_in, v_conv_w, v_conv_b, v_w_rgate, v_b_rgate, v_w_igate, v_b_igate, v_lru_lambda, v_attn_out_norm, v_lru_out_norm, v_w_out, v_w_ffn_gate, v_w_ffn_up, v_w_ffn_down):
    given = dict(locals())
    w = {n: given[n][0] for n in _WEIGHTS}
    m = {n: given["m_" + n][0] for n in _WEIGHTS}
    v = {n: given["v_" + n][0] for n in _WEIGHTS}
    xs, target = x[0], loss_target[0]
    d = xs.shape[1]
    chip = (2 * lax.axis_index("x") + lax.axis_index("y")).astype(jnp.int32)
    place = jnp.stack([chip, lax.axis_index("c").astype(jnp.int32)])

    slots = {n: _into_slot(w[n], place[0:1], _MXU, "slot_" + n) for n in _BIG}
    slots["conv_w"] = _into_slot(w["conv_w"], place[0:1], F32, "slot_conv_w")
    ex = _Exchange(slots, place)
    row = lambda a: a.reshape(1, -1)
    norms = tuple(row(w[n]) for n in ("pre_mix_norm", "post_mix_norm", "pre_ffn_norm", "post_ffn_norm"))

    loss_cols, grad_x, small = _local_step(
        xs, target, norms, ex, row(w["conv_b"]), w["w_rgate"], row(w["b_rgate"]),
        w["w_igate"], row(w["b_igate"]), row(w["lru_lambda"]), row(w["attn_out_norm"]), row(w["lru_out_norm"]))


    ex.alone("grads_w_in_share")
    reduced = {n: ex.full[n] for n in _BIG}
    early = _unpack(ex.small_sum("early"), [small[n].shape for n in _SMALL_EARLY])
    late = _unpack(ex.small_sum("late"), [small["pre_mix_norm"].shape, (1, 1)])
    loss = late[1][0, 0]
    for n, g in zip(_SMALL_EARLY + ("pre_mix_norm",), early + late[:1]):
        reduced[n] = g.reshape(w[n].shape) if n != "conv_w" else lax.dynamic_slice_in_dim(g, chip * w[n].shape[1], w[n].shape[1], axis=1)

    delta, new_m, new_v = {}, {}, {}
    for n in _BIG:
        delta[n], new_m[n], new_v[n], reduced[n] = _adamw(w[n], reduced[n], m[n], v[n], "adamw_" + n, regive=True)
    shapes = [w[n].shape for n in _SMALL]
    packed = _adamw(*[_pack([src[n] for n in _SMALL]) for src in (w, reduced, m, v)], "adamw_small")
    for out, p in zip((delta, new_m, new_v), packed):
        out.update(zip(_SMALL, _unpack(p, shapes)))

    lead = lambda a: a[None]
    return (loss, lead(grad_x), *[lead(reduced[n]) for n in _WEIGHTS], *[lead(delta[n]) for n in _WEIGHTS],
            *[lead(new_m[n]) for n in _WEIGHTS], *[lead(new_v[n]) for n in _WEIGHTS])
```

```python
import functools
import math

import jax
import jax.numpy as jnp
from jax import lax
from jax.experimental import pallas as pl
from jax.experimental.pallas import tpu as pltpu

F32 = jnp.float32
BF16 = jnp.bfloat16
_MXU = BF16
S = jax.ShapeDtypeStruct

RMS_EPS = 1e-6
HEAD_DIM = 128
CONV_WIDTH = 4
LRU_C = 8.0
ADAM_LR, ADAM_B1, ADAM_B2, ADAM_EPS, ADAM_WD, ADAM_STEP = 0.001, 0.9, 0.999, 1e-08, 0.01, 10
EXP_CUT = -105.0
VMEM_LIMIT = 60 * 1024 * 1024
ROW_TILE = 256
SEQ_TILE = 256
ATTN_BLOCK = 256
ATTN_HEADS = 2
MM_ROWS = 512
DW_TOKENS = 2048
DW_ROWS = 512
MESH = pl.DeviceIdType.MESH


def _cp(*sem):
    return pltpu.CompilerParams(dimension_semantics=sem, vmem_limit_bytes=VMEM_LIMIT)


def _dot(a, b):
    return jnp.dot(a, b, preferred_element_type=F32)


def _dot_nt(a, b):
    return lax.dot_general(a, b, (((1,), (1,)), ((), ())), preferred_element_type=F32)


def _dot_tn(a, b):
    return lax.dot_general(a, b, (((0,), (0,)), ((), ())), preferred_element_type=F32)


def _rstd(v):
    return lax.rsqrt(jnp.mean(v * v, axis=-1, keepdims=True) + RMS_EPS)


def _rms_bwd(dn, vh, r, gain):
    dvh = dn * gain
    dv = r * (dvh - vh * jnp.mean(dvh * vh, axis=-1, keepdims=True))
    return dv, jnp.sum(dn * vh, axis=0, keepdims=True)


def _log_sigmoid(z):
    return jnp.minimum(z, 0.0) - jnp.log(1.0 + jnp.exp(-jnp.abs(z)))


def _expm1(v):
    small = v * (1.0 + v * (0.5 + v * (1.0 / 6.0 + v * (1.0 / 24.0 + v * (1.0 / 120.0)))))
    return jnp.where(jnp.abs(v) < 0.04, small, jnp.exp(v) - 1.0)


_GELU_C = math.sqrt(2.0 / math.pi)


def _gelu(v):
    return 0.5 * v * (1.0 + jnp.tanh(_GELU_C * (v + 0.044715 * v * v * v)))


def _gelu_grad(v):
    th = jnp.tanh(_GELU_C * (v + 0.044715 * v * v * v))
    return 0.5 * (1.0 + th) + 0.5 * v * (1.0 - th * th) * _GELU_C * (1.0 + 3.0 * 0.044715 * v * v)


def _row_spec(tm, d):
    return pl.BlockSpec((tm, d), lambda i: (i, 0))


def _vec_spec(d):
    return pl.BlockSpec((1, d), lambda i: (0, 0))


_ANY = pl.BlockSpec(memory_space=pl.ANY)


def _place():
    x, y, c = lax.axis_index("x"), lax.axis_index("y"), lax.axis_index("c")
    return x, y, c, [(1 - x, y), (x, 1 - y), (1 - x, 1 - y)]


def _remote(src, dst, send_sem, recv_sem, to):
    return pltpu.make_async_remote_copy(src_ref=src, dst_ref=dst, send_sem=send_sem, recv_sem=recv_sem,
                                        device_id=to, device_id_type=MESH)


class _Carrier:
    def __init__(self):
        self.inputs, self.out_shapes, self.aliases, self.ops, self.n_sems, self.results = [], [], {}, [], 0, None

    def inplace(self, arr):
        self.aliases[len(self.inputs)] = len(self.out_shapes)
        self.inputs.append(arr)
        self.out_shapes.append(S(arr.shape, arr.dtype))
        return len(self.out_shapes) - 1

    def read(self, arr):
        self.inputs.append(arr)
        return len(self.inputs) - 1

    def fresh(self, shape, dtype):
        self.out_shapes.append(S(shape, dtype))
        return len(self.out_shapes) - 1

    def _add(self, n_sems, copies):
        base = self.n_sems
        self.n_sems += n_sems

        def start(ins, outs, send, recv):
            for k, (src, dst, _, to) in enumerate(copies(ins, outs)):
                _remote(src, dst, send.at[base + k], recv.at[base + k], to).start()

        def finish(ins, outs, send, recv):
            for k, (src, _, land, to) in enumerate(copies(ins, outs)):
                _remote(src, land, send.at[base + k], recv.at[base + k], to).wait()

        self.ops.append((start, finish))

    def gather_ici(self, o, rows=None, split=True):
        half = self.out_shapes[o].shape[1] // 2
        lo, size = rows or (0, half)

        def copies(ins, outs):
            x, y, c, chips = _place()
            part = (lambda ref: ref.at[pl.ds(c * half + lo, size)]) if split else (lambda ref: ref)
            mine = part(outs[o].at[2 * x + y])
            return [(mine, mine, part(outs[o].at[2 * px + py]), (px, py, c)) for px, py in chips]

        self._add(3, copies)

    def gather_d2d(self, o, rows=None):
        half = self.out_shapes[o].shape[1] // 2
        lo, size = rows or (0, half)

        def copies(ins, outs):
            x, y, c, chips = _place()
            at = lambda k, cc: outs[o].at[k].at[pl.ds(cc * half + lo, size)]
            return [(at(2 * px + py, c), at(2 * px + py, c), at(2 * px + py, 1 - c), (x, y, 1 - c)) for px, py in chips]

        self._add(3, copies)

    def swap(self, i, o):
        half = self.inputs[i].shape[1] // 2

        def copies(ins, outs):
            x, y, c, _ = _place()
            return [(ins[i].at[:, pl.ds((1 - c) * half, half)], outs[o], outs[o], (x, y, 1 - c))]

        self._add(1, copies)

    def scatter(self, i, o, rows=None):
        lo, size = rows or (0, self.inputs[i].shape[1])

        def copies(ins, outs):
            x, y, c, chips = _place()
            cut = lambda ref: ref.at[pl.ds(lo, size)]
            return [(cut(ins[i].at[2 * px + py]), cut(outs[o].at[2 * x + y]), cut(outs[o].at[2 * px + py]), (px, py, c)) for px, py in chips]

        self._add(3, copies)

    def share(self, o):
        r = self.out_shapes[o].shape[0] // 2

        def copies(ins, outs):
            x, y, c, _ = _place()
            mine = outs[o].at[pl.ds(c * r, r)]
            return [(mine, mine, outs[o].at[pl.ds((1 - c) * r, r)], (x, y, 1 - c))]

        self._add(1, copies)

    def spread(self, i, o):
        def copies(ins, outs):
            x, y, c, _ = _place()
            me = 4 * x + 2 * y + c
            out = []
            for d in range(1, 8):
                to, frm = (me + d) % 8, (me + 8 - d) % 8
                out.append((ins[i], outs[o].at[me], outs[o].at[frm], (to // 4, (to // 2) % 2, to % 2)))
            return out

        self._add(7, copies)

    def _pallas(self, body, n_in, n_out, scratch, **kw):
        k_in, k_out = len(self.inputs), len(self.out_shapes)
        grid = kw.get("grid", ())

        def wrapped(*refs):
            ins, cins = refs[:n_in], refs[n_in:n_in + k_in]
            outs = refs[n_in + k_in:n_in + k_in + n_out]
            couts = refs[n_in + k_in + n_out:n_in + k_in + n_out + k_out]
            own = refs[n_in + k_in + n_out + k_out:]
            send, recv = own[len(scratch):]
            ids = [pl.program_id(a) for a in range(len(grid))]
            first = functools.reduce(jnp.logical_and, [a == 0 for a in ids], True)
            last = functools.reduce(jnp.logical_and, [a == g - 1 for a, g in zip(ids, grid)], True)

            def go(stage):
                for op in self.ops:
                    op[stage](cins, couts, send, recv)

            if grid:
                pl.when(first)(lambda: go(0))
                body(*ins, *outs, *own[:len(scratch)])
                pl.when(last)(lambda: go(1))
            else:
                go(0)
                go(1)

        sem = pltpu.SemaphoreType.DMA((self.n_sems,))
        return pl.pallas_call(
            wrapped, in_specs=list(kw.get("in_specs", [])) + [_ANY] * k_in, out_specs=list(kw.get("out_specs", [])) + [_ANY] * k_out,
            out_shape=list(kw.get("out_shape", [])) + self.out_shapes, scratch_shapes=list(scratch) + [sem, sem],
            input_output_aliases={**kw.get("aliases", {}), **{n_in + i: n_out + o for i, o in self.aliases.items()}}, name=kw["name"],
            **({"grid": grid, "compiler_params": _cp(*["arbitrary"] * len(grid))} if grid else {}))

    def run(self, body, kw, *args):
        single = not isinstance(kw["out_shape"], (list, tuple))
        out_shape = [kw["out_shape"]] if single else list(kw["out_shape"])
        out_specs = [kw["out_specs"]] if single else list(kw["out_specs"])
        res = self._pallas(body, len(args), len(out_shape), kw.get("scratch_shapes", []), grid=kw["grid"], in_specs=kw["in_specs"],
                           out_specs=out_specs, out_shape=out_shape, name=kw["name"],
                           aliases=kw.get("input_output_aliases", {}))(*args, *self.inputs)
        self.results = list(res[len(out_shape):])
        return res[0] if single else list(res[:len(out_shape)])

    def run_alone(self, name):
        self.results = list(self._pallas(None, 0, 0, [], name=name)(*self.inputs))


def _call(comm, body, **kw):
    if comm is None:
        return pl.pallas_call(body, **kw)
    return functools.partial(comm.run, body, kw)


def _in_proj_streamed(x, gain, car, o_w, place, *, bm, name):
    m, k = x.shape
    n = car.out_shapes[o_w].shape[2]
    ni, half = m // bm, k // 2
    k_in, k_out = len(car.inputs), len(car.out_shapes)
    order = lambda p: ((p & 1) << 1) | (p >> 1)

    def body(place_ref, x_ref, g_ref, *refs):
        cins, (hn_ref, o_ref, ob_ref), couts = refs[:k_in], refs[k_in:k_in + 3], refs[k_in + 3:k_in + 3 + k_out]
        wbuf, hn_all, local, ici_send, ici_recv, d2d_send, d2d_recv, send, recv = refs[k_in + 3 + k_out:]
        p, i = pl.program_id(0), pl.program_id(1)
        x, y, c, chips = _place()
        me = 2 * x + y
        rows = lambda chunk, cc: couts[o_w].at[chunk].at[pl.ds(cc * half, half)]

        @pl.when(jnp.logical_and(p == 0, i == 0))
        def _():
            for j, (px, py) in enumerate(chips):
                _remote(rows(me, c), rows(me, c), ici_send.at[j], ici_recv.at[j], (px, py, c)).start()
            for op in car.ops:
                op[0](cins, couts, send, recv)

        for j, (px, py) in enumerate(chips):
            @pl.when(jnp.logical_and(p == j + 1, i == 0))
            def _(j=j, px=px, py=py):
                landed, other = rows(2 * px + py, c), rows(2 * px + py, 1 - c)
                _remote(landed, landed, ici_send.at[j], ici_recv.at[j], (px, py, c)).wait_recv()
                _remote(landed, landed, d2d_send.at[j], d2d_recv.at[j], (x, y, 1 - c)).start()
                _remote(other, other, d2d_send.at[j], d2d_recv.at[j], (x, y, 1 - c)).wait_recv()

        @pl.when(i == 0)
        def _():
            cp = pltpu.make_async_copy(couts[o_w].at[me ^ order(p)], wbuf, local.at[0])
            cp.start()
            cp.wait()

        tile = pl.ds(pl.multiple_of(i * bm, bm), bm)

        @pl.when(p == 0)
        def _():
            xv = x_ref[...]
            hn_all[tile, :] = ((xv * _rstd(xv)) * g_ref[...]).astype(_MXU)

        hn = hn_all[tile, :]
        hn_ref[...] = hn
        res = _dot(hn, wbuf[...])
        o_ref[...] = res
        ob_ref[...] = res.astype(ob_ref.dtype)

        @pl.when(jnp.logical_and(p == 3, i == ni - 1))
        def _():
            for j, (px, py) in enumerate(chips):
                _remote(rows(me, c), rows(me, c), ici_send.at[j], ici_recv.at[j], (px, py, c)).wait_send()
                _remote(rows(me, c), rows(me, c), d2d_send.at[j], d2d_recv.at[j], (x, y, 1 - c)).wait_send()
            for op in car.ops:
                op[1](cins, couts, send, recv)

    ospec = pl.BlockSpec((bm, n), lambda p, i, place_ref: (i, place_ref[0] ^ order(p)))
    rows = pl.BlockSpec((bm, k), lambda p, i, place_ref: (jnp.where(p == 0, i, 0), 0))
    three, sems = pltpu.SemaphoreType.DMA((3,)), pltpu.SemaphoreType.DMA((max(car.n_sems, 1),))
    res = pl.pallas_call(
        body,
        grid_spec=pltpu.PrefetchScalarGridSpec(
            num_scalar_prefetch=1, grid=(4, ni),
            in_specs=[rows, pl.BlockSpec((1, k), lambda p, i, place_ref: (0, 0))] + [_ANY] * k_in,
            out_specs=[pl.BlockSpec((bm, k), lambda p, i, place_ref: (p * ni + i, 0)), ospec, ospec] + [_ANY] * k_out,
            scratch_shapes=[pltpu.VMEM((k, n), _MXU), pltpu.VMEM((m, k), _MXU), pltpu.SemaphoreType.DMA((1,)),
                            three, three, three, three, sems, sems]),
        out_shape=[S((4 * m, k), _MXU), S((m, 4 * n), F32), S((m, 4 * n), _MXU)] + car.out_shapes,
        input_output_aliases={3 + a: 3 + o for a, o in car.aliases.items()},
        compiler_params=_cp("arbitrary", "arbitrary"), name=name)(place, x, gain, *car.inputs)
    car.results = list(res[3:])
    return res[0], res[1], res[2]


def _mm_nn(a, b3, *, bm, bn, name, also=None, comm=None):
    m, k = a.shape
    c, _, n = b3.shape
    ni, nj = m // bm, n // bn

    def body(a_ref, b_ref, *o_refs):
        res = _dot(a_ref[...], b_ref[...])
        for o_ref in o_refs:
            o_ref[...] = res.astype(o_ref.dtype)

    ospec = pl.BlockSpec((bm, bn), lambda cc, j, i: (i, cc * nj + j))
    dtypes = [F32] + ([] if also is None else [also])
    out = _call(
        comm, body, grid=(c, nj, ni),
        in_specs=[pl.BlockSpec((bm, k), lambda cc, j, i: (i, 0)), pl.BlockSpec((None, k, bn), lambda cc, j, i: (cc, 0, j))],
        out_specs=[ospec] * len(dtypes), out_shape=[S((m, c * n), dt) for dt in dtypes],
        compiler_params=_cp("parallel", "parallel", "parallel"), name=name)(a, b3)
    return out[0] if also is None else out


def _mm_nt(a, b3, *, bm, bo, out_dtype, name, comm=None):
    m = a.shape[0]
    c, ko, n = b3.shape
    ni, nj = m // bm, ko // bo

    def body(a_ref, b_ref, o_ref):
        acc = _dot_nt(a_ref[:, 0:n], b_ref[0])
        for cc in range(1, c):
            acc = acc + _dot_nt(a_ref[:, cc * n:(cc + 1) * n], b_ref[cc])
        o_ref[...] = acc.astype(o_ref.dtype)

    return _call(
        comm, body, grid=(nj, ni),
        in_specs=[pl.BlockSpec((bm, c * n), lambda j, i: (i, 0)),
                  pl.BlockSpec((c, bo, n), lambda j, i: (0, j, 0))],
        out_specs=pl.BlockSpec((bm, bo), lambda j, i: (i, j)),
        out_shape=S((m, ko), out_dtype),
        compiler_params=_cp("parallel", "parallel"), name=name)(a, b3)


def _mm_tn(a, b, c, *, bm, bk, out_dtype, name, comm=None):
    m, k = b.shape[0], a.shape[1]
    n = b.shape[1] // c
    nm, nk = m // bm, k // bk

    def body(a_ref, b_ref, o_ref, acc):
        mm = pl.program_id(2)

        @pl.when(mm == 0)
        def _():
            acc[...] = jnp.zeros_like(acc)

        acc[...] += _dot_tn(a_ref[...], b_ref[...])

        @pl.when(mm == nm - 1)
        def _():
            o_ref[...] = acc[...].astype(o_ref.dtype)

    return _call(
        comm, body, grid=(c, nk, nm),
        in_specs=[pl.BlockSpec((bm, bk), lambda cc, j, mm: (mm, j)),
                  pl.BlockSpec((bm, n), lambda cc, j, mm: (mm, cc))],
        out_specs=pl.BlockSpec((None, bk, n), lambda cc, j, mm: (cc, j, 0)),
        out_shape=S((c, k, n), out_dtype),
        scratch_shapes=[pltpu.VMEM((bk, n), F32)],
        compiler_params=_cp("parallel", "parallel", "arbitrary"), name=name)(a, b)


def _swiglu_fwd(hn, wg3, wu3, *, bm, name, comm=None):
    m, k = hn.shape
    c, _, n = wg3.shape

    def body(a_ref, g_ref, u_ref, dgate_ref, dup_ref, act_ref):
        a = a_ref[...]
        gate = _dot(a, g_ref[...])
        up = _dot(a, u_ref[...])
        sg = jax.nn.sigmoid(gate)
        silu = gate * sg
        dgate_ref[...] = (up * (sg * (1.0 + gate * (1.0 - sg)))).astype(dgate_ref.dtype)
        dup_ref[...] = silu.astype(dup_ref.dtype)
        act_ref[...] = (silu * up).astype(act_ref.dtype)

    wspec = pl.BlockSpec((None, k, n), lambda cc, i: (cc, 0, 0))
    ospec = pl.BlockSpec((bm, n), lambda cc, i: (i, cc))
    return _call(
        comm, body, grid=(c, m // bm),
        in_specs=[pl.BlockSpec((bm, k), lambda cc, i: (i, 0)), wspec, wspec],
        out_specs=[ospec, ospec, ospec],
        out_shape=[S((m, c * n), _MXU), S((m, c * n), _MXU), S((m, c * n), _MXU)],
        compiler_params=_cp("parallel", "parallel"), name=name)(hn, wg3, wu3)


def _swiglu_bwd(df, wd, act_dgate, act_dup, *, bm, bo, name):
    m, k = df.shape
    ko = wd.shape[0]

    def body(a_ref, b_ref, g_ref, u_ref, dg_ref, du_ref):
        dact = _dot_nt(a_ref[...], b_ref[...])
        dg_ref[...] = (dact * g_ref[...].astype(F32)).astype(dg_ref.dtype)
        du_ref[...] = (dact * u_ref[...].astype(F32)).astype(du_ref.dtype)

    ospec = pl.BlockSpec((bm, bo), lambda j, i: (i, j))
    return pl.pallas_call(
        body, grid=(ko // bo, m // bm),
        in_specs=[pl.BlockSpec((bm, k), lambda j, i: (i, 0)), pl.BlockSpec((bo, k), lambda j, i: (j, 0)), ospec, ospec],
        out_specs=[ospec, ospec],
        out_shape=[S((m, ko), _MXU), S((m, ko), _MXU)],
        compiler_params=_cp("parallel", "parallel"), name=name)(df, wd, act_dgate, act_dup)


def _rms_fwd(x, gain, name):
    t, d = x.shape
    tm = min(t, ROW_TILE)

    def body(x_ref, g_ref, o_ref):
        xv = x_ref[...]
        o_ref[...] = ((xv * _rstd(xv)) * g_ref[...]).astype(o_ref.dtype)

    return pl.pallas_call(body, grid=(t // tm,), in_specs=[_row_spec(tm, d), _vec_spec(d)], out_specs=_row_spec(tm, d),
                          out_shape=S((t, d), _MXU), compiler_params=_cp("parallel"), name=name)(x, gain)


def _outnorm_fwd(o, yl, ga, gl, name, comm=None):
    t, w = o.shape
    tm = min(t, ROW_TILE)

    def body(o_ref, l_ref, ga_ref, gl_ref, y_ref):
        ov, lv = o_ref[...], l_ref[...]
        y_ref[:, :w] = ((ov * _rstd(ov)) * ga_ref[...]).astype(y_ref.dtype)
        y_ref[:, w:] = ((lv * _rstd(lv)) * gl_ref[...]).astype(y_ref.dtype)

    return _call(comm, body, grid=(t // tm,), in_specs=[_row_spec(tm, w), _row_spec(tm, w), _vec_spec(w), _vec_spec(w)],
                 out_specs=_row_spec(tm, 2 * w), out_shape=S((t, 2 * w), _MXU),
                 compiler_params=_cp("parallel"), name=name)(o, yl, ga, gl)


def _mid_fwd(x, mix, g_post, g_pre, name, comm=None):
    t, d = x.shape
    tm = min(t, ROW_TILE)

    def body(x_ref, m_ref, gp_ref, gn_ref, x2_ref, hn_ref):
        mv = m_ref[...]
        x2 = x_ref[...] + (mv * _rstd(mv)) * gp_ref[...]
        x2_ref[...] = x2
        hn_ref[...] = ((x2 * _rstd(x2)) * gn_ref[...]).astype(hn_ref.dtype)

    return _call(comm, body, grid=(t // tm,), in_specs=[_row_spec(tm, d), _row_spec(tm, d), _vec_spec(d), _vec_spec(d)],
                          out_specs=[_row_spec(tm, d), _row_spec(tm, d)], out_shape=[S((t, d), F32), S((t, d), _MXU)],
                          compiler_params=_cp("parallel"), name=name)(x, mix, g_post, g_pre)


def _final(f, x2, target, g_post, name):
    t, d = f.shape
    tm = min(t, ROW_TILE)

    def body(f_ref, x2_ref, t_ref, g_ref, loss_ref, dout_ref, df_ref, dg_ref):
        @pl.when(pl.program_id(0) == 0)
        def _():
            loss_ref[...] = jnp.zeros_like(loss_ref)
            dg_ref[...] = jnp.zeros_like(dg_ref)

        fv = f_ref[...]
        r = _rstd(fv)
        fh = fv * r
        err = (x2_ref[...] + fh * g_ref[...]) - t_ref[...]
        loss_ref[...] += jnp.sum(err * err, axis=0, keepdims=True)
        dout = err * (1.0 / d)
        dout_ref[...] = dout
        dfv, dg = _rms_bwd(dout, fh, r, g_ref[...])
        df_ref[...] = dfv.astype(df_ref.dtype)
        dg_ref[...] += dg

    return pl.pallas_call(
        body, grid=(t // tm,),
        in_specs=[_row_spec(tm, d), _row_spec(tm, d), _row_spec(tm, d), _vec_spec(d)],
        out_specs=[_vec_spec(d), _row_spec(tm, d), _row_spec(tm, d), _vec_spec(d)],
        out_shape=[S((1, d), F32), S((t, d), F32), S((t, d), _MXU), S((1, d), F32)],
        compiler_params=_cp("arbitrary"), name=name)(f, x2, target, g_post)


def _mid_bwd(dhn_a, dhn_b, dout, x2, mix, g_pre, g_post, name, comm=None):
    t, d = x2.shape
    tm = min(t, ROW_TILE)

    def body(da_ref, db_ref, do_ref, x2_ref, m_ref, gn_ref, gp_ref, dx2_ref, dm_ref, dgn_ref, dgp_ref):
        @pl.when(pl.program_id(0) == 0)
        def _():
            dgn_ref[...] = jnp.zeros_like(dgn_ref)
            dgp_ref[...] = jnp.zeros_like(dgp_ref)

        x2 = x2_ref[...]
        r = _rstd(x2)
        dxa, dgn = _rms_bwd(da_ref[...] + db_ref[...], x2 * r, r, gn_ref[...])
        dx2 = do_ref[...] + dxa
        dx2_ref[...] = dx2
        dgn_ref[...] += dgn
        mv = m_ref[...]
        rm = _rstd(mv)
        dmv, dgp = _rms_bwd(dx2, mv * rm, rm, gp_ref[...])
        dm_ref[...] = dmv.astype(dm_ref.dtype)
        dgp_ref[...] += dgp

    rs, vs = _row_spec(tm, d), _vec_spec(d)
    return _call(
        comm, body, grid=(t // tm,), in_specs=[rs, rs, rs, rs, rs, vs, vs], out_specs=[rs, rs, vs, vs],
        out_shape=[S((t, d), F32), S((t, d), _MXU), S((1, d), F32), S((1, d), F32)],
        compiler_params=_cp("arbitrary"), name=name)(dhn_a, dhn_b, dout, x2, mix, g_pre, g_post)


def _first_bwd(dhn, dx2, x, gain, name, comm=None):
    t, d = x.shape
    tm = min(t, ROW_TILE)

    def body(dh_ref, dx2_ref, x_ref, g_ref, dx_ref, dg_ref):
        @pl.when(pl.program_id(0) == 0)
        def _():
            dg_ref[...] = jnp.zeros_like(dg_ref)

        xv = x_ref[...]
        r = _rstd(xv)
        dxa, dg = _rms_bwd(dh_ref[...], xv * r, r, g_ref[...])
        dx_ref[...] = dx2_ref[...] + dxa
        dg_ref[...] += dg

    rs, vs = _row_spec(tm, d), _vec_spec(d)
    return _call(comm, body, grid=(t // tm,), in_specs=[rs, rs, rs, vs], out_specs=[rs, vs],
                          out_shape=[S((t, d), F32), S((1, d), F32)], compiler_params=_cp("arbitrary"), name=name)(dhn, dx2, x, gain)


def _outnorm_bwd(dy, o, yl, ga, gl, name, comm=None):
    t, w = o.shape
    tm = min(t, ROW_TILE)

    def body(dy_ref, o_ref, l_ref, ga_ref, gl_ref, do_ref, dl_ref, dga_ref, dgl_ref):
        @pl.when(pl.program_id(0) == 0)
        def _():
            dga_ref[...] = jnp.zeros_like(dga_ref)
            dgl_ref[...] = jnp.zeros_like(dgl_ref)

        ov, lv = o_ref[...], l_ref[...]
        ra, rl = _rstd(ov), _rstd(lv)
        dov, dga = _rms_bwd(dy_ref[:, :w], ov * ra, ra, ga_ref[...])
        dlv, dgl = _rms_bwd(dy_ref[:, w:], lv * rl, rl, gl_ref[...])
        do_ref[...] = dov.astype(do_ref.dtype)
        dl_ref[...] = dlv
        dga_ref[...] += dga
        dgl_ref[...] += dgl

    rs, vs = _row_spec(tm, w), _vec_spec(w)
    return _call(comm, body, grid=(t // tm,), in_specs=[_row_spec(tm, 2 * w), rs, rs, vs, vs], out_specs=[rs, rs, vs, vs],
                          out_shape=[S((t, w), _MXU), S((t, w), F32), S((1, w), F32), S((1, w), F32)],
                          compiler_params=_cp("arbitrary"), name=name)(dy, o, yl, ga, gl)


def _tri_sum(v, tri):
    return _dot(v.astype(_MXU), tri)


def _attn_tile(qb, kb, row, col, shift, scale):
    z = _dot_nt(qb, kb) * scale
    mask = (col + shift) < row
    lb = _log_sigmoid(z)
    lm = jnp.where(mask, lb - z, 0.0)
    return mask, lb, lm


def _attn_fwd(proj, n_heads, name, comm=None):
    t = proj.shape[0]
    bq = min(t, ATTN_BLOCK)
    nq = t // bq
    scale = 1.0 / math.sqrt(HEAD_DIM)

    heads = [slice(a * HEAD_DIM, (a + 1) * HEAD_DIM) for a in range(ATTN_HEADS)]

    def body(q_ref, k_ref, v_ref, o_ref):
        row = lax.broadcasted_iota(jnp.int32, (bq, bq), 0)
        col = lax.broadcasted_iota(jnp.int32, (bq, bq), 1)
        tri = (row > col).astype(_MXU)

        def per_q(qi, _):
            q0 = pl.multiple_of(qi * bq, bq)
            qbs = [q_ref[pl.ds(q0, bq), hd] for hd in heads]

            def cond(st):
                return jnp.logical_and(st[0] >= 0, st[1])

            def step(st):
                kj, _, carries, accs = st
                k0 = pl.multiple_of(kj * bq, bq)
                alive, new_carries, new_accs = None, [], []
                for hd, qb, carry, acc in zip(heads, qbs, carries, accs):
                    mask, lb, lm = _attn_tile(qb, k_ref[pl.ds(k0, bq), hd], row, col, (kj - qi) * bq, scale)
                    w = jnp.where(mask, jnp.exp(lb + _tri_sum(lm, tri) + carry), 0.0)
                    new_accs.append(acc + _dot(w.astype(_MXU), v_ref[pl.ds(k0, bq), hd]))
                    carry = carry + jnp.sum(lm, axis=1, keepdims=True)
                    new_carries.append(carry)
                    live = jnp.max(carry) > EXP_CUT
                    alive = live if alive is None else jnp.logical_or(alive, live)
                return kj - 1, alive, tuple(new_carries), tuple(new_accs)

            st = lax.while_loop(cond, step, (qi, jnp.bool_(True), (jnp.zeros((bq, 1), F32),) * ATTN_HEADS,
                                             (jnp.zeros((bq, HEAD_DIM), F32),) * ATTN_HEADS))
            for hd, acc in zip(heads, st[3]):
                o_ref[pl.ds(q0, bq), hd] = acc
            return 0

        lax.fori_loop(0, nq, per_q, 0)

    groups = n_heads // ATTN_HEADS
    hs = lambda off: pl.BlockSpec((t, ATTN_HEADS * HEAD_DIM), lambda h: (0, off + h))
    return _call(
        comm, body, grid=(groups,), in_specs=[hs(0), hs(groups), hs(2 * groups)], out_specs=hs(0),
        out_shape=S((t, n_heads * HEAD_DIM), F32), compiler_params=_cp("parallel"), name=name)(proj, proj, proj)


def _emit(blocks, out_ref, starts, sems):
    copies = [pltpu.make_async_copy(b, out_ref.at[:, pl.ds(c0, b.shape[1])], sems.at[k]) for k, (b, c0) in enumerate(zip(blocks, starts))]
    for cp in copies:
        cp.start()
    for cp in copies:
        cp.wait()


def _attn_bwd(proj, do, dproj, n_heads, name, comm=None):
    t = proj.shape[0]
    bq = min(t, ATTN_BLOCK)
    nq = t // bq
    scale = 1.0 / math.sqrt(HEAD_DIM)
    groups = n_heads // ATTN_HEADS
    wide = ATTN_HEADS * HEAD_DIM

    heads = [slice(a * HEAD_DIM, (a + 1) * HEAD_DIM) for a in range(ATTN_HEADS)]

    def body(q_ref, k_ref, v_ref, do_ref, _, dproj_ref, dka_ref, dva_ref, g_ref, b_ref, dq_ref, dk_ref, dv_ref, out_sems):
        group = pl.program_id(0)
        dka_ref[...] = jnp.zeros_like(dka_ref)
        dva_ref[...] = jnp.zeros_like(dva_ref)
        row = lax.broadcasted_iota(jnp.int32, (bq, bq), 0)
        col = lax.broadcasted_iota(jnp.int32, (bq, bq), 1)
        tri = (row > col).astype(_MXU)
        tri_lt = (row < col).astype(_MXU)

        def per_q(qi, _):
            q0 = pl.multiple_of(qi * bq, bq)
            qbs = [q_ref[pl.ds(q0, bq), hd] for hd in heads]
            dobs = [do_ref[pl.ds(q0, bq), hd] for hd in heads]

            def cond(st):
                return jnp.logical_and(st[0] >= 0, st[1])

            def step(st):
                kj, _, carries = st
                k0 = pl.multiple_of(kj * bq, bq)
                alive, new_carries = None, []
                for a, (hd, qb, dob, carry) in enumerate(zip(heads, qbs, dobs, carries)):
                    mask, lb, lm = _attn_tile(qb, k_ref[pl.ds(k0, bq), hd], row, col, (kj - qi) * bq, scale)
                    w = jnp.where(mask, jnp.exp(lb + _tri_sum(lm, tri) + carry), 0.0)
                    g_ref[a, pl.ds(k0, bq), :] = w * _dot_nt(dob, v_ref[pl.ds(k0, bq), hd])
                    b_ref[a, pl.ds(k0, bq), :] = jnp.where(mask, jnp.exp(lb), 0.0)
                    dva_ref[pl.ds(k0, bq), hd] += _dot_tn(w.astype(_MXU), dob)
                    carry = carry + jnp.sum(lm, axis=1, keepdims=True)
                    new_carries.append(carry)
                    live = jnp.max(carry) > EXP_CUT
                    alive = live if alive is None else jnp.logical_or(alive, live)
                return kj - 1, alive, tuple(new_carries)

            st = lax.while_loop(cond, step, (qi, jnp.bool_(True), (jnp.zeros((bq, 1), F32),) * ATTN_HEADS))

            def back(kj, st2):
                k0 = pl.multiple_of(kj * bq, bq)
                out = []
                for a, (hd, qb, (before, dq)) in enumerate(zip(heads, qbs, st2)):
                    g = g_ref[a, pl.ds(k0, bq), :]
                    beta = b_ref[a, pl.ds(k0, bq), :]
                    dz = ((g * (1.0 - beta) - (before + _tri_sum(g, tri_lt)) * beta) * scale).astype(_MXU)
                    dka_ref[pl.ds(k0, bq), hd] += _dot_tn(dz, qb)
                    out.append((before + jnp.sum(g, axis=1, keepdims=True), dq + _dot(dz, k_ref[pl.ds(k0, bq), hd])))
                return tuple(out)

            st2 = lax.fori_loop(st[0] + 1, qi + 1, back, ((jnp.zeros((bq, 1), F32), jnp.zeros((bq, HEAD_DIM), F32)),) * ATTN_HEADS)
            for hd, (_, dq) in zip(heads, st2):
                dq_ref[pl.ds(q0, bq), hd] = dq.astype(dq_ref.dtype)
            return 0

        lax.fori_loop(0, nq, per_q, 0)
        dk_ref[...] = dka_ref[...].astype(dk_ref.dtype)
        dv_ref[...] = dva_ref[...].astype(dv_ref.dtype)
        _emit([dq_ref, dk_ref, dv_ref], dproj_ref, [(a * groups + group) * wide for a in range(3)], out_sems)

    hs = lambda off: pl.BlockSpec((t, wide), lambda h: (0, off + h))
    return _call(
        comm, body, grid=(groups,), in_specs=[hs(0), hs(groups), hs(2 * groups), hs(0), _ANY], out_specs=_ANY,
        out_shape=S(dproj.shape, dproj.dtype), input_output_aliases={4: 0},
        scratch_shapes=[pltpu.VMEM((t, wide), F32), pltpu.VMEM((t, wide), F32),
                        pltpu.VMEM((ATTN_HEADS, t, bq), F32), pltpu.VMEM((ATTN_HEADS, t, bq), F32)]
        + [pltpu.VMEM((t, wide), dproj.dtype)] * 3 + [pltpu.SemaphoreType.DMA((3,))],
        compiler_params=_cp("parallel"), name=name)(proj, proj, proj, do, dproj)


def _shift_down(cur, prev8, k):
    if k == 0:
        return cur
    row8 = lax.broadcasted_iota(jnp.int32, prev8.shape, 0)
    rc = pltpu.roll(cur, k, 0)
    top = jnp.where(row8 < k, pltpu.roll(prev8, k, 0), rc[0:8, :])
    return jnp.concatenate([top, rc[8:, :]], axis=0)


def _shift_up(cur, next8, k):
    if k == 0:
        return cur
    n = cur.shape[0]
    row8 = lax.broadcasted_iota(jnp.int32, next8.shape, 0)
    rc = pltpu.roll(cur, n - k, 0)
    bottom = jnp.where(row8 >= 8 - k, pltpu.roll(next8, 8 - k, 0), rc[n - 8:, :])
    return jnp.concatenate([rc[:n - 8, :], bottom], axis=0)


def _lru_conv(xl, prev8, cw, cb):
    xs = [_shift_down(xl, prev8, CONV_WIDTH - 1 - k) for k in range(CONV_WIDTH)]
    xc = xs[0] * cw[0:1, :]
    for k in range(1, CONV_WIDTH):
        xc = xc + xs[k] * cw[k:k + 1, :]
    return xs, xc + cb


def _lru_gates(xl, prev8, cw, cb, wr, br, wi, bi, ls):
    xs, xc = _lru_conv(xl, prev8, cw, cb)
    xcb = xc.astype(_MXU)
    r = jax.nn.sigmoid(_dot(xcb, wr) + br)
    i = jax.nn.sigmoid(_dot(xcb, wi) + bi)
    la = (LRU_C * r) * ls
    a = jnp.exp(la)
    mult = jnp.sqrt(-_expm1(2.0 * la))
    return xs, xc, r, i, a, mult


def _group_scan(a, b, reverse):
    n = a.shape[0]
    row = lax.broadcasted_iota(jnp.int32, a.shape, 0) % 8
    for d in (1, 2, 4):
        if reverse:
            m = row < 8 - d
            a_s, b_s = pltpu.roll(a, n - d, 0), pltpu.roll(b, n - d, 0)
        else:
            m = row >= d
            a_s, b_s = pltpu.roll(a, d, 0), pltpu.roll(b, d, 0)
        b = jnp.where(m, a * b_s + b, b)
        a = jnp.where(m, a * a_s, a)
    return a, b


def _lru_fwd(proj, col0, n_blocks, cw, cb, wr, br, wi, bi, lam, name, comm=None):
    t = proj.shape[0]
    tt = min(t, SEQ_TILE)
    nt = t // tt

    def body(xl_ref, gl_ref, cw_ref, cb_ref, wr_ref, br_ref, wi_ref, bi_ref, lam_ref, h_ref, y_ref, *kept):
        cwv, cbv, brv, biv = cw_ref[...], cb_ref[...], br_ref[...], bi_ref[...]
        wrv, wiv = wr_ref[...].astype(_MXU), wi_ref[...].astype(_MXU)
        ls = _log_sigmoid(lam_ref[...])

        def tile(ti, hin):
            t0 = pl.multiple_of(ti * tt, tt)
            p0 = pl.multiple_of(jnp.maximum(t0 - 8, 0), 8)
            prev8 = xl_ref[pl.ds(p0, 8), :] * (ti > 0).astype(F32)
            xl = xl_ref[pl.ds(t0, tt), :]
            _, xc, r, ig, a, mult = _lru_gates(xl, prev8, cwv, cbv, wrv, brv, wiv, biv, ls)
            for ref, val in zip(kept, (r, ig, a, mult)):
                ref[pl.ds(t0, tt), :] = val
            ga, gb = _group_scan(a, mult * (ig * xc), False)
            for g in range(tt // 8):
                hg = ga[8 * g:8 * g + 8, :] * hin + gb[8 * g:8 * g + 8, :]
                h_ref[pl.ds(t0 + 8 * g, 8), :] = hg
                hin = hg[7:8, :]
            y_ref[pl.ds(t0, tt), :] = h_ref[pl.ds(t0, tt), :] * _gelu(gl_ref[pl.ds(t0, tt), :])
            return hin

        lax.fori_loop(0, nt, tile, jnp.zeros((1, HEAD_DIM), F32))

    cs = lambda off: pl.BlockSpec((t, HEAD_DIM), lambda n: (0, off + n))
    vs = pl.BlockSpec((1, HEAD_DIM), lambda n: (0, n))
    ws = pl.BlockSpec((None, HEAD_DIM, HEAD_DIM), lambda n: (n, 0, 0))
    w = n_blocks * HEAD_DIM
    return _call(
        comm, body, grid=(n_blocks,),
        in_specs=[cs(col0), cs(col0 + n_blocks), pl.BlockSpec((CONV_WIDTH, HEAD_DIM), lambda n: (0, n)), vs, ws, vs, ws, vs, vs],
        out_specs=[cs(0)] * 6, out_shape=[S((t, w), F32)] * 6,
        compiler_params=_cp("parallel"), name=name)(proj, proj, cw, cb, wr, br, wi, bi, lam)


def _lru_bwd(proj, col0, n_blocks, h, kept, dyl, cw, cb, wr, wi, lam, name, comm=None):
    t = proj.shape[0]
    tt = min(t, SEQ_TILE)
    nt = t // tt

    def body(xl_ref, gl_ref, h_ref, r_ref, i_ref, a_ref, m_ref, dy_ref, cw_ref, cb_ref, wr_ref, wi_ref, lam_ref,
             dproj_ref, dcw_ref, dcb_ref, dwr_ref, dbr_ref, dwi_ref, dbi_ref, dlam_ref, g_ref, dxl_ref, dgl_ref, out_sems):
        block = pl.program_id(0)
        cwv, cbv = cw_ref[...], cb_ref[...]
        wrv, wiv = wr_ref[...].astype(_MXU), wi_ref[...].astype(_MXU)
        lamv = lam_ref[...]
        ls = _log_sigmoid(lamv)
        for ref in (dcw_ref, dcb_ref, dwr_ref, dbr_ref, dwi_ref, dbi_ref, dlam_ref):
            ref[...] = jnp.zeros_like(ref)

        def tile(s, carry):
            e_in, dxc_next8 = carry
            ti = nt - 1 - s
            t0 = pl.multiple_of(ti * tt, tt)
            p0 = pl.multiple_of(jnp.maximum(t0 - 8, 0), 8)
            first = (ti > 0).astype(F32)
            xl = xl_ref[pl.ds(t0, tt), :]
            xs, xc = _lru_conv(xl, xl_ref[pl.ds(p0, 8), :] * first, cwv, cbv)
            r, ig, a, mult = (ref[pl.ds(t0, tt), :] for ref in (r_ref, i_ref, a_ref, m_ref))
            hv = h_ref[pl.ds(t0, tt), :]
            h_before = _shift_down(hv, h_ref[pl.ds(p0, 8), :] * first, 1)
            glv = gl_ref[pl.ds(t0, tt), :]
            dyv = dy_ref[pl.ds(t0, tt), :]
            dgl_ref[pl.ds(t0, tt), :] = (dyv * hv * _gelu_grad(glv)).astype(dgl_ref.dtype)
            dh = dyv * _gelu(glv)
            row = lax.broadcasted_iota(jnp.int32, a.shape, 0)
            coef = jnp.where(row == tt - 1, 1.0, pltpu.roll(a, tt - 1, 0))
            ga, gb = _group_scan(coef, dh, True)
            gin = e_in
            for g in reversed(range(tt // 8)):
                gg = ga[8 * g:8 * g + 8, :] * gin + gb[8 * g:8 * g + 8, :]
                g_ref[8 * g:8 * g + 8, :] = gg
                gin = gg[0:1, :]
            gv = g_ref[...]
            e_out = a[0:1, :] * gv[0:1, :]
            ix = ig * xc
            dla = (gv * h_before) * a - (gv * ix) * (a * a / mult)
            dlam_ref[...] += jnp.sum(dla * (LRU_C * r), axis=0, keepdims=True)
            dpr = (dla * (LRU_C * ls)) * (r * (1.0 - r))
            dpi = (gv * mult * xc) * (ig * (1.0 - ig))
            dbr_ref[...] += jnp.sum(dpr, axis=0, keepdims=True)
            dbi_ref[...] += jnp.sum(dpi, axis=0, keepdims=True)
            xcb, dprb, dpib = xc.astype(_MXU), dpr.astype(_MXU), dpi.astype(_MXU)
            dwr_ref[...] += _dot_tn(xcb, dprb)
            dwi_ref[...] += _dot_tn(xcb, dpib)
            dxc = gv * mult * ig + _dot_nt(dprb, wrv) + _dot_nt(dpib, wiv)
            dcb_ref[...] += jnp.sum(dxc, axis=0, keepdims=True)
            dxl = None
            for k in range(CONV_WIDTH):
                dcw_ref[k:k + 1, :] += jnp.sum(dxc * xs[k], axis=0, keepdims=True)
                term = _shift_up(dxc, dxc_next8, CONV_WIDTH - 1 - k) * cwv[k:k + 1, :]
                dxl = term if dxl is None else dxl + term
            dxl_ref[pl.ds(t0, tt), :] = dxl.astype(dxl_ref.dtype)
            return e_out, dxc[0:8, :]

        lax.fori_loop(0, nt, tile, (jnp.zeros((1, HEAD_DIM), F32), jnp.zeros((8, HEAD_DIM), F32)))
        dlam_ref[...] = dlam_ref[...] * (1.0 - jax.nn.sigmoid(lamv))
        _emit([dxl_ref, dgl_ref], dproj_ref, [(col0 + block) * HEAD_DIM, (col0 + n_blocks + block) * HEAD_DIM], out_sems)

    cs = lambda off: pl.BlockSpec((t, HEAD_DIM), lambda n: (0, off + n))
    vs = pl.BlockSpec((1, HEAD_DIM), lambda n: (0, n))
    ws = pl.BlockSpec((None, HEAD_DIM, HEAD_DIM), lambda n: (n, 0, 0))
    cws = pl.BlockSpec((CONV_WIDTH, HEAD_DIM), lambda n: (0, n))
    w = n_blocks * HEAD_DIM
    vec = S((1, w), F32)
    mat = S((n_blocks, HEAD_DIM, HEAD_DIM), F32)
    return _call(
        comm, body, grid=(n_blocks,),
        in_specs=[cs(col0), cs(col0 + n_blocks)] + [cs(0)] * 6 + [cws, vs, ws, ws, vs],
        out_specs=[_ANY, cws, vs, ws, vs, ws, vs, vs],
        out_shape=[S(proj.shape, _MXU), S((CONV_WIDTH, w), F32), vec, mat, vec, mat, vec, vec],
        scratch_shapes=[pltpu.VMEM((tt, HEAD_DIM), F32), pltpu.VMEM((t, HEAD_DIM), _MXU), pltpu.VMEM((t, HEAD_DIM), _MXU),
                        pltpu.SemaphoreType.DMA((2,))],
        compiler_params=_cp("parallel"), name=name)(proj, proj, h, *kept, dyl, cw, cb, wr, wi, lam)


class _NoExchange:
    grad_dtype = F32

    def __init__(self, weights):
        self.weights, self.grads, self.packs = weights, {}, {}

    def weight(self, name):
        return self.weights[name]

    def in_proj(self, x, gain, bm):
        hn = _rms_fwd(x, gain, "rms1")
        return [hn, *_mm_nn(hn, self.weights["w_in"], bm=bm, bn=self.weights["w_in"].shape[2], name="in_proj", also=_MXU)]

    def conv_w(self):
        return self.weights["conv_w"]

    def carrier(self, call):
        return None

    def harvest(self, car):
        pass

    def alone(self, call):
        pass


def _local_step(x, target, norms, ex, cb, wr, br, wi, bi, lam, ga, gl):
    g_pre_mix, g_post_mix, g_pre_ffn, g_post_ffn = norms
    t, d = x.shape
    bm = min(t, MM_ROWS)
    bt = min(t, DW_TOKENS)

    def run(fn, name, *args, **kw):
        car = ex.carrier(name)
        out = fn(*args, name=name, comm=car, **kw)
        ex.harvest(car)
        return out

    hn1, proj, proj_mx = ex.in_proj(x, g_pre_mix, bm)
    win3, cw = ex.weight("w_in"), ex.conv_w()
    c = win3.shape[0]
    o = run(_attn_fwd, "attn_fwd", proj_mx, (proj.shape[1] - d) // 3 // HEAD_DIM)
    mix = 2 * o.shape[1]
    n_heads = n_blocks = o.shape[1] // HEAD_DIM
    h, yl, *kept = run(_lru_fwd, "lru_fwd", proj, 3 * n_heads, n_blocks, cw, cb, wr, br, wi, bi, lam)
    y = run(_outnorm_fwd, "outnorm_fwd", o, yl, ga, gl)
    wout = ex.weight("w_out")
    mixo = run(_mm_nn, "out_proj", y, wout[None], bm=bm, bn=d)
    x2, hn2 = run(_mid_fwd, "mid_fwd", x, mixo, g_post_mix, g_pre_ffn)
    ex.alone("gather_w_up_last")
    wg3, wu3 = ex.weight("w_ffn_gate"), ex.weight("w_ffn_up")
    act_dgate, act_dup, act = run(_swiglu_fwd, "ffn_gate_up", hn2, wg3, wu3, bm=bm)
    ex.alone("gather_w_down")
    wd = ex.weight("w_ffn_down")
    ff = wd.shape[0]
    f = _mm_nn(act, wd[None], bm=bm, bn=d // 2, name="ffn_down")
    loss_cols, dout, df, dg_post_ffn = _final(f, x2, target, g_post_ffn, "final")

    dgate, dup = _swiglu_bwd(df, wd, act_dgate, act_dup, bm=min(t, 2 * MM_ROWS), bo=ff // 4, name="ffn_down_bwd")
    ex.grads["w_ffn_down"] = _mm_tn(act, df, 1, bm=bt, bk=DW_ROWS, out_dtype=ex.grad_dtype, name="ffn_down_dw").reshape(c, ff // c, d)
    ex.grads["w_ffn_gate"] = run(_mm_tn, "ffn_gate_dw", hn2, dgate, c, bm=bt, bk=d // 2, out_dtype=ex.grad_dtype)
    ex.grads["w_ffn_up"] = run(_mm_tn, "ffn_up_dw", hn2, dup, c, bm=bt, bk=d // 2, out_dtype=ex.grad_dtype)
    dhn2_g = run(_mm_nt, "ffn_gate_dx", dgate, wg3, bm=bm, bo=d // 2, out_dtype=F32)
    dhn2_u = run(_mm_nt, "ffn_up_dx", dup, wu3, bm=bm, bo=d // 2, out_dtype=F32)
    dx2, dmix, dg_pre_ffn, dg_post_mix = run(_mid_bwd, "mid_bwd", dhn2_g, dhn2_u, dout, x2, mixo, g_pre_ffn, g_post_mix)
    dy = run(_mm_nt, "out_proj_dx", dmix, wout[None], bm=bm, bo=mix, out_dtype=F32)
    ex.grads["w_out"] = _mm_tn(y, dmix, 1, bm=bt, bk=mix // 4, out_dtype=ex.grad_dtype, name="out_proj_dw").reshape(c, mix // c, d)
    do, dyl, dga, dgl_norm = run(_outnorm_bwd, "outnorm_bwd", dy, o, yl, ga, gl)
    dproj, dcw, dcb, dwr, dbr, dwi, dbi, dlam = run(_lru_bwd, "lru_bwd", proj, 3 * n_heads, n_blocks, h, kept, dyl, cw, cb, wr, wi, lam)
    small = dict(post_mix_norm=dg_post_mix, pre_ffn_norm=dg_pre_ffn, post_ffn_norm=dg_post_ffn, conv_w=dcw, conv_b=dcb,
                 w_rgate=dwr, b_rgate=dbr, w_igate=dwi, b_igate=dbi, lru_lambda=dlam, attn_out_norm=dga, lru_out_norm=dgl_norm)
    ex.packs["early"] = _pack([small[n] for n in _SMALL_EARLY])
    dproj = run(_attn_bwd, "attn_bwd", proj_mx, do, dproj, n_heads)
    ex.grads["w_in"] = _mm_tn(hn1, dproj, c, bm=bt, bk=d // 2, out_dtype=ex.grad_dtype, name="in_proj_dw")
    ex.alone("grads_w_in_swap")
    dhn1 = run(_mm_nt, "in_proj_dx", dproj, win3, bm=bm, bo=d // 2, out_dtype=F32)
    grad_x, small["pre_mix_norm"] = run(_first_bwd, "first_bwd", dhn1, dx2, x, g_pre_mix)
    ex.packs["late"] = _pack([small["pre_mix_norm"], (0.5 / d) * jnp.sum(loss_cols, keepdims=True)])
    return loss_cols, grad_x, small


def _into_slot(wsh, slot, dtype, name):
    rows, n = wsh.shape
    rb = _row_block(rows, 256) if rows % 8 == 0 else rows

    def body(s_ref, w_ref, o_ref):
        o_ref[...] = w_ref[...].astype(o_ref.dtype)

    return pl.pallas_call(
        body,
        grid_spec=pltpu.PrefetchScalarGridSpec(
            num_scalar_prefetch=1, grid=(rows // rb,),
            in_specs=[pl.BlockSpec((rb, n), lambda i, s_ref: (i, 0))],
            out_specs=pl.BlockSpec((None, rb, n), lambda i, s_ref: (s_ref[0], i, 0))),
        out_shape=S((4, rows, n), dtype), compiler_params=_cp("parallel"), name=name)(slot, wsh)


class _Exchange:
    SCHEDULE = {
        "in_proj": [("stream", "w_in"), ("ici", "conv_w"), ("ici", "w_ffn_up", 0)],
        "attn_fwd": [("d2d", "w_ffn_up", 0), ("ici", "w_ffn_gate")],
        "lru_fwd": [("d2d", "w_ffn_gate"), ("ici", "w_out"), ("ici", "w_ffn_up", 1)],
        "outnorm_fwd": [("d2d", "w_out"), ("d2d", "w_ffn_up", 1)],
        "out_proj": [("ici", "w_ffn_up", 2)],
        "mid_fwd": [("d2d", "w_ffn_up", 2), ("ici", "w_ffn_up", 3)],
        "gather_w_up_last": [("d2d", "w_ffn_up", 3)],
        "ffn_gate_up": [("ici", "w_ffn_down")],
        "gather_w_down": [("d2d", "w_ffn_down")],
        "ffn_gate_dw": [("swap", "w_ffn_down")],
        "ffn_up_dw": [("scatter", "w_ffn_down", 0), ("scatter", "w_ffn_down", 1), ("scatter", "w_ffn_down", 2), ("swap", "w_ffn_gate")],
        "ffn_gate_dx": [("scatter", "w_ffn_down", 3), ("scatter", "w_ffn_gate", 0), ("scatter", "w_ffn_gate", 1), ("swap", "w_ffn_up")],
        "ffn_up_dx": [("share", "w_ffn_down"), ("scatter", "w_ffn_gate", 2), ("scatter", "w_ffn_gate", 3), ("scatter", "w_ffn_up", 0)],
        "mid_bwd": [("share", "w_ffn_gate"), ("scatter", "w_ffn_up", 1), ("scatter", "w_ffn_up", 2)],
        "out_proj_dx": [("scatter", "w_ffn_up", 3)],
        "outnorm_bwd": [("share", "w_ffn_up"), ("swap", "w_out")],
        "lru_bwd": [("scatter", "w_out")],
        "attn_bwd": [("share", "w_out"), ("spread", "early")],
        "grads_w_in_swap": [("swap", "w_in")],
        "in_proj_dx": [("scatter", "w_in")],
        "grads_w_in_share": [("share", "w_in"), ("spread", "late")],
    }
    PIECES = 4
    grad_dtype = BF16

    def __init__(self, slots, place):
        self.buf, self.place = dict(slots), place
        self.grads, self.packs, self.swapped, self.part, self.scattered, self.full, self.spreaded = {}, {}, {}, {}, {}, {}, {}

    def weight(self, name):
        b = self.buf[name]
        return b.reshape(-1, b.shape[2]) if name in ("w_out", "w_ffn_down") else b

    def in_proj(self, x, gain, bm):
        car = self.carrier("in_proj")
        out = _in_proj_streamed(x, gain, car, car.streamed, self.place, bm=bm, name="in_proj")
        self.harvest(car)
        return out

    def conv_w(self):
        return jnp.transpose(self.buf["conv_w"], (1, 0, 2)).reshape(CONV_WIDTH, -1)

    def carrier(self, call):
        if call not in self.SCHEDULE:
            return None
        car = _Carrier()
        car.todo, slot = [], {}
        for kind, name, *piece in self.SCHEDULE[call]:
            if kind in ("ici", "d2d", "stream"):
                if name not in slot:
                    slot[name] = car.inplace(self.buf[name])
                    car.todo.append((self.buf, name, slot[name]))
            if kind == "stream":
                car.streamed = slot[name]
            elif kind in ("ici", "d2d"):
                size = self.buf[name].shape[1] // 2 // self.PIECES
                rows = (piece[0] * size, size) if piece else None
                if kind == "ici":
                    car.gather_ici(slot[name], rows, split=name != "conv_w")
                else:
                    car.gather_d2d(slot[name], rows)
            elif kind == "swap":
                g = self.grads[name]
                o = car.fresh((4, g.shape[1] // 2, g.shape[2]), g.dtype)
                car.swap(car.read(g), o)
                car.todo.append((self.swapped, name, o))
            elif kind == "scatter":
                if name not in self.part:
                    self.part[name] = _add_own_half(self.grads[name], self.swapped[name], self.place[1:], "grads_add_" + name)
                p = self.part[name]
                key = ("scatter", name)
                if key not in slot:
                    slot[key] = (car.read(p), car.inplace(self.scattered[name]) if name in self.scattered else car.fresh(p.shape, p.dtype))
                    car.todo.append((self.scattered, name, slot[key][1]))
                size = p.shape[1] // self.PIECES
                car.scatter(*slot[key], (piece[0] * size, size) if piece else None)
            elif kind == "share":
                o = car.inplace(_sum_chips(self.part[name], self.scattered[name], self.place, "grads_sum_" + name))
                car.share(o)
                car.todo.append((self.full, name, o))
            else:
                o = car.fresh((8,) + self.packs[name].shape, F32)
                car.spread(car.read(self.packs[name]), o)
                car.todo.append((self.spreaded, name, o))
        return car

    def harvest(self, car):
        for state, name, o in (car.todo if car is not None else []):
            state[name] = car.results[o]

    def alone(self, call):
        car = self.carrier(call)
        car.run_alone(call)
        self.harvest(car)

    def small_sum(self, key):
        return _sum_devices(self.packs[key], self.spreaded[key], 2 * self.place[0:1] + self.place[1:], "grads_small_sum_" + key)


def _row_block(rows, cap):
    return max(b for b in range(8, cap + 1, 8) if rows % b == 0)


def _add_own_half(g, recv, core, name):
    _, rows, n = g.shape
    half = rows // 2
    rb = _row_block(half, 512)
    nb = half // rb

    def body(c_ref, g_ref, r_ref, o_ref):
        o_ref[...] = (g_ref[...].astype(F32) + r_ref[...].astype(F32)).astype(o_ref.dtype)

    return pl.pallas_call(
        body,
        grid_spec=pltpu.PrefetchScalarGridSpec(
            num_scalar_prefetch=1, grid=(4, nb),
            in_specs=[pl.BlockSpec((None, rb, n), lambda k, i, c_ref: (k, c_ref[0] * nb + i, 0)),
                      pl.BlockSpec((None, rb, n), lambda k, i, c_ref: (k, i, 0))],
            out_specs=pl.BlockSpec((None, rb, n), lambda k, i, c_ref: (k, i, 0))),
        out_shape=S((4, half, n), BF16), compiler_params=_cp("parallel", "parallel"), name=name)(core, g, recv)


def _sum_chips(part, recv, place, name):
    _, rows, n = part.shape
    rb = _row_block(rows, 64)
    nb = rows // rb

    def body(p_ref, own_ref, r0, r1, r2, r3, o_ref):
        own = own_ref[...].astype(F32)
        terms = [jnp.where(p_ref[0] == k, own, r[...].astype(F32)) for k, r in enumerate((r0, r1, r2, r3))]
        o_ref[...] = ((terms[0] + terms[1]) + terms[2]) + terms[3]

    def slot(k):
        return pl.BlockSpec((None, rb, n), lambda i, p_ref: (jnp.where(p_ref[0] == k, (k + 1) % 4, k), i, 0))

    return pl.pallas_call(
        body,
        grid_spec=pltpu.PrefetchScalarGridSpec(
            num_scalar_prefetch=1, grid=(nb,),
            in_specs=[pl.BlockSpec((None, rb, n), lambda i, p_ref: (p_ref[0], i, 0))] + [slot(k) for k in range(4)],
            out_specs=pl.BlockSpec((rb, n), lambda i, p_ref: (p_ref[1] * nb + i, 0))),
        out_shape=S((2 * rows, n), F32), compiler_params=_cp("parallel"), name=name)(place, part, recv, recv, recv, recv)


def _sum_devices(own, spread, me, name):
    rows = own.shape[0]

    def body(me_ref, own_ref, *refs):
        acc = None
        for k, r in enumerate(refs[:8]):
            term = jnp.where(me_ref[0] == k, own_ref[...], r[...])
            acc = term if acc is None else acc + term
        refs[8][...] = acc

    def slot(k):
        return pl.BlockSpec((None, rows, 128), lambda i, me_ref: (jnp.where(me_ref[0] == k, (k + 1) % 8, k), 0, 0))

    whole = pl.BlockSpec((rows, 128), lambda i, me_ref: (0, 0))
    return pl.pallas_call(
        body,
        grid_spec=pltpu.PrefetchScalarGridSpec(num_scalar_prefetch=1, grid=(1,), in_specs=[whole] + [slot(k) for k in range(8)],
                                               out_specs=whole),
        out_shape=S((rows, 128), F32), compiler_params=_cp("arbitrary"), name=name)(me, own, *[spread] * 8)


def _adamw(w, g, m, v, name, regive=False):
    rows, n = w.shape
    rb = rows if rows * n * 4 <= (1 << 21) else _row_block(rows, 256)
    c1 = 1.0 - ADAM_B1 ** ADAM_STEP
    c2 = 1.0 - ADAM_B2 ** ADAM_STEP

    def body(w_ref, g_ref, m_ref, v_ref, d_ref, nm_ref, nv_ref, *again):
        gv = g_ref[...]
        for ref in again:
            ref[...] = gv
        nm = ADAM_B1 * m_ref[...] + (1.0 - ADAM_B1) * gv
        nv = ADAM_B2 * v_ref[...] + (1.0 - ADAM_B2) * (gv * gv)
        nm_ref[...] = nm
        nv_ref[...] = nv
        d_ref[...] = -ADAM_LR * ((nm / c1) / (jnp.sqrt(nv / c2) + ADAM_EPS) + ADAM_WD * w_ref[...])

    bs = pl.BlockSpec((rb, n), lambda i: (i, 0))
    n_out = 4 if regive else 3
    return pl.pallas_call(body, grid=(rows // rb,), in_specs=[bs] * 4, out_specs=[bs] * n_out, out_shape=[S((rows, n), F32)] * n_out,
                          compiler_params=_cp("parallel"), name=name)(w, g, m, v)


_BIG = ("w_in", "w_out", "w_ffn_gate", "w_ffn_up", "w_ffn_down")
_SMALL = ("pre_mix_norm", "post_mix_norm", "pre_ffn_norm", "post_ffn_norm", "conv_w", "conv_b", "w_rgate", "b_rgate",
          "w_igate", "b_igate", "lru_lambda", "attn_out_norm", "lru_out_norm")
_SMALL_EARLY = _SMALL[1:]
_WEIGHTS = ("pre_mix_norm", "post_mix_norm", "pre_ffn_norm", "post_ffn_norm", "w_in", "conv_w", "conv_b", "w_rgate", "b_rgate",
            "w_igate", "b_igate", "lru_lambda", "attn_out_norm", "lru_out_norm", "w_out", "w_ffn_gate", "w_ffn_up", "w_ffn_down")


def _pack(arrays):
    flat = []
    for a in arrays:
        f = a.reshape(-1)
        flat.append(jnp.pad(f, (0, (-f.shape[0]) % 1024)))
    return jnp.concatenate(flat).reshape(-1, 128)


def _unpack(packed, shapes):
    out, pos = [], 0
    flat = packed.reshape(-1)
    for s in shapes:
        size = math.prod(s)
        out.append(flat[pos:pos + size].reshape(s))
        pos += size + (-size) % 1024
    return out


def kernel(x, pre_mix_norm, post_mix_norm, pre_ffn_norm, post_ffn_norm, w_in, conv_w, conv_b, w_rgate, b_rgate, w_igate, b_igate, lru_lambda, attn_out_norm, lru_out_norm, w_out, w_ffn_gate, w_ffn_up, w_ffn_down, loss_target, m_pre_mix_norm, m_post_mix_norm, m_pre_ffn_norm, m_post_ffn_norm, m_w_in, m_conv_w, m_conv_b, m_w_rgate, m_b_rgate, m_w_igate, m_b_igate, m_lru_lambda, m_attn_out_norm, m_lru_out_norm, m_w_out, m_w_ffn_gate, m_w_ffn_up, m_w_ffn_down, v_pre_mix_norm, v_post_mix_norm, v_pre_ffn_norm, v_post_ffn_norm, v_w_in, v_conv_w, v_conv_b, v_w_rgate, v_b_rgate, v_w_igate, v_b_igate, v_lru_lambda, v_attn_out_norm, v_lru_out_norm, v_w_out, v_w_ffn_gate, v_w_ffn_up, v_w_ffn_down):
    given = dict(locals())
    w = {n: given[n][0] for n in _WEIGHTS}
    m = {n: given["m_" + n][0] for n in _WEIGHTS}
    v = {n: given["v_" + n][0] for n in _WEIGHTS}
    xs, target = x[0], loss_target[0]
    d = xs.shape[1]
    chip = (2 * lax.axis_index("x") + lax.axis_index("y")).astype(jnp.int32)
    place = jnp.stack([chip, lax.axis_index("c").astype(jnp.int32)])

    slots = {n: _into_slot(w[n], place[0:1], _MXU, "slot_" + n) for n in _BIG}
    slots["conv_w"] = _into_slot(w["conv_w"], place[0:1], F32, "slot_conv_w")
    ex = _Exchange(slots, place)
    row = lambda a: a.reshape(1, -1)
    norms = tuple(row(w[n]) for n in ("pre_mix_norm", "post_mix_norm", "pre_ffn_norm", "post_ffn_norm"))

    loss_cols, grad_x, small = _local_step(
        xs, target, norms, ex, row(w["conv_b"]), w["w_rgate"], row(w["b_rgate"]),
        w["w_igate"], row(w["b_igate"]), row(w["lru_lambda"]), row(w["attn_out_norm"]), row(w["lru_out_norm"]))


    ex.alone("grads_w_in_share")
    reduced = {n: ex.full[n] for n in _BIG}
    early = _unpack(ex.small_sum("early"), [small[n].shape for n in _SMALL_EARLY])
    late = _unpack(ex.small_sum("late"), [small["pre_mix_norm"].shape, (1, 1)])
    loss = late[1][0, 0]
    for n, g in zip(_SMALL_EARLY + ("pre_mix_norm",), early + late[:1]):
        reduced[n] = g.reshape(w[n].shape) if n != "conv_w" else lax.dynamic_slice_in_dim(g, chip * w[n].shape[1], w[n].shape[1], axis=1)

    delta, new_m, new_v = {}, {}, {}
    for n in _BIG:
        delta[n], new_m[n], new_v[n], reduced[n] = _adamw(w[n], reduced[n], m[n], v[n], "adamw_" + n, regive=True)
    shapes = [w[n].shape for n in _SMALL]
    packed = _adamw(*[_pack([src[n] for n in _SMALL]) for src in (w, reduced, m, v)], "adamw_small")
    for out, p in zip((delta, new_m, new_v), packed):
        out.update(zip(_SMALL, _unpack(p, shapes)))

    lead = lambda a: a[None]
    return (loss, lead(grad_x), *[lead(reduced[n]) for n in _WEIGHTS], *[lead(delta[n]) for n in _WEIGHTS],
            *[lead(new_m[n]) for n in _WEIGHTS], *[lead(new_v[n]) for n in _WEIGHTS])
```

```python
import functools
import math

import jax
import jax.numpy as jnp
from jax import lax
from jax.experimental import pallas as pl
from jax.experimental.pallas import tpu as pltpu

F32 = jnp.float32
BF16 = jnp.bfloat16
_MXU = BF16
S = jax.ShapeDtypeStruct

RMS_EPS = 1e-6
HEAD_DIM = 128
CONV_WIDTH = 4
LRU_C = 8.0
ADAM_LR, ADAM_B1, ADAM_B2, ADAM_EPS, ADAM_WD, ADAM_STEP = 0.001, 0.9, 0.999, 1e-08, 0.01, 10
EXP_CUT = -105.0
VMEM_LIMIT = 60 * 1024 * 1024
ROW_TILE = 512
SEQ_TILE = 256
ATTN_BLOCK = 256
ATTN_HEADS = 2
MM_ROWS = 512
DW_TOKENS = 2048
DW_ROWS = 512
MESH = pl.DeviceIdType.MESH


def _cp(*sem):
    return pltpu.CompilerParams(dimension_semantics=sem, vmem_limit_bytes=VMEM_LIMIT)


def _dot(a, b):
    return jnp.dot(a, b, preferred_element_type=F32)


def _dot_nt(a, b):
    return lax.dot_general(a, b, (((1,), (1,)), ((), ())), preferred_element_type=F32)


def _dot_tn(a, b):
    return lax.dot_general(a, b, (((0,), (0,)), ((), ())), preferred_element_type=F32)


def _rstd(v):
    return lax.rsqrt(jnp.mean(v * v, axis=-1, keepdims=True) + RMS_EPS)


def _rms_bwd(dn, vh, r, gain):
    dvh = dn * gain
    dv = r * (dvh - vh * jnp.mean(dvh * vh, axis=-1, keepdims=True))
    return dv, jnp.sum(dn * vh, axis=0, keepdims=True)


def _log_sigmoid(z):
    return jnp.minimum(z, 0.0) - jnp.log(1.0 + jnp.exp(-jnp.abs(z)))


def _expm1(v):
    small = v * (1.0 + v * (0.5 + v * (1.0 / 6.0 + v * (1.0 / 24.0 + v * (1.0 / 120.0)))))
    return jnp.where(jnp.abs(v) < 0.04, small, jnp.exp(v) - 1.0)


_GELU_C = math.sqrt(2.0 / math.pi)


def _gelu(v):
    return 0.5 * v * (1.0 + jnp.tanh(_GELU_C * (v + 0.044715 * v * v * v)))


def _gelu_grad(v):
    th = jnp.tanh(_GELU_C * (v + 0.044715 * v * v * v))
    return 0.5 * (1.0 + th) + 0.5 * v * (1.0 - th * th) * _GELU_C * (1.0 + 3.0 * 0.044715 * v * v)


def _row_spec(tm, d):
    return pl.BlockSpec((tm, d), lambda i: (i, 0))


def _vec_spec(d):
    return pl.BlockSpec((1, d), lambda i: (0, 0))


_ANY = pl.BlockSpec(memory_space=pl.ANY)


def _place():
    x, y, c = lax.axis_index("x"), lax.axis_index("y"), lax.axis_index("c")
    return x, y, c, [(1 - x, y), (x, 1 - y), (1 - x, 1 - y)]


def _remote(src, dst, send_sem, recv_sem, to):
    return pltpu.make_async_remote_copy(src_ref=src, dst_ref=dst, send_sem=send_sem, recv_sem=recv_sem,
                                        device_id=to, device_id_type=MESH)


class _Carrier:
    def __init__(self):
        self.inputs, self.out_shapes, self.aliases, self.ops, self.n_sems, self.results = [], [], {}, [], 0, None

    def inplace(self, arr):
        self.aliases[len(self.inputs)] = len(self.out_shapes)
        self.inputs.append(arr)
        self.out_shapes.append(S(arr.shape, arr.dtype))
        return len(self.out_shapes) - 1

    def read(self, arr):
        self.inputs.append(arr)
        return len(self.inputs) - 1

    def fresh(self, shape, dtype):
        self.out_shapes.append(S(shape, dtype))
        return len(self.out_shapes) - 1

    def _add(self, n_sems, copies):
        base = self.n_sems
        self.n_sems += n_sems

        def start(ins, outs, send, recv):
            for k, (src, dst, _, to) in enumerate(copies(ins, outs)):
                _remote(src, dst, send.at[base + k], recv.at[base + k], to).start()

        def finish(ins, outs, send, recv):
            for k, (src, _, land, to) in enumerate(copies(ins, outs)):
                _remote(src, land, send.at[base + k], recv.at[base + k], to).wait()

        self.ops.append((start, finish))

    def gather_ici(self, o, rows=None, split=True):
        half = self.out_shapes[o].shape[1] // 2
        lo, size = rows or (0, half)

        def copies(ins, outs):
            x, y, c, chips = _place()
            part = (lambda ref: ref.at[pl.ds(c * half + lo, size)]) if split else (lambda ref: ref)
            mine = part(outs[o].at[2 * x + y])
            return [(mine, mine, part(outs[o].at[2 * px + py]), (px, py, c)) for px, py in chips]

        self._add(3, copies)

    def gather_d2d(self, o, rows=None):
        half = self.out_shapes[o].shape[1] // 2
        lo, size = rows or (0, half)

        def copies(ins, outs):
            x, y, c, chips = _place()
            at = lambda k, cc: outs[o].at[k].at[pl.ds(cc * half + lo, size)]
            return [(at(2 * px + py, c), at(2 * px + py, c), at(2 * px + py, 1 - c), (x, y, 1 - c)) for px, py in chips]

        self._add(3, copies)

    def swap(self, i, o):
        half = self.inputs[i].shape[1] // 2

        def copies(ins, outs):
            x, y, c, _ = _place()
            return [(ins[i].at[:, pl.ds((1 - c) * half, half)], outs[o], outs[o], (x, y, 1 - c))]

        self._add(1, copies)

    def scatter(self, i, o, rows=None):
        lo, size = rows or (0, self.inputs[i].shape[1])

        def copies(ins, outs):
            x, y, c, chips = _place()
            cut = lambda ref: ref.at[pl.ds(lo, size)]
            return [(cut(ins[i].at[2 * px + py]), cut(outs[o].at[2 * x + y]), cut(outs[o].at[2 * px + py]), (px, py, c)) for px, py in chips]

        self._add(3, copies)

    def share(self, o):
        r = self.out_shapes[o].shape[0] // 2

        def copies(ins, outs):
            x, y, c, _ = _place()
            mine = outs[o].at[pl.ds(c * r, r)]
            return [(mine, mine, outs[o].at[pl.ds((1 - c) * r, r)], (x, y, 1 - c))]

        self._add(1, copies)

    def spread(self, i, o):
        def copies(ins, outs):
            x, y, c, _ = _place()
            me = 4 * x + 2 * y + c
            out = []
            for d in range(1, 8):
                to, frm = (me + d) % 8, (me + 8 - d) % 8
                out.append((ins[i], outs[o].at[me], outs[o].at[frm], (to // 4, (to // 2) % 2, to % 2)))
            return out

        self._add(7, copies)

    def _pallas(self, body, n_in, n_out, scratch, **kw):
        k_in, k_out = len(self.inputs), len(self.out_shapes)
        grid = kw.get("grid", ())

        def wrapped(*refs):
            ins, cins = refs[:n_in], refs[n_in:n_in + k_in]
            outs = refs[n_in + k_in:n_in + k_in + n_out]
            couts = refs[n_in + k_in + n_out:n_in + k_in + n_out + k_out]
            own = refs[n_in + k_in + n_out + k_out:]
            send, recv = own[len(scratch):]
            ids = [pl.program_id(a) for a in range(len(grid))]
            first = functools.reduce(jnp.logical_and, [a == 0 for a in ids], True)
            last = functools.reduce(jnp.logical_and, [a == g - 1 for a, g in zip(ids, grid)], True)

            def go(stage):
                for op in self.ops:
                    op[stage](cins, couts, send, recv)

            if grid:
                pl.when(first)(lambda: go(0))
                body(*ins, *outs, *own[:len(scratch)])
                pl.when(last)(lambda: go(1))
            else:
                go(0)
                go(1)

        sem = pltpu.SemaphoreType.DMA((self.n_sems,))
        return pl.pallas_call(
            wrapped, in_specs=list(kw.get("in_specs", [])) + [_ANY] * k_in, out_specs=list(kw.get("out_specs", [])) + [_ANY] * k_out,
            out_shape=list(kw.get("out_shape", [])) + self.out_shapes, scratch_shapes=list(scratch) + [sem, sem],
            input_output_aliases={**kw.get("aliases", {}), **{n_in + i: n_out + o for i, o in self.aliases.items()}}, name=kw["name"],
            **({"grid": grid, "compiler_params": _cp(*["arbitrary"] * len(grid))} if grid else {}))

    def run(self, body, kw, *args):
        single = not isinstance(kw["out_shape"], (list, tuple))
        out_shape = [kw["out_shape"]] if single else list(kw["out_shape"])
        out_specs = [kw["out_specs"]] if single else list(kw["out_specs"])
        res = self._pallas(body, len(args), len(out_shape), kw.get("scratch_shapes", []), grid=kw["grid"], in_specs=kw["in_specs"],
                           out_specs=out_specs, out_shape=out_shape, name=kw["name"],
                           aliases=kw.get("input_output_aliases", {}))(*args, *self.inputs)
        self.results = list(res[len(out_shape):])
        return res[0] if single else list(res[:len(out_shape)])

    def run_alone(self, name):
        self.results = list(self._pallas(None, 0, 0, [], name=name)(*self.inputs))


def _call(comm, body, **kw):
    if comm is None:
        return pl.pallas_call(body, **kw)
    return functools.partial(comm.run, body, kw)


def _in_proj_streamed(x, gain, car, o_w, place, *, bm, name):
    m, k = x.shape
    n = car.out_shapes[o_w].shape[2]
    ni, half = m // bm, k // 2
    k_in, k_out = len(car.inputs), len(car.out_shapes)
    order = lambda p: ((p & 1) << 1) | (p >> 1)

    def body(place_ref, x_ref, g_ref, *refs):
        cins, (hn_ref, o_ref, ob_ref), couts = refs[:k_in], refs[k_in:k_in + 3], refs[k_in + 3:k_in + 3 + k_out]
        wbuf, hn_all, local, ici_send, ici_recv, d2d_send, d2d_recv, send, recv = refs[k_in + 3 + k_out:]
        p, i = pl.program_id(0), pl.program_id(1)
        x, y, c, chips = _place()
        me = 2 * x + y
        rows = lambda chunk, cc: couts[o_w].at[chunk].at[pl.ds(cc * half, half)]

        @pl.when(jnp.logical_and(p == 0, i == 0))
        def _():
            for j, (px, py) in enumerate(chips):
                _remote(rows(me, c), rows(me, c), ici_send.at[j], ici_recv.at[j], (px, py, c)).start()
            for op in car.ops:
                op[0](cins, couts, send, recv)

        for j, (px, py) in enumerate(chips):
            @pl.when(jnp.logical_and(p == j + 1, i == 0))
            def _(j=j, px=px, py=py):
                landed, other = rows(2 * px + py, c), rows(2 * px + py, 1 - c)
                _remote(landed, landed, ici_send.at[j], ici_recv.at[j], (px, py, c)).wait_recv()
                _remote(landed, landed, d2d_send.at[j], d2d_recv.at[j], (x, y, 1 - c)).start()
                _remote(other, other, d2d_send.at[j], d2d_recv.at[j], (x, y, 1 - c)).wait_recv()

        @pl.when(i == 0)
        def _():
            cp = pltpu.make_async_copy(couts[o_w].at[me ^ order(p)], wbuf, local.at[0])
            cp.start()
            cp.wait()

        tile = pl.ds(pl.multiple_of(i * bm, bm), bm)

        @pl.when(p == 0)
        def _():
            xv = x_ref[...]
            hn_all[tile, :] = ((xv * _rstd(xv)) * g_ref[...]).astype(_MXU)

        hn = hn_all[tile, :]
        hn_ref[...] = hn
        res = _dot(hn, wbuf[...])
        o_ref[...] = res
        ob_ref[...] = res.astype(ob_ref.dtype)

        @pl.when(jnp.logical_and(p == 3, i == ni - 1))
        def _():
            for j, (px, py) in enumerate(chips):
                _remote(rows(me, c), rows(me, c), ici_send.at[j], ici_recv.at[j], (px, py, c)).wait_send()
                _remote(rows(me, c), rows(me, c), d2d_send.at[j], d2d_recv.at[j], (x, y, 1 - c)).wait_send()
            for op in car.ops:
                op[1](cins, couts, send, recv)

    ospec = pl.BlockSpec((bm, n), lambda p, i, place_ref: (i, place_ref[0] ^ order(p)))
    rows = pl.BlockSpec((bm, k), lambda p, i, place_ref: (jnp.where(p == 0, i, 0), 0))
    three, sems = pltpu.SemaphoreType.DMA((3,)), pltpu.SemaphoreType.DMA((max(car.n_sems, 1),))
    res = pl.pallas_call(
        body,
        grid_spec=pltpu.PrefetchScalarGridSpec(
            num_scalar_prefetch=1, grid=(4, ni),
            in_specs=[rows, pl.BlockSpec((1, k), lambda p, i, place_ref: (0, 0))] + [_ANY] * k_in,
            out_specs=[pl.BlockSpec((bm, k), lambda p, i, place_ref: (p * ni + i, 0)), ospec, ospec] + [_ANY] * k_out,
            scratch_shapes=[pltpu.VMEM((k, n), _MXU), pltpu.VMEM((m, k), _MXU), pltpu.SemaphoreType.DMA((1,)),
                            three, three, three, three, sems, sems]),
        out_shape=[S((4 * m, k), _MXU), S((m, 4 * n), F32), S((m, 4 * n), _MXU)] + car.out_shapes,
        input_output_aliases={3 + a: 3 + o for a, o in car.aliases.items()},
        compiler_params=_cp("arbitrary", "arbitrary"), name=name)(place, x, gain, *car.inputs)
    car.results = list(res[3:])
    return res[0], res[1], res[2]


def _mm_nn(a, b3, *, bm, bn, name, also=None, comm=None):
    m, k = a.shape
    c, _, n = b3.shape
    ni, nj = m // bm, n // bn

    def body(a_ref, b_ref, *o_refs):
        res = _dot(a_ref[...], b_ref[...])
        for o_ref in o_refs:
            o_ref[...] = res.astype(o_ref.dtype)

    ospec = pl.BlockSpec((bm, bn), lambda cc, j, i: (i, cc * nj + j))
    dtypes = [F32] + ([] if also is None else [also])
    out = _call(
        comm, body, grid=(c, nj, ni),
        in_specs=[pl.BlockSpec((bm, k), lambda cc, j, i: (i, 0)), pl.BlockSpec((None, k, bn), lambda cc, j, i: (cc, 0, j))],
        out_specs=[ospec] * len(dtypes), out_shape=[S((m, c * n), dt) for dt in dtypes],
        compiler_params=_cp("parallel", "parallel", "parallel"), name=name)(a, b3)
    return out[0] if also is None else out


def _mm_nt(a, b3, *, bm, bo, out_dtype, name, comm=None):
    m = a.shape[0]
    c, ko, n = b3.shape
    ni, nj = m // bm, ko // bo

    def body(a_ref, b_ref, o_ref):
        acc = _dot_nt(a_ref[:, 0:n], b_ref[0])
        for cc in range(1, c):
            acc = acc + _dot_nt(a_ref[:, cc * n:(cc + 1) * n], b_ref[cc])
        o_ref[...] = acc.astype(o_ref.dtype)

    return _call(
        comm, body, grid=(nj, ni),
        in_specs=[pl.BlockSpec((bm, c * n), lambda j, i: (i, 0)),
                  pl.BlockSpec((c, bo, n), lambda j, i: (0, j, 0))],
        out_specs=pl.BlockSpec((bm, bo), lambda j, i: (i, j)),
        out_shape=S((m, ko), out_dtype),
        compiler_params=_cp("parallel", "parallel"), name=name)(a, b3)


def _mm_tn(a, b, c, *, bm, bk, out_dtype, name, comm=None):
    m, k = b.shape[0], a.shape[1]
    n = b.shape[1] // c
    nm, nk = m // bm, k // bk

    def body(a_ref, b_ref, o_ref, acc):
        mm = pl.program_id(2)

        @pl.when(mm == 0)
        def _():
            acc[...] = jnp.zeros_like(acc)

        acc[...] += _dot_tn(a_ref[...], b_ref[...])

        @pl.when(mm == nm - 1)
        def _():
            o_ref[...] = acc[...].astype(o_ref.dtype)

    return _call(
        comm, body, grid=(c, nk, nm),
        in_specs=[pl.BlockSpec((bm, bk), lambda cc, j, mm: (mm, j)),
                  pl.BlockSpec((bm, n), lambda cc, j, mm: (mm, cc))],
        out_specs=pl.BlockSpec((None, bk, n), lambda cc, j, mm: (cc, j, 0)),
        out_shape=S((c, k, n), out_dtype),
        scratch_shapes=[pltpu.VMEM((bk, n), F32)],
        compiler_params=_cp("parallel", "parallel", "arbitrary"), name=name)(a, b)


def _swiglu_fwd(hn, wg3, wu3, *, bm, name, comm=None):
    m, k = hn.shape
    c, _, n = wg3.shape

    def body(a_ref, g_ref, u_ref, dgate_ref, dup_ref, act_ref):
        a = a_ref[...]
        gate = _dot(a, g_ref[...])
        up = _dot(a, u_ref[...])
        sg = jax.nn.sigmoid(gate)
        silu = gate * sg
        dgate_ref[...] = (up * (sg * (1.0 + gate * (1.0 - sg)))).astype(dgate_ref.dtype)
        dup_ref[...] = silu.astype(dup_ref.dtype)
        act_ref[...] = (silu * up).astype(act_ref.dtype)

    wspec = pl.BlockSpec((None, k, n), lambda cc, i: (cc, 0, 0))
    ospec = pl.BlockSpec((bm, n), lambda cc, i: (i, cc))
    return _call(
        comm, body, grid=(c, m // bm),
        in_specs=[pl.BlockSpec((bm, k), lambda cc, i: (i, 0)), wspec, wspec],
        out_specs=[ospec, ospec, ospec],
        out_shape=[S((m, c * n), _MXU), S((m, c * n), _MXU), S((m, c * n), _MXU)],
        compiler_params=_cp("parallel", "parallel"), name=name)(hn, wg3, wu3)


def _swiglu_bwd(df, wd, act_dgate, act_dup, *, bm, bo, name):
    m, k = df.shape
    ko = wd.shape[0]

    def body(a_ref, b_ref, g_ref, u_ref, dg_ref, du_ref):
        dact = _dot_nt(a_ref[...], b_ref[...])
        dg_ref[...] = (dact * g_ref[...].astype(F32)).astype(dg_ref.dtype)
        du_ref[...] = (dact * u_ref[...].astype(F32)).astype(du_ref.dtype)

    ospec = pl.BlockSpec((bm, bo), lambda j, i: (i, j))
    return pl.pallas_call(
        body, grid=(ko // bo, m // bm),
        in_specs=[pl.BlockSpec((bm, k), lambda j, i: (i, 0)), pl.BlockSpec((bo, k), lambda j, i: (j, 0)), ospec, ospec],
        out_specs=[ospec, ospec],
        out_shape=[S((m, ko), _MXU), S((m, ko), _MXU)],
        compiler_params=_cp("parallel", "parallel"), name=name)(df, wd, act_dgate, act_dup)


def _rms_fwd(x, gain, name):
    t, d = x.shape
    tm = min(t, ROW_TILE)

    def body(x_ref, g_ref, o_ref):
        xv = x_ref[...]
        o_ref[...] = ((xv * _rstd(xv)) * g_ref[...]).astype(o_ref.dtype)

    return pl.pallas_call(body, grid=(t // tm,), in_specs=[_row_spec(tm, d), _vec_spec(d)], out_specs=_row_spec(tm, d),
                          out_shape=S((t, d), _MXU), compiler_params=_cp("parallel"), name=name)(x, gain)


def _outnorm_fwd(o, yl, ga, gl, name, comm=None):
    t, w = o.shape
    tm = min(t, ROW_TILE)

    def body(o_ref, l_ref, ga_ref, gl_ref, y_ref):
        ov, lv = o_ref[...], l_ref[...]
        y_ref[:, :w] = ((ov * _rstd(ov)) * ga_ref[...]).astype(y_ref.dtype)
        y_ref[:, w:] = ((lv * _rstd(lv)) * gl_ref[...]).astype(y_ref.dtype)

    return _call(comm, body, grid=(t // tm,), in_specs=[_row_spec(tm, w), _row_spec(tm, w), _vec_spec(w), _vec_spec(w)],
                 out_specs=_row_spec(tm, 2 * w), out_shape=S((t, 2 * w), _MXU),
                 compiler_params=_cp("parallel"), name=name)(o, yl, ga, gl)


def _mid_fwd(x, mix, g_post, g_pre, name, comm=None):
    t, d = x.shape
    tm = min(t, ROW_TILE)

    def body(x_ref, m_ref, gp_ref, gn_ref, x2_ref, hn_ref):
        mv = m_ref[...]
        x2 = x_ref[...] + (mv * _rstd(mv)) * gp_ref[...]
        x2_ref[...] = x2
        hn_ref[...] = ((x2 * _rstd(x2)) * gn_ref[...]).astype(hn_ref.dtype)

    return _call(comm, body, grid=(t // tm,), in_specs=[_row_spec(tm, d), _row_spec(tm, d), _vec_spec(d), _vec_spec(d)],
                          out_specs=[_row_spec(tm, d), _row_spec(tm, d)], out_shape=[S((t, d), F32), S((t, d), _MXU)],
                          compiler_params=_cp("parallel"), name=name)(x, mix, g_post, g_pre)


def _final(f, x2, target, g_post, name):
    t, d = f.shape
    tm = min(t, ROW_TILE // 2)

    def body(f_ref, x2_ref, t_ref, g_ref, loss_ref, dout_ref, df_ref, dg_ref):
        @pl.when(pl.program_id(0) == 0)
        def _():
            loss_ref[...] = jnp.zeros_like(loss_ref)
            dg_ref[...] = jnp.zeros_like(dg_ref)

        fv = f_ref[...]
        r = _rstd(fv)
        fh = fv * r
        err = (x2_ref[...] + fh * g_ref[...]) - t_ref[...]
        loss_ref[...] += jnp.sum(err * err, axis=0, keepdims=True)
        dout = err * (1.0 / d)
        dout_ref[...] = dout
        dfv, dg = _rms_bwd(dout, fh, r, g_ref[...])
        df_ref[...] = dfv.astype(df_ref.dtype)
        dg_ref[...] += dg

    return pl.pallas_call(
        body, grid=(t // tm,),
        in_specs=[_row_spec(tm, d), _row_spec(tm, d), _row_spec(tm, d), _vec_spec(d)],
        out_specs=[_vec_spec(d), _row_spec(tm, d), _row_spec(tm, d), _vec_spec(d)],
        out_shape=[S((1, d), F32), S((t, d), F32), S((t, d), _MXU), S((1, d), F32)],
        compiler_params=_cp("arbitrary"), name=name)(f, x2, target, g_post)


def _mid_bwd(dhn_a, dhn_b, dout, x2, mix, g_pre, g_post, name, comm=None):
    t, d = x2.shape
    tm = min(t, ROW_TILE // 2)

    def body(da_ref, db_ref, do_ref, x2_ref, m_ref, gn_ref, gp_ref, dx2_ref, dm_ref, dgn_ref, dgp_ref):
        @pl.when(pl.program_id(0) == 0)
        def _():
            dgn_ref[...] = jnp.zeros_like(dgn_ref)
            dgp_ref[...] = jnp.zeros_like(dgp_ref)

        x2 = x2_ref[...]
        r = _rstd(x2)
        dxa, dgn = _rms_bwd(da_ref[...] + db_ref[...], x2 * r, r, gn_ref[...])
        dx2 = do_ref[...] + dxa
        dx2_ref[...] = dx2
        dgn_ref[...] += dgn
        mv = m_ref[...]
        rm = _rstd(mv)
        dmv, dgp = _rms_bwd(dx2, mv * rm, rm, gp_ref[...])
        dm_ref[...] = dmv.astype(dm_ref.dtype)
        dgp_ref[...] += dgp

    rs, vs = _row_spec(tm, d), _vec_spec(d)
    return _call(
        comm, body, grid=(t // tm,), in_specs=[rs, rs, rs, rs, rs, vs, vs], out_specs=[rs, rs, vs, vs],
        out_shape=[S((t, d), F32), S((t, d), _MXU), S((1, d), F32), S((1, d), F32)],
        compiler_params=_cp("arbitrary"), name=name)(dhn_a, dhn_b, dout, x2, mix, g_pre, g_post)


def _first_bwd(dhn, dx2, x, gain, name, comm=None):
    t, d = x.shape
    tm = min(t, ROW_TILE)

    def body(dh_ref, dx2_ref, x_ref, g_ref, dx_ref, dg_ref):
        @pl.when(pl.program_id(0) == 0)
        def _():
            dg_ref[...] = jnp.zeros_like(dg_ref)

        xv = x_ref[...]
        r = _rstd(xv)
        dxa, dg = _rms_bwd(dh_ref[...], xv * r, r, g_ref[...])
        dx_ref[...] = dx2_ref[...] + dxa
        dg_ref[...] += dg

    rs, vs = _row_spec(tm, d), _vec_spec(d)
    return _call(comm, body, grid=(t // tm,), in_specs=[rs, rs, rs, vs], out_specs=[rs, vs],
                          out_shape=[S((t, d), F32), S((1, d), F32)], compiler_params=_cp("arbitrary"), name=name)(dhn, dx2, x, gain)


def _outnorm_bwd(dy, o, yl, ga, gl, name, comm=None):
    t, w = o.shape
    tm = min(t, ROW_TILE)

    def body(dy_ref, o_ref, l_ref, ga_ref, gl_ref, do_ref, dl_ref, dga_ref, dgl_ref):
        @pl.when(pl.program_id(0) == 0)
        def _():
            dga_ref[...] = jnp.zeros_like(dga_ref)
            dgl_ref[...] = jnp.zeros_like(dgl_ref)

        ov, lv = o_ref[...], l_ref[...]
        ra, rl = _rstd(ov), _rstd(lv)
        dov, dga = _rms_bwd(dy_ref[:, :w], ov * ra, ra, ga_ref[...])
        dlv, dgl = _rms_bwd(dy_ref[:, w:], lv * rl, rl, gl_ref[...])
        do_ref[...] = dov.astype(do_ref.dtype)
        dl_ref[...] = dlv
        dga_ref[...] += dga
        dgl_ref[...] += dgl

    rs, vs = _row_spec(tm, w), _vec_spec(w)
    return _call(comm, body, grid=(t // tm,), in_specs=[_row_spec(tm, 2 * w), rs, rs, vs, vs], out_specs=[rs, rs, vs, vs],
                          out_shape=[S((t, w), _MXU), S((t, w), F32), S((1, w), F32), S((1, w), F32)],
                          compiler_params=_cp("arbitrary"), name=name)(dy, o, yl, ga, gl)


def _tri_sum(v, tri):
    return _dot(v.astype(_MXU), tri)


def _attn_tile(qb, kb, row, col, shift, scale):
    z = _dot_nt(qb, kb) * scale
    mask = (col + shift) < row
    lb = _log_sigmoid(z)
    lm = jnp.where(mask, lb - z, 0.0)
    return mask, lb, lm


def _attn_fwd(proj, n_heads, name, comm=None):
    t = proj.shape[0]
    bq = min(t, ATTN_BLOCK)
    nq = t // bq
    scale = 1.0 / math.sqrt(HEAD_DIM)

    heads = [slice(a * HEAD_DIM, (a + 1) * HEAD_DIM) for a in range(ATTN_HEADS)]

    def body(q_ref, k_ref, v_ref, o_ref):
        row = lax.broadcasted_iota(jnp.int32, (bq, bq), 0)
        col = lax.broadcasted_iota(jnp.int32, (bq, bq), 1)
        tri = (row > col).astype(_MXU)

        def per_q(qi, _):
            q0 = pl.multiple_of(qi * bq, bq)
            qbs = [q_ref[pl.ds(q0, bq), hd] for hd in heads]

            def cond(st):
                return jnp.logical_and(st[0] >= 0, st[1])

            def step(st):
                kj, _, carries, accs = st
                k0 = pl.multiple_of(kj * bq, bq)
                alive, new_carries, new_accs = None, [], []
                for hd, qb, carry, acc in zip(heads, qbs, carries, accs):
                    mask, lb, lm = _attn_tile(qb, k_ref[pl.ds(k0, bq), hd], row, col, (kj - qi) * bq, scale)
                    w = jnp.where(mask, jnp.exp(lb + _tri_sum(lm, tri) + carry), 0.0)
                    new_accs.append(acc + _dot(w.astype(_MXU), v_ref[pl.ds(k0, bq), hd]))
                    carry = carry + jnp.sum(lm, axis=1, keepdims=True)
                    new_carries.append(carry)
                    live = jnp.max(carry) > EXP_CUT
                    alive = live if alive is None else jnp.logical_or(alive, live)
                return kj - 1, alive, tuple(new_carries), tuple(new_accs)

            st = lax.while_loop(cond, step, (qi, jnp.bool_(True), (jnp.zeros((bq, 1), F32),) * ATTN_HEADS,
                                             (jnp.zeros((bq, HEAD_DIM), F32),) * ATTN_HEADS))
            for hd, acc in zip(heads, st[3]):
                o_ref[pl.ds(q0, bq), hd] = acc
            return 0

        lax.fori_loop(0, nq, per_q, 0)

    groups = n_heads // ATTN_HEADS
    hs = lambda off: pl.BlockSpec((t, ATTN_HEADS * HEAD_DIM), lambda h: (0, off + h))
    return _call(
        comm, body, grid=(groups,), in_specs=[hs(0), hs(groups), hs(2 * groups)], out_specs=hs(0),
        out_shape=S((t, n_heads * HEAD_DIM), F32), compiler_params=_cp("parallel"), name=name)(proj, proj, proj)


def _emit(blocks, out_ref, starts, sems):
    copies = [pltpu.make_async_copy(b, out_ref.at[:, pl.ds(c0, b.shape[1])], sems.at[k]) for k, (b, c0) in enumerate(zip(blocks, starts))]
    for cp in copies:
        cp.start()
    for cp in copies:
        cp.wait()


def _attn_bwd(proj, do, dproj, n_heads, name, comm=None):
    t = proj.shape[0]
    bq = min(t, ATTN_BLOCK)
    nq = t // bq
    scale = 1.0 / math.sqrt(HEAD_DIM)
    groups = n_heads // ATTN_HEADS
    wide = ATTN_HEADS * HEAD_DIM

    heads = [slice(a * HEAD_DIM, (a + 1) * HEAD_DIM) for a in range(ATTN_HEADS)]

    def body(q_ref, k_ref, v_ref, do_ref, _, dproj_ref, dka_ref, dva_ref, g_ref, b_ref, dq_ref, dk_ref, dv_ref, out_sems):
        group = pl.program_id(0)
        dka_ref[...] = jnp.zeros_like(dka_ref)
        dva_ref[...] = jnp.zeros_like(dva_ref)
        row = lax.broadcasted_iota(jnp.int32, (bq, bq), 0)
        col = lax.broadcasted_iota(jnp.int32, (bq, bq), 1)
        tri = (row > col).astype(_MXU)
        tri_lt = (row < col).astype(_MXU)

        def per_q(qi, _):
            q0 = pl.multiple_of(qi * bq, bq)
            qbs = [q_ref[pl.ds(q0, bq), hd] for hd in heads]
            dobs = [do_ref[pl.ds(q0, bq), hd] for hd in heads]

            def cond(st):
                return jnp.logical_and(st[0] >= 0, st[1])

            def step(st):
                kj, _, carries = st
                k0 = pl.multiple_of(kj * bq, bq)
                alive, new_carries = None, []
                for a, (hd, qb, dob, carry) in enumerate(zip(heads, qbs, dobs, carries)):
                    mask, lb, lm = _attn_tile(qb, k_ref[pl.ds(k0, bq), hd], row, col, (kj - qi) * bq, scale)
                    w = jnp.where(mask, jnp.exp(lb + _tri_sum(lm, tri) + carry), 0.0)
                    g_ref[a, pl.ds(k0, bq), :] = w * _dot_nt(dob, v_ref[pl.ds(k0, bq), hd])
                    b_ref[a, pl.ds(k0, bq), :] = jnp.where(mask, jnp.exp(lb), 0.0)
                    dva_ref[pl.ds(k0, bq), hd] += _dot_tn(w.astype(_MXU), dob)
                    carry = carry + jnp.sum(lm, axis=1, keepdims=True)
                    new_carries.append(carry)
                    live = jnp.max(carry) > EXP_CUT
                    alive = live if alive is None else jnp.logical_or(alive, live)
                return kj - 1, alive, tuple(new_carries)

            st = lax.while_loop(cond, step, (qi, jnp.bool_(True), (jnp.zeros((bq, 1), F32),) * ATTN_HEADS))

            def back(kj, st2):
                k0 = pl.multiple_of(kj * bq, bq)
                out = []
                for a, (hd, qb, (before, dq)) in enumerate(zip(heads, qbs, st2)):
                    g = g_ref[a, pl.ds(k0, bq), :]
                    beta = b_ref[a, pl.ds(k0, bq), :]
                    dz = ((g * (1.0 - beta) - (before + _tri_sum(g, tri_lt)) * beta) * scale).astype(_MXU)
                    dka_ref[pl.ds(k0, bq), hd] += _dot_tn(dz, qb)
                    out.append((before + jnp.sum(g, axis=1, keepdims=True), dq + _dot(dz, k_ref[pl.ds(k0, bq), hd])))
                return tuple(out)

            st2 = lax.fori_loop(st[0] + 1, qi + 1, back, ((jnp.zeros((bq, 1), F32), jnp.zeros((bq, HEAD_DIM), F32)),) * ATTN_HEADS)
            for hd, (_, dq) in zip(heads, st2):
                dq_ref[pl.ds(q0, bq), hd] = dq.astype(dq_ref.dtype)
            return 0

        lax.fori_loop(0, nq, per_q, 0)
        dk_ref[...] = dka_ref[...].astype(dk_ref.dtype)
        dv_ref[...] = dva_ref[...].astype(dv_ref.dtype)
        _emit([dq_ref, dk_ref, dv_ref], dproj_ref, [(a * groups + group) * wide for a in range(3)], out_sems)

    hs = lambda off: pl.BlockSpec((t, wide), lambda h: (0, off + h))
    return _call(
        comm, body, grid=(groups,), in_specs=[hs(0), hs(groups), hs(2 * groups), hs(0), _ANY], out_specs=_ANY,
        out_shape=S(dproj.shape, dproj.dtype), input_output_aliases={4: 0},
        scratch_shapes=[pltpu.VMEM((t, wide), F32), pltpu.VMEM((t, wide), F32),
                        pltpu.VMEM((ATTN_HEADS, t, bq), F32), pltpu.VMEM((ATTN_HEADS, t, bq), F32)]
        + [pltpu.VMEM((t, wide), dproj.dtype)] * 3 + [pltpu.SemaphoreType.DMA((3,))],
        compiler_params=_cp("parallel"), name=name)(proj, proj, proj, do, dproj)


def _shift_down(cur, prev8, k):
    if k == 0:
        return cur
    row8 = lax.broadcasted_iota(jnp.int32, prev8.shape, 0)
    rc = pltpu.roll(cur, k, 0)
    top = jnp.where(row8 < k, pltpu.roll(prev8, k, 0), rc[0:8, :])
    return jnp.concatenate([top, rc[8:, :]], axis=0)


def _shift_up(cur, next8, k):
    if k == 0:
        return cur
    n = cur.shape[0]
    row8 = lax.broadcasted_iota(jnp.int32, next8.shape, 0)
    rc = pltpu.roll(cur, n - k, 0)
    bottom = jnp.where(row8 >= 8 - k, pltpu.roll(next8, 8 - k, 0), rc[n - 8:, :])
    return jnp.concatenate([rc[:n - 8, :], bottom], axis=0)


def _lru_conv(xl, prev8, cw, cb):
    xs = [_shift_down(xl, prev8, CONV_WIDTH - 1 - k) for k in range(CONV_WIDTH)]
    xc = xs[0] * cw[0:1, :]
    for k in range(1, CONV_WIDTH):
        xc = xc + xs[k] * cw[k:k + 1, :]
    return xs, xc + cb


def _lru_gates(xl, prev8, cw, cb, wr, br, wi, bi, ls):
    xs, xc = _lru_conv(xl, prev8, cw, cb)
    xcb = xc.astype(_MXU)
    r = jax.nn.sigmoid(_dot(xcb, wr) + br)
    i = jax.nn.sigmoid(_dot(xcb, wi) + bi)
    la = (LRU_C * r) * ls
    a = jnp.exp(la)
    mult = jnp.sqrt(-_expm1(2.0 * la))
    return xs, xc, r, i, a, mult


def _group_scan(a, b, reverse):
    n = a.shape[0]
    row = lax.broadcasted_iota(jnp.int32, a.shape, 0) % 8
    for d in (1, 2, 4):
        if reverse:
            m = row < 8 - d
            a_s, b_s = pltpu.roll(a, n - d, 0), pltpu.roll(b, n - d, 0)
        else:
            m = row >= d
            a_s, b_s = pltpu.roll(a, d, 0), pltpu.roll(b, d, 0)
        b = jnp.where(m, a * b_s + b, b)
        a = jnp.where(m, a * a_s, a)
    return a, b


def _lru_fwd(proj, col0, n_blocks, cw, cb, wr, br, wi, bi, lam, name, comm=None):
    t = proj.shape[0]
    tt = min(t, SEQ_TILE)
    nt = t // tt

    def body(xl_ref, gl_ref, cw_ref, cb_ref, wr_ref, br_ref, wi_ref, bi_ref, lam_ref, h_ref, y_ref, *kept):
        cwv, cbv, brv, biv = cw_ref[...], cb_ref[...], br_ref[...], bi_ref[...]
        wrv, wiv = wr_ref[...].astype(_MXU), wi_ref[...].astype(_MXU)
        ls = _log_sigmoid(lam_ref[...])

        def tile(ti, hin):
            t0 = pl.multiple_of(ti * tt, tt)
            p0 = pl.multiple_of(jnp.maximum(t0 - 8, 0), 8)
            prev8 = xl_ref[pl.ds(p0, 8), :] * (ti > 0).astype(F32)
            xl = xl_ref[pl.ds(t0, tt), :]
            _, xc, r, ig, a, mult = _lru_gates(xl, prev8, cwv, cbv, wrv, brv, wiv, biv, ls)
            for ref, val in zip(kept, (r, ig, a, mult)):
                ref[pl.ds(t0, tt), :] = val
            ga, gb = _group_scan(a, mult * (ig * xc), False)
            for g in range(tt // 8):
                hg = ga[8 * g:8 * g + 8, :] * hin + gb[8 * g:8 * g + 8, :]
                h_ref[pl.ds(t0 + 8 * g, 8), :] = hg
                hin = hg[7:8, :]
            y_ref[pl.ds(t0, tt), :] = h_ref[pl.ds(t0, tt), :] * _gelu(gl_ref[pl.ds(t0, tt), :])
            return hin

        lax.fori_loop(0, nt, tile, jnp.zeros((1, HEAD_DIM), F32))

    cs = lambda off: pl.BlockSpec((t, HEAD_DIM), lambda n: (0, off + n))
    vs = pl.BlockSpec((1, HEAD_DIM), lambda n: (0, n))
    ws = pl.BlockSpec((None, HEAD_DIM, HEAD_DIM), lambda n: (n, 0, 0))
    w = n_blocks * HEAD_DIM
    return _call(
        comm, body, grid=(n_blocks,),
        in_specs=[cs(col0), cs(col0 + n_blocks), pl.BlockSpec((CONV_WIDTH, HEAD_DIM), lambda n: (0, n)), vs, ws, vs, ws, vs, vs],
        out_specs=[cs(0)] * 6, out_shape=[S((t, w), F32)] * 6,
        compiler_params=_cp("parallel"), name=name)(proj, proj, cw, cb, wr, br, wi, bi, lam)


def _lru_bwd(proj, col0, n_blocks, h, kept, dyl, cw, cb, wr, wi, lam, name, comm=None):
    t = proj.shape[0]
    tt = min(t, SEQ_TILE)
    nt = t // tt

    def body(xl_ref, gl_ref, h_ref, r_ref, i_ref, a_ref, m_ref, dy_ref, cw_ref, cb_ref, wr_ref, wi_ref, lam_ref,
             dproj_ref, dcw_ref, dcb_ref, dwr_ref, dbr_ref, dwi_ref, dbi_ref, dlam_ref, g_ref, dxl_ref, dgl_ref, out_sems):
        block = pl.program_id(0)
        cwv, cbv = cw_ref[...], cb_ref[...]
        wrv, wiv = wr_ref[...].astype(_MXU), wi_ref[...].astype(_MXU)
        lamv = lam_ref[...]
        ls = _log_sigmoid(lamv)
        for ref in (dcw_ref, dcb_ref, dwr_ref, dbr_ref, dwi_ref, dbi_ref, dlam_ref):
            ref[...] = jnp.zeros_like(ref)

        def tile(s, carry):
            e_in, dxc_next8 = carry
            ti = nt - 1 - s
            t0 = pl.multiple_of(ti * tt, tt)
            p0 = pl.multiple_of(jnp.maximum(t0 - 8, 0), 8)
            first = (ti > 0).astype(F32)
            xl = xl_ref[pl.ds(t0, tt), :]
            xs, xc = _lru_conv(xl, xl_ref[pl.ds(p0, 8), :] * first, cwv, cbv)
            r, ig, a, mult = (ref[pl.ds(t0, tt), :] for ref in (r_ref, i_ref, a_ref, m_ref))
            hv = h_ref[pl.ds(t0, tt), :]
            h_before = _shift_down(hv, h_ref[pl.ds(p0, 8), :] * first, 1)
            glv = gl_ref[pl.ds(t0, tt), :]
            dyv = dy_ref[pl.ds(t0, tt), :]
            dgl_ref[pl.ds(t0, tt), :] = (dyv * hv * _gelu_grad(glv)).astype(dgl_ref.dtype)
            dh = dyv * _gelu(glv)
            row = lax.broadcasted_iota(jnp.int32, a.shape, 0)
            coef = jnp.where(row == tt - 1, 1.0, pltpu.roll(a, tt - 1, 0))
            ga, gb = _group_scan(coef, dh, True)
            gin = e_in
            for g in reversed(range(tt // 8)):
                gg = ga[8 * g:8 * g + 8, :] * gin + gb[8 * g:8 * g + 8, :]
                g_ref[8 * g:8 * g + 8, :] = gg
                gin = gg[0:1, :]
            gv = g_ref[...]
            e_out = a[0:1, :] * gv[0:1, :]
            ix = ig * xc
            dla = (gv * h_before) * a - (gv * ix) * (a * a / mult)
            dlam_ref[...] += jnp.sum(dla * (LRU_C * r), axis=0, keepdims=True)
            dpr = (dla * (LRU_C * ls)) * (r * (1.0 - r))
            dpi = (gv * mult * xc) * (ig * (1.0 - ig))
            dbr_ref[...] += jnp.sum(dpr, axis=0, keepdims=True)
            dbi_ref[...] += jnp.sum(dpi, axis=0, keepdims=True)
            xcb, dprb, dpib = xc.astype(_MXU), dpr.astype(_MXU), dpi.astype(_MXU)
            dwr_ref[...] += _dot_tn(xcb, dprb)
            dwi_ref[...] += _dot_tn(xcb, dpib)
            dxc = gv * mult * ig + _dot_nt(dprb, wrv) + _dot_nt(dpib, wiv)
            dcb_ref[...] += jnp.sum(dxc, axis=0, keepdims=True)
            dxl = None
            for k in range(CONV_WIDTH):
                dcw_ref[k:k + 1, :] += jnp.sum(dxc * xs[k], axis=0, keepdims=True)
                term = _shift_up(dxc, dxc_next8, CONV_WIDTH - 1 - k) * cwv[k:k + 1, :]
                dxl = term if dxl is None else dxl + term
            dxl_ref[pl.ds(t0, tt), :] = dxl.astype(dxl_ref.dtype)
            return e_out, dxc[0:8, :]

        lax.fori_loop(0, nt, tile, (jnp.zeros((1, HEAD_DIM), F32), jnp.zeros((8, HEAD_DIM), F32)))
        dlam_ref[...] = dlam_ref[...] * (1.0 - jax.nn.sigmoid(lamv))
        _emit([dxl_ref, dgl_ref], dproj_ref, [(col0 + block) * HEAD_DIM, (col0 + n_blocks + block) * HEAD_DIM], out_sems)

    cs = lambda off: pl.BlockSpec((t, HEAD_DIM), lambda n: (0, off + n))
    vs = pl.BlockSpec((1, HEAD_DIM), lambda n: (0, n))
    ws = pl.BlockSpec((None, HEAD_DIM, HEAD_DIM), lambda n: (n, 0, 0))
    cws = pl.BlockSpec((CONV_WIDTH, HEAD_DIM), lambda n: (0, n))
    w = n_blocks * HEAD_DIM
    vec = S((1, w), F32)
    mat = S((n_blocks, HEAD_DIM, HEAD_DIM), F32)
    return _call(
        comm, body, grid=(n_blocks,),
        in_specs=[cs(col0), cs(col0 + n_blocks)] + [cs(0)] * 6 + [cws, vs, ws, ws, vs],
        out_specs=[_ANY, cws, vs, ws, vs, ws, vs, vs],
        out_shape=[S(proj.shape, _MXU), S((CONV_WIDTH, w), F32), vec, mat, vec, mat, vec, vec],
        scratch_shapes=[pltpu.VMEM((tt, HEAD_DIM), F32), pltpu.VMEM((t, HEAD_DIM), _MXU), pltpu.VMEM((t, HEAD_DIM), _MXU),
                        pltpu.SemaphoreType.DMA((2,))],
        compiler_params=_cp("parallel"), name=name)(proj, proj, h, *kept, dyl, cw, cb, wr, wi, lam)


class _NoExchange:
    grad_dtype = F32

    def __init__(self, weights):
        self.weights, self.grads, self.packs = weights, {}, {}

    def weight(self, name):
        return self.weights[name]

    def in_proj(self, x, gain, bm):
        hn = _rms_fwd(x, gain, "rms1")
        return [hn, *_mm_nn(hn, self.weights["w_in"], bm=bm, bn=self.weights["w_in"].shape[2], name="in_proj", also=_MXU)]

    def conv_w(self):
        return self.weights["conv_w"]

    def carrier(self, call):
        return None

    def harvest(self, car):
        pass

    def alone(self, call):
        pass


def _local_step(x, target, norms, ex, cb, wr, br, wi, bi, lam, ga, gl):
    g_pre_mix, g_post_mix, g_pre_ffn, g_post_ffn = norms
    t, d = x.shape
    bm = min(t, MM_ROWS)
    bt = min(t, DW_TOKENS)

    def run(fn, name, *args, **kw):
        car = ex.carrier(name)
        out = fn(*args, name=name, comm=car, **kw)
        ex.harvest(car)
        return out

    hn1, proj, proj_mx = ex.in_proj(x, g_pre_mix, bm)
    win3, cw = ex.weight("w_in"), ex.conv_w()
    c = win3.shape[0]
    o = run(_attn_fwd, "attn_fwd", proj_mx, (proj.shape[1] - d) // 3 // HEAD_DIM)
    mix = 2 * o.shape[1]
    n_heads = n_blocks = o.shape[1] // HEAD_DIM
    h, yl, *kept = run(_lru_fwd, "lru_fwd", proj, 3 * n_heads, n_blocks, cw, cb, wr, br, wi, bi, lam)
    y = run(_outnorm_fwd, "outnorm_fwd", o, yl, ga, gl)
    wout = ex.weight("w_out")
    mixo = run(_mm_nn, "out_proj", y, wout[None], bm=bm, bn=d)
    x2, hn2 = run(_mid_fwd, "mid_fwd", x, mixo, g_post_mix, g_pre_ffn)
    ex.alone("gather_w_up_last")
    wg3, wu3 = ex.weight("w_ffn_gate"), ex.weight("w_ffn_up")
    act_dgate, act_dup, act = run(_swiglu_fwd, "ffn_gate_up", hn2, wg3, wu3, bm=bm)
    ex.alone("gather_w_down")
    wd = ex.weight("w_ffn_down")
    ff = wd.shape[0]
    f = _mm_nn(act, wd[None], bm=bm, bn=d // 2, name="ffn_down")
    loss_cols, dout, df, dg_post_ffn = _final(f, x2, target, g_post_ffn, "final")

    dgate, dup = _swiglu_bwd(df, wd, act_dgate, act_dup, bm=min(t, 2 * MM_ROWS), bo=ff // 4, name="ffn_down_bwd")
    ex.grads["w_ffn_down"] = _mm_tn(act, df, 1, bm=bt, bk=DW_ROWS, out_dtype=ex.grad_dtype, name="ffn_down_dw").reshape(c, ff // c, d)
    ex.grads["w_ffn_gate"] = run(_mm_tn, "ffn_gate_dw", hn2, dgate, c, bm=bt, bk=d // 2, out_dtype=ex.grad_dtype)
    ex.grads["w_ffn_up"] = run(_mm_tn, "ffn_up_dw", hn2, dup, c, bm=bt, bk=d // 2, out_dtype=ex.grad_dtype)
    dhn2_g = run(_mm_nt, "ffn_gate_dx", dgate, wg3, bm=bm, bo=d // 2, out_dtype=F32)
    dhn2_u = run(_mm_nt, "ffn_up_dx", dup, wu3, bm=bm, bo=d // 2, out_dtype=F32)
    dx2, dmix, dg_pre_ffn, dg_post_mix = run(_mid_bwd, "mid_bwd", dhn2_g, dhn2_u, dout, x2, mixo, g_pre_ffn, g_post_mix)
    dy = run(_mm_nt, "out_proj_dx", dmix, wout[None], bm=bm, bo=mix, out_dtype=F32)
    ex.grads["w_out"] = _mm_tn(y, dmix, 1, bm=bt, bk=mix // 4, out_dtype=ex.grad_dtype, name="out_proj_dw").reshape(c, mix // c, d)
    do, dyl, dga, dgl_norm = run(_outnorm_bwd, "outnorm_bwd", dy, o, yl, ga, gl)
    dproj, dcw, dcb, dwr, dbr, dwi, dbi, dlam = run(_lru_bwd, "lru_bwd", proj, 3 * n_heads, n_blocks, h, kept, dyl, cw, cb, wr, wi, lam)
    small = dict(post_mix_norm=dg_post_mix, pre_ffn_norm=dg_pre_ffn, post_ffn_norm=dg_post_ffn, conv_w=dcw, conv_b=dcb,
                 w_rgate=dwr, b_rgate=dbr, w_igate=dwi, b_igate=dbi, lru_lambda=dlam, attn_out_norm=dga, lru_out_norm=dgl_norm)
    ex.packs["early"] = _pack([small[n] for n in _SMALL_EARLY])
    dproj = run(_attn_bwd, "attn_bwd", proj_mx, do, dproj, n_heads)
    ex.grads["w_in"] = _mm_tn(hn1, dproj, c, bm=bt, bk=d // 2, out_dtype=ex.grad_dtype, name="in_proj_dw")
    ex.alone("grads_w_in_swap")
    dhn1 = run(_mm_nt, "in_proj_dx", dproj, win3, bm=bm, bo=d // 2, out_dtype=F32)
    grad_x, small["pre_mix_norm"] = run(_first_bwd, "first_bwd", dhn1, dx2, x, g_pre_mix)
    ex.packs["late"] = _pack([small["pre_mix_norm"], (0.5 / d) * jnp.sum(loss_cols, keepdims=True)])
    return loss_cols, grad_x, small


def _into_slot(wsh, slot, dtype, name):
    rows, n = wsh.shape
    rb = _row_block(rows, 512) if rows % 8 == 0 else rows

    def body(s_ref, w_ref, o_ref):
        o_ref[...] = w_ref[...].astype(o_ref.dtype)

    return pl.pallas_call(
        body,
        grid_spec=pltpu.PrefetchScalarGridSpec(
            num_scalar_prefetch=1, grid=(rows // rb,),
            in_specs=[pl.BlockSpec((rb, n), lambda i, s_ref: (i, 0))],
            out_specs=pl.BlockSpec((None, rb, n), lambda i, s_ref: (s_ref[0], i, 0))),
        out_shape=S((4, rows, n), dtype), compiler_params=_cp("parallel"), name=name)(slot, wsh)


class _Exchange:
    SCHEDULE = {
        "in_proj": [("stream", "w_in"), ("ici", "conv_w"), ("ici", "w_ffn_up", 0)],
        "attn_fwd": [("d2d", "w_ffn_up", 0), ("ici", "w_ffn_gate")],
        "lru_fwd": [("d2d", "w_ffn_gate"), ("ici", "w_out"), ("ici", "w_ffn_up", 1)],
        "outnorm_fwd": [("d2d", "w_out"), ("d2d", "w_ffn_up", 1)],
        "out_proj": [("ici", "w_ffn_up", 2)],
        "mid_fwd": [("d2d", "w_ffn_up", 2), ("ici", "w_ffn_up", 3)],
        "gather_w_up_last": [("d2d", "w_ffn_up", 3)],
        "ffn_gate_up": [("ici", "w_ffn_down")],
        "gather_w_down": [("d2d", "w_ffn_down")],
        "ffn_gate_dw": [("swap", "w_ffn_down")],
        "ffn_up_dw": [("scatter", "w_ffn_down", 0), ("scatter", "w_ffn_down", 1), ("scatter", "w_ffn_down", 2), ("swap", "w_ffn_gate")],
        "ffn_gate_dx": [("scatter", "w_ffn_down", 3), ("scatter", "w_ffn_gate", 0), ("scatter", "w_ffn_gate", 1), ("swap", "w_ffn_up")],
        "ffn_up_dx": [("share", "w_ffn_down"), ("scatter", "w_ffn_gate", 2), ("scatter", "w_ffn_gate", 3), ("scatter", "w_ffn_up", 0)],
        "mid_bwd": [("share", "w_ffn_gate"), ("scatter", "w_ffn_up", 1), ("scatter", "w_ffn_up", 2)],
        "out_proj_dx": [("scatter", "w_ffn_up", 3)],
        "outnorm_bwd": [("share", "w_ffn_up"), ("swap", "w_out")],
        "lru_bwd": [("scatter", "w_out")],
        "attn_bwd": [("share", "w_out"), ("spread", "early")],
        "grads_w_in_swap": [("swap", "w_in")],
        "in_proj_dx": [("scatter", "w_in")],
        "grads_w_in_share": [("share", "w_in"), ("spread", "late")],
    }
    PIECES = 4
    grad_dtype = BF16

    def __init__(self, slots, place):
        self.buf, self.place = dict(slots), place
        self.grads, self.packs, self.swapped, self.part, self.scattered, self.full, self.spreaded = {}, {}, {}, {}, {}, {}, {}

    def weight(self, name):
        b = self.buf[name]
        return b.reshape(-1, b.shape[2]) if name in ("w_out", "w_ffn_down") else b

    def in_proj(self, x, gain, bm):
        car = self.carrier("in_proj")
        out = _in_proj_streamed(x, gain, car, car.streamed, self.place, bm=bm, name="in_proj")
        self.harvest(car)
        return out

    def conv_w(self):
        return jnp.transpose(self.buf["conv_w"], (1, 0, 2)).reshape(CONV_WIDTH, -1)

    def carrier(self, call):
        if call not in self.SCHEDULE:
            return None
        car = _Carrier()
        car.todo, slot = [], {}
        for kind, name, *piece in self.SCHEDULE[call]:
            if kind in ("ici", "d2d", "stream"):
                if name not in slot:
                    slot[name] = car.inplace(self.buf[name])
                    car.todo.append((self.buf, name, slot[name]))
            if kind == "stream":
                car.streamed = slot[name]
            elif kind in ("ici", "d2d"):
                size = self.buf[name].shape[1] // 2 // self.PIECES
                rows = (piece[0] * size, size) if piece else None
                if kind == "ici":
                    car.gather_ici(slot[name], rows, split=name != "conv_w")
                else:
                    car.gather_d2d(slot[name], rows)
            elif kind == "swap":
                g = self.grads[name]
                o = car.fresh((4, g.shape[1] // 2, g.shape[2]), g.dtype)
                car.swap(car.read(g), o)
                car.todo.append((self.swapped, name, o))
            elif kind == "scatter":
                if name not in self.part:
                    self.part[name] = _add_own_half(self.grads[name], self.swapped[name], self.place[1:], "grads_add_" + name)
                p = self.part[name]
                key = ("scatter", name)
                if key not in slot:
                    slot[key] = (car.read(p), car.inplace(self.scattered[name]) if name in self.scattered else car.fresh(p.shape, p.dtype))
                    car.todo.append((self.scattered, name, slot[key][1]))
                size = p.shape[1] // self.PIECES
                car.scatter(*slot[key], (piece[0] * size, size) if piece else None)
            elif kind == "share":
                o = car.inplace(_sum_chips(self.part[name], self.scattered[name], self.place, "grads_sum_" + name))
                car.share(o)
                car.todo.append((self.full, name, o))
            else:
                o = car.fresh((8,) + self.packs[name].shape, F32)
                car.spread(car.read(self.packs[name]), o)
                car.todo.append((self.spreaded, name, o))
        return car

    def harvest(self, car):
        for state, name, o in (car.todo if car is not None else []):
            state[name] = car.results[o]

    def alone(self, call):
        car = self.carrier(call)
        car.run_alone(call)
        self.harvest(car)

    def small_sum(self, key):
        return _sum_devices(self.packs[key], self.spreaded[key], 2 * self.place[0:1] + self.place[1:], "grads_small_sum_" + key)


def _row_block(rows, cap):
    return max(b for b in range(8, cap + 1, 8) if rows % b == 0)


def _add_own_half(g, recv, core, name):
    _, rows, n = g.shape
    half = rows // 2
    rb = _row_block(half, 512)
    nb = half // rb

    def body(c_ref, g_ref, r_ref, o_ref):
        o_ref[...] = (g_ref[...].astype(F32) + r_ref[...].astype(F32)).astype(o_ref.dtype)

    return pl.pallas_call(
        body,
        grid_spec=pltpu.PrefetchScalarGridSpec(
            num_scalar_prefetch=1, grid=(4, nb),
            in_specs=[pl.BlockSpec((None, rb, n), lambda k, i, c_ref: (k, c_ref[0] * nb + i, 0)),
                      pl.BlockSpec((None, rb, n), lambda k, i, c_ref: (k, i, 0))],
            out_specs=pl.BlockSpec((None, rb, n), lambda k, i, c_ref: (k, i, 0))),
        out_shape=S((4, half, n), BF16), compiler_params=_cp("parallel", "parallel"), name=name)(core, g, recv)


def _sum_chips(part, recv, place, name):
    _, rows, n = part.shape
    rb = _row_block(rows, 256)
    nb = rows // rb

    def body(p_ref, own_ref, r0, r1, r2, r3, o_ref):
        own = own_ref[...].astype(F32)
        terms = [jnp.where(p_ref[0] == k, own, r[...].astype(F32)) for k, r in enumerate((r0, r1, r2, r3))]
        o_ref[...] = ((terms[0] + terms[1]) + terms[2]) + terms[3]

    def slot(k):
        return pl.BlockSpec((None, rb, n), lambda i, p_ref: (jnp.where(p_ref[0] == k, (k + 1) % 4, k), i, 0))

    return pl.pallas_call(
        body,
        grid_spec=pltpu.PrefetchScalarGridSpec(
            num_scalar_prefetch=1, grid=(nb,),
            in_specs=[pl.BlockSpec((None, rb, n), lambda i, p_ref: (p_ref[0], i, 0))] + [slot(k) for k in range(4)],
            out_specs=pl.BlockSpec((rb, n), lambda i, p_ref: (p_ref[1] * nb + i, 0))),
        out_shape=S((2 * rows, n), F32), compiler_params=_cp("parallel"), name=name)(place, part, recv, recv, recv, recv)


def _sum_devices(own, spread, me, name):
    rows = own.shape[0]

    def body(me_ref, own_ref, *refs):
        acc = None
        for k, r in enumerate(refs[:8]):
            term = jnp.where(me_ref[0] == k, own_ref[...], r[...])
            acc = term if acc is None else acc + term
        refs[8][...] = acc

    def slot(k):
        return pl.BlockSpec((None, rows, 128), lambda i, me_ref: (jnp.where(me_ref[0] == k, (k + 1) % 8, k), 0, 0))

    whole = pl.BlockSpec((rows, 128), lambda i, me_ref: (0, 0))
    return pl.pallas_call(
        body,
        grid_spec=pltpu.PrefetchScalarGridSpec(num_scalar_prefetch=1, grid=(1,), in_specs=[whole] + [slot(k) for k in range(8)],
                                               out_specs=whole),
        out_shape=S((rows, 128), F32), compiler_params=_cp("arbitrary"), name=name)(me, own, *[spread] * 8)


def _adamw(w, g, m, v, name, regive=False):
    rows, n = w.shape
    rb = rows if rows * n * 4 <= (1 << 21) else _row_block(rows, 256)
    c1 = 1.0 - ADAM_B1 ** ADAM_STEP
    c2 = 1.0 - ADAM_B2 ** ADAM_STEP

    def body(w_ref, g_ref, m_ref, v_ref, d_ref, nm_ref, nv_ref, *again):
        gv = g_ref[...]
        for ref in again:
            ref[...] = gv
        nm = ADAM_B1 * m_ref[...] + (1.0 - ADAM_B1) * gv
        nv = ADAM_B2 * v_ref[...] + (1.0 - ADAM_B2) * (gv * gv)
        nm_ref[...] = nm
        nv_ref[...] = nv
        d_ref[...] = -ADAM_LR * ((nm / c1) / (jnp.sqrt(nv / c2) + ADAM_EPS) + ADAM_WD * w_ref[...])

    bs = pl.BlockSpec((rb, n), lambda i: (i, 0))
    n_out = 4 if regive else 3
    return pl.pallas_call(body, grid=(rows // rb,), in_specs=[bs] * 4, out_specs=[bs] * n_out, out_shape=[S((rows, n), F32)] * n_out,
                          compiler_params=_cp("parallel"), name=name)(w, g, m, v)


_BIG = ("w_in", "w_out", "w_ffn_gate", "w_ffn_up", "w_ffn_down")
_SMALL = ("pre_mix_norm", "post_mix_norm", "pre_ffn_norm", "post_ffn_norm", "conv_w", "conv_b", "w_rgate", "b_rgate",
          "w_igate", "b_igate", "lru_lambda", "attn_out_norm", "lru_out_norm")
_SMALL_EARLY = _SMALL[1:]
_WEIGHTS = ("pre_mix_norm", "post_mix_norm", "pre_ffn_norm", "post_ffn_norm", "w_in", "conv_w", "conv_b", "w_rgate", "b_rgate",
            "w_igate", "b_igate", "lru_lambda", "attn_out_norm", "lru_out_norm", "w_out", "w_ffn_gate", "w_ffn_up", "w_ffn_down")


def _pack(arrays):
    flat = []
    for a in arrays:
        f = a.reshape(-1)
        flat.append(jnp.pad(f, (0, (-f.shape[0]) % 1024)))
    return jnp.concatenate(flat).reshape(-1, 128)


def _unpack(packed, shapes):
    out, pos = [], 0
    flat = packed.reshape(-1)
    for s in shapes:
        size = math.prod(s)
        out.append(flat[pos:pos + size].reshape(s))
        pos += size + (-size) % 1024
    return out


def kernel(x, pre_mix_norm, post_mix_norm, pre_ffn_norm, post_ffn_norm, w_in, conv_w, conv_b, w_rgate, b_rgate, w_igate, b_igate, lru_lambda, attn_out_norm, lru_out_norm, w_out, w_ffn_gate, w_ffn_up, w_ffn_down, loss_target, m_pre_mix_norm, m_post_mix_norm, m_pre_ffn_norm, m_post_ffn_norm, m_w_in, m_conv_w, m_conv_b, m_w_rgate, m_b_rgate, m_w_igate, m_b_igate, m_lru_lambda, m_attn_out_norm, m_lru_out_norm, m_w_out, m_w_ffn_gate, m_w_ffn_up, m_w_ffn_down, v_pre_mix_norm, v_post_mix_norm, v_pre_ffn_norm, v_post_ffn_norm, v_w_in, v_conv_w, v_conv_b, v_w_rgate, v_b_rgate, v_w_igate, v_b_igate, v_lru_lambda, v_attn_out_norm, v_lru_out_norm, v_w_out, v_w_ffn_gate, v_w_ffn_up, v_w_ffn_down):
    given = dict(locals())
    w = {n: given[n][0] for n in _WEIGHTS}
    m = {n: given["m_" + n][0] for n in _WEIGHTS}
    v = {n: given["v_" + n][0] for n in _WEIGHTS}
    xs, target = x[0], loss_target[0]
    d = xs.shape[1]
    chip = (2 * lax.axis_index("x") + lax.axis_index("y")).astype(jnp.int32)
    place = jnp.stack([chip, lax.axis_index("c").astype(jnp.int32)])

    slots = {n: _into_slot(w[n], place[0:1], _MXU, "slot_" + n) for n in _BIG}
    slots["conv_w"] = _into_slot(w["conv_w"], place[0:1], F32, "slot_conv_w")
    ex = _Exchange(slots, place)
    row = lambda a: a.reshape(1, -1)
    norms = tuple(row(w[n]) for n in ("pre_mix_norm", "post_mix_norm", "pre_ffn_norm", "post_ffn_norm"))

    loss_cols, grad_x, small = _local_step(
        xs, target, norms, ex, row(w["conv_b"]), w["w_rgate"], row(w["b_rgate"]),
        w["w_igate"], row(w["b_igate"]), row(w["lru_lambda"]), row(w["attn_out_norm"]), row(w["lru_out_norm"]))


    ex.alone("grads_w_in_share")
    reduced = {n: ex.full[n] for n in _BIG}
    early = _unpack(ex.small_sum("early"), [small[n].shape for n in _SMALL_EARLY])
    late = _unpack(ex.small_sum("late"), [small["pre_mix_norm"].shape, (1, 1)])
    loss = late[1][0, 0]
    for n, g in zip(_SMALL_EARLY + ("pre_mix_norm",), early + late[:1]):
        reduced[n] = g.reshape(w[n].shape) if n != "conv_w" else lax.dynamic_slice_in_dim(g, chip * w[n].shape[1], w[n].shape[1], axis=1)

    delta, new_m, new_v = {}, {}, {}
    for n in _BIG:
        delta[n], new_m[n], new_v[n], reduced[n] = _adamw(w[n], reduced[n], m[n], v[n], "adamw_" + n, regive=True)
    shapes = [w[n].shape for n in _SMALL]
    packed = _adamw(*[_pack([src[n] for n in _SMALL]) for src in (w, reduced, m, v)], "adamw_small")
    for out, p in zip((delta, new_m, new_v), packed):
        out.update(zip(_SMALL, _unpack(p, shapes)))

    lead = lambda a: a[None]
    return (loss, lead(grad_x), *[lead(reduced[n]) for n in _WEIGHTS], *[lead(delta[n]) for n in _WEIGHTS],
            *[lead(new_m[n]) for n in _WEIGHTS], *[lead(new_v[n]) for n in _WEIGHTS])
```

```python
import functools
import math

import jax
import jax.numpy as jnp
from jax import lax
from jax.experimental import pallas as pl
from jax.experimental.pallas import tpu as pltpu

F32 = jnp.float32
BF16 = jnp.bfloat16
_MXU = BF16
S = jax.ShapeDtypeStruct

RMS_EPS = 1e-6
HEAD_DIM = 128
CONV_WIDTH = 4
LRU_C = 8.0
ADAM_LR, ADAM_B1, ADAM_B2, ADAM_EPS, ADAM_WD, ADAM_STEP = 0.001, 0.9, 0.999, 1e-08, 0.01, 10
EXP_CUT = -105.0
VMEM_LIMIT = 60 * 1024 * 1024
ROW_TILE = 512
SEQ_TILE = 256
ATTN_BLOCK = 256
ATTN_HEADS = 2
MM_ROWS = 512
DW_TOKENS = 2048
DW_ROWS = 512
MESH = pl.DeviceIdType.MESH


def _cp(*sem):
    return pltpu.CompilerParams(dimension_semantics=sem, vmem_limit_bytes=VMEM_LIMIT)


def _dot(a, b):
    return jnp.dot(a, b, preferred_element_type=F32)


def _dot_nt(a, b):
    return lax.dot_general(a, b, (((1,), (1,)), ((), ())), preferred_element_type=F32)


def _dot_tn(a, b):
    return lax.dot_general(a, b, (((0,), (0,)), ((), ())), preferred_element_type=F32)


def _rstd(v):
    return lax.rsqrt(jnp.mean(v * v, axis=-1, keepdims=True) + RMS_EPS)


def _rms_bwd(dn, vh, r, gain):
    dvh = dn * gain
    dv = r * (dvh - vh * jnp.mean(dvh * vh, axis=-1, keepdims=True))
    return dv, jnp.sum(dn * vh, axis=0, keepdims=True)


def _log_sigmoid(z):
    return jnp.minimum(z, 0.0) - jnp.log(1.0 + jnp.exp(-jnp.abs(z)))


def _expm1(v):
    small = v * (1.0 + v * (0.5 + v * (1.0 / 6.0 + v * (1.0 / 24.0 + v * (1.0 / 120.0)))))
    return jnp.where(jnp.abs(v) < 0.04, small, jnp.exp(v) - 1.0)


_GELU_C = math.sqrt(2.0 / math.pi)


def _gelu(v):
    return 0.5 * v * (1.0 + jnp.tanh(_GELU_C * (v + 0.044715 * v * v * v)))


def _gelu_grad(v):
    th = jnp.tanh(_GELU_C * (v + 0.044715 * v * v * v))
    return 0.5 * (1.0 + th) + 0.5 * v * (1.0 - th * th) * _GELU_C * (1.0 + 3.0 * 0.044715 * v * v)


def _row_spec(tm, d):
    return pl.BlockSpec((tm, d), lambda i: (i, 0))


def _vec_spec(d):
    return pl.BlockSpec((1, d), lambda i: (0, 0))


_ANY = pl.BlockSpec(memory_space=pl.ANY)


def _place():
    x, y, c = lax.axis_index("x"), lax.axis_index("y"), lax.axis_index("c")
    return x, y, c, [(1 - x, y), (x, 1 - y), (1 - x, 1 - y)]


def _remote(src, dst, send_sem, recv_sem, to):
    return pltpu.make_async_remote_copy(src_ref=src, dst_ref=dst, send_sem=send_sem, recv_sem=recv_sem,
                                        device_id=to, device_id_type=MESH)


class _Carrier:
    def __init__(self):
        self.inputs, self.out_shapes, self.aliases, self.ops, self.n_sems, self.results = [], [], {}, [], 0, None

    def inplace(self, arr):
        self.aliases[len(self.inputs)] = len(self.out_shapes)
        self.inputs.append(arr)
        self.out_shapes.append(S(arr.shape, arr.dtype))
        return len(self.out_shapes) - 1

    def read(self, arr):
        self.inputs.append(arr)
        return len(self.inputs) - 1

    def fresh(self, shape, dtype):
        self.out_shapes.append(S(shape, dtype))
        return len(self.out_shapes) - 1

    def _add(self, n_sems, copies):
        base = self.n_sems
        self.n_sems += n_sems

        def start(ins, outs, send, recv):
            for k, (src, dst, _, to) in enumerate(copies(ins, outs)):
                _remote(src, dst, send.at[base + k], recv.at[base + k], to).start()

        def finish(ins, outs, send, recv):
            for k, (src, _, land, to) in enumerate(copies(ins, outs)):
                _remote(src, land, send.at[base + k], recv.at[base + k], to).wait()

        self.ops.append((start, finish))

    def gather_ici(self, o, rows=None, split=True):
        half = self.out_shapes[o].shape[1] // 2
        lo, size = rows or (0, half)

        def copies(ins, outs):
            x, y, c, chips = _place()
            part = (lambda ref: ref.at[pl.ds(c * half + lo, size)]) if split else (lambda ref: ref)
            mine = part(outs[o].at[2 * x + y])
            return [(mine, mine, part(outs[o].at[2 * px + py]), (px, py, c)) for px, py in chips]

        self._add(3, copies)

    def gather_d2d(self, o, rows=None):
        half = self.out_shapes[o].shape[1] // 2
        lo, size = rows or (0, half)

        def copies(ins, outs):
            x, y, c, chips = _place()
            at = lambda k, cc: outs[o].at[k].at[pl.ds(cc * half + lo, size)]
            return [(at(2 * px + py, c), at(2 * px + py, c), at(2 * px + py, 1 - c), (x, y, 1 - c)) for px, py in chips]

        self._add(3, copies)

    def swap(self, i, o):
        half = self.inputs[i].shape[1] // 2

        def copies(ins, outs):
            x, y, c, _ = _place()
            return [(ins[i].at[:, pl.ds((1 - c) * half, half)], outs[o], outs[o], (x, y, 1 - c))]

        self._add(1, copies)

    def scatter(self, i, o, rows=None):
        lo, size = rows or (0, self.inputs[i].shape[1])

        def copies(ins, outs):
            x, y, c, chips = _place()
            cut = lambda ref: ref.at[pl.ds(lo, size)]
            return [(cut(ins[i].at[2 * px + py]), cut(outs[o].at[2 * x + y]), cut(outs[o].at[2 * px + py]), (px, py, c)) for px, py in chips]

        self._add(3, copies)

    def share(self, o):
        r = self.out_shapes[o].shape[0] // 2

        def copies(ins, outs):
            x, y, c, _ = _place()
            mine = outs[o].at[pl.ds(c * r, r)]
            return [(mine, mine, outs[o].at[pl.ds((1 - c) * r, r)], (x, y, 1 - c))]

        self._add(1, copies)

    def spread(self, i, o):
        def copies(ins, outs):
            x, y, c, _ = _place()
            me = 4 * x + 2 * y + c
            out = []
            for d in range(1, 8):
                to, frm = (me + d) % 8, (me + 8 - d) % 8
                out.append((ins[i], outs[o].at[me], outs[o].at[frm], (to // 4, (to // 2) % 2, to % 2)))
            return out

        self._add(7, copies)

    def _pallas(self, body, n_in, n_out, scratch, **kw):
        k_in, k_out = len(self.inputs), len(self.out_shapes)
        grid = kw.get("grid", ())

        def wrapped(*refs):
            ins, cins = refs[:n_in], refs[n_in:n_in + k_in]
            outs = refs[n_in + k_in:n_in + k_in + n_out]
            couts = refs[n_in + k_in + n_out:n_in + k_in + n_out + k_out]
            own = refs[n_in + k_in + n_out + k_out:]
            send, recv = own[len(scratch):]
            ids = [pl.program_id(a) for a in range(len(grid))]
            first = functools.reduce(jnp.logical_and, [a == 0 for a in ids], True)
            last = functools.reduce(jnp.logical_and, [a == g - 1 for a, g in zip(ids, grid)], True)

            def go(stage):
                for op in self.ops:
                    op[stage](cins, couts, send, recv)

            if grid:
                pl.when(first)(lambda: go(0))
                body(*ins, *outs, *own[:len(scratch)])
                pl.when(last)(lambda: go(1))
            else:
                go(0)
                go(1)

        sem = pltpu.SemaphoreType.DMA((self.n_sems,))
        return pl.pallas_call(
            wrapped, in_specs=list(kw.get("in_specs", [])) + [_ANY] * k_in, out_specs=list(kw.get("out_specs", [])) + [_ANY] * k_out,
            out_shape=list(kw.get("out_shape", [])) + self.out_shapes, scratch_shapes=list(scratch) + [sem, sem],
            input_output_aliases={**kw.get("aliases", {}), **{n_in + i: n_out + o for i, o in self.aliases.items()}}, name=kw["name"],
            **({"grid": grid, "compiler_params": _cp(*["arbitrary"] * len(grid))} if grid else {}))

    def run(self, body, kw, *args):
        single = not isinstance(kw["out_shape"], (list, tuple))
        out_shape = [kw["out_shape"]] if single else list(kw["out_shape"])
        out_specs = [kw["out_specs"]] if single else list(kw["out_specs"])
        res = self._pallas(body, len(args), len(out_shape), kw.get("scratch_shapes", []), grid=kw["grid"], in_specs=kw["in_specs"],
                           out_specs=out_specs, out_shape=out_shape, name=kw["name"],
                           aliases=kw.get("input_output_aliases", {}))(*args, *self.inputs)
        self.results = list(res[len(out_shape):])
        return res[0] if single else list(res[:len(out_shape)])

    def run_alone(self, name):
        self.results = list(self._pallas(None, 0, 0, [], name=name)(*self.inputs))


def _call(comm, body, **kw):
    if comm is None:
        return pl.pallas_call(body, **kw)
    return functools.partial(comm.run, body, kw)


def _in_proj_streamed(x, gain, car, o_w, place, *, bm, name):
    m, k = x.shape
    n = car.out_shapes[o_w].shape[2]
    ni, half = m // bm, k // 2
    k_in, k_out = len(car.inputs), len(car.out_shapes)
    order = lambda p: ((p & 1) << 1) | (p >> 1)

    def body(place_ref, x_ref, g_ref, *refs):
        cins, (hn_ref, o_ref, ob_ref), couts = refs[:k_in], refs[k_in:k_in + 3], refs[k_in + 3:k_in + 3 + k_out]
        wbuf, hn_all, local, ici_send, ici_recv, d2d_send, d2d_recv, send, recv = refs[k_in + 3 + k_out:]
        p, i = pl.program_id(0), pl.program_id(1)
        x, y, c, chips = _place()
        me = 2 * x + y
        rows = lambda chunk, cc: couts[o_w].at[chunk].at[pl.ds(cc * half, half)]

        @pl.when(jnp.logical_and(p == 0, i == 0))
        def _():
            for j, (px, py) in enumerate(chips):
                _remote(rows(me, c), rows(me, c), ici_send.at[j], ici_recv.at[j], (px, py, c)).start()
            for op in car.ops:
                op[0](cins, couts, send, recv)

        for j, (px, py) in enumerate(chips):
            @pl.when(jnp.logical_and(p == j + 1, i == 0))
            def _(j=j, px=px, py=py):
                landed, other = rows(2 * px + py, c), rows(2 * px + py, 1 - c)
                _remote(landed, landed, ici_send.at[j], ici_recv.at[j], (px, py, c)).wait_recv()
                _remote(landed, landed, d2d_send.at[j], d2d_recv.at[j], (x, y, 1 - c)).start()
                _remote(other, other, d2d_send.at[j], d2d_recv.at[j], (x, y, 1 - c)).wait_recv()

        @pl.when(i == 0)
        def _():
            cp = pltpu.make_async_copy(couts[o_w].at[me ^ order(p)], wbuf, local.at[0])
            cp.start()
            cp.wait()

        tile = pl.ds(pl.multiple_of(i * bm, bm), bm)

        @pl.when(p == 0)
        def _():
            xv = x_ref[...]
            hn_all[tile, :] = ((xv * _rstd(xv)) * g_ref[...]).astype(_MXU)

        hn = hn_all[tile, :]
        hn_ref[...] = hn
        res = _dot(hn, wbuf[...])
        o_ref[...] = res
        ob_ref[...] = res.astype(ob_ref.dtype)

        @pl.when(jnp.logical_and(p == 3, i == ni - 1))
        def _():
            for j, (px, py) in enumerate(chips):
                _remote(rows(me, c), rows(me, c), ici_send.at[j], ici_recv.at[j], (px, py, c)).wait_send()
                _remote(rows(me, c), rows(me, c), d2d_send.at[j], d2d_recv.at[j], (x, y, 1 - c)).wait_send()
            for op in car.ops:
                op[1](cins, couts, send, recv)

    ospec = pl.BlockSpec((bm, n), lambda p, i, place_ref: (i, place_ref[0] ^ order(p)))
    rows = pl.BlockSpec((bm, k), lambda p, i, place_ref: (jnp.where(p == 0, i, 0), 0))
    three, sems = pltpu.SemaphoreType.DMA((3,)), pltpu.SemaphoreType.DMA((max(car.n_sems, 1),))
    res = pl.pallas_call(
        body,
        grid_spec=pltpu.PrefetchScalarGridSpec(
            num_scalar_prefetch=1, grid=(4, ni),
            in_specs=[rows, pl.BlockSpec((1, k), lambda p, i, place_ref: (0, 0))] + [_ANY] * k_in,
            out_specs=[pl.BlockSpec((bm, k), lambda p, i, place_ref: (p * ni + i, 0)), ospec, ospec] + [_ANY] * k_out,
            scratch_shapes=[pltpu.VMEM((k, n), _MXU), pltpu.VMEM((m, k), _MXU), pltpu.SemaphoreType.DMA((1,)),
                            three, three, three, three, sems, sems]),
        out_shape=[S((4 * m, k), _MXU), S((m, 4 * n), F32), S((m, 4 * n), _MXU)] + car.out_shapes,
        input_output_aliases={3 + a: 3 + o for a, o in car.aliases.items()},
        compiler_params=_cp("arbitrary", "arbitrary"), name=name)(place, x, gain, *car.inputs)
    car.results = list(res[3:])
    return res[0], res[1], res[2]


def _mm_nn(a, b3, *, bm, bn, name, also=None, comm=None):
    m, k = a.shape
    c, _, n = b3.shape
    ni, nj = m // bm, n // bn

    def body(a_ref, b_ref, *o_refs):
        res = _dot(a_ref[...], b_ref[...])
        for o_ref in o_refs:
            o_ref[...] = res.astype(o_ref.dtype)

    ospec = pl.BlockSpec((bm, bn), lambda cc, j, i: (i, cc * nj + j))
    dtypes = [F32] + ([] if also is None else [also])
    out = _call(
        comm, body, grid=(c, nj, ni),
        in_specs=[pl.BlockSpec((bm, k), lambda cc, j, i: (i, 0)), pl.BlockSpec((None, k, bn), lambda cc, j, i: (cc, 0, j))],
        out_specs=[ospec] * len(dtypes), out_shape=[S((m, c * n), dt) for dt in dtypes],
        compiler_params=_cp("parallel", "parallel", "parallel"), name=name)(a, b3)
    return out[0] if also is None else out


def _mm_nt(a, b3, *, bm, bo, out_dtype, name, comm=None):
    m = a.shape[0]
    c, ko, n = b3.shape
    ni, nj = m // bm, ko // bo

    def body(a_ref, b_ref, o_ref):
        acc = _dot_nt(a_ref[:, 0:n], b_ref[0])
        for cc in range(1, c):
            acc = acc + _dot_nt(a_ref[:, cc * n:(cc + 1) * n], b_ref[cc])
        o_ref[...] = acc.astype(o_ref.dtype)

    return _call(
        comm, body, grid=(nj, ni),
        in_specs=[pl.BlockSpec((bm, c * n), lambda j, i: (i, 0)),
                  pl.BlockSpec((c, bo, n), lambda j, i: (0, j, 0))],
        out_specs=pl.BlockSpec((bm, bo), lambda j, i: (i, j)),
        out_shape=S((m, ko), out_dtype),
        compiler_params=_cp("parallel", "parallel"), name=name)(a, b3)


def _mm_tn(a, b, c, *, bm, bk, out_dtype, name, comm=None):
    m, k = b.shape[0], a.shape[1]
    n = b.shape[1] // c
    nm, nk = m // bm, k // bk

    def body(a_ref, b_ref, o_ref, acc):
        mm = pl.program_id(2)

        @pl.when(mm == 0)
        def _():
            acc[...] = jnp.zeros_like(acc)

        acc[...] += _dot_tn(a_ref[...], b_ref[...])

        @pl.when(mm == nm - 1)
        def _():
            o_ref[...] = acc[...].astype(o_ref.dtype)

    return _call(
        comm, body, grid=(c, nk, nm),
        in_specs=[pl.BlockSpec((bm, bk), lambda cc, j, mm: (mm, j)),
                  pl.BlockSpec((bm, n), lambda cc, j, mm: (mm, cc))],
        out_specs=pl.BlockSpec((None, bk, n), lambda cc, j, mm: (cc, j, 0)),
        out_shape=S((c, k, n), out_dtype),
        scratch_shapes=[pltpu.VMEM((bk, n), F32)],
        compiler_params=_cp("parallel", "parallel", "arbitrary"), name=name)(a, b)


def _swiglu_fwd(hn, wg3, wu3, *, bm, name, comm=None):
    m, k = hn.shape
    c, _, n = wg3.shape

    def body(a_ref, g_ref, u_ref, dgate_ref, dup_ref, act_ref):
        a = a_ref[...]
        gate = _dot(a, g_ref[...])
        up = _dot(a, u_ref[...])
        sg = jax.nn.sigmoid(gate)
        silu = gate * sg
        dgate_ref[...] = (up * (sg * (1.0 + gate * (1.0 - sg)))).astype(dgate_ref.dtype)
        dup_ref[...] = silu.astype(dup_ref.dtype)
        act_ref[...] = (silu * up).astype(act_ref.dtype)

    wspec = pl.BlockSpec((None, k, n), lambda cc, i: (cc, 0, 0))
    ospec = pl.BlockSpec((bm, n), lambda cc, i: (i, cc))
    return _call(
        comm, body, grid=(c, m // bm),
        in_specs=[pl.BlockSpec((bm, k), lambda cc, i: (i, 0)), wspec, wspec],
        out_specs=[ospec, ospec, ospec],
        out_shape=[S((m, c * n), _MXU), S((m, c * n), _MXU), S((m, c * n), _MXU)],
        compiler_params=_cp("parallel", "parallel"), name=name)(hn, wg3, wu3)


def _swiglu_bwd(df, wd, act_dgate, act_dup, *, bm, bo, name):
    m, k = df.shape
    ko = wd.shape[0]

    def body(a_ref, b_ref, g_ref, u_ref, dg_ref, du_ref):
        dact = _dot_nt(a_ref[...], b_ref[...])
        dg_ref[...] = (dact * g_ref[...].astype(F32)).astype(dg_ref.dtype)
        du_ref[...] = (dact * u_ref[...].astype(F32)).astype(du_ref.dtype)

    ospec = pl.BlockSpec((bm, bo), lambda j, i: (i, j))
    return pl.pallas_call(
        body, grid=(ko // bo, m // bm),
        in_specs=[pl.BlockSpec((bm, k), lambda j, i: (i, 0)), pl.BlockSpec((bo, k), lambda j, i: (j, 0)), ospec, ospec],
        out_specs=[ospec, ospec],
        out_shape=[S((m, ko), _MXU), S((m, ko), _MXU)],
        compiler_params=_cp("parallel", "parallel"), name=name)(df, wd, act_dgate, act_dup)


def _rms_fwd(x, gain, name):
    t, d = x.shape
    tm = min(t, ROW_TILE)

    def body(x_ref, g_ref, o_ref):
        xv = x_ref[...]
        o_ref[...] = ((xv * _rstd(xv)) * g_ref[...]).astype(o_ref.dtype)

    return pl.pallas_call(body, grid=(t // tm,), in_specs=[_row_spec(tm, d), _vec_spec(d)], out_specs=_row_spec(tm, d),
                          out_shape=S((t, d), _MXU), compiler_params=_cp("parallel"), name=name)(x, gain)


def _outnorm_fwd(o, yl, ga, gl, name, comm=None):
    t, w = o.shape
    tm = min(t, ROW_TILE)

    def body(o_ref, l_ref, ga_ref, gl_ref, y_ref):
        ov, lv = o_ref[...], l_ref[...]
        y_ref[:, :w] = ((ov * _rstd(ov)) * ga_ref[...]).astype(y_ref.dtype)
        y_ref[:, w:] = ((lv * _rstd(lv)) * gl_ref[...]).astype(y_ref.dtype)

    return _call(comm, body, grid=(t // tm,), in_specs=[_row_spec(tm, w), _row_spec(tm, w), _vec_spec(w), _vec_spec(w)],
                 out_specs=_row_spec(tm, 2 * w), out_shape=S((t, 2 * w), _MXU),
                 compiler_params=_cp("parallel"), name=name)(o, yl, ga, gl)


def _mid_fwd(x, mix, g_post, g_pre, name, comm=None):
    t, d = x.shape
    tm = min(t, ROW_TILE)

    def body(x_ref, m_ref, gp_ref, gn_ref, x2_ref, hn_ref):
        mv = m_ref[...]
        x2 = x_ref[...] + (mv * _rstd(mv)) * gp_ref[...]
        x2_ref[...] = x2
        hn_ref[...] = ((x2 * _rstd(x2)) * gn_ref[...]).astype(hn_ref.dtype)

    return _call(comm, body, grid=(t // tm,), in_specs=[_row_spec(tm, d), _row_spec(tm, d), _vec_spec(d), _vec_spec(d)],
                          out_specs=[_row_spec(tm, d), _row_spec(tm, d)], out_shape=[S((t, d), F32), S((t, d), _MXU)],
                          compiler_params=_cp("parallel"), name=name)(x, mix, g_post, g_pre)


def _final(f, x2, target, g_post, name):
    t, d = f.shape
    tm = min(t, ROW_TILE // 2)

    def body(f_ref, x2_ref, t_ref, g_ref, loss_ref, dout_ref, df_ref, dg_ref):
        @pl.when(pl.program_id(0) == 0)
        def _():
            loss_ref[...] = jnp.zeros_like(loss_ref)
            dg_ref[...] = jnp.zeros_like(dg_ref)

        fv = f_ref[...]
        r = _rstd(fv)
        fh = fv * r
        err = (x2_ref[...] + fh * g_ref[...]) - t_ref[...]
        loss_ref[...] += jnp.sum(err * err, axis=0, keepdims=True)
        dout = err * (1.0 / d)
        dout_ref[...] = dout
        dfv, dg = _rms_bwd(dout, fh, r, g_ref[...])
        df_ref[...] = dfv.astype(df_ref.dtype)
        dg_ref[...] += dg

    return pl.pallas_call(
        body, grid=(t // tm,),
        in_specs=[_row_spec(tm, d), _row_spec(tm, d), _row_spec(tm, d), _vec_spec(d)],
        out_specs=[_vec_spec(d), _row_spec(tm, d), _row_spec(tm, d), _vec_spec(d)],
        out_shape=[S((1, d), F32), S((t, d), F32), S((t, d), _MXU), S((1, d), F32)],
        compiler_params=_cp("arbitrary"), name=name)(f, x2, target, g_post)


def _mid_bwd(dhn_a, dhn_b, dout, x2, mix, g_pre, g_post, name, comm=None):
    t, d = x2.shape
    tm = min(t, ROW_TILE // 2)

    def body(da_ref, db_ref, do_ref, x2_ref, m_ref, gn_ref, gp_ref, dx2_ref, dm_ref, dgn_ref, dgp_ref):
        @pl.when(pl.program_id(0) == 0)
        def _():
            dgn_ref[...] = jnp.zeros_like(dgn_ref)
            dgp_ref[...] = jnp.zeros_like(dgp_ref)

        x2 = x2_ref[...]
        r = _rstd(x2)
        dxa, dgn = _rms_bwd(da_ref[...] + db_ref[...], x2 * r, r, gn_ref[...])
        dx2 = do_ref[...] + dxa
        dx2_ref[...] = dx2
        dgn_ref[...] += dgn
        mv = m_ref[...]
        rm = _rstd(mv)
        dmv, dgp = _rms_bwd(dx2, mv * rm, rm, gp_ref[...])
        dm_ref[...] = dmv.astype(dm_ref.dtype)
        dgp_ref[...] += dgp

    rs, vs = _row_spec(tm, d), _vec_spec(d)
    return _call(
        comm, body, grid=(t // tm,), in_specs=[rs, rs, rs, rs, rs, vs, vs], out_specs=[rs, rs, vs, vs],
        out_shape=[S((t, d), F32), S((t, d), _MXU), S((1, d), F32), S((1, d), F32)],
        compiler_params=_cp("arbitrary"), name=name)(dhn_a, dhn_b, dout, x2, mix, g_pre, g_post)


def _first_bwd(dhn, dx2, x, gain, name, comm=None):
    t, d = x.shape
    tm = min(t, ROW_TILE)

    def body(dh_ref, dx2_ref, x_ref, g_ref, dx_ref, dg_ref):
        @pl.when(pl.program_id(0) == 0)
        def _():
            dg_ref[...] = jnp.zeros_like(dg_ref)

        xv = x_ref[...]
        r = _rstd(xv)
        dxa, dg = _rms_bwd(dh_ref[...], xv * r, r, g_ref[...])
        dx_ref[...] = dx2_ref[...] + dxa
        dg_ref[...] += dg

    rs, vs = _row_spec(tm, d), _vec_spec(d)
    return _call(comm, body, grid=(t // tm,), in_specs=[rs, rs, rs, vs], out_specs=[rs, vs],
                          out_shape=[S((t, d), F32), S((1, d), F32)], compiler_params=_cp("arbitrary"), name=name)(dhn, dx2, x, gain)


def _outnorm_bwd(dy, o, yl, ga, gl, name, comm=None):
    t, w = o.shape
    tm = min(t, ROW_TILE)

    def body(dy_ref, o_ref, l_ref, ga_ref, gl_ref, do_ref, dl_ref, dga_ref, dgl_ref):
        @pl.when(pl.program_id(0) == 0)
        def _():
            dga_ref[...] = jnp.zeros_like(dga_ref)
            dgl_ref[...] = jnp.zeros_like(dgl_ref)

        ov, lv = o_ref[...], l_ref[...]
        ra, rl = _rstd(ov), _rstd(lv)
        dov, dga = _rms_bwd(dy_ref[:, :w], ov * ra, ra, ga_ref[...])
        dlv, dgl = _rms_bwd(dy_ref[:, w:], lv * rl, rl, gl_ref[...])
        do_ref[...] = dov.astype(do_ref.dtype)
        dl_ref[...] = dlv
        dga_ref[...] += dga
        dgl_ref[...] += dgl

    rs, vs = _row_spec(tm, w), _vec_spec(w)
    return _call(comm, body, grid=(t // tm,), in_specs=[_row_spec(tm, 2 * w), rs, rs, vs, vs], out_specs=[rs, rs, vs, vs],
                          out_shape=[S((t, w), _MXU), S((t, w), F32), S((1, w), F32), S((1, w), F32)],
                          compiler_params=_cp("arbitrary"), name=name)(dy, o, yl, ga, gl)


def _tri_sum(v, tri):
    return _dot(v.astype(_MXU), tri)


def _attn_tile(qb, kb, row, col, shift, scale):
    z = _dot_nt(qb, kb) * scale
    mask = (col + shift) < row
    lb = _log_sigmoid(z)
    lm = jnp.where(mask, lb - z, 0.0)
    return mask, lb, lm


def _attn_fwd(proj, n_heads, name, comm=None):
    t = proj.shape[0]
    bq = min(t, ATTN_BLOCK)
    nq = t // bq
    scale = 1.0 / math.sqrt(HEAD_DIM)

    heads = [slice(a * HEAD_DIM, (a + 1) * HEAD_DIM) for a in range(ATTN_HEADS)]

    def body(q_ref, k_ref, v_ref, o_ref):
        row = lax.broadcasted_iota(jnp.int32, (bq, bq), 0)
        col = lax.broadcasted_iota(jnp.int32, (bq, bq), 1)
        tri = (row > col).astype(_MXU)

        def per_q(qi, _):
            q0 = pl.multiple_of(qi * bq, bq)
            qbs = [q_ref[pl.ds(q0, bq), hd] for hd in heads]

            def cond(st):
                return jnp.logical_and(st[0] >= 0, st[1])

            def step(st):
                kj, _, carries, accs = st
                k0 = pl.multiple_of(kj * bq, bq)
                alive, new_carries, new_accs = None, [], []
                for hd, qb, carry, acc in zip(heads, qbs, carries, accs):
                    mask, lb, lm = _attn_tile(qb, k_ref[pl.ds(k0, bq), hd], row, col, (kj - qi) * bq, scale)
                    w = jnp.where(mask, jnp.exp(lb + _tri_sum(lm, tri) + carry), 0.0)
                    new_accs.append(acc + _dot(w.astype(_MXU), v_ref[pl.ds(k0, bq), hd]))
                    carry = carry + jnp.sum(lm, axis=1, keepdims=True)
                    new_carries.append(carry)
                    live = jnp.max(carry) > EXP_CUT
                    alive = live if alive is None else jnp.logical_or(alive, live)
                return kj - 1, alive, tuple(new_carries), tuple(new_accs)

            st = lax.while_loop(cond, step, (qi, jnp.bool_(True), (jnp.zeros((bq, 1), F32),) * ATTN_HEADS,
                                             (jnp.zeros((bq, HEAD_DIM), F32),) * ATTN_HEADS))
            for hd, acc in zip(heads, st[3]):
                o_ref[pl.ds(q0, bq), hd] = acc
            return 0

        lax.fori_loop(0, nq, per_q, 0)

    groups = n_heads // ATTN_HEADS
    hs = lambda off: pl.BlockSpec((t, ATTN_HEADS * HEAD_DIM), lambda h: (0, off + h))
    return _call(
        comm, body, grid=(groups,), in_specs=[hs(0), hs(groups), hs(2 * groups)], out_specs=hs(0),
        out_shape=S((t, n_heads * HEAD_DIM), F32), compiler_params=_cp("parallel"), name=name)(proj, proj, proj)


def _emit(blocks, out_ref, starts, sems):
    copies = [pltpu.make_async_copy(b, out_ref.at[:, pl.ds(c0, b.shape[1])], sems.at[k]) for k, (b, c0) in enumerate(zip(blocks, starts))]
    for cp in copies:
        cp.start()
    for cp in copies:
        cp.wait()


def _attn_bwd(proj, do, dproj, n_heads, name, comm=None):
    t = proj.shape[0]
    bq = min(t, ATTN_BLOCK)
    nq = t // bq
    scale = 1.0 / math.sqrt(HEAD_DIM)
    groups = n_heads // ATTN_HEADS
    wide = ATTN_HEADS * HEAD_DIM

    heads = [slice(a * HEAD_DIM, (a + 1) * HEAD_DIM) for a in range(ATTN_HEADS)]

    def body(q_ref, k_ref, v_ref, do_ref, _, dproj_ref, dka_ref, dva_ref, g_ref, b_ref, dq_ref, dk_ref, dv_ref, out_sems):
        group = pl.program_id(0)
        dka_ref[...] = jnp.zeros_like(dka_ref)
        dva_ref[...] = jnp.zeros_like(dva_ref)
        row = lax.broadcasted_iota(jnp.int32, (bq, bq), 0)
        col = lax.broadcasted_iota(jnp.int32, (bq, bq), 1)
        tri = (row > col).astype(_MXU)
        tri_lt = (row < col).astype(_MXU)

        def per_q(qi, _):
            q0 = pl.multiple_of(qi * bq, bq)
            qbs = [q_ref[pl.ds(q0, bq), hd] for hd in heads]
            dobs = [do_ref[pl.ds(q0, bq), hd] for hd in heads]

            def cond(st):
                return jnp.logical_and(st[0] >= 0, st[1])

            def step(st):
                kj, _, carries = st
                k0 = pl.multiple_of(kj * bq, bq)
                alive, new_carries = None, []
                for a, (hd, qb, dob, carry) in enumerate(zip(heads, qbs, dobs, carries)):
                    mask, lb, lm = _attn_tile(qb, k_ref[pl.ds(k0, bq), hd], row, col, (kj - qi) * bq, scale)
                    w = jnp.where(mask, jnp.exp(lb + _tri_sum(lm, tri) + carry), 0.0)
                    g_ref[a, pl.ds(k0, bq), :] = w * _dot_nt(dob, v_ref[pl.ds(k0, bq), hd])
                    b_ref[a, pl.ds(k0, bq), :] = jnp.where(mask, jnp.exp(lb), 0.0)
                    dva_ref[pl.ds(k0, bq), hd] += _dot_tn(w.astype(_MXU), dob)
                    carry = carry + jnp.sum(lm, axis=1, keepdims=True)
                    new_carries.append(carry)
                    live = jnp.max(carry) > EXP_CUT
                    alive = live if alive is None else jnp.logical_or(alive, live)
                return kj - 1, alive, tuple(new_carries)

            st = lax.while_loop(cond, step, (qi, jnp.bool_(True), (jnp.zeros((bq, 1), F32),) * ATTN_HEADS))

            def back(kj, st2):
                k0 = pl.multiple_of(kj * bq, bq)
                out = []
                for a, (hd, qb, (before, dq)) in enumerate(zip(heads, qbs, st2)):
                    g = g_ref[a, pl.ds(k0, bq), :]
                    beta = b_ref[a, pl.ds(k0, bq), :]
                    dz = ((g * (1.0 - beta) - (before + _tri_sum(g, tri_lt)) * beta) * scale).astype(_MXU)
                    dka_ref[pl.ds(k0, bq), hd] += _dot_tn(dz, qb)
                    out.append((before + jnp.sum(g, axis=1, keepdims=True), dq + _dot(dz, k_ref[pl.ds(k0, bq), hd])))
                return tuple(out)

            st2 = lax.fori_loop(st[0] + 1, qi + 1, back, ((jnp.zeros((bq, 1), F32), jnp.zeros((bq, HEAD_DIM), F32)),) * ATTN_HEADS)
            for hd, (_, dq) in zip(heads, st2):
                dq_ref[pl.ds(q0, bq), hd] = dq.astype(dq_ref.dtype)
            return 0

        lax.fori_loop(0, nq, per_q, 0)
        dk_ref[...] = dka_ref[...].astype(dk_ref.dtype)
        dv_ref[...] = dva_ref[...].astype(dv_ref.dtype)
        _emit([dq_ref, dk_ref, dv_ref], dproj_ref, [(a * groups + group) * wide for a in range(3)], out_sems)

    hs = lambda off: pl.BlockSpec((t, wide), lambda h: (0, off + h))
    return _call(
        comm, body, grid=(groups,), in_specs=[hs(0), hs(groups), hs(2 * groups), hs(0), _ANY], out_specs=_ANY,
        out_shape=S(dproj.shape, dproj.dtype), input_output_aliases={4: 0},
        scratch_shapes=[pltpu.VMEM((t, wide), F32), pltpu.VMEM((t, wide), F32),
                        pltpu.VMEM((ATTN_HEADS, t, bq), F32), pltpu.VMEM((ATTN_HEADS, t, bq), F32)]
        + [pltpu.VMEM((t, wide), dproj.dtype)] * 3 + [pltpu.SemaphoreType.DMA((3,))],
        compiler_params=_cp("parallel"), name=name)(proj, proj, proj, do, dproj)


def _shift_down(cur, prev8, k):
    if k == 0:
        return cur
    row8 = lax.broadcasted_iota(jnp.int32, prev8.shape, 0)
    rc = pltpu.roll(cur, k, 0)
    top = jnp.where(row8 < k, pltpu.roll(prev8, k, 0), rc[0:8, :])
    return jnp.concatenate([top, rc[8:, :]], axis=0)


def _shift_up(cur, next8, k):
    if k == 0:
        return cur
    n = cur.shape[0]
    row8 = lax.broadcasted_iota(jnp.int32, next8.shape, 0)
    rc = pltpu.roll(cur, n - k, 0)
    bottom = jnp.where(row8 >= 8 - k, pltpu.roll(next8, 8 - k, 0), rc[n - 8:, :])
    return jnp.concatenate([rc[:n - 8, :], bottom], axis=0)


def _lru_conv(xl, prev8, cw, cb):
    xs = [_shift_down(xl, prev8, CONV_WIDTH - 1 - k) for k in range(CONV_WIDTH)]
    xc = xs[0] * cw[0:1, :]
    for k in range(1, CONV_WIDTH):
        xc = xc + xs[k] * cw[k:k + 1, :]
    return xs, xc + cb


def _lru_gates(xl, prev8, cw, cb, wr, br, wi, bi, ls):
    xs, xc = _lru_conv(xl, prev8, cw, cb)
    xcb = xc.astype(_MXU)
    r = jax.nn.sigmoid(_dot(xcb, wr) + br)
    i = jax.nn.sigmoid(_dot(xcb, wi) + bi)
    la = (LRU_C * r) * ls
    a = jnp.exp(la)
    mult = jnp.sqrt(-_expm1(2.0 * la))
    return xs, xc, r, i, a, mult


def _group_scan(a, b, reverse):
    n = a.shape[0]
    row = lax.broadcasted_iota(jnp.int32, a.shape, 0) % 8
    for d in (1, 2, 4):
        if reverse:
            m = row < 8 - d
            a_s, b_s = pltpu.roll(a, n - d, 0), pltpu.roll(b, n - d, 0)
        else:
            m = row >= d
            a_s, b_s = pltpu.roll(a, d, 0), pltpu.roll(b, d, 0)
        b = jnp.where(m, a * b_s + b, b)
        a = jnp.where(m, a * a_s, a)
    return a, b


def _lru_fwd(proj, col0, n_blocks, cw, cb, wr, br, wi, bi, lam, name, comm=None):
    t = proj.shape[0]
    tt = min(t, SEQ_TILE)
    nt = t // tt

    def body(xl_ref, gl_ref, cw_ref, cb_ref, wr_ref, br_ref, wi_ref, bi_ref, lam_ref, h_ref, y_ref, *kept):
        cwv, cbv, brv, biv = cw_ref[...], cb_ref[...], br_ref[...], bi_ref[...]
        wrv, wiv = wr_ref[...].astype(_MXU), wi_ref[...].astype(_MXU)
        ls = _log_sigmoid(lam_ref[...])

        def tile(ti, hin):
            t0 = pl.multiple_of(ti * tt, tt)
            p0 = pl.multiple_of(jnp.maximum(t0 - 8, 0), 8)
            prev8 = xl_ref[pl.ds(p0, 8), :] * (ti > 0).astype(F32)
            xl = xl_ref[pl.ds(t0, tt), :]
            _, xc, r, ig, a, mult = _lru_gates(xl, prev8, cwv, cbv, wrv, brv, wiv, biv, ls)
            for ref, val in zip(kept, (r, ig, a, mult)):
                ref[pl.ds(t0, tt), :] = val
            ga, gb = _group_scan(a, mult * (ig * xc), False)
            for g in range(tt // 8):
                hg = ga[8 * g:8 * g + 8, :] * hin + gb[8 * g:8 * g + 8, :]
                h_ref[pl.ds(t0 + 8 * g, 8), :] = hg
                hin = hg[7:8, :]
            y_ref[pl.ds(t0, tt), :] = h_ref[pl.ds(t0, tt), :] * _gelu(gl_ref[pl.ds(t0, tt), :])
            return hin

        lax.fori_loop(0, nt, tile, jnp.zeros((1, HEAD_DIM), F32))

    cs = lambda off: pl.BlockSpec((t, HEAD_DIM), lambda n: (0, off + n))
    vs = pl.BlockSpec((1, HEAD_DIM), lambda n: (0, n))
    ws = pl.BlockSpec((None, HEAD_DIM, HEAD_DIM), lambda n: (n, 0, 0))
    w = n_blocks * HEAD_DIM
    return _call(
        comm, body, grid=(n_blocks,),
        in_specs=[cs(col0), cs(col0 + n_blocks), pl.BlockSpec((CONV_WIDTH, HEAD_DIM), lambda n: (0, n)), vs, ws, vs, ws, vs, vs],
        out_specs=[cs(0)] * 6, out_shape=[S((t, w), F32)] * 6,
        compiler_params=_cp("parallel"), name=name)(proj, proj, cw, cb, wr, br, wi, bi, lam)


def _lru_bwd(proj, col0, n_blocks, h, kept, dyl, cw, cb, wr, wi, lam, name, comm=None):
    t = proj.shape[0]
    tt = min(t, SEQ_TILE)
    nt = t // tt

    def body(xl_ref, gl_ref, h_ref, r_ref, i_ref, a_ref, m_ref, dy_ref, cw_ref, cb_ref, wr_ref, wi_ref, lam_ref,
             dproj_ref, dcw_ref, dcb_ref, dwr_ref, dbr_ref, dwi_ref, dbi_ref, dlam_ref, g_ref, dxl_ref, dgl_ref, out_sems):
        block = pl.program_id(0)
        cwv, cbv = cw_ref[...], cb_ref[...]
        wrv, wiv = wr_ref[...].astype(_MXU), wi_ref[...].astype(_MXU)
        lamv = lam_ref[...]
        ls = _log_sigmoid(lamv)
        for ref in (dcw_ref, dcb_ref, dwr_ref, dbr_ref, dwi_ref, dbi_ref, dlam_ref):
            ref[...] = jnp.zeros_like(ref)

        def tile(s, carry):
            e_in, dxc_next8 = carry
            ti = nt - 1 - s
            t0 = pl.multiple_of(ti * tt, tt)
            p0 = pl.multiple_of(jnp.maximum(t0 - 8, 0), 8)
            first = (ti > 0).astype(F32)
            xl = xl_ref[pl.ds(t0, tt), :]
            xs, xc = _lru_conv(xl, xl_ref[pl.ds(p0, 8), :] * first, cwv, cbv)
            r, ig, a, mult = (ref[pl.ds(t0, tt), :] for ref in (r_ref, i_ref, a_ref, m_ref))
            hv = h_ref[pl.ds(t0, tt), :]
            h_before = _shift_down(hv, h_ref[pl.ds(p0, 8), :] * first, 1)
            glv = gl_ref[pl.ds(t0, tt), :]
            dyv = dy_ref[pl.ds(t0, tt), :]
            dgl_ref[pl.ds(t0, tt), :] = (dyv * hv * _gelu_grad(glv)).astype(dgl_ref.dtype)
            dh = dyv * _gelu(glv)
            row = lax.broadcasted_iota(jnp.int32, a.shape, 0)
            coef = jnp.where(row == tt - 1, 1.0, pltpu.roll(a, tt - 1, 0))
            ga, gb = _group_scan(coef, dh, True)
            gin = e_in
            for g in reversed(range(tt // 8)):
                gg = ga[8 * g:8 * g + 8, :] * gin + gb[8 * g:8 * g + 8, :]
                g_ref[8 * g:8 * g + 8, :] = gg
                gin = gg[0:1, :]
            gv = g_ref[...]
            e_out = a[0:1, :] * gv[0:1, :]
            ix = ig * xc
            dla = (gv * h_before) * a - (gv * ix) * (a * a / mult)
            dlam_ref[...] += jnp.sum(dla * (LRU_C * r), axis=0, keepdims=True)
            dpr = (dla * (LRU_C * ls)) * (r * (1.0 - r))
            dpi = (gv * mult * xc) * (ig * (1.0 - ig))
            dbr_ref[...] += jnp.sum(dpr, axis=0, keepdims=True)
            dbi_ref[...] += jnp.sum(dpi, axis=0, keepdims=True)
            xcb, dprb, dpib = xc.astype(_MXU), dpr.astype(_MXU), dpi.astype(_MXU)
            dwr_ref[...] += _dot_tn(xcb, dprb)
            dwi_ref[...] += _dot_tn(xcb, dpib)
            dxc = gv * mult * ig + _dot_nt(dprb, wrv) + _dot_nt(dpib, wiv)
            dcb_ref[...] += jnp.sum(dxc, axis=0, keepdims=True)
            dxl = None
            for k in range(CONV_WIDTH):
                dcw_ref[k:k + 1, :] += jnp.sum(dxc * xs[k], axis=0, keepdims=True)
                term = _shift_up(dxc, dxc_next8, CONV_WIDTH - 1 - k) * cwv[k:k + 1, :]
                dxl = term if dxl is None else dxl + term
            dxl_ref[pl.ds(t0, tt), :] = dxl.astype(dxl_ref.dtype)
            return e_out, dxc[0:8, :]

        lax.fori_loop(0, nt, tile, (jnp.zeros((1, HEAD_DIM), F32), jnp.zeros((8, HEAD_DIM), F32)))
        dlam_ref[...] = dlam_ref[...] * (1.0 - jax.nn.sigmoid(lamv))
        _emit([dxl_ref, dgl_ref], dproj_ref, [(col0 + block) * HEAD_DIM, (col0 + n_blocks + block) * HEAD_DIM], out_sems)

    cs = lambda off: pl.BlockSpec((t, HEAD_DIM), lambda n: (0, off + n))
    vs = pl.BlockSpec((1, HEAD_DIM), lambda n: (0, n))
    ws = pl.BlockSpec((None, HEAD_DIM, HEAD_DIM), lambda n: (n, 0, 0))
    cws = pl.BlockSpec((CONV_WIDTH, HEAD_DIM), lambda n: (0, n))
    w = n_blocks * HEAD_DIM
    vec = S((1, w), F32)
    mat = S((n_blocks, HEAD_DIM, HEAD_DIM), F32)
    return _call(
        comm, body, grid=(n_blocks,),
        in_specs=[cs(col0), cs(col0 + n_blocks)] + [cs(0)] * 6 + [cws, vs, ws, ws, vs],
        out_specs=[_ANY, cws, vs, ws, vs, ws, vs, vs],
        out_shape=[S(proj.shape, _MXU), S((CONV_WIDTH, w), F32), vec, mat, vec, mat, vec, vec],
        scratch_shapes=[pltpu.VMEM((tt, HEAD_DIM), F32), pltpu.VMEM((t, HEAD_DIM), _MXU), pltpu.VMEM((t, HEAD_DIM), _MXU),
                        pltpu.SemaphoreType.DMA((2,))],
        compiler_params=_cp("parallel"), name=name)(proj, proj, h, *kept, dyl, cw, cb, wr, wi, lam)


class _NoExchange:
    grad_dtype = F32

    def __init__(self, weights):
        self.weights, self.grads, self.packs = weights, {}, {}

    def weight(self, name):
        return self.weights[name]

    def in_proj(self, x, gain, bm):
        hn = _rms_fwd(x, gain, "rms1")
        return [hn, *_mm_nn(hn, self.weights["w_in"], bm=bm, bn=self.weights["w_in"].shape[2], name="in_proj", also=_MXU)]

    def conv_w(self):
        return self.weights["conv_w"]

    def carrier(self, call):
        return None

    def harvest(self, car):
        pass

    def alone(self, call):
        pass


def _local_step(x, target, norms, ex, cb, wr, br, wi, bi, lam, ga, gl):
    g_pre_mix, g_post_mix, g_pre_ffn, g_post_ffn = norms
    t, d = x.shape
    bm = min(t, MM_ROWS)
    bt = min(t, DW_TOKENS)

    def run(fn, name, *args, **kw):
        car = ex.carrier(name)
        out = fn(*args, name=name, comm=car, **kw)
        ex.harvest(car)
        return out

    hn1, proj, proj_mx = ex.in_proj(x, g_pre_mix, bm)
    win3, cw = ex.weight("w_in"), ex.conv_w()
    c = win3.shape[0]
    o = run(_attn_fwd, "attn_fwd", proj_mx, (proj.shape[1] - d) // 3 // HEAD_DIM)
    mix = 2 * o.shape[1]
    n_heads = n_blocks = o.shape[1] // HEAD_DIM
    h, yl, *kept = run(_lru_fwd, "lru_fwd", proj, 3 * n_heads, n_blocks, cw, cb, wr, br, wi, bi, lam)
    y = run(_outnorm_fwd, "outnorm_fwd", o, yl, ga, gl)
    wout = ex.weight("w_out")
    mixo = run(_mm_nn, "out_proj", y, wout[None], bm=bm, bn=d)
    x2, hn2 = run(_mid_fwd, "mid_fwd", x, mixo, g_post_mix, g_pre_ffn)
    ex.alone("gather_w_up_last")
    wg3, wu3 = ex.weight("w_ffn_gate"), ex.weight("w_ffn_up")
    act_dgate, act_dup, act = run(_swiglu_fwd, "ffn_gate_up", hn2, wg3, wu3, bm=bm)
    ex.alone("gather_w_down")
    wd = ex.weight("w_ffn_down")
    ff = wd.shape[0]
    f = _mm_nn(act, wd[None], bm=bm, bn=d // 2, name="ffn_down")
    loss_cols, dout, df, dg_post_ffn = _final(f, x2, target, g_post_ffn, "final")

    dgate, dup = _swiglu_bwd(df, wd, act_dgate, act_dup, bm=min(t, 2 * MM_ROWS), bo=ff // 4, name="ffn_down_bwd")
    ex.grads["w_ffn_down"] = _mm_tn(act, df, 1, bm=bt, bk=DW_ROWS, out_dtype=ex.grad_dtype, name="ffn_down_dw").reshape(c, ff // c, d)
    ex.grads["w_ffn_gate"] = run(_mm_tn, "ffn_gate_dw", hn2, dgate, c, bm=bt, bk=d // 2, out_dtype=ex.grad_dtype)
    ex.grads["w_ffn_up"] = run(_mm_tn, "ffn_up_dw", hn2, dup, c, bm=bt, bk=d // 2, out_dtype=ex.grad_dtype)
    dhn2_g = run(_mm_nt, "ffn_gate_dx", dgate, wg3, bm=bm, bo=d // 2, out_dtype=F32)
    dhn2_u = run(_mm_nt, "ffn_up_dx", dup, wu3, bm=bm, bo=d // 2, out_dtype=F32)
    dx2, dmix, dg_pre_ffn, dg_post_mix = run(_mid_bwd, "mid_bwd", dhn2_g, dhn2_u, dout, x2, mixo, g_pre_ffn, g_post_mix)
    dy = run(_mm_nt, "out_proj_dx", dmix, wout[None], bm=bm, bo=mix, out_dtype=F32)
    ex.grads["w_out"] = _mm_tn(y, dmix, 1, bm=bt, bk=mix // 4, out_dtype=ex.grad_dtype, name="out_proj_dw").reshape(c, mix // c, d)
    do, dyl, dga, dgl_norm = run(_outnorm_bwd, "outnorm_bwd", dy, o, yl, ga, gl)
    dproj, dcw, dcb, dwr, dbr, dwi, dbi, dlam = run(_lru_bwd, "lru_bwd", proj, 3 * n_heads, n_blocks, h, kept, dyl, cw, cb, wr, wi, lam)
    small = dict(post_mix_norm=dg_post_mix, pre_ffn_norm=dg_pre_ffn, post_ffn_norm=dg_post_ffn, conv_w=dcw, conv_b=dcb,
                 w_rgate=dwr, b_rgate=dbr, w_igate=dwi, b_igate=dbi, lru_lambda=dlam, attn_out_norm=dga, lru_out_norm=dgl_norm)
    ex.packs["early"] = _pack([small[n] for n in _SMALL_EARLY])
    dproj = run(_attn_bwd, "attn_bwd", proj_mx, do, dproj, n_heads)
    ex.grads["w_in"] = _mm_tn(hn1, dproj, c, bm=bt, bk=d // 2, out_dtype=ex.grad_dtype, name="in_proj_dw")
    ex.alone("grads_w_in_swap")
    dhn1 = run(_mm_nt, "in_proj_dx", dproj, win3, bm=bm, bo=d // 2, out_dtype=F32)
    grad_x, small["pre_mix_norm"] = run(_first_bwd, "first_bwd", dhn1, dx2, x, g_pre_mix)
    ex.packs["late"] = _pack([small["pre_mix_norm"], (0.5 / d) * jnp.sum(loss_cols, keepdims=True)])
    return loss_cols, grad_x, small


def _into_slot(wsh, slot, dtype, name):
    rows, n = wsh.shape
    rb = _row_block(rows, 512) if rows % 8 == 0 else rows

    def body(s_ref, w_ref, o_ref):
        o_ref[...] = w_ref[...].astype(o_ref.dtype)

    return pl.pallas_call(
        body,
        grid_spec=pltpu.PrefetchScalarGridSpec(
            num_scalar_prefetch=1, grid=(rows // rb,),
            in_specs=[pl.BlockSpec((rb, n), lambda i, s_ref: (i, 0))],
            out_specs=pl.BlockSpec((None, rb, n), lambda i, s_ref: (s_ref[0], i, 0))),
        out_shape=S((4, rows, n), dtype), compiler_params=_cp("parallel"), name=name)(slot, wsh)


class _Exchange:
    SCHEDULE = {
        "in_proj": [("stream", "w_in"), ("ici", "conv_w"), ("ici", "w_ffn_up", 0)],
        "attn_fwd": [("d2d", "w_ffn_up", 0), ("ici", "w_ffn_gate")],
        "lru_fwd": [("d2d", "w_ffn_gate"), ("ici", "w_out"), ("ici", "w_ffn_up", 1)],
        "outnorm_fwd": [("d2d", "w_out"), ("d2d", "w_ffn_up", 1)],
        "out_proj": [("ici", "w_ffn_up", 2)],
        "mid_fwd": [("d2d", "w_ffn_up", 2), ("ici", "w_ffn_up", 3)],
        "gather_w_up_last": [("d2d", "w_ffn_up", 3)],
        "ffn_gate_up": [("ici", "w_ffn_down")],
        "gather_w_down": [("d2d", "w_ffn_down")],
        "ffn_gate_dw": [("swap", "w_ffn_down")],
        "ffn_up_dw": [("scatter", "w_ffn_down", 0), ("scatter", "w_ffn_down", 1), ("scatter", "w_ffn_down", 2), ("swap", "w_ffn_gate")],
        "ffn_gate_dx": [("scatter", "w_ffn_down", 3), ("scatter", "w_ffn_gate", 0), ("scatter", "w_ffn_gate", 1), ("swap", "w_ffn_up")],
        "ffn_up_dx": [("share", "w_ffn_down"), ("scatter", "w_ffn_gate", 2), ("scatter", "w_ffn_gate", 3), ("scatter", "w_ffn_up", 0)],
        "mid_bwd": [("share", "w_ffn_gate"), ("scatter", "w_ffn_up", 1), ("scatter", "w_ffn_up", 2)],
        "out_proj_dx": [("scatter", "w_ffn_up", 3)],
        "outnorm_bwd": [("share", "w_ffn_up"), ("swap", "w_out")],
        "lru_bwd": [("scatter", "w_out")],
        "attn_bwd": [("share", "w_out"), ("spread", "early")],
        "grads_w_in_swap": [("swap", "w_in")],
        "in_proj_dx": [("scatter", "w_in")],
        "grads_w_in_share": [("share", "w_in"), ("spread", "late")],
    }
    PIECES = 4
    grad_dtype = BF16

    def __init__(self, slots, place):
        self.buf, self.place = dict(slots), place
        self.grads, self.packs, self.swapped, self.part, self.scattered, self.full, self.spreaded = {}, {}, {}, {}, {}, {}, {}

    def weight(self, name):
        b = self.buf[name]
        return b.reshape(-1, b.shape[2]) if name in ("w_out", "w_ffn_down") else b

    def in_proj(self, x, gain, bm):
        car = self.carrier("in_proj")
        out = _in_proj_streamed(x, gain, car, car.streamed, self.place, bm=bm, name="in_proj")
        self.harvest(car)
        return out

    def conv_w(self):
        return jnp.transpose(self.buf["conv_w"], (1, 0, 2)).reshape(CONV_WIDTH, -1)

    def carrier(self, call):
        if call not in self.SCHEDULE:
            return None
        car = _Carrier()
        car.todo, slot = [], {}
        for kind, name, *piece in self.SCHEDULE[call]:
            if kind in ("ici", "d2d", "stream"):
                if name not in slot:
                    slot[name] = car.inplace(self.buf[name])
                    car.todo.append((self.buf, name, slot[name]))
            if kind == "stream":
                car.streamed = slot[name]
            elif kind in ("ici", "d2d"):
                size = self.buf[name].shape[1] // 2 // self.PIECES
                rows = (piece[0] * size, size) if piece else None
                if kind == "ici":
                    car.gather_ici(slot[name], rows, split=name != "conv_w")
                else:
                    car.gather_d2d(slot[name], rows)
            elif kind == "swap":
                g = self.grads[name]
                o = car.fresh((4, g.shape[1] // 2, g.shape[2]), g.dtype)
                car.swap(car.read(g), o)
                car.todo.append((self.swapped, name, o))
            elif kind == "scatter":
                if name not in self.part:
                    self.part[name] = _add_own_half(self.grads[name], self.swapped[name], self.place[1:], "grads_add_" + name)
                p = self.part[name]
                key = ("scatter", name)
                if key not in slot:
                    slot[key] = (car.read(p), car.inplace(self.scattered[name]) if name in self.scattered else car.fresh(p.shape, p.dtype))
                    car.todo.append((self.scattered, name, slot[key][1]))
                size = p.shape[1] // self.PIECES
                car.scatter(*slot[key], (piece[0] * size, size) if piece else None)
            elif kind == "share":
                o = car.inplace(_sum_chips(self.part[name], self.scattered[name], self.place, "grads_sum_" + name))
                car.share(o)
                car.todo.append((self.full, name, o))
            else:
                o = car.fresh((8,) + self.packs[name].shape, F32)
                car.spread(car.read(self.packs[name]), o)
                car.todo.append((self.spreaded, name, o))
        return car

    def harvest(self, car):
        for state, name, o in (car.todo if car is not None else []):
            state[name] = car.results[o]

    def alone(self, call):
        car = self.carrier(call)
        car.run_alone(call)
        self.harvest(car)

    def small_sum(self, key):
        return _sum_devices(self.packs[key], self.spreaded[key], 2 * self.place[0:1] + self.place[1:], "grads_small_sum_" + key)


def _row_block(rows, cap):
    return max(b for b in range(8, cap + 1, 8) if rows % b == 0)


def _add_own_half(g, recv, core, name):
    _, rows, n = g.shape
    half = rows // 2
    rb = _row_block(half, 1024)
    nb = half // rb

    def body(c_ref, g_ref, r_ref, o_ref):
        o_ref[...] = (g_ref[...].astype(F32) + r_ref[...].astype(F32)).astype(o_ref.dtype)

    return pl.pallas_call(
        body,
        grid_spec=pltpu.PrefetchScalarGridSpec(
            num_scalar_prefetch=1, grid=(4, nb),
            in_specs=[pl.BlockSpec((None, rb, n), lambda k, i, c_ref: (k, c_ref[0] * nb + i, 0)),
                      pl.BlockSpec((None, rb, n), lambda k, i, c_ref: (k, i, 0))],
            out_specs=pl.BlockSpec((None, rb, n), lambda k, i, c_ref: (k, i, 0))),
        out_shape=S((4, half, n), BF16), compiler_params=_cp("parallel", "parallel"), name=name)(core, g, recv)


def _sum_chips(part, recv, place, name):
    _, rows, n = part.shape
    rb = _row_block(rows, 256)
    nb = rows // rb

    def body(p_ref, own_ref, r0, r1, r2, r3, o_ref):
        own = own_ref[...].astype(F32)
        terms = [jnp.where(p_ref[0] == k, own, r[...].astype(F32)) for k, r in enumerate((r0, r1, r2, r3))]
        o_ref[...] = ((terms[0] + terms[1]) + terms[2]) + terms[3]

    def slot(k):
        return pl.BlockSpec((None, rb, n), lambda i, p_ref: (jnp.where(p_ref[0] == k, (k + 1) % 4, k), i, 0))

    return pl.pallas_call(
        body,
        grid_spec=pltpu.PrefetchScalarGridSpec(
            num_scalar_prefetch=1, grid=(nb,),
            in_specs=[pl.BlockSpec((None, rb, n), lambda i, p_ref: (p_ref[0], i, 0))] + [slot(k) for k in range(4)],
            out_specs=pl.BlockSpec((rb, n), lambda i, p_ref: (p_ref[1] * nb + i, 0))),
        out_shape=S((2 * rows, n), F32), compiler_params=_cp("parallel"), name=name)(place, part, recv, recv, recv, recv)


def _sum_devices(own, spread, me, name):
    rows = own.shape[0]

    def body(me_ref, own_ref, *refs):
        acc = None
        for k, r in enumerate(refs[:8]):
            term = jnp.where(me_ref[0] == k, own_ref[...], r[...])
            acc = term if acc is None else acc + term
        refs[8][...] = acc

    def slot(k):
        return pl.BlockSpec((None, rows, 128), lambda i, me_ref: (jnp.where(me_ref[0] == k, (k + 1) % 8, k), 0, 0))

    whole = pl.BlockSpec((rows, 128), lambda i, me_ref: (0, 0))
    return pl.pallas_call(
        body,
        grid_spec=pltpu.PrefetchScalarGridSpec(num_scalar_prefetch=1, grid=(1,), in_specs=[whole] + [slot(k) for k in range(8)],
                                               out_specs=whole),
        out_shape=S((rows, 128), F32), compiler_params=_cp("arbitrary"), name=name)(me, own, *[spread] * 8)


def _adamw(w, g, m, v, name, regive=False):
    rows, n = w.shape
    rb = rows if rows * n * 4 <= (1 << 21) else _row_block(rows, 512)
    c1 = 1.0 - ADAM_B1 ** ADAM_STEP
    c2 = 1.0 - ADAM_B2 ** ADAM_STEP

    def body(w_ref, g_ref, m_ref, v_ref, d_ref, nm_ref, nv_ref, *again):
        gv = g_ref[...]
        for ref in again:
            ref[...] = gv
        nm = ADAM_B1 * m_ref[...] + (1.0 - ADAM_B1) * gv
        nv = ADAM_B2 * v_ref[...] + (1.0 - ADAM_B2) * (gv * gv)
        nm_ref[...] = nm
        nv_ref[...] = nv
        d_ref[...] = -ADAM_LR * ((nm / c1) / (jnp.sqrt(nv / c2) + ADAM_EPS) + ADAM_WD * w_ref[...])

    bs = pl.BlockSpec((rb, n), lambda i: (i, 0))
    n_out = 4 if regive else 3
    return pl.pallas_call(body, grid=(rows // rb,), in_specs=[bs] * 4, out_specs=[bs] * n_out, out_shape=[S((rows, n), F32)] * n_out,
                          compiler_params=_cp("parallel"), name=name)(w, g, m, v)


_BIG = ("w_in", "w_out", "w_ffn_gate", "w_ffn_up", "w_ffn_down")
_SMALL = ("pre_mix_norm", "post_mix_norm", "pre_ffn_norm", "post_ffn_norm", "conv_w", "conv_b", "w_rgate", "b_rgate",
          "w_igate", "b_igate", "lru_lambda", "attn_out_norm", "lru_out_norm")
_SMALL_EARLY = _SMALL[1:]
_WEIGHTS = ("pre_mix_norm", "post_mix_norm", "pre_ffn_norm", "post_ffn_norm", "w_in", "conv_w", "conv_b", "w_rgate", "b_rgate",
            "w_igate", "b_igate", "lru_lambda", "attn_out_norm", "lru_out_norm", "w_out", "w_ffn_gate", "w_ffn_up", "w_ffn_down")


def _pack(arrays):
    flat = []
    for a in arrays:
        f = a.reshape(-1)
        flat.append(jnp.pad(f, (0, (-f.shape[0]) % 1024)))
    return jnp.concatenate(flat).reshape(-1, 128)


def _unpack(packed, shapes):
    out, pos = [], 0
    flat = packed.reshape(-1)
    for s in shapes:
        size = math.prod(s)
        out.append(flat[pos:pos + size].reshape(s))
        pos += size + (-size) % 1024
    return out


def kernel(x, pre_mix_norm, post_mix_norm, pre_ffn_norm, post_ffn_norm, w_in, conv_w, conv_b, w_rgate, b_rgate, w_igate, b_igate, lru_lambda, attn_out_norm, lru_out_norm, w_out, w_ffn_gate, w_ffn_up, w_ffn_down, loss_target, m_pre_mix_norm, m_post_mix_norm, m_pre_ffn_norm, m_post_ffn_norm, m_w_in, m_conv_w, m_conv_b, m_w_rgate, m_b_rgate, m_w_igate, m_b_igate, m_lru_lambda, m_attn_out_norm, m_lru_out_norm, m_w_out, m_w_ffn_gate, m_w_ffn_up, m_w_ffn_down, v_pre_mix_norm, v_post_mix_norm, v_pre_ffn_norm, v_post_ffn_norm, v_w_in, v_conv_w, v_conv_b, v_w_rgate, v_b_rgate, v_w_igate, v_b_igate, v_lru_lambda, v_attn_out_norm, v_lru_out_norm, v_w_out, v_w_ffn_gate, v_w_ffn_up, v_w_ffn_down):
    given = dict(locals())
    w = {n: given[n][0] for n in _WEIGHTS}
    m = {n: given["m_" + n][0] for n in _WEIGHTS}
    v = {n: given["v_" + n][0] for n in _WEIGHTS}
    xs, target = x[0], loss_target[0]
    d = xs.shape[1]
    chip = (2 * lax.axis_index("x") + lax.axis_index("y")).astype(jnp.int32)
    place = jnp.stack([chip, lax.axis_index("c").astype(jnp.int32)])

    slots = {n: _into_slot(w[n], place[0:1], _MXU, "slot_" + n) for n in _BIG}
    slots["conv_w"] = _into_slot(w["conv_w"], place[0:1], F32, "slot_conv_w")
    ex = _Exchange(slots, place)
    row = lambda a: a.reshape(1, -1)
    norms = tuple(row(w[n]) for n in ("pre_mix_norm", "post_mix_norm", "pre_ffn_norm", "post_ffn_norm"))

    loss_cols, grad_x, small = _local_step(
        xs, target, norms, ex, row(w["conv_b"]), w["w_rgate"], row(w["b_rgate"]),
        w["w_igate"], row(w["b_igate"]), row(w["lru_lambda"]), row(w["attn_out_norm"]), row(w["lru_out_norm"]))


    ex.alone("grads_w_in_share")
    reduced = {n: ex.full[n] for n in _BIG}
    early = _unpack(ex.small_sum("early"), [small[n].shape for n in _SMALL_EARLY])
    late = _unpack(ex.small_sum("late"), [small["pre_mix_norm"].shape, (1, 1)])
    loss = late[1][0, 0]
    for n, g in zip(_SMALL_EARLY + ("pre_mix_norm",), early + late[:1]):
        reduced[n] = g.reshape(w[n].shape) if n != "conv_w" else lax.dynamic_slice_in_dim(g, chip * w[n].shape[1], w[n].shape[1], axis=1)

    delta, new_m, new_v = {}, {}, {}
    for n in _BIG:
        delta[n], new_m[n], new_v[n], reduced[n] = _adamw(w[n], reduced[n], m[n], v[n], "adamw_" + n, regive=True)
    shapes = [w[n].shape for n in _SMALL]
    packed = _adamw(*[_pack([src[n] for n in _SMALL]) for src in (w, reduced, m, v)], "adamw_small")
    for out, p in zip((delta, new_m, new_v), packed):
        out.update(zip(_SMALL, _unpack(p, shapes)))

    lead = lambda a: a[None]
    return (loss, lead(grad_x), *[lead(reduced[n]) for n in _WEIGHTS], *[lead(delta[n]) for n in _WEIGHTS],
            *[lead(new_m[n]) for n in _WEIGHTS], *[lead(new_v[n]) for n in _WEIGHTS])
```

```python
import functools
import math

import jax
import jax.numpy as jnp
from jax import lax
from jax.experimental import pallas as pl
from jax.experimental.pallas import tpu as pltpu

F32 = jnp.float32
BF16 = jnp.bfloat16
_MXU = BF16
S = jax.ShapeDtypeStruct

RMS_EPS = 1e-6
HEAD_DIM = 128
CONV_WIDTH = 4
LRU_C = 8.0
ADAM_LR, ADAM_B1, ADAM_B2, ADAM_EPS, ADAM_WD, ADAM_STEP = 0.001, 0.9, 0.999, 1e-08, 0.01, 10
EXP_CUT = -105.0
VMEM_LIMIT = 60 * 1024 * 1024
ROW_TILE = 512
SEQ_TILE = 256
ATTN_BLOCK = 256
ATTN_HEADS = 2
MM_ROWS = 512
DW_TOKENS = 2048
DW_ROWS = 512
MESH = pl.DeviceIdType.MESH


def _cp(*sem):
    return pltpu.CompilerParams(dimension_semantics=sem, vmem_limit_bytes=VMEM_LIMIT)


def _dot(a, b):
    return jnp.dot(a, b, preferred_element_type=F32)


def _dot_nt(a, b):
    return lax.dot_general(a, b, (((1,), (1,)), ((), ())), preferred_element_type=F32)


def _dot_tn(a, b):
    return lax.dot_general(a, b, (((0,), (0,)), ((), ())), preferred_element_type=F32)


def _rstd(v):
    return lax.rsqrt(jnp.mean(v * v, axis=-1, keepdims=True) + RMS_EPS)


def _rms_bwd(dn, vh, r, gain):
    dvh = dn * gain
    dv = r * (dvh - vh * jnp.mean(dvh * vh, axis=-1, keepdims=True))
    return dv, jnp.sum(dn * vh, axis=0, keepdims=True)


def _log_sigmoid(z):
    return jnp.minimum(z, 0.0) - jnp.log(1.0 + jnp.exp(-jnp.abs(z)))


def _expm1(v):
    small = v * (1.0 + v * (0.5 + v * (1.0 / 6.0 + v * (1.0 / 24.0 + v * (1.0 / 120.0)))))
    return jnp.where(jnp.abs(v) < 0.04, small, jnp.exp(v) - 1.0)


_GELU_C = math.sqrt(2.0 / math.pi)


def _gelu(v):
    return 0.5 * v * (1.0 + jnp.tanh(_GELU_C * (v + 0.044715 * v * v * v)))


def _gelu_grad(v):
    th = jnp.tanh(_GELU_C * (v + 0.044715 * v * v * v))
    return 0.5 * (1.0 + th) + 0.5 * v * (1.0 - th * th) * _GELU_C * (1.0 + 3.0 * 0.044715 * v * v)


def _row_spec(tm, d):
    return pl.BlockSpec((tm, d), lambda i: (i, 0))


def _vec_spec(d):
    return pl.BlockSpec((1, d), lambda i: (0, 0))


_ANY = pl.BlockSpec(memory_space=pl.ANY)


def _place():
    x, y, c = lax.axis_index("x"), lax.axis_index("y"), lax.axis_index("c")
    return x, y, c, [(1 - x, y), (x, 1 - y), (1 - x, 1 - y)]


def _remote(src, dst, send_sem, recv_sem, to):
    return pltpu.make_async_remote_copy(src_ref=src, dst_ref=dst, send_sem=send_sem, recv_sem=recv_sem,
                                        device_id=to, device_id_type=MESH)


class _Carrier:
    def __init__(self):
        self.inputs, self.out_shapes, self.aliases, self.ops, self.n_sems, self.results = [], [], {}, [], 0, None

    def inplace(self, arr):
        self.aliases[len(self.inputs)] = len(self.out_shapes)
        self.inputs.append(arr)
        self.out_shapes.append(S(arr.shape, arr.dtype))
        return len(self.out_shapes) - 1

    def read(self, arr):
        self.inputs.append(arr)
        return len(self.inputs) - 1

    def fresh(self, shape, dtype):
        self.out_shapes.append(S(shape, dtype))
        return len(self.out_shapes) - 1

    def _add(self, n_sems, copies):
        base = self.n_sems
        self.n_sems += n_sems

        def start(ins, outs, send, recv):
            for k, (src, dst, _, to) in enumerate(copies(ins, outs)):
                _remote(src, dst, send.at[base + k], recv.at[base + k], to).start()

        def finish(ins, outs, send, recv):
            for k, (src, _, land, to) in enumerate(copies(ins, outs)):
                _remote(src, land, send.at[base + k], recv.at[base + k], to).wait()

        self.ops.append((start, finish))

    def gather_ici(self, o, rows=None, split=True):
        half = self.out_shapes[o].shape[1] // 2
        lo, size = rows or (0, half)

        def copies(ins, outs):
            x, y, c, chips = _place()
            part = (lambda ref: ref.at[pl.ds(c * half + lo, size)]) if split else (lambda ref: ref)
            mine = part(outs[o].at[2 * x + y])
            return [(mine, mine, part(outs[o].at[2 * px + py]), (px, py, c)) for px, py in chips]

        self._add(3, copies)

    def gather_d2d(self, o, rows=None):
        half = self.out_shapes[o].shape[1] // 2
        lo, size = rows or (0, half)

        def copies(ins, outs):
            x, y, c, chips = _place()
            at = lambda k, cc: outs[o].at[k].at[pl.ds(cc * half + lo, size)]
            return [(at(2 * px + py, c), at(2 * px + py, c), at(2 * px + py, 1 - c), (x, y, 1 - c)) for px, py in chips]

        self._add(3, copies)

    def swap(self, i, o):
        half = self.inputs[i].shape[1] // 2

        def copies(ins, outs):
            x, y, c, _ = _place()
            return [(ins[i].at[:, pl.ds((1 - c) * half, half)], outs[o], outs[o], (x, y, 1 - c))]

        self._add(1, copies)

    def scatter(self, i, o, rows=None):
        lo, size = rows or (0, self.inputs[i].shape[1])

        def copies(ins, outs):
            x, y, c, chips = _place()
            cut = lambda ref: ref.at[pl.ds(lo, size)]
            return [(cut(ins[i].at[2 * px + py]), cut(outs[o].at[2 * x + y]), cut(outs[o].at[2 * px + py]), (px, py, c)) for px, py in chips]

        self._add(3, copies)

    def share(self, o):
        r = self.out_shapes[o].shape[0] // 2

        def copies(ins, outs):
            x, y, c, _ = _place()
            mine = outs[o].at[pl.ds(c * r, r)]
            return [(mine, mine, outs[o].at[pl.ds((1 - c) * r, r)], (x, y, 1 - c))]

        self._add(1, copies)

    def spread(self, i, o):
        def copies(ins, outs):
            x, y, c, _ = _place()
            me = 4 * x + 2 * y + c
            out = []
            for d in range(1, 8):
                to, frm = (me + d) % 8, (me + 8 - d) % 8
                out.append((ins[i], outs[o].at[me], outs[o].at[frm], (to // 4, (to // 2) % 2, to % 2)))
            return out

        self._add(7, copies)

    def _pallas(self, body, n_in, n_out, scratch, **kw):
        k_in, k_out = len(self.inputs), len(self.out_shapes)
        grid = kw.get("grid", ())

        def wrapped(*refs):
            ins, cins = refs[:n_in], refs[n_in:n_in + k_in]
            outs = refs[n_in + k_in:n_in + k_in + n_out]
            couts = refs[n_in + k_in + n_out:n_in + k_in + n_out + k_out]
            own = refs[n_in + k_in + n_out + k_out:]
            send, recv = own[len(scratch):]
            ids = [pl.program_id(a) for a in range(len(grid))]
            first = functools.reduce(jnp.logical_and, [a == 0 for a in ids], True)
            last = functools.reduce(jnp.logical_and, [a == g - 1 for a, g in zip(ids, grid)], True)

            def go(stage):
                for op in self.ops:
                    op[stage](cins, couts, send, recv)

            if grid:
                pl.when(first)(lambda: go(0))
                body(*ins, *outs, *own[:len(scratch)])
                pl.when(last)(lambda: go(1))
            else:
                go(0)
                go(1)

        sem = pltpu.SemaphoreType.DMA((self.n_sems,))
        return pl.pallas_call(
            wrapped, in_specs=list(kw.get("in_specs", [])) + [_ANY] * k_in, out_specs=list(kw.get("out_specs", [])) + [_ANY] * k_out,
            out_shape=list(kw.get("out_shape", [])) + self.out_shapes, scratch_shapes=list(scratch) + [sem, sem],
            input_output_aliases={**kw.get("aliases", {}), **{n_in + i: n_out + o for i, o in self.aliases.items()}}, name=kw["name"],
            **({"grid": grid, "compiler_params": _cp(*["arbitrary"] * len(grid))} if grid else {}))

    def run(self, body, kw, *args):
        single = not isinstance(kw["out_shape"], (list, tuple))
        out_shape = [kw["out_shape"]] if single else list(kw["out_shape"])
        out_specs = [kw["out_specs"]] if single else list(kw["out_specs"])
        res = self._pallas(body, len(args), len(out_shape), kw.get("scratch_shapes", []), grid=kw["grid"], in_specs=kw["in_specs"],
                           out_specs=out_specs, out_shape=out_shape, name=kw["name"],
                           aliases=kw.get("input_output_aliases", {}))(*args, *self.inputs)
        self.results = list(res[len(out_shape):])
        return res[0] if single else list(res[:len(out_shape)])

    def run_alone(self, name):
        self.results = list(self._pallas(None, 0, 0, [], name=name)(*self.inputs))


def _call(comm, body, **kw):
    if comm is None:
        return pl.pallas_call(body, **kw)
    return functools.partial(comm.run, body, kw)


def _in_proj_streamed(x, gain, car, o_w, place, *, bm, name):
    m, k = x.shape
    n = car.out_shapes[o_w].shape[2]
    ni, half = m // bm, k // 2
    k_in, k_out = len(car.inputs), len(car.out_shapes)
    order = lambda p: ((p & 1) << 1) | (p >> 1)

    def body(place_ref, x_ref, g_ref, *refs):
        cins, (hn_ref, o_ref, ob_ref), couts = refs[:k_in], refs[k_in:k_in + 3], refs[k_in + 3:k_in + 3 + k_out]
        wbuf, hn_all, local, ici_send, ici_recv, d2d_send, d2d_recv, send, recv = refs[k_in + 3 + k_out:]
        p, i = pl.program_id(0), pl.program_id(1)
        x, y, c, chips = _place()
        me = 2 * x + y
        rows = lambda chunk, cc: couts[o_w].at[chunk].at[pl.ds(cc * half, half)]

        @pl.when(jnp.logical_and(p == 0, i == 0))
        def _():
            for j, (px, py) in enumerate(chips):
                _remote(rows(me, c), rows(me, c), ici_send.at[j], ici_recv.at[j], (px, py, c)).start()
            for op in car.ops:
                op[0](cins, couts, send, recv)

        for j, (px, py) in enumerate(chips):
            @pl.when(jnp.logical_and(p == j + 1, i == 0))
            def _(j=j, px=px, py=py):
                landed, other = rows(2 * px + py, c), rows(2 * px + py, 1 - c)
                _remote(landed, landed, ici_send.at[j], ici_recv.at[j], (px, py, c)).wait_recv()
                _remote(landed, landed, d2d_send.at[j], d2d_recv.at[j], (x, y, 1 - c)).start()
                _remote(other, other, d2d_send.at[j], d2d_recv.at[j], (x, y, 1 - c)).wait_recv()

        @pl.when(i == 0)
        def _():
            cp = pltpu.make_async_copy(couts[o_w].at[me ^ order(p)], wbuf, local.at[0])
            cp.start()
            cp.wait()

        tile = pl.ds(pl.multiple_of(i * bm, bm), bm)

        @pl.when(p == 0)
        def _():
            xv = x_ref[...]
            hn_all[tile, :] = ((xv * _rstd(xv)) * g_ref[...]).astype(_MXU)

        hn = hn_all[tile, :]
        hn_ref[...] = hn
        res = _dot(hn, wbuf[...])
        o_ref[...] = res
        ob_ref[...] = res.astype(ob_ref.dtype)

        @pl.when(jnp.logical_and(p == 3, i == ni - 1))
        def _():
            for j, (px, py) in enumerate(chips):
                _remote(rows(me, c), rows(me, c), ici_send.at[j], ici_recv.at[j], (px, py, c)).wait_send()
                _remote(rows(me, c), rows(me, c), d2d_send.at[j], d2d_recv.at[j], (x, y, 1 - c)).wait_send()
            for op in car.ops:
                op[1](cins, couts, send, recv)

    ospec = pl.BlockSpec((bm, n), lambda p, i, place_ref: (i, place_ref[0] ^ order(p)))
    rows = pl.BlockSpec((bm, k), lambda p, i, place_ref: (jnp.where(p == 0, i, 0), 0))
    three, sems = pltpu.SemaphoreType.DMA((3,)), pltpu.SemaphoreType.DMA((max(car.n_sems, 1),))
    res = pl.pallas_call(
        body,
        grid_spec=pltpu.PrefetchScalarGridSpec(
            num_scalar_prefetch=1, grid=(4, ni),
            in_specs=[rows, pl.BlockSpec((1, k), lambda p, i, place_ref: (0, 0))] + [_ANY] * k_in,
            out_specs=[pl.BlockSpec((bm, k), lambda p, i, place_ref: (p * ni + i, 0)), ospec, ospec] + [_ANY] * k_out,
            scratch_shapes=[pltpu.VMEM((k, n), _MXU), pltpu.VMEM((m, k), _MXU), pltpu.SemaphoreType.DMA((1,)),
                            three, three, three, three, sems, sems]),
        out_shape=[S((4 * m, k), _MXU), S((m, 4 * n), F32), S((m, 4 * n), _MXU)] + car.out_shapes,
        input_output_aliases={3 + a: 3 + o for a, o in car.aliases.items()},
        compiler_params=_cp("arbitrary", "arbitrary"), name=name)(place, x, gain, *car.inputs)
    car.results = list(res[3:])
    return res[0], res[1], res[2]


def _mm_nn(a, b3, *, bm, bn, name, also=None, comm=None):
    m, k = a.shape
    c, _, n = b3.shape
    ni, nj = m // bm, n // bn

    def body(a_ref, b_ref, *o_refs):
        res = _dot(a_ref[...], b_ref[...])
        for o_ref in o_refs:
            o_ref[...] = res.astype(o_ref.dtype)

    ospec = pl.BlockSpec((bm, bn), lambda cc, j, i: (i, cc * nj + j))
    dtypes = [F32] + ([] if also is None else [also])
    out = _call(
        comm, body, grid=(c, nj, ni),
        in_specs=[pl.BlockSpec((bm, k), lambda cc, j, i: (i, 0)), pl.BlockSpec((None, k, bn), lambda cc, j, i: (cc, 0, j))],
        out_specs=[ospec] * len(dtypes), out_shape=[S((m, c * n), dt) for dt in dtypes],
        compiler_params=_cp("parallel", "parallel", "parallel"), name=name)(a, b3)
    return out[0] if also is None else out


def _mm_nt(a, b3, *, bm, bo, out_dtype, name, comm=None):
    m = a.shape[0]
    c, ko, n = b3.shape
    ni, nj = m // bm, ko // bo

    def body(a_ref, b_ref, o_ref):
        acc = _dot_nt(a_ref[:, 0:n], b_ref[0])
        for cc in range(1, c):
            acc = acc + _dot_nt(a_ref[:, cc * n:(cc + 1) * n], b_ref[cc])
        o_ref[...] = acc.astype(o_ref.dtype)

    return _call(
        comm, body, grid=(nj, ni),
        in_specs=[pl.BlockSpec((bm, c * n), lambda j, i: (i, 0)),
                  pl.BlockSpec((c, bo, n), lambda j, i: (0, j, 0))],
        out_specs=pl.BlockSpec((bm, bo), lambda j, i: (i, j)),
        out_shape=S((m, ko), out_dtype),
        compiler_params=_cp("parallel", "parallel"), name=name)(a, b3)


def _mm_tn(a, b, c, *, bm, bk, out_dtype, name, comm=None):
    m, k = b.shape[0], a.shape[1]
    n = b.shape[1] // c
    nm, nk = m // bm, k // bk

    def body(a_ref, b_ref, o_ref, acc):
        mm = pl.program_id(2)

        @pl.when(mm == 0)
        def _():
            acc[...] = jnp.zeros_like(acc)

        acc[...] += _dot_tn(a_ref[...], b_ref[...])

        @pl.when(mm == nm - 1)
        def _():
            o_ref[...] = acc[...].astype(o_ref.dtype)

    return _call(
        comm, body, grid=(c, nk, nm),
        in_specs=[pl.BlockSpec((bm, bk), lambda cc, j, mm: (mm, j)),
                  pl.BlockSpec((bm, n), lambda cc, j, mm: (mm, cc))],
        out_specs=pl.BlockSpec((None, bk, n), lambda cc, j, mm: (cc, j, 0)),
        out_shape=S((c, k, n), out_dtype),
        scratch_shapes=[pltpu.VMEM((bk, n), F32)],
        compiler_params=_cp("parallel", "parallel", "arbitrary"), name=name)(a, b)


def _swiglu_fwd(hn, wg3, wu3, *, bm, name, comm=None):
    m, k = hn.shape
    c, _, n = wg3.shape

    def body(a_ref, g_ref, u_ref, dgate_ref, dup_ref, act_ref):
        a = a_ref[...]
        gate = _dot(a, g_ref[...])
        up = _dot(a, u_ref[...])
        sg = jax.nn.sigmoid(gate)
        silu = gate * sg
        dgate_ref[...] = (up * (sg * (1.0 + gate * (1.0 - sg)))).astype(dgate_ref.dtype)
        dup_ref[...] = silu.astype(dup_ref.dtype)
        act_ref[...] = (silu * up).astype(act_ref.dtype)

    wspec = pl.BlockSpec((None, k, n), lambda cc, i: (cc, 0, 0))
    ospec = pl.BlockSpec((bm, n), lambda cc, i: (i, cc))
    return _call(
        comm, body, grid=(c, m // bm),
        in_specs=[pl.BlockSpec((bm, k), lambda cc, i: (i, 0)), wspec, wspec],
        out_specs=[ospec, ospec, ospec],
        out_shape=[S((m, c * n), _MXU), S((m, c * n), _MXU), S((m, c * n), _MXU)],
        compiler_params=_cp("parallel", "parallel"), name=name)(hn, wg3, wu3)


def _swiglu_bwd(df, wd, act_dgate, act_dup, *, bm, bo, name):
    m, k = df.shape
    ko = wd.shape[0]

    def body(a_ref, b_ref, g_ref, u_ref, dg_ref, du_ref):
        dact = _dot_nt(a_ref[...], b_ref[...])
        dg_ref[...] = (dact * g_ref[...].astype(F32)).astype(dg_ref.dtype)
        du_ref[...] = (dact * u_ref[...].astype(F32)).astype(du_ref.dtype)

    ospec = pl.BlockSpec((bm, bo), lambda j, i: (i, j))
    return pl.pallas_call(
        body, grid=(ko // bo, m // bm),
        in_specs=[pl.BlockSpec((bm, k), lambda j, i: (i, 0)), pl.BlockSpec((bo, k), lambda j, i: (j, 0)), ospec, ospec],
        out_specs=[ospec, ospec],
        out_shape=[S((m, ko), _MXU), S((m, ko), _MXU)],
        compiler_params=_cp("parallel", "parallel"), name=name)(df, wd, act_dgate, act_dup)


def _rms_fwd(x, gain, name):
    t, d = x.shape
    tm = min(t, ROW_TILE)

    def body(x_ref, g_ref, o_ref):
        xv = x_ref[...]
        o_ref[...] = ((xv * _rstd(xv)) * g_ref[...]).astype(o_ref.dtype)

    return pl.pallas_call(body, grid=(t // tm,), in_specs=[_row_spec(tm, d), _vec_spec(d)], out_specs=_row_spec(tm, d),
                          out_shape=S((t, d), _MXU), compiler_params=_cp("parallel"), name=name)(x, gain)


def _outnorm_fwd(o, yl, ga, gl, name, comm=None):
    t, w = o.shape
    tm = min(t, ROW_TILE)

    def body(o_ref, l_ref, ga_ref, gl_ref, y_ref):
        ov, lv = o_ref[...], l_ref[...]
        y_ref[:, :w] = ((ov * _rstd(ov)) * ga_ref[...]).astype(y_ref.dtype)
        y_ref[:, w:] = ((lv * _rstd(lv)) * gl_ref[...]).astype(y_ref.dtype)

    return _call(comm, body, grid=(t // tm,), in_specs=[_row_spec(tm, w), _row_spec(tm, w), _vec_spec(w), _vec_spec(w)],
                 out_specs=_row_spec(tm, 2 * w), out_shape=S((t, 2 * w), _MXU),
                 compiler_params=_cp("parallel"), name=name)(o, yl, ga, gl)


def _mid_fwd(x, mix, g_post, g_pre, name, comm=None):
    t, d = x.shape
    tm = min(t, ROW_TILE)

    def body(x_ref, m_ref, gp_ref, gn_ref, x2_ref, hn_ref):
        mv = m_ref[...]
        x2 = x_ref[...] + (mv * _rstd(mv)) * gp_ref[...]
        x2_ref[...] = x2
        hn_ref[...] = ((x2 * _rstd(x2)) * gn_ref[...]).astype(hn_ref.dtype)

    return _call(comm, body, grid=(t // tm,), in_specs=[_row_spec(tm, d), _row_spec(tm, d), _vec_spec(d), _vec_spec(d)],
                          out_specs=[_row_spec(tm, d), _row_spec(tm, d)], out_shape=[S((t, d), F32), S((t, d), _MXU)],
                          compiler_params=_cp("parallel"), name=name)(x, mix, g_post, g_pre)


def _final(f, x2, target, g_post, name):
    t, d = f.shape
    tm = min(t, ROW_TILE // 2)

    def body(f_ref, x2_ref, t_ref, g_ref, loss_ref, dout_ref, df_ref, dg_ref):
        @pl.when(pl.program_id(0) == 0)
        def _():
            loss_ref[...] = jnp.zeros_like(loss_ref)
            dg_ref[...] = jnp.zeros_like(dg_ref)

        fv = f_ref[...]
        r = _rstd(fv)
        fh = fv * r
        err = (x2_ref[...] + fh * g_ref[...]) - t_ref[...]
        loss_ref[...] += jnp.sum(err * err, axis=0, keepdims=True)
        dout = err * (1.0 / d)
        dout_ref[...] = dout
        dfv, dg = _rms_bwd(dout, fh, r, g_ref[...])
        df_ref[...] = dfv.astype(df_ref.dtype)
        dg_ref[...] += dg

    return pl.pallas_call(
        body, grid=(t // tm,),
        in_specs=[_row_spec(tm, d), _row_spec(tm, d), _row_spec(tm, d), _vec_spec(d)],
        out_specs=[_vec_spec(d), _row_spec(tm, d), _row_spec(tm, d), _vec_spec(d)],
        out_shape=[S((1, d), F32), S((t, d), F32), S((t, d), _MXU), S((1, d), F32)],
        compiler_params=_cp("arbitrary"), name=name)(f, x2, target, g_post)


def _mid_bwd(dhn_a, dhn_b, dout, x2, mix, g_pre, g_post, name, comm=None):
    t, d = x2.shape
    tm = min(t, ROW_TILE // 2)

    def body(da_ref, db_ref, do_ref, x2_ref, m_ref, gn_ref, gp_ref, dx2_ref, dm_ref, dgn_ref, dgp_ref):
        @pl.when(pl.program_id(0) == 0)
        def _():
            dgn_ref[...] = jnp.zeros_like(dgn_ref)
            dgp_ref[...] = jnp.zeros_like(dgp_ref)

        x2 = x2_ref[...]
        r = _rstd(x2)
        dxa, dgn = _rms_bwd(da_ref[...] + db_ref[...], x2 * r, r, gn_ref[...])
        dx2 = do_ref[...] + dxa
        dx2_ref[...] = dx2
        dgn_ref[...] += dgn
        mv = m_ref[...]
        rm = _rstd(mv)
        dmv, dgp = _rms_bwd(dx2, mv * rm, rm, gp_ref[...])
        dm_ref[...] = dmv.astype(dm_ref.dtype)
        dgp_ref[...] += dgp

    rs, vs = _row_spec(tm, d), _vec_spec(d)
    return _call(
        comm, body, grid=(t // tm,), in_specs=[rs, rs, rs, rs, rs, vs, vs], out_specs=[rs, rs, vs, vs],
        out_shape=[S((t, d), F32), S((t, d), _MXU), S((1, d), F32), S((1, d), F32)],
        compiler_params=_cp("arbitrary"), name=name)(dhn_a, dhn_b, dout, x2, mix, g_pre, g_post)


def _first_bwd(dhn, dx2, x, gain, name, comm=None):
    t, d = x.shape
    tm = min(t, ROW_TILE)

    def body(dh_ref, dx2_ref, x_ref, g_ref, dx_ref, dg_ref):
        @pl.when(pl.program_id(0) == 0)
        def _():
            dg_ref[...] = jnp.zeros_like(dg_ref)

        xv = x_ref[...]
        r = _rstd(xv)
        dxa, dg = _rms_bwd(dh_ref[...], xv * r, r, g_ref[...])
        dx_ref[...] = dx2_ref[...] + dxa
        dg_ref[...] += dg

    rs, vs = _row_spec(tm, d), _vec_spec(d)
    return _call(comm, body, grid=(t // tm,), in_specs=[rs, rs, rs, vs], out_specs=[rs, vs],
                          out_shape=[S((t, d), F32), S((1, d), F32)], compiler_params=_cp("arbitrary"), name=name)(dhn, dx2, x, gain)


def _outnorm_bwd(dy, o, yl, ga, gl, name, comm=None):
    t, w = o.shape
    tm = min(t, ROW_TILE)

    def body(dy_ref, o_ref, l_ref, ga_ref, gl_ref, do_ref, dl_ref, dga_ref, dgl_ref):
        @pl.when(pl.program_id(0) == 0)
        def _():
            dga_ref[...] = jnp.zeros_like(dga_ref)
            dgl_ref[...] = jnp.zeros_like(dgl_ref)

        ov, lv = o_ref[...], l_ref[...]
        ra, rl = _rstd(ov), _rstd(lv)
        dov, dga = _rms_bwd(dy_ref[:, :w], ov * ra, ra, ga_ref[...])
        dlv, dgl = _rms_bwd(dy_ref[:, w:], lv * rl, rl, gl_ref[...])
        do_ref[...] = dov.astype(do_ref.dtype)
        dl_ref[...] = dlv
        dga_ref[...] += dga
        dgl_ref[...] += dgl

    rs, vs = _row_spec(tm, w), _vec_spec(w)
    return _call(comm, body, grid=(t // tm,), in_specs=[_row_spec(tm, 2 * w), rs, rs, vs, vs], out_specs=[rs, rs, vs, vs],
                          out_shape=[S((t, w), _MXU), S((t, w), F32), S((1, w), F32), S((1, w), F32)],
                          compiler_params=_cp("arbitrary"), name=name)(dy, o, yl, ga, gl)


def _tri_sum(v, tri):
    return _dot(v.astype(_MXU), tri)


def _attn_tile(qb, kb, row, col, shift, scale):
    z = _dot_nt(qb, kb) * scale
    mask = (col + shift) < row
    lb = _log_sigmoid(z)
    lm = jnp.where(mask, lb - z, 0.0)
    return mask, lb, lm


def _attn_fwd(proj, n_heads, name, comm=None):
    t = proj.shape[0]
    bq = min(t, ATTN_BLOCK)
    nq = t // bq
    scale = 1.0 / math.sqrt(HEAD_DIM)

    heads = [slice(a * HEAD_DIM, (a + 1) * HEAD_DIM) for a in range(ATTN_HEADS)]

    def body(q_ref, k_ref, v_ref, o_ref):
        row = lax.broadcasted_iota(jnp.int32, (bq, bq), 0)
        col = lax.broadcasted_iota(jnp.int32, (bq, bq), 1)
        tri = (row > col).astype(_MXU)

        def per_q(qi, _):
            q0 = pl.multiple_of(qi * bq, bq)
            qbs = [q_ref[pl.ds(q0, bq), hd] for hd in heads]

            def cond(st):
                return jnp.logical_and(st[0] >= 0, st[1])

            def step(st):
                kj, _, carries, accs = st
                k0 = pl.multiple_of(kj * bq, bq)
                alive, new_carries, new_accs = None, [], []
                for hd, qb, carry, acc in zip(heads, qbs, carries, accs):
                    mask, lb, lm = _attn_tile(qb, k_ref[pl.ds(k0, bq), hd], row, col, (kj - qi) * bq, scale)
                    w = jnp.where(mask, jnp.exp(lb + _tri_sum(lm, tri) + carry), 0.0)
                    new_accs.append(acc + _dot(w.astype(_MXU), v_ref[pl.ds(k0, bq), hd]))
                    carry = carry + jnp.sum(lm, axis=1, keepdims=True)
                    new_carries.append(carry)
                    live = jnp.max(carry) > EXP_CUT
                    alive = live if alive is None else jnp.logical_or(alive, live)
                return kj - 1, alive, tuple(new_carries), tuple(new_accs)

            st = lax.while_loop(cond, step, (qi, jnp.bool_(True), (jnp.zeros((bq, 1), F32),) * ATTN_HEADS,
                                             (jnp.zeros((bq, HEAD_DIM), F32),) * ATTN_HEADS))
            for hd, acc in zip(heads, st[3]):
                o_ref[pl.ds(q0, bq), hd] = acc
            return 0

        lax.fori_loop(0, nq, per_q, 0)

    groups = n_heads // ATTN_HEADS
    hs = lambda off: pl.BlockSpec((t, ATTN_HEADS * HEAD_DIM), lambda h: (0, off + h))
    return _call(
        comm, body, grid=(groups,), in_specs=[hs(0), hs(groups), hs(2 * groups)], out_specs=hs(0),
        out_shape=S((t, n_heads * HEAD_DIM), F32), compiler_params=_cp("parallel"), name=name)(proj, proj, proj)


def _emit(blocks, out_ref, starts, sems):
    copies = [pltpu.make_async_copy(b, out_ref.at[:, pl.ds(c0, b.shape[1])], sems.at[k]) for k, (b, c0) in enumerate(zip(blocks, starts))]
    for cp in copies:
        cp.start()
    for cp in copies:
        cp.wait()


def _attn_bwd(proj, do, dproj, n_heads, name, comm=None):
    t = proj.shape[0]
    bq = min(t, ATTN_BLOCK)
    nq = t // bq
    scale = 1.0 / math.sqrt(HEAD_DIM)
    groups = n_heads // ATTN_HEADS
    wide = ATTN_HEADS * HEAD_DIM

    heads = [slice(a * HEAD_DIM, (a + 1) * HEAD_DIM) for a in range(ATTN_HEADS)]

    def body(q_ref, k_ref, v_ref, do_ref, _, dproj_ref, dka_ref, dva_ref, g_ref, b_ref, dq_ref, dk_ref, dv_ref, out_sems):
        group = pl.program_id(0)
        dka_ref[...] = jnp.zeros_like(dka_ref)
        dva_ref[...] = jnp.zeros_like(dva_ref)
        row = lax.broadcasted_iota(jnp.int32, (bq, bq), 0)
        col = lax.broadcasted_iota(jnp.int32, (bq, bq), 1)
        tri = (row > col).astype(_MXU)
        tri_lt = (row < col).astype(_MXU)

        def per_q(qi, _):
            q0 = pl.multiple_of(qi * bq, bq)
            qbs = [q_ref[pl.ds(q0, bq), hd] for hd in heads]
            dobs = [do_ref[pl.ds(q0, bq), hd] for hd in heads]

            def cond(st):
                return jnp.logical_and(st[0] >= 0, st[1])

            def step(st):
                kj, _, carries = st
                k0 = pl.multiple_of(kj * bq, bq)
                alive, new_carries = None, []
                for a, (hd, qb, dob, carry) in enumerate(zip(heads, qbs, dobs, carries)):
                    mask, lb, lm = _attn_tile(qb, k_ref[pl.ds(k0, bq), hd], row, col, (kj - qi) * bq, scale)
                    w = jnp.where(mask, jnp.exp(lb + _tri_sum(lm, tri) + carry), 0.0)
                    g_ref[a, pl.ds(k0, bq), :] = w * _dot_nt(dob, v_ref[pl.ds(k0, bq), hd])
                    b_ref[a, pl.ds(k0, bq), :] = jnp.where(mask, jnp.exp(lb), 0.0)
                    dva_ref[pl.ds(k0, bq), hd] += _dot_tn(w.astype(_MXU), dob)
                    carry = carry + jnp.sum(lm, axis=1, keepdims=True)
                    new_carries.append(carry)
                    live = jnp.max(carry) > EXP_CUT
                    alive = live if alive is None else jnp.logical_or(alive, live)
                return kj - 1, alive, tuple(new_carries)

            st = lax.while_loop(cond, step, (qi, jnp.bool_(True), (jnp.zeros((bq, 1), F32),) * ATTN_HEADS))

            def back(kj, st2):
                k0 = pl.multiple_of(kj * bq, bq)
                out = []
                for a, (hd, qb, (before, dq)) in enumerate(zip(heads, qbs, st2)):
                    g = g_ref[a, pl.ds(k0, bq), :]
                    beta = b_ref[a, pl.ds(k0, bq), :]
                    dz = ((g * (1.0 - beta) - (before + _tri_sum(g, tri_lt)) * beta) * scale).astype(_MXU)
                    dka_ref[pl.ds(k0, bq), hd] += _dot_tn(dz, qb)
                    out.append((before + jnp.sum(g, axis=1, keepdims=True), dq + _dot(dz, k_ref[pl.ds(k0, bq), hd])))
                return tuple(out)

            st2 = lax.fori_loop(st[0] + 1, qi + 1, back, ((jnp.zeros((bq, 1), F32), jnp.zeros((bq, HEAD_DIM), F32)),) * ATTN_HEADS)
            for hd, (_, dq) in zip(heads, st2):
                dq_ref[pl.ds(q0, bq), hd] = dq.astype(dq_ref.dtype)
            return 0

        lax.fori_loop(0, nq, per_q, 0)
        dk_ref[...] = dka_ref[...].astype(dk_ref.dtype)
        dv_ref[...] = dva_ref[...].astype(dv_ref.dtype)
        _emit([dq_ref, dk_ref, dv_ref], dproj_ref, [(a * groups + group) * wide for a in range(3)], out_sems)

    hs = lambda off: pl.BlockSpec((t, wide), lambda h: (0, off + h))
    return _call(
        comm, body, grid=(groups,), in_specs=[hs(0), hs(groups), hs(2 * groups), hs(0), _ANY], out_specs=_ANY,
        out_shape=S(dproj.shape, dproj.dtype), input_output_aliases={4: 0},
        scratch_shapes=[pltpu.VMEM((t, wide), F32), pltpu.VMEM((t, wide), F32),
                        pltpu.VMEM((ATTN_HEADS, t, bq), F32), pltpu.VMEM((ATTN_HEADS, t, bq), F32)]
        + [pltpu.VMEM((t, wide), dproj.dtype)] * 3 + [pltpu.SemaphoreType.DMA((3,))],
        compiler_params=_cp("parallel"), name=name)(proj, proj, proj, do, dproj)


def _shift_down(cur, prev8, k):
    if k == 0:
        return cur
    row8 = lax.broadcasted_iota(jnp.int32, prev8.shape, 0)
    rc = pltpu.roll(cur, k, 0)
    top = jnp.where(row8 < k, pltpu.roll(prev8, k, 0), rc[0:8, :])
    return jnp.concatenate([top, rc[8:, :]], axis=0)


def _shift_up(cur, next8, k):
    if k == 0:
        return cur
    n = cur.shape[0]
    row8 = lax.broadcasted_iota(jnp.int32, next8.shape, 0)
    rc = pltpu.roll(cur, n - k, 0)
    bottom = jnp.where(row8 >= 8 - k, pltpu.roll(next8, 8 - k, 0), rc[n - 8:, :])
    return jnp.concatenate([rc[:n - 8, :], bottom], axis=0)


def _lru_conv(xl, prev8, cw, cb):
    xs = [_shift_down(xl, prev8, CONV_WIDTH - 1 - k) for k in range(CONV_WIDTH)]
    xc = xs[0] * cw[0:1, :]
    for k in range(1, CONV_WIDTH):
        xc = xc + xs[k] * cw[k:k + 1, :]
    return xs, xc + cb


def _lru_gates(xl, prev8, cw, cb, wr, br, wi, bi, ls):
    xs, xc = _lru_conv(xl, prev8, cw, cb)
    xcb = xc.astype(_MXU)
    r = jax.nn.sigmoid(_dot(xcb, wr) + br)
    i = jax.nn.sigmoid(_dot(xcb, wi) + bi)
    la = (LRU_C * r) * ls
    a = jnp.exp(la)
    mult = jnp.sqrt(-_expm1(2.0 * la))
    return xs, xc, r, i, a, mult


def _group_scan(a, b, reverse):
    n = a.shape[0]
    row = lax.broadcasted_iota(jnp.int32, a.shape, 0) % 8
    for d in (1, 2, 4):
        if reverse:
            m = row < 8 - d
            a_s, b_s = pltpu.roll(a, n - d, 0), pltpu.roll(b, n - d, 0)
        else:
            m = row >= d
            a_s, b_s = pltpu.roll(a, d, 0), pltpu.roll(b, d, 0)
        b = jnp.where(m, a * b_s + b, b)
        a = jnp.where(m, a * a_s, a)
    return a, b


def _lru_fwd(proj, col0, n_blocks, cw, cb, wr, br, wi, bi, lam, name, comm=None):
    t = proj.shape[0]
    tt = min(t, SEQ_TILE)
    nt = t // tt

    def body(xl_ref, gl_ref, cw_ref, cb_ref, wr_ref, br_ref, wi_ref, bi_ref, lam_ref, h_ref, y_ref, *kept):
        cwv, cbv, brv, biv = cw_ref[...], cb_ref[...], br_ref[...], bi_ref[...]
        wrv, wiv = wr_ref[...].astype(_MXU), wi_ref[...].astype(_MXU)
        ls = _log_sigmoid(lam_ref[...])

        def tile(ti, hin):
            t0 = pl.multiple_of(ti * tt, tt)
            p0 = pl.multiple_of(jnp.maximum(t0 - 8, 0), 8)
            prev8 = xl_ref[pl.ds(p0, 8), :] * (ti > 0).astype(F32)
            xl = xl_ref[pl.ds(t0, tt), :]
            _, xc, r, ig, a, mult = _lru_gates(xl, prev8, cwv, cbv, wrv, brv, wiv, biv, ls)
            for ref, val in zip(kept, (r, ig, a, mult)):
                ref[pl.ds(t0, tt), :] = val
            ga, gb = _group_scan(a, mult * (ig * xc), False)
            for g in range(tt // 8):
                hg = ga[8 * g:8 * g + 8, :] * hin + gb[8 * g:8 * g + 8, :]
                h_ref[pl.ds(t0 + 8 * g, 8), :] = hg
                hin = hg[7:8, :]
            y_ref[pl.ds(t0, tt), :] = h_ref[pl.ds(t0, tt), :] * _gelu(gl_ref[pl.ds(t0, tt), :])
            return hin

        lax.fori_loop(0, nt, tile, jnp.zeros((1, HEAD_DIM), F32))

    cs = lambda off: pl.BlockSpec((t, HEAD_DIM), lambda n: (0, off + n))
    vs = pl.BlockSpec((1, HEAD_DIM), lambda n: (0, n))
    ws = pl.BlockSpec((None, HEAD_DIM, HEAD_DIM), lambda n: (n, 0, 0))
    w = n_blocks * HEAD_DIM
    return _call(
        comm, body, grid=(n_blocks,),
        in_specs=[cs(col0), cs(col0 + n_blocks), pl.BlockSpec((CONV_WIDTH, HEAD_DIM), lambda n: (0, n)), vs, ws, vs, ws, vs, vs],
        out_specs=[cs(0)] * 6, out_shape=[S((t, w), F32)] * 6,
        compiler_params=_cp("parallel"), name=name)(proj, proj, cw, cb, wr, br, wi, bi, lam)


def _lru_bwd(proj, col0, n_blocks, h, kept, dyl, cw, cb, wr, wi, lam, name, comm=None):
    t = proj.shape[0]
    tt = min(t, SEQ_TILE)
    nt = t // tt

    def body(xl_ref, gl_ref, h_ref, r_ref, i_ref, a_ref, m_ref, dy_ref, cw_ref, cb_ref, wr_ref, wi_ref, lam_ref,
             dproj_ref, dcw_ref, dcb_ref, dwr_ref, dbr_ref, dwi_ref, dbi_ref, dlam_ref, g_ref, dxl_ref, dgl_ref, out_sems):
        block = pl.program_id(0)
        cwv, cbv = cw_ref[...], cb_ref[...]
        wrv, wiv = wr_ref[...].astype(_MXU), wi_ref[...].astype(_MXU)
        lamv = lam_ref[...]
        ls = _log_sigmoid(lamv)
        for ref in (dcw_ref, dcb_ref, dwr_ref, dbr_ref, dwi_ref, dbi_ref, dlam_ref):
            ref[...] = jnp.zeros_like(ref)

        def tile(s, carry):
            e_in, dxc_next8 = carry
            ti = nt - 1 - s
            t0 = pl.multiple_of(ti * tt, tt)
            p0 = pl.multiple_of(jnp.maximum(t0 - 8, 0), 8)
            first = (ti > 0).astype(F32)
            xl = xl_ref[pl.ds(t0, tt), :]
            xs, xc = _lru_conv(xl, xl_ref[pl.ds(p0, 8), :] * first, cwv, cbv)
            r, ig, a, mult = (ref[pl.ds(t0, tt), :] for ref in (r_ref, i_ref, a_ref, m_ref))
            hv = h_ref[pl.ds(t0, tt), :]
            h_before = _shift_down(hv, h_ref[pl.ds(p0, 8), :] * first, 1)
            glv = gl_ref[pl.ds(t0, tt), :]
            dyv = dy_ref[pl.ds(t0, tt), :]
            dgl_ref[pl.ds(t0, tt), :] = (dyv * hv * _gelu_grad(glv)).astype(dgl_ref.dtype)
            dh = dyv * _gelu(glv)
            row = lax.broadcasted_iota(jnp.int32, a.shape, 0)
            coef = jnp.where(row == tt - 1, 1.0, pltpu.roll(a, tt - 1, 0))
            ga, gb = _group_scan(coef, dh, True)
            gin = e_in
            for g in reversed(range(tt // 8)):
                gg = ga[8 * g:8 * g + 8, :] * gin + gb[8 * g:8 * g + 8, :]
                g_ref[8 * g:8 * g + 8, :] = gg
                gin = gg[0:1, :]
            gv = g_ref[...]
            e_out = a[0:1, :] * gv[0:1, :]
            ix = ig * xc
            dla = (gv * h_before) * a - (gv * ix) * (a * a / mult)
            dlam_ref[...] += jnp.sum(dla * (LRU_C * r), axis=0, keepdims=True)
            dpr = (dla * (LRU_C * ls)) * (r * (1.0 - r))
            dpi = (gv * mult * xc) * (ig * (1.0 - ig))
            dbr_ref[...] += jnp.sum(dpr, axis=0, keepdims=True)
            dbi_ref[...] += jnp.sum(dpi, axis=0, keepdims=True)
            xcb, dprb, dpib = xc.astype(_MXU), dpr.astype(_MXU), dpi.astype(_MXU)
            dwr_ref[...] += _dot_tn(xcb, dprb)
            dwi_ref[...] += _dot_tn(xcb, dpib)
            dxc = gv * mult * ig + _dot_nt(dprb, wrv) + _dot_nt(dpib, wiv)
            dcb_ref[...] += jnp.sum(dxc, axis=0, keepdims=True)
            dxl = None
            for k in range(CONV_WIDTH):
                dcw_ref[k:k + 1, :] += jnp.sum(dxc * xs[k], axis=0, keepdims=True)
                term = _shift_up(dxc, dxc_next8, CONV_WIDTH - 1 - k) * cwv[k:k + 1, :]
                dxl = term if dxl is None else dxl + term
            dxl_ref[pl.ds(t0, tt), :] = dxl.astype(dxl_ref.dtype)
            return e_out, dxc[0:8, :]

        lax.fori_loop(0, nt, tile, (jnp.zeros((1, HEAD_DIM), F32), jnp.zeros((8, HEAD_DIM), F32)))
        dlam_ref[...] = dlam_ref[...] * (1.0 - jax.nn.sigmoid(lamv))
        _emit([dxl_ref, dgl_ref], dproj_ref, [(col0 + block) * HEAD_DIM, (col0 + n_blocks + block) * HEAD_DIM], out_sems)

    cs = lambda off: pl.BlockSpec((t, HEAD_DIM), lambda n: (0, off + n))
    vs = pl.BlockSpec((1, HEAD_DIM), lambda n: (0, n))
    ws = pl.BlockSpec((None, HEAD_DIM, HEAD_DIM), lambda n: (n, 0, 0))
    cws = pl.BlockSpec((CONV_WIDTH, HEAD_DIM), lambda n: (0, n))
    w = n_blocks * HEAD_DIM
    vec = S((1, w), F32)
    mat = S((n_blocks, HEAD_DIM, HEAD_DIM), F32)
    return _call(
        comm, body, grid=(n_blocks,),
        in_specs=[cs(col0), cs(col0 + n_blocks)] + [cs(0)] * 6 + [cws, vs, ws, ws, vs],
        out_specs=[_ANY, cws, vs, ws, vs, ws, vs, vs],
        out_shape=[S(proj.shape, _MXU), S((CONV_WIDTH, w), F32), vec, mat, vec, mat, vec, vec],
        scratch_shapes=[pltpu.VMEM((tt, HEAD_DIM), F32), pltpu.VMEM((t, HEAD_DIM), _MXU), pltpu.VMEM((t, HEAD_DIM), _MXU),
                        pltpu.SemaphoreType.DMA((2,))],
        compiler_params=_cp("parallel"), name=name)(proj, proj, h, *kept, dyl, cw, cb, wr, wi, lam)


class _NoExchange:
    grad_dtype = F32

    def __init__(self, weights):
        self.weights, self.grads, self.packs = weights, {}, {}

    def weight(self, name):
        return self.weights[name]

    def in_proj(self, x, gain, bm):
        hn = _rms_fwd(x, gain, "rms1")
        return [hn, *_mm_nn(hn, self.weights["w_in"], bm=bm, bn=self.weights["w_in"].shape[2], name="in_proj", also=_MXU)]

    def conv_w(self):
        return self.weights["conv_w"]

    def carrier(self, call):
        return None

    def harvest(self, car):
        pass

    def alone(self, call):
        pass


def _local_step(x, target, norms, ex, cb, wr, br, wi, bi, lam, ga, gl):
    g_pre_mix, g_post_mix, g_pre_ffn, g_post_ffn = norms
    t, d = x.shape
    bm = min(t, MM_ROWS)
    bt = min(t, DW_TOKENS)

    def run(fn, name, *args, **kw):
        car = ex.carrier(name)
        out = fn(*args, name=name, comm=car, **kw)
        ex.harvest(car)
        return out

    hn1, proj, proj_mx = ex.in_proj(x, g_pre_mix, bm)
    win3, cw = ex.weight("w_in"), ex.conv_w()
    c = win3.shape[0]
    o = run(_attn_fwd, "attn_fwd", proj_mx, (proj.shape[1] - d) // 3 // HEAD_DIM)
    mix = 2 * o.shape[1]
    n_heads = n_blocks = o.shape[1] // HEAD_DIM
    h, yl, *kept = run(_lru_fwd, "lru_fwd", proj, 3 * n_heads, n_blocks, cw, cb, wr, br, wi, bi, lam)
    y = run(_outnorm_fwd, "outnorm_fwd", o, yl, ga, gl)
    wout = ex.weight("w_out")
    mixo = run(_mm_nn, "out_proj", y, wout[None], bm=bm, bn=d)
    x2, hn2 = run(_mid_fwd, "mid_fwd", x, mixo, g_post_mix, g_pre_ffn)
    ex.alone("gather_w_up_last")
    wg3, wu3 = ex.weight("w_ffn_gate"), ex.weight("w_ffn_up")
    act_dgate, act_dup, act = run(_swiglu_fwd, "ffn_gate_up", hn2, wg3, wu3, bm=bm)
    ex.alone("gather_w_down")
    wd = ex.weight("w_ffn_down")
    ff = wd.shape[0]
    f = _mm_nn(act, wd[None], bm=bm, bn=d // 2, name="ffn_down")
    loss_cols, dout, df, dg_post_ffn = _final(f, x2, target, g_post_ffn, "final")

    dgate, dup = _swiglu_bwd(df, wd, act_dgate, act_dup, bm=min(t, 2 * MM_ROWS), bo=ff // 4, name="ffn_down_bwd")
    ex.grads["w_ffn_down"] = _mm_tn(act, df, 1, bm=bt, bk=DW_ROWS, out_dtype=ex.grad_dtype, name="ffn_down_dw").reshape(c, ff // c, d)
    ex.grads["w_ffn_gate"] = run(_mm_tn, "ffn_gate_dw", hn2, dgate, c, bm=bt, bk=d // 2, out_dtype=ex.grad_dtype)
    ex.grads["w_ffn_up"] = run(_mm_tn, "ffn_up_dw", hn2, dup, c, bm=bt, bk=d // 2, out_dtype=ex.grad_dtype)
    dhn2_g = run(_mm_nt, "ffn_gate_dx", dgate, wg3, bm=bm, bo=d // 2, out_dtype=F32)
    dhn2_u = run(_mm_nt, "ffn_up_dx", dup, wu3, bm=bm, bo=d // 2, out_dtype=F32)
    dx2, dmix, dg_pre_ffn, dg_post_mix = run(_mid_bwd, "mid_bwd", dhn2_g, dhn2_u, dout, x2, mixo, g_pre_ffn, g_post_mix)
    dy = run(_mm_nt, "out_proj_dx", dmix, wout[None], bm=bm, bo=mix, out_dtype=F32)
    ex.grads["w_out"] = _mm_tn(y, dmix, 1, bm=bt, bk=mix // 4, out_dtype=ex.grad_dtype, name="out_proj_dw").reshape(c, mix // c, d)
    do, dyl, dga, dgl_norm = run(_outnorm_bwd, "outnorm_bwd", dy, o, yl, ga, gl)
    dproj, dcw, dcb, dwr, dbr, dwi, dbi, dlam = run(_lru_bwd, "lru_bwd", proj, 3 * n_heads, n_blocks, h, kept, dyl, cw, cb, wr, wi, lam)
    small = dict(post_mix_norm=dg_post_mix, pre_ffn_norm=dg_pre_ffn, post_ffn_norm=dg_post_ffn, conv_w=dcw, conv_b=dcb,
                 w_rgate=dwr, b_rgate=dbr, w_igate=dwi, b_igate=dbi, lru_lambda=dlam, attn_out_norm=dga, lru_out_norm=dgl_norm)
    ex.packs["early"] = _pack([small[n] for n in _SMALL_EARLY])
    dproj = run(_attn_bwd, "attn_bwd", proj_mx, do, dproj, n_heads)
    ex.grads["w_in"] = _mm_tn(hn1, dproj, c, bm=bt, bk=d // 2, out_dtype=ex.grad_dtype, name="in_proj_dw")
    ex.alone("grads_w_in_swap")
    dhn1 = run(_mm_nt, "in_proj_dx", dproj, win3, bm=bm, bo=d // 2, out_dtype=F32)
    grad_x, small["pre_mix_norm"] = run(_first_bwd, "first_bwd", dhn1, dx2, x, g_pre_mix)
    ex.packs["late"] = _pack([small["pre_mix_norm"], (0.5 / d) * jnp.sum(loss_cols, keepdims=True)])
    return loss_cols, grad_x, small


def _into_slot(wsh, slot, dtype, name):
    rows, n = wsh.shape
    rb = _row_block(rows, 512) if rows % 8 == 0 else rows

    def body(s_ref, w_ref, o_ref):
        o_ref[...] = w_ref[...].astype(o_ref.dtype)

    return pl.pallas_call(
        body,
        grid_spec=pltpu.PrefetchScalarGridSpec(
            num_scalar_prefetch=1, grid=(rows // rb,),
            in_specs=[pl.BlockSpec((rb, n), lambda i, s_ref: (i, 0))],
            out_specs=pl.BlockSpec((None, rb, n), lambda i, s_ref: (s_ref[0], i, 0))),
        out_shape=S((4, rows, n), dtype), compiler_params=_cp("parallel"), name=name)(slot, wsh)


class _Exchange:
    SCHEDULE = {
        "in_proj": [("stream", "w_in"), ("ici", "conv_w"), ("ici", "w_ffn_up", 0)],
        "attn_fwd": [("d2d", "w_ffn_up", 0), ("ici", "w_ffn_gate")],
        "lru_fwd": [("d2d", "w_ffn_gate"), ("ici", "w_out"), ("ici", "w_ffn_up", 1)],
        "outnorm_fwd": [("d2d", "w_out"), ("d2d", "w_ffn_up", 1)],
        "out_proj": [("ici", "w_ffn_up", 2)],
        "mid_fwd": [("d2d", "w_ffn_up", 2), ("ici", "w_ffn_up", 3)],
        "gather_w_up_last": [("d2d", "w_ffn_up", 3)],
        "ffn_gate_up": [("ici", "w_ffn_down")],
        "gather_w_down": [("d2d", "w_ffn_down")],
        "ffn_gate_dw": [("swap", "w_ffn_down")],
        "ffn_up_dw": [("scatter", "w_ffn_down", 0), ("scatter", "w_ffn_down", 1), ("scatter", "w_ffn_down", 2), ("swap", "w_ffn_gate")],
        "ffn_gate_dx": [("scatter", "w_ffn_down", 3), ("scatter", "w_ffn_gate", 0), ("swap", "w_ffn_up")],
        "ffn_up_dx": [("share", "w_ffn_down"), ("scatter", "w_ffn_gate", 1), ("scatter", "w_ffn_gate", 2)],
        "mid_bwd": [("scatter", "w_ffn_gate", 3), ("scatter", "w_ffn_up", 0)],
        "out_proj_dx": [("share", "w_ffn_gate"), ("scatter", "w_ffn_up", 1)],
        "outnorm_bwd": [("scatter", "w_ffn_up", 2), ("swap", "w_out")],
        "lru_bwd": [("scatter", "w_ffn_up", 3), ("scatter", "w_out")],
        "attn_bwd": [("share", "w_ffn_up"), ("share", "w_out"), ("spread", "early")],
        "grads_w_in_swap": [("swap", "w_in")],
        "in_proj_dx": [("scatter", "w_in")],
        "grads_w_in_share": [("share", "w_in"), ("spread", "late")],
    }
    PIECES = 4
    grad_dtype = BF16

    def __init__(self, slots, place):
        self.buf, self.place = dict(slots), place
        self.grads, self.packs, self.swapped, self.part, self.scattered, self.full, self.spreaded = {}, {}, {}, {}, {}, {}, {}

    def weight(self, name):
        b = self.buf[name]
        return b.reshape(-1, b.shape[2]) if name in ("w_out", "w_ffn_down") else b

    def in_proj(self, x, gain, bm):
        car = self.carrier("in_proj")
        out = _in_proj_streamed(x, gain, car, car.streamed, self.place, bm=bm, name="in_proj")
        self.harvest(car)
        return out

    def conv_w(self):
        return jnp.transpose(self.buf["conv_w"], (1, 0, 2)).reshape(CONV_WIDTH, -1)

    def carrier(self, call):
        if call not in self.SCHEDULE:
            return None
        car = _Carrier()
        car.todo, slot = [], {}
        for kind, name, *piece in self.SCHEDULE[call]:
            if kind in ("ici", "d2d", "stream"):
                if name not in slot:
                    slot[name] = car.inplace(self.buf[name])
                    car.todo.append((self.buf, name, slot[name]))
            if kind == "stream":
                car.streamed = slot[name]
            elif kind in ("ici", "d2d"):
                size = self.buf[name].shape[1] // 2 // self.PIECES
                rows = (piece[0] * size, size) if piece else None
                if kind == "ici":
                    car.gather_ici(slot[name], rows, split=name != "conv_w")
                else:
                    car.gather_d2d(slot[name], rows)
            elif kind == "swap":
                g = self.grads[name]
                o = car.fresh((4, g.shape[1] // 2, g.shape[2]), g.dtype)
                car.swap(car.read(g), o)
                car.todo.append((self.swapped, name, o))
            elif kind == "scatter":
                if name not in self.part:
                    self.part[name] = _add_own_half(self.grads[name], self.swapped[name], self.place[1:], "grads_add_" + name)
                p = self.part[name]
                key = ("scatter", name)
                if key not in slot:
                    slot[key] = (car.read(p), car.inplace(self.scattered[name]) if name in self.scattered else car.fresh(p.shape, p.dtype))
                    car.todo.append((self.scattered, name, slot[key][1]))
                size = p.shape[1] // self.PIECES
                car.scatter(*slot[key], (piece[0] * size, size) if piece else None)
            elif kind == "share":
                o = car.inplace(_sum_chips(self.part[name], self.scattered[name], self.place, "grads_sum_" + name))
                car.share(o)
                car.todo.append((self.full, name, o))
            else:
                o = car.fresh((8,) + self.packs[name].shape, F32)
                car.spread(car.read(self.packs[name]), o)
                car.todo.append((self.spreaded, name, o))
        return car

    def harvest(self, car):
        for state, name, o in (car.todo if car is not None else []):
            state[name] = car.results[o]

    def alone(self, call):
        car = self.carrier(call)
        car.run_alone(call)
        self.harvest(car)

    def small_sum(self, key):
        return _sum_devices(self.packs[key], self.spreaded[key], 2 * self.place[0:1] + self.place[1:], "grads_small_sum_" + key)


def _row_block(rows, cap):
    return max(b for b in range(8, cap + 1, 8) if rows % b == 0)


def _add_own_half(g, recv, core, name):
    _, rows, n = g.shape
    half = rows // 2
    rb = _row_block(half, 1024)
    nb = half // rb

    def body(c_ref, g_ref, r_ref, o_ref):
        o_ref[...] = (g_ref[...].astype(F32) + r_ref[...].astype(F32)).astype(o_ref.dtype)

    return pl.pallas_call(
        body,
        grid_spec=pltpu.PrefetchScalarGridSpec(
            num_scalar_prefetch=1, grid=(4, nb),
            in_specs=[pl.BlockSpec((None, rb, n), lambda k, i, c_ref: (k, c_ref[0] * nb + i, 0)),
                      pl.BlockSpec((None, rb, n), lambda k, i, c_ref: (k, i, 0))],
            out_specs=pl.BlockSpec((None, rb, n), lambda k, i, c_ref: (k, i, 0))),
        out_shape=S((4, half, n), BF16), compiler_params=_cp("parallel", "parallel"), name=name)(core, g, recv)


def _sum_chips(part, recv, place, name):
    _, rows, n = part.shape
    rb = _row_block(rows, 256)
    nb = rows // rb

    def body(p_ref, own_ref, r0, r1, r2, r3, o_ref):
        own = own_ref[...].astype(F32)
        terms = [jnp.where(p_ref[0] == k, own, r[...].astype(F32)) for k, r in enumerate((r0, r1, r2, r3))]
        o_ref[...] = ((terms[0] + terms[1]) + terms[2]) + terms[3]

    def slot(k):
        return pl.BlockSpec((None, rb, n), lambda i, p_ref: (jnp.where(p_ref[0] == k, (k + 1) % 4, k), i, 0))

    return pl.pallas_call(
        body,
        grid_spec=pltpu.PrefetchScalarGridSpec(
            num_scalar_prefetch=1, grid=(nb,),
            in_specs=[pl.BlockSpec((None, rb, n), lambda i, p_ref: (p_ref[0], i, 0))] + [slot(k) for k in range(4)],
            out_specs=pl.BlockSpec((rb, n), lambda i, p_ref: (p_ref[1] * nb + i, 0))),
        out_shape=S((2 * rows, n), F32), compiler_params=_cp("parallel"), name=name)(place, part, recv, recv, recv, recv)


def _sum_devices(own, spread, me, name):
    rows = own.shape[0]

    def body(me_ref, own_ref, *refs):
        acc = None
        for k, r in enumerate(refs[:8]):
            term = jnp.where(me_ref[0] == k, own_ref[...], r[...])
            acc = term if acc is None else acc + term
        refs[8][...] = acc

    def slot(k):
        return pl.BlockSpec((None, rows, 128), lambda i, me_ref: (jnp.where(me_ref[0] == k, (k + 1) % 8, k), 0, 0))

    whole = pl.BlockSpec((rows, 128), lambda i, me_ref: (0, 0))
    return pl.pallas_call(
        body,
        grid_spec=pltpu.PrefetchScalarGridSpec(num_scalar_prefetch=1, grid=(1,), in_specs=[whole] + [slot(k) for k in range(8)],
                                               out_specs=whole),
        out_shape=S((rows, 128), F32), compiler_params=_cp("arbitrary"), name=name)(me, own, *[spread] * 8)


def _adamw(w, g, m, v, name, regive=False):
    rows, n = w.shape
    rb = rows if rows * n * 4 <= (1 << 21) else _row_block(rows, 512)
    c1 = 1.0 - ADAM_B1 ** ADAM_STEP
    c2 = 1.0 - ADAM_B2 ** ADAM_STEP

    def body(w_ref, g_ref, m_ref, v_ref, d_ref, nm_ref, nv_ref, *again):
        gv = g_ref[...]
        for ref in again:
            ref[...] = gv
        nm = ADAM_B1 * m_ref[...] + (1.0 - ADAM_B1) * gv
        nv = ADAM_B2 * v_ref[...] + (1.0 - ADAM_B2) * (gv * gv)
        nm_ref[...] = nm
        nv_ref[...] = nv
        d_ref[...] = -ADAM_LR * ((nm / c1) / (jnp.sqrt(nv / c2) + ADAM_EPS) + ADAM_WD * w_ref[...])

    bs = pl.BlockSpec((rb, n), lambda i: (i, 0))
    n_out = 4 if regive else 3
    return pl.pallas_call(body, grid=(rows // rb,), in_specs=[bs] * 4, out_specs=[bs] * n_out, out_shape=[S((rows, n), F32)] * n_out,
                          compiler_params=_cp("parallel"), name=name)(w, g, m, v)


_BIG = ("w_in", "w_out", "w_ffn_gate", "w_ffn_up", "w_ffn_down")
_SMALL = ("pre_mix_norm", "post_mix_norm", "pre_ffn_norm", "post_ffn_norm", "conv_w", "conv_b", "w_rgate", "b_rgate",
          "w_igate", "b_igate", "lru_lambda", "attn_out_norm", "lru_out_norm")
_SMALL_EARLY = _SMALL[1:]
_WEIGHTS = ("pre_mix_norm", "post_mix_norm", "pre_ffn_norm", "post_ffn_norm", "w_in", "conv_w", "conv_b", "w_rgate", "b_rgate",
            "w_igate", "b_igate", "lru_lambda", "attn_out_norm", "lru_out_norm", "w_out", "w_ffn_gate", "w_ffn_up", "w_ffn_down")


def _pack(arrays):
    flat = []
    for a in arrays:
        f = a.reshape(-1)
        flat.append(jnp.pad(f, (0, (-f.shape[0]) % 1024)))
    return jnp.concatenate(flat).reshape(-1, 128)


def _unpack(packed, shapes):
    out, pos = [], 0
    flat = packed.reshape(-1)
    for s in shapes:
        size = math.prod(s)
        out.append(flat[pos:pos + size].reshape(s))
        pos += size + (-size) % 1024
    return out


def kernel(x, pre_mix_norm, post_mix_norm, pre_ffn_norm, post_ffn_norm, w_in, conv_w, conv_b, w_rgate, b_rgate, w_igate, b_igate, lru_lambda, attn_out_norm, lru_out_norm, w_out, w_ffn_gate, w_ffn_up, w_ffn_down, loss_target, m_pre_mix_norm, m_post_mix_norm, m_pre_ffn_norm, m_post_ffn_norm, m_w_in, m_conv_w, m_conv_b, m_w_rgate, m_b_rgate, m_w_igate, m_b_igate, m_lru_lambda, m_attn_out_norm, m_lru_out_norm, m_w_out, m_w_ffn_gate, m_w_ffn_up, m_w_ffn_down, v_pre_mix_norm, v_post_mix_norm, v_pre_ffn_norm, v_post_ffn_norm, v_w_in, v_conv_w, v_conv_b, v_w_rgate, v_b_rgate, v_w_igate, v_b_igate, v_lru_lambda, v_attn_out_norm, v_lru_out_norm, v_w_out, v_w_ffn_gate, v_w_ffn_up, v_w_ffn_down):
    given = dict(locals())
    w = {n: given[n][0] for n in _WEIGHTS}
    m = {n: given["m_" + n][0] for n in _WEIGHTS}
    v = {n: given["v_" + n][0] for n in _WEIGHTS}
    xs, target = x[0], loss_target[0]
    d = xs.shape[1]
    chip = (2 * lax.axis_index("x") + lax.axis_index("y")).astype(jnp.int32)
    place = jnp.stack([chip, lax.axis_index("c").astype(jnp.int32)])

    slots = {n: _into_slot(w[n], place[0:1], _MXU, "slot_" + n) for n in _BIG}
    slots["conv_w"] = _into_slot(w["conv_w"], place[0:1], F32, "slot_conv_w")
    ex = _Exchange(slots, place)
    row = lambda a: a.reshape(1, -1)
    norms = tuple(row(w[n]) for n in ("pre_mix_norm", "post_mix_norm", "pre_ffn_norm", "post_ffn_norm"))

    loss_cols, grad_x, small = _local_step(
        xs, target, norms, ex, row(w["conv_b"]), w["w_rgate"], row(w["b_rgate"]),
        w["w_igate"], row(w["b_igate"]), row(w["lru_lambda"]), row(w["attn_out_norm"]), row(w["lru_out_norm"]))


    ex.alone("grads_w_in_share")
    reduced = {n: ex.full[n] for n in _BIG}
    early = _unpack(ex.small_sum("early"), [small[n].shape for n in _SMALL_EARLY])
    late = _unpack(ex.small_sum("late"), [small["pre_mix_norm"].shape, (1, 1)])
    loss = late[1][0, 0]
    for n, g in zip(_SMALL_EARLY + ("pre_mix_norm",), early + late[:1]):
        reduced[n] = g.reshape(w[n].shape) if n != "conv_w" else lax.dynamic_slice_in_dim(g, chip * w[n].shape[1], w[n].shape[1], axis=1)

    delta, new_m, new_v = {}, {}, {}
    for n in _BIG:
        delta[n], new_m[n], new_v[n], reduced[n] = _adamw(w[n], reduced[n], m[n], v[n], "adamw_" + n, regive=True)
    shapes = [w[n].shape for n in _SMALL]
    packed = _adamw(*[_pack([src[n] for n in _SMALL]) for src in (w, reduced, m, v)], "adamw_small")
    for out, p in zip((delta, new_m, new_v), packed):
        out.update(zip(_SMALL, _unpack(p, shapes)))

    lead = lambda a: a[None]
    return (loss, lead(grad_x), *[lead(reduced[n]) for n in _WEIGHTS], *[lead(delta[n]) for n in _WEIGHTS],
            *[lead(new_m[n]) for n in _WEIGHTS], *[lead(new_v[n]) for n in _WEIGHTS])
```

```python
import functools
import math

import jax
import jax.numpy as jnp
from jax import lax
from jax.experimental import pallas as pl
from jax.experimental.pallas import tpu as pltpu

F32 = jnp.float32
BF16 = jnp.bfloat16
_MXU = BF16
S = jax.ShapeDtypeStruct

RMS_EPS = 1e-6
HEAD_DIM = 128
CONV_WIDTH = 4
LRU_C = 8.0
ADAM_LR, ADAM_B1, ADAM_B2, ADAM_EPS, ADAM_WD, ADAM_STEP = 0.001, 0.9, 0.999, 1e-08, 0.01, 10
EXP_CUT = -105.0
VMEM_LIMIT = 60 * 1024 * 1024
ROW_TILE = 512
SEQ_TILE = 256
ATTN_BLOCK = 256
ATTN_HEADS = 2
MM_ROWS = 512
DW_TOKENS = 2048
DW_ROWS = 512
MESH = pl.DeviceIdType.MESH


def _cp(*sem):
    return pltpu.CompilerParams(dimension_semantics=sem, vmem_limit_bytes=VMEM_LIMIT)


def _dot(a, b):
    return jnp.dot(a, b, preferred_element_type=F32)


def _dot_nt(a, b):
    return lax.dot_general(a, b, (((1,), (1,)), ((), ())), preferred_element_type=F32)


def _dot_tn(a, b):
    return lax.dot_general(a, b, (((0,), (0,)), ((), ())), preferred_element_type=F32)


def _rstd(v):
    return lax.rsqrt(jnp.mean(v * v, axis=-1, keepdims=True) + RMS_EPS)


def _rms_bwd(dn, vh, r, gain):
    dvh = dn * gain
    dv = r * (dvh - vh * jnp.mean(dvh * vh, axis=-1, keepdims=True))
    return dv, jnp.sum(dn * vh, axis=0, keepdims=True)


def _log_sigmoid(z):
    return jnp.minimum(z, 0.0) - jnp.log(1.0 + jnp.exp(-jnp.abs(z)))


def _expm1(v):
    small = v * (1.0 + v * (0.5 + v * (1.0 / 6.0 + v * (1.0 / 24.0 + v * (1.0 / 120.0)))))
    return jnp.where(jnp.abs(v) < 0.04, small, jnp.exp(v) - 1.0)


_GELU_C = math.sqrt(2.0 / math.pi)


def _gelu(v):
    return 0.5 * v * (1.0 + jnp.tanh(_GELU_C * (v + 0.044715 * v * v * v)))


def _gelu_grad(v):
    th = jnp.tanh(_GELU_C * (v + 0.044715 * v * v * v))
    return 0.5 * (1.0 + th) + 0.5 * v * (1.0 - th * th) * _GELU_C * (1.0 + 3.0 * 0.044715 * v * v)


def _row_spec(tm, d):
    return pl.BlockSpec((tm, d), lambda i: (i, 0))


def _vec_spec(d):
    return pl.BlockSpec((1, d), lambda i: (0, 0))


_ANY = pl.BlockSpec(memory_space=pl.ANY)


def _place():
    x, y, c = lax.axis_index("x"), lax.axis_index("y"), lax.axis_index("c")
    return x, y, c, [(1 - x, y), (x, 1 - y), (1 - x, 1 - y)]


def _remote(src, dst, send_sem, recv_sem, to):
    return pltpu.make_async_remote_copy(src_ref=src, dst_ref=dst, send_sem=send_sem, recv_sem=recv_sem,
                                        device_id=to, device_id_type=MESH)


class _Carrier:
    def __init__(self):
        self.inputs, self.out_shapes, self.aliases, self.ops, self.n_sems, self.results = [], [], {}, [], 0, None

    def inplace(self, arr):
        self.aliases[len(self.inputs)] = len(self.out_shapes)
        self.inputs.append(arr)
        self.out_shapes.append(S(arr.shape, arr.dtype))
        return len(self.out_shapes) - 1

    def read(self, arr):
        self.inputs.append(arr)
        return len(self.inputs) - 1

    def fresh(self, shape, dtype):
        self.out_shapes.append(S(shape, dtype))
        return len(self.out_shapes) - 1

    def _add(self, n_sems, copies):
        base = self.n_sems
        self.n_sems += n_sems

        def start(ins, outs, send, recv):
            for k, (src, dst, _, to) in enumerate(copies(ins, outs)):
                _remote(src, dst, send.at[base + k], recv.at[base + k], to).start()

        def finish(ins, outs, send, recv):
            for k, (src, _, land, to) in enumerate(copies(ins, outs)):
                _remote(src, land, send.at[base + k], recv.at[base + k], to).wait()

        self.ops.append((start, finish))

    def gather_ici(self, o, rows=None, split=True):
        half = self.out_shapes[o].shape[1] // 2
        lo, size = rows or (0, half)

        def copies(ins, outs):
            x, y, c, chips = _place()
            part = (lambda ref: ref.at[pl.ds(c * half + lo, size)]) if split else (lambda ref: ref)
            mine = part(outs[o].at[2 * x + y])
            return [(mine, mine, part(outs[o].at[2 * px + py]), (px, py, c)) for px, py in chips]

        self._add(3, copies)

    def gather_d2d(self, o, rows=None):
        half = self.out_shapes[o].shape[1] // 2
        lo, size = rows or (0, half)

        def copies(ins, outs):
            x, y, c, chips = _place()
            at = lambda k, cc: outs[o].at[k].at[pl.ds(cc * half + lo, size)]
            return [(at(2 * px + py, c), at(2 * px + py, c), at(2 * px + py, 1 - c), (x, y, 1 - c)) for px, py in chips]

        self._add(3, copies)

    def swap(self, i, o):
        half = self.inputs[i].shape[1] // 2

        def copies(ins, outs):
            x, y, c, _ = _place()
            return [(ins[i].at[:, pl.ds((1 - c) * half, half)], outs[o], outs[o], (x, y, 1 - c))]

        self._add(1, copies)

    def scatter(self, i, o, rows=None):
        lo, size = rows or (0, self.inputs[i].shape[1])

        def copies(ins, outs):
            x, y, c, chips = _place()
            cut = lambda ref: ref.at[pl.ds(lo, size)]
            return [(cut(ins[i].at[2 * px + py]), cut(outs[o].at[2 * x + y]), cut(outs[o].at[2 * px + py]), (px, py, c)) for px, py in chips]

        self._add(3, copies)

    def share(self, o):
        r = self.out_shapes[o].shape[0] // 2

        def copies(ins, outs):
            x, y, c, _ = _place()
            mine = outs[o].at[pl.ds(c * r, r)]
            return [(mine, mine, outs[o].at[pl.ds((1 - c) * r, r)], (x, y, 1 - c))]

        self._add(1, copies)

    def spread(self, i, o):
        def copies(ins, outs):
            x, y, c, _ = _place()
            me = 4 * x + 2 * y + c
            out = []
            for d in range(1, 8):
                to, frm = (me + d) % 8, (me + 8 - d) % 8
                out.append((ins[i], outs[o].at[me], outs[o].at[frm], (to // 4, (to // 2) % 2, to % 2)))
            return out

        self._add(7, copies)

    def _pallas(self, body, n_in, n_out, scratch, **kw):
        k_in, k_out = len(self.inputs), len(self.out_shapes)
        grid = kw.get("grid", ())

        def wrapped(*refs):
            ins, cins = refs[:n_in], refs[n_in:n_in + k_in]
            outs = refs[n_in + k_in:n_in + k_in + n_out]
            couts = refs[n_in + k_in + n_out:n_in + k_in + n_out + k_out]
            own = refs[n_in + k_in + n_out + k_out:]
            send, recv = own[len(scratch):]
            ids = [pl.program_id(a) for a in range(len(grid))]
            first = functools.reduce(jnp.logical_and, [a == 0 for a in ids], True)
            last = functools.reduce(jnp.logical_and, [a == g - 1 for a, g in zip(ids, grid)], True)

            def go(stage):
                for op in self.ops:
                    op[stage](cins, couts, send, recv)

            if grid:
                pl.when(first)(lambda: go(0))
                body(*ins, *outs, *own[:len(scratch)])
                pl.when(last)(lambda: go(1))
            else:
                go(0)
                go(1)

        sem = pltpu.SemaphoreType.DMA((self.n_sems,))
        return pl.pallas_call(
            wrapped, in_specs=list(kw.get("in_specs", [])) + [_ANY] * k_in, out_specs=list(kw.get("out_specs", [])) + [_ANY] * k_out,
            out_shape=list(kw.get("out_shape", [])) + self.out_shapes, scratch_shapes=list(scratch) + [sem, sem],
            input_output_aliases={**kw.get("aliases", {}), **{n_in + i: n_out + o for i, o in self.aliases.items()}}, name=kw["name"],
            **({"grid": grid, "compiler_params": _cp(*["arbitrary"] * len(grid))} if grid else {}))

    def run(self, body, kw, *args):
        single = not isinstance(kw["out_shape"], (list, tuple))
        out_shape = [kw["out_shape"]] if single else list(kw["out_shape"])
        out_specs = [kw["out_specs"]] if single else list(kw["out_specs"])
        res = self._pallas(body, len(args), len(out_shape), kw.get("scratch_shapes", []), grid=kw["grid"], in_specs=kw["in_specs"],
                           out_specs=out_specs, out_shape=out_shape, name=kw["name"],
                           aliases=kw.get("input_output_aliases", {}))(*args, *self.inputs)
        self.results = list(res[len(out_shape):])
        return res[0] if single else list(res[:len(out_shape)])

    def run_alone(self, name):
        self.results = list(self._pallas(None, 0, 0, [], name=name)(*self.inputs))


def _call(comm, body, **kw):
    if comm is None:
        return pl.pallas_call(body, **kw)
    return functools.partial(comm.run, body, kw)


def _in_proj_streamed(x, gain, car, o_w, place, *, bm, name):
    m, k = x.shape
    n = car.out_shapes[o_w].shape[2]
    ni, half = m // bm, k // 2
    k_in, k_out = len(car.inputs), len(car.out_shapes)
    order = lambda p: ((p & 1) << 1) | (p >> 1)

    def body(place_ref, x_ref, g_ref, *refs):
        cins, (hn_ref, o_ref, ob_ref), couts = refs[:k_in], refs[k_in:k_in + 3], refs[k_in + 3:k_in + 3 + k_out]
        wbuf, hn_all, local, ici_send, ici_recv, d2d_send, d2d_recv, send, recv = refs[k_in + 3 + k_out:]
        p, i = pl.program_id(0), pl.program_id(1)
        x, y, c, chips = _place()
        me = 2 * x + y
        rows = lambda chunk, cc: couts[o_w].at[chunk].at[pl.ds(cc * half, half)]

        @pl.when(jnp.logical_and(p == 0, i == 0))
        def _():
            for j, (px, py) in enumerate(chips):
                _remote(rows(me, c), rows(me, c), ici_send.at[j], ici_recv.at[j], (px, py, c)).start()
            for op in car.ops:
                op[0](cins, couts, send, recv)

        for j, (px, py) in enumerate(chips):
            @pl.when(jnp.logical_and(p == j + 1, i == 0))
            def _(j=j, px=px, py=py):
                landed, other = rows(2 * px + py, c), rows(2 * px + py, 1 - c)
                _remote(landed, landed, ici_send.at[j], ici_recv.at[j], (px, py, c)).wait_recv()
                _remote(landed, landed, d2d_send.at[j], d2d_recv.at[j], (x, y, 1 - c)).start()
                _remote(other, other, d2d_send.at[j], d2d_recv.at[j], (x, y, 1 - c)).wait_recv()

        @pl.when(i == 0)
        def _():
            cp = pltpu.make_async_copy(couts[o_w].at[me ^ order(p)], wbuf, local.at[0])
            cp.start()
            cp.wait()

        tile = pl.ds(pl.multiple_of(i * bm, bm), bm)

        @pl.when(p == 0)
        def _():
            xv = x_ref[...]
            hn_all[tile, :] = ((xv * _rstd(xv)) * g_ref[...]).astype(_MXU)

        hn = hn_all[tile, :]
        hn_ref[...] = hn
        res = _dot(hn, wbuf[...])
        o_ref[...] = res
        ob_ref[...] = res.astype(ob_ref.dtype)

        @pl.when(jnp.logical_and(p == 3, i == ni - 1))
        def _():
            for j, (px, py) in enumerate(chips):
                _remote(rows(me, c), rows(me, c), ici_send.at[j], ici_recv.at[j], (px, py, c)).wait_send()
                _remote(rows(me, c), rows(me, c), d2d_send.at[j], d2d_recv.at[j], (x, y, 1 - c)).wait_send()
            for op in car.ops:
                op[1](cins, couts, send, recv)

    ospec = pl.BlockSpec((bm, n), lambda p, i, place_ref: (i, place_ref[0] ^ order(p)))
    rows = pl.BlockSpec((bm, k), lambda p, i, place_ref: (jnp.where(p == 0, i, 0), 0))
    three, sems = pltpu.SemaphoreType.DMA((3,)), pltpu.SemaphoreType.DMA((max(car.n_sems, 1),))
    res = pl.pallas_call(
        body,
        grid_spec=pltpu.PrefetchScalarGridSpec(
            num_scalar_prefetch=1, grid=(4, ni),
            in_specs=[rows, pl.BlockSpec((1, k), lambda p, i, place_ref: (0, 0))] + [_ANY] * k_in,
            out_specs=[pl.BlockSpec((bm, k), lambda p, i, place_ref: (p * ni + i, 0)), ospec, ospec] + [_ANY] * k_out,
            scratch_shapes=[pltpu.VMEM((k, n), _MXU), pltpu.VMEM((m, k), _MXU), pltpu.SemaphoreType.DMA((1,)),
                            three, three, three, three, sems, sems]),
        out_shape=[S((4 * m, k), _MXU), S((m, 4 * n), F32), S((m, 4 * n), _MXU)] + car.out_shapes,
        input_output_aliases={3 + a: 3 + o for a, o in car.aliases.items()},
        compiler_params=_cp("arbitrary", "arbitrary"), name=name)(place, x, gain, *car.inputs)
    car.results = list(res[3:])
    return res[0], res[1], res[2]


def _mm_nn(a, b3, *, bm, bn, name, also=None, comm=None):
    m, k = a.shape
    c, _, n = b3.shape
    ni, nj = m // bm, n // bn

    def body(a_ref, b_ref, *o_refs):
        res = _dot(a_ref[...], b_ref[...])
        for o_ref in o_refs:
            o_ref[...] = res.astype(o_ref.dtype)

    ospec = pl.BlockSpec((bm, bn), lambda cc, j, i: (i, cc * nj + j))
    dtypes = [F32] + ([] if also is None else [also])
    out = _call(
        comm, body, grid=(c, nj, ni),
        in_specs=[pl.BlockSpec((bm, k), lambda cc, j, i: (i, 0)), pl.BlockSpec((None, k, bn), lambda cc, j, i: (cc, 0, j))],
        out_specs=[ospec] * len(dtypes), out_shape=[S((m, c * n), dt) for dt in dtypes],
        compiler_params=_cp("parallel", "parallel", "parallel"), name=name)(a, b3)
    return out[0] if also is None else out


def _mm_nt(a, b3, *, bm, bo, out_dtype, name, comm=None):
    m = a.shape[0]
    c, ko, n = b3.shape
    ni, nj = m // bm, ko // bo

    def body(a_ref, b_ref, o_ref):
        acc = _dot_nt(a_ref[:, 0:n], b_ref[0])
        for cc in range(1, c):
            acc = acc + _dot_nt(a_ref[:, cc * n:(cc + 1) * n], b_ref[cc])
        o_ref[...] = acc.astype(o_ref.dtype)

    return _call(
        comm, body, grid=(nj, ni),
        in_specs=[pl.BlockSpec((bm, c * n), lambda j, i: (i, 0)),
                  pl.BlockSpec((c, bo, n), lambda j, i: (0, j, 0))],
        out_specs=pl.BlockSpec((bm, bo), lambda j, i: (i, j)),
        out_shape=S((m, ko), out_dtype),
        compiler_params=_cp("parallel", "parallel"), name=name)(a, b3)


def _mm_tn(a, b, c, *, bm, bk, out_dtype, name, comm=None):
    m, k = b.shape[0], a.shape[1]
    n = b.shape[1] // c
    nm, nk = m // bm, k // bk

    def body(a_ref, b_ref, o_ref, acc):
        mm = pl.program_id(2)

        @pl.when(mm == 0)
        def _():
            acc[...] = jnp.zeros_like(acc)

        acc[...] += _dot_tn(a_ref[...], b_ref[...])

        @pl.when(mm == nm - 1)
        def _():
            o_ref[...] = acc[...].astype(o_ref.dtype)

    return _call(
        comm, body, grid=(c, nk, nm),
        in_specs=[pl.BlockSpec((bm, bk), lambda cc, j, mm: (mm, j)),
                  pl.BlockSpec((bm, n), lambda cc, j, mm: (mm, cc))],
        out_specs=pl.BlockSpec((None, bk, n), lambda cc, j, mm: (cc, j, 0)),
        out_shape=S((c, k, n), out_dtype),
        scratch_shapes=[pltpu.VMEM((bk, n), F32)],
        compiler_params=_cp("parallel", "parallel", "arbitrary"), name=name)(a, b)


def _swiglu_fwd(hn, wg3, wu3, *, bm, name, comm=None):
    m, k = hn.shape
    c, _, n = wg3.shape

    def body(a_ref, g_ref, u_ref, dgate_ref, dup_ref, act_ref):
        a = a_ref[...]
        gate = _dot(a, g_ref[...])
        up = _dot(a, u_ref[...])
        sg = jax.nn.sigmoid(gate)
        silu = gate * sg
        dgate_ref[...] = (up * (sg * (1.0 + gate * (1.0 - sg)))).astype(dgate_ref.dtype)
        dup_ref[...] = silu.astype(dup_ref.dtype)
        act_ref[...] = (silu * up).astype(act_ref.dtype)

    wspec = pl.BlockSpec((None, k, n), lambda cc, i: (cc, 0, 0))
    ospec = pl.BlockSpec((bm, n), lambda cc, i: (i, cc))
    return _call(
        comm, body, grid=(c, m // bm),
        in_specs=[pl.BlockSpec((bm, k), lambda cc, i: (i, 0)), wspec, wspec],
        out_specs=[ospec, ospec, ospec],
        out_shape=[S((m, c * n), _MXU), S((m, c * n), _MXU), S((m, c * n), _MXU)],
        compiler_params=_cp("parallel", "parallel"), name=name)(hn, wg3, wu3)


def _swiglu_bwd(df, wd, act_dgate, act_dup, *, bm, bo, name):
    m, k = df.shape
    ko = wd.shape[0]

    def body(a_ref, b_ref, g_ref, u_ref, dg_ref, du_ref):
        dact = _dot_nt(a_ref[...], b_ref[...])
        dg_ref[...] = (dact * g_ref[...].astype(F32)).astype(dg_ref.dtype)
        du_ref[...] = (dact * u_ref[...].astype(F32)).astype(du_ref.dtype)

    ospec = pl.BlockSpec((bm, bo), lambda j, i: (i, j))
    return pl.pallas_call(
        body, grid=(ko // bo, m // bm),
        in_specs=[pl.BlockSpec((bm, k), lambda j, i: (i, 0)), pl.BlockSpec((bo, k), lambda j, i: (j, 0)), ospec, ospec],
        out_specs=[ospec, ospec],
        out_shape=[S((m, ko), _MXU), S((m, ko), _MXU)],
        compiler_params=_cp("parallel", "parallel"), name=name)(df, wd, act_dgate, act_dup)


def _rms_fwd(x, gain, name):
    t, d = x.shape
    tm = min(t, ROW_TILE)

    def body(x_ref, g_ref, o_ref):
        xv = x_ref[...]
        o_ref[...] = ((xv * _rstd(xv)) * g_ref[...]).astype(o_ref.dtype)

    return pl.pallas_call(body, grid=(t // tm,), in_specs=[_row_spec(tm, d), _vec_spec(d)], out_specs=_row_spec(tm, d),
                          out_shape=S((t, d), _MXU), compiler_params=_cp("parallel"), name=name)(x, gain)


def _outnorm_fwd(o, yl, ga, gl, name, comm=None):
    t, w = o.shape
    tm = min(t, ROW_TILE)

    def body(o_ref, l_ref, ga_ref, gl_ref, y_ref):
        ov, lv = o_ref[...], l_ref[...]
        y_ref[:, :w] = ((ov * _rstd(ov)) * ga_ref[...]).astype(y_ref.dtype)
        y_ref[:, w:] = ((lv * _rstd(lv)) * gl_ref[...]).astype(y_ref.dtype)

    return _call(comm, body, grid=(t // tm,), in_specs=[_row_spec(tm, w), _row_spec(tm, w), _vec_spec(w), _vec_spec(w)],
                 out_specs=_row_spec(tm, 2 * w), out_shape=S((t, 2 * w), _MXU),
                 compiler_params=_cp("parallel"), name=name)(o, yl, ga, gl)


def _mid_fwd(x, mix, g_post, g_pre, name, comm=None):
    t, d = x.shape
    tm = min(t, ROW_TILE)

    def body(x_ref, m_ref, gp_ref, gn_ref, x2_ref, hn_ref):
        mv = m_ref[...]
        x2 = x_ref[...] + (mv * _rstd(mv)) * gp_ref[...]
        x2_ref[...] = x2
        hn_ref[...] = ((x2 * _rstd(x2)) * gn_ref[...]).astype(hn_ref.dtype)

    return _call(comm, body, grid=(t // tm,), in_specs=[_row_spec(tm, d), _row_spec(tm, d), _vec_spec(d), _vec_spec(d)],
                          out_specs=[_row_spec(tm, d), _row_spec(tm, d)], out_shape=[S((t, d), F32), S((t, d), _MXU)],
                          compiler_params=_cp("parallel"), name=name)(x, mix, g_post, g_pre)


def _final(f, x2, target, g_post, name):
    t, d = f.shape
    tm = min(t, ROW_TILE // 2)

    def body(f_ref, x2_ref, t_ref, g_ref, loss_ref, dout_ref, df_ref, dg_ref):
        @pl.when(pl.program_id(0) == 0)
        def _():
            loss_ref[...] = jnp.zeros_like(loss_ref)
            dg_ref[...] = jnp.zeros_like(dg_ref)

        fv = f_ref[...]
        r = _rstd(fv)
        fh = fv * r
        err = (x2_ref[...] + fh * g_ref[...]) - t_ref[...]
        loss_ref[...] += jnp.sum(err * err, axis=0, keepdims=True)
        dout = err * (1.0 / d)
        dout_ref[...] = dout
        dfv, dg = _rms_bwd(dout, fh, r, g_ref[...])
        df_ref[...] = dfv.astype(df_ref.dtype)
        dg_ref[...] += dg

    return pl.pallas_call(
        body, grid=(t // tm,),
        in_specs=[_row_spec(tm, d), _row_spec(tm, d), _row_spec(tm, d), _vec_spec(d)],
        out_specs=[_vec_spec(d), _row_spec(tm, d), _row_spec(tm, d), _vec_spec(d)],
        out_shape=[S((1, d), F32), S((t, d), F32), S((t, d), _MXU), S((1, d), F32)],
        compiler_params=_cp("arbitrary"), name=name)(f, x2, target, g_post)


def _mid_bwd(dhn_a, dhn_b, dout, x2, mix, g_pre, g_post, name, comm=None):
    t, d = x2.shape
    tm = min(t, ROW_TILE // 2)

    def body(da_ref, db_ref, do_ref, x2_ref, m_ref, gn_ref, gp_ref, dx2_ref, dm_ref, dgn_ref, dgp_ref):
        @pl.when(pl.program_id(0) == 0)
        def _():
            dgn_ref[...] = jnp.zeros_like(dgn_ref)
            dgp_ref[...] = jnp.zeros_like(dgp_ref)

        x2 = x2_ref[...]
        r = _rstd(x2)
        dxa, dgn = _rms_bwd(da_ref[...] + db_ref[...], x2 * r, r, gn_ref[...])
        dx2 = do_ref[...] + dxa
        dx2_ref[...] = dx2
        dgn_ref[...] += dgn
        mv = m_ref[...]
        rm = _rstd(mv)
        dmv, dgp = _rms_bwd(dx2, mv * rm, rm, gp_ref[...])
        dm_ref[...] = dmv.astype(dm_ref.dtype)
        dgp_ref[...] += dgp

    rs, vs = _row_spec(tm, d), _vec_spec(d)
    return _call(
        comm, body, grid=(t // tm,), in_specs=[rs, rs, rs, rs, rs, vs, vs], out_specs=[rs, rs, vs, vs],
        out_shape=[S((t, d), F32), S((t, d), _MXU), S((1, d), F32), S((1, d), F32)],
        compiler_params=_cp("arbitrary"), name=name)(dhn_a, dhn_b, dout, x2, mix, g_pre, g_post)


def _first_bwd(dhn, dx2, x, gain, name, comm=None):
    t, d = x.shape
    tm = min(t, ROW_TILE)

    def body(dh_ref, dx2_ref, x_ref, g_ref, dx_ref, dg_ref):
        @pl.when(pl.program_id(0) == 0)
        def _():
            dg_ref[...] = jnp.zeros_like(dg_ref)

        xv = x_ref[...]
        r = _rstd(xv)
        dxa, dg = _rms_bwd(dh_ref[...], xv * r, r, g_ref[...])
        dx_ref[...] = dx2_ref[...] + dxa
        dg_ref[...] += dg

    rs, vs = _row_spec(tm, d), _vec_spec(d)
    return _call(comm, body, grid=(t // tm,), in_specs=[rs, rs, rs, vs], out_specs=[rs, vs],
                          out_shape=[S((t, d), F32), S((1, d), F32)], compiler_params=_cp("arbitrary"), name=name)(dhn, dx2, x, gain)


def _outnorm_bwd(dy, o, yl, ga, gl, name, comm=None):
    t, w = o.shape
    tm = min(t, ROW_TILE)

    def body(dy_ref, o_ref, l_ref, ga_ref, gl_ref, do_ref, dl_ref, dga_ref, dgl_ref):
        @pl.when(pl.program_id(0) == 0)
        def _():
            dga_ref[...] = jnp.zeros_like(dga_ref)
            dgl_ref[...] = jnp.zeros_like(dgl_ref)

        ov, lv = o_ref[...], l_ref[...]
        ra, rl = _rstd(ov), _rstd(lv)
        dov, dga = _rms_bwd(dy_ref[:, :w], ov * ra, ra, ga_ref[...])
        dlv, dgl = _rms_bwd(dy_ref[:, w:], lv * rl, rl, gl_ref[...])
        do_ref[...] = dov.astype(do_ref.dtype)
        dl_ref[...] = dlv
        dga_ref[...] += dga
        dgl_ref[...] += dgl

    rs, vs = _row_spec(tm, w), _vec_spec(w)
    return _call(comm, body, grid=(t // tm,), in_specs=[_row_spec(tm, 2 * w), rs, rs, vs, vs], out_specs=[rs, rs, vs, vs],
                          out_shape=[S((t, w), _MXU), S((t, w), F32), S((1, w), F32), S((1, w), F32)],
                          compiler_params=_cp("arbitrary"), name=name)(dy, o, yl, ga, gl)


def _tri_sum(v, tri):
    return _dot(v.astype(_MXU), tri)


def _attn_tile(qb, kb, row, col, shift, scale):
    z = _dot_nt(qb, kb) * scale
    mask = (col + shift) < row
    lb = _log_sigmoid(z)
    lm = jnp.where(mask, lb - z, 0.0)
    return mask, lb, lm


def _attn_fwd(proj, n_heads, name, comm=None):
    t = proj.shape[0]
    bq = min(t, ATTN_BLOCK)
    nq = t // bq
    scale = 1.0 / math.sqrt(HEAD_DIM)

    heads = [slice(a * HEAD_DIM, (a + 1) * HEAD_DIM) for a in range(ATTN_HEADS)]

    def body(q_ref, k_ref, v_ref, o_ref):
        row = lax.broadcasted_iota(jnp.int32, (bq, bq), 0)
        col = lax.broadcasted_iota(jnp.int32, (bq, bq), 1)
        tri = (row > col).astype(_MXU)

        def per_q(qi, _):
            q0 = pl.multiple_of(qi * bq, bq)
            qbs = [q_ref[pl.ds(q0, bq), hd] for hd in heads]

            def cond(st):
                return jnp.logical_and(st[0] >= 0, st[1])

            def step(st):
                kj, _, carries, accs = st
                k0 = pl.multiple_of(kj * bq, bq)
                alive, new_carries, new_accs = None, [], []
                for hd, qb, carry, acc in zip(heads, qbs, carries, accs):
                    mask, lb, lm = _attn_tile(qb, k_ref[pl.ds(k0, bq), hd], row, col, (kj - qi) * bq, scale)
                    w = jnp.where(mask, jnp.exp(lb + _tri_sum(lm, tri) + carry), 0.0)
                    new_accs.append(acc + _dot(w.astype(_MXU), v_ref[pl.ds(k0, bq), hd]))
                    carry = carry + jnp.sum(lm, axis=1, keepdims=True)
                    new_carries.append(carry)
                    live = jnp.max(carry) > EXP_CUT
                    alive = live if alive is None else jnp.logical_or(alive, live)
                return kj - 1, alive, tuple(new_carries), tuple(new_accs)

            st = lax.while_loop(cond, step, (qi, jnp.bool_(True), (jnp.zeros((bq, 1), F32),) * ATTN_HEADS,
                                             (jnp.zeros((bq, HEAD_DIM), F32),) * ATTN_HEADS))
            for hd, acc in zip(heads, st[3]):
                o_ref[pl.ds(q0, bq), hd] = acc
            return 0

        lax.fori_loop(0, nq, per_q, 0)

    groups = n_heads // ATTN_HEADS
    hs = lambda off: pl.BlockSpec((t, ATTN_HEADS * HEAD_DIM), lambda h: (0, off + h))
    return _call(
        comm, body, grid=(groups,), in_specs=[hs(0), hs(groups), hs(2 * groups)], out_specs=hs(0),
        out_shape=S((t, n_heads * HEAD_DIM), F32), compiler_params=_cp("parallel"), name=name)(proj, proj, proj)


def _emit(blocks, out_ref, starts, sems):
    copies = [pltpu.make_async_copy(b, out_ref.at[:, pl.ds(c0, b.shape[1])], sems.at[k]) for k, (b, c0) in enumerate(zip(blocks, starts))]
    for cp in copies:
        cp.start()
    for cp in copies:
        cp.wait()


def _attn_bwd(proj, do, dproj, n_heads, name, comm=None):
    t = proj.shape[0]
    bq = min(t, ATTN_BLOCK)
    nq = t // bq
    scale = 1.0 / math.sqrt(HEAD_DIM)
    groups = n_heads // ATTN_HEADS
    wide = ATTN_HEADS * HEAD_DIM

    heads = [slice(a * HEAD_DIM, (a + 1) * HEAD_DIM) for a in range(ATTN_HEADS)]

    def body(q_ref, k_ref, v_ref, do_ref, _, dproj_ref, dka_ref, dva_ref, g_ref, b_ref, dq_ref, dk_ref, dv_ref, out_sems):
        group = pl.program_id(0)
        dka_ref[...] = jnp.zeros_like(dka_ref)
        dva_ref[...] = jnp.zeros_like(dva_ref)
        row = lax.broadcasted_iota(jnp.int32, (bq, bq), 0)
        col = lax.broadcasted_iota(jnp.int32, (bq, bq), 1)
        tri = (row > col).astype(_MXU)
        tri_lt = (row < col).astype(_MXU)

        def per_q(qi, _):
            q0 = pl.multiple_of(qi * bq, bq)
            qbs = [q_ref[pl.ds(q0, bq), hd] for hd in heads]
            dobs = [do_ref[pl.ds(q0, bq), hd] for hd in heads]

            def cond(st):
                return jnp.logical_and(st[0] >= 0, st[1])

            def step(st):
                kj, _, carries = st
                k0 = pl.multiple_of(kj * bq, bq)
                alive, new_carries = None, []
                for a, (hd, qb, dob, carry) in enumerate(zip(heads, qbs, dobs, carries)):
                    mask, lb, lm = _attn_tile(qb, k_ref[pl.ds(k0, bq), hd], row, col, (kj - qi) * bq, scale)
                    w = jnp.where(mask, jnp.exp(lb + _tri_sum(lm, tri) + carry), 0.0)
                    g_ref[a, pl.ds(k0, bq), :] = w * _dot_nt(dob, v_ref[pl.ds(k0, bq), hd])
                    b_ref[a, pl.ds(k0, bq), :] = jnp.where(mask, jnp.exp(lb), 0.0)
                    dva_ref[pl.ds(k0, bq), hd] += _dot_tn(w.astype(_MXU), dob)
                    carry = carry + jnp.sum(lm, axis=1, keepdims=True)
                    new_carries.append(carry)
                    live = jnp.max(carry) > EXP_CUT
                    alive = live if alive is None else jnp.logical_or(alive, live)
                return kj - 1, alive, tuple(new_carries)

            st = lax.while_loop(cond, step, (qi, jnp.bool_(True), (jnp.zeros((bq, 1), F32),) * ATTN_HEADS))

            def back(kj, st2):
                k0 = pl.multiple_of(kj * bq, bq)
                out = []
                for a, (hd, qb, (before, dq)) in enumerate(zip(heads, qbs, st2)):
                    g = g_ref[a, pl.ds(k0, bq), :]
                    beta = b_ref[a, pl.ds(k0, bq), :]
                    dz = ((g * (1.0 - beta) - (before + _tri_sum(g, tri_lt)) * beta) * scale).astype(_MXU)
                    dka_ref[pl.ds(k0, bq), hd] += _dot_tn(dz, qb)
                    out.append((before + jnp.sum(g, axis=1, keepdims=True), dq + _dot(dz, k_ref[pl.ds(k0, bq), hd])))
                return tuple(out)

            st2 = lax.fori_loop(st[0] + 1, qi + 1, back, ((jnp.zeros((bq, 1), F32), jnp.zeros((bq, HEAD_DIM), F32)),) * ATTN_HEADS)
            for hd, (_, dq) in zip(heads, st2):
                dq_ref[pl.ds(q0, bq), hd] = dq.astype(dq_ref.dtype)
            return 0

        lax.fori_loop(0, nq, per_q, 0)
        dk_ref[...] = dka_ref[...].astype(dk_ref.dtype)
        dv_ref[...] = dva_ref[...].astype(dv_ref.dtype)
        _emit([dq_ref, dk_ref, dv_ref], dproj_ref, [(a * groups + group) * wide for a in range(3)], out_sems)

    hs = lambda off: pl.BlockSpec((t, wide), lambda h: (0, off + h))
    return _call(
        comm, body, grid=(groups,), in_specs=[hs(0), hs(groups), hs(2 * groups), hs(0), _ANY], out_specs=_ANY,
        out_shape=S(dproj.shape, dproj.dtype), input_output_aliases={4: 0},
        scratch_shapes=[pltpu.VMEM((t, wide), F32), pltpu.VMEM((t, wide), F32),
                        pltpu.VMEM((ATTN_HEADS, t, bq), F32), pltpu.VMEM((ATTN_HEADS, t, bq), F32)]
        + [pltpu.VMEM((t, wide), dproj.dtype)] * 3 + [pltpu.SemaphoreType.DMA((3,))],
        compiler_params=_cp("parallel"), name=name)(proj, proj, proj, do, dproj)


def _shift_down(cur, prev8, k):
    if k == 0:
        return cur
    row8 = lax.broadcasted_iota(jnp.int32, prev8.shape, 0)
    rc = pltpu.roll(cur, k, 0)
    top = jnp.where(row8 < k, pltpu.roll(prev8, k, 0), rc[0:8, :])
    return jnp.concatenate([top, rc[8:, :]], axis=0)


def _shift_up(cur, next8, k):
    if k == 0:
        return cur
    n = cur.shape[0]
    row8 = lax.broadcasted_iota(jnp.int32, next8.shape, 0)
    rc = pltpu.roll(cur, n - k, 0)
    bottom = jnp.where(row8 >= 8 - k, pltpu.roll(next8, 8 - k, 0), rc[n - 8:, :])
    return jnp.concatenate([rc[:n - 8, :], bottom], axis=0)


def _lru_conv(xl, prev8, cw, cb):
    xs = [_shift_down(xl, prev8, CONV_WIDTH - 1 - k) for k in range(CONV_WIDTH)]
    xc = xs[0] * cw[0:1, :]
    for k in range(1, CONV_WIDTH):
        xc = xc + xs[k] * cw[k:k + 1, :]
    return xs, xc + cb


def _lru_gates(xl, prev8, cw, cb, wr, br, wi, bi, ls):
    xs, xc = _lru_conv(xl, prev8, cw, cb)
    xcb = xc.astype(_MXU)
    r = jax.nn.sigmoid(_dot(xcb, wr) + br)
    i = jax.nn.sigmoid(_dot(xcb, wi) + bi)
    la = (LRU_C * r) * ls
    a = jnp.exp(la)
    mult = jnp.sqrt(-_expm1(2.0 * la))
    return xs, xc, r, i, a, mult


def _group_scan(a, b, reverse):
    n = a.shape[0]
    row = lax.broadcasted_iota(jnp.int32, a.shape, 0) % 8
    for d in (1, 2, 4):
        if reverse:
            m = row < 8 - d
            a_s, b_s = pltpu.roll(a, n - d, 0), pltpu.roll(b, n - d, 0)
        else:
            m = row >= d
            a_s, b_s = pltpu.roll(a, d, 0), pltpu.roll(b, d, 0)
        b = jnp.where(m, a * b_s + b, b)
        a = jnp.where(m, a * a_s, a)
    return a, b


def _lru_fwd(proj, col0, n_blocks, cw, cb, wr, br, wi, bi, lam, name, comm=None):
    t = proj.shape[0]
    tt = min(t, SEQ_TILE)
    nt = t // tt

    def body(xl_ref, gl_ref, cw_ref, cb_ref, wr_ref, br_ref, wi_ref, bi_ref, lam_ref, h_ref, y_ref, *kept):
        cwv, cbv, brv, biv = cw_ref[...], cb_ref[...], br_ref[...], bi_ref[...]
        wrv, wiv = wr_ref[...].astype(_MXU), wi_ref[...].astype(_MXU)
        ls = _log_sigmoid(lam_ref[...])

        def tile(ti, hin):
            t0 = pl.multiple_of(ti * tt, tt)
            p0 = pl.multiple_of(jnp.maximum(t0 - 8, 0), 8)
            prev8 = xl_ref[pl.ds(p0, 8), :] * (ti > 0).astype(F32)
            xl = xl_ref[pl.ds(t0, tt), :]
            _, xc, r, ig, a, mult = _lru_gates(xl, prev8, cwv, cbv, wrv, brv, wiv, biv, ls)
            for ref, val in zip(kept, (r, ig, a, mult)):
                ref[pl.ds(t0, tt), :] = val
            ga, gb = _group_scan(a, mult * (ig * xc), False)
            for g in range(tt // 8):
                hg = ga[8 * g:8 * g + 8, :] * hin + gb[8 * g:8 * g + 8, :]
                h_ref[pl.ds(t0 + 8 * g, 8), :] = hg
                hin = hg[7:8, :]
            y_ref[pl.ds(t0, tt), :] = h_ref[pl.ds(t0, tt), :] * _gelu(gl_ref[pl.ds(t0, tt), :])
            return hin

        lax.fori_loop(0, nt, tile, jnp.zeros((1, HEAD_DIM), F32))

    cs = lambda off: pl.BlockSpec((t, HEAD_DIM), lambda n: (0, off + n))
    vs = pl.BlockSpec((1, HEAD_DIM), lambda n: (0, n))
    ws = pl.BlockSpec((None, HEAD_DIM, HEAD_DIM), lambda n: (n, 0, 0))
    w = n_blocks * HEAD_DIM
    return _call(
        comm, body, grid=(n_blocks,),
        in_specs=[cs(col0), cs(col0 + n_blocks), pl.BlockSpec((CONV_WIDTH, HEAD_DIM), lambda n: (0, n)), vs, ws, vs, ws, vs, vs],
        out_specs=[cs(0)] * 6, out_shape=[S((t, w), F32)] * 6,
        compiler_params=_cp("parallel"), name=name)(proj, proj, cw, cb, wr, br, wi, bi, lam)


def _lru_bwd(proj, col0, n_blocks, h, kept, dyl, cw, cb, wr, wi, lam, name, comm=None):
    t = proj.shape[0]
    tt = min(t, SEQ_TILE)
    nt = t // tt

    def body(xl_ref, gl_ref, h_ref, r_ref, i_ref, a_ref, m_ref, dy_ref, cw_ref, cb_ref, wr_ref, wi_ref, lam_ref,
             dproj_ref, dcw_ref, dcb_ref, dwr_ref, dbr_ref, dwi_ref, dbi_ref, dlam_ref, g_ref, dxl_ref, dgl_ref, out_sems):
        block = pl.program_id(0)
        cwv, cbv = cw_ref[...], cb_ref[...]
        wrv, wiv = wr_ref[...].astype(_MXU), wi_ref[...].astype(_MXU)
        lamv = lam_ref[...]
        ls = _log_sigmoid(lamv)
        for ref in (dcw_ref, dcb_ref, dwr_ref, dbr_ref, dwi_ref, dbi_ref, dlam_ref):
            ref[...] = jnp.zeros_like(ref)

        def tile(s, carry):
            e_in, dxc_next8 = carry
            ti = nt - 1 - s
            t0 = pl.multiple_of(ti * tt, tt)
            p0 = pl.multiple_of(jnp.maximum(t0 - 8, 0), 8)
            first = (ti > 0).astype(F32)
            xl = xl_ref[pl.ds(t0, tt), :]
            xs, xc = _lru_conv(xl, xl_ref[pl.ds(p0, 8), :] * first, cwv, cbv)
            r, ig, a, mult = (ref[pl.ds(t0, tt), :] for ref in (r_ref, i_ref, a_ref, m_ref))
            hv = h_ref[pl.ds(t0, tt), :]
            h_before = _shift_down(hv, h_ref[pl.ds(p0, 8), :] * first, 1)
            glv = gl_ref[pl.ds(t0, tt), :]
            dyv = dy_ref[pl.ds(t0, tt), :]
            dgl_ref[pl.ds(t0, tt), :] = (dyv * hv * _gelu_grad(glv)).astype(dgl_ref.dtype)
            dh = dyv * _gelu(glv)
            row = lax.broadcasted_iota(jnp.int32, a.shape, 0)
            coef = jnp.where(row == tt - 1, 1.0, pltpu.roll(a, tt - 1, 0))
            ga, gb = _group_scan(coef, dh, True)
            gin = e_in
            for g in reversed(range(tt // 8)):
                gg = ga[8 * g:8 * g + 8, :] * gin + gb[8 * g:8 * g + 8, :]
                g_ref[8 * g:8 * g + 8, :] = gg
                gin = gg[0:1, :]
            gv = g_ref[...]
            e_out = a[0:1, :] * gv[0:1, :]
            ix = ig * xc
            dla = (gv * h_before) * a - (gv * ix) * (a * a / mult)
            dlam_ref[...] += jnp.sum(dla * (LRU_C * r), axis=0, keepdims=True)
            dpr = (dla * (LRU_C * ls)) * (r * (1.0 - r))
            dpi = (gv * mult * xc) * (ig * (1.0 - ig))
            dbr_ref[...] += jnp.sum(dpr, axis=0, keepdims=True)
            dbi_ref[...] += jnp.sum(dpi, axis=0, keepdims=True)
            xcb, dprb, dpib = xc.astype(_MXU), dpr.astype(_MXU), dpi.astype(_MXU)
            dwr_ref[...] += _dot_tn(xcb, dprb)
            dwi_ref[...] += _dot_tn(xcb, dpib)
            dxc = gv * mult * ig + _dot_nt(dprb, wrv) + _dot_nt(dpib, wiv)
            dcb_ref[...] += jnp.sum(dxc, axis=0, keepdims=True)
            dxl = None
            for k in range(CONV_WIDTH):
                dcw_ref[k:k + 1, :] += jnp.sum(dxc * xs[k], axis=0, keepdims=True)
                term = _shift_up(dxc, dxc_next8, CONV_WIDTH - 1 - k) * cwv[k:k + 1, :]
                dxl = term if dxl is None else dxl + term
            dxl_ref[pl.ds(t0, tt), :] = dxl.astype(dxl_ref.dtype)
            return e_out, dxc[0:8, :]

        lax.fori_loop(0, nt, tile, (jnp.zeros((1, HEAD_DIM), F32), jnp.zeros((8, HEAD_DIM), F32)))
        dlam_ref[...] = dlam_ref[...] * (1.0 - jax.nn.sigmoid(lamv))
        _emit([dxl_ref, dgl_ref], dproj_ref, [(col0 + block) * HEAD_DIM, (col0 + n_blocks + block) * HEAD_DIM], out_sems)

    cs = lambda off: pl.BlockSpec((t, HEAD_DIM), lambda n: (0, off + n))
    vs = pl.BlockSpec((1, HEAD_DIM), lambda n: (0, n))
    ws = pl.BlockSpec((None, HEAD_DIM, HEAD_DIM), lambda n: (n, 0, 0))
    cws = pl.BlockSpec((CONV_WIDTH, HEAD_DIM), lambda n: (0, n))
    w = n_blocks * HEAD_DIM
    vec = S((1, w), F32)
    mat = S((n_blocks, HEAD_DIM, HEAD_DIM), F32)
    return _call(
        comm, body, grid=(n_blocks,),
        in_specs=[cs(col0), cs(col0 + n_blocks)] + [cs(0)] * 6 + [cws, vs, ws, ws, vs],
        out_specs=[_ANY, cws, vs, ws, vs, ws, vs, vs],
        out_shape=[S(proj.shape, _MXU), S((CONV_WIDTH, w), F32), vec, mat, vec, mat, vec, vec],
        scratch_shapes=[pltpu.VMEM((tt, HEAD_DIM), F32), pltpu.VMEM((t, HEAD_DIM), _MXU), pltpu.VMEM((t, HEAD_DIM), _MXU),
                        pltpu.SemaphoreType.DMA((2,))],
        compiler_params=_cp("parallel"), name=name)(proj, proj, h, *kept, dyl, cw, cb, wr, wi, lam)


class _NoExchange:
    grad_dtype = F32

    def __init__(self, weights):
        self.weights, self.grads, self.packs = weights, {}, {}

    def weight(self, name):
        return self.weights[name]

    def in_proj(self, x, gain, bm):
        hn = _rms_fwd(x, gain, "rms1")
        return [hn, *_mm_nn(hn, self.weights["w_in"], bm=bm, bn=self.weights["w_in"].shape[2], name="in_proj", also=_MXU)]

    def conv_w(self):
        return self.weights["conv_w"]

    def carrier(self, call):
        return None

    def harvest(self, car):
        pass

    def alone(self, call):
        pass


def _local_step(x, target, norms, ex, cb, wr, br, wi, bi, lam, ga, gl):
    g_pre_mix, g_post_mix, g_pre_ffn, g_post_ffn = norms
    t, d = x.shape
    bm = min(t, MM_ROWS)
    bt = min(t, DW_TOKENS)

    def run(fn, name, *args, **kw):
        car = ex.carrier(name)
        out = fn(*args, name=name, comm=car, **kw)
        ex.harvest(car)
        return out

    hn1, proj, proj_mx = ex.in_proj(x, g_pre_mix, bm)
    win3, cw = ex.weight("w_in"), ex.conv_w()
    c = win3.shape[0]
    o = run(_attn_fwd, "attn_fwd", proj_mx, (proj.shape[1] - d) // 3 // HEAD_DIM)
    mix = 2 * o.shape[1]
    n_heads = n_blocks = o.shape[1] // HEAD_DIM
    h, yl, *kept = run(_lru_fwd, "lru_fwd", proj, 3 * n_heads, n_blocks, cw, cb, wr, br, wi, bi, lam)
    y = run(_outnorm_fwd, "outnorm_fwd", o, yl, ga, gl)
    wout = ex.weight("w_out")
    mixo = run(_mm_nn, "out_proj", y, wout[None], bm=bm, bn=d)
    x2, hn2 = run(_mid_fwd, "mid_fwd", x, mixo, g_post_mix, g_pre_ffn)
    ex.alone("gather_w_up_last")
    wg3, wu3 = ex.weight("w_ffn_gate"), ex.weight("w_ffn_up")
    act_dgate, act_dup, act = run(_swiglu_fwd, "ffn_gate_up", hn2, wg3, wu3, bm=bm)
    ex.alone("gather_w_down")
    wd = ex.weight("w_ffn_down")
    ff = wd.shape[0]
    f = _mm_nn(act, wd[None], bm=bm, bn=d // 2, name="ffn_down")
    loss_cols, dout, df, dg_post_ffn = _final(f, x2, target, g_post_ffn, "final")

    dgate, dup = _swiglu_bwd(df, wd, act_dgate, act_dup, bm=min(t, 2 * MM_ROWS), bo=ff // 4, name="ffn_down_bwd")
    ex.grads["w_ffn_down"] = _mm_tn(act, df, 1, bm=bt, bk=DW_ROWS, out_dtype=ex.grad_dtype, name="ffn_down_dw").reshape(c, ff // c, d)
    ex.grads["w_ffn_gate"] = run(_mm_tn, "ffn_gate_dw", hn2, dgate, c, bm=bt, bk=d // 2, out_dtype=ex.grad_dtype)
    ex.grads["w_ffn_up"] = run(_mm_tn, "ffn_up_dw", hn2, dup, c, bm=bt, bk=d // 2, out_dtype=ex.grad_dtype)
    dhn2_g = run(_mm_nt, "ffn_gate_dx", dgate, wg3, bm=bm, bo=d // 2, out_dtype=F32)
    dhn2_u = run(_mm_nt, "ffn_up_dx", dup, wu3, bm=bm, bo=d // 2, out_dtype=F32)
    dx2, dmix, dg_pre_ffn, dg_post_mix = run(_mid_bwd, "mid_bwd", dhn2_g, dhn2_u, dout, x2, mixo, g_pre_ffn, g_post_mix)
    dy = run(_mm_nt, "out_proj_dx", dmix, wout[None], bm=bm, bo=mix, out_dtype=F32)
    ex.grads["w_out"] = _mm_tn(y, dmix, 1, bm=bt, bk=mix // 4, out_dtype=ex.grad_dtype, name="out_proj_dw").reshape(c, mix // c, d)
    do, dyl, dga, dgl_norm = run(_outnorm_bwd, "outnorm_bwd", dy, o, yl, ga, gl)
    dproj, dcw, dcb, dwr, dbr, dwi, dbi, dlam = run(_lru_bwd, "lru_bwd", proj, 3 * n_heads, n_blocks, h, kept, dyl, cw, cb, wr, wi, lam)
    small = dict(post_mix_norm=dg_post_mix, pre_ffn_norm=dg_pre_ffn, post_ffn_norm=dg_post_ffn, conv_w=dcw, conv_b=dcb,
                 w_rgate=dwr, b_rgate=dbr, w_igate=dwi, b_igate=dbi, lru_lambda=dlam, attn_out_norm=dga, lru_out_norm=dgl_norm)
    ex.packs["early"] = _pack([small[n] for n in _SMALL_EARLY])
    dproj = run(_attn_bwd, "attn_bwd", proj_mx, do, dproj, n_heads)
    ex.grads["w_in"] = _mm_tn(hn1, dproj, c, bm=bt, bk=d // 2, out_dtype=ex.grad_dtype, name="in_proj_dw")
    ex.alone("grads_w_in_swap")
    dhn1 = run(_mm_nt, "in_proj_dx", dproj, win3, bm=bm, bo=d // 2, out_dtype=F32)
    grad_x, small["pre_mix_norm"] = run(_first_bwd, "first_bwd", dhn1, dx2, x, g_pre_mix)
    ex.packs["late"] = _pack([small["pre_mix_norm"], (0.5 / d) * jnp.sum(loss_cols, keepdims=True)])
    return loss_cols, grad_x, small


def _into_slot(wsh, slot, dtype, name):
    rows, n = wsh.shape
    rb = _row_block(rows, 512) if rows % 8 == 0 else rows

    def body(s_ref, w_ref, o_ref):
        o_ref[...] = w_ref[...].astype(o_ref.dtype)

    return pl.pallas_call(
        body,
        grid_spec=pltpu.PrefetchScalarGridSpec(
            num_scalar_prefetch=1, grid=(rows // rb,),
            in_specs=[pl.BlockSpec((rb, n), lambda i, s_ref: (i, 0))],
            out_specs=pl.BlockSpec((None, rb, n), lambda i, s_ref: (s_ref[0], i, 0))),
        out_shape=S((4, rows, n), dtype), compiler_params=_cp("parallel"), name=name)(slot, wsh)


class _Exchange:
    SCHEDULE = {
        "in_proj": [("stream", "w_in"), ("ici", "conv_w"), ("ici", "w_ffn_up", 0)],
        "attn_fwd": [("d2d", "w_ffn_up", 0), ("ici", "w_ffn_gate")],
        "lru_fwd": [("d2d", "w_ffn_gate"), ("ici", "w_out"), ("ici", "w_ffn_up", 1)],
        "outnorm_fwd": [("d2d", "w_out"), ("d2d", "w_ffn_up", 1)],
        "out_proj": [("ici", "w_ffn_up", 2)],
        "mid_fwd": [("d2d", "w_ffn_up", 2), ("ici", "w_ffn_up", 3)],
        "gather_w_up_last": [("d2d", "w_ffn_up", 3)],
        "ffn_gate_up": [("ici", "w_ffn_down")],
        "gather_w_down": [("d2d", "w_ffn_down")],
        "ffn_gate_dw": [("swap", "w_ffn_down")],
        "ffn_up_dw": [("scatter", "w_ffn_down", 0), ("scatter", "w_ffn_down", 1), ("swap", "w_ffn_gate")],
        "ffn_gate_dx": [("scatter", "w_ffn_down", 2), ("scatter", "w_ffn_down", 3), ("scatter", "w_ffn_gate", 0), ("swap", "w_ffn_up")],
        "ffn_up_dx": [("share", "w_ffn_down"), ("scatter", "w_ffn_gate", 1), ("scatter", "w_ffn_gate", 2)],
        "mid_bwd": [("scatter", "w_ffn_gate", 3), ("scatter", "w_ffn_up", 0)],
        "out_proj_dx": [("share", "w_ffn_gate"), ("scatter", "w_ffn_up", 1)],
        "outnorm_bwd": [("scatter", "w_ffn_up", 2), ("swap", "w_out")],
        "lru_bwd": [("scatter", "w_ffn_up", 3), ("scatter", "w_out")],
        "attn_bwd": [("share", "w_ffn_up"), ("share", "w_out"), ("spread", "early")],
        "grads_w_in_swap": [("swap", "w_in")],
        "in_proj_dx": [("scatter", "w_in")],
        "grads_w_in_share": [("share", "w_in"), ("spread", "late")],
    }
    PIECES = 4
    grad_dtype = BF16

    def __init__(self, slots, place):
        self.buf, self.place = dict(slots), place
        self.grads, self.packs, self.swapped, self.part, self.scattered, self.full, self.spreaded = {}, {}, {}, {}, {}, {}, {}

    def weight(self, name):
        b = self.buf[name]
        return b.reshape(-1, b.shape[2]) if name in ("w_out", "w_ffn_down") else b

    def in_proj(self, x, gain, bm):
        car = self.carrier("in_proj")
        out = _in_proj_streamed(x, gain, car, car.streamed, self.place, bm=bm, name="in_proj")
        self.harvest(car)
        return out

    def conv_w(self):
        return jnp.transpose(self.buf["conv_w"], (1, 0, 2)).reshape(CONV_WIDTH, -1)

    def carrier(self, call):
        if call not in self.SCHEDULE:
            return None
        car = _Carrier()
        car.todo, slot = [], {}
        for kind, name, *piece in self.SCHEDULE[call]:
            if kind in ("ici", "d2d", "stream"):
                if name not in slot:
                    slot[name] = car.inplace(self.buf[name])
                    car.todo.append((self.buf, name, slot[name]))
            if kind == "stream":
                car.streamed = slot[name]
            elif kind in ("ici", "d2d"):
                size = self.buf[name].shape[1] // 2 // self.PIECES
                rows = (piece[0] * size, size) if piece else None
                if kind == "ici":
                    car.gather_ici(slot[name], rows, split=name != "conv_w")
                else:
                    car.gather_d2d(slot[name], rows)
            elif kind == "swap":
                g = self.grads[name]
                o = car.fresh((4, g.shape[1] // 2, g.shape[2]), g.dtype)
                car.swap(car.read(g), o)
                car.todo.append((self.swapped, name, o))
            elif kind == "scatter":
                if name not in self.part:
                    self.part[name] = _add_own_half(self.grads[name], self.swapped[name], self.place[1:], "grads_add_" + name)
                p = self.part[name]
                key = ("scatter", name)
                if key not in slot:
                    slot[key] = (car.read(p), car.inplace(self.scattered[name]) if name in self.scattered else car.fresh(p.shape, p.dtype))
                    car.todo.append((self.scattered, name, slot[key][1]))
                size = p.shape[1] // self.PIECES
                car.scatter(*slot[key], (piece[0] * size, size) if piece else None)
            elif kind == "share":
                o = car.inplace(_sum_chips(self.part[name], self.scattered[name], self.place, "grads_sum_" + name))
                car.share(o)
                car.todo.append((self.full, name, o))
            else:
                o = car.fresh((8,) + self.packs[name].shape, F32)
                car.spread(car.read(self.packs[name]), o)
                car.todo.append((self.spreaded, name, o))
        return car

    def harvest(self, car):
        for state, name, o in (car.todo if car is not None else []):
            state[name] = car.results[o]

    def alone(self, call):
        car = self.carrier(call)
        car.run_alone(call)
        self.harvest(car)

    def small_sum(self, key):
        return _sum_devices(self.packs[key], self.spreaded[key], 2 * self.place[0:1] + self.place[1:], "grads_small_sum_" + key)


def _row_block(rows, cap):
    return max(b for b in range(8, cap + 1, 8) if rows % b == 0)


def _add_own_half(g, recv, core, name):
    _, rows, n = g.shape
    half = rows // 2
    rb = _row_block(half, 1024)
    nb = half // rb

    def body(c_ref, g_ref, r_ref, o_ref):
        o_ref[...] = (g_ref[...].astype(F32) + r_ref[...].astype(F32)).astype(o_ref.dtype)

    return pl.pallas_call(
        body,
        grid_spec=pltpu.PrefetchScalarGridSpec(
            num_scalar_prefetch=1, grid=(4, nb),
            in_specs=[pl.BlockSpec((None, rb, n), lambda k, i, c_ref: (k, c_ref[0] * nb + i, 0)),
                      pl.BlockSpec((None, rb, n), lambda k, i, c_ref: (k, i, 0))],
            out_specs=pl.BlockSpec((None, rb, n), lambda k, i, c_ref: (k, i, 0))),
        out_shape=S((4, half, n), BF16), compiler_params=_cp("parallel", "parallel"), name=name)(core, g, recv)


def _sum_chips(part, recv, place, name):
    _, rows, n = part.shape
    rb = _row_block(rows, 256)
    nb = rows // rb

    def body(p_ref, own_ref, r0, r1, r2, r3, o_ref):
        own = own_ref[...].astype(F32)
        terms = [jnp.where(p_ref[0] == k, own, r[...].astype(F32)) for k, r in enumerate((r0, r1, r2, r3))]
        o_ref[...] = ((terms[0] + terms[1]) + terms[2]) + terms[3]

    def slot(k):
        return pl.BlockSpec((None, rb, n), lambda i, p_ref: (jnp.where(p_ref[0] == k, (k + 1) % 4, k), i, 0))

    return pl.pallas_call(
        body,
        grid_spec=pltpu.PrefetchScalarGridSpec(
            num_scalar_prefetch=1, grid=(nb,),
            in_specs=[pl.BlockSpec((None, rb, n), lambda i, p_ref: (p_ref[0], i, 0))] + [slot(k) for k in range(4)],
            out_specs=pl.BlockSpec((rb, n), lambda i, p_ref: (p_ref[1] * nb + i, 0))),
        out_shape=S((2 * rows, n), F32), compiler_params=_cp("parallel"), name=name)(place, part, recv, recv, recv, recv)


def _sum_devices(own, spread, me, name):
    rows = own.shape[0]

    def body(me_ref, own_ref, *refs):
        acc = None
        for k, r in enumerate(refs[:8]):
            term = jnp.where(me_ref[0] == k, own_ref[...], r[...])
            acc = term if acc is None else acc + term
        refs[8][...] = acc

    def slot(k):
        return pl.BlockSpec((None, rows, 128), lambda i, me_ref: (jnp.where(me_ref[0] == k, (k + 1) % 8, k), 0, 0))

    whole = pl.BlockSpec((rows, 128), lambda i, me_ref: (0, 0))
    return pl.pallas_call(
        body,
        grid_spec=pltpu.PrefetchScalarGridSpec(num_scalar_prefetch=1, grid=(1,), in_specs=[whole] + [slot(k) for k in range(8)],
                                               out_specs=whole),
        out_shape=S((rows, 128), F32), compiler_params=_cp("arbitrary"), name=name)(me, own, *[spread] * 8)


def _adamw(w, g, m, v, name, regive=False):
    rows, n = w.shape
    rb = rows if rows * n * 4 <= (1 << 21) else _row_block(rows, 512)
    c1 = 1.0 - ADAM_B1 ** ADAM_STEP
    c2 = 1.0 - ADAM_B2 ** ADAM_STEP

    def body(w_ref, g_ref, m_ref, v_ref, d_ref, nm_ref, nv_ref, *again):
        gv = g_ref[...]
        for ref in again:
            ref[...] = gv
        nm = ADAM_B1 * m_ref[...] + (1.0 - ADAM_B1) * gv
        nv = ADAM_B2 * v_ref[...] + (1.0 - ADAM_B2) * (gv * gv)
        nm_ref[...] = nm
        nv_ref[...] = nv
        d_ref[...] = -ADAM_LR * ((nm / c1) / (jnp.sqrt(nv / c2) + ADAM_EPS) + ADAM_WD * w_ref[...])

    bs = pl.BlockSpec((rb, n), lambda i: (i, 0))
    n_out = 4 if regive else 3
    return pl.pallas_call(body, grid=(rows // rb,), in_specs=[bs] * 4, out_specs=[bs] * n_out, out_shape=[S((rows, n), F32)] * n_out,
                          compiler_params=_cp("parallel"), name=name)(w, g, m, v)


_BIG = ("w_in", "w_out", "w_ffn_gate", "w_ffn_up", "w_ffn_down")
_SMALL = ("pre_mix_norm", "post_mix_norm", "pre_ffn_norm", "post_ffn_norm", "conv_w", "conv_b", "w_rgate", "b_rgate",
          "w_igate", "b_igate", "lru_lambda", "attn_out_norm", "lru_out_norm")
_SMALL_EARLY = _SMALL[1:]
_WEIGHTS = ("pre_mix_norm", "post_mix_norm", "pre_ffn_norm", "post_ffn_norm", "w_in", "conv_w", "conv_b", "w_rgate", "b_rgate",
            "w_igate", "b_igate", "lru_lambda", "attn_out_norm", "lru_out_norm", "w_out", "w_ffn_gate", "w_ffn_up", "w_ffn_down")


def _pack(arrays):
    flat = []
    for a in arrays:
        f = a.reshape(-1)
        flat.append(jnp.pad(f, (0, (-f.shape[0]) % 1024)))
    return jnp.concatenate(flat).reshape(-1, 128)


def _unpack(packed, shapes):
    out, pos = [], 0
    flat = packed.reshape(-1)
    for s in shapes:
        size = math.prod(s)
        out.append(flat[pos:pos + size].reshape(s))
        pos += size + (-size) % 1024
    return out


def kernel(x, pre_mix_norm, post_mix_norm, pre_ffn_norm, post_ffn_norm, w_in, conv_w, conv_b, w_rgate, b_rgate, w_igate, b_igate, lru_lambda, attn_out_norm, lru_out_norm, w_out, w_ffn_gate, w_ffn_up, w_ffn_down, loss_target, m_pre_mix_norm, m_post_mix_norm, m_pre_ffn_norm, m_post_ffn_norm, m_w_in, m_conv_w, m_conv_b, m_w_rgate, m_b_rgate, m_w_igate, m_b_igate, m_lru_lambda, m_attn_out_norm, m_lru_out_norm, m_w_out, m_w_ffn_gate, m_w_ffn_up, m_w_ffn_down, v_pre_mix_norm, v_post_mix_norm, v_pre_ffn_norm, v_post_ffn_norm, v_w_in, v_conv_w, v_conv_b, v_w_rgate, v_b_rgate, v_w_igate, v_b_igate, v_lru_lambda, v_attn_out_norm, v_lru_out_norm, v_w_out, v_w_ffn_gate, v_w_ffn_up, v_w_ffn_down):
    given = dict(locals())
    w = {n: given[n][0] for n in _WEIGHTS}
    m = {n: given["m_" + n][0] for n in _WEIGHTS}
    v = {n: given["v_" + n][0] for n in _WEIGHTS}
    xs, target = x[0], loss_target[0]
    d = xs.shape[1]
    chip = (2 * lax.axis_index("x") + lax.axis_index("y")).astype(jnp.int32)
    place = jnp.stack([chip, lax.axis_index("c").astype(jnp.int32)])

    slots = {n: _into_slot(w[n], place[0:1], _MXU, "slot_" + n) for n in _BIG}
    slots["conv_w"] = _into_slot(w["conv_w"], place[0:1], F32, "slot_conv_w")
    ex = _Exchange(slots, place)
    row = lambda a: a.reshape(1, -1)
    norms = tuple(row(w[n]) for n in ("pre_mix_norm", "post_mix_norm", "pre_ffn_norm", "post_ffn_norm"))

    loss_cols, grad_x, small = _local_step(
        xs, target, norms, ex, row(w["conv_b"]), w["w_rgate"], row(w["b_rgate"]),
        w["w_igate"], row(w["b_igate"]), row(w["lru_lambda"]), row(w["attn_out_norm"]), row(w["lru_out_norm"]))


    ex.alone("grads_w_in_share")
    reduced = {n: ex.full[n] for n in _BIG}
    early = _unpack(ex.small_sum("early"), [small[n].shape for n in _SMALL_EARLY])
    late = _unpack(ex.small_sum("late"), [small["pre_mix_norm"].shape, (1, 1)])
    loss = late[1][0, 0]
    for n, g in zip(_SMALL_EARLY + ("pre_mix_norm",), early + late[:1]):
        reduced[n] = g.reshape(w[n].shape) if n != "conv_w" else lax.dynamic_slice_in_dim(g, chip * w[n].shape[1], w[n].shape[1], axis=1)

    delta, new_m, new_v = {}, {}, {}
    for n in _BIG:
        delta[n], new_m[n], new_v[n], reduced[n] = _adamw(w[n], reduced[n], m[n], v[n], "adamw_" + n, regive=True)
    shapes = [w[n].shape for n in _SMALL]
    packed = _adamw(*[_pack([src[n] for n in _SMALL]) for src in (w, reduced, m, v)], "adamw_small")
    for out, p in zip((delta, new_m, new_v), packed):
        out.update(zip(_SMALL, _unpack(p, shapes)))

    lead = lambda a: a[None]
    return (loss, lead(grad_x), *[lead(reduced[n]) for n in _WEIGHTS], *[lead(delta[n]) for n in _WEIGHTS],
            *[lead(new_m[n]) for n in _WEIGHTS], *[lead(new_v[n]) for n in _WEIGHTS])
```

```python
import functools
import math

import jax
import jax.numpy as jnp
from jax import lax
from jax.experimental import pallas as pl
from jax.experimental.pallas import tpu as pltpu

F32 = jnp.float32
BF16 = jnp.bfloat16
_MXU = BF16
S = jax.ShapeDtypeStruct

RMS_EPS = 1e-6
HEAD_DIM = 128
CONV_WIDTH = 4
LRU_C = 8.0
ADAM_LR, ADAM_B1, ADAM_B2, ADAM_EPS, ADAM_WD, ADAM_STEP = 0.001, 0.9, 0.999, 1e-08, 0.01, 10
EXP_CUT = -105.0
VMEM_LIMIT = 60 * 1024 * 1024
ROW_TILE = 512
SEQ_TILE = 256
ATTN_BLOCK = 256
ATTN_HEADS = 2
MM_ROWS = 512
DW_TOKENS = 4096
DW_ROWS = 512
MESH = pl.DeviceIdType.MESH


def _cp(*sem):
    return pltpu.CompilerParams(dimension_semantics=sem, vmem_limit_bytes=VMEM_LIMIT)


def _dot(a, b):
    return jnp.dot(a, b, preferred_element_type=F32)


def _dot_nt(a, b):
    return lax.dot_general(a, b, (((1,), (1,)), ((), ())), preferred_element_type=F32)


def _dot_tn(a, b):
    return lax.dot_general(a, b, (((0,), (0,)), ((), ())), preferred_element_type=F32)


def _rstd(v):
    return lax.rsqrt(jnp.mean(v * v, axis=-1, keepdims=True) + RMS_EPS)


def _rms_bwd(dn, vh, r, gain):
    dvh = dn * gain
    dv = r * (dvh - vh * jnp.mean(dvh * vh, axis=-1, keepdims=True))
    return dv, jnp.sum(dn * vh, axis=0, keepdims=True)


def _log_sigmoid(z):
    return jnp.minimum(z, 0.0) - jnp.log(1.0 + jnp.exp(-jnp.abs(z)))


def _expm1(v):
    small = v * (1.0 + v * (0.5 + v * (1.0 / 6.0 + v * (1.0 / 24.0 + v * (1.0 / 120.0)))))
    return jnp.where(jnp.abs(v) < 0.04, small, jnp.exp(v) - 1.0)


_GELU_C = math.sqrt(2.0 / math.pi)


def _gelu(v):
    return 0.5 * v * (1.0 + jnp.tanh(_GELU_C * (v + 0.044715 * v * v * v)))


def _gelu_grad(v):
    th = jnp.tanh(_GELU_C * (v + 0.044715 * v * v * v))
    return 0.5 * (1.0 + th) + 0.5 * v * (1.0 - th * th) * _GELU_C * (1.0 + 3.0 * 0.044715 * v * v)


def _row_spec(tm, d):
    return pl.BlockSpec((tm, d), lambda i: (i, 0))


def _vec_spec(d):
    return pl.BlockSpec((1, d), lambda i: (0, 0))


_ANY = pl.BlockSpec(memory_space=pl.ANY)


def _place():
    x, y, c = lax.axis_index("x"), lax.axis_index("y"), lax.axis_index("c")
    return x, y, c, [(1 - x, y), (x, 1 - y), (1 - x, 1 - y)]


def _remote(src, dst, send_sem, recv_sem, to):
    return pltpu.make_async_remote_copy(src_ref=src, dst_ref=dst, send_sem=send_sem, recv_sem=recv_sem,
                                        device_id=to, device_id_type=MESH)


class _Carrier:
    def __init__(self):
        self.inputs, self.out_shapes, self.aliases, self.ops, self.n_sems, self.results = [], [], {}, [], 0, None

    def inplace(self, arr):
        self.aliases[len(self.inputs)] = len(self.out_shapes)
        self.inputs.append(arr)
        self.out_shapes.append(S(arr.shape, arr.dtype))
        return len(self.out_shapes) - 1

    def read(self, arr):
        self.inputs.append(arr)
        return len(self.inputs) - 1

    def fresh(self, shape, dtype):
        self.out_shapes.append(S(shape, dtype))
        return len(self.out_shapes) - 1

    def _add(self, n_sems, copies):
        base = self.n_sems
        self.n_sems += n_sems

        def start(ins, outs, send, recv):
            for k, (src, dst, _, to) in enumerate(copies(ins, outs)):
                _remote(src, dst, send.at[base + k], recv.at[base + k], to).start()

        def finish(ins, outs, send, recv):
            for k, (src, _, land, to) in enumerate(copies(ins, outs)):
                _remote(src, land, send.at[base + k], recv.at[base + k], to).wait()

        self.ops.append((start, finish))

    def gather_ici(self, o, rows=None, split=True):
        half = self.out_shapes[o].shape[1] // 2
        lo, size = rows or (0, half)

        def copies(ins, outs):
            x, y, c, chips = _place()
            part = (lambda ref: ref.at[pl.ds(c * half + lo, size)]) if split else (lambda ref: ref)
            mine = part(outs[o].at[2 * x + y])
            return [(mine, mine, part(outs[o].at[2 * px + py]), (px, py, c)) for px, py in chips]

        self._add(3, copies)

    def gather_d2d(self, o, rows=None):
        half = self.out_shapes[o].shape[1] // 2
        lo, size = rows or (0, half)

        def copies(ins, outs):
            x, y, c, chips = _place()
            at = lambda k, cc: outs[o].at[k].at[pl.ds(cc * half + lo, size)]
            return [(at(2 * px + py, c), at(2 * px + py, c), at(2 * px + py, 1 - c), (x, y, 1 - c)) for px, py in chips]

        self._add(3, copies)

    def swap(self, i, o):
        half = self.inputs[i].shape[1] // 2

        def copies(ins, outs):
            x, y, c, _ = _place()
            return [(ins[i].at[:, pl.ds((1 - c) * half, half)], outs[o], outs[o], (x, y, 1 - c))]

        self._add(1, copies)

    def scatter(self, i, o, rows=None):
        lo, size = rows or (0, self.inputs[i].shape[1])

        def copies(ins, outs):
            x, y, c, chips = _place()
            cut = lambda ref: ref.at[pl.ds(lo, size)]
            return [(cut(ins[i].at[2 * px + py]), cut(outs[o].at[2 * x + y]), cut(outs[o].at[2 * px + py]), (px, py, c)) for px, py in chips]

        self._add(3, copies)

    def share(self, o):
        r = self.out_shapes[o].shape[0] // 2

        def copies(ins, outs):
            x, y, c, _ = _place()
            mine = outs[o].at[pl.ds(c * r, r)]
            return [(mine, mine, outs[o].at[pl.ds((1 - c) * r, r)], (x, y, 1 - c))]

        self._add(1, copies)

    def spread(self, i, o):
        def copies(ins, outs):
            x, y, c, _ = _place()
            me = 4 * x + 2 * y + c
            out = []
            for d in range(1, 8):
                to, frm = (me + d) % 8, (me + 8 - d) % 8
                out.append((ins[i], outs[o].at[me], outs[o].at[frm], (to // 4, (to // 2) % 2, to % 2)))
            return out

        self._add(7, copies)

    def _pallas(self, body, n_in, n_out, scratch, **kw):
        k_in, k_out = len(self.inputs), len(self.out_shapes)
        grid = kw.get("grid", ())

        def wrapped(*refs):
            ins, cins = refs[:n_in], refs[n_in:n_in + k_in]
            outs = refs[n_in + k_in:n_in + k_in + n_out]
            couts = refs[n_in + k_in + n_out:n_in + k_in + n_out + k_out]
            own = refs[n_in + k_in + n_out + k_out:]
            send, recv = own[len(scratch):]
            ids = [pl.program_id(a) for a in range(len(grid))]
            first = functools.reduce(jnp.logical_and, [a == 0 for a in ids], True)
            last = functools.reduce(jnp.logical_and, [a == g - 1 for a, g in zip(ids, grid)], True)

            def go(stage):
                for op in self.ops:
                    op[stage](cins, couts, send, recv)

            if grid:
                pl.when(first)(lambda: go(0))
                body(*ins, *outs, *own[:len(scratch)])
                pl.when(last)(lambda: go(1))
            else:
                go(0)
                go(1)

        sem = pltpu.SemaphoreType.DMA((self.n_sems,))
        return pl.pallas_call(
            wrapped, in_specs=list(kw.get("in_specs", [])) + [_ANY] * k_in, out_specs=list(kw.get("out_specs", [])) + [_ANY] * k_out,
            out_shape=list(kw.get("out_shape", [])) + self.out_shapes, scratch_shapes=list(scratch) + [sem, sem],
            input_output_aliases={**kw.get("aliases", {}), **{n_in + i: n_out + o for i, o in self.aliases.items()}}, name=kw["name"],
            **({"grid": grid, "compiler_params": _cp(*["arbitrary"] * len(grid))} if grid else {}))

    def run(self, body, kw, *args):
        single = not isinstance(kw["out_shape"], (list, tuple))
        out_shape = [kw["out_shape"]] if single else list(kw["out_shape"])
        out_specs = [kw["out_specs"]] if single else list(kw["out_specs"])
        res = self._pallas(body, len(args), len(out_shape), kw.get("scratch_shapes", []), grid=kw["grid"], in_specs=kw["in_specs"],
                           out_specs=out_specs, out_shape=out_shape, name=kw["name"],
                           aliases=kw.get("input_output_aliases", {}))(*args, *self.inputs)
        self.results = list(res[len(out_shape):])
        return res[0] if single else list(res[:len(out_shape)])

    def run_alone(self, name):
        self.results = list(self._pallas(None, 0, 0, [], name=name)(*self.inputs))


def _call(comm, body, **kw):
    if comm is None:
        return pl.pallas_call(body, **kw)
    return functools.partial(comm.run, body, kw)


def _in_proj_streamed(x, gain, car, o_w, place, *, bm, name):
    m, k = x.shape
    n = car.out_shapes[o_w].shape[2]
    ni, half = m // bm, k // 2
    k_in, k_out = len(car.inputs), len(car.out_shapes)
    order = lambda p: ((p & 1) << 1) | (p >> 1)

    def body(place_ref, x_ref, g_ref, *refs):
        cins, (hn_ref, o_ref, ob_ref), couts = refs[:k_in], refs[k_in:k_in + 3], refs[k_in + 3:k_in + 3 + k_out]
        wbuf, hn_all, local, ici_send, ici_recv, d2d_send, d2d_recv, send, recv = refs[k_in + 3 + k_out:]
        p, i = pl.program_id(0), pl.program_id(1)
        x, y, c, chips = _place()
        me = 2 * x + y
        rows = lambda chunk, cc: couts[o_w].at[chunk].at[pl.ds(cc * half, half)]

        @pl.when(jnp.logical_and(p == 0, i == 0))
        def _():
            for j, (px, py) in enumerate(chips):
                _remote(rows(me, c), rows(me, c), ici_send.at[j], ici_recv.at[j], (px, py, c)).start()
            for op in car.ops:
                op[0](cins, couts, send, recv)

        for j, (px, py) in enumerate(chips):
            @pl.when(jnp.logical_and(p == j + 1, i == 0))
            def _(j=j, px=px, py=py):
                landed, other = rows(2 * px + py, c), rows(2 * px + py, 1 - c)
                _remote(landed, landed, ici_send.at[j], ici_recv.at[j], (px, py, c)).wait_recv()
                _remote(landed, landed, d2d_send.at[j], d2d_recv.at[j], (x, y, 1 - c)).start()
                _remote(other, other, d2d_send.at[j], d2d_recv.at[j], (x, y, 1 - c)).wait_recv()

        @pl.when(i == 0)
        def _():
            cp = pltpu.make_async_copy(couts[o_w].at[me ^ order(p)], wbuf, local.at[0])
            cp.start()
            cp.wait()

        tile = pl.ds(pl.multiple_of(i * bm, bm), bm)

        @pl.when(p == 0)
        def _():
            xv = x_ref[...]
            hn_all[tile, :] = ((xv * _rstd(xv)) * g_ref[...]).astype(_MXU)

        hn = hn_all[tile, :]
        hn_ref[...] = hn
        res = _dot(hn, wbuf[...])
        o_ref[...] = res
        ob_ref[...] = res.astype(ob_ref.dtype)

        @pl.when(jnp.logical_and(p == 3, i == ni - 1))
        def _():
            for j, (px, py) in enumerate(chips):
                _remote(rows(me, c), rows(me, c), ici_send.at[j], ici_recv.at[j], (px, py, c)).wait_send()
                _remote(rows(me, c), rows(me, c), d2d_send.at[j], d2d_recv.at[j], (x, y, 1 - c)).wait_send()
            for op in car.ops:
                op[1](cins, couts, send, recv)

    ospec = pl.BlockSpec((bm, n), lambda p, i, place_ref: (i, place_ref[0] ^ order(p)))
    rows = pl.BlockSpec((bm, k), lambda p, i, place_ref: (jnp.where(p == 0, i, 0), 0))
    three, sems = pltpu.SemaphoreType.DMA((3,)), pltpu.SemaphoreType.DMA((max(car.n_sems, 1),))
    res = pl.pallas_call(
        body,
        grid_spec=pltpu.PrefetchScalarGridSpec(
            num_scalar_prefetch=1, grid=(4, ni),
            in_specs=[rows, pl.BlockSpec((1, k), lambda p, i, place_ref: (0, 0))] + [_ANY] * k_in,
            out_specs=[pl.BlockSpec((bm, k), lambda p, i, place_ref: (p * ni + i, 0)), ospec, ospec] + [_ANY] * k_out,
            scratch_shapes=[pltpu.VMEM((k, n), _MXU), pltpu.VMEM((m, k), _MXU), pltpu.SemaphoreType.DMA((1,)),
                            three, three, three, three, sems, sems]),
        out_shape=[S((4 * m, k), _MXU), S((m, 4 * n), F32), S((m, 4 * n), _MXU)] + car.out_shapes,
        input_output_aliases={3 + a: 3 + o for a, o in car.aliases.items()},
        compiler_params=_cp("arbitrary", "arbitrary"), name=name)(place, x, gain, *car.inputs)
    car.results = list(res[3:])
    return res[0], res[1], res[2]


def _mm_nn(a, b3, *, bm, bn, name, also=None, comm=None):
    m, k = a.shape
    c, _, n = b3.shape
    ni, nj = m // bm, n // bn

    def body(a_ref, b_ref, *o_refs):
        res = _dot(a_ref[...], b_ref[...])
        for o_ref in o_refs:
            o_ref[...] = res.astype(o_ref.dtype)

    ospec = pl.BlockSpec((bm, bn), lambda cc, j, i: (i, cc * nj + j))
    dtypes = [F32] + ([] if also is None else [also])
    out = _call(
        comm, body, grid=(c, nj, ni),
        in_specs=[pl.BlockSpec((bm, k), lambda cc, j, i: (i, 0)), pl.BlockSpec((None, k, bn), lambda cc, j, i: (cc, 0, j))],
        out_specs=[ospec] * len(dtypes), out_shape=[S((m, c * n), dt) for dt in dtypes],
        compiler_params=_cp("parallel", "parallel", "parallel"), name=name)(a, b3)
    return out[0] if also is None else out


def _mm_nt(a, b3, *, bm, bo, out_dtype, name, comm=None):
    m = a.shape[0]
    c, ko, n = b3.shape
    ni, nj = m // bm, ko // bo

    def body(a_ref, b_ref, o_ref):
        acc = _dot_nt(a_ref[:, 0:n], b_ref[0])
        for cc in range(1, c):
            acc = acc + _dot_nt(a_ref[:, cc * n:(cc + 1) * n], b_ref[cc])
        o_ref[...] = acc.astype(o_ref.dtype)

    return _call(
        comm, body, grid=(nj, ni),
        in_specs=[pl.BlockSpec((bm, c * n), lambda j, i: (i, 0)),
                  pl.BlockSpec((c, bo, n), lambda j, i: (0, j, 0))],
        out_specs=pl.BlockSpec((bm, bo), lambda j, i: (i, j)),
        out_shape=S((m, ko), out_dtype),
        compiler_params=_cp("parallel", "parallel"), name=name)(a, b3)


def _mm_tn(a, b, c, *, bm, bk, out_dtype, name, comm=None):
    m, k = b.shape[0], a.shape[1]
    n = b.shape[1] // c
    nm, nk = m // bm, k // bk

    def body(a_ref, b_ref, o_ref, acc):
        mm = pl.program_id(2)

        @pl.when(mm == 0)
        def _():
            acc[...] = jnp.zeros_like(acc)

        acc[...] += _dot_tn(a_ref[...], b_ref[...])

        @pl.when(mm == nm - 1)
        def _():
            o_ref[...] = acc[...].astype(o_ref.dtype)

    return _call(
        comm, body, grid=(c, nk, nm),
        in_specs=[pl.BlockSpec((bm, bk), lambda cc, j, mm: (mm, j)),
                  pl.BlockSpec((bm, n), lambda cc, j, mm: (mm, cc))],
        out_specs=pl.BlockSpec((None, bk, n), lambda cc, j, mm: (cc, j, 0)),
        out_shape=S((c, k, n), out_dtype),
        scratch_shapes=[pltpu.VMEM((bk, n), F32)],
        compiler_params=_cp("parallel", "parallel", "arbitrary"), name=name)(a, b)


def _swiglu_fwd(hn, wg3, wu3, *, bm, name, comm=None):
    m, k = hn.shape
    c, _, n = wg3.shape

    def body(a_ref, g_ref, u_ref, dgate_ref, dup_ref, act_ref):
        a = a_ref[...]
        gate = _dot(a, g_ref[...])
        up = _dot(a, u_ref[...])
        sg = jax.nn.sigmoid(gate)
        silu = gate * sg
        dgate_ref[...] = (up * (sg * (1.0 + gate * (1.0 - sg)))).astype(dgate_ref.dtype)
        dup_ref[...] = silu.astype(dup_ref.dtype)
        act_ref[...] = (silu * up).astype(act_ref.dtype)

    wspec = pl.BlockSpec((None, k, n), lambda cc, i: (cc, 0, 0))
    ospec = pl.BlockSpec((bm, n), lambda cc, i: (i, cc))
    return _call(
        comm, body, grid=(c, m // bm),
        in_specs=[pl.BlockSpec((bm, k), lambda cc, i: (i, 0)), wspec, wspec],
        out_specs=[ospec, ospec, ospec],
        out_shape=[S((m, c * n), _MXU), S((m, c * n), _MXU), S((m, c * n), _MXU)],
        compiler_params=_cp("parallel", "parallel"), name=name)(hn, wg3, wu3)


def _swiglu_bwd(df, wd, act_dgate, act_dup, *, bm, bo, name):
    m, k = df.shape
    ko = wd.shape[0]

    def body(a_ref, b_ref, g_ref, u_ref, dg_ref, du_ref):
        dact = _dot_nt(a_ref[...], b_ref[...])
        dg_ref[...] = (dact * g_ref[...].astype(F32)).astype(dg_ref.dtype)
        du_ref[...] = (dact * u_ref[...].astype(F32)).astype(du_ref.dtype)

    ospec = pl.BlockSpec((bm, bo), lambda j, i: (i, j))
    return pl.pallas_call(
        body, grid=(ko // bo, m // bm),
        in_specs=[pl.BlockSpec((bm, k), lambda j, i: (i, 0)), pl.BlockSpec((bo, k), lambda j, i: (j, 0)), ospec, ospec],
        out_specs=[ospec, ospec],
        out_shape=[S((m, ko), _MXU), S((m, ko), _MXU)],
        compiler_params=_cp("parallel", "parallel"), name=name)(df, wd, act_dgate, act_dup)


def _rms_fwd(x, gain, name):
    t, d = x.shape
    tm = min(t, ROW_TILE)

    def body(x_ref, g_ref, o_ref):
        xv = x_ref[...]
        o_ref[...] = ((xv * _rstd(xv)) * g_ref[...]).astype(o_ref.dtype)

    return pl.pallas_call(body, grid=(t // tm,), in_specs=[_row_spec(tm, d), _vec_spec(d)], out_specs=_row_spec(tm, d),
                          out_shape=S((t, d), _MXU), compiler_params=_cp("parallel"), name=name)(x, gain)


def _outnorm_fwd(o, yl, ga, gl, name, comm=None):
    t, w = o.shape
    tm = min(t, ROW_TILE)

    def body(o_ref, l_ref, ga_ref, gl_ref, y_ref):
        ov, lv = o_ref[...], l_ref[...]
        y_ref[:, :w] = ((ov * _rstd(ov)) * ga_ref[...]).astype(y_ref.dtype)
        y_ref[:, w:] = ((lv * _rstd(lv)) * gl_ref[...]).astype(y_ref.dtype)

    return _call(comm, body, grid=(t // tm,), in_specs=[_row_spec(tm, w), _row_spec(tm, w), _vec_spec(w), _vec_spec(w)],
                 out_specs=_row_spec(tm, 2 * w), out_shape=S((t, 2 * w), _MXU),
                 compiler_params=_cp("parallel"), name=name)(o, yl, ga, gl)


def _mid_fwd(x, mix, g_post, g_pre, name, comm=None):
    t, d = x.shape
    tm = min(t, ROW_TILE)

    def body(x_ref, m_ref, gp_ref, gn_ref, x2_ref, hn_ref):
        mv = m_ref[...]
        x2 = x_ref[...] + (mv * _rstd(mv)) * gp_ref[...]
        x2_ref[...] = x2
        hn_ref[...] = ((x2 * _rstd(x2)) * gn_ref[...]).astype(hn_ref.dtype)

    return _call(comm, body, grid=(t // tm,), in_specs=[_row_spec(tm, d), _row_spec(tm, d), _vec_spec(d), _vec_spec(d)],
                          out_specs=[_row_spec(tm, d), _row_spec(tm, d)], out_shape=[S((t, d), F32), S((t, d), _MXU)],
                          compiler_params=_cp("parallel"), name=name)(x, mix, g_post, g_pre)


def _final(f, x2, target, g_post, name):
    t, d = f.shape
    tm = min(t, ROW_TILE // 2)

    def body(f_ref, x2_ref, t_ref, g_ref, loss_ref, dout_ref, df_ref, dg_ref):
        @pl.when(pl.program_id(0) == 0)
        def _():
            loss_ref[...] = jnp.zeros_like(loss_ref)
            dg_ref[...] = jnp.zeros_like(dg_ref)

        fv = f_ref[...]
        r = _rstd(fv)
        fh = fv * r
        err = (x2_ref[...] + fh * g_ref[...]) - t_ref[...]
        loss_ref[...] += jnp.sum(err * err, axis=0, keepdims=True)
        dout = err * (1.0 / d)
        dout_ref[...] = dout
        dfv, dg = _rms_bwd(dout, fh, r, g_ref[...])
        df_ref[...] = dfv.astype(df_ref.dtype)
        dg_ref[...] += dg

    return pl.pallas_call(
        body, grid=(t // tm,),
        in_specs=[_row_spec(tm, d), _row_spec(tm, d), _row_spec(tm, d), _vec_spec(d)],
        out_specs=[_vec_spec(d), _row_spec(tm, d), _row_spec(tm, d), _vec_spec(d)],
        out_shape=[S((1, d), F32), S((t, d), F32), S((t, d), _MXU), S((1, d), F32)],
        compiler_params=_cp("arbitrary"), name=name)(f, x2, target, g_post)


def _mid_bwd(dhn_a, dhn_b, dout, x2, mix, g_pre, g_post, name, comm=None):
    t, d = x2.shape
    tm = min(t, ROW_TILE // 2)

    def body(da_ref, db_ref, do_ref, x2_ref, m_ref, gn_ref, gp_ref, dx2_ref, dm_ref, dgn_ref, dgp_ref):
        @pl.when(pl.program_id(0) == 0)
        def _():
            dgn_ref[...] = jnp.zeros_like(dgn_ref)
            dgp_ref[...] = jnp.zeros_like(dgp_ref)

        x2 = x2_ref[...]
        r = _rstd(x2)
        dxa, dgn = _rms_bwd(da_ref[...] + db_ref[...], x2 * r, r, gn_ref[...])
        dx2 = do_ref[...] + dxa
        dx2_ref[...] = dx2
        dgn_ref[...] += dgn
        mv = m_ref[...]
        rm = _rstd(mv)
        dmv, dgp = _rms_bwd(dx2, mv * rm, rm, gp_ref[...])
        dm_ref[...] = dmv.astype(dm_ref.dtype)
        dgp_ref[...] += dgp

    rs, vs = _row_spec(tm, d), _vec_spec(d)
    return _call(
        comm, body, grid=(t // tm,), in_specs=[rs, rs, rs, rs, rs, vs, vs], out_specs=[rs, rs, vs, vs],
        out_shape=[S((t, d), F32), S((t, d), _MXU), S((1, d), F32), S((1, d), F32)],
        compiler_params=_cp("arbitrary"), name=name)(dhn_a, dhn_b, dout, x2, mix, g_pre, g_post)


def _first_bwd(dhn, dx2, x, gain, name, comm=None):
    t, d = x.shape
    tm = min(t, ROW_TILE)

    def body(dh_ref, dx2_ref, x_ref, g_ref, dx_ref, dg_ref):
        @pl.when(pl.program_id(0) == 0)
        def _():
            dg_ref[...] = jnp.zeros_like(dg_ref)

        xv = x_ref[...]
        r = _rstd(xv)
        dxa, dg = _rms_bwd(dh_ref[...], xv * r, r, g_ref[...])
        dx_ref[...] = dx2_ref[...] + dxa
        dg_ref[...] += dg

    rs, vs = _row_spec(tm, d), _vec_spec(d)
    return _call(comm, body, grid=(t // tm,), in_specs=[rs, rs, rs, vs], out_specs=[rs, vs],
                          out_shape=[S((t, d), F32), S((1, d), F32)], compiler_params=_cp("arbitrary"), name=name)(dhn, dx2, x, gain)


def _outnorm_bwd(dy, o, yl, ga, gl, name, comm=None):
    t, w = o.shape
    tm = min(t, ROW_TILE)

    def body(dy_ref, o_ref, l_ref, ga_ref, gl_ref, do_ref, dl_ref, dga_ref, dgl_ref):
        @pl.when(pl.program_id(0) == 0)
        def _():
            dga_ref[...] = jnp.zeros_like(dga_ref)
            dgl_ref[...] = jnp.zeros_like(dgl_ref)

        ov, lv = o_ref[...], l_ref[...]
        ra, rl = _rstd(ov), _rstd(lv)
        dov, dga = _rms_bwd(dy_ref[:, :w], ov * ra, ra, ga_ref[...])
        dlv, dgl = _rms_bwd(dy_ref[:, w:], lv * rl, rl, gl_ref[...])
        do_ref[...] = dov.astype(do_ref.dtype)
        dl_ref[...] = dlv
        dga_ref[...] += dga
        dgl_ref[...] += dgl

    rs, vs = _row_spec(tm, w), _vec_spec(w)
    return _call(comm, body, grid=(t // tm,), in_specs=[_row_spec(tm, 2 * w), rs, rs, vs, vs], out_specs=[rs, rs, vs, vs],
                          out_shape=[S((t, w), _MXU), S((t, w), F32), S((1, w), F32), S((1, w), F32)],
                          compiler_params=_cp("arbitrary"), name=name)(dy, o, yl, ga, gl)


def _tri_sum(v, tri):
    return _dot(v.astype(_MXU), tri)


def _attn_tile(qb, kb, row, col, shift, scale):
    z = _dot_nt(qb, kb) * scale
    mask = (col + shift) < row
    lb = _log_sigmoid(z)
    lm = jnp.where(mask, lb - z, 0.0)
    return mask, lb, lm


def _attn_fwd(proj, n_heads, name, comm=None):
    t = proj.shape[0]
    bq = min(t, ATTN_BLOCK)
    nq = t // bq
    scale = 1.0 / math.sqrt(HEAD_DIM)

    heads = [slice(a * HEAD_DIM, (a + 1) * HEAD_DIM) for a in range(ATTN_HEADS)]

    def body(q_ref, k_ref, v_ref, o_ref):
        row = lax.broadcasted_iota(jnp.int32, (bq, bq), 0)
        col = lax.broadcasted_iota(jnp.int32, (bq, bq), 1)
        tri = (row > col).astype(_MXU)

        def per_q(qi, _):
            q0 = pl.multiple_of(qi * bq, bq)
            qbs = [q_ref[pl.ds(q0, bq), hd] for hd in heads]

            def cond(st):
                return jnp.logical_and(st[0] >= 0, st[1])

            def step(st):
                kj, _, carries, accs = st
                k0 = pl.multiple_of(kj * bq, bq)
                alive, new_carries, new_accs = None, [], []
                for hd, qb, carry, acc in zip(heads, qbs, carries, accs):
                    mask, lb, lm = _attn_tile(qb, k_ref[pl.ds(k0, bq), hd], row, col, (kj - qi) * bq, scale)
                    w = jnp.where(mask, jnp.exp(lb + _tri_sum(lm, tri) + carry), 0.0)
                    new_accs.append(acc + _dot(w.astype(_MXU), v_ref[pl.ds(k0, bq), hd]))
                    carry = carry + jnp.sum(lm, axis=1, keepdims=True)
                    new_carries.append(carry)
                    live = jnp.max(carry) > EXP_CUT
                    alive = live if alive is None else jnp.logical_or(alive, live)
                return kj - 1, alive, tuple(new_carries), tuple(new_accs)

            st = lax.while_loop(cond, step, (qi, jnp.bool_(True), (jnp.zeros((bq, 1), F32),) * ATTN_HEADS,
                                             (jnp.zeros((bq, HEAD_DIM), F32),) * ATTN_HEADS))
            for hd, acc in zip(heads, st[3]):
                o_ref[pl.ds(q0, bq), hd] = acc
            return 0

        lax.fori_loop(0, nq, per_q, 0)

    groups = n_heads // ATTN_HEADS
    hs = lambda off: pl.BlockSpec((t, ATTN_HEADS * HEAD_DIM), lambda h: (0, off + h))
    return _call(
        comm, body, grid=(groups,), in_specs=[hs(0), hs(groups), hs(2 * groups)], out_specs=hs(0),
        out_shape=S((t, n_heads * HEAD_DIM), F32), compiler_params=_cp("parallel"), name=name)(proj, proj, proj)


def _emit(blocks, out_ref, starts, sems):
    copies = [pltpu.make_async_copy(b, out_ref.at[:, pl.ds(c0, b.shape[1])], sems.at[k]) for k, (b, c0) in enumerate(zip(blocks, starts))]
    for cp in copies:
        cp.start()
    for cp in copies:
        cp.wait()


def _attn_bwd(proj, do, dproj, n_heads, name, comm=None):
    t = proj.shape[0]
    bq = min(t, ATTN_BLOCK)
    nq = t // bq
    scale = 1.0 / math.sqrt(HEAD_DIM)
    groups = n_heads // ATTN_HEADS
    wide = ATTN_HEADS * HEAD_DIM

    heads = [slice(a * HEAD_DIM, (a + 1) * HEAD_DIM) for a in range(ATTN_HEADS)]

    def body(q_ref, k_ref, v_ref, do_ref, _, dproj_ref, dka_ref, dva_ref, g_ref, b_ref, dq_ref, dk_ref, dv_ref, out_sems):
        group = pl.program_id(0)
        dka_ref[...] = jnp.zeros_like(dka_ref)
        dva_ref[...] = jnp.zeros_like(dva_ref)
        row = lax.broadcasted_iota(jnp.int32, (bq, bq), 0)
        col = lax.broadcasted_iota(jnp.int32, (bq, bq), 1)
        tri = (row > col).astype(_MXU)
        tri_lt = (row < col).astype(_MXU)

        def per_q(qi, _):
            q0 = pl.multiple_of(qi * bq, bq)
            qbs = [q_ref[pl.ds(q0, bq), hd] for hd in heads]
            dobs = [do_ref[pl.ds(q0, bq), hd] for hd in heads]

            def cond(st):
                return jnp.logical_and(st[0] >= 0, st[1])

            def step(st):
                kj, _, carries = st
                k0 = pl.multiple_of(kj * bq, bq)
                alive, new_carries = None, []
                for a, (hd, qb, dob, carry) in enumerate(zip(heads, qbs, dobs, carries)):
                    mask, lb, lm = _attn_tile(qb, k_ref[pl.ds(k0, bq), hd], row, col, (kj - qi) * bq, scale)
                    w = jnp.where(mask, jnp.exp(lb + _tri_sum(lm, tri) + carry), 0.0)
                    g_ref[a, pl.ds(k0, bq), :] = w * _dot_nt(dob, v_ref[pl.ds(k0, bq), hd])
                    b_ref[a, pl.ds(k0, bq), :] = jnp.where(mask, jnp.exp(lb), 0.0)
                    dva_ref[pl.ds(k0, bq), hd] += _dot_tn(w.astype(_MXU), dob)
                    carry = carry + jnp.sum(lm, axis=1, keepdims=True)
                    new_carries.append(carry)
                    live = jnp.max(carry) > EXP_CUT
                    alive = live if alive is None else jnp.logical_or(alive, live)
                return kj - 1, alive, tuple(new_carries)

            st = lax.while_loop(cond, step, (qi, jnp.bool_(True), (jnp.zeros((bq, 1), F32),) * ATTN_HEADS))

            def back(kj, st2):
                k0 = pl.multiple_of(kj * bq, bq)
                out = []
                for a, (hd, qb, (before, dq)) in enumerate(zip(heads, qbs, st2)):
                    g = g_ref[a, pl.ds(k0, bq), :]
                    beta = b_ref[a, pl.ds(k0, bq), :]
                    dz = ((g * (1.0 - beta) - (before + _tri_sum(g, tri_lt)) * beta) * scale).astype(_MXU)
                    dka_ref[pl.ds(k0, bq), hd] += _dot_tn(dz, qb)
                    out.append((before + jnp.sum(g, axis=1, keepdims=True), dq + _dot(dz, k_ref[pl.ds(k0, bq), hd])))
                return tuple(out)

            st2 = lax.fori_loop(st[0] + 1, qi + 1, back, ((jnp.zeros((bq, 1), F32), jnp.zeros((bq, HEAD_DIM), F32)),) * ATTN_HEADS)
            for hd, (_, dq) in zip(heads, st2):
                dq_ref[pl.ds(q0, bq), hd] = dq.astype(dq_ref.dtype)
            return 0

        lax.fori_loop(0, nq, per_q, 0)
        dk_ref[...] = dka_ref[...].astype(dk_ref.dtype)
        dv_ref[...] = dva_ref[...].astype(dv_ref.dtype)
        _emit([dq_ref, dk_ref, dv_ref], dproj_ref, [(a * groups + group) * wide for a in range(3)], out_sems)

    hs = lambda off: pl.BlockSpec((t, wide), lambda h: (0, off + h))
    return _call(
        comm, body, grid=(groups,), in_specs=[hs(0), hs(groups), hs(2 * groups), hs(0), _ANY], out_specs=_ANY,
        out_shape=S(dproj.shape, dproj.dtype), input_output_aliases={4: 0},
        scratch_shapes=[pltpu.VMEM((t, wide), F32), pltpu.VMEM((t, wide), F32),
                        pltpu.VMEM((ATTN_HEADS, t, bq), F32), pltpu.VMEM((ATTN_HEADS, t, bq), F32)]
        + [pltpu.VMEM((t, wide), dproj.dtype)] * 3 + [pltpu.SemaphoreType.DMA((3,))],
        compiler_params=_cp("parallel"), name=name)(proj, proj, proj, do, dproj)


def _shift_down(cur, prev8, k):
    if k == 0:
        return cur
    row8 = lax.broadcasted_iota(jnp.int32, prev8.shape, 0)
    rc = pltpu.roll(cur, k, 0)
    top = jnp.where(row8 < k, pltpu.roll(prev8, k, 0), rc[0:8, :])
    return jnp.concatenate([top, rc[8:, :]], axis=0)


def _shift_up(cur, next8, k):
    if k == 0:
        return cur
    n = cur.shape[0]
    row8 = lax.broadcasted_iota(jnp.int32, next8.shape, 0)
    rc = pltpu.roll(cur, n - k, 0)
    bottom = jnp.where(row8 >= 8 - k, pltpu.roll(next8, 8 - k, 0), rc[n - 8:, :])
    return jnp.concatenate([rc[:n - 8, :], bottom], axis=0)


def _lru_conv(xl, prev8, cw, cb):
    xs = [_shift_down(xl, prev8, CONV_WIDTH - 1 - k) for k in range(CONV_WIDTH)]
    xc = xs[0] * cw[0:1, :]
    for k in range(1, CONV_WIDTH):
        xc = xc + xs[k] * cw[k:k + 1, :]
    return xs, xc + cb


def _lru_gates(xl, prev8, cw, cb, wr, br, wi, bi, ls):
    xs, xc = _lru_conv(xl, prev8, cw, cb)
    xcb = xc.astype(_MXU)
    r = jax.nn.sigmoid(_dot(xcb, wr) + br)
    i = jax.nn.sigmoid(_dot(xcb, wi) + bi)
    la = (LRU_C * r) * ls
    a = jnp.exp(la)
    mult = jnp.sqrt(-_expm1(2.0 * la))
    return xs, xc, r, i, a, mult


def _group_scan(a, b, reverse):
    n = a.shape[0]
    row = lax.broadcasted_iota(jnp.int32, a.shape, 0) % 8
    for d in (1, 2, 4):
        if reverse:
            m = row < 8 - d
            a_s, b_s = pltpu.roll(a, n - d, 0), pltpu.roll(b, n - d, 0)
        else:
            m = row >= d
            a_s, b_s = pltpu.roll(a, d, 0), pltpu.roll(b, d, 0)
        b = jnp.where(m, a * b_s + b, b)
        a = jnp.where(m, a * a_s, a)
    return a, b


def _lru_fwd(proj, col0, n_blocks, cw, cb, wr, br, wi, bi, lam, name, comm=None):
    t = proj.shape[0]
    tt = min(t, SEQ_TILE)
    nt = t // tt

    def body(xl_ref, gl_ref, cw_ref, cb_ref, wr_ref, br_ref, wi_ref, bi_ref, lam_ref, h_ref, y_ref, *kept):
        cwv, cbv, brv, biv = cw_ref[...], cb_ref[...], br_ref[...], bi_ref[...]
        wrv, wiv = wr_ref[...].astype(_MXU), wi_ref[...].astype(_MXU)
        ls = _log_sigmoid(lam_ref[...])

        def tile(ti, hin):
            t0 = pl.multiple_of(ti * tt, tt)
            p0 = pl.multiple_of(jnp.maximum(t0 - 8, 0), 8)
            prev8 = xl_ref[pl.ds(p0, 8), :] * (ti > 0).astype(F32)
            xl = xl_ref[pl.ds(t0, tt), :]
            _, xc, r, ig, a, mult = _lru_gates(xl, prev8, cwv, cbv, wrv, brv, wiv, biv, ls)
            for ref, val in zip(kept, (r, ig, a, mult)):
                ref[pl.ds(t0, tt), :] = val
            ga, gb = _group_scan(a, mult * (ig * xc), False)
            for g in range(tt // 8):
                hg = ga[8 * g:8 * g + 8, :] * hin + gb[8 * g:8 * g + 8, :]
                h_ref[pl.ds(t0 + 8 * g, 8), :] = hg
                hin = hg[7:8, :]
            y_ref[pl.ds(t0, tt), :] = h_ref[pl.ds(t0, tt), :] * _gelu(gl_ref[pl.ds(t0, tt), :])
            return hin

        lax.fori_loop(0, nt, tile, jnp.zeros((1, HEAD_DIM), F32))

    cs = lambda off: pl.BlockSpec((t, HEAD_DIM), lambda n: (0, off + n))
    vs = pl.BlockSpec((1, HEAD_DIM), lambda n: (0, n))
    ws = pl.BlockSpec((None, HEAD_DIM, HEAD_DIM), lambda n: (n, 0, 0))
    w = n_blocks * HEAD_DIM
    return _call(
        comm, body, grid=(n_blocks,),
        in_specs=[cs(col0), cs(col0 + n_blocks), pl.BlockSpec((CONV_WIDTH, HEAD_DIM), lambda n: (0, n)), vs, ws, vs, ws, vs, vs],
        out_specs=[cs(0)] * 6, out_shape=[S((t, w), F32)] * 6,
        compiler_params=_cp("parallel"), name=name)(proj, proj, cw, cb, wr, br, wi, bi, lam)


def _lru_bwd(proj, col0, n_blocks, h, kept, dyl, cw, cb, wr, wi, lam, name, comm=None):
    t = proj.shape[0]
    tt = min(t, SEQ_TILE)
    nt = t // tt

    def body(xl_ref, gl_ref, h_ref, r_ref, i_ref, a_ref, m_ref, dy_ref, cw_ref, cb_ref, wr_ref, wi_ref, lam_ref,
             dproj_ref, dcw_ref, dcb_ref, dwr_ref, dbr_ref, dwi_ref, dbi_ref, dlam_ref, g_ref, dxl_ref, dgl_ref, out_sems):
        block = pl.program_id(0)
        cwv, cbv = cw_ref[...], cb_ref[...]
        wrv, wiv = wr_ref[...].astype(_MXU), wi_ref[...].astype(_MXU)
        lamv = lam_ref[...]
        ls = _log_sigmoid(lamv)
        for ref in (dcw_ref, dcb_ref, dwr_ref, dbr_ref, dwi_ref, dbi_ref, dlam_ref):
            ref[...] = jnp.zeros_like(ref)

        def tile(s, carry):
            e_in, dxc_next8 = carry
            ti = nt - 1 - s
            t0 = pl.multiple_of(ti * tt, tt)
            p0 = pl.multiple_of(jnp.maximum(t0 - 8, 0), 8)
            first = (ti > 0).astype(F32)
            xl = xl_ref[pl.ds(t0, tt), :]
            xs, xc = _lru_conv(xl, xl_ref[pl.ds(p0, 8), :] * first, cwv, cbv)
            r, ig, a, mult = (ref[pl.ds(t0, tt), :] for ref in (r_ref, i_ref, a_ref, m_ref))
            hv = h_ref[pl.ds(t0, tt), :]
            h_before = _shift_down(hv, h_ref[pl.ds(p0, 8), :] * first, 1)
            glv = gl_ref[pl.ds(t0, tt), :]
            dyv = dy_ref[pl.ds(t0, tt), :]
            dgl_ref[pl.ds(t0, tt), :] = (dyv * hv * _gelu_grad(glv)).astype(dgl_ref.dtype)
            dh = dyv * _gelu(glv)
            row = lax.broadcasted_iota(jnp.int32, a.shape, 0)
            coef = jnp.where(row == tt - 1, 1.0, pltpu.roll(a, tt - 1, 0))
            ga, gb = _group_scan(coef, dh, True)
            gin = e_in
            for g in reversed(range(tt // 8)):
                gg = ga[8 * g:8 * g + 8, :] * gin + gb[8 * g:8 * g + 8, :]
                g_ref[8 * g:8 * g + 8, :] = gg
                gin = gg[0:1, :]
            gv = g_ref[...]
            e_out = a[0:1, :] * gv[0:1, :]
            ix = ig * xc
            dla = (gv * h_before) * a - (gv * ix) * (a * a / mult)
            dlam_ref[...] += jnp.sum(dla * (LRU_C * r), axis=0, keepdims=True)
            dpr = (dla * (LRU_C * ls)) * (r * (1.0 - r))
            dpi = (gv * mult * xc) * (ig * (1.0 - ig))
            dbr_ref[...] += jnp.sum(dpr, axis=0, keepdims=True)
            dbi_ref[...] += jnp.sum(dpi, axis=0, keepdims=True)
            xcb, dprb, dpib = xc.astype(_MXU), dpr.astype(_MXU), dpi.astype(_MXU)
            dwr_ref[...] += _dot_tn(xcb, dprb)
            dwi_ref[...] += _dot_tn(xcb, dpib)
            dxc = gv * mult * ig + _dot_nt(dprb, wrv) + _dot_nt(dpib, wiv)
            dcb_ref[...] += jnp.sum(dxc, axis=0, keepdims=True)
            dxl = None
            for k in range(CONV_WIDTH):
                dcw_ref[k:k + 1, :] += jnp.sum(dxc * xs[k], axis=0, keepdims=True)
                term = _shift_up(dxc, dxc_next8, CONV_WIDTH - 1 - k) * cwv[k:k + 1, :]
                dxl = term if dxl is None else dxl + term
            dxl_ref[pl.ds(t0, tt), :] = dxl.astype(dxl_ref.dtype)
            return e_out, dxc[0:8, :]

        lax.fori_loop(0, nt, tile, (jnp.zeros((1, HEAD_DIM), F32), jnp.zeros((8, HEAD_DIM), F32)))
        dlam_ref[...] = dlam_ref[...] * (1.0 - jax.nn.sigmoid(lamv))
        _emit([dxl_ref, dgl_ref], dproj_ref, [(col0 + block) * HEAD_DIM, (col0 + n_blocks + block) * HEAD_DIM], out_sems)

    cs = lambda off: pl.BlockSpec((t, HEAD_DIM), lambda n: (0, off + n))
    vs = pl.BlockSpec((1, HEAD_DIM), lambda n: (0, n))
    ws = pl.BlockSpec((None, HEAD_DIM, HEAD_DIM), lambda n: (n, 0, 0))
    cws = pl.BlockSpec((CONV_WIDTH, HEAD_DIM), lambda n: (0, n))
    w = n_blocks * HEAD_DIM
    vec = S((1, w), F32)
    mat = S((n_blocks, HEAD_DIM, HEAD_DIM), F32)
    return _call(
        comm, body, grid=(n_blocks,),
        in_specs=[cs(col0), cs(col0 + n_blocks)] + [cs(0)] * 6 + [cws, vs, ws, ws, vs],
        out_specs=[_ANY, cws, vs, ws, vs, ws, vs, vs],
        out_shape=[S(proj.shape, _MXU), S((CONV_WIDTH, w), F32), vec, mat, vec, mat, vec, vec],
        scratch_shapes=[pltpu.VMEM((tt, HEAD_DIM), F32), pltpu.VMEM((t, HEAD_DIM), _MXU), pltpu.VMEM((t, HEAD_DIM), _MXU),
                        pltpu.SemaphoreType.DMA((2,))],
        compiler_params=_cp("parallel"), name=name)(proj, proj, h, *kept, dyl, cw, cb, wr, wi, lam)


class _NoExchange:
    grad_dtype = F32

    def __init__(self, weights):
        self.weights, self.grads, self.packs = weights, {}, {}

    def weight(self, name):
        return self.weights[name]

    def in_proj(self, x, gain, bm):
        hn = _rms_fwd(x, gain, "rms1")
        return [hn, *_mm_nn(hn, self.weights["w_in"], bm=bm, bn=self.weights["w_in"].shape[2], name="in_proj", also=_MXU)]

    def conv_w(self):
        return self.weights["conv_w"]

    def carrier(self, call):
        return None

    def harvest(self, car):
        pass

    def alone(self, call):
        pass


def _local_step(x, target, norms, ex, cb, wr, br, wi, bi, lam, ga, gl):
    g_pre_mix, g_post_mix, g_pre_ffn, g_post_ffn = norms
    t, d = x.shape
    bm = min(t, MM_ROWS)
    bt = min(t, DW_TOKENS)

    def run(fn, name, *args, **kw):
        car = ex.carrier(name)
        out = fn(*args, name=name, comm=car, **kw)
        ex.harvest(car)
        return out

    hn1, proj, proj_mx = ex.in_proj(x, g_pre_mix, bm)
    win3, cw = ex.weight("w_in"), ex.conv_w()
    c = win3.shape[0]
    o = run(_attn_fwd, "attn_fwd", proj_mx, (proj.shape[1] - d) // 3 // HEAD_DIM)
    mix = 2 * o.shape[1]
    n_heads = n_blocks = o.shape[1] // HEAD_DIM
    h, yl, *kept = run(_lru_fwd, "lru_fwd", proj, 3 * n_heads, n_blocks, cw, cb, wr, br, wi, bi, lam)
    y = run(_outnorm_fwd, "outnorm_fwd", o, yl, ga, gl)
    wout = ex.weight("w_out")
    mixo = run(_mm_nn, "out_proj", y, wout[None], bm=bm, bn=d)
    x2, hn2 = run(_mid_fwd, "mid_fwd", x, mixo, g_post_mix, g_pre_ffn)
    ex.alone("gather_w_up_last")
    wg3, wu3 = ex.weight("w_ffn_gate"), ex.weight("w_ffn_up")
    act_dgate, act_dup, act = run(_swiglu_fwd, "ffn_gate_up", hn2, wg3, wu3, bm=bm)
    ex.alone("gather_w_down")
    wd = ex.weight("w_ffn_down")
    ff = wd.shape[0]
    f = _mm_nn(act, wd[None], bm=bm, bn=d // 2, name="ffn_down")
    loss_cols, dout, df, dg_post_ffn = _final(f, x2, target, g_post_ffn, "final")

    dgate, dup = _swiglu_bwd(df, wd, act_dgate, act_dup, bm=min(t, 2 * MM_ROWS), bo=ff // 4, name="ffn_down_bwd")
    ex.grads["w_ffn_down"] = _mm_tn(act, df, 1, bm=bt, bk=DW_ROWS, out_dtype=ex.grad_dtype, name="ffn_down_dw").reshape(c, ff // c, d)
    ex.grads["w_ffn_gate"] = run(_mm_tn, "ffn_gate_dw", hn2, dgate, c, bm=bt, bk=d // 2, out_dtype=ex.grad_dtype)
    ex.grads["w_ffn_up"] = run(_mm_tn, "ffn_up_dw", hn2, dup, c, bm=bt, bk=d // 2, out_dtype=ex.grad_dtype)
    dhn2_g = run(_mm_nt, "ffn_gate_dx", dgate, wg3, bm=bm, bo=d // 2, out_dtype=F32)
    dhn2_u = run(_mm_nt, "ffn_up_dx", dup, wu3, bm=bm, bo=d // 2, out_dtype=F32)
    dx2, dmix, dg_pre_ffn, dg_post_mix = run(_mid_bwd, "mid_bwd", dhn2_g, dhn2_u, dout, x2, mixo, g_pre_ffn, g_post_mix)
    dy = run(_mm_nt, "out_proj_dx", dmix, wout[None], bm=bm, bo=mix, out_dtype=F32)
    ex.grads["w_out"] = _mm_tn(y, dmix, 1, bm=bt, bk=mix // 4, out_dtype=ex.grad_dtype, name="out_proj_dw").reshape(c, mix // c, d)
    do, dyl, dga, dgl_norm = run(_outnorm_bwd, "outnorm_bwd", dy, o, yl, ga, gl)
    dproj, dcw, dcb, dwr, dbr, dwi, dbi, dlam = run(_lru_bwd, "lru_bwd", proj, 3 * n_heads, n_blocks, h, kept, dyl, cw, cb, wr, wi, lam)
    small = dict(post_mix_norm=dg_post_mix, pre_ffn_norm=dg_pre_ffn, post_ffn_norm=dg_post_ffn, conv_w=dcw, conv_b=dcb,
                 w_rgate=dwr, b_rgate=dbr, w_igate=dwi, b_igate=dbi, lru_lambda=dlam, attn_out_norm=dga, lru_out_norm=dgl_norm)
    ex.packs["early"] = _pack([small[n] for n in _SMALL_EARLY])
    dproj = run(_attn_bwd, "attn_bwd", proj_mx, do, dproj, n_heads)
    ex.grads["w_in"] = _mm_tn(hn1, dproj, c, bm=bt, bk=d // 2, out_dtype=ex.grad_dtype, name="in_proj_dw")
    ex.alone("grads_w_in_swap")
    dhn1 = run(_mm_nt, "in_proj_dx", dproj, win3, bm=bm, bo=d // 2, out_dtype=F32)
    grad_x, small["pre_mix_norm"] = run(_first_bwd, "first_bwd", dhn1, dx2, x, g_pre_mix)
    ex.packs["late"] = _pack([small["pre_mix_norm"], (0.5 / d) * jnp.sum(loss_cols, keepdims=True)])
    return loss_cols, grad_x, small


def _into_slot(wsh, slot, dtype, name):
    rows, n = wsh.shape
    rb = _row_block(rows, 512) if rows % 8 == 0 else rows

    def body(s_ref, w_ref, o_ref):
        o_ref[...] = w_ref[...].astype(o_ref.dtype)

    return pl.pallas_call(
        body,
        grid_spec=pltpu.PrefetchScalarGridSpec(
            num_scalar_prefetch=1, grid=(rows // rb,),
            in_specs=[pl.BlockSpec((rb, n), lambda i, s_ref: (i, 0))],
            out_specs=pl.BlockSpec((None, rb, n), lambda i, s_ref: (s_ref[0], i, 0))),
        out_shape=S((4, rows, n), dtype), compiler_params=_cp("parallel"), name=name)(slot, wsh)


class _Exchange:
    SCHEDULE = {
        "in_proj": [("stream", "w_in"), ("ici", "conv_w"), ("ici", "w_ffn_up", 0)],
        "attn_fwd": [("d2d", "w_ffn_up", 0), ("ici", "w_ffn_gate")],
        "lru_fwd": [("d2d", "w_ffn_gate"), ("ici", "w_out"), ("ici", "w_ffn_up", 1)],
        "outnorm_fwd": [("d2d", "w_out"), ("d2d", "w_ffn_up", 1)],
        "out_proj": [("ici", "w_ffn_up", 2)],
        "mid_fwd": [("d2d", "w_ffn_up", 2), ("ici", "w_ffn_up", 3)],
        "gather_w_up_last": [("d2d", "w_ffn_up", 3)],
        "ffn_gate_up": [("ici", "w_ffn_down")],
        "gather_w_down": [("d2d", "w_ffn_down")],
        "ffn_gate_dw": [("swap", "w_ffn_down")],
        "ffn_up_dw": [("scatter", "w_ffn_down", 0), ("scatter", "w_ffn_down", 1), ("scatter", "w_ffn_down", 2), ("swap", "w_ffn_gate")],
        "ffn_gate_dx": [("scatter", "w_ffn_down", 3), ("scatter", "w_ffn_gate", 0), ("swap", "w_ffn_up")],
        "ffn_up_dx": [("share", "w_ffn_down"), ("scatter", "w_ffn_gate", 1), ("scatter", "w_ffn_gate", 2)],
        "mid_bwd": [("scatter", "w_ffn_gate", 3), ("scatter", "w_ffn_up", 0)],
        "out_proj_dx": [("share", "w_ffn_gate"), ("scatter", "w_ffn_up", 1)],
        "outnorm_bwd": [("scatter", "w_ffn_up", 2), ("swap", "w_out")],
        "lru_bwd": [("scatter", "w_ffn_up", 3), ("scatter", "w_out")],
        "attn_bwd": [("share", "w_ffn_up"), ("share", "w_out"), ("spread", "early")],
        "grads_w_in_swap": [("swap", "w_in")],
        "in_proj_dx": [("scatter", "w_in")],
        "grads_w_in_share": [("share", "w_in"), ("spread", "late")],
    }
    PIECES = 4
    grad_dtype = BF16

    def __init__(self, slots, place):
        self.buf, self.place = dict(slots), place
        self.grads, self.packs, self.swapped, self.part, self.scattered, self.full, self.spreaded = {}, {}, {}, {}, {}, {}, {}

    def weight(self, name):
        b = self.buf[name]
        return b.reshape(-1, b.shape[2]) if name in ("w_out", "w_ffn_down") else b

    def in_proj(self, x, gain, bm):
        car = self.carrier("in_proj")
        out = _in_proj_streamed(x, gain, car, car.streamed, self.place, bm=bm, name="in_proj")
        self.harvest(car)
        return out

    def conv_w(self):
        return jnp.transpose(self.buf["conv_w"], (1, 0, 2)).reshape(CONV_WIDTH, -1)

    def carrier(self, call):
        if call not in self.SCHEDULE:
            return None
        car = _Carrier()
        car.todo, slot = [], {}
        for kind, name, *piece in self.SCHEDULE[call]:
            if kind in ("ici", "d2d", "stream"):
                if name not in slot:
                    slot[name] = car.inplace(self.buf[name])
                    car.todo.append((self.buf, name, slot[name]))
            if kind == "stream":
                car.streamed = slot[name]
            elif kind in ("ici", "d2d"):
                size = self.buf[name].shape[1] // 2 // self.PIECES
                rows = (piece[0] * size, size) if piece else None
                if kind == "ici":
                    car.gather_ici(slot[name], rows, split=name != "conv_w")
                else:
                    car.gather_d2d(slot[name], rows)
            elif kind == "swap":
                g = self.grads[name]
                o = car.fresh((4, g.shape[1] // 2, g.shape[2]), g.dtype)
                car.swap(car.read(g), o)
                car.todo.append((self.swapped, name, o))
            elif kind == "scatter":
                if name not in self.part:
                    self.part[name] = _add_own_half(self.grads[name], self.swapped[name], self.place[1:], "grads_add_" + name)
                p = self.part[name]
                key = ("scatter", name)
                if key not in slot:
                    slot[key] = (car.read(p), car.inplace(self.scattered[name]) if name in self.scattered else car.fresh(p.shape, p.dtype))
                    car.todo.append((self.scattered, name, slot[key][1]))
                size = p.shape[1] // self.PIECES
                car.scatter(*slot[key], (piece[0] * size, size) if piece else None)
            elif kind == "share":
                o = car.inplace(_sum_chips(self.part[name], self.scattered[name], self.place, "grads_sum_" + name))
                car.share(o)
                car.todo.append((self.full, name, o))
            else:
                o = car.fresh((8,) + self.packs[name].shape, F32)
                car.spread(car.read(self.packs[name]), o)
                car.todo.append((self.spreaded, name, o))
        return car

    def harvest(self, car):
        for state, name, o in (car.todo if car is not None else []):
            state[name] = car.results[o]

    def alone(self, call):
        car = self.carrier(call)
        car.run_alone(call)
        self.harvest(car)

    def small_sum(self, key):
        return _sum_devices(self.packs[key], self.spreaded[key], 2 * self.place[0:1] + self.place[1:], "grads_small_sum_" + key)


def _row_block(rows, cap):
    return max(b for b in range(8, cap + 1, 8) if rows % b == 0)


def _add_own_half(g, recv, core, name):
    _, rows, n = g.shape
    half = rows // 2
    rb = _row_block(half, 1024)
    nb = half // rb

    def body(c_ref, g_ref, r_ref, o_ref):
        o_ref[...] = (g_ref[...].astype(F32) + r_ref[...].astype(F32)).astype(o_ref.dtype)

    return pl.pallas_call(
        body,
        grid_spec=pltpu.PrefetchScalarGridSpec(
            num_scalar_prefetch=1, grid=(4, nb),
            in_specs=[pl.BlockSpec((None, rb, n), lambda k, i, c_ref: (k, c_ref[0] * nb + i, 0)),
                      pl.BlockSpec((None, rb, n), lambda k, i, c_ref: (k, i, 0))],
            out_specs=pl.BlockSpec((None, rb, n), lambda k, i, c_ref: (k, i, 0))),
        out_shape=S((4, half, n), BF16), compiler_params=_cp("parallel", "parallel"), name=name)(core, g, recv)


def _sum_chips(part, recv, place, name):
    _, rows, n = part.shape
    rb = _row_block(rows, 256)
    nb = rows // rb

    def body(p_ref, own_ref, r0, r1, r2, r3, o_ref):
        own = own_ref[...].astype(F32)
        terms = [jnp.where(p_ref[0] == k, own, r[...].astype(F32)) for k, r in enumerate((r0, r1, r2, r3))]
        o_ref[...] = ((terms[0] + terms[1]) + terms[2]) + terms[3]

    def slot(k):
        return pl.BlockSpec((None, rb, n), lambda i, p_ref: (jnp.where(p_ref[0] == k, (k + 1) % 4, k), i, 0))

    return pl.pallas_call(
        body,
        grid_spec=pltpu.PrefetchScalarGridSpec(
            num_scalar_prefetch=1, grid=(nb,),
            in_specs=[pl.BlockSpec((None, rb, n), lambda i, p_ref: (p_ref[0], i, 0))] + [slot(k) for k in range(4)],
            out_specs=pl.BlockSpec((rb, n), lambda i, p_ref: (p_ref[1] * nb + i, 0))),
        out_shape=S((2 * rows, n), F32), compiler_params=_cp("parallel"), name=name)(place, part, recv, recv, recv, recv)


def _sum_devices(own, spread, me, name):
    rows = own.shape[0]

    def body(me_ref, own_ref, *refs):
        acc = None
        for k, r in enumerate(refs[:8]):
            term = jnp.where(me_ref[0] == k, own_ref[...], r[...])
            acc = term if acc is None else acc + term
        refs[8][...] = acc

    def slot(k):
        return pl.BlockSpec((None, rows, 128), lambda i, me_ref: (jnp.where(me_ref[0] == k, (k + 1) % 8, k), 0, 0))

    whole = pl.BlockSpec((rows, 128), lambda i, me_ref: (0, 0))
    return pl.pallas_call(
        body,
        grid_spec=pltpu.PrefetchScalarGridSpec(num_scalar_prefetch=1, grid=(1,), in_specs=[whole] + [slot(k) for k in range(8)],
                                               out_specs=whole),
        out_shape=S((rows, 128), F32), compiler_params=_cp("arbitrary"), name=name)(me, own, *[spread] * 8)


def _adamw(w, g, m, v, name, regive=False):
    rows, n = w.shape
    rb = rows if rows * n * 4 <= (1 << 21) else _row_block(rows, 512)
    c1 = 1.0 - ADAM_B1 ** ADAM_STEP
    c2 = 1.0 - ADAM_B2 ** ADAM_STEP

    def body(w_ref, g_ref, m_ref, v_ref, d_ref, nm_ref, nv_ref, *again):
        gv = g_ref[...]
        for ref in again:
            ref[...] = gv
        nm = ADAM_B1 * m_ref[...] + (1.0 - ADAM_B1) * gv
        nv = ADAM_B2 * v_ref[...] + (1.0 - ADAM_B2) * (gv * gv)
        nm_ref[...] = nm
        nv_ref[...] = nv
        d_ref[...] = -ADAM_LR * ((nm / c1) / (jnp.sqrt(nv / c2) + ADAM_EPS) + ADAM_WD * w_ref[...])

    bs = pl.BlockSpec((rb, n), lambda i: (i, 0))
    n_out = 4 if regive else 3
    return pl.pallas_call(body, grid=(rows // rb,), in_specs=[bs] * 4, out_specs=[bs] * n_out, out_shape=[S((rows, n), F32)] * n_out,
                          compiler_params=_cp("parallel"), name=name)(w, g, m, v)


_BIG = ("w_in", "w_out", "w_ffn_gate", "w_ffn_up", "w_ffn_down")
_SMALL = ("pre_mix_norm", "post_mix_norm", "pre_ffn_norm", "post_ffn_norm", "conv_w", "conv_b", "w_rgate", "b_rgate",
          "w_igate", "b_igate", "lru_lambda", "attn_out_norm", "lru_out_norm")
_SMALL_EARLY = _SMALL[1:]
_WEIGHTS = ("pre_mix_norm", "post_mix_norm", "pre_ffn_norm", "post_ffn_norm", "w_in", "conv_w", "conv_b", "w_rgate", "b_rgate",
            "w_igate", "b_igate", "lru_lambda", "attn_out_norm", "lru_out_norm", "w_out", "w_ffn_gate", "w_ffn_up", "w_ffn_down")


def _pack(arrays):
    flat = []
    for a in arrays:
        f = a.reshape(-1)
        flat.append(jnp.pad(f, (0, (-f.shape[0]) % 1024)))
    return jnp.concatenate(flat).reshape(-1, 128)


def _unpack(packed, shapes):
    out, pos = [], 0
    flat = packed.reshape(-1)
    for s in shapes:
        size = math.prod(s)
        out.append(flat[pos:pos + size].reshape(s))
        pos += size + (-size) % 1024
    return out


def kernel(x, pre_mix_norm, post_mix_norm, pre_ffn_norm, post_ffn_norm, w_in, conv_w, conv_b, w_rgate, b_rgate, w_igate, b_igate, lru_lambda, attn_out_norm, lru_out_norm, w_out, w_ffn_gate, w_ffn_up, w_ffn_down, loss_target, m_pre_mix_norm, m_post_mix_norm, m_pre_ffn_norm, m_post_ffn_norm, m_w_in, m_conv_w, m_conv_b, m_w_rgate, m_b_rgate, m_w_igate, m_b_igate, m_lru_lambda, m_attn_out_norm, m_lru_out_norm, m_w_out, m_w_ffn_gate, m_w_ffn_up, m_w_ffn_down, v_pre_mix_norm, v_post_mix_norm, v_pre_ffn_norm, v_post_ffn_norm, v_w_in, v_conv_w, v_conv_b, v_w_rgate, v_b_rgate, v_w_igate, v_b_igate, v_lru_lambda, v_attn_out_norm, v_lru_out_norm, v_w_out, v_w_ffn_gate, v_w_ffn_up, v_w_ffn_down):
    given = dict(locals())
    w = {n: given[n][0] for n in _WEIGHTS}
    m = {n: given["m_" + n][0] for n in _WEIGHTS}
    v = {n: given["v_" + n][0] for n in _WEIGHTS}
    xs, target = x[0], loss_target[0]
    d = xs.shape[1]
    chip = (2 * lax.axis_index("x") + lax.axis_index("y")).astype(jnp.int32)
    place = jnp.stack([chip, lax.axis_index("c").astype(jnp.int32)])

    slots = {n: _into_slot(w[n], place[0:1], _MXU, "slot_" + n) for n in _BIG}
    slots["conv_w"] = _into_slot(w["conv_w"], place[0:1], F32, "slot_conv_w")
    ex = _Exchange(slots, place)
    row = lambda a: a.reshape(1, -1)
    norms = tuple(row(w[n]) for n in ("pre_mix_norm", "post_mix_norm", "pre_ffn_norm", "post_ffn_norm"))

    loss_cols, grad_x, small = _local_step(
        xs, target, norms, ex, row(w["conv_b"]), w["w_rgate"], row(w["b_rgate"]),
        w["w_igate"], row(w["b_igate"]), row(w["lru_lambda"]), row(w["attn_out_norm"]), row(w["lru_out_norm"]))


    ex.alone("grads_w_in_share")
    reduced = {n: ex.full[n] for n in _BIG}
    early = _unpack(ex.small_sum("early"), [small[n].shape for n in _SMALL_EARLY])
    late = _unpack(ex.small_sum("late"), [small["pre_mix_norm"].shape, (1, 1)])
    loss = late[1][0, 0]
    for n, g in zip(_SMALL_EARLY + ("pre_mix_norm",), early + late[:1]):
        reduced[n] = g.reshape(w[n].shape) if n != "conv_w" else lax.dynamic_slice_in_dim(g, chip * w[n].shape[1], w[n].shape[1], axis=1)

    delta, new_m, new_v = {}, {}, {}
    for n in _BIG:
        delta[n], new_m[n], new_v[n], reduced[n] = _adamw(w[n], reduced[n], m[n], v[n], "adamw_" + n, regive=True)
    shapes = [w[n].shape for n in _SMALL]
    packed = _adamw(*[_pack([src[n] for n in _SMALL]) for src in (w, reduced, m, v)], "adamw_small")
    for out, p in zip((delta, new_m, new_v), packed):
        out.update(zip(_SMALL, _unpack(p, shapes)))

    lead = lambda a: a[None]
    return (loss, lead(grad_x), *[lead(reduced[n]) for n in _WEIGHTS], *[lead(delta[n]) for n in _WEIGHTS],
            *[lead(new_m[n]) for n in _WEIGHTS], *[lead(new_v[n]) for n in _WEIGHTS])
```

```python
import functools
import math

import jax
import jax.numpy as jnp
from jax import lax
from jax.experimental import pallas as pl
from jax.experimental.pallas import tpu as pltpu

F32 = jnp.float32
BF16 = jnp.bfloat16
_MXU = BF16
S = jax.ShapeDtypeStruct

RMS_EPS = 1e-6
HEAD_DIM = 128
CONV_WIDTH = 4
LRU_C = 8.0
ADAM_LR, ADAM_B1, ADAM_B2, ADAM_EPS, ADAM_WD, ADAM_STEP = 0.001, 0.9, 0.999, 1e-08, 0.01, 10
EXP_CUT = -105.0
VMEM_LIMIT = 60 * 1024 * 1024
ROW_TILE = 512
SEQ_TILE = 256
ATTN_BLOCK = 256
ATTN_HEADS = 2
MM_ROWS = 512
DW_TOKENS = 4096
DW_ROWS = 512
MESH = pl.DeviceIdType.MESH


def _cp(*sem):
    return pltpu.CompilerParams(dimension_semantics=sem, vmem_limit_bytes=VMEM_LIMIT)


def _dot(a, b):
    return jnp.dot(a, b, preferred_element_type=F32)


def _dot_nt(a, b):
    return lax.dot_general(a, b, (((1,), (1,)), ((), ())), preferred_element_type=F32)


def _dot_tn(a, b):
    return lax.dot_general(a, b, (((0,), (0,)), ((), ())), preferred_element_type=F32)


def _rstd(v):
    return lax.rsqrt(jnp.mean(v * v, axis=-1, keepdims=True) + RMS_EPS)


def _rms_bwd(dn, vh, r, gain):
    dvh = dn * gain
    dv = r * (dvh - vh * jnp.mean(dvh * vh, axis=-1, keepdims=True))
    return dv, jnp.sum(dn * vh, axis=0, keepdims=True)


def _log_sigmoid(z):
    return jnp.minimum(z, 0.0) - jnp.log(1.0 + jnp.exp(-jnp.abs(z)))


def _expm1(v):
    small = v * (1.0 + v * (0.5 + v * (1.0 / 6.0 + v * (1.0 / 24.0 + v * (1.0 / 120.0)))))
    return jnp.where(jnp.abs(v) < 0.04, small, jnp.exp(v) - 1.0)


_GELU_C = math.sqrt(2.0 / math.pi)


def _gelu(v):
    return 0.5 * v * (1.0 + jnp.tanh(_GELU_C * (v + 0.044715 * v * v * v)))


def _gelu_grad(v):
    th = jnp.tanh(_GELU_C * (v + 0.044715 * v * v * v))
    return 0.5 * (1.0 + th) + 0.5 * v * (1.0 - th * th) * _GELU_C * (1.0 + 3.0 * 0.044715 * v * v)


def _row_spec(tm, d):
    return pl.BlockSpec((tm, d), lambda i: (i, 0))


def _vec_spec(d):
    return pl.BlockSpec((1, d), lambda i: (0, 0))


_ANY = pl.BlockSpec(memory_space=pl.ANY)


def _place():
    x, y, c = lax.axis_index("x"), lax.axis_index("y"), lax.axis_index("c")
    return x, y, c, [(1 - x, y), (x, 1 - y), (1 - x, 1 - y)]


def _remote(src, dst, send_sem, recv_sem, to):
    return pltpu.make_async_remote_copy(src_ref=src, dst_ref=dst, send_sem=send_sem, recv_sem=recv_sem,
                                        device_id=to, device_id_type=MESH)


class _Carrier:
    def __init__(self):
        self.inputs, self.out_shapes, self.aliases, self.ops, self.n_sems, self.results = [], [], {}, [], 0, None

    def inplace(self, arr):
        self.aliases[len(self.inputs)] = len(self.out_shapes)
        self.inputs.append(arr)
        self.out_shapes.append(S(arr.shape, arr.dtype))
        return len(self.out_shapes) - 1

    def read(self, arr):
        self.inputs.append(arr)
        return len(self.inputs) - 1

    def fresh(self, shape, dtype):
        self.out_shapes.append(S(shape, dtype))
        return len(self.out_shapes) - 1

    def _add(self, n_sems, copies):
        base = self.n_sems
        self.n_sems += n_sems

        def start(ins, outs, send, recv):
            for k, (src, dst, _, to) in enumerate(copies(ins, outs)):
                _remote(src, dst, send.at[base + k], recv.at[base + k], to).start()

        def finish(ins, outs, send, recv):
            for k, (src, _, land, to) in enumerate(copies(ins, outs)):
                _remote(src, land, send.at[base + k], recv.at[base + k], to).wait()

        self.ops.append((start, finish))

    def gather_ici(self, o, rows=None, split=True):
        half = self.out_shapes[o].shape[1] // 2
        lo, size = rows or (0, half)

        def copies(ins, outs):
            x, y, c, chips = _place()
            part = (lambda ref: ref.at[pl.ds(c * half + lo, size)]) if split else (lambda ref: ref)
            mine = part(outs[o].at[2 * x + y])
            return [(mine, mine, part(outs[o].at[2 * px + py]), (px, py, c)) for px, py in chips]

        self._add(3, copies)

    def gather_d2d(self, o, rows=None):
        half = self.out_shapes[o].shape[1] // 2
        lo, size = rows or (0, half)

        def copies(ins, outs):
            x, y, c, chips = _place()
            at = lambda k, cc: outs[o].at[k].at[pl.ds(cc * half + lo, size)]
            return [(at(2 * px + py, c), at(2 * px + py, c), at(2 * px + py, 1 - c), (x, y, 1 - c)) for px, py in chips]

        self._add(3, copies)

    def swap(self, i, o):
        half = self.inputs[i].shape[1] // 2

        def copies(ins, outs):
            x, y, c, _ = _place()
            return [(ins[i].at[:, pl.ds((1 - c) * half, half)], outs[o], outs[o], (x, y, 1 - c))]

        self._add(1, copies)

    def scatter(self, i, o, rows=None):
        lo, size = rows or (0, self.inputs[i].shape[1])

        def copies(ins, outs):
            x, y, c, chips = _place()
            cut = lambda ref: ref.at[pl.ds(lo, size)]
            return [(cut(ins[i].at[2 * px + py]), cut(outs[o].at[2 * x + y]), cut(outs[o].at[2 * px + py]), (px, py, c)) for px, py in chips]

        self._add(3, copies)

    def share(self, o):
        r = self.out_shapes[o].shape[0] // 2

        def copies(ins, outs):
            x, y, c, _ = _place()
            mine = outs[o].at[pl.ds(c * r, r)]
            return [(mine, mine, outs[o].at[pl.ds((1 - c) * r, r)], (x, y, 1 - c))]

        self._add(1, copies)

    def spread(self, i, o):
        def copies(ins, outs):
            x, y, c, _ = _place()
            me = 4 * x + 2 * y + c
            out = []
            for d in range(1, 8):
                to, frm = (me + d) % 8, (me + 8 - d) % 8
                out.append((ins[i], outs[o].at[me], outs[o].at[frm], (to // 4, (to // 2) % 2, to % 2)))
            return out

        self._add(7, copies)

    def _pallas(self, body, n_in, n_out, scratch, **kw):
        k_in, k_out = len(self.inputs), len(self.out_shapes)
        grid = kw.get("grid", ())

        def wrapped(*refs):
            ins, cins = refs[:n_in], refs[n_in:n_in + k_in]
            outs = refs[n_in + k_in:n_in + k_in + n_out]
            couts = refs[n_in + k_in + n_out:n_in + k_in + n_out + k_out]
            own = refs[n_in + k_in + n_out + k_out:]
            send, recv = own[len(scratch):]
            ids = [pl.program_id(a) for a in range(len(grid))]
            first = functools.reduce(jnp.logical_and, [a == 0 for a in ids], True)
            last = functools.reduce(jnp.logical_and, [a == g - 1 for a, g in zip(ids, grid)], True)

            def go(stage):
                for op in self.ops:
                    op[stage](cins, couts, send, recv)

            if grid:
                pl.when(first)(lambda: go(0))
                body(*ins, *outs, *own[:len(scratch)])
                pl.when(last)(lambda: go(1))
            else:
                go(0)
                go(1)

        sem = pltpu.SemaphoreType.DMA((self.n_sems,))
        return pl.pallas_call(
            wrapped, in_specs=list(kw.get("in_specs", [])) + [_ANY] * k_in, out_specs=list(kw.get("out_specs", [])) + [_ANY] * k_out,
            out_shape=list(kw.get("out_shape", [])) + self.out_shapes, scratch_shapes=list(scratch) + [sem, sem],
            input_output_aliases={**kw.get("aliases", {}), **{n_in + i: n_out + o for i, o in self.aliases.items()}}, name=kw["name"],
            **({"grid": grid, "compiler_params": _cp(*["arbitrary"] * len(grid))} if grid else {}))

    def run(self, body, kw, *args):
        single = not isinstance(kw["out_shape"], (list, tuple))
        out_shape = [kw["out_shape"]] if single else list(kw["out_shape"])
        out_specs = [kw["out_specs"]] if single else list(kw["out_specs"])
        res = self._pallas(body, len(args), len(out_shape), kw.get("scratch_shapes", []), grid=kw["grid"], in_specs=kw["in_specs"],
                           out_specs=out_specs, out_shape=out_shape, name=kw["name"],
                           aliases=kw.get("input_output_aliases", {}))(*args, *self.inputs)
        self.results = list(res[len(out_shape):])
        return res[0] if single else list(res[:len(out_shape)])

    def run_alone(self, name):
        self.results = list(self._pallas(None, 0, 0, [], name=name)(*self.inputs))


def _call(comm, body, **kw):
    if comm is None:
        return pl.pallas_call(body, **kw)
    return functools.partial(comm.run, body, kw)


def _in_proj_streamed(x, gain, car, o_w, place, *, bm, name):
    m, k = x.shape
    n = car.out_shapes[o_w].shape[2]
    ni, half = m // bm, k // 2
    k_in, k_out = len(car.inputs), len(car.out_shapes)
    order = lambda p: ((p & 1) << 1) | (p >> 1)

    def body(place_ref, x_ref, g_ref, *refs):
        cins, (hn_ref, o_ref, ob_ref), couts = refs[:k_in], refs[k_in:k_in + 3], refs[k_in + 3:k_in + 3 + k_out]
        wbuf, hn_all, local, ici_send, ici_recv, d2d_send, d2d_recv, send, recv = refs[k_in + 3 + k_out:]
        p, i = pl.program_id(0), pl.program_id(1)
        x, y, c, chips = _place()
        me = 2 * x + y
        rows = lambda chunk, cc: couts[o_w].at[chunk].at[pl.ds(cc * half, half)]

        @pl.when(jnp.logical_and(p == 0, i == 0))
        def _():
            for j, (px, py) in enumerate(chips):
                _remote(rows(me, c), rows(me, c), ici_send.at[j], ici_recv.at[j], (px, py, c)).start()
            for op in car.ops:
                op[0](cins, couts, send, recv)

        for j, (px, py) in enumerate(chips):
            @pl.when(jnp.logical_and(p == j + 1, i == 0))
            def _(j=j, px=px, py=py):
                landed, other = rows(2 * px + py, c), rows(2 * px + py, 1 - c)
                _remote(landed, landed, ici_send.at[j], ici_recv.at[j], (px, py, c)).wait_recv()
                _remote(landed, landed, d2d_send.at[j], d2d_recv.at[j], (x, y, 1 - c)).start()
                _remote(other, other, d2d_send.at[j], d2d_recv.at[j], (x, y, 1 - c)).wait_recv()

        @pl.when(i == 0)
        def _():
            cp = pltpu.make_async_copy(couts[o_w].at[me ^ order(p)], wbuf, local.at[0])
            cp.start()
            cp.wait()

        tile = pl.ds(pl.multiple_of(i * bm, bm), bm)

        @pl.when(p == 0)
        def _():
            xv = x_ref[...]
            hn_all[tile, :] = ((xv * _rstd(xv)) * g_ref[...]).astype(_MXU)

        hn = hn_all[tile, :]
        hn_ref[...] = hn
        res = _dot(hn, wbuf[...])
        o_ref[...] = res
        ob_ref[...] = res.astype(ob_ref.dtype)

        @pl.when(jnp.logical_and(p == 3, i == ni - 1))
        def _():
            for j, (px, py) in enumerate(chips):
                _remote(rows(me, c), rows(me, c), ici_send.at[j], ici_recv.at[j], (px, py, c)).wait_send()
                _remote(rows(me, c), rows(me, c), d2d_send.at[j], d2d_recv.at[j], (x, y, 1 - c)).wait_send()
            for op in car.ops:
                op[1](cins, couts, send, recv)

    ospec = pl.BlockSpec((bm, n), lambda p, i, place_ref: (i, place_ref[0] ^ order(p)))
    rows = pl.BlockSpec((bm, k), lambda p, i, place_ref: (jnp.where(p == 0, i, 0), 0))
    three, sems = pltpu.SemaphoreType.DMA((3,)), pltpu.SemaphoreType.DMA((max(car.n_sems, 1),))
    res = pl.pallas_call(
        body,
        grid_spec=pltpu.PrefetchScalarGridSpec(
            num_scalar_prefetch=1, grid=(4, ni),
            in_specs=[rows, pl.BlockSpec((1, k), lambda p, i, place_ref: (0, 0))] + [_ANY] * k_in,
            out_specs=[pl.BlockSpec((bm, k), lambda p, i, place_ref: (p * ni + i, 0)), ospec, ospec] + [_ANY] * k_out,
            scratch_shapes=[pltpu.VMEM((k, n), _MXU), pltpu.VMEM((m, k), _MXU), pltpu.SemaphoreType.DMA((1,)),
                            three, three, three, three, sems, sems]),
        out_shape=[S((4 * m, k), _MXU), S((m, 4 * n), F32), S((m, 4 * n), _MXU)] + car.out_shapes,
        input_output_aliases={3 + a: 3 + o for a, o in car.aliases.items()},
        compiler_params=_cp("arbitrary", "arbitrary"), name=name)(place, x, gain, *car.inputs)
    car.results = list(res[3:])
    return res[0], res[1], res[2]


def _mm_nn(a, b3, *, bm, bn, name, also=None, comm=None):
    m, k = a.shape
    c, _, n = b3.shape
    ni, nj = m // bm, n // bn

    def body(a_ref, b_ref, *o_refs):
        res = _dot(a_ref[...], b_ref[...])
        for o_ref in o_refs:
            o_ref[...] = res.astype(o_ref.dtype)

    ospec = pl.BlockSpec((bm, bn), lambda cc, j, i: (i, cc * nj + j))
    dtypes = [F32] + ([] if also is None else [also])
    out = _call(
        comm, body, grid=(c, nj, ni),
        in_specs=[pl.BlockSpec((bm, k), lambda cc, j, i: (i, 0)), pl.BlockSpec((None, k, bn), lambda cc, j, i: (cc, 0, j))],
        out_specs=[ospec] * len(dtypes), out_shape=[S((m, c * n), dt) for dt in dtypes],
        compiler_params=_cp("parallel", "parallel", "parallel"), name=name)(a, b3)
    return out[0] if also is None else out


def _mm_nt(a, b3, *, bm, bo, out_dtype, name, comm=None):
    m = a.shape[0]
    c, ko, n = b3.shape
    ni, nj = m // bm, ko // bo

    def body(a_ref, b_ref, o_ref):
        acc = _dot_nt(a_ref[:, 0:n], b_ref[0])
        for cc in range(1, c):
            acc = acc + _dot_nt(a_ref[:, cc * n:(cc + 1) * n], b_ref[cc])
        o_ref[...] = acc.astype(o_ref.dtype)

    return _call(
        comm, body, grid=(nj, ni),
        in_specs=[pl.BlockSpec((bm, c * n), lambda j, i: (i, 0)),
                  pl.BlockSpec((c, bo, n), lambda j, i: (0, j, 0))],
        out_specs=pl.BlockSpec((bm, bo), lambda j, i: (i, j)),
        out_shape=S((m, ko), out_dtype),
        compiler_params=_cp("parallel", "parallel"), name=name)(a, b3)


def _mm_tn(a, b, c, *, bm, bk, out_dtype, name, comm=None):
    m, k = b.shape[0], a.shape[1]
    n = b.shape[1] // c
    nm, nk = m // bm, k // bk

    def body(a_ref, b_ref, o_ref, *acc):
        if nm == 1:
            o_ref[...] = _dot_tn(a_ref[...], b_ref[...]).astype(o_ref.dtype)
            return
        mm = pl.program_id(2)

        @pl.when(mm == 0)
        def _():
            acc[0][...] = jnp.zeros_like(acc[0])

        acc[0][...] += _dot_tn(a_ref[...], b_ref[...])

        @pl.when(mm == nm - 1)
        def _():
            o_ref[...] = acc[0][...].astype(o_ref.dtype)

    return _call(
        comm, body, grid=(c, nk, nm),
        in_specs=[pl.BlockSpec((bm, bk), lambda cc, j, mm: (mm, j)),
                  pl.BlockSpec((bm, n), lambda cc, j, mm: (mm, cc))],
        out_specs=pl.BlockSpec((None, bk, n), lambda cc, j, mm: (cc, j, 0)),
        out_shape=S((c, k, n), out_dtype),
        scratch_shapes=[] if nm == 1 else [pltpu.VMEM((bk, n), F32)],
        compiler_params=_cp("parallel", "parallel", "arbitrary"), name=name)(a, b)


def _swiglu_fwd(hn, wg3, wu3, *, bm, name, comm=None):
    m, k = hn.shape
    c, _, n = wg3.shape

    def body(a_ref, g_ref, u_ref, dgate_ref, dup_ref, act_ref):
        a = a_ref[...]
        gate = _dot(a, g_ref[...])
        up = _dot(a, u_ref[...])
        sg = jax.nn.sigmoid(gate)
        silu = gate * sg
        dgate_ref[...] = (up * (sg * (1.0 + gate * (1.0 - sg)))).astype(dgate_ref.dtype)
        dup_ref[...] = silu.astype(dup_ref.dtype)
        act_ref[...] = (silu * up).astype(act_ref.dtype)

    wspec = pl.BlockSpec((None, k, n), lambda cc, i: (cc, 0, 0))
    ospec = pl.BlockSpec((bm, n), lambda cc, i: (i, cc))
    return _call(
        comm, body, grid=(c, m // bm),
        in_specs=[pl.BlockSpec((bm, k), lambda cc, i: (i, 0)), wspec, wspec],
        out_specs=[ospec, ospec, ospec],
        out_shape=[S((m, c * n), _MXU), S((m, c * n), _MXU), S((m, c * n), _MXU)],
        compiler_params=_cp("parallel", "parallel"), name=name)(hn, wg3, wu3)


def _swiglu_bwd(df, wd, act_dgate, act_dup, *, bm, bo, name):
    m, k = df.shape
    ko = wd.shape[0]

    def body(a_ref, b_ref, g_ref, u_ref, dg_ref, du_ref):
        dact = _dot_nt(a_ref[...], b_ref[...])
        dg_ref[...] = (dact * g_ref[...].astype(F32)).astype(dg_ref.dtype)
        du_ref[...] = (dact * u_ref[...].astype(F32)).astype(du_ref.dtype)

    ospec = pl.BlockSpec((bm, bo), lambda j, i: (i, j))
    return pl.pallas_call(
        body, grid=(ko // bo, m // bm),
        in_specs=[pl.BlockSpec((bm, k), lambda j, i: (i, 0)), pl.BlockSpec((bo, k), lambda j, i: (j, 0)), ospec, ospec],
        out_specs=[ospec, ospec],
        out_shape=[S((m, ko), _MXU), S((m, ko), _MXU)],
        compiler_params=_cp("parallel", "parallel"), name=name)(df, wd, act_dgate, act_dup)


def _rms_fwd(x, gain, name):
    t, d = x.shape
    tm = min(t, ROW_TILE)

    def body(x_ref, g_ref, o_ref):
        xv = x_ref[...]
        o_ref[...] = ((xv * _rstd(xv)) * g_ref[...]).astype(o_ref.dtype)

    return pl.pallas_call(body, grid=(t // tm,), in_specs=[_row_spec(tm, d), _vec_spec(d)], out_specs=_row_spec(tm, d),
                          out_shape=S((t, d), _MXU), compiler_params=_cp("parallel"), name=name)(x, gain)


def _outnorm_fwd(o, yl, ga, gl, name, comm=None):
    t, w = o.shape
    tm = min(t, ROW_TILE)

    def body(o_ref, l_ref, ga_ref, gl_ref, y_ref):
        ov, lv = o_ref[...], l_ref[...]
        y_ref[:, :w] = ((ov * _rstd(ov)) * ga_ref[...]).astype(y_ref.dtype)
        y_ref[:, w:] = ((lv * _rstd(lv)) * gl_ref[...]).astype(y_ref.dtype)

    return _call(comm, body, grid=(t // tm,), in_specs=[_row_spec(tm, w), _row_spec(tm, w), _vec_spec(w), _vec_spec(w)],
                 out_specs=_row_spec(tm, 2 * w), out_shape=S((t, 2 * w), _MXU),
                 compiler_params=_cp("parallel"), name=name)(o, yl, ga, gl)


def _mid_fwd(x, mix, g_post, g_pre, name, comm=None):
    t, d = x.shape
    tm = min(t, ROW_TILE)

    def body(x_ref, m_ref, gp_ref, gn_ref, x2_ref, hn_ref):
        mv = m_ref[...]
        x2 = x_ref[...] + (mv * _rstd(mv)) * gp_ref[...]
        x2_ref[...] = x2
        hn_ref[...] = ((x2 * _rstd(x2)) * gn_ref[...]).astype(hn_ref.dtype)

    return _call(comm, body, grid=(t // tm,), in_specs=[_row_spec(tm, d), _row_spec(tm, d), _vec_spec(d), _vec_spec(d)],
                          out_specs=[_row_spec(tm, d), _row_spec(tm, d)], out_shape=[S((t, d), F32), S((t, d), _MXU)],
                          compiler_params=_cp("parallel"), name=name)(x, mix, g_post, g_pre)


def _final(f, x2, target, g_post, name):
    t, d = f.shape
    tm = min(t, ROW_TILE // 2)

    def body(f_ref, x2_ref, t_ref, g_ref, loss_ref, dout_ref, df_ref, dg_ref):
        @pl.when(pl.program_id(0) == 0)
        def _():
            loss_ref[...] = jnp.zeros_like(loss_ref)
            dg_ref[...] = jnp.zeros_like(dg_ref)

        fv = f_ref[...]
        r = _rstd(fv)
        fh = fv * r
        err = (x2_ref[...] + fh * g_ref[...]) - t_ref[...]
        loss_ref[...] += jnp.sum(err * err, axis=0, keepdims=True)
        dout = err * (1.0 / d)
        dout_ref[...] = dout
        dfv, dg = _rms_bwd(dout, fh, r, g_ref[...])
        df_ref[...] = dfv.astype(df_ref.dtype)
        dg_ref[...] += dg

    return pl.pallas_call(
        body, grid=(t // tm,),
        in_specs=[_row_spec(tm, d), _row_spec(tm, d), _row_spec(tm, d), _vec_spec(d)],
        out_specs=[_vec_spec(d), _row_spec(tm, d), _row_spec(tm, d), _vec_spec(d)],
        out_shape=[S((1, d), F32), S((t, d), F32), S((t, d), _MXU), S((1, d), F32)],
        compiler_params=_cp("arbitrary"), name=name)(f, x2, target, g_post)


def _mid_bwd(dhn_a, dhn_b, dout, x2, mix, g_pre, g_post, name, comm=None):
    t, d = x2.shape
    tm = min(t, ROW_TILE // 2)

    def body(da_ref, db_ref, do_ref, x2_ref, m_ref, gn_ref, gp_ref, dx2_ref, dm_ref, dgn_ref, dgp_ref):
        @pl.when(pl.program_id(0) == 0)
        def _():
            dgn_ref[...] = jnp.zeros_like(dgn_ref)
            dgp_ref[...] = jnp.zeros_like(dgp_ref)

        x2 = x2_ref[...]
        r = _rstd(x2)
        dxa, dgn = _rms_bwd(da_ref[...] + db_ref[...], x2 * r, r, gn_ref[...])
        dx2 = do_ref[...] + dxa
        dx2_ref[...] = dx2
        dgn_ref[...] += dgn
        mv = m_ref[...]
        rm = _rstd(mv)
        dmv, dgp = _rms_bwd(dx2, mv * rm, rm, gp_ref[...])
        dm_ref[...] = dmv.astype(dm_ref.dtype)
        dgp_ref[...] += dgp

    rs, vs = _row_spec(tm, d), _vec_spec(d)
    return _call(
        comm, body, grid=(t // tm,), in_specs=[rs, rs, rs, rs, rs, vs, vs], out_specs=[rs, rs, vs, vs],
        out_shape=[S((t, d), F32), S((t, d), _MXU), S((1, d), F32), S((1, d), F32)],
        compiler_params=_cp("arbitrary"), name=name)(dhn_a, dhn_b, dout, x2, mix, g_pre, g_post)


def _first_bwd(dhn, dx2, x, gain, name, comm=None):
    t, d = x.shape
    tm = min(t, ROW_TILE)

    def body(dh_ref, dx2_ref, x_ref, g_ref, dx_ref, dg_ref):
        @pl.when(pl.program_id(0) == 0)
        def _():
            dg_ref[...] = jnp.zeros_like(dg_ref)

        xv = x_ref[...]
        r = _rstd(xv)
        dxa, dg = _rms_bwd(dh_ref[...], xv * r, r, g_ref[...])
        dx_ref[...] = dx2_ref[...] + dxa
        dg_ref[...] += dg

    rs, vs = _row_spec(tm, d), _vec_spec(d)
    return _call(comm, body, grid=(t // tm,), in_specs=[rs, rs, rs, vs], out_specs=[rs, vs],
                          out_shape=[S((t, d), F32), S((1, d), F32)], compiler_params=_cp("arbitrary"), name=name)(dhn, dx2, x, gain)


def _outnorm_bwd(dy, o, yl, ga, gl, name, comm=None):
    t, w = o.shape
    tm = min(t, ROW_TILE)

    def body(dy_ref, o_ref, l_ref, ga_ref, gl_ref, do_ref, dl_ref, dga_ref, dgl_ref):
        @pl.when(pl.program_id(0) == 0)
        def _():
            dga_ref[...] = jnp.zeros_like(dga_ref)
            dgl_ref[...] = jnp.zeros_like(dgl_ref)

        ov, lv = o_ref[...], l_ref[...]
        ra, rl = _rstd(ov), _rstd(lv)
        dov, dga = _rms_bwd(dy_ref[:, :w], ov * ra, ra, ga_ref[...])
        dlv, dgl = _rms_bwd(dy_ref[:, w:], lv * rl, rl, gl_ref[...])
        do_ref[...] = dov.astype(do_ref.dtype)
        dl_ref[...] = dlv
        dga_ref[...] += dga
        dgl_ref[...] += dgl

    rs, vs = _row_spec(tm, w), _vec_spec(w)
    return _call(comm, body, grid=(t // tm,), in_specs=[_row_spec(tm, 2 * w), rs, rs, vs, vs], out_specs=[rs, rs, vs, vs],
                          out_shape=[S((t, w), _MXU), S((t, w), F32), S((1, w), F32), S((1, w), F32)],
                          compiler_params=_cp("arbitrary"), name=name)(dy, o, yl, ga, gl)


def _tri_sum(v, tri):
    return _dot(v.astype(_MXU), tri)


def _attn_tile(qb, kb, row, col, shift, scale):
    z = _dot_nt(qb, kb) * scale
    mask = (col + shift) < row
    lb = _log_sigmoid(z)
    lm = jnp.where(mask, lb - z, 0.0)
    return mask, lb, lm


def _attn_fwd(proj, n_heads, name, comm=None):
    t = proj.shape[0]
    bq = min(t, ATTN_BLOCK)
    nq = t // bq
    scale = 1.0 / math.sqrt(HEAD_DIM)

    heads = [slice(a * HEAD_DIM, (a + 1) * HEAD_DIM) for a in range(ATTN_HEADS)]

    def body(q_ref, k_ref, v_ref, o_ref):
        row = lax.broadcasted_iota(jnp.int32, (bq, bq), 0)
        col = lax.broadcasted_iota(jnp.int32, (bq, bq), 1)
        tri = (row > col).astype(_MXU)

        def per_q(qi, _):
            q0 = pl.multiple_of(qi * bq, bq)
            qbs = [q_ref[pl.ds(q0, bq), hd] for hd in heads]

            def cond(st):
                return jnp.logical_and(st[0] >= 0, st[1])

            def step(st):
                kj, _, carries, accs = st
                k0 = pl.multiple_of(kj * bq, bq)
                alive, new_carries, new_accs = None, [], []
                for hd, qb, carry, acc in zip(heads, qbs, carries, accs):
                    mask, lb, lm = _attn_tile(qb, k_ref[pl.ds(k0, bq), hd], row, col, (kj - qi) * bq, scale)
                    w = jnp.where(mask, jnp.exp(lb + _tri_sum(lm, tri) + carry), 0.0)
                    new_accs.append(acc + _dot(w.astype(_MXU), v_ref[pl.ds(k0, bq), hd]))
                    carry = carry + jnp.sum(lm, axis=1, keepdims=True)
                    new_carries.append(carry)
                    live = jnp.max(carry) > EXP_CUT
                    alive = live if alive is None else jnp.logical_or(alive, live)
                return kj - 1, alive, tuple(new_carries), tuple(new_accs)

            st = lax.while_loop(cond, step, (qi, jnp.bool_(True), (jnp.zeros((bq, 1), F32),) * ATTN_HEADS,
                                             (jnp.zeros((bq, HEAD_DIM), F32),) * ATTN_HEADS))
            for hd, acc in zip(heads, st[3]):
                o_ref[pl.ds(q0, bq), hd] = acc
            return 0

        lax.fori_loop(0, nq, per_q, 0)

    groups = n_heads // ATTN_HEADS
    hs = lambda off: pl.BlockSpec((t, ATTN_HEADS * HEAD_DIM), lambda h: (0, off + h))
    return _call(
        comm, body, grid=(groups,), in_specs=[hs(0), hs(groups), hs(2 * groups)], out_specs=hs(0),
        out_shape=S((t, n_heads * HEAD_DIM), F32), compiler_params=_cp("parallel"), name=name)(proj, proj, proj)


def _emit(blocks, out_ref, starts, sems):
    copies = [pltpu.make_async_copy(b, out_ref.at[:, pl.ds(c0, b.shape[1])], sems.at[k]) for k, (b, c0) in enumerate(zip(blocks, starts))]
    for cp in copies:
        cp.start()
    for cp in copies:
        cp.wait()


def _attn_bwd(proj, do, dproj, n_heads, name, comm=None):
    t = proj.shape[0]
    bq = min(t, ATTN_BLOCK)
    nq = t // bq
    scale = 1.0 / math.sqrt(HEAD_DIM)
    groups = n_heads // ATTN_HEADS
    wide = ATTN_HEADS * HEAD_DIM

    heads = [slice(a * HEAD_DIM, (a + 1) * HEAD_DIM) for a in range(ATTN_HEADS)]

    def body(q_ref, k_ref, v_ref, do_ref, _, dproj_ref, dka_ref, dva_ref, g_ref, b_ref, dq_ref, dk_ref, dv_ref, out_sems):
        group = pl.program_id(0)
        dka_ref[...] = jnp.zeros_like(dka_ref)
        dva_ref[...] = jnp.zeros_like(dva_ref)
        row = lax.broadcasted_iota(jnp.int32, (bq, bq), 0)
        col = lax.broadcasted_iota(jnp.int32, (bq, bq), 1)
        tri = (row > col).astype(_MXU)
        tri_lt = (row < col).astype(_MXU)

        def per_q(qi, _):
            q0 = pl.multiple_of(qi * bq, bq)
            qbs = [q_ref[pl.ds(q0, bq), hd] for hd in heads]
            dobs = [do_ref[pl.ds(q0, bq), hd] for hd in heads]

            def cond(st):
                return jnp.logical_and(st[0] >= 0, st[1])

            def step(st):
                kj, _, carries = st
                k0 = pl.multiple_of(kj * bq, bq)
                alive, new_carries = None, []
                for a, (hd, qb, dob, carry) in enumerate(zip(heads, qbs, dobs, carries)):
                    mask, lb, lm = _attn_tile(qb, k_ref[pl.ds(k0, bq), hd], row, col, (kj - qi) * bq, scale)
                    w = jnp.where(mask, jnp.exp(lb + _tri_sum(lm, tri) + carry), 0.0)
                    g_ref[a, pl.ds(k0, bq), :] = w * _dot_nt(dob, v_ref[pl.ds(k0, bq), hd])
                    b_ref[a, pl.ds(k0, bq), :] = jnp.where(mask, jnp.exp(lb), 0.0)
                    dva_ref[pl.ds(k0, bq), hd] += _dot_tn(w.astype(_MXU), dob)
                    carry = carry + jnp.sum(lm, axis=1, keepdims=True)
                    new_carries.append(carry)
                    live = jnp.max(carry) > EXP_CUT
                    alive = live if alive is None else jnp.logical_or(alive, live)
                return kj - 1, alive, tuple(new_carries)

            st = lax.while_loop(cond, step, (qi, jnp.bool_(True), (jnp.zeros((bq, 1), F32),) * ATTN_HEADS))

            def back(kj, st2):
                k0 = pl.multiple_of(kj * bq, bq)
                out = []
                for a, (hd, qb, (before, dq)) in enumerate(zip(heads, qbs, st2)):
                    g = g_ref[a, pl.ds(k0, bq), :]
                    beta = b_ref[a, pl.ds(k0, bq), :]
                    dz = ((g * (1.0 - beta) - (before + _tri_sum(g, tri_lt)) * beta) * scale).astype(_MXU)
                    dka_ref[pl.ds(k0, bq), hd] += _dot_tn(dz, qb)
                    out.append((before + jnp.sum(g, axis=1, keepdims=True), dq + _dot(dz, k_ref[pl.ds(k0, bq), hd])))
                return tuple(out)

            st2 = lax.fori_loop(st[0] + 1, qi + 1, back, ((jnp.zeros((bq, 1), F32), jnp.zeros((bq, HEAD_DIM), F32)),) * ATTN_HEADS)
            for hd, (_, dq) in zip(heads, st2):
                dq_ref[pl.ds(q0, bq), hd] = dq.astype(dq_ref.dtype)
            return 0

        lax.fori_loop(0, nq, per_q, 0)
        dk_ref[...] = dka_ref[...].astype(dk_ref.dtype)
        dv_ref[...] = dva_ref[...].astype(dv_ref.dtype)
        _emit([dq_ref, dk_ref, dv_ref], dproj_ref, [(a * groups + group) * wide for a in range(3)], out_sems)

    hs = lambda off: pl.BlockSpec((t, wide), lambda h: (0, off + h))
    return _call(
        comm, body, grid=(groups,), in_specs=[hs(0), hs(groups), hs(2 * groups), hs(0), _ANY], out_specs=_ANY,
        out_shape=S(dproj.shape, dproj.dtype), input_output_aliases={4: 0},
        scratch_shapes=[pltpu.VMEM((t, wide), F32), pltpu.VMEM((t, wide), F32),
                        pltpu.VMEM((ATTN_HEADS, t, bq), F32), pltpu.VMEM((ATTN_HEADS, t, bq), F32)]
        + [pltpu.VMEM((t, wide), dproj.dtype)] * 3 + [pltpu.SemaphoreType.DMA((3,))],
        compiler_params=_cp("parallel"), name=name)(proj, proj, proj, do, dproj)


def _shift_down(cur, prev8, k):
    if k == 0:
        return cur
    row8 = lax.broadcasted_iota(jnp.int32, prev8.shape, 0)
    rc = pltpu.roll(cur, k, 0)
    top = jnp.where(row8 < k, pltpu.roll(prev8, k, 0), rc[0:8, :])
    return jnp.concatenate([top, rc[8:, :]], axis=0)


def _shift_up(cur, next8, k):
    if k == 0:
        return cur
    n = cur.shape[0]
    row8 = lax.broadcasted_iota(jnp.int32, next8.shape, 0)
    rc = pltpu.roll(cur, n - k, 0)
    bottom = jnp.where(row8 >= 8 - k, pltpu.roll(next8, 8 - k, 0), rc[n - 8:, :])
    return jnp.concatenate([rc[:n - 8, :], bottom], axis=0)


def _lru_conv(xl, prev8, cw, cb):
    xs = [_shift_down(xl, prev8, CONV_WIDTH - 1 - k) for k in range(CONV_WIDTH)]
    xc = xs[0] * cw[0:1, :]
    for k in range(1, CONV_WIDTH):
        xc = xc + xs[k] * cw[k:k + 1, :]
    return xs, xc + cb


def _lru_gates(xl, prev8, cw, cb, wr, br, wi, bi, ls):
    xs, xc = _lru_conv(xl, prev8, cw, cb)
    xcb = xc.astype(_MXU)
    r = jax.nn.sigmoid(_dot(xcb, wr) + br)
    i = jax.nn.sigmoid(_dot(xcb, wi) + bi)
    la = (LRU_C * r) * ls
    a = jnp.exp(la)
    mult = jnp.sqrt(-_expm1(2.0 * la))
    return xs, xc, r, i, a, mult


def _group_scan(a, b, reverse):
    n = a.shape[0]
    row = lax.broadcasted_iota(jnp.int32, a.shape, 0) % 8
    for d in (1, 2, 4):
        if reverse:
            m = row < 8 - d
            a_s, b_s = pltpu.roll(a, n - d, 0), pltpu.roll(b, n - d, 0)
        else:
            m = row >= d
            a_s, b_s = pltpu.roll(a, d, 0), pltpu.roll(b, d, 0)
        b = jnp.where(m, a * b_s + b, b)
        a = jnp.where(m, a * a_s, a)
    return a, b


def _lru_fwd(proj, col0, n_blocks, cw, cb, wr, br, wi, bi, lam, name, comm=None):
    t = proj.shape[0]
    tt = min(t, SEQ_TILE)
    nt = t // tt

    def body(xl_ref, gl_ref, cw_ref, cb_ref, wr_ref, br_ref, wi_ref, bi_ref, lam_ref, h_ref, y_ref, *kept):
        cwv, cbv, brv, biv = cw_ref[...], cb_ref[...], br_ref[...], bi_ref[...]
        wrv, wiv = wr_ref[...].astype(_MXU), wi_ref[...].astype(_MXU)
        ls = _log_sigmoid(lam_ref[...])

        def tile(ti, hin):
            t0 = pl.multiple_of(ti * tt, tt)
            p0 = pl.multiple_of(jnp.maximum(t0 - 8, 0), 8)
            prev8 = xl_ref[pl.ds(p0, 8), :] * (ti > 0).astype(F32)
            xl = xl_ref[pl.ds(t0, tt), :]
            _, xc, r, ig, a, mult = _lru_gates(xl, prev8, cwv, cbv, wrv, brv, wiv, biv, ls)
            for ref, val in zip(kept, (r, ig, a, mult)):
                ref[pl.ds(t0, tt), :] = val
            ga, gb = _group_scan(a, mult * (ig * xc), False)
            for g in range(tt // 8):
                hg = ga[8 * g:8 * g + 8, :] * hin + gb[8 * g:8 * g + 8, :]
                h_ref[pl.ds(t0 + 8 * g, 8), :] = hg
                hin = hg[7:8, :]
            y_ref[pl.ds(t0, tt), :] = h_ref[pl.ds(t0, tt), :] * _gelu(gl_ref[pl.ds(t0, tt), :])
            return hin

        lax.fori_loop(0, nt, tile, jnp.zeros((1, HEAD_DIM), F32))

    cs = lambda off: pl.BlockSpec((t, HEAD_DIM), lambda n: (0, off + n))
    vs = pl.BlockSpec((1, HEAD_DIM), lambda n: (0, n))
    ws = pl.BlockSpec((None, HEAD_DIM, HEAD_DIM), lambda n: (n, 0, 0))
    w = n_blocks * HEAD_DIM
    return _call(
        comm, body, grid=(n_blocks,),
        in_specs=[cs(col0), cs(col0 + n_blocks), pl.BlockSpec((CONV_WIDTH, HEAD_DIM), lambda n: (0, n)), vs, ws, vs, ws, vs, vs],
        out_specs=[cs(0)] * 6, out_shape=[S((t, w), F32)] * 6,
        compiler_params=_cp("parallel"), name=name)(proj, proj, cw, cb, wr, br, wi, bi, lam)


def _lru_bwd(proj, col0, n_blocks, h, kept, dyl, cw, cb, wr, wi, lam, name, comm=None):
    t = proj.shape[0]
    tt = min(t, SEQ_TILE)
    nt = t // tt

    def body(xl_ref, gl_ref, h_ref, r_ref, i_ref, a_ref, m_ref, dy_ref, cw_ref, cb_ref, wr_ref, wi_ref, lam_ref,
             dproj_ref, dcw_ref, dcb_ref, dwr_ref, dbr_ref, dwi_ref, dbi_ref, dlam_ref, g_ref, dxl_ref, dgl_ref, out_sems):
        block = pl.program_id(0)
        cwv, cbv = cw_ref[...], cb_ref[...]
        wrv, wiv = wr_ref[...].astype(_MXU), wi_ref[...].astype(_MXU)
        lamv = lam_ref[...]
        ls = _log_sigmoid(lamv)
        for ref in (dcw_ref, dcb_ref, dwr_ref, dbr_ref, dwi_ref, dbi_ref, dlam_ref):
            ref[...] = jnp.zeros_like(ref)

        def tile(s, carry):
            e_in, dxc_next8 = carry
            ti = nt - 1 - s
            t0 = pl.multiple_of(ti * tt, tt)
            p0 = pl.multiple_of(jnp.maximum(t0 - 8, 0), 8)
            first = (ti > 0).astype(F32)
            xl = xl_ref[pl.ds(t0, tt), :]
            xs, xc = _lru_conv(xl, xl_ref[pl.ds(p0, 8), :] * first, cwv, cbv)
            r, ig, a, mult = (ref[pl.ds(t0, tt), :] for ref in (r_ref, i_ref, a_ref, m_ref))
            hv = h_ref[pl.ds(t0, tt), :]
            h_before = _shift_down(hv, h_ref[pl.ds(p0, 8), :] * first, 1)
            glv = gl_ref[pl.ds(t0, tt), :]
            dyv = dy_ref[pl.ds(t0, tt), :]
            dgl_ref[pl.ds(t0, tt), :] = (dyv * hv * _gelu_grad(glv)).astype(dgl_ref.dtype)
            dh = dyv * _gelu(glv)
            row = lax.broadcasted_iota(jnp.int32, a.shape, 0)
            coef = jnp.where(row == tt - 1, 1.0, pltpu.roll(a, tt - 1, 0))
            ga, gb = _group_scan(coef, dh, True)
            gin = e_in
            for g in reversed(range(tt // 8)):
                gg = ga[8 * g:8 * g + 8, :] * gin + gb[8 * g:8 * g + 8, :]
                g_ref[8 * g:8 * g + 8, :] = gg
                gin = gg[0:1, :]
            gv = g_ref[...]
            e_out = a[0:1, :] * gv[0:1, :]
            ix = ig * xc
            dla = (gv * h_before) * a - (gv * ix) * (a * a / mult)
            dlam_ref[...] += jnp.sum(dla * (LRU_C * r), axis=0, keepdims=True)
            dpr = (dla * (LRU_C * ls)) * (r * (1.0 - r))
            dpi = (gv * mult * xc) * (ig * (1.0 - ig))
            dbr_ref[...] += jnp.sum(dpr, axis=0, keepdims=True)
            dbi_ref[...] += jnp.sum(dpi, axis=0, keepdims=True)
            xcb, dprb, dpib = xc.astype(_MXU), dpr.astype(_MXU), dpi.astype(_MXU)
            dwr_ref[...] += _dot_tn(xcb, dprb)
            dwi_ref[...] += _dot_tn(xcb, dpib)
            dxc = gv * mult * ig + _dot_nt(dprb, wrv) + _dot_nt(dpib, wiv)
            dcb_ref[...] += jnp.sum(dxc, axis=0, keepdims=True)
            dxl = None
            for k in range(CONV_WIDTH):
                dcw_ref[k:k + 1, :] += jnp.sum(dxc * xs[k], axis=0, keepdims=True)
                term = _shift_up(dxc, dxc_next8, CONV_WIDTH - 1 - k) * cwv[k:k + 1, :]
                dxl = term if dxl is None else dxl + term
            dxl_ref[pl.ds(t0, tt), :] = dxl.astype(dxl_ref.dtype)
            return e_out, dxc[0:8, :]

        lax.fori_loop(0, nt, tile, (jnp.zeros((1, HEAD_DIM), F32), jnp.zeros((8, HEAD_DIM), F32)))
        dlam_ref[...] = dlam_ref[...] * (1.0 - jax.nn.sigmoid(lamv))
        _emit([dxl_ref, dgl_ref], dproj_ref, [(col0 + block) * HEAD_DIM, (col0 + n_blocks + block) * HEAD_DIM], out_sems)

    cs = lambda off: pl.BlockSpec((t, HEAD_DIM), lambda n: (0, off + n))
    vs = pl.BlockSpec((1, HEAD_DIM), lambda n: (0, n))
    ws = pl.BlockSpec((None, HEAD_DIM, HEAD_DIM), lambda n: (n, 0, 0))
    cws = pl.BlockSpec((CONV_WIDTH, HEAD_DIM), lambda n: (0, n))
    w = n_blocks * HEAD_DIM
    vec = S((1, w), F32)
    mat = S((n_blocks, HEAD_DIM, HEAD_DIM), F32)
    return _call(
        comm, body, grid=(n_blocks,),
        in_specs=[cs(col0), cs(col0 + n_blocks)] + [cs(0)] * 6 + [cws, vs, ws, ws, vs],
        out_specs=[_ANY, cws, vs, ws, vs, ws, vs, vs],
        out_shape=[S(proj.shape, _MXU), S((CONV_WIDTH, w), F32), vec, mat, vec, mat, vec, vec],
        scratch_shapes=[pltpu.VMEM((tt, HEAD_DIM), F32), pltpu.VMEM((t, HEAD_DIM), _MXU), pltpu.VMEM((t, HEAD_DIM), _MXU),
                        pltpu.SemaphoreType.DMA((2,))],
        compiler_params=_cp("parallel"), name=name)(proj, proj, h, *kept, dyl, cw, cb, wr, wi, lam)


class _NoExchange:
    grad_dtype = F32

    def __init__(self, weights):
        self.weights, self.grads, self.packs = weights, {}, {}

    def weight(self, name):
        return self.weights[name]

    def in_proj(self, x, gain, bm):
        hn = _rms_fwd(x, gain, "rms1")
        return [hn, *_mm_nn(hn, self.weights["w_in"], bm=bm, bn=self.weights["w_in"].shape[2], name="in_proj", also=_MXU)]

    def conv_w(self):
        return self.weights["conv_w"]

    def carrier(self, call):
        return None

    def harvest(self, car):
        pass

    def alone(self, call):
        pass


def _local_step(x, target, norms, ex, cb, wr, br, wi, bi, lam, ga, gl):
    g_pre_mix, g_post_mix, g_pre_ffn, g_post_ffn = norms
    t, d = x.shape
    bm = min(t, MM_ROWS)
    bt = min(t, DW_TOKENS)

    def run(fn, name, *args, **kw):
        car = ex.carrier(name)
        out = fn(*args, name=name, comm=car, **kw)
        ex.harvest(car)
        return out

    hn1, proj, proj_mx = ex.in_proj(x, g_pre_mix, bm)
    win3, cw = ex.weight("w_in"), ex.conv_w()
    c = win3.shape[0]
    o = run(_attn_fwd, "attn_fwd", proj_mx, (proj.shape[1] - d) // 3 // HEAD_DIM)
    mix = 2 * o.shape[1]
    n_heads = n_blocks = o.shape[1] // HEAD_DIM
    h, yl, *kept = run(_lru_fwd, "lru_fwd", proj, 3 * n_heads, n_blocks, cw, cb, wr, br, wi, bi, lam)
    y = run(_outnorm_fwd, "outnorm_fwd", o, yl, ga, gl)
    wout = ex.weight("w_out")
    mixo = run(_mm_nn, "out_proj", y, wout[None], bm=bm, bn=d)
    x2, hn2 = run(_mid_fwd, "mid_fwd", x, mixo, g_post_mix, g_pre_ffn)
    ex.alone("gather_w_up_last")
    wg3, wu3 = ex.weight("w_ffn_gate"), ex.weight("w_ffn_up")
    act_dgate, act_dup, act = run(_swiglu_fwd, "ffn_gate_up", hn2, wg3, wu3, bm=bm)
    ex.alone("gather_w_down")
    wd = ex.weight("w_ffn_down")
    ff = wd.shape[0]
    f = _mm_nn(act, wd[None], bm=bm, bn=d // 2, name="ffn_down")
    loss_cols, dout, df, dg_post_ffn = _final(f, x2, target, g_post_ffn, "final")

    dgate, dup = _swiglu_bwd(df, wd, act_dgate, act_dup, bm=min(t, 2 * MM_ROWS), bo=ff // 4, name="ffn_down_bwd")
    ex.grads["w_ffn_down"] = _mm_tn(act, df, 1, bm=bt, bk=DW_ROWS, out_dtype=ex.grad_dtype, name="ffn_down_dw").reshape(c, ff // c, d)
    ex.grads["w_ffn_gate"] = run(_mm_tn, "ffn_gate_dw", hn2, dgate, c, bm=bt, bk=d // 2, out_dtype=ex.grad_dtype)
    ex.grads["w_ffn_up"] = run(_mm_tn, "ffn_up_dw", hn2, dup, c, bm=bt, bk=d // 2, out_dtype=ex.grad_dtype)
    dhn2_g = run(_mm_nt, "ffn_gate_dx", dgate, wg3, bm=bm, bo=d // 2, out_dtype=F32)
    dhn2_u = run(_mm_nt, "ffn_up_dx", dup, wu3, bm=bm, bo=d // 2, out_dtype=F32)
    dx2, dmix, dg_pre_ffn, dg_post_mix = run(_mid_bwd, "mid_bwd", dhn2_g, dhn2_u, dout, x2, mixo, g_pre_ffn, g_post_mix)
    dy = run(_mm_nt, "out_proj_dx", dmix, wout[None], bm=bm, bo=mix, out_dtype=F32)
    ex.grads["w_out"] = _mm_tn(y, dmix, 1, bm=bt, bk=mix // 4, out_dtype=ex.grad_dtype, name="out_proj_dw").reshape(c, mix // c, d)
    do, dyl, dga, dgl_norm = run(_outnorm_bwd, "outnorm_bwd", dy, o, yl, ga, gl)
    dproj, dcw, dcb, dwr, dbr, dwi, dbi, dlam = run(_lru_bwd, "lru_bwd", proj, 3 * n_heads, n_blocks, h, kept, dyl, cw, cb, wr, wi, lam)
    small = dict(post_mix_norm=dg_post_mix, pre_ffn_norm=dg_pre_ffn, post_ffn_norm=dg_post_ffn, conv_w=dcw, conv_b=dcb,
                 w_rgate=dwr, b_rgate=dbr, w_igate=dwi, b_igate=dbi, lru_lambda=dlam, attn_out_norm=dga, lru_out_norm=dgl_norm)
    ex.packs["early"] = _pack([small[n] for n in _SMALL_EARLY])
    dproj = run(_attn_bwd, "attn_bwd", proj_mx, do, dproj, n_heads)
    ex.grads["w_in"] = _mm_tn(hn1, dproj, c, bm=bt, bk=d // 2, out_dtype=ex.grad_dtype, name="in_proj_dw")
    ex.alone("grads_w_in_swap")
    dhn1 = run(_mm_nt, "in_proj_dx", dproj, win3, bm=bm, bo=d // 2, out_dtype=F32)
    grad_x, small["pre_mix_norm"] = run(_first_bwd, "first_bwd", dhn1, dx2, x, g_pre_mix)
    ex.packs["late"] = _pack([small["pre_mix_norm"], (0.5 / d) * jnp.sum(loss_cols, keepdims=True)])
    return loss_cols, grad_x, small


def _into_slot(wsh, slot, dtype, name):
    rows, n = wsh.shape
    rb = _row_block(rows, 512) if rows % 8 == 0 else rows

    def body(s_ref, w_ref, o_ref):
        o_ref[...] = w_ref[...].astype(o_ref.dtype)

    return pl.pallas_call(
        body,
        grid_spec=pltpu.PrefetchScalarGridSpec(
            num_scalar_prefetch=1, grid=(rows // rb,),
            in_specs=[pl.BlockSpec((rb, n), lambda i, s_ref: (i, 0))],
            out_specs=pl.BlockSpec((None, rb, n), lambda i, s_ref: (s_ref[0], i, 0))),
        out_shape=S((4, rows, n), dtype), compiler_params=_cp("parallel"), name=name)(slot, wsh)


class _Exchange:
    SCHEDULE = {
        "in_proj": [("stream", "w_in"), ("ici", "conv_w"), ("ici", "w_ffn_up", 0)],
        "attn_fwd": [("d2d", "w_ffn_up", 0), ("ici", "w_ffn_gate")],
        "lru_fwd": [("d2d", "w_ffn_gate"), ("ici", "w_out"), ("ici", "w_ffn_up", 1)],
        "outnorm_fwd": [("d2d", "w_out"), ("d2d", "w_ffn_up", 1)],
        "out_proj": [("ici", "w_ffn_up", 2)],
        "mid_fwd": [("d2d", "w_ffn_up", 2), ("ici", "w_ffn_up", 3)],
        "gather_w_up_last": [("d2d", "w_ffn_up", 3)],
        "ffn_gate_up": [("ici", "w_ffn_down")],
        "gather_w_down": [("d2d", "w_ffn_down")],
        "ffn_gate_dw": [("swap", "w_ffn_down")],
        "ffn_up_dw": [("scatter", "w_ffn_down", 0), ("scatter", "w_ffn_down", 1), ("scatter", "w_ffn_down", 2), ("swap", "w_ffn_gate")],
        "ffn_gate_dx": [("scatter", "w_ffn_down", 3), ("scatter", "w_ffn_gate", 0), ("swap", "w_ffn_up")],
        "ffn_up_dx": [("share", "w_ffn_down"), ("scatter", "w_ffn_gate", 1), ("scatter", "w_ffn_gate", 2)],
        "mid_bwd": [("scatter", "w_ffn_gate", 3), ("scatter", "w_ffn_up", 0)],
        "out_proj_dx": [("share", "w_ffn_gate"), ("scatter", "w_ffn_up", 1)],
        "outnorm_bwd": [("scatter", "w_ffn_up", 2), ("swap", "w_out")],
        "lru_bwd": [("scatter", "w_ffn_up", 3), ("scatter", "w_out")],
        "attn_bwd": [("share", "w_ffn_up"), ("share", "w_out"), ("spread", "early")],
        "grads_w_in_swap": [("swap", "w_in")],
        "in_proj_dx": [("scatter", "w_in")],
        "grads_w_in_share": [("share", "w_in"), ("spread", "late")],
    }
    PIECES = 4
    grad_dtype = BF16

    def __init__(self, slots, place):
        self.buf, self.place = dict(slots), place
        self.grads, self.packs, self.swapped, self.part, self.scattered, self.full, self.spreaded = {}, {}, {}, {}, {}, {}, {}

    def weight(self, name):
        b = self.buf[name]
        return b.reshape(-1, b.shape[2]) if name in ("w_out", "w_ffn_down") else b

    def in_proj(self, x, gain, bm):
        car = self.carrier("in_proj")
        out = _in_proj_streamed(x, gain, car, car.streamed, self.place, bm=bm, name="in_proj")
        self.harvest(car)
        return out

    def conv_w(self):
        return jnp.transpose(self.buf["conv_w"], (1, 0, 2)).reshape(CONV_WIDTH, -1)

    def carrier(self, call):
        if call not in self.SCHEDULE:
            return None
        car = _Carrier()
        car.todo, slot = [], {}
        for kind, name, *piece in self.SCHEDULE[call]:
            if kind in ("ici", "d2d", "stream"):
                if name not in slot:
                    slot[name] = car.inplace(self.buf[name])
                    car.todo.append((self.buf, name, slot[name]))
            if kind == "stream":
                car.streamed = slot[name]
            elif kind in ("ici", "d2d"):
                size = self.buf[name].shape[1] // 2 // self.PIECES
                rows = (piece[0] * size, size) if piece else None
                if kind == "ici":
                    car.gather_ici(slot[name], rows, split=name != "conv_w")
                else:
                    car.gather_d2d(slot[name], rows)
            elif kind == "swap":
                g = self.grads[name]
                o = car.fresh((4, g.shape[1] // 2, g.shape[2]), g.dtype)
                car.swap(car.read(g), o)
                car.todo.append((self.swapped, name, o))
            elif kind == "scatter":
                if name not in self.part:
                    self.part[name] = _add_own_half(self.grads[name], self.swapped[name], self.place[1:], "grads_add_" + name)
                p = self.part[name]
                key = ("scatter", name)
                if key not in slot:
                    slot[key] = (car.read(p), car.inplace(self.scattered[name]) if name in self.scattered else car.fresh(p.shape, p.dtype))
                    car.todo.append((self.scattered, name, slot[key][1]))
                size = p.shape[1] // self.PIECES
                car.scatter(*slot[key], (piece[0] * size, size) if piece else None)
            elif kind == "share":
                o = car.inplace(_sum_chips(self.part[name], self.scattered[name], self.place, "grads_sum_" + name))
                car.share(o)
                car.todo.append((self.full, name, o))
            else:
                o = car.fresh((8,) + self.packs[name].shape, F32)
                car.spread(car.read(self.packs[name]), o)
                car.todo.append((self.spreaded, name, o))
        return car

    def harvest(self, car):
        for state, name, o in (car.todo if car is not None else []):
            state[name] = car.results[o]

    def alone(self, call):
        car = self.carrier(call)
        car.run_alone(call)
        self.harvest(car)

    def small_sum(self, key):
        return _sum_devices(self.packs[key], self.spreaded[key], 2 * self.place[0:1] + self.place[1:], "grads_small_sum_" + key)


def _row_block(rows, cap):
    return max(b for b in range(8, cap + 1, 8) if rows % b == 0)


def _add_own_half(g, recv, core, name):
    _, rows, n = g.shape
    half = rows // 2
    rb = _row_block(half, 1024)
    nb = half // rb

    def body(c_ref, g_ref, r_ref, o_ref):
        o_ref[...] = (g_ref[...].astype(F32) + r_ref[...].astype(F32)).astype(o_ref.dtype)

    return pl.pallas_call(
        body,
        grid_spec=pltpu.PrefetchScalarGridSpec(
            num_scalar_prefetch=1, grid=(4, nb),
            in_specs=[pl.BlockSpec((None, rb, n), lambda k, i, c_ref: (k, c_ref[0] * nb + i, 0)),
                      pl.BlockSpec((None, rb, n), lambda k, i, c_ref: (k, i, 0))],
            out_specs=pl.BlockSpec((None, rb, n), lambda k, i, c_ref: (k, i, 0))),
        out_shape=S((4, half, n), BF16), compiler_params=_cp("parallel", "parallel"), name=name)(core, g, recv)


def _sum_chips(part, recv, place, name):
    _, rows, n = part.shape
    rb = _row_block(rows, 256)
    nb = rows // rb

    def body(p_ref, own_ref, r0, r1, r2, r3, o_ref):
        own = own_ref[...].astype(F32)
        terms = [jnp.where(p_ref[0] == k, own, r[...].astype(F32)) for k, r in enumerate((r0, r1, r2, r3))]
        o_ref[...] = ((terms[0] + terms[1]) + terms[2]) + terms[3]

    def slot(k):
        return pl.BlockSpec((None, rb, n), lambda i, p_ref: (jnp.where(p_ref[0] == k, (k + 1) % 4, k), i, 0))

    return pl.pallas_call(
        body,
        grid_spec=pltpu.PrefetchScalarGridSpec(
            num_scalar_prefetch=1, grid=(nb,),
            in_specs=[pl.BlockSpec((None, rb, n), lambda i, p_ref: (p_ref[0], i, 0))] + [slot(k) for k in range(4)],
            out_specs=pl.BlockSpec((rb, n), lambda i, p_ref: (p_ref[1] * nb + i, 0))),
        out_shape=S((2 * rows, n), F32), compiler_params=_cp("parallel"), name=name)(place, part, recv, recv, recv, recv)


def _sum_devices(own, spread, me, name):
    rows = own.shape[0]

    def body(me_ref, own_ref, *refs):
        acc = None
        for k, r in enumerate(refs[:8]):
            term = jnp.where(me_ref[0] == k, own_ref[...], r[...])
            acc = term if acc is None else acc + term
        refs[8][...] = acc

    def slot(k):
        return pl.BlockSpec((None, rows, 128), lambda i, me_ref: (jnp.where(me_ref[0] == k, (k + 1) % 8, k), 0, 0))

    whole = pl.BlockSpec((rows, 128), lambda i, me_ref: (0, 0))
    return pl.pallas_call(
        body,
        grid_spec=pltpu.PrefetchScalarGridSpec(num_scalar_prefetch=1, grid=(1,), in_specs=[whole] + [slot(k) for k in range(8)],
                                               out_specs=whole),
        out_shape=S((rows, 128), F32), compiler_params=_cp("arbitrary"), name=name)(me, own, *[spread] * 8)


def _adamw(w, g, m, v, name, regive=False):
    rows, n = w.shape
    rb = rows if rows * n * 4 <= (1 << 21) else _row_block(rows, 512)
    c1 = 1.0 - ADAM_B1 ** ADAM_STEP
    c2 = 1.0 - ADAM_B2 ** ADAM_STEP

    def body(w_ref, g_ref, m_ref, v_ref, d_ref, nm_ref, nv_ref, *again):
        gv = g_ref[...]
        for ref in again:
            ref[...] = gv
        nm = ADAM_B1 * m_ref[...] + (1.0 - ADAM_B1) * gv
        nv = ADAM_B2 * v_ref[...] + (1.0 - ADAM_B2) * (gv * gv)
        nm_ref[...] = nm
        nv_ref[...] = nv
        d_ref[...] = -ADAM_LR * ((nm / c1) / (jnp.sqrt(nv / c2) + ADAM_EPS) + ADAM_WD * w_ref[...])

    bs = pl.BlockSpec((rb, n), lambda i: (i, 0))
    n_out = 4 if regive else 3
    return pl.pallas_call(body, grid=(rows // rb,), in_specs=[bs] * 4, out_specs=[bs] * n_out, out_shape=[S((rows, n), F32)] * n_out,
                          compiler_params=_cp("parallel"), name=name)(w, g, m, v)


_BIG = ("w_in", "w_out", "w_ffn_gate", "w_ffn_up", "w_ffn_down")
_SMALL = ("pre_mix_norm", "post_mix_norm", "pre_ffn_norm", "post_ffn_norm", "conv_w", "conv_b", "w_rgate", "b_rgate",
          "w_igate", "b_igate", "lru_lambda", "attn_out_norm", "lru_out_norm")
_SMALL_EARLY = _SMALL[1:]
_WEIGHTS = ("pre_mix_norm", "post_mix_norm", "pre_ffn_norm", "post_ffn_norm", "w_in", "conv_w", "conv_b", "w_rgate", "b_rgate",
            "w_igate", "b_igate", "lru_lambda", "attn_out_norm", "lru_out_norm", "w_out", "w_ffn_gate", "w_ffn_up", "w_ffn_down")


def _pack(arrays):
    flat = []
    for a in arrays:
        f = a.reshape(-1)
        flat.append(jnp.pad(f, (0, (-f.shape[0]) % 1024)))
    return jnp.concatenate(flat).reshape(-1, 128)


def _unpack(packed, shapes):
    out, pos = [], 0
    flat = packed.reshape(-1)
    for s in shapes:
        size = math.prod(s)
        out.append(flat[pos:pos + size].reshape(s))
        pos += size + (-size) % 1024
    return out


def kernel(x, pre_mix_norm, post_mix_norm, pre_ffn_norm, post_ffn_norm, w_in, conv_w, conv_b, w_rgate, b_rgate, w_igate, b_igate, lru_lambda, attn_out_norm, lru_out_norm, w_out, w_ffn_gate, w_ffn_up, w_ffn_down, loss_target, m_pre_mix_norm, m_post_mix_norm, m_pre_ffn_norm, m_post_ffn_norm, m_w_in, m_conv_w, m_conv_b, m_w_rgate, m_b_rgate, m_w_igate, m_b_igate, m_lru_lambda, m_attn_out_norm, m_lru_out_norm, m_w_out, m_w_ffn_gate, m_w_ffn_up, m_w_ffn_down, v_pre_mix_norm, v_post_mix_norm, v_pre_ffn_norm, v_post_ffn_norm, v_w_in, v_conv_w, v_conv_b, v_w_rgate, v_b_rgate, v_w_igate, v_b_igate, v_lru_lambda, v_attn_out_norm, v_lru_out_norm, v_w_out, v_w_ffn_gate, v_w_ffn_up, v_w_ffn_down):
    given = dict(locals())
    w = {n: given[n][0] for n in _WEIGHTS}
    m = {n: given["m_" + n][0] for n in _WEIGHTS}
    v = {n: given["v_" + n][0] for n in _WEIGHTS}
    xs, target = x[0], loss_target[0]
    d = xs.shape[1]
    chip = (2 * lax.axis_index("x") + lax.axis_index("y")).astype(jnp.int32)
    place = jnp.stack([chip, lax.axis_index("c").astype(jnp.int32)])

    slots = {n: _into_slot(w[n], place[0:1], _MXU, "slot_" + n) for n in _BIG}
    slots["conv_w"] = _into_slot(w["conv_w"], place[0:1], F32, "slot_conv_w")
    ex = _Exchange(slots, place)
    row = lambda a: a.reshape(1, -1)
    norms = tuple(row(w[n]) for n in ("pre_mix_norm", "post_mix_norm", "pre_ffn_norm", "post_ffn_norm"))

    loss_cols, grad_x, small = _local_step(
        xs, target, norms, ex, row(w["conv_b"]), w["w_rgate"], row(w["b_rgate"]),
        w["w_igate"], row(w["b_igate"]), row(w["lru_lambda"]), row(w["attn_out_norm"]), row(w["lru_out_norm"]))


    ex.alone("grads_w_in_share")
    reduced = {n: ex.full[n] for n in _BIG}
    early = _unpack(ex.small_sum("early"), [small[n].shape for n in _SMALL_EARLY])
    late = _unpack(ex.small_sum("late"), [small["pre_mix_norm"].shape, (1, 1)])
    loss = late[1][0, 0]
    for n, g in zip(_SMALL_EARLY + ("pre_mix_norm",), early + late[:1]):
        reduced[n] = g.reshape(w[n].shape) if n != "conv_w" else lax.dynamic_slice_in_dim(g, chip * w[n].shape[1], w[n].shape[1], axis=1)

    delta, new_m, new_v = {}, {}, {}
    for n in _BIG:
        delta[n], new_m[n], new_v[n], reduced[n] = _adamw(w[n], reduced[n], m[n], v[n], "adamw_" + n, regive=True)
    shapes = [w[n].shape for n in _SMALL]
    packed = _adamw(*[_pack([src[n] for n in _SMALL]) for src in (w, reduced, m, v)], "adamw_small")
    for out, p in zip((delta, new_m, new_v), packed):
        out.update(zip(_SMALL, _unpack(p, shapes)))

    lead = lambda a: a[None]
    return (loss, lead(grad_x), *[lead(reduced[n]) for n in _WEIGHTS], *[lead(delta[n]) for n in _WEIGHTS],
            *[lead(new_m[n]) for n in _WEIGHTS], *[lead(new_v[n]) for n in _WEIGHTS])
```

```python
import functools
import math

import jax
import jax.numpy as jnp
from jax import lax
from jax.experimental import pallas as pl
from jax.experimental.pallas import tpu as pltpu

F32 = jnp.float32
BF16 = jnp.bfloat16
_MXU = BF16
S = jax.ShapeDtypeStruct

RMS_EPS = 1e-6
HEAD_DIM = 128
CONV_WIDTH = 4
LRU_C = 8.0
ADAM_LR, ADAM_B1, ADAM_B2, ADAM_EPS, ADAM_WD, ADAM_STEP = 0.001, 0.9, 0.999, 1e-08, 0.01, 10
EXP_CUT = -105.0
VMEM_LIMIT = 60 * 1024 * 1024
ROW_TILE = 512
SEQ_TILE = 256
ATTN_BLOCK = 256
ATTN_HEADS = 2
ATTN_ROWS = 128
MM_ROWS = 512
DW_TOKENS = 4096
DW_ROWS = 512
MESH = pl.DeviceIdType.MESH


def _cp(*sem):
    return pltpu.CompilerParams(dimension_semantics=sem, vmem_limit_bytes=VMEM_LIMIT)


def _dot(a, b):
    return jnp.dot(a, b, preferred_element_type=F32)


def _dot_nt(a, b):
    return lax.dot_general(a, b, (((1,), (1,)), ((), ())), preferred_element_type=F32)


def _dot_tn(a, b):
    return lax.dot_general(a, b, (((0,), (0,)), ((), ())), preferred_element_type=F32)


def _rstd(v):
    return lax.rsqrt(jnp.mean(v * v, axis=-1, keepdims=True) + RMS_EPS)


def _rms_bwd(dn, vh, r, gain):
    dvh = dn * gain
    dv = r * (dvh - vh * jnp.mean(dvh * vh, axis=-1, keepdims=True))
    return dv, jnp.sum(dn * vh, axis=0, keepdims=True)


def _log_sigmoid(z):
    return jnp.minimum(z, 0.0) - jnp.log(1.0 + jnp.exp(-jnp.abs(z)))


def _expm1(v):
    small = v * (1.0 + v * (0.5 + v * (1.0 / 6.0 + v * (1.0 / 24.0 + v * (1.0 / 120.0)))))
    return jnp.where(jnp.abs(v) < 0.04, small, jnp.exp(v) - 1.0)


_GELU_C = math.sqrt(2.0 / math.pi)


def _gelu(v):
    return 0.5 * v * (1.0 + jnp.tanh(_GELU_C * (v + 0.044715 * v * v * v)))


def _gelu_grad(v):
    th = jnp.tanh(_GELU_C * (v + 0.044715 * v * v * v))
    return 0.5 * (1.0 + th) + 0.5 * v * (1.0 - th * th) * _GELU_C * (1.0 + 3.0 * 0.044715 * v * v)


def _row_spec(tm, d):
    return pl.BlockSpec((tm, d), lambda i: (i, 0))


def _vec_spec(d):
    return pl.BlockSpec((1, d), lambda i: (0, 0))


_ANY = pl.BlockSpec(memory_space=pl.ANY)


def _place():
    x, y, c = lax.axis_index("x"), lax.axis_index("y"), lax.axis_index("c")
    return x, y, c, [(1 - x, y), (x, 1 - y), (1 - x, 1 - y)]


def _remote(src, dst, send_sem, recv_sem, to):
    return pltpu.make_async_remote_copy(src_ref=src, dst_ref=dst, send_sem=send_sem, recv_sem=recv_sem,
                                        device_id=to, device_id_type=MESH)


class _Carrier:
    def __init__(self):
        self.inputs, self.out_shapes, self.aliases, self.ops, self.n_sems, self.results = [], [], {}, [], 0, None

    def inplace(self, arr):
        self.aliases[len(self.inputs)] = len(self.out_shapes)
        self.inputs.append(arr)
        self.out_shapes.append(S(arr.shape, arr.dtype))
        return len(self.out_shapes) - 1

    def read(self, arr):
        self.inputs.append(arr)
        return len(self.inputs) - 1

    def fresh(self, shape, dtype):
        self.out_shapes.append(S(shape, dtype))
        return len(self.out_shapes) - 1

    def _add(self, n_sems, copies):
        base = self.n_sems
        self.n_sems += n_sems

        def start(ins, outs, send, recv):
            for k, (src, dst, _, to) in enumerate(copies(ins, outs)):
                _remote(src, dst, send.at[base + k], recv.at[base + k], to).start()

        def finish(ins, outs, send, recv):
            for k, (src, _, land, to) in enumerate(copies(ins, outs)):
                _remote(src, land, send.at[base + k], recv.at[base + k], to).wait()

        self.ops.append((start, finish))

    def gather_ici(self, o, rows=None, split=True):
        half = self.out_shapes[o].shape[1] // 2
        lo, size = rows or (0, half)

        def copies(ins, outs):
            x, y, c, chips = _place()
            part = (lambda ref: ref.at[pl.ds(c * half + lo, size)]) if split else (lambda ref: ref)
            mine = part(outs[o].at[2 * x + y])
            return [(mine, mine, part(outs[o].at[2 * px + py]), (px, py, c)) for px, py in chips]

        self._add(3, copies)

    def gather_d2d(self, o, rows=None):
        half = self.out_shapes[o].shape[1] // 2
        lo, size = rows or (0, half)

        def copies(ins, outs):
            x, y, c, chips = _place()
            at = lambda k, cc: outs[o].at[k].at[pl.ds(cc * half + lo, size)]
            return [(at(2 * px + py, c), at(2 * px + py, c), at(2 * px + py, 1 - c), (x, y, 1 - c)) for px, py in chips]

        self._add(3, copies)

    def swap(self, i, o):
        half = self.inputs[i].shape[1] // 2

        def copies(ins, outs):
            x, y, c, _ = _place()
            return [(ins[i].at[:, pl.ds((1 - c) * half, half)], outs[o], outs[o], (x, y, 1 - c))]

        self._add(1, copies)

    def scatter(self, i, o, rows=None):
        lo, size = rows or (0, self.inputs[i].shape[1])

        def copies(ins, outs):
            x, y, c, chips = _place()
            cut = lambda ref: ref.at[pl.ds(lo, size)]
            return [(cut(ins[i].at[2 * px + py]), cut(outs[o].at[2 * x + y]), cut(outs[o].at[2 * px + py]), (px, py, c)) for px, py in chips]

        self._add(3, copies)

    def share(self, o):
        r = self.out_shapes[o].shape[0] // 2

        def copies(ins, outs):
            x, y, c, _ = _place()
            mine = outs[o].at[pl.ds(c * r, r)]
            return [(mine, mine, outs[o].at[pl.ds((1 - c) * r, r)], (x, y, 1 - c))]

        self._add(1, copies)

    def spread(self, i, o):
        def copies(ins, outs):
            x, y, c, _ = _place()
            me = 4 * x + 2 * y + c
            out = []
            for d in range(1, 8):
                to, frm = (me + d) % 8, (me + 8 - d) % 8
                out.append((ins[i], outs[o].at[me], outs[o].at[frm], (to // 4, (to // 2) % 2, to % 2)))
            return out

        self._add(7, copies)

    def _pallas(self, body, n_in, n_out, scratch, **kw):
        k_in, k_out = len(self.inputs), len(self.out_shapes)
        grid = kw.get("grid", ())

        def wrapped(*refs):
            ins, cins = refs[:n_in], refs[n_in:n_in + k_in]
            outs = refs[n_in + k_in:n_in + k_in + n_out]
            couts = refs[n_in + k_in + n_out:n_in + k_in + n_out + k_out]
            own = refs[n_in + k_in + n_out + k_out:]
            send, recv = own[len(scratch):]
            ids = [pl.program_id(a) for a in range(len(grid))]
            first = functools.reduce(jnp.logical_and, [a == 0 for a in ids], True)
            last = functools.reduce(jnp.logical_and, [a == g - 1 for a, g in zip(ids, grid)], True)

            def go(stage):
                for op in self.ops:
                    op[stage](cins, couts, send, recv)

            if grid:
                pl.when(first)(lambda: go(0))
                body(*ins, *outs, *own[:len(scratch)])
                pl.when(last)(lambda: go(1))
            else:
                go(0)
                go(1)

        sem = pltpu.SemaphoreType.DMA((self.n_sems,))
        return pl.pallas_call(
            wrapped, in_specs=list(kw.get("in_specs", [])) + [_ANY] * k_in, out_specs=list(kw.get("out_specs", [])) + [_ANY] * k_out,
            out_shape=list(kw.get("out_shape", [])) + self.out_shapes, scratch_shapes=list(scratch) + [sem, sem],
            input_output_aliases={**kw.get("aliases", {}), **{n_in + i: n_out + o for i, o in self.aliases.items()}}, name=kw["name"],
            **({"grid": grid, "compiler_params": _cp(*["arbitrary"] * len(grid))} if grid else {}))

    def run(self, body, kw, *args):
        single = not isinstance(kw["out_shape"], (list, tuple))
        out_shape = [kw["out_shape"]] if single else list(kw["out_shape"])
        out_specs = [kw["out_specs"]] if single else list(kw["out_specs"])
        res = self._pallas(body, len(args), len(out_shape), kw.get("scratch_shapes", []), grid=kw["grid"], in_specs=kw["in_specs"],
                           out_specs=out_specs, out_shape=out_shape, name=kw["name"],
                           aliases=kw.get("input_output_aliases", {}))(*args, *self.inputs)
        self.results = list(res[len(out_shape):])
        return res[0] if single else list(res[:len(out_shape)])

    def run_alone(self, name):
        self.results = list(self._pallas(None, 0, 0, [], name=name)(*self.inputs))


def _call(comm, body, **kw):
    if comm is None:
        return pl.pallas_call(body, **kw)
    return functools.partial(comm.run, body, kw)


def _in_proj_streamed(x, gain, car, o_w, place, *, bm, name):
    m, k = x.shape
    n = car.out_shapes[o_w].shape[2]
    ni, half = m // bm, k // 2
    k_in, k_out = len(car.inputs), len(car.out_shapes)
    order = lambda p: ((p & 1) << 1) | (p >> 1)

    def body(place_ref, x_ref, g_ref, *refs):
        cins, (hn_ref, o_ref, ob_ref), couts = refs[:k_in], refs[k_in:k_in + 3], refs[k_in + 3:k_in + 3 + k_out]
        wbuf, hn_all, local, ici_send, ici_recv, d2d_send, d2d_recv, send, recv = refs[k_in + 3 + k_out:]
        p, i = pl.program_id(0), pl.program_id(1)
        x, y, c, chips = _place()
        me = 2 * x + y
        rows = lambda chunk, cc: couts[o_w].at[chunk].at[pl.ds(cc * half, half)]

        @pl.when(jnp.logical_and(p == 0, i == 0))
        def _():
            for j, (px, py) in enumerate(chips):
                _remote(rows(me, c), rows(me, c), ici_send.at[j], ici_recv.at[j], (px, py, c)).start()
            for op in car.ops:
                op[0](cins, couts, send, recv)

        for j, (px, py) in enumerate(chips):
            @pl.when(jnp.logical_and(p == j + 1, i == 0))
            def _(j=j, px=px, py=py):
                landed, other = rows(2 * px + py, c), rows(2 * px + py, 1 - c)
                _remote(landed, landed, ici_send.at[j], ici_recv.at[j], (px, py, c)).wait_recv()
                _remote(landed, landed, d2d_send.at[j], d2d_recv.at[j], (x, y, 1 - c)).start()
                _remote(other, other, d2d_send.at[j], d2d_recv.at[j], (x, y, 1 - c)).wait_recv()

        @pl.when(i == 0)
        def _():
            cp = pltpu.make_async_copy(couts[o_w].at[me ^ order(p)], wbuf, local.at[0])
            cp.start()
            cp.wait()

        tile = pl.ds(pl.multiple_of(i * bm, bm), bm)

        @pl.when(p == 0)
        def _():
            xv = x_ref[...]
            hn_all[tile, :] = ((xv * _rstd(xv)) * g_ref[...]).astype(_MXU)

        hn = hn_all[tile, :]
        hn_ref[...] = hn
        res = _dot(hn, wbuf[...])
        o_ref[...] = res
        ob_ref[...] = res.astype(ob_ref.dtype)

        @pl.when(jnp.logical_and(p == 3, i == ni - 1))
        def _():
            for j, (px, py) in enumerate(chips):
                _remote(rows(me, c), rows(me, c), ici_send.at[j], ici_recv.at[j], (px, py, c)).wait_send()
                _remote(rows(me, c), rows(me, c), d2d_send.at[j], d2d_recv.at[j], (x, y, 1 - c)).wait_send()
            for op in car.ops:
                op[1](cins, couts, send, recv)

    ospec = pl.BlockSpec((bm, n), lambda p, i, place_ref: (i, place_ref[0] ^ order(p)))
    rows = pl.BlockSpec((bm, k), lambda p, i, place_ref: (jnp.where(p == 0, i, 0), 0))
    three, sems = pltpu.SemaphoreType.DMA((3,)), pltpu.SemaphoreType.DMA((max(car.n_sems, 1),))
    res = pl.pallas_call(
        body,
        grid_spec=pltpu.PrefetchScalarGridSpec(
            num_scalar_prefetch=1, grid=(4, ni),
            in_specs=[rows, pl.BlockSpec((1, k), lambda p, i, place_ref: (0, 0))] + [_ANY] * k_in,
            out_specs=[pl.BlockSpec((bm, k), lambda p, i, place_ref: (p * ni + i, 0)), ospec, ospec] + [_ANY] * k_out,
            scratch_shapes=[pltpu.VMEM((k, n), _MXU), pltpu.VMEM((m, k), _MXU), pltpu.SemaphoreType.DMA((1,)),
                            three, three, three, three, sems, sems]),
        out_shape=[S((4 * m, k), _MXU), S((m, 4 * n), F32), S((m, 4 * n), _MXU)] + car.out_shapes,
        input_output_aliases={3 + a: 3 + o for a, o in car.aliases.items()},
        compiler_params=_cp("arbitrary", "arbitrary"), name=name)(place, x, gain, *car.inputs)
    car.results = list(res[3:])
    return res[0], res[1], res[2]


def _mm_nn(a, b3, *, bm, bn, name, also=None, comm=None):
    m, k = a.shape
    c, _, n = b3.shape
    ni, nj = m // bm, n // bn

    def body(a_ref, b_ref, *o_refs):
        res = _dot(a_ref[...], b_ref[...])
        for o_ref in o_refs:
            o_ref[...] = res.astype(o_ref.dtype)

    ospec = pl.BlockSpec((bm, bn), lambda cc, j, i: (i, cc * nj + j))
    dtypes = [F32] + ([] if also is None else [also])
    out = _call(
        comm, body, grid=(c, nj, ni),
        in_specs=[pl.BlockSpec((bm, k), lambda cc, j, i: (i, 0)), pl.BlockSpec((None, k, bn), lambda cc, j, i: (cc, 0, j))],
        out_specs=[ospec] * len(dtypes), out_shape=[S((m, c * n), dt) for dt in dtypes],
        compiler_params=_cp("parallel", "parallel", "parallel"), name=name)(a, b3)
    return out[0] if also is None else out


def _mm_nt(a, b3, *, bm, bo, out_dtype, name, comm=None):
    m = a.shape[0]
    c, ko, n = b3.shape
    ni, nj = m // bm, ko // bo

    def body(a_ref, b_ref, o_ref):
        acc = _dot_nt(a_ref[:, 0:n], b_ref[0])
        for cc in range(1, c):
            acc = acc + _dot_nt(a_ref[:, cc * n:(cc + 1) * n], b_ref[cc])
        o_ref[...] = acc.astype(o_ref.dtype)

    return _call(
        comm, body, grid=(nj, ni),
        in_specs=[pl.BlockSpec((bm, c * n), lambda j, i: (i, 0)),
                  pl.BlockSpec((c, bo, n), lambda j, i: (0, j, 0))],
        out_specs=pl.BlockSpec((bm, bo), lambda j, i: (i, j)),
        out_shape=S((m, ko), out_dtype),
        compiler_params=_cp("parallel", "parallel"), name=name)(a, b3)


def _mm_tn(a, b, c, *, bm, bk, out_dtype, name, comm=None):
    m, k = b.shape[0], a.shape[1]
    n = b.shape[1] // c
    nm, nk = m // bm, k // bk

    def body(a_ref, b_ref, o_ref, *acc):
        if nm == 1:
            o_ref[...] = _dot_tn(a_ref[...], b_ref[...]).astype(o_ref.dtype)
            return
        mm = pl.program_id(2)

        @pl.when(mm == 0)
        def _():
            acc[0][...] = jnp.zeros_like(acc[0])

        acc[0][...] += _dot_tn(a_ref[...], b_ref[...])

        @pl.when(mm == nm - 1)
        def _():
            o_ref[...] = acc[0][...].astype(o_ref.dtype)

    return _call(
        comm, body, grid=(c, nk, nm),
        in_specs=[pl.BlockSpec((bm, bk), lambda cc, j, mm: (mm, j)),
                  pl.BlockSpec((bm, n), lambda cc, j, mm: (mm, cc))],
        out_specs=pl.BlockSpec((None, bk, n), lambda cc, j, mm: (cc, j, 0)),
        out_shape=S((c, k, n), out_dtype),
        scratch_shapes=[] if nm == 1 else [pltpu.VMEM((bk, n), F32)],
        compiler_params=_cp("parallel", "parallel", "arbitrary"), name=name)(a, b)


def _swiglu_fwd(hn, wg3, wu3, *, bm, name, comm=None):
    m, k = hn.shape
    c, _, n = wg3.shape

    def body(a_ref, g_ref, u_ref, dgate_ref, dup_ref, act_ref):
        a = a_ref[...]
        gate = _dot(a, g_ref[...])
        up = _dot(a, u_ref[...])
        sg = jax.nn.sigmoid(gate)
        silu = gate * sg
        dgate_ref[...] = (up * (sg * (1.0 + gate * (1.0 - sg)))).astype(dgate_ref.dtype)
        dup_ref[...] = silu.astype(dup_ref.dtype)
        act_ref[...] = (silu * up).astype(act_ref.dtype)

    wspec = pl.BlockSpec((None, k, n), lambda cc, i: (cc, 0, 0))
    ospec = pl.BlockSpec((bm, n), lambda cc, i: (i, cc))
    return _call(
        comm, body, grid=(c, m // bm),
        in_specs=[pl.BlockSpec((bm, k), lambda cc, i: (i, 0)), wspec, wspec],
        out_specs=[ospec, ospec, ospec],
        out_shape=[S((m, c * n), _MXU), S((m, c * n), _MXU), S((m, c * n), _MXU)],
        compiler_params=_cp("parallel", "parallel"), name=name)(hn, wg3, wu3)


def _swiglu_bwd(df, wd, act_dgate, act_dup, *, bm, bo, name):
    m, k = df.shape
    ko = wd.shape[0]

    def body(a_ref, b_ref, g_ref, u_ref, dg_ref, du_ref):
        dact = _dot_nt(a_ref[...], b_ref[...])
        dg_ref[...] = (dact * g_ref[...].astype(F32)).astype(dg_ref.dtype)
        du_ref[...] = (dact * u_ref[...].astype(F32)).astype(du_ref.dtype)

    ospec = pl.BlockSpec((bm, bo), lambda j, i: (i, j))
    return pl.pallas_call(
        body, grid=(ko // bo, m // bm),
        in_specs=[pl.BlockSpec((bm, k), lambda j, i: (i, 0)), pl.BlockSpec((bo, k), lambda j, i: (j, 0)), ospec, ospec],
        out_specs=[ospec, ospec],
        out_shape=[S((m, ko), _MXU), S((m, ko), _MXU)],
        compiler_params=_cp("parallel", "parallel"), name=name)(df, wd, act_dgate, act_dup)


def _rms_fwd(x, gain, name):
    t, d = x.shape
    tm = min(t, ROW_TILE)

    def body(x_ref, g_ref, o_ref):
        xv = x_ref[...]
        o_ref[...] = ((xv * _rstd(xv)) * g_ref[...]).astype(o_ref.dtype)

    return pl.pallas_call(body, grid=(t // tm,), in_specs=[_row_spec(tm, d), _vec_spec(d)], out_specs=_row_spec(tm, d),
                          out_shape=S((t, d), _MXU), compiler_params=_cp("parallel"), name=name)(x, gain)


def _outnorm_fwd(o, yl, ga, gl, name, comm=None):
    t, w = o.shape
    tm = min(t, ROW_TILE)

    def body(o_ref, l_ref, ga_ref, gl_ref, y_ref):
        ov, lv = o_ref[...], l_ref[...]
        y_ref[:, :w] = ((ov * _rstd(ov)) * ga_ref[...]).astype(y_ref.dtype)
        y_ref[:, w:] = ((lv * _rstd(lv)) * gl_ref[...]).astype(y_ref.dtype)

    return _call(comm, body, grid=(t // tm,), in_specs=[_row_spec(tm, w), _row_spec(tm, w), _vec_spec(w), _vec_spec(w)],
                 out_specs=_row_spec(tm, 2 * w), out_shape=S((t, 2 * w), _MXU),
                 compiler_params=_cp("parallel"), name=name)(o, yl, ga, gl)


def _mid_fwd(x, mix, g_post, g_pre, name, comm=None):
    t, d = x.shape
    tm = min(t, ROW_TILE)

    def body(x_ref, m_ref, gp_ref, gn_ref, x2_ref, hn_ref):
        mv = m_ref[...]
        x2 = x_ref[...] + (mv * _rstd(mv)) * gp_ref[...]
        x2_ref[...] = x2
        hn_ref[...] = ((x2 * _rstd(x2)) * gn_ref[...]).astype(hn_ref.dtype)

    return _call(comm, body, grid=(t // tm,), in_specs=[_row_spec(tm, d), _row_spec(tm, d), _vec_spec(d), _vec_spec(d)],
                          out_specs=[_row_spec(tm, d), _row_spec(tm, d)], out_shape=[S((t, d), F32), S((t, d), _MXU)],
                          compiler_params=_cp("parallel"), name=name)(x, mix, g_post, g_pre)


def _final(f, x2, target, g_post, name):
    t, d = f.shape
    tm = min(t, ROW_TILE // 2)

    def body(f_ref, x2_ref, t_ref, g_ref, loss_ref, dout_ref, df_ref, dg_ref):
        @pl.when(pl.program_id(0) == 0)
        def _():
            loss_ref[...] = jnp.zeros_like(loss_ref)
            dg_ref[...] = jnp.zeros_like(dg_ref)

        fv = f_ref[...]
        r = _rstd(fv)
        fh = fv * r
        err = (x2_ref[...] + fh * g_ref[...]) - t_ref[...]
        loss_ref[...] += jnp.sum(err * err, axis=0, keepdims=True)
        dout = err * (1.0 / d)
        dout_ref[...] = dout
        dfv, dg = _rms_bwd(dout, fh, r, g_ref[...])
        df_ref[...] = dfv.astype(df_ref.dtype)
        dg_ref[...] += dg

    return pl.pallas_call(
        body, grid=(t // tm,),
        in_specs=[_row_spec(tm, d), _row_spec(tm, d), _row_spec(tm, d), _vec_spec(d)],
        out_specs=[_vec_spec(d), _row_spec(tm, d), _row_spec(tm, d), _vec_spec(d)],
        out_shape=[S((1, d), F32), S((t, d), F32), S((t, d), _MXU), S((1, d), F32)],
        compiler_params=_cp("arbitrary"), name=name)(f, x2, target, g_post)


def _mid_bwd(dhn_a, dhn_b, dout, x2, mix, g_pre, g_post, name, comm=None):
    t, d = x2.shape
    tm = min(t, ROW_TILE // 2)

    def body(da_ref, db_ref, do_ref, x2_ref, m_ref, gn_ref, gp_ref, dx2_ref, dm_ref, dgn_ref, dgp_ref):
        @pl.when(pl.program_id(0) == 0)
        def _():
            dgn_ref[...] = jnp.zeros_like(dgn_ref)
            dgp_ref[...] = jnp.zeros_like(dgp_ref)

        x2 = x2_ref[...]
        r = _rstd(x2)
        dxa, dgn = _rms_bwd(da_ref[...] + db_ref[...], x2 * r, r, gn_ref[...])
        dx2 = do_ref[...] + dxa
        dx2_ref[...] = dx2
        dgn_ref[...] += dgn
        mv = m_ref[...]
        rm = _rstd(mv)
        dmv, dgp = _rms_bwd(dx2, mv * rm, rm, gp_ref[...])
        dm_ref[...] = dmv.astype(dm_ref.dtype)
        dgp_ref[...] += dgp

    rs, vs = _row_spec(tm, d), _vec_spec(d)
    return _call(
        comm, body, grid=(t // tm,), in_specs=[rs, rs, rs, rs, rs, vs, vs], out_specs=[rs, rs, vs, vs],
        out_shape=[S((t, d), F32), S((t, d), _MXU), S((1, d), F32), S((1, d), F32)],
        compiler_params=_cp("arbitrary"), name=name)(dhn_a, dhn_b, dout, x2, mix, g_pre, g_post)


def _first_bwd(dhn, dx2, x, gain, name, comm=None):
    t, d = x.shape
    tm = min(t, ROW_TILE)

    def body(dh_ref, dx2_ref, x_ref, g_ref, dx_ref, dg_ref):
        @pl.when(pl.program_id(0) == 0)
        def _():
            dg_ref[...] = jnp.zeros_like(dg_ref)

        xv = x_ref[...]
        r = _rstd(xv)
        dxa, dg = _rms_bwd(dh_ref[...], xv * r, r, g_ref[...])
        dx_ref[...] = dx2_ref[...] + dxa
        dg_ref[...] += dg

    rs, vs = _row_spec(tm, d), _vec_spec(d)
    return _call(comm, body, grid=(t // tm,), in_specs=[rs, rs, rs, vs], out_specs=[rs, vs],
                          out_shape=[S((t, d), F32), S((1, d), F32)], compiler_params=_cp("arbitrary"), name=name)(dhn, dx2, x, gain)


def _outnorm_bwd(dy, o, yl, ga, gl, name, comm=None):
    t, w = o.shape
    tm = min(t, ROW_TILE)

    def body(dy_ref, o_ref, l_ref, ga_ref, gl_ref, do_ref, dl_ref, dga_ref, dgl_ref):
        @pl.when(pl.program_id(0) == 0)
        def _():
            dga_ref[...] = jnp.zeros_like(dga_ref)
            dgl_ref[...] = jnp.zeros_like(dgl_ref)

        ov, lv = o_ref[...], l_ref[...]
        ra, rl = _rstd(ov), _rstd(lv)
        dov, dga = _rms_bwd(dy_ref[:, :w], ov * ra, ra, ga_ref[...])
        dlv, dgl = _rms_bwd(dy_ref[:, w:], lv * rl, rl, gl_ref[...])
        do_ref[...] = dov.astype(do_ref.dtype)
        dl_ref[...] = dlv
        dga_ref[...] += dga
        dgl_ref[...] += dgl

    rs, vs = _row_spec(tm, w), _vec_spec(w)
    return _call(comm, body, grid=(t // tm,), in_specs=[_row_spec(tm, 2 * w), rs, rs, vs, vs], out_specs=[rs, rs, vs, vs],
                          out_shape=[S((t, w), _MXU), S((t, w), F32), S((1, w), F32), S((1, w), F32)],
                          compiler_params=_cp("arbitrary"), name=name)(dy, o, yl, ga, gl)


def _tri_sum(v, tri):
    return _dot(v.astype(_MXU), tri)


def _attn_tile(qb, kb, row, col, shift, scale):
    z = _dot_nt(qb, kb) * scale
    mask = (col + shift) < row
    lb = _log_sigmoid(z)
    lm = jnp.where(mask, lb - z, 0.0)
    return mask, lb, lm


def _attn_fwd(proj, n_heads, name, comm=None):
    t = proj.shape[0]
    bq = min(t, ATTN_BLOCK)
    nq = t // bq
    scale = 1.0 / math.sqrt(HEAD_DIM)

    ar = min(bq, ATTN_ROWS)
    strips = [(slice(a * HEAD_DIM, (a + 1) * HEAD_DIM), off) for a in range(ATTN_HEADS) for off in range(0, bq, ar)]

    def body(q_ref, k_ref, v_ref, o_ref):
        tri = (lax.broadcasted_iota(jnp.int32, (bq, bq), 0) > lax.broadcasted_iota(jnp.int32, (bq, bq), 1)).astype(_MXU)
        row = lax.broadcasted_iota(jnp.int32, (ar, bq), 0)
        col = lax.broadcasted_iota(jnp.int32, (ar, bq), 1)

        def per_q(qi, _):
            q0 = pl.multiple_of(qi * bq, bq)
            rows = [pl.ds(pl.multiple_of(q0 + off, ar), ar) for _, off in strips]
            qbs = [q_ref[r, hd] for r, (hd, _) in zip(rows, strips)]

            def cond(st):
                return jnp.logical_and(st[0] >= 0, st[1])

            def step(st):
                kj, _, carries, accs = st
                k0 = pl.multiple_of(kj * bq, bq)
                alive, new_carries, new_accs = None, [], []
                for (hd, off), qb, carry, acc in zip(strips, qbs, carries, accs):
                    mask, lb, lm = _attn_tile(qb, k_ref[pl.ds(k0, bq), hd], row + off, col, (kj - qi) * bq, scale)
                    w = jnp.where(mask, jnp.exp(lb + _tri_sum(lm, tri) + carry), 0.0)
                    new_accs.append(acc + _dot(w.astype(_MXU), v_ref[pl.ds(k0, bq), hd]))
                    carry = carry + jnp.sum(lm, axis=1, keepdims=True)
                    new_carries.append(carry)
                    live = jnp.max(carry) > EXP_CUT
                    alive = live if alive is None else jnp.logical_or(alive, live)
                return kj - 1, alive, tuple(new_carries), tuple(new_accs)

            st = lax.while_loop(cond, step, (qi, jnp.bool_(True), (jnp.zeros((ar, 1), F32),) * len(strips),
                                             (jnp.zeros((ar, HEAD_DIM), F32),) * len(strips)))
            for r, (hd, _), acc in zip(rows, strips, st[3]):
                o_ref[r, hd] = acc
            return 0

        lax.fori_loop(0, nq, per_q, 0)

    groups = n_heads // ATTN_HEADS
    hs = lambda off: pl.BlockSpec((t, ATTN_HEADS * HEAD_DIM), lambda h: (0, off + h))
    return _call(
        comm, body, grid=(groups,), in_specs=[hs(0), hs(groups), hs(2 * groups)], out_specs=hs(0),
        out_shape=S((t, n_heads * HEAD_DIM), F32), compiler_params=_cp("parallel"), name=name)(proj, proj, proj)


def _emit(blocks, out_ref, starts, sems):
    copies = [pltpu.make_async_copy(b, out_ref.at[:, pl.ds(c0, b.shape[1])], sems.at[k]) for k, (b, c0) in enumerate(zip(blocks, starts))]
    for cp in copies:
        cp.start()
    for cp in copies:
        cp.wait()


def _attn_bwd(proj, do, dproj, n_heads, name, comm=None):
    t = proj.shape[0]
    bq = min(t, ATTN_BLOCK)
    nq = t // bq
    scale = 1.0 / math.sqrt(HEAD_DIM)
    groups = n_heads // ATTN_HEADS
    wide = ATTN_HEADS * HEAD_DIM

    ar = min(bq, ATTN_ROWS)
    strips = [(a, slice(a * HEAD_DIM, (a + 1) * HEAD_DIM), off) for a in range(ATTN_HEADS) for off in range(0, bq, ar)]

    def body(q_ref, k_ref, v_ref, do_ref, _, dproj_ref, dka_ref, dva_ref, g_ref, b_ref, dq_ref, dk_ref, dv_ref, out_sems):
        group = pl.program_id(0)
        dka_ref[...] = jnp.zeros_like(dka_ref)
        dva_ref[...] = jnp.zeros_like(dva_ref)
        row2 = lax.broadcasted_iota(jnp.int32, (bq, bq), 0)
        col2 = lax.broadcasted_iota(jnp.int32, (bq, bq), 1)
        tri = (row2 > col2).astype(_MXU)
        tri_lt = (row2 < col2).astype(_MXU)
        row = lax.broadcasted_iota(jnp.int32, (ar, bq), 0)
        col = lax.broadcasted_iota(jnp.int32, (ar, bq), 1)

        def per_q(qi, _):
            q0 = pl.multiple_of(qi * bq, bq)
            rows = [pl.ds(pl.multiple_of(q0 + off, ar), ar) for _, _, off in strips]
            qbs = [q_ref[r, hd] for r, (_, hd, _) in zip(rows, strips)]
            dobs = [do_ref[r, hd] for r, (_, hd, _) in zip(rows, strips)]

            def cond(st):
                return jnp.logical_and(st[0] >= 0, st[1])

            def step(st):
                kj, _, carries = st
                k0 = pl.multiple_of(kj * bq, bq)
                alive, new_carries = None, []
                for (a, hd, off), qb, dob, carry in zip(strips, qbs, dobs, carries):
                    kept = pl.ds(pl.multiple_of(k0 + off, ar), ar)
                    mask, lb, lm = _attn_tile(qb, k_ref[pl.ds(k0, bq), hd], row + off, col, (kj - qi) * bq, scale)
                    w = jnp.where(mask, jnp.exp(lb + _tri_sum(lm, tri) + carry), 0.0)
                    g_ref[a, kept, :] = w * _dot_nt(dob, v_ref[pl.ds(k0, bq), hd])
                    b_ref[a, kept, :] = jnp.where(mask, jnp.exp(lb), 0.0)
                    dva_ref[pl.ds(k0, bq), hd] += _dot_tn(w.astype(_MXU), dob)
                    carry = carry + jnp.sum(lm, axis=1, keepdims=True)
                    new_carries.append(carry)
                    live = jnp.max(carry) > EXP_CUT
                    alive = live if alive is None else jnp.logical_or(alive, live)
                return kj - 1, alive, tuple(new_carries)

            st = lax.while_loop(cond, step, (qi, jnp.bool_(True), (jnp.zeros((ar, 1), F32),) * len(strips)))

            def back(kj, st2):
                k0 = pl.multiple_of(kj * bq, bq)
                out = []
                for (a, hd, off), qb, (before, dq) in zip(strips, qbs, st2):
                    kept = pl.ds(pl.multiple_of(k0 + off, ar), ar)
                    g = g_ref[a, kept, :]
                    beta = b_ref[a, kept, :]
                    dz = ((g * (1.0 - beta) - (before + _tri_sum(g, tri_lt)) * beta) * scale).astype(_MXU)
                    dka_ref[pl.ds(k0, bq), hd] += _dot_tn(dz, qb)
                    out.append((before + jnp.sum(g, axis=1, keepdims=True), dq + _dot(dz, k_ref[pl.ds(k0, bq), hd])))
                return tuple(out)

            st2 = lax.fori_loop(st[0] + 1, qi + 1, back, ((jnp.zeros((ar, 1), F32), jnp.zeros((ar, HEAD_DIM), F32)),) * len(strips))
            for r, (_, hd, _), (_, dq) in zip(rows, strips, st2):
                dq_ref[r, hd] = dq.astype(dq_ref.dtype)
            return 0

        lax.fori_loop(0, nq, per_q, 0)
        dk_ref[...] = dka_ref[...].astype(dk_ref.dtype)
        dv_ref[...] = dva_ref[...].astype(dv_ref.dtype)
        _emit([dq_ref, dk_ref, dv_ref], dproj_ref, [(a * groups + group) * wide for a in range(3)], out_sems)

    hs = lambda off: pl.BlockSpec((t, wide), lambda h: (0, off + h))
    return _call(
        comm, body, grid=(groups,), in_specs=[hs(0), hs(groups), hs(2 * groups), hs(0), _ANY], out_specs=_ANY,
        out_shape=S(dproj.shape, dproj.dtype), input_output_aliases={4: 0},
        scratch_shapes=[pltpu.VMEM((t, wide), F32), pltpu.VMEM((t, wide), F32),
                        pltpu.VMEM((ATTN_HEADS, t, bq), F32), pltpu.VMEM((ATTN_HEADS, t, bq), F32)]
        + [pltpu.VMEM((t, wide), dproj.dtype)] * 3 + [pltpu.SemaphoreType.DMA((3,))],
        compiler_params=_cp("parallel"), name=name)(proj, proj, proj, do, dproj)


def _shift_down(cur, prev8, k):
    if k == 0:
        return cur
    row8 = lax.broadcasted_iota(jnp.int32, prev8.shape, 0)
    rc = pltpu.roll(cur, k, 0)
    top = jnp.where(row8 < k, pltpu.roll(prev8, k, 0), rc[0:8, :])
    return jnp.concatenate([top, rc[8:, :]], axis=0)


def _shift_up(cur, next8, k):
    if k == 0:
        return cur
    n = cur.shape[0]
    row8 = lax.broadcasted_iota(jnp.int32, next8.shape, 0)
    rc = pltpu.roll(cur, n - k, 0)
    bottom = jnp.where(row8 >= 8 - k, pltpu.roll(next8, 8 - k, 0), rc[n - 8:, :])
    return jnp.concatenate([rc[:n - 8, :], bottom], axis=0)


def _lru_conv(xl, prev8, cw, cb):
    xs = [_shift_down(xl, prev8, CONV_WIDTH - 1 - k) for k in range(CONV_WIDTH)]
    xc = xs[0] * cw[0:1, :]
    for k in range(1, CONV_WIDTH):
        xc = xc + xs[k] * cw[k:k + 1, :]
    return xs, xc + cb


def _lru_gates(xl, prev8, cw, cb, wr, br, wi, bi, ls):
    xs, xc = _lru_conv(xl, prev8, cw, cb)
    xcb = xc.astype(_MXU)
    r = jax.nn.sigmoid(_dot(xcb, wr) + br)
    i = jax.nn.sigmoid(_dot(xcb, wi) + bi)
    la = (LRU_C * r) * ls
    a = jnp.exp(la)
    mult = jnp.sqrt(-_expm1(2.0 * la))
    return xs, xc, r, i, a, mult


def _group_scan(a, b, reverse):
    n = a.shape[0]
    row = lax.broadcasted_iota(jnp.int32, a.shape, 0) % 8
    for d in (1, 2, 4):
        if reverse:
            m = row < 8 - d
            a_s, b_s = pltpu.roll(a, n - d, 0), pltpu.roll(b, n - d, 0)
        else:
            m = row >= d
            a_s, b_s = pltpu.roll(a, d, 0), pltpu.roll(b, d, 0)
        b = jnp.where(m, a * b_s + b, b)
        a = jnp.where(m, a * a_s, a)
    return a, b


def _lru_fwd(proj, col0, n_blocks, cw, cb, wr, br, wi, bi, lam, name, comm=None):
    t = proj.shape[0]
    tt = min(t, SEQ_TILE)
    nt = t // tt

    def body(xl_ref, gl_ref, cw_ref, cb_ref, wr_ref, br_ref, wi_ref, bi_ref, lam_ref, h_ref, y_ref, *kept):
        cwv, cbv, brv, biv = cw_ref[...], cb_ref[...], br_ref[...], bi_ref[...]
        wrv, wiv = wr_ref[...].astype(_MXU), wi_ref[...].astype(_MXU)
        ls = _log_sigmoid(lam_ref[...])

        def tile(ti, hin):
            t0 = pl.multiple_of(ti * tt, tt)
            p0 = pl.multiple_of(jnp.maximum(t0 - 8, 0), 8)
            prev8 = xl_ref[pl.ds(p0, 8), :] * (ti > 0).astype(F32)
            xl = xl_ref[pl.ds(t0, tt), :]
            _, xc, r, ig, a, mult = _lru_gates(xl, prev8, cwv, cbv, wrv, brv, wiv, biv, ls)
            for ref, val in zip(kept, (r, ig, a, mult)):
                ref[pl.ds(t0, tt), :] = val
            ga, gb = _group_scan(a, mult * (ig * xc), False)
            for g in range(tt // 8):
                hg = ga[8 * g:8 * g + 8, :] * hin + gb[8 * g:8 * g + 8, :]
                h_ref[pl.ds(t0 + 8 * g, 8), :] = hg
                hin = hg[7:8, :]
            y_ref[pl.ds(t0, tt), :] = h_ref[pl.ds(t0, tt), :] * _gelu(gl_ref[pl.ds(t0, tt), :])
            return hin

        lax.fori_loop(0, nt, tile, jnp.zeros((1, HEAD_DIM), F32))

    cs = lambda off: pl.BlockSpec((t, HEAD_DIM), lambda n: (0, off + n))
    vs = pl.BlockSpec((1, HEAD_DIM), lambda n: (0, n))
    ws = pl.BlockSpec((None, HEAD_DIM, HEAD_DIM), lambda n: (n, 0, 0))
    w = n_blocks * HEAD_DIM
    return _call(
        comm, body, grid=(n_blocks,),
        in_specs=[cs(col0), cs(col0 + n_blocks), pl.BlockSpec((CONV_WIDTH, HEAD_DIM), lambda n: (0, n)), vs, ws, vs, ws, vs, vs],
        out_specs=[cs(0)] * 6, out_shape=[S((t, w), F32)] * 6,
        compiler_params=_cp("parallel"), name=name)(proj, proj, cw, cb, wr, br, wi, bi, lam)


def _lru_bwd(proj, col0, n_blocks, h, kept, dyl, cw, cb, wr, wi, lam, name, comm=None):
    t = proj.shape[0]
    tt = min(t, SEQ_TILE)
    nt = t // tt

    def body(xl_ref, gl_ref, h_ref, r_ref, i_ref, a_ref, m_ref, dy_ref, cw_ref, cb_ref, wr_ref, wi_ref, lam_ref,
             dproj_ref, dcw_ref, dcb_ref, dwr_ref, dbr_ref, dwi_ref, dbi_ref, dlam_ref, g_ref, dxl_ref, dgl_ref, out_sems):
        block = pl.program_id(0)
        cwv, cbv = cw_ref[...], cb_ref[...]
        wrv, wiv = wr_ref[...].astype(_MXU), wi_ref[...].astype(_MXU)
        lamv = lam_ref[...]
        ls = _log_sigmoid(lamv)
        for ref in (dcw_ref, dcb_ref, dwr_ref, dbr_ref, dwi_ref, dbi_ref, dlam_ref):
            ref[...] = jnp.zeros_like(ref)

        def tile(s, carry):
            e_in, dxc_next8 = carry
            ti = nt - 1 - s
            t0 = pl.multiple_of(ti * tt, tt)
            p0 = pl.multiple_of(jnp.maximum(t0 - 8, 0), 8)
            first = (ti > 0).astype(F32)
            xl = xl_ref[pl.ds(t0, tt), :]
            xs, xc = _lru_conv(xl, xl_ref[pl.ds(p0, 8), :] * first, cwv, cbv)
            r, ig, a, mult = (ref[pl.ds(t0, tt), :] for ref in (r_ref, i_ref, a_ref, m_ref))
            hv = h_ref[pl.ds(t0, tt), :]
            h_before = _shift_down(hv, h_ref[pl.ds(p0, 8), :] * first, 1)
            glv = gl_ref[pl.ds(t0, tt), :]
            dyv = dy_ref[pl.ds(t0, tt), :]
            dgl_ref[pl.ds(t0, tt), :] = (dyv * hv * _gelu_grad(glv)).astype(dgl_ref.dtype)
            dh = dyv * _gelu(glv)
            row = lax.broadcasted_iota(jnp.int32, a.shape, 0)
            coef = jnp.where(row == tt - 1, 1.0, pltpu.roll(a, tt - 1, 0))
            ga, gb = _group_scan(coef, dh, True)
            gin = e_in
            for g in reversed(range(tt // 8)):
                gg = ga[8 * g:8 * g + 8, :] * gin + gb[8 * g:8 * g + 8, :]
                g_ref[8 * g:8 * g + 8, :] = gg
                gin = gg[0:1, :]
            gv = g_ref[...]
            e_out = a[0:1, :] * gv[0:1, :]
            ix = ig * xc
            dla = (gv * h_before) * a - (gv * ix) * (a * a / mult)
            dlam_ref[...] += jnp.sum(dla * (LRU_C * r), axis=0, keepdims=True)
            dpr = (dla * (LRU_C * ls)) * (r * (1.0 - r))
            dpi = (gv * mult * xc) * (ig * (1.0 - ig))
            dbr_ref[...] += jnp.sum(dpr, axis=0, keepdims=True)
            dbi_ref[...] += jnp.sum(dpi, axis=0, keepdims=True)
            xcb, dprb, dpib = xc.astype(_MXU), dpr.astype(_MXU), dpi.astype(_MXU)
            dwr_ref[...] += _dot_tn(xcb, dprb)
            dwi_ref[...] += _dot_tn(xcb, dpib)
            dxc = gv * mult * ig + _dot_nt(dprb, wrv) + _dot_nt(dpib, wiv)
            dcb_ref[...] += jnp.sum(dxc, axis=0, keepdims=True)
            dxl = None
            for k in range(CONV_WIDTH):
                dcw_ref[k:k + 1, :] += jnp.sum(dxc * xs[k], axis=0, keepdims=True)
                term = _shift_up(dxc, dxc_next8, CONV_WIDTH - 1 - k) * cwv[k:k + 1, :]
                dxl = term if dxl is None else dxl + term
            dxl_ref[pl.ds(t0, tt), :] = dxl.astype(dxl_ref.dtype)
            return e_out, dxc[0:8, :]

        lax.fori_loop(0, nt, tile, (jnp.zeros((1, HEAD_DIM), F32), jnp.zeros((8, HEAD_DIM), F32)))
        dlam_ref[...] = dlam_ref[...] * (1.0 - jax.nn.sigmoid(lamv))
        _emit([dxl_ref, dgl_ref], dproj_ref, [(col0 + block) * HEAD_DIM, (col0 + n_blocks + block) * HEAD_DIM], out_sems)

    cs = lambda off: pl.BlockSpec((t, HEAD_DIM), lambda n: (0, off + n))
    vs = pl.BlockSpec((1, HEAD_DIM), lambda n: (0, n))
    ws = pl.BlockSpec((None, HEAD_DIM, HEAD_DIM), lambda n: (n, 0, 0))
    cws = pl.BlockSpec((CONV_WIDTH, HEAD_DIM), lambda n: (0, n))
    w = n_blocks * HEAD_DIM
    vec = S((1, w), F32)
    mat = S((n_blocks, HEAD_DIM, HEAD_DIM), F32)
    return _call(
        comm, body, grid=(n_blocks,),
        in_specs=[cs(col0), cs(col0 + n_blocks)] + [cs(0)] * 6 + [cws, vs, ws, ws, vs],
        out_specs=[_ANY, cws, vs, ws, vs, ws, vs, vs],
        out_shape=[S(proj.shape, _MXU), S((CONV_WIDTH, w), F32), vec, mat, vec, mat, vec, vec],
        scratch_shapes=[pltpu.VMEM((tt, HEAD_DIM), F32), pltpu.VMEM((t, HEAD_DIM), _MXU), pltpu.VMEM((t, HEAD_DIM), _MXU),
                        pltpu.SemaphoreType.DMA((2,))],
        compiler_params=_cp("parallel"), name=name)(proj, proj, h, *kept, dyl, cw, cb, wr, wi, lam)


class _NoExchange:
    grad_dtype = F32

    def __init__(self, weights):
        self.weights, self.grads, self.packs = weights, {}, {}

    def weight(self, name):
        return self.weights[name]

    def in_proj(self, x, gain, bm):
        hn = _rms_fwd(x, gain, "rms1")
        return [hn, *_mm_nn(hn, self.weights["w_in"], bm=bm, bn=self.weights["w_in"].shape[2], name="in_proj", also=_MXU)]

    def conv_w(self):
        return self.weights["conv_w"]

    def carrier(self, call):
        return None

    def harvest(self, car):
        pass

    def alone(self, call):
        pass


def _local_step(x, target, norms, ex, cb, wr, br, wi, bi, lam, ga, gl):
    g_pre_mix, g_post_mix, g_pre_ffn, g_post_ffn = norms
    t, d = x.shape
    bm = min(t, MM_ROWS)
    bt = min(t, DW_TOKENS)

    def run(fn, name, *args, **kw):
        car = ex.carrier(name)
        out = fn(*args, name=name, comm=car, **kw)
        ex.harvest(car)
        return out

    hn1, proj, proj_mx = ex.in_proj(x, g_pre_mix, bm)
    win3, cw = ex.weight("w_in"), ex.conv_w()
    c = win3.shape[0]
    o = run(_attn_fwd, "attn_fwd", proj_mx, (proj.shape[1] - d) // 3 // HEAD_DIM)
    mix = 2 * o.shape[1]
    n_heads = n_blocks = o.shape[1] // HEAD_DIM
    h, yl, *kept = run(_lru_fwd, "lru_fwd", proj, 3 * n_heads, n_blocks, cw, cb, wr, br, wi, bi, lam)
    y = run(_outnorm_fwd, "outnorm_fwd", o, yl, ga, gl)
    wout = ex.weight("w_out")
    mixo = run(_mm_nn, "out_proj", y, wout[None], bm=bm, bn=d)
    x2, hn2 = run(_mid_fwd, "mid_fwd", x, mixo, g_post_mix, g_pre_ffn)
    ex.alone("gather_w_up_last")
    wg3, wu3 = ex.weight("w_ffn_gate"), ex.weight("w_ffn_up")
    act_dgate, act_dup, act = run(_swiglu_fwd, "ffn_gate_up", hn2, wg3, wu3, bm=bm)
    ex.alone("gather_w_down")
    wd = ex.weight("w_ffn_down")
    ff = wd.shape[0]
    f = _mm_nn(act, wd[None], bm=bm, bn=d // 2, name="ffn_down")
    loss_cols, dout, df, dg_post_ffn = _final(f, x2, target, g_post_ffn, "final")

    dgate, dup = _swiglu_bwd(df, wd, act_dgate, act_dup, bm=min(t, 2 * MM_ROWS), bo=ff // 4, name="ffn_down_bwd")
    ex.grads["w_ffn_down"] = _mm_tn(act, df, 1, bm=bt, bk=DW_ROWS, out_dtype=ex.grad_dtype, name="ffn_down_dw").reshape(c, ff // c, d)
    ex.grads["w_ffn_gate"] = run(_mm_tn, "ffn_gate_dw", hn2, dgate, c, bm=bt, bk=d // 2, out_dtype=ex.grad_dtype)
    ex.grads["w_ffn_up"] = run(_mm_tn, "ffn_up_dw", hn2, dup, c, bm=bt, bk=d // 2, out_dtype=ex.grad_dtype)
    dhn2_g = run(_mm_nt, "ffn_gate_dx", dgate, wg3, bm=bm, bo=d // 2, out_dtype=F32)
    dhn2_u = run(_mm_nt, "ffn_up_dx", dup, wu3, bm=bm, bo=d // 2, out_dtype=F32)
    dx2, dmix, dg_pre_ffn, dg_post_mix = run(_mid_bwd, "mid_bwd", dhn2_g, dhn2_u, dout, x2, mixo, g_pre_ffn, g_post_mix)
    dy = run(_mm_nt, "out_proj_dx", dmix, wout[None], bm=bm, bo=mix, out_dtype=F32)
    ex.grads["w_out"] = _mm_tn(y, dmix, 1, bm=bt, bk=mix // 4, out_dtype=ex.grad_dtype, name="out_proj_dw").reshape(c, mix // c, d)
    do, dyl, dga, dgl_norm = run(_outnorm_bwd, "outnorm_bwd", dy, o, yl, ga, gl)
    dproj, dcw, dcb, dwr, dbr, dwi, dbi, dlam = run(_lru_bwd, "lru_bwd", proj, 3 * n_heads, n_blocks, h, kept, dyl, cw, cb, wr, wi, lam)
    small = dict(post_mix_norm=dg_post_mix, pre_ffn_norm=dg_pre_ffn, post_ffn_norm=dg_post_ffn, conv_w=dcw, conv_b=dcb,
                 w_rgate=dwr, b_rgate=dbr, w_igate=dwi, b_igate=dbi, lru_lambda=dlam, attn_out_norm=dga, lru_out_norm=dgl_norm)
    ex.packs["early"] = _pack([small[n] for n in _SMALL_EARLY])
    dproj = run(_attn_bwd, "attn_bwd", proj_mx, do, dproj, n_heads)
    ex.grads["w_in"] = _mm_tn(hn1, dproj, c, bm=bt, bk=d // 2, out_dtype=ex.grad_dtype, name="in_proj_dw")
    ex.alone("grads_w_in_swap")
    dhn1 = run(_mm_nt, "in_proj_dx", dproj, win3, bm=bm, bo=d // 2, out_dtype=F32)
    grad_x, small["pre_mix_norm"] = run(_first_bwd, "first_bwd", dhn1, dx2, x, g_pre_mix)
    ex.packs["late"] = _pack([small["pre_mix_norm"], (0.5 / d) * jnp.sum(loss_cols, keepdims=True)])
    return loss_cols, grad_x, small


def _into_slot(wsh, slot, dtype, name):
    rows, n = wsh.shape
    rb = _row_block(rows, 512) if rows % 8 == 0 else rows

    def body(s_ref, w_ref, o_ref):
        o_ref[...] = w_ref[...].astype(o_ref.dtype)

    return pl.pallas_call(
        body,
        grid_spec=pltpu.PrefetchScalarGridSpec(
            num_scalar_prefetch=1, grid=(rows // rb,),
            in_specs=[pl.BlockSpec((rb, n), lambda i, s_ref: (i, 0))],
            out_specs=pl.BlockSpec((None, rb, n), lambda i, s_ref: (s_ref[0], i, 0))),
        out_shape=S((4, rows, n), dtype), compiler_params=_cp("parallel"), name=name)(slot, wsh)


class _Exchange:
    SCHEDULE = {
        "in_proj": [("stream", "w_in"), ("ici", "conv_w"), ("ici", "w_ffn_up", 0)],
        "attn_fwd": [("d2d", "w_ffn_up", 0), ("ici", "w_ffn_gate")],
        "lru_fwd": [("d2d", "w_ffn_gate"), ("ici", "w_out"), ("ici", "w_ffn_up", 1)],
        "outnorm_fwd": [("d2d", "w_out"), ("d2d", "w_ffn_up", 1)],
        "out_proj": [("ici", "w_ffn_up", 2)],
        "mid_fwd": [("d2d", "w_ffn_up", 2), ("ici", "w_ffn_up", 3)],
        "gather_w_up_last": [("d2d", "w_ffn_up", 3)],
        "ffn_gate_up": [("ici", "w_ffn_down")],
        "gather_w_down": [("d2d", "w_ffn_down")],
        "ffn_gate_dw": [("swap", "w_ffn_down")],
        "ffn_up_dw": [("scatter", "w_ffn_down", 0), ("scatter", "w_ffn_down", 1), ("scatter", "w_ffn_down", 2), ("swap", "w_ffn_gate")],
        "ffn_gate_dx": [("scatter", "w_ffn_down", 3), ("scatter", "w_ffn_gate", 0), ("swap", "w_ffn_up")],
        "ffn_up_dx": [("share", "w_ffn_down"), ("scatter", "w_ffn_gate", 1), ("scatter", "w_ffn_gate", 2)],
        "mid_bwd": [("scatter", "w_ffn_gate", 3), ("scatter", "w_ffn_up", 0)],
        "out_proj_dx": [("share", "w_ffn_gate"), ("scatter", "w_ffn_up", 1)],
        "outnorm_bwd": [("scatter", "w_ffn_up", 2), ("swap", "w_out")],
        "lru_bwd": [("scatter", "w_ffn_up", 3), ("scatter", "w_out")],
        "attn_bwd": [("share", "w_ffn_up"), ("share", "w_out"), ("spread", "early")],
        "grads_w_in_swap": [("swap", "w_in")],
        "in_proj_dx": [("scatter", "w_in")],
        "grads_w_in_share": [("share", "w_in"), ("spread", "late")],
    }
    PIECES = 4
    grad_dtype = BF16

    def __init__(self, slots, place):
        self.buf, self.place = dict(slots), place
        self.grads, self.packs, self.swapped, self.part, self.scattered, self.full, self.spreaded = {}, {}, {}, {}, {}, {}, {}

    def weight(self, name):
        b = self.buf[name]
        return b.reshape(-1, b.shape[2]) if name in ("w_out", "w_ffn_down") else b

    def in_proj(self, x, gain, bm):
        car = self.carrier("in_proj")
        out = _in_proj_streamed(x, gain, car, car.streamed, self.place, bm=bm, name="in_proj")
        self.harvest(car)
        return out

    def conv_w(self):
        return jnp.transpose(self.buf["conv_w"], (1, 0, 2)).reshape(CONV_WIDTH, -1)

    def carrier(self, call):
        if call not in self.SCHEDULE:
            return None
        car = _Carrier()
        car.todo, slot = [], {}
        for kind, name, *piece in self.SCHEDULE[call]:
            if kind in ("ici", "d2d", "stream"):
                if name not in slot:
                    slot[name] = car.inplace(self.buf[name])
                    car.todo.append((self.buf, name, slot[name]))
            if kind == "stream":
                car.streamed = slot[name]
            elif kind in ("ici", "d2d"):
                size = self.buf[name].shape[1] // 2 // self.PIECES
                rows = (piece[0] * size, size) if piece else None
                if kind == "ici":
                    car.gather_ici(slot[name], rows, split=name != "conv_w")
                else:
                    car.gather_d2d(slot[name], rows)
            elif kind == "swap":
                g = self.grads[name]
                o = car.fresh((4, g.shape[1] // 2, g.shape[2]), g.dtype)
                car.swap(car.read(g), o)
                car.todo.append((self.swapped, name, o))
            elif kind == "scatter":
                if name not in self.part:
                    self.part[name] = _add_own_half(self.grads[name], self.swapped[name], self.place[1:], "grads_add_" + name)
                p = self.part[name]
                key = ("scatter", name)
                if key not in slot:
                    slot[key] = (car.read(p), car.inplace(self.scattered[name]) if name in self.scattered else car.fresh(p.shape, p.dtype))
                    car.todo.append((self.scattered, name, slot[key][1]))
                size = p.shape[1] // self.PIECES
                car.scatter(*slot[key], (piece[0] * size, size) if piece else None)
            elif kind == "share":
                o = car.inplace(_sum_chips(self.part[name], self.scattered[name], self.place, "grads_sum_" + name))
                car.share(o)
                car.todo.append((self.full, name, o))
            else:
                o = car.fresh((8,) + self.packs[name].shape, F32)
                car.spread(car.read(self.packs[name]), o)
                car.todo.append((self.spreaded, name, o))
        return car

    def harvest(self, car):
        for state, name, o in (car.todo if car is not None else []):
            state[name] = car.results[o]

    def alone(self, call):
        car = self.carrier(call)
        car.run_alone(call)
        self.harvest(car)

    def small_sum(self, key):
        return _sum_devices(self.packs[key], self.spreaded[key], 2 * self.place[0:1] + self.place[1:], "grads_small_sum_" + key)


def _row_block(rows, cap):
    return max(b for b in range(8, cap + 1, 8) if rows % b == 0)


def _add_own_half(g, recv, core, name):
    _, rows, n = g.shape
    half = rows // 2
    rb = _row_block(half, 1024)
    nb = half // rb

    def body(c_ref, g_ref, r_ref, o_ref):
        o_ref[...] = (g_ref[...].astype(F32) + r_ref[...].astype(F32)).astype(o_ref.dtype)

    return pl.pallas_call(
        body,
        grid_spec=pltpu.PrefetchScalarGridSpec(
            num_scalar_prefetch=1, grid=(4, nb),
            in_specs=[pl.BlockSpec((None, rb, n), lambda k, i, c_ref: (k, c_ref[0] * nb + i, 0)),
                      pl.BlockSpec((None, rb, n), lambda k, i, c_ref: (k, i, 0))],
            out_specs=pl.BlockSpec((None, rb, n), lambda k, i, c_ref: (k, i, 0))),
        out_shape=S((4, half, n), BF16), compiler_params=_cp("parallel", "parallel"), name=name)(core, g, recv)


def _sum_chips(part, recv, place, name):
    _, rows, n = part.shape
    rb = _row_block(rows, 256)
    nb = rows // rb

    def body(p_ref, own_ref, r0, r1, r2, r3, o_ref):
        own = own_ref[...].astype(F32)
        terms = [jnp.where(p_ref[0] == k, own, r[...].astype(F32)) for k, r in enumerate((r0, r1, r2, r3))]
        o_ref[...] = ((terms[0] + terms[1]) + terms[2]) + terms[3]

    def slot(k):
        return pl.BlockSpec((None, rb, n), lambda i, p_ref: (jnp.where(p_ref[0] == k, (k + 1) % 4, k), i, 0))

    return pl.pallas_call(
        body,
        grid_spec=pltpu.PrefetchScalarGridSpec(
            num_scalar_prefetch=1, grid=(nb,),
            in_specs=[pl.BlockSpec((None, rb, n), lambda i, p_ref: (p_ref[0], i, 0))] + [slot(k) for k in range(4)],
            out_specs=pl.BlockSpec((rb, n), lambda i, p_ref: (p_ref[1] * nb + i, 0))),
        out_shape=S((2 * rows, n), F32), compiler_params=_cp("parallel"), name=name)(place, part, recv, recv, recv, recv)


def _sum_devices(own, spread, me, name):
    rows = own.shape[0]

    def body(me_ref, own_ref, *refs):
        acc = None
        for k, r in enumerate(refs[:8]):
            term = jnp.where(me_ref[0] == k, own_ref[...], r[...])
            acc = term if acc is None else acc + term
        refs[8][...] = acc

    def slot(k):
        return pl.BlockSpec((None, rows, 128), lambda i, me_ref: (jnp.where(me_ref[0] == k, (k + 1) % 8, k), 0, 0))

    whole = pl.BlockSpec((rows, 128), lambda i, me_ref: (0, 0))
    return pl.pallas_call(
        body,
        grid_spec=pltpu.PrefetchScalarGridSpec(num_scalar_prefetch=1, grid=(1,), in_specs=[whole] + [slot(k) for k in range(8)],
                                               out_specs=whole),
        out_shape=S((rows, 128), F32), compiler_params=_cp("arbitrary"), name=name)(me, own, *[spread] * 8)


def _adamw(w, g, m, v, name, regive=False):
    rows, n = w.shape
    rb = rows if rows * n * 4 <= (1 << 21) else _row_block(rows, 512)
    c1 = 1.0 - ADAM_B1 ** ADAM_STEP
    c2 = 1.0 - ADAM_B2 ** ADAM_STEP

    def body(w_ref, g_ref, m_ref, v_ref, d_ref, nm_ref, nv_ref, *again):
        gv = g_ref[...]
        for ref in again:
            ref[...] = gv
        nm = ADAM_B1 * m_ref[...] + (1.0 - ADAM_B1) * gv
        nv = ADAM_B2 * v_ref[...] + (1.0 - ADAM_B2) * (gv * gv)
        nm_ref[...] = nm
        nv_ref[...] = nv
        d_ref[...] = -ADAM_LR * ((nm / c1) / (jnp.sqrt(nv / c2) + ADAM_EPS) + ADAM_WD * w_ref[...])

    bs = pl.BlockSpec((rb, n), lambda i: (i, 0))
    n_out = 4 if regive else 3
    return pl.pallas_call(body, grid=(rows // rb,), in_specs=[bs] * 4, out_specs=[bs] * n_out, out_shape=[S((rows, n), F32)] * n_out,
                          compiler_params=_cp("parallel"), name=name)(w, g, m, v)


_BIG = ("w_in", "w_out", "w_ffn_gate", "w_ffn_up", "w_ffn_down")
_SMALL = ("pre_mix_norm", "post_mix_norm", "pre_ffn_norm", "post_ffn_norm", "conv_w", "conv_b", "w_rgate", "b_rgate",
          "w_igate", "b_igate", "lru_lambda", "attn_out_norm", "lru_out_norm")
_SMALL_EARLY = _SMALL[1:]
_WEIGHTS = ("pre_mix_norm", "post_mix_norm", "pre_ffn_norm", "post_ffn_norm", "w_in", "conv_w", "conv_b", "w_rgate", "b_rgate",
            "w_igate", "b_igate", "lru_lambda", "attn_out_norm", "lru_out_norm", "w_out", "w_ffn_gate", "w_ffn_up", "w_ffn_down")


def _pack(arrays):
    flat = []
    for a in arrays:
        f = a.reshape(-1)
        flat.append(jnp.pad(f, (0, (-f.shape[0]) % 1024)))
    return jnp.concatenate(flat).reshape(-1, 128)


def _unpack(packed, shapes):
    out, pos = [], 0
    flat = packed.reshape(-1)
    for s in shapes:
        size = math.prod(s)
        out.append(flat[pos:pos + size].reshape(s))
        pos += size + (-size) % 1024
    return out


def kernel(x, pre_mix_norm, post_mix_norm, pre_ffn_norm, post_ffn_norm, w_in, conv_w, conv_b, w_rgate, b_rgate, w_igate, b_igate, lru_lambda, attn_out_norm, lru_out_norm, w_out, w_ffn_gate, w_ffn_up, w_ffn_down, loss_target, m_pre_mix_norm, m_post_mix_norm, m_pre_ffn_norm, m_post_ffn_norm, m_w_in, m_conv_w, m_conv_b, m_w_rgate, m_b_rgate, m_w_igate, m_b_igate, m_lru_lambda, m_attn_out_norm, m_lru_out_norm, m_w_out, m_w_ffn_gate, m_w_ffn_up, m_w_ffn_down, v_pre_mix_norm, v_post_mix_norm, v_pre_ffn_norm, v_post_ffn_norm, v_w_in, v_conv_w, v_conv_b, v_w_rgate, v_b_rgate, v_w_igate, v_b_igate, v_lru_lambda, v_attn_out_norm, v_lru_out_norm, v_w_out, v_w_ffn_gate, v_w_ffn_up, v_w_ffn_down):
    given = dict(locals())
    w = {n: given[n][0] for n in _WEIGHTS}
    m = {n: given["m_" + n][0] for n in _WEIGHTS}
    v = {n: given["v_" + n][0] for n in _WEIGHTS}
    xs, target = x[0], loss_target[0]
    d = xs.shape[1]
    chip = (2 * lax.axis_index("x") + lax.axis_index("y")).astype(jnp.int32)
    place = jnp.stack([chip, lax.axis_index("c").astype(jnp.int32)])

    slots = {n: _into_slot(w[n], place[0:1], _MXU, "slot_" + n) for n in _BIG}
    slots["conv_w"] = _into_slot(w["conv_w"], place[0:1], F32, "slot_conv_w")
    ex = _Exchange(slots, place)
    row = lambda a: a.reshape(1, -1)
    norms = tuple(row(w[n]) for n in ("pre_mix_norm", "post_mix_norm", "pre_ffn_norm", "post_ffn_norm"))

    loss_cols, grad_x, small = _local_step(
        xs, target, norms, ex, row(w["conv_b"]), w["w_rgate"], row(w["b_rgate"]),
        w["w_igate"], row(w["b_igate"]), row(w["lru_lambda"]), row(w["attn_out_norm"]), row(w["lru_out_norm"]))


    ex.alone("grads_w_in_share")
    reduced = {n: ex.full[n] for n in _BIG}
    early = _unpack(ex.small_sum("early"), [small[n].shape for n in _SMALL_EARLY])
    late = _unpack(ex.small_sum("late"), [small["pre_mix_norm"].shape, (1, 1)])
    loss = late[1][0, 0]
    for n, g in zip(_SMALL_EARLY + ("pre_mix_norm",), early + late[:1]):
        reduced[n] = g.reshape(w[n].shape) if n != "conv_w" else lax.dynamic_slice_in_dim(g, chip * w[n].shape[1], w[n].shape[1], axis=1)

    delta, new_m, new_v = {}, {}, {}
    for n in _BIG:
        delta[n], new_m[n], new_v[n], reduced[n] = _adamw(w[n], reduced[n], m[n], v[n], "adamw_" + n, regive=True)
    shapes = [w[n].shape for n in _SMALL]
    packed = _adamw(*[_pack([src[n] for n in _SMALL]) for src in (w, reduced, m, v)], "adamw_small")
    for out, p in zip((delta, new_m, new_v), packed):
        out.update(zip(_SMALL, _unpack(p, shapes)))

    lead = lambda a: a[None]
    return (loss, lead(grad_x), *[lead(reduced[n]) for n in _WEIGHTS], *[lead(delta[n]) for n in _WEIGHTS],
            *[lead(new_m[n]) for n in _WEIGHTS], *[lead(new_v[n]) for n in _WEIGHTS])
```

```python
import functools
import math

import jax
import jax.numpy as jnp
from jax import lax
from jax.experimental import pallas as pl
from jax.experimental.pallas import tpu as pltpu

F32 = jnp.float32
BF16 = jnp.bfloat16
_MXU = BF16
S = jax.ShapeDtypeStruct

RMS_EPS = 1e-6
HEAD_DIM = 128
CONV_WIDTH = 4
LRU_C = 8.0
ADAM_LR, ADAM_B1, ADAM_B2, ADAM_EPS, ADAM_WD, ADAM_STEP = 0.001, 0.9, 0.999, 1e-08, 0.01, 10
EXP_CUT = -105.0
VMEM_LIMIT = 60 * 1024 * 1024
ROW_TILE = 512
SEQ_TILE = 512
ATTN_BLOCK = 256
ATTN_HEADS = 2
MM_ROWS = 512
DW_TOKENS = 4096
DW_ROWS = 512
MESH = pl.DeviceIdType.MESH


def _cp(*sem):
    return pltpu.CompilerParams(dimension_semantics=sem, vmem_limit_bytes=VMEM_LIMIT)


def _dot(a, b):
    return jnp.dot(a, b, preferred_element_type=F32)


def _dot_nt(a, b):
    return lax.dot_general(a, b, (((1,), (1,)), ((), ())), preferred_element_type=F32)


def _dot_tn(a, b):
    return lax.dot_general(a, b, (((0,), (0,)), ((), ())), preferred_element_type=F32)


def _rstd(v):
    return lax.rsqrt(jnp.mean(v * v, axis=-1, keepdims=True) + RMS_EPS)


def _rms_bwd(dn, vh, r, gain):
    dvh = dn * gain
    dv = r * (dvh - vh * jnp.mean(dvh * vh, axis=-1, keepdims=True))
    return dv, jnp.sum(dn * vh, axis=0, keepdims=True)


def _log_sigmoid(z):
    return jnp.minimum(z, 0.0) - jnp.log(1.0 + jnp.exp(-jnp.abs(z)))


def _expm1(v):
    small = v * (1.0 + v * (0.5 + v * (1.0 / 6.0 + v * (1.0 / 24.0 + v * (1.0 / 120.0)))))
    return jnp.where(jnp.abs(v) < 0.04, small, jnp.exp(v) - 1.0)


_GELU_C = math.sqrt(2.0 / math.pi)


def _gelu(v):
    return 0.5 * v * (1.0 + jnp.tanh(_GELU_C * (v + 0.044715 * v * v * v)))


def _gelu_grad(v):
    th = jnp.tanh(_GELU_C * (v + 0.044715 * v * v * v))
    return 0.5 * (1.0 + th) + 0.5 * v * (1.0 - th * th) * _GELU_C * (1.0 + 3.0 * 0.044715 * v * v)


def _row_spec(tm, d):
    return pl.BlockSpec((tm, d), lambda i: (i, 0))


def _vec_spec(d):
    return pl.BlockSpec((1, d), lambda i: (0, 0))


_ANY = pl.BlockSpec(memory_space=pl.ANY)


def _place():
    x, y, c = lax.axis_index("x"), lax.axis_index("y"), lax.axis_index("c")
    return x, y, c, [(1 - x, y), (x, 1 - y), (1 - x, 1 - y)]


def _remote(src, dst, send_sem, recv_sem, to):
    return pltpu.make_async_remote_copy(src_ref=src, dst_ref=dst, send_sem=send_sem, recv_sem=recv_sem,
                                        device_id=to, device_id_type=MESH)


class _Carrier:
    def __init__(self):
        self.inputs, self.out_shapes, self.aliases, self.ops, self.n_sems, self.results = [], [], {}, [], 0, None

    def inplace(self, arr):
        self.aliases[len(self.inputs)] = len(self.out_shapes)
        self.inputs.append(arr)
        self.out_shapes.append(S(arr.shape, arr.dtype))
        return len(self.out_shapes) - 1

    def read(self, arr):
        self.inputs.append(arr)
        return len(self.inputs) - 1

    def fresh(self, shape, dtype):
        self.out_shapes.append(S(shape, dtype))
        return len(self.out_shapes) - 1

    def _add(self, n_sems, copies):
        base = self.n_sems
        self.n_sems += n_sems

        def start(ins, outs, send, recv):
            for k, (src, dst, _, to) in enumerate(copies(ins, outs)):
                _remote(src, dst, send.at[base + k], recv.at[base + k], to).start()

        def finish(ins, outs, send, recv):
            for k, (src, _, land, to) in enumerate(copies(ins, outs)):
                _remote(src, land, send.at[base + k], recv.at[base + k], to).wait()

        self.ops.append((start, finish))

    def gather_ici(self, o, rows=None, split=True):
        half = self.out_shapes[o].shape[1] // 2
        lo, size = rows or (0, half)

        def copies(ins, outs):
            x, y, c, chips = _place()
            part = (lambda ref: ref.at[pl.ds(c * half + lo, size)]) if split else (lambda ref: ref)
            mine = part(outs[o].at[2 * x + y])
            return [(mine, mine, part(outs[o].at[2 * px + py]), (px, py, c)) for px, py in chips]

        self._add(3, copies)

    def gather_d2d(self, o, rows=None):
        half = self.out_shapes[o].shape[1] // 2
        lo, size = rows or (0, half)

        def copies(ins, outs):
            x, y, c, chips = _place()
            at = lambda k, cc: outs[o].at[k].at[pl.ds(cc * half + lo, size)]
            return [(at(2 * px + py, c), at(2 * px + py, c), at(2 * px + py, 1 - c), (x, y, 1 - c)) for px, py in chips]

        self._add(3, copies)

    def swap(self, i, o):
        half = self.inputs[i].shape[1] // 2

        def copies(ins, outs):
            x, y, c, _ = _place()
            return [(ins[i].at[:, pl.ds((1 - c) * half, half)], outs[o], outs[o], (x, y, 1 - c))]

        self._add(1, copies)

    def scatter(self, i, o, rows=None):
        lo, size = rows or (0, self.inputs[i].shape[1])

        def copies(ins, outs):
            x, y, c, chips = _place()
            cut = lambda ref: ref.at[pl.ds(lo, size)]
            return [(cut(ins[i].at[2 * px + py]), cut(outs[o].at[2 * x + y]), cut(outs[o].at[2 * px + py]), (px, py, c)) for px, py in chips]

        self._add(3, copies)

    def share(self, o):
        r = self.out_shapes[o].shape[0] // 2

        def copies(ins, outs):
            x, y, c, _ = _place()
            mine = outs[o].at[pl.ds(c * r, r)]
            return [(mine, mine, outs[o].at[pl.ds((1 - c) * r, r)], (x, y, 1 - c))]

        self._add(1, copies)

    def spread(self, i, o):
        def copies(ins, outs):
            x, y, c, _ = _place()
            me = 4 * x + 2 * y + c
            out = []
            for d in range(1, 8):
                to, frm = (me + d) % 8, (me + 8 - d) % 8
                out.append((ins[i], outs[o].at[me], outs[o].at[frm], (to // 4, (to // 2) % 2, to % 2)))
            return out

        self._add(7, copies)

    def _pallas(self, body, n_in, n_out, scratch, **kw):
        k_in, k_out = len(self.inputs), len(self.out_shapes)
        grid = kw.get("grid", ())

        def wrapped(*refs):
            ins, cins = refs[:n_in], refs[n_in:n_in + k_in]
            outs = refs[n_in + k_in:n_in + k_in + n_out]
            couts = refs[n_in + k_in + n_out:n_in + k_in + n_out + k_out]
            own = refs[n_in + k_in + n_out + k_out:]
            send, recv = own[len(scratch):]
            ids = [pl.program_id(a) for a in range(len(grid))]
            first = functools.reduce(jnp.logical_and, [a == 0 for a in ids], True)
            last = functools.reduce(jnp.logical_and, [a == g - 1 for a, g in zip(ids, grid)], True)

            def go(stage):
                for op in self.ops:
                    op[stage](cins, couts, send, recv)

            if grid:
                pl.when(first)(lambda: go(0))
                body(*ins, *outs, *own[:len(scratch)])
                pl.when(last)(lambda: go(1))
            else:
                go(0)
                go(1)

        sem = pltpu.SemaphoreType.DMA((self.n_sems,))
        return pl.pallas_call(
            wrapped, in_specs=list(kw.get("in_specs", [])) + [_ANY] * k_in, out_specs=list(kw.get("out_specs", [])) + [_ANY] * k_out,
            out_shape=list(kw.get("out_shape", [])) + self.out_shapes, scratch_shapes=list(scratch) + [sem, sem],
            input_output_aliases={**kw.get("aliases", {}), **{n_in + i: n_out + o for i, o in self.aliases.items()}}, name=kw["name"],
            **({"grid": grid, "compiler_params": _cp(*["arbitrary"] * len(grid))} if grid else {}))

    def run(self, body, kw, *args):
        single = not isinstance(kw["out_shape"], (list, tuple))
        out_shape = [kw["out_shape"]] if single else list(kw["out_shape"])
        out_specs = [kw["out_specs"]] if single else list(kw["out_specs"])
        res = self._pallas(body, len(args), len(out_shape), kw.get("scratch_shapes", []), grid=kw["grid"], in_specs=kw["in_specs"],
                           out_specs=out_specs, out_shape=out_shape, name=kw["name"],
                           aliases=kw.get("input_output_aliases", {}))(*args, *self.inputs)
        self.results = list(res[len(out_shape):])
        return res[0] if single else list(res[:len(out_shape)])

    def run_alone(self, name):
        self.results = list(self._pallas(None, 0, 0, [], name=name)(*self.inputs))


def _call(comm, body, **kw):
    if comm is None:
        return pl.pallas_call(body, **kw)
    return functools.partial(comm.run, body, kw)


def _in_proj_streamed(x, gain, car, o_w, place, *, bm, name):
    m, k = x.shape
    n = car.out_shapes[o_w].shape[2]
    ni, half = m // bm, k // 2
    k_in, k_out = len(car.inputs), len(car.out_shapes)
    order = lambda p: ((p & 1) << 1) | (p >> 1)

    def body(place_ref, x_ref, g_ref, *refs):
        cins, (hn_ref, o_ref, ob_ref), couts = refs[:k_in], refs[k_in:k_in + 3], refs[k_in + 3:k_in + 3 + k_out]
        wbuf, hn_all, local, ici_send, ici_recv, d2d_send, d2d_recv, send, recv = refs[k_in + 3 + k_out:]
        p, i = pl.program_id(0), pl.program_id(1)
        x, y, c, chips = _place()
        me = 2 * x + y
        rows = lambda chunk, cc: couts[o_w].at[chunk].at[pl.ds(cc * half, half)]

        @pl.when(jnp.logical_and(p == 0, i == 0))
        def _():
            for j, (px, py) in enumerate(chips):
                _remote(rows(me, c), rows(me, c), ici_send.at[j], ici_recv.at[j], (px, py, c)).start()
            for op in car.ops:
                op[0](cins, couts, send, recv)

        for j, (px, py) in enumerate(chips):
            @pl.when(jnp.logical_and(p == j + 1, i == 0))
            def _(j=j, px=px, py=py):
                landed, other = rows(2 * px + py, c), rows(2 * px + py, 1 - c)
                _remote(landed, landed, ici_send.at[j], ici_recv.at[j], (px, py, c)).wait_recv()
                _remote(landed, landed, d2d_send.at[j], d2d_recv.at[j], (x, y, 1 - c)).start()
                _remote(other, other, d2d_send.at[j], d2d_recv.at[j], (x, y, 1 - c)).wait_recv()

        @pl.when(i == 0)
        def _():
            cp = pltpu.make_async_copy(couts[o_w].at[me ^ order(p)], wbuf, local.at[0])
            cp.start()
            cp.wait()

        tile = pl.ds(pl.multiple_of(i * bm, bm), bm)

        @pl.when(p == 0)
        def _():
            xv = x_ref[...]
            hn_all[tile, :] = ((xv * _rstd(xv)) * g_ref[...]).astype(_MXU)

        hn = hn_all[tile, :]
        hn_ref[...] = hn
        res = _dot(hn, wbuf[...])
        o_ref[...] = res
        ob_ref[...] = res.astype(ob_ref.dtype)

        @pl.when(jnp.logical_and(p == 3, i == ni - 1))
        def _():
            for j, (px, py) in enumerate(chips):
                _remote(rows(me, c), rows(me, c), ici_send.at[j], ici_recv.at[j], (px, py, c)).wait_send()
                _remote(rows(me, c), rows(me, c), d2d_send.at[j], d2d_recv.at[j], (x, y, 1 - c)).wait_send()
            for op in car.ops:
                op[1](cins, couts, send, recv)

    ospec = pl.BlockSpec((bm, n), lambda p, i, place_ref: (i, place_ref[0] ^ order(p)))
    rows = pl.BlockSpec((bm, k), lambda p, i, place_ref: (jnp.where(p == 0, i, 0), 0))
    three, sems = pltpu.SemaphoreType.DMA((3,)), pltpu.SemaphoreType.DMA((max(car.n_sems, 1),))
    res = pl.pallas_call(
        body,
        grid_spec=pltpu.PrefetchScalarGridSpec(
            num_scalar_prefetch=1, grid=(4, ni),
            in_specs=[rows, pl.BlockSpec((1, k), lambda p, i, place_ref: (0, 0))] + [_ANY] * k_in,
            out_specs=[pl.BlockSpec((bm, k), lambda p, i, place_ref: (p * ni + i, 0)), ospec, ospec] + [_ANY] * k_out,
            scratch_shapes=[pltpu.VMEM((k, n), _MXU), pltpu.VMEM((m, k), _MXU), pltpu.SemaphoreType.DMA((1,)),
                            three, three, three, three, sems, sems]),
        out_shape=[S((4 * m, k), _MXU), S((m, 4 * n), F32), S((m, 4 * n), _MXU)] + car.out_shapes,
        input_output_aliases={3 + a: 3 + o for a, o in car.aliases.items()},
        compiler_params=_cp("arbitrary", "arbitrary"), name=name)(place, x, gain, *car.inputs)
    car.results = list(res[3:])
    return res[0], res[1], res[2]


def _mm_nn(a, b3, *, bm, bn, name, also=None, comm=None):
    m, k = a.shape
    c, _, n = b3.shape
    ni, nj = m // bm, n // bn

    def body(a_ref, b_ref, *o_refs):
        res = _dot(a_ref[...], b_ref[...])
        for o_ref in o_refs:
            o_ref[...] = res.astype(o_ref.dtype)

    ospec = pl.BlockSpec((bm, bn), lambda cc, j, i: (i, cc * nj + j))
    dtypes = [F32] + ([] if also is None else [also])
    out = _call(
        comm, body, grid=(c, nj, ni),
        in_specs=[pl.BlockSpec((bm, k), lambda cc, j, i: (i, 0)), pl.BlockSpec((None, k, bn), lambda cc, j, i: (cc, 0, j))],
        out_specs=[ospec] * len(dtypes), out_shape=[S((m, c * n), dt) for dt in dtypes],
        compiler_params=_cp("parallel", "parallel", "parallel"), name=name)(a, b3)
    return out[0] if also is None else out


def _mm_nt(a, b3, *, bm, bo, out_dtype, name, comm=None):
    m = a.shape[0]
    c, ko, n = b3.shape
    ni, nj = m // bm, ko // bo

    def body(a_ref, b_ref, o_ref):
        acc = _dot_nt(a_ref[:, 0:n], b_ref[0])
        for cc in range(1, c):
            acc = acc + _dot_nt(a_ref[:, cc * n:(cc + 1) * n], b_ref[cc])
        o_ref[...] = acc.astype(o_ref.dtype)

    return _call(
        comm, body, grid=(nj, ni),
        in_specs=[pl.BlockSpec((bm, c * n), lambda j, i: (i, 0)),
                  pl.BlockSpec((c, bo, n), lambda j, i: (0, j, 0))],
        out_specs=pl.BlockSpec((bm, bo), lambda j, i: (i, j)),
        out_shape=S((m, ko), out_dtype),
        compiler_params=_cp("parallel", "parallel"), name=name)(a, b3)


def _mm_tn(a, b, c, *, bm, bk, out_dtype, name, comm=None):
    m, k = b.shape[0], a.shape[1]
    n = b.shape[1] // c
    nm, nk = m // bm, k // bk

    def body(a_ref, b_ref, o_ref, *acc):
        if nm == 1:
            o_ref[...] = _dot_tn(a_ref[...], b_ref[...]).astype(o_ref.dtype)
            return
        mm = pl.program_id(2)

        @pl.when(mm == 0)
        def _():
            acc[0][...] = jnp.zeros_like(acc[0])

        acc[0][...] += _dot_tn(a_ref[...], b_ref[...])

        @pl.when(mm == nm - 1)
        def _():
            o_ref[...] = acc[0][...].astype(o_ref.dtype)

    return _call(
        comm, body, grid=(c, nk, nm),
        in_specs=[pl.BlockSpec((bm, bk), lambda cc, j, mm: (mm, j)),
                  pl.BlockSpec((bm, n), lambda cc, j, mm: (mm, cc))],
        out_specs=pl.BlockSpec((None, bk, n), lambda cc, j, mm: (cc, j, 0)),
        out_shape=S((c, k, n), out_dtype),
        scratch_shapes=[] if nm == 1 else [pltpu.VMEM((bk, n), F32)],
        compiler_params=_cp("parallel", "parallel", "arbitrary"), name=name)(a, b)


def _swiglu_fwd(hn, wg3, wu3, *, bm, name, comm=None):
    m, k = hn.shape
    c, _, n = wg3.shape

    def body(a_ref, g_ref, u_ref, dgate_ref, dup_ref, act_ref):
        a = a_ref[...]
        gate = _dot(a, g_ref[...])
        up = _dot(a, u_ref[...])
        sg = jax.nn.sigmoid(gate)
        silu = gate * sg
        dgate_ref[...] = (up * (sg * (1.0 + gate * (1.0 - sg)))).astype(dgate_ref.dtype)
        dup_ref[...] = silu.astype(dup_ref.dtype)
        act_ref[...] = (silu * up).astype(act_ref.dtype)

    wspec = pl.BlockSpec((None, k, n), lambda cc, i: (cc, 0, 0))
    ospec = pl.BlockSpec((bm, n), lambda cc, i: (i, cc))
    return _call(
        comm, body, grid=(c, m // bm),
        in_specs=[pl.BlockSpec((bm, k), lambda cc, i: (i, 0)), wspec, wspec],
        out_specs=[ospec, ospec, ospec],
        out_shape=[S((m, c * n), _MXU), S((m, c * n), _MXU), S((m, c * n), _MXU)],
        compiler_params=_cp("parallel", "parallel"), name=name)(hn, wg3, wu3)


def _swiglu_bwd(df, wd, act_dgate, act_dup, *, bm, bo, name):
    m, k = df.shape
    ko = wd.shape[0]

    def body(a_ref, b_ref, g_ref, u_ref, dg_ref, du_ref):
        dact = _dot_nt(a_ref[...], b_ref[...])
        dg_ref[...] = (dact * g_ref[...].astype(F32)).astype(dg_ref.dtype)
        du_ref[...] = (dact * u_ref[...].astype(F32)).astype(du_ref.dtype)

    ospec = pl.BlockSpec((bm, bo), lambda j, i: (i, j))
    return pl.pallas_call(
        body, grid=(ko // bo, m // bm),
        in_specs=[pl.BlockSpec((bm, k), lambda j, i: (i, 0)), pl.BlockSpec((bo, k), lambda j, i: (j, 0)), ospec, ospec],
        out_specs=[ospec, ospec],
        out_shape=[S((m, ko), _MXU), S((m, ko), _MXU)],
        compiler_params=_cp("parallel", "parallel"), name=name)(df, wd, act_dgate, act_dup)


def _rms_fwd(x, gain, name):
    t, d = x.shape
    tm = min(t, ROW_TILE)

    def body(x_ref, g_ref, o_ref):
        xv = x_ref[...]
        o_ref[...] = ((xv * _rstd(xv)) * g_ref[...]).astype(o_ref.dtype)

    return pl.pallas_call(body, grid=(t // tm,), in_specs=[_row_spec(tm, d), _vec_spec(d)], out_specs=_row_spec(tm, d),
                          out_shape=S((t, d), _MXU), compiler_params=_cp("parallel"), name=name)(x, gain)


def _outnorm_fwd(o, yl, ga, gl, name, comm=None):
    t, w = o.shape
    tm = min(t, ROW_TILE)

    def body(o_ref, l_ref, ga_ref, gl_ref, y_ref):
        ov, lv = o_ref[...], l_ref[...]
        y_ref[:, :w] = ((ov * _rstd(ov)) * ga_ref[...]).astype(y_ref.dtype)
        y_ref[:, w:] = ((lv * _rstd(lv)) * gl_ref[...]).astype(y_ref.dtype)

    return _call(comm, body, grid=(t // tm,), in_specs=[_row_spec(tm, w), _row_spec(tm, w), _vec_spec(w), _vec_spec(w)],
                 out_specs=_row_spec(tm, 2 * w), out_shape=S((t, 2 * w), _MXU),
                 compiler_params=_cp("parallel"), name=name)(o, yl, ga, gl)


def _mid_fwd(x, mix, g_post, g_pre, name, comm=None):
    t, d = x.shape
    tm = min(t, ROW_TILE)

    def body(x_ref, m_ref, gp_ref, gn_ref, x2_ref, hn_ref):
        mv = m_ref[...]
        x2 = x_ref[...] + (mv * _rstd(mv)) * gp_ref[...]
        x2_ref[...] = x2
        hn_ref[...] = ((x2 * _rstd(x2)) * gn_ref[...]).astype(hn_ref.dtype)

    return _call(comm, body, grid=(t // tm,), in_specs=[_row_spec(tm, d), _row_spec(tm, d), _vec_spec(d), _vec_spec(d)],
                          out_specs=[_row_spec(tm, d), _row_spec(tm, d)], out_shape=[S((t, d), F32), S((t, d), _MXU)],
                          compiler_params=_cp("parallel"), name=name)(x, mix, g_post, g_pre)


def _final(f, x2, target, g_post, name):
    t, d = f.shape
    tm = min(t, ROW_TILE // 2)

    def body(f_ref, x2_ref, t_ref, g_ref, loss_ref, dout_ref, df_ref, dg_ref):
        @pl.when(pl.program_id(0) == 0)
        def _():
            loss_ref[...] = jnp.zeros_like(loss_ref)
            dg_ref[...] = jnp.zeros_like(dg_ref)

        fv = f_ref[...]
        r = _rstd(fv)
        fh = fv * r
        err = (x2_ref[...] + fh * g_ref[...]) - t_ref[...]
        loss_ref[...] += jnp.sum(err * err, axis=0, keepdims=True)
        dout = err * (1.0 / d)
        dout_ref[...] = dout
        dfv, dg = _rms_bwd(dout, fh, r, g_ref[...])
        df_ref[...] = dfv.astype(df_ref.dtype)
        dg_ref[...] += dg

    return pl.pallas_call(
        body, grid=(t // tm,),
        in_specs=[_row_spec(tm, d), _row_spec(tm, d), _row_spec(tm, d), _vec_spec(d)],
        out_specs=[_vec_spec(d), _row_spec(tm, d), _row_spec(tm, d), _vec_spec(d)],
        out_shape=[S((1, d), F32), S((t, d), F32), S((t, d), _MXU), S((1, d), F32)],
        compiler_params=_cp("arbitrary"), name=name)(f, x2, target, g_post)


def _mid_bwd(dhn_a, dhn_b, dout, x2, mix, g_pre, g_post, name, comm=None):
    t, d = x2.shape
    tm = min(t, ROW_TILE // 2)

    def body(da_ref, db_ref, do_ref, x2_ref, m_ref, gn_ref, gp_ref, dx2_ref, dm_ref, dgn_ref, dgp_ref):
        @pl.when(pl.program_id(0) == 0)
        def _():
            dgn_ref[...] = jnp.zeros_like(dgn_ref)
            dgp_ref[...] = jnp.zeros_like(dgp_ref)

        x2 = x2_ref[...]
        r = _rstd(x2)
        dxa, dgn = _rms_bwd(da_ref[...] + db_ref[...], x2 * r, r, gn_ref[...])
        dx2 = do_ref[...] + dxa
        dx2_ref[...] = dx2
        dgn_ref[...] += dgn
        mv = m_ref[...]
        rm = _rstd(mv)
        dmv, dgp = _rms_bwd(dx2, mv * rm, rm, gp_ref[...])
        dm_ref[...] = dmv.astype(dm_ref.dtype)
        dgp_ref[...] += dgp

    rs, vs = _row_spec(tm, d), _vec_spec(d)
    return _call(
        comm, body, grid=(t // tm,), in_specs=[rs, rs, rs, rs, rs, vs, vs], out_specs=[rs, rs, vs, vs],
        out_shape=[S((t, d), F32), S((t, d), _MXU), S((1, d), F32), S((1, d), F32)],
        compiler_params=_cp("arbitrary"), name=name)(dhn_a, dhn_b, dout, x2, mix, g_pre, g_post)


def _first_bwd(dhn, dx2, x, gain, name, comm=None):
    t, d = x.shape
    tm = min(t, ROW_TILE)

    def body(dh_ref, dx2_ref, x_ref, g_ref, dx_ref, dg_ref):
        @pl.when(pl.program_id(0) == 0)
        def _():
            dg_ref[...] = jnp.zeros_like(dg_ref)

        xv = x_ref[...]
        r = _rstd(xv)
        dxa, dg = _rms_bwd(dh_ref[...], xv * r, r, g_ref[...])
        dx_ref[...] = dx2_ref[...] + dxa
        dg_ref[...] += dg

    rs, vs = _row_spec(tm, d), _vec_spec(d)
    return _call(comm, body, grid=(t // tm,), in_specs=[rs, rs, rs, vs], out_specs=[rs, vs],
                          out_shape=[S((t, d), F32), S((1, d), F32)], compiler_params=_cp("arbitrary"), name=name)(dhn, dx2, x, gain)


def _outnorm_bwd(dy, o, yl, ga, gl, name, comm=None):
    t, w = o.shape
    tm = min(t, ROW_TILE)

    def body(dy_ref, o_ref, l_ref, ga_ref, gl_ref, do_ref, dl_ref, dga_ref, dgl_ref):
        @pl.when(pl.program_id(0) == 0)
        def _():
            dga_ref[...] = jnp.zeros_like(dga_ref)
            dgl_ref[...] = jnp.zeros_like(dgl_ref)

        ov, lv = o_ref[...], l_ref[...]
        ra, rl = _rstd(ov), _rstd(lv)
        dov, dga = _rms_bwd(dy_ref[:, :w], ov * ra, ra, ga_ref[...])
        dlv, dgl = _rms_bwd(dy_ref[:, w:], lv * rl, rl, gl_ref[...])
        do_ref[...] = dov.astype(do_ref.dtype)
        dl_ref[...] = dlv
        dga_ref[...] += dga
        dgl_ref[...] += dgl

    rs, vs = _row_spec(tm, w), _vec_spec(w)
    return _call(comm, body, grid=(t // tm,), in_specs=[_row_spec(tm, 2 * w), rs, rs, vs, vs], out_specs=[rs, rs, vs, vs],
                          out_shape=[S((t, w), _MXU), S((t, w), F32), S((1, w), F32), S((1, w), F32)],
                          compiler_params=_cp("arbitrary"), name=name)(dy, o, yl, ga, gl)


def _tri_sum(v, tri):
    return _dot(v.astype(_MXU), tri)


def _attn_tile(qb, kb, row, col, shift, scale):
    z = _dot_nt(qb, kb) * scale
    mask = (col + shift) < row
    lb = _log_sigmoid(z)
    lm = jnp.where(mask, lb - z, 0.0)
    return mask, lb, lm


def _attn_fwd(proj, n_heads, name, comm=None):
    t = proj.shape[0]
    bq = min(t, ATTN_BLOCK)
    nq = t // bq
    scale = 1.0 / math.sqrt(HEAD_DIM)

    heads = [slice(a * HEAD_DIM, (a + 1) * HEAD_DIM) for a in range(ATTN_HEADS)]

    def body(q_ref, k_ref, v_ref, o_ref):
        row = lax.broadcasted_iota(jnp.int32, (bq, bq), 0)
        col = lax.broadcasted_iota(jnp.int32, (bq, bq), 1)
        tri = (row > col).astype(_MXU)

        def per_q(qi, _):
            q0 = pl.multiple_of(qi * bq, bq)
            qbs = [q_ref[pl.ds(q0, bq), hd] for hd in heads]

            def cond(st):
                return jnp.logical_and(st[0] >= 0, st[1])

            def step(st):
                kj, _, carries, accs = st
                k0 = pl.multiple_of(kj * bq, bq)
                alive, new_carries, new_accs = None, [], []
                for hd, qb, carry, acc in zip(heads, qbs, carries, accs):
                    mask, lb, lm = _attn_tile(qb, k_ref[pl.ds(k0, bq), hd], row, col, (kj - qi) * bq, scale)
                    w = jnp.where(mask, jnp.exp(lb + _tri_sum(lm, tri) + carry), 0.0)
                    new_accs.append(acc + _dot(w.astype(_MXU), v_ref[pl.ds(k0, bq), hd]))
                    carry = carry + jnp.sum(lm, axis=1, keepdims=True)
                    new_carries.append(carry)
                    live = jnp.max(carry) > EXP_CUT
                    alive = live if alive is None else jnp.logical_or(alive, live)
                return kj - 1, alive, tuple(new_carries), tuple(new_accs)

            st = lax.while_loop(cond, step, (qi, jnp.bool_(True), (jnp.zeros((bq, 1), F32),) * ATTN_HEADS,
                                             (jnp.zeros((bq, HEAD_DIM), F32),) * ATTN_HEADS))
            for hd, acc in zip(heads, st[3]):
                o_ref[pl.ds(q0, bq), hd] = acc
            return 0

        lax.fori_loop(0, nq, per_q, 0)

    groups = n_heads // ATTN_HEADS
    hs = lambda off: pl.BlockSpec((t, ATTN_HEADS * HEAD_DIM), lambda h: (0, off + h))
    return _call(
        comm, body, grid=(groups,), in_specs=[hs(0), hs(groups), hs(2 * groups)], out_specs=hs(0),
        out_shape=S((t, n_heads * HEAD_DIM), F32), compiler_params=_cp("parallel"), name=name)(proj, proj, proj)


def _emit(blocks, out_ref, starts, sems):
    copies = [pltpu.make_async_copy(b, out_ref.at[:, pl.ds(c0, b.shape[1])], sems.at[k]) for k, (b, c0) in enumerate(zip(blocks, starts))]
    for cp in copies:
        cp.start()
    for cp in copies:
        cp.wait()


def _attn_bwd(proj, do, dproj, n_heads, name, comm=None):
    t = proj.shape[0]
    bq = min(t, ATTN_BLOCK)
    nq = t // bq
    scale = 1.0 / math.sqrt(HEAD_DIM)
    groups = n_heads // ATTN_HEADS
    wide = ATTN_HEADS * HEAD_DIM

    heads = [slice(a * HEAD_DIM, (a + 1) * HEAD_DIM) for a in range(ATTN_HEADS)]

    def body(q_ref, k_ref, v_ref, do_ref, _, dproj_ref, dka_ref, dva_ref, g_ref, b_ref, dq_ref, dk_ref, dv_ref, out_sems):
        group = pl.program_id(0)
        dka_ref[...] = jnp.zeros_like(dka_ref)
        dva_ref[...] = jnp.zeros_like(dva_ref)
        row = lax.broadcasted_iota(jnp.int32, (bq, bq), 0)
        col = lax.broadcasted_iota(jnp.int32, (bq, bq), 1)
        tri = (row > col).astype(_MXU)
        tri_lt = (row < col).astype(_MXU)

        def per_q(qi, _):
            q0 = pl.multiple_of(qi * bq, bq)
            qbs = [q_ref[pl.ds(q0, bq), hd] for hd in heads]
            dobs = [do_ref[pl.ds(q0, bq), hd] for hd in heads]

            def cond(st):
                return jnp.logical_and(st[0] >= 0, st[1])

            def step(st):
                kj, _, carries = st
                k0 = pl.multiple_of(kj * bq, bq)
                alive, new_carries = None, []
                for a, (hd, qb, dob, carry) in enumerate(zip(heads, qbs, dobs, carries)):
                    mask, lb, lm = _attn_tile(qb, k_ref[pl.ds(k0, bq), hd], row, col, (kj - qi) * bq, scale)
                    w = jnp.where(mask, jnp.exp(lb + _tri_sum(lm, tri) + carry), 0.0)
                    g_ref[a, pl.ds(k0, bq), :] = w * _dot_nt(dob, v_ref[pl.ds(k0, bq), hd])
                    b_ref[a, pl.ds(k0, bq), :] = jnp.where(mask, jnp.exp(lb), 0.0)
                    dva_ref[pl.ds(k0, bq), hd] += _dot_tn(w.astype(_MXU), dob)
                    carry = carry + jnp.sum(lm, axis=1, keepdims=True)
                    new_carries.append(carry)
                    live = jnp.max(carry) > EXP_CUT
                    alive = live if alive is None else jnp.logical_or(alive, live)
                return kj - 1, alive, tuple(new_carries)

            st = lax.while_loop(cond, step, (qi, jnp.bool_(True), (jnp.zeros((bq, 1), F32),) * ATTN_HEADS))

            def back(kj, st2):
                k0 = pl.multiple_of(kj * bq, bq)
                out = []
                for a, (hd, qb, (before, dq)) in enumerate(zip(heads, qbs, st2)):
                    g = g_ref[a, pl.ds(k0, bq), :]
                    beta = b_ref[a, pl.ds(k0, bq), :]
                    dz = ((g * (1.0 - beta) - (before + _tri_sum(g, tri_lt)) * beta) * scale).astype(_MXU)
                    dka_ref[pl.ds(k0, bq), hd] += _dot_tn(dz, qb)
                    out.append((before + jnp.sum(g, axis=1, keepdims=True), dq + _dot(dz, k_ref[pl.ds(k0, bq), hd])))
                return tuple(out)

            st2 = lax.fori_loop(st[0] + 1, qi + 1, back, ((jnp.zeros((bq, 1), F32), jnp.zeros((bq, HEAD_DIM), F32)),) * ATTN_HEADS)
            for hd, (_, dq) in zip(heads, st2):
                dq_ref[pl.ds(q0, bq), hd] = dq.astype(dq_ref.dtype)
            return 0

        lax.fori_loop(0, nq, per_q, 0)
        dk_ref[...] = dka_ref[...].astype(dk_ref.dtype)
        dv_ref[...] = dva_ref[...].astype(dv_ref.dtype)
        _emit([dq_ref, dk_ref, dv_ref], dproj_ref, [(a * groups + group) * wide for a in range(3)], out_sems)

    hs = lambda off: pl.BlockSpec((t, wide), lambda h: (0, off + h))
    return _call(
        comm, body, grid=(groups,), in_specs=[hs(0), hs(groups), hs(2 * groups), hs(0), _ANY], out_specs=_ANY,
        out_shape=S(dproj.shape, dproj.dtype), input_output_aliases={4: 0},
        scratch_shapes=[pltpu.VMEM((t, wide), F32), pltpu.VMEM((t, wide), F32),
                        pltpu.VMEM((ATTN_HEADS, t, bq), F32), pltpu.VMEM((ATTN_HEADS, t, bq), F32)]
        + [pltpu.VMEM((t, wide), dproj.dtype)] * 3 + [pltpu.SemaphoreType.DMA((3,))],
        compiler_params=_cp("parallel"), name=name)(proj, proj, proj, do, dproj)


def _shift_down(cur, prev8, k):
    if k == 0:
        return cur
    row8 = lax.broadcasted_iota(jnp.int32, prev8.shape, 0)
    rc = pltpu.roll(cur, k, 0)
    top = jnp.where(row8 < k, pltpu.roll(prev8, k, 0), rc[0:8, :])
    return jnp.concatenate([top, rc[8:, :]], axis=0)


def _shift_up(cur, next8, k):
    if k == 0:
        return cur
    n = cur.shape[0]
    row8 = lax.broadcasted_iota(jnp.int32, next8.shape, 0)
    rc = pltpu.roll(cur, n - k, 0)
    bottom = jnp.where(row8 >= 8 - k, pltpu.roll(next8, 8 - k, 0), rc[n - 8:, :])
    return jnp.concatenate([rc[:n - 8, :], bottom], axis=0)


def _lru_conv(xl, prev8, cw, cb):
    xs = [_shift_down(xl, prev8, CONV_WIDTH - 1 - k) for k in range(CONV_WIDTH)]
    xc = xs[0] * cw[0:1, :]
    for k in range(1, CONV_WIDTH):
        xc = xc + xs[k] * cw[k:k + 1, :]
    return xs, xc + cb


def _lru_gates(xl, prev8, cw, cb, wr, br, wi, bi, ls):
    xs, xc = _lru_conv(xl, prev8, cw, cb)
    xcb = xc.astype(_MXU)
    r = jax.nn.sigmoid(_dot(xcb, wr) + br)
    i = jax.nn.sigmoid(_dot(xcb, wi) + bi)
    la = (LRU_C * r) * ls
    a = jnp.exp(la)
    mult = jnp.sqrt(-_expm1(2.0 * la))
    return xs, xc, r, i, a, mult


def _group_scan(a, b, reverse):
    n = a.shape[0]
    row = lax.broadcasted_iota(jnp.int32, a.shape, 0) % 8
    for d in (1, 2, 4):
        if reverse:
            m = row < 8 - d
            a_s, b_s = pltpu.roll(a, n - d, 0), pltpu.roll(b, n - d, 0)
        else:
            m = row >= d
            a_s, b_s = pltpu.roll(a, d, 0), pltpu.roll(b, d, 0)
        b = jnp.where(m, a * b_s + b, b)
        a = jnp.where(m, a * a_s, a)
    return a, b


def _lru_fwd(proj, col0, n_blocks, cw, cb, wr, br, wi, bi, lam, name, comm=None):
    t = proj.shape[0]
    tt = min(t, SEQ_TILE)
    nt = t // tt

    def body(xl_ref, gl_ref, cw_ref, cb_ref, wr_ref, br_ref, wi_ref, bi_ref, lam_ref, h_ref, y_ref, *kept):
        cwv, cbv, brv, biv = cw_ref[...], cb_ref[...], br_ref[...], bi_ref[...]
        wrv, wiv = wr_ref[...].astype(_MXU), wi_ref[...].astype(_MXU)
        ls = _log_sigmoid(lam_ref[...])

        def tile(ti, hin):
            t0 = pl.multiple_of(ti * tt, tt)
            p0 = pl.multiple_of(jnp.maximum(t0 - 8, 0), 8)
            prev8 = xl_ref[pl.ds(p0, 8), :] * (ti > 0).astype(F32)
            xl = xl_ref[pl.ds(t0, tt), :]
            _, xc, r, ig, a, mult = _lru_gates(xl, prev8, cwv, cbv, wrv, brv, wiv, biv, ls)
            for ref, val in zip(kept, (r, ig, a, mult)):
                ref[pl.ds(t0, tt), :] = val
            ga, gb = _group_scan(a, mult * (ig * xc), False)
            for g in range(tt // 8):
                hg = ga[8 * g:8 * g + 8, :] * hin + gb[8 * g:8 * g + 8, :]
                h_ref[pl.ds(t0 + 8 * g, 8), :] = hg
                hin = hg[7:8, :]
            y_ref[pl.ds(t0, tt), :] = h_ref[pl.ds(t0, tt), :] * _gelu(gl_ref[pl.ds(t0, tt), :])
            return hin

        lax.fori_loop(0, nt, tile, jnp.zeros((1, HEAD_DIM), F32))

    cs = lambda off: pl.BlockSpec((t, HEAD_DIM), lambda n: (0, off + n))
    vs = pl.BlockSpec((1, HEAD_DIM), lambda n: (0, n))
    ws = pl.BlockSpec((None, HEAD_DIM, HEAD_DIM), lambda n: (n, 0, 0))
    w = n_blocks * HEAD_DIM
    return _call(
        comm, body, grid=(n_blocks,),
        in_specs=[cs(col0), cs(col0 + n_blocks), pl.BlockSpec((CONV_WIDTH, HEAD_DIM), lambda n: (0, n)), vs, ws, vs, ws, vs, vs],
        out_specs=[cs(0)] * 6, out_shape=[S((t, w), F32)] * 6,
        compiler_params=_cp("parallel"), name=name)(proj, proj, cw, cb, wr, br, wi, bi, lam)


def _lru_bwd(proj, col0, n_blocks, h, kept, dyl, cw, cb, wr, wi, lam, name, comm=None):
    t = proj.shape[0]
    tt = min(t, SEQ_TILE)
    nt = t // tt

    def body(xl_ref, gl_ref, h_ref, r_ref, i_ref, a_ref, m_ref, dy_ref, cw_ref, cb_ref, wr_ref, wi_ref, lam_ref,
             dproj_ref, dcw_ref, dcb_ref, dwr_ref, dbr_ref, dwi_ref, dbi_ref, dlam_ref, g_ref, dxl_ref, dgl_ref, out_sems):
        block = pl.program_id(0)
        cwv, cbv = cw_ref[...], cb_ref[...]
        wrv, wiv = wr_ref[...].astype(_MXU), wi_ref[...].astype(_MXU)
        lamv = lam_ref[...]
        ls = _log_sigmoid(lamv)
        for ref in (dcw_ref, dcb_ref, dwr_ref, dbr_ref, dwi_ref, dbi_ref, dlam_ref):
            ref[...] = jnp.zeros_like(ref)

        def tile(s, carry):
            e_in, dxc_next8 = carry
            ti = nt - 1 - s
            t0 = pl.multiple_of(ti * tt, tt)
            p0 = pl.multiple_of(jnp.maximum(t0 - 8, 0), 8)
            first = (ti > 0).astype(F32)
            xl = xl_ref[pl.ds(t0, tt), :]
            xs, xc = _lru_conv(xl, xl_ref[pl.ds(p0, 8), :] * first, cwv, cbv)
            r, ig, a, mult = (ref[pl.ds(t0, tt), :] for ref in (r_ref, i_ref, a_ref, m_ref))
            hv = h_ref[pl.ds(t0, tt), :]
            h_before = _shift_down(hv, h_ref[pl.ds(p0, 8), :] * first, 1)
            glv = gl_ref[pl.ds(t0, tt), :]
            dyv = dy_ref[pl.ds(t0, tt), :]
            dgl_ref[pl.ds(t0, tt), :] = (dyv * hv * _gelu_grad(glv)).astype(dgl_ref.dtype)
            dh = dyv * _gelu(glv)
            row = lax.broadcasted_iota(jnp.int32, a.shape, 0)
            coef = jnp.where(row == tt - 1, 1.0, pltpu.roll(a, tt - 1, 0))
            ga, gb = _group_scan(coef, dh, True)
            gin = e_in
            for g in reversed(range(tt // 8)):
                gg = ga[8 * g:8 * g + 8, :] * gin + gb[8 * g:8 * g + 8, :]
                g_ref[8 * g:8 * g + 8, :] = gg
                gin = gg[0:1, :]
            gv = g_ref[...]
            e_out = a[0:1, :] * gv[0:1, :]
            ix = ig * xc
            dla = (gv * h_before) * a - (gv * ix) * (a * a / mult)
            dlam_ref[...] += jnp.sum(dla * (LRU_C * r), axis=0, keepdims=True)
            dpr = (dla * (LRU_C * ls)) * (r * (1.0 - r))
            dpi = (gv * mult * xc) * (ig * (1.0 - ig))
            dbr_ref[...] += jnp.sum(dpr, axis=0, keepdims=True)
            dbi_ref[...] += jnp.sum(dpi, axis=0, keepdims=True)
            xcb, dprb, dpib = xc.astype(_MXU), dpr.astype(_MXU), dpi.astype(_MXU)
            dwr_ref[...] += _dot_tn(xcb, dprb)
            dwi_ref[...] += _dot_tn(xcb, dpib)
            dxc = gv * mult * ig + _dot_nt(dprb, wrv) + _dot_nt(dpib, wiv)
            dcb_ref[...] += jnp.sum(dxc, axis=0, keepdims=True)
            dxl = None
            for k in range(CONV_WIDTH):
                dcw_ref[k:k + 1, :] += jnp.sum(dxc * xs[k], axis=0, keepdims=True)
                term = _shift_up(dxc, dxc_next8, CONV_WIDTH - 1 - k) * cwv[k:k + 1, :]
                dxl = term if dxl is None else dxl + term
            dxl_ref[pl.ds(t0, tt), :] = dxl.astype(dxl_ref.dtype)
            return e_out, dxc[0:8, :]

        lax.fori_loop(0, nt, tile, (jnp.zeros((1, HEAD_DIM), F32), jnp.zeros((8, HEAD_DIM), F32)))
        dlam_ref[...] = dlam_ref[...] * (1.0 - jax.nn.sigmoid(lamv))
        _emit([dxl_ref, dgl_ref], dproj_ref, [(col0 + block) * HEAD_DIM, (col0 + n_blocks + block) * HEAD_DIM], out_sems)

    cs = lambda off: pl.BlockSpec((t, HEAD_DIM), lambda n: (0, off + n))
    vs = pl.BlockSpec((1, HEAD_DIM), lambda n: (0, n))
    ws = pl.BlockSpec((None, HEAD_DIM, HEAD_DIM), lambda n: (n, 0, 0))
    cws = pl.BlockSpec((CONV_WIDTH, HEAD_DIM), lambda n: (0, n))
    w = n_blocks * HEAD_DIM
    vec = S((1, w), F32)
    mat = S((n_blocks, HEAD_DIM, HEAD_DIM), F32)
    return _call(
        comm, body, grid=(n_blocks,),
        in_specs=[cs(col0), cs(col0 + n_blocks)] + [cs(0)] * 6 + [cws, vs, ws, ws, vs],
        out_specs=[_ANY, cws, vs, ws, vs, ws, vs, vs],
        out_shape=[S(proj.shape, _MXU), S((CONV_WIDTH, w), F32), vec, mat, vec, mat, vec, vec],
        scratch_shapes=[pltpu.VMEM((tt, HEAD_DIM), F32), pltpu.VMEM((t, HEAD_DIM), _MXU), pltpu.VMEM((t, HEAD_DIM), _MXU),
                        pltpu.SemaphoreType.DMA((2,))],
        compiler_params=_cp("parallel"), name=name)(proj, proj, h, *kept, dyl, cw, cb, wr, wi, lam)


class _NoExchange:
    grad_dtype = F32

    def __init__(self, weights):
        self.weights, self.grads, self.packs = weights, {}, {}

    def weight(self, name):
        return self.weights[name]

    def in_proj(self, x, gain, bm):
        hn = _rms_fwd(x, gain, "rms1")
        return [hn, *_mm_nn(hn, self.weights["w_in"], bm=bm, bn=self.weights["w_in"].shape[2], name="in_proj", also=_MXU)]

    def conv_w(self):
        return self.weights["conv_w"]

    def carrier(self, call):
        return None

    def harvest(self, car):
        pass

    def alone(self, call):
        pass


def _local_step(x, target, norms, ex, cb, wr, br, wi, bi, lam, ga, gl):
    g_pre_mix, g_post_mix, g_pre_ffn, g_post_ffn = norms
    t, d = x.shape
    bm = min(t, MM_ROWS)
    bt = min(t, DW_TOKENS)

    def run(fn, name, *args, **kw):
        car = ex.carrier(name)
        out = fn(*args, name=name, comm=car, **kw)
        ex.harvest(car)
        return out

    hn1, proj, proj_mx = ex.in_proj(x, g_pre_mix, bm)
    win3, cw = ex.weight("w_in"), ex.conv_w()
    c = win3.shape[0]
    o = run(_attn_fwd, "attn_fwd", proj_mx, (proj.shape[1] - d) // 3 // HEAD_DIM)
    mix = 2 * o.shape[1]
    n_heads = n_blocks = o.shape[1] // HEAD_DIM
    h, yl, *kept = run(_lru_fwd, "lru_fwd", proj, 3 * n_heads, n_blocks, cw, cb, wr, br, wi, bi, lam)
    y = run(_outnorm_fwd, "outnorm_fwd", o, yl, ga, gl)
    wout = ex.weight("w_out")
    mixo = run(_mm_nn, "out_proj", y, wout[None], bm=bm, bn=d)
    x2, hn2 = run(_mid_fwd, "mid_fwd", x, mixo, g_post_mix, g_pre_ffn)
    ex.alone("gather_w_up_last")
    wg3, wu3 = ex.weight("w_ffn_gate"), ex.weight("w_ffn_up")
    act_dgate, act_dup, act = run(_swiglu_fwd, "ffn_gate_up", hn2, wg3, wu3, bm=bm)
    ex.alone("gather_w_down")
    wd = ex.weight("w_ffn_down")
    ff = wd.shape[0]
    f = _mm_nn(act, wd[None], bm=bm, bn=d // 2, name="ffn_down")
    loss_cols, dout, df, dg_post_ffn = _final(f, x2, target, g_post_ffn, "final")

    dgate, dup = _swiglu_bwd(df, wd, act_dgate, act_dup, bm=min(t, 2 * MM_ROWS), bo=ff // 4, name="ffn_down_bwd")
    ex.grads["w_ffn_down"] = _mm_tn(act, df, 1, bm=bt, bk=DW_ROWS, out_dtype=ex.grad_dtype, name="ffn_down_dw").reshape(c, ff // c, d)
    ex.grads["w_ffn_gate"] = run(_mm_tn, "ffn_gate_dw", hn2, dgate, c, bm=bt, bk=d // 2, out_dtype=ex.grad_dtype)
    ex.grads["w_ffn_up"] = run(_mm_tn, "ffn_up_dw", hn2, dup, c, bm=bt, bk=d // 2, out_dtype=ex.grad_dtype)
    dhn2_g = run(_mm_nt, "ffn_gate_dx", dgate, wg3, bm=bm, bo=d // 2, out_dtype=F32)
    dhn2_u = run(_mm_nt, "ffn_up_dx", dup, wu3, bm=bm, bo=d // 2, out_dtype=F32)
    dx2, dmix, dg_pre_ffn, dg_post_mix = run(_mid_bwd, "mid_bwd", dhn2_g, dhn2_u, dout, x2, mixo, g_pre_ffn, g_post_mix)
    dy = run(_mm_nt, "out_proj_dx", dmix, wout[None], bm=bm, bo=mix, out_dtype=F32)
    ex.grads["w_out"] = _mm_tn(y, dmix, 1, bm=bt, bk=mix // 4, out_dtype=ex.grad_dtype, name="out_proj_dw").reshape(c, mix // c, d)
    do, dyl, dga, dgl_norm = run(_outnorm_bwd, "outnorm_bwd", dy, o, yl, ga, gl)
    dproj, dcw, dcb, dwr, dbr, dwi, dbi, dlam = run(_lru_bwd, "lru_bwd", proj, 3 * n_heads, n_blocks, h, kept, dyl, cw, cb, wr, wi, lam)
    small = dict(post_mix_norm=dg_post_mix, pre_ffn_norm=dg_pre_ffn, post_ffn_norm=dg_post_ffn, conv_w=dcw, conv_b=dcb,
                 w_rgate=dwr, b_rgate=dbr, w_igate=dwi, b_igate=dbi, lru_lambda=dlam, attn_out_norm=dga, lru_out_norm=dgl_norm)
    ex.packs["early"] = _pack([small[n] for n in _SMALL_EARLY])
    dproj = run(_attn_bwd, "attn_bwd", proj_mx, do, dproj, n_heads)
    ex.grads["w_in"] = _mm_tn(hn1, dproj, c, bm=bt, bk=d // 2, out_dtype=ex.grad_dtype, name="in_proj_dw")
    ex.alone("grads_w_in_swap")
    dhn1 = run(_mm_nt, "in_proj_dx", dproj, win3, bm=bm, bo=d // 2, out_dtype=F32)
    grad_x, small["pre_mix_norm"] = run(_first_bwd, "first_bwd", dhn1, dx2, x, g_pre_mix)
    ex.packs["late"] = _pack([small["pre_mix_norm"], (0.5 / d) * jnp.sum(loss_cols, keepdims=True)])
    return loss_cols, grad_x, small


def _into_slot(wsh, slot, dtype, name):
    rows, n = wsh.shape
    rb = _row_block(rows, 512) if rows % 8 == 0 else rows

    def body(s_ref, w_ref, o_ref):
        o_ref[...] = w_ref[...].astype(o_ref.dtype)

    return pl.pallas_call(
        body,
        grid_spec=pltpu.PrefetchScalarGridSpec(
            num_scalar_prefetch=1, grid=(rows // rb,),
            in_specs=[pl.BlockSpec((rb, n), lambda i, s_ref: (i, 0))],
            out_specs=pl.BlockSpec((None, rb, n), lambda i, s_ref: (s_ref[0], i, 0))),
        out_shape=S((4, rows, n), dtype), compiler_params=_cp("parallel"), name=name)(slot, wsh)


class _Exchange:
    SCHEDULE = {
        "in_proj": [("stream", "w_in"), ("ici", "conv_w"), ("ici", "w_ffn_up", 0)],
        "attn_fwd": [("d2d", "w_ffn_up", 0), ("ici", "w_ffn_gate")],
        "lru_fwd": [("d2d", "w_ffn_gate"), ("ici", "w_out"), ("ici", "w_ffn_up", 1)],
        "outnorm_fwd": [("d2d", "w_out"), ("d2d", "w_ffn_up", 1)],
        "out_proj": [("ici", "w_ffn_up", 2)],
        "mid_fwd": [("d2d", "w_ffn_up", 2), ("ici", "w_ffn_up", 3)],
        "gather_w_up_last": [("d2d", "w_ffn_up", 3)],
        "ffn_gate_up": [("ici", "w_ffn_down")],
        "gather_w_down": [("d2d", "w_ffn_down")],
        "ffn_gate_dw": [("swap", "w_ffn_down")],
        "ffn_up_dw": [("scatter", "w_ffn_down", 0), ("scatter", "w_ffn_down", 1), ("scatter", "w_ffn_down", 2), ("swap", "w_ffn_gate")],
        "ffn_gate_dx": [("scatter", "w_ffn_down", 3), ("scatter", "w_ffn_gate", 0), ("swap", "w_ffn_up")],
        "ffn_up_dx": [("share", "w_ffn_down"), ("scatter", "w_ffn_gate", 1), ("scatter", "w_ffn_gate", 2)],
        "mid_bwd": [("scatter", "w_ffn_gate", 3), ("scatter", "w_ffn_up", 0)],
        "out_proj_dx": [("share", "w_ffn_gate"), ("scatter", "w_ffn_up", 1)],
        "outnorm_bwd": [("scatter", "w_ffn_up", 2), ("swap", "w_out")],
        "lru_bwd": [("scatter", "w_ffn_up", 3), ("scatter", "w_out")],
        "attn_bwd": [("share", "w_ffn_up"), ("share", "w_out"), ("spread", "early")],
        "grads_w_in_swap": [("swap", "w_in")],
        "in_proj_dx": [("scatter", "w_in")],
        "grads_w_in_share": [("share", "w_in"), ("spread", "late")],
    }
    PIECES = 4
    grad_dtype = BF16

    def __init__(self, slots, place):
        self.buf, self.place = dict(slots), place
        self.grads, self.packs, self.swapped, self.part, self.scattered, self.full, self.spreaded = {}, {}, {}, {}, {}, {}, {}

    def weight(self, name):
        b = self.buf[name]
        return b.reshape(-1, b.shape[2]) if name in ("w_out", "w_ffn_down") else b

    def in_proj(self, x, gain, bm):
        car = self.carrier("in_proj")
        out = _in_proj_streamed(x, gain, car, car.streamed, self.place, bm=bm, name="in_proj")
        self.harvest(car)
        return out

    def conv_w(self):
        return jnp.transpose(self.buf["conv_w"], (1, 0, 2)).reshape(CONV_WIDTH, -1)

    def carrier(self, call):
        if call not in self.SCHEDULE:
            return None
        car = _Carrier()
        car.todo, slot = [], {}
        for kind, name, *piece in self.SCHEDULE[call]:
            if kind in ("ici", "d2d", "stream"):
                if name not in slot:
                    slot[name] = car.inplace(self.buf[name])
                    car.todo.append((self.buf, name, slot[name]))
            if kind == "stream":
                car.streamed = slot[name]
            elif kind in ("ici", "d2d"):
                size = self.buf[name].shape[1] // 2 // self.PIECES
                rows = (piece[0] * size, size) if piece else None
                if kind == "ici":
                    car.gather_ici(slot[name], rows, split=name != "conv_w")
                else:
                    car.gather_d2d(slot[name], rows)
            elif kind == "swap":
                g = self.grads[name]
                o = car.fresh((4, g.shape[1] // 2, g.shape[2]), g.dtype)
                car.swap(car.read(g), o)
                car.todo.append((self.swapped, name, o))
            elif kind == "scatter":
                if name not in self.part:
                    self.part[name] = _add_own_half(self.grads[name], self.swapped[name], self.place[1:], "grads_add_" + name)
                p = self.part[name]
                key = ("scatter", name)
                if key not in slot:
                    slot[key] = (car.read(p), car.inplace(self.scattered[name]) if name in self.scattered else car.fresh(p.shape, p.dtype))
                    car.todo.append((self.scattered, name, slot[key][1]))
                size = p.shape[1] // self.PIECES
                car.scatter(*slot[key], (piece[0] * size, size) if piece else None)
            elif kind == "share":
                o = car.inplace(_sum_chips(self.part[name], self.scattered[name], self.place, "grads_sum_" + name))
                car.share(o)
                car.todo.append((self.full, name, o))
            else:
                o = car.fresh((8,) + self.packs[name].shape, F32)
                car.spread(car.read(self.packs[name]), o)
                car.todo.append((self.spreaded, name, o))
        return car

    def harvest(self, car):
        for state, name, o in (car.todo if car is not None else []):
            state[name] = car.results[o]

    def alone(self, call):
        car = self.carrier(call)
        car.run_alone(call)
        self.harvest(car)

    def small_sum(self, key):
        return _sum_devices(self.packs[key], self.spreaded[key], 2 * self.place[0:1] + self.place[1:], "grads_small_sum_" + key)


def _row_block(rows, cap):
    return max(b for b in range(8, cap + 1, 8) if rows % b == 0)


def _add_own_half(g, recv, core, name):
    _, rows, n = g.shape
    half = rows // 2
    rb = _row_block(half, 1024)
    nb = half // rb

    def body(c_ref, g_ref, r_ref, o_ref):
        o_ref[...] = (g_ref[...].astype(F32) + r_ref[...].astype(F32)).astype(o_ref.dtype)

    return pl.pallas_call(
        body,
        grid_spec=pltpu.PrefetchScalarGridSpec(
            num_scalar_prefetch=1, grid=(4, nb),
            in_specs=[pl.BlockSpec((None, rb, n), lambda k, i, c_ref: (k, c_ref[0] * nb + i, 0)),
                      pl.BlockSpec((None, rb, n), lambda k, i, c_ref: (k, i, 0))],
            out_specs=pl.BlockSpec((None, rb, n), lambda k, i, c_ref: (k, i, 0))),
        out_shape=S((4, half, n), BF16), compiler_params=_cp("parallel", "parallel"), name=name)(core, g, recv)


def _sum_chips(part, recv, place, name):
    _, rows, n = part.shape
    rb = _row_block(rows, 256)
    nb = rows // rb

    def body(p_ref, own_ref, r0, r1, r2, r3, o_ref):
        own = own_ref[...].astype(F32)
        terms = [jnp.where(p_ref[0] == k, own, r[...].astype(F32)) for k, r in enumerate((r0, r1, r2, r3))]
        o_ref[...] = ((terms[0] + terms[1]) + terms[2]) + terms[3]

    def slot(k):
        return pl.BlockSpec((None, rb, n), lambda i, p_ref: (jnp.where(p_ref[0] == k, (k + 1) % 4, k), i, 0))

    return pl.pallas_call(
        body,
        grid_spec=pltpu.PrefetchScalarGridSpec(
            num_scalar_prefetch=1, grid=(nb,),
            in_specs=[pl.BlockSpec((None, rb, n), lambda i, p_ref: (p_ref[0], i, 0))] + [slot(k) for k in range(4)],
            out_specs=pl.BlockSpec((rb, n), lambda i, p_ref: (p_ref[1] * nb + i, 0))),
        out_shape=S((2 * rows, n), F32), compiler_params=_cp("parallel"), name=name)(place, part, recv, recv, recv, recv)


def _sum_devices(own, spread, me, name):
    rows = own.shape[0]

    def body(me_ref, own_ref, *refs):
        acc = None
        for k, r in enumerate(refs[:8]):
            term = jnp.where(me_ref[0] == k, own_ref[...], r[...])
            acc = term if acc is None else acc + term
        refs[8][...] = acc

    def slot(k):
        return pl.BlockSpec((None, rows, 128), lambda i, me_ref: (jnp.where(me_ref[0] == k, (k + 1) % 8, k), 0, 0))

    whole = pl.BlockSpec((rows, 128), lambda i, me_ref: (0, 0))
    return pl.pallas_call(
        body,
        grid_spec=pltpu.PrefetchScalarGridSpec(num_scalar_prefetch=1, grid=(1,), in_specs=[whole] + [slot(k) for k in range(8)],
                                               out_specs=whole),
        out_shape=S((rows, 128), F32), compiler_params=_cp("arbitrary"), name=name)(me, own, *[spread] * 8)


def _adamw(w, g, m, v, name, regive=False):
    rows, n = w.shape
    rb = rows if rows * n * 4 <= (1 << 21) else _row_block(rows, 512)
    c1 = 1.0 - ADAM_B1 ** ADAM_STEP
    c2 = 1.0 - ADAM_B2 ** ADAM_STEP

    def body(w_ref, g_ref, m_ref, v_ref, d_ref, nm_ref, nv_ref, *again):
        gv = g_ref[...]
        for ref in again:
            ref[...] = gv
        nm = ADAM_B1 * m_ref[...] + (1.0 - ADAM_B1) * gv
        nv = ADAM_B2 * v_ref[...] + (1.0 - ADAM_B2) * (gv * gv)
        nm_ref[...] = nm
        nv_ref[...] = nv
        d_ref[...] = -ADAM_LR * ((nm / c1) / (jnp.sqrt(nv / c2) + ADAM_EPS) + ADAM_WD * w_ref[...])

    bs = pl.BlockSpec((rb, n), lambda i: (i, 0))
    n_out = 4 if regive else 3
    return pl.pallas_call(body, grid=(rows // rb,), in_specs=[bs] * 4, out_specs=[bs] * n_out, out_shape=[S((rows, n), F32)] * n_out,
                          compiler_params=_cp("parallel"), name=name)(w, g, m, v)


_BIG = ("w_in", "w_out", "w_ffn_gate", "w_ffn_up", "w_ffn_down")
_SMALL = ("pre_mix_norm", "post_mix_norm", "pre_ffn_norm", "post_ffn_norm", "conv_w", "conv_b", "w_rgate", "b_rgate",
          "w_igate", "b_igate", "lru_lambda", "attn_out_norm", "lru_out_norm")
_SMALL_EARLY = _SMALL[1:]
_WEIGHTS = ("pre_mix_norm", "post_mix_norm", "pre_ffn_norm", "post_ffn_norm", "w_in", "conv_w", "conv_b", "w_rgate", "b_rgate",
            "w_igate", "b_igate", "lru_lambda", "attn_out_norm", "lru_out_norm", "w_out", "w_ffn_gate", "w_ffn_up", "w_ffn_down")


def _pack(arrays):
    flat = []
    for a in arrays:
        f = a.reshape(-1)
        flat.append(jnp.pad(f, (0, (-f.shape[0]) % 1024)))
    return jnp.concatenate(flat).reshape(-1, 128)


def _unpack(packed, shapes):
    out, pos = [], 0
    flat = packed.reshape(-1)
    for s in shapes:
        size = math.prod(s)
        out.append(flat[pos:pos + size].reshape(s))
        pos += size + (-size) % 1024
    return out


def kernel(x, pre_mix_norm, post_mix_norm, pre_ffn_norm, post_ffn_norm, w_in, conv_w, conv_b, w_rgate, b_rgate, w_igate, b_igate, lru_lambda, attn_out_norm, lru_out_norm, w_out, w_ffn_gate, w_ffn_up, w_ffn_down, loss_target, m_pre_mix_norm, m_post_mix_norm, m_pre_ffn_norm, m_post_ffn_norm, m_w_in, m_conv_w, m_conv_b, m_w_rgate, m_b_rgate, m_w_igate, m_b_igate, m_lru_lambda, m_attn_out_norm, m_lru_out_norm, m_w_out, m_w_ffn_gate, m_w_ffn_up, m_w_ffn_down, v_pre_mix_norm, v_post_mix_norm, v_pre_ffn_norm, v_post_ffn_norm, v_w_in, v_conv_w, v_conv_b, v_w_rgate, v_b_rgate, v_w_igate, v_b_igate, v_lru_lambda, v_attn_out_norm, v_lru_out_norm, v_w_out, v_w_ffn_gate, v_w_ffn_up, v_w_ffn_down):
    given = dict(locals())
    w = {n: given[n][0] for n in _WEIGHTS}
    m = {n: given["m_" + n][0] for n in _WEIGHTS}
    v = {n: given["v_" + n][0] for n in _WEIGHTS}
    xs, target = x[0], loss_target[0]
    d = xs.shape[1]
    chip = (2 * lax.axis_index("x") + lax.axis_index("y")).astype(jnp.int32)
    place = jnp.stack([chip, lax.axis_index("c").astype(jnp.int32)])

    slots = {n: _into_slot(w[n], place[0:1], _MXU, "slot_" + n) for n in _BIG}
    slots["conv_w"] = _into_slot(w["conv_w"], place[0:1], F32, "slot_conv_w")
    ex = _Exchange(slots, place)
    row = lambda a: a.reshape(1, -1)
    norms = tuple(row(w[n]) for n in ("pre_mix_norm", "post_mix_norm", "pre_ffn_norm", "post_ffn_norm"))

    loss_cols, grad_x, small = _local_step(
        xs, target, norms, ex, row(w["conv_b"]), w["w_rgate"], row(w["b_rgate"]),
        w["w_igate"], row(w["b_igate"]), row(w["lru_lambda"]), row(w["attn_out_norm"]), row(w["lru_out_norm"]))


    ex.alone("grads_w_in_share")
    reduced = {n: ex.full[n] for n in _BIG}
    early = _unpack(ex.small_sum("early"), [small[n].shape for n in _SMALL_EARLY])
    late = _unpack(ex.small_sum("late"), [small["pre_mix_norm"].shape, (1, 1)])
    loss = late[1][0, 0]
    for n, g in zip(_SMALL_EARLY + ("pre_mix_norm",), early + late[:1]):
        reduced[n] = g.reshape(w[n].shape) if n != "conv_w" else lax.dynamic_slice_in_dim(g, chip * w[n].shape[1], w[n].shape[1], axis=1)

    delta, new_m, new_v = {}, {}, {}
    for n in _BIG:
        delta[n], new_m[n], new_v[n], reduced[n] = _adamw(w[n], reduced[n], m[n], v[n], "adamw_" + n, regive=True)
    shapes = [w[n].shape for n in _SMALL]
    packed = _adamw(*[_pack([src[n] for n in _SMALL]) for src in (w, reduced, m, v)], "adamw_small")
    for out, p in zip((delta, new_m, new_v), packed):
        out.update(zip(_SMALL, _unpack(p, shapes)))

    lead = lambda a: a[None]
    return (loss, lead(grad_x), *[lead(reduced[n]) for n in _WEIGHTS], *[lead(delta[n]) for n in _WEIGHTS],
            *[lead(new_m[n]) for n in _WEIGHTS], *[lead(new_v[n]) for n in _WEIGHTS])
```

```python
import functools
import math

import jax
import jax.numpy as jnp
from jax import lax
from jax.experimental import pallas as pl
from jax.experimental.pallas import tpu as pltpu

F32 = jnp.float32
BF16 = jnp.bfloat16
_MXU = BF16
S = jax.ShapeDtypeStruct

RMS_EPS = 1e-6
HEAD_DIM = 128
CONV_WIDTH = 4
LRU_C = 8.0
ADAM_LR, ADAM_B1, ADAM_B2, ADAM_EPS, ADAM_WD, ADAM_STEP = 0.001, 0.9, 0.999, 1e-08, 0.01, 10
EXP_CUT = -105.0
VMEM_LIMIT = 60 * 1024 * 1024
ROW_TILE = 512
SEQ_TILE = 1024
ATTN_BLOCK = 256
ATTN_HEADS = 2
MM_ROWS = 512
DW_TOKENS = 4096
DW_ROWS = 512
MESH = pl.DeviceIdType.MESH


def _cp(*sem):
    return pltpu.CompilerParams(dimension_semantics=sem, vmem_limit_bytes=VMEM_LIMIT)


def _dot(a, b):
    return jnp.dot(a, b, preferred_element_type=F32)


def _dot_nt(a, b):
    return lax.dot_general(a, b, (((1,), (1,)), ((), ())), preferred_element_type=F32)


def _dot_tn(a, b):
    return lax.dot_general(a, b, (((0,), (0,)), ((), ())), preferred_element_type=F32)


def _rstd(v):
    return lax.rsqrt(jnp.mean(v * v, axis=-1, keepdims=True) + RMS_EPS)


def _rms_bwd(dn, vh, r, gain):
    dvh = dn * gain
    dv = r * (dvh - vh * jnp.mean(dvh * vh, axis=-1, keepdims=True))
    return dv, jnp.sum(dn * vh, axis=0, keepdims=True)


def _log_sigmoid(z):
    return jnp.minimum(z, 0.0) - jnp.log(1.0 + jnp.exp(-jnp.abs(z)))


def _expm1(v):
    small = v * (1.0 + v * (0.5 + v * (1.0 / 6.0 + v * (1.0 / 24.0 + v * (1.0 / 120.0)))))
    return jnp.where(jnp.abs(v) < 0.04, small, jnp.exp(v) - 1.0)


_GELU_C = math.sqrt(2.0 / math.pi)


def _gelu(v):
    return 0.5 * v * (1.0 + jnp.tanh(_GELU_C * (v + 0.044715 * v * v * v)))


def _gelu_grad(v):
    th = jnp.tanh(_GELU_C * (v + 0.044715 * v * v * v))
    return 0.5 * (1.0 + th) + 0.5 * v * (1.0 - th * th) * _GELU_C * (1.0 + 3.0 * 0.044715 * v * v)


def _row_spec(tm, d):
    return pl.BlockSpec((tm, d), lambda i: (i, 0))


def _vec_spec(d):
    return pl.BlockSpec((1, d), lambda i: (0, 0))


_ANY = pl.BlockSpec(memory_space=pl.ANY)


def _place():
    x, y, c = lax.axis_index("x"), lax.axis_index("y"), lax.axis_index("c")
    return x, y, c, [(1 - x, y), (x, 1 - y), (1 - x, 1 - y)]


def _remote(src, dst, send_sem, recv_sem, to):
    return pltpu.make_async_remote_copy(src_ref=src, dst_ref=dst, send_sem=send_sem, recv_sem=recv_sem,
                                        device_id=to, device_id_type=MESH)


class _Carrier:
    def __init__(self):
        self.inputs, self.out_shapes, self.aliases, self.ops, self.n_sems, self.results = [], [], {}, [], 0, None

    def inplace(self, arr):
        self.aliases[len(self.inputs)] = len(self.out_shapes)
        self.inputs.append(arr)
        self.out_shapes.append(S(arr.shape, arr.dtype))
        return len(self.out_shapes) - 1

    def read(self, arr):
        self.inputs.append(arr)
        return len(self.inputs) - 1

    def fresh(self, shape, dtype):
        self.out_shapes.append(S(shape, dtype))
        return len(self.out_shapes) - 1

    def _add(self, n_sems, copies):
        base = self.n_sems
        self.n_sems += n_sems

        def start(ins, outs, send, recv):
            for k, (src, dst, _, to) in enumerate(copies(ins, outs)):
                _remote(src, dst, send.at[base + k], recv.at[base + k], to).start()

        def finish(ins, outs, send, recv):
            for k, (src, _, land, to) in enumerate(copies(ins, outs)):
                _remote(src, land, send.at[base + k], recv.at[base + k], to).wait()

        self.ops.append((start, finish))

    def gather_ici(self, o, rows=None, split=True):
        half = self.out_shapes[o].shape[1] // 2
        lo, size = rows or (0, half)

        def copies(ins, outs):
            x, y, c, chips = _place()
            part = (lambda ref: ref.at[pl.ds(c * half + lo, size)]) if split else (lambda ref: ref)
            mine = part(outs[o].at[2 * x + y])
            return [(mine, mine, part(outs[o].at[2 * px + py]), (px, py, c)) for px, py in chips]

        self._add(3, copies)

    def gather_d2d(self, o, rows=None):
        half = self.out_shapes[o].shape[1] // 2
        lo, size = rows or (0, half)

        def copies(ins, outs):
            x, y, c, chips = _place()
            at = lambda k, cc: outs[o].at[k].at[pl.ds(cc * half + lo, size)]
            return [(at(2 * px + py, c), at(2 * px + py, c), at(2 * px + py, 1 - c), (x, y, 1 - c)) for px, py in chips]

        self._add(3, copies)

    def swap(self, i, o):
        half = self.inputs[i].shape[1] // 2

        def copies(ins, outs):
            x, y, c, _ = _place()
            return [(ins[i].at[:, pl.ds((1 - c) * half, half)], outs[o], outs[o], (x, y, 1 - c))]

        self._add(1, copies)

    def scatter(self, i, o, rows=None):
        lo, size = rows or (0, self.inputs[i].shape[1])

        def copies(ins, outs):
            x, y, c, chips = _place()
            cut = lambda ref: ref.at[pl.ds(lo, size)]
            return [(cut(ins[i].at[2 * px + py]), cut(outs[o].at[2 * x + y]), cut(outs[o].at[2 * px + py]), (px, py, c)) for px, py in chips]

        self._add(3, copies)

    def share(self, o):
        r = self.out_shapes[o].shape[0] // 2

        def copies(ins, outs):
            x, y, c, _ = _place()
            mine = outs[o].at[pl.ds(c * r, r)]
            return [(mine, mine, outs[o].at[pl.ds((1 - c) * r, r)], (x, y, 1 - c))]

        self._add(1, copies)

    def spread(self, i, o):
        def copies(ins, outs):
            x, y, c, _ = _place()
            me = 4 * x + 2 * y + c
            out = []
            for d in range(1, 8):
                to, frm = (me + d) % 8, (me + 8 - d) % 8
                out.append((ins[i], outs[o].at[me], outs[o].at[frm], (to // 4, (to // 2) % 2, to % 2)))
            return out

        self._add(7, copies)

    def _pallas(self, body, n_in, n_out, scratch, **kw):
        k_in, k_out = len(self.inputs), len(self.out_shapes)
        grid = kw.get("grid", ())

        def wrapped(*refs):
            ins, cins = refs[:n_in], refs[n_in:n_in + k_in]
            outs = refs[n_in + k_in:n_in + k_in + n_out]
            couts = refs[n_in + k_in + n_out:n_in + k_in + n_out + k_out]
            own = refs[n_in + k_in + n_out + k_out:]
            send, recv = own[len(scratch):]
            ids = [pl.program_id(a) for a in range(len(grid))]
            first = functools.reduce(jnp.logical_and, [a == 0 for a in ids], True)
            last = functools.reduce(jnp.logical_and, [a == g - 1 for a, g in zip(ids, grid)], True)

            def go(stage):
                for op in self.ops:
                    op[stage](cins, couts, send, recv)

            if grid:
                pl.when(first)(lambda: go(0))
                body(*ins, *outs, *own[:len(scratch)])
                pl.when(last)(lambda: go(1))
            else:
                go(0)
                go(1)

        sem = pltpu.SemaphoreType.DMA((self.n_sems,))
        return pl.pallas_call(
            wrapped, in_specs=list(kw.get("in_specs", [])) + [_ANY] * k_in, out_specs=list(kw.get("out_specs", [])) + [_ANY] * k_out,
            out_shape=list(kw.get("out_shape", [])) + self.out_shapes, scratch_shapes=list(scratch) + [sem, sem],
            input_output_aliases={**kw.get("aliases", {}), **{n_in + i: n_out + o for i, o in self.aliases.items()}}, name=kw["name"],
            **({"grid": grid, "compiler_params": _cp(*["arbitrary"] * len(grid))} if grid else {}))

    def run(self, body, kw, *args):
        single = not isinstance(kw["out_shape"], (list, tuple))
        out_shape = [kw["out_shape"]] if single else list(kw["out_shape"])
        out_specs = [kw["out_specs"]] if single else list(kw["out_specs"])
        res = self._pallas(body, len(args), len(out_shape), kw.get("scratch_shapes", []), grid=kw["grid"], in_specs=kw["in_specs"],
                           out_specs=out_specs, out_shape=out_shape, name=kw["name"],
                           aliases=kw.get("input_output_aliases", {}))(*args, *self.inputs)
        self.results = list(res[len(out_shape):])
        return res[0] if single else list(res[:len(out_shape)])

    def run_alone(self, name):
        self.results = list(self._pallas(None, 0, 0, [], name=name)(*self.inputs))


def _call(comm, body, **kw):
    if comm is None:
        return pl.pallas_call(body, **kw)
    return functools.partial(comm.run, body, kw)


def _in_proj_streamed(x, gain, car, o_w, place, *, bm, name):
    m, k = x.shape
    n = car.out_shapes[o_w].shape[2]
    ni, half = m // bm, k // 2
    k_in, k_out = len(car.inputs), len(car.out_shapes)
    order = lambda p: ((p & 1) << 1) | (p >> 1)

    def body(place_ref, x_ref, g_ref, *refs):
        cins, (hn_ref, o_ref, ob_ref), couts = refs[:k_in], refs[k_in:k_in + 3], refs[k_in + 3:k_in + 3 + k_out]
        wbuf, hn_all, local, ici_send, ici_recv, d2d_send, d2d_recv, send, recv = refs[k_in + 3 + k_out:]
        p, i = pl.program_id(0), pl.program_id(1)
        x, y, c, chips = _place()
        me = 2 * x + y
        rows = lambda chunk, cc: couts[o_w].at[chunk].at[pl.ds(cc * half, half)]

        @pl.when(jnp.logical_and(p == 0, i == 0))
        def _():
            for j, (px, py) in enumerate(chips):
                _remote(rows(me, c), rows(me, c), ici_send.at[j], ici_recv.at[j], (px, py, c)).start()
            for op in car.ops:
                op[0](cins, couts, send, recv)

        for j, (px, py) in enumerate(chips):
            @pl.when(jnp.logical_and(p == j + 1, i == 0))
            def _(j=j, px=px, py=py):
                landed, other = rows(2 * px + py, c), rows(2 * px + py, 1 - c)
                _remote(landed, landed, ici_send.at[j], ici_recv.at[j], (px, py, c)).wait_recv()
                _remote(landed, landed, d2d_send.at[j], d2d_recv.at[j], (x, y, 1 - c)).start()
                _remote(other, other, d2d_send.at[j], d2d_recv.at[j], (x, y, 1 - c)).wait_recv()

        @pl.when(i == 0)
        def _():
            cp = pltpu.make_async_copy(couts[o_w].at[me ^ order(p)], wbuf, local.at[0])
            cp.start()
            cp.wait()

        tile = pl.ds(pl.multiple_of(i * bm, bm), bm)

        @pl.when(p == 0)
        def _():
            xv = x_ref[...]
            hn_all[tile, :] = ((xv * _rstd(xv)) * g_ref[...]).astype(_MXU)

        hn = hn_all[tile, :]
        hn_ref[...] = hn
        res = _dot(hn, wbuf[...])
        o_ref[...] = res
        ob_ref[...] = res.astype(ob_ref.dtype)

        @pl.when(jnp.logical_and(p == 3, i == ni - 1))
        def _():
            for j, (px, py) in enumerate(chips):
                _remote(rows(me, c), rows(me, c), ici_send.at[j], ici_recv.at[j], (px, py, c)).wait_send()
                _remote(rows(me, c), rows(me, c), d2d_send.at[j], d2d_recv.at[j], (x, y, 1 - c)).wait_send()
            for op in car.ops:
                op[1](cins, couts, send, recv)

    ospec = pl.BlockSpec((bm, n), lambda p, i, place_ref: (i, place_ref[0] ^ order(p)))
    rows = pl.BlockSpec((bm, k), lambda p, i, place_ref: (jnp.where(p == 0, i, 0), 0))
    three, sems = pltpu.SemaphoreType.DMA((3,)), pltpu.SemaphoreType.DMA((max(car.n_sems, 1),))
    res = pl.pallas_call(
        body,
        grid_spec=pltpu.PrefetchScalarGridSpec(
            num_scalar_prefetch=1, grid=(4, ni),
            in_specs=[rows, pl.BlockSpec((1, k), lambda p, i, place_ref: (0, 0))] + [_ANY] * k_in,
            out_specs=[pl.BlockSpec((bm, k), lambda p, i, place_ref: (p * ni + i, 0)), ospec, ospec] + [_ANY] * k_out,
            scratch_shapes=[pltpu.VMEM((k, n), _MXU), pltpu.VMEM((m, k), _MXU), pltpu.SemaphoreType.DMA((1,)),
                            three, three, three, three, sems, sems]),
        out_shape=[S((4 * m, k), _MXU), S((m, 4 * n), F32), S((m, 4 * n), _MXU)] + car.out_shapes,
        input_output_aliases={3 + a: 3 + o for a, o in car.aliases.items()},
        compiler_params=_cp("arbitrary", "arbitrary"), name=name)(place, x, gain, *car.inputs)
    car.results = list(res[3:])
    return res[0], res[1], res[2]


def _mm_nn(a, b3, *, bm, bn, name, also=None, comm=None):
    m, k = a.shape
    c, _, n = b3.shape
    ni, nj = m // bm, n // bn

    def body(a_ref, b_ref, *o_refs):
        res = _dot(a_ref[...], b_ref[...])
        for o_ref in o_refs:
            o_ref[...] = res.astype(o_ref.dtype)

    ospec = pl.BlockSpec((bm, bn), lambda cc, j, i: (i, cc * nj + j))
    dtypes = [F32] + ([] if also is None else [also])
    out = _call(
        comm, body, grid=(c, nj, ni),
        in_specs=[pl.BlockSpec((bm, k), lambda cc, j, i: (i, 0)), pl.BlockSpec((None, k, bn), lambda cc, j, i: (cc, 0, j))],
        out_specs=[ospec] * len(dtypes), out_shape=[S((m, c * n), dt) for dt in dtypes],
        compiler_params=_cp("parallel", "parallel", "parallel"), name=name)(a, b3)
    return out[0] if also is None else out


def _mm_nt(a, b3, *, bm, bo, out_dtype, name, comm=None):
    m = a.shape[0]
    c, ko, n = b3.shape
    ni, nj = m // bm, ko // bo

    def body(a_ref, b_ref, o_ref):
        acc = _dot_nt(a_ref[:, 0:n], b_ref[0])
        for cc in range(1, c):
            acc = acc + _dot_nt(a_ref[:, cc * n:(cc + 1) * n], b_ref[cc])
        o_ref[...] = acc.astype(o_ref.dtype)

    return _call(
        comm, body, grid=(nj, ni),
        in_specs=[pl.BlockSpec((bm, c * n), lambda j, i: (i, 0)),
                  pl.BlockSpec((c, bo, n), lambda j, i: (0, j, 0))],
        out_specs=pl.BlockSpec((bm, bo), lambda j, i: (i, j)),
        out_shape=S((m, ko), out_dtype),
        compiler_params=_cp("parallel", "parallel"), name=name)(a, b3)


def _mm_tn(a, b, c, *, bm, bk, out_dtype, name, comm=None):
    m, k = b.shape[0], a.shape[1]
    n = b.shape[1] // c
    nm, nk = m // bm, k // bk

    def body(a_ref, b_ref, o_ref, *acc):
        if nm == 1:
            o_ref[...] = _dot_tn(a_ref[...], b_ref[...]).astype(o_ref.dtype)
            return
        mm = pl.program_id(2)

        @pl.when(mm == 0)
        def _():
            acc[0][...] = jnp.zeros_like(acc[0])

        acc[0][...] += _dot_tn(a_ref[...], b_ref[...])

        @pl.when(mm == nm - 1)
        def _():
            o_ref[...] = acc[0][...].astype(o_ref.dtype)

    return _call(
        comm, body, grid=(c, nk, nm),
        in_specs=[pl.BlockSpec((bm, bk), lambda cc, j, mm: (mm, j)),
                  pl.BlockSpec((bm, n), lambda cc, j, mm: (mm, cc))],
        out_specs=pl.BlockSpec((None, bk, n), lambda cc, j, mm: (cc, j, 0)),
        out_shape=S((c, k, n), out_dtype),
        scratch_shapes=[] if nm == 1 else [pltpu.VMEM((bk, n), F32)],
        compiler_params=_cp("parallel", "parallel", "arbitrary"), name=name)(a, b)


def _swiglu_fwd(hn, wg3, wu3, *, bm, name, comm=None):
    m, k = hn.shape
    c, _, n = wg3.shape

    def body(a_ref, g_ref, u_ref, dgate_ref, dup_ref, act_ref):
        a = a_ref[...]
        gate = _dot(a, g_ref[...])
        up = _dot(a, u_ref[...])
        sg = jax.nn.sigmoid(gate)
        silu = gate * sg
        dgate_ref[...] = (up * (sg * (1.0 + gate * (1.0 - sg)))).astype(dgate_ref.dtype)
        dup_ref[...] = silu.astype(dup_ref.dtype)
        act_ref[...] = (silu * up).astype(act_ref.dtype)

    wspec = pl.BlockSpec((None, k, n), lambda cc, i: (cc, 0, 0))
    ospec = pl.BlockSpec((bm, n), lambda cc, i: (i, cc))
    return _call(
        comm, body, grid=(c, m // bm),
        in_specs=[pl.BlockSpec((bm, k), lambda cc, i: (i, 0)), wspec, wspec],
        out_specs=[ospec, ospec, ospec],
        out_shape=[S((m, c * n), _MXU), S((m, c * n), _MXU), S((m, c * n), _MXU)],
        compiler_params=_cp("parallel", "parallel"), name=name)(hn, wg3, wu3)


def _swiglu_bwd(df, wd, act_dgate, act_dup, *, bm, bo, name):
    m, k = df.shape
    ko = wd.shape[0]

    def body(a_ref, b_ref, g_ref, u_ref, dg_ref, du_ref):
        dact = _dot_nt(a_ref[...], b_ref[...])
        dg_ref[...] = (dact * g_ref[...].astype(F32)).astype(dg_ref.dtype)
        du_ref[...] = (dact * u_ref[...].astype(F32)).astype(du_ref.dtype)

    ospec = pl.BlockSpec((bm, bo), lambda j, i: (i, j))
    return pl.pallas_call(
        body, grid=(ko // bo, m // bm),
        in_specs=[pl.BlockSpec((bm, k), lambda j, i: (i, 0)), pl.BlockSpec((bo, k), lambda j, i: (j, 0)), ospec, ospec],
        out_specs=[ospec, ospec],
        out_shape=[S((m, ko), _MXU), S((m, ko), _MXU)],
        compiler_params=_cp("parallel", "parallel"), name=name)(df, wd, act_dgate, act_dup)


def _rms_fwd(x, gain, name):
    t, d = x.shape
    tm = min(t, ROW_TILE)

    def body(x_ref, g_ref, o_ref):
        xv = x_ref[...]
        o_ref[...] = ((xv * _rstd(xv)) * g_ref[...]).astype(o_ref.dtype)

    return pl.pallas_call(body, grid=(t // tm,), in_specs=[_row_spec(tm, d), _vec_spec(d)], out_specs=_row_spec(tm, d),
                          out_shape=S((t, d), _MXU), compiler_params=_cp("parallel"), name=name)(x, gain)


def _outnorm_fwd(o, yl, ga, gl, name, comm=None):
    t, w = o.shape
    tm = min(t, ROW_TILE)

    def body(o_ref, l_ref, ga_ref, gl_ref, y_ref):
        ov, lv = o_ref[...], l_ref[...]
        y_ref[:, :w] = ((ov * _rstd(ov)) * ga_ref[...]).astype(y_ref.dtype)
        y_ref[:, w:] = ((lv * _rstd(lv)) * gl_ref[...]).astype(y_ref.dtype)

    return _call(comm, body, grid=(t // tm,), in_specs=[_row_spec(tm, w), _row_spec(tm, w), _vec_spec(w), _vec_spec(w)],
                 out_specs=_row_spec(tm, 2 * w), out_shape=S((t, 2 * w), _MXU),
                 compiler_params=_cp("parallel"), name=name)(o, yl, ga, gl)


def _mid_fwd(x, mix, g_post, g_pre, name, comm=None):
    t, d = x.shape
    tm = min(t, ROW_TILE)

    def body(x_ref, m_ref, gp_ref, gn_ref, x2_ref, hn_ref):
        mv = m_ref[...]
        x2 = x_ref[...] + (mv * _rstd(mv)) * gp_ref[...]
        x2_ref[...] = x2
        hn_ref[...] = ((x2 * _rstd(x2)) * gn_ref[...]).astype(hn_ref.dtype)

    return _call(comm, body, grid=(t // tm,), in_specs=[_row_spec(tm, d), _row_spec(tm, d), _vec_spec(d), _vec_spec(d)],
                          out_specs=[_row_spec(tm, d), _row_spec(tm, d)], out_shape=[S((t, d), F32), S((t, d), _MXU)],
                          compiler_params=_cp("parallel"), name=name)(x, mix, g_post, g_pre)


def _final(f, x2, target, g_post, name):
    t, d = f.shape
    tm = min(t, ROW_TILE // 2)

    def body(f_ref, x2_ref, t_ref, g_ref, loss_ref, dout_ref, df_ref, dg_ref):
        @pl.when(pl.program_id(0) == 0)
        def _():
            loss_ref[...] = jnp.zeros_like(loss_ref)
            dg_ref[...] = jnp.zeros_like(dg_ref)

        fv = f_ref[...]
        r = _rstd(fv)
        fh = fv * r
        err = (x2_ref[...] + fh * g_ref[...]) - t_ref[...]
        loss_ref[...] += jnp.sum(err * err, axis=0, keepdims=True)
        dout = err * (1.0 / d)
        dout_ref[...] = dout
        dfv, dg = _rms_bwd(dout, fh, r, g_ref[...])
        df_ref[...] = dfv.astype(df_ref.dtype)
        dg_ref[...] += dg

    return pl.pallas_call(
        body, grid=(t // tm,),
        in_specs=[_row_spec(tm, d), _row_spec(tm, d), _row_spec(tm, d), _vec_spec(d)],
        out_specs=[_vec_spec(d), _row_spec(tm, d), _row_spec(tm, d), _vec_spec(d)],
        out_shape=[S((1, d), F32), S((t, d), F32), S((t, d), _MXU), S((1, d), F32)],
        compiler_params=_cp("arbitrary"), name=name)(f, x2, target, g_post)


def _mid_bwd(dhn_a, dhn_b, dout, x2, mix, g_pre, g_post, name, comm=None):
    t, d = x2.shape
    tm = min(t, ROW_TILE // 2)

    def body(da_ref, db_ref, do_ref, x2_ref, m_ref, gn_ref, gp_ref, dx2_ref, dm_ref, dgn_ref, dgp_ref):
        @pl.when(pl.program_id(0) == 0)
        def _():
            dgn_ref[...] = jnp.zeros_like(dgn_ref)
            dgp_ref[...] = jnp.zeros_like(dgp_ref)

        x2 = x2_ref[...]
        r = _rstd(x2)
        dxa, dgn = _rms_bwd(da_ref[...] + db_ref[...], x2 * r, r, gn_ref[...])
        dx2 = do_ref[...] + dxa
        dx2_ref[...] = dx2
        dgn_ref[...] += dgn
        mv = m_ref[...]
        rm = _rstd(mv)
        dmv, dgp = _rms_bwd(dx2, mv * rm, rm, gp_ref[...])
        dm_ref[...] = dmv.astype(dm_ref.dtype)
        dgp_ref[...] += dgp

    rs, vs = _row_spec(tm, d), _vec_spec(d)
    return _call(
        comm, body, grid=(t // tm,), in_specs=[rs, rs, rs, rs, rs, vs, vs], out_specs=[rs, rs, vs, vs],
        out_shape=[S((t, d), F32), S((t, d), _MXU), S((1, d), F32), S((1, d), F32)],
        compiler_params=_cp("arbitrary"), name=name)(dhn_a, dhn_b, dout, x2, mix, g_pre, g_post)


def _first_bwd(dhn, dx2, x, gain, name, comm=None):
    t, d = x.shape
    tm = min(t, ROW_TILE)

    def body(dh_ref, dx2_ref, x_ref, g_ref, dx_ref, dg_ref):
        @pl.when(pl.program_id(0) == 0)
        def _():
            dg_ref[...] = jnp.zeros_like(dg_ref)

        xv = x_ref[...]
        r = _rstd(xv)
        dxa, dg = _rms_bwd(dh_ref[...], xv * r, r, g_ref[...])
        dx_ref[...] = dx2_ref[...] + dxa
        dg_ref[...] += dg

    rs, vs = _row_spec(tm, d), _vec_spec(d)
    return _call(comm, body, grid=(t // tm,), in_specs=[rs, rs, rs, vs], out_specs=[rs, vs],
                          out_shape=[S((t, d), F32), S((1, d), F32)], compiler_params=_cp("arbitrary"), name=name)(dhn, dx2, x, gain)


def _outnorm_bwd(dy, o, yl, ga, gl, name, comm=None):
    t, w = o.shape
    tm = min(t, ROW_TILE)

    def body(dy_ref, o_ref, l_ref, ga_ref, gl_ref, do_ref, dl_ref, dga_ref, dgl_ref):
        @pl.when(pl.program_id(0) == 0)
        def _():
            dga_ref[...] = jnp.zeros_like(dga_ref)
            dgl_ref[...] = jnp.zeros_like(dgl_ref)

        ov, lv = o_ref[...], l_ref[...]
        ra, rl = _rstd(ov), _rstd(lv)
        dov, dga = _rms_bwd(dy_ref[:, :w], ov * ra, ra, ga_ref[...])
        dlv, dgl = _rms_bwd(dy_ref[:, w:], lv * rl, rl, gl_ref[...])
        do_ref[...] = dov.astype(do_ref.dtype)
        dl_ref[...] = dlv
        dga_ref[...] += dga
        dgl_ref[...] += dgl

    rs, vs = _row_spec(tm, w), _vec_spec(w)
    return _call(comm, body, grid=(t // tm,), in_specs=[_row_spec(tm, 2 * w), rs, rs, vs, vs], out_specs=[rs, rs, vs, vs],
                          out_shape=[S((t, w), _MXU), S((t, w), F32), S((1, w), F32), S((1, w), F32)],
                          compiler_params=_cp("arbitrary"), name=name)(dy, o, yl, ga, gl)


def _tri_sum(v, tri):
    return _dot(v.astype(_MXU), tri)


def _attn_tile(qb, kb, row, col, shift, scale):
    z = _dot_nt(qb, kb) * scale
    mask = (col + shift) < row
    lb = _log_sigmoid(z)
    lm = jnp.where(mask, lb - z, 0.0)
    return mask, lb, lm


def _attn_fwd(proj, n_heads, name, comm=None):
    t = proj.shape[0]
    bq = min(t, ATTN_BLOCK)
    nq = t // bq
    scale = 1.0 / math.sqrt(HEAD_DIM)

    heads = [slice(a * HEAD_DIM, (a + 1) * HEAD_DIM) for a in range(ATTN_HEADS)]

    def body(q_ref, k_ref, v_ref, o_ref):
        row = lax.broadcasted_iota(jnp.int32, (bq, bq), 0)
        col = lax.broadcasted_iota(jnp.int32, (bq, bq), 1)
        tri = (row > col).astype(_MXU)

        def per_q(qi, _):
            q0 = pl.multiple_of(qi * bq, bq)
            qbs = [q_ref[pl.ds(q0, bq), hd] for hd in heads]

            def cond(st):
                return jnp.logical_and(st[0] >= 0, st[1])

            def step(st):
                kj, _, carries, accs = st
                k0 = pl.multiple_of(kj * bq, bq)
                alive, new_carries, new_accs = None, [], []
                for hd, qb, carry, acc in zip(heads, qbs, carries, accs):
                    mask, lb, lm = _attn_tile(qb, k_ref[pl.ds(k0, bq), hd], row, col, (kj - qi) * bq, scale)
                    w = jnp.where(mask, jnp.exp(lb + _tri_sum(lm, tri) + carry), 0.0)
                    new_accs.append(acc + _dot(w.astype(_MXU), v_ref[pl.ds(k0, bq), hd]))
                    carry = carry + jnp.sum(lm, axis=1, keepdims=True)
                    new_carries.append(carry)
                    live = jnp.max(carry) > EXP_CUT
                    alive = live if alive is None else jnp.logical_or(alive, live)
                return kj - 1, alive, tuple(new_carries), tuple(new_accs)

            st = lax.while_loop(cond, step, (qi, jnp.bool_(True), (jnp.zeros((bq, 1), F32),) * ATTN_HEADS,
                                             (jnp.zeros((bq, HEAD_DIM), F32),) * ATTN_HEADS))
            for hd, acc in zip(heads, st[3]):
                o_ref[pl.ds(q0, bq), hd] = acc
            return 0

        lax.fori_loop(0, nq, per_q, 0)

    groups = n_heads // ATTN_HEADS
    hs = lambda off: pl.BlockSpec((t, ATTN_HEADS * HEAD_DIM), lambda h: (0, off + h))
    return _call(
        comm, body, grid=(groups,), in_specs=[hs(0), hs(groups), hs(2 * groups)], out_specs=hs(0),
        out_shape=S((t, n_heads * HEAD_DIM), F32), compiler_params=_cp("parallel"), name=name)(proj, proj, proj)


def _emit(blocks, out_ref, starts, sems):
    copies = [pltpu.make_async_copy(b, out_ref.at[:, pl.ds(c0, b.shape[1])], sems.at[k]) for k, (b, c0) in enumerate(zip(blocks, starts))]
    for cp in copies:
        cp.start()
    for cp in copies:
        cp.wait()


def _attn_bwd(proj, do, dproj, n_heads, name, comm=None):
    t = proj.shape[0]
    bq = min(t, ATTN_BLOCK)
    nq = t // bq
    scale = 1.0 / math.sqrt(HEAD_DIM)
    groups = n_heads // ATTN_HEADS
    wide = ATTN_HEADS * HEAD_DIM

    heads = [slice(a * HEAD_DIM, (a + 1) * HEAD_DIM) for a in range(ATTN_HEADS)]

    def body(q_ref, k_ref, v_ref, do_ref, _, dproj_ref, dka_ref, dva_ref, g_ref, b_ref, dq_ref, dk_ref, dv_ref, out_sems):
        group = pl.program_id(0)
        dka_ref[...] = jnp.zeros_like(dka_ref)
        dva_ref[...] = jnp.zeros_like(dva_ref)
        row = lax.broadcasted_iota(jnp.int32, (bq, bq), 0)
        col = lax.broadcasted_iota(jnp.int32, (bq, bq), 1)
        tri = (row > col).astype(_MXU)
        tri_lt = (row < col).astype(_MXU)

        def per_q(qi, _):
            q0 = pl.multiple_of(qi * bq, bq)
            qbs = [q_ref[pl.ds(q0, bq), hd] for hd in heads]
            dobs = [do_ref[pl.ds(q0, bq), hd] for hd in heads]

            def cond(st):
                return jnp.logical_and(st[0] >= 0, st[1])

            def step(st):
                kj, _, carries = st
                k0 = pl.multiple_of(kj * bq, bq)
                alive, new_carries = None, []
                for a, (hd, qb, dob, carry) in enumerate(zip(heads, qbs, dobs, carries)):
                    mask, lb, lm = _attn_tile(qb, k_ref[pl.ds(k0, bq), hd], row, col, (kj - qi) * bq, scale)
                    w = jnp.where(mask, jnp.exp(lb + _tri_sum(lm, tri) + carry), 0.0)
                    g_ref[a, pl.ds(k0, bq), :] = w * _dot_nt(dob, v_ref[pl.ds(k0, bq), hd])
                    b_ref[a, pl.ds(k0, bq), :] = jnp.where(mask, jnp.exp(lb), 0.0)
                    dva_ref[pl.ds(k0, bq), hd] += _dot_tn(w.astype(_MXU), dob)
                    carry = carry + jnp.sum(lm, axis=1, keepdims=True)
                    new_carries.append(carry)
                    live = jnp.max(carry) > EXP_CUT
                    alive = live if alive is None else jnp.logical_or(alive, live)
                return kj - 1, alive, tuple(new_carries)

            st = lax.while_loop(cond, step, (qi, jnp.bool_(True), (jnp.zeros((bq, 1), F32),) * ATTN_HEADS))

            def back(kj, st2):
                k0 = pl.multiple_of(kj * bq, bq)
                out = []
                for a, (hd, qb, (before, dq)) in enumerate(zip(heads, qbs, st2)):
                    g = g_ref[a, pl.ds(k0, bq), :]
                    beta = b_ref[a, pl.ds(k0, bq), :]
                    dz = ((g * (1.0 - beta) - (before + _tri_sum(g, tri_lt)) * beta) * scale).astype(_MXU)
                    dka_ref[pl.ds(k0, bq), hd] += _dot_tn(dz, qb)
                    out.append((before + jnp.sum(g, axis=1, keepdims=True), dq + _dot(dz, k_ref[pl.ds(k0, bq), hd])))
                return tuple(out)

            st2 = lax.fori_loop(st[0] + 1, qi + 1, back, ((jnp.zeros((bq, 1), F32), jnp.zeros((bq, HEAD_DIM), F32)),) * ATTN_HEADS)
            for hd, (_, dq) in zip(heads, st2):
                dq_ref[pl.ds(q0, bq), hd] = dq.astype(dq_ref.dtype)
            return 0

        lax.fori_loop(0, nq, per_q, 0)
        dk_ref[...] = dka_ref[...].astype(dk_ref.dtype)
        dv_ref[...] = dva_ref[...].astype(dv_ref.dtype)
        _emit([dq_ref, dk_ref, dv_ref], dproj_ref, [(a * groups + group) * wide for a in range(3)], out_sems)

    hs = lambda off: pl.BlockSpec((t, wide), lambda h: (0, off + h))
    return _call(
        comm, body, grid=(groups,), in_specs=[hs(0), hs(groups), hs(2 * groups), hs(0), _ANY], out_specs=_ANY,
        out_shape=S(dproj.shape, dproj.dtype), input_output_aliases={4: 0},
        scratch_shapes=[pltpu.VMEM((t, wide), F32), pltpu.VMEM((t, wide), F32),
                        pltpu.VMEM((ATTN_HEADS, t, bq), F32), pltpu.VMEM((ATTN_HEADS, t, bq), F32)]
        + [pltpu.VMEM((t, wide), dproj.dtype)] * 3 + [pltpu.SemaphoreType.DMA((3,))],
        compiler_params=_cp("parallel"), name=name)(proj, proj, proj, do, dproj)


def _shift_down(cur, prev8, k):
    if k == 0:
        return cur
    row8 = lax.broadcasted_iota(jnp.int32, prev8.shape, 0)
    rc = pltpu.roll(cur, k, 0)
    top = jnp.where(row8 < k, pltpu.roll(prev8, k, 0), rc[0:8, :])
    return jnp.concatenate([top, rc[8:, :]], axis=0)


def _shift_up(cur, next8, k):
    if k == 0:
        return cur
    n = cur.shape[0]
    row8 = lax.broadcasted_iota(jnp.int32, next8.shape, 0)
    rc = pltpu.roll(cur, n - k, 0)
    bottom = jnp.where(row8 >= 8 - k, pltpu.roll(next8, 8 - k, 0), rc[n - 8:, :])
    return jnp.concatenate([rc[:n - 8, :], bottom], axis=0)


def _lru_conv(xl, prev8, cw, cb):
    xs = [_shift_down(xl, prev8, CONV_WIDTH - 1 - k) for k in range(CONV_WIDTH)]
    xc = xs[0] * cw[0:1, :]
    for k in range(1, CONV_WIDTH):
        xc = xc + xs[k] * cw[k:k + 1, :]
    return xs, xc + cb


def _lru_gates(xl, prev8, cw, cb, wr, br, wi, bi, ls):
    xs, xc = _lru_conv(xl, prev8, cw, cb)
    xcb = xc.astype(_MXU)
    r = jax.nn.sigmoid(_dot(xcb, wr) + br)
    i = jax.nn.sigmoid(_dot(xcb, wi) + bi)
    la = (LRU_C * r) * ls
    a = jnp.exp(la)
    mult = jnp.sqrt(-_expm1(2.0 * la))
    return xs, xc, r, i, a, mult


def _group_scan(a, b, reverse):
    n = a.shape[0]
    row = lax.broadcasted_iota(jnp.int32, a.shape, 0) % 8
    for d in (1, 2, 4):
        if reverse:
            m = row < 8 - d
            a_s, b_s = pltpu.roll(a, n - d, 0), pltpu.roll(b, n - d, 0)
        else:
            m = row >= d
            a_s, b_s = pltpu.roll(a, d, 0), pltpu.roll(b, d, 0)
        b = jnp.where(m, a * b_s + b, b)
        a = jnp.where(m, a * a_s, a)
    return a, b


def _lru_fwd(proj, col0, n_blocks, cw, cb, wr, br, wi, bi, lam, name, comm=None):
    t = proj.shape[0]
    tt = min(t, SEQ_TILE)
    nt = t // tt

    def body(xl_ref, gl_ref, cw_ref, cb_ref, wr_ref, br_ref, wi_ref, bi_ref, lam_ref, h_ref, y_ref, *kept):
        cwv, cbv, brv, biv = cw_ref[...], cb_ref[...], br_ref[...], bi_ref[...]
        wrv, wiv = wr_ref[...].astype(_MXU), wi_ref[...].astype(_MXU)
        ls = _log_sigmoid(lam_ref[...])

        def tile(ti, hin):
            t0 = pl.multiple_of(ti * tt, tt)
            p0 = pl.multiple_of(jnp.maximum(t0 - 8, 0), 8)
            prev8 = xl_ref[pl.ds(p0, 8), :] * (ti > 0).astype(F32)
            xl = xl_ref[pl.ds(t0, tt), :]
            _, xc, r, ig, a, mult = _lru_gates(xl, prev8, cwv, cbv, wrv, brv, wiv, biv, ls)
            for ref, val in zip(kept, (r, ig, a, mult)):
                ref[pl.ds(t0, tt), :] = val
            ga, gb = _group_scan(a, mult * (ig * xc), False)
            for g in range(tt // 8):
                hg = ga[8 * g:8 * g + 8, :] * hin + gb[8 * g:8 * g + 8, :]
                h_ref[pl.ds(t0 + 8 * g, 8), :] = hg
                hin = hg[7:8, :]
            y_ref[pl.ds(t0, tt), :] = h_ref[pl.ds(t0, tt), :] * _gelu(gl_ref[pl.ds(t0, tt), :])
            return hin

        lax.fori_loop(0, nt, tile, jnp.zeros((1, HEAD_DIM), F32))

    cs = lambda off: pl.BlockSpec((t, HEAD_DIM), lambda n: (0, off + n))
    vs = pl.BlockSpec((1, HEAD_DIM), lambda n: (0, n))
    ws = pl.BlockSpec((None, HEAD_DIM, HEAD_DIM), lambda n: (n, 0, 0))
    w = n_blocks * HEAD_DIM
    return _call(
        comm, body, grid=(n_blocks,),
        in_specs=[cs(col0), cs(col0 + n_blocks), pl.BlockSpec((CONV_WIDTH, HEAD_DIM), lambda n: (0, n)), vs, ws, vs, ws, vs, vs],
        out_specs=[cs(0)] * 6, out_shape=[S((t, w), F32)] * 6,
        compiler_params=_cp("parallel"), name=name)(proj, proj, cw, cb, wr, br, wi, bi, lam)


def _lru_bwd(proj, col0, n_blocks, h, kept, dyl, cw, cb, wr, wi, lam, name, comm=None):
    t = proj.shape[0]
    tt = min(t, SEQ_TILE)
    nt = t // tt

    def body(xl_ref, gl_ref, h_ref, r_ref, i_ref, a_ref, m_ref, dy_ref, cw_ref, cb_ref, wr_ref, wi_ref, lam_ref,
             dproj_ref, dcw_ref, dcb_ref, dwr_ref, dbr_ref, dwi_ref, dbi_ref, dlam_ref, g_ref, dxl_ref, dgl_ref, out_sems):
        block = pl.program_id(0)
        cwv, cbv = cw_ref[...], cb_ref[...]
        wrv, wiv = wr_ref[...].astype(_MXU), wi_ref[...].astype(_MXU)
        lamv = lam_ref[...]
        ls = _log_sigmoid(lamv)
        for ref in (dcw_ref, dcb_ref, dwr_ref, dbr_ref, dwi_ref, dbi_ref, dlam_ref):
            ref[...] = jnp.zeros_like(ref)

        def tile(s, carry):
            e_in, dxc_next8 = carry
            ti = nt - 1 - s
            t0 = pl.multiple_of(ti * tt, tt)
            p0 = pl.multiple_of(jnp.maximum(t0 - 8, 0), 8)
            first = (ti > 0).astype(F32)
            xl = xl_ref[pl.ds(t0, tt), :]
            xs, xc = _lru_conv(xl, xl_ref[pl.ds(p0, 8), :] * first, cwv, cbv)
            r, ig, a, mult = (ref[pl.ds(t0, tt), :] for ref in (r_ref, i_ref, a_ref, m_ref))
            hv = h_ref[pl.ds(t0, tt), :]
            h_before = _shift_down(hv, h_ref[pl.ds(p0, 8), :] * first, 1)
            glv = gl_ref[pl.ds(t0, tt), :]
            dyv = dy_ref[pl.ds(t0, tt), :]
            dgl_ref[pl.ds(t0, tt), :] = (dyv * hv * _gelu_grad(glv)).astype(dgl_ref.dtype)
            dh = dyv * _gelu(glv)
            row = lax.broadcasted_iota(jnp.int32, a.shape, 0)
            coef = jnp.where(row == tt - 1, 1.0, pltpu.roll(a, tt - 1, 0))
            ga, gb = _group_scan(coef, dh, True)
            gin = e_in
            for g in reversed(range(tt // 8)):
                gg = ga[8 * g:8 * g + 8, :] * gin + gb[8 * g:8 * g + 8, :]
                g_ref[8 * g:8 * g + 8, :] = gg
                gin = gg[0:1, :]
            gv = g_ref[...]
            e_out = a[0:1, :] * gv[0:1, :]
            ix = ig * xc
            dla = (gv * h_before) * a - (gv * ix) * (a * a / mult)
            dlam_ref[...] += jnp.sum(dla * (LRU_C * r), axis=0, keepdims=True)
            dpr = (dla * (LRU_C * ls)) * (r * (1.0 - r))
            dpi = (gv * mult * xc) * (ig * (1.0 - ig))
            dbr_ref[...] += jnp.sum(dpr, axis=0, keepdims=True)
            dbi_ref[...] += jnp.sum(dpi, axis=0, keepdims=True)
            xcb, dprb, dpib = xc.astype(_MXU), dpr.astype(_MXU), dpi.astype(_MXU)
            dwr_ref[...] += _dot_tn(xcb, dprb)
            dwi_ref[...] += _dot_tn(xcb, dpib)
            dxc = gv * mult * ig + _dot_nt(dprb, wrv) + _dot_nt(dpib, wiv)
            dcb_ref[...] += jnp.sum(dxc, axis=0, keepdims=True)
            dxl = None
            for k in range(CONV_WIDTH):
                dcw_ref[k:k + 1, :] += jnp.sum(dxc * xs[k], axis=0, keepdims=True)
                term = _shift_up(dxc, dxc_next8, CONV_WIDTH - 1 - k) * cwv[k:k + 1, :]
                dxl = term if dxl is None else dxl + term
            dxl_ref[pl.ds(t0, tt), :] = dxl.astype(dxl_ref.dtype)
            return e_out, dxc[0:8, :]

        lax.fori_loop(0, nt, tile, (jnp.zeros((1, HEAD_DIM), F32), jnp.zeros((8, HEAD_DIM), F32)))
        dlam_ref[...] = dlam_ref[...] * (1.0 - jax.nn.sigmoid(lamv))
        _emit([dxl_ref, dgl_ref], dproj_ref, [(col0 + block) * HEAD_DIM, (col0 + n_blocks + block) * HEAD_DIM], out_sems)

    cs = lambda off: pl.BlockSpec((t, HEAD_DIM), lambda n: (0, off + n))
    vs = pl.BlockSpec((1, HEAD_DIM), lambda n: (0, n))
    ws = pl.BlockSpec((None, HEAD_DIM, HEAD_DIM), lambda n: (n, 0, 0))
    cws = pl.BlockSpec((CONV_WIDTH, HEAD_DIM), lambda n: (0, n))
    w = n_blocks * HEAD_DIM
    vec = S((1, w), F32)
    mat = S((n_blocks, HEAD_DIM, HEAD_DIM), F32)
    return _call(
        comm, body, grid=(n_blocks,),
        in_specs=[cs(col0), cs(col0 + n_blocks)] + [cs(0)] * 6 + [cws, vs, ws, ws, vs],
        out_specs=[_ANY, cws, vs, ws, vs, ws, vs, vs],
        out_shape=[S(proj.shape, _MXU), S((CONV_WIDTH, w), F32), vec, mat, vec, mat, vec, vec],
        scratch_shapes=[pltpu.VMEM((tt, HEAD_DIM), F32), pltpu.VMEM((t, HEAD_DIM), _MXU), pltpu.VMEM((t, HEAD_DIM), _MXU),
                        pltpu.SemaphoreType.DMA((2,))],
        compiler_params=_cp("parallel"), name=name)(proj, proj, h, *kept, dyl, cw, cb, wr, wi, lam)


class _NoExchange:
    grad_dtype = F32

    def __init__(self, weights):
        self.weights, self.grads, self.packs = weights, {}, {}

    def weight(self, name):
        return self.weights[name]

    def in_proj(self, x, gain, bm):
        hn = _rms_fwd(x, gain, "rms1")
        return [hn, *_mm_nn(hn, self.weights["w_in"], bm=bm, bn=self.weights["w_in"].shape[2], name="in_proj", also=_MXU)]

    def conv_w(self):
        return self.weights["conv_w"]

    def carrier(self, call):
        return None

    def harvest(self, car):
        pass

    def alone(self, call):
        pass


def _local_step(x, target, norms, ex, cb, wr, br, wi, bi, lam, ga, gl):
    g_pre_mix, g_post_mix, g_pre_ffn, g_post_ffn = norms
    t, d = x.shape
    bm = min(t, MM_ROWS)
    bt = min(t, DW_TOKENS)

    def run(fn, name, *args, **kw):
        car = ex.carrier(name)
        out = fn(*args, name=name, comm=car, **kw)
        ex.harvest(car)
        return out

    hn1, proj, proj_mx = ex.in_proj(x, g_pre_mix, bm)
    win3, cw = ex.weight("w_in"), ex.conv_w()
    c = win3.shape[0]
    o = run(_attn_fwd, "attn_fwd", proj_mx, (proj.shape[1] - d) // 3 // HEAD_DIM)
    mix = 2 * o.shape[1]
    n_heads = n_blocks = o.shape[1] // HEAD_DIM
    h, yl, *kept = run(_lru_fwd, "lru_fwd", proj, 3 * n_heads, n_blocks, cw, cb, wr, br, wi, bi, lam)
    y = run(_outnorm_fwd, "outnorm_fwd", o, yl, ga, gl)
    wout = ex.weight("w_out")
    mixo = run(_mm_nn, "out_proj", y, wout[None], bm=bm, bn=d)
    x2, hn2 = run(_mid_fwd, "mid_fwd", x, mixo, g_post_mix, g_pre_ffn)
    ex.alone("gather_w_up_last")
    wg3, wu3 = ex.weight("w_ffn_gate"), ex.weight("w_ffn_up")
    act_dgate, act_dup, act = run(_swiglu_fwd, "ffn_gate_up", hn2, wg3, wu3, bm=bm)
    ex.alone("gather_w_down")
    wd = ex.weight("w_ffn_down")
    ff = wd.shape[0]
    f = _mm_nn(act, wd[None], bm=bm, bn=d // 2, name="ffn_down")
    loss_cols, dout, df, dg_post_ffn = _final(f, x2, target, g_post_ffn, "final")

    dgate, dup = _swiglu_bwd(df, wd, act_dgate, act_dup, bm=min(t, 2 * MM_ROWS), bo=ff // 4, name="ffn_down_bwd")
    ex.grads["w_ffn_down"] = _mm_tn(act, df, 1, bm=bt, bk=DW_ROWS, out_dtype=ex.grad_dtype, name="ffn_down_dw").reshape(c, ff // c, d)
    ex.grads["w_ffn_gate"] = run(_mm_tn, "ffn_gate_dw", hn2, dgate, c, bm=bt, bk=d // 2, out_dtype=ex.grad_dtype)
    ex.grads["w_ffn_up"] = run(_mm_tn, "ffn_up_dw", hn2, dup, c, bm=bt, bk=d // 2, out_dtype=ex.grad_dtype)
    dhn2_g = run(_mm_nt, "ffn_gate_dx", dgate, wg3, bm=bm, bo=d // 2, out_dtype=F32)
    dhn2_u = run(_mm_nt, "ffn_up_dx", dup, wu3, bm=bm, bo=d // 2, out_dtype=F32)
    dx2, dmix, dg_pre_ffn, dg_post_mix = run(_mid_bwd, "mid_bwd", dhn2_g, dhn2_u, dout, x2, mixo, g_pre_ffn, g_post_mix)
    dy = run(_mm_nt, "out_proj_dx", dmix, wout[None], bm=bm, bo=mix, out_dtype=F32)
    ex.grads["w_out"] = _mm_tn(y, dmix, 1, bm=bt, bk=mix // 4, out_dtype=ex.grad_dtype, name="out_proj_dw").reshape(c, mix // c, d)
    do, dyl, dga, dgl_norm = run(_outnorm_bwd, "outnorm_bwd", dy, o, yl, ga, gl)
    dproj, dcw, dcb, dwr, dbr, dwi, dbi, dlam = run(_lru_bwd, "lru_bwd", proj, 3 * n_heads, n_blocks, h, kept, dyl, cw, cb, wr, wi, lam)
    small = dict(post_mix_norm=dg_post_mix, pre_ffn_norm=dg_pre_ffn, post_ffn_norm=dg_post_ffn, conv_w=dcw, conv_b=dcb,
                 w_rgate=dwr, b_rgate=dbr, w_igate=dwi, b_igate=dbi, lru_lambda=dlam, attn_out_norm=dga, lru_out_norm=dgl_norm)
    ex.packs["early"] = _pack([small[n] for n in _SMALL_EARLY])
    dproj = run(_attn_bwd, "attn_bwd", proj_mx, do, dproj, n_heads)
    ex.grads["w_in"] = _mm_tn(hn1, dproj, c, bm=bt, bk=d // 2, out_dtype=ex.grad_dtype, name="in_proj_dw")
    ex.alone("grads_w_in_swap")
    dhn1 = run(_mm_nt, "in_proj_dx", dproj, win3, bm=bm, bo=d // 2, out_dtype=F32)
    grad_x, small["pre_mix_norm"] = run(_first_bwd, "first_bwd", dhn1, dx2, x, g_pre_mix)
    ex.packs["late"] = _pack([small["pre_mix_norm"], (0.5 / d) * jnp.sum(loss_cols, keepdims=True)])
    return loss_cols, grad_x, small


def _into_slot(wsh, slot, dtype, name):
    rows, n = wsh.shape
    rb = _row_block(rows, 512) if rows % 8 == 0 else rows

    def body(s_ref, w_ref, o_ref):
        o_ref[...] = w_ref[...].astype(o_ref.dtype)

    return pl.pallas_call(
        body,
        grid_spec=pltpu.PrefetchScalarGridSpec(
            num_scalar_prefetch=1, grid=(rows // rb,),
            in_specs=[pl.BlockSpec((rb, n), lambda i, s_ref: (i, 0))],
            out_specs=pl.BlockSpec((None, rb, n), lambda i, s_ref: (s_ref[0], i, 0))),
        out_shape=S((4, rows, n), dtype), compiler_params=_cp("parallel"), name=name)(slot, wsh)


class _Exchange:
    SCHEDULE = {
        "in_proj": [("stream", "w_in"), ("ici", "conv_w"), ("ici", "w_ffn_up", 0)],
        "attn_fwd": [("d2d", "w_ffn_up", 0), ("ici", "w_ffn_gate")],
        "lru_fwd": [("d2d", "w_ffn_gate"), ("ici", "w_out"), ("ici", "w_ffn_up", 1)],
        "outnorm_fwd": [("d2d", "w_out"), ("d2d", "w_ffn_up", 1)],
        "out_proj": [("ici", "w_ffn_up", 2)],
        "mid_fwd": [("d2d", "w_ffn_up", 2), ("ici", "w_ffn_up", 3)],
        "gather_w_up_last": [("d2d", "w_ffn_up", 3)],
        "ffn_gate_up": [("ici", "w_ffn_down")],
        "gather_w_down": [("d2d", "w_ffn_down")],
        "ffn_gate_dw": [("swap", "w_ffn_down")],
        "ffn_up_dw": [("scatter", "w_ffn_down", 0), ("scatter", "w_ffn_down", 1), ("scatter", "w_ffn_down", 2), ("swap", "w_ffn_gate")],
        "ffn_gate_dx": [("scatter", "w_ffn_down", 3), ("scatter", "w_ffn_gate", 0), ("swap", "w_ffn_up")],
        "ffn_up_dx": [("share", "w_ffn_down"), ("scatter", "w_ffn_gate", 1), ("scatter", "w_ffn_gate", 2)],
        "mid_bwd": [("scatter", "w_ffn_gate", 3), ("scatter", "w_ffn_up", 0)],
        "out_proj_dx": [("share", "w_ffn_gate"), ("scatter", "w_ffn_up", 1)],
        "outnorm_bwd": [("scatter", "w_ffn_up", 2), ("swap", "w_out")],
        "lru_bwd": [("scatter", "w_ffn_up", 3), ("scatter", "w_out")],
        "attn_bwd": [("share", "w_ffn_up"), ("share", "w_out"), ("spread", "early")],
        "grads_w_in_swap": [("swap", "w_in")],
        "in_proj_dx": [("scatter", "w_in")],
        "grads_w_in_share": [("share", "w_in"), ("spread", "late")],
    }
    PIECES = 4
    grad_dtype = BF16

    def __init__(self, slots, place):
        self.buf, self.place = dict(slots), place
        self.grads, self.packs, self.swapped, self.part, self.scattered, self.full, self.spreaded = {}, {}, {}, {}, {}, {}, {}

    def weight(self, name):
        b = self.buf[name]
        return b.reshape(-1, b.shape[2]) if name in ("w_out", "w_ffn_down") else b

    def in_proj(self, x, gain, bm):
        car = self.carrier("in_proj")
        out = _in_proj_streamed(x, gain, car, car.streamed, self.place, bm=bm, name="in_proj")
        self.harvest(car)
        return out

    def conv_w(self):
        return jnp.transpose(self.buf["conv_w"], (1, 0, 2)).reshape(CONV_WIDTH, -1)

    def carrier(self, call):
        if call not in self.SCHEDULE:
            return None
        car = _Carrier()
        car.todo, slot = [], {}
        for kind, name, *piece in self.SCHEDULE[call]:
            if kind in ("ici", "d2d", "stream"):
                if name not in slot:
                    slot[name] = car.inplace(self.buf[name])
                    car.todo.append((self.buf, name, slot[name]))
            if kind == "stream":
                car.streamed = slot[name]
            elif kind in ("ici", "d2d"):
                size = self.buf[name].shape[1] // 2 // self.PIECES
                rows = (piece[0] * size, size) if piece else None
                if kind == "ici":
                    car.gather_ici(slot[name], rows, split=name != "conv_w")
                else:
                    car.gather_d2d(slot[name], rows)
            elif kind == "swap":
                g = self.grads[name]
                o = car.fresh((4, g.shape[1] // 2, g.shape[2]), g.dtype)
                car.swap(car.read(g), o)
                car.todo.append((self.swapped, name, o))
            elif kind == "scatter":
                if name not in self.part:
                    self.part[name] = _add_own_half(self.grads[name], self.swapped[name], self.place[1:], "grads_add_" + name)
                p = self.part[name]
                key = ("scatter", name)
                if key not in slot:
                    slot[key] = (car.read(p), car.inplace(self.scattered[name]) if name in self.scattered else car.fresh(p.shape, p.dtype))
                    car.todo.append((self.scattered, name, slot[key][1]))
                size = p.shape[1] // self.PIECES
                car.scatter(*slot[key], (piece[0] * size, size) if piece else None)
            elif kind == "share":
                o = car.inplace(_sum_chips(self.part[name], self.scattered[name], self.place, "grads_sum_" + name))
                car.share(o)
                car.todo.append((self.full, name, o))
            else:
                o = car.fresh((8,) + self.packs[name].shape, F32)
                car.spread(car.read(self.packs[name]), o)
                car.todo.append((self.spreaded, name, o))
        return car

    def harvest(self, car):
        for state, name, o in (car.todo if car is not None else []):
            state[name] = car.results[o]

    def alone(self, call):
        car = self.carrier(call)
        car.run_alone(call)
        self.harvest(car)

    def small_sum(self, key):
        return _sum_devices(self.packs[key], self.spreaded[key], 2 * self.place[0:1] + self.place[1:], "grads_small_sum_" + key)


def _row_block(rows, cap):
    return max(b for b in range(8, cap + 1, 8) if rows % b == 0)


def _add_own_half(g, recv, core, name):
    _, rows, n = g.shape
    half = rows // 2
    rb = _row_block(half, 1024)
    nb = half // rb

    def body(c_ref, g_ref, r_ref, o_ref):
        o_ref[...] = (g_ref[...].astype(F32) + r_ref[...].astype(F32)).astype(o_ref.dtype)

    return pl.pallas_call(
        body,
        grid_spec=pltpu.PrefetchScalarGridSpec(
            num_scalar_prefetch=1, grid=(4, nb),
            in_specs=[pl.BlockSpec((None, rb, n), lambda k, i, c_ref: (k, c_ref[0] * nb + i, 0)),
                      pl.BlockSpec((None, rb, n), lambda k, i, c_ref: (k, i, 0))],
            out_specs=pl.BlockSpec((None, rb, n), lambda k, i, c_ref: (k, i, 0))),
        out_shape=S((4, half, n), BF16), compiler_params=_cp("parallel", "parallel"), name=name)(core, g, recv)


def _sum_chips(part, recv, place, name):
    _, rows, n = part.shape
    rb = _row_block(rows, 256)
    nb = rows // rb

    def body(p_ref, own_ref, r0, r1, r2, r3, o_ref):
        own = own_ref[...].astype(F32)
        terms = [jnp.where(p_ref[0] == k, own, r[...].astype(F32)) for k, r in enumerate((r0, r1, r2, r3))]
        o_ref[...] = ((terms[0] + terms[1]) + terms[2]) + terms[3]

    def slot(k):
        return pl.BlockSpec((None, rb, n), lambda i, p_ref: (jnp.where(p_ref[0] == k, (k + 1) % 4, k), i, 0))

    return pl.pallas_call(
        body,
        grid_spec=pltpu.PrefetchScalarGridSpec(
            num_scalar_prefetch=1, grid=(nb,),
            in_specs=[pl.BlockSpec((None, rb, n), lambda i, p_ref: (p_ref[0], i, 0))] + [slot(k) for k in range(4)],
            out_specs=pl.BlockSpec((rb, n), lambda i, p_ref: (p_ref[1] * nb + i, 0))),
        out_shape=S((2 * rows, n), F32), compiler_params=_cp("parallel"), name=name)(place, part, recv, recv, recv, recv)


def _sum_devices(own, spread, me, name):
    rows = own.shape[0]

    def body(me_ref, own_ref, *refs):
        acc = None
        for k, r in enumerate(refs[:8]):
            term = jnp.where(me_ref[0] == k, own_ref[...], r[...])
            acc = term if acc is None else acc + term
        refs[8][...] = acc

    def slot(k):
        return pl.BlockSpec((None, rows, 128), lambda i, me_ref: (jnp.where(me_ref[0] == k, (k + 1) % 8, k), 0, 0))

    whole = pl.BlockSpec((rows, 128), lambda i, me_ref: (0, 0))
    return pl.pallas_call(
        body,
        grid_spec=pltpu.PrefetchScalarGridSpec(num_scalar_prefetch=1, grid=(1,), in_specs=[whole] + [slot(k) for k in range(8)],
                                               out_specs=whole),
        out_shape=S((rows, 128), F32), compiler_params=_cp("arbitrary"), name=name)(me, own, *[spread] * 8)


def _adamw(w, g, m, v, name, regive=False):
    rows, n = w.shape
    rb = rows if rows * n * 4 <= (1 << 21) else _row_block(rows, 512)
    c1 = 1.0 - ADAM_B1 ** ADAM_STEP
    c2 = 1.0 - ADAM_B2 ** ADAM_STEP

    def body(w_ref, g_ref, m_ref, v_ref, d_ref, nm_ref, nv_ref, *again):
        gv = g_ref[...]
        for ref in again:
            ref[...] = gv
        nm = ADAM_B1 * m_ref[...] + (1.0 - ADAM_B1) * gv
        nv = ADAM_B2 * v_ref[...] + (1.0 - ADAM_B2) * (gv * gv)
        nm_ref[...] = nm
        nv_ref[...] = nv
        d_ref[...] = -ADAM_LR * ((nm / c1) / (jnp.sqrt(nv / c2) + ADAM_EPS) + ADAM_WD * w_ref[...])

    bs = pl.BlockSpec((rb, n), lambda i: (i, 0))
    n_out = 4 if regive else 3
    return pl.pallas_call(body, grid=(rows // rb,), in_specs=[bs] * 4, out_specs=[bs] * n_out, out_shape=[S((rows, n), F32)] * n_out,
                          compiler_params=_cp("parallel"), name=name)(w, g, m, v)


_BIG = ("w_in", "w_out", "w_ffn_gate", "w_ffn_up", "w_ffn_down")
_SMALL = ("pre_mix_norm", "post_mix_norm", "pre_ffn_norm", "post_ffn_norm", "conv_w", "conv_b", "w_rgate", "b_rgate",
          "w_igate", "b_igate", "lru_lambda", "attn_out_norm", "lru_out_norm")
_SMALL_EARLY = _SMALL[1:]
_WEIGHTS = ("pre_mix_norm", "post_mix_norm", "pre_ffn_norm", "post_ffn_norm", "w_in", "conv_w", "conv_b", "w_rgate", "b_rgate",
            "w_igate", "b_igate", "lru_lambda", "attn_out_norm", "lru_out_norm", "w_out", "w_ffn_gate", "w_ffn_up", "w_ffn_down")


def _pack(arrays):
    flat = []
    for a in arrays:
        f = a.reshape(-1)
        flat.append(jnp.pad(f, (0, (-f.shape[0]) % 1024)))
    return jnp.concatenate(flat).reshape(-1, 128)


def _unpack(packed, shapes):
    out, pos = [], 0
    flat = packed.reshape(-1)
    for s in shapes:
        size = math.prod(s)
        out.append(flat[pos:pos + size].reshape(s))
        pos += size + (-size) % 1024
    return out


def kernel(x, pre_mix_norm, post_mix_norm, pre_ffn_norm, post_ffn_norm, w_in, conv_w, conv_b, w_rgate, b_rgate, w_igate, b_igate, lru_lambda, attn_out_norm, lru_out_norm, w_out, w_ffn_gate, w_ffn_up, w_ffn_down, loss_target, m_pre_mix_norm, m_post_mix_norm, m_pre_ffn_norm, m_post_ffn_norm, m_w_in, m_conv_w, m_conv_b, m_w_rgate, m_b_rgate, m_w_igate, m_b_igate, m_lru_lambda, m_attn_out_norm, m_lru_out_norm, m_w_out, m_w_ffn_gate, m_w_ffn_up, m_w_ffn_down, v_pre_mix_norm, v_post_mix_norm, v_pre_ffn_norm, v_post_ffn_norm, v_w_in, v_conv_w, v_conv_b, v_w_rgate, v_b_rgate, v_w_igate, v_b_igate, v_lru_lambda, v_attn_out_norm, v_lru_out_norm, v_w_out, v_w_ffn_gate, v_w_ffn_up, v_w_ffn_down):
    given = dict(locals())
    w = {n: given[n][0] for n in _WEIGHTS}
    m = {n: given["m_" + n][0] for n in _WEIGHTS}
    v = {n: given["v_" + n][0] for n in _WEIGHTS}
    xs, target = x[0], loss_target[0]
    d = xs.shape[1]
    chip = (2 * lax.axis_index("x") + lax.axis_index("y")).astype(jnp.int32)
    place = jnp.stack([chip, lax.axis_index("c").astype(jnp.int32)])

    slots = {n: _into_slot(w[n], place[0:1], _MXU, "slot_" + n) for n in _BIG}
    slots["conv_w"] = _into_slot(w["conv_w"], place[0:1], F32, "slot_conv_w")
    ex = _Exchange(slots, place)
    row = lambda a: a.reshape(1, -1)
    norms = tuple(row(w[n]) for n in ("pre_mix_norm", "post_mix_norm", "pre_ffn_norm", "post_ffn_norm"))

    loss_cols, grad_x, small = _local_step(
        xs, target, norms, ex, row(w["conv_b"]), w["w_rgate"], row(w["b_rgate"]),
        w["w_igate"], row(w["b_igate"]), row(w["lru_lambda"]), row(w["attn_out_norm"]), row(w["lru_out_norm"]))


    ex.alone("grads_w_in_share")
    reduced = {n: ex.full[n] for n in _BIG}
    early = _unpack(ex.small_sum("early"), [small[n].shape for n in _SMALL_EARLY])
    late = _unpack(ex.small_sum("late"), [small["pre_mix_norm"].shape, (1, 1)])
    loss = late[1][0, 0]
    for n, g in zip(_SMALL_EARLY + ("pre_mix_norm",), early + late[:1]):
        reduced[n] = g.reshape(w[n].shape) if n != "conv_w" else lax.dynamic_slice_in_dim(g, chip * w[n].shape[1], w[n].shape[1], axis=1)

    delta, new_m, new_v = {}, {}, {}
    for n in _BIG:
        delta[n], new_m[n], new_v[n], reduced[n] = _adamw(w[n], reduced[n], m[n], v[n], "adamw_" + n, regive=True)
    shapes = [w[n].shape for n in _SMALL]
    packed = _adamw(*[_pack([src[n] for n in _SMALL]) for src in (w, reduced, m, v)], "adamw_small")
    for out, p in zip((delta, new_m, new_v), packed):
        out.update(zip(_SMALL, _unpack(p, shapes)))

    lead = lambda a: a[None]
    return (loss, lead(grad_x), *[lead(reduced[n]) for n in _WEIGHTS], *[lead(delta[n]) for n in _WEIGHTS],
            *[lead(new_m[n]) for n in _WEIGHTS], *[lead(new_v[n]) for n in _WEIGHTS])
```
